```python
import jax, jax.numpy as jnp
from jax import lax
import numpy as np

D_MODEL = 1024
BATCH = 8
SEQ = 2048
DEPTH = 4

HEAD_DIM = 64
CONV_WIDTH = 3 * D_MODEL // 8
POOL_WIDTH = D_MODEL // 4
SGU_WIDTH = D_MODEL - CONV_WIDTH - POOL_WIDTH
CONV_HEADS = CONV_WIDTH // HEAD_DIM
SGU_HEADS = SGU_WIDTH // HEAD_DIM
POOL_WINDOWS = (2, 4, 8, 16)
POOL_GROUPS = len(POOL_WINDOWS)
POOL_GROUP_DIM = POOL_WIDTH // POOL_GROUPS
CONV_K = 3
CHUNK = 128
MIX_WIDTH = CONV_WIDTH + POOL_WIDTH + SGU_WIDTH
IN_WIDTH = 3 * CONV_WIDTH + POOL_WIDTH + 2 * SGU_WIDTH
D_FF = -(-8 * D_MODEL // (3 * 256)) * 256
ALPHA = float((2 * DEPTH) ** 0.25)
BETA = float((8 * DEPTH) ** -0.25)
LN_EPS = 1e-5

kernel_name = "hybrid_conv_pool_sgu_deepnorm"


def _norm_stats(x, eps=LN_EPS):
    xf = x.astype(jnp.float32)
    mu = jnp.mean(xf, axis=-1, keepdims=True)
    var = jnp.mean(jnp.square(xf - mu), axis=-1, keepdims=True)
    return ((xf - mu) * lax.rsqrt(var + eps)).astype(x.dtype)


def layer_norm(x, g, b):
    return _norm_stats(x) * g + b


def short_gated_conv(xa, gb, gc, w_conv):
    z = gc * xa
    s = z.shape[1]
    zp = jnp.pad(z, ((0, 0), (CONV_K - 1, 0), (0, 0)))
    y = w_conv[0] * zp[:, 0:s] + w_conv[1] * zp[:, 1:s + 1] + w_conv[2] * zp[:, 2:s + 2]
    return gb * y


def multiscale_pool(p, w_pool, pool_scale):
    b, s, _ = p.shape
    pg = p.reshape(b, s, POOL_GROUPS, POOL_GROUP_DIM)
    cs = jnp.cumsum(pg.astype(jnp.float32), axis=1)
    t1 = jnp.arange(1, s + 1, dtype=jnp.float32)
    means = []
    for g, w in enumerate(POOL_WINDOWS):
        csg = cs[:, :, g]
        csp = jnp.pad(csg, ((0, 0), (w, 0), (0, 0)))
        win_sum = csp[:, w:] - csp[:, :s]
        count = jnp.minimum(t1, float(w))[None, :, None]
        means.append(win_sum / count)
    mean = jnp.stack(means, axis=2).astype(p.dtype)
    d = mean - pg
    y = jnp.einsum('bsgc,gcd->bsgd', d, w_pool)
    return y.reshape(b, s, POOL_WIDTH) * pool_scale


def chunked_sgu(uv, sgu_ln_g, w_spatial, b_spatial):
    b, s, _ = uv.shape
    uv = jax.nn.gelu(uv, approximate=False)
    u, v = uv[..., :SGU_WIDTH], uv[..., SGU_WIDTH:]
    vh = v.reshape(b, s, SGU_HEADS, HEAD_DIM)
    vh = _norm_stats(vh) * sgu_ln_g.reshape(SGU_HEADS, HEAD_DIM)
    vc = vh.reshape(b, s // CHUNK, CHUNK, SGU_HEADS, HEAD_DIM)
    mask = jnp.tril(jnp.ones((CHUNK, CHUNK), dtype=w_spatial.dtype))
    wm = w_spatial * mask
    mixed = jnp.einsum('hts,bnshd->bnthd', wm, vc) + b_spatial.T[None, None, :, :, None]
    return u * mixed.reshape(b, s, SGU_WIDTH)


def swiglu(h, w_gate_up, w_down):
    gu = h @ w_gate_up
    g, u = gu[..., :D_FF], gu[..., D_FF:]
    return (jax.nn.silu(g) * u) @ w_down


def _fwd_setup_inputs(seed: int = 0) -> dict:
    key = jax.random.key(seed)
    ks = jax.random.split(key, 16)
    f32 = jnp.float32
    nrm = lambda k, shape: jax.random.normal(k, shape, dtype=f32)
    x = nrm(ks[0], (BATCH, SEQ, D_MODEL))
    w_in = nrm(ks[1], (DEPTH, D_MODEL, IN_WIDTH)) * D_MODEL ** -0.5
    w_conv = nrm(ks[2], (DEPTH, CONV_K, CONV_WIDTH)) * CONV_K ** -0.5
    w_pool = nrm(ks[3], (DEPTH, POOL_GROUPS, POOL_GROUP_DIM, POOL_GROUP_DIM)) * POOL_GROUP_DIM ** -0.5
    pool_scale = 1.0 + 0.1 * nrm(ks[4], (DEPTH, POOL_WIDTH))
    sgu_ln_g = 1.0 + 0.1 * nrm(ks[5], (DEPTH, SGU_WIDTH))
    w_spatial = nrm(ks[6], (DEPTH, SGU_HEADS, CHUNK, CHUNK)) * CHUNK ** -0.5
    b_spatial = 1.0 + 0.1 * nrm(ks[7], (DEPTH, SGU_HEADS, CHUNK))
    w_o = nrm(ks[8], (DEPTH, MIX_WIDTH, D_MODEL)) * (MIX_WIDTH ** -0.5) * BETA
    ln1_g = 1.0 + 0.1 * nrm(ks[9], (DEPTH, D_MODEL))
    ln1_b = 0.02 * nrm(ks[10], (DEPTH, D_MODEL))
    w_gate_up = nrm(ks[11], (DEPTH, D_MODEL, 2 * D_FF)) * D_MODEL ** -0.5
    w_down = nrm(ks[12], (DEPTH, D_FF, D_MODEL)) * (D_FF ** -0.5) * BETA
    ln2_g = 1.0 + 0.1 * nrm(ks[13], (DEPTH, D_MODEL))
    ln2_b = 0.02 * nrm(ks[14], (DEPTH, D_MODEL))
    return {"x": x, "w_in": w_in, "w_conv": w_conv, "w_pool": w_pool,
            "pool_scale": pool_scale, "sgu_ln_g": sgu_ln_g, "w_spatial": w_spatial,
            "b_spatial": b_spatial, "w_o": w_o, "ln1_g": ln1_g, "ln1_b": ln1_b,
            "w_gate_up": w_gate_up, "w_down": w_down, "ln2_g": ln2_g, "ln2_b": ln2_b}


def _fwd_reference(x, w_in, w_conv, w_pool, pool_scale, sgu_ln_g, w_spatial, b_spatial,
              w_o, ln1_g, ln1_b, w_gate_up, w_down, ln2_g, ln2_b):
    c0 = CONV_WIDTH
    for l in range(DEPTH):
        proj = x @ w_in[l]
        xa = proj[..., 0:c0]
        gb = proj[..., c0:2 * c0]
        gc = proj[..., 2 * c0:3 * c0]
        p = proj[..., 3 * c0:3 * c0 + POOL_WIDTH]
        uv = proj[..., 3 * c0 + POOL_WIDTH:]
        ya = short_gated_conv(xa, gb, gc, w_conv[l])
        yb = multiscale_pool(p, w_pool[l], pool_scale[l])
        yc = chunked_sgu(uv, sgu_ln_g[l], w_spatial[l], b_spatial[l])
        mix = jnp.concatenate([ya, yb, yc], axis=-1) @ w_o[l]
        h = layer_norm(ALPHA * x + mix, ln1_g[l], ln1_b[l])
        x = layer_norm(ALPHA * h + swiglu(h, w_gate_up[l], w_down[l]), ln2_g[l], ln2_b[l])
    return x


import jax as _jax
import jax.numpy as _jnp

TWIN_FORMAT = 'train_step'
FWD_PARAMS = ['x', 'w_in', 'w_conv', 'w_pool', 'pool_scale', 'sgu_ln_g', 'w_spatial', 'b_spatial', 'w_o', 'ln1_g', 'ln1_b', 'w_gate_up', 'w_down', 'ln2_g', 'ln2_b']
TWIN_WEIGHTS = ['w_in', 'w_conv', 'w_pool', 'pool_scale', 'sgu_ln_g', 'w_spatial', 'b_spatial', 'w_o', 'ln1_g', 'ln1_b', 'w_gate_up', 'w_down', 'ln2_g', 'ln2_b']
TWIN_DIFF_INPUT = 'x'
TWIN_INPUTS = ['x', 'w_in', 'w_conv', 'w_pool', 'pool_scale', 'sgu_ln_g', 'w_spatial', 'b_spatial', 'w_o', 'ln1_g', 'ln1_b', 'w_gate_up', 'w_down', 'ln2_g', 'ln2_b', 'loss_target', 'm_w_in', 'm_w_conv', 'm_w_pool', 'm_pool_scale', 'm_sgu_ln_g', 'm_w_spatial', 'm_b_spatial', 'm_w_o', 'm_ln1_g', 'm_ln1_b', 'm_w_gate_up', 'm_w_down', 'm_ln2_g', 'm_ln2_b', 'v_w_in', 'v_w_conv', 'v_w_pool', 'v_pool_scale', 'v_sgu_ln_g', 'v_w_spatial', 'v_b_spatial', 'v_w_o', 'v_ln1_g', 'v_ln1_b', 'v_w_gate_up', 'v_w_down', 'v_ln2_g', 'v_ln2_b']
TWIN_OUTPUTS = ['loss', 'grad_x', 'grad_w_in', 'grad_w_conv', 'grad_w_pool', 'grad_pool_scale', 'grad_sgu_ln_g', 'grad_w_spatial', 'grad_b_spatial', 'grad_w_o', 'grad_ln1_g', 'grad_ln1_b', 'grad_w_gate_up', 'grad_w_down', 'grad_ln2_g', 'grad_ln2_b', 'delta_w_in', 'delta_w_conv', 'delta_w_pool', 'delta_pool_scale', 'delta_sgu_ln_g', 'delta_w_spatial', 'delta_b_spatial', 'delta_w_o', 'delta_ln1_g', 'delta_ln1_b', 'delta_w_gate_up', 'delta_w_down', 'delta_ln2_g', 'delta_ln2_b', 'new_m_w_in', 'new_m_w_conv', 'new_m_w_pool', 'new_m_pool_scale', 'new_m_sgu_ln_g', 'new_m_w_spatial', 'new_m_b_spatial', 'new_m_w_o', 'new_m_ln1_g', 'new_m_ln1_b', 'new_m_w_gate_up', 'new_m_w_down', 'new_m_ln2_g', 'new_m_ln2_b', 'new_v_w_in', 'new_v_w_conv', 'new_v_w_pool', 'new_v_pool_scale', 'new_v_sgu_ln_g', 'new_v_w_spatial', 'new_v_b_spatial', 'new_v_w_o', 'new_v_ln1_g', 'new_v_ln1_b', 'new_v_w_gate_up', 'new_v_w_down', 'new_v_ln2_g', 'new_v_ln2_b']
TWIN_LEAF_KINDS = {'loss': 'loss', 'grad_x': 'grad_x', 'grad_w_in': 'grad_w', 'grad_w_conv': 'grad_w', 'grad_w_pool': 'grad_w', 'grad_pool_scale': 'grad_w', 'grad_sgu_ln_g': 'grad_w', 'grad_w_spatial': 'grad_w', 'grad_b_spatial': 'grad_w', 'grad_w_o': 'grad_w', 'grad_ln1_g': 'grad_w', 'grad_ln1_b': 'grad_w', 'grad_w_gate_up': 'grad_w', 'grad_w_down': 'grad_w', 'grad_ln2_g': 'grad_w', 'grad_ln2_b': 'grad_w', 'delta_w_in': 'delta_w', 'delta_w_conv': 'delta_w', 'delta_w_pool': 'delta_w', 'delta_pool_scale': 'delta_w', 'delta_sgu_ln_g': 'delta_w', 'delta_w_spatial': 'delta_w', 'delta_b_spatial': 'delta_w', 'delta_w_o': 'delta_w', 'delta_ln1_g': 'delta_w', 'delta_ln1_b': 'delta_w', 'delta_w_gate_up': 'delta_w', 'delta_w_down': 'delta_w', 'delta_ln2_g': 'delta_w', 'delta_ln2_b': 'delta_w', 'new_m_w_in': 'new_m', 'new_m_w_conv': 'new_m', 'new_m_w_pool': 'new_m', 'new_m_pool_scale': 'new_m', 'new_m_sgu_ln_g': 'new_m', 'new_m_w_spatial': 'new_m', 'new_m_b_spatial': 'new_m', 'new_m_w_o': 'new_m', 'new_m_ln1_g': 'new_m', 'new_m_ln1_b': 'new_m', 'new_m_w_gate_up': 'new_m', 'new_m_w_down': 'new_m', 'new_m_ln2_g': 'new_m', 'new_m_ln2_b': 'new_m', 'new_v_w_in': 'new_v', 'new_v_w_conv': 'new_v', 'new_v_w_pool': 'new_v', 'new_v_pool_scale': 'new_v', 'new_v_sgu_ln_g': 'new_v', 'new_v_w_spatial': 'new_v', 'new_v_b_spatial': 'new_v', 'new_v_w_o': 'new_v', 'new_v_ln1_g': 'new_v', 'new_v_ln1_b': 'new_v', 'new_v_w_gate_up': 'new_v', 'new_v_w_down': 'new_v', 'new_v_ln2_g': 'new_v', 'new_v_ln2_b': 'new_v'}


def _forward(args):
    return _fwd_reference(*[args[k] for k in FWD_PARAMS])


def _output_shape():
    out = _jax.eval_shape(lambda: _forward(_fwd_setup_inputs(0)))
    return out.shape, out.dtype

N_MICROBATCH = 1
ADAM_LR = 0.001
ADAM_B1 = 0.9
ADAM_B2 = 0.999
ADAM_EPS = 1e-08
ADAM_WD = 0.01
ADAM_STEP = 10
PER_EXAMPLE_BATCH_AXIS = {'x': 0, 'loss_target': 0}
SHARED_INPUTS = []
_WEIGHT_DTYPES = {'w_in': _jnp.float32, 'w_conv': _jnp.float32, 'w_pool': _jnp.float32, 'pool_scale': _jnp.float32, 'sgu_ln_g': _jnp.float32, 'w_spatial': _jnp.float32, 'b_spatial': _jnp.float32, 'w_o': _jnp.float32, 'ln1_g': _jnp.float32, 'ln1_b': _jnp.float32, 'w_gate_up': _jnp.float32, 'w_down': _jnp.float32, 'ln2_g': _jnp.float32, 'ln2_b': _jnp.float32}
MOMENT_SCALE = {'w_in': 3.056325e-02, 'w_conv': 3.362231e-02, 'w_pool': 3.237590e-02, 'pool_scale': 3.235126e-02, 'sgu_ln_g': 1.586072e-02, 'w_spatial': 1.095721e-02, 'b_spatial': 1.552780e-02, 'w_o': 7.993774e-02, 'ln1_g': 3.038187e+00, 'ln1_b': 3.446093e-01, 'w_gate_up': 1.244130e-02, 'w_down': 4.871691e-02, 'ln2_g': 9.263713e+00, 'ln2_b': 7.566226e-01}


def _to_microbatches(a, axis):
    t = _jnp.moveaxis(a, axis, 0)
    t = t.reshape((N_MICROBATCH, t.shape[0] // N_MICROBATCH) + t.shape[1:])
    return _jnp.moveaxis(t, 1, axis + 1)


def setup_inputs(seed: int = 0) -> dict:
    inp = _fwd_setup_inputs(seed)
    key = _jax.random.fold_in(_jax.random.key(seed), 7919)
    shape, _ = _output_shape()
    out = dict(inp)
    out["loss_target"] = _jax.random.normal(_jax.random.fold_in(key, 0), shape, _jnp.float32)
    for i, name in enumerate(TWIN_WEIGHTS):
        w = inp[name].astype(_jnp.float32)
        if MOMENT_SCALE is None:
            s = _jnp.sqrt(_jnp.mean(_jnp.square(w)) + 1e-30)
        else:
            s = MOMENT_SCALE[name]
        km, kv = _jax.random.split(_jax.random.fold_in(key, i + 1))
        out[name] = w
        out["m_" + name] = s * _jax.random.normal(km, w.shape, _jnp.float32)
        out["v_" + name] = (s * s) * _jax.random.uniform(kv, w.shape, _jnp.float32, 0.5, 1.5)
    if N_MICROBATCH > 1:
        for name, axis in PER_EXAMPLE_BATCH_AXIS.items():
            out[name] = _to_microbatches(out[name], axis)
    return {'x': out['x'], 'w_in': out['w_in'], 'w_conv': out['w_conv'], 'w_pool': out['w_pool'], 'pool_scale': out['pool_scale'], 'sgu_ln_g': out['sgu_ln_g'], 'w_spatial': out['w_spatial'], 'b_spatial': out['b_spatial'], 'w_o': out['w_o'], 'ln1_g': out['ln1_g'], 'ln1_b': out['ln1_b'], 'w_gate_up': out['w_gate_up'], 'w_down': out['w_down'], 'ln2_g': out['ln2_g'], 'ln2_b': out['ln2_b'], 'loss_target': out['loss_target'], 'm_w_in': out['m_w_in'], 'm_w_conv': out['m_w_conv'], 'm_w_pool': out['m_w_pool'], 'm_pool_scale': out['m_pool_scale'], 'm_sgu_ln_g': out['m_sgu_ln_g'], 'm_w_spatial': out['m_w_spatial'], 'm_b_spatial': out['m_b_spatial'], 'm_w_o': out['m_w_o'], 'm_ln1_g': out['m_ln1_g'], 'm_ln1_b': out['m_ln1_b'], 'm_w_gate_up': out['m_w_gate_up'], 'm_w_down': out['m_w_down'], 'm_ln2_g': out['m_ln2_g'], 'm_ln2_b': out['m_ln2_b'], 'v_w_in': out['v_w_in'], 'v_w_conv': out['v_w_conv'], 'v_w_pool': out['v_w_pool'], 'v_pool_scale': out['v_pool_scale'], 'v_sgu_ln_g': out['v_sgu_ln_g'], 'v_w_spatial': out['v_w_spatial'], 'v_b_spatial': out['v_b_spatial'], 'v_w_o': out['v_w_o'], 'v_ln1_g': out['v_ln1_g'], 'v_ln1_b': out['v_ln1_b'], 'v_w_gate_up': out['v_w_gate_up'], 'v_w_down': out['v_w_down'], 'v_ln2_g': out['v_ln2_g'], 'v_ln2_b': out['v_ln2_b']}


def _loss(weights, diff, rest, loss_target):
    with _jax.named_scope("forward"):
        args = {**rest, TWIN_DIFF_INPUT: diff, **{k: w.astype(_WEIGHT_DTYPES[k]) for k, w in weights.items()}}
        y = _forward(args)
    with _jax.named_scope("loss_head"):
        err = _jnp.square(y.astype(_jnp.float32) - loss_target)
        return 0.5 * _jnp.sum(_jnp.mean(err, axis=-1)) if err.ndim else 0.5 * err


def _adamw(w, g, m, v):
    m = ADAM_B1 * m + (1.0 - ADAM_B1) * g
    v = ADAM_B2 * v + (1.0 - ADAM_B2) * _jnp.square(g)
    m_hat = m / (1.0 - ADAM_B1 ** ADAM_STEP)
    v_hat = v / (1.0 - ADAM_B2 ** ADAM_STEP)
    delta = -ADAM_LR * (m_hat / (_jnp.sqrt(v_hat) + ADAM_EPS) + ADAM_WD * w)
    return delta, m, v


def reference(x, w_in, w_conv, w_pool, pool_scale, sgu_ln_g, w_spatial, b_spatial, w_o, ln1_g, ln1_b, w_gate_up, w_down, ln2_g, ln2_b, loss_target, m_w_in, m_w_conv, m_w_pool, m_pool_scale, m_sgu_ln_g, m_w_spatial, m_b_spatial, m_w_o, m_ln1_g, m_ln1_b, m_w_gate_up, m_w_down, m_ln2_g, m_ln2_b, v_w_in, v_w_conv, v_w_pool, v_pool_scale, v_sgu_ln_g, v_w_spatial, v_b_spatial, v_w_o, v_ln1_g, v_ln1_b, v_w_gate_up, v_w_down, v_ln2_g, v_ln2_b):
    given = dict(x=x, w_in=w_in, w_conv=w_conv, w_pool=w_pool, pool_scale=pool_scale, sgu_ln_g=sgu_ln_g, w_spatial=w_spatial, b_spatial=b_spatial, w_o=w_o, ln1_g=ln1_g, ln1_b=ln1_b, w_gate_up=w_gate_up, w_down=w_down, ln2_g=ln2_g, ln2_b=ln2_b, loss_target=loss_target, m_w_in=m_w_in, m_w_conv=m_w_conv, m_w_pool=m_w_pool, m_pool_scale=m_pool_scale, m_sgu_ln_g=m_sgu_ln_g, m_w_spatial=m_w_spatial, m_b_spatial=m_b_spatial, m_w_o=m_w_o, m_ln1_g=m_ln1_g, m_ln1_b=m_ln1_b, m_w_gate_up=m_w_gate_up, m_w_down=m_w_down, m_ln2_g=m_ln2_g, m_ln2_b=m_ln2_b, v_w_in=v_w_in, v_w_conv=v_w_conv, v_w_pool=v_w_pool, v_pool_scale=v_pool_scale, v_sgu_ln_g=v_sgu_ln_g, v_w_spatial=v_w_spatial, v_b_spatial=v_b_spatial, v_w_o=v_w_o, v_ln1_g=v_ln1_g, v_ln1_b=v_ln1_b, v_w_gate_up=v_w_gate_up, v_w_down=v_w_down, v_ln2_g=v_ln2_g, v_ln2_b=v_ln2_b)
    weights = {n: given[n] for n in TWIN_WEIGHTS}
    shared = {n: given[n] for n in SHARED_INPUTS}
    per_example = {n: given[n] for n in ['x']}
    grad_fn = _jax.value_and_grad(_loss, argnums=(0, 1))

    def one_microbatch(ex, loss_target):
        ex = dict(ex)
        diff = ex.pop(TWIN_DIFF_INPUT)
        return grad_fn(weights, diff, {**shared, **ex}, loss_target)

    if N_MICROBATCH == 1:
        loss, (grad_w, grad_x) = one_microbatch(per_example, given["loss_target"])
    else:
        def body(carry, xs):
            loss_sum, grad_sum = carry
            l_k, (gw_k, gx_k) = one_microbatch(xs[0], xs[1])
            with _jax.named_scope("update"):
                return (loss_sum + l_k, _jax.tree.map(_jnp.add, grad_sum, gw_k)), gx_k

        init = (_jnp.zeros((), _jnp.float32), _jax.tree.map(_jnp.zeros_like, weights))
        (loss, grad_w), grad_x = _jax.lax.scan(body, init, (per_example, given["loss_target"]))
    with _jax.named_scope("update"):
        delta_w, new_m, new_v = {}, {}, {}
        for n in TWIN_WEIGHTS:
            delta_w[n], new_m[n], new_v[n] = _adamw(weights[n], grad_w[n], given["m_" + n], given["v_" + n])
    return (loss, grad_x, *[grad_w[n] for n in TWIN_WEIGHTS], *[delta_w[n] for n in TWIN_WEIGHTS],
            *[new_m[n] for n in TWIN_WEIGHTS], *[new_v[n] for n in TWIN_WEIGHTS])
```

```python
import functools
import math

import jax
import jax.numpy as jnp
from jax import lax
from jax.experimental import pallas as pl
from jax.experimental.pallas import tpu as pltpu

F32 = jnp.float32
BF16 = jnp.bfloat16
MESH = pl.DeviceIdType.MESH

D_MODEL = 1024
DEPTH = 4
CONV_W = 384
POOL_W = 256
SGU_W = 384
IN_W = 3 * CONV_W + POOL_W + 2 * SGU_W
D_FF = 2816
CHUNK = 128
ALPHA = float((2 * DEPTH) ** 0.25)
LN_EPS = 1e-5
ADAM_LR, ADAM_B1, ADAM_B2, ADAM_EPS, ADAM_WD, ADAM_STEP = 0.001, 0.9, 0.999, 1e-08, 0.01, 10

N_DEV = 8
LANES = 128
HALF = 64
SHARD_ROWS = (IN_W // N_DEV, 2 * D_FF // N_DEV, D_MODEL // N_DEV, D_FF // N_DEV)
VMEM_LIMIT = 52 * 1024 * 1024

INV_SQRT2 = 0.7071067811865476
INV_SQRT_2PI = 0.3989422804014327


def _cparams(sem=None, **kw):
    if sem is not None:
        kw["dimension_semantics"] = sem
    return pltpu.CompilerParams(vmem_limit_bytes=VMEM_LIMIT, **kw)


_DN = {"nn": (((1,), (0,)), ((), ())), "nt": (((1,), (1,)), ((), ())), "tn": (((0,), (0,)), ((), ()))}


def _mm(a, b, mode, out_dtype, tm, tn, tk, name, b_layer=None, out_into=None, out_layer=None):
    bshape = b.shape[1:] if b_layer is not None else b.shape
    if mode == "nn":
        (M, K), N = a.shape, bshape[1]
    elif mode == "nt":
        (M, K), N = a.shape, bshape[0]
    else:
        (K, M), N = a.shape, bshape[1]
    assert M % tm == 0 and N % tn == 0 and K % tk == 0, (M, N, K, tm, tn, tk)
    nk = K // tk

    def body(*refs):
        if out_into is not None:
            a_ref, b_ref, _, o_ref = refs[:4]
            acc_ref = refs[4] if nk > 1 else None
        else:
            a_ref, b_ref, o_ref = refs[:3]
            acc_ref = refs[3] if nk > 1 else None
        p = lax.dot_general(a_ref[...], b_ref[...], _DN[mode], preferred_element_type=F32)
        if nk == 1:
            o_ref[...] = p.astype(o_ref.dtype)
        else:
            k = pl.program_id(2)

            @pl.when(k == 0)
            def _():
                acc_ref[...] = p

            @pl.when(k > 0)
            def _():
                acc_ref[...] += p

            @pl.when(k == nk - 1)
            def _():
                o_ref[...] = acc_ref[...].astype(o_ref.dtype)

    if mode == "nn":
        a_spec = pl.BlockSpec((tm, tk), lambda i, j, k: (i, k))
        b_blk, b_idx = (tk, tn), (lambda i, j, k: (k, j))
    elif mode == "nt":
        a_spec = pl.BlockSpec((tm, tk), lambda i, j, k: (i, k))
        b_blk, b_idx = (tn, tk), (lambda i, j, k: (j, k))
    else:
        a_spec = pl.BlockSpec((tk, tm), lambda i, j, k: (k, i))
        b_blk, b_idx = (tk, tn), (lambda i, j, k: (k, j))
    if b_layer is not None:
        b_spec = pl.BlockSpec((None,) + b_blk, lambda i, j, k: (b_layer,) + b_idx(i, j, k))
    else:
        b_spec = pl.BlockSpec(b_blk, b_idx)
    in_specs = [a_spec, b_spec]
    args = [a, b]
    aliases = {}
    if out_into is not None:
        out_shape = jax.ShapeDtypeStruct(out_into.shape, out_into.dtype)
        out_spec = pl.BlockSpec((None, tm, tn), lambda i, j, k: (out_layer, i, j))
        in_specs.append(pl.BlockSpec(memory_space=pl.ANY))
        args.append(out_into)
        aliases = {2: 0}
    else:
        out_shape = jax.ShapeDtypeStruct((M, N), out_dtype)
        out_spec = pl.BlockSpec((tm, tn), lambda i, j, k: (i, j))
    return pl.pallas_call(
        body,
        name=name,
        grid=(M // tm, N // tn, nk),
        in_specs=in_specs,
        out_specs=out_spec,
        out_shape=out_shape,
        scratch_shapes=[pltpu.VMEM((tm, tn), F32)] if nk > 1 else [],
        input_output_aliases=aliases,
        compiler_params=_cparams(("parallel", "parallel", "arbitrary")),
    )(*args)


def _gelu(x):
    return 0.5 * x * (1.0 + lax.erf(x * INV_SQRT2))


def _gelu_grad(x):
    return 0.5 * (1.0 + lax.erf(x * INV_SQRT2)) + x * (jnp.exp(-0.5 * x * x) * INV_SQRT_2PI)


def _shift_down(z, k):
    row = lax.broadcasted_iota(jnp.int32, z.shape, 0)
    return jnp.where(row >= k, pltpu.roll(z, k, 0), 0.0)


def _shift_up(z, k):
    n = z.shape[0]
    row = lax.broadcasted_iota(jnp.int32, z.shape, 0)
    return jnp.where(row < n - k, pltpu.roll(z, n - k, 0), 0.0)


def _lo_mask(shape):
    return lax.broadcasted_iota(jnp.int32, shape, len(shape) - 1) < HALF


def _seg_mean(x, lo):
    a = jnp.sum(jnp.where(lo, x, 0.0), axis=-1, keepdims=True)
    b = jnp.sum(jnp.where(lo, 0.0, x), axis=-1, keepdims=True)
    return jnp.where(lo, a, b) * (1.0 / HALF)


def _pool_windows(first):
    lo = _lo_mask((1, LANES))
    return jnp.where(first, jnp.where(lo, 2.0, 4.0), jnp.where(lo, 8.0, 16.0)), lo


def _pool_mean_minus_token(p, first):
    wl, lo = _pool_windows(first)
    s2 = p + _shift_down(p, 1)
    s4 = s2 + _shift_down(s2, 2)
    s8 = s4 + _shift_down(s4, 4)
    s16 = s8 + _shift_down(s8, 8)
    win = jnp.where(first, jnp.where(lo, s2, s4), jnp.where(lo, s8, s16))
    t1 = (lax.broadcasted_iota(jnp.int32, p.shape, 0) + 1).astype(F32)
    count = jnp.minimum(t1, wl)
    return win / count - p, count


def _tril_keep():
    r = lax.broadcasted_iota(jnp.int32, (2 * CHUNK, CHUNK), 0)
    s = lax.broadcasted_iota(jnp.int32, (2 * CHUNK, CHUNK), 1)
    return s <= (r & (CHUNK - 1))


def _sgu_chunk_fwd(u, v, g, wm, bias, lo):
    ug = _gelu(u)
    vg = _gelu(v)
    mu = _seg_mean(vg, lo)
    xc = vg - mu
    var = _seg_mean(xc * xc, lo)
    rstd = lax.rsqrt(var + LN_EPS)
    vn = xc * rstd
    vh = (vn * g).astype(BF16)
    mm2 = jnp.dot(wm, vh, preferred_element_type=F32)
    mixed = jnp.where(lo, mm2[:CHUNK], mm2[CHUNK:]) + bias
    return ug, vn, rstd, vh, mixed


def _mixer_fwd(proj, wconv, wpool_bd, pscale, lng, wsp, bias):
    T = proj.shape[0]
    nchunk = T // CHUNK

    def body(a_ref, b_ref, c_ref, wc_ref, wp_ref, ps_ref, lng_ref, wsp_ref, bias_ref, o_ref):
        j = pl.program_id(0)

        @pl.when(j < 3)
        def _conv():
            z = c_ref[...] * a_ref[...]
            w = wc_ref[...]
            y = w[0:1] * _shift_down(z, 2) + w[1:2] * _shift_down(z, 1) + w[2:3] * z
            o_ref[...] = (b_ref[...] * y).astype(o_ref.dtype)

        @pl.when((j >= 3) & (j < 5))
        def _pool():
            d, _ = _pool_mean_minus_token(a_ref[...], j == 3)
            y = jnp.dot(d.astype(BF16), wp_ref[...].astype(BF16), preferred_element_type=F32)
            o_ref[...] = (y * ps_ref[...]).astype(o_ref.dtype)

        @pl.when(j >= 5)
        def _sgu():
            lo = _lo_mask((CHUNK, LANES))
            wm = jnp.where(_tril_keep(), wsp_ref[...], 0.0).astype(BF16)
            bias_t = bias_ref[...]
            g = lng_ref[...]

            def chunk(n, carry):
                rows = pl.ds(pl.multiple_of(n * CHUNK, CHUNK), CHUNK)
                ug, _, _, _, mixed = _sgu_chunk_fwd(a_ref[rows, :], b_ref[rows, :], g, wm, bias_t, lo)
                o_ref[rows, :] = (ug * mixed).astype(o_ref.dtype)
                return carry

            lax.fori_loop(0, nchunk, chunk, 0)

    def col(f):
        return lambda j: (0, f(j))

    clip = lambda v, lo, hi: jnp.minimum(jnp.maximum(v, lo), hi)
    return pl.pallas_call(
        body,
        name="mixer_fwd",
        grid=(8,),
        in_specs=[
            pl.BlockSpec((T, LANES), col(lambda j: jnp.where(j < 3, j, jnp.where(j < 5, j + 6, j + 6)))),
            pl.BlockSpec((T, LANES), col(lambda j: jnp.where(j < 3, j + 3, jnp.where(j < 5, 5, j + 9)))),
            pl.BlockSpec((T, LANES), col(lambda j: jnp.where(j < 3, j + 6, 8))),
            pl.BlockSpec((3, LANES), col(lambda j: clip(j, 0, 2))),
            pl.BlockSpec((None, LANES, LANES), lambda j: (clip(j - 3, 0, 1), 0, 0)),
            pl.BlockSpec((1, LANES), col(lambda j: clip(j - 3, 0, 1))),
            pl.BlockSpec((1, LANES), col(lambda j: clip(j - 5, 0, 2))),
            pl.BlockSpec((None, 2 * CHUNK, CHUNK), lambda j: (clip(j - 5, 0, 2), 0, 0)),
            pl.BlockSpec((None, CHUNK, LANES), lambda j: (clip(j - 5, 0, 2), 0, 0)),
        ],
        out_specs=pl.BlockSpec((T, LANES), lambda j: (0, j)),
        out_shape=jax.ShapeDtypeStruct((T, D_MODEL), BF16),
        compiler_params=_cparams(("arbitrary",)),
    )(proj, proj, proj, wconv, wpool_bd, pscale, lng, wsp, bias)


def _mixer_bwd(proj, dmix, wconv, wpool_bd, pscale, lng, wsp, bias):
    T = proj.shape[0]
    nchunk = T // CHUNK

    def body(a_ref, b_ref, c_ref, dm_ref, wc_ref, wp_ref, ps_ref, lng_ref, wsp_ref, bias_ref,
             o_ref, dwc_ref, dwp_ref, dps_ref, dlng_ref, dwsp_ref, dbias_ref, keep1, keep2):
        k = pl.program_id(0)

        @pl.when(k < 3)
        def _conv():
            xa, gb, gc, dya = a_ref[...], b_ref[...], c_ref[...], dm_ref[...]
            w = wc_ref[...]
            z = gc * xa
            z1 = _shift_down(z, 1)
            z2 = _shift_down(z, 2)
            y = w[0:1] * z2 + w[1:2] * z1 + w[2:3] * z
            dyv = dya * gb
            dz = w[2:3] * dyv + w[1:2] * _shift_up(dyv, 1) + w[0:1] * _shift_up(dyv, 2)
            dwc_ref[0:1, :] = jnp.sum(dyv * z2, axis=0, keepdims=True)
            dwc_ref[1:2, :] = jnp.sum(dyv * z1, axis=0, keepdims=True)
            dwc_ref[2:3, :] = jnp.sum(dyv * z, axis=0, keepdims=True)
            o_ref[...] = (dz * gc).astype(o_ref.dtype)
            keep1[k] = (dya * y).astype(keep1.dtype)
            keep1[k + 3] = (dz * xa).astype(keep1.dtype)

        @pl.when((k >= 3) & (k < 9))
        def _emit_gb_gc():
            o_ref[...] = keep1[k - 3]

        @pl.when((k >= 9) & (k < 11))
        def _pool():
            first = k == 9
            p, dyb = a_ref[...], dm_ref[...]
            d, count = _pool_mean_minus_token(p, first)
            w2 = wp_ref[...].astype(BF16)
            db = d.astype(BF16)
            y = jnp.dot(db, w2, preferred_element_type=F32)
            dps_ref[...] = jnp.sum(dyb * y, axis=0, keepdims=True)
            dyv = (dyb * ps_ref[...]).astype(BF16)
            dd = lax.dot_general(dyv, w2, _DN["nt"], preferred_element_type=F32)
            dwp_ref[...] = lax.dot_general(db, dyv, _DN["tn"], preferred_element_type=F32)
            dwin = dd / count
            a2 = dwin + _shift_up(dwin, 1)
            a4 = a2 + _shift_up(a2, 2)
            a8 = a4 + _shift_up(a4, 4)
            a16 = a8 + _shift_up(a8, 8)
            _, lo = _pool_windows(first)
            back = jnp.where(first, jnp.where(lo, a2, a4), jnp.where(lo, a8, a16))
            o_ref[...] = (back - dd).astype(o_ref.dtype)

        @pl.when((k >= 11) & (k < 14))
        def _sgu():
            lo = _lo_mask((CHUNK, LANES))
            keep = _tril_keep()
            wm = jnp.where(keep, wsp_ref[...], 0.0).astype(BF16)
            bias_t = bias_ref[...]
            g = lng_ref[...]
            dwsp_ref[...] = jnp.zeros_like(dwsp_ref)
            dbias_ref[...] = jnp.zeros_like(dbias_ref)
            dlng_ref[...] = jnp.zeros_like(dlng_ref)

            def chunk(n, carry):
                rows = pl.ds(pl.multiple_of(n * CHUNK, CHUNK), CHUNK)
                u, v, dyc = a_ref[rows, :], b_ref[rows, :], dm_ref[rows, :]
                ug, vn, rstd, vh, mixed = _sgu_chunk_fwd(u, v, g, wm, bias_t, lo)
                dmx = dyc * ug
                o_ref[rows, :] = (dyc * mixed * _gelu_grad(u)).astype(o_ref.dtype)
                dbias_ref[...] += dmx
                dst = jnp.concatenate([jnp.where(lo, dmx, 0.0), jnp.where(lo, 0.0, dmx)], axis=0).astype(BF16)
                dwsp_ref[...] += lax.dot_general(dst, vh, _DN["nt"], preferred_element_type=F32)
                dvh = lax.dot_general(wm, dst, _DN["tn"], preferred_element_type=F32)
                dlng_ref[...] += jnp.sum(dvh * vn, axis=0, keepdims=True)
                dvn = dvh * g
                m1 = _seg_mean(dvn, lo)
                m2 = _seg_mean(dvn * vn, lo)
                dvg = rstd * (dvn - m1 - vn * m2)
                keep2[k - 11, rows, :] = (dvg * _gelu_grad(v)).astype(keep2.dtype)
                return carry

            lax.fori_loop(0, nchunk, chunk, 0)
            dwsp_ref[...] = jnp.where(keep, dwsp_ref[...], 0.0)
            dbt = dbias_ref[...]
            lane = lax.broadcasted_iota(jnp.int32, (CHUNK, LANES), 1)
            sa = jnp.sum(jnp.where(lo, dbt, 0.0), axis=-1, keepdims=True)
            sb = jnp.sum(jnp.where(lo, 0.0, dbt), axis=-1, keepdims=True)
            dbias_ref[...] = jnp.where(lane == 0, sa, jnp.where(lane == 1, sb, 0.0))

        @pl.when(k >= 14)
        def _emit_v():
            o_ref[...] = keep2[k - 14]

    def col(f):
        return lambda k: (0, f(k))

    clip = lambda v, lo, hi: jnp.minimum(jnp.maximum(v, lo), hi)
    view_a = lambda k: jnp.where(k < 3, k, jnp.where(k < 9, 2, jnp.where(k < 14, k, 13)))
    view_b = lambda k: jnp.where(k < 3, k + 3, jnp.where(k < 11, 5, jnp.where(k < 14, k + 3, 16)))
    view_c = lambda k: jnp.where(k < 3, k + 6, 8)
    view_dm = lambda k: jnp.where(k < 3, k, jnp.where(k < 9, 2, jnp.where(k < 14, k - 6, 7)))
    return pl.pallas_call(
        body,
        name="mixer_bwd",
        grid=(17,),
        in_specs=[
            pl.BlockSpec((T, LANES), col(view_a)),
            pl.BlockSpec((T, LANES), col(view_b)),
            pl.BlockSpec((T, LANES), col(view_c)),
            pl.BlockSpec((T, LANES), col(view_dm)),
            pl.BlockSpec((3, LANES), col(lambda k: clip(k, 0, 2))),
            pl.BlockSpec((None, LANES, LANES), lambda k: (clip(k - 9, 0, 1), 0, 0)),
            pl.BlockSpec((1, LANES), col(lambda k: clip(k - 9, 0, 1))),
            pl.BlockSpec((1, LANES), col(lambda k: clip(k - 11, 0, 2))),
            pl.BlockSpec((None, 2 * CHUNK, CHUNK), lambda k: (clip(k - 11, 0, 2), 0, 0)),
            pl.BlockSpec((None, CHUNK, LANES), lambda k: (clip(k - 11, 0, 2), 0, 0)),
        ],
        out_specs=[
            pl.BlockSpec((T, LANES), lambda k: (0, k)),
            pl.BlockSpec((3, LANES), col(lambda k: clip(k, 0, 2))),
            pl.BlockSpec((None, LANES, LANES), lambda k: (clip(k - 9, 0, 1), 0, 0)),
            pl.BlockSpec((1, LANES), col(lambda k: clip(k - 9, 0, 1))),
            pl.BlockSpec((1, LANES), col(lambda k: clip(k - 11, 0, 2))),
            pl.BlockSpec((None, 2 * CHUNK, CHUNK), lambda k: (clip(k - 11, 0, 2), 0, 0)),
            pl.BlockSpec((None, CHUNK, LANES), lambda k: (clip(k - 11, 0, 2), 0, 0)),
        ],
        out_shape=[
            jax.ShapeDtypeStruct((T, IN_W), BF16),
            jax.ShapeDtypeStruct((3, CONV_W), F32),
            jax.ShapeDtypeStruct((2, LANES, LANES), F32),
            jax.ShapeDtypeStruct((1, POOL_W), F32),
            jax.ShapeDtypeStruct((1, SGU_W), F32),
            jax.ShapeDtypeStruct((3, 2 * CHUNK, CHUNK), F32),
            jax.ShapeDtypeStruct((3, CHUNK, LANES), F32),
        ],
        scratch_shapes=[pltpu.VMEM((6, T, LANES), BF16), pltpu.VMEM((3, T, LANES), BF16)],
        compiler_params=_cparams(("arbitrary",)),
    )(proj, proj, proj, dmix, wconv, wpool_bd, pscale, lng, wsp, bias)


def _ln_fwd(prev, pg, pb, mmout, g, b, tm=256):
    T = prev.shape[0]

    def body(prev_ref, pg_ref, pb_ref, mm_ref, g_ref, b_ref, xhat_ref, rstd_ref, y_ref):
        r = ALPHA * (prev_ref[...] * pg_ref[...] + pb_ref[...]) + mm_ref[...]
        mu = jnp.mean(r, axis=-1, keepdims=True)
        xc = r - mu
        var = jnp.mean(xc * xc, axis=-1, keepdims=True)
        rstd = lax.rsqrt(var + LN_EPS)
        xhat = xc * rstd
        xhat_ref[...] = xhat
        rstd_ref[...] = rstd
        y_ref[...] = (xhat * g_ref[...] + b_ref[...]).astype(y_ref.dtype)

    row = pl.BlockSpec((tm, D_MODEL), lambda i: (i, 0))
    vec = pl.BlockSpec((1, D_MODEL), lambda i: (0, 0))
    return pl.pallas_call(
        body,
        name="ln_fwd",
        grid=(T // tm,),
        in_specs=[row, vec, vec, row, vec, vec],
        out_specs=[row, pl.BlockSpec((tm, 1), lambda i: (i, 0)), row],
        out_shape=[jax.ShapeDtypeStruct((T, D_MODEL), F32), jax.ShapeDtypeStruct((T, 1), F32),
                   jax.ShapeDtypeStruct((T, D_MODEL), BF16)],
        compiler_params=_cparams(("parallel",)),
    )(prev, pg, pb, mmout, g, b)


def _ln_bwd(dres, dmm, xhat, rstd, g, tm=256):
    T = xhat.shape[0]
    has_res = dres is not None

    def body(*refs):
        if has_res:
            dres_ref, dmm_ref, xhat_ref, rstd_ref, g_ref, dr_ref, drb_ref, dg_ref, db_ref = refs
            dy = ALPHA * dres_ref[...] + dmm_ref[...]
        else:
            dmm_ref, xhat_ref, rstd_ref, g_ref, dr_ref, drb_ref, dg_ref, db_ref = refs
            dy = dmm_ref[...]
        xhat_v = xhat_ref[...]

        @pl.when(pl.program_id(0) == 0)
        def _():
            dg_ref[...] = jnp.zeros_like(dg_ref)
            db_ref[...] = jnp.zeros_like(db_ref)

        dg_ref[...] += jnp.sum(dy * xhat_v, axis=0, keepdims=True)
        db_ref[...] += jnp.sum(dy, axis=0, keepdims=True)
        dxh = dy * g_ref[...]
        m1 = jnp.mean(dxh, axis=-1, keepdims=True)
        m2 = jnp.mean(dxh * xhat_v, axis=-1, keepdims=True)
        dr = rstd_ref[...] * (dxh - m1 - xhat_v * m2)
        dr_ref[...] = dr
        drb_ref[...] = dr.astype(drb_ref.dtype)

    row = pl.BlockSpec((tm, D_MODEL), lambda i: (i, 0))
    vec = pl.BlockSpec((1, D_MODEL), lambda i: (0, 0))
    in_specs = ([row] if has_res else []) + [row, row, pl.BlockSpec((tm, 1), lambda i: (i, 0)), vec]
    args = ([dres] if has_res else []) + [dmm, xhat, rstd, g]
    return pl.pallas_call(
        body,
        name="ln_bwd_res" if has_res else "ln_bwd",
        grid=(T // tm,),
        in_specs=in_specs,
        out_specs=[row, row, vec, vec],
        out_shape=[jax.ShapeDtypeStruct((T, D_MODEL), F32), jax.ShapeDtypeStruct((T, D_MODEL), BF16),
                   jax.ShapeDtypeStruct((1, D_MODEL), F32), jax.ShapeDtypeStruct((1, D_MODEL), F32)],
        compiler_params=_cparams(("arbitrary",)),
    )(*args)


def _loss_head(xhat, g, b, target, tm=256):
    T = xhat.shape[0]

    def body(xhat_ref, g_ref, b_ref, t_ref, loss_ref, dy_ref):
        err = xhat_ref[...] * g_ref[...] + b_ref[...] - t_ref[...]

        @pl.when(pl.program_id(0) == 0)
        def _():
            loss_ref[...] = jnp.zeros_like(loss_ref)

        part = jnp.sum(jnp.sum(err * err, axis=-1, keepdims=True), axis=0, keepdims=True)
        loss_ref[...] += jnp.broadcast_to(part * (0.5 / D_MODEL), loss_ref.shape)
        dy_ref[...] = err * (1.0 / D_MODEL)

    row = pl.BlockSpec((tm, D_MODEL), lambda i: (i, 0))
    vec = pl.BlockSpec((1, D_MODEL), lambda i: (0, 0))
    return pl.pallas_call(
        body,
        name="loss_head",
        grid=(T // tm,),
        in_specs=[row, vec, vec, row],
        out_specs=[pl.BlockSpec((8, LANES), lambda i: (0, 0)), row],
        out_shape=[jax.ShapeDtypeStruct((8, LANES), F32), jax.ShapeDtypeStruct((T, D_MODEL), F32)],
        compiler_params=_cparams(("arbitrary",)),
    )(xhat, g, b, target)


def _residual_out(dres, dmm, tm=256):
    T = dres.shape[0]

    def body(a_ref, b_ref, o_ref):
        o_ref[...] = ALPHA * a_ref[...] + b_ref[...]

    row = pl.BlockSpec((tm, D_MODEL), lambda i: (i, 0))
    return pl.pallas_call(
        body, name="residual_out", grid=(T // tm,), in_specs=[row, row], out_specs=row,
        out_shape=jax.ShapeDtypeStruct((T, D_MODEL), F32), compiler_params=_cparams(("parallel",)),
    )(dres, dmm)


SW_TC = 1408


def _swiglu_fwd(gu, tm=128):
    T = gu.shape[0]

    def body(gu_ref, o_ref):
        gv = gu_ref[:, :D_FF]
        o_ref[...] = (gv * jax.nn.sigmoid(gv) * gu_ref[:, D_FF:]).astype(o_ref.dtype)

    return pl.pallas_call(
        body, name="swiglu_fwd", grid=(T // tm,),
        in_specs=[pl.BlockSpec((tm, 2 * D_FF), lambda i: (i, 0))],
        out_specs=pl.BlockSpec((tm, D_FF), lambda i: (i, 0)),
        out_shape=jax.ShapeDtypeStruct((T, D_FF), BF16), compiler_params=_cparams(("parallel",)),
    )(gu)


def _swiglu_bwd(gu, dact, tm=128):
    T = gu.shape[0]

    def body(gu_ref, da_ref, dgu_ref, act_ref):
        gv, uv, da = gu_ref[:, :D_FF], gu_ref[:, D_FF:], da_ref[...]
        s = jax.nn.sigmoid(gv)
        sg = gv * s
        act_ref[...] = (sg * uv).astype(act_ref.dtype)
        dgu_ref[:, D_FF:] = (da * sg).astype(dgu_ref.dtype)
        dgu_ref[:, :D_FF] = (da * uv * (s * (1.0 + gv * (1.0 - s)))).astype(dgu_ref.dtype)

    wide = pl.BlockSpec((tm, 2 * D_FF), lambda i: (i, 0))
    half = pl.BlockSpec((tm, D_FF), lambda i: (i, 0))
    return pl.pallas_call(
        body, name="swiglu_bwd", grid=(T // tm,),
        in_specs=[wide, half], out_specs=[wide, half],
        out_shape=[jax.ShapeDtypeStruct((T, 2 * D_FF), BF16), jax.ShapeDtypeStruct((T, D_FF), BF16)],
        compiler_params=_cparams(("parallel",)),
    )(gu, dact)


def _adamw(w, g, m, v, tr):
    R, C = w.shape
    assert R % tr == 0
    c1 = 1.0 - ADAM_B1 ** ADAM_STEP
    c2 = 1.0 - ADAM_B2 ** ADAM_STEP

    def body(w_ref, g_ref, m_ref, v_ref, d_ref, mo_ref, vo_ref):
        gv = g_ref[...]
        mn = ADAM_B1 * m_ref[...] + (1.0 - ADAM_B1) * gv
        vn = ADAM_B2 * v_ref[...] + (1.0 - ADAM_B2) * (gv * gv)
        d_ref[...] = -ADAM_LR * ((mn / c1) / (jnp.sqrt(vn / c2) + ADAM_EPS) + ADAM_WD * w_ref[...])
        mo_ref[...] = mn
        vo_ref[...] = vn

    blk = pl.BlockSpec((tr, C), lambda i: (i, 0))
    return pl.pallas_call(
        body, name="adamw", grid=(R // tr,), in_specs=[blk] * 4, out_specs=[blk] * 3,
        out_shape=[jax.ShapeDtypeStruct((R, C), F32)] * 3, compiler_params=_cparams(("parallel",)),
    )(w, g, m, v)


def _my_place():
    return lax.axis_index("x"), lax.axis_index("y"), lax.axis_index("c")


ANY = pl.BlockSpec(memory_space=pl.ANY)


def _allgather_weights(shards):
    nw = len(shards)

    def body(*refs):
        s_refs, o_refs = refs[:nw], refs[nw:2 * nw]
        send_sems, recv_sems, local_sems = refs[2 * nw:]
        x, y, c = _my_place()
        me, sibling = (x, y, c), (x, y, 1 - c)
        chips = [(1 - x, y), (x, 1 - y), (1 - x, 1 - y)]

        def rows(w, dev):
            r = SHARD_ROWS[w]
            start = pl.multiple_of((4 * dev[0] + 2 * dev[1] + dev[2]) * r, 16)
            return o_refs[w].at[:, pl.ds(start, r), :]

        def copy(k, w, block, to, from_shard=False):
            return pltpu.make_async_remote_copy(
                src_ref=s_refs[w] if from_shard else rows(w, block), dst_ref=rows(w, block),
                send_sem=send_sems.at[k, w], recv_sem=recv_sems.at[k, w], device_id=to, device_id_type=MESH)

        mine = [pltpu.make_async_copy(s_refs[w], rows(w, me), local_sems.at[w]) for w in range(nw)]
        for cp in mine:
            cp.start()
        first = [copy(0, w, me, sibling, True) for w in range(nw)]
        for j, chip in enumerate(chips):
            first += [copy(1 + j, w, me, (*chip, c), True) for w in range(nw)]
        for cp in first:
            cp.start()
        passed = []
        for j, chip in enumerate(chips):
            for w in range(nw):
                copy(1 + j, w, (*chip, c), me).wait_recv()
            fwd = [copy(4 + j, w, (*chip, c), sibling) for w in range(nw)]
            for cp in fwd:
                cp.start()
            passed += fwd
        for w in range(nw):
            copy(0, w, sibling, me).wait_recv()
        for j, chip in enumerate(chips):
            for w in range(nw):
                copy(4 + j, w, (*chip, 1 - c), me).wait_recv()
        for cp in first + passed:
            cp.wait_send()
        for cp in mine:
            cp.wait()

    L = shards[0].shape[0]
    return pl.pallas_call(
        body, name="allgather_weights",
        in_specs=[ANY] * nw, out_specs=[ANY] * nw,
        out_shape=[jax.ShapeDtypeStruct((L, N_DEV * s.shape[1], D_MODEL), BF16) for s in shards],
        scratch_shapes=[pltpu.SemaphoreType.DMA((7, nw)), pltpu.SemaphoreType.DMA((7, nw)),
                        pltpu.SemaphoreType.DMA((nw,))],
    )(*shards)


def _rs_sibling_exchange(parts):
    nw = len(parts)

    def body(*refs):
        p_refs, o_refs = refs[:nw], refs[nw:2 * nw]
        send_sems, recv_sems = refs[2 * nw:]
        x, y, c = _my_place()
        copies = [pltpu.make_async_remote_copy(
            src_ref=p_refs[w].at[:, :, 1 - c], dst_ref=o_refs[w],
            send_sem=send_sems.at[w], recv_sem=recv_sems.at[w], device_id=(x, y, 1 - c), device_id_type=MESH)
            for w in range(nw)]
        for cp in copies:
            cp.start()
        for cp in copies:
            cp.wait()

    return pl.pallas_call(
        body, name="rs_sibling_exchange",
        in_specs=[ANY] * nw, out_specs=[ANY] * nw,
        out_shape=[jax.ShapeDtypeStruct(p.shape[:2] + p.shape[3:], BF16) for p in parts],
        scratch_shapes=[pltpu.SemaphoreType.DMA((nw,)), pltpu.SemaphoreType.DMA((nw,))],
    )(*parts)


def _rs_chip_sum(part, got, c):
    L, nxy, _, r, _ = part.shape

    def body(c_ref, p_ref, g_ref, o_ref):
        o_ref[...] = (p_ref[...].astype(F32) + g_ref[...].astype(F32)).astype(o_ref.dtype)

    return pl.pallas_call(
        body, name="rs_chip_sum",
        grid_spec=pltpu.PrefetchScalarGridSpec(
            num_scalar_prefetch=1, grid=(L, nxy),
            in_specs=[pl.BlockSpec((None, None, None, r, D_MODEL), lambda l, q, c_ref: (l, q, c_ref[0], 0, 0)),
                      pl.BlockSpec((None, None, r, D_MODEL), lambda l, q, c_ref: (l, q, 0, 0))],
            out_specs=pl.BlockSpec((None, None, r, D_MODEL), lambda l, q, c_ref: (l, q, 0, 0))),
        out_shape=jax.ShapeDtypeStruct(got.shape, BF16),
        compiler_params=_cparams(("parallel", "parallel")),
    )(c, part, got)


def _rs_chip_exchange(sums):
    nw = len(sums)

    def body(*refs):
        s_refs, o_refs = refs[:nw], refs[nw:2 * nw]
        send_sems, recv_sems = refs[2 * nw:]
        x, y, c = _my_place()
        chips = [(1 - x, y), (x, 1 - y), (1 - x, 1 - y)]
        copies = []
        for k, chip in enumerate(chips):
            q = 2 * chip[0] + chip[1]
            copies += [pltpu.make_async_remote_copy(
                src_ref=s_refs[w].at[:, q], dst_ref=o_refs[w].at[:, k],
                send_sem=send_sems.at[k, w], recv_sem=recv_sems.at[k, w], device_id=(*chip, c), device_id_type=MESH)
                for w in range(nw)]
        for cp in copies:
            cp.start()
        for cp in copies:
            cp.wait()

    return pl.pallas_call(
        body, name="rs_chip_exchange",
        in_specs=[ANY] * nw, out_specs=[ANY] * nw,
        out_shape=[jax.ShapeDtypeStruct((s.shape[0], 3) + s.shape[2:], BF16) for s in sums],
        scratch_shapes=[pltpu.SemaphoreType.DMA((3, nw)), pltpu.SemaphoreType.DMA((3, nw))],
    )(*sums)


def _rs_finish(sums, got, q):
    L, _, r, _ = sums.shape

    def body(q_ref, s_ref, g_ref, o_ref):
        o_ref[...] = ((s_ref[...].astype(F32) + g_ref[0].astype(F32)) + g_ref[1].astype(F32)) + g_ref[2].astype(F32)

    return pl.pallas_call(
        body, name="rs_finish",
        grid_spec=pltpu.PrefetchScalarGridSpec(
            num_scalar_prefetch=1, grid=(L,),
            in_specs=[pl.BlockSpec((None, None, r, D_MODEL), lambda l, q_ref: (l, q_ref[0], 0, 0)),
                      pl.BlockSpec((None, 3, r, D_MODEL), lambda l, q_ref: (l, 0, 0, 0))],
            out_specs=pl.BlockSpec((None, r, D_MODEL), lambda l, q_ref: (l, 0, 0))),
        out_shape=jax.ShapeDtypeStruct((L, r, D_MODEL), F32),
        compiler_params=_cparams(("parallel",)),
    )(q, sums, got)


def _allreduce_small(vec):
    R = vec.shape[0]

    def body(v_ref, o_ref, buf, send_sems, recv_sems):
        x, y, c = _my_place()
        me = 4 * x + 2 * y + c
        buf[0] = v_ref[...]
        copies = []
        for k in range(1, N_DEV):
            p = me ^ k
            copies.append(pltpu.make_async_remote_copy(
                src_ref=v_ref, dst_ref=buf.at[k], send_sem=send_sems.at[k - 1], recv_sem=recv_sems.at[k - 1],
                device_id=(p >> 2, (p >> 1) & 1, p & 1), device_id_type=MESH))
        for cp in copies:
            cp.start()
        for cp in copies:
            cp.wait()
        acc = buf[me]
        for d in range(1, N_DEV):
            acc = acc + buf[me ^ d]
        o_ref[...] = acc

    return pl.pallas_call(
        body, name="allreduce_small",
        in_specs=[pl.BlockSpec(memory_space=pltpu.VMEM)], out_specs=pl.BlockSpec(memory_space=pltpu.VMEM),
        out_shape=jax.ShapeDtypeStruct((R, LANES), F32),
        scratch_shapes=[pltpu.VMEM((N_DEV, R, LANES), F32), pltpu.SemaphoreType.DMA((N_DEV - 1,)),
                        pltpu.SemaphoreType.DMA((N_DEV - 1,))],
        compiler_params=_cparams(),
    )(vec)


def _pack(arrs):
    flat = jnp.concatenate([a.reshape(-1) for a in arrs])
    pad = (-flat.shape[0]) % (8 * LANES)
    return jnp.pad(flat, (0, pad)).reshape(-1, LANES)


def _unpack(packed, shapes):
    flat = packed.reshape(-1)
    out, off = [], 0
    for s in shapes:
        n = math.prod(s)
        out.append(flat[off:off + n].reshape(s))
        off += n
    return out


def kernel(x, w_in, w_conv, w_pool, pool_scale, sgu_ln_g, w_spatial, b_spatial, w_o, ln1_g, ln1_b, w_gate_up, w_down, ln2_g, ln2_b, loss_target, m_w_in, m_w_conv, m_w_pool, m_pool_scale, m_sgu_ln_g, m_w_spatial, m_b_spatial, m_w_o, m_ln1_g, m_ln1_b, m_w_gate_up, m_w_down, m_ln2_g, m_ln2_b, v_w_in, v_w_conv, v_w_pool, v_pool_scale, v_sgu_ln_g, v_w_spatial, v_b_spatial, v_w_o, v_ln1_g, v_ln1_b, v_w_gate_up, v_w_down, v_ln2_g, v_ln2_b):
    L = DEPTH
    T = x.shape[1]
    mx, my, mc = _my_place()
    dev = 4 * mx + 2 * my + mc
    xs = x[0]
    target = loss_target[0]

    shards = (jnp.swapaxes(w_in, 1, 2).astype(BF16), jnp.swapaxes(w_gate_up, 1, 2).astype(BF16),
              w_o.astype(BF16), w_down.astype(BF16))
    g_in, g_gu, g_o, g_dn = _allgather_weights(shards)

    conv_cols = w_conv.shape[2]
    w_conv_z = lax.dynamic_update_slice(jnp.zeros((L, 3, CONV_W), F32), w_conv, (0, 0, dev * conv_cols))
    w_conv_full = _allreduce_small(_pack([w_conv_z]))
    w_conv_full = _unpack(w_conv_full, [(L, 3, CONV_W)])[0]

    loss_tile, grad_x2, (p_in, p_gu, p_o, p_dn), small_grads = _local_step(
        xs, target, g_in, g_gu, g_o, g_dn, w_conv_full, w_pool, pool_scale, sgu_ln_g, w_spatial, b_spatial,
        ln1_g, ln1_b, ln2_g, ln2_b)
    loss = lax.psum(loss_tile[0, 0], ("x", "y", "c"))
    grad_x = grad_x2[None]
    big_w = (w_in, w_gate_up, w_o, w_down)
    big_m = (m_w_in, m_w_gate_up, m_w_o, m_w_down)
    big_v = (v_w_in, v_w_gate_up, v_w_o, v_w_down)
    small_w = [w_conv_full, w_pool, pool_scale, sgu_ln_g, w_spatial, b_spatial, ln1_g, ln1_b, ln2_g, ln2_b]
    small_m = [m_w_conv, m_w_pool, m_pool_scale, m_sgu_ln_g, m_w_spatial, m_b_spatial, m_ln1_g, m_ln1_b, m_ln2_g, m_ln2_b]
    small_v = [v_w_conv, v_w_pool, v_pool_scale, v_sgu_ln_g, v_w_spatial, v_b_spatial, v_ln1_g, v_ln1_b, v_ln2_g, v_ln2_b]
    grads, deltas, new_m, new_v = _reduce_and_update(
        (p_in, p_gu, p_o, p_dn), small_grads, big_w, big_m, big_v, small_w, small_m, small_v)
    return (loss, grad_x, *grads, *deltas, *new_m, *new_v)


def _local_step(xs, target, g_in, g_gu, g_o, g_dn, w_conv_full, w_pool, pool_scale, sgu_ln_g, w_spatial, b_spatial,
                ln1_g, ln1_b, ln2_g, ln2_b):
    L = DEPTH
    T = xs.shape[0]
    eye2 = jnp.eye(2, dtype=F32)
    wp = w_pool.reshape(L, 2, 2, HALF, HALF)
    wpool_bd = jnp.einsum("ltgcd,gh->ltgchd", wp, eye2).reshape(L, 2, LANES, LANES)
    wsp_t = w_spatial.reshape(L, 3, 2 * CHUNK, CHUNK)
    bias_t = jnp.repeat(jnp.swapaxes(b_spatial.reshape(L, 3, 2, CHUNK), 2, 3), HALF, axis=3)
    ones = jnp.ones((1, D_MODEL), F32)
    zeros = jnp.zeros((1, D_MODEL), F32)

    saved = []
    prev, pg, pb = xs, ones, zeros
    prev_b = xs.astype(BF16)
    for l in range(L):
        proj = _mm(prev_b, g_in, "nt", F32, 512, IN_W, D_MODEL, "mm_proj", b_layer=l)
        mixcat = _mixer_fwd(proj, w_conv_full[l], wpool_bd[l], pool_scale[l][None], sgu_ln_g[l][None], wsp_t[l], bias_t[l])
        mix = _mm(mixcat, g_o, "nn", F32, T, 512, D_MODEL, "mm_wo", b_layer=l)
        xhat1, rstd1, h_b = _ln_fwd(prev, pg, pb, mix, ln1_g[l][None], ln1_b[l][None])
        gu = _mm(h_b, g_gu, "nt", F32, T, 512, D_MODEL, "mm_gate_up", b_layer=l)
        act = _swiglu_fwd(gu)
        ff = _mm(act, g_dn, "nn", F32, T, 256, D_FF, "mm_down", b_layer=l)
        xhat2, rstd2, y_b = _ln_fwd(xhat1, ln1_g[l][None], ln1_b[l][None], ff, ln2_g[l][None], ln2_b[l][None])
        saved.append((prev_b, proj, mixcat, xhat1, rstd1, h_b, gu, xhat2, rstd2))
        prev, pg, pb, prev_b = xhat2, ln2_g[l][None], ln2_b[l][None], y_b

    loss_tile, dy = _loss_head(prev, pg, pb, target)

    p_in = lax.empty((L, IN_W, D_MODEL), BF16)
    p_gu = lax.empty((L, 2 * D_FF, D_MODEL), BF16)
    p_o = lax.empty((L, D_MODEL, D_MODEL), BF16)
    p_dn = lax.empty((L, D_FF, D_MODEL), BF16)
    small = [None] * L
    dres, dmm = None, dy
    for l in reversed(range(L)):
        prev_b, proj, mixcat, xhat1, rstd1, h_b, gu, xhat2, rstd2 = saved[l]
        dr2, dr2_b, dg2, db2 = _ln_bwd(dres, dmm, xhat2, rstd2, ln2_g[l][None])
        dact = _mm(dr2_b, g_dn, "nt", F32, T, 256, D_MODEL, "mm_dact", b_layer=l)
        dgu, act = _swiglu_bwd(gu, dact)
        p_dn = _mm(act, dr2_b, "tn", BF16, 256, D_MODEL, T, "mm_dw_down", out_into=p_dn, out_layer=l)
        p_gu = _mm(dgu, h_b, "tn", BF16, 512, D_MODEL, T, "mm_dw_gate_up", out_into=p_gu, out_layer=l)
        dh = _mm(dgu, g_gu, "nn", F32, T, 512, SW_TC, "mm_dh", b_layer=l)
        dr1, dr1_b, dg1, db1 = _ln_bwd(dr2, dh, xhat1, rstd1, ln1_g[l][None])
        dmix = _mm(dr1_b, g_o, "nt", F32, T, 512, D_MODEL, "mm_dmix", b_layer=l)
        p_o = _mm(mixcat, dr1_b, "tn", BF16, 512, D_MODEL, T, "mm_dw_o", out_into=p_o, out_layer=l)
        dproj, dwc, dwp, dps, dlng, dwsp, dbias = _mixer_bwd(
            proj, dmix, w_conv_full[l], wpool_bd[l], pool_scale[l][None], sgu_ln_g[l][None], wsp_t[l], bias_t[l])
        p_in = _mm(dproj, prev_b, "tn", BF16, IN_W, D_MODEL, T, "mm_dw_in", out_into=p_in, out_layer=l)
        dx = _mm(dproj, g_in, "nn", F32, T, 512, IN_W, "mm_dx", b_layer=l)
        small[l] = (dwc, dwp, dps, dlng, dwsp, dbias, dg1, db1, dg2, db2)
        dres, dmm = dr1, dx
    grad_x = _residual_out(dres, dmm)

    def stack(i):
        return jnp.stack([small[l][i] for l in range(L)])

    dwp_bd = stack(1).reshape(L, 2, 2, HALF, 2, HALF)
    dwp_all = jnp.einsum("ltgchd,gh->ltgcd", dwp_bd, eye2).reshape(L, 4, HALF, HALF)
    dbs_all = jnp.swapaxes(stack(5)[:, :, :, :2], 2, 3).reshape(L, 6, CHUNK)
    small_grads = [stack(0), dwp_all, stack(2).reshape(L, POOL_W), stack(3).reshape(L, SGU_W),
                   stack(4).reshape(L, 6, CHUNK, CHUNK), dbs_all] + [stack(i).reshape(L, D_MODEL) for i in (6, 7, 8, 9)]
    return loss_tile, grad_x, (p_in, p_gu, p_o, p_dn), small_grads


def _reduce_and_update(parts_big, small_grads, big_w, big_m, big_v, small_w, small_m, small_v):
    L = DEPTH
    mx, my, mc = _my_place()
    dev = 4 * mx + 2 * my + mc
    conv_cols = CONV_W // N_DEV
    w_in, w_gate_up, w_o, w_down = big_w
    m_w_in, m_w_gate_up, m_w_o, m_w_down = big_m
    v_w_in, v_w_gate_up, v_w_o, v_w_down = big_v

    parts = [p.reshape(L, 4, 2, r, D_MODEL) for p, r in zip(parts_big, SHARD_ROWS)]
    got1 = _rs_sibling_exchange(parts)
    c_arr = jnp.reshape(mc, (1,)).astype(jnp.int32)
    q_arr = jnp.reshape(2 * mx + my, (1,)).astype(jnp.int32)
    sums = [_rs_chip_sum(p, g, c_arr) for p, g in zip(parts, got1)]
    got2 = _rs_chip_exchange(sums)
    gt_in, gt_gu, g_w_o, g_w_dn = [_rs_finish(s, g, q_arr) for s, g in zip(sums, got2)]
    g_w_in = jnp.swapaxes(gt_in, 1, 2)
    g_w_gu = jnp.swapaxes(gt_gu, 1, 2)

    small_shapes = [a.shape for a in small_grads]
    packed_g = _allreduce_small(_pack(small_grads))

    def widen_conv(a):
        return lax.dynamic_update_slice(jnp.zeros((L, 3, CONV_W), F32), a, (0, 0, dev * conv_cols))

    small_m = [widen_conv(small_m[0])] + list(small_m[1:])
    small_v = [widen_conv(small_v[0])] + list(small_v[1:])
    pk_d, pk_m, pk_v = _adamw(_pack(small_w), packed_g, _pack(small_m), _pack(small_v), packed_g.shape[0] // 2)
    sg = _unpack(packed_g, small_shapes)
    sd = _unpack(pk_d, small_shapes)
    sm = _unpack(pk_m, small_shapes)
    sv = _unpack(pk_v, small_shapes)

    def conv_cols_of(a):
        return lax.dynamic_slice(a, (0, 0, dev * conv_cols), (L, 3, conv_cols))

    for lst in (sg, sd, sm, sv):
        lst[0] = conv_cols_of(lst[0])

    def big(w, g, m, v, tr):
        s = w.shape
        d, mn, vn = _adamw(w.reshape(-1, s[-1]), g.reshape(-1, s[-1]), m.reshape(-1, s[-1]), v.reshape(-1, s[-1]), tr)
        return d.reshape(s), mn.reshape(s), vn.reshape(s)

    d_in, m_in, v_in = big(w_in, g_w_in, m_w_in, v_w_in, 512)
    d_gu, m_gu, v_gu = big(w_gate_up, g_w_gu, m_w_gate_up, v_w_gate_up, 512)
    d_o, m_o, v_o = big(w_o, g_w_o, m_w_o, v_w_o, 128)
    d_dn, m_dn, v_dn = big(w_down, g_w_dn, m_w_down, v_w_down, 352)

    def ordered(big_in, big_o, big_gu, big_dn, sm_list):
        return [big_in, sm_list[0], sm_list[1], sm_list[2], sm_list[3], sm_list[4], sm_list[5], big_o,
                sm_list[6], sm_list[7], big_gu, big_dn, sm_list[8], sm_list[9]]

    grads = ordered(g_w_in, g_w_o, g_w_gu, g_w_dn, sg)
    deltas = ordered(d_in, d_o, d_gu, d_dn, sd)
    new_m = ordered(m_in, m_o, m_gu, m_dn, sm)
    new_v = ordered(v_in, v_o, v_gu, v_dn, sv)
    return grads, deltas, new_m, new_v
```

```python
import functools
import math

import jax
import jax.numpy as jnp
from jax import lax
from jax.experimental import pallas as pl
from jax.experimental.pallas import tpu as pltpu

F32 = jnp.float32
BF16 = jnp.bfloat16
MESH = pl.DeviceIdType.MESH

D_MODEL = 1024
DEPTH = 4
CONV_W = 384
POOL_W = 256
SGU_W = 384
IN_W = 3 * CONV_W + POOL_W + 2 * SGU_W
D_FF = 2816
CHUNK = 128
ALPHA = float((2 * DEPTH) ** 0.25)
LN_EPS = 1e-5
ADAM_LR, ADAM_B1, ADAM_B2, ADAM_EPS, ADAM_WD, ADAM_STEP = 0.001, 0.9, 0.999, 1e-08, 0.01, 10

N_DEV = 8
LANES = 128
HALF = 64
SHARD_ROWS = (IN_W // N_DEV, 2 * D_FF // N_DEV, D_MODEL // N_DEV, D_FF // N_DEV)
VMEM_LIMIT = 52 * 1024 * 1024

INV_SQRT2 = 0.7071067811865476
INV_SQRT_2PI = 0.3989422804014327


def _cparams(sem=None, **kw):
    if sem is not None:
        kw["dimension_semantics"] = sem
    return pltpu.CompilerParams(vmem_limit_bytes=VMEM_LIMIT, **kw)


_DN = {"nn": (((1,), (0,)), ((), ())), "nt": (((1,), (1,)), ((), ())), "tn": (((0,), (0,)), ((), ()))}


def _mm(a, b, mode, out_dtype, tm, tn, tk, name, deps=()):
    if mode == "nn":
        (M, K), N = a.shape, b.shape[1]
    elif mode == "nt":
        (M, K), N = a.shape, b.shape[0]
    else:
        (K, M), N = a.shape, b.shape[1]
    assert M % tm == 0 and N % tn == 0 and K % tk == 0, (M, N, K, tm, tn, tk)
    nk = K // tk
    nd = len(deps)

    def body(*refs):
        a_ref, b_ref, o_ref = refs[0], refs[1], refs[2 + nd]
        acc_ref = refs[3 + nd] if nk > 1 else None
        p = lax.dot_general(a_ref[...], b_ref[...], _DN[mode], preferred_element_type=F32)
        if nk == 1:
            o_ref[...] = p.astype(o_ref.dtype)
        else:
            k = pl.program_id(2)

            @pl.when(k == 0)
            def _():
                acc_ref[...] = p

            @pl.when(k > 0)
            def _():
                acc_ref[...] += p

            @pl.when(k == nk - 1)
            def _():
                o_ref[...] = acc_ref[...].astype(o_ref.dtype)

    if mode == "nn":
        a_spec = pl.BlockSpec((tm, tk), lambda i, j, k: (i, k))
        b_blk, b_idx = (tk, tn), (lambda i, j, k: (k, j))
    elif mode == "nt":
        a_spec = pl.BlockSpec((tm, tk), lambda i, j, k: (i, k))
        b_blk, b_idx = (tn, tk), (lambda i, j, k: (j, k))
    else:
        a_spec = pl.BlockSpec((tk, tm), lambda i, j, k: (k, i))
        b_blk, b_idx = (tk, tn), (lambda i, j, k: (k, j))
    return pl.pallas_call(
        body,
        name=name,
        grid=(M // tm, N // tn, nk),
        in_specs=[a_spec, pl.BlockSpec(b_blk, b_idx)] + [pl.BlockSpec(memory_space=pl.ANY)] * nd,
        out_specs=pl.BlockSpec((tm, tn), lambda i, j, k: (i, j)),
        out_shape=jax.ShapeDtypeStruct((M, N), out_dtype),
        scratch_shapes=[pltpu.VMEM((tm, tn), F32)] if nk > 1 else [],
        compiler_params=_cparams(("parallel", "parallel", "arbitrary")),
    )(a, b, *deps)


def _gelu(x):
    return 0.5 * x * (1.0 + lax.erf(x * INV_SQRT2))


def _gelu_grad(x):
    return 0.5 * (1.0 + lax.erf(x * INV_SQRT2)) + x * (jnp.exp(-0.5 * x * x) * INV_SQRT_2PI)


def _shift_down(z, k):
    row = lax.broadcasted_iota(jnp.int32, z.shape, 0)
    return jnp.where(row >= k, pltpu.roll(z, k, 0), 0.0)


def _shift_up(z, k):
    n = z.shape[0]
    row = lax.broadcasted_iota(jnp.int32, z.shape, 0)
    return jnp.where(row < n - k, pltpu.roll(z, n - k, 0), 0.0)


def _lo_mask(shape):
    return lax.broadcasted_iota(jnp.int32, shape, len(shape) - 1) < HALF


def _seg_mean(x, lo):
    a = jnp.sum(jnp.where(lo, x, 0.0), axis=-1, keepdims=True)
    b = jnp.sum(jnp.where(lo, 0.0, x), axis=-1, keepdims=True)
    return jnp.where(lo, a, b) * (1.0 / HALF)


def _pool_windows(first):
    lo = _lo_mask((1, LANES))
    return jnp.where(first, jnp.where(lo, 2.0, 4.0), jnp.where(lo, 8.0, 16.0)), lo


def _pool_mean_minus_token(p, first):
    wl, lo = _pool_windows(first)
    s2 = p + _shift_down(p, 1)
    s4 = s2 + _shift_down(s2, 2)
    s8 = s4 + _shift_down(s4, 4)
    s16 = s8 + _shift_down(s8, 8)
    win = jnp.where(first, jnp.where(lo, s2, s4), jnp.where(lo, s8, s16))
    t1 = (lax.broadcasted_iota(jnp.int32, p.shape, 0) + 1).astype(F32)
    count = jnp.minimum(t1, wl)
    return win / count - p, count


def _tril_keep():
    r = lax.broadcasted_iota(jnp.int32, (2 * CHUNK, CHUNK), 0)
    s = lax.broadcasted_iota(jnp.int32, (2 * CHUNK, CHUNK), 1)
    return s <= (r & (CHUNK - 1))


def _sgu_chunk_fwd(u, v, g, wm, bias, lo):
    ug = _gelu(u)
    vg = _gelu(v)
    mu = _seg_mean(vg, lo)
    xc = vg - mu
    var = _seg_mean(xc * xc, lo)
    rstd = lax.rsqrt(var + LN_EPS)
    vn = xc * rstd
    vh = (vn * g).astype(BF16)
    mm2 = jnp.dot(wm, vh, preferred_element_type=F32)
    mixed = jnp.where(lo, mm2[:CHUNK], mm2[CHUNK:]) + bias
    return ug, vn, rstd, vh, mixed


def _mixer_fwd(proj, wconv, wpool_bd, pscale, lng, wsp, bias):
    T = proj.shape[0]
    nchunk = T // CHUNK

    def body(a_ref, b_ref, c_ref, wc_ref, wp_ref, ps_ref, lng_ref, wsp_ref, bias_ref, o_ref):
        j = pl.program_id(0)

        @pl.when(j < 3)
        def _conv():
            z = c_ref[...] * a_ref[...]
            w = wc_ref[...]
            y = w[0:1] * _shift_down(z, 2) + w[1:2] * _shift_down(z, 1) + w[2:3] * z
            o_ref[...] = (b_ref[...] * y).astype(o_ref.dtype)

        @pl.when((j >= 3) & (j < 5))
        def _pool():
            d, _ = _pool_mean_minus_token(a_ref[...], j == 3)
            y = jnp.dot(d.astype(BF16), wp_ref[...].astype(BF16), preferred_element_type=F32)
            o_ref[...] = (y * ps_ref[...]).astype(o_ref.dtype)

        @pl.when(j >= 5)
        def _sgu():
            lo = _lo_mask((CHUNK, LANES))
            wm = jnp.where(_tril_keep(), wsp_ref[...], 0.0).astype(BF16)
            bias_t = bias_ref[...]
            g = lng_ref[...]

            def chunk(n, carry):
                rows = pl.ds(pl.multiple_of(n * CHUNK, CHUNK), CHUNK)
                ug, _, _, _, mixed = _sgu_chunk_fwd(a_ref[rows, :], b_ref[rows, :], g, wm, bias_t, lo)
                o_ref[rows, :] = (ug * mixed).astype(o_ref.dtype)
                return carry

            lax.fori_loop(0, nchunk, chunk, 0)

    def col(f):
        return lambda j: (0, f(j))

    clip = lambda v, lo, hi: jnp.minimum(jnp.maximum(v, lo), hi)
    return pl.pallas_call(
        body,
        name="mixer_fwd",
        grid=(8,),
        in_specs=[
            pl.BlockSpec((T, LANES), col(lambda j: jnp.where(j < 3, j, jnp.where(j < 5, j + 6, j + 6)))),
            pl.BlockSpec((T, LANES), col(lambda j: jnp.where(j < 3, j + 3, jnp.where(j < 5, 5, j + 9)))),
            pl.BlockSpec((T, LANES), col(lambda j: jnp.where(j < 3, j + 6, 8))),
            pl.BlockSpec((3, LANES), col(lambda j: clip(j, 0, 2))),
            pl.BlockSpec((None, LANES, LANES), lambda j: (clip(j - 3, 0, 1), 0, 0)),
            pl.BlockSpec((1, LANES), col(lambda j: clip(j - 3, 0, 1))),
            pl.BlockSpec((1, LANES), col(lambda j: clip(j - 5, 0, 2))),
            pl.BlockSpec((None, 2 * CHUNK, CHUNK), lambda j: (clip(j - 5, 0, 2), 0, 0)),
            pl.BlockSpec((None, CHUNK, LANES), lambda j: (clip(j - 5, 0, 2), 0, 0)),
        ],
        out_specs=pl.BlockSpec((T, LANES), lambda j: (0, j)),
        out_shape=jax.ShapeDtypeStruct((T, D_MODEL), BF16),
        compiler_params=_cparams(("arbitrary",)),
    )(proj, proj, proj, wconv, wpool_bd, pscale, lng, wsp, bias)


def _mixer_bwd(proj, dmix, wconv, wpool_bd, pscale, lng, wsp, bias):
    T = proj.shape[0]
    nchunk = T // CHUNK

    def body(a_ref, b_ref, c_ref, dm_ref, wc_ref, wp_ref, ps_ref, lng_ref, wsp_ref, bias_ref,
             o_ref, dwc_ref, dwp_ref, dps_ref, dlng_ref, dwsp_ref, dbias_ref, keep1, keep2):
        k = pl.program_id(0)

        @pl.when(k < 3)
        def _conv():
            xa, gb, gc, dya = a_ref[...], b_ref[...], c_ref[...], dm_ref[...]
            w = wc_ref[...]
            z = gc * xa
            z1 = _shift_down(z, 1)
            z2 = _shift_down(z, 2)
            y = w[0:1] * z2 + w[1:2] * z1 + w[2:3] * z
            dyv = dya * gb
            dz = w[2:3] * dyv + w[1:2] * _shift_up(dyv, 1) + w[0:1] * _shift_up(dyv, 2)
            dwc_ref[0:1, :] = jnp.sum(dyv * z2, axis=0, keepdims=True)
            dwc_ref[1:2, :] = jnp.sum(dyv * z1, axis=0, keepdims=True)
            dwc_ref[2:3, :] = jnp.sum(dyv * z, axis=0, keepdims=True)
            o_ref[...] = (dz * gc).astype(o_ref.dtype)
            keep1[k] = (dya * y).astype(keep1.dtype)
            keep1[k + 3] = (dz * xa).astype(keep1.dtype)

        @pl.when((k >= 3) & (k < 9))
        def _emit_gb_gc():
            o_ref[...] = keep1[k - 3]

        @pl.when((k >= 9) & (k < 11))
        def _pool():
            first = k == 9
            p, dyb = a_ref[...], dm_ref[...]
            d, count = _pool_mean_minus_token(p, first)
            w2 = wp_ref[...].astype(BF16)
            db = d.astype(BF16)
            y = jnp.dot(db, w2, preferred_element_type=F32)
            dps_ref[...] = jnp.sum(dyb * y, axis=0, keepdims=True)
            dyv = (dyb * ps_ref[...]).astype(BF16)
            dd = lax.dot_general(dyv, w2, _DN["nt"], preferred_element_type=F32)
            dwp_ref[...] = lax.dot_general(db, dyv, _DN["tn"], preferred_element_type=F32)
            dwin = dd / count
            a2 = dwin + _shift_up(dwin, 1)
            a4 = a2 + _shift_up(a2, 2)
            a8 = a4 + _shift_up(a4, 4)
            a16 = a8 + _shift_up(a8, 8)
            _, lo = _pool_windows(first)
            back = jnp.where(first, jnp.where(lo, a2, a4), jnp.where(lo, a8, a16))
            o_ref[...] = (back - dd).astype(o_ref.dtype)

        @pl.when((k >= 11) & (k < 14))
        def _sgu():
            lo = _lo_mask((CHUNK, LANES))
            keep = _tril_keep()
            wm = jnp.where(keep, wsp_ref[...], 0.0).astype(BF16)
            bias_t = bias_ref[...]
            g = lng_ref[...]
            dwsp_ref[...] = jnp.zeros_like(dwsp_ref)
            dbias_ref[...] = jnp.zeros_like(dbias_ref)
            dlng_ref[...] = jnp.zeros_like(dlng_ref)

            def chunk(n, carry):
                rows = pl.ds(pl.multiple_of(n * CHUNK, CHUNK), CHUNK)
                u, v, dyc = a_ref[rows, :], b_ref[rows, :], dm_ref[rows, :]
                ug, vn, rstd, vh, mixed = _sgu_chunk_fwd(u, v, g, wm, bias_t, lo)
                dmx = dyc * ug
                o_ref[rows, :] = (dyc * mixed * _gelu_grad(u)).astype(o_ref.dtype)
                dbias_ref[...] += dmx
                dst = jnp.concatenate([jnp.where(lo, dmx, 0.0), jnp.where(lo, 0.0, dmx)], axis=0).astype(BF16)
                dwsp_ref[...] += lax.dot_general(dst, vh, _DN["nt"], preferred_element_type=F32)
                dvh = lax.dot_general(wm, dst, _DN["tn"], preferred_element_type=F32)
                dlng_ref[...] += jnp.sum(dvh * vn, axis=0, keepdims=True)
                dvn = dvh * g
                m1 = _seg_mean(dvn, lo)
                m2 = _seg_mean(dvn * vn, lo)
                dvg = rstd * (dvn - m1 - vn * m2)
                keep2[k - 11, rows, :] = (dvg * _gelu_grad(v)).astype(keep2.dtype)
                return carry

            lax.fori_loop(0, nchunk, chunk, 0)
            dwsp_ref[...] = jnp.where(keep, dwsp_ref[...], 0.0)
            dbt = dbias_ref[...]
            lane = lax.broadcasted_iota(jnp.int32, (CHUNK, LANES), 1)
            sa = jnp.sum(jnp.where(lo, dbt, 0.0), axis=-1, keepdims=True)
            sb = jnp.sum(jnp.where(lo, 0.0, dbt), axis=-1, keepdims=True)
            dbias_ref[...] = jnp.where(lane == 0, sa, jnp.where(lane == 1, sb, 0.0))

        @pl.when(k >= 14)
        def _emit_v():
            o_ref[...] = keep2[k - 14]

    def col(f):
        return lambda k: (0, f(k))

    clip = lambda v, lo, hi: jnp.minimum(jnp.maximum(v, lo), hi)
    view_a = lambda k: jnp.where(k < 3, k, jnp.where(k < 9, 2, jnp.where(k < 14, k, 13)))
    view_b = lambda k: jnp.where(k < 3, k + 3, jnp.where(k < 11, 5, jnp.where(k < 14, k + 3, 16)))
    view_c = lambda k: jnp.where(k < 3, k + 6, 8)
    view_dm = lambda k: jnp.where(k < 3, k, jnp.where(k < 9, 2, jnp.where(k < 14, k - 6, 7)))
    return pl.pallas_call(
        body,
        name="mixer_bwd",
        grid=(17,),
        in_specs=[
            pl.BlockSpec((T, LANES), col(view_a)),
            pl.BlockSpec((T, LANES), col(view_b)),
            pl.BlockSpec((T, LANES), col(view_c)),
            pl.BlockSpec((T, LANES), col(view_dm)),
            pl.BlockSpec((3, LANES), col(lambda k: clip(k, 0, 2))),
            pl.BlockSpec((None, LANES, LANES), lambda k: (clip(k - 9, 0, 1), 0, 0)),
            pl.BlockSpec((1, LANES), col(lambda k: clip(k - 9, 0, 1))),
            pl.BlockSpec((1, LANES), col(lambda k: clip(k - 11, 0, 2))),
            pl.BlockSpec((None, 2 * CHUNK, CHUNK), lambda k: (clip(k - 11, 0, 2), 0, 0)),
            pl.BlockSpec((None, CHUNK, LANES), lambda k: (clip(k - 11, 0, 2), 0, 0)),
        ],
        out_specs=[
            pl.BlockSpec((T, LANES), lambda k: (0, k)),
            pl.BlockSpec((3, LANES), col(lambda k: clip(k, 0, 2))),
            pl.BlockSpec((None, LANES, LANES), lambda k: (clip(k - 9, 0, 1), 0, 0)),
            pl.BlockSpec((1, LANES), col(lambda k: clip(k - 9, 0, 1))),
            pl.BlockSpec((1, LANES), col(lambda k: clip(k - 11, 0, 2))),
            pl.BlockSpec((None, 2 * CHUNK, CHUNK), lambda k: (clip(k - 11, 0, 2), 0, 0)),
            pl.BlockSpec((None, CHUNK, LANES), lambda k: (clip(k - 11, 0, 2), 0, 0)),
        ],
        out_shape=[
            jax.ShapeDtypeStruct((T, IN_W), BF16),
            jax.ShapeDtypeStruct((3, CONV_W), F32),
            jax.ShapeDtypeStruct((2, LANES, LANES), F32),
            jax.ShapeDtypeStruct((1, POOL_W), F32),
            jax.ShapeDtypeStruct((1, SGU_W), F32),
            jax.ShapeDtypeStruct((3, 2 * CHUNK, CHUNK), F32),
            jax.ShapeDtypeStruct((3, CHUNK, LANES), F32),
        ],
        scratch_shapes=[pltpu.VMEM((6, T, LANES), BF16), pltpu.VMEM((3, T, LANES), BF16)],
        compiler_params=_cparams(("arbitrary",)),
    )(proj, proj, proj, dmix, wconv, wpool_bd, pscale, lng, wsp, bias)


def _ln_fwd(prev, pg, pb, mmout, g, b, tm=256):
    T = prev.shape[0]

    def body(prev_ref, pg_ref, pb_ref, mm_ref, g_ref, b_ref, xhat_ref, rstd_ref, y_ref):
        r = ALPHA * (prev_ref[...] * pg_ref[...] + pb_ref[...]) + mm_ref[...]
        mu = jnp.mean(r, axis=-1, keepdims=True)
        xc = r - mu
        var = jnp.mean(xc * xc, axis=-1, keepdims=True)
        rstd = lax.rsqrt(var + LN_EPS)
        xhat = xc * rstd
        xhat_ref[...] = xhat
        rstd_ref[...] = rstd
        y_ref[...] = (xhat * g_ref[...] + b_ref[...]).astype(y_ref.dtype)

    row = pl.BlockSpec((tm, D_MODEL), lambda i: (i, 0))
    vec = pl.BlockSpec((1, D_MODEL), lambda i: (0, 0))
    return pl.pallas_call(
        body,
        name="ln_fwd",
        grid=(T // tm,),
        in_specs=[row, vec, vec, row, vec, vec],
        out_specs=[row, pl.BlockSpec((tm, 1), lambda i: (i, 0)), row],
        out_shape=[jax.ShapeDtypeStruct((T, D_MODEL), F32), jax.ShapeDtypeStruct((T, 1), F32),
                   jax.ShapeDtypeStruct((T, D_MODEL), BF16)],
        compiler_params=_cparams(("parallel",)),
    )(prev, pg, pb, mmout, g, b)


def _ln_bwd(dres, dmm, xhat, rstd, g, tm=256, deps=()):
    T = xhat.shape[0]
    has_res = dres is not None
    nd = len(deps)

    def body(*refs):
        refs = refs[:len(refs) - 4 - nd] + refs[len(refs) - 4:]
        if has_res:
            dres_ref, dmm_ref, xhat_ref, rstd_ref, g_ref, dr_ref, drb_ref, dg_ref, db_ref = refs
            dy = ALPHA * dres_ref[...] + dmm_ref[...]
        else:
            dmm_ref, xhat_ref, rstd_ref, g_ref, dr_ref, drb_ref, dg_ref, db_ref = refs
            dy = dmm_ref[...]
        xhat_v = xhat_ref[...]

        @pl.when(pl.program_id(0) == 0)
        def _():
            dg_ref[...] = jnp.zeros_like(dg_ref)
            db_ref[...] = jnp.zeros_like(db_ref)

        dg_ref[...] += jnp.sum(dy * xhat_v, axis=0, keepdims=True)
        db_ref[...] += jnp.sum(dy, axis=0, keepdims=True)
        dxh = dy * g_ref[...]
        m1 = jnp.mean(dxh, axis=-1, keepdims=True)
        m2 = jnp.mean(dxh * xhat_v, axis=-1, keepdims=True)
        dr = rstd_ref[...] * (dxh - m1 - xhat_v * m2)
        dr_ref[...] = dr
        drb_ref[...] = dr.astype(drb_ref.dtype)

    row = pl.BlockSpec((tm, D_MODEL), lambda i: (i, 0))
    vec = pl.BlockSpec((1, D_MODEL), lambda i: (0, 0))
    in_specs = ([row] if has_res else []) + [row, row, pl.BlockSpec((tm, 1), lambda i: (i, 0)), vec]
    in_specs += [pl.BlockSpec(memory_space=pl.ANY)] * nd
    args = ([dres] if has_res else []) + [dmm, xhat, rstd, g] + list(deps)
    return pl.pallas_call(
        body,
        name="ln_bwd_res" if has_res else "ln_bwd",
        grid=(T // tm,),
        in_specs=in_specs,
        out_specs=[row, row, vec, vec],
        out_shape=[jax.ShapeDtypeStruct((T, D_MODEL), F32), jax.ShapeDtypeStruct((T, D_MODEL), BF16),
                   jax.ShapeDtypeStruct((1, D_MODEL), F32), jax.ShapeDtypeStruct((1, D_MODEL), F32)],
        compiler_params=_cparams(("arbitrary",)),
    )(*args)


def _loss_head(xhat, g, b, target, tm=256):
    T = xhat.shape[0]

    def body(xhat_ref, g_ref, b_ref, t_ref, loss_ref, dy_ref):
        err = xhat_ref[...] * g_ref[...] + b_ref[...] - t_ref[...]

        @pl.when(pl.program_id(0) == 0)
        def _():
            loss_ref[...] = jnp.zeros_like(loss_ref)

        part = jnp.sum(jnp.sum(err * err, axis=-1, keepdims=True), axis=0, keepdims=True)
        loss_ref[...] += jnp.broadcast_to(part * (0.5 / D_MODEL), loss_ref.shape)
        dy_ref[...] = err * (1.0 / D_MODEL)

    row = pl.BlockSpec((tm, D_MODEL), lambda i: (i, 0))
    vec = pl.BlockSpec((1, D_MODEL), lambda i: (0, 0))
    return pl.pallas_call(
        body,
        name="loss_head",
        grid=(T // tm,),
        in_specs=[row, vec, vec, row],
        out_specs=[pl.BlockSpec((8, LANES), lambda i: (0, 0)), row],
        out_shape=[jax.ShapeDtypeStruct((8, LANES), F32), jax.ShapeDtypeStruct((T, D_MODEL), F32)],
        compiler_params=_cparams(("arbitrary",)),
    )(xhat, g, b, target)


def _residual_out(dres, dmm, tm=256):
    T = dres.shape[0]

    def body(a_ref, b_ref, o_ref):
        o_ref[...] = ALPHA * a_ref[...] + b_ref[...]

    row = pl.BlockSpec((tm, D_MODEL), lambda i: (i, 0))
    return pl.pallas_call(
        body, name="residual_out", grid=(T // tm,), in_specs=[row, row], out_specs=row,
        out_shape=jax.ShapeDtypeStruct((T, D_MODEL), F32), compiler_params=_cparams(("parallel",)),
    )(dres, dmm)


SW_TC = 1408


def _swiglu_fwd(gu, tm=128):
    T = gu.shape[0]

    def body(gu_ref, o_ref):
        gv = gu_ref[:, :D_FF]
        o_ref[...] = (gv * jax.nn.sigmoid(gv) * gu_ref[:, D_FF:]).astype(o_ref.dtype)

    return pl.pallas_call(
        body, name="swiglu_fwd", grid=(T // tm,),
        in_specs=[pl.BlockSpec((tm, 2 * D_FF), lambda i: (i, 0))],
        out_specs=pl.BlockSpec((tm, D_FF), lambda i: (i, 0)),
        out_shape=jax.ShapeDtypeStruct((T, D_FF), BF16), compiler_params=_cparams(("parallel",)),
    )(gu)


def _swiglu_bwd(gu, dact, tm=128):
    T = gu.shape[0]

    def body(gu_ref, da_ref, dgu_ref, act_ref):
        gv, uv, da = gu_ref[:, :D_FF], gu_ref[:, D_FF:], da_ref[...]
        s = jax.nn.sigmoid(gv)
        sg = gv * s
        act_ref[...] = (sg * uv).astype(act_ref.dtype)
        dgu_ref[:, D_FF:] = (da * sg).astype(dgu_ref.dtype)
        dgu_ref[:, :D_FF] = (da * uv * (s * (1.0 + gv * (1.0 - s)))).astype(dgu_ref.dtype)

    wide = pl.BlockSpec((tm, 2 * D_FF), lambda i: (i, 0))
    half = pl.BlockSpec((tm, D_FF), lambda i: (i, 0))
    return pl.pallas_call(
        body, name="swiglu_bwd", grid=(T // tm,),
        in_specs=[wide, half], out_specs=[wide, half],
        out_shape=[jax.ShapeDtypeStruct((T, 2 * D_FF), BF16), jax.ShapeDtypeStruct((T, D_FF), BF16)],
        compiler_params=_cparams(("parallel",)),
    )(gu, dact)


def _adamw(w, g, m, v, tr):
    R, C = w.shape
    assert R % tr == 0
    c1 = 1.0 - ADAM_B1 ** ADAM_STEP
    c2 = 1.0 - ADAM_B2 ** ADAM_STEP

    def body(w_ref, g_ref, m_ref, v_ref, d_ref, mo_ref, vo_ref):
        gv = g_ref[...]
        mn = ADAM_B1 * m_ref[...] + (1.0 - ADAM_B1) * gv
        vn = ADAM_B2 * v_ref[...] + (1.0 - ADAM_B2) * (gv * gv)
        d_ref[...] = -ADAM_LR * ((mn / c1) / (jnp.sqrt(vn / c2) + ADAM_EPS) + ADAM_WD * w_ref[...])
        mo_ref[...] = mn
        vo_ref[...] = vn

    blk = pl.BlockSpec((tr, C), lambda i: (i, 0))
    return pl.pallas_call(
        body, name="adamw", grid=(R // tr,), in_specs=[blk] * 4, out_specs=[blk] * 3,
        out_shape=[jax.ShapeDtypeStruct((R, C), F32)] * 3, compiler_params=_cparams(("parallel",)),
    )(w, g, m, v)


def _my_place():
    return lax.axis_index("x"), lax.axis_index("y"), lax.axis_index("c")


ANY = pl.BlockSpec(memory_space=pl.ANY)
HBM = pl.BlockSpec(memory_space=pltpu.HBM)
SEM = pl.BlockSpec(memory_space=pltpu.SEMAPHORE)
EFFECT = pltpu.SideEffectType.DATAFLOW_SIDE_EFFECTING


def _in_hbm(a):
    return pltpu.with_memory_space_constraint(a, pltpu.HBM)


def _block_rows(ref, w, dev):
    r = SHARD_ROWS[w]
    start = pl.multiple_of((4 * dev[0] + 2 * dev[1] + dev[2]) * r, 16)
    return ref.at[pl.ds(start, r), :]


def _ag_first_copies(s_refs, land_refs, send_sems, recv_sems, receiving):
    x, y, c = _my_place()
    peers = [(x, y, 1 - c)] + [(*chip, c) for chip in _other_chips(x, y)]
    copies = []
    for k, peer in enumerate(peers):
        block = peer if receiving else (x, y, c)
        copies += [pltpu.make_async_remote_copy(
            src_ref=s_refs[w], dst_ref=_block_rows(land_refs[w], w, block),
            send_sem=send_sems.at[k * len(s_refs) + w], recv_sem=recv_sems.at[k * len(s_refs) + w],
            device_id=peer, device_id_type=MESH)
            for w in range(len(s_refs))]
    return copies


def _ag_start(shards, layer):
    nw = len(shards)

    def body(*refs):
        s_refs, land_refs = refs[:nw], refs[nw:2 * nw]
        token = refs[-1]
        for cp in _ag_first_copies(s_refs, land_refs, refs[2 * nw], refs[2 * nw + 1], False):
            cp.start()
        token[...] = jnp.zeros_like(token)

    lands = [lax.empty((N_DEV * s.shape[0], D_MODEL), BF16) for s in shards]
    out = pl.pallas_call(
        body, name="ag_start_%d" % layer,
        in_specs=[HBM] * (2 * nw),
        out_specs=(SEM, SEM, *[HBM] * (2 * nw), pl.BlockSpec(memory_space=pltpu.VMEM)),
        out_shape=(pltpu.SemaphoreType.DMA((4 * nw,)), pltpu.SemaphoreType.DMA((4 * nw,)),
                   *[pltpu.HBM(a.shape, a.dtype) for a in list(shards) + lands],
                   jax.ShapeDtypeStruct((8, LANES), F32)),
        input_output_aliases={i: 2 + i for i in range(2 * nw)},
        compiler_params=pltpu.CompilerParams(has_side_effects=EFFECT),
    )(*[_in_hbm(a) for a in list(shards) + lands])
    return out[0], out[1], out[2:2 + nw], out[2 + nw:2 + 2 * nw], out[-1]


def _ag_wait(send_sems, recv_sems, shards, lands, after, layer):
    nw = len(shards)

    def body(*refs):
        s_refs, land_refs = refs[:nw], refs[nw:2 * nw]
        for cp in _ag_first_copies(s_refs, land_refs, refs[2 * nw], refs[2 * nw + 1], True):
            cp.wait_send()
            cp.wait_recv()

    out = pl.pallas_call(
        body, name="ag_wait_%d" % layer,
        in_specs=[HBM] * (2 * nw) + [SEM, SEM] + [ANY] * len(after),
        out_specs=[HBM] * (2 * nw),
        out_shape=[pltpu.HBM(a.shape, a.dtype) for a in list(shards) + list(lands)],
        input_output_aliases={i: i for i in range(2 * nw)},
        compiler_params=pltpu.CompilerParams(has_side_effects=EFFECT),
    )(*shards, *lands, send_sems, recv_sems, *after)
    return out[:nw], out[nw:]


def _ag_pass_on(shards, lands):
    nw = len(shards)

    def body(*refs):
        s_refs, g_refs = refs[:nw], refs[2 * nw:3 * nw]
        send_sems, recv_sems, local_sems = refs[3 * nw:]
        x, y, c = _my_place()
        mine = [pltpu.make_async_copy(s_refs[w], _block_rows(g_refs[w], w, (x, y, c)), local_sems.at[w])
                for w in range(nw)]
        for cp in mine:
            cp.start()
        sends, arrivals = [], []
        for j, chip in enumerate(_other_chips(x, y)):
            for w in range(nw):
                rows_out = _block_rows(g_refs[w], w, (*chip, c))
                rows_in = _block_rows(g_refs[w], w, (*chip, 1 - c))
                sends.append(pltpu.make_async_remote_copy(
                    src_ref=rows_out, dst_ref=rows_out, send_sem=send_sems.at[j, w], recv_sem=recv_sems.at[j, w],
                    device_id=(x, y, 1 - c), device_id_type=MESH))
                arrivals.append(pltpu.make_async_remote_copy(
                    src_ref=rows_in, dst_ref=rows_in, send_sem=send_sems.at[j, w], recv_sem=recv_sems.at[j, w],
                    device_id=(x, y, 1 - c), device_id_type=MESH))
        for cp in sends:
            cp.start()
        for cp in arrivals:
            cp.wait_recv()
        for cp in sends:
            cp.wait_send()
        for cp in mine:
            cp.wait()

    return pl.pallas_call(
        body, name="ag_pass_on",
        in_specs=[ANY] * (2 * nw), out_specs=[ANY] * nw,
        out_shape=[jax.ShapeDtypeStruct(a.shape, a.dtype) for a in lands],
        input_output_aliases={nw + i: i for i in range(nw)},
        scratch_shapes=[pltpu.SemaphoreType.DMA((3, nw)), pltpu.SemaphoreType.DMA((3, nw)),
                        pltpu.SemaphoreType.DMA((nw,))],
    )(*shards, *lands)


def _rs_sibling_exchange(parts):
    nw = len(parts)

    def body(*refs):
        p_refs, o_refs = refs[:nw], refs[nw:2 * nw]
        send_sems, recv_sems = refs[2 * nw:]
        x, y, c = _my_place()
        copies = [pltpu.make_async_remote_copy(
            src_ref=p_refs[w].at[:, 1 - c], dst_ref=o_refs[w],
            send_sem=send_sems.at[w], recv_sem=recv_sems.at[w], device_id=(x, y, 1 - c), device_id_type=MESH)
            for w in range(nw)]
        for cp in copies:
            cp.start()
        for cp in copies:
            cp.wait()

    return pl.pallas_call(
        body, name="rs_sibling_exchange",
        in_specs=[ANY] * nw, out_specs=[ANY] * nw,
        out_shape=[jax.ShapeDtypeStruct(p.shape[:1] + p.shape[2:], BF16) for p in parts],
        scratch_shapes=[pltpu.SemaphoreType.DMA((nw,)), pltpu.SemaphoreType.DMA((nw,))],
    )(*parts)


def _rs_chip_sum(part, got, c):
    nxy, _, r, _ = part.shape

    def body(c_ref, p_ref, g_ref, o_ref):
        o_ref[...] = (p_ref[...].astype(F32) + g_ref[...].astype(F32)).astype(o_ref.dtype)

    return pl.pallas_call(
        body, name="rs_chip_sum",
        grid_spec=pltpu.PrefetchScalarGridSpec(
            num_scalar_prefetch=1, grid=(nxy,),
            in_specs=[pl.BlockSpec((None, None, r, D_MODEL), lambda q, c_ref: (q, c_ref[0], 0, 0)),
                      pl.BlockSpec((None, r, D_MODEL), lambda q, c_ref: (q, 0, 0))],
            out_specs=pl.BlockSpec((None, r, D_MODEL), lambda q, c_ref: (q, 0, 0))),
        out_shape=jax.ShapeDtypeStruct(got.shape, BF16),
        compiler_params=_cparams(("parallel",)),
    )(c, part, got)


def _other_chips(x, y):
    return [(1 - x, y), (x, 1 - y), (1 - x, 1 - y)]


def _rs_chip_copies(s_refs, land_refs, send_sems, recv_sems):
    x, y, c = _my_place()
    copies = []
    for k, chip in enumerate(_other_chips(x, y)):
        q = 2 * chip[0] + chip[1]
        copies += [pltpu.make_async_remote_copy(
            src_ref=s_refs[w].at[q], dst_ref=land_refs[w].at[k],
            send_sem=send_sems.at[k * len(s_refs) + w], recv_sem=recv_sems.at[k * len(s_refs) + w],
            device_id=(*chip, c), device_id_type=MESH)
            for w in range(len(s_refs))]
    return copies


def _rs_chip_start(sums, layer):
    nw = len(sums)

    def body(*refs):
        s_refs, land_refs = refs[:nw], refs[nw:2 * nw]
        send_sems, recv_sems = refs[2 * nw], refs[2 * nw + 1]
        token = refs[-1]
        for cp in _rs_chip_copies(s_refs, land_refs, send_sems, recv_sems):
            cp.start()
        token[...] = jnp.zeros_like(token)

    lands = [lax.empty((3,) + s.shape[1:], BF16) for s in sums]
    out = pl.pallas_call(
        body, name="rs_chip_start_%d" % layer,
        in_specs=[HBM] * (2 * nw),
        out_specs=(SEM, SEM, *[HBM] * (2 * nw), pl.BlockSpec(memory_space=pltpu.VMEM)),
        out_shape=(pltpu.SemaphoreType.DMA((3 * nw,)), pltpu.SemaphoreType.DMA((3 * nw,)),
                   *[pltpu.HBM(a.shape, a.dtype) for a in list(sums) + lands],
                   jax.ShapeDtypeStruct((8, LANES), F32)),
        input_output_aliases={i: 2 + i for i in range(2 * nw)},
        compiler_params=pltpu.CompilerParams(has_side_effects=EFFECT),
    )(*[_in_hbm(a) for a in list(sums) + lands])
    return out[0], out[1], out[2:2 + nw], out[2 + nw:2 + 2 * nw], out[-1]


def _rs_chip_wait(send_sems, recv_sems, sums, lands, after, layer):
    nw = len(sums)

    def body(*refs):
        s_refs, land_refs = refs[:nw], refs[nw:2 * nw]
        for cp in _rs_chip_copies(s_refs, land_refs, refs[2 * nw], refs[2 * nw + 1]):
            cp.wait_send()
            cp.wait_recv()

    out = pl.pallas_call(
        body, name="rs_chip_wait_%d" % layer,
        in_specs=[HBM] * (2 * nw) + [SEM, SEM] + [ANY] * len(after),
        out_specs=[HBM] * (2 * nw),
        out_shape=[pltpu.HBM(a.shape, a.dtype) for a in list(sums) + list(lands)],
        input_output_aliases={i: i for i in range(2 * nw)},
        compiler_params=pltpu.CompilerParams(has_side_effects=EFFECT),
    )(*sums, *lands, send_sems, recv_sems, *after)
    return out[:nw], out[nw:]


def _rs_finish(sums, got, q):
    _, r, _ = sums.shape

    def body(q_ref, s_ref, g_ref, o_ref):
        o_ref[...] = ((s_ref[...].astype(F32) + g_ref[0].astype(F32)) + g_ref[1].astype(F32)) + g_ref[2].astype(F32)

    return pl.pallas_call(
        body, name="rs_finish",
        grid_spec=pltpu.PrefetchScalarGridSpec(
            num_scalar_prefetch=1, grid=(1,),
            in_specs=[pl.BlockSpec((None, r, D_MODEL), lambda i, q_ref: (q_ref[0], 0, 0)),
                      pl.BlockSpec((3, r, D_MODEL), lambda i, q_ref: (0, 0, 0))],
            out_specs=pl.BlockSpec((r, D_MODEL), lambda i, q_ref: (0, 0))),
        out_shape=jax.ShapeDtypeStruct((r, D_MODEL), F32),
        compiler_params=_cparams(("arbitrary",)),
    )(q, sums, got)


def _allreduce_small(vec):
    R = vec.shape[0]

    def body(v_ref, o_ref, buf, send_sems, recv_sems):
        x, y, c = _my_place()
        me = 4 * x + 2 * y + c
        buf[0] = v_ref[...]
        copies = []
        for k in range(1, N_DEV):
            p = me ^ k
            copies.append(pltpu.make_async_remote_copy(
                src_ref=v_ref, dst_ref=buf.at[k], send_sem=send_sems.at[k - 1], recv_sem=recv_sems.at[k - 1],
                device_id=(p >> 2, (p >> 1) & 1, p & 1), device_id_type=MESH))
        for cp in copies:
            cp.start()
        for cp in copies:
            cp.wait()
        acc = buf[me]
        for d in range(1, N_DEV):
            acc = acc + buf[me ^ d]
        o_ref[...] = acc

    return pl.pallas_call(
        body, name="allreduce_small",
        in_specs=[pl.BlockSpec(memory_space=pltpu.VMEM)], out_specs=pl.BlockSpec(memory_space=pltpu.VMEM),
        out_shape=jax.ShapeDtypeStruct((R, LANES), F32),
        scratch_shapes=[pltpu.VMEM((N_DEV, R, LANES), F32), pltpu.SemaphoreType.DMA((N_DEV - 1,)),
                        pltpu.SemaphoreType.DMA((N_DEV - 1,))],
        compiler_params=_cparams(),
    )(vec)


def _pack(arrs):
    flat = jnp.concatenate([a.reshape(-1) for a in arrs])
    pad = (-flat.shape[0]) % (8 * LANES)
    return jnp.pad(flat, (0, pad)).reshape(-1, LANES)


def _unpack(packed, shapes):
    flat = packed.reshape(-1)
    out, off = [], 0
    for s in shapes:
        n = math.prod(s)
        out.append(flat[off:off + n].reshape(s))
        off += n
    return out


def kernel(x, w_in, w_conv, w_pool, pool_scale, sgu_ln_g, w_spatial, b_spatial, w_o, ln1_g, ln1_b, w_gate_up, w_down, ln2_g, ln2_b, loss_target, m_w_in, m_w_conv, m_w_pool, m_pool_scale, m_sgu_ln_g, m_w_spatial, m_b_spatial, m_w_o, m_ln1_g, m_ln1_b, m_w_gate_up, m_w_down, m_ln2_g, m_ln2_b, v_w_in, v_w_conv, v_w_pool, v_pool_scale, v_sgu_ln_g, v_w_spatial, v_b_spatial, v_w_o, v_ln1_g, v_ln1_b, v_w_gate_up, v_w_down, v_ln2_g, v_ln2_b):
    L = DEPTH
    T = x.shape[1]
    mx, my, mc = _my_place()
    dev = 4 * mx + 2 * my + mc
    xs = x[0]
    target = loss_target[0]

    shards = (jnp.swapaxes(w_in, 1, 2).astype(BF16), jnp.swapaxes(w_gate_up, 1, 2).astype(BF16),
              w_o.astype(BF16), w_down.astype(BF16))
    gathers = [_ag_start([s[l] for s in shards], l) for l in range(L)]

    conv_cols = w_conv.shape[2]
    w_conv_z = lax.dynamic_update_slice(jnp.zeros((L, 3, CONV_W), F32), w_conv, (0, 0, dev * conv_cols))
    w_conv_full = _allreduce_small(_pack([w_conv_z]))
    w_conv_full = _unpack(w_conv_full, [(L, 3, CONV_W)])[0]

    loss_tile, grad_x2, big_grads, small_grads = _local_step(
        xs, target, gathers, w_conv_full, w_pool, pool_scale, sgu_ln_g, w_spatial, b_spatial,
        ln1_g, ln1_b, ln2_g, ln2_b)
    loss = lax.psum(loss_tile[0, 0], ("x", "y", "c"))
    grad_x = grad_x2[None]
    big_w = (w_in, w_gate_up, w_o, w_down)
    big_m = (m_w_in, m_w_gate_up, m_w_o, m_w_down)
    big_v = (v_w_in, v_w_gate_up, v_w_o, v_w_down)
    small_w = [w_conv_full, w_pool, pool_scale, sgu_ln_g, w_spatial, b_spatial, ln1_g, ln1_b, ln2_g, ln2_b]
    small_m = [m_w_conv, m_w_pool, m_pool_scale, m_sgu_ln_g, m_w_spatial, m_b_spatial, m_ln1_g, m_ln1_b, m_ln2_g, m_ln2_b]
    small_v = [v_w_conv, v_w_pool, v_pool_scale, v_sgu_ln_g, v_w_spatial, v_b_spatial, v_ln1_g, v_ln1_b, v_ln2_g, v_ln2_b]
    grads, deltas, new_m, new_v = _reduce_and_update(
        big_grads, small_grads, big_w, big_m, big_v, small_w, small_m, small_v)
    return (loss, grad_x, *grads, *deltas, *new_m, *new_v)


def _local_step(xs, target, gathers, w_conv_full, w_pool, pool_scale, sgu_ln_g, w_spatial, b_spatial,
                ln1_g, ln1_b, ln2_g, ln2_b):
    L = DEPTH
    T = xs.shape[0]
    mx, my, mc = _my_place()
    c_arr = jnp.reshape(mc, (1,)).astype(jnp.int32)
    q_arr = jnp.reshape(2 * mx + my, (1,)).astype(jnp.int32)
    eye2 = jnp.eye(2, dtype=F32)
    wp = w_pool.reshape(L, 2, 2, HALF, HALF)
    wpool_bd = jnp.einsum("ltgcd,gh->ltgchd", wp, eye2).reshape(L, 2, LANES, LANES)
    wsp_t = w_spatial.reshape(L, 3, 2 * CHUNK, CHUNK)
    bias_t = jnp.repeat(jnp.swapaxes(b_spatial.reshape(L, 3, 2, CHUNK), 2, 3), HALF, axis=3)
    ones = jnp.ones((1, D_MODEL), F32)
    zeros = jnp.zeros((1, D_MODEL), F32)

    saved = []
    prev, pg, pb = xs, ones, zeros
    prev_b = xs.astype(BF16)
    weights = []
    for l in range(L):
        send_sems, recv_sems, shards_l, lands_l, _ = gathers[l]
        after = [g[4] for g in gathers[1:]] if l == 0 else [prev_b]
        shards_l, lands_l = _ag_wait(send_sems, recv_sems, shards_l, lands_l, after, l)
        g_in, g_gu, g_o, g_dn = _ag_pass_on(shards_l, lands_l)
        weights.append((g_in, g_gu, g_o, g_dn))
        proj = _mm(prev_b, g_in, "nt", F32, 512, IN_W, D_MODEL, "mm_proj")
        mixcat = _mixer_fwd(proj, w_conv_full[l], wpool_bd[l], pool_scale[l][None], sgu_ln_g[l][None], wsp_t[l], bias_t[l])
        mix = _mm(mixcat, g_o, "nn", F32, T, 512, D_MODEL, "mm_wo")
        xhat1, rstd1, h_b = _ln_fwd(prev, pg, pb, mix, ln1_g[l][None], ln1_b[l][None])
        gu = _mm(h_b, g_gu, "nt", F32, T, 512, D_MODEL, "mm_gate_up")
        act = _swiglu_fwd(gu)
        ff = _mm(act, g_dn, "nn", F32, T, 256, D_FF, "mm_down")
        xhat2, rstd2, y_b = _ln_fwd(xhat1, ln1_g[l][None], ln1_b[l][None], ff, ln2_g[l][None], ln2_b[l][None])
        saved.append((prev_b, proj, mixcat, xhat1, rstd1, h_b, gu, xhat2, rstd2))
        prev, pg, pb, prev_b = xhat2, ln2_g[l][None], ln2_b[l][None], y_b

    loss_tile, dy = _loss_head(prev, pg, pb, target)

    small = [None] * L
    big = [None] * L
    dres, dmm = None, dy
    in_flight = None
    for l in reversed(range(L)):
        prev_b, proj, mixcat, xhat1, rstd1, h_b, gu, xhat2, rstd2 = saved[l]
        g_in, g_gu, g_o, g_dn = weights[l]
        deps = [in_flight[4]] if in_flight is not None else []
        dr2, dr2_b, dg2, db2 = _ln_bwd(dres, dmm, xhat2, rstd2, ln2_g[l][None], deps=deps)
        dact = _mm(dr2_b, g_dn, "nt", F32, T, 256, D_MODEL, "mm_dact")
        dgu, act = _swiglu_bwd(gu, dact)
        p_dn = _mm(act, dr2_b, "tn", BF16, 256, D_MODEL, T, "mm_dw_down")
        p_gu = _mm(dgu, h_b, "tn", BF16, 512, D_MODEL, T, "mm_dw_gate_up")
        dh = _mm(dgu, g_gu, "nn", F32, T, 512, SW_TC, "mm_dh")
        dr1, dr1_b, dg1, db1 = _ln_bwd(dr2, dh, xhat1, rstd1, ln1_g[l][None])
        dmix = _mm(dr1_b, g_o, "nt", F32, T, 512, D_MODEL, "mm_dmix")
        p_o = _mm(mixcat, dr1_b, "tn", BF16, 512, D_MODEL, T, "mm_dw_o")
        dproj, dwc, dwp, dps, dlng, dwsp, dbias = _mixer_bwd(
            proj, dmix, w_conv_full[l], wpool_bd[l], pool_scale[l][None], sgu_ln_g[l][None], wsp_t[l], bias_t[l])
        p_in = _mm(dproj, prev_b, "tn", BF16, IN_W, D_MODEL, T, "mm_dw_in")
        dx = _mm(dproj, g_in, "nn", F32, T, 512, IN_W, "mm_dx")
        small[l] = (dwc, dwp, dps, dlng, dwsp, dbias, dg1, db1, dg2, db2)
        dres, dmm = dr1, dx
        if in_flight is not None:
            big[l + 1] = _rs_chip_finish(in_flight, [dx], q_arr, l + 1)
        parts = [p.reshape(4, 2, r, D_MODEL) for p, r in zip((p_in, p_gu, p_o, p_dn), SHARD_ROWS)]
        got1 = _rs_sibling_exchange(parts)
        sums = [_rs_chip_sum(p, g, c_arr) for p, g in zip(parts, got1)]
        in_flight = _rs_chip_start(sums, l)
    big[0] = _rs_chip_finish(in_flight, [], q_arr, 0)
    grad_x = _residual_out(dres, dmm)
    big_grads = [jnp.stack([big[l][w] for l in range(L)]) for w in range(4)]

    def stack(i):
        return jnp.stack([small[l][i] for l in range(L)])

    dwp_bd = stack(1).reshape(L, 2, 2, HALF, 2, HALF)
    dwp_all = jnp.einsum("ltgchd,gh->ltgcd", dwp_bd, eye2).reshape(L, 4, HALF, HALF)
    dbs_all = jnp.swapaxes(stack(5)[:, :, :, :2], 2, 3).reshape(L, 6, CHUNK)
    small_grads = [stack(0), dwp_all, stack(2).reshape(L, POOL_W), stack(3).reshape(L, SGU_W),
                   stack(4).reshape(L, 6, CHUNK, CHUNK), dbs_all] + [stack(i).reshape(L, D_MODEL) for i in (6, 7, 8, 9)]
    return loss_tile, grad_x, big_grads, small_grads


def _rs_chip_finish(in_flight, after, q, layer):
    send_sems, recv_sems, sums, lands, _ = in_flight
    sums, got = _rs_chip_wait(send_sems, recv_sems, sums, lands, after, layer)
    return [_rs_finish(s, g, q) for s, g in zip(sums, got)]


def _reduce_and_update(big_grads, small_grads, big_w, big_m, big_v, small_w, small_m, small_v):
    L = DEPTH
    mx, my, mc = _my_place()
    dev = 4 * mx + 2 * my + mc
    conv_cols = CONV_W // N_DEV
    w_in, w_gate_up, w_o, w_down = big_w
    m_w_in, m_w_gate_up, m_w_o, m_w_down = big_m
    v_w_in, v_w_gate_up, v_w_o, v_w_down = big_v
    gt_in, gt_gu, g_w_o, g_w_dn = big_grads
    g_w_in = jnp.swapaxes(gt_in, 1, 2)
    g_w_gu = jnp.swapaxes(gt_gu, 1, 2)

    small_shapes = [a.shape for a in small_grads]
    packed_g = _allreduce_small(_pack(small_grads))

    def widen_conv(a):
        return lax.dynamic_update_slice(jnp.zeros((L, 3, CONV_W), F32), a, (0, 0, dev * conv_cols))

    small_m = [widen_conv(small_m[0])] + list(small_m[1:])
    small_v = [widen_conv(small_v[0])] + list(small_v[1:])
    pk_d, pk_m, pk_v = _adamw(_pack(small_w), packed_g, _pack(small_m), _pack(small_v), packed_g.shape[0] // 2)
    sg = _unpack(packed_g, small_shapes)
    sd = _unpack(pk_d, small_shapes)
    sm = _unpack(pk_m, small_shapes)
    sv = _unpack(pk_v, small_shapes)

    def conv_cols_of(a):
        return lax.dynamic_slice(a, (0, 0, dev * conv_cols), (L, 3, conv_cols))

    for lst in (sg, sd, sm, sv):
        lst[0] = conv_cols_of(lst[0])

    def big(w, g, m, v, tr):
        s = w.shape
        d, mn, vn = _adamw(w.reshape(-1, s[-1]), g.reshape(-1, s[-1]), m.reshape(-1, s[-1]), v.reshape(-1, s[-1]), tr)
        return d.reshape(s), mn.reshape(s), vn.reshape(s)

    d_in, m_in, v_in = big(w_in, g_w_in, m_w_in, v_w_in, 512)
    d_gu, m_gu, v_gu = big(w_gate_up, g_w_gu, m_w_gate_up, v_w_gate_up, 512)
    d_o, m_o, v_o = big(w_o, g_w_o, m_w_o, v_w_o, 128)
    d_dn, m_dn, v_dn = big(w_down, g_w_dn, m_w_down, v_w_down, 352)

    def ordered(big_in, big_o, big_gu, big_dn, sm_list):
        return [big_in, sm_list[0], sm_list[1], sm_list[2], sm_list[3], sm_list[4], sm_list[5], big_o,
                sm_list[6], sm_list[7], big_gu, big_dn, sm_list[8], sm_list[9]]

    grads = ordered(g_w_in, g_w_o, g_w_gu, g_w_dn, sg)
    deltas = ordered(d_in, d_o, d_gu, d_dn, sd)
    new_m = ordered(m_in, m_o, m_gu, m_dn, sm)
    new_v = ordered(v_in, v_o, v_gu, v_dn, sv)
    return grads, deltas, new_m, new_v
```

```python
import functools
import math

import jax
import jax.numpy as jnp
from jax import lax
from jax.experimental import pallas as pl
from jax.experimental.pallas import tpu as pltpu

F32 = jnp.float32
BF16 = jnp.bfloat16
MESH = pl.DeviceIdType.MESH

D_MODEL = 1024
DEPTH = 4
CONV_W = 384
POOL_W = 256
SGU_W = 384
IN_W = 3 * CONV_W + POOL_W + 2 * SGU_W
D_FF = 2816
CHUNK = 128
ALPHA = float((2 * DEPTH) ** 0.25)
LN_EPS = 1e-5
ADAM_LR, ADAM_B1, ADAM_B2, ADAM_EPS, ADAM_WD, ADAM_STEP = 0.001, 0.9, 0.999, 1e-08, 0.01, 10

N_DEV = 8
LANES = 128
HALF = 64
SHARD_ROWS = (IN_W // N_DEV, 2 * D_FF // N_DEV, D_MODEL // N_DEV, D_FF // N_DEV)
VMEM_LIMIT = 52 * 1024 * 1024

INV_SQRT2 = 0.7071067811865476
INV_SQRT_2PI = 0.3989422804014327


def _cparams(sem=None, **kw):
    if sem is not None:
        kw["dimension_semantics"] = sem
    return pltpu.CompilerParams(vmem_limit_bytes=VMEM_LIMIT, **kw)


_DN = {"nn": (((1,), (0,)), ((), ())), "nt": (((1,), (1,)), ((), ())), "tn": (((0,), (0,)), ((), ()))}


def _mm(a, b, mode, out_dtype, tm, tn, tk, name, deps=()):
    if mode == "nn":
        (M, K), N = a.shape, b.shape[1]
    elif mode == "nt":
        (M, K), N = a.shape, b.shape[0]
    else:
        (K, M), N = a.shape, b.shape[1]
    assert M % tm == 0 and N % tn == 0 and K % tk == 0, (M, N, K, tm, tn, tk)
    nk = K // tk
    nd = len(deps)

    def body(*refs):
        a_ref, b_ref, o_ref = refs[0], refs[1], refs[2 + nd]
        acc_ref = refs[3 + nd] if nk > 1 else None
        p = lax.dot_general(a_ref[...], b_ref[...], _DN[mode], preferred_element_type=F32)
        if nk == 1:
            o_ref[...] = p.astype(o_ref.dtype)
        else:
            k = pl.program_id(2)

            @pl.when(k == 0)
            def _():
                acc_ref[...] = p

            @pl.when(k > 0)
            def _():
                acc_ref[...] += p

            @pl.when(k == nk - 1)
            def _():
                o_ref[...] = acc_ref[...].astype(o_ref.dtype)

    if mode == "nn":
        a_spec = pl.BlockSpec((tm, tk), lambda i, j, k: (i, k))
        b_blk, b_idx = (tk, tn), (lambda i, j, k: (k, j))
    elif mode == "nt":
        a_spec = pl.BlockSpec((tm, tk), lambda i, j, k: (i, k))
        b_blk, b_idx = (tn, tk), (lambda i, j, k: (j, k))
    else:
        a_spec = pl.BlockSpec((tk, tm), lambda i, j, k: (k, i))
        b_blk, b_idx = (tk, tn), (lambda i, j, k: (k, j))
    return pl.pallas_call(
        body,
        name=name,
        grid=(M // tm, N // tn, nk),
        in_specs=[a_spec, pl.BlockSpec(b_blk, b_idx)] + [pl.BlockSpec(memory_space=pl.ANY)] * nd,
        out_specs=pl.BlockSpec((tm, tn), lambda i, j, k: (i, j)),
        out_shape=jax.ShapeDtypeStruct((M, N), out_dtype),
        scratch_shapes=[pltpu.VMEM((tm, tn), F32)] if nk > 1 else [],
        compiler_params=_cparams(("parallel", "parallel", "arbitrary")),
    )(a, b, *deps)


def _gelu(x):
    return 0.5 * x * (1.0 + lax.erf(x * INV_SQRT2))


def _gelu_grad(x):
    return 0.5 * (1.0 + lax.erf(x * INV_SQRT2)) + x * (jnp.exp(-0.5 * x * x) * INV_SQRT_2PI)


def _shift_down(z, k):
    row = lax.broadcasted_iota(jnp.int32, z.shape, 0)
    return jnp.where(row >= k, pltpu.roll(z, k, 0), 0.0)


def _shift_up(z, k):
    n = z.shape[0]
    row = lax.broadcasted_iota(jnp.int32, z.shape, 0)
    return jnp.where(row < n - k, pltpu.roll(z, n - k, 0), 0.0)


def _lo_mask(shape):
    return lax.broadcasted_iota(jnp.int32, shape, len(shape) - 1) < HALF


def _seg_mean(x, lo):
    a = jnp.sum(jnp.where(lo, x, 0.0), axis=-1, keepdims=True)
    b = jnp.sum(jnp.where(lo, 0.0, x), axis=-1, keepdims=True)
    return jnp.where(lo, a, b) * (1.0 / HALF)


def _pool_windows(first):
    lo = _lo_mask((1, LANES))
    return jnp.where(first, jnp.where(lo, 2.0, 4.0), jnp.where(lo, 8.0, 16.0)), lo


def _pool_mean_minus_token(p, first):
    wl, lo = _pool_windows(first)
    s2 = p + _shift_down(p, 1)
    s4 = s2 + _shift_down(s2, 2)
    s8 = s4 + _shift_down(s4, 4)
    s16 = s8 + _shift_down(s8, 8)
    win = jnp.where(first, jnp.where(lo, s2, s4), jnp.where(lo, s8, s16))
    t1 = (lax.broadcasted_iota(jnp.int32, p.shape, 0) + 1).astype(F32)
    count = jnp.minimum(t1, wl)
    return win / count - p, count


def _tril_keep():
    r = lax.broadcasted_iota(jnp.int32, (2 * CHUNK, CHUNK), 0)
    s = lax.broadcasted_iota(jnp.int32, (2 * CHUNK, CHUNK), 1)
    return s <= (r & (CHUNK - 1))


def _sgu_chunk_fwd(u, v, g, wm, bias, lo):
    ug = _gelu(u)
    vg = _gelu(v)
    mu = _seg_mean(vg, lo)
    xc = vg - mu
    var = _seg_mean(xc * xc, lo)
    rstd = lax.rsqrt(var + LN_EPS)
    vn = xc * rstd
    vh = (vn * g).astype(BF16)
    mm2 = jnp.dot(wm, vh, preferred_element_type=F32)
    mixed = jnp.where(lo, mm2[:CHUNK], mm2[CHUNK:]) + bias
    return ug, vn, rstd, vh, mixed


def _mixer_fwd(proj, wconv, wpool_bd, pscale, lng, wsp, bias):
    T = proj.shape[0]
    nchunk = T // CHUNK

    def body(a_ref, b_ref, c_ref, wc_ref, wp_ref, ps_ref, lng_ref, wsp_ref, bias_ref, o_ref):
        j = pl.program_id(0)

        @pl.when(j < 3)
        def _conv():
            z = c_ref[...] * a_ref[...]
            w = wc_ref[...]
            y = w[0:1] * _shift_down(z, 2) + w[1:2] * _shift_down(z, 1) + w[2:3] * z
            o_ref[...] = (b_ref[...] * y).astype(o_ref.dtype)

        @pl.when((j >= 3) & (j < 5))
        def _pool():
            d, _ = _pool_mean_minus_token(a_ref[...], j == 3)
            y = jnp.dot(d.astype(BF16), wp_ref[...].astype(BF16), preferred_element_type=F32)
            o_ref[...] = (y * ps_ref[...]).astype(o_ref.dtype)

        @pl.when(j >= 5)
        def _sgu():
            lo = _lo_mask((CHUNK, LANES))
            wm = jnp.where(_tril_keep(), wsp_ref[...], 0.0).astype(BF16)
            bias_t = bias_ref[...]
            g = lng_ref[...]

            def chunk(n, carry):
                rows = pl.ds(pl.multiple_of(n * CHUNK, CHUNK), CHUNK)
                ug, _, _, _, mixed = _sgu_chunk_fwd(a_ref[rows, :], b_ref[rows, :], g, wm, bias_t, lo)
                o_ref[rows, :] = (ug * mixed).astype(o_ref.dtype)
                return carry

            lax.fori_loop(0, nchunk, chunk, 0)

    def col(f):
        return lambda j: (0, f(j))

    clip = lambda v, lo, hi: jnp.minimum(jnp.maximum(v, lo), hi)
    return pl.pallas_call(
        body,
        name="mixer_fwd",
        grid=(8,),
        in_specs=[
            pl.BlockSpec((T, LANES), col(lambda j: jnp.where(j < 3, j, jnp.where(j < 5, j + 6, j + 6)))),
            pl.BlockSpec((T, LANES), col(lambda j: jnp.where(j < 3, j + 3, jnp.where(j < 5, 5, j + 9)))),
            pl.BlockSpec((T, LANES), col(lambda j: jnp.where(j < 3, j + 6, 8))),
            pl.BlockSpec((3, LANES), col(lambda j: clip(j, 0, 2))),
            pl.BlockSpec((None, LANES, LANES), lambda j: (clip(j - 3, 0, 1), 0, 0)),
            pl.BlockSpec((1, LANES), col(lambda j: clip(j - 3, 0, 1))),
            pl.BlockSpec((1, LANES), col(lambda j: clip(j - 5, 0, 2))),
            pl.BlockSpec((None, 2 * CHUNK, CHUNK), lambda j: (clip(j - 5, 0, 2), 0, 0)),
            pl.BlockSpec((None, CHUNK, LANES), lambda j: (clip(j - 5, 0, 2), 0, 0)),
        ],
        out_specs=pl.BlockSpec((T, LANES), lambda j: (0, j)),
        out_shape=jax.ShapeDtypeStruct((T, D_MODEL), BF16),
        compiler_params=_cparams(("arbitrary",)),
    )(proj, proj, proj, wconv, wpool_bd, pscale, lng, wsp, bias)


def _mixer_bwd(proj, dmix, wconv, wpool_bd, pscale, lng, wsp, bias):
    T = proj.shape[0]
    nchunk = T // CHUNK

    def body(a_ref, b_ref, c_ref, dm_ref, wc_ref, wp_ref, ps_ref, lng_ref, wsp_ref, bias_ref,
             o_ref, dwc_ref, dwp_ref, dps_ref, dlng_ref, dwsp_ref, dbias_ref, keep1, keep2):
        k = pl.program_id(0)

        @pl.when(k < 3)
        def _conv():
            xa, gb, gc, dya = a_ref[...], b_ref[...], c_ref[...], dm_ref[...]
            w = wc_ref[...]
            z = gc * xa
            z1 = _shift_down(z, 1)
            z2 = _shift_down(z, 2)
            y = w[0:1] * z2 + w[1:2] * z1 + w[2:3] * z
            dyv = dya * gb
            dz = w[2:3] * dyv + w[1:2] * _shift_up(dyv, 1) + w[0:1] * _shift_up(dyv, 2)
            dwc_ref[0:1, :] = jnp.sum(dyv * z2, axis=0, keepdims=True)
            dwc_ref[1:2, :] = jnp.sum(dyv * z1, axis=0, keepdims=True)
            dwc_ref[2:3, :] = jnp.sum(dyv * z, axis=0, keepdims=True)
            o_ref[...] = (dz * gc).astype(o_ref.dtype)
            keep1[k] = (dya * y).astype(keep1.dtype)
            keep1[k + 3] = (dz * xa).astype(keep1.dtype)

        @pl.when((k >= 3) & (k < 9))
        def _emit_gb_gc():
            o_ref[...] = keep1[k - 3]

        @pl.when((k >= 9) & (k < 11))
        def _pool():
            first = k == 9
            p, dyb = a_ref[...], dm_ref[...]
            d, count = _pool_mean_minus_token(p, first)
            w2 = wp_ref[...].astype(BF16)
            db = d.astype(BF16)
            y = jnp.dot(db, w2, preferred_element_type=F32)
            dps_ref[...] = jnp.sum(dyb * y, axis=0, keepdims=True)
            dyv = (dyb * ps_ref[...]).astype(BF16)
            dd = lax.dot_general(dyv, w2, _DN["nt"], preferred_element_type=F32)
            dwp_ref[...] = lax.dot_general(db, dyv, _DN["tn"], preferred_element_type=F32)
            dwin = dd / count
            a2 = dwin + _shift_up(dwin, 1)
            a4 = a2 + _shift_up(a2, 2)
            a8 = a4 + _shift_up(a4, 4)
            a16 = a8 + _shift_up(a8, 8)
            _, lo = _pool_windows(first)
            back = jnp.where(first, jnp.where(lo, a2, a4), jnp.where(lo, a8, a16))
            o_ref[...] = (back - dd).astype(o_ref.dtype)

        @pl.when((k >= 11) & (k < 14))
        def _sgu():
            lo = _lo_mask((CHUNK, LANES))
            keep = _tril_keep()
            wm = jnp.where(keep, wsp_ref[...], 0.0).astype(BF16)
            bias_t = bias_ref[...]
            g = lng_ref[...]
            dwsp_ref[...] = jnp.zeros_like(dwsp_ref)
            dbias_ref[...] = jnp.zeros_like(dbias_ref)
            dlng_ref[...] = jnp.zeros_like(dlng_ref)

            def chunk(n, carry):
                rows = pl.ds(pl.multiple_of(n * CHUNK, CHUNK), CHUNK)
                u, v, dyc = a_ref[rows, :], b_ref[rows, :], dm_ref[rows, :]
                ug, vn, rstd, vh, mixed = _sgu_chunk_fwd(u, v, g, wm, bias_t, lo)
                dmx = dyc * ug
                o_ref[rows, :] = (dyc * mixed * _gelu_grad(u)).astype(o_ref.dtype)
                dbias_ref[...] += dmx
                dst = jnp.concatenate([jnp.where(lo, dmx, 0.0), jnp.where(lo, 0.0, dmx)], axis=0).astype(BF16)
                dwsp_ref[...] += lax.dot_general(dst, vh, _DN["nt"], preferred_element_type=F32)
                dvh = lax.dot_general(wm, dst, _DN["tn"], preferred_element_type=F32)
                dlng_ref[...] += jnp.sum(dvh * vn, axis=0, keepdims=True)
                dvn = dvh * g
                m1 = _seg_mean(dvn, lo)
                m2 = _seg_mean(dvn * vn, lo)
                dvg = rstd * (dvn - m1 - vn * m2)
                keep2[k - 11, rows, :] = (dvg * _gelu_grad(v)).astype(keep2.dtype)
                return carry

            lax.fori_loop(0, nchunk, chunk, 0)
            dwsp_ref[...] = jnp.where(keep, dwsp_ref[...], 0.0)
            dbt = dbias_ref[...]
            lane = lax.broadcasted_iota(jnp.int32, (CHUNK, LANES), 1)
            sa = jnp.sum(jnp.where(lo, dbt, 0.0), axis=-1, keepdims=True)
            sb = jnp.sum(jnp.where(lo, 0.0, dbt), axis=-1, keepdims=True)
            dbias_ref[...] = jnp.where(lane == 0, sa, jnp.where(lane == 1, sb, 0.0))

        @pl.when(k >= 14)
        def _emit_v():
            o_ref[...] = keep2[k - 14]

    def col(f):
        return lambda k: (0, f(k))

    clip = lambda v, lo, hi: jnp.minimum(jnp.maximum(v, lo), hi)
    view_a = lambda k: jnp.where(k < 3, k, jnp.where(k < 9, 2, jnp.where(k < 14, k, 13)))
    view_b = lambda k: jnp.where(k < 3, k + 3, jnp.where(k < 11, 5, jnp.where(k < 14, k + 3, 16)))
    view_c = lambda k: jnp.where(k < 3, k + 6, 8)
    view_dm = lambda k: jnp.where(k < 3, k, jnp.where(k < 9, 2, jnp.where(k < 14, k - 6, 7)))
    return pl.pallas_call(
        body,
        name="mixer_bwd",
        grid=(17,),
        in_specs=[
            pl.BlockSpec((T, LANES), col(view_a)),
            pl.BlockSpec((T, LANES), col(view_b)),
            pl.BlockSpec((T, LANES), col(view_c)),
            pl.BlockSpec((T, LANES), col(view_dm)),
            pl.BlockSpec((3, LANES), col(lambda k: clip(k, 0, 2))),
            pl.BlockSpec((None, LANES, LANES), lambda k: (clip(k - 9, 0, 1), 0, 0)),
            pl.BlockSpec((1, LANES), col(lambda k: clip(k - 9, 0, 1))),
            pl.BlockSpec((1, LANES), col(lambda k: clip(k - 11, 0, 2))),
            pl.BlockSpec((None, 2 * CHUNK, CHUNK), lambda k: (clip(k - 11, 0, 2), 0, 0)),
            pl.BlockSpec((None, CHUNK, LANES), lambda k: (clip(k - 11, 0, 2), 0, 0)),
        ],
        out_specs=[
            pl.BlockSpec((T, LANES), lambda k: (0, k)),
            pl.BlockSpec((3, LANES), col(lambda k: clip(k, 0, 2))),
            pl.BlockSpec((None, LANES, LANES), lambda k: (clip(k - 9, 0, 1), 0, 0)),
            pl.BlockSpec((1, LANES), col(lambda k: clip(k - 9, 0, 1))),
            pl.BlockSpec((1, LANES), col(lambda k: clip(k - 11, 0, 2))),
            pl.BlockSpec((None, 2 * CHUNK, CHUNK), lambda k: (clip(k - 11, 0, 2), 0, 0)),
            pl.BlockSpec((None, CHUNK, LANES), lambda k: (clip(k - 11, 0, 2), 0, 0)),
        ],
        out_shape=[
            jax.ShapeDtypeStruct((T, IN_W), BF16),
            jax.ShapeDtypeStruct((3, CONV_W), F32),
            jax.ShapeDtypeStruct((2, LANES, LANES), F32),
            jax.ShapeDtypeStruct((1, POOL_W), F32),
            jax.ShapeDtypeStruct((1, SGU_W), F32),
            jax.ShapeDtypeStruct((3, 2 * CHUNK, CHUNK), F32),
            jax.ShapeDtypeStruct((3, CHUNK, LANES), F32),
        ],
        scratch_shapes=[pltpu.VMEM((6, T, LANES), BF16), pltpu.VMEM((3, T, LANES), BF16)],
        compiler_params=_cparams(("arbitrary",)),
    )(proj, proj, proj, dmix, wconv, wpool_bd, pscale, lng, wsp, bias)


def _ln_fwd(prev, pg, pb, mmout, g, b, tm=256):
    T = prev.shape[0]

    def body(prev_ref, pg_ref, pb_ref, mm_ref, g_ref, b_ref, xhat_ref, rstd_ref, y_ref):
        r = ALPHA * (prev_ref[...] * pg_ref[...] + pb_ref[...]) + mm_ref[...]
        mu = jnp.mean(r, axis=-1, keepdims=True)
        xc = r - mu
        var = jnp.mean(xc * xc, axis=-1, keepdims=True)
        rstd = lax.rsqrt(var + LN_EPS)
        xhat = xc * rstd
        xhat_ref[...] = xhat
        rstd_ref[...] = rstd
        y_ref[...] = (xhat * g_ref[...] + b_ref[...]).astype(y_ref.dtype)

    row = pl.BlockSpec((tm, D_MODEL), lambda i: (i, 0))
    vec = pl.BlockSpec((1, D_MODEL), lambda i: (0, 0))
    return pl.pallas_call(
        body,
        name="ln_fwd",
        grid=(T // tm,),
        in_specs=[row, vec, vec, row, vec, vec],
        out_specs=[row, pl.BlockSpec((tm, 1), lambda i: (i, 0)), row],
        out_shape=[jax.ShapeDtypeStruct((T, D_MODEL), F32), jax.ShapeDtypeStruct((T, 1), F32),
                   jax.ShapeDtypeStruct((T, D_MODEL), BF16)],
        compiler_params=_cparams(("parallel",)),
    )(prev, pg, pb, mmout, g, b)


def _ln_bwd(dres, dmm, xhat, rstd, g, tm=256, deps=()):
    T = xhat.shape[0]
    has_res = dres is not None
    nd = len(deps)

    def body(*refs):
        refs = refs[:len(refs) - 4 - nd] + refs[len(refs) - 4:]
        if has_res:
            dres_ref, dmm_ref, xhat_ref, rstd_ref, g_ref, dr_ref, drb_ref, dg_ref, db_ref = refs
            dy = ALPHA * dres_ref[...] + dmm_ref[...]
        else:
            dmm_ref, xhat_ref, rstd_ref, g_ref, dr_ref, drb_ref, dg_ref, db_ref = refs
            dy = dmm_ref[...]
        xhat_v = xhat_ref[...]

        @pl.when(pl.program_id(0) == 0)
        def _():
            dg_ref[...] = jnp.zeros_like(dg_ref)
            db_ref[...] = jnp.zeros_like(db_ref)

        dg_ref[...] += jnp.sum(dy * xhat_v, axis=0, keepdims=True)
        db_ref[...] += jnp.sum(dy, axis=0, keepdims=True)
        dxh = dy * g_ref[...]
        m1 = jnp.mean(dxh, axis=-1, keepdims=True)
        m2 = jnp.mean(dxh * xhat_v, axis=-1, keepdims=True)
        dr = rstd_ref[...] * (dxh - m1 - xhat_v * m2)
        dr_ref[...] = dr
        drb_ref[...] = dr.astype(drb_ref.dtype)

    row = pl.BlockSpec((tm, D_MODEL), lambda i: (i, 0))
    vec = pl.BlockSpec((1, D_MODEL), lambda i: (0, 0))
    in_specs = ([row] if has_res else []) + [row, row, pl.BlockSpec((tm, 1), lambda i: (i, 0)), vec]
    in_specs += [pl.BlockSpec(memory_space=pl.ANY)] * nd
    args = ([dres] if has_res else []) + [dmm, xhat, rstd, g] + list(deps)
    return pl.pallas_call(
        body,
        name="ln_bwd_res" if has_res else "ln_bwd",
        grid=(T // tm,),
        in_specs=in_specs,
        out_specs=[row, row, vec, vec],
        out_shape=[jax.ShapeDtypeStruct((T, D_MODEL), F32), jax.ShapeDtypeStruct((T, D_MODEL), BF16),
                   jax.ShapeDtypeStruct((1, D_MODEL), F32), jax.ShapeDtypeStruct((1, D_MODEL), F32)],
        compiler_params=_cparams(("arbitrary",)),
    )(*args)


def _loss_head(xhat, g, b, target, tm=256):
    T = xhat.shape[0]

    def body(xhat_ref, g_ref, b_ref, t_ref, loss_ref, dy_ref):
        err = xhat_ref[...] * g_ref[...] + b_ref[...] - t_ref[...]

        @pl.when(pl.program_id(0) == 0)
        def _():
            loss_ref[...] = jnp.zeros_like(loss_ref)

        part = jnp.sum(jnp.sum(err * err, axis=-1, keepdims=True), axis=0, keepdims=True)
        loss_ref[...] += jnp.broadcast_to(part * (0.5 / D_MODEL), loss_ref.shape)
        dy_ref[...] = err * (1.0 / D_MODEL)

    row = pl.BlockSpec((tm, D_MODEL), lambda i: (i, 0))
    vec = pl.BlockSpec((1, D_MODEL), lambda i: (0, 0))
    return pl.pallas_call(
        body,
        name="loss_head",
        grid=(T // tm,),
        in_specs=[row, vec, vec, row],
        out_specs=[pl.BlockSpec((8, LANES), lambda i: (0, 0)), row],
        out_shape=[jax.ShapeDtypeStruct((8, LANES), F32), jax.ShapeDtypeStruct((T, D_MODEL), F32)],
        compiler_params=_cparams(("arbitrary",)),
    )(xhat, g, b, target)


def _residual_out(dres, dmm, tm=256):
    T = dres.shape[0]

    def body(a_ref, b_ref, o_ref):
        o_ref[...] = ALPHA * a_ref[...] + b_ref[...]

    row = pl.BlockSpec((tm, D_MODEL), lambda i: (i, 0))
    return pl.pallas_call(
        body, name="residual_out", grid=(T // tm,), in_specs=[row, row], out_specs=row,
        out_shape=jax.ShapeDtypeStruct((T, D_MODEL), F32), compiler_params=_cparams(("parallel",)),
    )(dres, dmm)


SW_TC = 1408


def _swiglu_fwd(gu, tm=128):
    T = gu.shape[0]

    def body(gu_ref, o_ref):
        gv = gu_ref[:, :D_FF]
        o_ref[...] = (gv * jax.nn.sigmoid(gv) * gu_ref[:, D_FF:]).astype(o_ref.dtype)

    return pl.pallas_call(
        body, name="swiglu_fwd", grid=(T // tm,),
        in_specs=[pl.BlockSpec((tm, 2 * D_FF), lambda i: (i, 0))],
        out_specs=pl.BlockSpec((tm, D_FF), lambda i: (i, 0)),
        out_shape=jax.ShapeDtypeStruct((T, D_FF), BF16), compiler_params=_cparams(("parallel",)),
    )(gu)


def _swiglu_bwd(gu, dact, tm=128):
    T = gu.shape[0]

    def body(gu_ref, da_ref, dgu_ref, act_ref):
        gv, uv, da = gu_ref[:, :D_FF], gu_ref[:, D_FF:], da_ref[...]
        s = jax.nn.sigmoid(gv)
        sg = gv * s
        act_ref[...] = (sg * uv).astype(act_ref.dtype)
        dgu_ref[:, D_FF:] = (da * sg).astype(dgu_ref.dtype)
        dgu_ref[:, :D_FF] = (da * uv * (s * (1.0 + gv * (1.0 - s)))).astype(dgu_ref.dtype)

    wide = pl.BlockSpec((tm, 2 * D_FF), lambda i: (i, 0))
    half = pl.BlockSpec((tm, D_FF), lambda i: (i, 0))
    return pl.pallas_call(
        body, name="swiglu_bwd", grid=(T // tm,),
        in_specs=[wide, half], out_specs=[wide, half],
        out_shape=[jax.ShapeDtypeStruct((T, 2 * D_FF), BF16), jax.ShapeDtypeStruct((T, D_FF), BF16)],
        compiler_params=_cparams(("parallel",)),
    )(gu, dact)


def _adamw(w, g, m, v, tr):
    R, C = w.shape
    assert R % tr == 0
    c1 = 1.0 - ADAM_B1 ** ADAM_STEP
    c2 = 1.0 - ADAM_B2 ** ADAM_STEP

    def body(w_ref, g_ref, m_ref, v_ref, d_ref, mo_ref, vo_ref):
        gv = g_ref[...]
        mn = ADAM_B1 * m_ref[...] + (1.0 - ADAM_B1) * gv
        vn = ADAM_B2 * v_ref[...] + (1.0 - ADAM_B2) * (gv * gv)
        d_ref[...] = -ADAM_LR * ((mn / c1) / (jnp.sqrt(vn / c2) + ADAM_EPS) + ADAM_WD * w_ref[...])
        mo_ref[...] = mn
        vo_ref[...] = vn

    blk = pl.BlockSpec((tr, C), lambda i: (i, 0))
    return pl.pallas_call(
        body, name="adamw", grid=(R // tr,), in_specs=[blk] * 4, out_specs=[blk] * 3,
        out_shape=[jax.ShapeDtypeStruct((R, C), F32)] * 3, compiler_params=_cparams(("parallel",)),
    )(w, g, m, v)


def _my_place():
    return lax.axis_index("x"), lax.axis_index("y"), lax.axis_index("c")


ANY = pl.BlockSpec(memory_space=pl.ANY)
HBM = pl.BlockSpec(memory_space=pltpu.HBM)
SEM = pl.BlockSpec(memory_space=pltpu.SEMAPHORE)
EFFECT = pltpu.SideEffectType.DATAFLOW_SIDE_EFFECTING


def _in_hbm(a):
    return pltpu.with_memory_space_constraint(a, pltpu.HBM)


def _block_rows(ref, w, dev):
    r = SHARD_ROWS[w]
    start = pl.multiple_of((4 * dev[0] + 2 * dev[1] + dev[2]) * r, 16)
    return ref.at[pl.ds(start, r), :]


def _ag_first_copies(s_refs, land_refs, send_sems, recv_sems, receiving):
    x, y, c = _my_place()
    peers = [(x, y, 1 - c)] + [(*chip, c) for chip in _other_chips(x, y)]
    copies = []
    for k, peer in enumerate(peers):
        block = peer if receiving else (x, y, c)
        copies += [pltpu.make_async_remote_copy(
            src_ref=s_refs[w], dst_ref=_block_rows(land_refs[w], w, block),
            send_sem=send_sems.at[k * len(s_refs) + w], recv_sem=recv_sems.at[k * len(s_refs) + w],
            device_id=peer, device_id_type=MESH)
            for w in range(len(s_refs))]
    return copies


def _ag_start(shards, layer, after=()):
    nw = len(shards)

    def body(*refs):
        s_refs, land_refs = refs[:nw], refs[nw:2 * nw]
        token = refs[-1]
        sems = 2 * nw + len(after)
        for cp in _ag_first_copies(s_refs, land_refs, refs[sems], refs[sems + 1], False):
            cp.start()
        token[...] = jnp.zeros_like(token)

    lands = [lax.empty((N_DEV * s.shape[0], D_MODEL), BF16) for s in shards]
    out = pl.pallas_call(
        body, name="ag_start_%d" % layer,
        in_specs=[HBM] * (2 * nw) + [ANY] * len(after),
        out_specs=(SEM, SEM, *[HBM] * (2 * nw), pl.BlockSpec(memory_space=pltpu.VMEM)),
        out_shape=(pltpu.SemaphoreType.DMA((4 * nw,)), pltpu.SemaphoreType.DMA((4 * nw,)),
                   *[pltpu.HBM(a.shape, a.dtype) for a in list(shards) + lands],
                   jax.ShapeDtypeStruct((8, LANES), F32)),
        input_output_aliases={i: 2 + i for i in range(2 * nw)},
        compiler_params=pltpu.CompilerParams(has_side_effects=EFFECT),
    )(*[_in_hbm(a) for a in list(shards) + lands], *after)
    return out[0], out[1], out[2:2 + nw], out[2 + nw:2 + 2 * nw], out[-1]


def _ag_wait(send_sems, recv_sems, shards, lands, after, layer):
    nw = len(shards)

    def body(*refs):
        s_refs, land_refs = refs[:nw], refs[nw:2 * nw]
        for cp in _ag_first_copies(s_refs, land_refs, refs[2 * nw], refs[2 * nw + 1], True):
            cp.wait_send()
            cp.wait_recv()

    out = pl.pallas_call(
        body, name="ag_wait_%d" % layer,
        in_specs=[HBM] * (2 * nw) + [SEM, SEM] + [ANY] * len(after),
        out_specs=[HBM] * (2 * nw),
        out_shape=[pltpu.HBM(a.shape, a.dtype) for a in list(shards) + list(lands)],
        input_output_aliases={i: i for i in range(2 * nw)},
        compiler_params=pltpu.CompilerParams(has_side_effects=EFFECT),
    )(*shards, *lands, send_sems, recv_sems, *after)
    return out[:nw], out[nw:]


def _ag_pass_on(shards, lands):
    nw = len(shards)

    def body(*refs):
        s_refs, g_refs = refs[:nw], refs[2 * nw:3 * nw]
        send_sems, recv_sems, local_sems = refs[3 * nw:]
        x, y, c = _my_place()
        mine = [pltpu.make_async_copy(s_refs[w], _block_rows(g_refs[w], w, (x, y, c)), local_sems.at[w])
                for w in range(nw)]
        for cp in mine:
            cp.start()
        sends, arrivals = [], []
        for j, chip in enumerate(_other_chips(x, y)):
            for w in range(nw):
                rows_out = _block_rows(g_refs[w], w, (*chip, c))
                rows_in = _block_rows(g_refs[w], w, (*chip, 1 - c))
                sends.append(pltpu.make_async_remote_copy(
                    src_ref=rows_out, dst_ref=rows_out, send_sem=send_sems.at[j, w], recv_sem=recv_sems.at[j, w],
                    device_id=(x, y, 1 - c), device_id_type=MESH))
                arrivals.append(pltpu.make_async_remote_copy(
                    src_ref=rows_in, dst_ref=rows_in, send_sem=send_sems.at[j, w], recv_sem=recv_sems.at[j, w],
                    device_id=(x, y, 1 - c), device_id_type=MESH))
        for cp in sends:
            cp.start()
        for cp in arrivals:
            cp.wait_recv()
        for cp in sends:
            cp.wait_send()
        for cp in mine:
            cp.wait()

    return pl.pallas_call(
        body, name="ag_pass_on",
        in_specs=[ANY] * (2 * nw), out_specs=[ANY] * nw,
        out_shape=[jax.ShapeDtypeStruct(a.shape, a.dtype) for a in lands],
        input_output_aliases={nw + i: i for i in range(nw)},
        scratch_shapes=[pltpu.SemaphoreType.DMA((3, nw)), pltpu.SemaphoreType.DMA((3, nw)),
                        pltpu.SemaphoreType.DMA((nw,))],
    )(*shards, *lands)


def _rs_sibling_exchange(parts):
    nw = len(parts)

    def body(*refs):
        p_refs, o_refs = refs[:nw], refs[nw:2 * nw]
        send_sems, recv_sems = refs[2 * nw:]
        x, y, c = _my_place()
        copies = [pltpu.make_async_remote_copy(
            src_ref=p_refs[w].at[:, 1 - c], dst_ref=o_refs[w],
            send_sem=send_sems.at[w], recv_sem=recv_sems.at[w], device_id=(x, y, 1 - c), device_id_type=MESH)
            for w in range(nw)]
        for cp in copies:
            cp.start()
        for cp in copies:
            cp.wait()

    return pl.pallas_call(
        body, name="rs_sibling_exchange",
        in_specs=[ANY] * nw, out_specs=[ANY] * nw,
        out_shape=[jax.ShapeDtypeStruct(p.shape[:1] + p.shape[2:], BF16) for p in parts],
        scratch_shapes=[pltpu.SemaphoreType.DMA((nw,)), pltpu.SemaphoreType.DMA((nw,))],
    )(*parts)


def _rs_chip_sum(part, got, c):
    nxy, _, r, _ = part.shape

    def body(c_ref, p_ref, g_ref, o_ref):
        o_ref[...] = (p_ref[...].astype(F32) + g_ref[...].astype(F32)).astype(o_ref.dtype)

    return pl.pallas_call(
        body, name="rs_chip_sum",
        grid_spec=pltpu.PrefetchScalarGridSpec(
            num_scalar_prefetch=1, grid=(nxy,),
            in_specs=[pl.BlockSpec((None, None, r, D_MODEL), lambda q, c_ref: (q, c_ref[0], 0, 0)),
                      pl.BlockSpec((None, r, D_MODEL), lambda q, c_ref: (q, 0, 0))],
            out_specs=pl.BlockSpec((None, r, D_MODEL), lambda q, c_ref: (q, 0, 0))),
        out_shape=jax.ShapeDtypeStruct(got.shape, BF16),
        compiler_params=_cparams(("parallel",)),
    )(c, part, got)


def _other_chips(x, y):
    return [(1 - x, y), (x, 1 - y), (1 - x, 1 - y)]


def _rs_chip_copies(s_refs, land_refs, send_sems, recv_sems):
    x, y, c = _my_place()
    copies = []
    for k, chip in enumerate(_other_chips(x, y)):
        q = 2 * chip[0] + chip[1]
        copies += [pltpu.make_async_remote_copy(
            src_ref=s_refs[w].at[q], dst_ref=land_refs[w].at[k],
            send_sem=send_sems.at[k * len(s_refs) + w], recv_sem=recv_sems.at[k * len(s_refs) + w],
            device_id=(*chip, c), device_id_type=MESH)
            for w in range(len(s_refs))]
    return copies


def _rs_chip_start(sums, layer):
    nw = len(sums)

    def body(*refs):
        s_refs, land_refs = refs[:nw], refs[nw:2 * nw]
        send_sems, recv_sems = refs[2 * nw], refs[2 * nw + 1]
        token = refs[-1]
        for cp in _rs_chip_copies(s_refs, land_refs, send_sems, recv_sems):
            cp.start()
        token[...] = jnp.zeros_like(token)

    lands = [lax.empty((3,) + s.shape[1:], BF16) for s in sums]
    out = pl.pallas_call(
        body, name="rs_chip_start_%d" % layer,
        in_specs=[HBM] * (2 * nw),
        out_specs=(SEM, SEM, *[HBM] * (2 * nw), pl.BlockSpec(memory_space=pltpu.VMEM)),
        out_shape=(pltpu.SemaphoreType.DMA((3 * nw,)), pltpu.SemaphoreType.DMA((3 * nw,)),
                   *[pltpu.HBM(a.shape, a.dtype) for a in list(sums) + lands],
                   jax.ShapeDtypeStruct((8, LANES), F32)),
        input_output_aliases={i: 2 + i for i in range(2 * nw)},
        compiler_params=pltpu.CompilerParams(has_side_effects=EFFECT),
    )(*[_in_hbm(a) for a in list(sums) + lands])
    return out[0], out[1], out[2:2 + nw], out[2 + nw:2 + 2 * nw], out[-1]


def _rs_chip_wait(send_sems, recv_sems, sums, lands, after, layer):
    nw = len(sums)

    def body(*refs):
        s_refs, land_refs = refs[:nw], refs[nw:2 * nw]
        for cp in _rs_chip_copies(s_refs, land_refs, refs[2 * nw], refs[2 * nw + 1]):
            cp.wait_send()
            cp.wait_recv()

    out = pl.pallas_call(
        body, name="rs_chip_wait_%d" % layer,
        in_specs=[HBM] * (2 * nw) + [SEM, SEM] + [ANY] * len(after),
        out_specs=[HBM] * (2 * nw),
        out_shape=[pltpu.HBM(a.shape, a.dtype) for a in list(sums) + list(lands)],
        input_output_aliases={i: i for i in range(2 * nw)},
        compiler_params=pltpu.CompilerParams(has_side_effects=EFFECT),
    )(*sums, *lands, send_sems, recv_sems, *after)
    return out[:nw], out[nw:]


def _rs_finish(sums, got, q):
    _, r, _ = sums.shape

    def body(q_ref, s_ref, g_ref, o_ref):
        o_ref[...] = ((s_ref[...].astype(F32) + g_ref[0].astype(F32)) + g_ref[1].astype(F32)) + g_ref[2].astype(F32)

    return pl.pallas_call(
        body, name="rs_finish",
        grid_spec=pltpu.PrefetchScalarGridSpec(
            num_scalar_prefetch=1, grid=(1,),
            in_specs=[pl.BlockSpec((None, r, D_MODEL), lambda i, q_ref: (q_ref[0], 0, 0)),
                      pl.BlockSpec((3, r, D_MODEL), lambda i, q_ref: (0, 0, 0))],
            out_specs=pl.BlockSpec((r, D_MODEL), lambda i, q_ref: (0, 0))),
        out_shape=jax.ShapeDtypeStruct((r, D_MODEL), F32),
        compiler_params=_cparams(("arbitrary",)),
    )(q, sums, got)


def _allreduce_small(vec):
    R = vec.shape[0]
    assert R % (8 * N_DEV) == 0
    P = R // N_DEV

    def body(v_ref, o_ref, buf, send1, recv1, send2, recv2):
        x, y, c = _my_place()
        me = 4 * x + 2 * y + c

        def piece(ref, d):
            return ref.at[pl.ds(pl.multiple_of(d * P, 8), P), :]

        def peer(k):
            p = me ^ k
            return p, (p >> 2, (p >> 1) & 1, p & 1)

        scatter = []
        for k in range(1, N_DEV):
            p, where = peer(k)
            scatter.append(pltpu.make_async_remote_copy(
                src_ref=piece(v_ref, p), dst_ref=buf.at[k], send_sem=send1.at[k - 1], recv_sem=recv1.at[k - 1],
                device_id=where, device_id_type=MESH))
        for cp in scatter:
            cp.start()
        buf[0] = piece(v_ref, me)[...]
        for cp in scatter:
            cp.wait()
        acc = buf[me]
        for d in range(1, N_DEV):
            acc = acc + buf[me ^ d]
        piece(o_ref, me)[...] = acc
        spread, arrivals = [], []
        for k in range(1, N_DEV):
            p, where = peer(k)
            spread.append(pltpu.make_async_remote_copy(
                src_ref=piece(o_ref, me), dst_ref=piece(o_ref, me), send_sem=send2.at[k - 1], recv_sem=recv2.at[k - 1],
                device_id=where, device_id_type=MESH))
            arrivals.append(pltpu.make_async_remote_copy(
                src_ref=piece(o_ref, p), dst_ref=piece(o_ref, p), send_sem=send2.at[k - 1], recv_sem=recv2.at[k - 1],
                device_id=where, device_id_type=MESH))
        for cp in spread:
            cp.start()
        for cp in arrivals:
            cp.wait_recv()
        for cp in spread:
            cp.wait_send()

    sems = pltpu.SemaphoreType.DMA((N_DEV - 1,))
    return pl.pallas_call(
        body, name="allreduce_small",
        in_specs=[pl.BlockSpec(memory_space=pltpu.VMEM)], out_specs=pl.BlockSpec(memory_space=pltpu.VMEM),
        out_shape=jax.ShapeDtypeStruct((R, LANES), F32),
        scratch_shapes=[pltpu.VMEM((N_DEV, P, LANES), F32), sems, sems, sems, sems],
        compiler_params=_cparams(),
    )(vec)


def _pack(arrs):
    flat = jnp.concatenate([a.reshape(-1) for a in arrs])
    pad = (-flat.shape[0]) % (8 * N_DEV * LANES)
    return jnp.pad(flat, (0, pad)).reshape(-1, LANES)


def _unpack(packed, shapes):
    flat = packed.reshape(-1)
    out, off = [], 0
    for s in shapes:
        n = math.prod(s)
        out.append(flat[off:off + n].reshape(s))
        off += n
    return out


def kernel(x, w_in, w_conv, w_pool, pool_scale, sgu_ln_g, w_spatial, b_spatial, w_o, ln1_g, ln1_b, w_gate_up, w_down, ln2_g, ln2_b, loss_target, m_w_in, m_w_conv, m_w_pool, m_pool_scale, m_sgu_ln_g, m_w_spatial, m_b_spatial, m_w_o, m_ln1_g, m_ln1_b, m_w_gate_up, m_w_down, m_ln2_g, m_ln2_b, v_w_in, v_w_conv, v_w_pool, v_pool_scale, v_sgu_ln_g, v_w_spatial, v_b_spatial, v_w_o, v_ln1_g, v_ln1_b, v_w_gate_up, v_w_down, v_ln2_g, v_ln2_b):
    L = DEPTH
    T = x.shape[1]
    mx, my, mc = _my_place()
    dev = 4 * mx + 2 * my + mc
    xs = x[0]
    target = loss_target[0]

    shards = (jnp.swapaxes(w_in, 1, 2).astype(BF16), jnp.swapaxes(w_gate_up, 1, 2).astype(BF16),
              w_o.astype(BF16), w_down.astype(BF16))
    first_gather = _ag_start([s[0] for s in shards], 0)

    conv_cols = w_conv.shape[2]
    w_conv_z = lax.dynamic_update_slice(jnp.zeros((L, 3, CONV_W), F32), w_conv, (0, 0, dev * conv_cols))
    w_conv_full = _allreduce_small(_pack([w_conv_z]))
    w_conv_full = _unpack(w_conv_full, [(L, 3, CONV_W)])[0]

    loss_tile, grad_x2, big_grads, small_grads = _local_step(
        xs, target, shards, first_gather, w_conv_full, w_pool, pool_scale, sgu_ln_g, w_spatial, b_spatial,
        ln1_g, ln1_b, ln2_g, ln2_b)
    loss = lax.psum(loss_tile[0, 0], ("x", "y", "c"))
    grad_x = grad_x2[None]
    big_w = (w_in, w_gate_up, w_o, w_down)
    big_m = (m_w_in, m_w_gate_up, m_w_o, m_w_down)
    big_v = (v_w_in, v_w_gate_up, v_w_o, v_w_down)
    small_w = [w_conv_full, w_pool, pool_scale, sgu_ln_g, w_spatial, b_spatial, ln1_g, ln1_b, ln2_g, ln2_b]
    small_m = [m_w_conv, m_w_pool, m_pool_scale, m_sgu_ln_g, m_w_spatial, m_b_spatial, m_ln1_g, m_ln1_b, m_ln2_g, m_ln2_b]
    small_v = [v_w_conv, v_w_pool, v_pool_scale, v_sgu_ln_g, v_w_spatial, v_b_spatial, v_ln1_g, v_ln1_b, v_ln2_g, v_ln2_b]
    grads, deltas, new_m, new_v = _reduce_and_update(
        big_grads, small_grads, big_w, big_m, big_v, small_w, small_m, small_v)
    return (loss, grad_x, *grads, *deltas, *new_m, *new_v)


def _local_step(xs, target, shards, gather, w_conv_full, w_pool, pool_scale, sgu_ln_g, w_spatial, b_spatial,
                ln1_g, ln1_b, ln2_g, ln2_b):
    L = DEPTH
    T = xs.shape[0]
    mx, my, mc = _my_place()
    c_arr = jnp.reshape(mc, (1,)).astype(jnp.int32)
    q_arr = jnp.reshape(2 * mx + my, (1,)).astype(jnp.int32)
    eye2 = jnp.eye(2, dtype=F32)
    wp = w_pool.reshape(L, 2, 2, HALF, HALF)
    wpool_bd = jnp.einsum("ltgcd,gh->ltgchd", wp, eye2).reshape(L, 2, LANES, LANES)
    wsp_t = w_spatial.reshape(L, 3, 2 * CHUNK, CHUNK)
    bias_t = jnp.repeat(jnp.swapaxes(b_spatial.reshape(L, 3, 2, CHUNK), 2, 3), HALF, axis=3)
    ones = jnp.ones((1, D_MODEL), F32)
    zeros = jnp.zeros((1, D_MODEL), F32)

    saved = []
    prev, pg, pb = xs, ones, zeros
    prev_b = xs.astype(BF16)
    weights = []
    for l in range(L):
        send_sems, recv_sems, shards_l, lands_l, _ = gather
        shards_l, lands_l = _ag_wait(send_sems, recv_sems, shards_l, lands_l, [] if l == 0 else [prev_b], l)
        g_in, g_gu, g_o, g_dn = _ag_pass_on(shards_l, lands_l)
        weights.append((g_in, g_gu, g_o, g_dn))
        deps = []
        if l + 1 < L:
            gather = _ag_start([s[l + 1] for s in shards], l + 1, after=[g_in])
            deps = [gather[4]]
        proj = _mm(prev_b, g_in, "nt", F32, 512, IN_W, D_MODEL, "mm_proj", deps=deps)
        mixcat = _mixer_fwd(proj, w_conv_full[l], wpool_bd[l], pool_scale[l][None], sgu_ln_g[l][None], wsp_t[l], bias_t[l])
        mix = _mm(mixcat, g_o, "nn", F32, T, 512, D_MODEL, "mm_wo")
        xhat1, rstd1, h_b = _ln_fwd(prev, pg, pb, mix, ln1_g[l][None], ln1_b[l][None])
        gu = _mm(h_b, g_gu, "nt", F32, T, 512, D_MODEL, "mm_gate_up")
        act = _swiglu_fwd(gu)
        ff = _mm(act, g_dn, "nn", F32, T, 256, D_FF, "mm_down")
        xhat2, rstd2, y_b = _ln_fwd(xhat1, ln1_g[l][None], ln1_b[l][None], ff, ln2_g[l][None], ln2_b[l][None])
        saved.append((prev_b, proj, mixcat, xhat1, rstd1, h_b, gu, xhat2, rstd2))
        prev, pg, pb, prev_b = xhat2, ln2_g[l][None], ln2_b[l][None], y_b

    loss_tile, dy = _loss_head(prev, pg, pb, target)

    small = [None] * L
    big = [None] * L
    dres, dmm = None, dy
    in_flight = None
    for l in reversed(range(L)):
        prev_b, proj, mixcat, xhat1, rstd1, h_b, gu, xhat2, rstd2 = saved[l]
        g_in, g_gu, g_o, g_dn = weights[l]
        deps = [in_flight[4]] if in_flight is not None else []
        dr2, dr2_b, dg2, db2 = _ln_bwd(dres, dmm, xhat2, rstd2, ln2_g[l][None], deps=deps)
        dact = _mm(dr2_b, g_dn, "nt", F32, T, 256, D_MODEL, "mm_dact")
        dgu, act = _swiglu_bwd(gu, dact)
        p_dn = _mm(act, dr2_b, "tn", BF16, 256, D_MODEL, T, "mm_dw_down")
        p_gu = _mm(dgu, h_b, "tn", BF16, 512, D_MODEL, T, "mm_dw_gate_up")
        dh = _mm(dgu, g_gu, "nn", F32, T, 512, SW_TC, "mm_dh")
        dr1, dr1_b, dg1, db1 = _ln_bwd(dr2, dh, xhat1, rstd1, ln1_g[l][None])
        dmix = _mm(dr1_b, g_o, "nt", F32, T, 512, D_MODEL, "mm_dmix")
        p_o = _mm(mixcat, dr1_b, "tn", BF16, 512, D_MODEL, T, "mm_dw_o")
        dproj, dwc, dwp, dps, dlng, dwsp, dbias = _mixer_bwd(
            proj, dmix, w_conv_full[l], wpool_bd[l], pool_scale[l][None], sgu_ln_g[l][None], wsp_t[l], bias_t[l])
        p_in = _mm(dproj, prev_b, "tn", BF16, IN_W, D_MODEL, T, "mm_dw_in")
        dx = _mm(dproj, g_in, "nn", F32, T, 512, IN_W, "mm_dx")
        small[l] = (dwc, dwp, dps, dlng, dwsp, dbias, dg1, db1, dg2, db2)
        dres, dmm = dr1, dx
        if in_flight is not None:
            big[l + 1] = _rs_chip_finish(in_flight, [dx], q_arr, l + 1)
        parts = [p.reshape(4, 2, r, D_MODEL) for p, r in zip((p_in, p_gu, p_o, p_dn), SHARD_ROWS)]
        got1 = _rs_sibling_exchange(parts)
        sums = [_rs_chip_sum(p, g, c_arr) for p, g in zip(parts, got1)]
        in_flight = _rs_chip_start(sums, l)
    big[0] = _rs_chip_finish(in_flight, [], q_arr, 0)
    grad_x = _residual_out(dres, dmm)
    big_grads = [jnp.stack([big[l][w] for l in range(L)]) for w in range(4)]

    def stack(i):
        return jnp.stack([small[l][i] for l in range(L)])

    dwp_bd = stack(1).reshape(L, 2, 2, HALF, 2, HALF)
    dwp_all = jnp.einsum("ltgchd,gh->ltgcd", dwp_bd, eye2).reshape(L, 4, HALF, HALF)
    dbs_all = jnp.swapaxes(stack(5)[:, :, :, :2], 2, 3).reshape(L, 6, CHUNK)
    small_grads = [stack(0), dwp_all, stack(2).reshape(L, POOL_W), stack(3).reshape(L, SGU_W),
                   stack(4).reshape(L, 6, CHUNK, CHUNK), dbs_all] + [stack(i).reshape(L, D_MODEL) for i in (6, 7, 8, 9)]
    return loss_tile, grad_x, big_grads, small_grads


def _rs_chip_finish(in_flight, after, q, layer):
    send_sems, recv_sems, sums, lands, _ = in_flight
    sums, got = _rs_chip_wait(send_sems, recv_sems, sums, lands, after, layer)
    return [_rs_finish(s, g, q) for s, g in zip(sums, got)]


def _reduce_and_update(big_grads, small_grads, big_w, big_m, big_v, small_w, small_m, small_v):
    L = DEPTH
    mx, my, mc = _my_place()
    dev = 4 * mx + 2 * my + mc
    conv_cols = CONV_W // N_DEV
    w_in, w_gate_up, w_o, w_down = big_w
    m_w_in, m_w_gate_up, m_w_o, m_w_down = big_m
    v_w_in, v_w_gate_up, v_w_o, v_w_down = big_v
    gt_in, gt_gu, g_w_o, g_w_dn = big_grads
    g_w_in = jnp.swapaxes(gt_in, 1, 2)
    g_w_gu = jnp.swapaxes(gt_gu, 1, 2)

    small_shapes = [a.shape for a in small_grads]
    packed_g = _allreduce_small(_pack(small_grads))

    def widen_conv(a):
        return lax.dynamic_update_slice(jnp.zeros((L, 3, CONV_W), F32), a, (0, 0, dev * conv_cols))

    small_m = [widen_conv(small_m[0])] + list(small_m[1:])
    small_v = [widen_conv(small_v[0])] + list(small_v[1:])
    pk_d, pk_m, pk_v = _adamw(_pack(small_w), packed_g, _pack(small_m), _pack(small_v), packed_g.shape[0] // 2)
    sg = _unpack(packed_g, small_shapes)
    sd = _unpack(pk_d, small_shapes)
    sm = _unpack(pk_m, small_shapes)
    sv = _unpack(pk_v, small_shapes)

    def conv_cols_of(a):
        return lax.dynamic_slice(a, (0, 0, dev * conv_cols), (L, 3, conv_cols))

    for lst in (sg, sd, sm, sv):
        lst[0] = conv_cols_of(lst[0])

    def big(w, g, m, v, tr):
        s = w.shape
        d, mn, vn = _adamw(w.reshape(-1, s[-1]), g.reshape(-1, s[-1]), m.reshape(-1, s[-1]), v.reshape(-1, s[-1]), tr)
        return d.reshape(s), mn.reshape(s), vn.reshape(s)

    d_in, m_in, v_in = big(w_in, g_w_in, m_w_in, v_w_in, 512)
    d_gu, m_gu, v_gu = big(w_gate_up, g_w_gu, m_w_gate_up, v_w_gate_up, 512)
    d_o, m_o, v_o = big(w_o, g_w_o, m_w_o, v_w_o, 128)
    d_dn, m_dn, v_dn = big(w_down, g_w_dn, m_w_down, v_w_down, 352)

    def ordered(big_in, big_o, big_gu, big_dn, sm_list):
        return [big_in, sm_list[0], sm_list[1], sm_list[2], sm_list[3], sm_list[4], sm_list[5], big_o,
                sm_list[6], sm_list[7], big_gu, big_dn, sm_list[8], sm_list[9]]

    grads = ordered(g_w_in, g_w_o, g_w_gu, g_w_dn, sg)
    deltas = ordered(d_in, d_o, d_gu, d_dn, sd)
    new_m = ordered(m_in, m_o, m_gu, m_dn, sm)
    new_v = ordered(v_in, v_o, v_gu, v_dn, sv)
    return grads, deltas, new_m, new_v
```

```python
import functools
import math

import jax
import jax.numpy as jnp
from jax import lax
from jax.experimental import pallas as pl
from jax.experimental.pallas import tpu as pltpu

F32 = jnp.float32
BF16 = jnp.bfloat16
MESH = pl.DeviceIdType.MESH

D_MODEL = 1024
DEPTH = 4
CONV_W = 384
POOL_W = 256
SGU_W = 384
IN_W = 3 * CONV_W + POOL_W + 2 * SGU_W
D_FF = 2816
CHUNK = 128
ALPHA = float((2 * DEPTH) ** 0.25)
LN_EPS = 1e-5
ADAM_LR, ADAM_B1, ADAM_B2, ADAM_EPS, ADAM_WD, ADAM_STEP = 0.001, 0.9, 0.999, 1e-08, 0.01, 10

N_DEV = 8
LANES = 128
HALF = 64
SHARD_ROWS = (IN_W // N_DEV, 2 * D_FF // N_DEV, D_MODEL // N_DEV, D_FF // N_DEV)
VMEM_LIMIT = 52 * 1024 * 1024

INV_SQRT2 = 0.7071067811865476
INV_SQRT_2PI = 0.3989422804014327


def _cparams(sem=None, **kw):
    if sem is not None:
        kw["dimension_semantics"] = sem
    return pltpu.CompilerParams(vmem_limit_bytes=VMEM_LIMIT, **kw)


_DN = {"nn": (((1,), (0,)), ((), ())), "nt": (((1,), (1,)), ((), ())), "tn": (((0,), (0,)), ((), ()))}


def _mm(a, b, mode, out_dtype, tm, tn, tk, name, deps=()):
    if mode == "nn":
        (M, K), N = a.shape, b.shape[1]
    elif mode == "nt":
        (M, K), N = a.shape, b.shape[0]
    else:
        (K, M), N = a.shape, b.shape[1]
    assert M % tm == 0 and N % tn == 0 and K % tk == 0, (M, N, K, tm, tn, tk)
    nk = K // tk
    nd = len(deps)

    def body(*refs):
        a_ref, b_ref, o_ref = refs[0], refs[1], refs[2 + nd]
        acc_ref = refs[3 + nd] if nk > 1 else None
        p = lax.dot_general(a_ref[...], b_ref[...], _DN[mode], preferred_element_type=F32)
        if nk == 1:
            o_ref[...] = p.astype(o_ref.dtype)
        else:
            k = pl.program_id(2)

            @pl.when(k == 0)
            def _():
                acc_ref[...] = p

            @pl.when(k > 0)
            def _():
                acc_ref[...] += p

            @pl.when(k == nk - 1)
            def _():
                o_ref[...] = acc_ref[...].astype(o_ref.dtype)

    if mode == "nn":
        a_spec = pl.BlockSpec((tm, tk), lambda i, j, k: (i, k))
        b_blk, b_idx = (tk, tn), (lambda i, j, k: (k, j))
    elif mode == "nt":
        a_spec = pl.BlockSpec((tm, tk), lambda i, j, k: (i, k))
        b_blk, b_idx = (tn, tk), (lambda i, j, k: (j, k))
    else:
        a_spec = pl.BlockSpec((tk, tm), lambda i, j, k: (k, i))
        b_blk, b_idx = (tk, tn), (lambda i, j, k: (k, j))
    return pl.pallas_call(
        body,
        name=name,
        grid=(M // tm, N // tn, nk),
        in_specs=[a_spec, pl.BlockSpec(b_blk, b_idx)] + [pl.BlockSpec(memory_space=pl.ANY)] * nd,
        out_specs=pl.BlockSpec((tm, tn), lambda i, j, k: (i, j)),
        out_shape=jax.ShapeDtypeStruct((M, N), out_dtype),
        scratch_shapes=[pltpu.VMEM((tm, tn), F32)] if nk > 1 else [],
        compiler_params=_cparams(("parallel", "parallel", "arbitrary")),
    )(a, b, *deps)


def _gelu(x):
    return 0.5 * x * (1.0 + lax.erf(x * INV_SQRT2))


def _gelu_grad(x):
    return 0.5 * (1.0 + lax.erf(x * INV_SQRT2)) + x * (jnp.exp(-0.5 * x * x) * INV_SQRT_2PI)


def _shift_down(z, k):
    row = lax.broadcasted_iota(jnp.int32, z.shape, 0)
    return jnp.where(row >= k, pltpu.roll(z, k, 0), 0.0)


def _shift_up(z, k):
    n = z.shape[0]
    row = lax.broadcasted_iota(jnp.int32, z.shape, 0)
    return jnp.where(row < n - k, pltpu.roll(z, n - k, 0), 0.0)


def _lo_mask(shape):
    return lax.broadcasted_iota(jnp.int32, shape, len(shape) - 1) < HALF


def _seg_mean(x, lo):
    a = jnp.sum(jnp.where(lo, x, 0.0), axis=-1, keepdims=True)
    b = jnp.sum(jnp.where(lo, 0.0, x), axis=-1, keepdims=True)
    return jnp.where(lo, a, b) * (1.0 / HALF)


def _pool_windows(first):
    lo = _lo_mask((1, LANES))
    return jnp.where(first, jnp.where(lo, 2.0, 4.0), jnp.where(lo, 8.0, 16.0)), lo


def _pool_mean_minus_token(p, first):
    wl, lo = _pool_windows(first)
    s2 = p + _shift_down(p, 1)
    s4 = s2 + _shift_down(s2, 2)
    s8 = s4 + _shift_down(s4, 4)
    s16 = s8 + _shift_down(s8, 8)
    win = jnp.where(first, jnp.where(lo, s2, s4), jnp.where(lo, s8, s16))
    t1 = (lax.broadcasted_iota(jnp.int32, p.shape, 0) + 1).astype(F32)
    count = jnp.minimum(t1, wl)
    return win / count - p, count


def _tril_keep():
    r = lax.broadcasted_iota(jnp.int32, (2 * CHUNK, CHUNK), 0)
    s = lax.broadcasted_iota(jnp.int32, (2 * CHUNK, CHUNK), 1)
    return s <= (r & (CHUNK - 1))


def _sgu_chunk_fwd(u, v, g, wm, bias, lo):
    ug = _gelu(u)
    vg = _gelu(v)
    mu = _seg_mean(vg, lo)
    xc = vg - mu
    var = _seg_mean(xc * xc, lo)
    rstd = lax.rsqrt(var + LN_EPS)
    vn = xc * rstd
    vh = (vn * g).astype(BF16)
    mm2 = jnp.dot(wm, vh, preferred_element_type=F32)
    mixed = jnp.where(lo, mm2[:CHUNK], mm2[CHUNK:]) + bias
    return ug, vn, rstd, vh, mixed


def _mixer_fwd(proj, wconv, wpool_bd, pscale, lng, wsp, bias):
    T = proj.shape[0]
    nchunk = T // CHUNK

    def body(a_ref, b_ref, c_ref, wc_ref, wp_ref, ps_ref, lng_ref, wsp_ref, bias_ref, o_ref):
        j = pl.program_id(0)

        @pl.when(j < 3)
        def _conv():
            z = c_ref[...] * a_ref[...]
            w = wc_ref[...]
            y = w[0:1] * _shift_down(z, 2) + w[1:2] * _shift_down(z, 1) + w[2:3] * z
            o_ref[...] = (b_ref[...] * y).astype(o_ref.dtype)

        @pl.when((j >= 3) & (j < 5))
        def _pool():
            d, _ = _pool_mean_minus_token(a_ref[...], j == 3)
            y = jnp.dot(d.astype(BF16), wp_ref[...].astype(BF16), preferred_element_type=F32)
            o_ref[...] = (y * ps_ref[...]).astype(o_ref.dtype)

        @pl.when(j >= 5)
        def _sgu():
            lo = _lo_mask((CHUNK, LANES))
            wm = jnp.where(_tril_keep(), wsp_ref[...], 0.0).astype(BF16)
            bias_t = bias_ref[...]
            g = lng_ref[...]

            def chunk(n, carry):
                rows = pl.ds(pl.multiple_of(n * CHUNK, CHUNK), CHUNK)
                ug, _, _, _, mixed = _sgu_chunk_fwd(a_ref[rows, :], b_ref[rows, :], g, wm, bias_t, lo)
                o_ref[rows, :] = (ug * mixed).astype(o_ref.dtype)
                return carry

            lax.fori_loop(0, nchunk, chunk, 0)

    def col(f):
        return lambda j: (0, f(j))

    clip = lambda v, lo, hi: jnp.minimum(jnp.maximum(v, lo), hi)
    return pl.pallas_call(
        body,
        name="mixer_fwd",
        grid=(8,),
        in_specs=[
            pl.BlockSpec((T, LANES), col(lambda j: jnp.where(j < 3, j, jnp.where(j < 5, j + 6, j + 6)))),
            pl.BlockSpec((T, LANES), col(lambda j: jnp.where(j < 3, j + 3, jnp.where(j < 5, 5, j + 9)))),
            pl.BlockSpec((T, LANES), col(lambda j: jnp.where(j < 3, j + 6, 8))),
            pl.BlockSpec((3, LANES), col(lambda j: clip(j, 0, 2))),
            pl.BlockSpec((None, LANES, LANES), lambda j: (clip(j - 3, 0, 1), 0, 0)),
            pl.BlockSpec((1, LANES), col(lambda j: clip(j - 3, 0, 1))),
            pl.BlockSpec((1, LANES), col(lambda j: clip(j - 5, 0, 2))),
            pl.BlockSpec((None, 2 * CHUNK, CHUNK), lambda j: (clip(j - 5, 0, 2), 0, 0)),
            pl.BlockSpec((None, CHUNK, LANES), lambda j: (clip(j - 5, 0, 2), 0, 0)),
        ],
        out_specs=pl.BlockSpec((T, LANES), lambda j: (0, j)),
        out_shape=jax.ShapeDtypeStruct((T, D_MODEL), BF16),
        compiler_params=_cparams(("arbitrary",)),
    )(proj, proj, proj, wconv, wpool_bd, pscale, lng, wsp, bias)


def _mixer_bwd(proj, dmix, wconv, wpool_bd, pscale, lng, wsp, bias):
    T = proj.shape[0]
    nchunk = T // CHUNK

    def body(a_ref, b_ref, c_ref, dm_ref, wc_ref, wp_ref, ps_ref, lng_ref, wsp_ref, bias_ref,
             o_ref, dwc_ref, dwp_ref, dps_ref, dlng_ref, dwsp_ref, dbias_ref, keep1, keep2):
        k = pl.program_id(0)

        @pl.when(k < 3)
        def _conv():
            xa, gb, gc, dya = a_ref[...], b_ref[...], c_ref[...], dm_ref[...]
            w = wc_ref[...]
            z = gc * xa
            z1 = _shift_down(z, 1)
            z2 = _shift_down(z, 2)
            y = w[0:1] * z2 + w[1:2] * z1 + w[2:3] * z
            dyv = dya * gb
            dz = w[2:3] * dyv + w[1:2] * _shift_up(dyv, 1) + w[0:1] * _shift_up(dyv, 2)
            dwc_ref[0:1, :] = jnp.sum(dyv * z2, axis=0, keepdims=True)
            dwc_ref[1:2, :] = jnp.sum(dyv * z1, axis=0, keepdims=True)
            dwc_ref[2:3, :] = jnp.sum(dyv * z, axis=0, keepdims=True)
            o_ref[...] = (dz * gc).astype(o_ref.dtype)
            keep1[k] = (dya * y).astype(keep1.dtype)
            keep1[k + 3] = (dz * xa).astype(keep1.dtype)

        @pl.when((k >= 3) & (k < 9))
        def _emit_gb_gc():
            o_ref[...] = keep1[k - 3]

        @pl.when((k >= 9) & (k < 11))
        def _pool():
            first = k == 9
            p, dyb = a_ref[...], dm_ref[...]
            d, count = _pool_mean_minus_token(p, first)
            w2 = wp_ref[...].astype(BF16)
            db = d.astype(BF16)
            y = jnp.dot(db, w2, preferred_element_type=F32)
            dps_ref[...] = jnp.sum(dyb * y, axis=0, keepdims=True)
            dyv = (dyb * ps_ref[...]).astype(BF16)
            dd = lax.dot_general(dyv, w2, _DN["nt"], preferred_element_type=F32)
            dwp_ref[...] = lax.dot_general(db, dyv, _DN["tn"], preferred_element_type=F32)
            dwin = dd / count
            a2 = dwin + _shift_up(dwin, 1)
            a4 = a2 + _shift_up(a2, 2)
            a8 = a4 + _shift_up(a4, 4)
            a16 = a8 + _shift_up(a8, 8)
            _, lo = _pool_windows(first)
            back = jnp.where(first, jnp.where(lo, a2, a4), jnp.where(lo, a8, a16))
            o_ref[...] = (back - dd).astype(o_ref.dtype)

        @pl.when((k >= 11) & (k < 14))
        def _sgu():
            lo = _lo_mask((CHUNK, LANES))
            keep = _tril_keep()
            wm = jnp.where(keep, wsp_ref[...], 0.0).astype(BF16)
            bias_t = bias_ref[...]
            g = lng_ref[...]
            dwsp_ref[...] = jnp.zeros_like(dwsp_ref)
            dbias_ref[...] = jnp.zeros_like(dbias_ref)
            dlng_ref[...] = jnp.zeros_like(dlng_ref)

            def chunk(n, carry):
                rows = pl.ds(pl.multiple_of(n * CHUNK, CHUNK), CHUNK)
                u, v, dyc = a_ref[rows, :], b_ref[rows, :], dm_ref[rows, :]
                ug, vn, rstd, vh, mixed = _sgu_chunk_fwd(u, v, g, wm, bias_t, lo)
                dmx = dyc * ug
                o_ref[rows, :] = (dyc * mixed * _gelu_grad(u)).astype(o_ref.dtype)
                dbias_ref[...] += dmx
                dst = jnp.concatenate([jnp.where(lo, dmx, 0.0), jnp.where(lo, 0.0, dmx)], axis=0).astype(BF16)
                dwsp_ref[...] += lax.dot_general(dst, vh, _DN["nt"], preferred_element_type=F32)
                dvh = lax.dot_general(wm, dst, _DN["tn"], preferred_element_type=F32)
                dlng_ref[...] += jnp.sum(dvh * vn, axis=0, keepdims=True)
                dvn = dvh * g
                m1 = _seg_mean(dvn, lo)
                m2 = _seg_mean(dvn * vn, lo)
                dvg = rstd * (dvn - m1 - vn * m2)
                keep2[k - 11, rows, :] = (dvg * _gelu_grad(v)).astype(keep2.dtype)
                return carry

            lax.fori_loop(0, nchunk, chunk, 0)
            dwsp_ref[...] = jnp.where(keep, dwsp_ref[...], 0.0)
            dbt = dbias_ref[...]
            lane = lax.broadcasted_iota(jnp.int32, (CHUNK, LANES), 1)
            sa = jnp.sum(jnp.where(lo, dbt, 0.0), axis=-1, keepdims=True)
            sb = jnp.sum(jnp.where(lo, 0.0, dbt), axis=-1, keepdims=True)
            dbias_ref[...] = jnp.where(lane == 0, sa, jnp.where(lane == 1, sb, 0.0))

        @pl.when(k >= 14)
        def _emit_v():
            o_ref[...] = keep2[k - 14]

    def col(f):
        return lambda k: (0, f(k))

    clip = lambda v, lo, hi: jnp.minimum(jnp.maximum(v, lo), hi)
    view_a = lambda k: jnp.where(k < 3, k, jnp.where(k < 9, 2, jnp.where(k < 14, k, 13)))
    view_b = lambda k: jnp.where(k < 3, k + 3, jnp.where(k < 11, 5, jnp.where(k < 14, k + 3, 16)))
    view_c = lambda k: jnp.where(k < 3, k + 6, 8)
    view_dm = lambda k: jnp.where(k < 3, k, jnp.where(k < 9, 2, jnp.where(k < 14, k - 6, 7)))
    return pl.pallas_call(
        body,
        name="mixer_bwd",
        grid=(17,),
        in_specs=[
            pl.BlockSpec((T, LANES), col(view_a)),
            pl.BlockSpec((T, LANES), col(view_b)),
            pl.BlockSpec((T, LANES), col(view_c)),
            pl.BlockSpec((T, LANES), col(view_dm)),
            pl.BlockSpec((3, LANES), col(lambda k: clip(k, 0, 2))),
            pl.BlockSpec((None, LANES, LANES), lambda k: (clip(k - 9, 0, 1), 0, 0)),
            pl.BlockSpec((1, LANES), col(lambda k: clip(k - 9, 0, 1))),
            pl.BlockSpec((1, LANES), col(lambda k: clip(k - 11, 0, 2))),
            pl.BlockSpec((None, 2 * CHUNK, CHUNK), lambda k: (clip(k - 11, 0, 2), 0, 0)),
            pl.BlockSpec((None, CHUNK, LANES), lambda k: (clip(k - 11, 0, 2), 0, 0)),
        ],
        out_specs=[
            pl.BlockSpec((T, LANES), lambda k: (0, k)),
            pl.BlockSpec((3, LANES), col(lambda k: clip(k, 0, 2))),
            pl.BlockSpec((None, LANES, LANES), lambda k: (clip(k - 9, 0, 1), 0, 0)),
            pl.BlockSpec((1, LANES), col(lambda k: clip(k - 9, 0, 1))),
            pl.BlockSpec((1, LANES), col(lambda k: clip(k - 11, 0, 2))),
            pl.BlockSpec((None, 2 * CHUNK, CHUNK), lambda k: (clip(k - 11, 0, 2), 0, 0)),
            pl.BlockSpec((None, CHUNK, LANES), lambda k: (clip(k - 11, 0, 2), 0, 0)),
        ],
        out_shape=[
            jax.ShapeDtypeStruct((T, IN_W), BF16),
            jax.ShapeDtypeStruct((3, CONV_W), F32),
            jax.ShapeDtypeStruct((2, LANES, LANES), F32),
            jax.ShapeDtypeStruct((1, POOL_W), F32),
            jax.ShapeDtypeStruct((1, SGU_W), F32),
            jax.ShapeDtypeStruct((3, 2 * CHUNK, CHUNK), F32),
            jax.ShapeDtypeStruct((3, CHUNK, LANES), F32),
        ],
        scratch_shapes=[pltpu.VMEM((6, T, LANES), BF16), pltpu.VMEM((3, T, LANES), BF16)],
        compiler_params=_cparams(("arbitrary",)),
    )(proj, proj, proj, dmix, wconv, wpool_bd, pscale, lng, wsp, bias)


def _ln_fwd(prev, pg, pb, mmout, g, b, tm=256):
    T = prev.shape[0]

    def body(prev_ref, pg_ref, pb_ref, mm_ref, g_ref, b_ref, xhat_ref, rstd_ref, y_ref):
        r = ALPHA * (prev_ref[...] * pg_ref[...] + pb_ref[...]) + mm_ref[...]
        mu = jnp.mean(r, axis=-1, keepdims=True)
        xc = r - mu
        var = jnp.mean(xc * xc, axis=-1, keepdims=True)
        rstd = lax.rsqrt(var + LN_EPS)
        xhat = xc * rstd
        xhat_ref[...] = xhat
        rstd_ref[...] = rstd
        y_ref[...] = (xhat * g_ref[...] + b_ref[...]).astype(y_ref.dtype)

    row = pl.BlockSpec((tm, D_MODEL), lambda i: (i, 0))
    vec = pl.BlockSpec((1, D_MODEL), lambda i: (0, 0))
    return pl.pallas_call(
        body,
        name="ln_fwd",
        grid=(T // tm,),
        in_specs=[row, vec, vec, row, vec, vec],
        out_specs=[row, pl.BlockSpec((tm, 1), lambda i: (i, 0)), row],
        out_shape=[jax.ShapeDtypeStruct((T, D_MODEL), F32), jax.ShapeDtypeStruct((T, 1), F32),
                   jax.ShapeDtypeStruct((T, D_MODEL), BF16)],
        compiler_params=_cparams(("parallel",)),
    )(prev, pg, pb, mmout, g, b)


def _ln_bwd(dres, dmm, xhat, rstd, g, tm=256, deps=()):
    T = xhat.shape[0]
    has_res = dres is not None
    nd = len(deps)

    def body(*refs):
        refs = refs[:len(refs) - 4 - nd] + refs[len(refs) - 4:]
        if has_res:
            dres_ref, dmm_ref, xhat_ref, rstd_ref, g_ref, dr_ref, drb_ref, dg_ref, db_ref = refs
            dy = ALPHA * dres_ref[...] + dmm_ref[...]
        else:
            dmm_ref, xhat_ref, rstd_ref, g_ref, dr_ref, drb_ref, dg_ref, db_ref = refs
            dy = dmm_ref[...]
        xhat_v = xhat_ref[...]

        @pl.when(pl.program_id(0) == 0)
        def _():
            dg_ref[...] = jnp.zeros_like(dg_ref)
            db_ref[...] = jnp.zeros_like(db_ref)

        dg_ref[...] += jnp.sum(dy * xhat_v, axis=0, keepdims=True)
        db_ref[...] += jnp.sum(dy, axis=0, keepdims=True)
        dxh = dy * g_ref[...]
        m1 = jnp.mean(dxh, axis=-1, keepdims=True)
        m2 = jnp.mean(dxh * xhat_v, axis=-1, keepdims=True)
        dr = rstd_ref[...] * (dxh - m1 - xhat_v * m2)
        dr_ref[...] = dr
        drb_ref[...] = dr.astype(drb_ref.dtype)

    row = pl.BlockSpec((tm, D_MODEL), lambda i: (i, 0))
    vec = pl.BlockSpec((1, D_MODEL), lambda i: (0, 0))
    in_specs = ([row] if has_res else []) + [row, row, pl.BlockSpec((tm, 1), lambda i: (i, 0)), vec]
    in_specs += [pl.BlockSpec(memory_space=pl.ANY)] * nd
    args = ([dres] if has_res else []) + [dmm, xhat, rstd, g] + list(deps)
    return pl.pallas_call(
        body,
        name="ln_bwd_res" if has_res else "ln_bwd",
        grid=(T // tm,),
        in_specs=in_specs,
        out_specs=[row, row, vec, vec],
        out_shape=[jax.ShapeDtypeStruct((T, D_MODEL), F32), jax.ShapeDtypeStruct((T, D_MODEL), BF16),
                   jax.ShapeDtypeStruct((1, D_MODEL), F32), jax.ShapeDtypeStruct((1, D_MODEL), F32)],
        compiler_params=_cparams(("arbitrary",)),
    )(*args)


def _loss_head(xhat, g, b, target, tm=256):
    T = xhat.shape[0]

    def body(xhat_ref, g_ref, b_ref, t_ref, loss_ref, dy_ref):
        err = xhat_ref[...] * g_ref[...] + b_ref[...] - t_ref[...]

        @pl.when(pl.program_id(0) == 0)
        def _():
            loss_ref[...] = jnp.zeros_like(loss_ref)

        part = jnp.sum(jnp.sum(err * err, axis=-1, keepdims=True), axis=0, keepdims=True)
        loss_ref[...] += jnp.broadcast_to(part * (0.5 / D_MODEL), loss_ref.shape)
        dy_ref[...] = err * (1.0 / D_MODEL)

    row = pl.BlockSpec((tm, D_MODEL), lambda i: (i, 0))
    vec = pl.BlockSpec((1, D_MODEL), lambda i: (0, 0))
    return pl.pallas_call(
        body,
        name="loss_head",
        grid=(T // tm,),
        in_specs=[row, vec, vec, row],
        out_specs=[pl.BlockSpec((8, LANES), lambda i: (0, 0)), row],
        out_shape=[jax.ShapeDtypeStruct((8, LANES), F32), jax.ShapeDtypeStruct((T, D_MODEL), F32)],
        compiler_params=_cparams(("arbitrary",)),
    )(xhat, g, b, target)


def _residual_out(dres, dmm, tm=256):
    T = dres.shape[0]

    def body(a_ref, b_ref, o_ref):
        o_ref[...] = ALPHA * a_ref[...] + b_ref[...]

    row = pl.BlockSpec((tm, D_MODEL), lambda i: (i, 0))
    return pl.pallas_call(
        body, name="residual_out", grid=(T // tm,), in_specs=[row, row], out_specs=row,
        out_shape=jax.ShapeDtypeStruct((T, D_MODEL), F32), compiler_params=_cparams(("parallel",)),
    )(dres, dmm)


SW_TC = 1408


def _swiglu_fwd(gu, tm=128):
    T = gu.shape[0]

    def body(gu_ref, o_ref):
        gv = gu_ref[:, :D_FF]
        o_ref[...] = (gv * jax.nn.sigmoid(gv) * gu_ref[:, D_FF:]).astype(o_ref.dtype)

    return pl.pallas_call(
        body, name="swiglu_fwd", grid=(T // tm,),
        in_specs=[pl.BlockSpec((tm, 2 * D_FF), lambda i: (i, 0))],
        out_specs=pl.BlockSpec((tm, D_FF), lambda i: (i, 0)),
        out_shape=jax.ShapeDtypeStruct((T, D_FF), BF16), compiler_params=_cparams(("parallel",)),
    )(gu)


def _swiglu_bwd(gu, dact, tm=128):
    T = gu.shape[0]

    def body(gu_ref, da_ref, dgu_ref, act_ref):
        gv, uv, da = gu_ref[:, :D_FF], gu_ref[:, D_FF:], da_ref[...]
        s = jax.nn.sigmoid(gv)
        sg = gv * s
        act_ref[...] = (sg * uv).astype(act_ref.dtype)
        dgu_ref[:, D_FF:] = (da * sg).astype(dgu_ref.dtype)
        dgu_ref[:, :D_FF] = (da * uv * (s * (1.0 + gv * (1.0 - s)))).astype(dgu_ref.dtype)

    wide = pl.BlockSpec((tm, 2 * D_FF), lambda i: (i, 0))
    half = pl.BlockSpec((tm, D_FF), lambda i: (i, 0))
    return pl.pallas_call(
        body, name="swiglu_bwd", grid=(T // tm,),
        in_specs=[wide, half], out_specs=[wide, half],
        out_shape=[jax.ShapeDtypeStruct((T, 2 * D_FF), BF16), jax.ShapeDtypeStruct((T, D_FF), BF16)],
        compiler_params=_cparams(("parallel",)),
    )(gu, dact)


def _adamw(w, g, m, v, tr):
    R, C = w.shape
    assert R % tr == 0
    c1 = 1.0 - ADAM_B1 ** ADAM_STEP
    c2 = 1.0 - ADAM_B2 ** ADAM_STEP

    def body(w_ref, g_ref, m_ref, v_ref, d_ref, mo_ref, vo_ref):
        gv = g_ref[...]
        mn = ADAM_B1 * m_ref[...] + (1.0 - ADAM_B1) * gv
        vn = ADAM_B2 * v_ref[...] + (1.0 - ADAM_B2) * (gv * gv)
        d_ref[...] = -ADAM_LR * ((mn / c1) / (jnp.sqrt(vn / c2) + ADAM_EPS) + ADAM_WD * w_ref[...])
        mo_ref[...] = mn
        vo_ref[...] = vn

    blk = pl.BlockSpec((tr, C), lambda i: (i, 0))
    return pl.pallas_call(
        body, name="adamw", grid=(R // tr,), in_specs=[blk] * 4, out_specs=[blk] * 3,
        out_shape=[jax.ShapeDtypeStruct((R, C), F32)] * 3, compiler_params=_cparams(("parallel",)),
    )(w, g, m, v)


def _my_place():
    return lax.axis_index("x"), lax.axis_index("y"), lax.axis_index("c")


ANY = pl.BlockSpec(memory_space=pl.ANY)
HBM = pl.BlockSpec(memory_space=pltpu.HBM)
SEM = pl.BlockSpec(memory_space=pltpu.SEMAPHORE)
EFFECT = pltpu.SideEffectType.DATAFLOW_SIDE_EFFECTING


def _in_hbm(a):
    return pltpu.with_memory_space_constraint(a, pltpu.HBM)


def _block_rows(ref, w, dev):
    r = SHARD_ROWS[w]
    start = pl.multiple_of((4 * dev[0] + 2 * dev[1] + dev[2]) * r, 16)
    return ref.at[pl.ds(start, r), :]


def _ag_first_copies(s_refs, land_refs, send_sems, recv_sems, receiving):
    x, y, c = _my_place()
    peers = [(x, y, 1 - c)] + [(*chip, c) for chip in _other_chips(x, y)]
    copies = []
    for k, peer in enumerate(peers):
        block = peer if receiving else (x, y, c)
        copies += [pltpu.make_async_remote_copy(
            src_ref=s_refs[w], dst_ref=_block_rows(land_refs[w], w, block),
            send_sem=send_sems.at[k * len(s_refs) + w], recv_sem=recv_sems.at[k * len(s_refs) + w],
            device_id=peer, device_id_type=MESH)
            for w in range(len(s_refs))]
    return copies


def _ag_start(shards, layer, after=()):
    nw = len(shards)

    def body(*refs):
        s_refs, land_refs = refs[:nw], refs[nw:2 * nw]
        token = refs[-1]
        sems = 2 * nw + len(after)
        for cp in _ag_first_copies(s_refs, land_refs, refs[sems], refs[sems + 1], False):
            cp.start()
        token[...] = jnp.zeros_like(token)

    lands = [lax.empty((N_DEV * s.shape[0], D_MODEL), BF16) for s in shards]
    out = pl.pallas_call(
        body, name="ag_start_%d" % layer,
        in_specs=[HBM] * (2 * nw) + [ANY] * len(after),
        out_specs=(SEM, SEM, *[HBM] * (2 * nw), pl.BlockSpec(memory_space=pltpu.VMEM)),
        out_shape=(pltpu.SemaphoreType.DMA((4 * nw,)), pltpu.SemaphoreType.DMA((4 * nw,)),
                   *[pltpu.HBM(a.shape, a.dtype) for a in list(shards) + lands],
                   jax.ShapeDtypeStruct((8, LANES), F32)),
        input_output_aliases={i: 2 + i for i in range(2 * nw)},
        compiler_params=pltpu.CompilerParams(has_side_effects=EFFECT),
    )(*[_in_hbm(a) for a in list(shards) + lands], *after)
    return out[0], out[1], out[2:2 + nw], out[2 + nw:2 + 2 * nw], out[-1]


def _ag_wait(send_sems, recv_sems, shards, lands, after, layer):
    nw = len(shards)

    def body(*refs):
        s_refs, land_refs = refs[:nw], refs[nw:2 * nw]
        for cp in _ag_first_copies(s_refs, land_refs, refs[2 * nw], refs[2 * nw + 1], True):
            cp.wait_send()
            cp.wait_recv()

    out = pl.pallas_call(
        body, name="ag_wait_%d" % layer,
        in_specs=[HBM] * (2 * nw) + [SEM, SEM] + [ANY] * len(after),
        out_specs=[HBM] * (2 * nw),
        out_shape=[pltpu.HBM(a.shape, a.dtype) for a in list(shards) + list(lands)],
        input_output_aliases={i: i for i in range(2 * nw)},
        compiler_params=pltpu.CompilerParams(has_side_effects=EFFECT),
    )(*shards, *lands, send_sems, recv_sems, *after)
    return out[:nw], out[nw:]


def _ag_pass_on(shards, lands):
    nw = len(shards)

    def body(*refs):
        s_refs, g_refs = refs[:nw], refs[2 * nw:3 * nw]
        send_sems, recv_sems, local_sems = refs[3 * nw:3 * nw + 3]
        stage = refs[3 * nw + 3:]
        x, y, c = _my_place()
        load = [pltpu.make_async_copy(s_refs[w], stage[w], local_sems.at[w]) for w in range(nw)]
        mine = [pltpu.make_async_copy(stage[w], _block_rows(g_refs[w], w, (x, y, c)), local_sems.at[w])
                for w in range(nw)]
        for cp in load:
            cp.start()
        sends, arrivals = [], []
        for j, chip in enumerate(_other_chips(x, y)):
            for w in range(nw):
                rows_out = _block_rows(g_refs[w], w, (*chip, c))
                rows_in = _block_rows(g_refs[w], w, (*chip, 1 - c))
                sends.append(pltpu.make_async_remote_copy(
                    src_ref=rows_out, dst_ref=rows_out, send_sem=send_sems.at[j, w], recv_sem=recv_sems.at[j, w],
                    device_id=(x, y, 1 - c), device_id_type=MESH))
                arrivals.append(pltpu.make_async_remote_copy(
                    src_ref=rows_in, dst_ref=rows_in, send_sem=send_sems.at[j, w], recv_sem=recv_sems.at[j, w],
                    device_id=(x, y, 1 - c), device_id_type=MESH))
        for cp in sends:
            cp.start()
        for w in range(nw):
            load[w].wait()
            mine[w].start()
        for cp in arrivals:
            cp.wait_recv()
        for cp in sends:
            cp.wait_send()
        for cp in mine:
            cp.wait()

    return pl.pallas_call(
        body, name="ag_pass_on",
        in_specs=[ANY] * (2 * nw), out_specs=[ANY] * nw,
        out_shape=[jax.ShapeDtypeStruct(a.shape, a.dtype) for a in lands],
        input_output_aliases={nw + i: i for i in range(nw)},
        scratch_shapes=[pltpu.SemaphoreType.DMA((3, nw)), pltpu.SemaphoreType.DMA((3, nw)),
                        pltpu.SemaphoreType.DMA((nw,))] + [pltpu.VMEM(s.shape, s.dtype) for s in shards],
        compiler_params=_cparams(),
    )(*shards, *lands)


def _rs_sibling_exchange(parts):
    nw = len(parts)

    def body(*refs):
        p_refs, o_refs = refs[:nw], refs[nw:2 * nw]
        send_sems, recv_sems = refs[2 * nw:]
        x, y, c = _my_place()
        copies = [pltpu.make_async_remote_copy(
            src_ref=p_refs[w].at[:, 1 - c], dst_ref=o_refs[w],
            send_sem=send_sems.at[w], recv_sem=recv_sems.at[w], device_id=(x, y, 1 - c), device_id_type=MESH)
            for w in range(nw)]
        for cp in copies:
            cp.start()
        for cp in copies:
            cp.wait()

    return pl.pallas_call(
        body, name="rs_sibling_exchange",
        in_specs=[ANY] * nw, out_specs=[ANY] * nw,
        out_shape=[jax.ShapeDtypeStruct(p.shape[:1] + p.shape[2:], BF16) for p in parts],
        scratch_shapes=[pltpu.SemaphoreType.DMA((nw,)), pltpu.SemaphoreType.DMA((nw,))],
    )(*parts)


def _rs_chip_sum(part, got, c):
    nxy, _, r, _ = part.shape

    def body(c_ref, p_ref, g_ref, o_ref):
        o_ref[...] = (p_ref[...].astype(F32) + g_ref[...].astype(F32)).astype(o_ref.dtype)

    return pl.pallas_call(
        body, name="rs_chip_sum",
        grid_spec=pltpu.PrefetchScalarGridSpec(
            num_scalar_prefetch=1, grid=(nxy,),
            in_specs=[pl.BlockSpec((None, None, r, D_MODEL), lambda q, c_ref: (q, c_ref[0], 0, 0)),
                      pl.BlockSpec((None, r, D_MODEL), lambda q, c_ref: (q, 0, 0))],
            out_specs=pl.BlockSpec((None, r, D_MODEL), lambda q, c_ref: (q, 0, 0))),
        out_shape=jax.ShapeDtypeStruct(got.shape, BF16),
        compiler_params=_cparams(("parallel",)),
    )(c, part, got)


def _other_chips(x, y):
    return [(1 - x, y), (x, 1 - y), (1 - x, 1 - y)]


def _rs_chip_copies(s_refs, land_refs, send_sems, recv_sems):
    x, y, c = _my_place()
    copies = []
    for k, chip in enumerate(_other_chips(x, y)):
        q = 2 * chip[0] + chip[1]
        copies += [pltpu.make_async_remote_copy(
            src_ref=s_refs[w].at[q], dst_ref=land_refs[w].at[k],
            send_sem=send_sems.at[k * len(s_refs) + w], recv_sem=recv_sems.at[k * len(s_refs) + w],
            device_id=(*chip, c), device_id_type=MESH)
            for w in range(len(s_refs))]
    return copies


def _rs_chip_start(sums, layer):
    nw = len(sums)

    def body(*refs):
        s_refs, land_refs = refs[:nw], refs[nw:2 * nw]
        send_sems, recv_sems = refs[2 * nw], refs[2 * nw + 1]
        token = refs[-1]
        for cp in _rs_chip_copies(s_refs, land_refs, send_sems, recv_sems):
            cp.start()
        token[...] = jnp.zeros_like(token)

    lands = [lax.empty((3,) + s.shape[1:], BF16) for s in sums]
    out = pl.pallas_call(
        body, name="rs_chip_start_%d" % layer,
        in_specs=[HBM] * (2 * nw),
        out_specs=(SEM, SEM, *[HBM] * (2 * nw), pl.BlockSpec(memory_space=pltpu.VMEM)),
        out_shape=(pltpu.SemaphoreType.DMA((3 * nw,)), pltpu.SemaphoreType.DMA((3 * nw,)),
                   *[pltpu.HBM(a.shape, a.dtype) for a in list(sums) + lands],
                   jax.ShapeDtypeStruct((8, LANES), F32)),
        input_output_aliases={i: 2 + i for i in range(2 * nw)},
        compiler_params=pltpu.CompilerParams(has_side_effects=EFFECT),
    )(*[_in_hbm(a) for a in list(sums) + lands])
    return out[0], out[1], out[2:2 + nw], out[2 + nw:2 + 2 * nw], out[-1]


def _rs_chip_wait(send_sems, recv_sems, sums, lands, after, layer):
    nw = len(sums)

    def body(*refs):
        s_refs, land_refs = refs[:nw], refs[nw:2 * nw]
        for cp in _rs_chip_copies(s_refs, land_refs, refs[2 * nw], refs[2 * nw + 1]):
            cp.wait_send()
            cp.wait_recv()

    out = pl.pallas_call(
        body, name="rs_chip_wait_%d" % layer,
        in_specs=[HBM] * (2 * nw) + [SEM, SEM] + [ANY] * len(after),
        out_specs=[HBM] * (2 * nw),
        out_shape=[pltpu.HBM(a.shape, a.dtype) for a in list(sums) + list(lands)],
        input_output_aliases={i: i for i in range(2 * nw)},
        compiler_params=pltpu.CompilerParams(has_side_effects=EFFECT),
    )(*sums, *lands, send_sems, recv_sems, *after)
    return out[:nw], out[nw:]


def _rs_finish(sums, got, q):
    _, r, _ = sums.shape

    def body(q_ref, s_ref, g_ref, o_ref):
        o_ref[...] = ((s_ref[...].astype(F32) + g_ref[0].astype(F32)) + g_ref[1].astype(F32)) + g_ref[2].astype(F32)

    return pl.pallas_call(
        body, name="rs_finish",
        grid_spec=pltpu.PrefetchScalarGridSpec(
            num_scalar_prefetch=1, grid=(1,),
            in_specs=[pl.BlockSpec((None, r, D_MODEL), lambda i, q_ref: (q_ref[0], 0, 0)),
                      pl.BlockSpec((3, r, D_MODEL), lambda i, q_ref: (0, 0, 0))],
            out_specs=pl.BlockSpec((r, D_MODEL), lambda i, q_ref: (0, 0))),
        out_shape=jax.ShapeDtypeStruct((r, D_MODEL), F32),
        compiler_params=_cparams(("arbitrary",)),
    )(q, sums, got)


def _allreduce_small(vec):
    R = vec.shape[0]
    assert R % (8 * N_DEV) == 0
    P = R // N_DEV

    def body(v_ref, o_ref, buf, send1, recv1, send2, recv2):
        x, y, c = _my_place()
        me = 4 * x + 2 * y + c

        def piece(ref, d):
            return ref.at[pl.ds(pl.multiple_of(d * P, 8), P), :]

        def peer(k):
            p = me ^ k
            return p, (p >> 2, (p >> 1) & 1, p & 1)

        scatter = []
        for k in range(1, N_DEV):
            p, where = peer(k)
            scatter.append(pltpu.make_async_remote_copy(
                src_ref=piece(v_ref, p), dst_ref=buf.at[k], send_sem=send1.at[k - 1], recv_sem=recv1.at[k - 1],
                device_id=where, device_id_type=MESH))
        for cp in scatter:
            cp.start()
        buf[0] = piece(v_ref, me)[...]
        for cp in scatter:
            cp.wait()
        acc = buf[me]
        for d in range(1, N_DEV):
            acc = acc + buf[me ^ d]
        piece(o_ref, me)[...] = acc
        spread, arrivals = [], []
        for k in range(1, N_DEV):
            p, where = peer(k)
            spread.append(pltpu.make_async_remote_copy(
                src_ref=piece(o_ref, me), dst_ref=piece(o_ref, me), send_sem=send2.at[k - 1], recv_sem=recv2.at[k - 1],
                device_id=where, device_id_type=MESH))
            arrivals.append(pltpu.make_async_remote_copy(
                src_ref=piece(o_ref, p), dst_ref=piece(o_ref, p), send_sem=send2.at[k - 1], recv_sem=recv2.at[k - 1],
                device_id=where, device_id_type=MESH))
        for cp in spread:
            cp.start()
        for cp in arrivals:
            cp.wait_recv()
        for cp in spread:
            cp.wait_send()

    sems = pltpu.SemaphoreType.DMA((N_DEV - 1,))
    return pl.pallas_call(
        body, name="allreduce_small",
        in_specs=[pl.BlockSpec(memory_space=pltpu.VMEM)], out_specs=pl.BlockSpec(memory_space=pltpu.VMEM),
        out_shape=jax.ShapeDtypeStruct((R, LANES), F32),
        scratch_shapes=[pltpu.VMEM((N_DEV, P, LANES), F32), sems, sems, sems, sems],
        compiler_params=_cparams(),
    )(vec)


def _pack(arrs):
    flat = jnp.concatenate([a.reshape(-1) for a in arrs])
    pad = (-flat.shape[0]) % (8 * N_DEV * LANES)
    return jnp.pad(flat, (0, pad)).reshape(-1, LANES)


def _unpack(packed, shapes):
    flat = packed.reshape(-1)
    out, off = [], 0
    for s in shapes:
        n = math.prod(s)
        out.append(flat[off:off + n].reshape(s))
        off += n
    return out


def kernel(x, w_in, w_conv, w_pool, pool_scale, sgu_ln_g, w_spatial, b_spatial, w_o, ln1_g, ln1_b, w_gate_up, w_down, ln2_g, ln2_b, loss_target, m_w_in, m_w_conv, m_w_pool, m_pool_scale, m_sgu_ln_g, m_w_spatial, m_b_spatial, m_w_o, m_ln1_g, m_ln1_b, m_w_gate_up, m_w_down, m_ln2_g, m_ln2_b, v_w_in, v_w_conv, v_w_pool, v_pool_scale, v_sgu_ln_g, v_w_spatial, v_b_spatial, v_w_o, v_ln1_g, v_ln1_b, v_w_gate_up, v_w_down, v_ln2_g, v_ln2_b):
    L = DEPTH
    T = x.shape[1]
    mx, my, mc = _my_place()
    dev = 4 * mx + 2 * my + mc
    xs = x[0]
    target = loss_target[0]

    shards = (jnp.swapaxes(w_in, 1, 2).astype(BF16), jnp.swapaxes(w_gate_up, 1, 2).astype(BF16),
              w_o.astype(BF16), w_down.astype(BF16))
    first_gather = _ag_start([s[0] for s in shards], 0)

    conv_cols = w_conv.shape[2]
    w_conv_z = lax.dynamic_update_slice(jnp.zeros((L, 3, CONV_W), F32), w_conv, (0, 0, dev * conv_cols))
    w_conv_full = _allreduce_small(_pack([w_conv_z]))
    w_conv_full = _unpack(w_conv_full, [(L, 3, CONV_W)])[0]

    loss_tile, grad_x2, big_grads, small_grads = _local_step(
        xs, target, shards, first_gather, w_conv_full, w_pool, pool_scale, sgu_ln_g, w_spatial, b_spatial,
        ln1_g, ln1_b, ln2_g, ln2_b)
    loss = lax.psum(loss_tile[0, 0], ("x", "y", "c"))
    grad_x = grad_x2[None]
    big_w = (w_in, w_gate_up, w_o, w_down)
    big_m = (m_w_in, m_w_gate_up, m_w_o, m_w_down)
    big_v = (v_w_in, v_w_gate_up, v_w_o, v_w_down)
    small_w = [w_conv_full, w_pool, pool_scale, sgu_ln_g, w_spatial, b_spatial, ln1_g, ln1_b, ln2_g, ln2_b]
    small_m = [m_w_conv, m_w_pool, m_pool_scale, m_sgu_ln_g, m_w_spatial, m_b_spatial, m_ln1_g, m_ln1_b, m_ln2_g, m_ln2_b]
    small_v = [v_w_conv, v_w_pool, v_pool_scale, v_sgu_ln_g, v_w_spatial, v_b_spatial, v_ln1_g, v_ln1_b, v_ln2_g, v_ln2_b]
    grads, deltas, new_m, new_v = _reduce_and_update(
        big_grads, small_grads, big_w, big_m, big_v, small_w, small_m, small_v)
    return (loss, grad_x, *grads, *deltas, *new_m, *new_v)


def _local_step(xs, target, shards, gather, w_conv_full, w_pool, pool_scale, sgu_ln_g, w_spatial, b_spatial,
                ln1_g, ln1_b, ln2_g, ln2_b):
    L = DEPTH
    T = xs.shape[0]
    mx, my, mc = _my_place()
    c_arr = jnp.reshape(mc, (1,)).astype(jnp.int32)
    q_arr = jnp.reshape(2 * mx + my, (1,)).astype(jnp.int32)
    eye2 = jnp.eye(2, dtype=F32)
    wp = w_pool.reshape(L, 2, 2, HALF, HALF)
    wpool_bd = jnp.einsum("ltgcd,gh->ltgchd", wp, eye2).reshape(L, 2, LANES, LANES)
    wsp_t = w_spatial.reshape(L, 3, 2 * CHUNK, CHUNK)
    bias_t = jnp.repeat(jnp.swapaxes(b_spatial.reshape(L, 3, 2, CHUNK), 2, 3), HALF, axis=3)
    ones = jnp.ones((1, D_MODEL), F32)
    zeros = jnp.zeros((1, D_MODEL), F32)

    saved = []
    prev, pg, pb = xs, ones, zeros
    prev_b = xs.astype(BF16)
    weights = []
    for l in range(L):
        send_sems, recv_sems, shards_l, lands_l, _ = gather
        shards_l, lands_l = _ag_wait(send_sems, recv_sems, shards_l, lands_l, [] if l == 0 else [prev_b], l)
        g_in, g_gu, g_o, g_dn = _ag_pass_on(shards_l, lands_l)
        weights.append((g_in, g_gu, g_o, g_dn))
        deps = []
        if l + 1 < L:
            gather = _ag_start([s[l + 1] for s in shards], l + 1, after=[g_in])
            deps = [gather[4]]
        proj = _mm(prev_b, g_in, "nt", F32, 512, IN_W, D_MODEL, "mm_proj", deps=deps)
        mixcat = _mixer_fwd(proj, w_conv_full[l], wpool_bd[l], pool_scale[l][None], sgu_ln_g[l][None], wsp_t[l], bias_t[l])
        mix = _mm(mixcat, g_o, "nn", F32, T, 512, D_MODEL, "mm_wo")
        xhat1, rstd1, h_b = _ln_fwd(prev, pg, pb, mix, ln1_g[l][None], ln1_b[l][None])
        gu = _mm(h_b, g_gu, "nt", F32, T, 512, D_MODEL, "mm_gate_up")
        act = _swiglu_fwd(gu)
        ff = _mm(act, g_dn, "nn", F32, T, 256, D_FF, "mm_down")
        xhat2, rstd2, y_b = _ln_fwd(xhat1, ln1_g[l][None], ln1_b[l][None], ff, ln2_g[l][None], ln2_b[l][None])
        saved.append((prev_b, proj, mixcat, xhat1, rstd1, h_b, gu, xhat2, rstd2))
        prev, pg, pb, prev_b = xhat2, ln2_g[l][None], ln2_b[l][None], y_b

    loss_tile, dy = _loss_head(prev, pg, pb, target)

    small = [None] * L
    big = [None] * L
    dres, dmm = None, dy
    in_flight = None
    for l in reversed(range(L)):
        prev_b, proj, mixcat, xhat1, rstd1, h_b, gu, xhat2, rstd2 = saved[l]
        g_in, g_gu, g_o, g_dn = weights[l]
        deps = [in_flight[4]] if in_flight is not None else []
        dr2, dr2_b, dg2, db2 = _ln_bwd(dres, dmm, xhat2, rstd2, ln2_g[l][None], deps=deps)
        dact = _mm(dr2_b, g_dn, "nt", F32, T, 256, D_MODEL, "mm_dact")
        dgu, act = _swiglu_bwd(gu, dact)
        p_dn = _mm(act, dr2_b, "tn", BF16, 256, D_MODEL, T, "mm_dw_down")
        p_gu = _mm(dgu, h_b, "tn", BF16, 512, D_MODEL, T, "mm_dw_gate_up")
        dh = _mm(dgu, g_gu, "nn", F32, T, 512, SW_TC, "mm_dh")
        dr1, dr1_b, dg1, db1 = _ln_bwd(dr2, dh, xhat1, rstd1, ln1_g[l][None])
        dmix = _mm(dr1_b, g_o, "nt", F32, T, 512, D_MODEL, "mm_dmix")
        p_o = _mm(mixcat, dr1_b, "tn", BF16, 512, D_MODEL, T, "mm_dw_o")
        dproj, dwc, dwp, dps, dlng, dwsp, dbias = _mixer_bwd(
            proj, dmix, w_conv_full[l], wpool_bd[l], pool_scale[l][None], sgu_ln_g[l][None], wsp_t[l], bias_t[l])
        p_in = _mm(dproj, prev_b, "tn", BF16, IN_W, D_MODEL, T, "mm_dw_in")
        dx = _mm(dproj, g_in, "nn", F32, T, 512, IN_W, "mm_dx")
        small[l] = (dwc, dwp, dps, dlng, dwsp, dbias, dg1, db1, dg2, db2)
        dres, dmm = dr1, dx
        if in_flight is not None:
            big[l + 1] = _rs_chip_finish(in_flight, [dx], q_arr, l + 1)
        parts = [p.reshape(4, 2, r, D_MODEL) for p, r in zip((p_in, p_gu, p_o, p_dn), SHARD_ROWS)]
        got1 = _rs_sibling_exchange(parts)
        sums = [_rs_chip_sum(p, g, c_arr) for p, g in zip(parts, got1)]
        in_flight = _rs_chip_start(sums, l)
    big[0] = _rs_chip_finish(in_flight, [], q_arr, 0)
    grad_x = _residual_out(dres, dmm)
    big_grads = [jnp.stack([big[l][w] for l in range(L)]) for w in range(4)]

    def stack(i):
        return jnp.stack([small[l][i] for l in range(L)])

    dwp_bd = stack(1).reshape(L, 2, 2, HALF, 2, HALF)
    dwp_all = jnp.einsum("ltgchd,gh->ltgcd", dwp_bd, eye2).reshape(L, 4, HALF, HALF)
    dbs_all = jnp.swapaxes(stack(5)[:, :, :, :2], 2, 3).reshape(L, 6, CHUNK)
    small_grads = [stack(0), dwp_all, stack(2).reshape(L, POOL_W), stack(3).reshape(L, SGU_W),
                   stack(4).reshape(L, 6, CHUNK, CHUNK), dbs_all] + [stack(i).reshape(L, D_MODEL) for i in (6, 7, 8, 9)]
    return loss_tile, grad_x, big_grads, small_grads


def _rs_chip_finish(in_flight, after, q, layer):
    send_sems, recv_sems, sums, lands, _ = in_flight
    sums, got = _rs_chip_wait(send_sems, recv_sems, sums, lands, after, layer)
    return [_rs_finish(s, g, q) for s, g in zip(sums, got)]


def _reduce_and_update(big_grads, small_grads, big_w, big_m, big_v, small_w, small_m, small_v):
    L = DEPTH
    mx, my, mc = _my_place()
    dev = 4 * mx + 2 * my + mc
    conv_cols = CONV_W // N_DEV
    w_in, w_gate_up, w_o, w_down = big_w
    m_w_in, m_w_gate_up, m_w_o, m_w_down = big_m
    v_w_in, v_w_gate_up, v_w_o, v_w_down = big_v
    gt_in, gt_gu, g_w_o, g_w_dn = big_grads
    g_w_in = jnp.swapaxes(gt_in, 1, 2)
    g_w_gu = jnp.swapaxes(gt_gu, 1, 2)

    small_shapes = [a.shape for a in small_grads]
    packed_g = _allreduce_small(_pack(small_grads))

    def widen_conv(a):
        return lax.dynamic_update_slice(jnp.zeros((L, 3, CONV_W), F32), a, (0, 0, dev * conv_cols))

    small_m = [widen_conv(small_m[0])] + list(small_m[1:])
    small_v = [widen_conv(small_v[0])] + list(small_v[1:])
    pk_d, pk_m, pk_v = _adamw(_pack(small_w), packed_g, _pack(small_m), _pack(small_v), packed_g.shape[0] // 2)
    sg = _unpack(packed_g, small_shapes)
    sd = _unpack(pk_d, small_shapes)
    sm = _unpack(pk_m, small_shapes)
    sv = _unpack(pk_v, small_shapes)

    def conv_cols_of(a):
        return lax.dynamic_slice(a, (0, 0, dev * conv_cols), (L, 3, conv_cols))

    for lst in (sg, sd, sm, sv):
        lst[0] = conv_cols_of(lst[0])

    def big(w, g, m, v, tr):
        s = w.shape
        d, mn, vn = _adamw(w.reshape(-1, s[-1]), g.reshape(-1, s[-1]), m.reshape(-1, s[-1]), v.reshape(-1, s[-1]), tr)
        return d.reshape(s), mn.reshape(s), vn.reshape(s)

    d_in, m_in, v_in = big(w_in, g_w_in, m_w_in, v_w_in, 512)
    d_gu, m_gu, v_gu = big(w_gate_up, g_w_gu, m_w_gate_up, v_w_gate_up, 512)
    d_o, m_o, v_o = big(w_o, g_w_o, m_w_o, v_w_o, 128)
    d_dn, m_dn, v_dn = big(w_down, g_w_dn, m_w_down, v_w_down, 352)

    def ordered(big_in, big_o, big_gu, big_dn, sm_list):
        return [big_in, sm_list[0], sm_list[1], sm_list[2], sm_list[3], sm_list[4], sm_list[5], big_o,
                sm_list[6], sm_list[7], big_gu, big_dn, sm_list[8], sm_list[9]]

    grads = ordered(g_w_in, g_w_o, g_w_gu, g_w_dn, sg)
    deltas = ordered(d_in, d_o, d_gu, d_dn, sd)
    new_m = ordered(m_in, m_o, m_gu, m_dn, sm)
    new_v = ordered(v_in, v_o, v_gu, v_dn, sv)
    return grads, deltas, new_m, new_v
```

```python
import functools
import math

import jax
import jax.numpy as jnp
from jax import lax
from jax.experimental import pallas as pl
from jax.experimental.pallas import tpu as pltpu

F32 = jnp.float32
BF16 = jnp.bfloat16
MESH = pl.DeviceIdType.MESH

D_MODEL = 1024
DEPTH = 4
CONV_W = 384
POOL_W = 256
SGU_W = 384
IN_W = 3 * CONV_W + POOL_W + 2 * SGU_W
D_FF = 2816
CHUNK = 128
ALPHA = float((2 * DEPTH) ** 0.25)
LN_EPS = 1e-5
ADAM_LR, ADAM_B1, ADAM_B2, ADAM_EPS, ADAM_WD, ADAM_STEP = 0.001, 0.9, 0.999, 1e-08, 0.01, 10

N_DEV = 8
LANES = 128
HALF = 64
SHARD_ROWS = (IN_W // N_DEV, 2 * D_FF // N_DEV, D_MODEL // N_DEV, D_FF // N_DEV)
VMEM_LIMIT = 52 * 1024 * 1024

INV_SQRT2 = 0.7071067811865476
INV_SQRT_2PI = 0.3989422804014327


def _cparams(sem=None, **kw):
    if sem is not None:
        kw["dimension_semantics"] = sem
    return pltpu.CompilerParams(vmem_limit_bytes=VMEM_LIMIT, **kw)


_DN = {"nn": (((1,), (0,)), ((), ())), "nt": (((1,), (1,)), ((), ())), "tn": (((0,), (0,)), ((), ()))}


def _mm(a, b, mode, out_dtype, tm, tn, tk, name, deps=()):
    if mode == "nn":
        (M, K), N = a.shape, b.shape[1]
    elif mode == "nt":
        (M, K), N = a.shape, b.shape[0]
    else:
        (K, M), N = a.shape, b.shape[1]
    assert M % tm == 0 and N % tn == 0 and K % tk == 0, (M, N, K, tm, tn, tk)
    nk = K // tk
    nd = len(deps)

    def body(*refs):
        a_ref, b_ref, o_ref = refs[0], refs[1], refs[2 + nd]
        acc_ref = refs[3 + nd] if nk > 1 else None
        p = lax.dot_general(a_ref[...], b_ref[...], _DN[mode], preferred_element_type=F32)
        if nk == 1:
            o_ref[...] = p.astype(o_ref.dtype)
        else:
            k = pl.program_id(2)

            @pl.when(k == 0)
            def _():
                acc_ref[...] = p

            @pl.when(k > 0)
            def _():
                acc_ref[...] += p

            @pl.when(k == nk - 1)
            def _():
                o_ref[...] = acc_ref[...].astype(o_ref.dtype)

    if mode == "nn":
        a_spec = pl.BlockSpec((tm, tk), lambda i, j, k: (i, k))
        b_blk, b_idx = (tk, tn), (lambda i, j, k: (k, j))
    elif mode == "nt":
        a_spec = pl.BlockSpec((tm, tk), lambda i, j, k: (i, k))
        b_blk, b_idx = (tn, tk), (lambda i, j, k: (j, k))
    else:
        a_spec = pl.BlockSpec((tk, tm), lambda i, j, k: (k, i))
        b_blk, b_idx = (tk, tn), (lambda i, j, k: (k, j))
    return pl.pallas_call(
        body,
        name=name,
        grid=(M // tm, N // tn, nk),
        in_specs=[a_spec, pl.BlockSpec(b_blk, b_idx)] + [pl.BlockSpec(memory_space=pl.ANY)] * nd,
        out_specs=pl.BlockSpec((tm, tn), lambda i, j, k: (i, j)),
        out_shape=jax.ShapeDtypeStruct((M, N), out_dtype),
        scratch_shapes=[pltpu.VMEM((tm, tn), F32)] if nk > 1 else [],
        compiler_params=_cparams(("parallel", "parallel", "arbitrary")),
    )(a, b, *deps)


def _gelu(x):
    return 0.5 * x * (1.0 + lax.erf(x * INV_SQRT2))


def _gelu_grad(x):
    return 0.5 * (1.0 + lax.erf(x * INV_SQRT2)) + x * (jnp.exp(-0.5 * x * x) * INV_SQRT_2PI)


def _shift_down(z, k):
    row = lax.broadcasted_iota(jnp.int32, z.shape, 0)
    return jnp.where(row >= k, pltpu.roll(z, k, 0), 0.0)


def _shift_up(z, k):
    n = z.shape[0]
    row = lax.broadcasted_iota(jnp.int32, z.shape, 0)
    return jnp.where(row < n - k, pltpu.roll(z, n - k, 0), 0.0)


def _lo_mask(shape):
    return lax.broadcasted_iota(jnp.int32, shape, len(shape) - 1) < HALF


def _seg_mean(x, lo):
    a = jnp.sum(jnp.where(lo, x, 0.0), axis=-1, keepdims=True)
    b = jnp.sum(jnp.where(lo, 0.0, x), axis=-1, keepdims=True)
    return jnp.where(lo, a, b) * (1.0 / HALF)


def _pool_windows(first):
    lo = _lo_mask((1, LANES))
    return jnp.where(first, jnp.where(lo, 2.0, 4.0), jnp.where(lo, 8.0, 16.0)), lo


def _pool_mean_minus_token(p, first):
    wl, lo = _pool_windows(first)
    s2 = p + _shift_down(p, 1)
    s4 = s2 + _shift_down(s2, 2)
    s8 = s4 + _shift_down(s4, 4)
    s16 = s8 + _shift_down(s8, 8)
    win = jnp.where(first, jnp.where(lo, s2, s4), jnp.where(lo, s8, s16))
    t1 = (lax.broadcasted_iota(jnp.int32, p.shape, 0) + 1).astype(F32)
    count = jnp.minimum(t1, wl)
    return win / count - p, count


def _tril_keep():
    r = lax.broadcasted_iota(jnp.int32, (2 * CHUNK, CHUNK), 0)
    s = lax.broadcasted_iota(jnp.int32, (2 * CHUNK, CHUNK), 1)
    return s <= (r & (CHUNK - 1))


def _sgu_chunk_fwd(u, v, g, wm, bias, lo):
    ug = _gelu(u)
    vg = _gelu(v)
    mu = _seg_mean(vg, lo)
    xc = vg - mu
    var = _seg_mean(xc * xc, lo)
    rstd = lax.rsqrt(var + LN_EPS)
    vn = xc * rstd
    vh = (vn * g).astype(BF16)
    mm2 = jnp.dot(wm, vh, preferred_element_type=F32)
    mixed = jnp.where(lo, mm2[:CHUNK], mm2[CHUNK:]) + bias
    return ug, vn, rstd, vh, mixed


def _mixer_fwd(proj, wconv, wpool_bd, pscale, lng, wsp, bias):
    T = proj.shape[0]
    nchunk = T // CHUNK

    def body(a_ref, b_ref, c_ref, wc_ref, wp_ref, ps_ref, lng_ref, wsp_ref, bias_ref, o_ref):
        j = pl.program_id(0)

        @pl.when(j < 3)
        def _conv():
            z = c_ref[...] * a_ref[...]
            w = wc_ref[...]
            y = w[0:1] * _shift_down(z, 2) + w[1:2] * _shift_down(z, 1) + w[2:3] * z
            o_ref[...] = (b_ref[...] * y).astype(o_ref.dtype)

        @pl.when((j >= 3) & (j < 5))
        def _pool():
            d, _ = _pool_mean_minus_token(a_ref[...], j == 3)
            y = jnp.dot(d.astype(BF16), wp_ref[...].astype(BF16), preferred_element_type=F32)
            o_ref[...] = (y * ps_ref[...]).astype(o_ref.dtype)

        @pl.when(j >= 5)
        def _sgu():
            lo = _lo_mask((CHUNK, LANES))
            wm = jnp.where(_tril_keep(), wsp_ref[...], 0.0).astype(BF16)
            bias_t = bias_ref[...]
            g = lng_ref[...]

            def chunk(n, carry):
                rows = pl.ds(pl.multiple_of(n * CHUNK, CHUNK), CHUNK)
                ug, _, _, _, mixed = _sgu_chunk_fwd(a_ref[rows, :], b_ref[rows, :], g, wm, bias_t, lo)
                o_ref[rows, :] = (ug * mixed).astype(o_ref.dtype)
                return carry

            lax.fori_loop(0, nchunk, chunk, 0)

    def col(f):
        return lambda j: (0, f(j))

    clip = lambda v, lo, hi: jnp.minimum(jnp.maximum(v, lo), hi)
    return pl.pallas_call(
        body,
        name="mixer_fwd",
        grid=(8,),
        in_specs=[
            pl.BlockSpec((T, LANES), col(lambda j: jnp.where(j < 3, j, jnp.where(j < 5, j + 6, j + 6)))),
            pl.BlockSpec((T, LANES), col(lambda j: jnp.where(j < 3, j + 3, jnp.where(j < 5, 5, j + 9)))),
            pl.BlockSpec((T, LANES), col(lambda j: jnp.where(j < 3, j + 6, 8))),
            pl.BlockSpec((3, LANES), col(lambda j: clip(j, 0, 2))),
            pl.BlockSpec((None, LANES, LANES), lambda j: (clip(j - 3, 0, 1), 0, 0)),
            pl.BlockSpec((1, LANES), col(lambda j: clip(j - 3, 0, 1))),
            pl.BlockSpec((1, LANES), col(lambda j: clip(j - 5, 0, 2))),
            pl.BlockSpec((None, 2 * CHUNK, CHUNK), lambda j: (clip(j - 5, 0, 2), 0, 0)),
            pl.BlockSpec((None, CHUNK, LANES), lambda j: (clip(j - 5, 0, 2), 0, 0)),
        ],
        out_specs=pl.BlockSpec((T, LANES), lambda j: (0, j)),
        out_shape=jax.ShapeDtypeStruct((T, D_MODEL), BF16),
        compiler_params=_cparams(("arbitrary",)),
    )(proj, proj, proj, wconv, wpool_bd, pscale, lng, wsp, bias)


def _mixer_bwd(proj, dmix, wconv, wpool_bd, pscale, lng, wsp, bias):
    T = proj.shape[0]
    nchunk = T // CHUNK

    def body(a_ref, b_ref, c_ref, dm_ref, wc_ref, wp_ref, ps_ref, lng_ref, wsp_ref, bias_ref,
             o_ref, dwc_ref, dwp_ref, dps_ref, dlng_ref, dwsp_ref, dbias_ref, keep1, keep2):
        k = pl.program_id(0)

        @pl.when(k < 3)
        def _conv():
            xa, gb, gc, dya = a_ref[...], b_ref[...], c_ref[...], dm_ref[...]
            w = wc_ref[...]
            z = gc * xa
            z1 = _shift_down(z, 1)
            z2 = _shift_down(z, 2)
            y = w[0:1] * z2 + w[1:2] * z1 + w[2:3] * z
            dyv = dya * gb
            dz = w[2:3] * dyv + w[1:2] * _shift_up(dyv, 1) + w[0:1] * _shift_up(dyv, 2)
            dwc_ref[0:1, :] = jnp.sum(dyv * z2, axis=0, keepdims=True)
            dwc_ref[1:2, :] = jnp.sum(dyv * z1, axis=0, keepdims=True)
            dwc_ref[2:3, :] = jnp.sum(dyv * z, axis=0, keepdims=True)
            o_ref[...] = (dz * gc).astype(o_ref.dtype)
            keep1[k] = (dya * y).astype(keep1.dtype)
            keep1[k + 3] = (dz * xa).astype(keep1.dtype)

        @pl.when((k >= 3) & (k < 9))
        def _emit_gb_gc():
            o_ref[...] = keep1[k - 3]

        @pl.when((k >= 9) & (k < 11))
        def _pool():
            first = k == 9
            p, dyb = a_ref[...], dm_ref[...]
            d, count = _pool_mean_minus_token(p, first)
            w2 = wp_ref[...].astype(BF16)
            db = d.astype(BF16)
            y = jnp.dot(db, w2, preferred_element_type=F32)
            dps_ref[...] = jnp.sum(dyb * y, axis=0, keepdims=True)
            dyv = (dyb * ps_ref[...]).astype(BF16)
            dd = lax.dot_general(dyv, w2, _DN["nt"], preferred_element_type=F32)
            dwp_ref[...] = lax.dot_general(db, dyv, _DN["tn"], preferred_element_type=F32)
            dwin = dd / count
            a2 = dwin + _shift_up(dwin, 1)
            a4 = a2 + _shift_up(a2, 2)
            a8 = a4 + _shift_up(a4, 4)
            a16 = a8 + _shift_up(a8, 8)
            _, lo = _pool_windows(first)
            back = jnp.where(first, jnp.where(lo, a2, a4), jnp.where(lo, a8, a16))
            o_ref[...] = (back - dd).astype(o_ref.dtype)

        @pl.when((k >= 11) & (k < 14))
        def _sgu():
            lo = _lo_mask((CHUNK, LANES))
            keep = _tril_keep()
            wm = jnp.where(keep, wsp_ref[...], 0.0).astype(BF16)
            bias_t = bias_ref[...]
            g = lng_ref[...]
            dwsp_ref[...] = jnp.zeros_like(dwsp_ref)
            dbias_ref[...] = jnp.zeros_like(dbias_ref)
            dlng_ref[...] = jnp.zeros_like(dlng_ref)

            def chunk(n, carry):
                rows = pl.ds(pl.multiple_of(n * CHUNK, CHUNK), CHUNK)
                u, v, dyc = a_ref[rows, :], b_ref[rows, :], dm_ref[rows, :]
                ug, vn, rstd, vh, mixed = _sgu_chunk_fwd(u, v, g, wm, bias_t, lo)
                dmx = dyc * ug
                o_ref[rows, :] = (dyc * mixed * _gelu_grad(u)).astype(o_ref.dtype)
                dbias_ref[...] += dmx
                dst = jnp.concatenate([jnp.where(lo, dmx, 0.0), jnp.where(lo, 0.0, dmx)], axis=0).astype(BF16)
                dwsp_ref[...] += lax.dot_general(dst, vh, _DN["nt"], preferred_element_type=F32)
                dvh = lax.dot_general(wm, dst, _DN["tn"], preferred_element_type=F32)
                dlng_ref[...] += jnp.sum(dvh * vn, axis=0, keepdims=True)
                dvn = dvh * g
                m1 = _seg_mean(dvn, lo)
                m2 = _seg_mean(dvn * vn, lo)
                dvg = rstd * (dvn - m1 - vn * m2)
                keep2[k - 11, rows, :] = (dvg * _gelu_grad(v)).astype(keep2.dtype)
                return carry

            lax.fori_loop(0, nchunk, chunk, 0)
            dwsp_ref[...] = jnp.where(keep, dwsp_ref[...], 0.0)
            dbt = dbias_ref[...]
            lane = lax.broadcasted_iota(jnp.int32, (CHUNK, LANES), 1)
            sa = jnp.sum(jnp.where(lo, dbt, 0.0), axis=-1, keepdims=True)
            sb = jnp.sum(jnp.where(lo, 0.0, dbt), axis=-1, keepdims=True)
            dbias_ref[...] = jnp.where(lane == 0, sa, jnp.where(lane == 1, sb, 0.0))

        @pl.when(k >= 14)
        def _emit_v():
            o_ref[...] = keep2[k - 14]

    def col(f):
        return lambda k: (0, f(k))

    clip = lambda v, lo, hi: jnp.minimum(jnp.maximum(v, lo), hi)
    view_a = lambda k: jnp.where(k < 3, k, jnp.where(k < 9, 2, jnp.where(k < 14, k, 13)))
    view_b = lambda k: jnp.where(k < 3, k + 3, jnp.where(k < 11, 5, jnp.where(k < 14, k + 3, 16)))
    view_c = lambda k: jnp.where(k < 3, k + 6, 8)
    view_dm = lambda k: jnp.where(k < 3, k, jnp.where(k < 9, 2, jnp.where(k < 14, k - 6, 7)))
    return pl.pallas_call(
        body,
        name="mixer_bwd",
        grid=(17,),
        in_specs=[
            pl.BlockSpec((T, LANES), col(view_a)),
            pl.BlockSpec((T, LANES), col(view_b)),
            pl.BlockSpec((T, LANES), col(view_c)),
            pl.BlockSpec((T, LANES), col(view_dm)),
            pl.BlockSpec((3, LANES), col(lambda k: clip(k, 0, 2))),
            pl.BlockSpec((None, LANES, LANES), lambda k: (clip(k - 9, 0, 1), 0, 0)),
            pl.BlockSpec((1, LANES), col(lambda k: clip(k - 9, 0, 1))),
            pl.BlockSpec((1, LANES), col(lambda k: clip(k - 11, 0, 2))),
            pl.BlockSpec((None, 2 * CHUNK, CHUNK), lambda k: (clip(k - 11, 0, 2), 0, 0)),
            pl.BlockSpec((None, CHUNK, LANES), lambda k: (clip(k - 11, 0, 2), 0, 0)),
        ],
        out_specs=[
            pl.BlockSpec((T, LANES), lambda k: (0, k)),
            pl.BlockSpec((3, LANES), col(lambda k: clip(k, 0, 2))),
            pl.BlockSpec((None, LANES, LANES), lambda k: (clip(k - 9, 0, 1), 0, 0)),
            pl.BlockSpec((1, LANES), col(lambda k: clip(k - 9, 0, 1))),
            pl.BlockSpec((1, LANES), col(lambda k: clip(k - 11, 0, 2))),
            pl.BlockSpec((None, 2 * CHUNK, CHUNK), lambda k: (clip(k - 11, 0, 2), 0, 0)),
            pl.BlockSpec((None, CHUNK, LANES), lambda k: (clip(k - 11, 0, 2), 0, 0)),
        ],
        out_shape=[
            jax.ShapeDtypeStruct((T, IN_W), BF16),
            jax.ShapeDtypeStruct((3, CONV_W), F32),
            jax.ShapeDtypeStruct((2, LANES, LANES), F32),
            jax.ShapeDtypeStruct((1, POOL_W), F32),
            jax.ShapeDtypeStruct((1, SGU_W), F32),
            jax.ShapeDtypeStruct((3, 2 * CHUNK, CHUNK), F32),
            jax.ShapeDtypeStruct((3, CHUNK, LANES), F32),
        ],
        scratch_shapes=[pltpu.VMEM((6, T, LANES), BF16), pltpu.VMEM((3, T, LANES), BF16)],
        compiler_params=_cparams(("arbitrary",)),
    )(proj, proj, proj, dmix, wconv, wpool_bd, pscale, lng, wsp, bias)


def _ln_fwd(prev, pg, pb, mmout, g, b, tm=256):
    T = prev.shape[0]

    def body(prev_ref, pg_ref, pb_ref, mm_ref, g_ref, b_ref, xhat_ref, rstd_ref, y_ref):
        r = ALPHA * (prev_ref[...] * pg_ref[...] + pb_ref[...]) + mm_ref[...]
        mu = jnp.mean(r, axis=-1, keepdims=True)
        xc = r - mu
        var = jnp.mean(xc * xc, axis=-1, keepdims=True)
        rstd = lax.rsqrt(var + LN_EPS)
        xhat = xc * rstd
        xhat_ref[...] = xhat
        rstd_ref[...] = rstd
        y_ref[...] = (xhat * g_ref[...] + b_ref[...]).astype(y_ref.dtype)

    row = pl.BlockSpec((tm, D_MODEL), lambda i: (i, 0))
    vec = pl.BlockSpec((1, D_MODEL), lambda i: (0, 0))
    return pl.pallas_call(
        body,
        name="ln_fwd",
        grid=(T // tm,),
        in_specs=[row, vec, vec, row, vec, vec],
        out_specs=[row, pl.BlockSpec((tm, 1), lambda i: (i, 0)), row],
        out_shape=[jax.ShapeDtypeStruct((T, D_MODEL), F32), jax.ShapeDtypeStruct((T, 1), F32),
                   jax.ShapeDtypeStruct((T, D_MODEL), BF16)],
        compiler_params=_cparams(("parallel",)),
    )(prev, pg, pb, mmout, g, b)


def _ln_bwd(dres, dmm, xhat, rstd, g, tm=256, deps=()):
    T = xhat.shape[0]
    has_res = dres is not None
    nd = len(deps)

    def body(*refs):
        refs = refs[:len(refs) - 4 - nd] + refs[len(refs) - 4:]
        if has_res:
            dres_ref, dmm_ref, xhat_ref, rstd_ref, g_ref, dr_ref, drb_ref, dg_ref, db_ref = refs
            dy = ALPHA * dres_ref[...] + dmm_ref[...]
        else:
            dmm_ref, xhat_ref, rstd_ref, g_ref, dr_ref, drb_ref, dg_ref, db_ref = refs
            dy = dmm_ref[...]
        xhat_v = xhat_ref[...]

        @pl.when(pl.program_id(0) == 0)
        def _():
            dg_ref[...] = jnp.zeros_like(dg_ref)
            db_ref[...] = jnp.zeros_like(db_ref)

        dg_ref[...] += jnp.sum(dy * xhat_v, axis=0, keepdims=True)
        db_ref[...] += jnp.sum(dy, axis=0, keepdims=True)
        dxh = dy * g_ref[...]
        m1 = jnp.mean(dxh, axis=-1, keepdims=True)
        m2 = jnp.mean(dxh * xhat_v, axis=-1, keepdims=True)
        dr = rstd_ref[...] * (dxh - m1 - xhat_v * m2)
        dr_ref[...] = dr
        drb_ref[...] = dr.astype(drb_ref.dtype)

    row = pl.BlockSpec((tm, D_MODEL), lambda i: (i, 0))
    vec = pl.BlockSpec((1, D_MODEL), lambda i: (0, 0))
    in_specs = ([row] if has_res else []) + [row, row, pl.BlockSpec((tm, 1), lambda i: (i, 0)), vec]
    in_specs += [pl.BlockSpec(memory_space=pl.ANY)] * nd
    args = ([dres] if has_res else []) + [dmm, xhat, rstd, g] + list(deps)
    return pl.pallas_call(
        body,
        name="ln_bwd_res" if has_res else "ln_bwd",
        grid=(T // tm,),
        in_specs=in_specs,
        out_specs=[row, row, vec, vec],
        out_shape=[jax.ShapeDtypeStruct((T, D_MODEL), F32), jax.ShapeDtypeStruct((T, D_MODEL), BF16),
                   jax.ShapeDtypeStruct((1, D_MODEL), F32), jax.ShapeDtypeStruct((1, D_MODEL), F32)],
        compiler_params=_cparams(("arbitrary",)),
    )(*args)


def _loss_head(xhat, g, b, target, tm=256):
    T = xhat.shape[0]

    def body(xhat_ref, g_ref, b_ref, t_ref, loss_ref, dy_ref):
        err = xhat_ref[...] * g_ref[...] + b_ref[...] - t_ref[...]

        @pl.when(pl.program_id(0) == 0)
        def _():
            loss_ref[...] = jnp.zeros_like(loss_ref)

        part = jnp.sum(jnp.sum(err * err, axis=-1, keepdims=True), axis=0, keepdims=True)
        loss_ref[...] += jnp.broadcast_to(part * (0.5 / D_MODEL), loss_ref.shape)
        dy_ref[...] = err * (1.0 / D_MODEL)

    row = pl.BlockSpec((tm, D_MODEL), lambda i: (i, 0))
    vec = pl.BlockSpec((1, D_MODEL), lambda i: (0, 0))
    return pl.pallas_call(
        body,
        name="loss_head",
        grid=(T // tm,),
        in_specs=[row, vec, vec, row],
        out_specs=[pl.BlockSpec((8, LANES), lambda i: (0, 0)), row],
        out_shape=[jax.ShapeDtypeStruct((8, LANES), F32), jax.ShapeDtypeStruct((T, D_MODEL), F32)],
        compiler_params=_cparams(("arbitrary",)),
    )(xhat, g, b, target)


def _residual_out(dres, dmm, tm=256):
    T = dres.shape[0]

    def body(a_ref, b_ref, o_ref):
        o_ref[...] = ALPHA * a_ref[...] + b_ref[...]

    row = pl.BlockSpec((tm, D_MODEL), lambda i: (i, 0))
    return pl.pallas_call(
        body, name="residual_out", grid=(T // tm,), in_specs=[row, row], out_specs=row,
        out_shape=jax.ShapeDtypeStruct((T, D_MODEL), F32), compiler_params=_cparams(("parallel",)),
    )(dres, dmm)


SW_TC = 1408


def _swiglu_fwd(gu, tm=128):
    T = gu.shape[0]

    def body(gu_ref, o_ref):
        gv = gu_ref[:, :D_FF]
        o_ref[...] = (gv * jax.nn.sigmoid(gv) * gu_ref[:, D_FF:]).astype(o_ref.dtype)

    return pl.pallas_call(
        body, name="swiglu_fwd", grid=(T // tm,),
        in_specs=[pl.BlockSpec((tm, 2 * D_FF), lambda i: (i, 0))],
        out_specs=pl.BlockSpec((tm, D_FF), lambda i: (i, 0)),
        out_shape=jax.ShapeDtypeStruct((T, D_FF), BF16), compiler_params=_cparams(("parallel",)),
    )(gu)


def _swiglu_bwd(gu, dact, tm=128):
    T = gu.shape[0]

    def body(gu_ref, da_ref, dgu_ref, act_ref):
        gv, uv, da = gu_ref[:, :D_FF], gu_ref[:, D_FF:], da_ref[...]
        s = jax.nn.sigmoid(gv)
        sg = gv * s
        act_ref[...] = (sg * uv).astype(act_ref.dtype)
        dgu_ref[:, D_FF:] = (da * sg).astype(dgu_ref.dtype)
        dgu_ref[:, :D_FF] = (da * uv * (s * (1.0 + gv * (1.0 - s)))).astype(dgu_ref.dtype)

    wide = pl.BlockSpec((tm, 2 * D_FF), lambda i: (i, 0))
    half = pl.BlockSpec((tm, D_FF), lambda i: (i, 0))
    return pl.pallas_call(
        body, name="swiglu_bwd", grid=(T // tm,),
        in_specs=[wide, half], out_specs=[wide, half],
        out_shape=[jax.ShapeDtypeStruct((T, 2 * D_FF), BF16), jax.ShapeDtypeStruct((T, D_FF), BF16)],
        compiler_params=_cparams(("parallel",)),
    )(gu, dact)


def _adamw(w, g, m, v, tr):
    R, C = w.shape
    assert R % tr == 0
    c1 = 1.0 - ADAM_B1 ** ADAM_STEP
    c2 = 1.0 - ADAM_B2 ** ADAM_STEP

    def body(w_ref, g_ref, m_ref, v_ref, d_ref, mo_ref, vo_ref):
        gv = g_ref[...]
        mn = ADAM_B1 * m_ref[...] + (1.0 - ADAM_B1) * gv
        vn = ADAM_B2 * v_ref[...] + (1.0 - ADAM_B2) * (gv * gv)
        d_ref[...] = -ADAM_LR * ((mn / c1) / (jnp.sqrt(vn / c2) + ADAM_EPS) + ADAM_WD * w_ref[...])
        mo_ref[...] = mn
        vo_ref[...] = vn

    blk = pl.BlockSpec((tr, C), lambda i: (i, 0))
    return pl.pallas_call(
        body, name="adamw", grid=(R // tr,), in_specs=[blk] * 4, out_specs=[blk] * 3,
        out_shape=[jax.ShapeDtypeStruct((R, C), F32)] * 3, compiler_params=_cparams(("parallel",)),
    )(w, g, m, v)


def _my_place():
    return lax.axis_index("x"), lax.axis_index("y"), lax.axis_index("c")


ANY = pl.BlockSpec(memory_space=pl.ANY)
HBM = pl.BlockSpec(memory_space=pltpu.HBM)
SEM = pl.BlockSpec(memory_space=pltpu.SEMAPHORE)
EFFECT = pltpu.SideEffectType.DATAFLOW_SIDE_EFFECTING


def _in_hbm(a):
    return pltpu.with_memory_space_constraint(a, pltpu.HBM)


def _block_rows(ref, dev):
    r = ref.shape[0] // N_DEV
    start = pl.multiple_of((4 * dev[0] + 2 * dev[1] + dev[2]) * r, 16)
    return ref.at[pl.ds(start, r), :]


def _ag_first_copies(s_refs, land_refs, send_sems, recv_sems, receiving):
    x, y, c = _my_place()
    peers = [(x, y, 1 - c)] + [(*chip, c) for chip in _other_chips(x, y)]
    copies = []
    for k, peer in enumerate(peers):
        block = peer if receiving else (x, y, c)
        copies += [pltpu.make_async_remote_copy(
            src_ref=s_refs[w], dst_ref=_block_rows(land_refs[w], block),
            send_sem=send_sems.at[k * len(s_refs) + w], recv_sem=recv_sems.at[k * len(s_refs) + w],
            device_id=peer, device_id_type=MESH)
            for w in range(len(s_refs))]
    return copies


def _ag_start(shards, layer, after=()):
    nw = len(shards)

    def body(*refs):
        s_refs, land_refs = refs[:nw], refs[nw:2 * nw]
        token = refs[-1]
        sems = 2 * nw + len(after)
        for cp in _ag_first_copies(s_refs, land_refs, refs[sems], refs[sems + 1], False):
            cp.start()
        token[...] = jnp.zeros_like(token)

    lands = [lax.empty((N_DEV * s.shape[0], D_MODEL), BF16) for s in shards]
    out = pl.pallas_call(
        body, name="ag_start_%s" % layer,
        in_specs=[HBM] * (2 * nw) + [ANY] * len(after),
        out_specs=(SEM, SEM, *[HBM] * (2 * nw), pl.BlockSpec(memory_space=pltpu.VMEM)),
        out_shape=(pltpu.SemaphoreType.DMA((4 * nw,)), pltpu.SemaphoreType.DMA((4 * nw,)),
                   *[pltpu.HBM(a.shape, a.dtype) for a in list(shards) + lands],
                   jax.ShapeDtypeStruct((8, LANES), F32)),
        input_output_aliases={i: 2 + i for i in range(2 * nw)},
        compiler_params=pltpu.CompilerParams(has_side_effects=EFFECT),
    )(*[_in_hbm(a) for a in list(shards) + lands], *after)
    return out[0], out[1], out[2:2 + nw], out[2 + nw:2 + 2 * nw], out[-1]


def _ag_wait(send_sems, recv_sems, shards, lands, after, layer):
    nw = len(shards)

    def body(*refs):
        s_refs, land_refs = refs[:nw], refs[nw:2 * nw]
        for cp in _ag_first_copies(s_refs, land_refs, refs[2 * nw], refs[2 * nw + 1], True):
            cp.wait_send()
            cp.wait_recv()

    out = pl.pallas_call(
        body, name="ag_wait_%s" % layer,
        in_specs=[HBM] * (2 * nw) + [SEM, SEM] + [ANY] * len(after),
        out_specs=[HBM] * (2 * nw),
        out_shape=[pltpu.HBM(a.shape, a.dtype) for a in list(shards) + list(lands)],
        input_output_aliases={i: i for i in range(2 * nw)},
        compiler_params=pltpu.CompilerParams(has_side_effects=EFFECT),
    )(*shards, *lands, send_sems, recv_sems, *after)
    return out[:nw], out[nw:]


def _ag_pass_on(shards, lands):
    nw = len(shards)

    def body(*refs):
        s_refs, g_refs = refs[:nw], refs[2 * nw:3 * nw]
        send_sems, recv_sems, local_sems = refs[3 * nw:3 * nw + 3]
        stage = refs[3 * nw + 3:]
        x, y, c = _my_place()
        load = [pltpu.make_async_copy(s_refs[w], stage[w], local_sems.at[w]) for w in range(nw)]
        mine = [pltpu.make_async_copy(stage[w], _block_rows(g_refs[w], (x, y, c)), local_sems.at[w])
                for w in range(nw)]
        for cp in load:
            cp.start()
        sends, arrivals = [], []
        for j, chip in enumerate(_other_chips(x, y)):
            for w in range(nw):
                rows_out = _block_rows(g_refs[w], (*chip, c))
                rows_in = _block_rows(g_refs[w], (*chip, 1 - c))
                sends.append(pltpu.make_async_remote_copy(
                    src_ref=rows_out, dst_ref=rows_out, send_sem=send_sems.at[j, w], recv_sem=recv_sems.at[j, w],
                    device_id=(x, y, 1 - c), device_id_type=MESH))
                arrivals.append(pltpu.make_async_remote_copy(
                    src_ref=rows_in, dst_ref=rows_in, send_sem=send_sems.at[j, w], recv_sem=recv_sems.at[j, w],
                    device_id=(x, y, 1 - c), device_id_type=MESH))
        for cp in sends:
            cp.start()
        for w in range(nw):
            load[w].wait()
            mine[w].start()
        for cp in arrivals:
            cp.wait_recv()
        for cp in sends:
            cp.wait_send()
        for cp in mine:
            cp.wait()

    return pl.pallas_call(
        body, name="ag_pass_on",
        in_specs=[ANY] * (2 * nw), out_specs=[ANY] * nw,
        out_shape=[jax.ShapeDtypeStruct(a.shape, a.dtype) for a in lands],
        input_output_aliases={nw + i: i for i in range(nw)},
        scratch_shapes=[pltpu.SemaphoreType.DMA((3, nw)), pltpu.SemaphoreType.DMA((3, nw)),
                        pltpu.SemaphoreType.DMA((nw,))] + [pltpu.VMEM(s.shape, s.dtype) for s in shards],
        compiler_params=_cparams(),
    )(*shards, *lands)


def _rs_sibling_exchange(parts):
    nw = len(parts)

    def body(*refs):
        p_refs, o_refs = refs[:nw], refs[nw:2 * nw]
        send_sems, recv_sems = refs[2 * nw:]
        x, y, c = _my_place()
        copies = [pltpu.make_async_remote_copy(
            src_ref=p_refs[w].at[:, 1 - c], dst_ref=o_refs[w],
            send_sem=send_sems.at[w], recv_sem=recv_sems.at[w], device_id=(x, y, 1 - c), device_id_type=MESH)
            for w in range(nw)]
        for cp in copies:
            cp.start()
        for cp in copies:
            cp.wait()

    return pl.pallas_call(
        body, name="rs_sibling_exchange",
        in_specs=[ANY] * nw, out_specs=[ANY] * nw,
        out_shape=[jax.ShapeDtypeStruct(p.shape[:1] + p.shape[2:], BF16) for p in parts],
        scratch_shapes=[pltpu.SemaphoreType.DMA((nw,)), pltpu.SemaphoreType.DMA((nw,))],
    )(*parts)


def _rs_chip_sum(part, got, c):
    nxy, _, r, _ = part.shape

    def body(c_ref, p_ref, g_ref, o_ref):
        o_ref[...] = (p_ref[...].astype(F32) + g_ref[...].astype(F32)).astype(o_ref.dtype)

    return pl.pallas_call(
        body, name="rs_chip_sum",
        grid_spec=pltpu.PrefetchScalarGridSpec(
            num_scalar_prefetch=1, grid=(nxy,),
            in_specs=[pl.BlockSpec((None, None, r, D_MODEL), lambda q, c_ref: (q, c_ref[0], 0, 0)),
                      pl.BlockSpec((None, r, D_MODEL), lambda q, c_ref: (q, 0, 0))],
            out_specs=pl.BlockSpec((None, r, D_MODEL), lambda q, c_ref: (q, 0, 0))),
        out_shape=jax.ShapeDtypeStruct(got.shape, BF16),
        compiler_params=_cparams(("parallel",)),
    )(c, part, got)


def _other_chips(x, y):
    return [(1 - x, y), (x, 1 - y), (1 - x, 1 - y)]


def _rs_chip_copies(s_refs, land_refs, send_sems, recv_sems):
    x, y, c = _my_place()
    copies = []
    for k, chip in enumerate(_other_chips(x, y)):
        q = 2 * chip[0] + chip[1]
        copies += [pltpu.make_async_remote_copy(
            src_ref=s_refs[w].at[q], dst_ref=land_refs[w].at[k],
            send_sem=send_sems.at[k * len(s_refs) + w], recv_sem=recv_sems.at[k * len(s_refs) + w],
            device_id=(*chip, c), device_id_type=MESH)
            for w in range(len(s_refs))]
    return copies


def _rs_chip_start(sums, layer):
    nw = len(sums)

    def body(*refs):
        s_refs, land_refs = refs[:nw], refs[nw:2 * nw]
        send_sems, recv_sems = refs[2 * nw], refs[2 * nw + 1]
        token = refs[-1]
        for cp in _rs_chip_copies(s_refs, land_refs, send_sems, recv_sems):
            cp.start()
        token[...] = jnp.zeros_like(token)

    lands = [lax.empty((3,) + s.shape[1:], BF16) for s in sums]
    out = pl.pallas_call(
        body, name="rs_chip_start_%s" % layer,
        in_specs=[HBM] * (2 * nw),
        out_specs=(SEM, SEM, *[HBM] * (2 * nw), pl.BlockSpec(memory_space=pltpu.VMEM)),
        out_shape=(pltpu.SemaphoreType.DMA((3 * nw,)), pltpu.SemaphoreType.DMA((3 * nw,)),
                   *[pltpu.HBM(a.shape, a.dtype) for a in list(sums) + lands],
                   jax.ShapeDtypeStruct((8, LANES), F32)),
        input_output_aliases={i: 2 + i for i in range(2 * nw)},
        compiler_params=pltpu.CompilerParams(has_side_effects=EFFECT),
    )(*[_in_hbm(a) for a in list(sums) + lands])
    return out[0], out[1], out[2:2 + nw], out[2 + nw:2 + 2 * nw], out[-1]


def _rs_chip_wait(send_sems, recv_sems, sums, lands, after, layer):
    nw = len(sums)

    def body(*refs):
        s_refs, land_refs = refs[:nw], refs[nw:2 * nw]
        for cp in _rs_chip_copies(s_refs, land_refs, refs[2 * nw], refs[2 * nw + 1]):
            cp.wait_send()
            cp.wait_recv()

    out = pl.pallas_call(
        body, name="rs_chip_wait_%s" % layer,
        in_specs=[HBM] * (2 * nw) + [SEM, SEM] + [ANY] * len(after),
        out_specs=[HBM] * (2 * nw),
        out_shape=[pltpu.HBM(a.shape, a.dtype) for a in list(sums) + list(lands)],
        input_output_aliases={i: i for i in range(2 * nw)},
        compiler_params=pltpu.CompilerParams(has_side_effects=EFFECT),
    )(*sums, *lands, send_sems, recv_sems, *after)
    return out[:nw], out[nw:]


def _rs_finish(sums, got, q):
    _, r, _ = sums.shape

    def body(q_ref, s_ref, g_ref, o_ref):
        o_ref[...] = ((s_ref[...].astype(F32) + g_ref[0].astype(F32)) + g_ref[1].astype(F32)) + g_ref[2].astype(F32)

    return pl.pallas_call(
        body, name="rs_finish",
        grid_spec=pltpu.PrefetchScalarGridSpec(
            num_scalar_prefetch=1, grid=(1,),
            in_specs=[pl.BlockSpec((None, r, D_MODEL), lambda i, q_ref: (q_ref[0], 0, 0)),
                      pl.BlockSpec((3, r, D_MODEL), lambda i, q_ref: (0, 0, 0))],
            out_specs=pl.BlockSpec((r, D_MODEL), lambda i, q_ref: (0, 0))),
        out_shape=jax.ShapeDtypeStruct((r, D_MODEL), F32),
        compiler_params=_cparams(("arbitrary",)),
    )(q, sums, got)


def _allreduce_small(vec):
    R = vec.shape[0]
    assert R % (8 * N_DEV) == 0
    P = R // N_DEV

    def body(v_ref, o_ref, buf, send1, recv1, send2, recv2):
        x, y, c = _my_place()
        me = 4 * x + 2 * y + c

        def piece(ref, d):
            return ref.at[pl.ds(pl.multiple_of(d * P, 8), P), :]

        def peer(k):
            p = me ^ k
            return p, (p >> 2, (p >> 1) & 1, p & 1)

        scatter = []
        for k in range(1, N_DEV):
            p, where = peer(k)
            scatter.append(pltpu.make_async_remote_copy(
                src_ref=piece(v_ref, p), dst_ref=buf.at[k], send_sem=send1.at[k - 1], recv_sem=recv1.at[k - 1],
                device_id=where, device_id_type=MESH))
        for cp in scatter:
            cp.start()
        buf[0] = piece(v_ref, me)[...]
        for cp in scatter:
            cp.wait()
        acc = buf[me]
        for d in range(1, N_DEV):
            acc = acc + buf[me ^ d]
        piece(o_ref, me)[...] = acc
        spread, arrivals = [], []
        for k in range(1, N_DEV):
            p, where = peer(k)
            spread.append(pltpu.make_async_remote_copy(
                src_ref=piece(o_ref, me), dst_ref=piece(o_ref, me), send_sem=send2.at[k - 1], recv_sem=recv2.at[k - 1],
                device_id=where, device_id_type=MESH))
            arrivals.append(pltpu.make_async_remote_copy(
                src_ref=piece(o_ref, p), dst_ref=piece(o_ref, p), send_sem=send2.at[k - 1], recv_sem=recv2.at[k - 1],
                device_id=where, device_id_type=MESH))
        for cp in spread:
            cp.start()
        for cp in arrivals:
            cp.wait_recv()
        for cp in spread:
            cp.wait_send()

    sems = pltpu.SemaphoreType.DMA((N_DEV - 1,))
    return pl.pallas_call(
        body, name="allreduce_small",
        in_specs=[pl.BlockSpec(memory_space=pltpu.VMEM)], out_specs=pl.BlockSpec(memory_space=pltpu.VMEM),
        out_shape=jax.ShapeDtypeStruct((R, LANES), F32),
        scratch_shapes=[pltpu.VMEM((N_DEV, P, LANES), F32), sems, sems, sems, sems],
        compiler_params=_cparams(),
    )(vec)


def _pack(arrs):
    flat = jnp.concatenate([a.reshape(-1) for a in arrs])
    pad = (-flat.shape[0]) % (8 * N_DEV * LANES)
    return jnp.pad(flat, (0, pad)).reshape(-1, LANES)


def _unpack(packed, shapes):
    flat = packed.reshape(-1)
    out, off = [], 0
    for s in shapes:
        n = math.prod(s)
        out.append(flat[off:off + n].reshape(s))
        off += n
    return out


def kernel(x, w_in, w_conv, w_pool, pool_scale, sgu_ln_g, w_spatial, b_spatial, w_o, ln1_g, ln1_b, w_gate_up, w_down, ln2_g, ln2_b, loss_target, m_w_in, m_w_conv, m_w_pool, m_pool_scale, m_sgu_ln_g, m_w_spatial, m_b_spatial, m_w_o, m_ln1_g, m_ln1_b, m_w_gate_up, m_w_down, m_ln2_g, m_ln2_b, v_w_in, v_w_conv, v_w_pool, v_pool_scale, v_sgu_ln_g, v_w_spatial, v_b_spatial, v_w_o, v_ln1_g, v_ln1_b, v_w_gate_up, v_w_down, v_ln2_g, v_ln2_b):
    L = DEPTH
    T = x.shape[1]
    mx, my, mc = _my_place()
    dev = 4 * mx + 2 * my + mc
    xs = x[0]
    target = loss_target[0]

    shards = (jnp.swapaxes(w_in, 1, 2).astype(BF16), jnp.swapaxes(w_gate_up, 1, 2).astype(BF16),
              w_o.astype(BF16), w_down.astype(BF16))
    first_gather = _ag_start_layer(shards, 0, [])

    conv_cols = w_conv.shape[2]
    w_conv_z = lax.dynamic_update_slice(jnp.zeros((L, 3, CONV_W), F32), w_conv, (0, 0, dev * conv_cols))
    w_conv_full = _allreduce_small(_pack([w_conv_z]))
    w_conv_full = _unpack(w_conv_full, [(L, 3, CONV_W)])[0]

    loss_tile, grad_x2, big_grads, small_grads = _local_step(
        xs, target, shards, first_gather, w_conv_full, w_pool, pool_scale, sgu_ln_g, w_spatial, b_spatial,
        ln1_g, ln1_b, ln2_g, ln2_b)
    loss = lax.psum(loss_tile[0, 0], ("x", "y", "c"))
    grad_x = grad_x2[None]
    big_w = (w_in, w_gate_up, w_o, w_down)
    big_m = (m_w_in, m_w_gate_up, m_w_o, m_w_down)
    big_v = (v_w_in, v_w_gate_up, v_w_o, v_w_down)
    small_w = [w_conv_full, w_pool, pool_scale, sgu_ln_g, w_spatial, b_spatial, ln1_g, ln1_b, ln2_g, ln2_b]
    small_m = [m_w_conv, m_w_pool, m_pool_scale, m_sgu_ln_g, m_w_spatial, m_b_spatial, m_ln1_g, m_ln1_b, m_ln2_g, m_ln2_b]
    small_v = [v_w_conv, v_w_pool, v_pool_scale, v_sgu_ln_g, v_w_spatial, v_b_spatial, v_ln1_g, v_ln1_b, v_ln2_g, v_ln2_b]
    grads, deltas, new_m, new_v = _reduce_and_update(
        big_grads, small_grads, big_w, big_m, big_v, small_w, small_m, small_v)
    return (loss, grad_x, *grads, *deltas, *new_m, *new_v)


def _ag_start_layer(shards, l, after):
    s_in, s_gu, s_o, s_dn = [s[l] for s in shards]
    first = _ag_start([s_in, s_o], "%da" % l, after=after)
    return first, _ag_start([s_gu, s_dn], "%db" % l, after=[first[4]])


def _ag_finish(gather, after, tag):
    send_sems, recv_sems, shards, lands, _ = gather
    shards, lands = _ag_wait(send_sems, recv_sems, shards, lands, after, tag)
    return _ag_pass_on(shards, lands)


def _rs_begin(parts, c_arr, tag):
    parts = [p.reshape(4, 2, p.shape[0] // N_DEV, D_MODEL) for p in parts]
    got = _rs_sibling_exchange(parts)
    return _rs_chip_start([_rs_chip_sum(p, g, c_arr) for p, g in zip(parts, got)], tag)


def _local_step(xs, target, shards, gather, w_conv_full, w_pool, pool_scale, sgu_ln_g, w_spatial, b_spatial,
                ln1_g, ln1_b, ln2_g, ln2_b):
    L = DEPTH
    T = xs.shape[0]
    mx, my, mc = _my_place()
    c_arr = jnp.reshape(mc, (1,)).astype(jnp.int32)
    q_arr = jnp.reshape(2 * mx + my, (1,)).astype(jnp.int32)
    eye2 = jnp.eye(2, dtype=F32)
    wp = w_pool.reshape(L, 2, 2, HALF, HALF)
    wpool_bd = jnp.einsum("ltgcd,gh->ltgchd", wp, eye2).reshape(L, 2, LANES, LANES)
    wsp_t = w_spatial.reshape(L, 3, 2 * CHUNK, CHUNK)
    bias_t = jnp.repeat(jnp.swapaxes(b_spatial.reshape(L, 3, 2, CHUNK), 2, 3), HALF, axis=3)
    ones = jnp.ones((1, D_MODEL), F32)
    zeros = jnp.zeros((1, D_MODEL), F32)

    saved = []
    prev, pg, pb = xs, ones, zeros
    prev_b = xs.astype(BF16)
    weights = []
    for l in range(L):
        g_in, g_o = _ag_finish(gather[0], [] if l == 0 else [prev_b], "%da" % l)
        proj = _mm(prev_b, g_in, "nt", F32, 512, IN_W, D_MODEL, "mm_proj")
        mixcat = _mixer_fwd(proj, w_conv_full[l], wpool_bd[l], pool_scale[l][None], sgu_ln_g[l][None], wsp_t[l], bias_t[l])
        mix = _mm(mixcat, g_o, "nn", F32, T, 512, D_MODEL, "mm_wo")
        xhat1, rstd1, h_b = _ln_fwd(prev, pg, pb, mix, ln1_g[l][None], ln1_b[l][None])
        g_gu, g_dn = _ag_finish(gather[1], [h_b], "%db" % l)
        weights.append((g_in, g_gu, g_o, g_dn))
        deps = []
        if l + 1 < L:
            gather = _ag_start_layer(shards, l + 1, [g_gu])
            deps = [gather[1][4]]
        gu = _mm(h_b, g_gu, "nt", F32, T, 512, D_MODEL, "mm_gate_up", deps=deps)
        act = _swiglu_fwd(gu)
        ff = _mm(act, g_dn, "nn", F32, T, 256, D_FF, "mm_down")
        xhat2, rstd2, y_b = _ln_fwd(xhat1, ln1_g[l][None], ln1_b[l][None], ff, ln2_g[l][None], ln2_b[l][None])
        saved.append((prev_b, proj, mixcat, xhat1, rstd1, h_b, gu, xhat2, rstd2))
        prev, pg, pb, prev_b = xhat2, ln2_g[l][None], ln2_b[l][None], y_b

    loss_tile, dy = _loss_head(prev, pg, pb, target)

    small = [None] * L
    big = [None] * L
    dres, dmm = None, dy
    in_flight = None
    for l in reversed(range(L)):
        prev_b, proj, mixcat, xhat1, rstd1, h_b, gu, xhat2, rstd2 = saved[l]
        g_in, g_gu, g_o, g_dn = weights[l]
        deps = [in_flight[4]] if in_flight is not None else []
        dr2, dr2_b, dg2, db2 = _ln_bwd(dres, dmm, xhat2, rstd2, ln2_g[l][None], deps=deps)
        dact = _mm(dr2_b, g_dn, "nt", F32, T, 256, D_MODEL, "mm_dact")
        dgu, act = _swiglu_bwd(gu, dact)
        p_dn = _mm(act, dr2_b, "tn", BF16, 256, D_MODEL, T, "mm_dw_down")
        p_gu = _mm(dgu, h_b, "tn", BF16, 512, D_MODEL, T, "mm_dw_gate_up")
        ffn_flight = _rs_begin([p_gu, p_dn], c_arr, "%db" % l)
        dh = _mm(dgu, g_gu, "nn", F32, T, 512, SW_TC, "mm_dh", deps=[ffn_flight[4]])
        dr1, dr1_b, dg1, db1 = _ln_bwd(dr2, dh, xhat1, rstd1, ln1_g[l][None])
        dmix = _mm(dr1_b, g_o, "nt", F32, T, 512, D_MODEL, "mm_dmix")
        p_o = _mm(mixcat, dr1_b, "tn", BF16, 512, D_MODEL, T, "mm_dw_o")
        dproj, dwc, dwp, dps, dlng, dwsp, dbias = _mixer_bwd(
            proj, dmix, w_conv_full[l], wpool_bd[l], pool_scale[l][None], sgu_ln_g[l][None], wsp_t[l], bias_t[l])
        p_in = _mm(dproj, prev_b, "tn", BF16, IN_W, D_MODEL, T, "mm_dw_in")
        dx = _mm(dproj, g_in, "nn", F32, T, 512, IN_W, "mm_dx")
        small[l] = (dwc, dwp, dps, dlng, dwsp, dbias, dg1, db1, dg2, db2)
        dres, dmm = dr1, dx
        if in_flight is not None:
            big[l + 1] += _rs_chip_finish(in_flight, [dx], q_arr, "%da" % (l + 1))
        big[l] = _rs_chip_finish(ffn_flight, [dx], q_arr, "%db" % l)
        in_flight = _rs_begin([p_in, p_o], c_arr, "%da" % l)
    big[0] += _rs_chip_finish(in_flight, [], q_arr, "0a")
    grad_x = _residual_out(dres, dmm)
    big_grads = [jnp.stack([big[l][w] for l in range(L)]) for w in (2, 0, 3, 1)]

    def stack(i):
        return jnp.stack([small[l][i] for l in range(L)])

    dwp_bd = stack(1).reshape(L, 2, 2, HALF, 2, HALF)
    dwp_all = jnp.einsum("ltgchd,gh->ltgcd", dwp_bd, eye2).reshape(L, 4, HALF, HALF)
    dbs_all = jnp.swapaxes(stack(5)[:, :, :, :2], 2, 3).reshape(L, 6, CHUNK)
    small_grads = [stack(0), dwp_all, stack(2).reshape(L, POOL_W), stack(3).reshape(L, SGU_W),
                   stack(4).reshape(L, 6, CHUNK, CHUNK), dbs_all] + [stack(i).reshape(L, D_MODEL) for i in (6, 7, 8, 9)]
    return loss_tile, grad_x, big_grads, small_grads


def _rs_chip_finish(in_flight, after, q, layer):
    send_sems, recv_sems, sums, lands, _ = in_flight
    sums, got = _rs_chip_wait(send_sems, recv_sems, sums, lands, after, layer)
    return [_rs_finish(s, g, q) for s, g in zip(sums, got)]


def _reduce_and_update(big_grads, small_grads, big_w, big_m, big_v, small_w, small_m, small_v):
    L = DEPTH
    mx, my, mc = _my_place()
    dev = 4 * mx + 2 * my + mc
    conv_cols = CONV_W // N_DEV
    w_in, w_gate_up, w_o, w_down = big_w
    m_w_in, m_w_gate_up, m_w_o, m_w_down = big_m
    v_w_in, v_w_gate_up, v_w_o, v_w_down = big_v
    gt_in, gt_gu, g_w_o, g_w_dn = big_grads
    g_w_in = jnp.swapaxes(gt_in, 1, 2)
    g_w_gu = jnp.swapaxes(gt_gu, 1, 2)

    small_shapes = [a.shape for a in small_grads]
    packed_g = _allreduce_small(_pack(small_grads))

    def widen_conv(a):
        return lax.dynamic_update_slice(jnp.zeros((L, 3, CONV_W), F32), a, (0, 0, dev * conv_cols))

    small_m = [widen_conv(small_m[0])] + list(small_m[1:])
    small_v = [widen_conv(small_v[0])] + list(small_v[1:])
    pk_d, pk_m, pk_v = _adamw(_pack(small_w), packed_g, _pack(small_m), _pack(small_v), packed_g.shape[0] // 2)
    sg = _unpack(packed_g, small_shapes)
    sd = _unpack(pk_d, small_shapes)
    sm = _unpack(pk_m, small_shapes)
    sv = _unpack(pk_v, small_shapes)

    def conv_cols_of(a):
        return lax.dynamic_slice(a, (0, 0, dev * conv_cols), (L, 3, conv_cols))

    for lst in (sg, sd, sm, sv):
        lst[0] = conv_cols_of(lst[0])

    def big(w, g, m, v, tr):
        s = w.shape
        d, mn, vn = _adamw(w.reshape(-1, s[-1]), g.reshape(-1, s[-1]), m.reshape(-1, s[-1]), v.reshape(-1, s[-1]), tr)
        return d.reshape(s), mn.reshape(s), vn.reshape(s)

    d_in, m_in, v_in = big(w_in, g_w_in, m_w_in, v_w_in, 512)
    d_gu, m_gu, v_gu = big(w_gate_up, g_w_gu, m_w_gate_up, v_w_gate_up, 512)
    d_o, m_o, v_o = big(w_o, g_w_o, m_w_o, v_w_o, 128)
    d_dn, m_dn, v_dn = big(w_down, g_w_dn, m_w_down, v_w_down, 352)

    def ordered(big_in, big_o, big_gu, big_dn, sm_list):
        return [big_in, sm_list[0], sm_list[1], sm_list[2], sm_list[3], sm_list[4], sm_list[5], big_o,
                sm_list[6], sm_list[7], big_gu, big_dn, sm_list[8], sm_list[9]]

    grads = ordered(g_w_in, g_w_o, g_w_gu, g_w_dn, sg)
    deltas = ordered(d_in, d_o, d_gu, d_dn, sd)
    new_m = ordered(m_in, m_o, m_gu, m_dn, sm)
    new_v = ordered(v_in, v_o, v_gu, v_dn, sv)
    return grads, deltas, new_m, new_v
```

```python
import functools
import math

import jax
import jax.numpy as jnp
from jax import lax
from jax.experimental import pallas as pl
from jax.experimental.pallas import tpu as pltpu

F32 = jnp.float32
BF16 = jnp.bfloat16
MESH = pl.DeviceIdType.MESH

D_MODEL = 1024
DEPTH = 4
CONV_W = 384
POOL_W = 256
SGU_W = 384
IN_W = 3 * CONV_W + POOL_W + 2 * SGU_W
D_FF = 2816
CHUNK = 128
ALPHA = float((2 * DEPTH) ** 0.25)
LN_EPS = 1e-5
ADAM_LR, ADAM_B1, ADAM_B2, ADAM_EPS, ADAM_WD, ADAM_STEP = 0.001, 0.9, 0.999, 1e-08, 0.01, 10

N_DEV = 8
LANES = 128
HALF = 64
SHARD_ROWS = (IN_W // N_DEV, 2 * D_FF // N_DEV, D_MODEL // N_DEV, D_FF // N_DEV)
VMEM_LIMIT = 52 * 1024 * 1024

INV_SQRT2 = 0.7071067811865476
INV_SQRT_2PI = 0.3989422804014327


def _cparams(sem=None, **kw):
    if sem is not None:
        kw["dimension_semantics"] = sem
    return pltpu.CompilerParams(vmem_limit_bytes=VMEM_LIMIT, **kw)


_DN = {"nn": (((1,), (0,)), ((), ())), "nt": (((1,), (1,)), ((), ())), "tn": (((0,), (0,)), ((), ()))}


def _mm(a, b, mode, out_dtype, tm, tn, tk, name, deps=(), out_rows=None, out_off=0, out_into=None):
    if mode == "nn":
        (M, K), N = a.shape, b.shape[1]
    elif mode == "nt":
        (M, K), N = a.shape, b.shape[0]
    else:
        (K, M), N = a.shape, b.shape[1]
    assert M % tm == 0 and N % tn == 0 and K % tk == 0 and out_off % tm == 0, (M, N, K, tm, tn, tk)
    nk = K // tk
    if out_into is not None:
        deps = tuple(deps) + (out_into,)
    nd = len(deps)
    row_off = out_off // tm

    def body(*refs):
        a_ref, b_ref, o_ref = refs[0], refs[1], refs[2 + nd]
        acc_ref = refs[3 + nd] if nk > 1 else None
        p = lax.dot_general(a_ref[...], b_ref[...], _DN[mode], preferred_element_type=F32)
        if nk == 1:
            o_ref[...] = p.astype(o_ref.dtype)
        else:
            k = pl.program_id(2)

            @pl.when(k == 0)
            def _():
                acc_ref[...] = p

            @pl.when(k > 0)
            def _():
                acc_ref[...] += p

            @pl.when(k == nk - 1)
            def _():
                o_ref[...] = acc_ref[...].astype(o_ref.dtype)

    if mode == "nn":
        a_spec = pl.BlockSpec((tm, tk), lambda i, j, k: (i, k))
        b_blk, b_idx = (tk, tn), (lambda i, j, k: (k, j))
    elif mode == "nt":
        a_spec = pl.BlockSpec((tm, tk), lambda i, j, k: (i, k))
        b_blk, b_idx = (tn, tk), (lambda i, j, k: (j, k))
    else:
        a_spec = pl.BlockSpec((tk, tm), lambda i, j, k: (k, i))
        b_blk, b_idx = (tk, tn), (lambda i, j, k: (k, j))
    return pl.pallas_call(
        body,
        name=name,
        grid=(M // tm, N // tn, nk),
        in_specs=[a_spec, pl.BlockSpec(b_blk, b_idx)] + [pl.BlockSpec(memory_space=pl.ANY)] * nd,
        out_specs=pl.BlockSpec((tm, tn), lambda i, j, k: (i + row_off, j)),
        out_shape=jax.ShapeDtypeStruct((out_rows or M, N), out_dtype),
        scratch_shapes=[pltpu.VMEM((tm, tn), F32)] if nk > 1 else [],
        input_output_aliases={1 + nd: 0} if out_into is not None else {},
        compiler_params=_cparams(("parallel", "parallel", "arbitrary")),
    )(a, b, *deps)


LN_TM = 256


def _mm_ln_fwd(a, b, prev, pg, pb, g, bias, name):
    T, K = a.shape
    tm = LN_TM

    def body(a_ref, b_ref, prev_ref, pg_ref, pb_ref, g_ref, bias_ref, xhat_ref, rstd_ref, y_ref):
        mm = jnp.dot(a_ref[...], b_ref[...], preferred_element_type=F32)
        r = ALPHA * (prev_ref[...] * pg_ref[...] + pb_ref[...]) + mm
        mu = jnp.mean(r, axis=-1, keepdims=True)
        xc = r - mu
        var = jnp.mean(xc * xc, axis=-1, keepdims=True)
        rstd = lax.rsqrt(var + LN_EPS)
        xhat = xc * rstd
        xhat_ref[...] = xhat
        rstd_ref[...] = rstd
        y_ref[...] = (xhat * g_ref[...] + bias_ref[...]).astype(y_ref.dtype)

    row = pl.BlockSpec((tm, D_MODEL), lambda i: (i, 0))
    vec = pl.BlockSpec((1, D_MODEL), lambda i: (0, 0))
    return pl.pallas_call(
        body, name=name, grid=(T // tm,),
        in_specs=[pl.BlockSpec((tm, K), lambda i: (i, 0)), pl.BlockSpec((K, D_MODEL), lambda i: (0, 0)),
                  row, vec, vec, vec, vec],
        out_specs=[row, pl.BlockSpec((tm, 1), lambda i: (i, 0)), row],
        out_shape=[jax.ShapeDtypeStruct((T, D_MODEL), F32), jax.ShapeDtypeStruct((T, 1), F32),
                   jax.ShapeDtypeStruct((T, D_MODEL), BF16)],
        compiler_params=_cparams(("parallel",)),
    )(a, b, prev, pg, pb, g, bias)


def _mm_ln_bwd(a_list, b, dres, xhat, rstd, g, name, deps=()):
    T = a_list[0].shape[0]
    tm = LN_TM
    na, nd = len(a_list), len(deps)
    ks = [a.shape[1] for a in a_list]
    last = xhat is None

    def body(*refs):
        a_refs, b_ref, dres_ref = refs[:na], refs[na], refs[na + 1]
        mm, off = None, 0
        for a_ref, k in zip(a_refs, ks):
            part = jnp.dot(a_ref[...], b_ref[off:off + k, :], preferred_element_type=F32)
            mm = part if mm is None else mm + part
            off += k
        dy = ALPHA * dres_ref[...] + mm
        if last:
            refs[-1][...] = dy
            return
        xhat_ref, rstd_ref, g_ref = refs[na + 2:na + 5]
        dr_ref, drb_ref, dg_ref, db_ref = refs[-4:]
        xhat_v = xhat_ref[...]

        @pl.when(pl.program_id(0) == 0)
        def _():
            dg_ref[...] = jnp.zeros_like(dg_ref)
            db_ref[...] = jnp.zeros_like(db_ref)

        dg_ref[...] += jnp.sum(dy * xhat_v, axis=0, keepdims=True)
        db_ref[...] += jnp.sum(dy, axis=0, keepdims=True)
        dxh = dy * g_ref[...]
        m1 = jnp.mean(dxh, axis=-1, keepdims=True)
        m2 = jnp.mean(dxh * xhat_v, axis=-1, keepdims=True)
        dr = rstd_ref[...] * (dxh - m1 - xhat_v * m2)
        dr_ref[...] = dr
        drb_ref[...] = dr.astype(drb_ref.dtype)

    row = pl.BlockSpec((tm, D_MODEL), lambda i: (i, 0))
    vec = pl.BlockSpec((1, D_MODEL), lambda i: (0, 0))
    in_specs = [pl.BlockSpec((tm, k), lambda i: (i, 0)) for k in ks]
    in_specs += [pl.BlockSpec((sum(ks), D_MODEL), lambda i: (0, 0)), row]
    args = list(a_list) + [b, dres]
    if last:
        out_specs, out_shape = row, jax.ShapeDtypeStruct((T, D_MODEL), F32)
    else:
        in_specs += [row, pl.BlockSpec((tm, 1), lambda i: (i, 0)), vec]
        args += [xhat, rstd, g]
        out_specs = [row, row, vec, vec]
        out_shape = [jax.ShapeDtypeStruct((T, D_MODEL), F32), jax.ShapeDtypeStruct((T, D_MODEL), BF16),
                     jax.ShapeDtypeStruct((1, D_MODEL), F32), jax.ShapeDtypeStruct((1, D_MODEL), F32)]
    return pl.pallas_call(
        body, name=name, grid=(T // tm,),
        in_specs=in_specs + [pl.BlockSpec(memory_space=pl.ANY)] * nd,
        out_specs=out_specs, out_shape=out_shape,
        compiler_params=_cparams(("parallel",) if last else ("arbitrary",)),
    )(*args, *deps)


FF_TN = 256


def _mm_swiglu_fwd(h, w_gu, deps=()):
    T = h.shape[0]
    nj = D_FF // FF_TN
    nd = len(deps)

    def body(*refs):
        h_ref, wg_ref, wu_ref = refs[:3]
        g_ref, u_ref, act_ref = refs[3 + nd:]
        hv = h_ref[...]
        gv = lax.dot_general(hv, wg_ref[...], _DN["nt"], preferred_element_type=F32)
        uv = lax.dot_general(hv, wu_ref[...], _DN["nt"], preferred_element_type=F32)
        g_ref[...] = gv
        u_ref[...] = uv
        act_ref[...] = (gv * jax.nn.sigmoid(gv) * uv).astype(act_ref.dtype)

    col = pl.BlockSpec((T, FF_TN), lambda j: (0, j))
    return pl.pallas_call(
        body, name="mm_gate_up_swiglu", grid=(nj,),
        in_specs=[pl.BlockSpec((T, D_MODEL), lambda j: (0, 0)),
                  pl.BlockSpec((FF_TN, D_MODEL), lambda j: (j, 0)),
                  pl.BlockSpec((FF_TN, D_MODEL), lambda j: (j + nj, 0))] + [pl.BlockSpec(memory_space=pl.ANY)] * nd,
        out_specs=[col, col, col],
        out_shape=[jax.ShapeDtypeStruct((T, D_FF), F32), jax.ShapeDtypeStruct((T, D_FF), F32),
                   jax.ShapeDtypeStruct((T, D_FF), BF16)],
        compiler_params=_cparams(("parallel",)),
    )(h, w_gu, w_gu, *deps)


def _mm_swiglu_bwd(dr, w_dn, g, u):
    T = dr.shape[0]

    def body(dr_ref, w_ref, g_ref, u_ref, dg_ref, du_ref, act_ref):
        da = lax.dot_general(dr_ref[...], w_ref[...], _DN["nt"], preferred_element_type=F32)
        gv, uv = g_ref[...], u_ref[...]
        s = jax.nn.sigmoid(gv)
        sg = gv * s
        act_ref[...] = (sg * uv).astype(act_ref.dtype)
        du_ref[...] = (da * sg).astype(du_ref.dtype)
        dg_ref[...] = (da * uv * (s * (1.0 + gv * (1.0 - s)))).astype(dg_ref.dtype)

    col = pl.BlockSpec((T, FF_TN), lambda j: (0, j))
    return pl.pallas_call(
        body, name="mm_dact_swiglu", grid=(D_FF // FF_TN,),
        in_specs=[pl.BlockSpec((T, D_MODEL), lambda j: (0, 0)), pl.BlockSpec((FF_TN, D_MODEL), lambda j: (j, 0)),
                  col, col],
        out_specs=[col, col, col],
        out_shape=[jax.ShapeDtypeStruct((T, D_FF), BF16)] * 3,
        compiler_params=_cparams(("parallel",)),
    )(dr, w_dn, g, u)


def _gelu(x):
    return 0.5 * x * (1.0 + lax.erf(x * INV_SQRT2))


def _gelu_grad(x):
    return 0.5 * (1.0 + lax.erf(x * INV_SQRT2)) + x * (jnp.exp(-0.5 * x * x) * INV_SQRT_2PI)


def _shift_down(z, k):
    row = lax.broadcasted_iota(jnp.int32, z.shape, 0)
    return jnp.where(row >= k, pltpu.roll(z, k, 0), 0.0)


def _shift_up(z, k):
    n = z.shape[0]
    row = lax.broadcasted_iota(jnp.int32, z.shape, 0)
    return jnp.where(row < n - k, pltpu.roll(z, n - k, 0), 0.0)


def _lo_mask(shape):
    return lax.broadcasted_iota(jnp.int32, shape, len(shape) - 1) < HALF


def _seg_mean(x, lo):
    a = jnp.sum(jnp.where(lo, x, 0.0), axis=-1, keepdims=True)
    b = jnp.sum(jnp.where(lo, 0.0, x), axis=-1, keepdims=True)
    return jnp.where(lo, a, b) * (1.0 / HALF)


def _pool_windows(first):
    lo = _lo_mask((1, LANES))
    return jnp.where(first, jnp.where(lo, 2.0, 4.0), jnp.where(lo, 8.0, 16.0)), lo


def _pool_mean_minus_token(p, first):
    wl, lo = _pool_windows(first)
    s2 = p + _shift_down(p, 1)
    s4 = s2 + _shift_down(s2, 2)
    s8 = s4 + _shift_down(s4, 4)
    s16 = s8 + _shift_down(s8, 8)
    win = jnp.where(first, jnp.where(lo, s2, s4), jnp.where(lo, s8, s16))
    t1 = (lax.broadcasted_iota(jnp.int32, p.shape, 0) + 1).astype(F32)
    count = jnp.minimum(t1, wl)
    return win / count - p, count


def _tril_keep():
    r = lax.broadcasted_iota(jnp.int32, (2 * CHUNK, CHUNK), 0)
    s = lax.broadcasted_iota(jnp.int32, (2 * CHUNK, CHUNK), 1)
    return s <= (r & (CHUNK - 1))


def _sgu_chunk_fwd(u, v, g, wm, bias, lo):
    ug = _gelu(u)
    vg = _gelu(v)
    mu = _seg_mean(vg, lo)
    xc = vg - mu
    var = _seg_mean(xc * xc, lo)
    rstd = lax.rsqrt(var + LN_EPS)
    vn = xc * rstd
    vh = (vn * g).astype(BF16)
    mm2 = jnp.dot(wm, vh, preferred_element_type=F32)
    mixed = jnp.where(lo, mm2[:CHUNK], mm2[CHUNK:]) + bias
    return ug, vn, rstd, vh, mixed


def _mixer_fwd(proj, wconv, wpool_bd, pscale, lng, wsp, bias):
    T = proj.shape[0]
    nchunk = T // CHUNK

    def body(a_ref, b_ref, c_ref, wc_ref, wp_ref, ps_ref, lng_ref, wsp_ref, bias_ref, o_ref):
        j = pl.program_id(0)

        @pl.when(j < 3)
        def _conv():
            z = c_ref[...] * a_ref[...]
            w = wc_ref[...]
            y = w[0:1] * _shift_down(z, 2) + w[1:2] * _shift_down(z, 1) + w[2:3] * z
            o_ref[...] = (b_ref[...] * y).astype(o_ref.dtype)

        @pl.when((j >= 3) & (j < 5))
        def _pool():
            d, _ = _pool_mean_minus_token(a_ref[...], j == 3)
            y = jnp.dot(d.astype(BF16), wp_ref[...].astype(BF16), preferred_element_type=F32)
            o_ref[...] = (y * ps_ref[...]).astype(o_ref.dtype)

        @pl.when(j >= 5)
        def _sgu():
            lo = _lo_mask((CHUNK, LANES))
            wm = jnp.where(_tril_keep(), wsp_ref[...], 0.0).astype(BF16)
            bias_t = bias_ref[...]
            g = lng_ref[...]

            def chunk(n, carry):
                rows = pl.ds(pl.multiple_of(n * CHUNK, CHUNK), CHUNK)
                ug, _, _, _, mixed = _sgu_chunk_fwd(a_ref[rows, :], b_ref[rows, :], g, wm, bias_t, lo)
                o_ref[rows, :] = (ug * mixed).astype(o_ref.dtype)
                return carry

            lax.fori_loop(0, nchunk, chunk, 0)

    def col(f):
        return lambda j: (0, f(j))

    clip = lambda v, lo, hi: jnp.minimum(jnp.maximum(v, lo), hi)
    return pl.pallas_call(
        body,
        name="mixer_fwd",
        grid=(8,),
        in_specs=[
            pl.BlockSpec((T, LANES), col(lambda j: jnp.where(j < 3, j, jnp.where(j < 5, j + 6, j + 6)))),
            pl.BlockSpec((T, LANES), col(lambda j: jnp.where(j < 3, j + 3, jnp.where(j < 5, 5, j + 9)))),
            pl.BlockSpec((T, LANES), col(lambda j: jnp.where(j < 3, j + 6, 8))),
            pl.BlockSpec((3, LANES), col(lambda j: clip(j, 0, 2))),
            pl.BlockSpec((None, LANES, LANES), lambda j: (clip(j - 3, 0, 1), 0, 0)),
            pl.BlockSpec((1, LANES), col(lambda j: clip(j - 3, 0, 1))),
            pl.BlockSpec((1, LANES), col(lambda j: clip(j - 5, 0, 2))),
            pl.BlockSpec((None, 2 * CHUNK, CHUNK), lambda j: (clip(j - 5, 0, 2), 0, 0)),
            pl.BlockSpec((None, CHUNK, LANES), lambda j: (clip(j - 5, 0, 2), 0, 0)),
        ],
        out_specs=pl.BlockSpec((T, LANES), lambda j: (0, j)),
        out_shape=jax.ShapeDtypeStruct((T, D_MODEL), BF16),
        compiler_params=_cparams(("arbitrary",)),
    )(proj, proj, proj, wconv, wpool_bd, pscale, lng, wsp, bias)


def _mixer_bwd(proj, dmix, wconv, wpool_bd, pscale, lng, wsp, bias):
    T = proj.shape[0]
    nchunk = T // CHUNK

    def body(a_ref, b_ref, c_ref, dm_ref, wc_ref, wp_ref, ps_ref, lng_ref, wsp_ref, bias_ref,
             o_ref, dwc_ref, dwp_ref, dps_ref, dlng_ref, dwsp_ref, dbias_ref, keep1, keep2):
        k = pl.program_id(0)

        @pl.when(k < 3)
        def _conv():
            xa, gb, gc, dya = a_ref[...], b_ref[...], c_ref[...], dm_ref[...]
            w = wc_ref[...]
            z = gc * xa
            z1 = _shift_down(z, 1)
            z2 = _shift_down(z, 2)
            y = w[0:1] * z2 + w[1:2] * z1 + w[2:3] * z
            dyv = dya * gb
            dz = w[2:3] * dyv + w[1:2] * _shift_up(dyv, 1) + w[0:1] * _shift_up(dyv, 2)
            dwc_ref[0:1, :] = jnp.sum(dyv * z2, axis=0, keepdims=True)
            dwc_ref[1:2, :] = jnp.sum(dyv * z1, axis=0, keepdims=True)
            dwc_ref[2:3, :] = jnp.sum(dyv * z, axis=0, keepdims=True)
            o_ref[...] = (dz * gc).astype(o_ref.dtype)
            keep1[k] = (dya * y).astype(keep1.dtype)
            keep1[k + 3] = (dz * xa).astype(keep1.dtype)

        @pl.when((k >= 3) & (k < 9))
        def _emit_gb_gc():
            o_ref[...] = keep1[k - 3]

        @pl.when((k >= 9) & (k < 11))
        def _pool():
            first = k == 9
            p, dyb = a_ref[...], dm_ref[...]
            d, count = _pool_mean_minus_token(p, first)
            w2 = wp_ref[...].astype(BF16)
            db = d.astype(BF16)
            y = jnp.dot(db, w2, preferred_element_type=F32)
            dps_ref[...] = jnp.sum(dyb * y, axis=0, keepdims=True)
            dyv = (dyb * ps_ref[...]).astype(BF16)
            dd = lax.dot_general(dyv, w2, _DN["nt"], preferred_element_type=F32)
            dwp_ref[...] = lax.dot_general(db, dyv, _DN["tn"], preferred_element_type=F32)
            dwin = dd / count
            a2 = dwin + _shift_up(dwin, 1)
            a4 = a2 + _shift_up(a2, 2)
            a8 = a4 + _shift_up(a4, 4)
            a16 = a8 + _shift_up(a8, 8)
            _, lo = _pool_windows(first)
            back = jnp.where(first, jnp.where(lo, a2, a4), jnp.where(lo, a8, a16))
            o_ref[...] = (back - dd).astype(o_ref.dtype)

        @pl.when((k >= 11) & (k < 14))
        def _sgu():
            lo = _lo_mask((CHUNK, LANES))
            keep = _tril_keep()
            wm = jnp.where(keep, wsp_ref[...], 0.0).astype(BF16)
            bias_t = bias_ref[...]
            g = lng_ref[...]
            dwsp_ref[...] = jnp.zeros_like(dwsp_ref)
            dbias_ref[...] = jnp.zeros_like(dbias_ref)
            dlng_ref[...] = jnp.zeros_like(dlng_ref)

            def chunk(n, carry):
                rows = pl.ds(pl.multiple_of(n * CHUNK, CHUNK), CHUNK)
                u, v, dyc = a_ref[rows, :], b_ref[rows, :], dm_ref[rows, :]
                ug, vn, rstd, vh, mixed = _sgu_chunk_fwd(u, v, g, wm, bias_t, lo)
                dmx = dyc * ug
                o_ref[rows, :] = (dyc * mixed * _gelu_grad(u)).astype(o_ref.dtype)
                dbias_ref[...] += dmx
                dst = jnp.concatenate([jnp.where(lo, dmx, 0.0), jnp.where(lo, 0.0, dmx)], axis=0).astype(BF16)
                dwsp_ref[...] += lax.dot_general(dst, vh, _DN["nt"], preferred_element_type=F32)
                dvh = lax.dot_general(wm, dst, _DN["tn"], preferred_element_type=F32)
                dlng_ref[...] += jnp.sum(dvh * vn, axis=0, keepdims=True)
                dvn = dvh * g
                m1 = _seg_mean(dvn, lo)
                m2 = _seg_mean(dvn * vn, lo)
                dvg = rstd * (dvn - m1 - vn * m2)
                keep2[k - 11, rows, :] = (dvg * _gelu_grad(v)).astype(keep2.dtype)
                return carry

            lax.fori_loop(0, nchunk, chunk, 0)
            dwsp_ref[...] = jnp.where(keep, dwsp_ref[...], 0.0)
            dbt = dbias_ref[...]
            lane = lax.broadcasted_iota(jnp.int32, (CHUNK, LANES), 1)
            sa = jnp.sum(jnp.where(lo, dbt, 0.0), axis=-1, keepdims=True)
            sb = jnp.sum(jnp.where(lo, 0.0, dbt), axis=-1, keepdims=True)
            dbias_ref[...] = jnp.where(lane == 0, sa, jnp.where(lane == 1, sb, 0.0))

        @pl.when(k >= 14)
        def _emit_v():
            o_ref[...] = keep2[k - 14]

    def col(f):
        return lambda k: (0, f(k))

    clip = lambda v, lo, hi: jnp.minimum(jnp.maximum(v, lo), hi)
    view_a = lambda k: jnp.where(k < 3, k, jnp.where(k < 9, 2, jnp.where(k < 14, k, 13)))
    view_b = lambda k: jnp.where(k < 3, k + 3, jnp.where(k < 11, 5, jnp.where(k < 14, k + 3, 16)))
    view_c = lambda k: jnp.where(k < 3, k + 6, 8)
    view_dm = lambda k: jnp.where(k < 3, k, jnp.where(k < 9, 2, jnp.where(k < 14, k - 6, 7)))
    return pl.pallas_call(
        body,
        name="mixer_bwd",
        grid=(17,),
        in_specs=[
            pl.BlockSpec((T, LANES), col(view_a)),
            pl.BlockSpec((T, LANES), col(view_b)),
            pl.BlockSpec((T, LANES), col(view_c)),
            pl.BlockSpec((T, LANES), col(view_dm)),
            pl.BlockSpec((3, LANES), col(lambda k: clip(k, 0, 2))),
            pl.BlockSpec((None, LANES, LANES), lambda k: (clip(k - 9, 0, 1), 0, 0)),
            pl.BlockSpec((1, LANES), col(lambda k: clip(k - 9, 0, 1))),
            pl.BlockSpec((1, LANES), col(lambda k: clip(k - 11, 0, 2))),
            pl.BlockSpec((None, 2 * CHUNK, CHUNK), lambda k: (clip(k - 11, 0, 2), 0, 0)),
            pl.BlockSpec((None, CHUNK, LANES), lambda k: (clip(k - 11, 0, 2), 0, 0)),
        ],
        out_specs=[
            pl.BlockSpec((T, LANES), lambda k: (0, k)),
            pl.BlockSpec((3, LANES), col(lambda k: clip(k, 0, 2))),
            pl.BlockSpec((None, LANES, LANES), lambda k: (clip(k - 9, 0, 1), 0, 0)),
            pl.BlockSpec((1, LANES), col(lambda k: clip(k - 9, 0, 1))),
            pl.BlockSpec((1, LANES), col(lambda k: clip(k - 11, 0, 2))),
            pl.BlockSpec((None, 2 * CHUNK, CHUNK), lambda k: (clip(k - 11, 0, 2), 0, 0)),
            pl.BlockSpec((None, CHUNK, LANES), lambda k: (clip(k - 11, 0, 2), 0, 0)),
        ],
        out_shape=[
            jax.ShapeDtypeStruct((T, IN_W), BF16),
            jax.ShapeDtypeStruct((3, CONV_W), F32),
            jax.ShapeDtypeStruct((2, LANES, LANES), F32),
            jax.ShapeDtypeStruct((1, POOL_W), F32),
            jax.ShapeDtypeStruct((1, SGU_W), F32),
            jax.ShapeDtypeStruct((3, 2 * CHUNK, CHUNK), F32),
            jax.ShapeDtypeStruct((3, CHUNK, LANES), F32),
        ],
        scratch_shapes=[pltpu.VMEM((6, T, LANES), BF16), pltpu.VMEM((3, T, LANES), BF16)],
        compiler_params=_cparams(("arbitrary",)),
    )(proj, proj, proj, dmix, wconv, wpool_bd, pscale, lng, wsp, bias)


def _ln_fwd(prev, pg, pb, mmout, g, b, tm=256):
    T = prev.shape[0]

    def body(prev_ref, pg_ref, pb_ref, mm_ref, g_ref, b_ref, xhat_ref, rstd_ref, y_ref):
        r = ALPHA * (prev_ref[...] * pg_ref[...] + pb_ref[...]) + mm_ref[...]
        mu = jnp.mean(r, axis=-1, keepdims=True)
        xc = r - mu
        var = jnp.mean(xc * xc, axis=-1, keepdims=True)
        rstd = lax.rsqrt(var + LN_EPS)
        xhat = xc * rstd
        xhat_ref[...] = xhat
        rstd_ref[...] = rstd
        y_ref[...] = (xhat * g_ref[...] + b_ref[...]).astype(y_ref.dtype)

    row = pl.BlockSpec((tm, D_MODEL), lambda i: (i, 0))
    vec = pl.BlockSpec((1, D_MODEL), lambda i: (0, 0))
    return pl.pallas_call(
        body,
        name="ln_fwd",
        grid=(T // tm,),
        in_specs=[row, vec, vec, row, vec, vec],
        out_specs=[row, pl.BlockSpec((tm, 1), lambda i: (i, 0)), row],
        out_shape=[jax.ShapeDtypeStruct((T, D_MODEL), F32), jax.ShapeDtypeStruct((T, 1), F32),
                   jax.ShapeDtypeStruct((T, D_MODEL), BF16)],
        compiler_params=_cparams(("parallel",)),
    )(prev, pg, pb, mmout, g, b)


def _ln_bwd(dres, dmm, xhat, rstd, g, tm=256, deps=()):
    T = xhat.shape[0]
    has_res = dres is not None
    nd = len(deps)

    def body(*refs):
        refs = refs[:len(refs) - 4 - nd] + refs[len(refs) - 4:]
        if has_res:
            dres_ref, dmm_ref, xhat_ref, rstd_ref, g_ref, dr_ref, drb_ref, dg_ref, db_ref = refs
            dy = ALPHA * dres_ref[...] + dmm_ref[...]
        else:
            dmm_ref, xhat_ref, rstd_ref, g_ref, dr_ref, drb_ref, dg_ref, db_ref = refs
            dy = dmm_ref[...]
        xhat_v = xhat_ref[...]

        @pl.when(pl.program_id(0) == 0)
        def _():
            dg_ref[...] = jnp.zeros_like(dg_ref)
            db_ref[...] = jnp.zeros_like(db_ref)

        dg_ref[...] += jnp.sum(dy * xhat_v, axis=0, keepdims=True)
        db_ref[...] += jnp.sum(dy, axis=0, keepdims=True)
        dxh = dy * g_ref[...]
        m1 = jnp.mean(dxh, axis=-1, keepdims=True)
        m2 = jnp.mean(dxh * xhat_v, axis=-1, keepdims=True)
        dr = rstd_ref[...] * (dxh - m1 - xhat_v * m2)
        dr_ref[...] = dr
        drb_ref[...] = dr.astype(drb_ref.dtype)

    row = pl.BlockSpec((tm, D_MODEL), lambda i: (i, 0))
    vec = pl.BlockSpec((1, D_MODEL), lambda i: (0, 0))
    in_specs = ([row] if has_res else []) + [row, row, pl.BlockSpec((tm, 1), lambda i: (i, 0)), vec]
    in_specs += [pl.BlockSpec(memory_space=pl.ANY)] * nd
    args = ([dres] if has_res else []) + [dmm, xhat, rstd, g] + list(deps)
    return pl.pallas_call(
        body,
        name="ln_bwd_res" if has_res else "ln_bwd",
        grid=(T // tm,),
        in_specs=in_specs,
        out_specs=[row, row, vec, vec],
        out_shape=[jax.ShapeDtypeStruct((T, D_MODEL), F32), jax.ShapeDtypeStruct((T, D_MODEL), BF16),
                   jax.ShapeDtypeStruct((1, D_MODEL), F32), jax.ShapeDtypeStruct((1, D_MODEL), F32)],
        compiler_params=_cparams(("arbitrary",)),
    )(*args)


def _loss_head(xhat, g, b, target, tm=256):
    T = xhat.shape[0]

    def body(xhat_ref, g_ref, b_ref, t_ref, loss_ref, dy_ref):
        err = xhat_ref[...] * g_ref[...] + b_ref[...] - t_ref[...]

        @pl.when(pl.program_id(0) == 0)
        def _():
            loss_ref[...] = jnp.zeros_like(loss_ref)

        part = jnp.sum(jnp.sum(err * err, axis=-1, keepdims=True), axis=0, keepdims=True)
        loss_ref[...] += jnp.broadcast_to(part * (0.5 / D_MODEL), loss_ref.shape)
        dy_ref[...] = err * (1.0 / D_MODEL)

    row = pl.BlockSpec((tm, D_MODEL), lambda i: (i, 0))
    vec = pl.BlockSpec((1, D_MODEL), lambda i: (0, 0))
    return pl.pallas_call(
        body,
        name="loss_head",
        grid=(T // tm,),
        in_specs=[row, vec, vec, row],
        out_specs=[pl.BlockSpec((8, LANES), lambda i: (0, 0)), row],
        out_shape=[jax.ShapeDtypeStruct((8, LANES), F32), jax.ShapeDtypeStruct((T, D_MODEL), F32)],
        compiler_params=_cparams(("arbitrary",)),
    )(xhat, g, b, target)


def _residual_out(dres, dmm, tm=256):
    T = dres.shape[0]

    def body(a_ref, b_ref, o_ref):
        o_ref[...] = ALPHA * a_ref[...] + b_ref[...]

    row = pl.BlockSpec((tm, D_MODEL), lambda i: (i, 0))
    return pl.pallas_call(
        body, name="residual_out", grid=(T // tm,), in_specs=[row, row], out_specs=row,
        out_shape=jax.ShapeDtypeStruct((T, D_MODEL), F32), compiler_params=_cparams(("parallel",)),
    )(dres, dmm)


SW_TC = 1408


def _swiglu_fwd(gu, tm=128):
    T = gu.shape[0]

    def body(gu_ref, o_ref):
        gv = gu_ref[:, :D_FF]
        o_ref[...] = (gv * jax.nn.sigmoid(gv) * gu_ref[:, D_FF:]).astype(o_ref.dtype)

    return pl.pallas_call(
        body, name="swiglu_fwd", grid=(T // tm,),
        in_specs=[pl.BlockSpec((tm, 2 * D_FF), lambda i: (i, 0))],
        out_specs=pl.BlockSpec((tm, D_FF), lambda i: (i, 0)),
        out_shape=jax.ShapeDtypeStruct((T, D_FF), BF16), compiler_params=_cparams(("parallel",)),
    )(gu)


def _swiglu_bwd(gu, dact, tm=128):
    T = gu.shape[0]

    def body(gu_ref, da_ref, dgu_ref, act_ref):
        gv, uv, da = gu_ref[:, :D_FF], gu_ref[:, D_FF:], da_ref[...]
        s = jax.nn.sigmoid(gv)
        sg = gv * s
        act_ref[...] = (sg * uv).astype(act_ref.dtype)
        dgu_ref[:, D_FF:] = (da * sg).astype(dgu_ref.dtype)
        dgu_ref[:, :D_FF] = (da * uv * (s * (1.0 + gv * (1.0 - s)))).astype(dgu_ref.dtype)

    wide = pl.BlockSpec((tm, 2 * D_FF), lambda i: (i, 0))
    half = pl.BlockSpec((tm, D_FF), lambda i: (i, 0))
    return pl.pallas_call(
        body, name="swiglu_bwd", grid=(T // tm,),
        in_specs=[wide, half], out_specs=[wide, half],
        out_shape=[jax.ShapeDtypeStruct((T, 2 * D_FF), BF16), jax.ShapeDtypeStruct((T, D_FF), BF16)],
        compiler_params=_cparams(("parallel",)),
    )(gu, dact)


def _adamw(w, g, m, v, tr):
    R, C = w.shape
    assert R % tr == 0
    c1 = 1.0 - ADAM_B1 ** ADAM_STEP
    c2 = 1.0 - ADAM_B2 ** ADAM_STEP

    def body(w_ref, g_ref, m_ref, v_ref, d_ref, mo_ref, vo_ref):
        gv = g_ref[...]
        mn = ADAM_B1 * m_ref[...] + (1.0 - ADAM_B1) * gv
        vn = ADAM_B2 * v_ref[...] + (1.0 - ADAM_B2) * (gv * gv)
        d_ref[...] = -ADAM_LR * ((mn / c1) / (jnp.sqrt(vn / c2) + ADAM_EPS) + ADAM_WD * w_ref[...])
        mo_ref[...] = mn
        vo_ref[...] = vn

    blk = pl.BlockSpec((tr, C), lambda i: (i, 0))
    return pl.pallas_call(
        body, name="adamw", grid=(R // tr,), in_specs=[blk] * 4, out_specs=[blk] * 3,
        out_shape=[jax.ShapeDtypeStruct((R, C), F32)] * 3, compiler_params=_cparams(("parallel",)),
    )(w, g, m, v)


def _my_place():
    return lax.axis_index("x"), lax.axis_index("y"), lax.axis_index("c")


ANY = pl.BlockSpec(memory_space=pl.ANY)
HBM = pl.BlockSpec(memory_space=pltpu.HBM)
SEM = pl.BlockSpec(memory_space=pltpu.SEMAPHORE)
EFFECT = pltpu.SideEffectType.DATAFLOW_SIDE_EFFECTING


def _in_hbm(a):
    return pltpu.with_memory_space_constraint(a, pltpu.HBM)


def _block_rows(ref, dev):
    r = ref.shape[0] // N_DEV
    start = pl.multiple_of((4 * dev[0] + 2 * dev[1] + dev[2]) * r, 16)
    return ref.at[pl.ds(start, r), :]


def _ag_first_copies(s_refs, land_refs, send_sems, recv_sems, receiving):
    x, y, c = _my_place()
    peers = [(x, y, 1 - c)] + [(*chip, c) for chip in _other_chips(x, y)]
    copies = []
    for k, peer in enumerate(peers):
        block = peer if receiving else (x, y, c)
        copies += [pltpu.make_async_remote_copy(
            src_ref=s_refs[w], dst_ref=_block_rows(land_refs[w], block),
            send_sem=send_sems.at[k * len(s_refs) + w], recv_sem=recv_sems.at[k * len(s_refs) + w],
            device_id=peer, device_id_type=MESH)
            for w in range(len(s_refs))]
    return copies


def _ag_start(shards, layer, after=()):
    nw = len(shards)

    def body(*refs):
        s_refs, land_refs = refs[:nw], refs[nw:2 * nw]
        token = refs[-1]
        sems = 2 * nw + len(after)
        for cp in _ag_first_copies(s_refs, land_refs, refs[sems], refs[sems + 1], False):
            cp.start()
        token[...] = jnp.zeros_like(token)

    lands = [lax.empty((N_DEV * s.shape[0], D_MODEL), BF16) for s in shards]
    out = pl.pallas_call(
        body, name="ag_start_%s" % layer,
        in_specs=[HBM] * (2 * nw) + [ANY] * len(after),
        out_specs=(SEM, SEM, *[HBM] * (2 * nw), pl.BlockSpec(memory_space=pltpu.VMEM)),
        out_shape=(pltpu.SemaphoreType.DMA((4 * nw,)), pltpu.SemaphoreType.DMA((4 * nw,)),
                   *[pltpu.HBM(a.shape, a.dtype) for a in list(shards) + lands],
                   jax.ShapeDtypeStruct((8, LANES), F32)),
        input_output_aliases={i: 2 + i for i in range(2 * nw)},
        compiler_params=pltpu.CompilerParams(has_side_effects=EFFECT),
    )(*[_in_hbm(a) for a in list(shards) + lands], *after)
    return out[0], out[1], out[2:2 + nw], out[2 + nw:2 + 2 * nw], out[-1]


def _ag_wait(send_sems, recv_sems, shards, lands, after, layer):
    nw = len(shards)

    def body(*refs):
        s_refs, land_refs = refs[:nw], refs[nw:2 * nw]
        for cp in _ag_first_copies(s_refs, land_refs, refs[2 * nw], refs[2 * nw + 1], True):
            cp.wait_send()
            cp.wait_recv()

    out = pl.pallas_call(
        body, name="ag_wait_%s" % layer,
        in_specs=[HBM] * (2 * nw) + [SEM, SEM] + [ANY] * len(after),
        out_specs=[HBM] * (2 * nw),
        out_shape=[pltpu.HBM(a.shape, a.dtype) for a in list(shards) + list(lands)],
        input_output_aliases={i: i for i in range(2 * nw)},
        compiler_params=pltpu.CompilerParams(has_side_effects=EFFECT),
    )(*shards, *lands, send_sems, recv_sems, *after)
    return out[:nw], out[nw:]


def _ag_pass_on(shards, lands):
    nw = len(shards)

    def body(*refs):
        s_refs, g_refs = refs[:nw], refs[2 * nw:3 * nw]
        send_sems, recv_sems, local_sems = refs[3 * nw:3 * nw + 3]
        stage = refs[3 * nw + 3:]
        x, y, c = _my_place()
        load = [pltpu.make_async_copy(s_refs[w], stage[w], local_sems.at[w]) for w in range(nw)]
        mine = [pltpu.make_async_copy(stage[w], _block_rows(g_refs[w], (x, y, c)), local_sems.at[w])
                for w in range(nw)]
        for cp in load:
            cp.start()
        sends, arrivals = [], []
        for j, chip in enumerate(_other_chips(x, y)):
            for w in range(nw):
                rows_out = _block_rows(g_refs[w], (*chip, c))
                rows_in = _block_rows(g_refs[w], (*chip, 1 - c))
                sends.append(pltpu.make_async_remote_copy(
                    src_ref=rows_out, dst_ref=rows_out, send_sem=send_sems.at[j, w], recv_sem=recv_sems.at[j, w],
                    device_id=(x, y, 1 - c), device_id_type=MESH))
                arrivals.append(pltpu.make_async_remote_copy(
                    src_ref=rows_in, dst_ref=rows_in, send_sem=send_sems.at[j, w], recv_sem=recv_sems.at[j, w],
                    device_id=(x, y, 1 - c), device_id_type=MESH))
        for cp in sends:
            cp.start()
        for w in range(nw):
            load[w].wait()
            mine[w].start()
        for cp in arrivals:
            cp.wait_recv()
        for cp in sends:
            cp.wait_send()
        for cp in mine:
            cp.wait()

    return pl.pallas_call(
        body, name="ag_pass_on",
        in_specs=[ANY] * (2 * nw), out_specs=[ANY] * nw,
        out_shape=[jax.ShapeDtypeStruct(a.shape, a.dtype) for a in lands],
        input_output_aliases={nw + i: i for i in range(nw)},
        scratch_shapes=[pltpu.SemaphoreType.DMA((3, nw)), pltpu.SemaphoreType.DMA((3, nw)),
                        pltpu.SemaphoreType.DMA((nw,))] + [pltpu.VMEM(s.shape, s.dtype) for s in shards],
        compiler_params=_cparams(),
    )(*shards, *lands)


def _rs_sibling_exchange(parts):
    nw = len(parts)

    def body(*refs):
        p_refs, o_refs = refs[:nw], refs[nw:2 * nw]
        send_sems, recv_sems = refs[2 * nw:]
        x, y, c = _my_place()
        copies = [pltpu.make_async_remote_copy(
            src_ref=p_refs[w].at[:, 1 - c], dst_ref=o_refs[w],
            send_sem=send_sems.at[w], recv_sem=recv_sems.at[w], device_id=(x, y, 1 - c), device_id_type=MESH)
            for w in range(nw)]
        for cp in copies:
            cp.start()
        for cp in copies:
            cp.wait()

    return pl.pallas_call(
        body, name="rs_sibling_exchange",
        in_specs=[ANY] * nw, out_specs=[ANY] * nw,
        out_shape=[jax.ShapeDtypeStruct(p.shape[:1] + p.shape[2:], BF16) for p in parts],
        scratch_shapes=[pltpu.SemaphoreType.DMA((nw,)), pltpu.SemaphoreType.DMA((nw,))],
    )(*parts)


def _rs_chip_sum(part, got, c):
    nxy, _, r, _ = part.shape

    def body(c_ref, p_ref, g_ref, o_ref):
        o_ref[...] = (p_ref[...].astype(F32) + g_ref[...].astype(F32)).astype(o_ref.dtype)

    return pl.pallas_call(
        body, name="rs_chip_sum",
        grid_spec=pltpu.PrefetchScalarGridSpec(
            num_scalar_prefetch=1, grid=(nxy,),
            in_specs=[pl.BlockSpec((None, None, r, D_MODEL), lambda q, c_ref: (q, c_ref[0], 0, 0)),
                      pl.BlockSpec((None, r, D_MODEL), lambda q, c_ref: (q, 0, 0))],
            out_specs=pl.BlockSpec((None, r, D_MODEL), lambda q, c_ref: (q, 0, 0))),
        out_shape=jax.ShapeDtypeStruct(got.shape, BF16),
        compiler_params=_cparams(("parallel",)),
    )(c, part, got)


def _other_chips(x, y):
    return [(1 - x, y), (x, 1 - y), (1 - x, 1 - y)]


def _rs_chip_copies(s_refs, land_refs, send_sems, recv_sems):
    x, y, c = _my_place()
    copies = []
    for k, chip in enumerate(_other_chips(x, y)):
        q = 2 * chip[0] + chip[1]
        copies += [pltpu.make_async_remote_copy(
            src_ref=s_refs[w].at[q], dst_ref=land_refs[w].at[k],
            send_sem=send_sems.at[k * len(s_refs) + w], recv_sem=recv_sems.at[k * len(s_refs) + w],
            device_id=(*chip, c), device_id_type=MESH)
            for w in range(len(s_refs))]
    return copies


def _rs_chip_start(sums, layer):
    nw = len(sums)

    def body(*refs):
        s_refs, land_refs = refs[:nw], refs[nw:2 * nw]
        send_sems, recv_sems = refs[2 * nw], refs[2 * nw + 1]
        token = refs[-1]
        for cp in _rs_chip_copies(s_refs, land_refs, send_sems, recv_sems):
            cp.start()
        token[...] = jnp.zeros_like(token)

    lands = [lax.empty((3,) + s.shape[1:], BF16) for s in sums]
    out = pl.pallas_call(
        body, name="rs_chip_start_%s" % layer,
        in_specs=[HBM] * (2 * nw),
        out_specs=(SEM, SEM, *[HBM] * (2 * nw), pl.BlockSpec(memory_space=pltpu.VMEM)),
        out_shape=(pltpu.SemaphoreType.DMA((3 * nw,)), pltpu.SemaphoreType.DMA((3 * nw,)),
                   *[pltpu.HBM(a.shape, a.dtype) for a in list(sums) + lands],
                   jax.ShapeDtypeStruct((8, LANES), F32)),
        input_output_aliases={i: 2 + i for i in range(2 * nw)},
        compiler_params=pltpu.CompilerParams(has_side_effects=EFFECT),
    )(*[_in_hbm(a) for a in list(sums) + lands])
    return out[0], out[1], out[2:2 + nw], out[2 + nw:2 + 2 * nw], out[-1]


def _rs_chip_wait(send_sems, recv_sems, sums, lands, after, layer):
    nw = len(sums)

    def body(*refs):
        s_refs, land_refs = refs[:nw], refs[nw:2 * nw]
        for cp in _rs_chip_copies(s_refs, land_refs, refs[2 * nw], refs[2 * nw + 1]):
            cp.wait_send()
            cp.wait_recv()

    out = pl.pallas_call(
        body, name="rs_chip_wait_%s" % layer,
        in_specs=[HBM] * (2 * nw) + [SEM, SEM] + [ANY] * len(after),
        out_specs=[HBM] * (2 * nw),
        out_shape=[pltpu.HBM(a.shape, a.dtype) for a in list(sums) + list(lands)],
        input_output_aliases={i: i for i in range(2 * nw)},
        compiler_params=pltpu.CompilerParams(has_side_effects=EFFECT),
    )(*sums, *lands, send_sems, recv_sems, *after)
    return out[:nw], out[nw:]


def _rs_finish(sums, got, q):
    _, r, _ = sums.shape

    def body(q_ref, s_ref, g_ref, o_ref):
        o_ref[...] = ((s_ref[...].astype(F32) + g_ref[0].astype(F32)) + g_ref[1].astype(F32)) + g_ref[2].astype(F32)

    return pl.pallas_call(
        body, name="rs_finish",
        grid_spec=pltpu.PrefetchScalarGridSpec(
            num_scalar_prefetch=1, grid=(1,),
            in_specs=[pl.BlockSpec((None, r, D_MODEL), lambda i, q_ref: (q_ref[0], 0, 0)),
                      pl.BlockSpec((3, r, D_MODEL), lambda i, q_ref: (0, 0, 0))],
            out_specs=pl.BlockSpec((r, D_MODEL), lambda i, q_ref: (0, 0))),
        out_shape=jax.ShapeDtypeStruct((r, D_MODEL), F32),
        compiler_params=_cparams(("arbitrary",)),
    )(q, sums, got)


def _allreduce_small(vec):
    R = vec.shape[0]
    assert R % (8 * N_DEV) == 0
    P = R // N_DEV

    def body(v_ref, o_ref, buf, send1, recv1, send2, recv2):
        x, y, c = _my_place()
        me = 4 * x + 2 * y + c

        def piece(ref, d):
            return ref.at[pl.ds(pl.multiple_of(d * P, 8), P), :]

        def peer(k):
            p = me ^ k
            return p, (p >> 2, (p >> 1) & 1, p & 1)

        scatter = []
        for k in range(1, N_DEV):
            p, where = peer(k)
            scatter.append(pltpu.make_async_remote_copy(
                src_ref=piece(v_ref, p), dst_ref=buf.at[k], send_sem=send1.at[k - 1], recv_sem=recv1.at[k - 1],
                device_id=where, device_id_type=MESH))
        for cp in scatter:
            cp.start()
        buf[0] = piece(v_ref, me)[...]
        for cp in scatter:
            cp.wait()
        acc = buf[me]
        for d in range(1, N_DEV):
            acc = acc + buf[me ^ d]
        piece(o_ref, me)[...] = acc
        spread, arrivals = [], []
        for k in range(1, N_DEV):
            p, where = peer(k)
            spread.append(pltpu.make_async_remote_copy(
                src_ref=piece(o_ref, me), dst_ref=piece(o_ref, me), send_sem=send2.at[k - 1], recv_sem=recv2.at[k - 1],
                device_id=where, device_id_type=MESH))
            arrivals.append(pltpu.make_async_remote_copy(
                src_ref=piece(o_ref, p), dst_ref=piece(o_ref, p), send_sem=send2.at[k - 1], recv_sem=recv2.at[k - 1],
                device_id=where, device_id_type=MESH))
        for cp in spread:
            cp.start()
        for cp in arrivals:
            cp.wait_recv()
        for cp in spread:
            cp.wait_send()

    sems = pltpu.SemaphoreType.DMA((N_DEV - 1,))
    return pl.pallas_call(
        body, name="allreduce_small",
        in_specs=[pl.BlockSpec(memory_space=pltpu.VMEM)], out_specs=pl.BlockSpec(memory_space=pltpu.VMEM),
        out_shape=jax.ShapeDtypeStruct((R, LANES), F32),
        scratch_shapes=[pltpu.VMEM((N_DEV, P, LANES), F32), sems, sems, sems, sems],
        compiler_params=_cparams(),
    )(vec)


def _pack(arrs):
    flat = jnp.concatenate([a.reshape(-1) for a in arrs])
    pad = (-flat.shape[0]) % (8 * N_DEV * LANES)
    return jnp.pad(flat, (0, pad)).reshape(-1, LANES)


def _unpack(packed, shapes):
    flat = packed.reshape(-1)
    out, off = [], 0
    for s in shapes:
        n = math.prod(s)
        out.append(flat[off:off + n].reshape(s))
        off += n
    return out


def kernel(x, w_in, w_conv, w_pool, pool_scale, sgu_ln_g, w_spatial, b_spatial, w_o, ln1_g, ln1_b, w_gate_up, w_down, ln2_g, ln2_b, loss_target, m_w_in, m_w_conv, m_w_pool, m_pool_scale, m_sgu_ln_g, m_w_spatial, m_b_spatial, m_w_o, m_ln1_g, m_ln1_b, m_w_gate_up, m_w_down, m_ln2_g, m_ln2_b, v_w_in, v_w_conv, v_w_pool, v_pool_scale, v_sgu_ln_g, v_w_spatial, v_b_spatial, v_w_o, v_ln1_g, v_ln1_b, v_w_gate_up, v_w_down, v_ln2_g, v_ln2_b):
    L = DEPTH
    T = x.shape[1]
    mx, my, mc = _my_place()
    dev = 4 * mx + 2 * my + mc
    xs = x[0]
    target = loss_target[0]

    shards = (jnp.swapaxes(w_in, 1, 2).astype(BF16), jnp.swapaxes(w_gate_up, 1, 2).astype(BF16),
              w_o.astype(BF16), w_down.astype(BF16))
    first_gather = _ag_start_layer(shards, 0, [])

    conv_cols = w_conv.shape[2]
    w_conv_z = lax.dynamic_update_slice(jnp.zeros((L, 3, CONV_W), F32), w_conv, (0, 0, dev * conv_cols))
    w_conv_full = _allreduce_small(_pack([w_conv_z]))
    w_conv_full = _unpack(w_conv_full, [(L, 3, CONV_W)])[0]

    loss_tile, grad_x2, big_grads, small_grads = _local_step(
        xs, target, shards, first_gather, w_conv_full, w_pool, pool_scale, sgu_ln_g, w_spatial, b_spatial,
        ln1_g, ln1_b, ln2_g, ln2_b)
    loss = lax.psum(loss_tile[0, 0], ("x", "y", "c"))
    grad_x = grad_x2[None]
    big_w = (w_in, w_gate_up, w_o, w_down)
    big_m = (m_w_in, m_w_gate_up, m_w_o, m_w_down)
    big_v = (v_w_in, v_w_gate_up, v_w_o, v_w_down)
    small_w = [w_conv_full, w_pool, pool_scale, sgu_ln_g, w_spatial, b_spatial, ln1_g, ln1_b, ln2_g, ln2_b]
    small_m = [m_w_conv, m_w_pool, m_pool_scale, m_sgu_ln_g, m_w_spatial, m_b_spatial, m_ln1_g, m_ln1_b, m_ln2_g, m_ln2_b]
    small_v = [v_w_conv, v_w_pool, v_pool_scale, v_sgu_ln_g, v_w_spatial, v_b_spatial, v_ln1_g, v_ln1_b, v_ln2_g, v_ln2_b]
    grads, deltas, new_m, new_v = _reduce_and_update(
        big_grads, small_grads, big_w, big_m, big_v, small_w, small_m, small_v)
    return (loss, grad_x, *grads, *deltas, *new_m, *new_v)


def _ag_start_layer(shards, l, after):
    s_in, s_gu, s_o, s_dn = [s[l] for s in shards]
    first = _ag_start([s_in, s_o], "%da" % l, after=after)
    return first, _ag_start([s_gu, s_dn], "%db" % l, after=[first[4]])


def _ag_finish(gather, after, tag):
    send_sems, recv_sems, shards, lands, _ = gather
    shards, lands = _ag_wait(send_sems, recv_sems, shards, lands, after, tag)
    return _ag_pass_on(shards, lands)


def _rs_begin(parts, c_arr, tag):
    parts = [p.reshape(4, 2, p.shape[0] // N_DEV, D_MODEL) for p in parts]
    got = _rs_sibling_exchange(parts)
    return _rs_chip_start([_rs_chip_sum(p, g, c_arr) for p, g in zip(parts, got)], tag)


def _local_step(xs, target, shards, gather, w_conv_full, w_pool, pool_scale, sgu_ln_g, w_spatial, b_spatial,
                ln1_g, ln1_b, ln2_g, ln2_b):
    L = DEPTH
    T = xs.shape[0]
    mx, my, mc = _my_place()
    c_arr = jnp.reshape(mc, (1,)).astype(jnp.int32)
    q_arr = jnp.reshape(2 * mx + my, (1,)).astype(jnp.int32)
    eye2 = jnp.eye(2, dtype=F32)
    wp = w_pool.reshape(L, 2, 2, HALF, HALF)
    wpool_bd = jnp.einsum("ltgcd,gh->ltgchd", wp, eye2).reshape(L, 2, LANES, LANES)
    wsp_t = w_spatial.reshape(L, 3, 2 * CHUNK, CHUNK)
    bias_t = jnp.repeat(jnp.swapaxes(b_spatial.reshape(L, 3, 2, CHUNK), 2, 3), HALF, axis=3)
    ones = jnp.ones((1, D_MODEL), F32)
    zeros = jnp.zeros((1, D_MODEL), F32)

    saved = []
    prev, pg, pb = xs, ones, zeros
    prev_b = xs.astype(BF16)
    weights = []
    for l in range(L):
        g_in, g_o = _ag_finish(gather[0], [] if l == 0 else [prev_b], "%da" % l)
        proj = _mm(prev_b, g_in, "nt", F32, 512, IN_W, D_MODEL, "mm_proj")
        mixcat = _mixer_fwd(proj, w_conv_full[l], wpool_bd[l], pool_scale[l][None], sgu_ln_g[l][None], wsp_t[l], bias_t[l])
        xhat1, rstd1, h_b = _mm_ln_fwd(mixcat, g_o, prev, pg, pb, ln1_g[l][None], ln1_b[l][None], "mm_wo_ln")
        g_gu, g_dn = _ag_finish(gather[1], [h_b], "%db" % l)
        weights.append((g_in, g_gu, g_o, g_dn))
        deps = []
        if l + 1 < L:
            gather = _ag_start_layer(shards, l + 1, [g_gu])
            deps = [gather[1][4]]
        g_act, u_act, act = _mm_swiglu_fwd(h_b, g_gu, deps=deps)
        xhat2, rstd2, y_b = _mm_ln_fwd(act, g_dn, xhat1, ln1_g[l][None], ln1_b[l][None], ln2_g[l][None], ln2_b[l][None],
                                       "mm_down_ln")
        saved.append((prev_b, proj, mixcat, xhat1, rstd1, h_b, g_act, u_act, xhat2, rstd2))
        prev, pg, pb, prev_b = xhat2, ln2_g[l][None], ln2_b[l][None], y_b

    loss_tile, dy = _loss_head(prev, pg, pb, target)

    small = [None] * L
    big = [None] * L
    in_flight = None
    above = None
    for l in reversed(range(L)):
        prev_b, proj, mixcat, xhat1, rstd1, h_b, g_act, u_act, xhat2, rstd2 = saved[l]
        g_in, g_gu, g_o, g_dn = weights[l]
        if above is None:
            dr2, dr2_b, dg2, db2 = _ln_bwd(None, dy, xhat2, rstd2, ln2_g[l][None])
        else:
            dr2, dr2_b, dg2, db2 = _mm_ln_bwd([above[0]], above[1], above[2], xhat2, rstd2, ln2_g[l][None],
                                              "mm_dx_ln", deps=[in_flight[4]])
        dg_b, du_b, act = _mm_swiglu_bwd(dr2_b, g_dn, g_act, u_act)
        p_dn = _mm(act, dr2_b, "tn", BF16, 256, D_MODEL, T, "mm_dw_down")
        p_gu = _mm(dg_b, h_b, "tn", BF16, 256, D_MODEL, T, "mm_dw_gate", out_rows=2 * D_FF)
        p_gu = _mm(du_b, h_b, "tn", BF16, 256, D_MODEL, T, "mm_dw_up", out_rows=2 * D_FF, out_off=D_FF, out_into=p_gu)
        ffn_flight = _rs_begin([p_gu, p_dn], c_arr, "%db" % l)
        dr1, dr1_b, dg1, db1 = _mm_ln_bwd([dg_b, du_b], g_gu, dr2, xhat1, rstd1, ln1_g[l][None], "mm_dh_ln",
                                          deps=[ffn_flight[4]])
        dmix = _mm(dr1_b, g_o, "nt", F32, T, 512, D_MODEL, "mm_dmix")
        p_o = _mm(mixcat, dr1_b, "tn", BF16, 512, D_MODEL, T, "mm_dw_o")
        dproj, dwc, dwp, dps, dlng, dwsp, dbias = _mixer_bwd(
            proj, dmix, w_conv_full[l], wpool_bd[l], pool_scale[l][None], sgu_ln_g[l][None], wsp_t[l], bias_t[l])
        p_in = _mm(dproj, prev_b, "tn", BF16, IN_W, D_MODEL, T, "mm_dw_in")
        small[l] = (dwc, dwp, dps, dlng, dwsp, dbias, dg1, db1, dg2, db2)
        above = (dproj, g_in, dr1)
        if in_flight is not None:
            big[l + 1] += _rs_chip_finish(in_flight, [p_in], q_arr, "%da" % (l + 1))
        big[l] = _rs_chip_finish(ffn_flight, [p_in], q_arr, "%db" % l)
        in_flight = _rs_begin([p_in, p_o], c_arr, "%da" % l)
    grad_x = _mm_ln_bwd([above[0]], above[1], above[2], None, None, None, "mm_dx_out", deps=[in_flight[4]])
    big[0] += _rs_chip_finish(in_flight, [grad_x], q_arr, "0a")
    big_grads = [jnp.stack([big[l][w] for l in range(L)]) for w in (2, 0, 3, 1)]

    def stack(i):
        return jnp.stack([small[l][i] for l in range(L)])

    dwp_bd = stack(1).reshape(L, 2, 2, HALF, 2, HALF)
    dwp_all = jnp.einsum("ltgchd,gh->ltgcd", dwp_bd, eye2).reshape(L, 4, HALF, HALF)
    dbs_all = jnp.swapaxes(stack(5)[:, :, :, :2], 2, 3).reshape(L, 6, CHUNK)
    small_grads = [stack(0), dwp_all, stack(2).reshape(L, POOL_W), stack(3).reshape(L, SGU_W),
                   stack(4).reshape(L, 6, CHUNK, CHUNK), dbs_all] + [stack(i).reshape(L, D_MODEL) for i in (6, 7, 8, 9)]
    return loss_tile, grad_x, big_grads, small_grads


def _rs_chip_finish(in_flight, after, q, layer):
    send_sems, recv_sems, sums, lands, _ = in_flight
    sums, got = _rs_chip_wait(send_sems, recv_sems, sums, lands, after, layer)
    return [_rs_finish(s, g, q) for s, g in zip(sums, got)]


def _reduce_and_update(big_grads, small_grads, big_w, big_m, big_v, small_w, small_m, small_v):
    L = DEPTH
    mx, my, mc = _my_place()
    dev = 4 * mx + 2 * my + mc
    conv_cols = CONV_W // N_DEV
    w_in, w_gate_up, w_o, w_down = big_w
    m_w_in, m_w_gate_up, m_w_o, m_w_down = big_m
    v_w_in, v_w_gate_up, v_w_o, v_w_down = big_v
    gt_in, gt_gu, g_w_o, g_w_dn = big_grads
    g_w_in = jnp.swapaxes(gt_in, 1, 2)
    g_w_gu = jnp.swapaxes(gt_gu, 1, 2)

    small_shapes = [a.shape for a in small_grads]
    packed_g = _allreduce_small(_pack(small_grads))

    def widen_conv(a):
        return lax.dynamic_update_slice(jnp.zeros((L, 3, CONV_W), F32), a, (0, 0, dev * conv_cols))

    small_m = [widen_conv(small_m[0])] + list(small_m[1:])
    small_v = [widen_conv(small_v[0])] + list(small_v[1:])
    pk_d, pk_m, pk_v = _adamw(_pack(small_w), packed_g, _pack(small_m), _pack(small_v), packed_g.shape[0] // 2)
    sg = _unpack(packed_g, small_shapes)
    sd = _unpack(pk_d, small_shapes)
    sm = _unpack(pk_m, small_shapes)
    sv = _unpack(pk_v, small_shapes)

    def conv_cols_of(a):
        return lax.dynamic_slice(a, (0, 0, dev * conv_cols), (L, 3, conv_cols))

    for lst in (sg, sd, sm, sv):
        lst[0] = conv_cols_of(lst[0])

    def big(w, g, m, v, tr):
        s = w.shape
        d, mn, vn = _adamw(w.reshape(-1, s[-1]), g.reshape(-1, s[-1]), m.reshape(-1, s[-1]), v.reshape(-1, s[-1]), tr)
        return d.reshape(s), mn.reshape(s), vn.reshape(s)

    d_in, m_in, v_in = big(w_in, g_w_in, m_w_in, v_w_in, 512)
    d_gu, m_gu, v_gu = big(w_gate_up, g_w_gu, m_w_gate_up, v_w_gate_up, 512)
    d_o, m_o, v_o = big(w_o, g_w_o, m_w_o, v_w_o, 128)
    d_dn, m_dn, v_dn = big(w_down, g_w_dn, m_w_down, v_w_down, 352)

    def ordered(big_in, big_o, big_gu, big_dn, sm_list):
        return [big_in, sm_list[0], sm_list[1], sm_list[2], sm_list[3], sm_list[4], sm_list[5], big_o,
                sm_list[6], sm_list[7], big_gu, big_dn, sm_list[8], sm_list[9]]

    grads = ordered(g_w_in, g_w_o, g_w_gu, g_w_dn, sg)
    deltas = ordered(d_in, d_o, d_gu, d_dn, sd)
    new_m = ordered(m_in, m_o, m_gu, m_dn, sm)
    new_v = ordered(v_in, v_o, v_gu, v_dn, sv)
    return grads, deltas, new_m, new_v
```

```python
import functools
import math

import jax
import jax.numpy as jnp
from jax import lax
from jax.experimental import pallas as pl
from jax.experimental.pallas import tpu as pltpu

F32 = jnp.float32
BF16 = jnp.bfloat16
MESH = pl.DeviceIdType.MESH

D_MODEL = 1024
DEPTH = 4
CONV_W = 384
POOL_W = 256
SGU_W = 384
IN_W = 3 * CONV_W + POOL_W + 2 * SGU_W
D_FF = 2816
CHUNK = 128
ALPHA = float((2 * DEPTH) ** 0.25)
LN_EPS = 1e-5
ADAM_LR, ADAM_B1, ADAM_B2, ADAM_EPS, ADAM_WD, ADAM_STEP = 0.001, 0.9, 0.999, 1e-08, 0.01, 10

N_DEV = 8
LANES = 128
HALF = 64
SHARD_ROWS = (IN_W // N_DEV, 2 * D_FF // N_DEV, D_MODEL // N_DEV, D_FF // N_DEV)
VMEM_LIMIT = 52 * 1024 * 1024

INV_SQRT2 = 0.7071067811865476
INV_SQRT_2PI = 0.3989422804014327


def _cparams(sem=None, **kw):
    if sem is not None:
        kw["dimension_semantics"] = sem
    return pltpu.CompilerParams(vmem_limit_bytes=VMEM_LIMIT, **kw)


_DN = {"nn": (((1,), (0,)), ((), ())), "nt": (((1,), (1,)), ((), ())), "tn": (((0,), (0,)), ((), ()))}


def _mm(a, b, mode, out_dtype, tm, tn, tk, name, deps=(), out_rows=None, out_off=0, out_into=None):
    if mode == "nn":
        (M, K), N = a.shape, b.shape[1]
    elif mode == "nt":
        (M, K), N = a.shape, b.shape[0]
    else:
        (K, M), N = a.shape, b.shape[1]
    assert M % tm == 0 and N % tn == 0 and K % tk == 0 and out_off % tm == 0, (M, N, K, tm, tn, tk)
    nk = K // tk
    if out_into is not None:
        deps = tuple(deps) + (out_into,)
    nd = len(deps)
    row_off = out_off // tm

    def body(*refs):
        a_ref, b_ref, o_ref = refs[0], refs[1], refs[2 + nd]
        acc_ref = refs[3 + nd] if nk > 1 else None
        p = lax.dot_general(a_ref[...], b_ref[...], _DN[mode], preferred_element_type=F32)
        if nk == 1:
            o_ref[...] = p.astype(o_ref.dtype)
        else:
            k = pl.program_id(2)

            @pl.when(k == 0)
            def _():
                acc_ref[...] = p

            @pl.when(k > 0)
            def _():
                acc_ref[...] += p

            @pl.when(k == nk - 1)
            def _():
                o_ref[...] = acc_ref[...].astype(o_ref.dtype)

    if mode == "nn":
        a_spec = pl.BlockSpec((tm, tk), lambda i, j, k: (i, k))
        b_blk, b_idx = (tk, tn), (lambda i, j, k: (k, j))
    elif mode == "nt":
        a_spec = pl.BlockSpec((tm, tk), lambda i, j, k: (i, k))
        b_blk, b_idx = (tn, tk), (lambda i, j, k: (j, k))
    else:
        a_spec = pl.BlockSpec((tk, tm), lambda i, j, k: (k, i))
        b_blk, b_idx = (tk, tn), (lambda i, j, k: (k, j))
    return pl.pallas_call(
        body,
        name=name,
        grid=(M // tm, N // tn, nk),
        in_specs=[a_spec, pl.BlockSpec(b_blk, b_idx)] + [pl.BlockSpec(memory_space=pl.ANY)] * nd,
        out_specs=pl.BlockSpec((tm, tn), lambda i, j, k: (i + row_off, j)),
        out_shape=jax.ShapeDtypeStruct((out_rows or M, N), out_dtype),
        scratch_shapes=[pltpu.VMEM((tm, tn), F32)] if nk > 1 else [],
        input_output_aliases={1 + nd: 0} if out_into is not None else {},
        compiler_params=_cparams(("parallel", "parallel", "arbitrary")),
    )(a, b, *deps)


LN_TM = 512


def _mm_ln_fwd(a, b, prev, pg, pb, g, bias, name):
    T, K = a.shape
    tm = LN_TM

    def body(a_ref, b_ref, prev_ref, pg_ref, pb_ref, g_ref, bias_ref, xhat_ref, rstd_ref, y_ref):
        mm = jnp.dot(a_ref[...], b_ref[...], preferred_element_type=F32)
        r = ALPHA * (prev_ref[...] * pg_ref[...] + pb_ref[...]) + mm
        mu = jnp.mean(r, axis=-1, keepdims=True)
        xc = r - mu
        var = jnp.mean(xc * xc, axis=-1, keepdims=True)
        rstd = lax.rsqrt(var + LN_EPS)
        xhat = xc * rstd
        xhat_ref[...] = xhat
        rstd_ref[...] = rstd
        y_ref[...] = (xhat * g_ref[...] + bias_ref[...]).astype(y_ref.dtype)

    row = pl.BlockSpec((tm, D_MODEL), lambda i: (i, 0))
    vec = pl.BlockSpec((1, D_MODEL), lambda i: (0, 0))
    return pl.pallas_call(
        body, name=name, grid=(T // tm,),
        in_specs=[pl.BlockSpec((tm, K), lambda i: (i, 0)),
                  pl.BlockSpec((K, D_MODEL), lambda i: (0, 0), pipeline_mode=pl.Buffered(1)),
                  row, vec, vec, vec, vec],
        out_specs=[row, pl.BlockSpec((tm, 1), lambda i: (i, 0)), row],
        out_shape=[jax.ShapeDtypeStruct((T, D_MODEL), F32), jax.ShapeDtypeStruct((T, 1), F32),
                   jax.ShapeDtypeStruct((T, D_MODEL), BF16)],
        compiler_params=_cparams(("parallel",)),
    )(a, b, prev, pg, pb, g, bias)


def _mm_ln_bwd(a_list, b, dres, xhat, rstd, g, name, deps=()):
    T = a_list[0].shape[0]
    tm = LN_TM
    na, nd = len(a_list), len(deps)
    ks = [a.shape[1] for a in a_list]
    last = xhat is None

    def body(*refs):
        a_refs, b_ref, dres_ref = refs[:na], refs[na], refs[na + 1]
        mm, off = None, 0
        for a_ref, k in zip(a_refs, ks):
            part = jnp.dot(a_ref[...], b_ref[off:off + k, :], preferred_element_type=F32)
            mm = part if mm is None else mm + part
            off += k
        dy = ALPHA * dres_ref[...] + mm
        if last:
            refs[-1][...] = dy
            return
        xhat_ref, rstd_ref, g_ref = refs[na + 2:na + 5]
        dr_ref, drb_ref, dg_ref, db_ref = refs[-4:]
        xhat_v = xhat_ref[...]

        @pl.when(pl.program_id(0) == 0)
        def _():
            dg_ref[...] = jnp.zeros_like(dg_ref)
            db_ref[...] = jnp.zeros_like(db_ref)

        dg_ref[...] += jnp.sum(dy * xhat_v, axis=0, keepdims=True)
        db_ref[...] += jnp.sum(dy, axis=0, keepdims=True)
        dxh = dy * g_ref[...]
        m1 = jnp.mean(dxh, axis=-1, keepdims=True)
        m2 = jnp.mean(dxh * xhat_v, axis=-1, keepdims=True)
        dr = rstd_ref[...] * (dxh - m1 - xhat_v * m2)
        dr_ref[...] = dr
        drb_ref[...] = dr.astype(drb_ref.dtype)

    row = pl.BlockSpec((tm, D_MODEL), lambda i: (i, 0))
    vec = pl.BlockSpec((1, D_MODEL), lambda i: (0, 0))
    in_specs = [pl.BlockSpec((tm, k), lambda i: (i, 0)) for k in ks]
    in_specs += [pl.BlockSpec((sum(ks), D_MODEL), lambda i: (0, 0), pipeline_mode=pl.Buffered(1)), row]
    args = list(a_list) + [b, dres]
    if last:
        out_specs, out_shape = row, jax.ShapeDtypeStruct((T, D_MODEL), F32)
    else:
        in_specs += [row, pl.BlockSpec((tm, 1), lambda i: (i, 0)), vec]
        args += [xhat, rstd, g]
        out_specs = [row, row, vec, vec]
        out_shape = [jax.ShapeDtypeStruct((T, D_MODEL), F32), jax.ShapeDtypeStruct((T, D_MODEL), BF16),
                     jax.ShapeDtypeStruct((1, D_MODEL), F32), jax.ShapeDtypeStruct((1, D_MODEL), F32)]
    return pl.pallas_call(
        body, name=name, grid=(T // tm,),
        in_specs=in_specs + [pl.BlockSpec(memory_space=pl.ANY)] * nd,
        out_specs=out_specs, out_shape=out_shape,
        compiler_params=_cparams(("parallel",) if last else ("arbitrary",)),
    )(*args, *deps)


FF_TN = 256


def _mm_swiglu_fwd(h, w_gu, deps=()):
    T = h.shape[0]
    nj = D_FF // FF_TN
    nd = len(deps)

    def body(*refs):
        h_ref, wg_ref, wu_ref = refs[:3]
        g_ref, u_ref, act_ref = refs[3 + nd:]
        hv = h_ref[...]
        gv = lax.dot_general(hv, wg_ref[...], _DN["nt"], preferred_element_type=F32)
        uv = lax.dot_general(hv, wu_ref[...], _DN["nt"], preferred_element_type=F32)
        g_ref[...] = gv
        u_ref[...] = uv
        act_ref[...] = (gv * jax.nn.sigmoid(gv) * uv).astype(act_ref.dtype)

    col = pl.BlockSpec((T, FF_TN), lambda j: (0, j))
    return pl.pallas_call(
        body, name="mm_gate_up_swiglu", grid=(nj,),
        in_specs=[pl.BlockSpec((T, D_MODEL), lambda j: (0, 0)),
                  pl.BlockSpec((FF_TN, D_MODEL), lambda j: (j, 0)),
                  pl.BlockSpec((FF_TN, D_MODEL), lambda j: (j + nj, 0))] + [pl.BlockSpec(memory_space=pl.ANY)] * nd,
        out_specs=[col, col, col],
        out_shape=[jax.ShapeDtypeStruct((T, D_FF), F32), jax.ShapeDtypeStruct((T, D_FF), F32),
                   jax.ShapeDtypeStruct((T, D_FF), BF16)],
        compiler_params=_cparams(("parallel",)),
    )(h, w_gu, w_gu, *deps)


def _mm_swiglu_bwd(dr, w_dn, g, u):
    T = dr.shape[0]

    def body(dr_ref, w_ref, g_ref, u_ref, dg_ref, du_ref, act_ref):
        da = lax.dot_general(dr_ref[...], w_ref[...], _DN["nt"], preferred_element_type=F32)
        gv, uv = g_ref[...], u_ref[...]
        s = jax.nn.sigmoid(gv)
        sg = gv * s
        act_ref[...] = (sg * uv).astype(act_ref.dtype)
        du_ref[...] = (da * sg).astype(du_ref.dtype)
        dg_ref[...] = (da * uv * (s * (1.0 + gv * (1.0 - s)))).astype(dg_ref.dtype)

    col = pl.BlockSpec((T, FF_TN), lambda j: (0, j))
    return pl.pallas_call(
        body, name="mm_dact_swiglu", grid=(D_FF // FF_TN,),
        in_specs=[pl.BlockSpec((T, D_MODEL), lambda j: (0, 0)), pl.BlockSpec((FF_TN, D_MODEL), lambda j: (j, 0)),
                  col, col],
        out_specs=[col, col, col],
        out_shape=[jax.ShapeDtypeStruct((T, D_FF), BF16)] * 3,
        compiler_params=_cparams(("parallel",)),
    )(dr, w_dn, g, u)


def _gelu(x):
    return 0.5 * x * (1.0 + lax.erf(x * INV_SQRT2))


def _gelu_grad(x):
    return 0.5 * (1.0 + lax.erf(x * INV_SQRT2)) + x * (jnp.exp(-0.5 * x * x) * INV_SQRT_2PI)


def _shift_down(z, k):
    row = lax.broadcasted_iota(jnp.int32, z.shape, 0)
    return jnp.where(row >= k, pltpu.roll(z, k, 0), 0.0)


def _shift_up(z, k):
    n = z.shape[0]
    row = lax.broadcasted_iota(jnp.int32, z.shape, 0)
    return jnp.where(row < n - k, pltpu.roll(z, n - k, 0), 0.0)


def _lo_mask(shape):
    return lax.broadcasted_iota(jnp.int32, shape, len(shape) - 1) < HALF


def _seg_mean(x, lo):
    a = jnp.sum(jnp.where(lo, x, 0.0), axis=-1, keepdims=True)
    b = jnp.sum(jnp.where(lo, 0.0, x), axis=-1, keepdims=True)
    return jnp.where(lo, a, b) * (1.0 / HALF)


def _pool_windows(first):
    lo = _lo_mask((1, LANES))
    return jnp.where(first, jnp.where(lo, 2.0, 4.0), jnp.where(lo, 8.0, 16.0)), lo


def _pool_mean_minus_token(p, first):
    wl, lo = _pool_windows(first)
    s2 = p + _shift_down(p, 1)
    s4 = s2 + _shift_down(s2, 2)
    s8 = s4 + _shift_down(s4, 4)
    s16 = s8 + _shift_down(s8, 8)
    win = jnp.where(first, jnp.where(lo, s2, s4), jnp.where(lo, s8, s16))
    t1 = (lax.broadcasted_iota(jnp.int32, p.shape, 0) + 1).astype(F32)
    count = jnp.minimum(t1, wl)
    return win / count - p, count


SGU_UNROLL = 4


def _tril_keep():
    r = lax.broadcasted_iota(jnp.int32, (2 * CHUNK, CHUNK), 0)
    s = lax.broadcasted_iota(jnp.int32, (2 * CHUNK, CHUNK), 1)
    return s <= (r & (CHUNK - 1))


def _sgu_chunk_fwd(u, v, g, wm, bias, lo):
    ug = _gelu(u)
    vg = _gelu(v)
    mu = _seg_mean(vg, lo)
    xc = vg - mu
    var = _seg_mean(xc * xc, lo)
    rstd = lax.rsqrt(var + LN_EPS)
    vn = xc * rstd
    vh = (vn * g).astype(BF16)
    mm2 = jnp.dot(wm, vh, preferred_element_type=F32)
    mixed = jnp.where(lo, mm2[:CHUNK], mm2[CHUNK:]) + bias
    return ug, vn, rstd, vh, mixed


def _mixer_fwd(proj, wconv, wpool_bd, pscale, lng, wsp, bias):
    T = proj.shape[0]
    nchunk = T // CHUNK

    def body(a_ref, b_ref, c_ref, wc_ref, wp_ref, ps_ref, lng_ref, wsp_ref, bias_ref, o_ref):
        j = pl.program_id(0)

        @pl.when(j < 3)
        def _conv():
            z = c_ref[...] * a_ref[...]
            w = wc_ref[...]
            y = w[0:1] * _shift_down(z, 2) + w[1:2] * _shift_down(z, 1) + w[2:3] * z
            o_ref[...] = (b_ref[...] * y).astype(o_ref.dtype)

        @pl.when((j >= 3) & (j < 5))
        def _pool():
            d, _ = _pool_mean_minus_token(a_ref[...], j == 3)
            y = jnp.dot(d.astype(BF16), wp_ref[...].astype(BF16), preferred_element_type=F32)
            o_ref[...] = (y * ps_ref[...]).astype(o_ref.dtype)

        @pl.when(j >= 5)
        def _sgu():
            lo = _lo_mask((CHUNK, LANES))
            wm = jnp.where(_tril_keep(), wsp_ref[...], 0.0).astype(BF16)
            bias_t = bias_ref[...]
            g = lng_ref[...]

            def chunk(n, carry):
                rows = pl.ds(pl.multiple_of(n * CHUNK, CHUNK), CHUNK)
                ug, _, _, _, mixed = _sgu_chunk_fwd(a_ref[rows, :], b_ref[rows, :], g, wm, bias_t, lo)
                o_ref[rows, :] = (ug * mixed).astype(o_ref.dtype)
                return carry

            lax.fori_loop(0, nchunk, chunk, 0, unroll=SGU_UNROLL)

    def col(f):
        return lambda j: (0, f(j))

    clip = lambda v, lo, hi: jnp.minimum(jnp.maximum(v, lo), hi)
    return pl.pallas_call(
        body,
        name="mixer_fwd",
        grid=(8,),
        in_specs=[
            pl.BlockSpec((T, LANES), col(lambda j: jnp.where(j < 3, j, jnp.where(j < 5, j + 6, j + 6)))),
            pl.BlockSpec((T, LANES), col(lambda j: jnp.where(j < 3, j + 3, jnp.where(j < 5, 5, j + 9)))),
            pl.BlockSpec((T, LANES), col(lambda j: jnp.where(j < 3, j + 6, 8))),
            pl.BlockSpec((3, LANES), col(lambda j: clip(j, 0, 2))),
            pl.BlockSpec((None, LANES, LANES), lambda j: (clip(j - 3, 0, 1), 0, 0)),
            pl.BlockSpec((1, LANES), col(lambda j: clip(j - 3, 0, 1))),
            pl.BlockSpec((1, LANES), col(lambda j: clip(j - 5, 0, 2))),
            pl.BlockSpec((None, 2 * CHUNK, CHUNK), lambda j: (clip(j - 5, 0, 2), 0, 0)),
            pl.BlockSpec((None, CHUNK, LANES), lambda j: (clip(j - 5, 0, 2), 0, 0)),
        ],
        out_specs=pl.BlockSpec((T, LANES), lambda j: (0, j)),
        out_shape=jax.ShapeDtypeStruct((T, D_MODEL), BF16),
        compiler_params=_cparams(("arbitrary",)),
    )(proj, proj, proj, wconv, wpool_bd, pscale, lng, wsp, bias)


def _mixer_bwd(proj, dmix, wconv, wpool_bd, pscale, lng, wsp, bias):
    T = proj.shape[0]
    nchunk = T // CHUNK

    def body(a_ref, b_ref, c_ref, dm_ref, wc_ref, wp_ref, ps_ref, lng_ref, wsp_ref, bias_ref,
             o_ref, dwc_ref, dwp_ref, dps_ref, dlng_ref, dwsp_ref, dbias_ref, keep1, keep2):
        k = pl.program_id(0)

        @pl.when(k < 3)
        def _conv():
            xa, gb, gc, dya = a_ref[...], b_ref[...], c_ref[...], dm_ref[...]
            w = wc_ref[...]
            z = gc * xa
            z1 = _shift_down(z, 1)
            z2 = _shift_down(z, 2)
            y = w[0:1] * z2 + w[1:2] * z1 + w[2:3] * z
            dyv = dya * gb
            dz = w[2:3] * dyv + w[1:2] * _shift_up(dyv, 1) + w[0:1] * _shift_up(dyv, 2)
            dwc_ref[0:1, :] = jnp.sum(dyv * z2, axis=0, keepdims=True)
            dwc_ref[1:2, :] = jnp.sum(dyv * z1, axis=0, keepdims=True)
            dwc_ref[2:3, :] = jnp.sum(dyv * z, axis=0, keepdims=True)
            o_ref[...] = (dz * gc).astype(o_ref.dtype)
            keep1[k] = (dya * y).astype(keep1.dtype)
            keep1[k + 3] = (dz * xa).astype(keep1.dtype)

        @pl.when((k >= 3) & (k < 9))
        def _emit_gb_gc():
            o_ref[...] = keep1[k - 3]

        @pl.when((k >= 9) & (k < 11))
        def _pool():
            first = k == 9
            p, dyb = a_ref[...], dm_ref[...]
            d, count = _pool_mean_minus_token(p, first)
            w2 = wp_ref[...].astype(BF16)
            db = d.astype(BF16)
            y = jnp.dot(db, w2, preferred_element_type=F32)
            dps_ref[...] = jnp.sum(dyb * y, axis=0, keepdims=True)
            dyv = (dyb * ps_ref[...]).astype(BF16)
            dd = lax.dot_general(dyv, w2, _DN["nt"], preferred_element_type=F32)
            dwp_ref[...] = lax.dot_general(db, dyv, _DN["tn"], preferred_element_type=F32)
            dwin = dd / count
            a2 = dwin + _shift_up(dwin, 1)
            a4 = a2 + _shift_up(a2, 2)
            a8 = a4 + _shift_up(a4, 4)
            a16 = a8 + _shift_up(a8, 8)
            _, lo = _pool_windows(first)
            back = jnp.where(first, jnp.where(lo, a2, a4), jnp.where(lo, a8, a16))
            o_ref[...] = (back - dd).astype(o_ref.dtype)

        @pl.when((k >= 11) & (k < 14))
        def _sgu():
            lo = _lo_mask((CHUNK, LANES))
            keep = _tril_keep()
            wm = jnp.where(keep, wsp_ref[...], 0.0).astype(BF16)
            bias_t = bias_ref[...]
            g = lng_ref[...]
            dwsp_ref[...] = jnp.zeros_like(dwsp_ref)
            dbias_ref[...] = jnp.zeros_like(dbias_ref)
            dlng_ref[...] = jnp.zeros_like(dlng_ref)

            def chunk(n, carry):
                rows = pl.ds(pl.multiple_of(n * CHUNK, CHUNK), CHUNK)
                u, v, dyc = a_ref[rows, :], b_ref[rows, :], dm_ref[rows, :]
                ug, vn, rstd, vh, mixed = _sgu_chunk_fwd(u, v, g, wm, bias_t, lo)
                dmx = dyc * ug
                o_ref[rows, :] = (dyc * mixed * _gelu_grad(u)).astype(o_ref.dtype)
                dbias_ref[...] += dmx
                dst = jnp.concatenate([jnp.where(lo, dmx, 0.0), jnp.where(lo, 0.0, dmx)], axis=0).astype(BF16)
                dwsp_ref[...] += lax.dot_general(dst, vh, _DN["nt"], preferred_element_type=F32)
                dvh = lax.dot_general(wm, dst, _DN["tn"], preferred_element_type=F32)
                dlng_ref[...] += jnp.sum(dvh * vn, axis=0, keepdims=True)
                dvn = dvh * g
                m1 = _seg_mean(dvn, lo)
                m2 = _seg_mean(dvn * vn, lo)
                dvg = rstd * (dvn - m1 - vn * m2)
                keep2[k - 11, rows, :] = (dvg * _gelu_grad(v)).astype(keep2.dtype)
                return carry

            lax.fori_loop(0, nchunk, chunk, 0, unroll=SGU_UNROLL)
            dwsp_ref[...] = jnp.where(keep, dwsp_ref[...], 0.0)
            dbt = dbias_ref[...]
            lane = lax.broadcasted_iota(jnp.int32, (CHUNK, LANES), 1)
            sa = jnp.sum(jnp.where(lo, dbt, 0.0), axis=-1, keepdims=True)
            sb = jnp.sum(jnp.where(lo, 0.0, dbt), axis=-1, keepdims=True)
            dbias_ref[...] = jnp.where(lane == 0, sa, jnp.where(lane == 1, sb, 0.0))

        @pl.when(k >= 14)
        def _emit_v():
            o_ref[...] = keep2[k - 14]

    def col(f):
        return lambda k: (0, f(k))

    clip = lambda v, lo, hi: jnp.minimum(jnp.maximum(v, lo), hi)
    view_a = lambda k: jnp.where(k < 3, k, jnp.where(k < 9, 2, jnp.where(k < 14, k, 13)))
    view_b = lambda k: jnp.where(k < 3, k + 3, jnp.where(k < 11, 5, jnp.where(k < 14, k + 3, 16)))
    view_c = lambda k: jnp.where(k < 3, k + 6, 8)
    view_dm = lambda k: jnp.where(k < 3, k, jnp.where(k < 9, 2, jnp.where(k < 14, k - 6, 7)))
    return pl.pallas_call(
        body,
        name="mixer_bwd",
        grid=(17,),
        in_specs=[
            pl.BlockSpec((T, LANES), col(view_a)),
            pl.BlockSpec((T, LANES), col(view_b)),
            pl.BlockSpec((T, LANES), col(view_c)),
            pl.BlockSpec((T, LANES), col(view_dm)),
            pl.BlockSpec((3, LANES), col(lambda k: clip(k, 0, 2))),
            pl.BlockSpec((None, LANES, LANES), lambda k: (clip(k - 9, 0, 1), 0, 0)),
            pl.BlockSpec((1, LANES), col(lambda k: clip(k - 9, 0, 1))),
            pl.BlockSpec((1, LANES), col(lambda k: clip(k - 11, 0, 2))),
            pl.BlockSpec((None, 2 * CHUNK, CHUNK), lambda k: (clip(k - 11, 0, 2), 0, 0)),
            pl.BlockSpec((None, CHUNK, LANES), lambda k: (clip(k - 11, 0, 2), 0, 0)),
        ],
        out_specs=[
            pl.BlockSpec((T, LANES), lambda k: (0, k)),
            pl.BlockSpec((3, LANES), col(lambda k: clip(k, 0, 2))),
            pl.BlockSpec((None, LANES, LANES), lambda k: (clip(k - 9, 0, 1), 0, 0)),
            pl.BlockSpec((1, LANES), col(lambda k: clip(k - 9, 0, 1))),
            pl.BlockSpec((1, LANES), col(lambda k: clip(k - 11, 0, 2))),
            pl.BlockSpec((None, 2 * CHUNK, CHUNK), lambda k: (clip(k - 11, 0, 2), 0, 0)),
            pl.BlockSpec((None, CHUNK, LANES), lambda k: (clip(k - 11, 0, 2), 0, 0)),
        ],
        out_shape=[
            jax.ShapeDtypeStruct((T, IN_W), BF16),
            jax.ShapeDtypeStruct((3, CONV_W), F32),
            jax.ShapeDtypeStruct((2, LANES, LANES), F32),
            jax.ShapeDtypeStruct((1, POOL_W), F32),
            jax.ShapeDtypeStruct((1, SGU_W), F32),
            jax.ShapeDtypeStruct((3, 2 * CHUNK, CHUNK), F32),
            jax.ShapeDtypeStruct((3, CHUNK, LANES), F32),
        ],
        scratch_shapes=[pltpu.VMEM((6, T, LANES), BF16), pltpu.VMEM((3, T, LANES), BF16)],
        compiler_params=_cparams(("arbitrary",)),
    )(proj, proj, proj, dmix, wconv, wpool_bd, pscale, lng, wsp, bias)


def _ln_fwd(prev, pg, pb, mmout, g, b, tm=256):
    T = prev.shape[0]

    def body(prev_ref, pg_ref, pb_ref, mm_ref, g_ref, b_ref, xhat_ref, rstd_ref, y_ref):
        r = ALPHA * (prev_ref[...] * pg_ref[...] + pb_ref[...]) + mm_ref[...]
        mu = jnp.mean(r, axis=-1, keepdims=True)
        xc = r - mu
        var = jnp.mean(xc * xc, axis=-1, keepdims=True)
        rstd = lax.rsqrt(var + LN_EPS)
        xhat = xc * rstd
        xhat_ref[...] = xhat
        rstd_ref[...] = rstd
        y_ref[...] = (xhat * g_ref[...] + b_ref[...]).astype(y_ref.dtype)

    row = pl.BlockSpec((tm, D_MODEL), lambda i: (i, 0))
    vec = pl.BlockSpec((1, D_MODEL), lambda i: (0, 0))
    return pl.pallas_call(
        body,
        name="ln_fwd",
        grid=(T // tm,),
        in_specs=[row, vec, vec, row, vec, vec],
        out_specs=[row, pl.BlockSpec((tm, 1), lambda i: (i, 0)), row],
        out_shape=[jax.ShapeDtypeStruct((T, D_MODEL), F32), jax.ShapeDtypeStruct((T, 1), F32),
                   jax.ShapeDtypeStruct((T, D_MODEL), BF16)],
        compiler_params=_cparams(("parallel",)),
    )(prev, pg, pb, mmout, g, b)


def _ln_bwd(dres, dmm, xhat, rstd, g, tm=256, deps=()):
    T = xhat.shape[0]
    has_res = dres is not None
    nd = len(deps)

    def body(*refs):
        refs = refs[:len(refs) - 4 - nd] + refs[len(refs) - 4:]
        if has_res:
            dres_ref, dmm_ref, xhat_ref, rstd_ref, g_ref, dr_ref, drb_ref, dg_ref, db_ref = refs
            dy = ALPHA * dres_ref[...] + dmm_ref[...]
        else:
            dmm_ref, xhat_ref, rstd_ref, g_ref, dr_ref, drb_ref, dg_ref, db_ref = refs
            dy = dmm_ref[...]
        xhat_v = xhat_ref[...]

        @pl.when(pl.program_id(0) == 0)
        def _():
            dg_ref[...] = jnp.zeros_like(dg_ref)
            db_ref[...] = jnp.zeros_like(db_ref)

        dg_ref[...] += jnp.sum(dy * xhat_v, axis=0, keepdims=True)
        db_ref[...] += jnp.sum(dy, axis=0, keepdims=True)
        dxh = dy * g_ref[...]
        m1 = jnp.mean(dxh, axis=-1, keepdims=True)
        m2 = jnp.mean(dxh * xhat_v, axis=-1, keepdims=True)
        dr = rstd_ref[...] * (dxh - m1 - xhat_v * m2)
        dr_ref[...] = dr
        drb_ref[...] = dr.astype(drb_ref.dtype)

    row = pl.BlockSpec((tm, D_MODEL), lambda i: (i, 0))
    vec = pl.BlockSpec((1, D_MODEL), lambda i: (0, 0))
    in_specs = ([row] if has_res else []) + [row, row, pl.BlockSpec((tm, 1), lambda i: (i, 0)), vec]
    in_specs += [pl.BlockSpec(memory_space=pl.ANY)] * nd
    args = ([dres] if has_res else []) + [dmm, xhat, rstd, g] + list(deps)
    return pl.pallas_call(
        body,
        name="ln_bwd_res" if has_res else "ln_bwd",
        grid=(T // tm,),
        in_specs=in_specs,
        out_specs=[row, row, vec, vec],
        out_shape=[jax.ShapeDtypeStruct((T, D_MODEL), F32), jax.ShapeDtypeStruct((T, D_MODEL), BF16),
                   jax.ShapeDtypeStruct((1, D_MODEL), F32), jax.ShapeDtypeStruct((1, D_MODEL), F32)],
        compiler_params=_cparams(("arbitrary",)),
    )(*args)


def _loss_head(xhat, g, b, target, tm=256):
    T = xhat.shape[0]

    def body(xhat_ref, g_ref, b_ref, t_ref, loss_ref, dy_ref):
        err = xhat_ref[...] * g_ref[...] + b_ref[...] - t_ref[...]

        @pl.when(pl.program_id(0) == 0)
        def _():
            loss_ref[...] = jnp.zeros_like(loss_ref)

        part = jnp.sum(jnp.sum(err * err, axis=-1, keepdims=True), axis=0, keepdims=True)
        loss_ref[...] += jnp.broadcast_to(part * (0.5 / D_MODEL), loss_ref.shape)
        dy_ref[...] = err * (1.0 / D_MODEL)

    row = pl.BlockSpec((tm, D_MODEL), lambda i: (i, 0))
    vec = pl.BlockSpec((1, D_MODEL), lambda i: (0, 0))
    return pl.pallas_call(
        body,
        name="loss_head",
        grid=(T // tm,),
        in_specs=[row, vec, vec, row],
        out_specs=[pl.BlockSpec((8, LANES), lambda i: (0, 0)), row],
        out_shape=[jax.ShapeDtypeStruct((8, LANES), F32), jax.ShapeDtypeStruct((T, D_MODEL), F32)],
        compiler_params=_cparams(("arbitrary",)),
    )(xhat, g, b, target)


def _residual_out(dres, dmm, tm=256):
    T = dres.shape[0]

    def body(a_ref, b_ref, o_ref):
        o_ref[...] = ALPHA * a_ref[...] + b_ref[...]

    row = pl.BlockSpec((tm, D_MODEL), lambda i: (i, 0))
    return pl.pallas_call(
        body, name="residual_out", grid=(T // tm,), in_specs=[row, row], out_specs=row,
        out_shape=jax.ShapeDtypeStruct((T, D_MODEL), F32), compiler_params=_cparams(("parallel",)),
    )(dres, dmm)


SW_TC = 1408


def _swiglu_fwd(gu, tm=128):
    T = gu.shape[0]

    def body(gu_ref, o_ref):
        gv = gu_ref[:, :D_FF]
        o_ref[...] = (gv * jax.nn.sigmoid(gv) * gu_ref[:, D_FF:]).astype(o_ref.dtype)

    return pl.pallas_call(
        body, name="swiglu_fwd", grid=(T // tm,),
        in_specs=[pl.BlockSpec((tm, 2 * D_FF), lambda i: (i, 0))],
        out_specs=pl.BlockSpec((tm, D_FF), lambda i: (i, 0)),
        out_shape=jax.ShapeDtypeStruct((T, D_FF), BF16), compiler_params=_cparams(("parallel",)),
    )(gu)


def _swiglu_bwd(gu, dact, tm=128):
    T = gu.shape[0]

    def body(gu_ref, da_ref, dgu_ref, act_ref):
        gv, uv, da = gu_ref[:, :D_FF], gu_ref[:, D_FF:], da_ref[...]
        s = jax.nn.sigmoid(gv)
        sg = gv * s
        act_ref[...] = (sg * uv).astype(act_ref.dtype)
        dgu_ref[:, D_FF:] = (da * sg).astype(dgu_ref.dtype)
        dgu_ref[:, :D_FF] = (da * uv * (s * (1.0 + gv * (1.0 - s)))).astype(dgu_ref.dtype)

    wide = pl.BlockSpec((tm, 2 * D_FF), lambda i: (i, 0))
    half = pl.BlockSpec((tm, D_FF), lambda i: (i, 0))
    return pl.pallas_call(
        body, name="swiglu_bwd", grid=(T // tm,),
        in_specs=[wide, half], out_specs=[wide, half],
        out_shape=[jax.ShapeDtypeStruct((T, 2 * D_FF), BF16), jax.ShapeDtypeStruct((T, D_FF), BF16)],
        compiler_params=_cparams(("parallel",)),
    )(gu, dact)


def _adamw(w, g, m, v, tr):
    R, C = w.shape
    assert R % tr == 0
    c1 = 1.0 - ADAM_B1 ** ADAM_STEP
    c2 = 1.0 - ADAM_B2 ** ADAM_STEP

    def body(w_ref, g_ref, m_ref, v_ref, d_ref, mo_ref, vo_ref):
        gv = g_ref[...]
        mn = ADAM_B1 * m_ref[...] + (1.0 - ADAM_B1) * gv
        vn = ADAM_B2 * v_ref[...] + (1.0 - ADAM_B2) * (gv * gv)
        d_ref[...] = -ADAM_LR * ((mn / c1) / (jnp.sqrt(vn / c2) + ADAM_EPS) + ADAM_WD * w_ref[...])
        mo_ref[...] = mn
        vo_ref[...] = vn

    blk = pl.BlockSpec((tr, C), lambda i: (i, 0))
    return pl.pallas_call(
        body, name="adamw", grid=(R // tr,), in_specs=[blk] * 4, out_specs=[blk] * 3,
        out_shape=[jax.ShapeDtypeStruct((R, C), F32)] * 3, compiler_params=_cparams(("parallel",)),
    )(w, g, m, v)


def _my_place():
    return lax.axis_index("x"), lax.axis_index("y"), lax.axis_index("c")


ANY = pl.BlockSpec(memory_space=pl.ANY)
HBM = pl.BlockSpec(memory_space=pltpu.HBM)
SEM = pl.BlockSpec(memory_space=pltpu.SEMAPHORE)
EFFECT = pltpu.SideEffectType.DATAFLOW_SIDE_EFFECTING


def _in_hbm(a):
    return pltpu.with_memory_space_constraint(a, pltpu.HBM)


def _block_rows(ref, dev):
    r = ref.shape[0] // N_DEV
    start = pl.multiple_of((4 * dev[0] + 2 * dev[1] + dev[2]) * r, 16)
    return ref.at[pl.ds(start, r), :]


def _ag_first_copies(s_refs, land_refs, send_sems, recv_sems, receiving):
    x, y, c = _my_place()
    peers = [(x, y, 1 - c)] + [(*chip, c) for chip in _other_chips(x, y)]
    copies = []
    for k, peer in enumerate(peers):
        block = peer if receiving else (x, y, c)
        copies += [pltpu.make_async_remote_copy(
            src_ref=s_refs[w], dst_ref=_block_rows(land_refs[w], block),
            send_sem=send_sems.at[k * len(s_refs) + w], recv_sem=recv_sems.at[k * len(s_refs) + w],
            device_id=peer, device_id_type=MESH)
            for w in range(len(s_refs))]
    return copies


def _ag_start(shards, layer, after=()):
    nw = len(shards)

    def body(*refs):
        s_refs, land_refs = refs[:nw], refs[nw:2 * nw]
        token = refs[-1]
        sems = 2 * nw + len(after)
        for cp in _ag_first_copies(s_refs, land_refs, refs[sems], refs[sems + 1], False):
            cp.start()
        token[...] = jnp.zeros_like(token)

    lands = [lax.empty((N_DEV * s.shape[0], D_MODEL), BF16) for s in shards]
    out = pl.pallas_call(
        body, name="ag_start_%s" % layer,
        in_specs=[HBM] * (2 * nw) + [ANY] * len(after),
        out_specs=(SEM, SEM, *[HBM] * (2 * nw), pl.BlockSpec(memory_space=pltpu.VMEM)),
        out_shape=(pltpu.SemaphoreType.DMA((4 * nw,)), pltpu.SemaphoreType.DMA((4 * nw,)),
                   *[pltpu.HBM(a.shape, a.dtype) for a in list(shards) + lands],
                   jax.ShapeDtypeStruct((8, LANES), F32)),
        input_output_aliases={i: 2 + i for i in range(2 * nw)},
        compiler_params=pltpu.CompilerParams(has_side_effects=EFFECT),
    )(*[_in_hbm(a) for a in list(shards) + lands], *after)
    return out[0], out[1], out[2:2 + nw], out[2 + nw:2 + 2 * nw], out[-1]


def _ag_wait(send_sems, recv_sems, shards, lands, after, layer):
    nw = len(shards)

    def body(*refs):
        s_refs, land_refs = refs[:nw], refs[nw:2 * nw]
        for cp in _ag_first_copies(s_refs, land_refs, refs[2 * nw], refs[2 * nw + 1], True):
            cp.wait_send()
            cp.wait_recv()

    out = pl.pallas_call(
        body, name="ag_wait_%s" % layer,
        in_specs=[HBM] * (2 * nw) + [SEM, SEM] + [ANY] * len(after),
        out_specs=[HBM] * (2 * nw),
        out_shape=[pltpu.HBM(a.shape, a.dtype) for a in list(shards) + list(lands)],
        input_output_aliases={i: i for i in range(2 * nw)},
        compiler_params=pltpu.CompilerParams(has_side_effects=EFFECT),
    )(*shards, *lands, send_sems, recv_sems, *after)
    return out[:nw], out[nw:]


def _ag_pass_on(shards, lands):
    nw = len(shards)

    def body(*refs):
        s_refs, g_refs = refs[:nw], refs[2 * nw:3 * nw]
        send_sems, recv_sems, local_sems = refs[3 * nw:3 * nw + 3]
        stage = refs[3 * nw + 3:]
        x, y, c = _my_place()
        load = [pltpu.make_async_copy(s_refs[w], stage[w], local_sems.at[w]) for w in range(nw)]
        mine = [pltpu.make_async_copy(stage[w], _block_rows(g_refs[w], (x, y, c)), local_sems.at[w])
                for w in range(nw)]
        for cp in load:
            cp.start()
        sends, arrivals = [], []
        for j, chip in enumerate(_other_chips(x, y)):
            for w in range(nw):
                rows_out = _block_rows(g_refs[w], (*chip, c))
                rows_in = _block_rows(g_refs[w], (*chip, 1 - c))
                sends.append(pltpu.make_async_remote_copy(
                    src_ref=rows_out, dst_ref=rows_out, send_sem=send_sems.at[j, w], recv_sem=recv_sems.at[j, w],
                    device_id=(x, y, 1 - c), device_id_type=MESH))
                arrivals.append(pltpu.make_async_remote_copy(
                    src_ref=rows_in, dst_ref=rows_in, send_sem=send_sems.at[j, w], recv_sem=recv_sems.at[j, w],
                    device_id=(x, y, 1 - c), device_id_type=MESH))
        for cp in sends:
            cp.start()
        for w in range(nw):
            load[w].wait()
            mine[w].start()
        for cp in arrivals:
            cp.wait_recv()
        for cp in sends:
            cp.wait_send()
        for cp in mine:
            cp.wait()

    return pl.pallas_call(
        body, name="ag_pass_on",
        in_specs=[ANY] * (2 * nw), out_specs=[ANY] * nw,
        out_shape=[jax.ShapeDtypeStruct(a.shape, a.dtype) for a in lands],
        input_output_aliases={nw + i: i for i in range(nw)},
        scratch_shapes=[pltpu.SemaphoreType.DMA((3, nw)), pltpu.SemaphoreType.DMA((3, nw)),
                        pltpu.SemaphoreType.DMA((nw,))] + [pltpu.VMEM(s.shape, s.dtype) for s in shards],
        compiler_params=_cparams(),
    )(*shards, *lands)


def _rs_sibling_exchange(parts):
    nw = len(parts)

    def body(*refs):
        p_refs, o_refs = refs[:nw], refs[nw:2 * nw]
        send_sems, recv_sems = refs[2 * nw:]
        x, y, c = _my_place()
        copies = [pltpu.make_async_remote_copy(
            src_ref=p_refs[w].at[:, 1 - c], dst_ref=o_refs[w],
            send_sem=send_sems.at[w], recv_sem=recv_sems.at[w], device_id=(x, y, 1 - c), device_id_type=MESH)
            for w in range(nw)]
        for cp in copies:
            cp.start()
        for cp in copies:
            cp.wait()

    return pl.pallas_call(
        body, name="rs_sibling_exchange",
        in_specs=[ANY] * nw, out_specs=[ANY] * nw,
        out_shape=[jax.ShapeDtypeStruct(p.shape[:1] + p.shape[2:], BF16) for p in parts],
        scratch_shapes=[pltpu.SemaphoreType.DMA((nw,)), pltpu.SemaphoreType.DMA((nw,))],
    )(*parts)


def _rs_chip_sum(parts, gots, c):
    n = len(parts)

    def body(c_ref, *refs):
        for p_ref, g_ref, o_ref in zip(refs[:n], refs[n:2 * n], refs[2 * n:]):
            o_ref[...] = (p_ref[...].astype(F32) + g_ref[...].astype(F32)).astype(o_ref.dtype)

    mine = [pl.BlockSpec((None, None, p.shape[2], D_MODEL), lambda q, c_ref: (q, c_ref[0], 0, 0)) for p in parts]
    theirs = [pl.BlockSpec((None, g.shape[1], D_MODEL), lambda q, c_ref: (q, 0, 0)) for g in gots]
    return pl.pallas_call(
        body, name="rs_chip_sum",
        grid_spec=pltpu.PrefetchScalarGridSpec(
            num_scalar_prefetch=1, grid=(4,), in_specs=mine + theirs, out_specs=theirs),
        out_shape=[jax.ShapeDtypeStruct(g.shape, BF16) for g in gots],
        compiler_params=_cparams(("parallel",)),
    )(c, *parts, *gots)


def _other_chips(x, y):
    return [(1 - x, y), (x, 1 - y), (1 - x, 1 - y)]


def _rs_chip_copies(s_refs, land_refs, send_sems, recv_sems):
    x, y, c = _my_place()
    copies = []
    for k, chip in enumerate(_other_chips(x, y)):
        q = 2 * chip[0] + chip[1]
        copies += [pltpu.make_async_remote_copy(
            src_ref=s_refs[w].at[q], dst_ref=land_refs[w].at[k],
            send_sem=send_sems.at[k * len(s_refs) + w], recv_sem=recv_sems.at[k * len(s_refs) + w],
            device_id=(*chip, c), device_id_type=MESH)
            for w in range(len(s_refs))]
    return copies


def _rs_chip_start(sums, layer):
    nw = len(sums)

    def body(*refs):
        s_refs, land_refs = refs[:nw], refs[nw:2 * nw]
        send_sems, recv_sems = refs[2 * nw], refs[2 * nw + 1]
        token = refs[-1]
        for cp in _rs_chip_copies(s_refs, land_refs, send_sems, recv_sems):
            cp.start()
        token[...] = jnp.zeros_like(token)

    lands = [lax.empty((3,) + s.shape[1:], BF16) for s in sums]
    out = pl.pallas_call(
        body, name="rs_chip_start_%s" % layer,
        in_specs=[HBM] * (2 * nw),
        out_specs=(SEM, SEM, *[HBM] * (2 * nw), pl.BlockSpec(memory_space=pltpu.VMEM)),
        out_shape=(pltpu.SemaphoreType.DMA((3 * nw,)), pltpu.SemaphoreType.DMA((3 * nw,)),
                   *[pltpu.HBM(a.shape, a.dtype) for a in list(sums) + lands],
                   jax.ShapeDtypeStruct((8, LANES), F32)),
        input_output_aliases={i: 2 + i for i in range(2 * nw)},
        compiler_params=pltpu.CompilerParams(has_side_effects=EFFECT),
    )(*[_in_hbm(a) for a in list(sums) + lands])
    return out[0], out[1], out[2:2 + nw], out[2 + nw:2 + 2 * nw], out[-1]


def _rs_chip_wait(send_sems, recv_sems, sums, lands, after, layer):
    nw = len(sums)

    def body(*refs):
        s_refs, land_refs = refs[:nw], refs[nw:2 * nw]
        for cp in _rs_chip_copies(s_refs, land_refs, refs[2 * nw], refs[2 * nw + 1]):
            cp.wait_send()
            cp.wait_recv()

    out = pl.pallas_call(
        body, name="rs_chip_wait_%s" % layer,
        in_specs=[HBM] * (2 * nw) + [SEM, SEM] + [ANY] * len(after),
        out_specs=[HBM] * (2 * nw),
        out_shape=[pltpu.HBM(a.shape, a.dtype) for a in list(sums) + list(lands)],
        input_output_aliases={i: i for i in range(2 * nw)},
        compiler_params=pltpu.CompilerParams(has_side_effects=EFFECT),
    )(*sums, *lands, send_sems, recv_sems, *after)
    return out[:nw], out[nw:]


def _rs_finish(sums, gots, q, layer, into):
    n = len(sums)

    def body(q_ref, *refs):
        for s_ref, g_ref, o_ref in zip(refs[:n], refs[n:2 * n], refs[len(refs) - n:]):
            o_ref[...] = ((s_ref[...].astype(F32) + g_ref[0].astype(F32)) + g_ref[1].astype(F32)) + g_ref[2].astype(F32)

    rows = [s.shape[1] for s in sums]
    in_specs = [pl.BlockSpec((None, r, D_MODEL), lambda i, q_ref: (q_ref[0], 0, 0)) for r in rows]
    in_specs += [pl.BlockSpec((3, r, D_MODEL), lambda i, q_ref: (0, 0, 0)) for r in rows]
    args = [q, *sums, *gots]
    aliases = {}
    if into is not None:
        in_specs += [ANY] * n
        aliases = {len(args) + i: i for i in range(n)}
        args += list(into)
    return pl.pallas_call(
        body, name="rs_finish",
        grid_spec=pltpu.PrefetchScalarGridSpec(
            num_scalar_prefetch=1, grid=(1,), in_specs=in_specs,
            out_specs=[pl.BlockSpec((None, r, D_MODEL), lambda i, q_ref: (layer, 0, 0)) for r in rows]),
        out_shape=[jax.ShapeDtypeStruct((DEPTH, r, D_MODEL), F32) for r in rows],
        input_output_aliases=aliases,
        compiler_params=_cparams(("arbitrary",)),
    )(*args)


def _allreduce_small(vec):
    R = vec.shape[0]
    assert R % (8 * N_DEV) == 0
    P = R // N_DEV

    def body(v_ref, o_ref, buf, send1, recv1, send2, recv2):
        x, y, c = _my_place()
        me = 4 * x + 2 * y + c

        def piece(ref, d):
            return ref.at[pl.ds(pl.multiple_of(d * P, 8), P), :]

        def peer(k):
            p = me ^ k
            return p, (p >> 2, (p >> 1) & 1, p & 1)

        scatter = []
        for k in range(1, N_DEV):
            p, where = peer(k)
            scatter.append(pltpu.make_async_remote_copy(
                src_ref=piece(v_ref, p), dst_ref=buf.at[k], send_sem=send1.at[k - 1], recv_sem=recv1.at[k - 1],
                device_id=where, device_id_type=MESH))
        for cp in scatter:
            cp.start()
        buf[0] = piece(v_ref, me)[...]
        for cp in scatter:
            cp.wait()
        acc = buf[me]
        for d in range(1, N_DEV):
            acc = acc + buf[me ^ d]
        piece(o_ref, me)[...] = acc
        spread, arrivals = [], []
        for k in range(1, N_DEV):
            p, where = peer(k)
            spread.append(pltpu.make_async_remote_copy(
                src_ref=piece(o_ref, me), dst_ref=piece(o_ref, me), send_sem=send2.at[k - 1], recv_sem=recv2.at[k - 1],
                device_id=where, device_id_type=MESH))
            arrivals.append(pltpu.make_async_remote_copy(
                src_ref=piece(o_ref, p), dst_ref=piece(o_ref, p), send_sem=send2.at[k - 1], recv_sem=recv2.at[k - 1],
                device_id=where, device_id_type=MESH))
        for cp in spread:
            cp.start()
        for cp in arrivals:
            cp.wait_recv()
        for cp in spread:
            cp.wait_send()

    sems = pltpu.SemaphoreType.DMA((N_DEV - 1,))
    return pl.pallas_call(
        body, name="allreduce_small",
        in_specs=[pl.BlockSpec(memory_space=pltpu.VMEM)], out_specs=pl.BlockSpec(memory_space=pltpu.VMEM),
        out_shape=jax.ShapeDtypeStruct((R, LANES), F32),
        scratch_shapes=[pltpu.VMEM((N_DEV, P, LANES), F32), sems, sems, sems, sems],
        compiler_params=_cparams(),
    )(vec)


def _pack(arrs):
    flat = jnp.concatenate([a.reshape(-1) for a in arrs])
    pad = (-flat.shape[0]) % (8 * N_DEV * LANES)
    return jnp.pad(flat, (0, pad)).reshape(-1, LANES)


def _unpack(packed, shapes):
    flat = packed.reshape(-1)
    out, off = [], 0
    for s in shapes:
        n = math.prod(s)
        out.append(flat[off:off + n].reshape(s))
        off += n
    return out


def kernel(x, w_in, w_conv, w_pool, pool_scale, sgu_ln_g, w_spatial, b_spatial, w_o, ln1_g, ln1_b, w_gate_up, w_down, ln2_g, ln2_b, loss_target, m_w_in, m_w_conv, m_w_pool, m_pool_scale, m_sgu_ln_g, m_w_spatial, m_b_spatial, m_w_o, m_ln1_g, m_ln1_b, m_w_gate_up, m_w_down, m_ln2_g, m_ln2_b, v_w_in, v_w_conv, v_w_pool, v_pool_scale, v_sgu_ln_g, v_w_spatial, v_b_spatial, v_w_o, v_ln1_g, v_ln1_b, v_w_gate_up, v_w_down, v_ln2_g, v_ln2_b):
    L = DEPTH
    T = x.shape[1]
    mx, my, mc = _my_place()
    dev = 4 * mx + 2 * my + mc
    xs = x[0]
    target = loss_target[0]

    shards = (jnp.swapaxes(w_in, 1, 2).astype(BF16), jnp.swapaxes(w_gate_up, 1, 2).astype(BF16),
              w_o.astype(BF16), w_down.astype(BF16))
    first_gather = _ag_start_layer(shards, 0, [])

    conv_cols = w_conv.shape[2]
    w_conv_z = lax.dynamic_update_slice(jnp.zeros((L, 3, CONV_W), F32), w_conv, (0, 0, dev * conv_cols))
    w_conv_full = _allreduce_small(_pack([w_conv_z]))
    w_conv_full = _unpack(w_conv_full, [(L, 3, CONV_W)])[0]

    loss_tile, grad_x2, big_grads, small_grads = _local_step(
        xs, target, shards, first_gather, w_conv_full, w_pool, pool_scale, sgu_ln_g, w_spatial, b_spatial,
        ln1_g, ln1_b, ln2_g, ln2_b)
    loss = lax.psum(loss_tile[0, 0], ("x", "y", "c"))
    grad_x = grad_x2[None]
    big_w = (w_in, w_gate_up, w_o, w_down)
    big_m = (m_w_in, m_w_gate_up, m_w_o, m_w_down)
    big_v = (v_w_in, v_w_gate_up, v_w_o, v_w_down)
    small_w = [w_conv_full, w_pool, pool_scale, sgu_ln_g, w_spatial, b_spatial, ln1_g, ln1_b, ln2_g, ln2_b]
    small_m = [m_w_conv, m_w_pool, m_pool_scale, m_sgu_ln_g, m_w_spatial, m_b_spatial, m_ln1_g, m_ln1_b, m_ln2_g, m_ln2_b]
    small_v = [v_w_conv, v_w_pool, v_pool_scale, v_sgu_ln_g, v_w_spatial, v_b_spatial, v_ln1_g, v_ln1_b, v_ln2_g, v_ln2_b]
    grads, deltas, new_m, new_v = _reduce_and_update(
        big_grads, small_grads, big_w, big_m, big_v, small_w, small_m, small_v)
    return (loss, grad_x, *grads, *deltas, *new_m, *new_v)


def _ag_start_layer(shards, l, after):
    s_in, s_gu, s_o, s_dn = [s[l] for s in shards]
    first = _ag_start([s_in, s_o], "%da" % l, after=after)
    return first, _ag_start([s_gu, s_dn], "%db" % l, after=[first[4]])


def _ag_finish(gather, after, tag):
    send_sems, recv_sems, shards, lands, _ = gather
    shards, lands = _ag_wait(send_sems, recv_sems, shards, lands, after, tag)
    return _ag_pass_on(shards, lands)


def _rs_begin(parts, c_arr, tag):
    parts = [p.reshape(4, 2, p.shape[0] // N_DEV, D_MODEL) for p in parts]
    return _rs_chip_start(_rs_chip_sum(parts, _rs_sibling_exchange(parts), c_arr), tag)


def _local_step(xs, target, shards, gather, w_conv_full, w_pool, pool_scale, sgu_ln_g, w_spatial, b_spatial,
                ln1_g, ln1_b, ln2_g, ln2_b):
    L = DEPTH
    T = xs.shape[0]
    mx, my, mc = _my_place()
    c_arr = jnp.reshape(mc, (1,)).astype(jnp.int32)
    q_arr = jnp.reshape(2 * mx + my, (1,)).astype(jnp.int32)
    eye2 = jnp.eye(2, dtype=F32)
    wp = w_pool.reshape(L, 2, 2, HALF, HALF)
    wpool_bd = jnp.einsum("ltgcd,gh->ltgchd", wp, eye2).reshape(L, 2, LANES, LANES)
    wsp_t = w_spatial.reshape(L, 3, 2 * CHUNK, CHUNK)
    bias_t = jnp.repeat(jnp.swapaxes(b_spatial.reshape(L, 3, 2, CHUNK), 2, 3), HALF, axis=3)
    ones = jnp.ones((1, D_MODEL), F32)
    zeros = jnp.zeros((1, D_MODEL), F32)

    saved = []
    prev, pg, pb = xs, ones, zeros
    prev_b = xs.astype(BF16)
    weights = []
    for l in range(L):
        g_in, g_o = _ag_finish(gather[0], [] if l == 0 else [prev_b], "%da" % l)
        proj = _mm(prev_b, g_in, "nt", F32, 512, IN_W, D_MODEL, "mm_proj")
        mixcat = _mixer_fwd(proj, w_conv_full[l], wpool_bd[l], pool_scale[l][None], sgu_ln_g[l][None], wsp_t[l], bias_t[l])
        xhat1, rstd1, h_b = _mm_ln_fwd(mixcat, g_o, prev, pg, pb, ln1_g[l][None], ln1_b[l][None], "mm_wo_ln")
        g_gu, g_dn = _ag_finish(gather[1], [h_b], "%db" % l)
        weights.append((g_in, g_gu, g_o, g_dn))
        deps = []
        if l + 1 < L:
            gather = _ag_start_layer(shards, l + 1, [g_gu])
            deps = [gather[1][4]]
        g_act, u_act, act = _mm_swiglu_fwd(h_b, g_gu, deps=deps)
        xhat2, rstd2, y_b = _mm_ln_fwd(act, g_dn, xhat1, ln1_g[l][None], ln1_b[l][None], ln2_g[l][None], ln2_b[l][None],
                                       "mm_down_ln")
        saved.append((prev_b, proj, mixcat, xhat1, rstd1, h_b, g_act, u_act, xhat2, rstd2))
        prev, pg, pb, prev_b = xhat2, ln2_g[l][None], ln2_b[l][None], y_b

    loss_tile, dy = _loss_head(prev, pg, pb, target)

    small = [None] * L
    big = None
    in_flight = None
    above = None
    for l in reversed(range(L)):
        prev_b, proj, mixcat, xhat1, rstd1, h_b, g_act, u_act, xhat2, rstd2 = saved[l]
        g_in, g_gu, g_o, g_dn = weights[l]
        if above is None:
            dr2, dr2_b, dg2, db2 = _ln_bwd(None, dy, xhat2, rstd2, ln2_g[l][None])
        else:
            dr2, dr2_b, dg2, db2 = _mm_ln_bwd([above[0]], above[1], above[2], xhat2, rstd2, ln2_g[l][None],
                                              "mm_dx_ln", deps=[in_flight[4]])
        dg_b, du_b, act = _mm_swiglu_bwd(dr2_b, g_dn, g_act, u_act)
        p_dn = _mm(act, dr2_b, "tn", BF16, 256, D_MODEL, T, "mm_dw_down")
        p_gu = _mm(dg_b, h_b, "tn", BF16, 256, D_MODEL, T, "mm_dw_gate", out_rows=2 * D_FF)
        p_gu = _mm(du_b, h_b, "tn", BF16, 256, D_MODEL, T, "mm_dw_up", out_rows=2 * D_FF, out_off=D_FF, out_into=p_gu)
        ffn_flight = _rs_begin([p_gu, p_dn], c_arr, "0b") if l == 0 else None
        dr1, dr1_b, dg1, db1 = _mm_ln_bwd([dg_b, du_b], g_gu, dr2, xhat1, rstd1, ln1_g[l][None], "mm_dh_ln",
                                          deps=[ffn_flight[4]] if l == 0 else [])
        dmix = _mm(dr1_b, g_o, "nt", F32, T, 512, D_MODEL, "mm_dmix")
        p_o = _mm(mixcat, dr1_b, "tn", BF16, 512, D_MODEL, T, "mm_dw_o")
        dproj, dwc, dwp, dps, dlng, dwsp, dbias = _mixer_bwd(
            proj, dmix, w_conv_full[l], wpool_bd[l], pool_scale[l][None], sgu_ln_g[l][None], wsp_t[l], bias_t[l])
        p_in = _mm(dproj, prev_b, "tn", BF16, IN_W, D_MODEL, T, "mm_dw_in")
        small[l] = (dwc, dwp, dps, dlng, dwsp, dbias, dg1, db1, dg2, db2)
        above = (dproj, g_in, dr1)
        if in_flight is not None:
            big = list(_rs_chip_finish(in_flight, [p_in], q_arr, str(l + 1), l + 1, big))
        if l > 0:
            in_flight = _rs_begin([p_in, p_gu, p_o, p_dn], c_arr, str(l))
        else:
            big[1], big[3] = _rs_chip_finish(ffn_flight, [p_in], q_arr, "0b", 0, [big[1], big[3]])
            in_flight = _rs_begin([p_in, p_o], c_arr, "0a")
    grad_x = _mm_ln_bwd([above[0]], above[1], above[2], None, None, None, "mm_dx_out", deps=[in_flight[4]])
    big[0], big[2] = _rs_chip_finish(in_flight, [grad_x], q_arr, "0a", 0, [big[0], big[2]])
    big_grads = big

    def stack(i):
        return jnp.stack([small[l][i] for l in range(L)])

    dwp_bd = stack(1).reshape(L, 2, 2, HALF, 2, HALF)
    dwp_all = jnp.einsum("ltgchd,gh->ltgcd", dwp_bd, eye2).reshape(L, 4, HALF, HALF)
    dbs_all = jnp.swapaxes(stack(5)[:, :, :, :2], 2, 3).reshape(L, 6, CHUNK)
    small_grads = [stack(0), dwp_all, stack(2).reshape(L, POOL_W), stack(3).reshape(L, SGU_W),
                   stack(4).reshape(L, 6, CHUNK, CHUNK), dbs_all] + [stack(i).reshape(L, D_MODEL) for i in (6, 7, 8, 9)]
    return loss_tile, grad_x, big_grads, small_grads


def _rs_chip_finish(in_flight, after, q, tag, layer, into):
    send_sems, recv_sems, sums, lands, _ = in_flight
    sums, got = _rs_chip_wait(send_sems, recv_sems, sums, lands, after, tag)
    return _rs_finish(sums, got, q, layer, into)


def _reduce_and_update(big_grads, small_grads, big_w, big_m, big_v, small_w, small_m, small_v):
    L = DEPTH
    mx, my, mc = _my_place()
    dev = 4 * mx + 2 * my + mc
    conv_cols = CONV_W // N_DEV
    w_in, w_gate_up, w_o, w_down = big_w
    m_w_in, m_w_gate_up, m_w_o, m_w_down = big_m
    v_w_in, v_w_gate_up, v_w_o, v_w_down = big_v
    gt_in, gt_gu, g_w_o, g_w_dn = big_grads
    g_w_in = jnp.swapaxes(gt_in, 1, 2)
    g_w_gu = jnp.swapaxes(gt_gu, 1, 2)

    small_shapes = [a.shape for a in small_grads]
    packed_g = _allreduce_small(_pack(small_grads))

    def widen_conv(a):
        return lax.dynamic_update_slice(jnp.zeros((L, 3, CONV_W), F32), a, (0, 0, dev * conv_cols))

    small_m = [widen_conv(small_m[0])] + list(small_m[1:])
    small_v = [widen_conv(small_v[0])] + list(small_v[1:])
    pk_d, pk_m, pk_v = _adamw(_pack(small_w), packed_g, _pack(small_m), _pack(small_v), packed_g.shape[0] // 2)
    sg = _unpack(packed_g, small_shapes)
    sd = _unpack(pk_d, small_shapes)
    sm = _unpack(pk_m, small_shapes)
    sv = _unpack(pk_v, small_shapes)

    def conv_cols_of(a):
        return lax.dynamic_slice(a, (0, 0, dev * conv_cols), (L, 3, conv_cols))

    for lst in (sg, sd, sm, sv):
        lst[0] = conv_cols_of(lst[0])

    def big(w, g, m, v, tr):
        s = w.shape
        d, mn, vn = _adamw(w.reshape(-1, s[-1]), g.reshape(-1, s[-1]), m.reshape(-1, s[-1]), v.reshape(-1, s[-1]), tr)
        return d.reshape(s), mn.reshape(s), vn.reshape(s)

    d_in, m_in, v_in = big(w_in, g_w_in, m_w_in, v_w_in, 512)
    d_gu, m_gu, v_gu = big(w_gate_up, g_w_gu, m_w_gate_up, v_w_gate_up, 512)
    d_o, m_o, v_o = big(w_o, g_w_o, m_w_o, v_w_o, 128)
    d_dn, m_dn, v_dn = big(w_down, g_w_dn, m_w_down, v_w_down, 352)

    def ordered(big_in, big_o, big_gu, big_dn, sm_list):
        return [big_in, sm_list[0], sm_list[1], sm_list[2], sm_list[3], sm_list[4], sm_list[5], big_o,
                sm_list[6], sm_list[7], big_gu, big_dn, sm_list[8], sm_list[9]]

    grads = ordered(g_w_in, g_w_o, g_w_gu, g_w_dn, sg)
    deltas = ordered(d_in, d_o, d_gu, d_dn, sd)
    new_m = ordered(m_in, m_o, m_gu, m_dn, sm)
    new_v = ordered(v_in, v_o, v_gu, v_dn, sv)
    return grads, deltas, new_m, new_v
```

```python
import functools
import math

import jax
import jax.numpy as jnp
from jax import lax
from jax.experimental import pallas as pl
from jax.experimental.pallas import tpu as pltpu

F32 = jnp.float32
BF16 = jnp.bfloat16
MESH = pl.DeviceIdType.MESH

D_MODEL = 1024
DEPTH = 4
CONV_W = 384
POOL_W = 256
SGU_W = 384
IN_W = 3 * CONV_W + POOL_W + 2 * SGU_W
D_FF = 2816
CHUNK = 128
ALPHA = float((2 * DEPTH) ** 0.25)
LN_EPS = 1e-5
ADAM_LR, ADAM_B1, ADAM_B2, ADAM_EPS, ADAM_WD, ADAM_STEP = 0.001, 0.9, 0.999, 1e-08, 0.01, 10

N_DEV = 8
LANES = 128
HALF = 64
SHARD_ROWS = (IN_W // N_DEV, 2 * D_FF // N_DEV, D_MODEL // N_DEV, D_FF // N_DEV)
VMEM_LIMIT = 52 * 1024 * 1024

INV_SQRT2 = 0.7071067811865476
INV_SQRT_2PI = 0.3989422804014327


def _cparams(sem=None, **kw):
    if sem is not None:
        kw["dimension_semantics"] = sem
    return pltpu.CompilerParams(vmem_limit_bytes=VMEM_LIMIT, **kw)


_DN = {"nn": (((1,), (0,)), ((), ())), "nt": (((1,), (1,)), ((), ())), "tn": (((0,), (0,)), ((), ()))}


def _mm(a, b, mode, out_dtype, tm, tn, tk, name, deps=(), out_rows=None, out_off=0, out_into=None):
    if mode == "nn":
        (M, K), N = a.shape, b.shape[1]
    elif mode == "nt":
        (M, K), N = a.shape, b.shape[0]
    else:
        (K, M), N = a.shape, b.shape[1]
    assert M % tm == 0 and N % tn == 0 and K % tk == 0 and out_off % tm == 0, (M, N, K, tm, tn, tk)
    nk = K // tk
    if out_into is not None:
        deps = tuple(deps) + (out_into,)
    nd = len(deps)
    row_off = out_off // tm

    def body(*refs):
        a_ref, b_ref, o_ref = refs[0], refs[1], refs[2 + nd]
        acc_ref = refs[3 + nd] if nk > 1 else None
        p = lax.dot_general(a_ref[...], b_ref[...], _DN[mode], preferred_element_type=F32)
        if nk == 1:
            o_ref[...] = p.astype(o_ref.dtype)
        else:
            k = pl.program_id(2)

            @pl.when(k == 0)
            def _():
                acc_ref[...] = p

            @pl.when(k > 0)
            def _():
                acc_ref[...] += p

            @pl.when(k == nk - 1)
            def _():
                o_ref[...] = acc_ref[...].astype(o_ref.dtype)

    if mode == "nn":
        a_spec = pl.BlockSpec((tm, tk), lambda i, j, k: (i, k))
        b_blk, b_idx = (tk, tn), (lambda i, j, k: (k, j))
    elif mode == "nt":
        a_spec = pl.BlockSpec((tm, tk), lambda i, j, k: (i, k))
        b_blk, b_idx = (tn, tk), (lambda i, j, k: (j, k))
    else:
        a_spec = pl.BlockSpec((tk, tm), lambda i, j, k: (k, i))
        b_blk, b_idx = (tk, tn), (lambda i, j, k: (k, j))
    return pl.pallas_call(
        body,
        name=name,
        grid=(M // tm, N // tn, nk),
        in_specs=[a_spec, pl.BlockSpec(b_blk, b_idx)] + [pl.BlockSpec(memory_space=pl.ANY)] * nd,
        out_specs=pl.BlockSpec((tm, tn), lambda i, j, k: (i + row_off, j)),
        out_shape=jax.ShapeDtypeStruct((out_rows or M, N), out_dtype),
        scratch_shapes=[pltpu.VMEM((tm, tn), F32)] if nk > 1 else [],
        input_output_aliases={1 + nd: 0} if out_into is not None else {},
        compiler_params=_cparams(("parallel", "parallel", "arbitrary")),
    )(a, b, *deps)


LN_TM = 512


def _mm_ln_fwd(a, b, prev, pg, pb, g, bias, name):
    T, K = a.shape
    tm = LN_TM

    def body(a_ref, b_ref, prev_ref, pg_ref, pb_ref, g_ref, bias_ref, xhat_ref, rstd_ref, y_ref):
        mm = jnp.dot(a_ref[...], b_ref[...], preferred_element_type=F32)
        r = ALPHA * (prev_ref[...] * pg_ref[...] + pb_ref[...]) + mm
        mu = jnp.mean(r, axis=-1, keepdims=True)
        xc = r - mu
        var = jnp.mean(xc * xc, axis=-1, keepdims=True)
        rstd = lax.rsqrt(var + LN_EPS)
        xhat = xc * rstd
        xhat_ref[...] = xhat
        rstd_ref[...] = rstd
        y_ref[...] = (xhat * g_ref[...] + bias_ref[...]).astype(y_ref.dtype)

    row = pl.BlockSpec((tm, D_MODEL), lambda i: (i, 0))
    vec = pl.BlockSpec((1, D_MODEL), lambda i: (0, 0))
    return pl.pallas_call(
        body, name=name, grid=(T // tm,),
        in_specs=[pl.BlockSpec((tm, K), lambda i: (i, 0)),
                  pl.BlockSpec((K, D_MODEL), lambda i: (0, 0), pipeline_mode=pl.Buffered(1)),
                  row, vec, vec, vec, vec],
        out_specs=[row, pl.BlockSpec((tm, 1), lambda i: (i, 0)), row],
        out_shape=[jax.ShapeDtypeStruct((T, D_MODEL), F32), jax.ShapeDtypeStruct((T, 1), F32),
                   jax.ShapeDtypeStruct((T, D_MODEL), BF16)],
        compiler_params=_cparams(("parallel",)),
    )(a, b, prev, pg, pb, g, bias)


def _mm_ln_bwd(a_list, b, dres, xhat, rstd, g, name, deps=()):
    T = a_list[0].shape[0]
    tm = LN_TM
    na, nd = len(a_list), len(deps)
    ks = [a.shape[1] for a in a_list]
    last = xhat is None

    def body(*refs):
        a_refs, b_ref, dres_ref = refs[:na], refs[na], refs[na + 1]
        mm, off = None, 0
        for a_ref, k in zip(a_refs, ks):
            part = jnp.dot(a_ref[...], b_ref[off:off + k, :], preferred_element_type=F32)
            mm = part if mm is None else mm + part
            off += k
        dy = ALPHA * dres_ref[...] + mm
        if last:
            refs[-1][...] = dy
            return
        xhat_ref, rstd_ref, g_ref = refs[na + 2:na + 5]
        dr_ref, drb_ref, dg_ref, db_ref = refs[-4:]
        xhat_v = xhat_ref[...]

        @pl.when(pl.program_id(0) == 0)
        def _():
            dg_ref[...] = jnp.zeros_like(dg_ref)
            db_ref[...] = jnp.zeros_like(db_ref)

        dg_ref[...] += jnp.sum(dy * xhat_v, axis=0, keepdims=True)
        db_ref[...] += jnp.sum(dy, axis=0, keepdims=True)
        dxh = dy * g_ref[...]
        m1 = jnp.mean(dxh, axis=-1, keepdims=True)
        m2 = jnp.mean(dxh * xhat_v, axis=-1, keepdims=True)
        dr = rstd_ref[...] * (dxh - m1 - xhat_v * m2)
        dr_ref[...] = dr
        drb_ref[...] = dr.astype(drb_ref.dtype)

    row = pl.BlockSpec((tm, D_MODEL), lambda i: (i, 0))
    vec = pl.BlockSpec((1, D_MODEL), lambda i: (0, 0))
    in_specs = [pl.BlockSpec((tm, k), lambda i: (i, 0)) for k in ks]
    in_specs += [pl.BlockSpec((sum(ks), D_MODEL), lambda i: (0, 0), pipeline_mode=pl.Buffered(1)), row]
    args = list(a_list) + [b, dres]
    if last:
        out_specs, out_shape = row, jax.ShapeDtypeStruct((T, D_MODEL), F32)
    else:
        in_specs += [row, pl.BlockSpec((tm, 1), lambda i: (i, 0)), vec]
        args += [xhat, rstd, g]
        out_specs = [row, row, vec, vec]
        out_shape = [jax.ShapeDtypeStruct((T, D_MODEL), F32), jax.ShapeDtypeStruct((T, D_MODEL), BF16),
                     jax.ShapeDtypeStruct((1, D_MODEL), F32), jax.ShapeDtypeStruct((1, D_MODEL), F32)]
    return pl.pallas_call(
        body, name=name, grid=(T // tm,),
        in_specs=in_specs + [pl.BlockSpec(memory_space=pl.ANY)] * nd,
        out_specs=out_specs, out_shape=out_shape,
        compiler_params=_cparams(("parallel",) if last else ("arbitrary",)),
    )(*args, *deps)


FF_TN = 256
SAVED_GU = BF16


def _mm_swiglu_fwd(h, w_gu, deps=()):
    T = h.shape[0]
    nj = D_FF // FF_TN
    nd = len(deps)

    def body(*refs):
        h_ref, wg_ref, wu_ref = refs[:3]
        g_ref, u_ref, act_ref = refs[3 + nd:]
        hv = h_ref[...]
        gv = lax.dot_general(hv, wg_ref[...], _DN["nt"], preferred_element_type=F32)
        uv = lax.dot_general(hv, wu_ref[...], _DN["nt"], preferred_element_type=F32)
        g_ref[...] = gv.astype(g_ref.dtype)
        u_ref[...] = uv.astype(u_ref.dtype)
        act_ref[...] = (gv * jax.nn.sigmoid(gv) * uv).astype(act_ref.dtype)

    col = pl.BlockSpec((T, FF_TN), lambda j: (0, j))
    return pl.pallas_call(
        body, name="mm_gate_up_swiglu", grid=(nj,),
        in_specs=[pl.BlockSpec((T, D_MODEL), lambda j: (0, 0)),
                  pl.BlockSpec((FF_TN, D_MODEL), lambda j: (j, 0)),
                  pl.BlockSpec((FF_TN, D_MODEL), lambda j: (j + nj, 0))] + [pl.BlockSpec(memory_space=pl.ANY)] * nd,
        out_specs=[col, col, col],
        out_shape=[jax.ShapeDtypeStruct((T, D_FF), SAVED_GU), jax.ShapeDtypeStruct((T, D_FF), SAVED_GU),
                   jax.ShapeDtypeStruct((T, D_FF), BF16)],
        compiler_params=_cparams(("parallel",)),
    )(h, w_gu, w_gu, *deps)


def _mm_swiglu_bwd(dr, w_dn, g, u):
    T = dr.shape[0]

    def body(dr_ref, w_ref, g_ref, u_ref, dg_ref, du_ref, act_ref):
        da = lax.dot_general(dr_ref[...], w_ref[...], _DN["nt"], preferred_element_type=F32)
        gv, uv = g_ref[...].astype(F32), u_ref[...].astype(F32)
        s = jax.nn.sigmoid(gv)
        sg = gv * s
        act_ref[...] = (sg * uv).astype(act_ref.dtype)
        du_ref[...] = (da * sg).astype(du_ref.dtype)
        dg_ref[...] = (da * uv * (s * (1.0 + gv * (1.0 - s)))).astype(dg_ref.dtype)

    col = pl.BlockSpec((T, FF_TN), lambda j: (0, j))
    return pl.pallas_call(
        body, name="mm_dact_swiglu", grid=(D_FF // FF_TN,),
        in_specs=[pl.BlockSpec((T, D_MODEL), lambda j: (0, 0)), pl.BlockSpec((FF_TN, D_MODEL), lambda j: (j, 0)),
                  col, col],
        out_specs=[col, col, col],
        out_shape=[jax.ShapeDtypeStruct((T, D_FF), BF16)] * 3,
        compiler_params=_cparams(("parallel",)),
    )(dr, w_dn, g, u)


def _gelu(x):
    return 0.5 * x * (1.0 + lax.erf(x * INV_SQRT2))


def _gelu_grad(x):
    return 0.5 * (1.0 + lax.erf(x * INV_SQRT2)) + x * (jnp.exp(-0.5 * x * x) * INV_SQRT_2PI)


def _shift_down(z, k):
    row = lax.broadcasted_iota(jnp.int32, z.shape, 0)
    return jnp.where(row >= k, pltpu.roll(z, k, 0), 0.0)


def _shift_up(z, k):
    n = z.shape[0]
    row = lax.broadcasted_iota(jnp.int32, z.shape, 0)
    return jnp.where(row < n - k, pltpu.roll(z, n - k, 0), 0.0)


def _lo_mask(shape):
    return lax.broadcasted_iota(jnp.int32, shape, len(shape) - 1) < HALF


def _seg_mean(x, lo):
    a = jnp.sum(jnp.where(lo, x, 0.0), axis=-1, keepdims=True)
    b = jnp.sum(jnp.where(lo, 0.0, x), axis=-1, keepdims=True)
    return jnp.where(lo, a, b) * (1.0 / HALF)


def _pool_windows(first):
    lo = _lo_mask((1, LANES))
    return jnp.where(first, jnp.where(lo, 2.0, 4.0), jnp.where(lo, 8.0, 16.0)), lo


def _pool_mean_minus_token(p, first):
    wl, lo = _pool_windows(first)
    s2 = p + _shift_down(p, 1)
    s4 = s2 + _shift_down(s2, 2)
    s8 = s4 + _shift_down(s4, 4)
    s16 = s8 + _shift_down(s8, 8)
    win = jnp.where(first, jnp.where(lo, s2, s4), jnp.where(lo, s8, s16))
    t1 = (lax.broadcasted_iota(jnp.int32, p.shape, 0) + 1).astype(F32)
    count = jnp.minimum(t1, wl)
    return win / count - p, count


SGU_UNROLL = 4


def _tril_keep():
    r = lax.broadcasted_iota(jnp.int32, (2 * CHUNK, CHUNK), 0)
    s = lax.broadcasted_iota(jnp.int32, (2 * CHUNK, CHUNK), 1)
    return s <= (r & (CHUNK - 1))


def _sgu_chunk_fwd(u, v, g, wm, bias, lo):
    ug = _gelu(u)
    vg = _gelu(v)
    mu = _seg_mean(vg, lo)
    xc = vg - mu
    var = _seg_mean(xc * xc, lo)
    rstd = lax.rsqrt(var + LN_EPS)
    vn = xc * rstd
    vh = (vn * g).astype(BF16)
    mm2 = jnp.dot(wm, vh, preferred_element_type=F32)
    mixed = jnp.where(lo, mm2[:CHUNK], mm2[CHUNK:]) + bias
    return ug, vn, rstd, vh, mixed


def _mixer_fwd(proj, wconv, wpool_bd, pscale, lng, wsp, bias):
    T = proj.shape[0]
    nchunk = T // CHUNK

    def body(a_ref, b_ref, c_ref, wc_ref, wp_ref, ps_ref, lng_ref, wsp_ref, bias_ref, o_ref):
        j = pl.program_id(0)

        @pl.when(j < 3)
        def _conv():
            z = c_ref[...] * a_ref[...]
            w = wc_ref[...]
            y = w[0:1] * _shift_down(z, 2) + w[1:2] * _shift_down(z, 1) + w[2:3] * z
            o_ref[...] = (b_ref[...] * y).astype(o_ref.dtype)

        @pl.when((j >= 3) & (j < 5))
        def _pool():
            d, _ = _pool_mean_minus_token(a_ref[...], j == 3)
            y = jnp.dot(d.astype(BF16), wp_ref[...].astype(BF16), preferred_element_type=F32)
            o_ref[...] = (y * ps_ref[...]).astype(o_ref.dtype)

        @pl.when(j >= 5)
        def _sgu():
            lo = _lo_mask((CHUNK, LANES))
            wm = jnp.where(_tril_keep(), wsp_ref[...], 0.0).astype(BF16)
            bias_t = bias_ref[...]
            g = lng_ref[...]

            def chunk(n, carry):
                rows = pl.ds(pl.multiple_of(n * CHUNK, CHUNK), CHUNK)
                ug, _, _, _, mixed = _sgu_chunk_fwd(a_ref[rows, :], b_ref[rows, :], g, wm, bias_t, lo)
                o_ref[rows, :] = (ug * mixed).astype(o_ref.dtype)
                return carry

            lax.fori_loop(0, nchunk, chunk, 0, unroll=SGU_UNROLL)

    def col(f):
        return lambda j: (0, f(j))

    clip = lambda v, lo, hi: jnp.minimum(jnp.maximum(v, lo), hi)
    return pl.pallas_call(
        body,
        name="mixer_fwd",
        grid=(8,),
        in_specs=[
            pl.BlockSpec((T, LANES), col(lambda j: jnp.where(j < 3, j, jnp.where(j < 5, j + 6, j + 6)))),
            pl.BlockSpec((T, LANES), col(lambda j: jnp.where(j < 3, j + 3, jnp.where(j < 5, 5, j + 9)))),
            pl.BlockSpec((T, LANES), col(lambda j: jnp.where(j < 3, j + 6, 8))),
            pl.BlockSpec((3, LANES), col(lambda j: clip(j, 0, 2))),
            pl.BlockSpec((None, LANES, LANES), lambda j: (clip(j - 3, 0, 1), 0, 0)),
            pl.BlockSpec((1, LANES), col(lambda j: clip(j - 3, 0, 1))),
            pl.BlockSpec((1, LANES), col(lambda j: clip(j - 5, 0, 2))),
            pl.BlockSpec((None, 2 * CHUNK, CHUNK), lambda j: (clip(j - 5, 0, 2), 0, 0)),
            pl.BlockSpec((None, CHUNK, LANES), lambda j: (clip(j - 5, 0, 2), 0, 0)),
        ],
        out_specs=pl.BlockSpec((T, LANES), lambda j: (0, j)),
        out_shape=jax.ShapeDtypeStruct((T, D_MODEL), BF16),
        compiler_params=_cparams(("arbitrary",)),
    )(proj, proj, proj, wconv, wpool_bd, pscale, lng, wsp, bias)


def _mixer_bwd(proj, dmix, wconv, wpool_bd, pscale, lng, wsp, bias):
    T = proj.shape[0]
    nchunk = T // CHUNK

    def body(a_ref, b_ref, c_ref, dm_ref, wc_ref, wp_ref, ps_ref, lng_ref, wsp_ref, bias_ref,
             o_ref, dwc_ref, dwp_ref, dps_ref, dlng_ref, dwsp_ref, dbias_ref, keep1, keep2):
        k = pl.program_id(0)

        @pl.when(k < 3)
        def _conv():
            xa, gb, gc, dya = a_ref[...], b_ref[...], c_ref[...], dm_ref[...]
            w = wc_ref[...]
            z = gc * xa
            z1 = _shift_down(z, 1)
            z2 = _shift_down(z, 2)
            y = w[0:1] * z2 + w[1:2] * z1 + w[2:3] * z
            dyv = dya * gb
            dz = w[2:3] * dyv + w[1:2] * _shift_up(dyv, 1) + w[0:1] * _shift_up(dyv, 2)
            dwc_ref[0:1, :] = jnp.sum(dyv * z2, axis=0, keepdims=True)
            dwc_ref[1:2, :] = jnp.sum(dyv * z1, axis=0, keepdims=True)
            dwc_ref[2:3, :] = jnp.sum(dyv * z, axis=0, keepdims=True)
            o_ref[...] = (dz * gc).astype(o_ref.dtype)
            keep1[k] = (dya * y).astype(keep1.dtype)
            keep1[k + 3] = (dz * xa).astype(keep1.dtype)

        @pl.when((k >= 3) & (k < 9))
        def _emit_gb_gc():
            o_ref[...] = keep1[k - 3]

        @pl.when((k >= 9) & (k < 11))
        def _pool():
            first = k == 9
            p, dyb = a_ref[...], dm_ref[...]
            d, count = _pool_mean_minus_token(p, first)
            w2 = wp_ref[...].astype(BF16)
            db = d.astype(BF16)
            y = jnp.dot(db, w2, preferred_element_type=F32)
            dps_ref[...] = jnp.sum(dyb * y, axis=0, keepdims=True)
            dyv = (dyb * ps_ref[...]).astype(BF16)
            dd = lax.dot_general(dyv, w2, _DN["nt"], preferred_element_type=F32)
            dwp_ref[...] = lax.dot_general(db, dyv, _DN["tn"], preferred_element_type=F32)
            dwin = dd / count
            a2 = dwin + _shift_up(dwin, 1)
            a4 = a2 + _shift_up(a2, 2)
            a8 = a4 + _shift_up(a4, 4)
            a16 = a8 + _shift_up(a8, 8)
            _, lo = _pool_windows(first)
            back = jnp.where(first, jnp.where(lo, a2, a4), jnp.where(lo, a8, a16))
            o_ref[...] = (back - dd).astype(o_ref.dtype)

        @pl.when((k >= 11) & (k < 14))
        def _sgu():
            lo = _lo_mask((CHUNK, LANES))
            keep = _tril_keep()
            wm = jnp.where(keep, wsp_ref[...], 0.0).astype(BF16)
            bias_t = bias_ref[...]
            g = lng_ref[...]
            dwsp_ref[...] = jnp.zeros_like(dwsp_ref)
            dbias_ref[...] = jnp.zeros_like(dbias_ref)
            dlng_ref[...] = jnp.zeros_like(dlng_ref)

            def chunk(n, carry):
                rows = pl.ds(pl.multiple_of(n * CHUNK, CHUNK), CHUNK)
                u, v, dyc = a_ref[rows, :], b_ref[rows, :], dm_ref[rows, :]
                ug, vn, rstd, vh, mixed = _sgu_chunk_fwd(u, v, g, wm, bias_t, lo)
                dmx = dyc * ug
                o_ref[rows, :] = (dyc * mixed * _gelu_grad(u)).astype(o_ref.dtype)
                dbias_ref[...] += dmx
                dst = jnp.concatenate([jnp.where(lo, dmx, 0.0), jnp.where(lo, 0.0, dmx)], axis=0).astype(BF16)
                dwsp_ref[...] += lax.dot_general(dst, vh, _DN["nt"], preferred_element_type=F32)
                dvh = lax.dot_general(wm, dst, _DN["tn"], preferred_element_type=F32)
                dlng_ref[...] += jnp.sum(dvh * vn, axis=0, keepdims=True)
                dvn = dvh * g
                m1 = _seg_mean(dvn, lo)
                m2 = _seg_mean(dvn * vn, lo)
                dvg = rstd * (dvn - m1 - vn * m2)
                keep2[k - 11, rows, :] = (dvg * _gelu_grad(v)).astype(keep2.dtype)
                return carry

            lax.fori_loop(0, nchunk, chunk, 0, unroll=SGU_UNROLL)
            dwsp_ref[...] = jnp.where(keep, dwsp_ref[...], 0.0)
            dbt = dbias_ref[...]
            lane = lax.broadcasted_iota(jnp.int32, (CHUNK, LANES), 1)
            sa = jnp.sum(jnp.where(lo, dbt, 0.0), axis=-1, keepdims=True)
            sb = jnp.sum(jnp.where(lo, 0.0, dbt), axis=-1, keepdims=True)
            dbias_ref[...] = jnp.where(lane == 0, sa, jnp.where(lane == 1, sb, 0.0))

        @pl.when(k >= 14)
        def _emit_v():
            o_ref[...] = keep2[k - 14]

    def col(f):
        return lambda k: (0, f(k))

    clip = lambda v, lo, hi: jnp.minimum(jnp.maximum(v, lo), hi)
    view_a = lambda k: jnp.where(k < 3, k, jnp.where(k < 9, 2, jnp.where(k < 14, k, 13)))
    view_b = lambda k: jnp.where(k < 3, k + 3, jnp.where(k < 11, 5, jnp.where(k < 14, k + 3, 16)))
    view_c = lambda k: jnp.where(k < 3, k + 6, 8)
    view_dm = lambda k: jnp.where(k < 3, k, jnp.where(k < 9, 2, jnp.where(k < 14, k - 6, 7)))
    return pl.pallas_call(
        body,
        name="mixer_bwd",
        grid=(17,),
        in_specs=[
            pl.BlockSpec((T, LANES), col(view_a)),
            pl.BlockSpec((T, LANES), col(view_b)),
            pl.BlockSpec((T, LANES), col(view_c)),
            pl.BlockSpec((T, LANES), col(view_dm)),
            pl.BlockSpec((3, LANES), col(lambda k: clip(k, 0, 2))),
            pl.BlockSpec((None, LANES, LANES), lambda k: (clip(k - 9, 0, 1), 0, 0)),
            pl.BlockSpec((1, LANES), col(lambda k: clip(k - 9, 0, 1))),
            pl.BlockSpec((1, LANES), col(lambda k: clip(k - 11, 0, 2))),
            pl.BlockSpec((None, 2 * CHUNK, CHUNK), lambda k: (clip(k - 11, 0, 2), 0, 0)),
            pl.BlockSpec((None, CHUNK, LANES), lambda k: (clip(k - 11, 0, 2), 0, 0)),
        ],
        out_specs=[
            pl.BlockSpec((T, LANES), lambda k: (0, k)),
            pl.BlockSpec((3, LANES), col(lambda k: clip(k, 0, 2))),
            pl.BlockSpec((None, LANES, LANES), lambda k: (clip(k - 9, 0, 1), 0, 0)),
            pl.BlockSpec((1, LANES), col(lambda k: clip(k - 9, 0, 1))),
            pl.BlockSpec((1, LANES), col(lambda k: clip(k - 11, 0, 2))),
            pl.BlockSpec((None, 2 * CHUNK, CHUNK), lambda k: (clip(k - 11, 0, 2), 0, 0)),
            pl.BlockSpec((None, CHUNK, LANES), lambda k: (clip(k - 11, 0, 2), 0, 0)),
        ],
        out_shape=[
            jax.ShapeDtypeStruct((T, IN_W), BF16),
            jax.ShapeDtypeStruct((3, CONV_W), F32),
            jax.ShapeDtypeStruct((2, LANES, LANES), F32),
            jax.ShapeDtypeStruct((1, POOL_W), F32),
            jax.ShapeDtypeStruct((1, SGU_W), F32),
            jax.ShapeDtypeStruct((3, 2 * CHUNK, CHUNK), F32),
            jax.ShapeDtypeStruct((3, CHUNK, LANES), F32),
        ],
        scratch_shapes=[pltpu.VMEM((6, T, LANES), BF16), pltpu.VMEM((3, T, LANES), BF16)],
        compiler_params=_cparams(("arbitrary",)),
    )(proj, proj, proj, dmix, wconv, wpool_bd, pscale, lng, wsp, bias)


def _ln_fwd(prev, pg, pb, mmout, g, b, tm=256):
    T = prev.shape[0]

    def body(prev_ref, pg_ref, pb_ref, mm_ref, g_ref, b_ref, xhat_ref, rstd_ref, y_ref):
        r = ALPHA * (prev_ref[...] * pg_ref[...] + pb_ref[...]) + mm_ref[...]
        mu = jnp.mean(r, axis=-1, keepdims=True)
        xc = r - mu
        var = jnp.mean(xc * xc, axis=-1, keepdims=True)
        rstd = lax.rsqrt(var + LN_EPS)
        xhat = xc * rstd
        xhat_ref[...] = xhat
        rstd_ref[...] = rstd
        y_ref[...] = (xhat * g_ref[...] + b_ref[...]).astype(y_ref.dtype)

    row = pl.BlockSpec((tm, D_MODEL), lambda i: (i, 0))
    vec = pl.BlockSpec((1, D_MODEL), lambda i: (0, 0))
    return pl.pallas_call(
        body,
        name="ln_fwd",
        grid=(T // tm,),
        in_specs=[row, vec, vec, row, vec, vec],
        out_specs=[row, pl.BlockSpec((tm, 1), lambda i: (i, 0)), row],
        out_shape=[jax.ShapeDtypeStruct((T, D_MODEL), F32), jax.ShapeDtypeStruct((T, 1), F32),
                   jax.ShapeDtypeStruct((T, D_MODEL), BF16)],
        compiler_params=_cparams(("parallel",)),
    )(prev, pg, pb, mmout, g, b)


def _ln_bwd(dres, dmm, xhat, rstd, g, tm=256, deps=()):
    T = xhat.shape[0]
    has_res = dres is not None
    nd = len(deps)

    def body(*refs):
        refs = refs[:len(refs) - 4 - nd] + refs[len(refs) - 4:]
        if has_res:
            dres_ref, dmm_ref, xhat_ref, rstd_ref, g_ref, dr_ref, drb_ref, dg_ref, db_ref = refs
            dy = ALPHA * dres_ref[...] + dmm_ref[...]
        else:
            dmm_ref, xhat_ref, rstd_ref, g_ref, dr_ref, drb_ref, dg_ref, db_ref = refs
            dy = dmm_ref[...]
        xhat_v = xhat_ref[...]

        @pl.when(pl.program_id(0) == 0)
        def _():
            dg_ref[...] = jnp.zeros_like(dg_ref)
            db_ref[...] = jnp.zeros_like(db_ref)

        dg_ref[...] += jnp.sum(dy * xhat_v, axis=0, keepdims=True)
        db_ref[...] += jnp.sum(dy, axis=0, keepdims=True)
        dxh = dy * g_ref[...]
        m1 = jnp.mean(dxh, axis=-1, keepdims=True)
        m2 = jnp.mean(dxh * xhat_v, axis=-1, keepdims=True)
        dr = rstd_ref[...] * (dxh - m1 - xhat_v * m2)
        dr_ref[...] = dr
        drb_ref[...] = dr.astype(drb_ref.dtype)

    row = pl.BlockSpec((tm, D_MODEL), lambda i: (i, 0))
    vec = pl.BlockSpec((1, D_MODEL), lambda i: (0, 0))
    in_specs = ([row] if has_res else []) + [row, row, pl.BlockSpec((tm, 1), lambda i: (i, 0)), vec]
    in_specs += [pl.BlockSpec(memory_space=pl.ANY)] * nd
    args = ([dres] if has_res else []) + [dmm, xhat, rstd, g] + list(deps)
    return pl.pallas_call(
        body,
        name="ln_bwd_res" if has_res else "ln_bwd",
        grid=(T // tm,),
        in_specs=in_specs,
        out_specs=[row, row, vec, vec],
        out_shape=[jax.ShapeDtypeStruct((T, D_MODEL), F32), jax.ShapeDtypeStruct((T, D_MODEL), BF16),
                   jax.ShapeDtypeStruct((1, D_MODEL), F32), jax.ShapeDtypeStruct((1, D_MODEL), F32)],
        compiler_params=_cparams(("arbitrary",)),
    )(*args)


def _loss_head(xhat, g, b, target, tm=256):
    T = xhat.shape[0]

    def body(xhat_ref, g_ref, b_ref, t_ref, loss_ref, dy_ref):
        err = xhat_ref[...] * g_ref[...] + b_ref[...] - t_ref[...]

        @pl.when(pl.program_id(0) == 0)
        def _():
            loss_ref[...] = jnp.zeros_like(loss_ref)

        part = jnp.sum(jnp.sum(err * err, axis=-1, keepdims=True), axis=0, keepdims=True)
        loss_ref[...] += jnp.broadcast_to(part * (0.5 / D_MODEL), loss_ref.shape)
        dy_ref[...] = err * (1.0 / D_MODEL)

    row = pl.BlockSpec((tm, D_MODEL), lambda i: (i, 0))
    vec = pl.BlockSpec((1, D_MODEL), lambda i: (0, 0))
    return pl.pallas_call(
        body,
        name="loss_head",
        grid=(T // tm,),
        in_specs=[row, vec, vec, row],
        out_specs=[pl.BlockSpec((8, LANES), lambda i: (0, 0)), row],
        out_shape=[jax.ShapeDtypeStruct((8, LANES), F32), jax.ShapeDtypeStruct((T, D_MODEL), F32)],
        compiler_params=_cparams(("arbitrary",)),
    )(xhat, g, b, target)


def _residual_out(dres, dmm, tm=256):
    T = dres.shape[0]

    def body(a_ref, b_ref, o_ref):
        o_ref[...] = ALPHA * a_ref[...] + b_ref[...]

    row = pl.BlockSpec((tm, D_MODEL), lambda i: (i, 0))
    return pl.pallas_call(
        body, name="residual_out", grid=(T // tm,), in_specs=[row, row], out_specs=row,
        out_shape=jax.ShapeDtypeStruct((T, D_MODEL), F32), compiler_params=_cparams(("parallel",)),
    )(dres, dmm)


SW_TC = 1408


def _swiglu_fwd(gu, tm=128):
    T = gu.shape[0]

    def body(gu_ref, o_ref):
        gv = gu_ref[:, :D_FF]
        o_ref[...] = (gv * jax.nn.sigmoid(gv) * gu_ref[:, D_FF:]).astype(o_ref.dtype)

    return pl.pallas_call(
        body, name="swiglu_fwd", grid=(T // tm,),
        in_specs=[pl.BlockSpec((tm, 2 * D_FF), lambda i: (i, 0))],
        out_specs=pl.BlockSpec((tm, D_FF), lambda i: (i, 0)),
        out_shape=jax.ShapeDtypeStruct((T, D_FF), BF16), compiler_params=_cparams(("parallel",)),
    )(gu)


def _swiglu_bwd(gu, dact, tm=128):
    T = gu.shape[0]

    def body(gu_ref, da_ref, dgu_ref, act_ref):
        gv, uv, da = gu_ref[:, :D_FF], gu_ref[:, D_FF:], da_ref[...]
        s = jax.nn.sigmoid(gv)
        sg = gv * s
        act_ref[...] = (sg * uv).astype(act_ref.dtype)
        dgu_ref[:, D_FF:] = (da * sg).astype(dgu_ref.dtype)
        dgu_ref[:, :D_FF] = (da * uv * (s * (1.0 + gv * (1.0 - s)))).astype(dgu_ref.dtype)

    wide = pl.BlockSpec((tm, 2 * D_FF), lambda i: (i, 0))
    half = pl.BlockSpec((tm, D_FF), lambda i: (i, 0))
    return pl.pallas_call(
        body, name="swiglu_bwd", grid=(T // tm,),
        in_specs=[wide, half], out_specs=[wide, half],
        out_shape=[jax.ShapeDtypeStruct((T, 2 * D_FF), BF16), jax.ShapeDtypeStruct((T, D_FF), BF16)],
        compiler_params=_cparams(("parallel",)),
    )(gu, dact)


def _adamw(w, g, m, v, tr):
    R, C = w.shape[-2:]
    assert R % tr == 0
    c1 = 1.0 - ADAM_B1 ** ADAM_STEP
    c2 = 1.0 - ADAM_B2 ** ADAM_STEP

    def body(w_ref, g_ref, m_ref, v_ref, d_ref, mo_ref, vo_ref):
        gv = g_ref[...]
        mn = ADAM_B1 * m_ref[...] + (1.0 - ADAM_B1) * gv
        vn = ADAM_B2 * v_ref[...] + (1.0 - ADAM_B2) * (gv * gv)
        d_ref[...] = -ADAM_LR * ((mn / c1) / (jnp.sqrt(vn / c2) + ADAM_EPS) + ADAM_WD * w_ref[...])
        mo_ref[...] = mn
        vo_ref[...] = vn

    if w.ndim == 2:
        grid, blk = (R // tr,), pl.BlockSpec((tr, C), lambda i: (i, 0))
    else:
        grid, blk = (w.shape[0], R // tr), pl.BlockSpec((None, tr, C), lambda l, i: (l, i, 0))
    return pl.pallas_call(
        body, name="adamw", grid=grid, in_specs=[blk] * 4, out_specs=[blk] * 3,
        out_shape=[jax.ShapeDtypeStruct(w.shape, F32)] * 3, compiler_params=_cparams(("parallel",) * len(grid)),
    )(w, g, m, v)


def _my_place():
    return lax.axis_index("x"), lax.axis_index("y"), lax.axis_index("c")


ANY = pl.BlockSpec(memory_space=pl.ANY)
HBM = pl.BlockSpec(memory_space=pltpu.HBM)
SEM = pl.BlockSpec(memory_space=pltpu.SEMAPHORE)
EFFECT = pltpu.SideEffectType.DATAFLOW_SIDE_EFFECTING


def _in_hbm(a):
    return pltpu.with_memory_space_constraint(a, pltpu.HBM)


def _block_rows(ref, dev):
    r = ref.shape[0] // N_DEV
    start = pl.multiple_of((4 * dev[0] + 2 * dev[1] + dev[2]) * r, 16)
    return ref.at[pl.ds(start, r), :]


def _ag_first_copies(s_refs, land_refs, send_sems, recv_sems, receiving):
    x, y, c = _my_place()
    peers = [(x, y, 1 - c)] + [(*chip, c) for chip in _other_chips(x, y)]
    copies = []
    for k, peer in enumerate(peers):
        block = peer if receiving else (x, y, c)
        copies += [pltpu.make_async_remote_copy(
            src_ref=s_refs[w], dst_ref=_block_rows(land_refs[w], block),
            send_sem=send_sems.at[k * len(s_refs) + w], recv_sem=recv_sems.at[k * len(s_refs) + w],
            device_id=peer, device_id_type=MESH)
            for w in range(len(s_refs))]
    return copies


def _ag_start(shards, layer, after=()):
    nw = len(shards)

    def body(*refs):
        s_refs, land_refs = refs[:nw], refs[nw:2 * nw]
        token = refs[-1]
        sems = 2 * nw + len(after)
        for cp in _ag_first_copies(s_refs, land_refs, refs[sems], refs[sems + 1], False):
            cp.start()
        token[...] = jnp.zeros_like(token)

    lands = [lax.empty((N_DEV * s.shape[0], D_MODEL), BF16) for s in shards]
    out = pl.pallas_call(
        body, name="ag_start_%s" % layer,
        in_specs=[HBM] * (2 * nw) + [ANY] * len(after),
        out_specs=(SEM, SEM, *[HBM] * (2 * nw), pl.BlockSpec(memory_space=pltpu.VMEM)),
        out_shape=(pltpu.SemaphoreType.DMA((4 * nw,)), pltpu.SemaphoreType.DMA((4 * nw,)),
                   *[pltpu.HBM(a.shape, a.dtype) for a in list(shards) + lands],
                   jax.ShapeDtypeStruct((8, LANES), F32)),
        input_output_aliases={i: 2 + i for i in range(2 * nw)},
        compiler_params=pltpu.CompilerParams(has_side_effects=EFFECT),
    )(*[_in_hbm(a) for a in list(shards) + lands], *after)
    return out[0], out[1], out[2:2 + nw], out[2 + nw:2 + 2 * nw], out[-1]


def _ag_wait(send_sems, recv_sems, shards, lands, after, layer):
    nw = len(shards)

    def body(*refs):
        s_refs, land_refs = refs[:nw], refs[nw:2 * nw]
        for cp in _ag_first_copies(s_refs, land_refs, refs[2 * nw], refs[2 * nw + 1], True):
            cp.wait_send()
            cp.wait_recv()

    out = pl.pallas_call(
        body, name="ag_wait_%s" % layer,
        in_specs=[HBM] * (2 * nw) + [SEM, SEM] + [ANY] * len(after),
        out_specs=[HBM] * (2 * nw),
        out_shape=[pltpu.HBM(a.shape, a.dtype) for a in list(shards) + list(lands)],
        input_output_aliases={i: i for i in range(2 * nw)},
        compiler_params=pltpu.CompilerParams(has_side_effects=EFFECT),
    )(*shards, *lands, send_sems, recv_sems, *after)
    return out[:nw], out[nw:]


def _ag_pass_on(shards, lands):
    nw = len(shards)

    def body(*refs):
        s_refs, g_refs = refs[:nw], refs[2 * nw:3 * nw]
        send_sems, recv_sems, local_sems = refs[3 * nw:3 * nw + 3]
        stage = refs[3 * nw + 3:]
        x, y, c = _my_place()
        load = [pltpu.make_async_copy(s_refs[w], stage[w], local_sems.at[w]) for w in range(nw)]
        mine = [pltpu.make_async_copy(stage[w], _block_rows(g_refs[w], (x, y, c)), local_sems.at[w])
                for w in range(nw)]
        for cp in load:
            cp.start()
        sends, arrivals = [], []
        for j, chip in enumerate(_other_chips(x, y)):
            for w in range(nw):
                rows_out = _block_rows(g_refs[w], (*chip, c))
                rows_in = _block_rows(g_refs[w], (*chip, 1 - c))
                sends.append(pltpu.make_async_remote_copy(
                    src_ref=rows_out, dst_ref=rows_out, send_sem=send_sems.at[j, w], recv_sem=recv_sems.at[j, w],
                    device_id=(x, y, 1 - c), device_id_type=MESH))
                arrivals.append(pltpu.make_async_remote_copy(
                    src_ref=rows_in, dst_ref=rows_in, send_sem=send_sems.at[j, w], recv_sem=recv_sems.at[j, w],
                    device_id=(x, y, 1 - c), device_id_type=MESH))
        for cp in sends:
            cp.start()
        for w in range(nw):
            load[w].wait()
            mine[w].start()
        for cp in arrivals:
            cp.wait_recv()
        for cp in sends:
            cp.wait_send()
        for cp in mine:
            cp.wait()

    return pl.pallas_call(
        body, name="ag_pass_on",
        in_specs=[ANY] * (2 * nw), out_specs=[ANY] * nw,
        out_shape=[jax.ShapeDtypeStruct(a.shape, a.dtype) for a in lands],
        input_output_aliases={nw + i: i for i in range(nw)},
        scratch_shapes=[pltpu.SemaphoreType.DMA((3, nw)), pltpu.SemaphoreType.DMA((3, nw)),
                        pltpu.SemaphoreType.DMA((nw,))] + [pltpu.VMEM(s.shape, s.dtype) for s in shards],
        compiler_params=_cparams(),
    )(*shards, *lands)


def _rs_sibling_exchange(parts):
    nw = len(parts)

    def body(*refs):
        p_refs, o_refs = refs[:nw], refs[nw:2 * nw]
        send_sems, recv_sems = refs[2 * nw:]
        x, y, c = _my_place()
        copies = [pltpu.make_async_remote_copy(
            src_ref=p_refs[w].at[:, 1 - c], dst_ref=o_refs[w],
            send_sem=send_sems.at[w], recv_sem=recv_sems.at[w], device_id=(x, y, 1 - c), device_id_type=MESH)
            for w in range(nw)]
        for cp in copies:
            cp.start()
        for cp in copies:
            cp.wait()

    return pl.pallas_call(
        body, name="rs_sibling_exchange",
        in_specs=[ANY] * nw, out_specs=[ANY] * nw,
        out_shape=[jax.ShapeDtypeStruct(p.shape[:1] + p.shape[2:], BF16) for p in parts],
        scratch_shapes=[pltpu.SemaphoreType.DMA((nw,)), pltpu.SemaphoreType.DMA((nw,))],
    )(*parts)


def _rs_chip_sum(parts, gots, c):
    n = len(parts)

    def body(c_ref, *refs):
        for p_ref, g_ref, o_ref in zip(refs[:n], refs[n:2 * n], refs[2 * n:]):
            o_ref[...] = (p_ref[...].astype(F32) + g_ref[...].astype(F32)).astype(o_ref.dtype)

    mine = [pl.BlockSpec((None, None, p.shape[2], D_MODEL), lambda q, c_ref: (q, c_ref[0], 0, 0)) for p in parts]
    theirs = [pl.BlockSpec((None, g.shape[1], D_MODEL), lambda q, c_ref: (q, 0, 0)) for g in gots]
    return pl.pallas_call(
        body, name="rs_chip_sum",
        grid_spec=pltpu.PrefetchScalarGridSpec(
            num_scalar_prefetch=1, grid=(4,), in_specs=mine + theirs, out_specs=theirs),
        out_shape=[jax.ShapeDtypeStruct(g.shape, BF16) for g in gots],
        compiler_params=_cparams(("parallel",)),
    )(c, *parts, *gots)


def _other_chips(x, y):
    return [(1 - x, y), (x, 1 - y), (1 - x, 1 - y)]


def _rs_chip_copies(s_refs, land_refs, send_sems, recv_sems):
    x, y, c = _my_place()
    copies = []
    for k, chip in enumerate(_other_chips(x, y)):
        q = 2 * chip[0] + chip[1]
        copies += [pltpu.make_async_remote_copy(
            src_ref=s_refs[w].at[q], dst_ref=land_refs[w].at[k],
            send_sem=send_sems.at[k * len(s_refs) + w], recv_sem=recv_sems.at[k * len(s_refs) + w],
            device_id=(*chip, c), device_id_type=MESH)
            for w in range(len(s_refs))]
    return copies


def _rs_chip_start(sums, layer):
    nw = len(sums)

    def body(*refs):
        s_refs, land_refs = refs[:nw], refs[nw:2 * nw]
        send_sems, recv_sems = refs[2 * nw], refs[2 * nw + 1]
        token = refs[-1]
        for cp in _rs_chip_copies(s_refs, land_refs, send_sems, recv_sems):
            cp.start()
        token[...] = jnp.zeros_like(token)

    lands = [lax.empty((3,) + s.shape[1:], BF16) for s in sums]
    out = pl.pallas_call(
        body, name="rs_chip_start_%s" % layer,
        in_specs=[HBM] * (2 * nw),
        out_specs=(SEM, SEM, *[HBM] * (2 * nw), pl.BlockSpec(memory_space=pltpu.VMEM)),
        out_shape=(pltpu.SemaphoreType.DMA((3 * nw,)), pltpu.SemaphoreType.DMA((3 * nw,)),
                   *[pltpu.HBM(a.shape, a.dtype) for a in list(sums) + lands],
                   jax.ShapeDtypeStruct((8, LANES), F32)),
        input_output_aliases={i: 2 + i for i in range(2 * nw)},
        compiler_params=pltpu.CompilerParams(has_side_effects=EFFECT),
    )(*[_in_hbm(a) for a in list(sums) + lands])
    return out[0], out[1], out[2:2 + nw], out[2 + nw:2 + 2 * nw], out[-1]


def _rs_chip_wait(send_sems, recv_sems, sums, lands, after, layer):
    nw = len(sums)

    def body(*refs):
        s_refs, land_refs = refs[:nw], refs[nw:2 * nw]
        for cp in _rs_chip_copies(s_refs, land_refs, refs[2 * nw], refs[2 * nw + 1]):
            cp.wait_send()
            cp.wait_recv()

    out = pl.pallas_call(
        body, name="rs_chip_wait_%s" % layer,
        in_specs=[HBM] * (2 * nw) + [SEM, SEM] + [ANY] * len(after),
        out_specs=[HBM] * (2 * nw),
        out_shape=[pltpu.HBM(a.shape, a.dtype) for a in list(sums) + list(lands)],
        input_output_aliases={i: i for i in range(2 * nw)},
        compiler_params=pltpu.CompilerParams(has_side_effects=EFFECT),
    )(*sums, *lands, send_sems, recv_sems, *after)
    return out[:nw], out[nw:]


def _rs_finish(sums, gots, q, layer, into):
    n = len(sums)

    def body(q_ref, *refs):
        for s_ref, g_ref, o_ref in zip(refs[:n], refs[n:2 * n], refs[len(refs) - n:]):
            o_ref[...] = ((s_ref[...].astype(F32) + g_ref[0].astype(F32)) + g_ref[1].astype(F32)) + g_ref[2].astype(F32)

    rows = [s.shape[1] for s in sums]
    in_specs = [pl.BlockSpec((None, r, D_MODEL), lambda i, q_ref: (q_ref[0], 0, 0)) for r in rows]
    in_specs += [pl.BlockSpec((3, r, D_MODEL), lambda i, q_ref: (0, 0, 0)) for r in rows]
    args = [q, *sums, *gots]
    aliases = {}
    if into is not None:
        in_specs += [ANY] * n
        aliases = {len(args) + i: i for i in range(n)}
        args += list(into)
    return pl.pallas_call(
        body, name="rs_finish",
        grid_spec=pltpu.PrefetchScalarGridSpec(
            num_scalar_prefetch=1, grid=(1,), in_specs=in_specs,
            out_specs=[pl.BlockSpec((None, r, D_MODEL), lambda i, q_ref: (layer, 0, 0)) for r in rows]),
        out_shape=[jax.ShapeDtypeStruct((DEPTH, r, D_MODEL), F32) for r in rows],
        input_output_aliases=aliases,
        compiler_params=_cparams(("arbitrary",)),
    )(*args)


def _allreduce_small(vec):
    R = vec.shape[0]
    assert R % (8 * N_DEV) == 0
    P = R // N_DEV

    def body(v_ref, o_ref, buf, send1, recv1, send2, recv2):
        x, y, c = _my_place()
        me = 4 * x + 2 * y + c

        def piece(ref, d):
            return ref.at[pl.ds(pl.multiple_of(d * P, 8), P), :]

        def peer(k):
            p = me ^ k
            return p, (p >> 2, (p >> 1) & 1, p & 1)

        scatter = []
        for k in range(1, N_DEV):
            p, where = peer(k)
            scatter.append(pltpu.make_async_remote_copy(
                src_ref=piece(v_ref, p), dst_ref=buf.at[k], send_sem=send1.at[k - 1], recv_sem=recv1.at[k - 1],
                device_id=where, device_id_type=MESH))
        for cp in scatter:
            cp.start()
        buf[0] = piece(v_ref, me)[...]
        for cp in scatter:
            cp.wait()
        acc = buf[me]
        for d in range(1, N_DEV):
            acc = acc + buf[me ^ d]
        piece(o_ref, me)[...] = acc
        spread, arrivals = [], []
        for k in range(1, N_DEV):
            p, where = peer(k)
            spread.append(pltpu.make_async_remote_copy(
                src_ref=piece(o_ref, me), dst_ref=piece(o_ref, me), send_sem=send2.at[k - 1], recv_sem=recv2.at[k - 1],
                device_id=where, device_id_type=MESH))
            arrivals.append(pltpu.make_async_remote_copy(
                src_ref=piece(o_ref, p), dst_ref=piece(o_ref, p), send_sem=send2.at[k - 1], recv_sem=recv2.at[k - 1],
                device_id=where, device_id_type=MESH))
        for cp in spread:
            cp.start()
        for cp in arrivals:
            cp.wait_recv()
        for cp in spread:
            cp.wait_send()

    sems = pltpu.SemaphoreType.DMA((N_DEV - 1,))
    return pl.pallas_call(
        body, name="allreduce_small",
        in_specs=[pl.BlockSpec(memory_space=pltpu.VMEM)], out_specs=pl.BlockSpec(memory_space=pltpu.VMEM),
        out_shape=jax.ShapeDtypeStruct((R, LANES), F32),
        scratch_shapes=[pltpu.VMEM((N_DEV, P, LANES), F32), sems, sems, sems, sems],
        compiler_params=_cparams(),
    )(vec)


def _pack(arrs):
    flat = jnp.concatenate([a.reshape(-1) for a in arrs])
    pad = (-flat.shape[0]) % (8 * N_DEV * LANES)
    return jnp.pad(flat, (0, pad)).reshape(-1, LANES)


def _unpack(packed, shapes):
    flat = packed.reshape(-1)
    out, off = [], 0
    for s in shapes:
        n = math.prod(s)
        out.append(flat[off:off + n].reshape(s))
        off += n
    return out


def kernel(x, w_in, w_conv, w_pool, pool_scale, sgu_ln_g, w_spatial, b_spatial, w_o, ln1_g, ln1_b, w_gate_up, w_down, ln2_g, ln2_b, loss_target, m_w_in, m_w_conv, m_w_pool, m_pool_scale, m_sgu_ln_g, m_w_spatial, m_b_spatial, m_w_o, m_ln1_g, m_ln1_b, m_w_gate_up, m_w_down, m_ln2_g, m_ln2_b, v_w_in, v_w_conv, v_w_pool, v_pool_scale, v_sgu_ln_g, v_w_spatial, v_b_spatial, v_w_o, v_ln1_g, v_ln1_b, v_w_gate_up, v_w_down, v_ln2_g, v_ln2_b):
    L = DEPTH
    T = x.shape[1]
    mx, my, mc = _my_place()
    dev = 4 * mx + 2 * my + mc
    xs = x[0]
    target = loss_target[0]

    shards = (jnp.swapaxes(w_in, 1, 2).astype(BF16), jnp.swapaxes(w_gate_up, 1, 2).astype(BF16),
              w_o.astype(BF16), w_down.astype(BF16))
    first_gather = _ag_start_layer(shards, 0, [])

    conv_cols = w_conv.shape[2]
    w_conv_z = lax.dynamic_update_slice(jnp.zeros((L, 3, CONV_W), F32), w_conv, (0, 0, dev * conv_cols))
    w_conv_full = _allreduce_small(_pack([w_conv_z]))
    w_conv_full = _unpack(w_conv_full, [(L, 3, CONV_W)])[0]

    loss_tile, grad_x2, big_grads, small_grads = _local_step(
        xs, target, shards, first_gather, w_conv_full, w_pool, pool_scale, sgu_ln_g, w_spatial, b_spatial,
        ln1_g, ln1_b, ln2_g, ln2_b)
    loss = lax.psum(loss_tile[0, 0], ("x", "y", "c"))
    grad_x = grad_x2[None]
    big_w = (w_in, w_gate_up, w_o, w_down)
    big_m = (m_w_in, m_w_gate_up, m_w_o, m_w_down)
    big_v = (v_w_in, v_w_gate_up, v_w_o, v_w_down)
    small_w = [w_conv_full, w_pool, pool_scale, sgu_ln_g, w_spatial, b_spatial, ln1_g, ln1_b, ln2_g, ln2_b]
    small_m = [m_w_conv, m_w_pool, m_pool_scale, m_sgu_ln_g, m_w_spatial, m_b_spatial, m_ln1_g, m_ln1_b, m_ln2_g, m_ln2_b]
    small_v = [v_w_conv, v_w_pool, v_pool_scale, v_sgu_ln_g, v_w_spatial, v_b_spatial, v_ln1_g, v_ln1_b, v_ln2_g, v_ln2_b]
    grads, deltas, new_m, new_v = _reduce_and_update(
        big_grads, small_grads, big_w, big_m, big_v, small_w, small_m, small_v)
    return (loss, grad_x, *grads, *deltas, *new_m, *new_v)


def _ag_start_layer(shards, l, after):
    s_in, s_gu, s_o, s_dn = [s[l] for s in shards]
    first = _ag_start([s_in, s_o], "%da" % l, after=after)
    return first, _ag_start([s_gu, s_dn], "%db" % l, after=[first[4]])


def _ag_finish(gather, after, tag):
    send_sems, recv_sems, shards, lands, _ = gather
    shards, lands = _ag_wait(send_sems, recv_sems, shards, lands, after, tag)
    return _ag_pass_on(shards, lands)


def _rs_begin(parts, c_arr, tag):
    parts = [p.reshape(4, 2, p.shape[0] // N_DEV, D_MODEL) for p in parts]
    return _rs_chip_start(_rs_chip_sum(parts, _rs_sibling_exchange(parts), c_arr), tag)


def _local_step(xs, target, shards, gather, w_conv_full, w_pool, pool_scale, sgu_ln_g, w_spatial, b_spatial,
                ln1_g, ln1_b, ln2_g, ln2_b):
    L = DEPTH
    T = xs.shape[0]
    mx, my, mc = _my_place()
    c_arr = jnp.reshape(mc, (1,)).astype(jnp.int32)
    q_arr = jnp.reshape(2 * mx + my, (1,)).astype(jnp.int32)
    eye2 = jnp.eye(2, dtype=F32)
    wp = w_pool.reshape(L, 2, 2, HALF, HALF)
    wpool_bd = jnp.einsum("ltgcd,gh->ltgchd", wp, eye2).reshape(L, 2, LANES, LANES)
    wsp_t = w_spatial.reshape(L, 3, 2 * CHUNK, CHUNK)
    bias_t = jnp.repeat(jnp.swapaxes(b_spatial.reshape(L, 3, 2, CHUNK), 2, 3), HALF, axis=3)
    ones = jnp.ones((1, D_MODEL), F32)
    zeros = jnp.zeros((1, D_MODEL), F32)

    saved = []
    prev, pg, pb = xs, ones, zeros
    prev_b = xs.astype(BF16)
    weights = []
    for l in range(L):
        g_in, g_o = _ag_finish(gather[0], [] if l == 0 else [prev_b], "%da" % l)
        proj = _mm(prev_b, g_in, "nt", F32, 512, IN_W, D_MODEL, "mm_proj")
        mixcat = _mixer_fwd(proj, w_conv_full[l], wpool_bd[l], pool_scale[l][None], sgu_ln_g[l][None], wsp_t[l], bias_t[l])
        xhat1, rstd1, h_b = _mm_ln_fwd(mixcat, g_o, prev, pg, pb, ln1_g[l][None], ln1_b[l][None], "mm_wo_ln")
        g_gu, g_dn = _ag_finish(gather[1], [h_b], "%db" % l)
        weights.append((g_in, g_gu, g_o, g_dn))
        deps = []
        if l + 1 < L:
            gather = _ag_start_layer(shards, l + 1, [g_gu])
            deps = [gather[1][4]]
        g_act, u_act, act = _mm_swiglu_fwd(h_b, g_gu, deps=deps)
        xhat2, rstd2, y_b = _mm_ln_fwd(act, g_dn, xhat1, ln1_g[l][None], ln1_b[l][None], ln2_g[l][None], ln2_b[l][None],
                                       "mm_down_ln")
        saved.append((prev_b, proj, mixcat, xhat1, rstd1, h_b, g_act, u_act, xhat2, rstd2))
        prev, pg, pb, prev_b = xhat2, ln2_g[l][None], ln2_b[l][None], y_b

    loss_tile, dy = _loss_head(prev, pg, pb, target)

    small = [None] * L
    big = None
    in_flight = None
    above = None
    for l in reversed(range(L)):
        prev_b, proj, mixcat, xhat1, rstd1, h_b, g_act, u_act, xhat2, rstd2 = saved[l]
        g_in, g_gu, g_o, g_dn = weights[l]
        if above is None:
            dr2, dr2_b, dg2, db2 = _ln_bwd(None, dy, xhat2, rstd2, ln2_g[l][None])
        else:
            dr2, dr2_b, dg2, db2 = _mm_ln_bwd([above[0]], above[1], above[2], xhat2, rstd2, ln2_g[l][None],
                                              "mm_dx_ln", deps=[in_flight[4]])
        dg_b, du_b, act = _mm_swiglu_bwd(dr2_b, g_dn, g_act, u_act)
        p_dn = _mm(act, dr2_b, "tn", BF16, 256, D_MODEL, T, "mm_dw_down")
        p_gu = _mm(dg_b, h_b, "tn", BF16, 256, D_MODEL, T, "mm_dw_gate", out_rows=2 * D_FF)
        p_gu = _mm(du_b, h_b, "tn", BF16, 256, D_MODEL, T, "mm_dw_up", out_rows=2 * D_FF, out_off=D_FF, out_into=p_gu)
        ffn_flight = _rs_begin([p_gu, p_dn], c_arr, "0b") if l == 0 else None
        dr1, dr1_b, dg1, db1 = _mm_ln_bwd([dg_b, du_b], g_gu, dr2, xhat1, rstd1, ln1_g[l][None], "mm_dh_ln",
                                          deps=[ffn_flight[4]] if l == 0 else [])
        dmix = _mm(dr1_b, g_o, "nt", F32, T, 512, D_MODEL, "mm_dmix")
        p_o = _mm(mixcat, dr1_b, "tn", BF16, 512, D_MODEL, T, "mm_dw_o")
        dproj, dwc, dwp, dps, dlng, dwsp, dbias = _mixer_bwd(
            proj, dmix, w_conv_full[l], wpool_bd[l], pool_scale[l][None], sgu_ln_g[l][None], wsp_t[l], bias_t[l])
        p_in = _mm(dproj, prev_b, "tn", BF16, IN_W, D_MODEL, T, "mm_dw_in")
        small[l] = (dwc, dwp, dps, dlng, dwsp, dbias, dg1, db1, dg2, db2)
        above = (dproj, g_in, dr1)
        if in_flight is not None:
            big = list(_rs_chip_finish(in_flight, [p_in], q_arr, str(l + 1), l + 1, big))
        if l > 0:
            in_flight = _rs_begin([p_in, p_gu, p_o, p_dn], c_arr, str(l))
        else:
            big[1], big[3] = _rs_chip_finish(ffn_flight, [p_in], q_arr, "0b", 0, [big[1], big[3]])
            in_flight = _rs_begin([p_in, p_o], c_arr, "0a")
    grad_x = _mm_ln_bwd([above[0]], above[1], above[2], None, None, None, "mm_dx_out", deps=[in_flight[4]])
    big[0], big[2] = _rs_chip_finish(in_flight, [grad_x], q_arr, "0a", 0, [big[0], big[2]])
    big_grads = big

    def stack(i):
        return jnp.stack([small[l][i] for l in range(L)])

    dwp_bd = stack(1).reshape(L, 2, 2, HALF, 2, HALF)
    dwp_all = jnp.einsum("ltgchd,gh->ltgcd", dwp_bd, eye2).reshape(L, 4, HALF, HALF)
    dbs_all = jnp.swapaxes(stack(5)[:, :, :, :2], 2, 3).reshape(L, 6, CHUNK)
    small_grads = [stack(0), dwp_all, stack(2).reshape(L, POOL_W), stack(3).reshape(L, SGU_W),
                   stack(4).reshape(L, 6, CHUNK, CHUNK), dbs_all] + [stack(i).reshape(L, D_MODEL) for i in (6, 7, 8, 9)]
    return loss_tile, grad_x, big_grads, small_grads


def _rs_chip_finish(in_flight, after, q, tag, layer, into):
    send_sems, recv_sems, sums, lands, _ = in_flight
    sums, got = _rs_chip_wait(send_sems, recv_sems, sums, lands, after, tag)
    return _rs_finish(sums, got, q, layer, into)


def _reduce_and_update(big_grads, small_grads, big_w, big_m, big_v, small_w, small_m, small_v):
    L = DEPTH
    mx, my, mc = _my_place()
    dev = 4 * mx + 2 * my + mc
    conv_cols = CONV_W // N_DEV
    w_in, w_gate_up, w_o, w_down = big_w
    m_w_in, m_w_gate_up, m_w_o, m_w_down = big_m
    v_w_in, v_w_gate_up, v_w_o, v_w_down = big_v
    gt_in, gt_gu, g_w_o, g_w_dn = big_grads
    g_w_in = jnp.swapaxes(gt_in, 1, 2)
    g_w_gu = jnp.swapaxes(gt_gu, 1, 2)

    small_shapes = [a.shape for a in small_grads]
    packed_g = _allreduce_small(_pack(small_grads))

    def widen_conv(a):
        return lax.dynamic_update_slice(jnp.zeros((L, 3, CONV_W), F32), a, (0, 0, dev * conv_cols))

    small_m = [widen_conv(small_m[0])] + list(small_m[1:])
    small_v = [widen_conv(small_v[0])] + list(small_v[1:])
    pk_d, pk_m, pk_v = _adamw(_pack(small_w), packed_g, _pack(small_m), _pack(small_v), packed_g.shape[0] // 2)
    sg = _unpack(packed_g, small_shapes)
    sd = _unpack(pk_d, small_shapes)
    sm = _unpack(pk_m, small_shapes)
    sv = _unpack(pk_v, small_shapes)

    def conv_cols_of(a):
        return lax.dynamic_slice(a, (0, 0, dev * conv_cols), (L, 3, conv_cols))

    for lst in (sg, sd, sm, sv):
        lst[0] = conv_cols_of(lst[0])

    d_in, m_in, v_in = _adamw(w_in, g_w_in, m_w_in, v_w_in, 512)
    d_gu, m_gu, v_gu = _adamw(w_gate_up, g_w_gu, m_w_gate_up, v_w_gate_up, 512)
    d_o, m_o, v_o = _adamw(w_o, g_w_o, m_w_o, v_w_o, 128)
    d_dn, m_dn, v_dn = _adamw(w_down, g_w_dn, m_w_down, v_w_down, 352)

    def ordered(big_in, big_o, big_gu, big_dn, sm_list):
        return [big_in, sm_list[0], sm_list[1], sm_list[2], sm_list[3], sm_list[4], sm_list[5], big_o,
                sm_list[6], sm_list[7], big_gu, big_dn, sm_list[8], sm_list[9]]

    grads = ordered(g_w_in, g_w_o, g_w_gu, g_w_dn, sg)
    deltas = ordered(d_in, d_o, d_gu, d_dn, sd)
    new_m = ordered(m_in, m_o, m_gu, m_dn, sm)
    new_v = ordered(v_in, v_o, v_gu, v_dn, sv)
    return grads, deltas, new_m, new_v
```

```python
import functools
import math

import jax
import jax.numpy as jnp
from jax import lax
from jax.experimental import pallas as pl
from jax.experimental.pallas import tpu as pltpu

F32 = jnp.float32
BF16 = jnp.bfloat16
MESH = pl.DeviceIdType.MESH

D_MODEL = 1024
DEPTH = 4
CONV_W = 384
POOL_W = 256
SGU_W = 384
IN_W = 3 * CONV_W + POOL_W + 2 * SGU_W
D_FF = 2816
CHUNK = 128
ALPHA = float((2 * DEPTH) ** 0.25)
LN_EPS = 1e-5
ADAM_LR, ADAM_B1, ADAM_B2, ADAM_EPS, ADAM_WD, ADAM_STEP = 0.001, 0.9, 0.999, 1e-08, 0.01, 10

N_DEV = 8
LANES = 128
HALF = 64
SHARD_ROWS = (IN_W // N_DEV, 2 * D_FF // N_DEV, D_MODEL // N_DEV, D_FF // N_DEV)
VMEM_LIMIT = 52 * 1024 * 1024

INV_SQRT2 = 0.7071067811865476
INV_SQRT_2PI = 0.3989422804014327


def _cparams(sem=None, **kw):
    if sem is not None:
        kw["dimension_semantics"] = sem
    return pltpu.CompilerParams(vmem_limit_bytes=VMEM_LIMIT, **kw)


_DN = {"nn": (((1,), (0,)), ((), ())), "nt": (((1,), (1,)), ((), ())), "tn": (((0,), (0,)), ((), ()))}


def _mm(a, b, mode, out_dtype, tm, tn, tk, name, deps=(), out_rows=None, out_off=0, out_into=None):
    if mode == "nn":
        (M, K), N = a.shape, b.shape[1]
    elif mode == "nt":
        (M, K), N = a.shape, b.shape[0]
    else:
        (K, M), N = a.shape, b.shape[1]
    assert M % tm == 0 and N % tn == 0 and K % tk == 0 and out_off % tm == 0, (M, N, K, tm, tn, tk)
    nk = K // tk
    if out_into is not None:
        deps = tuple(deps) + (out_into,)
    nd = len(deps)
    row_off = out_off // tm

    def body(*refs):
        a_ref, b_ref, o_ref = refs[0], refs[1], refs[2 + nd]
        acc_ref = refs[3 + nd] if nk > 1 else None
        p = lax.dot_general(a_ref[...], b_ref[...], _DN[mode], preferred_element_type=F32)
        if nk == 1:
            o_ref[...] = p.astype(o_ref.dtype)
        else:
            k = pl.program_id(2)

            @pl.when(k == 0)
            def _():
                acc_ref[...] = p

            @pl.when(k > 0)
            def _():
                acc_ref[...] += p

            @pl.when(k == nk - 1)
            def _():
                o_ref[...] = acc_ref[...].astype(o_ref.dtype)

    if mode == "nn":
        a_spec = pl.BlockSpec((tm, tk), lambda i, j, k: (i, k))
        b_blk, b_idx = (tk, tn), (lambda i, j, k: (k, j))
    elif mode == "nt":
        a_spec = pl.BlockSpec((tm, tk), lambda i, j, k: (i, k))
        b_blk, b_idx = (tn, tk), (lambda i, j, k: (j, k))
    else:
        a_spec = pl.BlockSpec((tk, tm), lambda i, j, k: (k, i))
        b_blk, b_idx = (tk, tn), (lambda i, j, k: (k, j))
    return pl.pallas_call(
        body,
        name=name,
        grid=(M // tm, N // tn, nk),
        in_specs=[a_spec, pl.BlockSpec(b_blk, b_idx)] + [pl.BlockSpec(memory_space=pl.ANY)] * nd,
        out_specs=pl.BlockSpec((tm, tn), lambda i, j, k: (i + row_off, j)),
        out_shape=jax.ShapeDtypeStruct((out_rows or M, N), out_dtype),
        scratch_shapes=[pltpu.VMEM((tm, tn), F32)] if nk > 1 else [],
        input_output_aliases={1 + nd: 0} if out_into is not None else {},
        compiler_params=_cparams(("parallel", "parallel", "arbitrary")),
    )(a, b, *deps)


LN_TM = 512


def _mm_ln_fwd(a, b, prev, pg, pb, g, bias, name):
    T, K = a.shape
    tm = LN_TM

    def body(a_ref, b_ref, prev_ref, pg_ref, pb_ref, g_ref, bias_ref, xhat_ref, rstd_ref, y_ref):
        mm = jnp.dot(a_ref[...], b_ref[...], preferred_element_type=F32)
        r = ALPHA * (prev_ref[...] * pg_ref[...] + pb_ref[...]) + mm
        mu = jnp.mean(r, axis=-1, keepdims=True)
        xc = r - mu
        var = jnp.mean(xc * xc, axis=-1, keepdims=True)
        rstd = lax.rsqrt(var + LN_EPS)
        xhat = xc * rstd
        xhat_ref[...] = xhat
        rstd_ref[...] = rstd
        y_ref[...] = (xhat * g_ref[...] + bias_ref[...]).astype(y_ref.dtype)

    row = pl.BlockSpec((tm, D_MODEL), lambda i: (i, 0))
    vec = pl.BlockSpec((1, D_MODEL), lambda i: (0, 0))
    return pl.pallas_call(
        body, name=name, grid=(T // tm,),
        in_specs=[pl.BlockSpec((tm, K), lambda i: (i, 0)),
                  pl.BlockSpec((K, D_MODEL), lambda i: (0, 0), pipeline_mode=pl.Buffered(1)),
                  row, vec, vec, vec, vec],
        out_specs=[row, pl.BlockSpec((tm, 1), lambda i: (i, 0)), row],
        out_shape=[jax.ShapeDtypeStruct((T, D_MODEL), F32), jax.ShapeDtypeStruct((T, 1), F32),
                   jax.ShapeDtypeStruct((T, D_MODEL), BF16)],
        compiler_params=_cparams(("parallel",)),
    )(a, b, prev, pg, pb, g, bias)


def _mm_ln_bwd(a_list, b, dres, xhat, rstd, g, name, deps=()):
    T = a_list[0].shape[0]
    tm = LN_TM
    na, nd = len(a_list), len(deps)
    ks = [a.shape[1] for a in a_list]
    last = xhat is None

    def body(*refs):
        a_refs, b_ref, dres_ref = refs[:na], refs[na], refs[na + 1]
        mm, off = None, 0
        for a_ref, k in zip(a_refs, ks):
            part = jnp.dot(a_ref[...], b_ref[off:off + k, :], preferred_element_type=F32)
            mm = part if mm is None else mm + part
            off += k
        dy = ALPHA * dres_ref[...] + mm
        if last:
            refs[-1][...] = dy
            return
        xhat_ref, rstd_ref, g_ref = refs[na + 2:na + 5]
        dr_ref, drb_ref, dg_ref, db_ref = refs[-4:]
        xhat_v = xhat_ref[...]

        @pl.when(pl.program_id(0) == 0)
        def _():
            dg_ref[...] = jnp.zeros_like(dg_ref)
            db_ref[...] = jnp.zeros_like(db_ref)

        dg_ref[...] += jnp.sum(dy * xhat_v, axis=0, keepdims=True)
        db_ref[...] += jnp.sum(dy, axis=0, keepdims=True)
        dxh = dy * g_ref[...]
        m1 = jnp.mean(dxh, axis=-1, keepdims=True)
        m2 = jnp.mean(dxh * xhat_v, axis=-1, keepdims=True)
        dr = rstd_ref[...] * (dxh - m1 - xhat_v * m2)
        dr_ref[...] = dr
        drb_ref[...] = dr.astype(drb_ref.dtype)

    row = pl.BlockSpec((tm, D_MODEL), lambda i: (i, 0))
    vec = pl.BlockSpec((1, D_MODEL), lambda i: (0, 0))
    in_specs = [pl.BlockSpec((tm, k), lambda i: (i, 0)) for k in ks]
    in_specs += [pl.BlockSpec((sum(ks), D_MODEL), lambda i: (0, 0), pipeline_mode=pl.Buffered(1)), row]
    args = list(a_list) + [b, dres]
    if last:
        out_specs, out_shape = row, jax.ShapeDtypeStruct((T, D_MODEL), F32)
    else:
        in_specs += [row, pl.BlockSpec((tm, 1), lambda i: (i, 0)), vec]
        args += [xhat, rstd, g]
        out_specs = [row, row, vec, vec]
        out_shape = [jax.ShapeDtypeStruct((T, D_MODEL), F32), jax.ShapeDtypeStruct((T, D_MODEL), BF16),
                     jax.ShapeDtypeStruct((1, D_MODEL), F32), jax.ShapeDtypeStruct((1, D_MODEL), F32)]
    return pl.pallas_call(
        body, name=name, grid=(T // tm,),
        in_specs=in_specs + [pl.BlockSpec(memory_space=pl.ANY)] * nd,
        out_specs=out_specs, out_shape=out_shape,
        compiler_params=_cparams(("parallel",) if last else ("arbitrary",)),
    )(*args, *deps)


DW_TM = 1408
FF_TN = 256
SAVED_GU = BF16


def _mm_swiglu_fwd(h, w_gu, deps=()):
    T = h.shape[0]
    nj = D_FF // FF_TN
    nd = len(deps)

    def body(*refs):
        h_ref, wg_ref, wu_ref = refs[:3]
        g_ref, u_ref, act_ref = refs[3 + nd:]
        hv = h_ref[...]
        gv = lax.dot_general(hv, wg_ref[...], _DN["nt"], preferred_element_type=F32)
        uv = lax.dot_general(hv, wu_ref[...], _DN["nt"], preferred_element_type=F32)
        g_ref[...] = gv.astype(g_ref.dtype)
        u_ref[...] = uv.astype(u_ref.dtype)
        act_ref[...] = (gv * jax.nn.sigmoid(gv) * uv).astype(act_ref.dtype)

    col = pl.BlockSpec((T, FF_TN), lambda j: (0, j))
    return pl.pallas_call(
        body, name="mm_gate_up_swiglu", grid=(nj,),
        in_specs=[pl.BlockSpec((T, D_MODEL), lambda j: (0, 0)),
                  pl.BlockSpec((FF_TN, D_MODEL), lambda j: (j, 0)),
                  pl.BlockSpec((FF_TN, D_MODEL), lambda j: (j + nj, 0))] + [pl.BlockSpec(memory_space=pl.ANY)] * nd,
        out_specs=[col, col, col],
        out_shape=[jax.ShapeDtypeStruct((T, D_FF), SAVED_GU), jax.ShapeDtypeStruct((T, D_FF), SAVED_GU),
                   jax.ShapeDtypeStruct((T, D_FF), BF16)],
        compiler_params=_cparams(("parallel",)),
    )(h, w_gu, w_gu, *deps)


def _mm_swiglu_bwd(dr, w_dn, g, u):
    T = dr.shape[0]

    def body(dr_ref, w_ref, g_ref, u_ref, dg_ref, du_ref, act_ref):
        da = lax.dot_general(dr_ref[...], w_ref[...], _DN["nt"], preferred_element_type=F32)
        gv, uv = g_ref[...].astype(F32), u_ref[...].astype(F32)
        s = jax.nn.sigmoid(gv)
        sg = gv * s
        act_ref[...] = (sg * uv).astype(act_ref.dtype)
        du_ref[...] = (da * sg).astype(du_ref.dtype)
        dg_ref[...] = (da * uv * (s * (1.0 + gv * (1.0 - s)))).astype(dg_ref.dtype)

    col = pl.BlockSpec((T, FF_TN), lambda j: (0, j))
    return pl.pallas_call(
        body, name="mm_dact_swiglu", grid=(D_FF // FF_TN,),
        in_specs=[pl.BlockSpec((T, D_MODEL), lambda j: (0, 0)), pl.BlockSpec((FF_TN, D_MODEL), lambda j: (j, 0)),
                  col, col],
        out_specs=[col, col, col],
        out_shape=[jax.ShapeDtypeStruct((T, D_FF), BF16)] * 3,
        compiler_params=_cparams(("parallel",)),
    )(dr, w_dn, g, u)


def _gelu(x):
    return 0.5 * x * (1.0 + lax.erf(x * INV_SQRT2))


def _gelu_grad(x):
    return 0.5 * (1.0 + lax.erf(x * INV_SQRT2)) + x * (jnp.exp(-0.5 * x * x) * INV_SQRT_2PI)


def _shift_down(z, k):
    row = lax.broadcasted_iota(jnp.int32, z.shape, 0)
    return jnp.where(row >= k, pltpu.roll(z, k, 0), 0.0)


def _shift_up(z, k):
    n = z.shape[0]
    row = lax.broadcasted_iota(jnp.int32, z.shape, 0)
    return jnp.where(row < n - k, pltpu.roll(z, n - k, 0), 0.0)


def _lo_mask(shape):
    return lax.broadcasted_iota(jnp.int32, shape, len(shape) - 1) < HALF


def _seg_mean(x, lo):
    a = jnp.sum(jnp.where(lo, x, 0.0), axis=-1, keepdims=True)
    b = jnp.sum(jnp.where(lo, 0.0, x), axis=-1, keepdims=True)
    return jnp.where(lo, a, b) * (1.0 / HALF)


def _pool_windows(first):
    lo = _lo_mask((1, LANES))
    return jnp.where(first, jnp.where(lo, 2.0, 4.0), jnp.where(lo, 8.0, 16.0)), lo


def _pool_mean_minus_token(p, first):
    wl, lo = _pool_windows(first)
    s2 = p + _shift_down(p, 1)
    s4 = s2 + _shift_down(s2, 2)
    s8 = s4 + _shift_down(s4, 4)
    s16 = s8 + _shift_down(s8, 8)
    win = jnp.where(first, jnp.where(lo, s2, s4), jnp.where(lo, s8, s16))
    t1 = (lax.broadcasted_iota(jnp.int32, p.shape, 0) + 1).astype(F32)
    count = jnp.minimum(t1, wl)
    return win / count - p, count


SGU_UNROLL = 4


def _tril_keep():
    r = lax.broadcasted_iota(jnp.int32, (2 * CHUNK, CHUNK), 0)
    s = lax.broadcasted_iota(jnp.int32, (2 * CHUNK, CHUNK), 1)
    return s <= (r & (CHUNK - 1))


def _sgu_chunk_fwd(u, v, g, wm, bias, lo):
    ug = _gelu(u)
    vg = _gelu(v)
    mu = _seg_mean(vg, lo)
    xc = vg - mu
    var = _seg_mean(xc * xc, lo)
    rstd = lax.rsqrt(var + LN_EPS)
    vn = xc * rstd
    vh = (vn * g).astype(BF16)
    mm2 = jnp.dot(wm, vh, preferred_element_type=F32)
    mixed = jnp.where(lo, mm2[:CHUNK], mm2[CHUNK:]) + bias
    return ug, vn, rstd, vh, mixed


def _mixer_fwd(proj, wconv, wpool_bd, pscale, lng, wsp, bias):
    T = proj.shape[0]
    nchunk = T // CHUNK

    def body(a_ref, b_ref, c_ref, wc_ref, wp_ref, ps_ref, lng_ref, wsp_ref, bias_ref, o_ref):
        j = pl.program_id(0)

        @pl.when(j < 3)
        def _conv():
            z = c_ref[...] * a_ref[...]
            w = wc_ref[...]
            y = w[0:1] * _shift_down(z, 2) + w[1:2] * _shift_down(z, 1) + w[2:3] * z
            o_ref[...] = (b_ref[...] * y).astype(o_ref.dtype)

        @pl.when((j >= 3) & (j < 5))
        def _pool():
            d, _ = _pool_mean_minus_token(a_ref[...], j == 3)
            y = jnp.dot(d.astype(BF16), wp_ref[...].astype(BF16), preferred_element_type=F32)
            o_ref[...] = (y * ps_ref[...]).astype(o_ref.dtype)

        @pl.when(j >= 5)
        def _sgu():
            lo = _lo_mask((CHUNK, LANES))
            wm = jnp.where(_tril_keep(), wsp_ref[...], 0.0).astype(BF16)
            bias_t = bias_ref[...]
            g = lng_ref[...]

            def chunk(n, carry):
                rows = pl.ds(pl.multiple_of(n * CHUNK, CHUNK), CHUNK)
                ug, _, _, _, mixed = _sgu_chunk_fwd(a_ref[rows, :], b_ref[rows, :], g, wm, bias_t, lo)
                o_ref[rows, :] = (ug * mixed).astype(o_ref.dtype)
                return carry

            lax.fori_loop(0, nchunk, chunk, 0, unroll=SGU_UNROLL)

    def col(f):
        return lambda j: (0, f(j))

    clip = lambda v, lo, hi: jnp.minimum(jnp.maximum(v, lo), hi)
    return pl.pallas_call(
        body,
        name="mixer_fwd",
        grid=(8,),
        in_specs=[
            pl.BlockSpec((T, LANES), col(lambda j: jnp.where(j < 3, j, jnp.where(j < 5, j + 6, j + 6)))),
            pl.BlockSpec((T, LANES), col(lambda j: jnp.where(j < 3, j + 3, jnp.where(j < 5, 5, j + 9)))),
            pl.BlockSpec((T, LANES), col(lambda j: jnp.where(j < 3, j + 6, 8))),
            pl.BlockSpec((3, LANES), col(lambda j: clip(j, 0, 2))),
            pl.BlockSpec((None, LANES, LANES), lambda j: (clip(j - 3, 0, 1), 0, 0)),
            pl.BlockSpec((1, LANES), col(lambda j: clip(j - 3, 0, 1))),
            pl.BlockSpec((1, LANES), col(lambda j: clip(j - 5, 0, 2))),
            pl.BlockSpec((None, 2 * CHUNK, CHUNK), lambda j: (clip(j - 5, 0, 2), 0, 0)),
            pl.BlockSpec((None, CHUNK, LANES), lambda j: (clip(j - 5, 0, 2), 0, 0)),
        ],
        out_specs=pl.BlockSpec((T, LANES), lambda j: (0, j)),
        out_shape=jax.ShapeDtypeStruct((T, D_MODEL), BF16),
        compiler_params=_cparams(("arbitrary",)),
    )(proj, proj, proj, wconv, wpool_bd, pscale, lng, wsp, bias)


def _mixer_bwd(proj, dmix, wconv, wpool_bd, pscale, lng, wsp, bias):
    T = proj.shape[0]
    nchunk = T // CHUNK

    def body(a_ref, b_ref, c_ref, dm_ref, wc_ref, wp_ref, ps_ref, lng_ref, wsp_ref, bias_ref,
             o_ref, dwc_ref, dwp_ref, dps_ref, dlng_ref, dwsp_ref, dbias_ref, keep1, keep2):
        k = pl.program_id(0)

        @pl.when(k < 3)
        def _conv():
            xa, gb, gc, dya = a_ref[...], b_ref[...], c_ref[...], dm_ref[...]
            w = wc_ref[...]
            z = gc * xa
            z1 = _shift_down(z, 1)
            z2 = _shift_down(z, 2)
            y = w[0:1] * z2 + w[1:2] * z1 + w[2:3] * z
            dyv = dya * gb
            dz = w[2:3] * dyv + w[1:2] * _shift_up(dyv, 1) + w[0:1] * _shift_up(dyv, 2)
            dwc_ref[0:1, :] = jnp.sum(dyv * z2, axis=0, keepdims=True)
            dwc_ref[1:2, :] = jnp.sum(dyv * z1, axis=0, keepdims=True)
            dwc_ref[2:3, :] = jnp.sum(dyv * z, axis=0, keepdims=True)
            o_ref[...] = (dz * gc).astype(o_ref.dtype)
            keep1[k] = (dya * y).astype(keep1.dtype)
            keep1[k + 3] = (dz * xa).astype(keep1.dtype)

        @pl.when((k >= 3) & (k < 9))
        def _emit_gb_gc():
            o_ref[...] = keep1[k - 3]

        @pl.when((k >= 9) & (k < 11))
        def _pool():
            first = k == 9
            p, dyb = a_ref[...], dm_ref[...]
            d, count = _pool_mean_minus_token(p, first)
            w2 = wp_ref[...].astype(BF16)
            db = d.astype(BF16)
            y = jnp.dot(db, w2, preferred_element_type=F32)
            dps_ref[...] = jnp.sum(dyb * y, axis=0, keepdims=True)
            dyv = (dyb * ps_ref[...]).astype(BF16)
            dd = lax.dot_general(dyv, w2, _DN["nt"], preferred_element_type=F32)
            dwp_ref[...] = lax.dot_general(db, dyv, _DN["tn"], preferred_element_type=F32)
            dwin = dd / count
            a2 = dwin + _shift_up(dwin, 1)
            a4 = a2 + _shift_up(a2, 2)
            a8 = a4 + _shift_up(a4, 4)
            a16 = a8 + _shift_up(a8, 8)
            _, lo = _pool_windows(first)
            back = jnp.where(first, jnp.where(lo, a2, a4), jnp.where(lo, a8, a16))
            o_ref[...] = (back - dd).astype(o_ref.dtype)

        @pl.when((k >= 11) & (k < 14))
        def _sgu():
            lo = _lo_mask((CHUNK, LANES))
            keep = _tril_keep()
            wm = jnp.where(keep, wsp_ref[...], 0.0).astype(BF16)
            bias_t = bias_ref[...]
            g = lng_ref[...]
            dwsp_ref[...] = jnp.zeros_like(dwsp_ref)
            dbias_ref[...] = jnp.zeros_like(dbias_ref)
            dlng_ref[...] = jnp.zeros_like(dlng_ref)

            def chunk(n, carry):
                rows = pl.ds(pl.multiple_of(n * CHUNK, CHUNK), CHUNK)
                u, v, dyc = a_ref[rows, :], b_ref[rows, :], dm_ref[rows, :]
                ug, vn, rstd, vh, mixed = _sgu_chunk_fwd(u, v, g, wm, bias_t, lo)
                dmx = dyc * ug
                o_ref[rows, :] = (dyc * mixed * _gelu_grad(u)).astype(o_ref.dtype)
                dbias_ref[...] += dmx
                dst = jnp.concatenate([jnp.where(lo, dmx, 0.0), jnp.where(lo, 0.0, dmx)], axis=0).astype(BF16)
                dwsp_ref[...] += lax.dot_general(dst, vh, _DN["nt"], preferred_element_type=F32)
                dvh = lax.dot_general(wm, dst, _DN["tn"], preferred_element_type=F32)
                dlng_ref[...] += jnp.sum(dvh * vn, axis=0, keepdims=True)
                dvn = dvh * g
                m1 = _seg_mean(dvn, lo)
                m2 = _seg_mean(dvn * vn, lo)
                dvg = rstd * (dvn - m1 - vn * m2)
                keep2[k - 11, rows, :] = (dvg * _gelu_grad(v)).astype(keep2.dtype)
                return carry

            lax.fori_loop(0, nchunk, chunk, 0, unroll=SGU_UNROLL)
            dwsp_ref[...] = jnp.where(keep, dwsp_ref[...], 0.0)
            dbt = dbias_ref[...]
            lane = lax.broadcasted_iota(jnp.int32, (CHUNK, LANES), 1)
            sa = jnp.sum(jnp.where(lo, dbt, 0.0), axis=-1, keepdims=True)
            sb = jnp.sum(jnp.where(lo, 0.0, dbt), axis=-1, keepdims=True)
            dbias_ref[...] = jnp.where(lane == 0, sa, jnp.where(lane == 1, sb, 0.0))

        @pl.when(k >= 14)
        def _emit_v():
            o_ref[...] = keep2[k - 14]

    def col(f):
        return lambda k: (0, f(k))

    clip = lambda v, lo, hi: jnp.minimum(jnp.maximum(v, lo), hi)
    view_a = lambda k: jnp.where(k < 3, k, jnp.where(k < 9, 2, jnp.where(k < 14, k, 13)))
    view_b = lambda k: jnp.where(k < 3, k + 3, jnp.where(k < 11, 5, jnp.where(k < 14, k + 3, 16)))
    view_c = lambda k: jnp.where(k < 3, k + 6, 8)
    view_dm = lambda k: jnp.where(k < 3, k, jnp.where(k < 9, 2, jnp.where(k < 14, k - 6, 7)))
    return pl.pallas_call(
        body,
        name="mixer_bwd",
        grid=(17,),
        in_specs=[
            pl.BlockSpec((T, LANES), col(view_a)),
            pl.BlockSpec((T, LANES), col(view_b)),
            pl.BlockSpec((T, LANES), col(view_c)),
            pl.BlockSpec((T, LANES), col(view_dm)),
            pl.BlockSpec((3, LANES), col(lambda k: clip(k, 0, 2))),
            pl.BlockSpec((None, LANES, LANES), lambda k: (clip(k - 9, 0, 1), 0, 0)),
            pl.BlockSpec((1, LANES), col(lambda k: clip(k - 9, 0, 1))),
            pl.BlockSpec((1, LANES), col(lambda k: clip(k - 11, 0, 2))),
            pl.BlockSpec((None, 2 * CHUNK, CHUNK), lambda k: (clip(k - 11, 0, 2), 0, 0)),
            pl.BlockSpec((None, CHUNK, LANES), lambda k: (clip(k - 11, 0, 2), 0, 0)),
        ],
        out_specs=[
            pl.BlockSpec((T, LANES), lambda k: (0, k)),
            pl.BlockSpec((3, LANES), col(lambda k: clip(k, 0, 2))),
            pl.BlockSpec((None, LANES, LANES), lambda k: (clip(k - 9, 0, 1), 0, 0)),
            pl.BlockSpec((1, LANES), col(lambda k: clip(k - 9, 0, 1))),
            pl.BlockSpec((1, LANES), col(lambda k: clip(k - 11, 0, 2))),
            pl.BlockSpec((None, 2 * CHUNK, CHUNK), lambda k: (clip(k - 11, 0, 2), 0, 0)),
            pl.BlockSpec((None, CHUNK, LANES), lambda k: (clip(k - 11, 0, 2), 0, 0)),
        ],
        out_shape=[
            jax.ShapeDtypeStruct((T, IN_W), BF16),
            jax.ShapeDtypeStruct((3, CONV_W), F32),
            jax.ShapeDtypeStruct((2, LANES, LANES), F32),
            jax.ShapeDtypeStruct((1, POOL_W), F32),
            jax.ShapeDtypeStruct((1, SGU_W), F32),
            jax.ShapeDtypeStruct((3, 2 * CHUNK, CHUNK), F32),
            jax.ShapeDtypeStruct((3, CHUNK, LANES), F32),
        ],
        scratch_shapes=[pltpu.VMEM((6, T, LANES), BF16), pltpu.VMEM((3, T, LANES), BF16)],
        compiler_params=_cparams(("arbitrary",)),
    )(proj, proj, proj, dmix, wconv, wpool_bd, pscale, lng, wsp, bias)


def _ln_fwd(prev, pg, pb, mmout, g, b, tm=256):
    T = prev.shape[0]

    def body(prev_ref, pg_ref, pb_ref, mm_ref, g_ref, b_ref, xhat_ref, rstd_ref, y_ref):
        r = ALPHA * (prev_ref[...] * pg_ref[...] + pb_ref[...]) + mm_ref[...]
        mu = jnp.mean(r, axis=-1, keepdims=True)
        xc = r - mu
        var = jnp.mean(xc * xc, axis=-1, keepdims=True)
        rstd = lax.rsqrt(var + LN_EPS)
        xhat = xc * rstd
        xhat_ref[...] = xhat
        rstd_ref[...] = rstd
        y_ref[...] = (xhat * g_ref[...] + b_ref[...]).astype(y_ref.dtype)

    row = pl.BlockSpec((tm, D_MODEL), lambda i: (i, 0))
    vec = pl.BlockSpec((1, D_MODEL), lambda i: (0, 0))
    return pl.pallas_call(
        body,
        name="ln_fwd",
        grid=(T // tm,),
        in_specs=[row, vec, vec, row, vec, vec],
        out_specs=[row, pl.BlockSpec((tm, 1), lambda i: (i, 0)), row],
        out_shape=[jax.ShapeDtypeStruct((T, D_MODEL), F32), jax.ShapeDtypeStruct((T, 1), F32),
                   jax.ShapeDtypeStruct((T, D_MODEL), BF16)],
        compiler_params=_cparams(("parallel",)),
    )(prev, pg, pb, mmout, g, b)


def _ln_bwd(dres, dmm, xhat, rstd, g, tm=256, deps=()):
    T = xhat.shape[0]
    has_res = dres is not None
    nd = len(deps)

    def body(*refs):
        refs = refs[:len(refs) - 4 - nd] + refs[len(refs) - 4:]
        if has_res:
            dres_ref, dmm_ref, xhat_ref, rstd_ref, g_ref, dr_ref, drb_ref, dg_ref, db_ref = refs
            dy = ALPHA * dres_ref[...] + dmm_ref[...]
        else:
            dmm_ref, xhat_ref, rstd_ref, g_ref, dr_ref, drb_ref, dg_ref, db_ref = refs
            dy = dmm_ref[...]
        xhat_v = xhat_ref[...]

        @pl.when(pl.program_id(0) == 0)
        def _():
            dg_ref[...] = jnp.zeros_like(dg_ref)
            db_ref[...] = jnp.zeros_like(db_ref)

        dg_ref[...] += jnp.sum(dy * xhat_v, axis=0, keepdims=True)
        db_ref[...] += jnp.sum(dy, axis=0, keepdims=True)
        dxh = dy * g_ref[...]
        m1 = jnp.mean(dxh, axis=-1, keepdims=True)
        m2 = jnp.mean(dxh * xhat_v, axis=-1, keepdims=True)
        dr = rstd_ref[...] * (dxh - m1 - xhat_v * m2)
        dr_ref[...] = dr
        drb_ref[...] = dr.astype(drb_ref.dtype)

    row = pl.BlockSpec((tm, D_MODEL), lambda i: (i, 0))
    vec = pl.BlockSpec((1, D_MODEL), lambda i: (0, 0))
    in_specs = ([row] if has_res else []) + [row, row, pl.BlockSpec((tm, 1), lambda i: (i, 0)), vec]
    in_specs += [pl.BlockSpec(memory_space=pl.ANY)] * nd
    args = ([dres] if has_res else []) + [dmm, xhat, rstd, g] + list(deps)
    return pl.pallas_call(
        body,
        name="ln_bwd_res" if has_res else "ln_bwd",
        grid=(T // tm,),
        in_specs=in_specs,
        out_specs=[row, row, vec, vec],
        out_shape=[jax.ShapeDtypeStruct((T, D_MODEL), F32), jax.ShapeDtypeStruct((T, D_MODEL), BF16),
                   jax.ShapeDtypeStruct((1, D_MODEL), F32), jax.ShapeDtypeStruct((1, D_MODEL), F32)],
        compiler_params=_cparams(("arbitrary",)),
    )(*args)


def _loss_head(xhat, g, b, target, tm=256):
    T = xhat.shape[0]

    def body(xhat_ref, g_ref, b_ref, t_ref, loss_ref, dy_ref):
        err = xhat_ref[...] * g_ref[...] + b_ref[...] - t_ref[...]

        @pl.when(pl.program_id(0) == 0)
        def _():
            loss_ref[...] = jnp.zeros_like(loss_ref)

        part = jnp.sum(jnp.sum(err * err, axis=-1, keepdims=True), axis=0, keepdims=True)
        loss_ref[...] += jnp.broadcast_to(part * (0.5 / D_MODEL), loss_ref.shape)
        dy_ref[...] = err * (1.0 / D_MODEL)

    row = pl.BlockSpec((tm, D_MODEL), lambda i: (i, 0))
    vec = pl.BlockSpec((1, D_MODEL), lambda i: (0, 0))
    return pl.pallas_call(
        body,
        name="loss_head",
        grid=(T // tm,),
        in_specs=[row, vec, vec, row],
        out_specs=[pl.BlockSpec((8, LANES), lambda i: (0, 0)), row],
        out_shape=[jax.ShapeDtypeStruct((8, LANES), F32), jax.ShapeDtypeStruct((T, D_MODEL), F32)],
        compiler_params=_cparams(("arbitrary",)),
    )(xhat, g, b, target)


def _residual_out(dres, dmm, tm=256):
    T = dres.shape[0]

    def body(a_ref, b_ref, o_ref):
        o_ref[...] = ALPHA * a_ref[...] + b_ref[...]

    row = pl.BlockSpec((tm, D_MODEL), lambda i: (i, 0))
    return pl.pallas_call(
        body, name="residual_out", grid=(T // tm,), in_specs=[row, row], out_specs=row,
        out_shape=jax.ShapeDtypeStruct((T, D_MODEL), F32), compiler_params=_cparams(("parallel",)),
    )(dres, dmm)


SW_TC = 1408


def _swiglu_fwd(gu, tm=128):
    T = gu.shape[0]

    def body(gu_ref, o_ref):
        gv = gu_ref[:, :D_FF]
        o_ref[...] = (gv * jax.nn.sigmoid(gv) * gu_ref[:, D_FF:]).astype(o_ref.dtype)

    return pl.pallas_call(
        body, name="swiglu_fwd", grid=(T // tm,),
        in_specs=[pl.BlockSpec((tm, 2 * D_FF), lambda i: (i, 0))],
        out_specs=pl.BlockSpec((tm, D_FF), lambda i: (i, 0)),
        out_shape=jax.ShapeDtypeStruct((T, D_FF), BF16), compiler_params=_cparams(("parallel",)),
    )(gu)


def _swiglu_bwd(gu, dact, tm=128):
    T = gu.shape[0]

    def body(gu_ref, da_ref, dgu_ref, act_ref):
        gv, uv, da = gu_ref[:, :D_FF], gu_ref[:, D_FF:], da_ref[...]
        s = jax.nn.sigmoid(gv)
        sg = gv * s
        act_ref[...] = (sg * uv).astype(act_ref.dtype)
        dgu_ref[:, D_FF:] = (da * sg).astype(dgu_ref.dtype)
        dgu_ref[:, :D_FF] = (da * uv * (s * (1.0 + gv * (1.0 - s)))).astype(dgu_ref.dtype)

    wide = pl.BlockSpec((tm, 2 * D_FF), lambda i: (i, 0))
    half = pl.BlockSpec((tm, D_FF), lambda i: (i, 0))
    return pl.pallas_call(
        body, name="swiglu_bwd", grid=(T // tm,),
        in_specs=[wide, half], out_specs=[wide, half],
        out_shape=[jax.ShapeDtypeStruct((T, 2 * D_FF), BF16), jax.ShapeDtypeStruct((T, D_FF), BF16)],
        compiler_params=_cparams(("parallel",)),
    )(gu, dact)


def _adamw(w, g, m, v, tr):
    R, C = w.shape[-2:]
    assert R % tr == 0
    c1 = 1.0 - ADAM_B1 ** ADAM_STEP
    c2 = 1.0 - ADAM_B2 ** ADAM_STEP

    def body(w_ref, g_ref, m_ref, v_ref, d_ref, mo_ref, vo_ref):
        gv = g_ref[...]
        mn = ADAM_B1 * m_ref[...] + (1.0 - ADAM_B1) * gv
        vn = ADAM_B2 * v_ref[...] + (1.0 - ADAM_B2) * (gv * gv)
        d_ref[...] = -ADAM_LR * ((mn / c1) / (jnp.sqrt(vn / c2) + ADAM_EPS) + ADAM_WD * w_ref[...])
        mo_ref[...] = mn
        vo_ref[...] = vn

    if w.ndim == 2:
        grid, blk = (R // tr,), pl.BlockSpec((tr, C), lambda i: (i, 0))
    else:
        grid, blk = (w.shape[0], R // tr), pl.BlockSpec((None, tr, C), lambda l, i: (l, i, 0))
    return pl.pallas_call(
        body, name="adamw", grid=grid, in_specs=[blk] * 4, out_specs=[blk] * 3,
        out_shape=[jax.ShapeDtypeStruct(w.shape, F32)] * 3, compiler_params=_cparams(("parallel",) * len(grid)),
    )(w, g, m, v)


def _my_place():
    return lax.axis_index("x"), lax.axis_index("y"), lax.axis_index("c")


ANY = pl.BlockSpec(memory_space=pl.ANY)
HBM = pl.BlockSpec(memory_space=pltpu.HBM)
SEM = pl.BlockSpec(memory_space=pltpu.SEMAPHORE)
EFFECT = pltpu.SideEffectType.DATAFLOW_SIDE_EFFECTING


def _in_hbm(a):
    return pltpu.with_memory_space_constraint(a, pltpu.HBM)


def _block_rows(ref, dev):
    r = ref.shape[0] // N_DEV
    start = pl.multiple_of((4 * dev[0] + 2 * dev[1] + dev[2]) * r, 16)
    return ref.at[pl.ds(start, r), :]


def _ag_first_copies(s_refs, land_refs, send_sems, recv_sems, receiving):
    x, y, c = _my_place()
    peers = [(x, y, 1 - c)] + [(*chip, c) for chip in _other_chips(x, y)]
    copies = []
    for k, peer in enumerate(peers):
        block = peer if receiving else (x, y, c)
        copies += [pltpu.make_async_remote_copy(
            src_ref=s_refs[w], dst_ref=_block_rows(land_refs[w], block),
            send_sem=send_sems.at[k * len(s_refs) + w], recv_sem=recv_sems.at[k * len(s_refs) + w],
            device_id=peer, device_id_type=MESH)
            for w in range(len(s_refs))]
    return copies


def _ag_start(shards, layer, after=()):
    nw = len(shards)

    def body(*refs):
        s_refs, land_refs = refs[:nw], refs[nw:2 * nw]
        token = refs[-1]
        sems = 2 * nw + len(after)
        for cp in _ag_first_copies(s_refs, land_refs, refs[sems], refs[sems + 1], False):
            cp.start()
        token[...] = jnp.zeros_like(token)

    lands = [lax.empty((N_DEV * s.shape[0], D_MODEL), BF16) for s in shards]
    out = pl.pallas_call(
        body, name="ag_start_%s" % layer,
        in_specs=[HBM] * (2 * nw) + [ANY] * len(after),
        out_specs=(SEM, SEM, *[HBM] * (2 * nw), pl.BlockSpec(memory_space=pltpu.VMEM)),
        out_shape=(pltpu.SemaphoreType.DMA((4 * nw,)), pltpu.SemaphoreType.DMA((4 * nw,)),
                   *[pltpu.HBM(a.shape, a.dtype) for a in list(shards) + lands],
                   jax.ShapeDtypeStruct((8, LANES), F32)),
        input_output_aliases={i: 2 + i for i in range(2 * nw)},
        compiler_params=pltpu.CompilerParams(has_side_effects=EFFECT),
    )(*[_in_hbm(a) for a in list(shards) + lands], *after)
    return out[0], out[1], out[2:2 + nw], out[2 + nw:2 + 2 * nw], out[-1]


def _ag_wait(send_sems, recv_sems, shards, lands, after, layer):
    nw = len(shards)

    def body(*refs):
        s_refs, land_refs = refs[:nw], refs[nw:2 * nw]
        for cp in _ag_first_copies(s_refs, land_refs, refs[2 * nw], refs[2 * nw + 1], True):
            cp.wait_send()
            cp.wait_recv()

    out = pl.pallas_call(
        body, name="ag_wait_%s" % layer,
        in_specs=[HBM] * (2 * nw) + [SEM, SEM] + [ANY] * len(after),
        out_specs=[HBM] * (2 * nw),
        out_shape=[pltpu.HBM(a.shape, a.dtype) for a in list(shards) + list(lands)],
        input_output_aliases={i: i for i in range(2 * nw)},
        compiler_params=pltpu.CompilerParams(has_side_effects=EFFECT),
    )(*shards, *lands, send_sems, recv_sems, *after)
    return out[:nw], out[nw:]


def _ag_pass_on(shards, lands):
    nw = len(shards)

    def body(*refs):
        s_refs, g_refs = refs[:nw], refs[2 * nw:3 * nw]
        send_sems, recv_sems, local_sems = refs[3 * nw:3 * nw + 3]
        stage = refs[3 * nw + 3:]
        x, y, c = _my_place()
        load = [pltpu.make_async_copy(s_refs[w], stage[w], local_sems.at[w]) for w in range(nw)]
        mine = [pltpu.make_async_copy(stage[w], _block_rows(g_refs[w], (x, y, c)), local_sems.at[w])
                for w in range(nw)]
        for cp in load:
            cp.start()
        sends, arrivals = [], []
        for j, chip in enumerate(_other_chips(x, y)):
            for w in range(nw):
                rows_out = _block_rows(g_refs[w], (*chip, c))
                rows_in = _block_rows(g_refs[w], (*chip, 1 - c))
                sends.append(pltpu.make_async_remote_copy(
                    src_ref=rows_out, dst_ref=rows_out, send_sem=send_sems.at[j, w], recv_sem=recv_sems.at[j, w],
                    device_id=(x, y, 1 - c), device_id_type=MESH))
                arrivals.append(pltpu.make_async_remote_copy(
                    src_ref=rows_in, dst_ref=rows_in, send_sem=send_sems.at[j, w], recv_sem=recv_sems.at[j, w],
                    device_id=(x, y, 1 - c), device_id_type=MESH))
        for cp in sends:
            cp.start()
        for w in range(nw):
            load[w].wait()
            mine[w].start()
        for cp in arrivals:
            cp.wait_recv()
        for cp in sends:
            cp.wait_send()
        for cp in mine:
            cp.wait()

    return pl.pallas_call(
        body, name="ag_pass_on",
        in_specs=[ANY] * (2 * nw), out_specs=[ANY] * nw,
        out_shape=[jax.ShapeDtypeStruct(a.shape, a.dtype) for a in lands],
        input_output_aliases={nw + i: i for i in range(nw)},
        scratch_shapes=[pltpu.SemaphoreType.DMA((3, nw)), pltpu.SemaphoreType.DMA((3, nw)),
                        pltpu.SemaphoreType.DMA((nw,))] + [pltpu.VMEM(s.shape, s.dtype) for s in shards],
        compiler_params=_cparams(),
    )(*shards, *lands)


def _rs_sibling_exchange(parts):
    nw = len(parts)

    def body(*refs):
        p_refs, o_refs = refs[:nw], refs[nw:2 * nw]
        send_sems, recv_sems = refs[2 * nw:]
        x, y, c = _my_place()
        copies = [pltpu.make_async_remote_copy(
            src_ref=p_refs[w].at[:, 1 - c], dst_ref=o_refs[w],
            send_sem=send_sems.at[w], recv_sem=recv_sems.at[w], device_id=(x, y, 1 - c), device_id_type=MESH)
            for w in range(nw)]
        for cp in copies:
            cp.start()
        for cp in copies:
            cp.wait()

    return pl.pallas_call(
        body, name="rs_sibling_exchange",
        in_specs=[ANY] * nw, out_specs=[ANY] * nw,
        out_shape=[jax.ShapeDtypeStruct(p.shape[:1] + p.shape[2:], BF16) for p in parts],
        scratch_shapes=[pltpu.SemaphoreType.DMA((nw,)), pltpu.SemaphoreType.DMA((nw,))],
    )(*parts)


def _rs_chip_sum(parts, gots, c):
    n = len(parts)

    def body(c_ref, *refs):
        for p_ref, g_ref, o_ref in zip(refs[:n], refs[n:2 * n], refs[2 * n:]):
            o_ref[...] = (p_ref[...].astype(F32) + g_ref[...].astype(F32)).astype(o_ref.dtype)

    mine = [pl.BlockSpec((None, None, p.shape[2], D_MODEL), lambda q, c_ref: (q, c_ref[0], 0, 0)) for p in parts]
    theirs = [pl.BlockSpec((None, g.shape[1], D_MODEL), lambda q, c_ref: (q, 0, 0)) for g in gots]
    return pl.pallas_call(
        body, name="rs_chip_sum",
        grid_spec=pltpu.PrefetchScalarGridSpec(
            num_scalar_prefetch=1, grid=(4,), in_specs=mine + theirs, out_specs=theirs),
        out_shape=[jax.ShapeDtypeStruct(g.shape, BF16) for g in gots],
        compiler_params=_cparams(("parallel",)),
    )(c, *parts, *gots)


def _other_chips(x, y):
    return [(1 - x, y), (x, 1 - y), (1 - x, 1 - y)]


def _rs_chip_copies(s_refs, land_refs, send_sems, recv_sems):
    x, y, c = _my_place()
    copies = []
    for k, chip in enumerate(_other_chips(x, y)):
        q = 2 * chip[0] + chip[1]
        copies += [pltpu.make_async_remote_copy(
            src_ref=s_refs[w].at[q], dst_ref=land_refs[w].at[k],
            send_sem=send_sems.at[k * len(s_refs) + w], recv_sem=recv_sems.at[k * len(s_refs) + w],
            device_id=(*chip, c), device_id_type=MESH)
            for w in range(len(s_refs))]
    return copies


def _rs_chip_start(sums, layer):
    nw = len(sums)

    def body(*refs):
        s_refs, land_refs = refs[:nw], refs[nw:2 * nw]
        send_sems, recv_sems = refs[2 * nw], refs[2 * nw + 1]
        token = refs[-1]
        for cp in _rs_chip_copies(s_refs, land_refs, send_sems, recv_sems):
            cp.start()
        token[...] = jnp.zeros_like(token)

    lands = [lax.empty((3,) + s.shape[1:], BF16) for s in sums]
    out = pl.pallas_call(
        body, name="rs_chip_start_%s" % layer,
        in_specs=[HBM] * (2 * nw),
        out_specs=(SEM, SEM, *[HBM] * (2 * nw), pl.BlockSpec(memory_space=pltpu.VMEM)),
        out_shape=(pltpu.SemaphoreType.DMA((3 * nw,)), pltpu.SemaphoreType.DMA((3 * nw,)),
                   *[pltpu.HBM(a.shape, a.dtype) for a in list(sums) + lands],
                   jax.ShapeDtypeStruct((8, LANES), F32)),
        input_output_aliases={i: 2 + i for i in range(2 * nw)},
        compiler_params=pltpu.CompilerParams(has_side_effects=EFFECT),
    )(*[_in_hbm(a) for a in list(sums) + lands])
    return out[0], out[1], out[2:2 + nw], out[2 + nw:2 + 2 * nw], out[-1]


def _rs_chip_wait(send_sems, recv_sems, sums, lands, after, layer):
    nw = len(sums)

    def body(*refs):
        s_refs, land_refs = refs[:nw], refs[nw:2 * nw]
        for cp in _rs_chip_copies(s_refs, land_refs, refs[2 * nw], refs[2 * nw + 1]):
            cp.wait_send()
            cp.wait_recv()

    out = pl.pallas_call(
        body, name="rs_chip_wait_%s" % layer,
        in_specs=[HBM] * (2 * nw) + [SEM, SEM] + [ANY] * len(after),
        out_specs=[HBM] * (2 * nw),
        out_shape=[pltpu.HBM(a.shape, a.dtype) for a in list(sums) + list(lands)],
        input_output_aliases={i: i for i in range(2 * nw)},
        compiler_params=pltpu.CompilerParams(has_side_effects=EFFECT),
    )(*sums, *lands, send_sems, recv_sems, *after)
    return out[:nw], out[nw:]


def _rs_finish(sums, gots, q, layer, into):
    n = len(sums)

    def body(q_ref, *refs):
        for s_ref, g_ref, o_ref in zip(refs[:n], refs[n:2 * n], refs[len(refs) - n:]):
            o_ref[...] = ((s_ref[...].astype(F32) + g_ref[0].astype(F32)) + g_ref[1].astype(F32)) + g_ref[2].astype(F32)

    rows = [s.shape[1] for s in sums]
    in_specs = [pl.BlockSpec((None, r, D_MODEL), lambda i, q_ref: (q_ref[0], 0, 0)) for r in rows]
    in_specs += [pl.BlockSpec((3, r, D_MODEL), lambda i, q_ref: (0, 0, 0)) for r in rows]
    args = [q, *sums, *gots]
    aliases = {}
    if into is not None:
        in_specs += [ANY] * n
        aliases = {len(args) + i: i for i in range(n)}
        args += list(into)
    return pl.pallas_call(
        body, name="rs_finish",
        grid_spec=pltpu.PrefetchScalarGridSpec(
            num_scalar_prefetch=1, grid=(1,), in_specs=in_specs,
            out_specs=[pl.BlockSpec((None, r, D_MODEL), lambda i, q_ref: (layer, 0, 0)) for r in rows]),
        out_shape=[jax.ShapeDtypeStruct((DEPTH, r, D_MODEL), F32) for r in rows],
        input_output_aliases=aliases,
        compiler_params=_cparams(("arbitrary",)),
    )(*args)


def _allreduce_small(vec):
    R = vec.shape[0]
    assert R % (8 * N_DEV) == 0
    P = R // N_DEV

    def body(v_ref, o_ref, buf, send1, recv1, send2, recv2):
        x, y, c = _my_place()
        me = 4 * x + 2 * y + c

        def piece(ref, d):
            return ref.at[pl.ds(pl.multiple_of(d * P, 8), P), :]

        def peer(k):
            p = me ^ k
            return p, (p >> 2, (p >> 1) & 1, p & 1)

        scatter = []
        for k in range(1, N_DEV):
            p, where = peer(k)
            scatter.append(pltpu.make_async_remote_copy(
                src_ref=piece(v_ref, p), dst_ref=buf.at[k], send_sem=send1.at[k - 1], recv_sem=recv1.at[k - 1],
                device_id=where, device_id_type=MESH))
        for cp in scatter:
            cp.start()
        buf[0] = piece(v_ref, me)[...]
        for cp in scatter:
            cp.wait()
        acc = buf[me]
        for d in range(1, N_DEV):
            acc = acc + buf[me ^ d]
        piece(o_ref, me)[...] = acc
        spread, arrivals = [], []
        for k in range(1, N_DEV):
            p, where = peer(k)
            spread.append(pltpu.make_async_remote_copy(
                src_ref=piece(o_ref, me), dst_ref=piece(o_ref, me), send_sem=send2.at[k - 1], recv_sem=recv2.at[k - 1],
                device_id=where, device_id_type=MESH))
            arrivals.append(pltpu.make_async_remote_copy(
                src_ref=piece(o_ref, p), dst_ref=piece(o_ref, p), send_sem=send2.at[k - 1], recv_sem=recv2.at[k - 1],
                device_id=where, device_id_type=MESH))
        for cp in spread:
            cp.start()
        for cp in arrivals:
            cp.wait_recv()
        for cp in spread:
            cp.wait_send()

    sems = pltpu.SemaphoreType.DMA((N_DEV - 1,))
    return pl.pallas_call(
        body, name="allreduce_small",
        in_specs=[pl.BlockSpec(memory_space=pltpu.VMEM)], out_specs=pl.BlockSpec(memory_space=pltpu.VMEM),
        out_shape=jax.ShapeDtypeStruct((R, LANES), F32),
        scratch_shapes=[pltpu.VMEM((N_DEV, P, LANES), F32), sems, sems, sems, sems],
        compiler_params=_cparams(),
    )(vec)


def _pack(arrs):
    flat = jnp.concatenate([a.reshape(-1) for a in arrs])
    pad = (-flat.shape[0]) % (8 * N_DEV * LANES)
    return jnp.pad(flat, (0, pad)).reshape(-1, LANES)


def _unpack(packed, shapes):
    flat = packed.reshape(-1)
    out, off = [], 0
    for s in shapes:
        n = math.prod(s)
        out.append(flat[off:off + n].reshape(s))
        off += n
    return out


def kernel(x, w_in, w_conv, w_pool, pool_scale, sgu_ln_g, w_spatial, b_spatial, w_o, ln1_g, ln1_b, w_gate_up, w_down, ln2_g, ln2_b, loss_target, m_w_in, m_w_conv, m_w_pool, m_pool_scale, m_sgu_ln_g, m_w_spatial, m_b_spatial, m_w_o, m_ln1_g, m_ln1_b, m_w_gate_up, m_w_down, m_ln2_g, m_ln2_b, v_w_in, v_w_conv, v_w_pool, v_pool_scale, v_sgu_ln_g, v_w_spatial, v_b_spatial, v_w_o, v_ln1_g, v_ln1_b, v_w_gate_up, v_w_down, v_ln2_g, v_ln2_b):
    L = DEPTH
    T = x.shape[1]
    mx, my, mc = _my_place()
    dev = 4 * mx + 2 * my + mc
    xs = x[0]
    target = loss_target[0]

    shards = (jnp.swapaxes(w_in, 1, 2).astype(BF16), jnp.swapaxes(w_gate_up, 1, 2).astype(BF16),
              w_o.astype(BF16), w_down.astype(BF16))
    first_gather = _ag_start_layer(shards, 0, [])

    conv_cols = w_conv.shape[2]
    w_conv_z = lax.dynamic_update_slice(jnp.zeros((L, 3, CONV_W), F32), w_conv, (0, 0, dev * conv_cols))
    w_conv_full = _allreduce_small(_pack([w_conv_z]))
    w_conv_full = _unpack(w_conv_full, [(L, 3, CONV_W)])[0]

    loss_tile, grad_x2, big_grads, small_grads = _local_step(
        xs, target, shards, first_gather, w_conv_full, w_pool, pool_scale, sgu_ln_g, w_spatial, b_spatial,
        ln1_g, ln1_b, ln2_g, ln2_b)
    loss = lax.psum(loss_tile[0, 0], ("x", "y", "c"))
    grad_x = grad_x2[None]
    big_w = (w_in, w_gate_up, w_o, w_down)
    big_m = (m_w_in, m_w_gate_up, m_w_o, m_w_down)
    big_v = (v_w_in, v_w_gate_up, v_w_o, v_w_down)
    small_w = [w_conv_full, w_pool, pool_scale, sgu_ln_g, w_spatial, b_spatial, ln1_g, ln1_b, ln2_g, ln2_b]
    small_m = [m_w_conv, m_w_pool, m_pool_scale, m_sgu_ln_g, m_w_spatial, m_b_spatial, m_ln1_g, m_ln1_b, m_ln2_g, m_ln2_b]
    small_v = [v_w_conv, v_w_pool, v_pool_scale, v_sgu_ln_g, v_w_spatial, v_b_spatial, v_ln1_g, v_ln1_b, v_ln2_g, v_ln2_b]
    grads, deltas, new_m, new_v = _reduce_and_update(
        big_grads, small_grads, big_w, big_m, big_v, small_w, small_m, small_v)
    return (loss, grad_x, *grads, *deltas, *new_m, *new_v)


def _ag_start_layer(shards, l, after):
    s_in, s_gu, s_o, s_dn = [s[l] for s in shards]
    first = _ag_start([s_in, s_o], "%da" % l, after=after)
    return first, _ag_start([s_gu, s_dn], "%db" % l, after=[first[4]])


def _ag_finish(gather, after, tag):
    send_sems, recv_sems, shards, lands, _ = gather
    shards, lands = _ag_wait(send_sems, recv_sems, shards, lands, after, tag)
    return _ag_pass_on(shards, lands)


def _rs_begin(parts, c_arr, tag):
    parts = [p.reshape(4, 2, p.shape[0] // N_DEV, D_MODEL) for p in parts]
    return _rs_chip_start(_rs_chip_sum(parts, _rs_sibling_exchange(parts), c_arr), tag)


def _local_step(xs, target, shards, gather, w_conv_full, w_pool, pool_scale, sgu_ln_g, w_spatial, b_spatial,
                ln1_g, ln1_b, ln2_g, ln2_b):
    L = DEPTH
    T = xs.shape[0]
    mx, my, mc = _my_place()
    c_arr = jnp.reshape(mc, (1,)).astype(jnp.int32)
    q_arr = jnp.reshape(2 * mx + my, (1,)).astype(jnp.int32)
    eye2 = jnp.eye(2, dtype=F32)
    wp = w_pool.reshape(L, 2, 2, HALF, HALF)
    wpool_bd = jnp.einsum("ltgcd,gh->ltgchd", wp, eye2).reshape(L, 2, LANES, LANES)
    wsp_t = w_spatial.reshape(L, 3, 2 * CHUNK, CHUNK)
    bias_t = jnp.repeat(jnp.swapaxes(b_spatial.reshape(L, 3, 2, CHUNK), 2, 3), HALF, axis=3)
    ones = jnp.ones((1, D_MODEL), F32)
    zeros = jnp.zeros((1, D_MODEL), F32)

    saved = []
    prev, pg, pb = xs, ones, zeros
    prev_b = xs.astype(BF16)
    weights = []
    for l in range(L):
        g_in, g_o = _ag_finish(gather[0], [] if l == 0 else [prev_b], "%da" % l)
        proj = _mm(prev_b, g_in, "nt", F32, 512, IN_W, D_MODEL, "mm_proj")
        mixcat = _mixer_fwd(proj, w_conv_full[l], wpool_bd[l], pool_scale[l][None], sgu_ln_g[l][None], wsp_t[l], bias_t[l])
        xhat1, rstd1, h_b = _mm_ln_fwd(mixcat, g_o, prev, pg, pb, ln1_g[l][None], ln1_b[l][None], "mm_wo_ln")
        g_gu, g_dn = _ag_finish(gather[1], [h_b], "%db" % l)
        weights.append((g_in, g_gu, g_o, g_dn))
        deps = []
        if l + 1 < L:
            gather = _ag_start_layer(shards, l + 1, [g_gu])
            deps = [gather[1][4]]
        g_act, u_act, act = _mm_swiglu_fwd(h_b, g_gu, deps=deps)
        xhat2, rstd2, y_b = _mm_ln_fwd(act, g_dn, xhat1, ln1_g[l][None], ln1_b[l][None], ln2_g[l][None], ln2_b[l][None],
                                       "mm_down_ln")
        saved.append((prev_b, proj, mixcat, xhat1, rstd1, h_b, g_act, u_act, xhat2, rstd2))
        prev, pg, pb, prev_b = xhat2, ln2_g[l][None], ln2_b[l][None], y_b

    loss_tile, dy = _loss_head(prev, pg, pb, target)

    small = [None] * L
    big = None
    in_flight = None
    above = None
    for l in reversed(range(L)):
        prev_b, proj, mixcat, xhat1, rstd1, h_b, g_act, u_act, xhat2, rstd2 = saved[l]
        g_in, g_gu, g_o, g_dn = weights[l]
        if above is None:
            dr2, dr2_b, dg2, db2 = _ln_bwd(None, dy, xhat2, rstd2, ln2_g[l][None])
        else:
            dr2, dr2_b, dg2, db2 = _mm_ln_bwd([above[0]], above[1], above[2], xhat2, rstd2, ln2_g[l][None],
                                              "mm_dx_ln", deps=[in_flight[4]])
        dg_b, du_b, act = _mm_swiglu_bwd(dr2_b, g_dn, g_act, u_act)
        p_dn = _mm(act, dr2_b, "tn", BF16, DW_TM, D_MODEL, T, "mm_dw_down")
        p_gu = _mm(dg_b, h_b, "tn", BF16, DW_TM, D_MODEL, T, "mm_dw_gate", out_rows=2 * D_FF)
        p_gu = _mm(du_b, h_b, "tn", BF16, DW_TM, D_MODEL, T, "mm_dw_up", out_rows=2 * D_FF, out_off=D_FF, out_into=p_gu)
        ffn_flight = _rs_begin([p_gu, p_dn], c_arr, "0b") if l == 0 else None
        dr1, dr1_b, dg1, db1 = _mm_ln_bwd([dg_b, du_b], g_gu, dr2, xhat1, rstd1, ln1_g[l][None], "mm_dh_ln",
                                          deps=[ffn_flight[4]] if l == 0 else [])
        dmix = _mm(dr1_b, g_o, "nt", F32, T, 512, D_MODEL, "mm_dmix")
        p_o = _mm(mixcat, dr1_b, "tn", BF16, 512, D_MODEL, T, "mm_dw_o")
        dproj, dwc, dwp, dps, dlng, dwsp, dbias = _mixer_bwd(
            proj, dmix, w_conv_full[l], wpool_bd[l], pool_scale[l][None], sgu_ln_g[l][None], wsp_t[l], bias_t[l])
        p_in = _mm(dproj, prev_b, "tn", BF16, IN_W, D_MODEL, min(T, 512), "mm_dw_in")
        small[l] = (dwc, dwp, dps, dlng, dwsp, dbias, dg1, db1, dg2, db2)
        above = (dproj, g_in, dr1)
        if in_flight is not None:
            big = list(_rs_chip_finish(in_flight, [p_in], q_arr, str(l + 1), l + 1, big))
        if l > 0:
            in_flight = _rs_begin([p_in, p_gu, p_o, p_dn], c_arr, str(l))
        else:
            big[1], big[3] = _rs_chip_finish(ffn_flight, [p_in], q_arr, "0b", 0, [big[1], big[3]])
            in_flight = _rs_begin([p_in, p_o], c_arr, "0a")
    grad_x = _mm_ln_bwd([above[0]], above[1], above[2], None, None, None, "mm_dx_out", deps=[in_flight[4]])
    big[0], big[2] = _rs_chip_finish(in_flight, [grad_x], q_arr, "0a", 0, [big[0], big[2]])
    big_grads = big

    def stack(i):
        return jnp.stack([small[l][i] for l in range(L)])

    dwp_bd = stack(1).reshape(L, 2, 2, HALF, 2, HALF)
    dwp_all = jnp.einsum("ltgchd,gh->ltgcd", dwp_bd, eye2).reshape(L, 4, HALF, HALF)
    dbs_all = jnp.swapaxes(stack(5)[:, :, :, :2], 2, 3).reshape(L, 6, CHUNK)
    small_grads = [stack(0), dwp_all, stack(2).reshape(L, POOL_W), stack(3).reshape(L, SGU_W),
                   stack(4).reshape(L, 6, CHUNK, CHUNK), dbs_all] + [stack(i).reshape(L, D_MODEL) for i in (6, 7, 8, 9)]
    return loss_tile, grad_x, big_grads, small_grads


def _rs_chip_finish(in_flight, after, q, tag, layer, into):
    send_sems, recv_sems, sums, lands, _ = in_flight
    sums, got = _rs_chip_wait(send_sems, recv_sems, sums, lands, after, tag)
    return _rs_finish(sums, got, q, layer, into)


def _reduce_and_update(big_grads, small_grads, big_w, big_m, big_v, small_w, small_m, small_v):
    L = DEPTH
    mx, my, mc = _my_place()
    dev = 4 * mx + 2 * my + mc
    conv_cols = CONV_W // N_DEV
    w_in, w_gate_up, w_o, w_down = big_w
    m_w_in, m_w_gate_up, m_w_o, m_w_down = big_m
    v_w_in, v_w_gate_up, v_w_o, v_w_down = big_v
    gt_in, gt_gu, g_w_o, g_w_dn = big_grads
    g_w_in = jnp.swapaxes(gt_in, 1, 2)
    g_w_gu = jnp.swapaxes(gt_gu, 1, 2)

    small_shapes = [a.shape for a in small_grads]
    packed_g = _allreduce_small(_pack(small_grads))

    def widen_conv(a):
        return lax.dynamic_update_slice(jnp.zeros((L, 3, CONV_W), F32), a, (0, 0, dev * conv_cols))

    small_m = [widen_conv(small_m[0])] + list(small_m[1:])
    small_v = [widen_conv(small_v[0])] + list(small_v[1:])
    pk_d, pk_m, pk_v = _adamw(_pack(small_w), packed_g, _pack(small_m), _pack(small_v), packed_g.shape[0] // 2)
    sg = _unpack(packed_g, small_shapes)
    sd = _unpack(pk_d, small_shapes)
    sm = _unpack(pk_m, small_shapes)
    sv = _unpack(pk_v, small_shapes)

    def conv_cols_of(a):
        return lax.dynamic_slice(a, (0, 0, dev * conv_cols), (L, 3, conv_cols))

    for lst in (sg, sd, sm, sv):
        lst[0] = conv_cols_of(lst[0])

    tr = lambda a: jnp.swapaxes(a, 1, 2)
    d_in, m_in, v_in = [tr(a) for a in _adamw(tr(w_in), gt_in, tr(m_w_in), tr(v_w_in), gt_in.shape[1])]
    d_gu, m_gu, v_gu = [tr(a) for a in _adamw(tr(w_gate_up), gt_gu, tr(m_w_gate_up), tr(v_w_gate_up), gt_gu.shape[1] // 2)]
    d_o, m_o, v_o = _adamw(w_o, g_w_o, m_w_o, v_w_o, 128)
    d_dn, m_dn, v_dn = _adamw(w_down, g_w_dn, m_w_down, v_w_down, 352)

    def ordered(big_in, big_o, big_gu, big_dn, sm_list):
        return [big_in, sm_list[0], sm_list[1], sm_list[2], sm_list[3], sm_list[4], sm_list[5], big_o,
                sm_list[6], sm_list[7], big_gu, big_dn, sm_list[8], sm_list[9]]

    grads = ordered(g_w_in, g_w_o, g_w_gu, g_w_dn, sg)
    deltas = ordered(d_in, d_o, d_gu, d_dn, sd)
    new_m = ordered(m_in, m_o, m_gu, m_dn, sm)
    new_v = ordered(v_in, v_o, v_gu, v_dn, sv)
    return grads, deltas, new_m, new_v
```

```python
import functools
import math

import jax
import jax.numpy as jnp
from jax import lax
from jax.experimental import pallas as pl
from jax.experimental.pallas import tpu as pltpu

F32 = jnp.float32
BF16 = jnp.bfloat16
MESH = pl.DeviceIdType.MESH

D_MODEL = 1024
DEPTH = 4
CONV_W = 384
POOL_W = 256
SGU_W = 384
IN_W = 3 * CONV_W + POOL_W + 2 * SGU_W
D_FF = 2816
CHUNK = 128
ALPHA = float((2 * DEPTH) ** 0.25)
LN_EPS = 1e-5
ADAM_LR, ADAM_B1, ADAM_B2, ADAM_EPS, ADAM_WD, ADAM_STEP = 0.001, 0.9, 0.999, 1e-08, 0.01, 10

N_DEV = 8
LANES = 128
HALF = 64
SHARD_ROWS = (IN_W // N_DEV, 2 * D_FF // N_DEV, D_MODEL // N_DEV, D_FF // N_DEV)
VMEM_LIMIT = 52 * 1024 * 1024

INV_SQRT2 = 0.7071067811865476
INV_SQRT_2PI = 0.3989422804014327


def _cparams(sem=None, **kw):
    if sem is not None:
        kw["dimension_semantics"] = sem
    return pltpu.CompilerParams(vmem_limit_bytes=VMEM_LIMIT, **kw)


_DN = {"nn": (((1,), (0,)), ((), ())), "nt": (((1,), (1,)), ((), ())), "tn": (((0,), (0,)), ((), ()))}


def _mm(a, b, mode, out_dtype, tm, tn, tk, name, deps=(), out_rows=None, out_off=0, out_into=None):
    if mode == "nn":
        (M, K), N = a.shape, b.shape[1]
    elif mode == "nt":
        (M, K), N = a.shape, b.shape[0]
    else:
        (K, M), N = a.shape, b.shape[1]
    assert M % tm == 0 and N % tn == 0 and K % tk == 0 and out_off % tm == 0, (M, N, K, tm, tn, tk)
    nk = K // tk
    if out_into is not None:
        deps = tuple(deps) + (out_into,)
    nd = len(deps)
    row_off = out_off // tm

    def body(*refs):
        a_ref, b_ref, o_ref = refs[0], refs[1], refs[2 + nd]
        acc_ref = refs[3 + nd] if nk > 1 else None
        p = lax.dot_general(a_ref[...], b_ref[...], _DN[mode], preferred_element_type=F32)
        if nk == 1:
            o_ref[...] = p.astype(o_ref.dtype)
        else:
            k = pl.program_id(2)

            @pl.when(k == 0)
            def _():
                acc_ref[...] = p

            @pl.when(k > 0)
            def _():
                acc_ref[...] += p

            @pl.when(k == nk - 1)
            def _():
                o_ref[...] = acc_ref[...].astype(o_ref.dtype)

    if mode == "nn":
        a_spec = pl.BlockSpec((tm, tk), lambda i, j, k: (i, k))
        b_blk, b_idx = (tk, tn), (lambda i, j, k: (k, j))
    elif mode == "nt":
        a_spec = pl.BlockSpec((tm, tk), lambda i, j, k: (i, k))
        b_blk, b_idx = (tn, tk), (lambda i, j, k: (j, k))
    else:
        a_spec = pl.BlockSpec((tk, tm), lambda i, j, k: (k, i))
        b_blk, b_idx = (tk, tn), (lambda i, j, k: (k, j))
    return pl.pallas_call(
        body,
        name=name,
        grid=(M // tm, N // tn, nk),
        in_specs=[a_spec, pl.BlockSpec(b_blk, b_idx)] + [pl.BlockSpec(memory_space=pl.ANY)] * nd,
        out_specs=pl.BlockSpec((tm, tn), lambda i, j, k: (i + row_off, j)),
        out_shape=jax.ShapeDtypeStruct((out_rows or M, N), out_dtype),
        scratch_shapes=[pltpu.VMEM((tm, tn), F32)] if nk > 1 else [],
        input_output_aliases={1 + nd: 0} if out_into is not None else {},
        compiler_params=_cparams(("parallel", "parallel", "arbitrary")),
    )(a, b, *deps)


LN_TM = 512


def _mm_ln_fwd(a, b, prev, pg, pb, g, bias, name):
    T, K = a.shape
    tm = LN_TM

    def body(a_ref, b_ref, prev_ref, pg_ref, pb_ref, g_ref, bias_ref, xhat_ref, rstd_ref, y_ref):
        mm = jnp.dot(a_ref[...], b_ref[...], preferred_element_type=F32)
        r = ALPHA * (prev_ref[...] * pg_ref[...] + pb_ref[...]) + mm
        mu = jnp.mean(r, axis=-1, keepdims=True)
        xc = r - mu
        var = jnp.mean(xc * xc, axis=-1, keepdims=True)
        rstd = lax.rsqrt(var + LN_EPS)
        xhat = xc * rstd
        xhat_ref[...] = xhat
        rstd_ref[...] = rstd
        y_ref[...] = (xhat * g_ref[...] + bias_ref[...]).astype(y_ref.dtype)

    row = pl.BlockSpec((tm, D_MODEL), lambda i: (i, 0))
    vec = pl.BlockSpec((1, D_MODEL), lambda i: (0, 0))
    return pl.pallas_call(
        body, name=name, grid=(T // tm,),
        in_specs=[pl.BlockSpec((tm, K), lambda i: (i, 0)),
                  pl.BlockSpec((K, D_MODEL), lambda i: (0, 0), pipeline_mode=pl.Buffered(1)),
                  row, vec, vec, vec, vec],
        out_specs=[row, pl.BlockSpec((tm, 1), lambda i: (i, 0)), row],
        out_shape=[jax.ShapeDtypeStruct((T, D_MODEL), F32), jax.ShapeDtypeStruct((T, 1), F32),
                   jax.ShapeDtypeStruct((T, D_MODEL), BF16)],
        compiler_params=_cparams(("parallel",)),
    )(a, b, prev, pg, pb, g, bias)


def _mm_ln_bwd(a_list, b, dres, xhat, rstd, g, name, deps=()):
    T = a_list[0].shape[0]
    tm = LN_TM
    na, nd = len(a_list), len(deps)
    ks = [a.shape[1] for a in a_list]
    last = xhat is None

    def body(*refs):
        a_refs, b_ref, dres_ref = refs[:na], refs[na], refs[na + 1]
        mm, off = None, 0
        for a_ref, k in zip(a_refs, ks):
            part = jnp.dot(a_ref[...], b_ref[off:off + k, :], preferred_element_type=F32)
            mm = part if mm is None else mm + part
            off += k
        dy = ALPHA * dres_ref[...] + mm
        if last:
            refs[-1][...] = dy
            return
        xhat_ref, rstd_ref, g_ref = refs[na + 2:na + 5]
        dr_ref, drb_ref, dg_ref, db_ref = refs[-4:]
        xhat_v = xhat_ref[...]

        @pl.when(pl.program_id(0) == 0)
        def _():
            dg_ref[...] = jnp.zeros_like(dg_ref)
            db_ref[...] = jnp.zeros_like(db_ref)

        dg_ref[...] += jnp.sum(dy * xhat_v, axis=0, keepdims=True)
        db_ref[...] += jnp.sum(dy, axis=0, keepdims=True)
        dxh = dy * g_ref[...]
        m1 = jnp.mean(dxh, axis=-1, keepdims=True)
        m2 = jnp.mean(dxh * xhat_v, axis=-1, keepdims=True)
        dr = rstd_ref[...] * (dxh - m1 - xhat_v * m2)
        dr_ref[...] = dr
        drb_ref[...] = dr.astype(drb_ref.dtype)

    row = pl.BlockSpec((tm, D_MODEL), lambda i: (i, 0))
    vec = pl.BlockSpec((1, D_MODEL), lambda i: (0, 0))
    in_specs = [pl.BlockSpec((tm, k), lambda i: (i, 0)) for k in ks]
    in_specs += [pl.BlockSpec((sum(ks), D_MODEL), lambda i: (0, 0), pipeline_mode=pl.Buffered(1)), row]
    args = list(a_list) + [b, dres]
    if last:
        out_specs, out_shape = row, jax.ShapeDtypeStruct((T, D_MODEL), F32)
    else:
        in_specs += [row, pl.BlockSpec((tm, 1), lambda i: (i, 0)), vec]
        args += [xhat, rstd, g]
        out_specs = [row, row, vec, vec]
        out_shape = [jax.ShapeDtypeStruct((T, D_MODEL), F32), jax.ShapeDtypeStruct((T, D_MODEL), BF16),
                     jax.ShapeDtypeStruct((1, D_MODEL), F32), jax.ShapeDtypeStruct((1, D_MODEL), F32)]
    return pl.pallas_call(
        body, name=name, grid=(T // tm,),
        in_specs=in_specs + [pl.BlockSpec(memory_space=pl.ANY)] * nd,
        out_specs=out_specs, out_shape=out_shape,
        compiler_params=_cparams(("parallel",) if last else ("arbitrary",)),
    )(*args, *deps)


DW_TM = 1408
FF_TN = 256
SAVED_GU = BF16


def _mm_swiglu_fwd(h, w_gu, deps=()):
    T = h.shape[0]
    nj = D_FF // FF_TN
    nd = len(deps)

    def body(*refs):
        h_ref, wg_ref, wu_ref = refs[:3]
        g_ref, u_ref, act_ref = refs[3 + nd:]
        hv = h_ref[...]
        gv = lax.dot_general(hv, wg_ref[...], _DN["nt"], preferred_element_type=F32)
        uv = lax.dot_general(hv, wu_ref[...], _DN["nt"], preferred_element_type=F32)
        g_ref[...] = gv.astype(g_ref.dtype)
        u_ref[...] = uv.astype(u_ref.dtype)
        act_ref[...] = (gv * jax.nn.sigmoid(gv) * uv).astype(act_ref.dtype)

    col = pl.BlockSpec((T, FF_TN), lambda j: (0, j))
    return pl.pallas_call(
        body, name="mm_gate_up_swiglu", grid=(nj,),
        in_specs=[pl.BlockSpec((T, D_MODEL), lambda j: (0, 0)),
                  pl.BlockSpec((FF_TN, D_MODEL), lambda j: (j, 0)),
                  pl.BlockSpec((FF_TN, D_MODEL), lambda j: (j + nj, 0))] + [pl.BlockSpec(memory_space=pl.ANY)] * nd,
        out_specs=[col, col, col],
        out_shape=[jax.ShapeDtypeStruct((T, D_FF), SAVED_GU), jax.ShapeDtypeStruct((T, D_FF), SAVED_GU),
                   jax.ShapeDtypeStruct((T, D_FF), BF16)],
        compiler_params=_cparams(("parallel",)),
    )(h, w_gu, w_gu, *deps)


def _mm_swiglu_bwd(dr, w_dn, g, u, deps=()):
    T = dr.shape[0]

    def body(*refs):
        dr_ref, w_ref, g_ref, u_ref = refs[:4]
        dg_ref, du_ref = refs[-2:]
        da = lax.dot_general(dr_ref[...], w_ref[...], _DN["nt"], preferred_element_type=F32)
        gv, uv = g_ref[...].astype(F32), u_ref[...].astype(F32)
        s = jax.nn.sigmoid(gv)
        du_ref[...] = (da * (gv * s)).astype(du_ref.dtype)
        dg_ref[...] = (da * uv * (s * (1.0 + gv * (1.0 - s)))).astype(dg_ref.dtype)

    col = pl.BlockSpec((T, FF_TN), lambda j: (0, j))
    return pl.pallas_call(
        body, name="mm_dact_swiglu", grid=(D_FF // FF_TN,),
        in_specs=[pl.BlockSpec((T, D_MODEL), lambda j: (0, 0)), pl.BlockSpec((FF_TN, D_MODEL), lambda j: (j, 0)),
                  col, col] + [ANY] * len(deps),
        out_specs=[col, col],
        out_shape=[jax.ShapeDtypeStruct((T, D_FF), BF16)] * 2,
        compiler_params=_cparams(("parallel",)),
    )(dr, w_dn, g, u, *deps)


def _gelu(x):
    return 0.5 * x * (1.0 + lax.erf(x * INV_SQRT2))


def _gelu_grad(x):
    return 0.5 * (1.0 + lax.erf(x * INV_SQRT2)) + x * (jnp.exp(-0.5 * x * x) * INV_SQRT_2PI)


def _shift_down(z, k):
    row = lax.broadcasted_iota(jnp.int32, z.shape, 0)
    return jnp.where(row >= k, pltpu.roll(z, k, 0), 0.0)


def _shift_up(z, k):
    n = z.shape[0]
    row = lax.broadcasted_iota(jnp.int32, z.shape, 0)
    return jnp.where(row < n - k, pltpu.roll(z, n - k, 0), 0.0)


def _lo_mask(shape):
    return lax.broadcasted_iota(jnp.int32, shape, len(shape) - 1) < HALF


def _seg_mean(x, lo):
    a = jnp.sum(jnp.where(lo, x, 0.0), axis=-1, keepdims=True)
    b = jnp.sum(jnp.where(lo, 0.0, x), axis=-1, keepdims=True)
    return jnp.where(lo, a, b) * (1.0 / HALF)


def _pool_windows(first):
    lo = _lo_mask((1, LANES))
    return jnp.where(first, jnp.where(lo, 2.0, 4.0), jnp.where(lo, 8.0, 16.0)), lo


def _pool_mean_minus_token(p, first):
    wl, lo = _pool_windows(first)
    s2 = p + _shift_down(p, 1)
    s4 = s2 + _shift_down(s2, 2)
    s8 = s4 + _shift_down(s4, 4)
    s16 = s8 + _shift_down(s8, 8)
    win = jnp.where(first, jnp.where(lo, s2, s4), jnp.where(lo, s8, s16))
    t1 = (lax.broadcasted_iota(jnp.int32, p.shape, 0) + 1).astype(F32)
    count = jnp.minimum(t1, wl)
    return win / count - p, count


SGU_UNROLL = 4


def _tril_keep():
    r = lax.broadcasted_iota(jnp.int32, (2 * CHUNK, CHUNK), 0)
    s = lax.broadcasted_iota(jnp.int32, (2 * CHUNK, CHUNK), 1)
    return s <= (r & (CHUNK - 1))


def _sgu_chunk_fwd(u, v, g, wm, bias, lo):
    ug = _gelu(u)
    vg = _gelu(v)
    mu = _seg_mean(vg, lo)
    xc = vg - mu
    var = _seg_mean(xc * xc, lo)
    rstd = lax.rsqrt(var + LN_EPS)
    vn = xc * rstd
    vh = (vn * g).astype(BF16)
    mm2 = jnp.dot(wm, vh, preferred_element_type=F32)
    mixed = jnp.where(lo, mm2[:CHUNK], mm2[CHUNK:]) + bias
    return ug, vn, rstd, vh, mixed


def _mixer_fwd(proj, wconv, wpool_bd, pscale, lng, wsp, bias):
    T = proj.shape[0]
    nchunk = T // CHUNK

    def body(a_ref, b_ref, c_ref, wc_ref, wp_ref, ps_ref, lng_ref, wsp_ref, bias_ref, o_ref):
        j = pl.program_id(0)

        @pl.when(j < 3)
        def _conv():
            z = c_ref[...] * a_ref[...]
            w = wc_ref[...]
            y = w[0:1] * _shift_down(z, 2) + w[1:2] * _shift_down(z, 1) + w[2:3] * z
            o_ref[...] = (b_ref[...] * y).astype(o_ref.dtype)

        @pl.when((j >= 3) & (j < 5))
        def _pool():
            d, _ = _pool_mean_minus_token(a_ref[...], j == 3)
            y = jnp.dot(d.astype(BF16), wp_ref[...].astype(BF16), preferred_element_type=F32)
            o_ref[...] = (y * ps_ref[...]).astype(o_ref.dtype)

        @pl.when(j >= 5)
        def _sgu():
            lo = _lo_mask((CHUNK, LANES))
            wm = jnp.where(_tril_keep(), wsp_ref[...], 0.0).astype(BF16)
            bias_t = bias_ref[...]
            g = lng_ref[...]

            def chunk(n, carry):
                rows = pl.ds(pl.multiple_of(n * CHUNK, CHUNK), CHUNK)
                ug, _, _, _, mixed = _sgu_chunk_fwd(a_ref[rows, :], b_ref[rows, :], g, wm, bias_t, lo)
                o_ref[rows, :] = (ug * mixed).astype(o_ref.dtype)
                return carry

            lax.fori_loop(0, nchunk, chunk, 0, unroll=SGU_UNROLL)

    def col(f):
        return lambda j: (0, f(j))

    clip = lambda v, lo, hi: jnp.minimum(jnp.maximum(v, lo), hi)
    return pl.pallas_call(
        body,
        name="mixer_fwd",
        grid=(8,),
        in_specs=[
            pl.BlockSpec((T, LANES), col(lambda j: jnp.where(j < 3, j, jnp.where(j < 5, j + 6, j + 6)))),
            pl.BlockSpec((T, LANES), col(lambda j: jnp.where(j < 3, j + 3, jnp.where(j < 5, 5, j + 9)))),
            pl.BlockSpec((T, LANES), col(lambda j: jnp.where(j < 3, j + 6, 8))),
            pl.BlockSpec((3, LANES), col(lambda j: clip(j, 0, 2))),
            pl.BlockSpec((None, LANES, LANES), lambda j: (clip(j - 3, 0, 1), 0, 0)),
            pl.BlockSpec((1, LANES), col(lambda j: clip(j - 3, 0, 1))),
            pl.BlockSpec((1, LANES), col(lambda j: clip(j - 5, 0, 2))),
            pl.BlockSpec((None, 2 * CHUNK, CHUNK), lambda j: (clip(j - 5, 0, 2), 0, 0)),
            pl.BlockSpec((None, CHUNK, LANES), lambda j: (clip(j - 5, 0, 2), 0, 0)),
        ],
        out_specs=pl.BlockSpec((T, LANES), lambda j: (0, j)),
        out_shape=jax.ShapeDtypeStruct((T, D_MODEL), BF16),
        compiler_params=_cparams(("arbitrary",)),
    )(proj, proj, proj, wconv, wpool_bd, pscale, lng, wsp, bias)


def _mixer_bwd(proj, dmix, wconv, wpool_bd, pscale, lng, wsp, bias):
    T = proj.shape[0]
    nchunk = T // CHUNK

    def body(a_ref, b_ref, c_ref, dm_ref, wc_ref, wp_ref, ps_ref, lng_ref, wsp_ref, bias_ref,
             o_ref, dwc_ref, dwp_ref, dps_ref, dlng_ref, dwsp_ref, dbias_ref, keep1, keep2):
        k = pl.program_id(0)

        @pl.when(k < 3)
        def _conv():
            xa, gb, gc, dya = a_ref[...], b_ref[...], c_ref[...], dm_ref[...]
            w = wc_ref[...]
            z = gc * xa
            z1 = _shift_down(z, 1)
            z2 = _shift_down(z, 2)
            y = w[0:1] * z2 + w[1:2] * z1 + w[2:3] * z
            dyv = dya * gb
            dz = w[2:3] * dyv + w[1:2] * _shift_up(dyv, 1) + w[0:1] * _shift_up(dyv, 2)
            dwc_ref[0:1, :] = jnp.sum(dyv * z2, axis=0, keepdims=True)
            dwc_ref[1:2, :] = jnp.sum(dyv * z1, axis=0, keepdims=True)
            dwc_ref[2:3, :] = jnp.sum(dyv * z, axis=0, keepdims=True)
            o_ref[...] = (dz * gc).astype(o_ref.dtype)
            keep1[k] = (dya * y).astype(keep1.dtype)
            keep1[k + 3] = (dz * xa).astype(keep1.dtype)

        @pl.when((k >= 3) & (k < 9))
        def _emit_gb_gc():
            o_ref[...] = keep1[k - 3]

        @pl.when((k >= 9) & (k < 11))
        def _pool():
            first = k == 9
            p, dyb = a_ref[...], dm_ref[...]
            d, count = _pool_mean_minus_token(p, first)
            w2 = wp_ref[...].astype(BF16)
            db = d.astype(BF16)
            y = jnp.dot(db, w2, preferred_element_type=F32)
            dps_ref[...] = jnp.sum(dyb * y, axis=0, keepdims=True)
            dyv = (dyb * ps_ref[...]).astype(BF16)
            dd = lax.dot_general(dyv, w2, _DN["nt"], preferred_element_type=F32)
            dwp_ref[...] = lax.dot_general(db, dyv, _DN["tn"], preferred_element_type=F32)
            dwin = dd / count
            a2 = dwin + _shift_up(dwin, 1)
            a4 = a2 + _shift_up(a2, 2)
            a8 = a4 + _shift_up(a4, 4)
            a16 = a8 + _shift_up(a8, 8)
            _, lo = _pool_windows(first)
            back = jnp.where(first, jnp.where(lo, a2, a4), jnp.where(lo, a8, a16))
            o_ref[...] = (back - dd).astype(o_ref.dtype)

        @pl.when((k >= 11) & (k < 14))
        def _sgu():
            lo = _lo_mask((CHUNK, LANES))
            keep = _tril_keep()
            wm = jnp.where(keep, wsp_ref[...], 0.0).astype(BF16)
            bias_t = bias_ref[...]
            g = lng_ref[...]
            dwsp_ref[...] = jnp.zeros_like(dwsp_ref)
            dbias_ref[...] = jnp.zeros_like(dbias_ref)
            dlng_ref[...] = jnp.zeros_like(dlng_ref)

            def chunk(n, carry):
                rows = pl.ds(pl.multiple_of(n * CHUNK, CHUNK), CHUNK)
                u, v, dyc = a_ref[rows, :], b_ref[rows, :], dm_ref[rows, :]
                ug, vn, rstd, vh, mixed = _sgu_chunk_fwd(u, v, g, wm, bias_t, lo)
                dmx = dyc * ug
                o_ref[rows, :] = (dyc * mixed * _gelu_grad(u)).astype(o_ref.dtype)
                dbias_ref[...] += dmx
                dst = jnp.concatenate([jnp.where(lo, dmx, 0.0), jnp.where(lo, 0.0, dmx)], axis=0).astype(BF16)
                dwsp_ref[...] += lax.dot_general(dst, vh, _DN["nt"], preferred_element_type=F32)
                dvh = lax.dot_general(wm, dst, _DN["tn"], preferred_element_type=F32)
                dlng_ref[...] += jnp.sum(dvh * vn, axis=0, keepdims=True)
                dvn = dvh * g
                m1 = _seg_mean(dvn, lo)
                m2 = _seg_mean(dvn * vn, lo)
                dvg = rstd * (dvn - m1 - vn * m2)
                keep2[k - 11, rows, :] = (dvg * _gelu_grad(v)).astype(keep2.dtype)
                return carry

            lax.fori_loop(0, nchunk, chunk, 0, unroll=SGU_UNROLL)
            dwsp_ref[...] = jnp.where(keep, dwsp_ref[...], 0.0)
            dbt = dbias_ref[...]
            lane = lax.broadcasted_iota(jnp.int32, (CHUNK, LANES), 1)
            sa = jnp.sum(jnp.where(lo, dbt, 0.0), axis=-1, keepdims=True)
            sb = jnp.sum(jnp.where(lo, 0.0, dbt), axis=-1, keepdims=True)
            dbias_ref[...] = jnp.where(lane == 0, sa, jnp.where(lane == 1, sb, 0.0))

        @pl.when(k >= 14)
        def _emit_v():
            o_ref[...] = keep2[k - 14]

    def col(f):
        return lambda k: (0, f(k))

    clip = lambda v, lo, hi: jnp.minimum(jnp.maximum(v, lo), hi)
    view_a = lambda k: jnp.where(k < 3, k, jnp.where(k < 9, 2, jnp.where(k < 14, k, 13)))
    view_b = lambda k: jnp.where(k < 3, k + 3, jnp.where(k < 11, 5, jnp.where(k < 14, k + 3, 16)))
    view_c = lambda k: jnp.where(k < 3, k + 6, 8)
    view_dm = lambda k: jnp.where(k < 3, k, jnp.where(k < 9, 2, jnp.where(k < 14, k - 6, 7)))
    return pl.pallas_call(
        body,
        name="mixer_bwd",
        grid=(17,),
        in_specs=[
            pl.BlockSpec((T, LANES), col(view_a)),
            pl.BlockSpec((T, LANES), col(view_b)),
            pl.BlockSpec((T, LANES), col(view_c)),
            pl.BlockSpec((T, LANES), col(view_dm)),
            pl.BlockSpec((3, LANES), col(lambda k: clip(k, 0, 2))),
            pl.BlockSpec((None, LANES, LANES), lambda k: (clip(k - 9, 0, 1), 0, 0)),
            pl.BlockSpec((1, LANES), col(lambda k: clip(k - 9, 0, 1))),
            pl.BlockSpec((1, LANES), col(lambda k: clip(k - 11, 0, 2))),
            pl.BlockSpec((None, 2 * CHUNK, CHUNK), lambda k: (clip(k - 11, 0, 2), 0, 0)),
            pl.BlockSpec((None, CHUNK, LANES), lambda k: (clip(k - 11, 0, 2), 0, 0)),
        ],
        out_specs=[
            pl.BlockSpec((T, LANES), lambda k: (0, k)),
            pl.BlockSpec((3, LANES), col(lambda k: clip(k, 0, 2))),
            pl.BlockSpec((None, LANES, LANES), lambda k: (clip(k - 9, 0, 1), 0, 0)),
            pl.BlockSpec((1, LANES), col(lambda k: clip(k - 9, 0, 1))),
            pl.BlockSpec((1, LANES), col(lambda k: clip(k - 11, 0, 2))),
            pl.BlockSpec((None, 2 * CHUNK, CHUNK), lambda k: (clip(k - 11, 0, 2), 0, 0)),
            pl.BlockSpec((None, CHUNK, LANES), lambda k: (clip(k - 11, 0, 2), 0, 0)),
        ],
        out_shape=[
            jax.ShapeDtypeStruct((T, IN_W), BF16),
            jax.ShapeDtypeStruct((3, CONV_W), F32),
            jax.ShapeDtypeStruct((2, LANES, LANES), F32),
            jax.ShapeDtypeStruct((1, POOL_W), F32),
            jax.ShapeDtypeStruct((1, SGU_W), F32),
            jax.ShapeDtypeStruct((3, 2 * CHUNK, CHUNK), F32),
            jax.ShapeDtypeStruct((3, CHUNK, LANES), F32),
        ],
        scratch_shapes=[pltpu.VMEM((6, T, LANES), BF16), pltpu.VMEM((3, T, LANES), BF16)],
        compiler_params=_cparams(("arbitrary",)),
    )(proj, proj, proj, dmix, wconv, wpool_bd, pscale, lng, wsp, bias)


def _ln_fwd(prev, pg, pb, mmout, g, b, tm=256):
    T = prev.shape[0]

    def body(prev_ref, pg_ref, pb_ref, mm_ref, g_ref, b_ref, xhat_ref, rstd_ref, y_ref):
        r = ALPHA * (prev_ref[...] * pg_ref[...] + pb_ref[...]) + mm_ref[...]
        mu = jnp.mean(r, axis=-1, keepdims=True)
        xc = r - mu
        var = jnp.mean(xc * xc, axis=-1, keepdims=True)
        rstd = lax.rsqrt(var + LN_EPS)
        xhat = xc * rstd
        xhat_ref[...] = xhat
        rstd_ref[...] = rstd
        y_ref[...] = (xhat * g_ref[...] + b_ref[...]).astype(y_ref.dtype)

    row = pl.BlockSpec((tm, D_MODEL), lambda i: (i, 0))
    vec = pl.BlockSpec((1, D_MODEL), lambda i: (0, 0))
    return pl.pallas_call(
        body,
        name="ln_fwd",
        grid=(T // tm,),
        in_specs=[row, vec, vec, row, vec, vec],
        out_specs=[row, pl.BlockSpec((tm, 1), lambda i: (i, 0)), row],
        out_shape=[jax.ShapeDtypeStruct((T, D_MODEL), F32), jax.ShapeDtypeStruct((T, 1), F32),
                   jax.ShapeDtypeStruct((T, D_MODEL), BF16)],
        compiler_params=_cparams(("parallel",)),
    )(prev, pg, pb, mmout, g, b)


def _ln_bwd(dres, dmm, xhat, rstd, g, tm=256, deps=()):
    T = xhat.shape[0]
    has_res = dres is not None
    nd = len(deps)

    def body(*refs):
        refs = refs[:len(refs) - 4 - nd] + refs[len(refs) - 4:]
        if has_res:
            dres_ref, dmm_ref, xhat_ref, rstd_ref, g_ref, dr_ref, drb_ref, dg_ref, db_ref = refs
            dy = ALPHA * dres_ref[...] + dmm_ref[...]
        else:
            dmm_ref, xhat_ref, rstd_ref, g_ref, dr_ref, drb_ref, dg_ref, db_ref = refs
            dy = dmm_ref[...]
        xhat_v = xhat_ref[...]

        @pl.when(pl.program_id(0) == 0)
        def _():
            dg_ref[...] = jnp.zeros_like(dg_ref)
            db_ref[...] = jnp.zeros_like(db_ref)

        dg_ref[...] += jnp.sum(dy * xhat_v, axis=0, keepdims=True)
        db_ref[...] += jnp.sum(dy, axis=0, keepdims=True)
        dxh = dy * g_ref[...]
        m1 = jnp.mean(dxh, axis=-1, keepdims=True)
        m2 = jnp.mean(dxh * xhat_v, axis=-1, keepdims=True)
        dr = rstd_ref[...] * (dxh - m1 - xhat_v * m2)
        dr_ref[...] = dr
        drb_ref[...] = dr.astype(drb_ref.dtype)

    row = pl.BlockSpec((tm, D_MODEL), lambda i: (i, 0))
    vec = pl.BlockSpec((1, D_MODEL), lambda i: (0, 0))
    in_specs = ([row] if has_res else []) + [row, row, pl.BlockSpec((tm, 1), lambda i: (i, 0)), vec]
    in_specs += [pl.BlockSpec(memory_space=pl.ANY)] * nd
    args = ([dres] if has_res else []) + [dmm, xhat, rstd, g] + list(deps)
    return pl.pallas_call(
        body,
        name="ln_bwd_res" if has_res else "ln_bwd",
        grid=(T // tm,),
        in_specs=in_specs,
        out_specs=[row, row, vec, vec],
        out_shape=[jax.ShapeDtypeStruct((T, D_MODEL), F32), jax.ShapeDtypeStruct((T, D_MODEL), BF16),
                   jax.ShapeDtypeStruct((1, D_MODEL), F32), jax.ShapeDtypeStruct((1, D_MODEL), F32)],
        compiler_params=_cparams(("arbitrary",)),
    )(*args)


def _loss_head(xhat, g, b, target, tm=256):
    T = xhat.shape[0]

    def body(xhat_ref, g_ref, b_ref, t_ref, loss_ref, dy_ref):
        err = xhat_ref[...] * g_ref[...] + b_ref[...] - t_ref[...]

        @pl.when(pl.program_id(0) == 0)
        def _():
            loss_ref[...] = jnp.zeros_like(loss_ref)

        part = jnp.sum(jnp.sum(err * err, axis=-1, keepdims=True), axis=0, keepdims=True)
        loss_ref[...] += jnp.broadcast_to(part * (0.5 / D_MODEL), loss_ref.shape)
        dy_ref[...] = err * (1.0 / D_MODEL)

    row = pl.BlockSpec((tm, D_MODEL), lambda i: (i, 0))
    vec = pl.BlockSpec((1, D_MODEL), lambda i: (0, 0))
    return pl.pallas_call(
        body,
        name="loss_head",
        grid=(T // tm,),
        in_specs=[row, vec, vec, row],
        out_specs=[pl.BlockSpec((8, LANES), lambda i: (0, 0)), row],
        out_shape=[jax.ShapeDtypeStruct((8, LANES), F32), jax.ShapeDtypeStruct((T, D_MODEL), F32)],
        compiler_params=_cparams(("arbitrary",)),
    )(xhat, g, b, target)


def _residual_out(dres, dmm, tm=256):
    T = dres.shape[0]

    def body(a_ref, b_ref, o_ref):
        o_ref[...] = ALPHA * a_ref[...] + b_ref[...]

    row = pl.BlockSpec((tm, D_MODEL), lambda i: (i, 0))
    return pl.pallas_call(
        body, name="residual_out", grid=(T // tm,), in_specs=[row, row], out_specs=row,
        out_shape=jax.ShapeDtypeStruct((T, D_MODEL), F32), compiler_params=_cparams(("parallel",)),
    )(dres, dmm)


SW_TC = 1408


def _swiglu_fwd(gu, tm=128):
    T = gu.shape[0]

    def body(gu_ref, o_ref):
        gv = gu_ref[:, :D_FF]
        o_ref[...] = (gv * jax.nn.sigmoid(gv) * gu_ref[:, D_FF:]).astype(o_ref.dtype)

    return pl.pallas_call(
        body, name="swiglu_fwd", grid=(T // tm,),
        in_specs=[pl.BlockSpec((tm, 2 * D_FF), lambda i: (i, 0))],
        out_specs=pl.BlockSpec((tm, D_FF), lambda i: (i, 0)),
        out_shape=jax.ShapeDtypeStruct((T, D_FF), BF16), compiler_params=_cparams(("parallel",)),
    )(gu)


def _swiglu_bwd(gu, dact, tm=128):
    T = gu.shape[0]

    def body(gu_ref, da_ref, dgu_ref, act_ref):
        gv, uv, da = gu_ref[:, :D_FF], gu_ref[:, D_FF:], da_ref[...]
        s = jax.nn.sigmoid(gv)
        sg = gv * s
        act_ref[...] = (sg * uv).astype(act_ref.dtype)
        dgu_ref[:, D_FF:] = (da * sg).astype(dgu_ref.dtype)
        dgu_ref[:, :D_FF] = (da * uv * (s * (1.0 + gv * (1.0 - s)))).astype(dgu_ref.dtype)

    wide = pl.BlockSpec((tm, 2 * D_FF), lambda i: (i, 0))
    half = pl.BlockSpec((tm, D_FF), lambda i: (i, 0))
    return pl.pallas_call(
        body, name="swiglu_bwd", grid=(T // tm,),
        in_specs=[wide, half], out_specs=[wide, half],
        out_shape=[jax.ShapeDtypeStruct((T, 2 * D_FF), BF16), jax.ShapeDtypeStruct((T, D_FF), BF16)],
        compiler_params=_cparams(("parallel",)),
    )(gu, dact)


def _adamw(w, g, m, v, tr):
    R, C = w.shape[-2:]
    assert R % tr == 0
    c1 = 1.0 - ADAM_B1 ** ADAM_STEP
    c2 = 1.0 - ADAM_B2 ** ADAM_STEP

    def body(w_ref, g_ref, m_ref, v_ref, d_ref, mo_ref, vo_ref):
        gv = g_ref[...]
        mn = ADAM_B1 * m_ref[...] + (1.0 - ADAM_B1) * gv
        vn = ADAM_B2 * v_ref[...] + (1.0 - ADAM_B2) * (gv * gv)
        d_ref[...] = -ADAM_LR * ((mn / c1) / (jnp.sqrt(vn / c2) + ADAM_EPS) + ADAM_WD * w_ref[...])
        mo_ref[...] = mn
        vo_ref[...] = vn

    if w.ndim == 2:
        grid, blk = (R // tr,), pl.BlockSpec((tr, C), lambda i: (i, 0))
    else:
        grid, blk = (w.shape[0], R // tr), pl.BlockSpec((None, tr, C), lambda l, i: (l, i, 0))
    return pl.pallas_call(
        body, name="adamw", grid=grid, in_specs=[blk] * 4, out_specs=[blk] * 3,
        out_shape=[jax.ShapeDtypeStruct(w.shape, F32)] * 3, compiler_params=_cparams(("parallel",) * len(grid)),
    )(w, g, m, v)


def _my_place():
    return lax.axis_index("x"), lax.axis_index("y"), lax.axis_index("c")


ANY = pl.BlockSpec(memory_space=pl.ANY)
HBM = pl.BlockSpec(memory_space=pltpu.HBM)
SEM = pl.BlockSpec(memory_space=pltpu.SEMAPHORE)
EFFECT = pltpu.SideEffectType.DATAFLOW_SIDE_EFFECTING


def _in_hbm(a):
    return pltpu.with_memory_space_constraint(a, pltpu.HBM)


def _block_rows(ref, dev):
    r = ref.shape[0] // N_DEV
    start = pl.multiple_of((4 * dev[0] + 2 * dev[1] + dev[2]) * r, 16)
    return ref.at[pl.ds(start, r), :]


def _ag_first_copies(s_refs, land_refs, send_sems, recv_sems, receiving):
    x, y, c = _my_place()
    peers = [(x, y, 1 - c)] + [(*chip, c) for chip in _other_chips(x, y)]
    copies = []
    for k, peer in enumerate(peers):
        block = peer if receiving else (x, y, c)
        copies += [pltpu.make_async_remote_copy(
            src_ref=s_refs[w], dst_ref=_block_rows(land_refs[w], block),
            send_sem=send_sems.at[k * len(s_refs) + w], recv_sem=recv_sems.at[k * len(s_refs) + w],
            device_id=peer, device_id_type=MESH)
            for w in range(len(s_refs))]
    return copies


def _ag_start(shards, layer, after=()):
    nw = len(shards)

    def body(*refs):
        s_refs, land_refs = refs[:nw], refs[nw:2 * nw]
        token = refs[-1]
        sems = 2 * nw + len(after)
        for cp in _ag_first_copies(s_refs, land_refs, refs[sems], refs[sems + 1], False):
            cp.start()
        token[...] = jnp.zeros_like(token)

    lands = [lax.empty((N_DEV * s.shape[0], D_MODEL), BF16) for s in shards]
    out = pl.pallas_call(
        body, name="ag_start_%s" % layer,
        in_specs=[HBM] * (2 * nw) + [ANY] * len(after),
        out_specs=(SEM, SEM, *[HBM] * (2 * nw), pl.BlockSpec(memory_space=pltpu.VMEM)),
        out_shape=(pltpu.SemaphoreType.DMA((4 * nw,)), pltpu.SemaphoreType.DMA((4 * nw,)),
                   *[pltpu.HBM(a.shape, a.dtype) for a in list(shards) + lands],
                   jax.ShapeDtypeStruct((8, LANES), F32)),
        input_output_aliases={i: 2 + i for i in range(2 * nw)},
        compiler_params=pltpu.CompilerParams(has_side_effects=EFFECT),
    )(*[_in_hbm(a) for a in list(shards) + lands], *after)
    return out[0], out[1], out[2:2 + nw], out[2 + nw:2 + 2 * nw], out[-1]


def _ag_wait(send_sems, recv_sems, shards, lands, after, layer):
    nw = len(shards)

    def body(*refs):
        s_refs, land_refs = refs[:nw], refs[nw:2 * nw]
        for cp in _ag_first_copies(s_refs, land_refs, refs[2 * nw], refs[2 * nw + 1], True):
            cp.wait_send()
            cp.wait_recv()

    out = pl.pallas_call(
        body, name="ag_wait_%s" % layer,
        in_specs=[HBM] * (2 * nw) + [SEM, SEM] + [ANY] * len(after),
        out_specs=[HBM] * (2 * nw),
        out_shape=[pltpu.HBM(a.shape, a.dtype) for a in list(shards) + list(lands)],
        input_output_aliases={i: i for i in range(2 * nw)},
        compiler_params=pltpu.CompilerParams(has_side_effects=EFFECT),
    )(*shards, *lands, send_sems, recv_sems, *after)
    return out[:nw], out[nw:]


def _ag_pass_on(shards, lands):
    nw = len(shards)

    def body(*refs):
        s_refs, g_refs = refs[:nw], refs[2 * nw:3 * nw]
        send_sems, recv_sems, local_sems = refs[3 * nw:3 * nw + 3]
        stage = refs[3 * nw + 3:]
        x, y, c = _my_place()
        load = [pltpu.make_async_copy(s_refs[w], stage[w], local_sems.at[w]) for w in range(nw)]
        mine = [pltpu.make_async_copy(stage[w], _block_rows(g_refs[w], (x, y, c)), local_sems.at[w])
                for w in range(nw)]
        for cp in load:
            cp.start()
        sends, arrivals = [], []
        for j, chip in enumerate(_other_chips(x, y)):
            for w in range(nw):
                rows_out = _block_rows(g_refs[w], (*chip, c))
                rows_in = _block_rows(g_refs[w], (*chip, 1 - c))
                sends.append(pltpu.make_async_remote_copy(
                    src_ref=rows_out, dst_ref=rows_out, send_sem=send_sems.at[j, w], recv_sem=recv_sems.at[j, w],
                    device_id=(x, y, 1 - c), device_id_type=MESH))
                arrivals.append(pltpu.make_async_remote_copy(
                    src_ref=rows_in, dst_ref=rows_in, send_sem=send_sems.at[j, w], recv_sem=recv_sems.at[j, w],
                    device_id=(x, y, 1 - c), device_id_type=MESH))
        for cp in sends:
            cp.start()
        for w in range(nw):
            load[w].wait()
            mine[w].start()
        for cp in arrivals:
            cp.wait_recv()
        for cp in sends:
            cp.wait_send()
        for cp in mine:
            cp.wait()

    return pl.pallas_call(
        body, name="ag_pass_on",
        in_specs=[ANY] * (2 * nw), out_specs=[ANY] * nw,
        out_shape=[jax.ShapeDtypeStruct(a.shape, a.dtype) for a in lands],
        input_output_aliases={nw + i: i for i in range(nw)},
        scratch_shapes=[pltpu.SemaphoreType.DMA((3, nw)), pltpu.SemaphoreType.DMA((3, nw)),
                        pltpu.SemaphoreType.DMA((nw,))] + [pltpu.VMEM(s.shape, s.dtype) for s in shards],
        compiler_params=_cparams(),
    )(*shards, *lands)


def _rs_sibling_copies(p_refs, land_refs, send_sems, recv_sems):
    x, y, c = _my_place()
    return [pltpu.make_async_remote_copy(
        src_ref=p_refs[w].at[:, 1 - c], dst_ref=land_refs[w],
        send_sem=send_sems.at[w], recv_sem=recv_sems.at[w], device_id=(x, y, 1 - c), device_id_type=MESH)
        for w in range(len(p_refs))]


def _rs_sibling_start(parts, tag):
    nw = len(parts)

    def body(*refs):
        for cp in _rs_sibling_copies(refs[:nw], refs[nw:2 * nw], refs[2 * nw], refs[2 * nw + 1]):
            cp.start()
        refs[-1][...] = jnp.zeros_like(refs[-1])

    lands = [lax.empty(p.shape[:1] + p.shape[2:], BF16) for p in parts]
    out = pl.pallas_call(
        body, name="rs_sibling_start_%s" % tag,
        in_specs=[HBM] * (2 * nw),
        out_specs=(SEM, SEM, *[HBM] * (2 * nw), pl.BlockSpec(memory_space=pltpu.VMEM)),
        out_shape=(pltpu.SemaphoreType.DMA((nw,)), pltpu.SemaphoreType.DMA((nw,)),
                   *[pltpu.HBM(a.shape, a.dtype) for a in list(parts) + lands],
                   jax.ShapeDtypeStruct((8, LANES), F32)),
        input_output_aliases={i: 2 + i for i in range(2 * nw)},
        compiler_params=pltpu.CompilerParams(has_side_effects=EFFECT),
    )(*[_in_hbm(a) for a in list(parts) + lands])
    return out[0], out[1], out[2:2 + nw], out[2 + nw:2 + 2 * nw], out[-1]


def _rs_sibling_wait(send_sems, recv_sems, parts, lands, after, tag):
    nw = len(parts)

    def body(*refs):
        for cp in _rs_sibling_copies(refs[:nw], refs[nw:2 * nw], refs[2 * nw], refs[2 * nw + 1]):
            cp.wait_send()
            cp.wait_recv()

    out = pl.pallas_call(
        body, name="rs_sibling_wait_%s" % tag,
        in_specs=[HBM] * (2 * nw) + [SEM, SEM] + [ANY] * len(after),
        out_specs=[HBM] * (2 * nw),
        out_shape=[pltpu.HBM(a.shape, a.dtype) for a in list(parts) + list(lands)],
        input_output_aliases={i: i for i in range(2 * nw)},
        compiler_params=pltpu.CompilerParams(has_side_effects=EFFECT),
    )(*parts, *lands, send_sems, recv_sems, *after)
    return out[:nw], out[nw:]


def _rs_chip_sum(parts, gots, c):
    n = len(parts)

    def body(c_ref, *refs):
        for p_ref, g_ref, o_ref in zip(refs[:n], refs[n:2 * n], refs[2 * n:]):
            o_ref[...] = (p_ref[...].astype(F32) + g_ref[...].astype(F32)).astype(o_ref.dtype)

    mine = [pl.BlockSpec((None, None, p.shape[2], D_MODEL), lambda q, c_ref: (q, c_ref[0], 0, 0)) for p in parts]
    theirs = [pl.BlockSpec((None, g.shape[1], D_MODEL), lambda q, c_ref: (q, 0, 0)) for g in gots]
    return pl.pallas_call(
        body, name="rs_chip_sum",
        grid_spec=pltpu.PrefetchScalarGridSpec(
            num_scalar_prefetch=1, grid=(4,), in_specs=mine + theirs, out_specs=theirs),
        out_shape=[jax.ShapeDtypeStruct(g.shape, BF16) for g in gots],
        compiler_params=_cparams(("parallel",)),
    )(c, *parts, *gots)


def _other_chips(x, y):
    return [(1 - x, y), (x, 1 - y), (1 - x, 1 - y)]


def _rs_chip_copies(s_refs, land_refs, send_sems, recv_sems):
    x, y, c = _my_place()
    copies = []
    for k, chip in enumerate(_other_chips(x, y)):
        q = 2 * chip[0] + chip[1]
        copies += [pltpu.make_async_remote_copy(
            src_ref=s_refs[w].at[q], dst_ref=land_refs[w].at[k],
            send_sem=send_sems.at[k * len(s_refs) + w], recv_sem=recv_sems.at[k * len(s_refs) + w],
            device_id=(*chip, c), device_id_type=MESH)
            for w in range(len(s_refs))]
    return copies


def _rs_chip_start(sums, layer):
    nw = len(sums)

    def body(*refs):
        s_refs, land_refs = refs[:nw], refs[nw:2 * nw]
        send_sems, recv_sems = refs[2 * nw], refs[2 * nw + 1]
        token = refs[-1]
        for cp in _rs_chip_copies(s_refs, land_refs, send_sems, recv_sems):
            cp.start()
        token[...] = jnp.zeros_like(token)

    lands = [lax.empty((3,) + s.shape[1:], BF16) for s in sums]
    out = pl.pallas_call(
        body, name="rs_chip_start_%s" % layer,
        in_specs=[HBM] * (2 * nw),
        out_specs=(SEM, SEM, *[HBM] * (2 * nw), pl.BlockSpec(memory_space=pltpu.VMEM)),
        out_shape=(pltpu.SemaphoreType.DMA((3 * nw,)), pltpu.SemaphoreType.DMA((3 * nw,)),
                   *[pltpu.HBM(a.shape, a.dtype) for a in list(sums) + lands],
                   jax.ShapeDtypeStruct((8, LANES), F32)),
        input_output_aliases={i: 2 + i for i in range(2 * nw)},
        compiler_params=pltpu.CompilerParams(has_side_effects=EFFECT),
    )(*[_in_hbm(a) for a in list(sums) + lands])
    return out[0], out[1], out[2:2 + nw], out[2 + nw:2 + 2 * nw], out[-1]


def _rs_chip_wait(send_sems, recv_sems, sums, lands, after, layer):
    nw = len(sums)

    def body(*refs):
        s_refs, land_refs = refs[:nw], refs[nw:2 * nw]
        for cp in _rs_chip_copies(s_refs, land_refs, refs[2 * nw], refs[2 * nw + 1]):
            cp.wait_send()
            cp.wait_recv()

    out = pl.pallas_call(
        body, name="rs_chip_wait_%s" % layer,
        in_specs=[HBM] * (2 * nw) + [SEM, SEM] + [ANY] * len(after),
        out_specs=[HBM] * (2 * nw),
        out_shape=[pltpu.HBM(a.shape, a.dtype) for a in list(sums) + list(lands)],
        input_output_aliases={i: i for i in range(2 * nw)},
        compiler_params=pltpu.CompilerParams(has_side_effects=EFFECT),
    )(*sums, *lands, send_sems, recv_sems, *after)
    return out[:nw], out[nw:]


def _rs_finish(sums, gots, q, layer, into):
    n = len(sums)

    def body(q_ref, *refs):
        for s_ref, g_ref, o_ref in zip(refs[:n], refs[n:2 * n], refs[len(refs) - n:]):
            o_ref[...] = ((s_ref[...].astype(F32) + g_ref[0].astype(F32)) + g_ref[1].astype(F32)) + g_ref[2].astype(F32)

    rows = [s.shape[1] for s in sums]
    in_specs = [pl.BlockSpec((None, r, D_MODEL), lambda i, q_ref: (q_ref[0], 0, 0)) for r in rows]
    in_specs += [pl.BlockSpec((3, r, D_MODEL), lambda i, q_ref: (0, 0, 0)) for r in rows]
    args = [q, *sums, *gots]
    aliases = {}
    if into is not None:
        in_specs += [ANY] * n
        aliases = {len(args) + i: i for i in range(n)}
        args += list(into)
    return pl.pallas_call(
        body, name="rs_finish",
        grid_spec=pltpu.PrefetchScalarGridSpec(
            num_scalar_prefetch=1, grid=(1,), in_specs=in_specs,
            out_specs=[pl.BlockSpec((None, r, D_MODEL), lambda i, q_ref: (layer, 0, 0)) for r in rows]),
        out_shape=[jax.ShapeDtypeStruct((DEPTH, r, D_MODEL), F32) for r in rows],
        input_output_aliases=aliases,
        compiler_params=_cparams(("arbitrary",)),
    )(*args)


def _allreduce_small(vec):
    R = vec.shape[0]
    assert R % (8 * N_DEV) == 0
    P = R // N_DEV

    def body(v_ref, o_ref, buf, send1, recv1, send2, recv2):
        x, y, c = _my_place()
        me = 4 * x + 2 * y + c

        def piece(ref, d):
            return ref.at[pl.ds(pl.multiple_of(d * P, 8), P), :]

        def peer(k):
            p = me ^ k
            return p, (p >> 2, (p >> 1) & 1, p & 1)

        scatter = []
        for k in range(1, N_DEV):
            p, where = peer(k)
            scatter.append(pltpu.make_async_remote_copy(
                src_ref=piece(v_ref, p), dst_ref=buf.at[k], send_sem=send1.at[k - 1], recv_sem=recv1.at[k - 1],
                device_id=where, device_id_type=MESH))
        for cp in scatter:
            cp.start()
        buf[0] = piece(v_ref, me)[...]
        for cp in scatter:
            cp.wait()
        acc = buf[me]
        for d in range(1, N_DEV):
            acc = acc + buf[me ^ d]
        piece(o_ref, me)[...] = acc
        spread, arrivals = [], []
        for k in range(1, N_DEV):
            p, where = peer(k)
            spread.append(pltpu.make_async_remote_copy(
                src_ref=piece(o_ref, me), dst_ref=piece(o_ref, me), send_sem=send2.at[k - 1], recv_sem=recv2.at[k - 1],
                device_id=where, device_id_type=MESH))
            arrivals.append(pltpu.make_async_remote_copy(
                src_ref=piece(o_ref, p), dst_ref=piece(o_ref, p), send_sem=send2.at[k - 1], recv_sem=recv2.at[k - 1],
                device_id=where, device_id_type=MESH))
        for cp in spread:
            cp.start()
        for cp in arrivals:
            cp.wait_recv()
        for cp in spread:
            cp.wait_send()

    sems = pltpu.SemaphoreType.DMA((N_DEV - 1,))
    return pl.pallas_call(
        body, name="allreduce_small",
        in_specs=[pl.BlockSpec(memory_space=pltpu.VMEM)], out_specs=pl.BlockSpec(memory_space=pltpu.VMEM),
        out_shape=jax.ShapeDtypeStruct((R, LANES), F32),
        scratch_shapes=[pltpu.VMEM((N_DEV, P, LANES), F32), sems, sems, sems, sems],
        compiler_params=_cparams(),
    )(vec)


def _pack(arrs):
    flat = jnp.concatenate([a.reshape(-1) for a in arrs])
    pad = (-flat.shape[0]) % (8 * N_DEV * LANES)
    return jnp.pad(flat, (0, pad)).reshape(-1, LANES)


def _unpack(packed, shapes):
    flat = packed.reshape(-1)
    out, off = [], 0
    for s in shapes:
        n = math.prod(s)
        out.append(flat[off:off + n].reshape(s))
        off += n
    return out


def kernel(x, w_in, w_conv, w_pool, pool_scale, sgu_ln_g, w_spatial, b_spatial, w_o, ln1_g, ln1_b, w_gate_up, w_down, ln2_g, ln2_b, loss_target, m_w_in, m_w_conv, m_w_pool, m_pool_scale, m_sgu_ln_g, m_w_spatial, m_b_spatial, m_w_o, m_ln1_g, m_ln1_b, m_w_gate_up, m_w_down, m_ln2_g, m_ln2_b, v_w_in, v_w_conv, v_w_pool, v_pool_scale, v_sgu_ln_g, v_w_spatial, v_b_spatial, v_w_o, v_ln1_g, v_ln1_b, v_w_gate_up, v_w_down, v_ln2_g, v_ln2_b):
    L = DEPTH
    T = x.shape[1]
    mx, my, mc = _my_place()
    dev = 4 * mx + 2 * my + mc
    xs = x[0]
    target = loss_target[0]

    shards = (jnp.swapaxes(w_in, 1, 2).astype(BF16), jnp.swapaxes(w_gate_up, 1, 2).astype(BF16),
              w_o.astype(BF16), w_down.astype(BF16))
    first_gather = _ag_start_layer(shards, 0, [])

    conv_cols = w_conv.shape[2]
    w_conv_z = lax.dynamic_update_slice(jnp.zeros((L, 3, CONV_W), F32), w_conv, (0, 0, dev * conv_cols))
    w_conv_full = _allreduce_small(_pack([w_conv_z]))
    w_conv_full = _unpack(w_conv_full, [(L, 3, CONV_W)])[0]

    loss_tile, grad_x2, big_grads, small_grads = _local_step(
        xs, target, shards, first_gather, w_conv_full, w_pool, pool_scale, sgu_ln_g, w_spatial, b_spatial,
        ln1_g, ln1_b, ln2_g, ln2_b)
    loss = lax.psum(loss_tile[0, 0], ("x", "y", "c"))
    grad_x = grad_x2[None]
    big_w = (w_in, w_gate_up, w_o, w_down)
    big_m = (m_w_in, m_w_gate_up, m_w_o, m_w_down)
    big_v = (v_w_in, v_w_gate_up, v_w_o, v_w_down)
    small_w = [w_conv_full, w_pool, pool_scale, sgu_ln_g, w_spatial, b_spatial, ln1_g, ln1_b, ln2_g, ln2_b]
    small_m = [m_w_conv, m_w_pool, m_pool_scale, m_sgu_ln_g, m_w_spatial, m_b_spatial, m_ln1_g, m_ln1_b, m_ln2_g, m_ln2_b]
    small_v = [v_w_conv, v_w_pool, v_pool_scale, v_sgu_ln_g, v_w_spatial, v_b_spatial, v_ln1_g, v_ln1_b, v_ln2_g, v_ln2_b]
    grads, deltas, new_m, new_v = _reduce_and_update(
        big_grads, small_grads, big_w, big_m, big_v, small_w, small_m, small_v)
    return (loss, grad_x, *grads, *deltas, *new_m, *new_v)


def _ag_start_layer(shards, l, after):
    s_in, s_gu, s_o, s_dn = [s[l] for s in shards]
    first = _ag_start([s_in, s_o], "%da" % l, after=after)
    return first, _ag_start([s_gu, s_dn], "%db" % l, after=[first[4]])


def _ag_finish(gather, after, tag):
    send_sems, recv_sems, shards, lands, _ = gather
    shards, lands = _ag_wait(send_sems, recv_sems, shards, lands, after, tag)
    return _ag_pass_on(shards, lands)


def _rs_begin(parts, tag):
    return _rs_sibling_start([p.reshape(4, 2, p.shape[0] // N_DEV, D_MODEL) for p in parts], tag)


def _rs_continue(sibling_flight, after, c_arr, tag):
    send_sems, recv_sems, parts, lands, _ = sibling_flight
    parts, got = _rs_sibling_wait(send_sems, recv_sems, parts, lands, after, tag)
    return _rs_chip_start(_rs_chip_sum(parts, got, c_arr), tag)


def _local_step(xs, target, shards, gather, w_conv_full, w_pool, pool_scale, sgu_ln_g, w_spatial, b_spatial,
                ln1_g, ln1_b, ln2_g, ln2_b):
    L = DEPTH
    T = xs.shape[0]
    mx, my, mc = _my_place()
    c_arr = jnp.reshape(mc, (1,)).astype(jnp.int32)
    q_arr = jnp.reshape(2 * mx + my, (1,)).astype(jnp.int32)
    eye2 = jnp.eye(2, dtype=F32)
    wp = w_pool.reshape(L, 2, 2, HALF, HALF)
    wpool_bd = jnp.einsum("ltgcd,gh->ltgchd", wp, eye2).reshape(L, 2, LANES, LANES)
    wsp_t = w_spatial.reshape(L, 3, 2 * CHUNK, CHUNK)
    bias_t = jnp.repeat(jnp.swapaxes(b_spatial.reshape(L, 3, 2, CHUNK), 2, 3), HALF, axis=3)
    ones = jnp.ones((1, D_MODEL), F32)
    zeros = jnp.zeros((1, D_MODEL), F32)

    saved = []
    prev, pg, pb = xs, ones, zeros
    prev_b = xs.astype(BF16)
    weights = []
    for l in range(L):
        g_in, g_o = _ag_finish(gather[0], [] if l == 0 else [prev_b], "%da" % l)
        proj = _mm(prev_b, g_in, "nt", F32, 512, IN_W, D_MODEL, "mm_proj", deps=[gather[1][4]] if l == 0 else [])
        mixcat = _mixer_fwd(proj, w_conv_full[l], wpool_bd[l], pool_scale[l][None], sgu_ln_g[l][None], wsp_t[l], bias_t[l])
        xhat1, rstd1, h_b = _mm_ln_fwd(mixcat, g_o, prev, pg, pb, ln1_g[l][None], ln1_b[l][None], "mm_wo_ln")
        g_gu, g_dn = _ag_finish(gather[1], [h_b], "%db" % l)
        weights.append((g_in, g_gu, g_o, g_dn))
        deps = []
        if l + 1 < L:
            gather = _ag_start_layer(shards, l + 1, [g_gu])
            deps = [gather[1][4]]
        g_act, u_act, act = _mm_swiglu_fwd(h_b, g_gu, deps=deps)
        xhat2, rstd2, y_b = _mm_ln_fwd(act, g_dn, xhat1, ln1_g[l][None], ln1_b[l][None], ln2_g[l][None], ln2_b[l][None],
                                       "mm_down_ln")
        saved.append((prev_b, proj, mixcat, xhat1, rstd1, h_b, g_act, u_act, act, xhat2, rstd2))
        prev, pg, pb, prev_b = xhat2, ln2_g[l][None], ln2_b[l][None], y_b

    loss_tile, dy = _loss_head(prev, pg, pb, target)

    small = [None] * L
    big = None
    sibling_flight = None
    above = None
    for l in reversed(range(L)):
        prev_b, proj, mixcat, xhat1, rstd1, h_b, g_act, u_act, act, xhat2, rstd2 = saved[l]
        g_in, g_gu, g_o, g_dn = weights[l]
        chip_flight = None
        if above is None:
            dr2, dr2_b, dg2, db2 = _ln_bwd(None, dy, xhat2, rstd2, ln2_g[l][None])
        else:
            dr2, dr2_b, dg2, db2 = _mm_ln_bwd([above[0]], above[1], above[2], xhat2, rstd2, ln2_g[l][None],
                                              "mm_dx_ln", deps=[sibling_flight[4]])
            chip_flight = _rs_continue(sibling_flight, [dr2_b], c_arr, str(l + 1))
        dg_b, du_b = _mm_swiglu_bwd(dr2_b, g_dn, g_act, u_act, deps=[chip_flight[4]] if chip_flight else [])
        p_dn = _mm(act, dr2_b, "tn", BF16, DW_TM, D_MODEL, T, "mm_dw_down")
        p_gu = _mm(dg_b, h_b, "tn", BF16, DW_TM, D_MODEL, T, "mm_dw_gate", out_rows=2 * D_FF)
        p_gu = _mm(du_b, h_b, "tn", BF16, DW_TM, D_MODEL, T, "mm_dw_up", out_rows=2 * D_FF, out_off=D_FF, out_into=p_gu)
        ffn_sibling = _rs_begin([p_gu, p_dn], "0b") if l == 0 else None
        dr1, dr1_b, dg1, db1 = _mm_ln_bwd([dg_b, du_b], g_gu, dr2, xhat1, rstd1, ln1_g[l][None], "mm_dh_ln",
                                          deps=[ffn_sibling[4]] if l == 0 else [])
        ffn_flight = _rs_continue(ffn_sibling, [dr1_b], c_arr, "0b") if l == 0 else None
        dmix = _mm(dr1_b, g_o, "nt", F32, T, 512, D_MODEL, "mm_dmix", deps=[ffn_flight[4]] if l == 0 else [])
        p_o = _mm(mixcat, dr1_b, "tn", BF16, 512, D_MODEL, T, "mm_dw_o")
        dproj, dwc, dwp, dps, dlng, dwsp, dbias = _mixer_bwd(
            proj, dmix, w_conv_full[l], wpool_bd[l], pool_scale[l][None], sgu_ln_g[l][None], wsp_t[l], bias_t[l])
        p_in = _mm(dproj, prev_b, "tn", BF16, IN_W, D_MODEL, T, "mm_dw_in")
        small[l] = (dwc, dwp, dps, dlng, dwsp, dbias, dg1, db1, dg2, db2)
        above = (dproj, g_in, dr1)
        if chip_flight is not None:
            big = list(_rs_chip_finish(chip_flight, [p_in], q_arr, str(l + 1), l + 1, big))
        if l > 0:
            sibling_flight = _rs_begin([p_in, p_gu, p_o, p_dn], str(l))
        else:
            big[1], big[3] = _rs_chip_finish(ffn_flight, [p_in], q_arr, "0b", 0, [big[1], big[3]])
            sibling_flight = _rs_begin([p_in, p_o], "0a")
    grad_x = _mm_ln_bwd([above[0]], above[1], above[2], None, None, None, "mm_dx_out", deps=[sibling_flight[4]])
    last_flight = _rs_continue(sibling_flight, [grad_x], c_arr, "0a")
    big_grads = (big, last_flight, q_arr)

    def stack(i):
        return jnp.stack([small[l][i] for l in range(L)])

    dwp_bd = stack(1).reshape(L, 2, 2, HALF, 2, HALF)
    dwp_all = jnp.einsum("ltgchd,gh->ltgcd", dwp_bd, eye2).reshape(L, 4, HALF, HALF)
    dbs_all = jnp.swapaxes(stack(5)[:, :, :, :2], 2, 3).reshape(L, 6, CHUNK)
    small_grads = [stack(0), dwp_all, stack(2).reshape(L, POOL_W), stack(3).reshape(L, SGU_W),
                   stack(4).reshape(L, 6, CHUNK, CHUNK), dbs_all] + [stack(i).reshape(L, D_MODEL) for i in (6, 7, 8, 9)]
    return loss_tile, grad_x, big_grads, small_grads


def _rs_chip_finish(in_flight, after, q, tag, layer, into):
    send_sems, recv_sems, sums, lands, _ = in_flight
    sums, got = _rs_chip_wait(send_sems, recv_sems, sums, lands, after, tag)
    return _rs_finish(sums, got, q, layer, into)


def _reduce_and_update(big_grads, small_grads, big_w, big_m, big_v, small_w, small_m, small_v):
    L = DEPTH
    mx, my, mc = _my_place()
    dev = 4 * mx + 2 * my + mc
    conv_cols = CONV_W // N_DEV
    w_in, w_gate_up, w_o, w_down = big_w
    m_w_in, m_w_gate_up, m_w_o, m_w_down = big_m
    v_w_in, v_w_gate_up, v_w_o, v_w_down = big_v
    small_shapes = [a.shape for a in small_grads]
    packed_g = _allreduce_small(_pack(small_grads))

    big, last_flight, q_arr = big_grads
    big[0], big[2] = _rs_chip_finish(last_flight, [packed_g], q_arr, "0a", 0, [big[0], big[2]])
    gt_in, gt_gu, g_w_o, g_w_dn = big
    g_w_in = jnp.swapaxes(gt_in, 1, 2)
    g_w_gu = jnp.swapaxes(gt_gu, 1, 2)

    def widen_conv(a):
        return lax.dynamic_update_slice(jnp.zeros((L, 3, CONV_W), F32), a, (0, 0, dev * conv_cols))

    small_m = [widen_conv(small_m[0])] + list(small_m[1:])
    small_v = [widen_conv(small_v[0])] + list(small_v[1:])
    pk_d, pk_m, pk_v = _adamw(_pack(small_w), packed_g, _pack(small_m), _pack(small_v), packed_g.shape[0] // 2)
    sg = _unpack(packed_g, small_shapes)
    sd = _unpack(pk_d, small_shapes)
    sm = _unpack(pk_m, small_shapes)
    sv = _unpack(pk_v, small_shapes)

    def conv_cols_of(a):
        return lax.dynamic_slice(a, (0, 0, dev * conv_cols), (L, 3, conv_cols))

    for lst in (sg, sd, sm, sv):
        lst[0] = conv_cols_of(lst[0])

    tr = lambda a: jnp.swapaxes(a, 1, 2)
    d_in, m_in, v_in = [tr(a) for a in _adamw(tr(w_in), gt_in, tr(m_w_in), tr(v_w_in), gt_in.shape[1])]
    d_gu, m_gu, v_gu = [tr(a) for a in _adamw(tr(w_gate_up), gt_gu, tr(m_w_gate_up), tr(v_w_gate_up), gt_gu.shape[1] // 2)]
    d_o, m_o, v_o = _adamw(w_o, g_w_o, m_w_o, v_w_o, 128)
    d_dn, m_dn, v_dn = _adamw(w_down, g_w_dn, m_w_down, v_w_down, 352)

    def ordered(big_in, big_o, big_gu, big_dn, sm_list):
        return [big_in, sm_list[0], sm_list[1], sm_list[2], sm_list[3], sm_list[4], sm_list[5], big_o,
                sm_list[6], sm_list[7], big_gu, big_dn, sm_list[8], sm_list[9]]

    grads = ordered(g_w_in, g_w_o, g_w_gu, g_w_dn, sg)
    deltas = ordered(d_in, d_o, d_gu, d_dn, sd)
    new_m = ordered(m_in, m_o, m_gu, m_dn, sm)
    new_v = ordered(v_in, v_o, v_gu, v_dn, sv)
    return grads, deltas, new_m, new_v
```

```python
import functools
import math

import jax
import jax.numpy as jnp
from jax import lax
from jax.experimental import pallas as pl
from jax.experimental.pallas import tpu as pltpu

F32 = jnp.float32
BF16 = jnp.bfloat16
MESH = pl.DeviceIdType.MESH

D_MODEL = 1024
DEPTH = 4
CONV_W = 384
POOL_W = 256
SGU_W = 384
IN_W = 3 * CONV_W + POOL_W + 2 * SGU_W
D_FF = 2816
CHUNK = 128
ALPHA = float((2 * DEPTH) ** 0.25)
LN_EPS = 1e-5
ADAM_LR, ADAM_B1, ADAM_B2, ADAM_EPS, ADAM_WD, ADAM_STEP = 0.001, 0.9, 0.999, 1e-08, 0.01, 10

N_DEV = 8
LANES = 128
HALF = 64
SHARD_ROWS = (IN_W // N_DEV, 2 * D_FF // N_DEV, D_MODEL // N_DEV, D_FF // N_DEV)
VMEM_LIMIT = 52 * 1024 * 1024

INV_SQRT2 = 0.7071067811865476
INV_SQRT_2PI = 0.3989422804014327


def _cparams(sem=None, **kw):
    if sem is not None:
        kw["dimension_semantics"] = sem
    return pltpu.CompilerParams(vmem_limit_bytes=VMEM_LIMIT, **kw)


_DN = {"nn": (((1,), (0,)), ((), ())), "nt": (((1,), (1,)), ((), ())), "tn": (((0,), (0,)), ((), ()))}


def _mm(a, b, mode, out_dtype, tm, tn, tk, name, deps=(), out_rows=None, out_off=0, out_into=None):
    if mode == "nn":
        (M, K), N = a.shape, b.shape[1]
    elif mode == "nt":
        (M, K), N = a.shape, b.shape[0]
    else:
        (K, M), N = a.shape, b.shape[1]
    assert M % tm == 0 and N % tn == 0 and K % tk == 0 and out_off % tm == 0, (M, N, K, tm, tn, tk)
    nk = K // tk
    if out_into is not None:
        deps = tuple(deps) + (out_into,)
    nd = len(deps)
    row_off = out_off // tm

    def body(*refs):
        a_ref, b_ref, o_ref = refs[0], refs[1], refs[2 + nd]
        acc_ref = refs[3 + nd] if nk > 1 else None
        p = lax.dot_general(a_ref[...], b_ref[...], _DN[mode], preferred_element_type=F32)
        if nk == 1:
            o_ref[...] = p.astype(o_ref.dtype)
        else:
            k = pl.program_id(2)

            @pl.when(k == 0)
            def _():
                acc_ref[...] = p

            @pl.when(k > 0)
            def _():
                acc_ref[...] += p

            @pl.when(k == nk - 1)
            def _():
                o_ref[...] = acc_ref[...].astype(o_ref.dtype)

    if mode == "nn":
        a_spec = pl.BlockSpec((tm, tk), lambda i, j, k: (i, k))
        b_blk, b_idx = (tk, tn), (lambda i, j, k: (k, j))
    elif mode == "nt":
        a_spec = pl.BlockSpec((tm, tk), lambda i, j, k: (i, k))
        b_blk, b_idx = (tn, tk), (lambda i, j, k: (j, k))
    else:
        a_spec = pl.BlockSpec((tk, tm), lambda i, j, k: (k, i))
        b_blk, b_idx = (tk, tn), (lambda i, j, k: (k, j))
    return pl.pallas_call(
        body,
        name=name,
        grid=(M // tm, N // tn, nk),
        in_specs=[a_spec, pl.BlockSpec(b_blk, b_idx)] + [pl.BlockSpec(memory_space=pl.ANY)] * nd,
        out_specs=pl.BlockSpec((tm, tn), lambda i, j, k: (i + row_off, j)),
        out_shape=jax.ShapeDtypeStruct((out_rows or M, N), out_dtype),
        scratch_shapes=[pltpu.VMEM((tm, tn), F32)] if nk > 1 else [],
        input_output_aliases={1 + nd: 0} if out_into is not None else {},
        compiler_params=_cparams(("parallel", "parallel", "arbitrary")),
    )(a, b, *deps)


LN_TM = 512


def _mm_ln_fwd(a, b, prev, pg, pb, g, bias, name):
    T, K = a.shape
    tm = LN_TM

    def body(a_ref, b_ref, prev_ref, pg_ref, pb_ref, g_ref, bias_ref, xhat_ref, rstd_ref, y_ref):
        mm = jnp.dot(a_ref[...], b_ref[...], preferred_element_type=F32)
        r = ALPHA * (prev_ref[...] * pg_ref[...] + pb_ref[...]) + mm
        mu = jnp.mean(r, axis=-1, keepdims=True)
        xc = r - mu
        var = jnp.mean(xc * xc, axis=-1, keepdims=True)
        rstd = lax.rsqrt(var + LN_EPS)
        xhat = xc * rstd
        xhat_ref[...] = xhat
        rstd_ref[...] = rstd
        y_ref[...] = (xhat * g_ref[...] + bias_ref[...]).astype(y_ref.dtype)

    row = pl.BlockSpec((tm, D_MODEL), lambda i: (i, 0))
    vec = pl.BlockSpec((1, D_MODEL), lambda i: (0, 0))
    return pl.pallas_call(
        body, name=name, grid=(T // tm,),
        in_specs=[pl.BlockSpec((tm, K), lambda i: (i, 0)),
                  pl.BlockSpec((K, D_MODEL), lambda i: (0, 0), pipeline_mode=pl.Buffered(1)),
                  row, vec, vec, vec, vec],
        out_specs=[row, pl.BlockSpec((tm, 1), lambda i: (i, 0)), row],
        out_shape=[jax.ShapeDtypeStruct((T, D_MODEL), F32), jax.ShapeDtypeStruct((T, 1), F32),
                   jax.ShapeDtypeStruct((T, D_MODEL), BF16)],
        compiler_params=_cparams(("parallel",)),
    )(a, b, prev, pg, pb, g, bias)


def _mm_ln_bwd(a_list, b, dres, xhat, rstd, g, name, deps=()):
    T = a_list[0].shape[0]
    tm = LN_TM
    na, nd = len(a_list), len(deps)
    ks = [a.shape[1] for a in a_list]
    last = xhat is None

    def body(*refs):
        a_refs, b_ref, dres_ref = refs[:na], refs[na], refs[na + 1]
        mm, off = None, 0
        for a_ref, k in zip(a_refs, ks):
            part = jnp.dot(a_ref[...], b_ref[off:off + k, :], preferred_element_type=F32)
            mm = part if mm is None else mm + part
            off += k
        dy = ALPHA * dres_ref[...] + mm
        if last:
            refs[-1][...] = dy
            return
        xhat_ref, rstd_ref, g_ref = refs[na + 2:na + 5]
        dr_ref, drb_ref, dg_ref, db_ref = refs[-4:]
        xhat_v = xhat_ref[...]

        @pl.when(pl.program_id(0) == 0)
        def _():
            dg_ref[...] = jnp.zeros_like(dg_ref)
            db_ref[...] = jnp.zeros_like(db_ref)

        dg_ref[...] += jnp.sum(dy * xhat_v, axis=0, keepdims=True)
        db_ref[...] += jnp.sum(dy, axis=0, keepdims=True)
        dxh = dy * g_ref[...]
        m1 = jnp.mean(dxh, axis=-1, keepdims=True)
        m2 = jnp.mean(dxh * xhat_v, axis=-1, keepdims=True)
        dr = rstd_ref[...] * (dxh - m1 - xhat_v * m2)
        dr_ref[...] = dr
        drb_ref[...] = dr.astype(drb_ref.dtype)

    row = pl.BlockSpec((tm, D_MODEL), lambda i: (i, 0))
    vec = pl.BlockSpec((1, D_MODEL), lambda i: (0, 0))
    in_specs = [pl.BlockSpec((tm, k), lambda i: (i, 0)) for k in ks]
    in_specs += [pl.BlockSpec((sum(ks), D_MODEL), lambda i: (0, 0), pipeline_mode=pl.Buffered(1)), row]
    args = list(a_list) + [b, dres]
    if last:
        out_specs, out_shape = row, jax.ShapeDtypeStruct((T, D_MODEL), F32)
    else:
        in_specs += [row, pl.BlockSpec((tm, 1), lambda i: (i, 0)), vec]
        args += [xhat, rstd, g]
        out_specs = [row, row, vec, vec]
        out_shape = [jax.ShapeDtypeStruct((T, D_MODEL), F32), jax.ShapeDtypeStruct((T, D_MODEL), BF16),
                     jax.ShapeDtypeStruct((1, D_MODEL), F32), jax.ShapeDtypeStruct((1, D_MODEL), F32)]
    return pl.pallas_call(
        body, name=name, grid=(T // tm,),
        in_specs=in_specs + [pl.BlockSpec(memory_space=pl.ANY)] * nd,
        out_specs=out_specs, out_shape=out_shape,
        compiler_params=_cparams(("parallel",) if last else ("arbitrary",)),
    )(*args, *deps)


DW_TM = 1408
FF_TN = 256
SAVED_GU = BF16


def _mm_swiglu_fwd(h, w_gu, deps=()):
    T = h.shape[0]
    nj = D_FF // FF_TN
    nd = len(deps)

    def body(*refs):
        h_ref, wg_ref, wu_ref = refs[:3]
        g_ref, u_ref, act_ref = refs[3 + nd:]
        hv = h_ref[...]
        gv = lax.dot_general(hv, wg_ref[...], _DN["nt"], preferred_element_type=F32)
        uv = lax.dot_general(hv, wu_ref[...], _DN["nt"], preferred_element_type=F32)
        g_ref[...] = gv.astype(g_ref.dtype)
        u_ref[...] = uv.astype(u_ref.dtype)
        act_ref[...] = (gv * jax.nn.sigmoid(gv) * uv).astype(act_ref.dtype)

    col = pl.BlockSpec((T, FF_TN), lambda j: (0, j))
    return pl.pallas_call(
        body, name="mm_gate_up_swiglu", grid=(nj,),
        in_specs=[pl.BlockSpec((T, D_MODEL), lambda j: (0, 0)),
                  pl.BlockSpec((FF_TN, D_MODEL), lambda j: (j, 0)),
                  pl.BlockSpec((FF_TN, D_MODEL), lambda j: (j + nj, 0))] + [pl.BlockSpec(memory_space=pl.ANY)] * nd,
        out_specs=[col, col, col],
        out_shape=[jax.ShapeDtypeStruct((T, D_FF), SAVED_GU), jax.ShapeDtypeStruct((T, D_FF), SAVED_GU),
                   jax.ShapeDtypeStruct((T, D_FF), BF16)],
        compiler_params=_cparams(("parallel",)),
    )(h, w_gu, w_gu, *deps)


def _mm_swiglu_bwd(dr, w_dn, g, u, deps=()):
    T = dr.shape[0]

    def body(*refs):
        dr_ref, w_ref, g_ref, u_ref = refs[:4]
        dg_ref, du_ref = refs[-2:]
        da = lax.dot_general(dr_ref[...], w_ref[...], _DN["nt"], preferred_element_type=F32)
        gv, uv = g_ref[...].astype(F32), u_ref[...].astype(F32)
        s = jax.nn.sigmoid(gv)
        du_ref[...] = (da * (gv * s)).astype(du_ref.dtype)
        dg_ref[...] = (da * uv * (s * (1.0 + gv * (1.0 - s)))).astype(dg_ref.dtype)

    col = pl.BlockSpec((T, FF_TN), lambda j: (0, j))
    return pl.pallas_call(
        body, name="mm_dact_swiglu", grid=(D_FF // FF_TN,),
        in_specs=[pl.BlockSpec((T, D_MODEL), lambda j: (0, 0)), pl.BlockSpec((FF_TN, D_MODEL), lambda j: (j, 0)),
                  col, col] + [ANY] * len(deps),
        out_specs=[col, col],
        out_shape=[jax.ShapeDtypeStruct((T, D_FF), BF16)] * 2,
        compiler_params=_cparams(("parallel",)),
    )(dr, w_dn, g, u, *deps)


def _gelu(x):
    return 0.5 * x * (1.0 + lax.erf(x * INV_SQRT2))


def _gelu_grad(x):
    return 0.5 * (1.0 + lax.erf(x * INV_SQRT2)) + x * (jnp.exp(-0.5 * x * x) * INV_SQRT_2PI)


def _shift_down(z, k):
    row = lax.broadcasted_iota(jnp.int32, z.shape, 0)
    return jnp.where(row >= k, pltpu.roll(z, k, 0), 0.0)


def _shift_up(z, k):
    n = z.shape[0]
    row = lax.broadcasted_iota(jnp.int32, z.shape, 0)
    return jnp.where(row < n - k, pltpu.roll(z, n - k, 0), 0.0)


def _lo_mask(shape):
    return lax.broadcasted_iota(jnp.int32, shape, len(shape) - 1) < HALF


def _seg_mean(x, lo):
    a = jnp.sum(jnp.where(lo, x, 0.0), axis=-1, keepdims=True)
    b = jnp.sum(jnp.where(lo, 0.0, x), axis=-1, keepdims=True)
    return jnp.where(lo, a, b) * (1.0 / HALF)


def _pool_windows(first):
    lo = _lo_mask((1, LANES))
    return jnp.where(first, jnp.where(lo, 2.0, 4.0), jnp.where(lo, 8.0, 16.0)), lo


def _pool_mean_minus_token(p, first):
    wl, lo = _pool_windows(first)
    s2 = p + _shift_down(p, 1)
    s4 = s2 + _shift_down(s2, 2)
    s8 = s4 + _shift_down(s4, 4)
    s16 = s8 + _shift_down(s8, 8)
    win = jnp.where(first, jnp.where(lo, s2, s4), jnp.where(lo, s8, s16))
    t1 = (lax.broadcasted_iota(jnp.int32, p.shape, 0) + 1).astype(F32)
    count = jnp.minimum(t1, wl)
    return win / count - p, count


SGU_UNROLL = 4


def _tril_keep():
    r = lax.broadcasted_iota(jnp.int32, (2 * CHUNK, CHUNK), 0)
    s = lax.broadcasted_iota(jnp.int32, (2 * CHUNK, CHUNK), 1)
    return s <= (r & (CHUNK - 1))


def _sgu_chunk_fwd(u, v, g, wm, bias, lo):
    ug = _gelu(u)
    vg = _gelu(v)
    mu = _seg_mean(vg, lo)
    xc = vg - mu
    var = _seg_mean(xc * xc, lo)
    rstd = lax.rsqrt(var + LN_EPS)
    vn = xc * rstd
    vh = (vn * g).astype(BF16)
    mm2 = jnp.dot(wm, vh, preferred_element_type=F32)
    mixed = jnp.where(lo, mm2[:CHUNK], mm2[CHUNK:]) + bias
    return ug, vn, rstd, vh, mixed


def _mixer_fwd(proj, wconv, wpool_bd, pscale, lng, wsp, bias):
    T = proj.shape[0]
    nchunk = T // CHUNK

    def body(a_ref, b_ref, c_ref, wc_ref, wp_ref, ps_ref, lng_ref, wsp_ref, bias_ref, o_ref):
        j = pl.program_id(0)

        @pl.when(j < 3)
        def _conv():
            z = c_ref[...] * a_ref[...]
            w = wc_ref[...]
            y = w[0:1] * _shift_down(z, 2) + w[1:2] * _shift_down(z, 1) + w[2:3] * z
            o_ref[...] = (b_ref[...] * y).astype(o_ref.dtype)

        @pl.when((j >= 3) & (j < 5))
        def _pool():
            d, _ = _pool_mean_minus_token(a_ref[...], j == 3)
            y = jnp.dot(d.astype(BF16), wp_ref[...].astype(BF16), preferred_element_type=F32)
            o_ref[...] = (y * ps_ref[...]).astype(o_ref.dtype)

        @pl.when(j >= 5)
        def _sgu():
            lo = _lo_mask((CHUNK, LANES))
            wm = jnp.where(_tril_keep(), wsp_ref[...], 0.0).astype(BF16)
            bias_t = bias_ref[...]
            g = lng_ref[...]

            def chunk(n, carry):
                rows = pl.ds(pl.multiple_of(n * CHUNK, CHUNK), CHUNK)
                ug, _, _, _, mixed = _sgu_chunk_fwd(a_ref[rows, :], b_ref[rows, :], g, wm, bias_t, lo)
                o_ref[rows, :] = (ug * mixed).astype(o_ref.dtype)
                return carry

            lax.fori_loop(0, nchunk, chunk, 0, unroll=SGU_UNROLL)

    def col(f):
        return lambda j: (0, f(j))

    clip = lambda v, lo, hi: jnp.minimum(jnp.maximum(v, lo), hi)
    return pl.pallas_call(
        body,
        name="mixer_fwd",
        grid=(8,),
        in_specs=[
            pl.BlockSpec((T, LANES), col(lambda j: jnp.where(j < 3, j, jnp.where(j < 5, j + 6, j + 6)))),
            pl.BlockSpec((T, LANES), col(lambda j: jnp.where(j < 3, j + 3, jnp.where(j < 5, 5, j + 9)))),
            pl.BlockSpec((T, LANES), col(lambda j: jnp.where(j < 3, j + 6, 8))),
            pl.BlockSpec((3, LANES), col(lambda j: clip(j, 0, 2))),
            pl.BlockSpec((None, LANES, LANES), lambda j: (clip(j - 3, 0, 1), 0, 0)),
            pl.BlockSpec((1, LANES), col(lambda j: clip(j - 3, 0, 1))),
            pl.BlockSpec((1, LANES), col(lambda j: clip(j - 5, 0, 2))),
            pl.BlockSpec((None, 2 * CHUNK, CHUNK), lambda j: (clip(j - 5, 0, 2), 0, 0)),
            pl.BlockSpec((None, CHUNK, LANES), lambda j: (clip(j - 5, 0, 2), 0, 0)),
        ],
        out_specs=pl.BlockSpec((T, LANES), lambda j: (0, j)),
        out_shape=jax.ShapeDtypeStruct((T, D_MODEL), BF16),
        compiler_params=_cparams(("arbitrary",)),
    )(proj, proj, proj, wconv, wpool_bd, pscale, lng, wsp, bias)


def _mixer_bwd(proj, dmix, wconv, wpool_bd, pscale, lng, wsp, bias):
    T = proj.shape[0]
    nchunk = T // CHUNK

    def body(a_ref, b_ref, c_ref, dm_ref, wc_ref, wp_ref, ps_ref, lng_ref, wsp_ref, bias_ref,
             o_ref, dwc_ref, dwp_ref, dps_ref, dlng_ref, dwsp_ref, dbias_ref, keep1, keep2):
        k = pl.program_id(0)

        @pl.when(k < 3)
        def _conv():
            xa, gb, gc, dya = a_ref[...], b_ref[...], c_ref[...], dm_ref[...]
            w = wc_ref[...]
            z = gc * xa
            z1 = _shift_down(z, 1)
            z2 = _shift_down(z, 2)
            y = w[0:1] * z2 + w[1:2] * z1 + w[2:3] * z
            dyv = dya * gb
            dz = w[2:3] * dyv + w[1:2] * _shift_up(dyv, 1) + w[0:1] * _shift_up(dyv, 2)
            dwc_ref[0:1, :] = jnp.sum(dyv * z2, axis=0, keepdims=True)
            dwc_ref[1:2, :] = jnp.sum(dyv * z1, axis=0, keepdims=True)
            dwc_ref[2:3, :] = jnp.sum(dyv * z, axis=0, keepdims=True)
            o_ref[...] = (dz * gc).astype(o_ref.dtype)
            keep1[k] = (dya * y).astype(keep1.dtype)
            keep1[k + 3] = (dz * xa).astype(keep1.dtype)

        @pl.when((k >= 3) & (k < 9))
        def _emit_gb_gc():
            o_ref[...] = keep1[k - 3]

        @pl.when((k >= 9) & (k < 11))
        def _pool():
            first = k == 9
            p, dyb = a_ref[...], dm_ref[...]
            d, count = _pool_mean_minus_token(p, first)
            w2 = wp_ref[...].astype(BF16)
            db = d.astype(BF16)
            y = jnp.dot(db, w2, preferred_element_type=F32)
            dps_ref[...] = jnp.sum(dyb * y, axis=0, keepdims=True)
            dyv = (dyb * ps_ref[...]).astype(BF16)
            dd = lax.dot_general(dyv, w2, _DN["nt"], preferred_element_type=F32)
            dwp_ref[...] = lax.dot_general(db, dyv, _DN["tn"], preferred_element_type=F32)
            dwin = dd / count
            a2 = dwin + _shift_up(dwin, 1)
            a4 = a2 + _shift_up(a2, 2)
            a8 = a4 + _shift_up(a4, 4)
            a16 = a8 + _shift_up(a8, 8)
            _, lo = _pool_windows(first)
            back = jnp.where(first, jnp.where(lo, a2, a4), jnp.where(lo, a8, a16))
            o_ref[...] = (back - dd).astype(o_ref.dtype)

        @pl.when((k >= 11) & (k < 14))
        def _sgu():
            lo = _lo_mask((CHUNK, LANES))
            keep = _tril_keep()
            wm = jnp.where(keep, wsp_ref[...], 0.0).astype(BF16)
            bias_t = bias_ref[...]
            g = lng_ref[...]
            dwsp_ref[...] = jnp.zeros_like(dwsp_ref)
            dbias_ref[...] = jnp.zeros_like(dbias_ref)
            dlng_ref[...] = jnp.zeros_like(dlng_ref)

            def chunk(n, carry):
                rows = pl.ds(pl.multiple_of(n * CHUNK, CHUNK), CHUNK)
                u, v, dyc = a_ref[rows, :], b_ref[rows, :], dm_ref[rows, :]
                ug, vn, rstd, vh, mixed = _sgu_chunk_fwd(u, v, g, wm, bias_t, lo)
                dmx = dyc * ug
                o_ref[rows, :] = (dyc * mixed * _gelu_grad(u)).astype(o_ref.dtype)
                dbias_ref[...] += dmx
                dst = jnp.concatenate([jnp.where(lo, dmx, 0.0), jnp.where(lo, 0.0, dmx)], axis=0).astype(BF16)
                dwsp_ref[...] += lax.dot_general(dst, vh, _DN["nt"], preferred_element_type=F32)
                dvh = lax.dot_general(wm, dst, _DN["tn"], preferred_element_type=F32)
                dlng_ref[...] += jnp.sum(dvh * vn, axis=0, keepdims=True)
                dvn = dvh * g
                m1 = _seg_mean(dvn, lo)
                m2 = _seg_mean(dvn * vn, lo)
                dvg = rstd * (dvn - m1 - vn * m2)
                keep2[k - 11, rows, :] = (dvg * _gelu_grad(v)).astype(keep2.dtype)
                return carry

            lax.fori_loop(0, nchunk, chunk, 0, unroll=SGU_UNROLL)
            dwsp_ref[...] = jnp.where(keep, dwsp_ref[...], 0.0)
            dbt = dbias_ref[...]
            lane = lax.broadcasted_iota(jnp.int32, (CHUNK, LANES), 1)
            sa = jnp.sum(jnp.where(lo, dbt, 0.0), axis=-1, keepdims=True)
            sb = jnp.sum(jnp.where(lo, 0.0, dbt), axis=-1, keepdims=True)
            dbias_ref[...] = jnp.where(lane == 0, sa, jnp.where(lane == 1, sb, 0.0))

        @pl.when(k >= 14)
        def _emit_v():
            o_ref[...] = keep2[k - 14]

    def col(f):
        return lambda k: (0, f(k))

    clip = lambda v, lo, hi: jnp.minimum(jnp.maximum(v, lo), hi)
    view_a = lambda k: jnp.where(k < 3, k, jnp.where(k < 9, 2, jnp.where(k < 14, k, 13)))
    view_b = lambda k: jnp.where(k < 3, k + 3, jnp.where(k < 11, 5, jnp.where(k < 14, k + 3, 16)))
    view_c = lambda k: jnp.where(k < 3, k + 6, 8)
    view_dm = lambda k: jnp.where(k < 3, k, jnp.where(k < 9, 2, jnp.where(k < 14, k - 6, 7)))
    return pl.pallas_call(
        body,
        name="mixer_bwd",
        grid=(17,),
        in_specs=[
            pl.BlockSpec((T, LANES), col(view_a)),
            pl.BlockSpec((T, LANES), col(view_b)),
            pl.BlockSpec((T, LANES), col(view_c)),
            pl.BlockSpec((T, LANES), col(view_dm)),
            pl.BlockSpec((3, LANES), col(lambda k: clip(k, 0, 2))),
            pl.BlockSpec((None, LANES, LANES), lambda k: (clip(k - 9, 0, 1), 0, 0)),
            pl.BlockSpec((1, LANES), col(lambda k: clip(k - 9, 0, 1))),
            pl.BlockSpec((1, LANES), col(lambda k: clip(k - 11, 0, 2))),
            pl.BlockSpec((None, 2 * CHUNK, CHUNK), lambda k: (clip(k - 11, 0, 2), 0, 0)),
            pl.BlockSpec((None, CHUNK, LANES), lambda k: (clip(k - 11, 0, 2), 0, 0)),
        ],
        out_specs=[
            pl.BlockSpec((T, LANES), lambda k: (0, k)),
            pl.BlockSpec((3, LANES), col(lambda k: clip(k, 0, 2))),
            pl.BlockSpec((None, LANES, LANES), lambda k: (clip(k - 9, 0, 1), 0, 0)),
            pl.BlockSpec((1, LANES), col(lambda k: clip(k - 9, 0, 1))),
            pl.BlockSpec((1, LANES), col(lambda k: clip(k - 11, 0, 2))),
            pl.BlockSpec((None, 2 * CHUNK, CHUNK), lambda k: (clip(k - 11, 0, 2), 0, 0)),
            pl.BlockSpec((None, CHUNK, LANES), lambda k: (clip(k - 11, 0, 2), 0, 0)),
        ],
        out_shape=[
            jax.ShapeDtypeStruct((T, IN_W), BF16),
            jax.ShapeDtypeStruct((3, CONV_W), F32),
            jax.ShapeDtypeStruct((2, LANES, LANES), F32),
            jax.ShapeDtypeStruct((1, POOL_W), F32),
            jax.ShapeDtypeStruct((1, SGU_W), F32),
            jax.ShapeDtypeStruct((3, 2 * CHUNK, CHUNK), F32),
            jax.ShapeDtypeStruct((3, CHUNK, LANES), F32),
        ],
        scratch_shapes=[pltpu.VMEM((6, T, LANES), BF16), pltpu.VMEM((3, T, LANES), BF16)],
        compiler_params=_cparams(("arbitrary",)),
    )(proj, proj, proj, dmix, wconv, wpool_bd, pscale, lng, wsp, bias)


def _ln_fwd(prev, pg, pb, mmout, g, b, tm=256):
    T = prev.shape[0]

    def body(prev_ref, pg_ref, pb_ref, mm_ref, g_ref, b_ref, xhat_ref, rstd_ref, y_ref):
        r = ALPHA * (prev_ref[...] * pg_ref[...] + pb_ref[...]) + mm_ref[...]
        mu = jnp.mean(r, axis=-1, keepdims=True)
        xc = r - mu
        var = jnp.mean(xc * xc, axis=-1, keepdims=True)
        rstd = lax.rsqrt(var + LN_EPS)
        xhat = xc * rstd
        xhat_ref[...] = xhat
        rstd_ref[...] = rstd
        y_ref[...] = (xhat * g_ref[...] + b_ref[...]).astype(y_ref.dtype)

    row = pl.BlockSpec((tm, D_MODEL), lambda i: (i, 0))
    vec = pl.BlockSpec((1, D_MODEL), lambda i: (0, 0))
    return pl.pallas_call(
        body,
        name="ln_fwd",
        grid=(T // tm,),
        in_specs=[row, vec, vec, row, vec, vec],
        out_specs=[row, pl.BlockSpec((tm, 1), lambda i: (i, 0)), row],
        out_shape=[jax.ShapeDtypeStruct((T, D_MODEL), F32), jax.ShapeDtypeStruct((T, 1), F32),
                   jax.ShapeDtypeStruct((T, D_MODEL), BF16)],
        compiler_params=_cparams(("parallel",)),
    )(prev, pg, pb, mmout, g, b)


def _ln_bwd(dres, dmm, xhat, rstd, g, tm=256, deps=()):
    T = xhat.shape[0]
    has_res = dres is not None
    nd = len(deps)

    def body(*refs):
        refs = refs[:len(refs) - 4 - nd] + refs[len(refs) - 4:]
        if has_res:
            dres_ref, dmm_ref, xhat_ref, rstd_ref, g_ref, dr_ref, drb_ref, dg_ref, db_ref = refs
            dy = ALPHA * dres_ref[...] + dmm_ref[...]
        else:
            dmm_ref, xhat_ref, rstd_ref, g_ref, dr_ref, drb_ref, dg_ref, db_ref = refs
            dy = dmm_ref[...]
        xhat_v = xhat_ref[...]

        @pl.when(pl.program_id(0) == 0)
        def _():
            dg_ref[...] = jnp.zeros_like(dg_ref)
            db_ref[...] = jnp.zeros_like(db_ref)

        dg_ref[...] += jnp.sum(dy * xhat_v, axis=0, keepdims=True)
        db_ref[...] += jnp.sum(dy, axis=0, keepdims=True)
        dxh = dy * g_ref[...]
        m1 = jnp.mean(dxh, axis=-1, keepdims=True)
        m2 = jnp.mean(dxh * xhat_v, axis=-1, keepdims=True)
        dr = rstd_ref[...] * (dxh - m1 - xhat_v * m2)
        dr_ref[...] = dr
        drb_ref[...] = dr.astype(drb_ref.dtype)

    row = pl.BlockSpec((tm, D_MODEL), lambda i: (i, 0))
    vec = pl.BlockSpec((1, D_MODEL), lambda i: (0, 0))
    in_specs = ([row] if has_res else []) + [row, row, pl.BlockSpec((tm, 1), lambda i: (i, 0)), vec]
    in_specs += [pl.BlockSpec(memory_space=pl.ANY)] * nd
    args = ([dres] if has_res else []) + [dmm, xhat, rstd, g] + list(deps)
    return pl.pallas_call(
        body,
        name="ln_bwd_res" if has_res else "ln_bwd",
        grid=(T // tm,),
        in_specs=in_specs,
        out_specs=[row, row, vec, vec],
        out_shape=[jax.ShapeDtypeStruct((T, D_MODEL), F32), jax.ShapeDtypeStruct((T, D_MODEL), BF16),
                   jax.ShapeDtypeStruct((1, D_MODEL), F32), jax.ShapeDtypeStruct((1, D_MODEL), F32)],
        compiler_params=_cparams(("arbitrary",)),
    )(*args)


def _loss_head(xhat, g, b, target, tm=256):
    T = xhat.shape[0]

    def body(xhat_ref, g_ref, b_ref, t_ref, loss_ref, dy_ref):
        err = xhat_ref[...] * g_ref[...] + b_ref[...] - t_ref[...]

        @pl.when(pl.program_id(0) == 0)
        def _():
            loss_ref[...] = jnp.zeros_like(loss_ref)

        part = jnp.sum(jnp.sum(err * err, axis=-1, keepdims=True), axis=0, keepdims=True)
        loss_ref[...] += jnp.broadcast_to(part * (0.5 / D_MODEL), loss_ref.shape)
        dy_ref[...] = err * (1.0 / D_MODEL)

    row = pl.BlockSpec((tm, D_MODEL), lambda i: (i, 0))
    vec = pl.BlockSpec((1, D_MODEL), lambda i: (0, 0))
    return pl.pallas_call(
        body,
        name="loss_head",
        grid=(T // tm,),
        in_specs=[row, vec, vec, row],
        out_specs=[pl.BlockSpec((8, LANES), lambda i: (0, 0)), row],
        out_shape=[jax.ShapeDtypeStruct((8, LANES), F32), jax.ShapeDtypeStruct((T, D_MODEL), F32)],
        compiler_params=_cparams(("arbitrary",)),
    )(xhat, g, b, target)


def _residual_out(dres, dmm, tm=256):
    T = dres.shape[0]

    def body(a_ref, b_ref, o_ref):
        o_ref[...] = ALPHA * a_ref[...] + b_ref[...]

    row = pl.BlockSpec((tm, D_MODEL), lambda i: (i, 0))
    return pl.pallas_call(
        body, name="residual_out", grid=(T // tm,), in_specs=[row, row], out_specs=row,
        out_shape=jax.ShapeDtypeStruct((T, D_MODEL), F32), compiler_params=_cparams(("parallel",)),
    )(dres, dmm)


SW_TC = 1408


def _swiglu_fwd(gu, tm=128):
    T = gu.shape[0]

    def body(gu_ref, o_ref):
        gv = gu_ref[:, :D_FF]
        o_ref[...] = (gv * jax.nn.sigmoid(gv) * gu_ref[:, D_FF:]).astype(o_ref.dtype)

    return pl.pallas_call(
        body, name="swiglu_fwd", grid=(T // tm,),
        in_specs=[pl.BlockSpec((tm, 2 * D_FF), lambda i: (i, 0))],
        out_specs=pl.BlockSpec((tm, D_FF), lambda i: (i, 0)),
        out_shape=jax.ShapeDtypeStruct((T, D_FF), BF16), compiler_params=_cparams(("parallel",)),
    )(gu)


def _swiglu_bwd(gu, dact, tm=128):
    T = gu.shape[0]

    def body(gu_ref, da_ref, dgu_ref, act_ref):
        gv, uv, da = gu_ref[:, :D_FF], gu_ref[:, D_FF:], da_ref[...]
        s = jax.nn.sigmoid(gv)
        sg = gv * s
        act_ref[...] = (sg * uv).astype(act_ref.dtype)
        dgu_ref[:, D_FF:] = (da * sg).astype(dgu_ref.dtype)
        dgu_ref[:, :D_FF] = (da * uv * (s * (1.0 + gv * (1.0 - s)))).astype(dgu_ref.dtype)

    wide = pl.BlockSpec((tm, 2 * D_FF), lambda i: (i, 0))
    half = pl.BlockSpec((tm, D_FF), lambda i: (i, 0))
    return pl.pallas_call(
        body, name="swiglu_bwd", grid=(T // tm,),
        in_specs=[wide, half], out_specs=[wide, half],
        out_shape=[jax.ShapeDtypeStruct((T, 2 * D_FF), BF16), jax.ShapeDtypeStruct((T, D_FF), BF16)],
        compiler_params=_cparams(("parallel",)),
    )(gu, dact)


def _adamw(w, g, m, v, tr):
    R, C = w.shape[-2:]
    assert R % tr == 0
    c1 = 1.0 - ADAM_B1 ** ADAM_STEP
    c2 = 1.0 - ADAM_B2 ** ADAM_STEP

    def body(w_ref, g_ref, m_ref, v_ref, d_ref, mo_ref, vo_ref):
        gv = g_ref[...]
        mn = ADAM_B1 * m_ref[...] + (1.0 - ADAM_B1) * gv
        vn = ADAM_B2 * v_ref[...] + (1.0 - ADAM_B2) * (gv * gv)
        d_ref[...] = -ADAM_LR * ((mn / c1) / (jnp.sqrt(vn / c2) + ADAM_EPS) + ADAM_WD * w_ref[...])
        mo_ref[...] = mn
        vo_ref[...] = vn

    if w.ndim == 2:
        grid, blk = (R // tr,), pl.BlockSpec((tr, C), lambda i: (i, 0))
    else:
        grid, blk = (w.shape[0], R // tr), pl.BlockSpec((None, tr, C), lambda l, i: (l, i, 0))
    return pl.pallas_call(
        body, name="adamw", grid=grid, in_specs=[blk] * 4, out_specs=[blk] * 3,
        out_shape=[jax.ShapeDtypeStruct(w.shape, F32)] * 3, compiler_params=_cparams(("parallel",) * len(grid)),
    )(w, g, m, v)


def _my_place():
    return lax.axis_index("x"), lax.axis_index("y"), lax.axis_index("c")


ANY = pl.BlockSpec(memory_space=pl.ANY)
HBM = pl.BlockSpec(memory_space=pltpu.HBM)
SEM = pl.BlockSpec(memory_space=pltpu.SEMAPHORE)
EFFECT = pltpu.SideEffectType.DATAFLOW_SIDE_EFFECTING


def _in_hbm(a):
    return pltpu.with_memory_space_constraint(a, pltpu.HBM)


def _block_rows(ref, dev):
    r = ref.shape[0] // N_DEV
    start = pl.multiple_of((4 * dev[0] + 2 * dev[1] + dev[2]) * r, 16)
    return ref.at[pl.ds(start, r), :]


def _ag_first_copies(s_refs, land_refs, send_sems, recv_sems, receiving):
    x, y, c = _my_place()
    peers = [(x, y, 1 - c)] + [(*chip, c) for chip in _other_chips(x, y)]
    copies = []
    for k, peer in enumerate(peers):
        block = peer if receiving else (x, y, c)
        copies += [pltpu.make_async_remote_copy(
            src_ref=s_refs[w], dst_ref=_block_rows(land_refs[w], block),
            send_sem=send_sems.at[k * len(s_refs) + w], recv_sem=recv_sems.at[k * len(s_refs) + w],
            device_id=peer, device_id_type=MESH)
            for w in range(len(s_refs))]
    return copies


def _ag_start(shards, layer, after=()):
    nw = len(shards)

    def body(*refs):
        s_refs, land_refs = refs[:nw], refs[nw:2 * nw]
        token = refs[-1]
        sems = 2 * nw + len(after)
        for cp in _ag_first_copies(s_refs, land_refs, refs[sems], refs[sems + 1], False):
            cp.start()
        token[...] = jnp.zeros_like(token)

    lands = [lax.empty((N_DEV * s.shape[0], D_MODEL), BF16) for s in shards]
    out = pl.pallas_call(
        body, name="ag_start_%s" % layer,
        in_specs=[HBM] * (2 * nw) + [ANY] * len(after),
        out_specs=(SEM, SEM, *[HBM] * (2 * nw), pl.BlockSpec(memory_space=pltpu.VMEM)),
        out_shape=(pltpu.SemaphoreType.DMA((4 * nw,)), pltpu.SemaphoreType.DMA((4 * nw,)),
                   *[pltpu.HBM(a.shape, a.dtype) for a in list(shards) + lands],
                   jax.ShapeDtypeStruct((8, LANES), F32)),
        input_output_aliases={i: 2 + i for i in range(2 * nw)},
        compiler_params=pltpu.CompilerParams(has_side_effects=EFFECT),
    )(*[_in_hbm(a) for a in list(shards) + lands], *after)
    return out[0], out[1], out[2:2 + nw], out[2 + nw:2 + 2 * nw], out[-1]


def _ag_wait(send_sems, recv_sems, shards, lands, after, layer):
    nw = len(shards)

    def body(*refs):
        s_refs, land_refs = refs[:nw], refs[nw:2 * nw]
        for cp in _ag_first_copies(s_refs, land_refs, refs[2 * nw], refs[2 * nw + 1], True):
            cp.wait_send()
            cp.wait_recv()

    out = pl.pallas_call(
        body, name="ag_wait_%s" % layer,
        in_specs=[HBM] * (2 * nw) + [SEM, SEM] + [ANY] * len(after),
        out_specs=[HBM] * (2 * nw),
        out_shape=[pltpu.HBM(a.shape, a.dtype) for a in list(shards) + list(lands)],
        input_output_aliases={i: i for i in range(2 * nw)},
        compiler_params=pltpu.CompilerParams(has_side_effects=EFFECT),
    )(*shards, *lands, send_sems, recv_sems, *after)
    return out[:nw], out[nw:]


def _ag_pass_on(shards, lands):
    nw = len(shards)

    def body(*refs):
        s_refs, g_refs = refs[:nw], refs[2 * nw:3 * nw]
        send_sems, recv_sems, local_sems = refs[3 * nw:3 * nw + 3]
        stage = refs[3 * nw + 3:]
        x, y, c = _my_place()
        load = [pltpu.make_async_copy(s_refs[w], stage[w], local_sems.at[w]) for w in range(nw)]
        mine = [pltpu.make_async_copy(stage[w], _block_rows(g_refs[w], (x, y, c)), local_sems.at[w])
                for w in range(nw)]
        for cp in load:
            cp.start()
        sends, arrivals = [], []
        for j, chip in enumerate(_other_chips(x, y)):
            for w in range(nw):
                rows_out = _block_rows(g_refs[w], (*chip, c))
                rows_in = _block_rows(g_refs[w], (*chip, 1 - c))
                sends.append(pltpu.make_async_remote_copy(
                    src_ref=rows_out, dst_ref=rows_out, send_sem=send_sems.at[j, w], recv_sem=recv_sems.at[j, w],
                    device_id=(x, y, 1 - c), device_id_type=MESH))
                arrivals.append(pltpu.make_async_remote_copy(
                    src_ref=rows_in, dst_ref=rows_in, send_sem=send_sems.at[j, w], recv_sem=recv_sems.at[j, w],
                    device_id=(x, y, 1 - c), device_id_type=MESH))
        for cp in sends:
            cp.start()
        for w in range(nw):
            load[w].wait()
            mine[w].start()
        for cp in arrivals:
            cp.wait_recv()
        for cp in sends:
            cp.wait_send()
        for cp in mine:
            cp.wait()

    return pl.pallas_call(
        body, name="ag_pass_on",
        in_specs=[ANY] * (2 * nw), out_specs=[ANY] * nw,
        out_shape=[jax.ShapeDtypeStruct(a.shape, a.dtype) for a in lands],
        input_output_aliases={nw + i: i for i in range(nw)},
        scratch_shapes=[pltpu.SemaphoreType.DMA((3, nw)), pltpu.SemaphoreType.DMA((3, nw)),
                        pltpu.SemaphoreType.DMA((nw,))] + [pltpu.VMEM(s.shape, s.dtype) for s in shards],
        compiler_params=_cparams(),
    )(*shards, *lands)


def _rs_sibling_copies(p_refs, land_refs, send_sems, recv_sems):
    x, y, c = _my_place()
    return [pltpu.make_async_remote_copy(
        src_ref=p_refs[w].at[:, 1 - c], dst_ref=land_refs[w],
        send_sem=send_sems.at[w], recv_sem=recv_sems.at[w], device_id=(x, y, 1 - c), device_id_type=MESH)
        for w in range(len(p_refs))]


def _rs_sibling_start(parts, tag, after=()):
    nw = len(parts)
    sems = 2 * nw + len(after)

    def body(*refs):
        for cp in _rs_sibling_copies(refs[:nw], refs[nw:2 * nw], refs[sems], refs[sems + 1]):
            cp.start()
        refs[-1][...] = jnp.zeros_like(refs[-1])

    lands = [lax.empty(p.shape[:1] + p.shape[2:], BF16) for p in parts]
    out = pl.pallas_call(
        body, name="rs_sibling_start_%s" % tag,
        in_specs=[HBM] * (2 * nw) + [ANY] * len(after),
        out_specs=(SEM, SEM, *[HBM] * (2 * nw), pl.BlockSpec(memory_space=pltpu.VMEM)),
        out_shape=(pltpu.SemaphoreType.DMA((nw,)), pltpu.SemaphoreType.DMA((nw,)),
                   *[pltpu.HBM(a.shape, a.dtype) for a in list(parts) + lands],
                   jax.ShapeDtypeStruct((8, LANES), F32)),
        input_output_aliases={i: 2 + i for i in range(2 * nw)},
        compiler_params=pltpu.CompilerParams(has_side_effects=EFFECT),
    )(*[_in_hbm(a) for a in list(parts) + lands], *after)
    return out[0], out[1], out[2:2 + nw], out[2 + nw:2 + 2 * nw], out[-1]


def _rs_sibling_wait(send_sems, recv_sems, parts, lands, after, tag):
    nw = len(parts)

    def body(*refs):
        for cp in _rs_sibling_copies(refs[:nw], refs[nw:2 * nw], refs[2 * nw], refs[2 * nw + 1]):
            cp.wait_send()
            cp.wait_recv()

    out = pl.pallas_call(
        body, name="rs_sibling_wait_%s" % tag,
        in_specs=[HBM] * (2 * nw) + [SEM, SEM] + [ANY] * len(after),
        out_specs=[HBM] * (2 * nw),
        out_shape=[pltpu.HBM(a.shape, a.dtype) for a in list(parts) + list(lands)],
        input_output_aliases={i: i for i in range(2 * nw)},
        compiler_params=pltpu.CompilerParams(has_side_effects=EFFECT),
    )(*parts, *lands, send_sems, recv_sems, *after)
    return out[:nw], out[nw:]


def _rs_chip_sum(parts, gots, c):
    n = len(parts)

    def body(c_ref, *refs):
        for p_ref, g_ref, o_ref in zip(refs[:n], refs[n:2 * n], refs[2 * n:]):
            o_ref[...] = (p_ref[...].astype(F32) + g_ref[...].astype(F32)).astype(o_ref.dtype)

    mine = [pl.BlockSpec((None, None, p.shape[2], D_MODEL), lambda q, c_ref: (q, c_ref[0], 0, 0)) for p in parts]
    theirs = [pl.BlockSpec((None, g.shape[1], D_MODEL), lambda q, c_ref: (q, 0, 0)) for g in gots]
    return pl.pallas_call(
        body, name="rs_chip_sum",
        grid_spec=pltpu.PrefetchScalarGridSpec(
            num_scalar_prefetch=1, grid=(4,), in_specs=mine + theirs, out_specs=theirs),
        out_shape=[jax.ShapeDtypeStruct(g.shape, BF16) for g in gots],
        compiler_params=_cparams(("parallel",)),
    )(c, *parts, *gots)


def _other_chips(x, y):
    return [(1 - x, y), (x, 1 - y), (1 - x, 1 - y)]


def _rs_chip_copies(s_refs, land_refs, send_sems, recv_sems):
    x, y, c = _my_place()
    copies = []
    for k, chip in enumerate(_other_chips(x, y)):
        q = 2 * chip[0] + chip[1]
        copies += [pltpu.make_async_remote_copy(
            src_ref=s_refs[w].at[q], dst_ref=land_refs[w].at[k],
            send_sem=send_sems.at[k * len(s_refs) + w], recv_sem=recv_sems.at[k * len(s_refs) + w],
            device_id=(*chip, c), device_id_type=MESH)
            for w in range(len(s_refs))]
    return copies


def _rs_chip_start(sums, layer):
    nw = len(sums)

    def body(*refs):
        s_refs, land_refs = refs[:nw], refs[nw:2 * nw]
        send_sems, recv_sems = refs[2 * nw], refs[2 * nw + 1]
        token = refs[-1]
        for cp in _rs_chip_copies(s_refs, land_refs, send_sems, recv_sems):
            cp.start()
        token[...] = jnp.zeros_like(token)

    lands = [lax.empty((3,) + s.shape[1:], BF16) for s in sums]
    out = pl.pallas_call(
        body, name="rs_chip_start_%s" % layer,
        in_specs=[HBM] * (2 * nw),
        out_specs=(SEM, SEM, *[HBM] * (2 * nw), pl.BlockSpec(memory_space=pltpu.VMEM)),
        out_shape=(pltpu.SemaphoreType.DMA((3 * nw,)), pltpu.SemaphoreType.DMA((3 * nw,)),
                   *[pltpu.HBM(a.shape, a.dtype) for a in list(sums) + lands],
                   jax.ShapeDtypeStruct((8, LANES), F32)),
        input_output_aliases={i: 2 + i for i in range(2 * nw)},
        compiler_params=pltpu.CompilerParams(has_side_effects=EFFECT),
    )(*[_in_hbm(a) for a in list(sums) + lands])
    return out[0], out[1], out[2:2 + nw], out[2 + nw:2 + 2 * nw], out[-1]


def _rs_chip_wait(send_sems, recv_sems, sums, lands, after, layer):
    nw = len(sums)

    def body(*refs):
        s_refs, land_refs = refs[:nw], refs[nw:2 * nw]
        for cp in _rs_chip_copies(s_refs, land_refs, refs[2 * nw], refs[2 * nw + 1]):
            cp.wait_send()
            cp.wait_recv()

    out = pl.pallas_call(
        body, name="rs_chip_wait_%s" % layer,
        in_specs=[HBM] * (2 * nw) + [SEM, SEM] + [ANY] * len(after),
        out_specs=[HBM] * (2 * nw),
        out_shape=[pltpu.HBM(a.shape, a.dtype) for a in list(sums) + list(lands)],
        input_output_aliases={i: i for i in range(2 * nw)},
        compiler_params=pltpu.CompilerParams(has_side_effects=EFFECT),
    )(*sums, *lands, send_sems, recv_sems, *after)
    return out[:nw], out[nw:]


def _rs_finish(sums, gots, q, layer, into):
    n = len(sums)

    def body(q_ref, *refs):
        for s_ref, g_ref, o_ref in zip(refs[:n], refs[n:2 * n], refs[len(refs) - n:]):
            o_ref[...] = ((s_ref[...].astype(F32) + g_ref[0].astype(F32)) + g_ref[1].astype(F32)) + g_ref[2].astype(F32)

    rows = [s.shape[1] for s in sums]
    in_specs = [pl.BlockSpec((None, r, D_MODEL), lambda i, q_ref: (q_ref[0], 0, 0)) for r in rows]
    in_specs += [pl.BlockSpec((3, r, D_MODEL), lambda i, q_ref: (0, 0, 0)) for r in rows]
    args = [q, *sums, *gots]
    aliases = {}
    if into is not None:
        in_specs += [ANY] * n
        aliases = {len(args) + i: i for i in range(n)}
        args += list(into)
    return pl.pallas_call(
        body, name="rs_finish",
        grid_spec=pltpu.PrefetchScalarGridSpec(
            num_scalar_prefetch=1, grid=(1,), in_specs=in_specs,
            out_specs=[pl.BlockSpec((None, r, D_MODEL), lambda i, q_ref: (layer, 0, 0)) for r in rows]),
        out_shape=[jax.ShapeDtypeStruct((DEPTH, r, D_MODEL), F32) for r in rows],
        input_output_aliases=aliases,
        compiler_params=_cparams(("arbitrary",)),
    )(*args)


def _allreduce_small(vec):
    R = vec.shape[0]
    assert R % (8 * N_DEV) == 0
    P = R // N_DEV

    def body(v_ref, o_ref, buf, send1, recv1, send2, recv2):
        x, y, c = _my_place()
        me = 4 * x + 2 * y + c

        def piece(ref, d):
            return ref.at[pl.ds(pl.multiple_of(d * P, 8), P), :]

        def peer(k):
            p = me ^ k
            return p, (p >> 2, (p >> 1) & 1, p & 1)

        scatter = []
        for k in range(1, N_DEV):
            p, where = peer(k)
            scatter.append(pltpu.make_async_remote_copy(
                src_ref=piece(v_ref, p), dst_ref=buf.at[k], send_sem=send1.at[k - 1], recv_sem=recv1.at[k - 1],
                device_id=where, device_id_type=MESH))
        for cp in scatter:
            cp.start()
        buf[0] = piece(v_ref, me)[...]
        for cp in scatter:
            cp.wait()
        acc = buf[me]
        for d in range(1, N_DEV):
            acc = acc + buf[me ^ d]
        piece(o_ref, me)[...] = acc
        spread, arrivals = [], []
        for k in range(1, N_DEV):
            p, where = peer(k)
            spread.append(pltpu.make_async_remote_copy(
                src_ref=piece(o_ref, me), dst_ref=piece(o_ref, me), send_sem=send2.at[k - 1], recv_sem=recv2.at[k - 1],
                device_id=where, device_id_type=MESH))
            arrivals.append(pltpu.make_async_remote_copy(
                src_ref=piece(o_ref, p), dst_ref=piece(o_ref, p), send_sem=send2.at[k - 1], recv_sem=recv2.at[k - 1],
                device_id=where, device_id_type=MESH))
        for cp in spread:
            cp.start()
        for cp in arrivals:
            cp.wait_recv()
        for cp in spread:
            cp.wait_send()

    sems = pltpu.SemaphoreType.DMA((N_DEV - 1,))
    return pl.pallas_call(
        body, name="allreduce_small",
        in_specs=[pl.BlockSpec(memory_space=pltpu.VMEM)], out_specs=pl.BlockSpec(memory_space=pltpu.VMEM),
        out_shape=jax.ShapeDtypeStruct((R, LANES), F32),
        scratch_shapes=[pltpu.VMEM((N_DEV, P, LANES), F32), sems, sems, sems, sems],
        compiler_params=_cparams(),
    )(vec)


def _pack(arrs):
    flat = jnp.concatenate([a.reshape(-1) for a in arrs])
    pad = (-flat.shape[0]) % (8 * N_DEV * LANES)
    return jnp.pad(flat, (0, pad)).reshape(-1, LANES)


def _unpack(packed, shapes):
    flat = packed.reshape(-1)
    out, off = [], 0
    for s in shapes:
        n = math.prod(s)
        out.append(flat[off:off + n].reshape(s))
        off += n
    return out


def kernel(x, w_in, w_conv, w_pool, pool_scale, sgu_ln_g, w_spatial, b_spatial, w_o, ln1_g, ln1_b, w_gate_up, w_down, ln2_g, ln2_b, loss_target, m_w_in, m_w_conv, m_w_pool, m_pool_scale, m_sgu_ln_g, m_w_spatial, m_b_spatial, m_w_o, m_ln1_g, m_ln1_b, m_w_gate_up, m_w_down, m_ln2_g, m_ln2_b, v_w_in, v_w_conv, v_w_pool, v_pool_scale, v_sgu_ln_g, v_w_spatial, v_b_spatial, v_w_o, v_ln1_g, v_ln1_b, v_w_gate_up, v_w_down, v_ln2_g, v_ln2_b):
    L = DEPTH
    T = x.shape[1]
    mx, my, mc = _my_place()
    dev = 4 * mx + 2 * my + mc
    xs = x[0]
    target = loss_target[0]

    conv_cols = w_conv.shape[2]
    w_conv_z = lax.dynamic_update_slice(jnp.zeros((L, 3, CONV_W), F32), w_conv, (0, 0, dev * conv_cols))
    w_conv_packed = _allreduce_small(_pack([w_conv_z]))
    w_conv_full = _unpack(w_conv_packed, [(L, 3, CONV_W)])[0]

    shards = (jnp.swapaxes(w_in, 1, 2).astype(BF16), jnp.swapaxes(w_gate_up, 1, 2).astype(BF16),
              w_o.astype(BF16), w_down.astype(BF16))
    first_gather = _ag_start_layer(shards, 0, [w_conv_packed])

    loss_tile, grad_x2, big_grads, small_grads = _local_step(
        xs, target, shards, first_gather, w_conv_full, w_pool, pool_scale, sgu_ln_g, w_spatial, b_spatial,
        ln1_g, ln1_b, ln2_g, ln2_b)
    loss = lax.psum(loss_tile[0, 0], ("x", "y", "c"))
    grad_x = grad_x2[None]
    big_w = (w_in, w_gate_up, w_o, w_down)
    big_m = (m_w_in, m_w_gate_up, m_w_o, m_w_down)
    big_v = (v_w_in, v_w_gate_up, v_w_o, v_w_down)
    small_w = [w_conv_full, w_pool, pool_scale, sgu_ln_g, w_spatial, b_spatial, ln1_g, ln1_b, ln2_g, ln2_b]
    small_m = [m_w_conv, m_w_pool, m_pool_scale, m_sgu_ln_g, m_w_spatial, m_b_spatial, m_ln1_g, m_ln1_b, m_ln2_g, m_ln2_b]
    small_v = [v_w_conv, v_w_pool, v_pool_scale, v_sgu_ln_g, v_w_spatial, v_b_spatial, v_ln1_g, v_ln1_b, v_ln2_g, v_ln2_b]
    grads, deltas, new_m, new_v = _reduce_and_update(
        big_grads, small_grads, big_w, big_m, big_v, small_w, small_m, small_v)
    return (loss, grad_x, *grads, *deltas, *new_m, *new_v)


def _ag_start_layer(shards, l, after):
    s_in, s_gu, s_o, s_dn = [s[l] for s in shards]
    first = _ag_start([s_in, s_o], "%da" % l, after=after)
    return first, _ag_start([s_gu, s_dn], "%db" % l, after=[first[4]])


def _ag_finish(gather, after, tag):
    send_sems, recv_sems, shards, lands, _ = gather
    shards, lands = _ag_wait(send_sems, recv_sems, shards, lands, after, tag)
    return _ag_pass_on(shards, lands)


def _rs_begin(parts, tag, after=()):
    return _rs_sibling_start([p.reshape(4, 2, p.shape[0] // N_DEV, D_MODEL) for p in parts], tag, after)


def _rs_continue(sibling_flight, after, c_arr, tag):
    send_sems, recv_sems, parts, lands, _ = sibling_flight
    parts, got = _rs_sibling_wait(send_sems, recv_sems, parts, lands, after, tag)
    return _rs_chip_start(_rs_chip_sum(parts, got, c_arr), tag)


def _local_step(xs, target, shards, gather, w_conv_full, w_pool, pool_scale, sgu_ln_g, w_spatial, b_spatial,
                ln1_g, ln1_b, ln2_g, ln2_b):
    L = DEPTH
    T = xs.shape[0]
    mx, my, mc = _my_place()
    c_arr = jnp.reshape(mc, (1,)).astype(jnp.int32)
    q_arr = jnp.reshape(2 * mx + my, (1,)).astype(jnp.int32)
    eye2 = jnp.eye(2, dtype=F32)
    wp = w_pool.reshape(L, 2, 2, HALF, HALF)
    wpool_bd = jnp.einsum("ltgcd,gh->ltgchd", wp, eye2).reshape(L, 2, LANES, LANES)
    wsp_t = w_spatial.reshape(L, 3, 2 * CHUNK, CHUNK)
    bias_t = jnp.repeat(jnp.swapaxes(b_spatial.reshape(L, 3, 2, CHUNK), 2, 3), HALF, axis=3)
    ones = jnp.ones((1, D_MODEL), F32)
    zeros = jnp.zeros((1, D_MODEL), F32)

    saved = []
    prev, pg, pb = xs, ones, zeros
    prev_b = xs.astype(BF16)
    weights = []
    for l in range(L):
        g_in, g_o = _ag_finish(gather[0], [] if l == 0 else [prev_b], "%da" % l)
        proj = _mm(prev_b, g_in, "nt", F32, 512, IN_W, D_MODEL, "mm_proj", deps=[gather[1][4]] if l == 0 else [])
        mixcat = _mixer_fwd(proj, w_conv_full[l], wpool_bd[l], pool_scale[l][None], sgu_ln_g[l][None], wsp_t[l], bias_t[l])
        xhat1, rstd1, h_b = _mm_ln_fwd(mixcat, g_o, prev, pg, pb, ln1_g[l][None], ln1_b[l][None], "mm_wo_ln")
        g_gu, g_dn = _ag_finish(gather[1], [h_b], "%db" % l)
        weights.append((g_in, g_gu, g_o, g_dn))
        deps = []
        if l + 1 < L:
            gather = _ag_start_layer(shards, l + 1, [g_gu])
            deps = [gather[1][4]]
        g_act, u_act, act = _mm_swiglu_fwd(h_b, g_gu, deps=deps)
        xhat2, rstd2, y_b = _mm_ln_fwd(act, g_dn, xhat1, ln1_g[l][None], ln1_b[l][None], ln2_g[l][None], ln2_b[l][None],
                                       "mm_down_ln")
        saved.append((prev_b, proj, mixcat, xhat1, rstd1, h_b, g_act, u_act, act, xhat2, rstd2))
        prev, pg, pb, prev_b = xhat2, ln2_g[l][None], ln2_b[l][None], y_b

    loss_tile, dy = _loss_head(prev, pg, pb, target)

    small = [None] * L
    big = None
    sibling_flight = None
    above = None
    for l in reversed(range(L)):
        prev_b, proj, mixcat, xhat1, rstd1, h_b, g_act, u_act, act, xhat2, rstd2 = saved[l]
        g_in, g_gu, g_o, g_dn = weights[l]
        chip_flight = None
        if above is None:
            dr2, dr2_b, dg2, db2 = _ln_bwd(None, dy, xhat2, rstd2, ln2_g[l][None])
        else:
            dr2, dr2_b, dg2, db2 = _mm_ln_bwd([above[0]], above[1], above[2], xhat2, rstd2, ln2_g[l][None],
                                              "mm_dx_ln", deps=[sibling_flight[4]])
            chip_flight = _rs_continue(sibling_flight, [dr2_b], c_arr, str(l + 1))
        dg_b, du_b = _mm_swiglu_bwd(dr2_b, g_dn, g_act, u_act, deps=[chip_flight[4]] if chip_flight else [])
        p_dn = _mm(act, dr2_b, "tn", BF16, DW_TM, D_MODEL, T, "mm_dw_down")
        p_gu = _mm(dg_b, h_b, "tn", BF16, DW_TM, D_MODEL, T, "mm_dw_gate", out_rows=2 * D_FF)
        p_gu = _mm(du_b, h_b, "tn", BF16, DW_TM, D_MODEL, T, "mm_dw_up", out_rows=2 * D_FF, out_off=D_FF, out_into=p_gu)
        ffn_sibling = _rs_begin([p_gu, p_dn], "0b") if l == 0 else None
        dr1, dr1_b, dg1, db1 = _mm_ln_bwd([dg_b, du_b], g_gu, dr2, xhat1, rstd1, ln1_g[l][None], "mm_dh_ln",
                                          deps=[ffn_sibling[4]] if l == 0 else [])
        ffn_flight = _rs_continue(ffn_sibling, [dr1_b], c_arr, "0b") if l == 0 else None
        dmix = _mm(dr1_b, g_o, "nt", F32, T, 512, D_MODEL, "mm_dmix", deps=[ffn_flight[4]] if l == 0 else [])
        p_o = _mm(mixcat, dr1_b, "tn", BF16, 512, D_MODEL, T, "mm_dw_o")
        dproj, dwc, dwp, dps, dlng, dwsp, dbias = _mixer_bwd(
            proj, dmix, w_conv_full[l], wpool_bd[l], pool_scale[l][None], sgu_ln_g[l][None], wsp_t[l], bias_t[l])
        p_in = _mm(dproj, prev_b, "tn", BF16, IN_W, D_MODEL, T, "mm_dw_in")
        small[l] = (dwc, dwp, dps, dlng, dwsp, dbias, dg1, db1, dg2, db2)
        above = (dproj, g_in, dr1)
        if chip_flight is not None:
            big = list(_rs_chip_finish(chip_flight, [p_in], q_arr, str(l + 1), l + 1, big))
        if l > 0:
            sibling_flight = _rs_begin([p_in, p_gu, p_o, p_dn], str(l))
        else:
            big[1], big[3] = _rs_chip_finish(ffn_flight, [p_in], q_arr, "0b", 0, [big[1], big[3]])

    def stack(i):
        return jnp.stack([small[l][i] for l in range(L)])

    dwp_bd = stack(1).reshape(L, 2, 2, HALF, 2, HALF)
    dwp_all = jnp.einsum("ltgchd,gh->ltgcd", dwp_bd, eye2).reshape(L, 4, HALF, HALF)
    dbs_all = jnp.swapaxes(stack(5)[:, :, :, :2], 2, 3).reshape(L, 6, CHUNK)
    small_grads = [stack(0), dwp_all, stack(2).reshape(L, POOL_W), stack(3).reshape(L, SGU_W),
                   stack(4).reshape(L, 6, CHUNK, CHUNK), dbs_all] + [stack(i).reshape(L, D_MODEL) for i in (6, 7, 8, 9)]
    packed_small = _allreduce_small(_pack(small_grads))
    sibling_flight = _rs_begin([p_in, p_o], "0a", after=[packed_small])
    grad_x = _mm_ln_bwd([above[0]], above[1], above[2], None, None, None, "mm_dx_out", deps=[sibling_flight[4]])
    last_flight = _rs_continue(sibling_flight, [grad_x], c_arr, "0a")
    big[0], big[2] = _rs_chip_finish(last_flight, [grad_x], q_arr, "0a", 0, [big[0], big[2]])
    return loss_tile, grad_x, big, (packed_small, [a.shape for a in small_grads])


def _rs_chip_finish(in_flight, after, q, tag, layer, into):
    send_sems, recv_sems, sums, lands, _ = in_flight
    sums, got = _rs_chip_wait(send_sems, recv_sems, sums, lands, after, tag)
    return _rs_finish(sums, got, q, layer, into)


def _reduce_and_update(big_grads, small_grads, big_w, big_m, big_v, small_w, small_m, small_v):
    L = DEPTH
    mx, my, mc = _my_place()
    dev = 4 * mx + 2 * my + mc
    conv_cols = CONV_W // N_DEV
    w_in, w_gate_up, w_o, w_down = big_w
    m_w_in, m_w_gate_up, m_w_o, m_w_down = big_m
    v_w_in, v_w_gate_up, v_w_o, v_w_down = big_v
    packed_g, small_shapes = small_grads
    gt_in, gt_gu, g_w_o, g_w_dn = big_grads
    g_w_in = jnp.swapaxes(gt_in, 1, 2)
    g_w_gu = jnp.swapaxes(gt_gu, 1, 2)

    def widen_conv(a):
        return lax.dynamic_update_slice(jnp.zeros((L, 3, CONV_W), F32), a, (0, 0, dev * conv_cols))

    small_m = [widen_conv(small_m[0])] + list(small_m[1:])
    small_v = [widen_conv(small_v[0])] + list(small_v[1:])
    pk_d, pk_m, pk_v = _adamw(_pack(small_w), packed_g, _pack(small_m), _pack(small_v), packed_g.shape[0] // 2)
    sg = _unpack(packed_g, small_shapes)
    sd = _unpack(pk_d, small_shapes)
    sm = _unpack(pk_m, small_shapes)
    sv = _unpack(pk_v, small_shapes)

    def conv_cols_of(a):
        return lax.dynamic_slice(a, (0, 0, dev * conv_cols), (L, 3, conv_cols))

    for lst in (sg, sd, sm, sv):
        lst[0] = conv_cols_of(lst[0])

    tr = lambda a: jnp.swapaxes(a, 1, 2)
    d_in, m_in, v_in = [tr(a) for a in _adamw(tr(w_in), gt_in, tr(m_w_in), tr(v_w_in), gt_in.shape[1])]
    d_gu, m_gu, v_gu = [tr(a) for a in _adamw(tr(w_gate_up), gt_gu, tr(m_w_gate_up), tr(v_w_gate_up), gt_gu.shape[1] // 2)]
    d_o, m_o, v_o = _adamw(w_o, g_w_o, m_w_o, v_w_o, 128)
    d_dn, m_dn, v_dn = _adamw(w_down, g_w_dn, m_w_down, v_w_down, 352)

    def ordered(big_in, big_o, big_gu, big_dn, sm_list):
        return [big_in, sm_list[0], sm_list[1], sm_list[2], sm_list[3], sm_list[4], sm_list[5], big_o,
                sm_list[6], sm_list[7], big_gu, big_dn, sm_list[8], sm_list[9]]

    grads = ordered(g_w_in, g_w_o, g_w_gu, g_w_dn, sg)
    deltas = ordered(d_in, d_o, d_gu, d_dn, sd)
    new_m = ordered(m_in, m_o, m_gu, m_dn, sm)
    new_v = ordered(v_in, v_o, v_gu, v_dn, sv)
    return grads, deltas, new_m, new_v
```

```python
import functools
import math

import jax
import jax.numpy as jnp
from jax import lax
from jax.experimental import pallas as pl
from jax.experimental.pallas import tpu as pltpu

F32 = jnp.float32
BF16 = jnp.bfloat16
MESH = pl.DeviceIdType.MESH

D_MODEL = 1024
DEPTH = 4
CONV_W = 384
POOL_W = 256
SGU_W = 384
IN_W = 3 * CONV_W + POOL_W + 2 * SGU_W
D_FF = 2816
CHUNK = 128
ALPHA = float((2 * DEPTH) ** 0.25)
LN_EPS = 1e-5
ADAM_LR, ADAM_B1, ADAM_B2, ADAM_EPS, ADAM_WD, ADAM_STEP = 0.001, 0.9, 0.999, 1e-08, 0.01, 10

N_DEV = 8
LANES = 128
HALF = 64
SHARD_ROWS = (IN_W // N_DEV, 2 * D_FF // N_DEV, D_MODEL // N_DEV, D_FF // N_DEV)
VMEM_LIMIT = 52 * 1024 * 1024

INV_SQRT2 = 0.7071067811865476
INV_SQRT_2PI = 0.3989422804014327


def _cparams(sem=None, **kw):
    if sem is not None:
        kw["dimension_semantics"] = sem
    return pltpu.CompilerParams(vmem_limit_bytes=VMEM_LIMIT, **kw)


_DN = {"nn": (((1,), (0,)), ((), ())), "nt": (((1,), (1,)), ((), ())), "tn": (((0,), (0,)), ((), ()))}


def _mm(a, b, mode, out_dtype, tm, tn, tk, name, deps=(), out_rows=None, out_off=0, out_into=None):
    if mode == "nn":
        (M, K), N = a.shape, b.shape[1]
    elif mode == "nt":
        (M, K), N = a.shape, b.shape[0]
    else:
        (K, M), N = a.shape, b.shape[1]
    assert M % tm == 0 and N % tn == 0 and K % tk == 0 and out_off % tm == 0, (M, N, K, tm, tn, tk)
    nk = K // tk
    if out_into is not None:
        deps = tuple(deps) + (out_into,)
    nd = len(deps)
    row_off = out_off // tm

    def body(*refs):
        a_ref, b_ref, o_ref = refs[0], refs[1], refs[2 + nd]
        acc_ref = refs[3 + nd] if nk > 1 else None
        p = lax.dot_general(a_ref[...], b_ref[...], _DN[mode], preferred_element_type=F32)
        if nk == 1:
            o_ref[...] = p.astype(o_ref.dtype)
        else:
            k = pl.program_id(2)

            @pl.when(k == 0)
            def _():
                acc_ref[...] = p

            @pl.when(k > 0)
            def _():
                acc_ref[...] += p

            @pl.when(k == nk - 1)
            def _():
                o_ref[...] = acc_ref[...].astype(o_ref.dtype)

    if mode == "nn":
        a_spec = pl.BlockSpec((tm, tk), lambda i, j, k: (i, k))
        b_blk, b_idx = (tk, tn), (lambda i, j, k: (k, j))
    elif mode == "nt":
        a_spec = pl.BlockSpec((tm, tk), lambda i, j, k: (i, k))
        b_blk, b_idx = (tn, tk), (lambda i, j, k: (j, k))
    else:
        a_spec = pl.BlockSpec((tk, tm), lambda i, j, k: (k, i))
        b_blk, b_idx = (tk, tn), (lambda i, j, k: (k, j))
    return pl.pallas_call(
        body,
        name=name,
        grid=(M // tm, N // tn, nk),
        in_specs=[a_spec, pl.BlockSpec(b_blk, b_idx)] + [pl.BlockSpec(memory_space=pl.ANY)] * nd,
        out_specs=pl.BlockSpec((tm, tn), lambda i, j, k: (i + row_off, j)),
        out_shape=jax.ShapeDtypeStruct((out_rows or M, N), out_dtype),
        scratch_shapes=[pltpu.VMEM((tm, tn), F32)] if nk > 1 else [],
        input_output_aliases={1 + nd: 0} if out_into is not None else {},
        compiler_params=_cparams(("parallel", "parallel", "arbitrary")),
    )(a, b, *deps)


LN_TM = 512


def _mm_ln_fwd(a, b, prev, pg, pb, g, bias, name):
    T, K = a.shape
    tm = LN_TM

    def body(a_ref, b_ref, prev_ref, pg_ref, pb_ref, g_ref, bias_ref, xhat_ref, rstd_ref, y_ref):
        mm = jnp.dot(a_ref[...], b_ref[...], preferred_element_type=F32)
        r = ALPHA * (prev_ref[...] * pg_ref[...] + pb_ref[...]) + mm
        mu = jnp.mean(r, axis=-1, keepdims=True)
        xc = r - mu
        var = jnp.mean(xc * xc, axis=-1, keepdims=True)
        rstd = lax.rsqrt(var + LN_EPS)
        xhat = xc * rstd
        xhat_ref[...] = xhat
        rstd_ref[...] = rstd
        y_ref[...] = (xhat * g_ref[...] + bias_ref[...]).astype(y_ref.dtype)

    row = pl.BlockSpec((tm, D_MODEL), lambda i: (i, 0))
    vec = pl.BlockSpec((1, D_MODEL), lambda i: (0, 0))
    return pl.pallas_call(
        body, name=name, grid=(T // tm,),
        in_specs=[pl.BlockSpec((tm, K), lambda i: (i, 0)),
                  pl.BlockSpec((K, D_MODEL), lambda i: (0, 0), pipeline_mode=pl.Buffered(1)),
                  row, vec, vec, vec, vec],
        out_specs=[row, pl.BlockSpec((tm, 1), lambda i: (i, 0)), row],
        out_shape=[jax.ShapeDtypeStruct((T, D_MODEL), F32), jax.ShapeDtypeStruct((T, 1), F32),
                   jax.ShapeDtypeStruct((T, D_MODEL), BF16)],
        compiler_params=_cparams(("parallel",)),
    )(a, b, prev, pg, pb, g, bias)


def _mm_ln_bwd(a_list, b, dres, xhat, rstd, g, name, deps=()):
    T = a_list[0].shape[0]
    tm = LN_TM
    na, nd = len(a_list), len(deps)
    ks = [a.shape[1] for a in a_list]
    last = xhat is None

    def body(*refs):
        a_refs, b_ref, dres_ref = refs[:na], refs[na], refs[na + 1]
        mm, off = None, 0
        for a_ref, k in zip(a_refs, ks):
            part = jnp.dot(a_ref[...], b_ref[off:off + k, :], preferred_element_type=F32)
            mm = part if mm is None else mm + part
            off += k
        dy = ALPHA * dres_ref[...] + mm
        if last:
            refs[-1][...] = dy
            return
        xhat_ref, rstd_ref, g_ref = refs[na + 2:na + 5]
        dr_ref, drb_ref, dg_ref, db_ref = refs[-4:]
        xhat_v = xhat_ref[...]

        @pl.when(pl.program_id(0) == 0)
        def _():
            dg_ref[...] = jnp.zeros_like(dg_ref)
            db_ref[...] = jnp.zeros_like(db_ref)

        dg_ref[...] += jnp.sum(dy * xhat_v, axis=0, keepdims=True)
        db_ref[...] += jnp.sum(dy, axis=0, keepdims=True)
        dxh = dy * g_ref[...]
        m1 = jnp.mean(dxh, axis=-1, keepdims=True)
        m2 = jnp.mean(dxh * xhat_v, axis=-1, keepdims=True)
        dr = rstd_ref[...] * (dxh - m1 - xhat_v * m2)
        dr_ref[...] = dr
        drb_ref[...] = dr.astype(drb_ref.dtype)

    row = pl.BlockSpec((tm, D_MODEL), lambda i: (i, 0))
    vec = pl.BlockSpec((1, D_MODEL), lambda i: (0, 0))
    in_specs = [pl.BlockSpec((tm, k), lambda i: (i, 0)) for k in ks]
    in_specs += [pl.BlockSpec((sum(ks), D_MODEL), lambda i: (0, 0), pipeline_mode=pl.Buffered(1)), row]
    args = list(a_list) + [b, dres]
    if last:
        out_specs, out_shape = row, jax.ShapeDtypeStruct((T, D_MODEL), F32)
    else:
        in_specs += [row, pl.BlockSpec((tm, 1), lambda i: (i, 0)), vec]
        args += [xhat, rstd, g]
        out_specs = [row, row, vec, vec]
        out_shape = [jax.ShapeDtypeStruct((T, D_MODEL), F32), jax.ShapeDtypeStruct((T, D_MODEL), BF16),
                     jax.ShapeDtypeStruct((1, D_MODEL), F32), jax.ShapeDtypeStruct((1, D_MODEL), F32)]
    return pl.pallas_call(
        body, name=name, grid=(T // tm,),
        in_specs=in_specs + [pl.BlockSpec(memory_space=pl.ANY)] * nd,
        out_specs=out_specs, out_shape=out_shape,
        compiler_params=_cparams(("parallel",) if last else ("arbitrary",)),
    )(*args, *deps)


DW_TM = 1408
FF_TN = 256
SAVED_GU = BF16


def _mm_swiglu_fwd(h, w_gu, deps=()):
    T = h.shape[0]
    nj = D_FF // FF_TN
    nd = len(deps)

    def body(*refs):
        h_ref, wg_ref, wu_ref = refs[:3]
        g_ref, u_ref, act_ref = refs[3 + nd:]
        hv = h_ref[...]
        gv = lax.dot_general(hv, wg_ref[...], _DN["nt"], preferred_element_type=F32)
        uv = lax.dot_general(hv, wu_ref[...], _DN["nt"], preferred_element_type=F32)
        g_ref[...] = gv.astype(g_ref.dtype)
        u_ref[...] = uv.astype(u_ref.dtype)
        act_ref[...] = (gv * jax.nn.sigmoid(gv) * uv).astype(act_ref.dtype)

    col = pl.BlockSpec((T, FF_TN), lambda j: (0, j))
    return pl.pallas_call(
        body, name="mm_gate_up_swiglu", grid=(nj,),
        in_specs=[pl.BlockSpec((T, D_MODEL), lambda j: (0, 0)),
                  pl.BlockSpec((FF_TN, D_MODEL), lambda j: (j, 0)),
                  pl.BlockSpec((FF_TN, D_MODEL), lambda j: (j + nj, 0))] + [pl.BlockSpec(memory_space=pl.ANY)] * nd,
        out_specs=[col, col, col],
        out_shape=[jax.ShapeDtypeStruct((T, D_FF), SAVED_GU), jax.ShapeDtypeStruct((T, D_FF), SAVED_GU),
                   jax.ShapeDtypeStruct((T, D_FF), BF16)],
        compiler_params=_cparams(("parallel",)),
    )(h, w_gu, w_gu, *deps)


def _mm_swiglu_bwd(dr, w_dn, g, u, deps=()):
    T = dr.shape[0]

    def body(*refs):
        dr_ref, w_ref, g_ref, u_ref = refs[:4]
        dg_ref, du_ref = refs[-2:]
        da = lax.dot_general(dr_ref[...], w_ref[...], _DN["nt"], preferred_element_type=F32)
        gv, uv = g_ref[...].astype(F32), u_ref[...].astype(F32)
        s = jax.nn.sigmoid(gv)
        du_ref[...] = (da * (gv * s)).astype(du_ref.dtype)
        dg_ref[...] = (da * uv * (s * (1.0 + gv * (1.0 - s)))).astype(dg_ref.dtype)

    col = pl.BlockSpec((T, FF_TN), lambda j: (0, j))
    return pl.pallas_call(
        body, name="mm_dact_swiglu", grid=(D_FF // FF_TN,),
        in_specs=[pl.BlockSpec((T, D_MODEL), lambda j: (0, 0)), pl.BlockSpec((FF_TN, D_MODEL), lambda j: (j, 0)),
                  col, col] + [ANY] * len(deps),
        out_specs=[col, col],
        out_shape=[jax.ShapeDtypeStruct((T, D_FF), BF16)] * 2,
        compiler_params=_cparams(("parallel",)),
    )(dr, w_dn, g, u, *deps)


def _gelu(x):
    return 0.5 * x * (1.0 + lax.erf(x * INV_SQRT2))


def _gelu_grad(x):
    return 0.5 * (1.0 + lax.erf(x * INV_SQRT2)) + x * (jnp.exp(-0.5 * x * x) * INV_SQRT_2PI)


def _shift_down(z, k):
    row = lax.broadcasted_iota(jnp.int32, z.shape, 0)
    return jnp.where(row >= k, pltpu.roll(z, k, 0), 0.0)


def _shift_up(z, k):
    n = z.shape[0]
    row = lax.broadcasted_iota(jnp.int32, z.shape, 0)
    return jnp.where(row < n - k, pltpu.roll(z, n - k, 0), 0.0)


def _lo_mask(shape):
    return lax.broadcasted_iota(jnp.int32, shape, len(shape) - 1) < HALF


def _seg_mean(x, lo):
    a = jnp.sum(jnp.where(lo, x, 0.0), axis=-1, keepdims=True)
    b = jnp.sum(jnp.where(lo, 0.0, x), axis=-1, keepdims=True)
    return jnp.where(lo, a, b) * (1.0 / HALF)


def _pool_windows(first):
    lo = _lo_mask((1, LANES))
    return jnp.where(first, jnp.where(lo, 2.0, 4.0), jnp.where(lo, 8.0, 16.0)), lo


def _pool_mean_minus_token(p, first):
    wl, lo = _pool_windows(first)
    s2 = p + _shift_down(p, 1)
    s4 = s2 + _shift_down(s2, 2)
    s8 = s4 + _shift_down(s4, 4)
    s16 = s8 + _shift_down(s8, 8)
    win = jnp.where(first, jnp.where(lo, s2, s4), jnp.where(lo, s8, s16))
    t1 = (lax.broadcasted_iota(jnp.int32, p.shape, 0) + 1).astype(F32)
    count = jnp.minimum(t1, wl)
    return win / count - p, count


SGU_UNROLL = 4


def _tril_keep():
    r = lax.broadcasted_iota(jnp.int32, (2 * CHUNK, CHUNK), 0)
    s = lax.broadcasted_iota(jnp.int32, (2 * CHUNK, CHUNK), 1)
    return s <= (r & (CHUNK - 1))


def _sgu_chunk_fwd(u, v, g, wm, bias, lo):
    ug = _gelu(u)
    vg = _gelu(v)
    mu = _seg_mean(vg, lo)
    xc = vg - mu
    var = _seg_mean(xc * xc, lo)
    rstd = lax.rsqrt(var + LN_EPS)
    vn = xc * rstd
    vh = (vn * g).astype(BF16)
    mm2 = jnp.dot(wm, vh, preferred_element_type=F32)
    mixed = jnp.where(lo, mm2[:CHUNK], mm2[CHUNK:]) + bias
    return ug, vn, rstd, vh, mixed


def _mixer_fwd(proj, wconv, wpool_bd, pscale, lng, wsp, bias):
    T = proj.shape[0]
    nchunk = T // CHUNK

    def body(a_ref, b_ref, c_ref, wc_ref, wp_ref, ps_ref, lng_ref, wsp_ref, bias_ref, o_ref):
        j = pl.program_id(0)

        @pl.when(j < 3)
        def _conv():
            z = c_ref[...] * a_ref[...]
            w = wc_ref[...]
            y = w[0:1] * _shift_down(z, 2) + w[1:2] * _shift_down(z, 1) + w[2:3] * z
            o_ref[...] = (b_ref[...] * y).astype(o_ref.dtype)

        @pl.when((j >= 3) & (j < 5))
        def _pool():
            d, _ = _pool_mean_minus_token(a_ref[...], j == 3)
            y = jnp.dot(d.astype(BF16), wp_ref[...].astype(BF16), preferred_element_type=F32)
            o_ref[...] = (y * ps_ref[...]).astype(o_ref.dtype)

        @pl.when(j >= 5)
        def _sgu():
            lo = _lo_mask((CHUNK, LANES))
            wm = jnp.where(_tril_keep(), wsp_ref[...], 0.0).astype(BF16)
            bias_t = bias_ref[...]
            g = lng_ref[...]

            def chunk(n, carry):
                rows = pl.ds(pl.multiple_of(n * CHUNK, CHUNK), CHUNK)
                ug, _, _, _, mixed = _sgu_chunk_fwd(a_ref[rows, :], b_ref[rows, :], g, wm, bias_t, lo)
                o_ref[rows, :] = (ug * mixed).astype(o_ref.dtype)
                return carry

            lax.fori_loop(0, nchunk, chunk, 0, unroll=SGU_UNROLL)

    def col(f):
        return lambda j: (0, f(j))

    clip = lambda v, lo, hi: jnp.minimum(jnp.maximum(v, lo), hi)
    return pl.pallas_call(
        body,
        name="mixer_fwd",
        grid=(8,),
        in_specs=[
            pl.BlockSpec((T, LANES), col(lambda j: jnp.where(j < 3, j, jnp.where(j < 5, j + 6, j + 6)))),
            pl.BlockSpec((T, LANES), col(lambda j: jnp.where(j < 3, j + 3, jnp.where(j < 5, 5, j + 9)))),
            pl.BlockSpec((T, LANES), col(lambda j: jnp.where(j < 3, j + 6, 8))),
            pl.BlockSpec((3, LANES), col(lambda j: clip(j, 0, 2))),
            pl.BlockSpec((None, LANES, LANES), lambda j: (clip(j - 3, 0, 1), 0, 0)),
            pl.BlockSpec((1, LANES), col(lambda j: clip(j - 3, 0, 1))),
            pl.BlockSpec((1, LANES), col(lambda j: clip(j - 5, 0, 2))),
            pl.BlockSpec((None, 2 * CHUNK, CHUNK), lambda j: (clip(j - 5, 0, 2), 0, 0)),
            pl.BlockSpec((None, CHUNK, LANES), lambda j: (clip(j - 5, 0, 2), 0, 0)),
        ],
        out_specs=pl.BlockSpec((T, LANES), lambda j: (0, j)),
        out_shape=jax.ShapeDtypeStruct((T, D_MODEL), BF16),
        compiler_params=_cparams(("arbitrary",)),
    )(proj, proj, proj, wconv, wpool_bd, pscale, lng, wsp, bias)


def _mixer_bwd(proj, dmix, wconv, wpool_bd, pscale, lng, wsp, bias):
    T = proj.shape[0]
    nchunk = T // CHUNK

    def body(a_ref, b_ref, c_ref, dm_ref, wc_ref, wp_ref, ps_ref, lng_ref, wsp_ref, bias_ref,
             o_ref, dwc_ref, dwp_ref, dps_ref, dlng_ref, dwsp_ref, dbias_ref, keep1, keep2):
        k = pl.program_id(0)

        @pl.when(k < 3)
        def _conv():
            xa, gb, gc, dya = a_ref[...], b_ref[...], c_ref[...], dm_ref[...]
            w = wc_ref[...]
            z = gc * xa
            z1 = _shift_down(z, 1)
            z2 = _shift_down(z, 2)
            y = w[0:1] * z2 + w[1:2] * z1 + w[2:3] * z
            dyv = dya * gb
            dz = w[2:3] * dyv + w[1:2] * _shift_up(dyv, 1) + w[0:1] * _shift_up(dyv, 2)
            dwc_ref[0:1, :] = jnp.sum(dyv * z2, axis=0, keepdims=True)
            dwc_ref[1:2, :] = jnp.sum(dyv * z1, axis=0, keepdims=True)
            dwc_ref[2:3, :] = jnp.sum(dyv * z, axis=0, keepdims=True)
            o_ref[...] = (dz * gc).astype(o_ref.dtype)
            keep1[k] = (dya * y).astype(keep1.dtype)
            keep1[k + 3] = (dz * xa).astype(keep1.dtype)

        @pl.when((k >= 3) & (k < 9))
        def _emit_gb_gc():
            o_ref[...] = keep1[k - 3]

        @pl.when((k >= 9) & (k < 11))
        def _pool():
            first = k == 9
            p, dyb = a_ref[...], dm_ref[...]
            d, count = _pool_mean_minus_token(p, first)
            w2 = wp_ref[...].astype(BF16)
            db = d.astype(BF16)
            y = jnp.dot(db, w2, preferred_element_type=F32)
            dps_ref[...] = jnp.sum(dyb * y, axis=0, keepdims=True)
            dyv = (dyb * ps_ref[...]).astype(BF16)
            dd = lax.dot_general(dyv, w2, _DN["nt"], preferred_element_type=F32)
            dwp_ref[...] = lax.dot_general(db, dyv, _DN["tn"], preferred_element_type=F32)
            dwin = dd / count
            a2 = dwin + _shift_up(dwin, 1)
            a4 = a2 + _shift_up(a2, 2)
            a8 = a4 + _shift_up(a4, 4)
            a16 = a8 + _shift_up(a8, 8)
            _, lo = _pool_windows(first)
            back = jnp.where(first, jnp.where(lo, a2, a4), jnp.where(lo, a8, a16))
            o_ref[...] = (back - dd).astype(o_ref.dtype)

        @pl.when((k >= 11) & (k < 14))
        def _sgu():
            lo = _lo_mask((CHUNK, LANES))
            keep = _tril_keep()
            wm = jnp.where(keep, wsp_ref[...], 0.0).astype(BF16)
            bias_t = bias_ref[...]
            g = lng_ref[...]
            dwsp_ref[...] = jnp.zeros_like(dwsp_ref)
            dbias_ref[...] = jnp.zeros_like(dbias_ref)
            dlng_ref[...] = jnp.zeros_like(dlng_ref)

            def chunk(n, carry):
                rows = pl.ds(pl.multiple_of(n * CHUNK, CHUNK), CHUNK)
                u, v, dyc = a_ref[rows, :], b_ref[rows, :], dm_ref[rows, :]
                ug, vn, rstd, vh, mixed = _sgu_chunk_fwd(u, v, g, wm, bias_t, lo)
                dmx = dyc * ug
                o_ref[rows, :] = (dyc * mixed * _gelu_grad(u)).astype(o_ref.dtype)
                dbias_ref[...] += dmx
                dst = jnp.concatenate([jnp.where(lo, dmx, 0.0), jnp.where(lo, 0.0, dmx)], axis=0).astype(BF16)
                dwsp_ref[...] += lax.dot_general(dst, vh, _DN["nt"], preferred_element_type=F32)
                dvh = lax.dot_general(wm, dst, _DN["tn"], preferred_element_type=F32)
                dlng_ref[...] += jnp.sum(dvh * vn, axis=0, keepdims=True)
                dvn = dvh * g
                m1 = _seg_mean(dvn, lo)
                m2 = _seg_mean(dvn * vn, lo)
                dvg = rstd * (dvn - m1 - vn * m2)
                keep2[k - 11, rows, :] = (dvg * _gelu_grad(v)).astype(keep2.dtype)
                return carry

            lax.fori_loop(0, nchunk, chunk, 0, unroll=SGU_UNROLL)
            dwsp_ref[...] = jnp.where(keep, dwsp_ref[...], 0.0)
            dbt = dbias_ref[...]
            lane = lax.broadcasted_iota(jnp.int32, (CHUNK, LANES), 1)
            sa = jnp.sum(jnp.where(lo, dbt, 0.0), axis=-1, keepdims=True)
            sb = jnp.sum(jnp.where(lo, 0.0, dbt), axis=-1, keepdims=True)
            dbias_ref[...] = jnp.where(lane == 0, sa, jnp.where(lane == 1, sb, 0.0))

        @pl.when(k >= 14)
        def _emit_v():
            o_ref[...] = keep2[k - 14]

    def col(f):
        return lambda k: (0, f(k))

    clip = lambda v, lo, hi: jnp.minimum(jnp.maximum(v, lo), hi)
    view_a = lambda k: jnp.where(k < 3, k, jnp.where(k < 9, 2, jnp.where(k < 14, k, 13)))
    view_b = lambda k: jnp.where(k < 3, k + 3, jnp.where(k < 11, 5, jnp.where(k < 14, k + 3, 16)))
    view_c = lambda k: jnp.where(k < 3, k + 6, 8)
    view_dm = lambda k: jnp.where(k < 3, k, jnp.where(k < 9, 2, jnp.where(k < 14, k - 6, 7)))
    return pl.pallas_call(
        body,
        name="mixer_bwd",
        grid=(17,),
        in_specs=[
            pl.BlockSpec((T, LANES), col(view_a)),
            pl.BlockSpec((T, LANES), col(view_b)),
            pl.BlockSpec((T, LANES), col(view_c)),
            pl.BlockSpec((T, LANES), col(view_dm)),
            pl.BlockSpec((3, LANES), col(lambda k: clip(k, 0, 2))),
            pl.BlockSpec((None, LANES, LANES), lambda k: (clip(k - 9, 0, 1), 0, 0)),
            pl.BlockSpec((1, LANES), col(lambda k: clip(k - 9, 0, 1))),
            pl.BlockSpec((1, LANES), col(lambda k: clip(k - 11, 0, 2))),
            pl.BlockSpec((None, 2 * CHUNK, CHUNK), lambda k: (clip(k - 11, 0, 2), 0, 0)),
            pl.BlockSpec((None, CHUNK, LANES), lambda k: (clip(k - 11, 0, 2), 0, 0)),
        ],
        out_specs=[
            pl.BlockSpec((T, LANES), lambda k: (0, k)),
            pl.BlockSpec((3, LANES), col(lambda k: clip(k, 0, 2))),
            pl.BlockSpec((None, LANES, LANES), lambda k: (clip(k - 9, 0, 1), 0, 0)),
            pl.BlockSpec((1, LANES), col(lambda k: clip(k - 9, 0, 1))),
            pl.BlockSpec((1, LANES), col(lambda k: clip(k - 11, 0, 2))),
            pl.BlockSpec((None, 2 * CHUNK, CHUNK), lambda k: (clip(k - 11, 0, 2), 0, 0)),
            pl.BlockSpec((None, CHUNK, LANES), lambda k: (clip(k - 11, 0, 2), 0, 0)),
        ],
        out_shape=[
            jax.ShapeDtypeStruct((T, IN_W), BF16),
            jax.ShapeDtypeStruct((3, CONV_W), F32),
            jax.ShapeDtypeStruct((2, LANES, LANES), F32),
            jax.ShapeDtypeStruct((1, POOL_W), F32),
            jax.ShapeDtypeStruct((1, SGU_W), F32),
            jax.ShapeDtypeStruct((3, 2 * CHUNK, CHUNK), F32),
            jax.ShapeDtypeStruct((3, CHUNK, LANES), F32),
        ],
        scratch_shapes=[pltpu.VMEM((6, T, LANES), BF16), pltpu.VMEM((3, T, LANES), BF16)],
        compiler_params=_cparams(("arbitrary",)),
    )(proj, proj, proj, dmix, wconv, wpool_bd, pscale, lng, wsp, bias)


def _ln_fwd(prev, pg, pb, mmout, g, b, tm=256):
    T = prev.shape[0]

    def body(prev_ref, pg_ref, pb_ref, mm_ref, g_ref, b_ref, xhat_ref, rstd_ref, y_ref):
        r = ALPHA * (prev_ref[...] * pg_ref[...] + pb_ref[...]) + mm_ref[...]
        mu = jnp.mean(r, axis=-1, keepdims=True)
        xc = r - mu
        var = jnp.mean(xc * xc, axis=-1, keepdims=True)
        rstd = lax.rsqrt(var + LN_EPS)
        xhat = xc * rstd
        xhat_ref[...] = xhat
        rstd_ref[...] = rstd
        y_ref[...] = (xhat * g_ref[...] + b_ref[...]).astype(y_ref.dtype)

    row = pl.BlockSpec((tm, D_MODEL), lambda i: (i, 0))
    vec = pl.BlockSpec((1, D_MODEL), lambda i: (0, 0))
    return pl.pallas_call(
        body,
        name="ln_fwd",
        grid=(T // tm,),
        in_specs=[row, vec, vec, row, vec, vec],
        out_specs=[row, pl.BlockSpec((tm, 1), lambda i: (i, 0)), row],
        out_shape=[jax.ShapeDtypeStruct((T, D_MODEL), F32), jax.ShapeDtypeStruct((T, 1), F32),
                   jax.ShapeDtypeStruct((T, D_MODEL), BF16)],
        compiler_params=_cparams(("parallel",)),
    )(prev, pg, pb, mmout, g, b)


def _ln_bwd(dres, dmm, xhat, rstd, g, tm=256, deps=()):
    T = xhat.shape[0]
    has_res = dres is not None
    nd = len(deps)

    def body(*refs):
        refs = refs[:len(refs) - 4 - nd] + refs[len(refs) - 4:]
        if has_res:
            dres_ref, dmm_ref, xhat_ref, rstd_ref, g_ref, dr_ref, drb_ref, dg_ref, db_ref = refs
            dy = ALPHA * dres_ref[...] + dmm_ref[...]
        else:
            dmm_ref, xhat_ref, rstd_ref, g_ref, dr_ref, drb_ref, dg_ref, db_ref = refs
            dy = dmm_ref[...]
        xhat_v = xhat_ref[...]

        @pl.when(pl.program_id(0) == 0)
        def _():
            dg_ref[...] = jnp.zeros_like(dg_ref)
            db_ref[...] = jnp.zeros_like(db_ref)

        dg_ref[...] += jnp.sum(dy * xhat_v, axis=0, keepdims=True)
        db_ref[...] += jnp.sum(dy, axis=0, keepdims=True)
        dxh = dy * g_ref[...]
        m1 = jnp.mean(dxh, axis=-1, keepdims=True)
        m2 = jnp.mean(dxh * xhat_v, axis=-1, keepdims=True)
        dr = rstd_ref[...] * (dxh - m1 - xhat_v * m2)
        dr_ref[...] = dr
        drb_ref[...] = dr.astype(drb_ref.dtype)

    row = pl.BlockSpec((tm, D_MODEL), lambda i: (i, 0))
    vec = pl.BlockSpec((1, D_MODEL), lambda i: (0, 0))
    in_specs = ([row] if has_res else []) + [row, row, pl.BlockSpec((tm, 1), lambda i: (i, 0)), vec]
    in_specs += [pl.BlockSpec(memory_space=pl.ANY)] * nd
    args = ([dres] if has_res else []) + [dmm, xhat, rstd, g] + list(deps)
    return pl.pallas_call(
        body,
        name="ln_bwd_res" if has_res else "ln_bwd",
        grid=(T // tm,),
        in_specs=in_specs,
        out_specs=[row, row, vec, vec],
        out_shape=[jax.ShapeDtypeStruct((T, D_MODEL), F32), jax.ShapeDtypeStruct((T, D_MODEL), BF16),
                   jax.ShapeDtypeStruct((1, D_MODEL), F32), jax.ShapeDtypeStruct((1, D_MODEL), F32)],
        compiler_params=_cparams(("arbitrary",)),
    )(*args)


def _loss_head(xhat, g, b, target, tm=256):
    T = xhat.shape[0]

    def body(xhat_ref, g_ref, b_ref, t_ref, loss_ref, dy_ref):
        err = xhat_ref[...] * g_ref[...] + b_ref[...] - t_ref[...]

        @pl.when(pl.program_id(0) == 0)
        def _():
            loss_ref[...] = jnp.zeros_like(loss_ref)

        part = jnp.sum(jnp.sum(err * err, axis=-1, keepdims=True), axis=0, keepdims=True)
        loss_ref[...] += jnp.broadcast_to(part * (0.5 / D_MODEL), loss_ref.shape)
        dy_ref[...] = err * (1.0 / D_MODEL)

    row = pl.BlockSpec((tm, D_MODEL), lambda i: (i, 0))
    vec = pl.BlockSpec((1, D_MODEL), lambda i: (0, 0))
    return pl.pallas_call(
        body,
        name="loss_head",
        grid=(T // tm,),
        in_specs=[row, vec, vec, row],
        out_specs=[pl.BlockSpec((8, LANES), lambda i: (0, 0)), row],
        out_shape=[jax.ShapeDtypeStruct((8, LANES), F32), jax.ShapeDtypeStruct((T, D_MODEL), F32)],
        compiler_params=_cparams(("arbitrary",)),
    )(xhat, g, b, target)


def _residual_out(dres, dmm, tm=256):
    T = dres.shape[0]

    def body(a_ref, b_ref, o_ref):
        o_ref[...] = ALPHA * a_ref[...] + b_ref[...]

    row = pl.BlockSpec((tm, D_MODEL), lambda i: (i, 0))
    return pl.pallas_call(
        body, name="residual_out", grid=(T // tm,), in_specs=[row, row], out_specs=row,
        out_shape=jax.ShapeDtypeStruct((T, D_MODEL), F32), compiler_params=_cparams(("parallel",)),
    )(dres, dmm)


SW_TC = 1408


def _swiglu_fwd(gu, tm=128):
    T = gu.shape[0]

    def body(gu_ref, o_ref):
        gv = gu_ref[:, :D_FF]
        o_ref[...] = (gv * jax.nn.sigmoid(gv) * gu_ref[:, D_FF:]).astype(o_ref.dtype)

    return pl.pallas_call(
        body, name="swiglu_fwd", grid=(T // tm,),
        in_specs=[pl.BlockSpec((tm, 2 * D_FF), lambda i: (i, 0))],
        out_specs=pl.BlockSpec((tm, D_FF), lambda i: (i, 0)),
        out_shape=jax.ShapeDtypeStruct((T, D_FF), BF16), compiler_params=_cparams(("parallel",)),
    )(gu)


def _swiglu_bwd(gu, dact, tm=128):
    T = gu.shape[0]

    def body(gu_ref, da_ref, dgu_ref, act_ref):
        gv, uv, da = gu_ref[:, :D_FF], gu_ref[:, D_FF:], da_ref[...]
        s = jax.nn.sigmoid(gv)
        sg = gv * s
        act_ref[...] = (sg * uv).astype(act_ref.dtype)
        dgu_ref[:, D_FF:] = (da * sg).astype(dgu_ref.dtype)
        dgu_ref[:, :D_FF] = (da * uv * (s * (1.0 + gv * (1.0 - s)))).astype(dgu_ref.dtype)

    wide = pl.BlockSpec((tm, 2 * D_FF), lambda i: (i, 0))
    half = pl.BlockSpec((tm, D_FF), lambda i: (i, 0))
    return pl.pallas_call(
        body, name="swiglu_bwd", grid=(T // tm,),
        in_specs=[wide, half], out_specs=[wide, half],
        out_shape=[jax.ShapeDtypeStruct((T, 2 * D_FF), BF16), jax.ShapeDtypeStruct((T, D_FF), BF16)],
        compiler_params=_cparams(("parallel",)),
    )(gu, dact)


def _adamw(w, g, m, v, tr):
    R, C = w.shape[-2:]
    assert R % tr == 0
    c1 = 1.0 - ADAM_B1 ** ADAM_STEP
    c2 = 1.0 - ADAM_B2 ** ADAM_STEP

    def body(w_ref, g_ref, m_ref, v_ref, d_ref, mo_ref, vo_ref):
        gv = g_ref[...]
        mn = ADAM_B1 * m_ref[...] + (1.0 - ADAM_B1) * gv
        vn = ADAM_B2 * v_ref[...] + (1.0 - ADAM_B2) * (gv * gv)
        d_ref[...] = -ADAM_LR * ((mn / c1) / (jnp.sqrt(vn / c2) + ADAM_EPS) + ADAM_WD * w_ref[...])
        mo_ref[...] = mn
        vo_ref[...] = vn

    if w.ndim == 2:
        grid, blk = (R // tr,), pl.BlockSpec((tr, C), lambda i: (i, 0))
    else:
        grid, blk = (w.shape[0], R // tr), pl.BlockSpec((None, tr, C), lambda l, i: (l, i, 0))
    return pl.pallas_call(
        body, name="adamw", grid=grid, in_specs=[blk] * 4, out_specs=[blk] * 3,
        out_shape=[jax.ShapeDtypeStruct(w.shape, F32)] * 3, compiler_params=_cparams(("parallel",) * len(grid)),
    )(w, g, m, v)


def _my_place():
    return lax.axis_index("x"), lax.axis_index("y"), lax.axis_index("c")


ANY = pl.BlockSpec(memory_space=pl.ANY)
HBM = pl.BlockSpec(memory_space=pltpu.HBM)
SEM = pl.BlockSpec(memory_space=pltpu.SEMAPHORE)
EFFECT = pltpu.SideEffectType.DATAFLOW_SIDE_EFFECTING


def _in_hbm(a):
    return pltpu.with_memory_space_constraint(a, pltpu.HBM)


def _block_rows(ref, dev):
    r = ref.shape[0] // N_DEV
    start = pl.multiple_of((4 * dev[0] + 2 * dev[1] + dev[2]) * r, 16)
    return ref.at[pl.ds(start, r), :]


def _ag_first_copies(s_refs, land_refs, send_sems, recv_sems, receiving):
    x, y, c = _my_place()
    peers = [(x, y, 1 - c)] + [(*chip, c) for chip in _other_chips(x, y)]
    copies = []
    for k, peer in enumerate(peers):
        block = peer if receiving else (x, y, c)
        copies += [pltpu.make_async_remote_copy(
            src_ref=s_refs[w], dst_ref=_block_rows(land_refs[w], block),
            send_sem=send_sems.at[k * len(s_refs) + w], recv_sem=recv_sems.at[k * len(s_refs) + w],
            device_id=peer, device_id_type=MESH)
            for w in range(len(s_refs))]
    return copies


def _ag_start(shards, layer, after=()):
    nw = len(shards)

    def body(*refs):
        s_refs, land_refs = refs[:nw], refs[nw:2 * nw]
        token = refs[-1]
        sems = 2 * nw + len(after)
        for cp in _ag_first_copies(s_refs, land_refs, refs[sems], refs[sems + 1], False):
            cp.start()
        token[...] = jnp.zeros_like(token)

    lands = [lax.empty((N_DEV * s.shape[0], D_MODEL), BF16) for s in shards]
    out = pl.pallas_call(
        body, name="ag_start_%s" % layer,
        in_specs=[HBM] * (2 * nw) + [ANY] * len(after),
        out_specs=(SEM, SEM, *[HBM] * (2 * nw), pl.BlockSpec(memory_space=pltpu.VMEM)),
        out_shape=(pltpu.SemaphoreType.DMA((4 * nw,)), pltpu.SemaphoreType.DMA((4 * nw,)),
                   *[pltpu.HBM(a.shape, a.dtype) for a in list(shards) + lands],
                   jax.ShapeDtypeStruct((8, LANES), F32)),
        input_output_aliases={i: 2 + i for i in range(2 * nw)},
        compiler_params=pltpu.CompilerParams(has_side_effects=EFFECT),
    )(*[_in_hbm(a) for a in list(shards) + lands], *after)
    return out[0], out[1], out[2:2 + nw], out[2 + nw:2 + 2 * nw], out[-1]


def _ag_wait(send_sems, recv_sems, shards, lands, after, layer):
    nw = len(shards)

    def body(*refs):
        s_refs, land_refs = refs[:nw], refs[nw:2 * nw]
        for cp in _ag_first_copies(s_refs, land_refs, refs[2 * nw], refs[2 * nw + 1], True):
            cp.wait_send()
            cp.wait_recv()

    out = pl.pallas_call(
        body, name="ag_wait_%s" % layer,
        in_specs=[HBM] * (2 * nw) + [SEM, SEM] + [ANY] * len(after),
        out_specs=[HBM] * (2 * nw),
        out_shape=[pltpu.HBM(a.shape, a.dtype) for a in list(shards) + list(lands)],
        input_output_aliases={i: i for i in range(2 * nw)},
        compiler_params=pltpu.CompilerParams(has_side_effects=EFFECT),
    )(*shards, *lands, send_sems, recv_sems, *after)
    return out[:nw], out[nw:]


def _ag_pass_on(shards, lands):
    nw = len(shards)

    def body(*refs):
        s_refs, g_refs = refs[:nw], refs[2 * nw:3 * nw]
        send_sems, recv_sems, local_sems = refs[3 * nw:3 * nw + 3]
        stage = refs[3 * nw + 3:]
        x, y, c = _my_place()
        load = [pltpu.make_async_copy(s_refs[w], stage[w], local_sems.at[w]) for w in range(nw)]
        mine = [pltpu.make_async_copy(stage[w], _block_rows(g_refs[w], (x, y, c)), local_sems.at[w])
                for w in range(nw)]
        for cp in load:
            cp.start()
        sends, arrivals = [], []
        for j, chip in enumerate(_other_chips(x, y)):
            for w in range(nw):
                rows_out = _block_rows(g_refs[w], (*chip, c))
                rows_in = _block_rows(g_refs[w], (*chip, 1 - c))
                sends.append(pltpu.make_async_remote_copy(
                    src_ref=rows_out, dst_ref=rows_out, send_sem=send_sems.at[j, w], recv_sem=recv_sems.at[j, w],
                    device_id=(x, y, 1 - c), device_id_type=MESH))
                arrivals.append(pltpu.make_async_remote_copy(
                    src_ref=rows_in, dst_ref=rows_in, send_sem=send_sems.at[j, w], recv_sem=recv_sems.at[j, w],
                    device_id=(x, y, 1 - c), device_id_type=MESH))
        for cp in sends:
            cp.start()
        for w in range(nw):
            load[w].wait()
            mine[w].start()
        for cp in arrivals:
            cp.wait_recv()
        for cp in sends:
            cp.wait_send()
        for cp in mine:
            cp.wait()

    return pl.pallas_call(
        body, name="ag_pass_on",
        in_specs=[ANY] * (2 * nw), out_specs=[ANY] * nw,
        out_shape=[jax.ShapeDtypeStruct(a.shape, a.dtype) for a in lands],
        input_output_aliases={nw + i: i for i in range(nw)},
        scratch_shapes=[pltpu.SemaphoreType.DMA((3, nw)), pltpu.SemaphoreType.DMA((3, nw)),
                        pltpu.SemaphoreType.DMA((nw,))] + [pltpu.VMEM(s.shape, s.dtype) for s in shards],
        compiler_params=_cparams(),
    )(*shards, *lands)


def _rs_sibling_copies(p_refs, land_refs, send_sems, recv_sems):
    x, y, c = _my_place()
    return [pltpu.make_async_remote_copy(
        src_ref=p_refs[w].at[:, 1 - c], dst_ref=land_refs[w],
        send_sem=send_sems.at[w], recv_sem=recv_sems.at[w], device_id=(x, y, 1 - c), device_id_type=MESH)
        for w in range(len(p_refs))]


def _rs_sibling_start(parts, tag, after=()):
    nw = len(parts)
    sems = 2 * nw + len(after)

    def body(*refs):
        for cp in _rs_sibling_copies(refs[:nw], refs[nw:2 * nw], refs[sems], refs[sems + 1]):
            cp.start()
        refs[-1][...] = jnp.zeros_like(refs[-1])

    lands = [lax.empty(p.shape[:1] + p.shape[2:], BF16) for p in parts]
    out = pl.pallas_call(
        body, name="rs_sibling_start_%s" % tag,
        in_specs=[HBM] * (2 * nw) + [ANY] * len(after),
        out_specs=(SEM, SEM, *[HBM] * (2 * nw), pl.BlockSpec(memory_space=pltpu.VMEM)),
        out_shape=(pltpu.SemaphoreType.DMA((nw,)), pltpu.SemaphoreType.DMA((nw,)),
                   *[pltpu.HBM(a.shape, a.dtype) for a in list(parts) + lands],
                   jax.ShapeDtypeStruct((8, LANES), F32)),
        input_output_aliases={i: 2 + i for i in range(2 * nw)},
        compiler_params=pltpu.CompilerParams(has_side_effects=EFFECT),
    )(*[_in_hbm(a) for a in list(parts) + lands], *after)
    return out[0], out[1], out[2:2 + nw], out[2 + nw:2 + 2 * nw], out[-1]


def _rs_sibling_wait(send_sems, recv_sems, parts, lands, after, tag):
    nw = len(parts)

    def body(*refs):
        for cp in _rs_sibling_copies(refs[:nw], refs[nw:2 * nw], refs[2 * nw], refs[2 * nw + 1]):
            cp.wait_send()
            cp.wait_recv()

    out = pl.pallas_call(
        body, name="rs_sibling_wait_%s" % tag,
        in_specs=[HBM] * (2 * nw) + [SEM, SEM] + [ANY] * len(after),
        out_specs=[HBM] * (2 * nw),
        out_shape=[pltpu.HBM(a.shape, a.dtype) for a in list(parts) + list(lands)],
        input_output_aliases={i: i for i in range(2 * nw)},
        compiler_params=pltpu.CompilerParams(has_side_effects=EFFECT),
    )(*parts, *lands, send_sems, recv_sems, *after)
    return out[:nw], out[nw:]


def _rs_chip_sum(parts, gots, c):
    n = len(parts)

    def body(c_ref, *refs):
        for p_ref, g_ref, o_ref in zip(refs[:n], refs[n:2 * n], refs[2 * n:]):
            o_ref[...] = (p_ref[...].astype(F32) + g_ref[...].astype(F32)).astype(o_ref.dtype)

    mine = [pl.BlockSpec((None, None, p.shape[2], D_MODEL), lambda q, c_ref: (q, c_ref[0], 0, 0)) for p in parts]
    theirs = [pl.BlockSpec((None, g.shape[1], D_MODEL), lambda q, c_ref: (q, 0, 0)) for g in gots]
    return pl.pallas_call(
        body, name="rs_chip_sum",
        grid_spec=pltpu.PrefetchScalarGridSpec(
            num_scalar_prefetch=1, grid=(4,), in_specs=mine + theirs, out_specs=theirs),
        out_shape=[jax.ShapeDtypeStruct(g.shape, BF16) for g in gots],
        compiler_params=_cparams(("parallel",)),
    )(c, *parts, *gots)


def _other_chips(x, y):
    return [(1 - x, y), (x, 1 - y), (1 - x, 1 - y)]


def _rs_chip_copies(s_refs, land_refs, send_sems, recv_sems):
    x, y, c = _my_place()
    copies = []
    for k, chip in enumerate(_other_chips(x, y)):
        q = 2 * chip[0] + chip[1]
        copies += [pltpu.make_async_remote_copy(
            src_ref=s_refs[w].at[q], dst_ref=land_refs[w].at[k],
            send_sem=send_sems.at[k * len(s_refs) + w], recv_sem=recv_sems.at[k * len(s_refs) + w],
            device_id=(*chip, c), device_id_type=MESH)
            for w in range(len(s_refs))]
    return copies


def _rs_chip_start(sums, layer):
    nw = len(sums)

    def body(*refs):
        s_refs, land_refs = refs[:nw], refs[nw:2 * nw]
        send_sems, recv_sems = refs[2 * nw], refs[2 * nw + 1]
        token = refs[-1]
        for cp in _rs_chip_copies(s_refs, land_refs, send_sems, recv_sems):
            cp.start()
        token[...] = jnp.zeros_like(token)

    lands = [lax.empty((3,) + s.shape[1:], BF16) for s in sums]
    out = pl.pallas_call(
        body, name="rs_chip_start_%s" % layer,
        in_specs=[HBM] * (2 * nw),
        out_specs=(SEM, SEM, *[HBM] * (2 * nw), pl.BlockSpec(memory_space=pltpu.VMEM)),
        out_shape=(pltpu.SemaphoreType.DMA((3 * nw,)), pltpu.SemaphoreType.DMA((3 * nw,)),
                   *[pltpu.HBM(a.shape, a.dtype) for a in list(sums) + lands],
                   jax.ShapeDtypeStruct((8, LANES), F32)),
        input_output_aliases={i: 2 + i for i in range(2 * nw)},
        compiler_params=pltpu.CompilerParams(has_side_effects=EFFECT),
    )(*[_in_hbm(a) for a in list(sums) + lands])
    return out[0], out[1], out[2:2 + nw], out[2 + nw:2 + 2 * nw], out[-1]


def _rs_chip_wait(send_sems, recv_sems, sums, lands, after, layer):
    nw = len(sums)

    def body(*refs):
        s_refs, land_refs = refs[:nw], refs[nw:2 * nw]
        for cp in _rs_chip_copies(s_refs, land_refs, refs[2 * nw], refs[2 * nw + 1]):
            cp.wait_send()
            cp.wait_recv()

    out = pl.pallas_call(
        body, name="rs_chip_wait_%s" % layer,
        in_specs=[HBM] * (2 * nw) + [SEM, SEM] + [ANY] * len(after),
        out_specs=[HBM] * (2 * nw),
        out_shape=[pltpu.HBM(a.shape, a.dtype) for a in list(sums) + list(lands)],
        input_output_aliases={i: i for i in range(2 * nw)},
        compiler_params=pltpu.CompilerParams(has_side_effects=EFFECT),
    )(*sums, *lands, send_sems, recv_sems, *after)
    return out[:nw], out[nw:]


def _rs_finish(sums, gots, q, layer, into):
    n = len(sums)

    def body(q_ref, *refs):
        for s_ref, g_ref, o_ref in zip(refs[:n], refs[n:2 * n], refs[len(refs) - n:]):
            o_ref[...] = ((s_ref[...].astype(F32) + g_ref[0].astype(F32)) + g_ref[1].astype(F32)) + g_ref[2].astype(F32)

    rows = [s.shape[1] for s in sums]
    in_specs = [pl.BlockSpec((None, r, D_MODEL), lambda i, q_ref: (q_ref[0], 0, 0)) for r in rows]
    in_specs += [pl.BlockSpec((3, r, D_MODEL), lambda i, q_ref: (0, 0, 0)) for r in rows]
    args = [q, *sums, *gots]
    aliases = {}
    if into is not None:
        in_specs += [ANY] * n
        aliases = {len(args) + i: i for i in range(n)}
        args += list(into)
    return pl.pallas_call(
        body, name="rs_finish",
        grid_spec=pltpu.PrefetchScalarGridSpec(
            num_scalar_prefetch=1, grid=(1,), in_specs=in_specs,
            out_specs=[pl.BlockSpec((None, r, D_MODEL), lambda i, q_ref: (layer, 0, 0)) for r in rows]),
        out_shape=[jax.ShapeDtypeStruct((DEPTH, r, D_MODEL), F32) for r in rows],
        input_output_aliases=aliases,
        compiler_params=_cparams(("arbitrary",)),
    )(*args)


def _allreduce_small(vec, deps=()):
    R = vec.shape[0]
    assert R % (8 * N_DEV) == 0
    P = R // N_DEV
    nd = len(deps)

    def body(*refs):
        v_ref = refs[0]
        o_ref, buf, send1, recv1, send2, recv2 = refs[1 + nd:]
        x, y, c = _my_place()
        me = 4 * x + 2 * y + c

        def piece(ref, d):
            return ref.at[pl.ds(pl.multiple_of(d * P, 8), P), :]

        def peer(k):
            p = me ^ k
            return p, (p >> 2, (p >> 1) & 1, p & 1)

        scatter = []
        for k in range(1, N_DEV):
            p, where = peer(k)
            scatter.append(pltpu.make_async_remote_copy(
                src_ref=piece(v_ref, p), dst_ref=buf.at[k], send_sem=send1.at[k - 1], recv_sem=recv1.at[k - 1],
                device_id=where, device_id_type=MESH))
        for cp in scatter:
            cp.start()
        buf[0] = piece(v_ref, me)[...]
        for cp in scatter:
            cp.wait()
        acc = buf[me]
        for d in range(1, N_DEV):
            acc = acc + buf[me ^ d]
        piece(o_ref, me)[...] = acc
        spread, arrivals = [], []
        for k in range(1, N_DEV):
            p, where = peer(k)
            spread.append(pltpu.make_async_remote_copy(
                src_ref=piece(o_ref, me), dst_ref=piece(o_ref, me), send_sem=send2.at[k - 1], recv_sem=recv2.at[k - 1],
                device_id=where, device_id_type=MESH))
            arrivals.append(pltpu.make_async_remote_copy(
                src_ref=piece(o_ref, p), dst_ref=piece(o_ref, p), send_sem=send2.at[k - 1], recv_sem=recv2.at[k - 1],
                device_id=where, device_id_type=MESH))
        for cp in spread:
            cp.start()
        for cp in arrivals:
            cp.wait_recv()
        for cp in spread:
            cp.wait_send()

    sems = pltpu.SemaphoreType.DMA((N_DEV - 1,))
    return pl.pallas_call(
        body, name="allreduce_small",
        in_specs=[pl.BlockSpec(memory_space=pltpu.VMEM)] + [ANY] * nd, out_specs=pl.BlockSpec(memory_space=pltpu.VMEM),
        out_shape=jax.ShapeDtypeStruct((R, LANES), F32),
        scratch_shapes=[pltpu.VMEM((N_DEV, P, LANES), F32), sems, sems, sems, sems],
        compiler_params=_cparams(),
    )(vec, *deps)


def _pack(arrs):
    flat = jnp.concatenate([a.reshape(-1) for a in arrs])
    pad = (-flat.shape[0]) % (8 * N_DEV * LANES)
    return jnp.pad(flat, (0, pad)).reshape(-1, LANES)


def _unpack(packed, shapes):
    flat = packed.reshape(-1)
    out, off = [], 0
    for s in shapes:
        n = math.prod(s)
        out.append(flat[off:off + n].reshape(s))
        off += n
    return out


def kernel(x, w_in, w_conv, w_pool, pool_scale, sgu_ln_g, w_spatial, b_spatial, w_o, ln1_g, ln1_b, w_gate_up, w_down, ln2_g, ln2_b, loss_target, m_w_in, m_w_conv, m_w_pool, m_pool_scale, m_sgu_ln_g, m_w_spatial, m_b_spatial, m_w_o, m_ln1_g, m_ln1_b, m_w_gate_up, m_w_down, m_ln2_g, m_ln2_b, v_w_in, v_w_conv, v_w_pool, v_pool_scale, v_sgu_ln_g, v_w_spatial, v_b_spatial, v_w_o, v_ln1_g, v_ln1_b, v_w_gate_up, v_w_down, v_ln2_g, v_ln2_b):
    L = DEPTH
    T = x.shape[1]
    mx, my, mc = _my_place()
    dev = 4 * mx + 2 * my + mc
    xs = x[0]
    target = loss_target[0]

    conv_cols = w_conv.shape[2]
    w_conv_z = lax.dynamic_update_slice(jnp.zeros((L, 3, CONV_W), F32), w_conv, (0, 0, dev * conv_cols))
    w_conv_packed = _allreduce_small(_pack([w_conv_z]))
    w_conv_full = _unpack(w_conv_packed, [(L, 3, CONV_W)])[0]

    shards = (jnp.swapaxes(w_in, 1, 2).astype(BF16), jnp.swapaxes(w_gate_up, 1, 2).astype(BF16),
              w_o.astype(BF16), w_down.astype(BF16))
    first_gather = _ag_start_layer(shards, 0, [w_conv_packed])

    loss_tile, grad_x2, big_grads, small_grads = _local_step(
        xs, target, shards, first_gather, w_conv_full, w_pool, pool_scale, sgu_ln_g, w_spatial, b_spatial,
        ln1_g, ln1_b, ln2_g, ln2_b)
    loss = lax.psum(loss_tile[0, 0], ("x", "y", "c"))
    grad_x = grad_x2[None]
    big_w = (w_in, w_gate_up, w_o, w_down)
    big_m = (m_w_in, m_w_gate_up, m_w_o, m_w_down)
    big_v = (v_w_in, v_w_gate_up, v_w_o, v_w_down)
    small_w = [w_conv_full, w_pool, pool_scale, sgu_ln_g, w_spatial, b_spatial, ln1_g, ln1_b, ln2_g, ln2_b]
    small_m = [m_w_conv, m_w_pool, m_pool_scale, m_sgu_ln_g, m_w_spatial, m_b_spatial, m_ln1_g, m_ln1_b, m_ln2_g, m_ln2_b]
    small_v = [v_w_conv, v_w_pool, v_pool_scale, v_sgu_ln_g, v_w_spatial, v_b_spatial, v_ln1_g, v_ln1_b, v_ln2_g, v_ln2_b]
    grads, deltas, new_m, new_v = _reduce_and_update(
        big_grads, small_grads, big_w, big_m, big_v, small_w, small_m, small_v)
    return (loss, grad_x, *grads, *deltas, *new_m, *new_v)


def _ag_start_layer(shards, l, after):
    s_in, s_gu, s_o, s_dn = [s[l] for s in shards]
    first = _ag_start([s_in, s_o], "%da" % l, after=after)
    return first, _ag_start([s_gu, s_dn], "%db" % l, after=[first[4]])


def _ag_finish(gather, after, tag):
    send_sems, recv_sems, shards, lands, _ = gather
    shards, lands = _ag_wait(send_sems, recv_sems, shards, lands, after, tag)
    return _ag_pass_on(shards, lands)


def _rs_begin(parts, tag, after=()):
    return _rs_sibling_start([p.reshape(4, 2, p.shape[0] // N_DEV, D_MODEL) for p in parts], tag, after)


def _rs_continue(sibling_flight, after, c_arr, tag):
    send_sems, recv_sems, parts, lands, _ = sibling_flight
    parts, got = _rs_sibling_wait(send_sems, recv_sems, parts, lands, after, tag)
    return _rs_chip_start(_rs_chip_sum(parts, got, c_arr), tag)


def _local_step(xs, target, shards, gather, w_conv_full, w_pool, pool_scale, sgu_ln_g, w_spatial, b_spatial,
                ln1_g, ln1_b, ln2_g, ln2_b):
    L = DEPTH
    T = xs.shape[0]
    mx, my, mc = _my_place()
    c_arr = jnp.reshape(mc, (1,)).astype(jnp.int32)
    q_arr = jnp.reshape(2 * mx + my, (1,)).astype(jnp.int32)
    eye2 = jnp.eye(2, dtype=F32)
    wp = w_pool.reshape(L, 2, 2, HALF, HALF)
    wpool_bd = jnp.einsum("ltgcd,gh->ltgchd", wp, eye2).reshape(L, 2, LANES, LANES)
    wsp_t = w_spatial.reshape(L, 3, 2 * CHUNK, CHUNK)
    bias_t = jnp.repeat(jnp.swapaxes(b_spatial.reshape(L, 3, 2, CHUNK), 2, 3), HALF, axis=3)
    ones = jnp.ones((1, D_MODEL), F32)
    zeros = jnp.zeros((1, D_MODEL), F32)

    saved = []
    prev, pg, pb = xs, ones, zeros
    prev_b = xs.astype(BF16)
    weights = []
    for l in range(L):
        g_in, g_o = _ag_finish(gather[0], [] if l == 0 else [prev_b], "%da" % l)
        proj = _mm(prev_b, g_in, "nt", F32, 512, IN_W, D_MODEL, "mm_proj", deps=[gather[1][4]] if l == 0 else [])
        mixcat = _mixer_fwd(proj, w_conv_full[l], wpool_bd[l], pool_scale[l][None], sgu_ln_g[l][None], wsp_t[l], bias_t[l])
        xhat1, rstd1, h_b = _mm_ln_fwd(mixcat, g_o, prev, pg, pb, ln1_g[l][None], ln1_b[l][None], "mm_wo_ln")
        g_gu, g_dn = _ag_finish(gather[1], [h_b], "%db" % l)
        weights.append((g_in, g_gu, g_o, g_dn))
        deps = []
        if l + 1 < L:
            gather = _ag_start_layer(shards, l + 1, [g_gu])
            deps = [gather[1][4]]
        g_act, u_act, act = _mm_swiglu_fwd(h_b, g_gu, deps=deps)
        xhat2, rstd2, y_b = _mm_ln_fwd(act, g_dn, xhat1, ln1_g[l][None], ln1_b[l][None], ln2_g[l][None], ln2_b[l][None],
                                       "mm_down_ln")
        saved.append((prev_b, proj, mixcat, xhat1, rstd1, h_b, g_act, u_act, act, xhat2, rstd2))
        prev, pg, pb, prev_b = xhat2, ln2_g[l][None], ln2_b[l][None], y_b

    loss_tile, dy = _loss_head(prev, pg, pb, target)

    small = [None] * L
    big = None
    sibling_flight = None
    above = None
    for l in reversed(range(L)):
        prev_b, proj, mixcat, xhat1, rstd1, h_b, g_act, u_act, act, xhat2, rstd2 = saved[l]
        g_in, g_gu, g_o, g_dn = weights[l]
        chip_flight = None
        if above is None:
            dr2, dr2_b, dg2, db2 = _ln_bwd(None, dy, xhat2, rstd2, ln2_g[l][None])
        else:
            dr2, dr2_b, dg2, db2 = _mm_ln_bwd([above[0]], above[1], above[2], xhat2, rstd2, ln2_g[l][None],
                                              "mm_dx_ln", deps=[sibling_flight[4]])
            chip_flight = _rs_continue(sibling_flight, [dr2_b], c_arr, str(l + 1))
        dg_b, du_b = _mm_swiglu_bwd(dr2_b, g_dn, g_act, u_act, deps=[chip_flight[4]] if chip_flight else [])
        p_dn = _mm(act, dr2_b, "tn", BF16, DW_TM, D_MODEL, T, "mm_dw_down")
        p_gu = _mm(dg_b, h_b, "tn", BF16, DW_TM, D_MODEL, T, "mm_dw_gate", out_rows=2 * D_FF)
        p_gu = _mm(du_b, h_b, "tn", BF16, DW_TM, D_MODEL, T, "mm_dw_up", out_rows=2 * D_FF, out_off=D_FF, out_into=p_gu)
        ffn_sibling = _rs_begin([p_gu, p_dn], "0b") if l == 0 else None
        dr1, dr1_b, dg1, db1 = _mm_ln_bwd([dg_b, du_b], g_gu, dr2, xhat1, rstd1, ln1_g[l][None], "mm_dh_ln",
                                          deps=[ffn_sibling[4]] if l == 0 else [])
        ffn_flight = _rs_continue(ffn_sibling, [dr1_b], c_arr, "0b") if l == 0 else None
        dmix = _mm(dr1_b, g_o, "nt", F32, T, 512, D_MODEL, "mm_dmix", deps=[ffn_flight[4]] if l == 0 else [])
        p_o = _mm(mixcat, dr1_b, "tn", BF16, 512, D_MODEL, T, "mm_dw_o")
        dproj, dwc, dwp, dps, dlng, dwsp, dbias = _mixer_bwd(
            proj, dmix, w_conv_full[l], wpool_bd[l], pool_scale[l][None], sgu_ln_g[l][None], wsp_t[l], bias_t[l])
        p_in = _mm(dproj, prev_b, "tn", BF16, IN_W, D_MODEL, T, "mm_dw_in")
        small[l] = (dwc, dwp, dps, dlng, dwsp, dbias, dg1, db1, dg2, db2)
        above = (dproj, g_in, dr1)
        if chip_flight is not None:
            big = list(_rs_chip_finish(chip_flight, [p_in], q_arr, str(l + 1), l + 1, big))
        if l > 0:
            sibling_flight = _rs_begin([p_in, p_gu, p_o, p_dn], str(l))
        else:
            big[1], big[3] = _rs_chip_finish(ffn_flight, [p_in], q_arr, "0b", 0, [big[1], big[3]])

    def stack(i):
        return jnp.stack([small[l][i] for l in range(L)])

    dwp_bd = stack(1).reshape(L, 2, 2, HALF, 2, HALF)
    dwp_all = jnp.einsum("ltgchd,gh->ltgcd", dwp_bd, eye2).reshape(L, 4, HALF, HALF)
    dbs_all = jnp.swapaxes(stack(5)[:, :, :, :2], 2, 3).reshape(L, 6, CHUNK)
    small_grads = [stack(0), dwp_all, stack(2).reshape(L, POOL_W), stack(3).reshape(L, SGU_W),
                   stack(4).reshape(L, 6, CHUNK, CHUNK), dbs_all] + [stack(i).reshape(L, D_MODEL) for i in (6, 7, 8, 9)]
    packed_small = _allreduce_small(_pack(small_grads), deps=[big[1]])
    sibling_flight = _rs_begin([p_in, p_o], "0a", after=[packed_small])
    grad_x = _mm_ln_bwd([above[0]], above[1], above[2], None, None, None, "mm_dx_out", deps=[sibling_flight[4]])
    last_flight = _rs_continue(sibling_flight, [grad_x], c_arr, "0a")
    return loss_tile, grad_x, (big, last_flight, q_arr), (packed_small, [a.shape for a in small_grads])


def _rs_chip_finish(in_flight, after, q, tag, layer, into):
    send_sems, recv_sems, sums, lands, _ = in_flight
    sums, got = _rs_chip_wait(send_sems, recv_sems, sums, lands, after, tag)
    return _rs_finish(sums, got, q, layer, into)


def _reduce_and_update(big_grads, small_grads, big_w, big_m, big_v, small_w, small_m, small_v):
    L = DEPTH
    mx, my, mc = _my_place()
    dev = 4 * mx + 2 * my + mc
    conv_cols = CONV_W // N_DEV
    w_in, w_gate_up, w_o, w_down = big_w
    m_w_in, m_w_gate_up, m_w_o, m_w_down = big_m
    v_w_in, v_w_gate_up, v_w_o, v_w_down = big_v
    packed_g, small_shapes = small_grads
    big, last_flight, q_arr = big_grads

    def widen_conv(a):
        return lax.dynamic_update_slice(jnp.zeros((L, 3, CONV_W), F32), a, (0, 0, dev * conv_cols))

    small_m = [widen_conv(small_m[0])] + list(small_m[1:])
    small_v = [widen_conv(small_v[0])] + list(small_v[1:])
    pk_d, pk_m, pk_v = _adamw(_pack(small_w), packed_g, _pack(small_m), _pack(small_v), packed_g.shape[0] // 2)
    sg = _unpack(packed_g, small_shapes)
    sd = _unpack(pk_d, small_shapes)
    sm = _unpack(pk_m, small_shapes)
    sv = _unpack(pk_v, small_shapes)

    def conv_cols_of(a):
        return lax.dynamic_slice(a, (0, 0, dev * conv_cols), (L, 3, conv_cols))

    for lst in (sg, sd, sm, sv):
        lst[0] = conv_cols_of(lst[0])

    tr = lambda a: jnp.swapaxes(a, 1, 2)
    gt_gu, g_w_dn = big[1], big[3]
    d_gu, m_gu, v_gu = [tr(a) for a in _adamw(tr(w_gate_up), gt_gu, tr(m_w_gate_up), tr(v_w_gate_up), gt_gu.shape[1] // 2)]
    d_dn, m_dn, v_dn = _adamw(w_down, g_w_dn, m_w_down, v_w_down, 352)
    gt_in, g_w_o = _rs_chip_finish(last_flight, [d_gu, d_dn, pk_d], q_arr, "0a", 0, [big[0], big[2]])
    d_in, m_in, v_in = [tr(a) for a in _adamw(tr(w_in), gt_in, tr(m_w_in), tr(v_w_in), gt_in.shape[1])]
    d_o, m_o, v_o = _adamw(w_o, g_w_o, m_w_o, v_w_o, 128)
    g_w_in, g_w_gu = tr(gt_in), tr(gt_gu)

    def ordered(big_in, big_o, big_gu, big_dn, sm_list):
        return [big_in, sm_list[0], sm_list[1], sm_list[2], sm_list[3], sm_list[4], sm_list[5], big_o,
                sm_list[6], sm_list[7], big_gu, big_dn, sm_list[8], sm_list[9]]

    grads = ordered(g_w_in, g_w_o, g_w_gu, g_w_dn, sg)
    deltas = ordered(d_in, d_o, d_gu, d_dn, sd)
    new_m = ordered(m_in, m_o, m_gu, m_dn, sm)
    new_v = ordered(v_in, v_o, v_gu, v_dn, sv)
    return grads, deltas, new_m, new_v
```

```python
import functools
import math

import jax
import jax.numpy as jnp
from jax import lax
from jax.experimental import pallas as pl
from jax.experimental.pallas import tpu as pltpu

F32 = jnp.float32
BF16 = jnp.bfloat16
MESH = pl.DeviceIdType.MESH

D_MODEL = 1024
DEPTH = 4
CONV_W = 384
POOL_W = 256
SGU_W = 384
IN_W = 3 * CONV_W + POOL_W + 2 * SGU_W
D_FF = 2816
CHUNK = 128
ALPHA = float((2 * DEPTH) ** 0.25)
LN_EPS = 1e-5
ADAM_LR, ADAM_B1, ADAM_B2, ADAM_EPS, ADAM_WD, ADAM_STEP = 0.001, 0.9, 0.999, 1e-08, 0.01, 10

N_DEV = 8
LANES = 128
HALF = 64
SHARD_ROWS = (IN_W // N_DEV, 2 * D_FF // N_DEV, D_MODEL // N_DEV, D_FF // N_DEV)
VMEM_LIMIT = 52 * 1024 * 1024

INV_SQRT2 = 0.7071067811865476
INV_SQRT_2PI = 0.3989422804014327


def _cparams(sem=None, **kw):
    if sem is not None:
        kw["dimension_semantics"] = sem
    return pltpu.CompilerParams(vmem_limit_bytes=VMEM_LIMIT, **kw)


_DN = {"nn": (((1,), (0,)), ((), ())), "nt": (((1,), (1,)), ((), ())), "tn": (((0,), (0,)), ((), ()))}


def _mm(a, b, mode, out_dtype, tm, tn, tk, name, deps=(), out_rows=None, out_off=0, out_into=None):
    if mode == "nn":
        (M, K), N = a.shape, b.shape[1]
    elif mode == "nt":
        (M, K), N = a.shape, b.shape[0]
    else:
        (K, M), N = a.shape, b.shape[1]
    assert M % tm == 0 and N % tn == 0 and K % tk == 0 and out_off % tm == 0, (M, N, K, tm, tn, tk)
    nk = K // tk
    if out_into is not None:
        deps = tuple(deps) + (out_into,)
    nd = len(deps)
    row_off = out_off // tm

    def body(*refs):
        a_ref, b_ref, o_ref = refs[0], refs[1], refs[2 + nd]
        acc_ref = refs[3 + nd] if nk > 1 else None
        p = lax.dot_general(a_ref[...], b_ref[...], _DN[mode], preferred_element_type=F32)
        if nk == 1:
            o_ref[...] = p.astype(o_ref.dtype)
        else:
            k = pl.program_id(2)

            @pl.when(k == 0)
            def _():
                acc_ref[...] = p

            @pl.when(k > 0)
            def _():
                acc_ref[...] += p

            @pl.when(k == nk - 1)
            def _():
                o_ref[...] = acc_ref[...].astype(o_ref.dtype)

    if mode == "nn":
        a_spec = pl.BlockSpec((tm, tk), lambda i, j, k: (i, k))
        b_blk, b_idx = (tk, tn), (lambda i, j, k: (k, j))
    elif mode == "nt":
        a_spec = pl.BlockSpec((tm, tk), lambda i, j, k: (i, k))
        b_blk, b_idx = (tn, tk), (lambda i, j, k: (j, k))
    else:
        a_spec = pl.BlockSpec((tk, tm), lambda i, j, k: (k, i))
        b_blk, b_idx = (tk, tn), (lambda i, j, k: (k, j))
    return pl.pallas_call(
        body,
        name=name,
        grid=(M // tm, N // tn, nk),
        in_specs=[a_spec, pl.BlockSpec(b_blk, b_idx)] + [pl.BlockSpec(memory_space=pl.ANY)] * nd,
        out_specs=pl.BlockSpec((tm, tn), lambda i, j, k: (i + row_off, j)),
        out_shape=jax.ShapeDtypeStruct((out_rows or M, N), out_dtype),
        scratch_shapes=[pltpu.VMEM((tm, tn), F32)] if nk > 1 else [],
        input_output_aliases={1 + nd: 0} if out_into is not None else {},
        compiler_params=_cparams(("parallel", "parallel", "arbitrary")),
    )(a, b, *deps)


def _mm_tn_pair(a1, a2, b, tm, name):
    K, M = a1.shape
    N = b.shape[1]
    n1 = M // tm

    def body(a1_ref, a2_ref, b_ref, o_ref):
        i = pl.program_id(0)

        @pl.when(i < n1)
        def _():
            o_ref[...] = lax.dot_general(a1_ref[...], b_ref[...], _DN["tn"], preferred_element_type=F32).astype(o_ref.dtype)

        @pl.when(i >= n1)
        def _():
            o_ref[...] = lax.dot_general(a2_ref[...], b_ref[...], _DN["tn"], preferred_element_type=F32).astype(o_ref.dtype)

    return pl.pallas_call(
        body, name=name, grid=(2 * n1,),
        in_specs=[pl.BlockSpec((K, tm), lambda i: (0, jnp.minimum(i, n1 - 1))),
                  pl.BlockSpec((K, tm), lambda i: (0, jnp.maximum(i - n1, 0))),
                  pl.BlockSpec((K, N), lambda i: (0, 0))],
        out_specs=pl.BlockSpec((tm, N), lambda i: (i, 0)),
        out_shape=jax.ShapeDtypeStruct((2 * M, N), BF16),
        compiler_params=_cparams(("arbitrary",)),
    )(a1, a2, b)


LN_SUB = 256
LN_TM = 512


def _mm_ln_fwd(a, b, prev, pg, pb, g, bias, name):
    T, K = a.shape
    tm = LN_TM

    def body(a_ref, b_ref, prev_ref, pg_ref, pb_ref, g_ref, bias_ref, xhat_ref, rstd_ref, y_ref):
        for s in range(tm // LN_SUB):
            rows = slice(s * LN_SUB, (s + 1) * LN_SUB)
            mm = jnp.dot(a_ref[rows, :], b_ref[...], preferred_element_type=F32)
            r = ALPHA * (prev_ref[rows, :] * pg_ref[...] + pb_ref[...]) + mm
            mu = jnp.mean(r, axis=-1, keepdims=True)
            xc = r - mu
            var = jnp.mean(xc * xc, axis=-1, keepdims=True)
            rstd = lax.rsqrt(var + LN_EPS)
            xhat = xc * rstd
            xhat_ref[rows, :] = xhat
            rstd_ref[rows, :] = rstd
            y_ref[rows, :] = (xhat * g_ref[...] + bias_ref[...]).astype(y_ref.dtype)

    row = pl.BlockSpec((tm, D_MODEL), lambda i: (i, 0))
    vec = pl.BlockSpec((1, D_MODEL), lambda i: (0, 0))
    return pl.pallas_call(
        body, name=name, grid=(T // tm,),
        in_specs=[pl.BlockSpec((tm, K), lambda i: (i, 0)),
                  pl.BlockSpec((K, D_MODEL), lambda i: (0, 0), pipeline_mode=pl.Buffered(1)),
                  row, vec, vec, vec, vec],
        out_specs=[row, pl.BlockSpec((tm, 1), lambda i: (i, 0)), row],
        out_shape=[jax.ShapeDtypeStruct((T, D_MODEL), F32), jax.ShapeDtypeStruct((T, 1), F32),
                   jax.ShapeDtypeStruct((T, D_MODEL), BF16)],
        compiler_params=_cparams(("parallel",)),
    )(a, b, prev, pg, pb, g, bias)


def _mm_ln_bwd(a_list, b, dres, xhat, rstd, g, name, deps=()):
    T = a_list[0].shape[0]
    tm = LN_TM
    na, nd = len(a_list), len(deps)
    ks = [a.shape[1] for a in a_list]
    last = xhat is None

    def body(*refs):
        a_refs, b_ref, dres_ref = refs[:na], refs[na], refs[na + 1]
        if not last:
            xhat_ref, rstd_ref, g_ref = refs[na + 2:na + 5]
            dr_ref, drb_ref, dg_ref, db_ref = refs[-4:]

            @pl.when(pl.program_id(0) == 0)
            def _():
                dg_ref[...] = jnp.zeros_like(dg_ref)
                db_ref[...] = jnp.zeros_like(db_ref)

        for s in range(tm // LN_SUB):
            rows = slice(s * LN_SUB, (s + 1) * LN_SUB)
            mm, off = None, 0
            for a_ref, k in zip(a_refs, ks):
                part = jnp.dot(a_ref[rows, :], b_ref[off:off + k, :], preferred_element_type=F32)
                mm = part if mm is None else mm + part
                off += k
            dy = ALPHA * dres_ref[rows, :] + mm
            if last:
                refs[-1][rows, :] = dy
                continue
            xhat_v = xhat_ref[rows, :]
            dg_ref[...] += jnp.sum(dy * xhat_v, axis=0, keepdims=True)
            db_ref[...] += jnp.sum(dy, axis=0, keepdims=True)
            dxh = dy * g_ref[...]
            m1 = jnp.mean(dxh, axis=-1, keepdims=True)
            m2 = jnp.mean(dxh * xhat_v, axis=-1, keepdims=True)
            dr = rstd_ref[rows, :] * (dxh - m1 - xhat_v * m2)
            dr_ref[rows, :] = dr
            drb_ref[rows, :] = dr.astype(drb_ref.dtype)

    row = pl.BlockSpec((tm, D_MODEL), lambda i: (i, 0))
    vec = pl.BlockSpec((1, D_MODEL), lambda i: (0, 0))
    in_specs = [pl.BlockSpec((tm, k), lambda i: (i, 0)) for k in ks]
    in_specs += [pl.BlockSpec((sum(ks), D_MODEL), lambda i: (0, 0), pipeline_mode=pl.Buffered(1)), row]
    args = list(a_list) + [b, dres]
    if last:
        out_specs, out_shape = row, jax.ShapeDtypeStruct((T, D_MODEL), F32)
    else:
        in_specs += [row, pl.BlockSpec((tm, 1), lambda i: (i, 0)), vec]
        args += [xhat, rstd, g]
        out_specs = [row, row, vec, vec]
        out_shape = [jax.ShapeDtypeStruct((T, D_MODEL), F32), jax.ShapeDtypeStruct((T, D_MODEL), BF16),
                     jax.ShapeDtypeStruct((1, D_MODEL), F32), jax.ShapeDtypeStruct((1, D_MODEL), F32)]
    return pl.pallas_call(
        body, name=name, grid=(T // tm,),
        in_specs=in_specs + [pl.BlockSpec(memory_space=pl.ANY)] * nd,
        out_specs=out_specs, out_shape=out_shape,
        compiler_params=_cparams(("parallel",) if last else ("arbitrary",)),
    )(*args, *deps)


DW_TM = 1408
FF_TN = 256
SAVED_GU = BF16


def _mm_swiglu_fwd(h, w_gu, deps=()):
    T = h.shape[0]
    nj = D_FF // FF_TN
    nd = len(deps)

    def body(*refs):
        h_ref, wg_ref, wu_ref = refs[:3]
        g_ref, u_ref, act_ref = refs[3 + nd:]
        hv = h_ref[...]
        gv = lax.dot_general(hv, wg_ref[...], _DN["nt"], preferred_element_type=F32)
        uv = lax.dot_general(hv, wu_ref[...], _DN["nt"], preferred_element_type=F32)
        g_ref[...] = gv.astype(g_ref.dtype)
        u_ref[...] = uv.astype(u_ref.dtype)
        act_ref[...] = (gv * jax.nn.sigmoid(gv) * uv).astype(act_ref.dtype)

    col = pl.BlockSpec((T, FF_TN), lambda j: (0, j))
    return pl.pallas_call(
        body, name="mm_gate_up_swiglu", grid=(nj,),
        in_specs=[pl.BlockSpec((T, D_MODEL), lambda j: (0, 0)),
                  pl.BlockSpec((FF_TN, D_MODEL), lambda j: (j, 0)),
                  pl.BlockSpec((FF_TN, D_MODEL), lambda j: (j + nj, 0))] + [pl.BlockSpec(memory_space=pl.ANY)] * nd,
        out_specs=[col, col, col],
        out_shape=[jax.ShapeDtypeStruct((T, D_FF), SAVED_GU), jax.ShapeDtypeStruct((T, D_FF), SAVED_GU),
                   jax.ShapeDtypeStruct((T, D_FF), BF16)],
        compiler_params=_cparams(("parallel",)),
    )(h, w_gu, w_gu, *deps)


def _mm_swiglu_bwd(dr, w_dn, g, u, deps=()):
    T = dr.shape[0]

    def body(*refs):
        dr_ref, w_ref, g_ref, u_ref = refs[:4]
        dg_ref, du_ref = refs[-2:]
        da = lax.dot_general(dr_ref[...], w_ref[...], _DN["nt"], preferred_element_type=F32)
        gv, uv = g_ref[...].astype(F32), u_ref[...].astype(F32)
        s = jax.nn.sigmoid(gv)
        du_ref[...] = (da * (gv * s)).astype(du_ref.dtype)
        dg_ref[...] = (da * uv * (s * (1.0 + gv * (1.0 - s)))).astype(dg_ref.dtype)

    col = pl.BlockSpec((T, FF_TN), lambda j: (0, j))
    return pl.pallas_call(
        body, name="mm_dact_swiglu", grid=(D_FF // FF_TN,),
        in_specs=[pl.BlockSpec((T, D_MODEL), lambda j: (0, 0)), pl.BlockSpec((FF_TN, D_MODEL), lambda j: (j, 0)),
                  col, col] + [ANY] * len(deps),
        out_specs=[col, col],
        out_shape=[jax.ShapeDtypeStruct((T, D_FF), BF16)] * 2,
        compiler_params=_cparams(("parallel",)),
    )(dr, w_dn, g, u, *deps)


def _gelu(x):
    return 0.5 * x * (1.0 + lax.erf(x * INV_SQRT2))


def _gelu_grad(x):
    return 0.5 * (1.0 + lax.erf(x * INV_SQRT2)) + x * (jnp.exp(-0.5 * x * x) * INV_SQRT_2PI)


def _shift_down(z, k):
    row = lax.broadcasted_iota(jnp.int32, z.shape, 0)
    return jnp.where(row >= k, pltpu.roll(z, k, 0), 0.0)


def _shift_up(z, k):
    n = z.shape[0]
    row = lax.broadcasted_iota(jnp.int32, z.shape, 0)
    return jnp.where(row < n - k, pltpu.roll(z, n - k, 0), 0.0)


def _lo_mask(shape):
    return lax.broadcasted_iota(jnp.int32, shape, len(shape) - 1) < HALF


def _seg_mean(x, lo):
    a = jnp.sum(jnp.where(lo, x, 0.0), axis=-1, keepdims=True)
    b = jnp.sum(jnp.where(lo, 0.0, x), axis=-1, keepdims=True)
    return jnp.where(lo, a, b) * (1.0 / HALF)


def _pool_windows(first):
    lo = _lo_mask((1, LANES))
    return jnp.where(first, jnp.where(lo, 2.0, 4.0), jnp.where(lo, 8.0, 16.0)), lo


def _pool_mean_minus_token(p, first):
    wl, lo = _pool_windows(first)
    s2 = p + _shift_down(p, 1)
    s4 = s2 + _shift_down(s2, 2)
    s8 = s4 + _shift_down(s4, 4)
    s16 = s8 + _shift_down(s8, 8)
    win = jnp.where(first, jnp.where(lo, s2, s4), jnp.where(lo, s8, s16))
    t1 = (lax.broadcasted_iota(jnp.int32, p.shape, 0) + 1).astype(F32)
    count = jnp.minimum(t1, wl)
    return win / count - p, count


SGU_UNROLL = 8


def _tril_keep():
    r = lax.broadcasted_iota(jnp.int32, (2 * CHUNK, CHUNK), 0)
    s = lax.broadcasted_iota(jnp.int32, (2 * CHUNK, CHUNK), 1)
    return s <= (r & (CHUNK - 1))


def _sgu_chunk_fwd(u, v, g, wm, bias, lo):
    ug = _gelu(u)
    vg = _gelu(v)
    mu = _seg_mean(vg, lo)
    xc = vg - mu
    var = _seg_mean(xc * xc, lo)
    rstd = lax.rsqrt(var + LN_EPS)
    vn = xc * rstd
    vh = (vn * g).astype(BF16)
    mm2 = jnp.dot(wm, vh, preferred_element_type=F32)
    mixed = jnp.where(lo, mm2[:CHUNK], mm2[CHUNK:]) + bias
    return ug, vn, rstd, vh, mixed


def _mixer_fwd(proj, wconv, wpool_bd, pscale, lng, wsp, bias):
    T = proj.shape[0]
    nchunk = T // CHUNK

    def body(a_ref, b_ref, c_ref, wc_ref, wp_ref, ps_ref, lng_ref, wsp_ref, bias_ref, o_ref):
        j = pl.program_id(0)

        @pl.when(j < 3)
        def _conv():
            z = c_ref[...] * a_ref[...]
            w = wc_ref[...]
            y = w[0:1] * _shift_down(z, 2) + w[1:2] * _shift_down(z, 1) + w[2:3] * z
            o_ref[...] = (b_ref[...] * y).astype(o_ref.dtype)

        @pl.when((j >= 3) & (j < 5))
        def _pool():
            d, _ = _pool_mean_minus_token(a_ref[...], j == 3)
            y = jnp.dot(d.astype(BF16), wp_ref[...].astype(BF16), preferred_element_type=F32)
            o_ref[...] = (y * ps_ref[...]).astype(o_ref.dtype)

        @pl.when(j >= 5)
        def _sgu():
            lo = _lo_mask((CHUNK, LANES))
            wm = jnp.where(_tril_keep(), wsp_ref[...], 0.0).astype(BF16)
            bias_t = bias_ref[...]
            g = lng_ref[...]

            def chunk(n, carry):
                rows = pl.ds(pl.multiple_of(n * CHUNK, CHUNK), CHUNK)
                ug, _, _, _, mixed = _sgu_chunk_fwd(a_ref[rows, :], b_ref[rows, :], g, wm, bias_t, lo)
                o_ref[rows, :] = (ug * mixed).astype(o_ref.dtype)
                return carry

            lax.fori_loop(0, nchunk, chunk, 0, unroll=SGU_UNROLL)

    def col(f):
        return lambda j: (0, f(j))

    clip = lambda v, lo, hi: jnp.minimum(jnp.maximum(v, lo), hi)
    return pl.pallas_call(
        body,
        name="mixer_fwd",
        grid=(8,),
        in_specs=[
            pl.BlockSpec((T, LANES), col(lambda j: jnp.where(j < 3, j, jnp.where(j < 5, j + 6, j + 6)))),
            pl.BlockSpec((T, LANES), col(lambda j: jnp.where(j < 3, j + 3, jnp.where(j < 5, 5, j + 9)))),
            pl.BlockSpec((T, LANES), col(lambda j: jnp.where(j < 3, j + 6, 8))),
            pl.BlockSpec((3, LANES), col(lambda j: clip(j, 0, 2))),
            pl.BlockSpec((None, LANES, LANES), lambda j: (clip(j - 3, 0, 1), 0, 0)),
            pl.BlockSpec((1, LANES), col(lambda j: clip(j - 3, 0, 1))),
            pl.BlockSpec((1, LANES), col(lambda j: clip(j - 5, 0, 2))),
            pl.BlockSpec((None, 2 * CHUNK, CHUNK), lambda j: (clip(j - 5, 0, 2), 0, 0)),
            pl.BlockSpec((None, CHUNK, LANES), lambda j: (clip(j - 5, 0, 2), 0, 0)),
        ],
        out_specs=pl.BlockSpec((T, LANES), lambda j: (0, j)),
        out_shape=jax.ShapeDtypeStruct((T, D_MODEL), BF16),
        compiler_params=_cparams(("arbitrary",)),
    )(proj, proj, proj, wconv, wpool_bd, pscale, lng, wsp, bias)


def _mixer_bwd(proj, dmix, wconv, wpool_bd, pscale, lng, wsp, bias):
    T = proj.shape[0]
    nchunk = T // CHUNK

    def body(a_ref, b_ref, c_ref, dm_ref, wc_ref, wp_ref, ps_ref, lng_ref, wsp_ref, bias_ref,
             o_ref, dwc_ref, dwp_ref, dps_ref, dlng_ref, dwsp_ref, dbias_ref, keep1, keep2):
        k = pl.program_id(0)

        @pl.when(k < 3)
        def _conv():
            xa, gb, gc, dya = a_ref[...], b_ref[...], c_ref[...], dm_ref[...]
            w = wc_ref[...]
            z = gc * xa
            z1 = _shift_down(z, 1)
            z2 = _shift_down(z, 2)
            y = w[0:1] * z2 + w[1:2] * z1 + w[2:3] * z
            dyv = dya * gb
            dz = w[2:3] * dyv + w[1:2] * _shift_up(dyv, 1) + w[0:1] * _shift_up(dyv, 2)
            dwc_ref[0:1, :] = jnp.sum(dyv * z2, axis=0, keepdims=True)
            dwc_ref[1:2, :] = jnp.sum(dyv * z1, axis=0, keepdims=True)
            dwc_ref[2:3, :] = jnp.sum(dyv * z, axis=0, keepdims=True)
            o_ref[...] = (dz * gc).astype(o_ref.dtype)
            keep1[k] = (dya * y).astype(keep1.dtype)
            keep1[k + 3] = (dz * xa).astype(keep1.dtype)

        @pl.when((k >= 3) & (k < 9))
        def _emit_gb_gc():
            o_ref[...] = keep1[k - 3]

        @pl.when((k >= 9) & (k < 11))
        def _pool():
            first = k == 9
            p, dyb = a_ref[...], dm_ref[...]
            d, count = _pool_mean_minus_token(p, first)
            w2 = wp_ref[...].astype(BF16)
            db = d.astype(BF16)
            y = jnp.dot(db, w2, preferred_element_type=F32)
            dps_ref[...] = jnp.sum(dyb * y, axis=0, keepdims=True)
            dyv = (dyb * ps_ref[...]).astype(BF16)
            dd = lax.dot_general(dyv, w2, _DN["nt"], preferred_element_type=F32)
            dwp_ref[...] = lax.dot_general(db, dyv, _DN["tn"], preferred_element_type=F32)
            dwin = dd / count
            a2 = dwin + _shift_up(dwin, 1)
            a4 = a2 + _shift_up(a2, 2)
            a8 = a4 + _shift_up(a4, 4)
            a16 = a8 + _shift_up(a8, 8)
            _, lo = _pool_windows(first)
            back = jnp.where(first, jnp.where(lo, a2, a4), jnp.where(lo, a8, a16))
            o_ref[...] = (back - dd).astype(o_ref.dtype)

        @pl.when((k >= 11) & (k < 14))
        def _sgu():
            lo = _lo_mask((CHUNK, LANES))
            keep = _tril_keep()
            wm = jnp.where(keep, wsp_ref[...], 0.0).astype(BF16)
            bias_t = bias_ref[...]
            g = lng_ref[...]
            dwsp_ref[...] = jnp.zeros_like(dwsp_ref)
            dbias_ref[...] = jnp.zeros_like(dbias_ref)
            dlng_ref[...] = jnp.zeros_like(dlng_ref)

            def chunk(n, carry):
                rows = pl.ds(pl.multiple_of(n * CHUNK, CHUNK), CHUNK)
                u, v, dyc = a_ref[rows, :], b_ref[rows, :], dm_ref[rows, :]
                ug, vn, rstd, vh, mixed = _sgu_chunk_fwd(u, v, g, wm, bias_t, lo)
                dmx = dyc * ug
                o_ref[rows, :] = (dyc * mixed * _gelu_grad(u)).astype(o_ref.dtype)
                dbias_ref[...] += dmx
                dst = jnp.concatenate([jnp.where(lo, dmx, 0.0), jnp.where(lo, 0.0, dmx)], axis=0).astype(BF16)
                dwsp_ref[...] += lax.dot_general(dst, vh, _DN["nt"], preferred_element_type=F32)
                dvh = lax.dot_general(wm, dst, _DN["tn"], preferred_element_type=F32)
                dlng_ref[...] += jnp.sum(dvh * vn, axis=0, keepdims=True)
                dvn = dvh * g
                m1 = _seg_mean(dvn, lo)
                m2 = _seg_mean(dvn * vn, lo)
                dvg = rstd * (dvn - m1 - vn * m2)
                keep2[k - 11, rows, :] = (dvg * _gelu_grad(v)).astype(keep2.dtype)
                return carry

            lax.fori_loop(0, nchunk, chunk, 0, unroll=SGU_UNROLL)
            dwsp_ref[...] = jnp.where(keep, dwsp_ref[...], 0.0)
            dbt = dbias_ref[...]
            lane = lax.broadcasted_iota(jnp.int32, (CHUNK, LANES), 1)
            sa = jnp.sum(jnp.where(lo, dbt, 0.0), axis=-1, keepdims=True)
            sb = jnp.sum(jnp.where(lo, 0.0, dbt), axis=-1, keepdims=True)
            dbias_ref[...] = jnp.where(lane == 0, sa, jnp.where(lane == 1, sb, 0.0))

        @pl.when(k >= 14)
        def _emit_v():
            o_ref[...] = keep2[k - 14]

    def col(f):
        return lambda k: (0, f(k))

    clip = lambda v, lo, hi: jnp.minimum(jnp.maximum(v, lo), hi)
    view_a = lambda k: jnp.where(k < 3, k, jnp.where(k < 9, 2, jnp.where(k < 14, k, 13)))
    view_b = lambda k: jnp.where(k < 3, k + 3, jnp.where(k < 11, 5, jnp.where(k < 14, k + 3, 16)))
    view_c = lambda k: jnp.where(k < 3, k + 6, 8)
    view_dm = lambda k: jnp.where(k < 3, k, jnp.where(k < 9, 2, jnp.where(k < 14, k - 6, 7)))
    return pl.pallas_call(
        body,
        name="mixer_bwd",
        grid=(17,),
        in_specs=[
            pl.BlockSpec((T, LANES), col(view_a)),
            pl.BlockSpec((T, LANES), col(view_b)),
            pl.BlockSpec((T, LANES), col(view_c)),
            pl.BlockSpec((T, LANES), col(view_dm)),
            pl.BlockSpec((3, LANES), col(lambda k: clip(k, 0, 2))),
            pl.BlockSpec((None, LANES, LANES), lambda k: (clip(k - 9, 0, 1), 0, 0)),
            pl.BlockSpec((1, LANES), col(lambda k: clip(k - 9, 0, 1))),
            pl.BlockSpec((1, LANES), col(lambda k: clip(k - 11, 0, 2))),
            pl.BlockSpec((None, 2 * CHUNK, CHUNK), lambda k: (clip(k - 11, 0, 2), 0, 0)),
            pl.BlockSpec((None, CHUNK, LANES), lambda k: (clip(k - 11, 0, 2), 0, 0)),
        ],
        out_specs=[
            pl.BlockSpec((T, LANES), lambda k: (0, k)),
            pl.BlockSpec((3, LANES), col(lambda k: clip(k, 0, 2))),
            pl.BlockSpec((None, LANES, LANES), lambda k: (clip(k - 9, 0, 1), 0, 0)),
            pl.BlockSpec((1, LANES), col(lambda k: clip(k - 9, 0, 1))),
            pl.BlockSpec((1, LANES), col(lambda k: clip(k - 11, 0, 2))),
            pl.BlockSpec((None, 2 * CHUNK, CHUNK), lambda k: (clip(k - 11, 0, 2), 0, 0)),
            pl.BlockSpec((None, CHUNK, LANES), lambda k: (clip(k - 11, 0, 2), 0, 0)),
        ],
        out_shape=[
            jax.ShapeDtypeStruct((T, IN_W), BF16),
            jax.ShapeDtypeStruct((3, CONV_W), F32),
            jax.ShapeDtypeStruct((2, LANES, LANES), F32),
            jax.ShapeDtypeStruct((1, POOL_W), F32),
            jax.ShapeDtypeStruct((1, SGU_W), F32),
            jax.ShapeDtypeStruct((3, 2 * CHUNK, CHUNK), F32),
            jax.ShapeDtypeStruct((3, CHUNK, LANES), F32),
        ],
        scratch_shapes=[pltpu.VMEM((6, T, LANES), BF16), pltpu.VMEM((3, T, LANES), BF16)],
        compiler_params=_cparams(("arbitrary",)),
    )(proj, proj, proj, dmix, wconv, wpool_bd, pscale, lng, wsp, bias)


def _ln_fwd(prev, pg, pb, mmout, g, b, tm=256):
    T = prev.shape[0]

    def body(prev_ref, pg_ref, pb_ref, mm_ref, g_ref, b_ref, xhat_ref, rstd_ref, y_ref):
        r = ALPHA * (prev_ref[...] * pg_ref[...] + pb_ref[...]) + mm_ref[...]
        mu = jnp.mean(r, axis=-1, keepdims=True)
        xc = r - mu
        var = jnp.mean(xc * xc, axis=-1, keepdims=True)
        rstd = lax.rsqrt(var + LN_EPS)
        xhat = xc * rstd
        xhat_ref[...] = xhat
        rstd_ref[...] = rstd
        y_ref[...] = (xhat * g_ref[...] + b_ref[...]).astype(y_ref.dtype)

    row = pl.BlockSpec((tm, D_MODEL), lambda i: (i, 0))
    vec = pl.BlockSpec((1, D_MODEL), lambda i: (0, 0))
    return pl.pallas_call(
        body,
        name="ln_fwd",
        grid=(T // tm,),
        in_specs=[row, vec, vec, row, vec, vec],
        out_specs=[row, pl.BlockSpec((tm, 1), lambda i: (i, 0)), row],
        out_shape=[jax.ShapeDtypeStruct((T, D_MODEL), F32), jax.ShapeDtypeStruct((T, 1), F32),
                   jax.ShapeDtypeStruct((T, D_MODEL), BF16)],
        compiler_params=_cparams(("parallel",)),
    )(prev, pg, pb, mmout, g, b)


def _ln_bwd(dres, dmm, xhat, rstd, g, tm=256, deps=()):
    T = xhat.shape[0]
    has_res = dres is not None
    nd = len(deps)

    def body(*refs):
        refs = refs[:len(refs) - 4 - nd] + refs[len(refs) - 4:]
        if has_res:
            dres_ref, dmm_ref, xhat_ref, rstd_ref, g_ref, dr_ref, drb_ref, dg_ref, db_ref = refs
            dy = ALPHA * dres_ref[...] + dmm_ref[...]
        else:
            dmm_ref, xhat_ref, rstd_ref, g_ref, dr_ref, drb_ref, dg_ref, db_ref = refs
            dy = dmm_ref[...]
        xhat_v = xhat_ref[...]

        @pl.when(pl.program_id(0) == 0)
        def _():
            dg_ref[...] = jnp.zeros_like(dg_ref)
            db_ref[...] = jnp.zeros_like(db_ref)

        dg_ref[...] += jnp.sum(dy * xhat_v, axis=0, keepdims=True)
        db_ref[...] += jnp.sum(dy, axis=0, keepdims=True)
        dxh = dy * g_ref[...]
        m1 = jnp.mean(dxh, axis=-1, keepdims=True)
        m2 = jnp.mean(dxh * xhat_v, axis=-1, keepdims=True)
        dr = rstd_ref[...] * (dxh - m1 - xhat_v * m2)
        dr_ref[...] = dr
        drb_ref[...] = dr.astype(drb_ref.dtype)

    row = pl.BlockSpec((tm, D_MODEL), lambda i: (i, 0))
    vec = pl.BlockSpec((1, D_MODEL), lambda i: (0, 0))
    in_specs = ([row] if has_res else []) + [row, row, pl.BlockSpec((tm, 1), lambda i: (i, 0)), vec]
    in_specs += [pl.BlockSpec(memory_space=pl.ANY)] * nd
    args = ([dres] if has_res else []) + [dmm, xhat, rstd, g] + list(deps)
    return pl.pallas_call(
        body,
        name="ln_bwd_res" if has_res else "ln_bwd",
        grid=(T // tm,),
        in_specs=in_specs,
        out_specs=[row, row, vec, vec],
        out_shape=[jax.ShapeDtypeStruct((T, D_MODEL), F32), jax.ShapeDtypeStruct((T, D_MODEL), BF16),
                   jax.ShapeDtypeStruct((1, D_MODEL), F32), jax.ShapeDtypeStruct((1, D_MODEL), F32)],
        compiler_params=_cparams(("arbitrary",)),
    )(*args)


def _loss_head(xhat, g, b, target, tm=256):
    T = xhat.shape[0]

    def body(xhat_ref, g_ref, b_ref, t_ref, loss_ref, dy_ref):
        err = xhat_ref[...] * g_ref[...] + b_ref[...] - t_ref[...]

        @pl.when(pl.program_id(0) == 0)
        def _():
            loss_ref[...] = jnp.zeros_like(loss_ref)

        part = jnp.sum(jnp.sum(err * err, axis=-1, keepdims=True), axis=0, keepdims=True)
        loss_ref[...] += jnp.broadcast_to(part * (0.5 / D_MODEL), loss_ref.shape)
        dy_ref[...] = err * (1.0 / D_MODEL)

    row = pl.BlockSpec((tm, D_MODEL), lambda i: (i, 0))
    vec = pl.BlockSpec((1, D_MODEL), lambda i: (0, 0))
    return pl.pallas_call(
        body,
        name="loss_head",
        grid=(T // tm,),
        in_specs=[row, vec, vec, row],
        out_specs=[pl.BlockSpec((8, LANES), lambda i: (0, 0)), row],
        out_shape=[jax.ShapeDtypeStruct((8, LANES), F32), jax.ShapeDtypeStruct((T, D_MODEL), F32)],
        compiler_params=_cparams(("arbitrary",)),
    )(xhat, g, b, target)


def _residual_out(dres, dmm, tm=256):
    T = dres.shape[0]

    def body(a_ref, b_ref, o_ref):
        o_ref[...] = ALPHA * a_ref[...] + b_ref[...]

    row = pl.BlockSpec((tm, D_MODEL), lambda i: (i, 0))
    return pl.pallas_call(
        body, name="residual_out", grid=(T // tm,), in_specs=[row, row], out_specs=row,
        out_shape=jax.ShapeDtypeStruct((T, D_MODEL), F32), compiler_params=_cparams(("parallel",)),
    )(dres, dmm)


SW_TC = 1408


def _swiglu_fwd(gu, tm=128):
    T = gu.shape[0]

    def body(gu_ref, o_ref):
        gv = gu_ref[:, :D_FF]
        o_ref[...] = (gv * jax.nn.sigmoid(gv) * gu_ref[:, D_FF:]).astype(o_ref.dtype)

    return pl.pallas_call(
        body, name="swiglu_fwd", grid=(T // tm,),
        in_specs=[pl.BlockSpec((tm, 2 * D_FF), lambda i: (i, 0))],
        out_specs=pl.BlockSpec((tm, D_FF), lambda i: (i, 0)),
        out_shape=jax.ShapeDtypeStruct((T, D_FF), BF16), compiler_params=_cparams(("parallel",)),
    )(gu)


def _swiglu_bwd(gu, dact, tm=128):
    T = gu.shape[0]

    def body(gu_ref, da_ref, dgu_ref, act_ref):
        gv, uv, da = gu_ref[:, :D_FF], gu_ref[:, D_FF:], da_ref[...]
        s = jax.nn.sigmoid(gv)
        sg = gv * s
        act_ref[...] = (sg * uv).astype(act_ref.dtype)
        dgu_ref[:, D_FF:] = (da * sg).astype(dgu_ref.dtype)
        dgu_ref[:, :D_FF] = (da * uv * (s * (1.0 + gv * (1.0 - s)))).astype(dgu_ref.dtype)

    wide = pl.BlockSpec((tm, 2 * D_FF), lambda i: (i, 0))
    half = pl.BlockSpec((tm, D_FF), lambda i: (i, 0))
    return pl.pallas_call(
        body, name="swiglu_bwd", grid=(T // tm,),
        in_specs=[wide, half], out_specs=[wide, half],
        out_shape=[jax.ShapeDtypeStruct((T, 2 * D_FF), BF16), jax.ShapeDtypeStruct((T, D_FF), BF16)],
        compiler_params=_cparams(("parallel",)),
    )(gu, dact)


def _adamw(w, g, m, v, tr):
    R, C = w.shape[-2:]
    assert R % tr == 0
    c1 = 1.0 - ADAM_B1 ** ADAM_STEP
    c2 = 1.0 - ADAM_B2 ** ADAM_STEP

    def body(w_ref, g_ref, m_ref, v_ref, d_ref, mo_ref, vo_ref):
        gv = g_ref[...]
        mn = ADAM_B1 * m_ref[...] + (1.0 - ADAM_B1) * gv
        vn = ADAM_B2 * v_ref[...] + (1.0 - ADAM_B2) * (gv * gv)
        d_ref[...] = -ADAM_LR * ((mn / c1) / (jnp.sqrt(vn / c2) + ADAM_EPS) + ADAM_WD * w_ref[...])
        mo_ref[...] = mn
        vo_ref[...] = vn

    if w.ndim == 2:
        grid, blk = (R // tr,), pl.BlockSpec((tr, C), lambda i: (i, 0))
    else:
        grid, blk = (w.shape[0], R // tr), pl.BlockSpec((None, tr, C), lambda l, i: (l, i, 0))
    return pl.pallas_call(
        body, name="adamw", grid=grid, in_specs=[blk] * 4, out_specs=[blk] * 3,
        out_shape=[jax.ShapeDtypeStruct(w.shape, F32)] * 3, compiler_params=_cparams(("parallel",) * len(grid)),
    )(w, g, m, v)


def _my_place():
    return lax.axis_index("x"), lax.axis_index("y"), lax.axis_index("c")


ANY = pl.BlockSpec(memory_space=pl.ANY)
HBM = pl.BlockSpec(memory_space=pltpu.HBM)
SEM = pl.BlockSpec(memory_space=pltpu.SEMAPHORE)
EFFECT = pltpu.SideEffectType.DATAFLOW_SIDE_EFFECTING


def _in_hbm(a):
    return pltpu.with_memory_space_constraint(a, pltpu.HBM)


def _block_rows(ref, dev):
    r = ref.shape[0] // N_DEV
    start = pl.multiple_of((4 * dev[0] + 2 * dev[1] + dev[2]) * r, 16)
    return ref.at[pl.ds(start, r), :]


def _ag_first_copies(s_refs, land_refs, send_sems, recv_sems, receiving):
    x, y, c = _my_place()
    peers = [(x, y, 1 - c)] + [(*chip, c) for chip in _other_chips(x, y)]
    copies = []
    for k, peer in enumerate(peers):
        block = peer if receiving else (x, y, c)
        copies += [pltpu.make_async_remote_copy(
            src_ref=s_refs[w], dst_ref=_block_rows(land_refs[w], block),
            send_sem=send_sems.at[k * len(s_refs) + w], recv_sem=recv_sems.at[k * len(s_refs) + w],
            device_id=peer, device_id_type=MESH)
            for w in range(len(s_refs))]
    return copies


def _ag_start(shards, layer, after=()):
    nw = len(shards)

    def body(*refs):
        s_refs, land_refs = refs[:nw], refs[nw:2 * nw]
        token = refs[-1]
        sems = 2 * nw + len(after)
        for cp in _ag_first_copies(s_refs, land_refs, refs[sems], refs[sems + 1], False):
            cp.start()
        token[...] = jnp.zeros_like(token)

    lands = [lax.empty((N_DEV * s.shape[0], D_MODEL), BF16) for s in shards]
    out = pl.pallas_call(
        body, name="ag_start_%s" % layer,
        in_specs=[HBM] * (2 * nw) + [ANY] * len(after),
        out_specs=(SEM, SEM, *[HBM] * (2 * nw), pl.BlockSpec(memory_space=pltpu.VMEM)),
        out_shape=(pltpu.SemaphoreType.DMA((4 * nw,)), pltpu.SemaphoreType.DMA((4 * nw,)),
                   *[pltpu.HBM(a.shape, a.dtype) for a in list(shards) + lands],
                   jax.ShapeDtypeStruct((8, LANES), F32)),
        input_output_aliases={i: 2 + i for i in range(2 * nw)},
        compiler_params=pltpu.CompilerParams(has_side_effects=EFFECT),
    )(*[_in_hbm(a) for a in list(shards) + lands], *after)
    return out[0], out[1], out[2:2 + nw], out[2 + nw:2 + 2 * nw], out[-1]


def _ag_wait(send_sems, recv_sems, shards, lands, after, layer):
    nw = len(shards)

    def body(*refs):
        s_refs, land_refs = refs[:nw], refs[nw:2 * nw]
        for cp in _ag_first_copies(s_refs, land_refs, refs[2 * nw], refs[2 * nw + 1], True):
            cp.wait_send()
            cp.wait_recv()

    out = pl.pallas_call(
        body, name="ag_wait_%s" % layer,
        in_specs=[HBM] * (2 * nw) + [SEM, SEM] + [ANY] * len(after),
        out_specs=[HBM] * (2 * nw),
        out_shape=[pltpu.HBM(a.shape, a.dtype) for a in list(shards) + list(lands)],
        input_output_aliases={i: i for i in range(2 * nw)},
        compiler_params=pltpu.CompilerParams(has_side_effects=EFFECT),
    )(*shards, *lands, send_sems, recv_sems, *after)
    return out[:nw], out[nw:]


def _ag_pass_on(shards, lands):
    nw = len(shards)

    def body(*refs):
        s_refs, g_refs = refs[:nw], refs[2 * nw:3 * nw]
        send_sems, recv_sems, local_sems = refs[3 * nw:3 * nw + 3]
        stage = refs[3 * nw + 3:]
        x, y, c = _my_place()
        load = [pltpu.make_async_copy(s_refs[w], stage[w], local_sems.at[w]) for w in range(nw)]
        mine = [pltpu.make_async_copy(stage[w], _block_rows(g_refs[w], (x, y, c)), local_sems.at[w])
                for w in range(nw)]
        for cp in load:
            cp.start()
        sends, arrivals = [], []
        for j, chip in enumerate(_other_chips(x, y)):
            for w in range(nw):
                rows_out = _block_rows(g_refs[w], (*chip, c))
                rows_in = _block_rows(g_refs[w], (*chip, 1 - c))
                sends.append(pltpu.make_async_remote_copy(
                    src_ref=rows_out, dst_ref=rows_out, send_sem=send_sems.at[j, w], recv_sem=recv_sems.at[j, w],
                    device_id=(x, y, 1 - c), device_id_type=MESH))
                arrivals.append(pltpu.make_async_remote_copy(
                    src_ref=rows_in, dst_ref=rows_in, send_sem=send_sems.at[j, w], recv_sem=recv_sems.at[j, w],
                    device_id=(x, y, 1 - c), device_id_type=MESH))
        for cp in sends:
            cp.start()
        for w in range(nw):
            load[w].wait()
            mine[w].start()
        for cp in arrivals:
            cp.wait_recv()
        for cp in sends:
            cp.wait_send()
        for cp in mine:
            cp.wait()

    return pl.pallas_call(
        body, name="ag_pass_on",
        in_specs=[ANY] * (2 * nw), out_specs=[ANY] * nw,
        out_shape=[jax.ShapeDtypeStruct(a.shape, a.dtype) for a in lands],
        input_output_aliases={nw + i: i for i in range(nw)},
        scratch_shapes=[pltpu.SemaphoreType.DMA((3, nw)), pltpu.SemaphoreType.DMA((3, nw)),
                        pltpu.SemaphoreType.DMA((nw,))] + [pltpu.VMEM(s.shape, s.dtype) for s in shards],
        compiler_params=_cparams(),
    )(*shards, *lands)


def _rs_sibling_copies(p_refs, land_refs, send_sems, recv_sems):
    x, y, c = _my_place()
    return [pltpu.make_async_remote_copy(
        src_ref=p_refs[w].at[:, 1 - c], dst_ref=land_refs[w],
        send_sem=send_sems.at[w], recv_sem=recv_sems.at[w], device_id=(x, y, 1 - c), device_id_type=MESH)
        for w in range(len(p_refs))]


def _rs_sibling_start(parts, tag, after=()):
    nw = len(parts)
    sems = 2 * nw + len(after)

    def body(*refs):
        for cp in _rs_sibling_copies(refs[:nw], refs[nw:2 * nw], refs[sems], refs[sems + 1]):
            cp.start()
        refs[-1][...] = jnp.zeros_like(refs[-1])

    lands = [lax.empty(p.shape[:1] + p.shape[2:], BF16) for p in parts]
    out = pl.pallas_call(
        body, name="rs_sibling_start_%s" % tag,
        in_specs=[HBM] * (2 * nw) + [ANY] * len(after),
        out_specs=(SEM, SEM, *[HBM] * (2 * nw), pl.BlockSpec(memory_space=pltpu.VMEM)),
        out_shape=(pltpu.SemaphoreType.DMA((nw,)), pltpu.SemaphoreType.DMA((nw,)),
                   *[pltpu.HBM(a.shape, a.dtype) for a in list(parts) + lands],
                   jax.ShapeDtypeStruct((8, LANES), F32)),
        input_output_aliases={i: 2 + i for i in range(2 * nw)},
        compiler_params=pltpu.CompilerParams(has_side_effects=EFFECT),
    )(*[_in_hbm(a) for a in list(parts) + lands], *after)
    return out[0], out[1], out[2:2 + nw], out[2 + nw:2 + 2 * nw], out[-1]


def _rs_sibling_wait(send_sems, recv_sems, parts, lands, after, tag):
    nw = len(parts)

    def body(*refs):
        for cp in _rs_sibling_copies(refs[:nw], refs[nw:2 * nw], refs[2 * nw], refs[2 * nw + 1]):
            cp.wait_send()
            cp.wait_recv()

    out = pl.pallas_call(
        body, name="rs_sibling_wait_%s" % tag,
        in_specs=[HBM] * (2 * nw) + [SEM, SEM] + [ANY] * len(after),
        out_specs=[HBM] * (2 * nw),
        out_shape=[pltpu.HBM(a.shape, a.dtype) for a in list(parts) + list(lands)],
        input_output_aliases={i: i for i in range(2 * nw)},
        compiler_params=pltpu.CompilerParams(has_side_effects=EFFECT),
    )(*parts, *lands, send_sems, recv_sems, *after)
    return out[:nw], out[nw:]


def _rs_chip_sum(parts, gots, c):
    n = len(parts)

    def body(c_ref, *refs):
        for p_ref, g_ref, o_ref in zip(refs[:n], refs[n:2 * n], refs[2 * n:]):
            o_ref[...] = (p_ref[...].astype(F32) + g_ref[...].astype(F32)).astype(o_ref.dtype)

    mine = [pl.BlockSpec((None, None, p.shape[2], D_MODEL), lambda q, c_ref: (q, c_ref[0], 0, 0)) for p in parts]
    theirs = [pl.BlockSpec((None, g.shape[1], D_MODEL), lambda q, c_ref: (q, 0, 0)) for g in gots]
    return pl.pallas_call(
        body, name="rs_chip_sum",
        grid_spec=pltpu.PrefetchScalarGridSpec(
            num_scalar_prefetch=1, grid=(4,), in_specs=mine + theirs, out_specs=theirs),
        out_shape=[jax.ShapeDtypeStruct(g.shape, BF16) for g in gots],
        compiler_params=_cparams(("parallel",)),
    )(c, *parts, *gots)


def _other_chips(x, y):
    return [(1 - x, y), (x, 1 - y), (1 - x, 1 - y)]


def _rs_chip_copies(s_refs, land_refs, send_sems, recv_sems):
    x, y, c = _my_place()
    copies = []
    for k, chip in enumerate(_other_chips(x, y)):
        q = 2 * chip[0] + chip[1]
        copies += [pltpu.make_async_remote_copy(
            src_ref=s_refs[w].at[q], dst_ref=land_refs[w].at[k],
            send_sem=send_sems.at[k * len(s_refs) + w], recv_sem=recv_sems.at[k * len(s_refs) + w],
            device_id=(*chip, c), device_id_type=MESH)
            for w in range(len(s_refs))]
    return copies


def _rs_chip_start(sums, layer):
    nw = len(sums)

    def body(*refs):
        s_refs, land_refs = refs[:nw], refs[nw:2 * nw]
        send_sems, recv_sems = refs[2 * nw], refs[2 * nw + 1]
        token = refs[-1]
        for cp in _rs_chip_copies(s_refs, land_refs, send_sems, recv_sems):
            cp.start()
        token[...] = jnp.zeros_like(token)

    lands = [lax.empty((3,) + s.shape[1:], BF16) for s in sums]
    out = pl.pallas_call(
        body, name="rs_chip_start_%s" % layer,
        in_specs=[HBM] * (2 * nw),
        out_specs=(SEM, SEM, *[HBM] * (2 * nw), pl.BlockSpec(memory_space=pltpu.VMEM)),
        out_shape=(pltpu.SemaphoreType.DMA((3 * nw,)), pltpu.SemaphoreType.DMA((3 * nw,)),
                   *[pltpu.HBM(a.shape, a.dtype) for a in list(sums) + lands],
                   jax.ShapeDtypeStruct((8, LANES), F32)),
        input_output_aliases={i: 2 + i for i in range(2 * nw)},
        compiler_params=pltpu.CompilerParams(has_side_effects=EFFECT),
    )(*[_in_hbm(a) for a in list(sums) + lands])
    return out[0], out[1], out[2:2 + nw], out[2 + nw:2 + 2 * nw], out[-1]


def _rs_chip_wait(send_sems, recv_sems, sums, lands, after, layer):
    nw = len(sums)

    def body(*refs):
        s_refs, land_refs = refs[:nw], refs[nw:2 * nw]
        for cp in _rs_chip_copies(s_refs, land_refs, refs[2 * nw], refs[2 * nw + 1]):
            cp.wait_send()
            cp.wait_recv()

    out = pl.pallas_call(
        body, name="rs_chip_wait_%s" % layer,
        in_specs=[HBM] * (2 * nw) + [SEM, SEM] + [ANY] * len(after),
        out_specs=[HBM] * (2 * nw),
        out_shape=[pltpu.HBM(a.shape, a.dtype) for a in list(sums) + list(lands)],
        input_output_aliases={i: i for i in range(2 * nw)},
        compiler_params=pltpu.CompilerParams(has_side_effects=EFFECT),
    )(*sums, *lands, send_sems, recv_sems, *after)
    return out[:nw], out[nw:]


def _rs_finish(sums, gots, q, layer, into):
    n = len(sums)

    def body(q_ref, *refs):
        for s_ref, g_ref, o_ref in zip(refs[:n], refs[n:2 * n], refs[len(refs) - n:]):
            o_ref[...] = ((s_ref[...].astype(F32) + g_ref[0].astype(F32)) + g_ref[1].astype(F32)) + g_ref[2].astype(F32)

    rows = [s.shape[1] for s in sums]
    in_specs = [pl.BlockSpec((None, r, D_MODEL), lambda i, q_ref: (q_ref[0], 0, 0)) for r in rows]
    in_specs += [pl.BlockSpec((3, r, D_MODEL), lambda i, q_ref: (0, 0, 0)) for r in rows]
    args = [q, *sums, *gots]
    aliases = {}
    if into is not None:
        in_specs += [ANY] * n
        aliases = {len(args) + i: i for i in range(n)}
        args += list(into)
    return pl.pallas_call(
        body, name="rs_finish",
        grid_spec=pltpu.PrefetchScalarGridSpec(
            num_scalar_prefetch=1, grid=(1,), in_specs=in_specs,
            out_specs=[pl.BlockSpec((None, r, D_MODEL), lambda i, q_ref: (layer, 0, 0)) for r in rows]),
        out_shape=[jax.ShapeDtypeStruct((DEPTH, r, D_MODEL), F32) for r in rows],
        input_output_aliases=aliases,
        compiler_params=_cparams(("arbitrary",)),
    )(*args)


def _allreduce_small(vec, deps=()):
    R = vec.shape[0]
    assert R % (8 * N_DEV) == 0
    P = R // N_DEV
    nd = len(deps)

    def body(*refs):
        v_ref = refs[0]
        o_ref, buf, send1, recv1, send2, recv2 = refs[1 + nd:]
        x, y, c = _my_place()
        me = 4 * x + 2 * y + c

        def piece(ref, d):
            return ref.at[pl.ds(pl.multiple_of(d * P, 8), P), :]

        def peer(k):
            p = me ^ k
            return p, (p >> 2, (p >> 1) & 1, p & 1)

        scatter = []
        for k in range(1, N_DEV):
            p, where = peer(k)
            scatter.append(pltpu.make_async_remote_copy(
                src_ref=piece(v_ref, p), dst_ref=buf.at[k], send_sem=send1.at[k - 1], recv_sem=recv1.at[k - 1],
                device_id=where, device_id_type=MESH))
        for cp in scatter:
            cp.start()
        buf[0] = piece(v_ref, me)[...]
        for cp in scatter:
            cp.wait()
        acc = buf[me]
        for d in range(1, N_DEV):
            acc = acc + buf[me ^ d]
        piece(o_ref, me)[...] = acc
        spread, arrivals = [], []
        for k in range(1, N_DEV):
            p, where = peer(k)
            spread.append(pltpu.make_async_remote_copy(
                src_ref=piece(o_ref, me), dst_ref=piece(o_ref, me), send_sem=send2.at[k - 1], recv_sem=recv2.at[k - 1],
                device_id=where, device_id_type=MESH))
            arrivals.append(pltpu.make_async_remote_copy(
                src_ref=piece(o_ref, p), dst_ref=piece(o_ref, p), send_sem=send2.at[k - 1], recv_sem=recv2.at[k - 1],
                device_id=where, device_id_type=MESH))
        for cp in spread:
            cp.start()
        for cp in arrivals:
            cp.wait_recv()
        for cp in spread:
            cp.wait_send()

    sems = pltpu.SemaphoreType.DMA((N_DEV - 1,))
    return pl.pallas_call(
        body, name="allreduce_small",
        in_specs=[pl.BlockSpec(memory_space=pltpu.VMEM)] + [ANY] * nd, out_specs=pl.BlockSpec(memory_space=pltpu.VMEM),
        out_shape=jax.ShapeDtypeStruct((R, LANES), F32),
        scratch_shapes=[pltpu.VMEM((N_DEV, P, LANES), F32), sems, sems, sems, sems],
        compiler_params=_cparams(),
    )(vec, *deps)


def _pack(arrs):
    flat = jnp.concatenate([a.reshape(-1) for a in arrs])
    pad = (-flat.shape[0]) % (8 * N_DEV * LANES)
    return jnp.pad(flat, (0, pad)).reshape(-1, LANES)


def _unpack(packed, shapes):
    flat = packed.reshape(-1)
    out, off = [], 0
    for s in shapes:
        n = math.prod(s)
        out.append(flat[off:off + n].reshape(s))
        off += n
    return out


def kernel(x, w_in, w_conv, w_pool, pool_scale, sgu_ln_g, w_spatial, b_spatial, w_o, ln1_g, ln1_b, w_gate_up, w_down, ln2_g, ln2_b, loss_target, m_w_in, m_w_conv, m_w_pool, m_pool_scale, m_sgu_ln_g, m_w_spatial, m_b_spatial, m_w_o, m_ln1_g, m_ln1_b, m_w_gate_up, m_w_down, m_ln2_g, m_ln2_b, v_w_in, v_w_conv, v_w_pool, v_pool_scale, v_sgu_ln_g, v_w_spatial, v_b_spatial, v_w_o, v_ln1_g, v_ln1_b, v_w_gate_up, v_w_down, v_ln2_g, v_ln2_b):
    L = DEPTH
    T = x.shape[1]
    mx, my, mc = _my_place()
    dev = 4 * mx + 2 * my + mc
    xs = x[0]
    target = loss_target[0]

    conv_cols = w_conv.shape[2]
    w_conv_z = lax.dynamic_update_slice(jnp.zeros((L, 3, CONV_W), F32), w_conv, (0, 0, dev * conv_cols))
    w_conv_packed = _allreduce_small(_pack([w_conv_z]))
    w_conv_full = _unpack(w_conv_packed, [(L, 3, CONV_W)])[0]

    shards = (jnp.swapaxes(w_in, 1, 2).astype(BF16), jnp.swapaxes(w_gate_up, 1, 2).astype(BF16),
              w_o.astype(BF16), w_down.astype(BF16))
    first_gather = _ag_start_layer(shards, 0, [w_conv_packed])

    loss_tile, grad_x2, big_grads, small_grads = _local_step(
        xs, target, shards, first_gather, w_conv_full, w_pool, pool_scale, sgu_ln_g, w_spatial, b_spatial,
        ln1_g, ln1_b, ln2_g, ln2_b)
    loss = lax.psum(loss_tile[0, 0], ("x", "y", "c"))
    grad_x = grad_x2[None]
    big_w = (w_in, w_gate_up, w_o, w_down)
    big_m = (m_w_in, m_w_gate_up, m_w_o, m_w_down)
    big_v = (v_w_in, v_w_gate_up, v_w_o, v_w_down)
    small_w = [w_conv_full, w_pool, pool_scale, sgu_ln_g, w_spatial, b_spatial, ln1_g, ln1_b, ln2_g, ln2_b]
    small_m = [m_w_conv, m_w_pool, m_pool_scale, m_sgu_ln_g, m_w_spatial, m_b_spatial, m_ln1_g, m_ln1_b, m_ln2_g, m_ln2_b]
    small_v = [v_w_conv, v_w_pool, v_pool_scale, v_sgu_ln_g, v_w_spatial, v_b_spatial, v_ln1_g, v_ln1_b, v_ln2_g, v_ln2_b]
    grads, deltas, new_m, new_v = _reduce_and_update(
        big_grads, small_grads, big_w, big_m, big_v, small_w, small_m, small_v)
    return (loss, grad_x, *grads, *deltas, *new_m, *new_v)


def _ag_start_layer(shards, l, after):
    s_in, s_gu, s_o, s_dn = [s[l] for s in shards]
    first = _ag_start([s_in, s_o], "%da" % l, after=after)
    return first, _ag_start([s_gu, s_dn], "%db" % l, after=[first[4]])


def _ag_finish(gather, after, tag):
    send_sems, recv_sems, shards, lands, _ = gather
    shards, lands = _ag_wait(send_sems, recv_sems, shards, lands, after, tag)
    return _ag_pass_on(shards, lands)


def _rs_begin(parts, tag, after=()):
    return _rs_sibling_start([p.reshape(4, 2, p.shape[0] // N_DEV, D_MODEL) for p in parts], tag, after)


def _rs_continue(sibling_flight, after, c_arr, tag):
    send_sems, recv_sems, parts, lands, _ = sibling_flight
    parts, got = _rs_sibling_wait(send_sems, recv_sems, parts, lands, after, tag)
    return _rs_chip_start(_rs_chip_sum(parts, got, c_arr), tag)


def _local_step(xs, target, shards, gather, w_conv_full, w_pool, pool_scale, sgu_ln_g, w_spatial, b_spatial,
                ln1_g, ln1_b, ln2_g, ln2_b):
    L = DEPTH
    T = xs.shape[0]
    mx, my, mc = _my_place()
    c_arr = jnp.reshape(mc, (1,)).astype(jnp.int32)
    q_arr = jnp.reshape(2 * mx + my, (1,)).astype(jnp.int32)
    eye2 = jnp.eye(2, dtype=F32)
    wp = w_pool.reshape(L, 2, 2, HALF, HALF)
    wpool_bd = jnp.einsum("ltgcd,gh->ltgchd", wp, eye2).reshape(L, 2, LANES, LANES)
    wsp_t = w_spatial.reshape(L, 3, 2 * CHUNK, CHUNK)
    bias_t = jnp.repeat(jnp.swapaxes(b_spatial.reshape(L, 3, 2, CHUNK), 2, 3), HALF, axis=3)
    ones = jnp.ones((1, D_MODEL), F32)
    zeros = jnp.zeros((1, D_MODEL), F32)

    saved = []
    prev, pg, pb = xs, ones, zeros
    prev_b = xs.astype(BF16)
    weights = []
    for l in range(L):
        g_in, g_o = _ag_finish(gather[0], [] if l == 0 else [prev_b], "%da" % l)
        proj = _mm(prev_b, g_in, "nt", F32, 512, IN_W, D_MODEL, "mm_proj", deps=[gather[1][4]] if l == 0 else [])
        mixcat = _mixer_fwd(proj, w_conv_full[l], wpool_bd[l], pool_scale[l][None], sgu_ln_g[l][None], wsp_t[l], bias_t[l])
        xhat1, rstd1, h_b = _mm_ln_fwd(mixcat, g_o, prev, pg, pb, ln1_g[l][None], ln1_b[l][None], "mm_wo_ln")
        g_gu, g_dn = _ag_finish(gather[1], [h_b], "%db" % l)
        weights.append((g_in, g_gu, g_o, g_dn))
        deps = []
        if l + 1 < L:
            gather = _ag_start_layer(shards, l + 1, [g_gu])
            deps = [gather[1][4]]
        g_act, u_act, act = _mm_swiglu_fwd(h_b, g_gu, deps=deps)
        xhat2, rstd2, y_b = _mm_ln_fwd(act, g_dn, xhat1, ln1_g[l][None], ln1_b[l][None], ln2_g[l][None], ln2_b[l][None],
                                       "mm_down_ln")
        saved.append((prev_b, proj, mixcat, xhat1, rstd1, h_b, g_act, u_act, act, xhat2, rstd2))
        prev, pg, pb, prev_b = xhat2, ln2_g[l][None], ln2_b[l][None], y_b

    loss_tile, dy = _loss_head(prev, pg, pb, target)

    small = [None] * L
    big = None
    sibling_flight = None
    above = None
    for l in reversed(range(L)):
        prev_b, proj, mixcat, xhat1, rstd1, h_b, g_act, u_act, act, xhat2, rstd2 = saved[l]
        g_in, g_gu, g_o, g_dn = weights[l]
        chip_flight = None
        if above is None:
            dr2, dr2_b, dg2, db2 = _ln_bwd(None, dy, xhat2, rstd2, ln2_g[l][None])
        else:
            dr2, dr2_b, dg2, db2 = _mm_ln_bwd([above[0]], above[1], above[2], xhat2, rstd2, ln2_g[l][None],
                                              "mm_dx_ln", deps=[sibling_flight[4]])
            chip_flight = _rs_continue(sibling_flight, [dr2_b], c_arr, str(l + 1))
        dg_b, du_b = _mm_swiglu_bwd(dr2_b, g_dn, g_act, u_act, deps=[chip_flight[4]] if chip_flight else [])
        p_dn = _mm(act, dr2_b, "tn", BF16, DW_TM, D_MODEL, T, "mm_dw_down")
        p_gu = _mm_tn_pair(dg_b, du_b, h_b, DW_TM, "mm_dw_gate_up")
        ffn_sibling = _rs_begin([p_gu, p_dn], "0b") if l == 0 else None
        dr1, dr1_b, dg1, db1 = _mm_ln_bwd([dg_b, du_b], g_gu, dr2, xhat1, rstd1, ln1_g[l][None], "mm_dh_ln",
                                          deps=[ffn_sibling[4]] if l == 0 else [])
        ffn_flight = _rs_continue(ffn_sibling, [dr1_b], c_arr, "0b") if l == 0 else None
        dmix = _mm(dr1_b, g_o, "nt", F32, T, 512, D_MODEL, "mm_dmix", deps=[ffn_flight[4]] if l == 0 else [])
        p_o = _mm(mixcat, dr1_b, "tn", BF16, 512, D_MODEL, T, "mm_dw_o")
        dproj, dwc, dwp, dps, dlng, dwsp, dbias = _mixer_bwd(
            proj, dmix, w_conv_full[l], wpool_bd[l], pool_scale[l][None], sgu_ln_g[l][None], wsp_t[l], bias_t[l])
        p_in = _mm(dproj, prev_b, "tn", BF16, IN_W, D_MODEL, T, "mm_dw_in")
        small[l] = (dwc, dwp, dps, dlng, dwsp, dbias, dg1, db1, dg2, db2)
        above = (dproj, g_in, dr1)
        if chip_flight is not None:
            big = list(_rs_chip_finish(chip_flight, [p_in], q_arr, str(l + 1), l + 1, big))
        if l > 0:
            sibling_flight = _rs_begin([p_in, p_gu, p_o, p_dn], str(l))
        else:
            big[1], big[3] = _rs_chip_finish(ffn_flight, [p_in], q_arr, "0b", 0, [big[1], big[3]])

    def stack(i):
        return jnp.stack([small[l][i] for l in range(L)])

    dwp_bd = stack(1).reshape(L, 2, 2, HALF, 2, HALF)
    dwp_all = jnp.einsum("ltgchd,gh->ltgcd", dwp_bd, eye2).reshape(L, 4, HALF, HALF)
    dbs_all = jnp.swapaxes(stack(5)[:, :, :, :2], 2, 3).reshape(L, 6, CHUNK)
    small_grads = [stack(0), dwp_all, stack(2).reshape(L, POOL_W), stack(3).reshape(L, SGU_W),
                   stack(4).reshape(L, 6, CHUNK, CHUNK), dbs_all] + [stack(i).reshape(L, D_MODEL) for i in (6, 7, 8, 9)]
    packed_small = _allreduce_small(_pack(small_grads), deps=[big[1]])
    sibling_flight = _rs_begin([p_in, p_o], "0a", after=[packed_small])
    grad_x = _mm_ln_bwd([above[0]], above[1], above[2], None, None, None, "mm_dx_out", deps=[sibling_flight[4]])
    last_flight = _rs_continue(sibling_flight, [grad_x], c_arr, "0a")
    return loss_tile, grad_x, (big, last_flight, q_arr), (packed_small, [a.shape for a in small_grads])


def _rs_chip_finish(in_flight, after, q, tag, layer, into):
    send_sems, recv_sems, sums, lands, _ = in_flight
    sums, got = _rs_chip_wait(send_sems, recv_sems, sums, lands, after, tag)
    return _rs_finish(sums, got, q, layer, into)


def _reduce_and_update(big_grads, small_grads, big_w, big_m, big_v, small_w, small_m, small_v):
    L = DEPTH
    mx, my, mc = _my_place()
    dev = 4 * mx + 2 * my + mc
    conv_cols = CONV_W // N_DEV
    w_in, w_gate_up, w_o, w_down = big_w
    m_w_in, m_w_gate_up, m_w_o, m_w_down = big_m
    v_w_in, v_w_gate_up, v_w_o, v_w_down = big_v
    packed_g, small_shapes = small_grads
    big, last_flight, q_arr = big_grads

    def widen_conv(a):
        return lax.dynamic_update_slice(jnp.zeros((L, 3, CONV_W), F32), a, (0, 0, dev * conv_cols))

    small_m = [widen_conv(small_m[0])] + list(small_m[1:])
    small_v = [widen_conv(small_v[0])] + list(small_v[1:])
    pk_d, pk_m, pk_v = _adamw(_pack(small_w), packed_g, _pack(small_m), _pack(small_v), packed_g.shape[0] // 2)
    sg = _unpack(packed_g, small_shapes)
    sd = _unpack(pk_d, small_shapes)
    sm = _unpack(pk_m, small_shapes)
    sv = _unpack(pk_v, small_shapes)

    def conv_cols_of(a):
        return lax.dynamic_slice(a, (0, 0, dev * conv_cols), (L, 3, conv_cols))

    for lst in (sg, sd, sm, sv):
        lst[0] = conv_cols_of(lst[0])

    tr = lambda a: jnp.swapaxes(a, 1, 2)
    gt_gu, g_w_dn = big[1], big[3]
    d_gu, m_gu, v_gu = [tr(a) for a in _adamw(tr(w_gate_up), gt_gu, tr(m_w_gate_up), tr(v_w_gate_up), gt_gu.shape[1] // 2)]
    d_dn, m_dn, v_dn = _adamw(w_down, g_w_dn, m_w_down, v_w_down, 352)
    gt_in, g_w_o = _rs_chip_finish(last_flight, [d_gu, d_dn, pk_d], q_arr, "0a", 0, [big[0], big[2]])
    d_in, m_in, v_in = [tr(a) for a in _adamw(tr(w_in), gt_in, tr(m_w_in), tr(v_w_in), gt_in.shape[1])]
    d_o, m_o, v_o = _adamw(w_o, g_w_o, m_w_o, v_w_o, 128)
    g_w_in, g_w_gu = tr(gt_in), tr(gt_gu)

    def ordered(big_in, big_o, big_gu, big_dn, sm_list):
        return [big_in, sm_list[0], sm_list[1], sm_list[2], sm_list[3], sm_list[4], sm_list[5], big_o,
                sm_list[6], sm_list[7], big_gu, big_dn, sm_list[8], sm_list[9]]

    grads = ordered(g_w_in, g_w_o, g_w_gu, g_w_dn, sg)
    deltas = ordered(d_in, d_o, d_gu, d_dn, sd)
    new_m = ordered(m_in, m_o, m_gu, m_dn, sm)
    new_v = ordered(v_in, v_o, v_gu, v_dn, sv)
    return grads, deltas, new_m, new_v
```

```python
import functools
import math

import jax
import jax.numpy as jnp
from jax import lax
from jax.experimental import pallas as pl
from jax.experimental.pallas import tpu as pltpu

F32 = jnp.float32
BF16 = jnp.bfloat16
MESH = pl.DeviceIdType.MESH

D_MODEL = 1024
DEPTH = 4
CONV_W = 384
POOL_W = 256
SGU_W = 384
IN_W = 3 * CONV_W + POOL_W + 2 * SGU_W
D_FF = 2816
CHUNK = 128
ALPHA = float((2 * DEPTH) ** 0.25)
LN_EPS = 1e-5
ADAM_LR, ADAM_B1, ADAM_B2, ADAM_EPS, ADAM_WD, ADAM_STEP = 0.001, 0.9, 0.999, 1e-08, 0.01, 10

N_DEV = 8
LANES = 128
HALF = 64
SHARD_ROWS = (IN_W // N_DEV, 2 * D_FF // N_DEV, D_MODEL // N_DEV, D_FF // N_DEV)
VMEM_LIMIT = 52 * 1024 * 1024

INV_SQRT2 = 0.7071067811865476
INV_SQRT_2PI = 0.3989422804014327


def _cparams(sem=None, **kw):
    if sem is not None:
        kw["dimension_semantics"] = sem
    return pltpu.CompilerParams(vmem_limit_bytes=VMEM_LIMIT, **kw)


_DN = {"nn": (((1,), (0,)), ((), ())), "nt": (((1,), (1,)), ((), ())), "tn": (((0,), (0,)), ((), ()))}


def _mm(a, b, mode, out_dtype, tm, tn, tk, name, deps=(), out_rows=None, out_off=0, out_into=None):
    if mode == "nn":
        (M, K), N = a.shape, b.shape[1]
    elif mode == "nt":
        (M, K), N = a.shape, b.shape[0]
    else:
        (K, M), N = a.shape, b.shape[1]
    assert M % tm == 0 and N % tn == 0 and K % tk == 0 and out_off % tm == 0, (M, N, K, tm, tn, tk)
    nk = K // tk
    if out_into is not None:
        deps = tuple(deps) + (out_into,)
    nd = len(deps)
    row_off = out_off // tm

    def body(*refs):
        a_ref, b_ref, o_ref = refs[0], refs[1], refs[2 + nd]
        acc_ref = refs[3 + nd] if nk > 1 else None
        p = lax.dot_general(a_ref[...], b_ref[...], _DN[mode], preferred_element_type=F32)
        if nk == 1:
            o_ref[...] = p.astype(o_ref.dtype)
        else:
            k = pl.program_id(2)

            @pl.when(k == 0)
            def _():
                acc_ref[...] = p

            @pl.when(k > 0)
            def _():
                acc_ref[...] += p

            @pl.when(k == nk - 1)
            def _():
                o_ref[...] = acc_ref[...].astype(o_ref.dtype)

    if mode == "nn":
        a_spec = pl.BlockSpec((tm, tk), lambda i, j, k: (i, k))
        b_blk, b_idx = (tk, tn), (lambda i, j, k: (k, j))
    elif mode == "nt":
        a_spec = pl.BlockSpec((tm, tk), lambda i, j, k: (i, k))
        b_blk, b_idx = (tn, tk), (lambda i, j, k: (j, k))
    else:
        a_spec = pl.BlockSpec((tk, tm), lambda i, j, k: (k, i))
        b_blk, b_idx = (tk, tn), (lambda i, j, k: (k, j))
    return pl.pallas_call(
        body,
        name=name,
        grid=(M // tm, N // tn, nk),
        in_specs=[a_spec, pl.BlockSpec(b_blk, b_idx)] + [pl.BlockSpec(memory_space=pl.ANY)] * nd,
        out_specs=pl.BlockSpec((tm, tn), lambda i, j, k: (i + row_off, j)),
        out_shape=jax.ShapeDtypeStruct((out_rows or M, N), out_dtype),
        scratch_shapes=[pltpu.VMEM((tm, tn), F32)] if nk > 1 else [],
        input_output_aliases={1 + nd: 0} if out_into is not None else {},
        compiler_params=_cparams(("parallel", "parallel", "arbitrary")),
    )(a, b, *deps)


def _mm_tn_pair(a1, a2, b, tm, name):
    K, M = a1.shape
    N = b.shape[1]
    n1 = M // tm

    def body(a1_ref, a2_ref, b_ref, o_ref):
        i = pl.program_id(0)

        @pl.when(i < n1)
        def _():
            o_ref[...] = lax.dot_general(a1_ref[...], b_ref[...], _DN["tn"], preferred_element_type=F32).astype(o_ref.dtype)

        @pl.when(i >= n1)
        def _():
            o_ref[...] = lax.dot_general(a2_ref[...], b_ref[...], _DN["tn"], preferred_element_type=F32).astype(o_ref.dtype)

    return pl.pallas_call(
        body, name=name, grid=(2 * n1,),
        in_specs=[pl.BlockSpec((K, tm), lambda i: (0, jnp.minimum(i, n1 - 1))),
                  pl.BlockSpec((K, tm), lambda i: (0, jnp.maximum(i - n1, 0))),
                  pl.BlockSpec((K, N), lambda i: (0, 0))],
        out_specs=pl.BlockSpec((tm, N), lambda i: (i, 0)),
        out_shape=jax.ShapeDtypeStruct((2 * M, N), BF16),
        compiler_params=_cparams(("arbitrary",)),
    )(a1, a2, b)


LN_SUB = 256
LN_TM = 512


def _vec(v):
    arr, layer = v
    return arr, pl.BlockSpec((None, 1, D_MODEL), lambda *_: (layer, 0, 0))


def _mm_ln_fwd(a, b, prev, pg, pb, g, bias, name):
    T, K = a.shape
    tm = LN_TM

    def body(a_ref, b_ref, prev_ref, pg_ref, pb_ref, g_ref, bias_ref, xhat_ref, rstd_ref, y_ref):
        for s in range(tm // LN_SUB):
            rows = slice(s * LN_SUB, (s + 1) * LN_SUB)
            mm = jnp.dot(a_ref[rows, :], b_ref[...], preferred_element_type=F32)
            r = ALPHA * (prev_ref[rows, :] * pg_ref[...] + pb_ref[...]) + mm
            mu = jnp.mean(r, axis=-1, keepdims=True)
            xc = r - mu
            var = jnp.mean(xc * xc, axis=-1, keepdims=True)
            rstd = lax.rsqrt(var + LN_EPS)
            xhat = xc * rstd
            xhat_ref[rows, :] = xhat
            rstd_ref[rows, :] = rstd
            y_ref[rows, :] = (xhat * g_ref[...] + bias_ref[...]).astype(y_ref.dtype)

    row = pl.BlockSpec((tm, D_MODEL), lambda i: (i, 0))
    vecs = [_vec(v) for v in (pg, pb, g, bias)]
    return pl.pallas_call(
        body, name=name, grid=(T // tm,),
        in_specs=[pl.BlockSpec((tm, K), lambda i: (i, 0)),
                  pl.BlockSpec((K, D_MODEL), lambda i: (0, 0), pipeline_mode=pl.Buffered(1)),
                  row] + [s for _, s in vecs],
        out_specs=[row, pl.BlockSpec((tm, 1), lambda i: (i, 0)), row],
        out_shape=[jax.ShapeDtypeStruct((T, D_MODEL), F32), jax.ShapeDtypeStruct((T, 1), F32),
                   jax.ShapeDtypeStruct((T, D_MODEL), BF16)],
        compiler_params=_cparams(("parallel",)),
    )(a, b, prev, *[a_ for a_, _ in vecs])


def _mm_ln_bwd(a_list, b, dres, xhat, rstd, g, name, deps=(), w_back=None):
    T = a_list[0].shape[0]
    tm = LN_TM
    na, nd = len(a_list), len(deps)
    ks = [a.shape[1] for a in a_list]
    last = xhat is None
    nout = 1 if last else (5 if w_back is not None else 4)

    def body(*refs):
        a_refs, b_ref, dres_ref = refs[:na], refs[na], refs[na + 1]
        if not last:
            xhat_ref, rstd_ref, g_ref = refs[na + 2:na + 5]
            dr_ref, drb_ref, dg_ref, db_ref = refs[len(refs) - nout:len(refs) - nout + 4]

            @pl.when(pl.program_id(0) == 0)
            def _():
                dg_ref[...] = jnp.zeros_like(dg_ref)
                db_ref[...] = jnp.zeros_like(db_ref)

        for s in range(tm // LN_SUB):
            rows = slice(s * LN_SUB, (s + 1) * LN_SUB)
            mm, off = None, 0
            for a_ref, k in zip(a_refs, ks):
                part = jnp.dot(a_ref[rows, :], b_ref[off:off + k, :], preferred_element_type=F32)
                mm = part if mm is None else mm + part
                off += k
            dy = ALPHA * dres_ref[rows, :] + mm
            if last:
                refs[-1][rows, :] = dy
                continue
            xhat_v = xhat_ref[rows, :]
            dg_ref[...] += jnp.sum(dy * xhat_v, axis=0, keepdims=True)
            db_ref[...] += jnp.sum(dy, axis=0, keepdims=True)
            dxh = dy * g_ref[...]
            m1 = jnp.mean(dxh, axis=-1, keepdims=True)
            m2 = jnp.mean(dxh * xhat_v, axis=-1, keepdims=True)
            dr = rstd_ref[rows, :] * (dxh - m1 - xhat_v * m2)
            dr_ref[rows, :] = dr
            dr_b = dr.astype(drb_ref.dtype)
            drb_ref[rows, :] = dr_b
            if w_back is not None:
                refs[-1][rows, :] = lax.dot_general(dr_b, refs[na + 5][...], _DN["nt"], preferred_element_type=F32)

    row = pl.BlockSpec((tm, D_MODEL), lambda i: (i, 0))
    vec = pl.BlockSpec((1, D_MODEL), lambda i: (0, 0))
    in_specs = [pl.BlockSpec((tm, k), lambda i: (i, 0)) for k in ks]
    in_specs += [pl.BlockSpec((sum(ks), D_MODEL), lambda i: (0, 0), pipeline_mode=pl.Buffered(1)), row]
    args = list(a_list) + [b, dres]
    if last:
        out_specs, out_shape = row, jax.ShapeDtypeStruct((T, D_MODEL), F32)
    else:
        g_arr, g_spec = _vec(g)
        in_specs += [row, pl.BlockSpec((tm, 1), lambda i: (i, 0)), g_spec]
        args += [xhat, rstd, g_arr]
        out_specs = [row, row, vec, vec]
        out_shape = [jax.ShapeDtypeStruct((T, D_MODEL), F32), jax.ShapeDtypeStruct((T, D_MODEL), BF16),
                     jax.ShapeDtypeStruct((1, D_MODEL), F32), jax.ShapeDtypeStruct((1, D_MODEL), F32)]
        if w_back is not None:
            in_specs.append(pl.BlockSpec(w_back.shape, lambda i: (0, 0), pipeline_mode=pl.Buffered(1)))
            args.append(w_back)
            out_specs.append(row)
            out_shape.append(jax.ShapeDtypeStruct((T, w_back.shape[0]), F32))
    return pl.pallas_call(
        body, name=name, grid=(T // tm,),
        in_specs=in_specs + [pl.BlockSpec(memory_space=pl.ANY)] * nd,
        out_specs=out_specs, out_shape=out_shape,
        compiler_params=_cparams(("parallel",) if last else ("arbitrary",)),
    )(*args, *deps)


DW_TM = 1408
FF_TN = 256
SAVED_GU = BF16


def _mm_swiglu_fwd(h, w_gu, deps=()):
    T = h.shape[0]
    nj = D_FF // FF_TN
    nd = len(deps)

    def body(*refs):
        h_ref, wg_ref, wu_ref = refs[:3]
        g_ref, u_ref, act_ref = refs[3 + nd:]
        hv = h_ref[...]
        gv = lax.dot_general(hv, wg_ref[...], _DN["nt"], preferred_element_type=F32)
        uv = lax.dot_general(hv, wu_ref[...], _DN["nt"], preferred_element_type=F32)
        g_ref[...] = gv.astype(g_ref.dtype)
        u_ref[...] = uv.astype(u_ref.dtype)
        act_ref[...] = (gv * jax.nn.sigmoid(gv) * uv).astype(act_ref.dtype)

    col = pl.BlockSpec((T, FF_TN), lambda j: (0, j))
    return pl.pallas_call(
        body, name="mm_gate_up_swiglu", grid=(nj,),
        in_specs=[pl.BlockSpec((T, D_MODEL), lambda j: (0, 0)),
                  pl.BlockSpec((FF_TN, D_MODEL), lambda j: (j, 0)),
                  pl.BlockSpec((FF_TN, D_MODEL), lambda j: (j + nj, 0))] + [pl.BlockSpec(memory_space=pl.ANY)] * nd,
        out_specs=[col, col, col],
        out_shape=[jax.ShapeDtypeStruct((T, D_FF), SAVED_GU), jax.ShapeDtypeStruct((T, D_FF), SAVED_GU),
                   jax.ShapeDtypeStruct((T, D_FF), BF16)],
        compiler_params=_cparams(("parallel",)),
    )(h, w_gu, w_gu, *deps)


def _mm_swiglu_bwd(dr, w_dn, g, u, deps=()):
    T = dr.shape[0]

    def body(*refs):
        dr_ref, w_ref, g_ref, u_ref = refs[:4]
        dg_ref, du_ref = refs[-2:]
        da = lax.dot_general(dr_ref[...], w_ref[...], _DN["nt"], preferred_element_type=F32)
        gv, uv = g_ref[...].astype(F32), u_ref[...].astype(F32)
        s = jax.nn.sigmoid(gv)
        du_ref[...] = (da * (gv * s)).astype(du_ref.dtype)
        dg_ref[...] = (da * uv * (s * (1.0 + gv * (1.0 - s)))).astype(dg_ref.dtype)

    col = pl.BlockSpec((T, FF_TN), lambda j: (0, j))
    return pl.pallas_call(
        body, name="mm_dact_swiglu", grid=(D_FF // FF_TN,),
        in_specs=[pl.BlockSpec((T, D_MODEL), lambda j: (0, 0)), pl.BlockSpec((FF_TN, D_MODEL), lambda j: (j, 0)),
                  col, col] + [ANY] * len(deps),
        out_specs=[col, col],
        out_shape=[jax.ShapeDtypeStruct((T, D_FF), BF16)] * 2,
        compiler_params=_cparams(("parallel",)),
    )(dr, w_dn, g, u, *deps)


def _gelu(x):
    return 0.5 * x * (1.0 + lax.erf(x * INV_SQRT2))


def _gelu_grad(x):
    return 0.5 * (1.0 + lax.erf(x * INV_SQRT2)) + x * (jnp.exp(-0.5 * x * x) * INV_SQRT_2PI)


def _shift_down(z, k):
    row = lax.broadcasted_iota(jnp.int32, z.shape, 0)
    return jnp.where(row >= k, pltpu.roll(z, k, 0), 0.0)


def _shift_up(z, k):
    n = z.shape[0]
    row = lax.broadcasted_iota(jnp.int32, z.shape, 0)
    return jnp.where(row < n - k, pltpu.roll(z, n - k, 0), 0.0)


def _lo_mask(shape):
    return lax.broadcasted_iota(jnp.int32, shape, len(shape) - 1) < HALF


def _seg_mean(x, lo):
    a = jnp.sum(jnp.where(lo, x, 0.0), axis=-1, keepdims=True)
    b = jnp.sum(jnp.where(lo, 0.0, x), axis=-1, keepdims=True)
    return jnp.where(lo, a, b) * (1.0 / HALF)


def _pool_windows(first):
    lo = _lo_mask((1, LANES))
    return jnp.where(first, jnp.where(lo, 2.0, 4.0), jnp.where(lo, 8.0, 16.0)), lo


def _pool_mean_minus_token(p, first):
    wl, lo = _pool_windows(first)
    s2 = p + _shift_down(p, 1)
    s4 = s2 + _shift_down(s2, 2)
    s8 = s4 + _shift_down(s4, 4)
    s16 = s8 + _shift_down(s8, 8)
    win = jnp.where(first, jnp.where(lo, s2, s4), jnp.where(lo, s8, s16))
    t1 = (lax.broadcasted_iota(jnp.int32, p.shape, 0) + 1).astype(F32)
    count = jnp.minimum(t1, wl)
    return win / count - p, count


SGU_UNROLL = 4


def _tril_keep():
    r = lax.broadcasted_iota(jnp.int32, (2 * CHUNK, CHUNK), 0)
    s = lax.broadcasted_iota(jnp.int32, (2 * CHUNK, CHUNK), 1)
    return s <= (r & (CHUNK - 1))


def _sgu_chunk_fwd(u, v, g, wm, bias, lo):
    ug = _gelu(u)
    vg = _gelu(v)
    mu = _seg_mean(vg, lo)
    xc = vg - mu
    var = _seg_mean(xc * xc, lo)
    rstd = lax.rsqrt(var + LN_EPS)
    vn = xc * rstd
    vh = (vn * g).astype(BF16)
    mm2 = jnp.dot(wm, vh, preferred_element_type=F32)
    mixed = jnp.where(lo, mm2[:CHUNK], mm2[CHUNK:]) + bias
    return ug, vn, rstd, vh, mixed


def _mixer_fwd(proj, wconv, wpool_bd, pscale, lng, wsp, bias, layer):
    T = proj.shape[0]
    nchunk = T // CHUNK

    def body(a_ref, b_ref, c_ref, wc_ref, wp_ref, ps_ref, lng_ref, wsp_ref, bias_ref, o_ref):
        j = pl.program_id(0)

        @pl.when(j < 3)
        def _conv():
            z = c_ref[...] * a_ref[...]
            w = wc_ref[...]
            y = w[0:1] * _shift_down(z, 2) + w[1:2] * _shift_down(z, 1) + w[2:3] * z
            o_ref[...] = (b_ref[...] * y).astype(o_ref.dtype)

        @pl.when((j >= 3) & (j < 5))
        def _pool():
            d, _ = _pool_mean_minus_token(a_ref[...], j == 3)
            y = jnp.dot(d.astype(BF16), wp_ref[...].astype(BF16), preferred_element_type=F32)
            o_ref[...] = (y * ps_ref[...]).astype(o_ref.dtype)

        @pl.when(j >= 5)
        def _sgu():
            lo = _lo_mask((CHUNK, LANES))
            wm = jnp.where(_tril_keep(), wsp_ref[...], 0.0).astype(BF16)
            bias_t = bias_ref[...]
            g = lng_ref[...]

            def chunk(n, carry):
                rows = pl.ds(pl.multiple_of(n * CHUNK, CHUNK), CHUNK)
                ug, _, _, _, mixed = _sgu_chunk_fwd(a_ref[rows, :], b_ref[rows, :], g, wm, bias_t, lo)
                o_ref[rows, :] = (ug * mixed).astype(o_ref.dtype)
                return carry

            lax.fori_loop(0, nchunk, chunk, 0, unroll=SGU_UNROLL)

    def col(f):
        return lambda j: (0, f(j))

    clip = lambda v, lo, hi: jnp.minimum(jnp.maximum(v, lo), hi)
    return pl.pallas_call(
        body,
        name="mixer_fwd",
        grid=(8,),
        in_specs=[
            pl.BlockSpec((T, LANES), col(lambda j: jnp.where(j < 3, j, jnp.where(j < 5, j + 6, j + 6)))),
            pl.BlockSpec((T, LANES), col(lambda j: jnp.where(j < 3, j + 3, jnp.where(j < 5, 5, j + 9)))),
            pl.BlockSpec((T, LANES), col(lambda j: jnp.where(j < 3, j + 6, 8))),
            pl.BlockSpec((None, 3, LANES), lambda j: (layer, 0, clip(j, 0, 2))),
            pl.BlockSpec((None, None, LANES, LANES), lambda j: (layer, clip(j - 3, 0, 1), 0, 0)),
            pl.BlockSpec((None, 1, LANES), lambda j: (layer, 0, clip(j - 3, 0, 1))),
            pl.BlockSpec((None, 1, LANES), lambda j: (layer, 0, clip(j - 5, 0, 2))),
            pl.BlockSpec((None, None, 2 * CHUNK, CHUNK), lambda j: (layer, clip(j - 5, 0, 2), 0, 0)),
            pl.BlockSpec((None, None, CHUNK, LANES), lambda j: (layer, clip(j - 5, 0, 2), 0, 0)),
        ],
        out_specs=pl.BlockSpec((T, LANES), lambda j: (0, j)),
        out_shape=jax.ShapeDtypeStruct((T, D_MODEL), BF16),
        compiler_params=_cparams(("arbitrary",)),
    )(proj, proj, proj, wconv, wpool_bd, pscale, lng, wsp, bias)


def _mixer_bwd(proj, dmix, wconv, wpool_bd, pscale, lng, wsp, bias, layer):
    T = proj.shape[0]
    nchunk = T // CHUNK

    def body(a_ref, b_ref, c_ref, dm_ref, wc_ref, wp_ref, ps_ref, lng_ref, wsp_ref, bias_ref,
             o_ref, dwc_ref, dwp_ref, dps_ref, dlng_ref, dwsp_ref, dbias_ref, keep1, keep2):
        k = pl.program_id(0)

        @pl.when(k < 3)
        def _conv():
            xa, gb, gc, dya = a_ref[...], b_ref[...], c_ref[...], dm_ref[...]
            w = wc_ref[...]
            z = gc * xa
            z1 = _shift_down(z, 1)
            z2 = _shift_down(z, 2)
            y = w[0:1] * z2 + w[1:2] * z1 + w[2:3] * z
            dyv = dya * gb
            dz = w[2:3] * dyv + w[1:2] * _shift_up(dyv, 1) + w[0:1] * _shift_up(dyv, 2)
            dwc_ref[0:1, :] = jnp.sum(dyv * z2, axis=0, keepdims=True)
            dwc_ref[1:2, :] = jnp.sum(dyv * z1, axis=0, keepdims=True)
            dwc_ref[2:3, :] = jnp.sum(dyv * z, axis=0, keepdims=True)
            o_ref[...] = (dz * gc).astype(o_ref.dtype)
            keep1[k] = (dya * y).astype(keep1.dtype)
            keep1[k + 3] = (dz * xa).astype(keep1.dtype)

        @pl.when((k >= 3) & (k < 9))
        def _emit_gb_gc():
            o_ref[...] = keep1[k - 3]

        @pl.when((k >= 9) & (k < 11))
        def _pool():
            first = k == 9
            p, dyb = a_ref[...], dm_ref[...]
            d, count = _pool_mean_minus_token(p, first)
            w2 = wp_ref[...].astype(BF16)
            db = d.astype(BF16)
            y = jnp.dot(db, w2, preferred_element_type=F32)
            dps_ref[...] = jnp.sum(dyb * y, axis=0, keepdims=True)
            dyv = (dyb * ps_ref[...]).astype(BF16)
            dd = lax.dot_general(dyv, w2, _DN["nt"], preferred_element_type=F32)
            dwp_ref[...] = lax.dot_general(db, dyv, _DN["tn"], preferred_element_type=F32)
            dwin = dd / count
            a2 = dwin + _shift_up(dwin, 1)
            a4 = a2 + _shift_up(a2, 2)
            a8 = a4 + _shift_up(a4, 4)
            a16 = a8 + _shift_up(a8, 8)
            _, lo = _pool_windows(first)
            back = jnp.where(first, jnp.where(lo, a2, a4), jnp.where(lo, a8, a16))
            o_ref[...] = (back - dd).astype(o_ref.dtype)

        @pl.when((k >= 11) & (k < 14))
        def _sgu():
            lo = _lo_mask((CHUNK, LANES))
            keep = _tril_keep()
            wm = jnp.where(keep, wsp_ref[...], 0.0).astype(BF16)
            bias_t = bias_ref[...]
            g = lng_ref[...]
            dwsp_ref[...] = jnp.zeros_like(dwsp_ref)
            dbias_ref[...] = jnp.zeros_like(dbias_ref)
            dlng_ref[...] = jnp.zeros_like(dlng_ref)

            def chunk(n, carry):
                rows = pl.ds(pl.multiple_of(n * CHUNK, CHUNK), CHUNK)
                u, v, dyc = a_ref[rows, :], b_ref[rows, :], dm_ref[rows, :]
                ug, vn, rstd, vh, mixed = _sgu_chunk_fwd(u, v, g, wm, bias_t, lo)
                dmx = dyc * ug
                o_ref[rows, :] = (dyc * mixed * _gelu_grad(u)).astype(o_ref.dtype)
                dbias_ref[...] += dmx
                dst = jnp.concatenate([jnp.where(lo, dmx, 0.0), jnp.where(lo, 0.0, dmx)], axis=0).astype(BF16)
                dwsp_ref[...] += lax.dot_general(dst, vh, _DN["nt"], preferred_element_type=F32)
                dvh = lax.dot_general(wm, dst, _DN["tn"], preferred_element_type=F32)
                dlng_ref[...] += jnp.sum(dvh * vn, axis=0, keepdims=True)
                dvn = dvh * g
                m1 = _seg_mean(dvn, lo)
                m2 = _seg_mean(dvn * vn, lo)
                dvg = rstd * (dvn - m1 - vn * m2)
                keep2[k - 11, rows, :] = (dvg * _gelu_grad(v)).astype(keep2.dtype)
                return carry

            lax.fori_loop(0, nchunk, chunk, 0, unroll=SGU_UNROLL)
            dwsp_ref[...] = jnp.where(keep, dwsp_ref[...], 0.0)
            dbt = dbias_ref[...]
            lane = lax.broadcasted_iota(jnp.int32, (CHUNK, LANES), 1)
            sa = jnp.sum(jnp.where(lo, dbt, 0.0), axis=-1, keepdims=True)
            sb = jnp.sum(jnp.where(lo, 0.0, dbt), axis=-1, keepdims=True)
            dbias_ref[...] = jnp.where(lane == 0, sa, jnp.where(lane == 1, sb, 0.0))

        @pl.when(k >= 14)
        def _emit_v():
            o_ref[...] = keep2[k - 14]

    def col(f):
        return lambda k: (0, f(k))

    clip = lambda v, lo, hi: jnp.minimum(jnp.maximum(v, lo), hi)
    view_a = lambda k: jnp.where(k < 3, k, jnp.where(k < 9, 2, jnp.where(k < 14, k, 13)))
    view_b = lambda k: jnp.where(k < 3, k + 3, jnp.where(k < 11, 5, jnp.where(k < 14, k + 3, 16)))
    view_c = lambda k: jnp.where(k < 3, k + 6, 8)
    view_dm = lambda k: jnp.where(k < 3, k, jnp.where(k < 9, 2, jnp.where(k < 14, k - 6, 7)))
    return pl.pallas_call(
        body,
        name="mixer_bwd",
        grid=(17,),
        in_specs=[
            pl.BlockSpec((T, LANES), col(view_a)),
            pl.BlockSpec((T, LANES), col(view_b)),
            pl.BlockSpec((T, LANES), col(view_c)),
            pl.BlockSpec((T, LANES), col(view_dm)),
            pl.BlockSpec((None, 3, LANES), lambda k: (layer, 0, clip(k, 0, 2))),
            pl.BlockSpec((None, None, LANES, LANES), lambda k: (layer, clip(k - 9, 0, 1), 0, 0)),
            pl.BlockSpec((None, 1, LANES), lambda k: (layer, 0, clip(k - 9, 0, 1))),
            pl.BlockSpec((None, 1, LANES), lambda k: (layer, 0, clip(k - 11, 0, 2))),
            pl.BlockSpec((None, None, 2 * CHUNK, CHUNK), lambda k: (layer, clip(k - 11, 0, 2), 0, 0)),
            pl.BlockSpec((None, None, CHUNK, LANES), lambda k: (layer, clip(k - 11, 0, 2), 0, 0)),
        ],
        out_specs=[
            pl.BlockSpec((T, LANES), lambda k: (0, k)),
            pl.BlockSpec((3, LANES), col(lambda k: clip(k, 0, 2))),
            pl.BlockSpec((None, LANES, LANES), lambda k: (clip(k - 9, 0, 1), 0, 0)),
            pl.BlockSpec((1, LANES), col(lambda k: clip(k - 9, 0, 1))),
            pl.BlockSpec((1, LANES), col(lambda k: clip(k - 11, 0, 2))),
            pl.BlockSpec((None, 2 * CHUNK, CHUNK), lambda k: (clip(k - 11, 0, 2), 0, 0)),
            pl.BlockSpec((None, CHUNK, LANES), lambda k: (clip(k - 11, 0, 2), 0, 0)),
        ],
        out_shape=[
            jax.ShapeDtypeStruct((T, IN_W), BF16),
            jax.ShapeDtypeStruct((3, CONV_W), F32),
            jax.ShapeDtypeStruct((2, LANES, LANES), F32),
            jax.ShapeDtypeStruct((1, POOL_W), F32),
            jax.ShapeDtypeStruct((1, SGU_W), F32),
            jax.ShapeDtypeStruct((3, 2 * CHUNK, CHUNK), F32),
            jax.ShapeDtypeStruct((3, CHUNK, LANES), F32),
        ],
        scratch_shapes=[pltpu.VMEM((6, T, LANES), BF16), pltpu.VMEM((3, T, LANES), BF16)],
        compiler_params=_cparams(("arbitrary",)),
    )(proj, proj, proj, dmix, wconv, wpool_bd, pscale, lng, wsp, bias)


def _ln_fwd(prev, pg, pb, mmout, g, b, tm=256):
    T = prev.shape[0]

    def body(prev_ref, pg_ref, pb_ref, mm_ref, g_ref, b_ref, xhat_ref, rstd_ref, y_ref):
        r = ALPHA * (prev_ref[...] * pg_ref[...] + pb_ref[...]) + mm_ref[...]
        mu = jnp.mean(r, axis=-1, keepdims=True)
        xc = r - mu
        var = jnp.mean(xc * xc, axis=-1, keepdims=True)
        rstd = lax.rsqrt(var + LN_EPS)
        xhat = xc * rstd
        xhat_ref[...] = xhat
        rstd_ref[...] = rstd
        y_ref[...] = (xhat * g_ref[...] + b_ref[...]).astype(y_ref.dtype)

    row = pl.BlockSpec((tm, D_MODEL), lambda i: (i, 0))
    vec = pl.BlockSpec((1, D_MODEL), lambda i: (0, 0))
    return pl.pallas_call(
        body,
        name="ln_fwd",
        grid=(T // tm,),
        in_specs=[row, vec, vec, row, vec, vec],
        out_specs=[row, pl.BlockSpec((tm, 1), lambda i: (i, 0)), row],
        out_shape=[jax.ShapeDtypeStruct((T, D_MODEL), F32), jax.ShapeDtypeStruct((T, 1), F32),
                   jax.ShapeDtypeStruct((T, D_MODEL), BF16)],
        compiler_params=_cparams(("parallel",)),
    )(prev, pg, pb, mmout, g, b)


def _ln_bwd(dres, dmm, xhat, rstd, g, tm=256, deps=()):
    T = xhat.shape[0]
    has_res = dres is not None
    nd = len(deps)

    def body(*refs):
        refs = refs[:len(refs) - 4 - nd] + refs[len(refs) - 4:]
        if has_res:
            dres_ref, dmm_ref, xhat_ref, rstd_ref, g_ref, dr_ref, drb_ref, dg_ref, db_ref = refs
            dy = ALPHA * dres_ref[...] + dmm_ref[...]
        else:
            dmm_ref, xhat_ref, rstd_ref, g_ref, dr_ref, drb_ref, dg_ref, db_ref = refs
            dy = dmm_ref[...]
        xhat_v = xhat_ref[...]

        @pl.when(pl.program_id(0) == 0)
        def _():
            dg_ref[...] = jnp.zeros_like(dg_ref)
            db_ref[...] = jnp.zeros_like(db_ref)

        dg_ref[...] += jnp.sum(dy * xhat_v, axis=0, keepdims=True)
        db_ref[...] += jnp.sum(dy, axis=0, keepdims=True)
        dxh = dy * g_ref[...]
        m1 = jnp.mean(dxh, axis=-1, keepdims=True)
        m2 = jnp.mean(dxh * xhat_v, axis=-1, keepdims=True)
        dr = rstd_ref[...] * (dxh - m1 - xhat_v * m2)
        dr_ref[...] = dr
        drb_ref[...] = dr.astype(drb_ref.dtype)

    row = pl.BlockSpec((tm, D_MODEL), lambda i: (i, 0))
    vec = pl.BlockSpec((1, D_MODEL), lambda i: (0, 0))
    g_arr, g_spec = _vec(g)
    in_specs = ([row] if has_res else []) + [row, row, pl.BlockSpec((tm, 1), lambda i: (i, 0)), g_spec]
    in_specs += [pl.BlockSpec(memory_space=pl.ANY)] * nd
    args = ([dres] if has_res else []) + [dmm, xhat, rstd, g_arr] + list(deps)
    return pl.pallas_call(
        body,
        name="ln_bwd_res" if has_res else "ln_bwd",
        grid=(T // tm,),
        in_specs=in_specs,
        out_specs=[row, row, vec, vec],
        out_shape=[jax.ShapeDtypeStruct((T, D_MODEL), F32), jax.ShapeDtypeStruct((T, D_MODEL), BF16),
                   jax.ShapeDtypeStruct((1, D_MODEL), F32), jax.ShapeDtypeStruct((1, D_MODEL), F32)],
        compiler_params=_cparams(("arbitrary",)),
    )(*args)


def _loss_head(xhat, g, b, target, tm=256):
    T = xhat.shape[0]

    def body(xhat_ref, g_ref, b_ref, t_ref, loss_ref, dy_ref):
        err = xhat_ref[...] * g_ref[...] + b_ref[...] - t_ref[...]

        @pl.when(pl.program_id(0) == 0)
        def _():
            loss_ref[...] = jnp.zeros_like(loss_ref)

        part = jnp.sum(jnp.sum(err * err, axis=-1, keepdims=True), axis=0, keepdims=True)
        loss_ref[...] += jnp.broadcast_to(part * (0.5 / D_MODEL), loss_ref.shape)
        dy_ref[...] = err * (1.0 / D_MODEL)

    row = pl.BlockSpec((tm, D_MODEL), lambda i: (i, 0))
    (g_arr, g_spec), (b_arr, b_spec) = _vec(g), _vec(b)
    return pl.pallas_call(
        body,
        name="loss_head",
        grid=(T // tm,),
        in_specs=[row, g_spec, b_spec, row],
        out_specs=[pl.BlockSpec((8, LANES), lambda i: (0, 0)), row],
        out_shape=[jax.ShapeDtypeStruct((8, LANES), F32), jax.ShapeDtypeStruct((T, D_MODEL), F32)],
        compiler_params=_cparams(("arbitrary",)),
    )(xhat, g_arr, b_arr, target)


def _residual_out(dres, dmm, tm=256):
    T = dres.shape[0]

    def body(a_ref, b_ref, o_ref):
        o_ref[...] = ALPHA * a_ref[...] + b_ref[...]

    row = pl.BlockSpec((tm, D_MODEL), lambda i: (i, 0))
    return pl.pallas_call(
        body, name="residual_out", grid=(T // tm,), in_specs=[row, row], out_specs=row,
        out_shape=jax.ShapeDtypeStruct((T, D_MODEL), F32), compiler_params=_cparams(("parallel",)),
    )(dres, dmm)


SW_TC = 1408


def _swiglu_fwd(gu, tm=128):
    T = gu.shape[0]

    def body(gu_ref, o_ref):
        gv = gu_ref[:, :D_FF]
        o_ref[...] = (gv * jax.nn.sigmoid(gv) * gu_ref[:, D_FF:]).astype(o_ref.dtype)

    return pl.pallas_call(
        body, name="swiglu_fwd", grid=(T // tm,),
        in_specs=[pl.BlockSpec((tm, 2 * D_FF), lambda i: (i, 0))],
        out_specs=pl.BlockSpec((tm, D_FF), lambda i: (i, 0)),
        out_shape=jax.ShapeDtypeStruct((T, D_FF), BF16), compiler_params=_cparams(("parallel",)),
    )(gu)


def _swiglu_bwd(gu, dact, tm=128):
    T = gu.shape[0]

    def body(gu_ref, da_ref, dgu_ref, act_ref):
        gv, uv, da = gu_ref[:, :D_FF], gu_ref[:, D_FF:], da_ref[...]
        s = jax.nn.sigmoid(gv)
        sg = gv * s
        act_ref[...] = (sg * uv).astype(act_ref.dtype)
        dgu_ref[:, D_FF:] = (da * sg).astype(dgu_ref.dtype)
        dgu_ref[:, :D_FF] = (da * uv * (s * (1.0 + gv * (1.0 - s)))).astype(dgu_ref.dtype)

    wide = pl.BlockSpec((tm, 2 * D_FF), lambda i: (i, 0))
    half = pl.BlockSpec((tm, D_FF), lambda i: (i, 0))
    return pl.pallas_call(
        body, name="swiglu_bwd", grid=(T // tm,),
        in_specs=[wide, half], out_specs=[wide, half],
        out_shape=[jax.ShapeDtypeStruct((T, 2 * D_FF), BF16), jax.ShapeDtypeStruct((T, D_FF), BF16)],
        compiler_params=_cparams(("parallel",)),
    )(gu, dact)


def _adamw(w, g, m, v, tr):
    R, C = w.shape[-2:]
    assert R % tr == 0
    c1 = 1.0 - ADAM_B1 ** ADAM_STEP
    c2 = 1.0 - ADAM_B2 ** ADAM_STEP

    def body(w_ref, g_ref, m_ref, v_ref, d_ref, mo_ref, vo_ref):
        gv = g_ref[...]
        mn = ADAM_B1 * m_ref[...] + (1.0 - ADAM_B1) * gv
        vn = ADAM_B2 * v_ref[...] + (1.0 - ADAM_B2) * (gv * gv)
        d_ref[...] = -ADAM_LR * ((mn / c1) / (jnp.sqrt(vn / c2) + ADAM_EPS) + ADAM_WD * w_ref[...])
        mo_ref[...] = mn
        vo_ref[...] = vn

    if w.ndim == 2:
        grid, blk = (R // tr,), pl.BlockSpec((tr, C), lambda i: (i, 0))
    else:
        grid, blk = (w.shape[0], R // tr), pl.BlockSpec((None, tr, C), lambda l, i: (l, i, 0))
    return pl.pallas_call(
        body, name="adamw", grid=grid, in_specs=[blk] * 4, out_specs=[blk] * 3,
        out_shape=[jax.ShapeDtypeStruct(w.shape, F32)] * 3, compiler_params=_cparams(("parallel",) * len(grid)),
    )(w, g, m, v)


def _my_place():
    return lax.axis_index("x"), lax.axis_index("y"), lax.axis_index("c")


ANY = pl.BlockSpec(memory_space=pl.ANY)
HBM = pl.BlockSpec(memory_space=pltpu.HBM)
SEM = pl.BlockSpec(memory_space=pltpu.SEMAPHORE)
EFFECT = pltpu.SideEffectType.DATAFLOW_SIDE_EFFECTING


def _in_hbm(a):
    return pltpu.with_memory_space_constraint(a, pltpu.HBM)


def _block_rows(ref, dev):
    r = ref.shape[0] // N_DEV
    start = pl.multiple_of((4 * dev[0] + 2 * dev[1] + dev[2]) * r, 16)
    return ref.at[pl.ds(start, r), :]


def _ag_first_copies(s_refs, land_refs, send_sems, recv_sems, receiving):
    x, y, c = _my_place()
    peers = [(x, y, 1 - c)] + [(*chip, c) for chip in _other_chips(x, y)]
    copies = []
    for k, peer in enumerate(peers):
        block = peer if receiving else (x, y, c)
        copies += [pltpu.make_async_remote_copy(
            src_ref=s_refs[w], dst_ref=_block_rows(land_refs[w], block),
            send_sem=send_sems.at[k * len(s_refs) + w], recv_sem=recv_sems.at[k * len(s_refs) + w],
            device_id=peer, device_id_type=MESH)
            for w in range(len(s_refs))]
    return copies


def _ag_start(shards, layer, after=()):
    nw = len(shards)

    def body(*refs):
        s_refs, land_refs = refs[:nw], refs[nw:2 * nw]
        token = refs[-1]
        sems = 2 * nw + len(after)
        for cp in _ag_first_copies(s_refs, land_refs, refs[sems], refs[sems + 1], False):
            cp.start()
        token[...] = jnp.zeros_like(token)

    lands = [lax.empty((N_DEV * s.shape[0], D_MODEL), BF16) for s in shards]
    out = pl.pallas_call(
        body, name="ag_start_%s" % layer,
        in_specs=[HBM] * (2 * nw) + [ANY] * len(after),
        out_specs=(SEM, SEM, *[HBM] * (2 * nw), pl.BlockSpec(memory_space=pltpu.VMEM)),
        out_shape=(pltpu.SemaphoreType.DMA((4 * nw,)), pltpu.SemaphoreType.DMA((4 * nw,)),
                   *[pltpu.HBM(a.shape, a.dtype) for a in list(shards) + lands],
                   jax.ShapeDtypeStruct((8, LANES), F32)),
        input_output_aliases={i: 2 + i for i in range(2 * nw)},
        compiler_params=pltpu.CompilerParams(has_side_effects=EFFECT),
    )(*[_in_hbm(a) for a in list(shards) + lands], *after)
    return out[0], out[1], out[2:2 + nw], out[2 + nw:2 + 2 * nw], out[-1]


def _ag_wait(send_sems, recv_sems, shards, lands, after, layer):
    nw = len(shards)

    def body(*refs):
        s_refs, land_refs = refs[:nw], refs[nw:2 * nw]
        for cp in _ag_first_copies(s_refs, land_refs, refs[2 * nw], refs[2 * nw + 1], True):
            cp.wait_send()
            cp.wait_recv()

    out = pl.pallas_call(
        body, name="ag_wait_%s" % layer,
        in_specs=[HBM] * (2 * nw) + [SEM, SEM] + [ANY] * len(after),
        out_specs=[HBM] * (2 * nw),
        out_shape=[pltpu.HBM(a.shape, a.dtype) for a in list(shards) + list(lands)],
        input_output_aliases={i: i for i in range(2 * nw)},
        compiler_params=pltpu.CompilerParams(has_side_effects=EFFECT),
    )(*shards, *lands, send_sems, recv_sems, *after)
    return out[:nw], out[nw:]


def _ag_pass_on(shards, lands):
    nw = len(shards)

    def body(*refs):
        s_refs, g_refs = refs[:nw], refs[2 * nw:3 * nw]
        send_sems, recv_sems, local_sems = refs[3 * nw:3 * nw + 3]
        stage = refs[3 * nw + 3:]
        x, y, c = _my_place()
        load = [pltpu.make_async_copy(s_refs[w], stage[w], local_sems.at[w]) for w in range(nw)]
        mine = [pltpu.make_async_copy(stage[w], _block_rows(g_refs[w], (x, y, c)), local_sems.at[w])
                for w in range(nw)]
        for cp in load:
            cp.start()
        sends, arrivals = [], []
        for j, chip in enumerate(_other_chips(x, y)):
            for w in range(nw):
                rows_out = _block_rows(g_refs[w], (*chip, c))
                rows_in = _block_rows(g_refs[w], (*chip, 1 - c))
                sends.append(pltpu.make_async_remote_copy(
                    src_ref=rows_out, dst_ref=rows_out, send_sem=send_sems.at[j, w], recv_sem=recv_sems.at[j, w],
                    device_id=(x, y, 1 - c), device_id_type=MESH))
                arrivals.append(pltpu.make_async_remote_copy(
                    src_ref=rows_in, dst_ref=rows_in, send_sem=send_sems.at[j, w], recv_sem=recv_sems.at[j, w],
                    device_id=(x, y, 1 - c), device_id_type=MESH))
        for cp in sends:
            cp.start()
        for w in range(nw):
            load[w].wait()
            mine[w].start()
        for cp in arrivals:
            cp.wait_recv()
        for cp in sends:
            cp.wait_send()
        for cp in mine:
            cp.wait()

    return pl.pallas_call(
        body, name="ag_pass_on",
        in_specs=[ANY] * (2 * nw), out_specs=[ANY] * nw,
        out_shape=[jax.ShapeDtypeStruct(a.shape, a.dtype) for a in lands],
        input_output_aliases={nw + i: i for i in range(nw)},
        scratch_shapes=[pltpu.SemaphoreType.DMA((3, nw)), pltpu.SemaphoreType.DMA((3, nw)),
                        pltpu.SemaphoreType.DMA((nw,))] + [pltpu.VMEM(s.shape, s.dtype) for s in shards],
        compiler_params=_cparams(),
    )(*shards, *lands)


def _rs_sibling_copies(p_refs, land_refs, send_sems, recv_sems):
    x, y, c = _my_place()
    return [pltpu.make_async_remote_copy(
        src_ref=p_refs[w].at[:, 1 - c], dst_ref=land_refs[w],
        send_sem=send_sems.at[w], recv_sem=recv_sems.at[w], device_id=(x, y, 1 - c), device_id_type=MESH)
        for w in range(len(p_refs))]


def _rs_sibling_start(parts, tag, after=()):
    nw = len(parts)
    sems = 2 * nw + len(after)

    def body(*refs):
        for cp in _rs_sibling_copies(refs[:nw], refs[nw:2 * nw], refs[sems], refs[sems + 1]):
            cp.start()
        refs[-1][...] = jnp.zeros_like(refs[-1])

    lands = [lax.empty(p.shape[:1] + p.shape[2:], BF16) for p in parts]
    out = pl.pallas_call(
        body, name="rs_sibling_start_%s" % tag,
        in_specs=[HBM] * (2 * nw) + [ANY] * len(after),
        out_specs=(SEM, SEM, *[HBM] * (2 * nw), pl.BlockSpec(memory_space=pltpu.VMEM)),
        out_shape=(pltpu.SemaphoreType.DMA((nw,)), pltpu.SemaphoreType.DMA((nw,)),
                   *[pltpu.HBM(a.shape, a.dtype) for a in list(parts) + lands],
                   jax.ShapeDtypeStruct((8, LANES), F32)),
        input_output_aliases={i: 2 + i for i in range(2 * nw)},
        compiler_params=pltpu.CompilerParams(has_side_effects=EFFECT),
    )(*[_in_hbm(a) for a in list(parts) + lands], *after)
    return out[0], out[1], out[2:2 + nw], out[2 + nw:2 + 2 * nw], out[-1]


def _rs_sibling_wait(send_sems, recv_sems, parts, lands, after, tag):
    nw = len(parts)

    def body(*refs):
        for cp in _rs_sibling_copies(refs[:nw], refs[nw:2 * nw], refs[2 * nw], refs[2 * nw + 1]):
            cp.wait_send()
            cp.wait_recv()

    out = pl.pallas_call(
        body, name="rs_sibling_wait_%s" % tag,
        in_specs=[HBM] * (2 * nw) + [SEM, SEM] + [ANY] * len(after),
        out_specs=[HBM] * (2 * nw),
        out_shape=[pltpu.HBM(a.shape, a.dtype) for a in list(parts) + list(lands)],
        input_output_aliases={i: i for i in range(2 * nw)},
        compiler_params=pltpu.CompilerParams(has_side_effects=EFFECT),
    )(*parts, *lands, send_sems, recv_sems, *after)
    return out[:nw], out[nw:]


def _rs_chip_sum(parts, gots, c):
    n = len(parts)

    def body(c_ref, *refs):
        for p_ref, g_ref, o_ref in zip(refs[:n], refs[n:2 * n], refs[2 * n:]):
            o_ref[...] = (p_ref[...].astype(F32) + g_ref[...].astype(F32)).astype(o_ref.dtype)

    mine = [pl.BlockSpec((None, None, p.shape[2], D_MODEL), lambda q, c_ref: (q, c_ref[0], 0, 0)) for p in parts]
    theirs = [pl.BlockSpec((None, g.shape[1], D_MODEL), lambda q, c_ref: (q, 0, 0)) for g in gots]
    return pl.pallas_call(
        body, name="rs_chip_sum",
        grid_spec=pltpu.PrefetchScalarGridSpec(
            num_scalar_prefetch=1, grid=(4,), in_specs=mine + theirs, out_specs=theirs),
        out_shape=[jax.ShapeDtypeStruct(g.shape, BF16) for g in gots],
        compiler_params=_cparams(("parallel",)),
    )(c, *parts, *gots)


def _other_chips(x, y):
    return [(1 - x, y), (x, 1 - y), (1 - x, 1 - y)]


def _rs_chip_copies(s_refs, land_refs, send_sems, recv_sems):
    x, y, c = _my_place()
    copies = []
    for k, chip in enumerate(_other_chips(x, y)):
        q = 2 * chip[0] + chip[1]
        copies += [pltpu.make_async_remote_copy(
            src_ref=s_refs[w].at[q], dst_ref=land_refs[w].at[k],
            send_sem=send_sems.at[k * len(s_refs) + w], recv_sem=recv_sems.at[k * len(s_refs) + w],
            device_id=(*chip, c), device_id_type=MESH)
            for w in range(len(s_refs))]
    return copies


def _rs_chip_start(sums, layer):
    nw = len(sums)

    def body(*refs):
        s_refs, land_refs = refs[:nw], refs[nw:2 * nw]
        send_sems, recv_sems = refs[2 * nw], refs[2 * nw + 1]
        token = refs[-1]
        for cp in _rs_chip_copies(s_refs, land_refs, send_sems, recv_sems):
            cp.start()
        token[...] = jnp.zeros_like(token)

    lands = [lax.empty((3,) + s.shape[1:], BF16) for s in sums]
    out = pl.pallas_call(
        body, name="rs_chip_start_%s" % layer,
        in_specs=[HBM] * (2 * nw),
        out_specs=(SEM, SEM, *[HBM] * (2 * nw), pl.BlockSpec(memory_space=pltpu.VMEM)),
        out_shape=(pltpu.SemaphoreType.DMA((3 * nw,)), pltpu.SemaphoreType.DMA((3 * nw,)),
                   *[pltpu.HBM(a.shape, a.dtype) for a in list(sums) + lands],
                   jax.ShapeDtypeStruct((8, LANES), F32)),
        input_output_aliases={i: 2 + i for i in range(2 * nw)},
        compiler_params=pltpu.CompilerParams(has_side_effects=EFFECT),
    )(*[_in_hbm(a) for a in list(sums) + lands])
    return out[0], out[1], out[2:2 + nw], out[2 + nw:2 + 2 * nw], out[-1]


def _rs_chip_wait(send_sems, recv_sems, sums, lands, after, layer):
    nw = len(sums)

    def body(*refs):
        s_refs, land_refs = refs[:nw], refs[nw:2 * nw]
        for cp in _rs_chip_copies(s_refs, land_refs, refs[2 * nw], refs[2 * nw + 1]):
            cp.wait_send()
            cp.wait_recv()

    out = pl.pallas_call(
        body, name="rs_chip_wait_%s" % layer,
        in_specs=[HBM] * (2 * nw) + [SEM, SEM] + [ANY] * len(after),
        out_specs=[HBM] * (2 * nw),
        out_shape=[pltpu.HBM(a.shape, a.dtype) for a in list(sums) + list(lands)],
        input_output_aliases={i: i for i in range(2 * nw)},
        compiler_params=pltpu.CompilerParams(has_side_effects=EFFECT),
    )(*sums, *lands, send_sems, recv_sems, *after)
    return out[:nw], out[nw:]


def _rs_finish(sums, gots, q, layer, into):
    n = len(sums)

    def body(q_ref, *refs):
        for s_ref, g_ref, o_ref in zip(refs[:n], refs[n:2 * n], refs[len(refs) - n:]):
            o_ref[...] = ((s_ref[...].astype(F32) + g_ref[0].astype(F32)) + g_ref[1].astype(F32)) + g_ref[2].astype(F32)

    rows = [s.shape[1] for s in sums]
    in_specs = [pl.BlockSpec((None, r, D_MODEL), lambda i, q_ref: (q_ref[0], 0, 0)) for r in rows]
    in_specs += [pl.BlockSpec((3, r, D_MODEL), lambda i, q_ref: (0, 0, 0)) for r in rows]
    args = [q, *sums, *gots]
    aliases = {}
    if into is not None:
        in_specs += [ANY] * n
        aliases = {len(args) + i: i for i in range(n)}
        args += list(into)
    return pl.pallas_call(
        body, name="rs_finish",
        grid_spec=pltpu.PrefetchScalarGridSpec(
            num_scalar_prefetch=1, grid=(1,), in_specs=in_specs,
            out_specs=[pl.BlockSpec((None, r, D_MODEL), lambda i, q_ref: (layer, 0, 0)) for r in rows]),
        out_shape=[jax.ShapeDtypeStruct((DEPTH, r, D_MODEL), F32) for r in rows],
        input_output_aliases=aliases,
        compiler_params=_cparams(("arbitrary",)),
    )(*args)


def _allreduce_small(vec, deps=()):
    R = vec.shape[0]
    assert R % (8 * N_DEV) == 0
    P = R // N_DEV
    nd = len(deps)

    def body(*refs):
        v_ref = refs[0]
        o_ref, buf, send1, recv1, send2, recv2 = refs[1 + nd:]
        x, y, c = _my_place()
        me = 4 * x + 2 * y + c

        def piece(ref, d):
            return ref.at[pl.ds(pl.multiple_of(d * P, 8), P), :]

        def peer(k):
            p = me ^ k
            return p, (p >> 2, (p >> 1) & 1, p & 1)

        scatter = []
        for k in range(1, N_DEV):
            p, where = peer(k)
            scatter.append(pltpu.make_async_remote_copy(
                src_ref=piece(v_ref, p), dst_ref=buf.at[k], send_sem=send1.at[k - 1], recv_sem=recv1.at[k - 1],
                device_id=where, device_id_type=MESH))
        for cp in scatter:
            cp.start()
        buf[0] = piece(v_ref, me)[...]
        for cp in scatter:
            cp.wait()
        acc = buf[me]
        for d in range(1, N_DEV):
            acc = acc + buf[me ^ d]
        piece(o_ref, me)[...] = acc
        spread, arrivals = [], []
        for k in range(1, N_DEV):
            p, where = peer(k)
            spread.append(pltpu.make_async_remote_copy(
                src_ref=piece(o_ref, me), dst_ref=piece(o_ref, me), send_sem=send2.at[k - 1], recv_sem=recv2.at[k - 1],
                device_id=where, device_id_type=MESH))
            arrivals.append(pltpu.make_async_remote_copy(
                src_ref=piece(o_ref, p), dst_ref=piece(o_ref, p), send_sem=send2.at[k - 1], recv_sem=recv2.at[k - 1],
                device_id=where, device_id_type=MESH))
        for cp in spread:
            cp.start()
        for cp in arrivals:
            cp.wait_recv()
        for cp in spread:
            cp.wait_send()

    sems = pltpu.SemaphoreType.DMA((N_DEV - 1,))
    return pl.pallas_call(
        body, name="allreduce_small",
        in_specs=[pl.BlockSpec(memory_space=pltpu.VMEM)] + [ANY] * nd, out_specs=pl.BlockSpec(memory_space=pltpu.VMEM),
        out_shape=jax.ShapeDtypeStruct((R, LANES), F32),
        scratch_shapes=[pltpu.VMEM((N_DEV, P, LANES), F32), sems, sems, sems, sems],
        compiler_params=_cparams(),
    )(vec, *deps)


def _pack(arrs):
    flat = jnp.concatenate([a.reshape(-1) for a in arrs])
    pad = (-flat.shape[0]) % (8 * N_DEV * LANES)
    return jnp.pad(flat, (0, pad)).reshape(-1, LANES)


def _unpack(packed, shapes):
    flat = packed.reshape(-1)
    out, off = [], 0
    for s in shapes:
        n = math.prod(s)
        out.append(flat[off:off + n].reshape(s))
        off += n
    return out


def kernel(x, w_in, w_conv, w_pool, pool_scale, sgu_ln_g, w_spatial, b_spatial, w_o, ln1_g, ln1_b, w_gate_up, w_down, ln2_g, ln2_b, loss_target, m_w_in, m_w_conv, m_w_pool, m_pool_scale, m_sgu_ln_g, m_w_spatial, m_b_spatial, m_w_o, m_ln1_g, m_ln1_b, m_w_gate_up, m_w_down, m_ln2_g, m_ln2_b, v_w_in, v_w_conv, v_w_pool, v_pool_scale, v_sgu_ln_g, v_w_spatial, v_b_spatial, v_w_o, v_ln1_g, v_ln1_b, v_w_gate_up, v_w_down, v_ln2_g, v_ln2_b):
    L = DEPTH
    T = x.shape[1]
    mx, my, mc = _my_place()
    dev = 4 * mx + 2 * my + mc
    xs = x[0]
    target = loss_target[0]

    conv_cols = w_conv.shape[2]
    w_conv_z = lax.dynamic_update_slice(jnp.zeros((L, 3, CONV_W), F32), w_conv, (0, 0, dev * conv_cols))
    w_conv_packed = _allreduce_small(_pack([w_conv_z]))
    w_conv_full = _unpack(w_conv_packed, [(L, 3, CONV_W)])[0]

    shards = (jnp.swapaxes(w_in, 1, 2).astype(BF16), jnp.swapaxes(w_gate_up, 1, 2).astype(BF16),
              w_o.astype(BF16), w_down.astype(BF16))
    first_gather = _ag_start_layer(shards, 0, [w_conv_packed])

    loss_tile, grad_x2, big_grads, small_grads = _local_step(
        xs, target, shards, first_gather, w_conv_full, w_pool, pool_scale, sgu_ln_g, w_spatial, b_spatial,
        ln1_g, ln1_b, ln2_g, ln2_b)
    loss = lax.psum(loss_tile[0, 0], ("x", "y", "c"))
    grad_x = grad_x2[None]
    big_w = (w_in, w_gate_up, w_o, w_down)
    big_m = (m_w_in, m_w_gate_up, m_w_o, m_w_down)
    big_v = (v_w_in, v_w_gate_up, v_w_o, v_w_down)
    small_w = [w_conv_full, w_pool, pool_scale, sgu_ln_g, w_spatial, b_spatial, ln1_g, ln1_b, ln2_g, ln2_b]
    small_m = [m_w_conv, m_w_pool, m_pool_scale, m_sgu_ln_g, m_w_spatial, m_b_spatial, m_ln1_g, m_ln1_b, m_ln2_g, m_ln2_b]
    small_v = [v_w_conv, v_w_pool, v_pool_scale, v_sgu_ln_g, v_w_spatial, v_b_spatial, v_ln1_g, v_ln1_b, v_ln2_g, v_ln2_b]
    grads, deltas, new_m, new_v = _reduce_and_update(
        big_grads, small_grads, big_w, big_m, big_v, small_w, small_m, small_v)
    return (loss, grad_x, *grads, *deltas, *new_m, *new_v)


def _ag_start_layer(shards, l, after):
    s_in, s_gu, s_o, s_dn = [s[l] for s in shards]
    first = _ag_start([s_in, s_o], "%da" % l, after=after)
    return first, _ag_start([s_gu, s_dn], "%db" % l, after=[first[4]])


def _ag_finish(gather, after, tag):
    send_sems, recv_sems, shards, lands, _ = gather
    shards, lands = _ag_wait(send_sems, recv_sems, shards, lands, after, tag)
    return _ag_pass_on(shards, lands)


def _rs_begin(parts, tag, after=()):
    return _rs_sibling_start([p.reshape(4, 2, p.shape[0] // N_DEV, D_MODEL) for p in parts], tag, after)


def _rs_continue(sibling_flight, after, c_arr, tag):
    send_sems, recv_sems, parts, lands, _ = sibling_flight
    parts, got = _rs_sibling_wait(send_sems, recv_sems, parts, lands, after, tag)
    return _rs_chip_start(_rs_chip_sum(parts, got, c_arr), tag)


def _local_step(xs, target, shards, gather, w_conv_full, w_pool, pool_scale, sgu_ln_g, w_spatial, b_spatial,
                ln1_g, ln1_b, ln2_g, ln2_b):
    L = DEPTH
    T = xs.shape[0]
    mx, my, mc = _my_place()
    c_arr = jnp.reshape(mc, (1,)).astype(jnp.int32)
    q_arr = jnp.reshape(2 * mx + my, (1,)).astype(jnp.int32)
    eye2 = jnp.eye(2, dtype=F32)
    wp = w_pool.reshape(L, 2, 2, HALF, HALF)
    wpool_bd = jnp.einsum("ltgcd,gh->ltgchd", wp, eye2).reshape(L, 2, LANES, LANES)
    wsp_t = w_spatial.reshape(L, 3, 2 * CHUNK, CHUNK)
    bias_t = jnp.repeat(jnp.swapaxes(b_spatial.reshape(L, 3, 2, CHUNK), 2, 3), HALF, axis=3)
    mixer_w = (w_conv_full, wpool_bd, pool_scale[:, None, :], sgu_ln_g[:, None, :], wsp_t, bias_t)
    g1, b1, g2, b2 = [a[:, None, :] for a in (ln1_g, ln1_b, ln2_g, ln2_b)]
    one, zero = jnp.ones((1, 1, D_MODEL), F32), jnp.zeros((1, 1, D_MODEL), F32)

    saved = []
    prev, pg, pb = xs, (one, 0), (zero, 0)
    prev_b = xs.astype(BF16)
    weights = []
    for l in range(L):
        g_in, g_o = _ag_finish(gather[0], [] if l == 0 else [prev_b], "%da" % l)
        proj = _mm(prev_b, g_in, "nt", F32, 512, IN_W, D_MODEL, "mm_proj", deps=[gather[1][4]] if l == 0 else [])
        mixcat = _mixer_fwd(proj, *mixer_w, l)
        xhat1, rstd1, h_b = _mm_ln_fwd(mixcat, g_o, prev, pg, pb, (g1, l), (b1, l), "mm_wo_ln")
        g_gu, g_dn = _ag_finish(gather[1], [h_b], "%db" % l)
        weights.append((g_in, g_gu, g_o, g_dn))
        deps = []
        if l + 1 < L:
            gather = _ag_start_layer(shards, l + 1, [g_gu])
            deps = [gather[1][4]]
        g_act, u_act, act = _mm_swiglu_fwd(h_b, g_gu, deps=deps)
        xhat2, rstd2, y_b = _mm_ln_fwd(act, g_dn, xhat1, (g1, l), (b1, l), (g2, l), (b2, l), "mm_down_ln")
        saved.append((prev_b, proj, mixcat, xhat1, rstd1, h_b, g_act, u_act, act, xhat2, rstd2))
        prev, pg, pb, prev_b = xhat2, (g2, l), (b2, l), y_b

    loss_tile, dy = _loss_head(prev, pg, pb, target)

    small = [None] * L
    big = None
    sibling_flight = None
    above = None
    for l in reversed(range(L)):
        prev_b, proj, mixcat, xhat1, rstd1, h_b, g_act, u_act, act, xhat2, rstd2 = saved[l]
        g_in, g_gu, g_o, g_dn = weights[l]
        chip_flight = None
        if above is None:
            dr2, dr2_b, dg2, db2 = _ln_bwd(None, dy, xhat2, rstd2, (g2, l))
        else:
            dr2, dr2_b, dg2, db2 = _mm_ln_bwd([above[0]], above[1], above[2], xhat2, rstd2, (g2, l),
                                              "mm_dx_ln", deps=[sibling_flight[4]])
            chip_flight = _rs_continue(sibling_flight, [dr2_b], c_arr, str(l + 1))
        dg_b, du_b = _mm_swiglu_bwd(dr2_b, g_dn, g_act, u_act, deps=[chip_flight[4]] if chip_flight else [])
        p_dn = _mm(act, dr2_b, "tn", BF16, DW_TM, D_MODEL, T, "mm_dw_down")
        p_gu = _mm_tn_pair(dg_b, du_b, h_b, DW_TM, "mm_dw_gate_up")
        ffn_sibling = _rs_begin([p_gu, p_dn], "0b") if l == 0 else None
        dr1, dr1_b, dg1, db1, dmix = _mm_ln_bwd([dg_b, du_b], g_gu, dr2, xhat1, rstd1, (g1, l), "mm_dh_ln",
                                                deps=[ffn_sibling[4]] if l == 0 else [], w_back=g_o)
        ffn_flight = _rs_continue(ffn_sibling, [dr1_b], c_arr, "0b") if l == 0 else None
        p_o = _mm(mixcat, dr1_b, "tn", BF16, 512, D_MODEL, T, "mm_dw_o", deps=[ffn_flight[4]] if l == 0 else [])
        dproj, dwc, dwp, dps, dlng, dwsp, dbias = _mixer_bwd(proj, dmix, *mixer_w, l)
        p_in = _mm(dproj, prev_b, "tn", BF16, IN_W, D_MODEL, T, "mm_dw_in")
        small[l] = (dwc, dwp, dps, dlng, dwsp, dbias, dg1, db1, dg2, db2)
        above = (dproj, g_in, dr1)
        if chip_flight is not None:
            big = list(_rs_chip_finish(chip_flight, [p_in], q_arr, str(l + 1), l + 1, big))
        if l > 0:
            sibling_flight = _rs_begin([p_in, p_gu, p_o, p_dn], str(l))
        else:
            big[1], big[3] = _rs_chip_finish(ffn_flight, [p_in, p_o], q_arr, "0b", 0, [big[1], big[3]])

    def stack(i):
        return jnp.stack([small[l][i] for l in range(L)])

    dwp_bd = stack(1).reshape(L, 2, 2, HALF, 2, HALF)
    dwp_all = jnp.einsum("ltgchd,gh->ltgcd", dwp_bd, eye2).reshape(L, 4, HALF, HALF)
    dbs_all = jnp.swapaxes(stack(5)[:, :, :, :2], 2, 3).reshape(L, 6, CHUNK)
    small_grads = [stack(0), dwp_all, stack(2).reshape(L, POOL_W), stack(3).reshape(L, SGU_W),
                   stack(4).reshape(L, 6, CHUNK, CHUNK), dbs_all] + [stack(i).reshape(L, D_MODEL) for i in (6, 7, 8, 9)]
    packed_small = _allreduce_small(_pack(small_grads), deps=[big[1]])
    sibling_flight = _rs_begin([p_in, p_o], "0a", after=[packed_small])
    grad_x = _mm_ln_bwd([above[0]], above[1], above[2], None, None, None, "mm_dx_out", deps=[sibling_flight[4]])
    last_flight = _rs_continue(sibling_flight, [grad_x], c_arr, "0a")
    return loss_tile, grad_x, (big, last_flight, q_arr), (packed_small, [a.shape for a in small_grads])


def _rs_chip_finish(in_flight, after, q, tag, layer, into):
    send_sems, recv_sems, sums, lands, _ = in_flight
    sums, got = _rs_chip_wait(send_sems, recv_sems, sums, lands, after, tag)
    return _rs_finish(sums, got, q, layer, into)


def _reduce_and_update(big_grads, small_grads, big_w, big_m, big_v, small_w, small_m, small_v):
    L = DEPTH
    mx, my, mc = _my_place()
    dev = 4 * mx + 2 * my + mc
    conv_cols = CONV_W // N_DEV
    w_in, w_gate_up, w_o, w_down = big_w
    m_w_in, m_w_gate_up, m_w_o, m_w_down = big_m
    v_w_in, v_w_gate_up, v_w_o, v_w_down = big_v
    packed_g, small_shapes = small_grads
    big, last_flight, q_arr = big_grads

    def widen_conv(a):
        return lax.dynamic_update_slice(jnp.zeros((L, 3, CONV_W), F32), a, (0, 0, dev * conv_cols))

    small_m = [widen_conv(small_m[0])] + list(small_m[1:])
    small_v = [widen_conv(small_v[0])] + list(small_v[1:])
    pk_d, pk_m, pk_v = _adamw(_pack(small_w), packed_g, _pack(small_m), _pack(small_v), packed_g.shape[0] // 2)
    sg = _unpack(packed_g, small_shapes)
    sd = _unpack(pk_d, small_shapes)
    sm = _unpack(pk_m, small_shapes)
    sv = _unpack(pk_v, small_shapes)

    def conv_cols_of(a):
        return lax.dynamic_slice(a, (0, 0, dev * conv_cols), (L, 3, conv_cols))

    for lst in (sg, sd, sm, sv):
        lst[0] = conv_cols_of(lst[0])

    tr = lambda a: jnp.swapaxes(a, 1, 2)
    gt_gu, g_w_dn = big[1], big[3]
    d_gu, m_gu, v_gu = [tr(a) for a in _adamw(tr(w_gate_up), gt_gu, tr(m_w_gate_up), tr(v_w_gate_up), gt_gu.shape[1] // 2)]
    d_dn, m_dn, v_dn = _adamw(w_down, g_w_dn, m_w_down, v_w_down, 352)
    gt_in, g_w_o = _rs_chip_finish(last_flight, [d_gu, d_dn, pk_d], q_arr, "0a", 0, [big[0], big[2]])
    d_in, m_in, v_in = [tr(a) for a in _adamw(tr(w_in), gt_in, tr(m_w_in), tr(v_w_in), gt_in.shape[1])]
    d_o, m_o, v_o = _adamw(w_o, g_w_o, m_w_o, v_w_o, 128)
    g_w_in, g_w_gu = tr(gt_in), tr(gt_gu)

    def ordered(big_in, big_o, big_gu, big_dn, sm_list):
        return [big_in, sm_list[0], sm_list[1], sm_list[2], sm_list[3], sm_list[4], sm_list[5], big_o,
                sm_list[6], sm_list[7], big_gu, big_dn, sm_list[8], sm_list[9]]

    grads = ordered(g_w_in, g_w_o, g_w_gu, g_w_dn, sg)
    deltas = ordered(d_in, d_o, d_gu, d_dn, sd)
    new_m = ordered(m_in, m_o, m_gu, m_dn, sm)
    new_v = ordered(v_in, v_o, v_gu, v_dn, sv)
    return grads, deltas, new_m, new_v
```

```python
import math

import jax
import jax.numpy as jnp
from jax import lax
from jax.experimental import pallas as pl
from jax.experimental.pallas import tpu as pltpu

F32 = jnp.float32
BF16 = jnp.bfloat16
MESH = pl.DeviceIdType.MESH

D_MODEL = 1024
DEPTH = 4
CONV_W = 384
POOL_W = 256
SGU_W = 384
IN_W = 3 * CONV_W + POOL_W + 2 * SGU_W
D_FF = 2816
CHUNK = 128
ALPHA = float((2 * DEPTH) ** 0.25)
LN_EPS = 1e-5
ADAM_LR, ADAM_B1, ADAM_B2, ADAM_EPS, ADAM_WD, ADAM_STEP = 0.001, 0.9, 0.999, 1e-08, 0.01, 10

N_DEV = 8
LANES = 128
HALF = 64
VMEM_LIMIT = 52 * 1024 * 1024

INV_SQRT2 = 0.7071067811865476
INV_SQRT_2PI = 0.3989422804014327


def _cparams(sem=None, **kw):
    if sem is not None:
        kw["dimension_semantics"] = sem
    return pltpu.CompilerParams(vmem_limit_bytes=VMEM_LIMIT, **kw)


_DN = {"nt": (((1,), (1,)), ((), ())), "tn": (((0,), (0,)), ((), ()))}


def _mm(a, b, mode, out_dtype, tm, tn, name, deps=()):
    if mode == "nt":
        (M, K), N = a.shape, b.shape[0]
        a_spec = pl.BlockSpec((tm, K), lambda i, j: (i, 0))
        b_spec = pl.BlockSpec((tn, K), lambda i, j: (j, 0))
    else:
        (K, M), N = a.shape, b.shape[1]
        a_spec = pl.BlockSpec((K, tm), lambda i, j: (0, i))
        b_spec = pl.BlockSpec((K, tn), lambda i, j: (0, j))
    assert M % tm == 0 and N % tn == 0, (M, N, K, tm, tn)
    nd = len(deps)

    def body(*refs):
        a_ref, b_ref, o_ref = refs[0], refs[1], refs[2 + nd]
        o_ref[...] = lax.dot_general(a_ref[...], b_ref[...], _DN[mode], preferred_element_type=F32).astype(o_ref.dtype)

    return pl.pallas_call(
        body,
        name=name,
        grid=(M // tm, N // tn),
        in_specs=[a_spec, b_spec] + [pl.BlockSpec(memory_space=pl.ANY)] * nd,
        out_specs=pl.BlockSpec((tm, tn), lambda i, j: (i, j)),
        out_shape=jax.ShapeDtypeStruct((M, N), out_dtype),
        compiler_params=_cparams(("parallel", "parallel")),
    )(a, b, *deps)


def _mm_tn_pair(a1, a2, b, tm, name):
    K, M = a1.shape
    N = b.shape[1]
    n1 = M // tm

    def body(a1_ref, a2_ref, b_ref, o_ref):
        i = pl.program_id(0)

        @pl.when(i < n1)
        def _():
            o_ref[...] = lax.dot_general(a1_ref[...], b_ref[...], _DN["tn"], preferred_element_type=F32).astype(o_ref.dtype)

        @pl.when(i >= n1)
        def _():
            o_ref[...] = lax.dot_general(a2_ref[...], b_ref[...], _DN["tn"], preferred_element_type=F32).astype(o_ref.dtype)

    return pl.pallas_call(
        body, name=name, grid=(2 * n1,),
        in_specs=[pl.BlockSpec((K, tm), lambda i: (0, jnp.minimum(i, n1 - 1))),
                  pl.BlockSpec((K, tm), lambda i: (0, jnp.maximum(i - n1, 0))),
                  pl.BlockSpec((K, N), lambda i: (0, 0))],
        out_specs=pl.BlockSpec((tm, N), lambda i: (i, 0)),
        out_shape=jax.ShapeDtypeStruct((2 * M, N), BF16),
        compiler_params=_cparams(("arbitrary",)),
    )(a1, a2, b)


LN_SUB = 256
LN_TM = 512


def _vec(v):
    arr, layer = v
    return arr, pl.BlockSpec((None, 1, D_MODEL), lambda *_: (layer, 0, 0))


def _mm_ln_fwd(a, b, prev, pg, pb, g, bias, name):
    T, K = a.shape
    tm = LN_TM

    def body(a_ref, b_ref, prev_ref, pg_ref, pb_ref, g_ref, bias_ref, xhat_ref, rstd_ref, y_ref):
        for s in range(tm // LN_SUB):
            rows = slice(s * LN_SUB, (s + 1) * LN_SUB)
            mm = jnp.dot(a_ref[rows, :], b_ref[...], preferred_element_type=F32)
            r = ALPHA * (prev_ref[rows, :] * pg_ref[...] + pb_ref[...]) + mm
            mu = jnp.mean(r, axis=-1, keepdims=True)
            xc = r - mu
            var = jnp.mean(xc * xc, axis=-1, keepdims=True)
            rstd = lax.rsqrt(var + LN_EPS)
            xhat = xc * rstd
            xhat_ref[rows, :] = xhat
            rstd_ref[rows, :] = rstd
            y_ref[rows, :] = (xhat * g_ref[...] + bias_ref[...]).astype(y_ref.dtype)

    row = pl.BlockSpec((tm, D_MODEL), lambda i: (i, 0))
    vecs = [_vec(v) for v in (pg, pb, g, bias)]
    return pl.pallas_call(
        body, name=name, grid=(T // tm,),
        in_specs=[pl.BlockSpec((tm, K), lambda i: (i, 0)),
                  pl.BlockSpec((K, D_MODEL), lambda i: (0, 0), pipeline_mode=pl.Buffered(1)),
                  row] + [s for _, s in vecs],
        out_specs=[row, pl.BlockSpec((tm, 1), lambda i: (i, 0)), row],
        out_shape=[jax.ShapeDtypeStruct((T, D_MODEL), F32), jax.ShapeDtypeStruct((T, 1), F32),
                   jax.ShapeDtypeStruct((T, D_MODEL), BF16)],
        compiler_params=_cparams(("parallel",)),
    )(a, b, prev, *[a_ for a_, _ in vecs])


def _mm_ln_bwd(a_list, b, dres, xhat, rstd, g, name, deps=(), w_back=None):
    T = a_list[0].shape[0]
    tm = LN_TM
    na, nd = len(a_list), len(deps)
    ks = [a.shape[1] for a in a_list]
    last = xhat is None
    nout = 1 if last else (5 if w_back is not None else 4)

    def body(*refs):
        a_refs, b_ref, dres_ref = refs[:na], refs[na], refs[na + 1]
        if not last:
            xhat_ref, rstd_ref, g_ref = refs[na + 2:na + 5]
            dr_ref, drb_ref, dg_ref, db_ref = refs[len(refs) - nout:len(refs) - nout + 4]

            @pl.when(pl.program_id(0) == 0)
            def _():
                dg_ref[...] = jnp.zeros_like(dg_ref)
                db_ref[...] = jnp.zeros_like(db_ref)

        for s in range(tm // LN_SUB):
            rows = slice(s * LN_SUB, (s + 1) * LN_SUB)
            mm, off = None, 0
            for a_ref, k in zip(a_refs, ks):
                part = jnp.dot(a_ref[rows, :], b_ref[off:off + k, :], preferred_element_type=F32)
                mm = part if mm is None else mm + part
                off += k
            dy = ALPHA * dres_ref[rows, :] + mm
            if last:
                refs[-1][rows, :] = dy
                continue
            xhat_v = xhat_ref[rows, :]
            dg_ref[...] += jnp.sum(dy * xhat_v, axis=0, keepdims=True)
            db_ref[...] += jnp.sum(dy, axis=0, keepdims=True)
            dxh = dy * g_ref[...]
            m1 = jnp.mean(dxh, axis=-1, keepdims=True)
            m2 = jnp.mean(dxh * xhat_v, axis=-1, keepdims=True)
            dr = rstd_ref[rows, :] * (dxh - m1 - xhat_v * m2)
            dr_ref[rows, :] = dr
            dr_b = dr.astype(drb_ref.dtype)
            drb_ref[rows, :] = dr_b
            if w_back is not None:
                refs[-1][rows, :] = lax.dot_general(dr_b, refs[na + 5][...], _DN["nt"], preferred_element_type=F32)

    row = pl.BlockSpec((tm, D_MODEL), lambda i: (i, 0))
    vec = pl.BlockSpec((1, D_MODEL), lambda i: (0, 0))
    in_specs = [pl.BlockSpec((tm, k), lambda i: (i, 0)) for k in ks]
    in_specs += [pl.BlockSpec((sum(ks), D_MODEL), lambda i: (0, 0), pipeline_mode=pl.Buffered(1)), row]
    args = list(a_list) + [b, dres]
    if last:
        out_specs, out_shape = row, jax.ShapeDtypeStruct((T, D_MODEL), F32)
    else:
        g_arr, g_spec = _vec(g)
        in_specs += [row, pl.BlockSpec((tm, 1), lambda i: (i, 0)), g_spec]
        args += [xhat, rstd, g_arr]
        out_specs = [row, row, vec, vec]
        out_shape = [jax.ShapeDtypeStruct((T, D_MODEL), F32), jax.ShapeDtypeStruct((T, D_MODEL), BF16),
                     jax.ShapeDtypeStruct((1, D_MODEL), F32), jax.ShapeDtypeStruct((1, D_MODEL), F32)]
        if w_back is not None:
            in_specs.append(pl.BlockSpec(w_back.shape, lambda i: (0, 0), pipeline_mode=pl.Buffered(1)))
            args.append(w_back)
            out_specs.append(row)
            out_shape.append(jax.ShapeDtypeStruct((T, w_back.shape[0]), F32))
    return pl.pallas_call(
        body, name=name, grid=(T // tm,),
        in_specs=in_specs + [pl.BlockSpec(memory_space=pl.ANY)] * nd,
        out_specs=out_specs, out_shape=out_shape,
        compiler_params=_cparams(("parallel",) if last else ("arbitrary",)),
    )(*args, *deps)


DW_TM = 1408
FF_TN = 256
SAVED_GU = BF16


def _mm_swiglu_fwd(h, w_gu, deps=()):
    T = h.shape[0]
    nj = D_FF // FF_TN
    nd = len(deps)

    def body(*refs):
        h_ref, wg_ref, wu_ref = refs[:3]
        g_ref, u_ref, act_ref = refs[3 + nd:]
        hv = h_ref[...]
        gv = lax.dot_general(hv, wg_ref[...], _DN["nt"], preferred_element_type=F32)
        uv = lax.dot_general(hv, wu_ref[...], _DN["nt"], preferred_element_type=F32)
        g_ref[...] = gv.astype(g_ref.dtype)
        u_ref[...] = uv.astype(u_ref.dtype)
        act_ref[...] = (gv * jax.nn.sigmoid(gv) * uv).astype(act_ref.dtype)

    col = pl.BlockSpec((T, FF_TN), lambda j: (0, j))
    return pl.pallas_call(
        body, name="mm_gate_up_swiglu", grid=(nj,),
        in_specs=[pl.BlockSpec((T, D_MODEL), lambda j: (0, 0)),
                  pl.BlockSpec((FF_TN, D_MODEL), lambda j: (j, 0)),
                  pl.BlockSpec((FF_TN, D_MODEL), lambda j: (j + nj, 0))] + [pl.BlockSpec(memory_space=pl.ANY)] * nd,
        out_specs=[col, col, col],
        out_shape=[jax.ShapeDtypeStruct((T, D_FF), SAVED_GU), jax.ShapeDtypeStruct((T, D_FF), SAVED_GU),
                   jax.ShapeDtypeStruct((T, D_FF), BF16)],
        compiler_params=_cparams(("parallel",)),
    )(h, w_gu, w_gu, *deps)


def _mm_swiglu_bwd(dr, w_dn, g, u, deps=()):
    T = dr.shape[0]

    def body(*refs):
        dr_ref, w_ref, g_ref, u_ref = refs[:4]
        dg_ref, du_ref = refs[-2:]
        da = lax.dot_general(dr_ref[...], w_ref[...], _DN["nt"], preferred_element_type=F32)
        gv, uv = g_ref[...].astype(F32), u_ref[...].astype(F32)
        s = jax.nn.sigmoid(gv)
        du_ref[...] = (da * (gv * s)).astype(du_ref.dtype)
        dg_ref[...] = (da * uv * (s * (1.0 + gv * (1.0 - s)))).astype(dg_ref.dtype)

    col = pl.BlockSpec((T, FF_TN), lambda j: (0, j))
    return pl.pallas_call(
        body, name="mm_dact_swiglu", grid=(D_FF // FF_TN,),
        in_specs=[pl.BlockSpec((T, D_MODEL), lambda j: (0, 0)), pl.BlockSpec((FF_TN, D_MODEL), lambda j: (j, 0)),
                  col, col] + [ANY] * len(deps),
        out_specs=[col, col],
        out_shape=[jax.ShapeDtypeStruct((T, D_FF), BF16)] * 2,
        compiler_params=_cparams(("parallel",)),
    )(dr, w_dn, g, u, *deps)


def _gelu(x):
    return 0.5 * x * (1.0 + lax.erf(x * INV_SQRT2))


def _gelu_grad(x):
    return 0.5 * (1.0 + lax.erf(x * INV_SQRT2)) + x * (jnp.exp(-0.5 * x * x) * INV_SQRT_2PI)


def _shift_down(z, k):
    row = lax.broadcasted_iota(jnp.int32, z.shape, 0)
    return jnp.where(row >= k, pltpu.roll(z, k, 0), 0.0)


def _shift_up(z, k):
    n = z.shape[0]
    row = lax.broadcasted_iota(jnp.int32, z.shape, 0)
    return jnp.where(row < n - k, pltpu.roll(z, n - k, 0), 0.0)


def _lo_mask(shape):
    return lax.broadcasted_iota(jnp.int32, shape, len(shape) - 1) < HALF


def _seg_mean(x, lo):
    a = jnp.sum(jnp.where(lo, x, 0.0), axis=-1, keepdims=True)
    b = jnp.sum(jnp.where(lo, 0.0, x), axis=-1, keepdims=True)
    return jnp.where(lo, a, b) * (1.0 / HALF)


def _pool_windows(first):
    lo = _lo_mask((1, LANES))
    return jnp.where(first, jnp.where(lo, 2.0, 4.0), jnp.where(lo, 8.0, 16.0)), lo


def _pool_mean_minus_token(p, first):
    wl, lo = _pool_windows(first)
    s2 = p + _shift_down(p, 1)
    s4 = s2 + _shift_down(s2, 2)
    s8 = s4 + _shift_down(s4, 4)
    s16 = s8 + _shift_down(s8, 8)
    win = jnp.where(first, jnp.where(lo, s2, s4), jnp.where(lo, s8, s16))
    t1 = (lax.broadcasted_iota(jnp.int32, p.shape, 0) + 1).astype(F32)
    count = jnp.minimum(t1, wl)
    return win / count - p, count


SGU_UNROLL = 4


def _tril_keep():
    r = lax.broadcasted_iota(jnp.int32, (2 * CHUNK, CHUNK), 0)
    s = lax.broadcasted_iota(jnp.int32, (2 * CHUNK, CHUNK), 1)
    return s <= (r & (CHUNK - 1))


def _sgu_chunk_fwd(u, v, g, wm, bias, lo):
    ug = _gelu(u)
    vg = _gelu(v)
    mu = _seg_mean(vg, lo)
    xc = vg - mu
    var = _seg_mean(xc * xc, lo)
    rstd = lax.rsqrt(var + LN_EPS)
    vn = xc * rstd
    vh = (vn * g).astype(BF16)
    mm2 = jnp.dot(wm, vh, preferred_element_type=F32)
    mixed = jnp.where(lo, mm2[:CHUNK], mm2[CHUNK:]) + bias
    return ug, vn, rstd, vh, mixed


def _mixer_fwd(proj, wconv, wpool_bd, pscale, lng, wsp, bias, layer):
    T = proj.shape[0]
    nchunk = T // CHUNK

    def body(a_ref, b_ref, c_ref, wc_ref, wp_ref, ps_ref, lng_ref, wsp_ref, bias_ref, o_ref):
        j = pl.program_id(0)

        @pl.when(j < 3)
        def _conv():
            z = c_ref[...] * a_ref[...]
            w = wc_ref[...]
            y = w[0:1] * _shift_down(z, 2) + w[1:2] * _shift_down(z, 1) + w[2:3] * z
            o_ref[...] = (b_ref[...] * y).astype(o_ref.dtype)

        @pl.when((j >= 3) & (j < 5))
        def _pool():
            d, _ = _pool_mean_minus_token(a_ref[...], j == 3)
            y = jnp.dot(d.astype(BF16), wp_ref[...].astype(BF16), preferred_element_type=F32)
            o_ref[...] = (y * ps_ref[...]).astype(o_ref.dtype)

        @pl.when(j >= 5)
        def _sgu():
            lo = _lo_mask((CHUNK, LANES))
            wm = jnp.where(_tril_keep(), wsp_ref[...], 0.0).astype(BF16)
            bias_t = bias_ref[...]
            g = lng_ref[...]

            def chunk(n, carry):
                rows = pl.ds(pl.multiple_of(n * CHUNK, CHUNK), CHUNK)
                ug, _, _, _, mixed = _sgu_chunk_fwd(a_ref[rows, :], b_ref[rows, :], g, wm, bias_t, lo)
                o_ref[rows, :] = (ug * mixed).astype(o_ref.dtype)
                return carry

            lax.fori_loop(0, nchunk, chunk, 0, unroll=SGU_UNROLL)

    def col(f):
        return lambda j: (0, f(j))

    clip = lambda v, lo, hi: jnp.minimum(jnp.maximum(v, lo), hi)
    return pl.pallas_call(
        body,
        name="mixer_fwd",
        grid=(8,),
        in_specs=[
            pl.BlockSpec((T, LANES), col(lambda j: jnp.where(j < 3, j, jnp.where(j < 5, j + 6, j + 6)))),
            pl.BlockSpec((T, LANES), col(lambda j: jnp.where(j < 3, j + 3, jnp.where(j < 5, 5, j + 9)))),
            pl.BlockSpec((T, LANES), col(lambda j: jnp.where(j < 3, j + 6, 8))),
            pl.BlockSpec((None, 3, LANES), lambda j: (layer, 0, clip(j, 0, 2))),
            pl.BlockSpec((None, None, LANES, LANES), lambda j: (layer, clip(j - 3, 0, 1), 0, 0)),
            pl.BlockSpec((None, 1, LANES), lambda j: (layer, 0, clip(j - 3, 0, 1))),
            pl.BlockSpec((None, 1, LANES), lambda j: (layer, 0, clip(j - 5, 0, 2))),
            pl.BlockSpec((None, None, 2 * CHUNK, CHUNK), lambda j: (layer, clip(j - 5, 0, 2), 0, 0)),
            pl.BlockSpec((None, None, CHUNK, LANES), lambda j: (layer, clip(j - 5, 0, 2), 0, 0)),
        ],
        out_specs=pl.BlockSpec((T, LANES), lambda j: (0, j)),
        out_shape=jax.ShapeDtypeStruct((T, D_MODEL), BF16),
        compiler_params=_cparams(("arbitrary",)),
    )(proj, proj, proj, wconv, wpool_bd, pscale, lng, wsp, bias)


def _mixer_bwd(proj, dmix, wconv, wpool_bd, pscale, lng, wsp, bias, layer, deps=()):
    T = proj.shape[0]
    nchunk = T // CHUNK

    def body(*refs):
        a_ref, b_ref, c_ref, dm_ref, wc_ref, wp_ref, ps_ref, lng_ref, wsp_ref, bias_ref = refs[:10]
        o_ref, dwc_ref, dwp_ref, dps_ref, dlng_ref, dwsp_ref, dbias_ref, keep1, keep2 = refs[10 + len(deps):]
        k = pl.program_id(0)

        @pl.when(k < 3)
        def _conv():
            xa, gb, gc, dya = a_ref[...], b_ref[...], c_ref[...], dm_ref[...]
            w = wc_ref[...]
            z = gc * xa
            z1 = _shift_down(z, 1)
            z2 = _shift_down(z, 2)
            y = w[0:1] * z2 + w[1:2] * z1 + w[2:3] * z
            dyv = dya * gb
            dz = w[2:3] * dyv + w[1:2] * _shift_up(dyv, 1) + w[0:1] * _shift_up(dyv, 2)
            dwc_ref[0:1, :] = jnp.sum(dyv * z2, axis=0, keepdims=True)
            dwc_ref[1:2, :] = jnp.sum(dyv * z1, axis=0, keepdims=True)
            dwc_ref[2:3, :] = jnp.sum(dyv * z, axis=0, keepdims=True)
            o_ref[...] = (dz * gc).astype(o_ref.dtype)
            keep1[k] = (dya * y).astype(keep1.dtype)
            keep1[k + 3] = (dz * xa).astype(keep1.dtype)

        @pl.when((k >= 3) & (k < 9))
        def _emit_gb_gc():
            o_ref[...] = keep1[k - 3]

        @pl.when((k >= 9) & (k < 11))
        def _pool():
            first = k == 9
            p, dyb = a_ref[...], dm_ref[...]
            d, count = _pool_mean_minus_token(p, first)
            w2 = wp_ref[...].astype(BF16)
            db = d.astype(BF16)
            y = jnp.dot(db, w2, preferred_element_type=F32)
            dps_ref[...] = jnp.sum(dyb * y, axis=0, keepdims=True)
            dyv = (dyb * ps_ref[...]).astype(BF16)
            dd = lax.dot_general(dyv, w2, _DN["nt"], preferred_element_type=F32)
            dwp_ref[...] = lax.dot_general(db, dyv, _DN["tn"], preferred_element_type=F32)
            dwin = dd / count
            a2 = dwin + _shift_up(dwin, 1)
            a4 = a2 + _shift_up(a2, 2)
            a8 = a4 + _shift_up(a4, 4)
            a16 = a8 + _shift_up(a8, 8)
            _, lo = _pool_windows(first)
            back = jnp.where(first, jnp.where(lo, a2, a4), jnp.where(lo, a8, a16))
            o_ref[...] = (back - dd).astype(o_ref.dtype)

        @pl.when((k >= 11) & (k < 14))
        def _sgu():
            lo = _lo_mask((CHUNK, LANES))
            keep = _tril_keep()
            wm = jnp.where(keep, wsp_ref[...], 0.0).astype(BF16)
            bias_t = bias_ref[...]
            g = lng_ref[...]
            dwsp_ref[...] = jnp.zeros_like(dwsp_ref)
            dbias_ref[...] = jnp.zeros_like(dbias_ref)
            dlng_ref[...] = jnp.zeros_like(dlng_ref)

            def chunk(n, carry):
                rows = pl.ds(pl.multiple_of(n * CHUNK, CHUNK), CHUNK)
                u, v, dyc = a_ref[rows, :], b_ref[rows, :], dm_ref[rows, :]
                ug, vn, rstd, vh, mixed = _sgu_chunk_fwd(u, v, g, wm, bias_t, lo)
                dmx = dyc * ug
                o_ref[rows, :] = (dyc * mixed * _gelu_grad(u)).astype(o_ref.dtype)
                dbias_ref[...] += dmx
                dst = jnp.concatenate([jnp.where(lo, dmx, 0.0), jnp.where(lo, 0.0, dmx)], axis=0).astype(BF16)
                dwsp_ref[...] += lax.dot_general(dst, vh, _DN["nt"], preferred_element_type=F32)
                dvh = lax.dot_general(wm, dst, _DN["tn"], preferred_element_type=F32)
                dlng_ref[...] += jnp.sum(dvh * vn, axis=0, keepdims=True)
                dvn = dvh * g
                m1 = _seg_mean(dvn, lo)
                m2 = _seg_mean(dvn * vn, lo)
                dvg = rstd * (dvn - m1 - vn * m2)
                keep2[k - 11, rows, :] = (dvg * _gelu_grad(v)).astype(keep2.dtype)
                return carry

            lax.fori_loop(0, nchunk, chunk, 0, unroll=SGU_UNROLL)
            dwsp_ref[...] = jnp.where(keep, dwsp_ref[...], 0.0)
            dbt = dbias_ref[...]
            lane = lax.broadcasted_iota(jnp.int32, (CHUNK, LANES), 1)
            sa = jnp.sum(jnp.where(lo, dbt, 0.0), axis=-1, keepdims=True)
            sb = jnp.sum(jnp.where(lo, 0.0, dbt), axis=-1, keepdims=True)
            dbias_ref[...] = jnp.where(lane == 0, sa, jnp.where(lane == 1, sb, 0.0))

        @pl.when(k >= 14)
        def _emit_v():
            o_ref[...] = keep2[k - 14]

    def col(f):
        return lambda k: (0, f(k))

    clip = lambda v, lo, hi: jnp.minimum(jnp.maximum(v, lo), hi)
    view_a = lambda k: jnp.where(k < 3, k, jnp.where(k < 9, 2, jnp.where(k < 14, k, 13)))
    view_b = lambda k: jnp.where(k < 3, k + 3, jnp.where(k < 11, 5, jnp.where(k < 14, k + 3, 16)))
    view_c = lambda k: jnp.where(k < 3, k + 6, 8)
    view_dm = lambda k: jnp.where(k < 3, k, jnp.where(k < 9, 2, jnp.where(k < 14, k - 6, 7)))
    return pl.pallas_call(
        body,
        name="mixer_bwd",
        grid=(17,),
        in_specs=[
            pl.BlockSpec((T, LANES), col(view_a)),
            pl.BlockSpec((T, LANES), col(view_b)),
            pl.BlockSpec((T, LANES), col(view_c)),
            pl.BlockSpec((T, LANES), col(view_dm)),
            pl.BlockSpec((None, 3, LANES), lambda k: (layer, 0, clip(k, 0, 2))),
            pl.BlockSpec((None, None, LANES, LANES), lambda k: (layer, clip(k - 9, 0, 1), 0, 0)),
            pl.BlockSpec((None, 1, LANES), lambda k: (layer, 0, clip(k - 9, 0, 1))),
            pl.BlockSpec((None, 1, LANES), lambda k: (layer, 0, clip(k - 11, 0, 2))),
            pl.BlockSpec((None, None, 2 * CHUNK, CHUNK), lambda k: (layer, clip(k - 11, 0, 2), 0, 0)),
            pl.BlockSpec((None, None, CHUNK, LANES), lambda k: (layer, clip(k - 11, 0, 2), 0, 0)),
        ] + [pl.BlockSpec(memory_space=pl.ANY)] * len(deps),
        out_specs=[
            pl.BlockSpec((T, LANES), lambda k: (0, k)),
            pl.BlockSpec((3, LANES), col(lambda k: clip(k, 0, 2))),
            pl.BlockSpec((None, LANES, LANES), lambda k: (clip(k - 9, 0, 1), 0, 0)),
            pl.BlockSpec((1, LANES), col(lambda k: clip(k - 9, 0, 1))),
            pl.BlockSpec((1, LANES), col(lambda k: clip(k - 11, 0, 2))),
            pl.BlockSpec((None, 2 * CHUNK, CHUNK), lambda k: (clip(k - 11, 0, 2), 0, 0)),
            pl.BlockSpec((None, CHUNK, LANES), lambda k: (clip(k - 11, 0, 2), 0, 0)),
        ],
        out_shape=[
            jax.ShapeDtypeStruct((T, IN_W), BF16),
            jax.ShapeDtypeStruct((3, CONV_W), F32),
            jax.ShapeDtypeStruct((2, LANES, LANES), F32),
            jax.ShapeDtypeStruct((1, POOL_W), F32),
            jax.ShapeDtypeStruct((1, SGU_W), F32),
            jax.ShapeDtypeStruct((3, 2 * CHUNK, CHUNK), F32),
            jax.ShapeDtypeStruct((3, CHUNK, LANES), F32),
        ],
        scratch_shapes=[pltpu.VMEM((6, T, LANES), BF16), pltpu.VMEM((3, T, LANES), BF16)],
        compiler_params=_cparams(("arbitrary",)),
    )(proj, proj, proj, dmix, wconv, wpool_bd, pscale, lng, wsp, bias, *deps)


def _ln_bwd(dres, dmm, xhat, rstd, g, tm=256, deps=()):
    T = xhat.shape[0]
    has_res = dres is not None
    nd = len(deps)

    def body(*refs):
        refs = refs[:len(refs) - 4 - nd] + refs[len(refs) - 4:]
        if has_res:
            dres_ref, dmm_ref, xhat_ref, rstd_ref, g_ref, dr_ref, drb_ref, dg_ref, db_ref = refs
            dy = ALPHA * dres_ref[...] + dmm_ref[...]
        else:
            dmm_ref, xhat_ref, rstd_ref, g_ref, dr_ref, drb_ref, dg_ref, db_ref = refs
            dy = dmm_ref[...]
        xhat_v = xhat_ref[...]

        @pl.when(pl.program_id(0) == 0)
        def _():
            dg_ref[...] = jnp.zeros_like(dg_ref)
            db_ref[...] = jnp.zeros_like(db_ref)

        dg_ref[...] += jnp.sum(dy * xhat_v, axis=0, keepdims=True)
        db_ref[...] += jnp.sum(dy, axis=0, keepdims=True)
        dxh = dy * g_ref[...]
        m1 = jnp.mean(dxh, axis=-1, keepdims=True)
        m2 = jnp.mean(dxh * xhat_v, axis=-1, keepdims=True)
        dr = rstd_ref[...] * (dxh - m1 - xhat_v * m2)
        dr_ref[...] = dr
        drb_ref[...] = dr.astype(drb_ref.dtype)

    row = pl.BlockSpec((tm, D_MODEL), lambda i: (i, 0))
    vec = pl.BlockSpec((1, D_MODEL), lambda i: (0, 0))
    g_arr, g_spec = _vec(g)
    in_specs = ([row] if has_res else []) + [row, row, pl.BlockSpec((tm, 1), lambda i: (i, 0)), g_spec]
    in_specs += [pl.BlockSpec(memory_space=pl.ANY)] * nd
    args = ([dres] if has_res else []) + [dmm, xhat, rstd, g_arr] + list(deps)
    return pl.pallas_call(
        body,
        name="ln_bwd_res" if has_res else "ln_bwd",
        grid=(T // tm,),
        in_specs=in_specs,
        out_specs=[row, row, vec, vec],
        out_shape=[jax.ShapeDtypeStruct((T, D_MODEL), F32), jax.ShapeDtypeStruct((T, D_MODEL), BF16),
                   jax.ShapeDtypeStruct((1, D_MODEL), F32), jax.ShapeDtypeStruct((1, D_MODEL), F32)],
        compiler_params=_cparams(("arbitrary",)),
    )(*args)


def _loss_head(xhat, g, b, target, tm=256):
    T = xhat.shape[0]

    def body(xhat_ref, g_ref, b_ref, t_ref, loss_ref, dy_ref):
        err = xhat_ref[...] * g_ref[...] + b_ref[...] - t_ref[...]

        @pl.when(pl.program_id(0) == 0)
        def _():
            loss_ref[...] = jnp.zeros_like(loss_ref)

        part = jnp.sum(jnp.sum(err * err, axis=-1, keepdims=True), axis=0, keepdims=True)
        loss_ref[...] += jnp.broadcast_to(part * (0.5 / D_MODEL), loss_ref.shape)
        dy_ref[...] = err * (1.0 / D_MODEL)

    row = pl.BlockSpec((tm, D_MODEL), lambda i: (i, 0))
    (g_arr, g_spec), (b_arr, b_spec) = _vec(g), _vec(b)
    return pl.pallas_call(
        body,
        name="loss_head",
        grid=(T // tm,),
        in_specs=[row, g_spec, b_spec, row],
        out_specs=[pl.BlockSpec((8, LANES), lambda i: (0, 0)), row],
        out_shape=[jax.ShapeDtypeStruct((8, LANES), F32), jax.ShapeDtypeStruct((T, D_MODEL), F32)],
        compiler_params=_cparams(("arbitrary",)),
    )(xhat, g_arr, b_arr, target)


def _adamw(w, g, m, v, tr):
    R, C = w.shape[-2:]
    assert R % tr == 0
    c1 = 1.0 - ADAM_B1 ** ADAM_STEP
    c2 = 1.0 - ADAM_B2 ** ADAM_STEP

    def body(w_ref, g_ref, m_ref, v_ref, d_ref, mo_ref, vo_ref):
        gv = g_ref[...]
        mn = ADAM_B1 * m_ref[...] + (1.0 - ADAM_B1) * gv
        vn = ADAM_B2 * v_ref[...] + (1.0 - ADAM_B2) * (gv * gv)
        d_ref[...] = -ADAM_LR * ((mn / c1) / (jnp.sqrt(vn / c2) + ADAM_EPS) + ADAM_WD * w_ref[...])
        mo_ref[...] = mn
        vo_ref[...] = vn

    if w.ndim == 2:
        grid, blk = (R // tr,), pl.BlockSpec((tr, C), lambda i: (i, 0))
    else:
        grid, blk = (w.shape[0], R // tr), pl.BlockSpec((None, tr, C), lambda l, i: (l, i, 0))
    return pl.pallas_call(
        body, name="adamw", grid=grid, in_specs=[blk] * 4, out_specs=[blk] * 3,
        out_shape=[jax.ShapeDtypeStruct(w.shape, F32)] * 3, compiler_params=_cparams(("parallel",) * len(grid)),
    )(w, g, m, v)


def _my_place():
    return lax.axis_index("x"), lax.axis_index("y"), lax.axis_index("c")


ANY = pl.BlockSpec(memory_space=pl.ANY)
HBM = pl.BlockSpec(memory_space=pltpu.HBM)
SEM = pl.BlockSpec(memory_space=pltpu.SEMAPHORE)
EFFECT = pltpu.SideEffectType.DATAFLOW_SIDE_EFFECTING


def _in_hbm(a):
    return pltpu.with_memory_space_constraint(a, pltpu.HBM)


def _block_rows(ref, dev):
    r = ref.shape[0] // N_DEV
    start = pl.multiple_of((4 * dev[0] + 2 * dev[1] + dev[2]) * r, 16)
    return ref.at[pl.ds(start, r), :]


def _ag_first_copies(s_refs, land_refs, send_sems, recv_sems, receiving):
    x, y, c = _my_place()
    peers = [(x, y, 1 - c)] + [(*chip, c) for chip in _other_chips(x, y)]
    copies = []
    for k, peer in enumerate(peers):
        block = peer if receiving else (x, y, c)
        copies += [pltpu.make_async_remote_copy(
            src_ref=s_refs[w], dst_ref=_block_rows(land_refs[w], block),
            send_sem=send_sems.at[k * len(s_refs) + w], recv_sem=recv_sems.at[k * len(s_refs) + w],
            device_id=peer, device_id_type=MESH)
            for w in range(len(s_refs))]
    return copies


def _ag_start(shards, layer, after=()):
    nw = len(shards)

    def body(*refs):
        s_refs, land_refs = refs[:nw], refs[nw:2 * nw]
        token = refs[-1]
        sems = 2 * nw + len(after)
        for cp in _ag_first_copies(s_refs, land_refs, refs[sems], refs[sems + 1], False):
            cp.start()
        token[...] = jnp.zeros_like(token)

    lands = [lax.empty((N_DEV * s.shape[0], D_MODEL), BF16) for s in shards]
    out = pl.pallas_call(
        body, name="ag_start_%s" % layer,
        in_specs=[HBM] * (2 * nw) + [ANY] * len(after),
        out_specs=(SEM, SEM, *[HBM] * (2 * nw), pl.BlockSpec(memory_space=pltpu.VMEM)),
        out_shape=(pltpu.SemaphoreType.DMA((4 * nw,)), pltpu.SemaphoreType.DMA((4 * nw,)),
                   *[pltpu.HBM(a.shape, a.dtype) for a in list(shards) + lands],
                   jax.ShapeDtypeStruct((8, LANES), F32)),
        input_output_aliases={i: 2 + i for i in range(2 * nw)},
        compiler_params=pltpu.CompilerParams(has_side_effects=EFFECT),
    )(*[_in_hbm(a) for a in list(shards) + lands], *after)
    return out[0], out[1], out[2:2 + nw], out[2 + nw:2 + 2 * nw], out[-1]


def _ag_wait(send_sems, recv_sems, shards, lands, after, layer):
    nw = len(shards)

    def body(*refs):
        s_refs, land_refs = refs[:nw], refs[nw:2 * nw]
        for cp in _ag_first_copies(s_refs, land_refs, refs[2 * nw], refs[2 * nw + 1], True):
            cp.wait_send()
            cp.wait_recv()

    out = pl.pallas_call(
        body, name="ag_wait_%s" % layer,
        in_specs=[HBM] * (2 * nw) + [SEM, SEM] + [ANY] * len(after),
        out_specs=[HBM] * (2 * nw),
        out_shape=[pltpu.HBM(a.shape, a.dtype) for a in list(shards) + list(lands)],
        input_output_aliases={i: i for i in range(2 * nw)},
        compiler_params=pltpu.CompilerParams(has_side_effects=EFFECT),
    )(*shards, *lands, send_sems, recv_sems, *after)
    return out[:nw], out[nw:]


def _ag_pass_on(shards, lands):
    nw = len(shards)

    def body(*refs):
        s_refs, g_refs = refs[:nw], refs[2 * nw:3 * nw]
        send_sems, recv_sems, local_sems = refs[3 * nw:3 * nw + 3]
        stage = refs[3 * nw + 3:]
        x, y, c = _my_place()
        load = [pltpu.make_async_copy(s_refs[w], stage[w], local_sems.at[w]) for w in range(nw)]
        mine = [pltpu.make_async_copy(stage[w], _block_rows(g_refs[w], (x, y, c)), local_sems.at[w])
                for w in range(nw)]
        for cp in load:
            cp.start()
        sends, arrivals = [], []
        for j, chip in enumerate(_other_chips(x, y)):
            for w in range(nw):
                rows_out = _block_rows(g_refs[w], (*chip, c))
                rows_in = _block_rows(g_refs[w], (*chip, 1 - c))
                sends.append(pltpu.make_async_remote_copy(
                    src_ref=rows_out, dst_ref=rows_out, send_sem=send_sems.at[j, w], recv_sem=recv_sems.at[j, w],
                    device_id=(x, y, 1 - c), device_id_type=MESH))
                arrivals.append(pltpu.make_async_remote_copy(
                    src_ref=rows_in, dst_ref=rows_in, send_sem=send_sems.at[j, w], recv_sem=recv_sems.at[j, w],
                    device_id=(x, y, 1 - c), device_id_type=MESH))
        for cp in sends:
            cp.start()
        for w in range(nw):
            load[w].wait()
            mine[w].start()
        for cp in arrivals:
            cp.wait_recv()
        for cp in sends:
            cp.wait_send()
        for cp in mine:
            cp.wait()

    return pl.pallas_call(
        body, name="ag_pass_on",
        in_specs=[ANY] * (2 * nw), out_specs=[ANY] * nw,
        out_shape=[jax.ShapeDtypeStruct(a.shape, a.dtype) for a in lands],
        input_output_aliases={nw + i: i for i in range(nw)},
        scratch_shapes=[pltpu.SemaphoreType.DMA((3, nw)), pltpu.SemaphoreType.DMA((3, nw)),
                        pltpu.SemaphoreType.DMA((nw,))] + [pltpu.VMEM(s.shape, s.dtype) for s in shards],
        compiler_params=_cparams(),
    )(*shards, *lands)


def _rs_sibling_copies(p_refs, land_refs, send_sems, recv_sems):
    x, y, c = _my_place()
    return [pltpu.make_async_remote_copy(
        src_ref=p_refs[w].at[:, 1 - c], dst_ref=land_refs[w],
        send_sem=send_sems.at[w], recv_sem=recv_sems.at[w], device_id=(x, y, 1 - c), device_id_type=MESH)
        for w in range(len(p_refs))]


def _rs_sibling_start(parts, tag, after=()):
    nw = len(parts)
    sems = 2 * nw + len(after)

    def body(*refs):
        for cp in _rs_sibling_copies(refs[:nw], refs[nw:2 * nw], refs[sems], refs[sems + 1]):
            cp.start()
        refs[-1][...] = jnp.zeros_like(refs[-1])

    lands = [lax.empty(p.shape[:1] + p.shape[2:], BF16) for p in parts]
    out = pl.pallas_call(
        body, name="rs_sibling_start_%s" % tag,
        in_specs=[HBM] * (2 * nw) + [ANY] * len(after),
        out_specs=(SEM, SEM, *[HBM] * (2 * nw), pl.BlockSpec(memory_space=pltpu.VMEM)),
        out_shape=(pltpu.SemaphoreType.DMA((nw,)), pltpu.SemaphoreType.DMA((nw,)),
                   *[pltpu.HBM(a.shape, a.dtype) for a in list(parts) + lands],
                   jax.ShapeDtypeStruct((8, LANES), F32)),
        input_output_aliases={i: 2 + i for i in range(2 * nw)},
        compiler_params=pltpu.CompilerParams(has_side_effects=EFFECT),
    )(*[_in_hbm(a) for a in list(parts) + lands], *after)
    return out[0], out[1], out[2:2 + nw], out[2 + nw:2 + 2 * nw], out[-1]


def _rs_sibling_wait(send_sems, recv_sems, parts, lands, after, tag):
    nw = len(parts)

    def body(*refs):
        for cp in _rs_sibling_copies(refs[:nw], refs[nw:2 * nw], refs[2 * nw], refs[2 * nw + 1]):
            cp.wait_send()
            cp.wait_recv()

    out = pl.pallas_call(
        body, name="rs_sibling_wait_%s" % tag,
        in_specs=[HBM] * (2 * nw) + [SEM, SEM] + [ANY] * len(after),
        out_specs=[HBM] * (2 * nw),
        out_shape=[pltpu.HBM(a.shape, a.dtype) for a in list(parts) + list(lands)],
        input_output_aliases={i: i for i in range(2 * nw)},
        compiler_params=pltpu.CompilerParams(has_side_effects=EFFECT),
    )(*parts, *lands, send_sems, recv_sems, *after)
    return out[:nw], out[nw:]


def _rs_chip_sum(parts, gots, c):
    n = len(parts)

    def body(c_ref, *refs):
        for p_ref, g_ref, o_ref in zip(refs[:n], refs[n:2 * n], refs[2 * n:]):
            o_ref[...] = (p_ref[...].astype(F32) + g_ref[...].astype(F32)).astype(o_ref.dtype)

    mine = [pl.BlockSpec((None, None, p.shape[2], D_MODEL), lambda q, c_ref: (q, c_ref[0], 0, 0)) for p in parts]
    theirs = [pl.BlockSpec((None, g.shape[1], D_MODEL), lambda q, c_ref: (q, 0, 0)) for g in gots]
    return pl.pallas_call(
        body, name="rs_chip_sum",
        grid_spec=pltpu.PrefetchScalarGridSpec(
            num_scalar_prefetch=1, grid=(4,), in_specs=mine + theirs, out_specs=theirs),
        out_shape=[jax.ShapeDtypeStruct(g.shape, BF16) for g in gots],
        compiler_params=_cparams(("parallel",)),
    )(c, *parts, *gots)


def _other_chips(x, y):
    return [(1 - x, y), (x, 1 - y), (1 - x, 1 - y)]


def _rs_chip_copies(s_refs, land_refs, send_sems, recv_sems):
    x, y, c = _my_place()
    copies = []
    for k, chip in enumerate(_other_chips(x, y)):
        q = 2 * chip[0] + chip[1]
        copies += [pltpu.make_async_remote_copy(
            src_ref=s_refs[w].at[q], dst_ref=land_refs[w].at[k],
            send_sem=send_sems.at[k * len(s_refs) + w], recv_sem=recv_sems.at[k * len(s_refs) + w],
            device_id=(*chip, c), device_id_type=MESH)
            for w in range(len(s_refs))]
    return copies


def _rs_chip_start(sums, layer):
    nw = len(sums)

    def body(*refs):
        s_refs, land_refs = refs[:nw], refs[nw:2 * nw]
        send_sems, recv_sems = refs[2 * nw], refs[2 * nw + 1]
        token = refs[-1]
        for cp in _rs_chip_copies(s_refs, land_refs, send_sems, recv_sems):
            cp.start()
        token[...] = jnp.zeros_like(token)

    lands = [lax.empty((3,) + s.shape[1:], BF16) for s in sums]
    out = pl.pallas_call(
        body, name="rs_chip_start_%s" % layer,
        in_specs=[HBM] * (2 * nw),
        out_specs=(SEM, SEM, *[HBM] * (2 * nw), pl.BlockSpec(memory_space=pltpu.VMEM)),
        out_shape=(pltpu.SemaphoreType.DMA((3 * nw,)), pltpu.SemaphoreType.DMA((3 * nw,)),
                   *[pltpu.HBM(a.shape, a.dtype) for a in list(sums) + lands],
                   jax.ShapeDtypeStruct((8, LANES), F32)),
        input_output_aliases={i: 2 + i for i in range(2 * nw)},
        compiler_params=pltpu.CompilerParams(has_side_effects=EFFECT),
    )(*[_in_hbm(a) for a in list(sums) + lands])
    return out[0], out[1], out[2:2 + nw], out[2 + nw:2 + 2 * nw], out[-1]


def _rs_chip_wait(send_sems, recv_sems, sums, lands, after, layer):
    nw = len(sums)

    def body(*refs):
        s_refs, land_refs = refs[:nw], refs[nw:2 * nw]
        for cp in _rs_chip_copies(s_refs, land_refs, refs[2 * nw], refs[2 * nw + 1]):
            cp.wait_send()
            cp.wait_recv()

    out = pl.pallas_call(
        body, name="rs_chip_wait_%s" % layer,
        in_specs=[HBM] * (2 * nw) + [SEM, SEM] + [ANY] * len(after),
        out_specs=[HBM] * (2 * nw),
        out_shape=[pltpu.HBM(a.shape, a.dtype) for a in list(sums) + list(lands)],
        input_output_aliases={i: i for i in range(2 * nw)},
        compiler_params=pltpu.CompilerParams(has_side_effects=EFFECT),
    )(*sums, *lands, send_sems, recv_sems, *after)
    return out[:nw], out[nw:]


def _rs_finish(sums, gots, q, layer, into):
    n = len(sums)

    def body(q_ref, *refs):
        for s_ref, g_ref, o_ref in zip(refs[:n], refs[n:2 * n], refs[len(refs) - n:]):
            o_ref[...] = ((s_ref[...].astype(F32) + g_ref[0].astype(F32)) + g_ref[1].astype(F32)) + g_ref[2].astype(F32)

    rows = [s.shape[1] for s in sums]
    in_specs = [pl.BlockSpec((None, r, D_MODEL), lambda i, q_ref: (q_ref[0], 0, 0)) for r in rows]
    in_specs += [pl.BlockSpec((3, r, D_MODEL), lambda i, q_ref: (0, 0, 0)) for r in rows]
    args = [q, *sums, *gots]
    aliases = {}
    if into is not None:
        in_specs += [ANY] * n
        aliases = {len(args) + i: i for i in range(n)}
        args += list(into)
    return pl.pallas_call(
        body, name="rs_finish",
        grid_spec=pltpu.PrefetchScalarGridSpec(
            num_scalar_prefetch=1, grid=(1,), in_specs=in_specs,
            out_specs=[pl.BlockSpec((None, r, D_MODEL), lambda i, q_ref: (layer, 0, 0)) for r in rows]),
        out_shape=[jax.ShapeDtypeStruct((DEPTH, r, D_MODEL), F32) for r in rows],
        input_output_aliases=aliases,
        compiler_params=_cparams(("arbitrary",)),
    )(*args)


def _allreduce_small(vec, deps=()):
    R = vec.shape[0]
    assert R % (8 * N_DEV) == 0
    P = R // N_DEV
    nd = len(deps)

    def body(*refs):
        v_ref = refs[0]
        o_ref, buf, send1, recv1, send2, recv2 = refs[1 + nd:]
        x, y, c = _my_place()
        me = 4 * x + 2 * y + c

        def piece(ref, d):
            return ref.at[pl.ds(pl.multiple_of(d * P, 8), P), :]

        def peer(k):
            p = me ^ k
            return p, (p >> 2, (p >> 1) & 1, p & 1)

        scatter = []
        for k in range(1, N_DEV):
            p, where = peer(k)
            scatter.append(pltpu.make_async_remote_copy(
                src_ref=piece(v_ref, p), dst_ref=buf.at[k], send_sem=send1.at[k - 1], recv_sem=recv1.at[k - 1],
                device_id=where, device_id_type=MESH))
        for cp in scatter:
            cp.start()
        buf[0] = piece(v_ref, me)[...]
        for cp in scatter:
            cp.wait()
        acc = buf[me]
        for d in range(1, N_DEV):
            acc = acc + buf[me ^ d]
        piece(o_ref, me)[...] = acc
        spread, arrivals = [], []
        for k in range(1, N_DEV):
            p, where = peer(k)
            spread.append(pltpu.make_async_remote_copy(
                src_ref=piece(o_ref, me), dst_ref=piece(o_ref, me), send_sem=send2.at[k - 1], recv_sem=recv2.at[k - 1],
                device_id=where, device_id_type=MESH))
            arrivals.append(pltpu.make_async_remote_copy(
                src_ref=piece(o_ref, p), dst_ref=piece(o_ref, p), send_sem=send2.at[k - 1], recv_sem=recv2.at[k - 1],
                device_id=where, device_id_type=MESH))
        for cp in spread:
            cp.start()
        for cp in arrivals:
            cp.wait_recv()
        for cp in spread:
            cp.wait_send()

    sems = pltpu.SemaphoreType.DMA((N_DEV - 1,))
    return pl.pallas_call(
        body, name="allreduce_small",
        in_specs=[pl.BlockSpec(memory_space=pltpu.VMEM)] + [ANY] * nd, out_specs=pl.BlockSpec(memory_space=pltpu.VMEM),
        out_shape=jax.ShapeDtypeStruct((R, LANES), F32),
        scratch_shapes=[pltpu.VMEM((N_DEV, P, LANES), F32), sems, sems, sems, sems],
        compiler_params=_cparams(),
    )(vec, *deps)


def _pack(arrs):
    flat = jnp.concatenate([a.reshape(-1) for a in arrs])
    pad = (-flat.shape[0]) % (8 * N_DEV * LANES)
    return jnp.pad(flat, (0, pad)).reshape(-1, LANES)


def _unpack(packed, shapes):
    flat = packed.reshape(-1)
    out, off = [], 0
    for s in shapes:
        n = math.prod(s)
        out.append(flat[off:off + n].reshape(s))
        off += n
    return out


def kernel(x, w_in, w_conv, w_pool, pool_scale, sgu_ln_g, w_spatial, b_spatial, w_o, ln1_g, ln1_b, w_gate_up, w_down, ln2_g, ln2_b, loss_target, m_w_in, m_w_conv, m_w_pool, m_pool_scale, m_sgu_ln_g, m_w_spatial, m_b_spatial, m_w_o, m_ln1_g, m_ln1_b, m_w_gate_up, m_w_down, m_ln2_g, m_ln2_b, v_w_in, v_w_conv, v_w_pool, v_pool_scale, v_sgu_ln_g, v_w_spatial, v_b_spatial, v_w_o, v_ln1_g, v_ln1_b, v_w_gate_up, v_w_down, v_ln2_g, v_ln2_b):
    L = DEPTH
    T = x.shape[1]
    mx, my, mc = _my_place()
    dev = 4 * mx + 2 * my + mc
    xs = x[0]
    target = loss_target[0]

    conv_cols = w_conv.shape[2]
    w_conv_z = lax.dynamic_update_slice(jnp.zeros((L, 3, CONV_W), F32), w_conv, (0, 0, dev * conv_cols))
    w_conv_packed = _allreduce_small(_pack([w_conv_z]))
    w_conv_full = _unpack(w_conv_packed, [(L, 3, CONV_W)])[0]

    shards = (jnp.swapaxes(w_in, 1, 2).astype(BF16), jnp.swapaxes(w_gate_up, 1, 2).astype(BF16),
              w_o.astype(BF16), w_down.astype(BF16))
    first_gather = _ag_start_layer(shards, 0, [w_conv_packed])

    loss_tile, grad_x2, big_grads, small_grads = _local_step(
        xs, target, shards, first_gather, w_conv_full, w_pool, pool_scale, sgu_ln_g, w_spatial, b_spatial,
        ln1_g, ln1_b, ln2_g, ln2_b)
    loss = lax.psum(loss_tile[0, 0], ("x", "y", "c"))
    grad_x = grad_x2[None]
    big_w = (w_in, w_gate_up, w_o, w_down)
    big_m = (m_w_in, m_w_gate_up, m_w_o, m_w_down)
    big_v = (v_w_in, v_w_gate_up, v_w_o, v_w_down)
    small_w = [w_conv_full, w_pool, pool_scale, sgu_ln_g, w_spatial, b_spatial, ln1_g, ln1_b, ln2_g, ln2_b]
    small_m = [m_w_conv, m_w_pool, m_pool_scale, m_sgu_ln_g, m_w_spatial, m_b_spatial, m_ln1_g, m_ln1_b, m_ln2_g, m_ln2_b]
    small_v = [v_w_conv, v_w_pool, v_pool_scale, v_sgu_ln_g, v_w_spatial, v_b_spatial, v_ln1_g, v_ln1_b, v_ln2_g, v_ln2_b]
    grads, deltas, new_m, new_v = _reduce_and_update(
        big_grads, small_grads, big_w, big_m, big_v, small_w, small_m, small_v)
    return (loss, grad_x, *grads, *deltas, *new_m, *new_v)


def _ag_start_layer(shards, l, after):
    s_in, s_gu, s_o, s_dn = [s[l] for s in shards]
    first = _ag_start([s_in, s_o], "%da" % l, after=after)
    return first, _ag_start([s_gu, s_dn], "%db" % l, after=[first[4]])


def _ag_finish(gather, after, tag):
    send_sems, recv_sems, shards, lands, _ = gather
    shards, lands = _ag_wait(send_sems, recv_sems, shards, lands, after, tag)
    return _ag_pass_on(shards, lands)


def _rs_begin(parts, tag, after=()):
    return _rs_sibling_start([p.reshape(4, 2, p.shape[0] // N_DEV, D_MODEL) for p in parts], tag, after)


def _rs_continue(sibling_flight, after, c_arr, tag):
    send_sems, recv_sems, parts, lands, _ = sibling_flight
    parts, got = _rs_sibling_wait(send_sems, recv_sems, parts, lands, after, tag)
    return _rs_chip_start(_rs_chip_sum(parts, got, c_arr), tag)


def _local_step(xs, target, shards, gather, w_conv_full, w_pool, pool_scale, sgu_ln_g, w_spatial, b_spatial,
                ln1_g, ln1_b, ln2_g, ln2_b):
    L = DEPTH
    T = xs.shape[0]
    mx, my, mc = _my_place()
    c_arr = jnp.reshape(mc, (1,)).astype(jnp.int32)
    q_arr = jnp.reshape(2 * mx + my, (1,)).astype(jnp.int32)
    eye2 = jnp.eye(2, dtype=F32)
    wp = w_pool.reshape(L, 2, 2, HALF, HALF)
    wpool_bd = jnp.einsum("ltgcd,gh->ltgchd", wp, eye2).reshape(L, 2, LANES, LANES)
    wsp_t = w_spatial.reshape(L, 3, 2 * CHUNK, CHUNK)
    bias_t = jnp.repeat(jnp.swapaxes(b_spatial.reshape(L, 3, 2, CHUNK), 2, 3), HALF, axis=3)
    mixer_w = (w_conv_full, wpool_bd, pool_scale[:, None, :], sgu_ln_g[:, None, :], wsp_t, bias_t)
    g1, b1, g2, b2 = [a[:, None, :] for a in (ln1_g, ln1_b, ln2_g, ln2_b)]
    one, zero = jnp.ones((1, 1, D_MODEL), F32), jnp.zeros((1, 1, D_MODEL), F32)

    saved = []
    prev, pg, pb = xs, (one, 0), (zero, 0)
    prev_b = xs.astype(BF16)
    weights = []
    for l in range(L):
        g_in, g_o = _ag_finish(gather[0], [] if l == 0 else [prev_b], "%da" % l)
        proj = _mm(prev_b, g_in, "nt", F32, 512, IN_W, "mm_proj", deps=[gather[1][4]] if l == 0 else [])
        mixcat = _mixer_fwd(proj, *mixer_w, l)
        xhat1, rstd1, h_b = _mm_ln_fwd(mixcat, g_o, prev, pg, pb, (g1, l), (b1, l), "mm_wo_ln")
        g_gu, g_dn = _ag_finish(gather[1], [h_b], "%db" % l)
        weights.append((g_in, g_gu, g_o, g_dn))
        deps = []
        if l + 1 < L:
            gather = _ag_start_layer(shards, l + 1, [g_gu])
            deps = [gather[1][4]]
        g_act, u_act, act = _mm_swiglu_fwd(h_b, g_gu, deps=deps)
        xhat2, rstd2, y_b = _mm_ln_fwd(act, g_dn, xhat1, (g1, l), (b1, l), (g2, l), (b2, l), "mm_down_ln")
        saved.append((prev_b, proj, mixcat, xhat1, rstd1, h_b, g_act, u_act, act, xhat2, rstd2))
        prev, pg, pb, prev_b = xhat2, (g2, l), (b2, l), y_b

    loss_tile, dy = _loss_head(prev, pg, pb, target)

    small = [None] * L
    big = None
    sibling_flight = None
    above = None
    for l in reversed(range(L)):
        prev_b, proj, mixcat, xhat1, rstd1, h_b, g_act, u_act, act, xhat2, rstd2 = saved[l]
        g_in, g_gu, g_o, g_dn = weights[l]
        chip_flight = None
        if above is None:
            dr2, dr2_b, dg2, db2 = _ln_bwd(None, dy, xhat2, rstd2, (g2, l))
        else:
            dr2, dr2_b, dg2, db2 = _mm_ln_bwd([above[0]], above[1], above[2], xhat2, rstd2, (g2, l),
                                              "mm_dx_ln", deps=[sibling_flight[4]])
            chip_flight = _rs_continue(sibling_flight, [dr2_b], c_arr, str(l + 1))
        dg_b, du_b = _mm_swiglu_bwd(dr2_b, g_dn, g_act, u_act, deps=[chip_flight[4]] if chip_flight else [])
        p_dn = _mm(act, dr2_b, "tn", BF16, DW_TM, D_MODEL, "mm_dw_down")
        p_gu = _mm_tn_pair(dg_b, du_b, h_b, DW_TM, "mm_dw_gate_up")
        ffn_sibling = _rs_begin([p_gu, p_dn], "0b") if l == 0 else None
        dr1, dr1_b, dg1, db1, dmix = _mm_ln_bwd([dg_b, du_b], g_gu, dr2, xhat1, rstd1, (g1, l), "mm_dh_ln",
                                                deps=[ffn_sibling[4]] if l == 0 else [], w_back=g_o)
        ffn_flight = _rs_continue(ffn_sibling, [dr1_b], c_arr, "0b") if l == 0 else None
        p_o = _mm(mixcat, dr1_b, "tn", BF16, 512, D_MODEL, "mm_dw_o")
        dproj, dwc, dwp, dps, dlng, dwsp, dbias = _mixer_bwd(proj, dmix, *mixer_w, l,
                                                             deps=[ffn_flight[4]] if l == 0 else [])
        p_in = _mm(dproj, prev_b, "tn", BF16, IN_W, D_MODEL, "mm_dw_in")
        small[l] = (dwc, dwp, dps, dlng, dwsp, dbias, dg1, db1, dg2, db2)
        above = (dproj, g_in, dr1)
        if chip_flight is not None:
            big = list(_rs_chip_finish(chip_flight, [p_in], q_arr, str(l + 1), l + 1, big))
        if l > 0:
            sibling_flight = _rs_begin([p_in, p_gu, p_o, p_dn], str(l))
        else:
            big[1], big[3] = _rs_chip_finish(ffn_flight, [p_in, p_o], q_arr, "0b", 0, [big[1], big[3]])

    def stack(i):
        return jnp.stack([small[l][i] for l in range(L)])

    dwp_bd = stack(1).reshape(L, 2, 2, HALF, 2, HALF)
    dwp_all = jnp.einsum("ltgchd,gh->ltgcd", dwp_bd, eye2).reshape(L, 4, HALF, HALF)
    dbs_all = jnp.swapaxes(stack(5)[:, :, :, :2], 2, 3).reshape(L, 6, CHUNK)
    small_grads = [stack(0), dwp_all, stack(2).reshape(L, POOL_W), stack(3).reshape(L, SGU_W),
                   stack(4).reshape(L, 6, CHUNK, CHUNK), dbs_all] + [stack(i).reshape(L, D_MODEL) for i in (6, 7, 8, 9)]
    packed_small = _allreduce_small(_pack(small_grads), deps=[big[1]])
    sibling_flight = _rs_begin([p_in, p_o], "0a", after=[packed_small])
    grad_x = _mm_ln_bwd([above[0]], above[1], above[2], None, None, None, "mm_dx_out", deps=[sibling_flight[4]])
    last_flight = _rs_continue(sibling_flight, [grad_x], c_arr, "0a")
    return loss_tile, grad_x, (big, last_flight, q_arr), (packed_small, [a.shape for a in small_grads])


def _rs_chip_finish(in_flight, after, q, tag, layer, into):
    send_sems, recv_sems, sums, lands, _ = in_flight
    sums, got = _rs_chip_wait(send_sems, recv_sems, sums, lands, after, tag)
    return _rs_finish(sums, got, q, layer, into)


def _reduce_and_update(big_grads, small_grads, big_w, big_m, big_v, small_w, small_m, small_v):
    L = DEPTH
    mx, my, mc = _my_place()
    dev = 4 * mx + 2 * my + mc
    conv_cols = CONV_W // N_DEV
    w_in, w_gate_up, w_o, w_down = big_w
    m_w_in, m_w_gate_up, m_w_o, m_w_down = big_m
    v_w_in, v_w_gate_up, v_w_o, v_w_down = big_v
    packed_g, small_shapes = small_grads
    big, last_flight, q_arr = big_grads

    def widen_conv(a):
        return lax.dynamic_update_slice(jnp.zeros((L, 3, CONV_W), F32), a, (0, 0, dev * conv_cols))

    small_m = [widen_conv(small_m[0])] + list(small_m[1:])
    small_v = [widen_conv(small_v[0])] + list(small_v[1:])
    pk_d, pk_m, pk_v = _adamw(_pack(small_w), packed_g, _pack(small_m), _pack(small_v), packed_g.shape[0] // 2)
    sg = _unpack(packed_g, small_shapes)
    sd = _unpack(pk_d, small_shapes)
    sm = _unpack(pk_m, small_shapes)
    sv = _unpack(pk_v, small_shapes)

    def conv_cols_of(a):
        return lax.dynamic_slice(a, (0, 0, dev * conv_cols), (L, 3, conv_cols))

    for lst in (sg, sd, sm, sv):
        lst[0] = conv_cols_of(lst[0])

    tr = lambda a: jnp.swapaxes(a, 1, 2)
    gt_gu, g_w_dn = big[1], big[3]
    d_gu, m_gu, v_gu = [tr(a) for a in _adamw(tr(w_gate_up), gt_gu, tr(m_w_gate_up), tr(v_w_gate_up), gt_gu.shape[1] // 2)]
    d_dn, m_dn, v_dn = _adamw(w_down, g_w_dn, m_w_down, v_w_down, 352)
    gt_in, g_w_o = _rs_chip_finish(last_flight, [d_gu, d_dn, pk_d], q_arr, "0a", 0, [big[0], big[2]])
    d_in, m_in, v_in = [tr(a) for a in _adamw(tr(w_in), gt_in, tr(m_w_in), tr(v_w_in), gt_in.shape[1])]
    d_o, m_o, v_o = _adamw(w_o, g_w_o, m_w_o, v_w_o, 128)
    g_w_in, g_w_gu = tr(gt_in), tr(gt_gu)

    def ordered(big_in, big_o, big_gu, big_dn, sm_list):
        return [big_in, sm_list[0], sm_list[1], sm_list[2], sm_list[3], sm_list[4], sm_list[5], big_o,
                sm_list[6], sm_list[7], big_gu, big_dn, sm_list[8], sm_list[9]]

    grads = ordered(g_w_in, g_w_o, g_w_gu, g_w_dn, sg)
    deltas = ordered(d_in, d_o, d_gu, d_dn, sd)
    new_m = ordered(m_in, m_o, m_gu, m_dn, sm)
    new_v = ordered(v_in, v_o, v_gu, v_dn, sv)
    return grads, deltas, new_m, new_v
```

```python
import math

import jax
import jax.numpy as jnp
from jax import lax
from jax.experimental import pallas as pl
from jax.experimental.pallas import tpu as pltpu

F32 = jnp.float32
BF16 = jnp.bfloat16
MESH = pl.DeviceIdType.MESH

D_MODEL = 1024
DEPTH = 4
CONV_W = 384
POOL_W = 256
SGU_W = 384
IN_W = 3 * CONV_W + POOL_W + 2 * SGU_W
D_FF = 2816
CHUNK = 128
ALPHA = float((2 * DEPTH) ** 0.25)
LN_EPS = 1e-5
ADAM_LR, ADAM_B1, ADAM_B2, ADAM_EPS, ADAM_WD, ADAM_STEP = 0.001, 0.9, 0.999, 1e-08, 0.01, 10

N_DEV = 8
LANES = 128
HALF = 64
VMEM_LIMIT = 52 * 1024 * 1024

INV_SQRT2 = 0.7071067811865476
INV_SQRT_2PI = 0.3989422804014327


def _cparams(sem=None, **kw):
    if sem is not None:
        kw["dimension_semantics"] = sem
    return pltpu.CompilerParams(vmem_limit_bytes=VMEM_LIMIT, **kw)


_DN = {"nt": (((1,), (1,)), ((), ())), "tn": (((0,), (0,)), ((), ()))}


def _mm(a, b, mode, out_dtype, tm, tn, name, deps=()):
    if mode == "nt":
        (M, K), N = a.shape, b.shape[0]
        a_spec = pl.BlockSpec((tm, K), lambda i, j: (i, 0))
        b_spec = pl.BlockSpec((tn, K), lambda i, j: (j, 0))
    else:
        (K, M), N = a.shape, b.shape[1]
        a_spec = pl.BlockSpec((K, tm), lambda i, j: (0, i))
        b_spec = pl.BlockSpec((K, tn), lambda i, j: (0, j))
    assert M % tm == 0 and N % tn == 0, (M, N, K, tm, tn)
    nd = len(deps)

    def body(*refs):
        a_ref, b_ref, o_ref = refs[0], refs[1], refs[2 + nd]
        o_ref[...] = lax.dot_general(a_ref[...], b_ref[...], _DN[mode], preferred_element_type=F32).astype(o_ref.dtype)

    return pl.pallas_call(
        body,
        name=name,
        grid=(M // tm, N // tn),
        in_specs=[a_spec, b_spec] + [pl.BlockSpec(memory_space=pl.ANY)] * nd,
        out_specs=pl.BlockSpec((tm, tn), lambda i, j: (i, j)),
        out_shape=jax.ShapeDtypeStruct((M, N), out_dtype),
        compiler_params=_cparams(("parallel", "parallel")),
    )(a, b, *deps)


def _mm_tn_pair(a1, a2, b, tm, name):
    K, M = a1.shape
    N = b.shape[1]
    n1 = M // tm

    def body(a1_ref, a2_ref, b_ref, o_ref):
        i = pl.program_id(0)

        @pl.when(i < n1)
        def _():
            o_ref[...] = lax.dot_general(a1_ref[...], b_ref[...], _DN["tn"], preferred_element_type=F32).astype(o_ref.dtype)

        @pl.when(i >= n1)
        def _():
            o_ref[...] = lax.dot_general(a2_ref[...], b_ref[...], _DN["tn"], preferred_element_type=F32).astype(o_ref.dtype)

    return pl.pallas_call(
        body, name=name, grid=(2 * n1,),
        in_specs=[pl.BlockSpec((K, tm), lambda i: (0, jnp.minimum(i, n1 - 1))),
                  pl.BlockSpec((K, tm), lambda i: (0, jnp.maximum(i - n1, 0))),
                  pl.BlockSpec((K, N), lambda i: (0, 0))],
        out_specs=pl.BlockSpec((tm, N), lambda i: (i, 0)),
        out_shape=jax.ShapeDtypeStruct((2 * M, N), BF16),
        compiler_params=_cparams(("arbitrary",)),
    )(a1, a2, b)


LN_SUB = 256
LN_TM = 512


def _vec(v):
    arr, layer = v
    return arr, pl.BlockSpec((None, 1, D_MODEL), lambda *_: (layer, 0, 0))


def _mm_ln_fwd(a, b, prev, pg, pb, g, bias, name):
    T, K = a.shape
    tm = LN_TM

    def body(a_ref, b_ref, prev_ref, pg_ref, pb_ref, g_ref, bias_ref, xhat_ref, rstd_ref, y_ref):
        for s in range(tm // LN_SUB):
            rows = slice(s * LN_SUB, (s + 1) * LN_SUB)
            mm = jnp.dot(a_ref[rows, :], b_ref[...], preferred_element_type=F32)
            r = ALPHA * (prev_ref[rows, :] * pg_ref[...] + pb_ref[...]) + mm
            mu = jnp.mean(r, axis=-1, keepdims=True)
            xc = r - mu
            var = jnp.mean(xc * xc, axis=-1, keepdims=True)
            rstd = lax.rsqrt(var + LN_EPS)
            xhat = xc * rstd
            xhat_ref[rows, :] = xhat
            rstd_ref[rows, :] = rstd
            y_ref[rows, :] = (xhat * g_ref[...] + bias_ref[...]).astype(y_ref.dtype)

    row = pl.BlockSpec((tm, D_MODEL), lambda i: (i, 0))
    vecs = [_vec(v) for v in (pg, pb, g, bias)]
    return pl.pallas_call(
        body, name=name, grid=(T // tm,),
        in_specs=[pl.BlockSpec((tm, K), lambda i: (i, 0)),
                  pl.BlockSpec((K, D_MODEL), lambda i: (0, 0), pipeline_mode=pl.Buffered(1)),
                  row] + [s for _, s in vecs],
        out_specs=[row, pl.BlockSpec((tm, 1), lambda i: (i, 0)), row],
        out_shape=[jax.ShapeDtypeStruct((T, D_MODEL), F32), jax.ShapeDtypeStruct((T, 1), F32),
                   jax.ShapeDtypeStruct((T, D_MODEL), BF16)],
        compiler_params=_cparams(("parallel",)),
    )(a, b, prev, *[a_ for a_, _ in vecs])


def _mm_ln_bwd(a_list, b, dres, xhat, rstd, g, name, deps=(), w_back=None):
    T = a_list[0].shape[0]
    tm = LN_TM
    na, nd = len(a_list), len(deps)
    ks = [a.shape[1] for a in a_list]
    last = xhat is None
    nout = 1 if last else (5 if w_back is not None else 4)

    def body(*refs):
        a_refs, b_ref, dres_ref = refs[:na], refs[na], refs[na + 1]
        if not last:
            xhat_ref, rstd_ref, g_ref = refs[na + 2:na + 5]
            dr_ref, drb_ref, dg_ref, db_ref = refs[len(refs) - nout:len(refs) - nout + 4]

            @pl.when(pl.program_id(0) == 0)
            def _():
                dg_ref[...] = jnp.zeros_like(dg_ref)
                db_ref[...] = jnp.zeros_like(db_ref)

        for s in range(tm // LN_SUB):
            rows = slice(s * LN_SUB, (s + 1) * LN_SUB)
            mm, off = None, 0
            for a_ref, k in zip(a_refs, ks):
                part = jnp.dot(a_ref[rows, :], b_ref[off:off + k, :], preferred_element_type=F32)
                mm = part if mm is None else mm + part
                off += k
            dy = ALPHA * dres_ref[rows, :] + mm
            if last:
                refs[-1][rows, :] = dy
                continue
            xhat_v = xhat_ref[rows, :]
            dg_ref[...] += jnp.sum(dy * xhat_v, axis=0, keepdims=True)
            db_ref[...] += jnp.sum(dy, axis=0, keepdims=True)
            dxh = dy * g_ref[...]
            m1 = jnp.mean(dxh, axis=-1, keepdims=True)
            m2 = jnp.mean(dxh * xhat_v, axis=-1, keepdims=True)
            dr = rstd_ref[rows, :] * (dxh - m1 - xhat_v * m2)
            dr_ref[rows, :] = dr
            dr_b = dr.astype(drb_ref.dtype)
            drb_ref[rows, :] = dr_b
            if w_back is not None:
                refs[-1][rows, :] = lax.dot_general(dr_b, refs[na + 5][...], _DN["nt"], preferred_element_type=F32)

    row = pl.BlockSpec((tm, D_MODEL), lambda i: (i, 0))
    vec = pl.BlockSpec((1, D_MODEL), lambda i: (0, 0))
    in_specs = [pl.BlockSpec((tm, k), lambda i: (i, 0)) for k in ks]
    in_specs += [pl.BlockSpec((sum(ks), D_MODEL), lambda i: (0, 0), pipeline_mode=pl.Buffered(1)), row]
    args = list(a_list) + [b, dres]
    if last:
        out_specs, out_shape = row, jax.ShapeDtypeStruct((T, D_MODEL), F32)
    else:
        g_arr, g_spec = _vec(g)
        in_specs += [row, pl.BlockSpec((tm, 1), lambda i: (i, 0)), g_spec]
        args += [xhat, rstd, g_arr]
        out_specs = [row, row, vec, vec]
        out_shape = [jax.ShapeDtypeStruct((T, D_MODEL), F32), jax.ShapeDtypeStruct((T, D_MODEL), BF16),
                     jax.ShapeDtypeStruct((1, D_MODEL), F32), jax.ShapeDtypeStruct((1, D_MODEL), F32)]
        if w_back is not None:
            in_specs.append(pl.BlockSpec(w_back.shape, lambda i: (0, 0), pipeline_mode=pl.Buffered(1)))
            args.append(w_back)
            out_specs.append(row)
            out_shape.append(jax.ShapeDtypeStruct((T, w_back.shape[0]), F32))
    return pl.pallas_call(
        body, name=name, grid=(T // tm,),
        in_specs=in_specs + [pl.BlockSpec(memory_space=pl.ANY)] * nd,
        out_specs=out_specs, out_shape=out_shape,
        compiler_params=_cparams(("parallel",) if last else ("arbitrary",)),
    )(*args, *deps)


DW_TM = 1408
FF_TN = 1408
FF_TM = 512
SAVED_GU = BF16


def _mm_swiglu_fwd(h, w_gu, deps=()):
    T = h.shape[0]
    tm = min(T, FF_TM)
    nj = D_FF // FF_TN
    nd = len(deps)

    def body(*refs):
        h_ref, wg_ref, wu_ref = refs[:3]
        g_ref, u_ref, act_ref = refs[3 + nd:]
        hv = h_ref[...]
        gv = lax.dot_general(hv, wg_ref[...], _DN["nt"], preferred_element_type=F32)
        uv = lax.dot_general(hv, wu_ref[...], _DN["nt"], preferred_element_type=F32)
        g_ref[...] = gv.astype(g_ref.dtype)
        u_ref[...] = uv.astype(u_ref.dtype)
        act_ref[...] = (gv * jax.nn.sigmoid(gv) * uv).astype(act_ref.dtype)

    tile = pl.BlockSpec((tm, FF_TN), lambda j, i: (i, j))
    return pl.pallas_call(
        body, name="mm_gate_up_swiglu", grid=(nj, T // tm),
        in_specs=[pl.BlockSpec((tm, D_MODEL), lambda j, i: (i, 0)),
                  pl.BlockSpec((FF_TN, D_MODEL), lambda j, i: (j, 0)),
                  pl.BlockSpec((FF_TN, D_MODEL), lambda j, i: (j + nj, 0))] + [pl.BlockSpec(memory_space=pl.ANY)] * nd,
        out_specs=[tile, tile, tile],
        out_shape=[jax.ShapeDtypeStruct((T, D_FF), SAVED_GU), jax.ShapeDtypeStruct((T, D_FF), SAVED_GU),
                   jax.ShapeDtypeStruct((T, D_FF), BF16)],
        compiler_params=_cparams(("parallel", "parallel")),
    )(h, w_gu, w_gu, *deps)


def _mm_swiglu_bwd(dr, w_dn, g, u, deps=()):
    T = dr.shape[0]
    tm = min(T, FF_TM)

    def body(*refs):
        dr_ref, w_ref, g_ref, u_ref = refs[:4]
        dg_ref, du_ref = refs[-2:]
        da = lax.dot_general(dr_ref[...], w_ref[...], _DN["nt"], preferred_element_type=F32)
        gv, uv = g_ref[...].astype(F32), u_ref[...].astype(F32)
        s = jax.nn.sigmoid(gv)
        du_ref[...] = (da * (gv * s)).astype(du_ref.dtype)
        dg_ref[...] = (da * uv * (s * (1.0 + gv * (1.0 - s)))).astype(dg_ref.dtype)

    tile = pl.BlockSpec((tm, FF_TN), lambda j, i: (i, j))
    return pl.pallas_call(
        body, name="mm_dact_swiglu", grid=(D_FF // FF_TN, T // tm),
        in_specs=[pl.BlockSpec((tm, D_MODEL), lambda j, i: (i, 0)), pl.BlockSpec((FF_TN, D_MODEL), lambda j, i: (j, 0)),
                  tile, tile] + [ANY] * len(deps),
        out_specs=[tile, tile],
        out_shape=[jax.ShapeDtypeStruct((T, D_FF), BF16)] * 2,
        compiler_params=_cparams(("parallel", "parallel")),
    )(dr, w_dn, g, u, *deps)


def _gelu(x):
    return 0.5 * x * (1.0 + lax.erf(x * INV_SQRT2))


def _gelu_grad(x):
    return 0.5 * (1.0 + lax.erf(x * INV_SQRT2)) + x * (jnp.exp(-0.5 * x * x) * INV_SQRT_2PI)


def _shift_down(z, k):
    row = lax.broadcasted_iota(jnp.int32, z.shape, 0)
    return jnp.where(row >= k, pltpu.roll(z, k, 0), 0.0)


def _shift_up(z, k):
    n = z.shape[0]
    row = lax.broadcasted_iota(jnp.int32, z.shape, 0)
    return jnp.where(row < n - k, pltpu.roll(z, n - k, 0), 0.0)


def _lo_mask(shape):
    return lax.broadcasted_iota(jnp.int32, shape, len(shape) - 1) < HALF


def _seg_mean(x, lo):
    a = jnp.sum(jnp.where(lo, x, 0.0), axis=-1, keepdims=True)
    b = jnp.sum(jnp.where(lo, 0.0, x), axis=-1, keepdims=True)
    return jnp.where(lo, a, b) * (1.0 / HALF)


def _pool_windows(first):
    lo = _lo_mask((1, LANES))
    return jnp.where(first, jnp.where(lo, 2.0, 4.0), jnp.where(lo, 8.0, 16.0)), lo


def _pool_mean_minus_token(p, first):
    wl, lo = _pool_windows(first)
    s2 = p + _shift_down(p, 1)
    s4 = s2 + _shift_down(s2, 2)
    s8 = s4 + _shift_down(s4, 4)
    s16 = s8 + _shift_down(s8, 8)
    win = jnp.where(first, jnp.where(lo, s2, s4), jnp.where(lo, s8, s16))
    t1 = (lax.broadcasted_iota(jnp.int32, p.shape, 0) + 1).astype(F32)
    count = jnp.minimum(t1, wl)
    return win / count - p, count


SGU_UNROLL = 4


def _tril_keep():
    r = lax.broadcasted_iota(jnp.int32, (2 * CHUNK, CHUNK), 0)
    s = lax.broadcasted_iota(jnp.int32, (2 * CHUNK, CHUNK), 1)
    return s <= (r & (CHUNK - 1))


def _sgu_chunk_fwd(u, v, g, wm, bias, lo):
    ug = _gelu(u)
    vg = _gelu(v)
    mu = _seg_mean(vg, lo)
    xc = vg - mu
    var = _seg_mean(xc * xc, lo)
    rstd = lax.rsqrt(var + LN_EPS)
    vn = xc * rstd
    vh = (vn * g).astype(BF16)
    mm2 = jnp.dot(wm, vh, preferred_element_type=F32)
    mixed = jnp.where(lo, mm2[:CHUNK], mm2[CHUNK:]) + bias
    return ug, vn, rstd, vh, mixed


def _mixer_fwd(proj, wconv, wpool_bd, pscale, lng, wsp, bias, layer):
    T = proj.shape[0]
    nchunk = T // CHUNK

    def body(a_ref, b_ref, c_ref, wc_ref, wp_ref, ps_ref, lng_ref, wsp_ref, bias_ref, o_ref):
        j = pl.program_id(0)

        @pl.when(j < 3)
        def _conv():
            z = c_ref[...] * a_ref[...]
            w = wc_ref[...]
            y = w[0:1] * _shift_down(z, 2) + w[1:2] * _shift_down(z, 1) + w[2:3] * z
            o_ref[...] = (b_ref[...] * y).astype(o_ref.dtype)

        @pl.when((j >= 3) & (j < 5))
        def _pool():
            d, _ = _pool_mean_minus_token(a_ref[...], j == 3)
            y = jnp.dot(d.astype(BF16), wp_ref[...].astype(BF16), preferred_element_type=F32)
            o_ref[...] = (y * ps_ref[...]).astype(o_ref.dtype)

        @pl.when(j >= 5)
        def _sgu():
            lo = _lo_mask((CHUNK, LANES))
            wm = jnp.where(_tril_keep(), wsp_ref[...], 0.0).astype(BF16)
            bias_t = bias_ref[...]
            g = lng_ref[...]

            def chunk(n, carry):
                rows = pl.ds(pl.multiple_of(n * CHUNK, CHUNK), CHUNK)
                ug, _, _, _, mixed = _sgu_chunk_fwd(a_ref[rows, :], b_ref[rows, :], g, wm, bias_t, lo)
                o_ref[rows, :] = (ug * mixed).astype(o_ref.dtype)
                return carry

            lax.fori_loop(0, nchunk, chunk, 0, unroll=SGU_UNROLL)

    def col(f):
        return lambda j: (0, f(j))

    clip = lambda v, lo, hi: jnp.minimum(jnp.maximum(v, lo), hi)
    return pl.pallas_call(
        body,
        name="mixer_fwd",
        grid=(8,),
        in_specs=[
            pl.BlockSpec((T, LANES), col(lambda j: jnp.where(j < 3, j, jnp.where(j < 5, j + 6, j + 6)))),
            pl.BlockSpec((T, LANES), col(lambda j: jnp.where(j < 3, j + 3, jnp.where(j < 5, 5, j + 9)))),
            pl.BlockSpec((T, LANES), col(lambda j: jnp.where(j < 3, j + 6, 8))),
            pl.BlockSpec((None, 3, LANES), lambda j: (layer, 0, clip(j, 0, 2))),
            pl.BlockSpec((None, None, LANES, LANES), lambda j: (layer, clip(j - 3, 0, 1), 0, 0)),
            pl.BlockSpec((None, 1, LANES), lambda j: (layer, 0, clip(j - 3, 0, 1))),
            pl.BlockSpec((None, 1, LANES), lambda j: (layer, 0, clip(j - 5, 0, 2))),
            pl.BlockSpec((None, None, 2 * CHUNK, CHUNK), lambda j: (layer, clip(j - 5, 0, 2), 0, 0)),
            pl.BlockSpec((None, None, CHUNK, LANES), lambda j: (layer, clip(j - 5, 0, 2), 0, 0)),
        ],
        out_specs=pl.BlockSpec((T, LANES), lambda j: (0, j)),
        out_shape=jax.ShapeDtypeStruct((T, D_MODEL), BF16),
        compiler_params=_cparams(("arbitrary",)),
    )(proj, proj, proj, wconv, wpool_bd, pscale, lng, wsp, bias)


def _mixer_bwd(proj, dmix, wconv, wpool_bd, pscale, lng, wsp, bias, layer, deps=()):
    T = proj.shape[0]
    nchunk = T // CHUNK

    def body(*refs):
        a_ref, b_ref, c_ref, dm_ref, wc_ref, wp_ref, ps_ref, lng_ref, wsp_ref, bias_ref = refs[:10]
        o_ref, dwc_ref, dwp_ref, dps_ref, dlng_ref, dwsp_ref, dbias_ref, keep1, keep2 = refs[10 + len(deps):]
        k = pl.program_id(0)

        @pl.when(k < 3)
        def _conv():
            xa, gb, gc, dya = a_ref[...], b_ref[...], c_ref[...], dm_ref[...]
            w = wc_ref[...]
            z = gc * xa
            z1 = _shift_down(z, 1)
            z2 = _shift_down(z, 2)
            y = w[0:1] * z2 + w[1:2] * z1 + w[2:3] * z
            dyv = dya * gb
            dz = w[2:3] * dyv + w[1:2] * _shift_up(dyv, 1) + w[0:1] * _shift_up(dyv, 2)
            dwc_ref[0:1, :] = jnp.sum(dyv * z2, axis=0, keepdims=True)
            dwc_ref[1:2, :] = jnp.sum(dyv * z1, axis=0, keepdims=True)
            dwc_ref[2:3, :] = jnp.sum(dyv * z, axis=0, keepdims=True)
            o_ref[...] = (dz * gc).astype(o_ref.dtype)
            keep1[k] = (dya * y).astype(keep1.dtype)
            keep1[k + 3] = (dz * xa).astype(keep1.dtype)

        @pl.when((k >= 3) & (k < 9))
        def _emit_gb_gc():
            o_ref[...] = keep1[k - 3]

        @pl.when((k >= 9) & (k < 11))
        def _pool():
            first = k == 9
            p, dyb = a_ref[...], dm_ref[...]
            d, count = _pool_mean_minus_token(p, first)
            w2 = wp_ref[...].astype(BF16)
            db = d.astype(BF16)
            y = jnp.dot(db, w2, preferred_element_type=F32)
            dps_ref[...] = jnp.sum(dyb * y, axis=0, keepdims=True)
            dyv = (dyb * ps_ref[...]).astype(BF16)
            dd = lax.dot_general(dyv, w2, _DN["nt"], preferred_element_type=F32)
            dwp_ref[...] = lax.dot_general(db, dyv, _DN["tn"], preferred_element_type=F32)
            dwin = dd / count
            a2 = dwin + _shift_up(dwin, 1)
            a4 = a2 + _shift_up(a2, 2)
            a8 = a4 + _shift_up(a4, 4)
            a16 = a8 + _shift_up(a8, 8)
            _, lo = _pool_windows(first)
            back = jnp.where(first, jnp.where(lo, a2, a4), jnp.where(lo, a8, a16))
            o_ref[...] = (back - dd).astype(o_ref.dtype)

        @pl.when((k >= 11) & (k < 14))
        def _sgu():
            lo = _lo_mask((CHUNK, LANES))
            keep = _tril_keep()
            wm = jnp.where(keep, wsp_ref[...], 0.0).astype(BF16)
            bias_t = bias_ref[...]
            g = lng_ref[...]
            dwsp_ref[...] = jnp.zeros_like(dwsp_ref)
            dbias_ref[...] = jnp.zeros_like(dbias_ref)
            dlng_ref[...] = jnp.zeros_like(dlng_ref)

            def chunk(n, carry):
                rows = pl.ds(pl.multiple_of(n * CHUNK, CHUNK), CHUNK)
                u, v, dyc = a_ref[rows, :], b_ref[rows, :], dm_ref[rows, :]
                ug, vn, rstd, vh, mixed = _sgu_chunk_fwd(u, v, g, wm, bias_t, lo)
                dmx = dyc * ug
                o_ref[rows, :] = (dyc * mixed * _gelu_grad(u)).astype(o_ref.dtype)
                dbias_ref[...] += dmx
                dst = jnp.concatenate([jnp.where(lo, dmx, 0.0), jnp.where(lo, 0.0, dmx)], axis=0).astype(BF16)
                dwsp_ref[...] += lax.dot_general(dst, vh, _DN["nt"], preferred_element_type=F32)
                dvh = lax.dot_general(wm, dst, _DN["tn"], preferred_element_type=F32)
                dlng_ref[...] += jnp.sum(dvh * vn, axis=0, keepdims=True)
                dvn = dvh * g
                m1 = _seg_mean(dvn, lo)
                m2 = _seg_mean(dvn * vn, lo)
                dvg = rstd * (dvn - m1 - vn * m2)
                keep2[k - 11, rows, :] = (dvg * _gelu_grad(v)).astype(keep2.dtype)
                return carry

            lax.fori_loop(0, nchunk, chunk, 0, unroll=SGU_UNROLL)
            dwsp_ref[...] = jnp.where(keep, dwsp_ref[...], 0.0)
            dbt = dbias_ref[...]
            lane = lax.broadcasted_iota(jnp.int32, (CHUNK, LANES), 1)
            sa = jnp.sum(jnp.where(lo, dbt, 0.0), axis=-1, keepdims=True)
            sb = jnp.sum(jnp.where(lo, 0.0, dbt), axis=-1, keepdims=True)
            dbias_ref[...] = jnp.where(lane == 0, sa, jnp.where(lane == 1, sb, 0.0))

        @pl.when(k >= 14)
        def _emit_v():
            o_ref[...] = keep2[k - 14]

    def col(f):
        return lambda k: (0, f(k))

    clip = lambda v, lo, hi: jnp.minimum(jnp.maximum(v, lo), hi)
    view_a = lambda k: jnp.where(k < 3, k, jnp.where(k < 9, 2, jnp.where(k < 14, k, 13)))
    view_b = lambda k: jnp.where(k < 3, k + 3, jnp.where(k < 11, 5, jnp.where(k < 14, k + 3, 16)))
    view_c = lambda k: jnp.where(k < 3, k + 6, 8)
    view_dm = lambda k: jnp.where(k < 3, k, jnp.where(k < 9, 2, jnp.where(k < 14, k - 6, 7)))
    return pl.pallas_call(
        body,
        name="mixer_bwd",
        grid=(17,),
        in_specs=[
            pl.BlockSpec((T, LANES), col(view_a)),
            pl.BlockSpec((T, LANES), col(view_b)),
            pl.BlockSpec((T, LANES), col(view_c)),
            pl.BlockSpec((T, LANES), col(view_dm)),
            pl.BlockSpec((None, 3, LANES), lambda k: (layer, 0, clip(k, 0, 2))),
            pl.BlockSpec((None, None, LANES, LANES), lambda k: (layer, clip(k - 9, 0, 1), 0, 0)),
            pl.BlockSpec((None, 1, LANES), lambda k: (layer, 0, clip(k - 9, 0, 1))),
            pl.BlockSpec((None, 1, LANES), lambda k: (layer, 0, clip(k - 11, 0, 2))),
            pl.BlockSpec((None, None, 2 * CHUNK, CHUNK), lambda k: (layer, clip(k - 11, 0, 2), 0, 0)),
            pl.BlockSpec((None, None, CHUNK, LANES), lambda k: (layer, clip(k - 11, 0, 2), 0, 0)),
        ] + [pl.BlockSpec(memory_space=pl.ANY)] * len(deps),
        out_specs=[
            pl.BlockSpec((T, LANES), lambda k: (0, k)),
            pl.BlockSpec((3, LANES), col(lambda k: clip(k, 0, 2))),
            pl.BlockSpec((None, LANES, LANES), lambda k: (clip(k - 9, 0, 1), 0, 0)),
            pl.BlockSpec((1, LANES), col(lambda k: clip(k - 9, 0, 1))),
            pl.BlockSpec((1, LANES), col(lambda k: clip(k - 11, 0, 2))),
            pl.BlockSpec((None, 2 * CHUNK, CHUNK), lambda k: (clip(k - 11, 0, 2), 0, 0)),
            pl.BlockSpec((None, CHUNK, LANES), lambda k: (clip(k - 11, 0, 2), 0, 0)),
        ],
        out_shape=[
            jax.ShapeDtypeStruct((T, IN_W), BF16),
            jax.ShapeDtypeStruct((3, CONV_W), F32),
            jax.ShapeDtypeStruct((2, LANES, LANES), F32),
            jax.ShapeDtypeStruct((1, POOL_W), F32),
            jax.ShapeDtypeStruct((1, SGU_W), F32),
            jax.ShapeDtypeStruct((3, 2 * CHUNK, CHUNK), F32),
            jax.ShapeDtypeStruct((3, CHUNK, LANES), F32),
        ],
        scratch_shapes=[pltpu.VMEM((6, T, LANES), BF16), pltpu.VMEM((3, T, LANES), BF16)],
        compiler_params=_cparams(("arbitrary",)),
    )(proj, proj, proj, dmix, wconv, wpool_bd, pscale, lng, wsp, bias, *deps)


def _ln_bwd(dres, dmm, xhat, rstd, g, tm=256, deps=()):
    T = xhat.shape[0]
    has_res = dres is not None
    nd = len(deps)

    def body(*refs):
        refs = refs[:len(refs) - 4 - nd] + refs[len(refs) - 4:]
        if has_res:
            dres_ref, dmm_ref, xhat_ref, rstd_ref, g_ref, dr_ref, drb_ref, dg_ref, db_ref = refs
            dy = ALPHA * dres_ref[...] + dmm_ref[...]
        else:
            dmm_ref, xhat_ref, rstd_ref, g_ref, dr_ref, drb_ref, dg_ref, db_ref = refs
            dy = dmm_ref[...]
        xhat_v = xhat_ref[...]

        @pl.when(pl.program_id(0) == 0)
        def _():
            dg_ref[...] = jnp.zeros_like(dg_ref)
            db_ref[...] = jnp.zeros_like(db_ref)

        dg_ref[...] += jnp.sum(dy * xhat_v, axis=0, keepdims=True)
        db_ref[...] += jnp.sum(dy, axis=0, keepdims=True)
        dxh = dy * g_ref[...]
        m1 = jnp.mean(dxh, axis=-1, keepdims=True)
        m2 = jnp.mean(dxh * xhat_v, axis=-1, keepdims=True)
        dr = rstd_ref[...] * (dxh - m1 - xhat_v * m2)
        dr_ref[...] = dr
        drb_ref[...] = dr.astype(drb_ref.dtype)

    row = pl.BlockSpec((tm, D_MODEL), lambda i: (i, 0))
    vec = pl.BlockSpec((1, D_MODEL), lambda i: (0, 0))
    g_arr, g_spec = _vec(g)
    in_specs = ([row] if has_res else []) + [row, row, pl.BlockSpec((tm, 1), lambda i: (i, 0)), g_spec]
    in_specs += [pl.BlockSpec(memory_space=pl.ANY)] * nd
    args = ([dres] if has_res else []) + [dmm, xhat, rstd, g_arr] + list(deps)
    return pl.pallas_call(
        body,
        name="ln_bwd_res" if has_res else "ln_bwd",
        grid=(T // tm,),
        in_specs=in_specs,
        out_specs=[row, row, vec, vec],
        out_shape=[jax.ShapeDtypeStruct((T, D_MODEL), F32), jax.ShapeDtypeStruct((T, D_MODEL), BF16),
                   jax.ShapeDtypeStruct((1, D_MODEL), F32), jax.ShapeDtypeStruct((1, D_MODEL), F32)],
        compiler_params=_cparams(("arbitrary",)),
    )(*args)


def _loss_head(xhat, g, b, target, tm=256):
    T = xhat.shape[0]

    def body(xhat_ref, g_ref, b_ref, t_ref, loss_ref, dy_ref):
        err = xhat_ref[...] * g_ref[...] + b_ref[...] - t_ref[...]

        @pl.when(pl.program_id(0) == 0)
        def _():
            loss_ref[...] = jnp.zeros_like(loss_ref)

        part = jnp.sum(jnp.sum(err * err, axis=-1, keepdims=True), axis=0, keepdims=True)
        loss_ref[...] += jnp.broadcast_to(part * (0.5 / D_MODEL), loss_ref.shape)
        dy_ref[...] = err * (1.0 / D_MODEL)

    row = pl.BlockSpec((tm, D_MODEL), lambda i: (i, 0))
    (g_arr, g_spec), (b_arr, b_spec) = _vec(g), _vec(b)
    return pl.pallas_call(
        body,
        name="loss_head",
        grid=(T // tm,),
        in_specs=[row, g_spec, b_spec, row],
        out_specs=[pl.BlockSpec((8, LANES), lambda i: (0, 0)), row],
        out_shape=[jax.ShapeDtypeStruct((8, LANES), F32), jax.ShapeDtypeStruct((T, D_MODEL), F32)],
        compiler_params=_cparams(("arbitrary",)),
    )(xhat, g_arr, b_arr, target)


def _adamw(w, g, m, v, tr):
    R, C = w.shape[-2:]
    assert R % tr == 0
    c1 = 1.0 - ADAM_B1 ** ADAM_STEP
    c2 = 1.0 - ADAM_B2 ** ADAM_STEP

    def body(w_ref, g_ref, m_ref, v_ref, d_ref, mo_ref, vo_ref):
        gv = g_ref[...]
        mn = ADAM_B1 * m_ref[...] + (1.0 - ADAM_B1) * gv
        vn = ADAM_B2 * v_ref[...] + (1.0 - ADAM_B2) * (gv * gv)
        d_ref[...] = -ADAM_LR * ((mn / c1) / (jnp.sqrt(vn / c2) + ADAM_EPS) + ADAM_WD * w_ref[...])
        mo_ref[...] = mn
        vo_ref[...] = vn

    if w.ndim == 2:
        grid, blk = (R // tr,), pl.BlockSpec((tr, C), lambda i: (i, 0))
    else:
        grid, blk = (w.shape[0], R // tr), pl.BlockSpec((None, tr, C), lambda l, i: (l, i, 0))
    return pl.pallas_call(
        body, name="adamw", grid=grid, in_specs=[blk] * 4, out_specs=[blk] * 3,
        out_shape=[jax.ShapeDtypeStruct(w.shape, F32)] * 3, compiler_params=_cparams(("parallel",) * len(grid)),
    )(w, g, m, v)


def _my_place():
    return lax.axis_index("x"), lax.axis_index("y"), lax.axis_index("c")


ANY = pl.BlockSpec(memory_space=pl.ANY)
HBM = pl.BlockSpec(memory_space=pltpu.HBM)
SEM = pl.BlockSpec(memory_space=pltpu.SEMAPHORE)
EFFECT = pltpu.SideEffectType.DATAFLOW_SIDE_EFFECTING


def _in_hbm(a):
    return pltpu.with_memory_space_constraint(a, pltpu.HBM)


def _block_rows(ref, dev):
    r = ref.shape[0] // N_DEV
    start = pl.multiple_of((4 * dev[0] + 2 * dev[1] + dev[2]) * r, 16)
    return ref.at[pl.ds(start, r), :]


def _ag_first_copies(s_refs, land_refs, send_sems, recv_sems, receiving):
    x, y, c = _my_place()
    peers = [(x, y, 1 - c)] + [(*chip, c) for chip in _other_chips(x, y)]
    copies = []
    for k, peer in enumerate(peers):
        block = peer if receiving else (x, y, c)
        copies += [pltpu.make_async_remote_copy(
            src_ref=s_refs[w], dst_ref=_block_rows(land_refs[w], block),
            send_sem=send_sems.at[k * len(s_refs) + w], recv_sem=recv_sems.at[k * len(s_refs) + w],
            device_id=peer, device_id_type=MESH)
            for w in range(len(s_refs))]
    return copies


def _ag_start(shards, layer, after=()):
    nw = len(shards)

    def body(*refs):
        s_refs, land_refs = refs[:nw], refs[nw:2 * nw]
        token = refs[-1]
        sems = 2 * nw + len(after)
        for cp in _ag_first_copies(s_refs, land_refs, refs[sems], refs[sems + 1], False):
            cp.start()
        token[...] = jnp.zeros_like(token)

    lands = [lax.empty((N_DEV * s.shape[0], D_MODEL), BF16) for s in shards]
    out = pl.pallas_call(
        body, name="ag_start_%s" % layer,
        in_specs=[HBM] * (2 * nw) + [ANY] * len(after),
        out_specs=(SEM, SEM, *[HBM] * (2 * nw), pl.BlockSpec(memory_space=pltpu.VMEM)),
        out_shape=(pltpu.SemaphoreType.DMA((4 * nw,)), pltpu.SemaphoreType.DMA((4 * nw,)),
                   *[pltpu.HBM(a.shape, a.dtype) for a in list(shards) + lands],
                   jax.ShapeDtypeStruct((8, LANES), F32)),
        input_output_aliases={i: 2 + i for i in range(2 * nw)},
        compiler_params=pltpu.CompilerParams(has_side_effects=EFFECT),
    )(*[_in_hbm(a) for a in list(shards) + lands], *after)
    return out[0], out[1], out[2:2 + nw], out[2 + nw:2 + 2 * nw], out[-1]


def _ag_wait(send_sems, recv_sems, shards, lands, after, layer):
    nw = len(shards)

    def body(*refs):
        s_refs, land_refs = refs[:nw], refs[nw:2 * nw]
        for cp in _ag_first_copies(s_refs, land_refs, refs[2 * nw], refs[2 * nw + 1], True):
            cp.wait_send()
            cp.wait_recv()

    out = pl.pallas_call(
        body, name="ag_wait_%s" % layer,
        in_specs=[HBM] * (2 * nw) + [SEM, SEM] + [ANY] * len(after),
        out_specs=[HBM] * (2 * nw),
        out_shape=[pltpu.HBM(a.shape, a.dtype) for a in list(shards) + list(lands)],
        input_output_aliases={i: i for i in range(2 * nw)},
        compiler_params=pltpu.CompilerParams(has_side_effects=EFFECT),
    )(*shards, *lands, send_sems, recv_sems, *after)
    return out[:nw], out[nw:]


def _ag_pass_on(shards, lands):
    nw = len(shards)

    def body(*refs):
        s_refs, g_refs = refs[:nw], refs[2 * nw:3 * nw]
        send_sems, recv_sems, local_sems = refs[3 * nw:3 * nw + 3]
        stage = refs[3 * nw + 3:]
        x, y, c = _my_place()
        load = [pltpu.make_async_copy(s_refs[w], stage[w], local_sems.at[w]) for w in range(nw)]
        mine = [pltpu.make_async_copy(stage[w], _block_rows(g_refs[w], (x, y, c)), local_sems.at[w])
                for w in range(nw)]
        for cp in load:
            cp.start()
        sends, arrivals = [], []
        for j, chip in enumerate(_other_chips(x, y)):
            for w in range(nw):
                rows_out = _block_rows(g_refs[w], (*chip, c))
                rows_in = _block_rows(g_refs[w], (*chip, 1 - c))
                sends.append(pltpu.make_async_remote_copy(
                    src_ref=rows_out, dst_ref=rows_out, send_sem=send_sems.at[j, w], recv_sem=recv_sems.at[j, w],
                    device_id=(x, y, 1 - c), device_id_type=MESH))
                arrivals.append(pltpu.make_async_remote_copy(
                    src_ref=rows_in, dst_ref=rows_in, send_sem=send_sems.at[j, w], recv_sem=recv_sems.at[j, w],
                    device_id=(x, y, 1 - c), device_id_type=MESH))
        for cp in sends:
            cp.start()
        for w in range(nw):
            load[w].wait()
            mine[w].start()
        for cp in arrivals:
            cp.wait_recv()
        for cp in sends:
            cp.wait_send()
        for cp in mine:
            cp.wait()

    return pl.pallas_call(
        body, name="ag_pass_on",
        in_specs=[ANY] * (2 * nw), out_specs=[ANY] * nw,
        out_shape=[jax.ShapeDtypeStruct(a.shape, a.dtype) for a in lands],
        input_output_aliases={nw + i: i for i in range(nw)},
        scratch_shapes=[pltpu.SemaphoreType.DMA((3, nw)), pltpu.SemaphoreType.DMA((3, nw)),
                        pltpu.SemaphoreType.DMA((nw,))] + [pltpu.VMEM(s.shape, s.dtype) for s in shards],
        compiler_params=_cparams(),
    )(*shards, *lands)


def _rs_sibling_copies(p_refs, land_refs, send_sems, recv_sems):
    x, y, c = _my_place()
    return [pltpu.make_async_remote_copy(
        src_ref=p_refs[w].at[:, 1 - c], dst_ref=land_refs[w],
        send_sem=send_sems.at[w], recv_sem=recv_sems.at[w], device_id=(x, y, 1 - c), device_id_type=MESH)
        for w in range(len(p_refs))]


def _rs_sibling_start(parts, tag, after=()):
    nw = len(parts)
    sems = 2 * nw + len(after)

    def body(*refs):
        for cp in _rs_sibling_copies(refs[:nw], refs[nw:2 * nw], refs[sems], refs[sems + 1]):
            cp.start()
        refs[-1][...] = jnp.zeros_like(refs[-1])

    lands = [lax.empty(p.shape[:1] + p.shape[2:], BF16) for p in parts]
    out = pl.pallas_call(
        body, name="rs_sibling_start_%s" % tag,
        in_specs=[HBM] * (2 * nw) + [ANY] * len(after),
        out_specs=(SEM, SEM, *[HBM] * (2 * nw), pl.BlockSpec(memory_space=pltpu.VMEM)),
        out_shape=(pltpu.SemaphoreType.DMA((nw,)), pltpu.SemaphoreType.DMA((nw,)),
                   *[pltpu.HBM(a.shape, a.dtype) for a in list(parts) + lands],
                   jax.ShapeDtypeStruct((8, LANES), F32)),
        input_output_aliases={i: 2 + i for i in range(2 * nw)},
        compiler_params=pltpu.CompilerParams(has_side_effects=EFFECT),
    )(*[_in_hbm(a) for a in list(parts) + lands], *after)
    return out[0], out[1], out[2:2 + nw], out[2 + nw:2 + 2 * nw], out[-1]


def _rs_sibling_wait(send_sems, recv_sems, parts, lands, after, tag):
    nw = len(parts)

    def body(*refs):
        for cp in _rs_sibling_copies(refs[:nw], refs[nw:2 * nw], refs[2 * nw], refs[2 * nw + 1]):
            cp.wait_send()
            cp.wait_recv()

    out = pl.pallas_call(
        body, name="rs_sibling_wait_%s" % tag,
        in_specs=[HBM] * (2 * nw) + [SEM, SEM] + [ANY] * len(after),
        out_specs=[HBM] * (2 * nw),
        out_shape=[pltpu.HBM(a.shape, a.dtype) for a in list(parts) + list(lands)],
        input_output_aliases={i: i for i in range(2 * nw)},
        compiler_params=pltpu.CompilerParams(has_side_effects=EFFECT),
    )(*parts, *lands, send_sems, recv_sems, *after)
    return out[:nw], out[nw:]


def _rs_chip_sum(parts, gots, c):
    n = len(parts)

    def body(c_ref, *refs):
        for p_ref, g_ref, o_ref in zip(refs[:n], refs[n:2 * n], refs[2 * n:]):
            o_ref[...] = (p_ref[...].astype(F32) + g_ref[...].astype(F32)).astype(o_ref.dtype)

    mine = [pl.BlockSpec((None, None, p.shape[2], D_MODEL), lambda q, c_ref: (q, c_ref[0], 0, 0)) for p in parts]
    theirs = [pl.BlockSpec((None, g.shape[1], D_MODEL), lambda q, c_ref: (q, 0, 0)) for g in gots]
    return pl.pallas_call(
        body, name="rs_chip_sum",
        grid_spec=pltpu.PrefetchScalarGridSpec(
            num_scalar_prefetch=1, grid=(4,), in_specs=mine + theirs, out_specs=theirs),
        out_shape=[jax.ShapeDtypeStruct(g.shape, BF16) for g in gots],
        compiler_params=_cparams(("parallel",)),
    )(c, *parts, *gots)


def _other_chips(x, y):
    return [(1 - x, y), (x, 1 - y), (1 - x, 1 - y)]


def _rs_chip_copies(s_refs, land_refs, send_sems, recv_sems):
    x, y, c = _my_place()
    copies = []
    for k, chip in enumerate(_other_chips(x, y)):
        q = 2 * chip[0] + chip[1]
        copies += [pltpu.make_async_remote_copy(
            src_ref=s_refs[w].at[q], dst_ref=land_refs[w].at[k],
            send_sem=send_sems.at[k * len(s_refs) + w], recv_sem=recv_sems.at[k * len(s_refs) + w],
            device_id=(*chip, c), device_id_type=MESH)
            for w in range(len(s_refs))]
    return copies


def _rs_chip_start(sums, layer):
    nw = len(sums)

    def body(*refs):
        s_refs, land_refs = refs[:nw], refs[nw:2 * nw]
        send_sems, recv_sems = refs[2 * nw], refs[2 * nw + 1]
        token = refs[-1]
        for cp in _rs_chip_copies(s_refs, land_refs, send_sems, recv_sems):
            cp.start()
        token[...] = jnp.zeros_like(token)

    lands = [lax.empty((3,) + s.shape[1:], BF16) for s in sums]
    out = pl.pallas_call(
        body, name="rs_chip_start_%s" % layer,
        in_specs=[HBM] * (2 * nw),
        out_specs=(SEM, SEM, *[HBM] * (2 * nw), pl.BlockSpec(memory_space=pltpu.VMEM)),
        out_shape=(pltpu.SemaphoreType.DMA((3 * nw,)), pltpu.SemaphoreType.DMA((3 * nw,)),
                   *[pltpu.HBM(a.shape, a.dtype) for a in list(sums) + lands],
                   jax.ShapeDtypeStruct((8, LANES), F32)),
        input_output_aliases={i: 2 + i for i in range(2 * nw)},
        compiler_params=pltpu.CompilerParams(has_side_effects=EFFECT),
    )(*[_in_hbm(a) for a in list(sums) + lands])
    return out[0], out[1], out[2:2 + nw], out[2 + nw:2 + 2 * nw], out[-1]


def _rs_chip_wait(send_sems, recv_sems, sums, lands, after, layer):
    nw = len(sums)

    def body(*refs):
        s_refs, land_refs = refs[:nw], refs[nw:2 * nw]
        for cp in _rs_chip_copies(s_refs, land_refs, refs[2 * nw], refs[2 * nw + 1]):
            cp.wait_send()
            cp.wait_recv()

    out = pl.pallas_call(
        body, name="rs_chip_wait_%s" % layer,
        in_specs=[HBM] * (2 * nw) + [SEM, SEM] + [ANY] * len(after),
        out_specs=[HBM] * (2 * nw),
        out_shape=[pltpu.HBM(a.shape, a.dtype) for a in list(sums) + list(lands)],
        input_output_aliases={i: i for i in range(2 * nw)},
        compiler_params=pltpu.CompilerParams(has_side_effects=EFFECT),
    )(*sums, *lands, send_sems, recv_sems, *after)
    return out[:nw], out[nw:]


def _rs_finish(sums, gots, q, layer, into):
    n = len(sums)

    def body(q_ref, *refs):
        for s_ref, g_ref, o_ref in zip(refs[:n], refs[n:2 * n], refs[len(refs) - n:]):
            o_ref[...] = ((s_ref[...].astype(F32) + g_ref[0].astype(F32)) + g_ref[1].astype(F32)) + g_ref[2].astype(F32)

    rows = [s.shape[1] for s in sums]
    in_specs = [pl.BlockSpec((None, r, D_MODEL), lambda i, q_ref: (q_ref[0], 0, 0)) for r in rows]
    in_specs += [pl.BlockSpec((3, r, D_MODEL), lambda i, q_ref: (0, 0, 0)) for r in rows]
    args = [q, *sums, *gots]
    aliases = {}
    if into is not None:
        in_specs += [ANY] * n
        aliases = {len(args) + i: i for i in range(n)}
        args += list(into)
    return pl.pallas_call(
        body, name="rs_finish",
        grid_spec=pltpu.PrefetchScalarGridSpec(
            num_scalar_prefetch=1, grid=(1,), in_specs=in_specs,
            out_specs=[pl.BlockSpec((None, r, D_MODEL), lambda i, q_ref: (layer, 0, 0)) for r in rows]),
        out_shape=[jax.ShapeDtypeStruct((DEPTH, r, D_MODEL), F32) for r in rows],
        input_output_aliases=aliases,
        compiler_params=_cparams(("arbitrary",)),
    )(*args)


def _allreduce_small(vec, deps=()):
    R = vec.shape[0]
    assert R % (8 * N_DEV) == 0
    P = R // N_DEV
    nd = len(deps)

    def body(*refs):
        v_ref = refs[0]
        o_ref, buf, send1, recv1, send2, recv2 = refs[1 + nd:]
        x, y, c = _my_place()
        me = 4 * x + 2 * y + c

        def piece(ref, d):
            return ref.at[pl.ds(pl.multiple_of(d * P, 8), P), :]

        def peer(k):
            p = me ^ k
            return p, (p >> 2, (p >> 1) & 1, p & 1)

        scatter = []
        for k in range(1, N_DEV):
            p, where = peer(k)
            scatter.append(pltpu.make_async_remote_copy(
                src_ref=piece(v_ref, p), dst_ref=buf.at[k], send_sem=send1.at[k - 1], recv_sem=recv1.at[k - 1],
                device_id=where, device_id_type=MESH))
        for cp in scatter:
            cp.start()
        buf[0] = piece(v_ref, me)[...]
        for cp in scatter:
            cp.wait()
        acc = buf[me]
        for d in range(1, N_DEV):
            acc = acc + buf[me ^ d]
        piece(o_ref, me)[...] = acc
        spread, arrivals = [], []
        for k in range(1, N_DEV):
            p, where = peer(k)
            spread.append(pltpu.make_async_remote_copy(
                src_ref=piece(o_ref, me), dst_ref=piece(o_ref, me), send_sem=send2.at[k - 1], recv_sem=recv2.at[k - 1],
                device_id=where, device_id_type=MESH))
            arrivals.append(pltpu.make_async_remote_copy(
                src_ref=piece(o_ref, p), dst_ref=piece(o_ref, p), send_sem=send2.at[k - 1], recv_sem=recv2.at[k - 1],
                device_id=where, device_id_type=MESH))
        for cp in spread:
            cp.start()
        for cp in arrivals:
            cp.wait_recv()
        for cp in spread:
            cp.wait_send()

    sems = pltpu.SemaphoreType.DMA((N_DEV - 1,))
    return pl.pallas_call(
        body, name="allreduce_small",
        in_specs=[pl.BlockSpec(memory_space=pltpu.VMEM)] + [ANY] * nd, out_specs=pl.BlockSpec(memory_space=pltpu.VMEM),
        out_shape=jax.ShapeDtypeStruct((R, LANES), F32),
        scratch_shapes=[pltpu.VMEM((N_DEV, P, LANES), F32), sems, sems, sems, sems],
        compiler_params=_cparams(),
    )(vec, *deps)


def _pack(arrs):
    flat = jnp.concatenate([a.reshape(-1) for a in arrs])
    pad = (-flat.shape[0]) % (8 * N_DEV * LANES)
    return jnp.pad(flat, (0, pad)).reshape(-1, LANES)


def _unpack(packed, shapes):
    flat = packed.reshape(-1)
    out, off = [], 0
    for s in shapes:
        n = math.prod(s)
        out.append(flat[off:off + n].reshape(s))
        off += n
    return out


def kernel(x, w_in, w_conv, w_pool, pool_scale, sgu_ln_g, w_spatial, b_spatial, w_o, ln1_g, ln1_b, w_gate_up, w_down, ln2_g, ln2_b, loss_target, m_w_in, m_w_conv, m_w_pool, m_pool_scale, m_sgu_ln_g, m_w_spatial, m_b_spatial, m_w_o, m_ln1_g, m_ln1_b, m_w_gate_up, m_w_down, m_ln2_g, m_ln2_b, v_w_in, v_w_conv, v_w_pool, v_pool_scale, v_sgu_ln_g, v_w_spatial, v_b_spatial, v_w_o, v_ln1_g, v_ln1_b, v_w_gate_up, v_w_down, v_ln2_g, v_ln2_b):
    L = DEPTH
    T = x.shape[1]
    mx, my, mc = _my_place()
    dev = 4 * mx + 2 * my + mc
    xs = x[0]
    target = loss_target[0]

    conv_cols = w_conv.shape[2]
    w_conv_z = lax.dynamic_update_slice(jnp.zeros((L, 3, CONV_W), F32), w_conv, (0, 0, dev * conv_cols))
    w_conv_packed = _allreduce_small(_pack([w_conv_z]))
    w_conv_full = _unpack(w_conv_packed, [(L, 3, CONV_W)])[0]

    shards = (jnp.swapaxes(w_in, 1, 2).astype(BF16), jnp.swapaxes(w_gate_up, 1, 2).astype(BF16),
              w_o.astype(BF16), w_down.astype(BF16))
    first_gather = _ag_start_layer(shards, 0, [w_conv_packed])

    loss_tile, grad_x2, big_grads, small_grads = _local_step(
        xs, target, shards, first_gather, w_conv_full, w_pool, pool_scale, sgu_ln_g, w_spatial, b_spatial,
        ln1_g, ln1_b, ln2_g, ln2_b)
    grad_x = grad_x2[None]
    big_w = (w_in, w_gate_up, w_o, w_down)
    big_m = (m_w_in, m_w_gate_up, m_w_o, m_w_down)
    big_v = (v_w_in, v_w_gate_up, v_w_o, v_w_down)
    small_w = [w_conv_full, w_pool, pool_scale, sgu_ln_g, w_spatial, b_spatial, ln1_g, ln1_b, ln2_g, ln2_b]
    small_m = [m_w_conv, m_w_pool, m_pool_scale, m_sgu_ln_g, m_w_spatial, m_b_spatial, m_ln1_g, m_ln1_b, m_ln2_g, m_ln2_b]
    small_v = [v_w_conv, v_w_pool, v_pool_scale, v_sgu_ln_g, v_w_spatial, v_b_spatial, v_ln1_g, v_ln1_b, v_ln2_g, v_ln2_b]
    loss, grads, deltas, new_m, new_v = _reduce_and_update(
        big_grads, small_grads, big_w, big_m, big_v, small_w, small_m, small_v)
    return (loss, grad_x, *grads, *deltas, *new_m, *new_v)


def _ag_start_layer(shards, l, after):
    s_in, s_gu, s_o, s_dn = [s[l] for s in shards]
    first = _ag_start([s_in, s_o], "%da" % l, after=after)
    return first, _ag_start([s_gu, s_dn], "%db" % l, after=[first[4]])


def _ag_finish(gather, after, tag):
    send_sems, recv_sems, shards, lands, _ = gather
    shards, lands = _ag_wait(send_sems, recv_sems, shards, lands, after, tag)
    return _ag_pass_on(shards, lands)


def _rs_begin(parts, tag, after=()):
    return _rs_sibling_start([p.reshape(4, 2, p.shape[0] // N_DEV, D_MODEL) for p in parts], tag, after)


def _rs_continue(sibling_flight, after, c_arr, tag):
    send_sems, recv_sems, parts, lands, _ = sibling_flight
    parts, got = _rs_sibling_wait(send_sems, recv_sems, parts, lands, after, tag)
    return _rs_chip_start(_rs_chip_sum(parts, got, c_arr), tag)


def _local_step(xs, target, shards, gather, w_conv_full, w_pool, pool_scale, sgu_ln_g, w_spatial, b_spatial,
                ln1_g, ln1_b, ln2_g, ln2_b):
    L = DEPTH
    T = xs.shape[0]
    mx, my, mc = _my_place()
    c_arr = jnp.reshape(mc, (1,)).astype(jnp.int32)
    q_arr = jnp.reshape(2 * mx + my, (1,)).astype(jnp.int32)
    eye2 = jnp.eye(2, dtype=F32)
    wp = w_pool.reshape(L, 2, 2, HALF, HALF)
    wpool_bd = jnp.einsum("ltgcd,gh->ltgchd", wp, eye2).reshape(L, 2, LANES, LANES)
    wsp_t = w_spatial.reshape(L, 3, 2 * CHUNK, CHUNK)
    bias_t = jnp.repeat(jnp.swapaxes(b_spatial.reshape(L, 3, 2, CHUNK), 2, 3), HALF, axis=3)
    mixer_w = (w_conv_full, wpool_bd, pool_scale[:, None, :], sgu_ln_g[:, None, :], wsp_t, bias_t)
    g1, b1, g2, b2 = [a[:, None, :] for a in (ln1_g, ln1_b, ln2_g, ln2_b)]
    one, zero = jnp.ones((1, 1, D_MODEL), F32), jnp.zeros((1, 1, D_MODEL), F32)

    saved = []
    prev, pg, pb = xs, (one, 0), (zero, 0)
    prev_b = xs.astype(BF16)
    weights = []
    for l in range(L):
        g_in, g_o = _ag_finish(gather[0], [] if l == 0 else [prev_b], "%da" % l)
        proj = _mm(prev_b, g_in, "nt", F32, 512, IN_W, "mm_proj", deps=[gather[1][4]] if l == 0 else [])
        mixcat = _mixer_fwd(proj, *mixer_w, l)
        xhat1, rstd1, h_b = _mm_ln_fwd(mixcat, g_o, prev, pg, pb, (g1, l), (b1, l), "mm_wo_ln")
        g_gu, g_dn = _ag_finish(gather[1], [h_b], "%db" % l)
        weights.append((g_in, g_gu, g_o, g_dn))
        deps = []
        if l + 1 < L:
            gather = _ag_start_layer(shards, l + 1, [g_gu])
            deps = [gather[1][4]]
        g_act, u_act, act = _mm_swiglu_fwd(h_b, g_gu, deps=deps)
        xhat2, rstd2, y_b = _mm_ln_fwd(act, g_dn, xhat1, (g1, l), (b1, l), (g2, l), (b2, l), "mm_down_ln")
        saved.append((prev_b, proj, mixcat, xhat1, rstd1, h_b, g_act, u_act, act, xhat2, rstd2))
        prev, pg, pb, prev_b = xhat2, (g2, l), (b2, l), y_b

    loss_tile, dy = _loss_head(prev, pg, pb, target)

    small = [None] * L
    big = None
    sibling_flight = None
    above = None
    for l in reversed(range(L)):
        prev_b, proj, mixcat, xhat1, rstd1, h_b, g_act, u_act, act, xhat2, rstd2 = saved[l]
        g_in, g_gu, g_o, g_dn = weights[l]
        chip_flight = None
        if above is None:
            dr2, dr2_b, dg2, db2 = _ln_bwd(None, dy, xhat2, rstd2, (g2, l))
        else:
            dr2, dr2_b, dg2, db2 = _mm_ln_bwd([above[0]], above[1], above[2], xhat2, rstd2, (g2, l),
                                              "mm_dx_ln", deps=[sibling_flight[4]])
            chip_flight = _rs_continue(sibling_flight, [dr2_b], c_arr, str(l + 1))
        dg_b, du_b = _mm_swiglu_bwd(dr2_b, g_dn, g_act, u_act, deps=[chip_flight[4]] if chip_flight else [])
        p_dn = _mm(act, dr2_b, "tn", BF16, DW_TM, D_MODEL, "mm_dw_down")
        p_gu = _mm_tn_pair(dg_b, du_b, h_b, DW_TM, "mm_dw_gate_up")
        ffn_sibling = _rs_begin([p_gu, p_dn], "0b") if l == 0 else None
        dr1, dr1_b, dg1, db1, dmix = _mm_ln_bwd([dg_b, du_b], g_gu, dr2, xhat1, rstd1, (g1, l), "mm_dh_ln",
                                                deps=[ffn_sibling[4]] if l == 0 else [], w_back=g_o)
        ffn_flight = _rs_continue(ffn_sibling, [dr1_b], c_arr, "0b") if l == 0 else None
        p_o = _mm(mixcat, dr1_b, "tn", BF16, 512, D_MODEL, "mm_dw_o")
        dproj, dwc, dwp, dps, dlng, dwsp, dbias = _mixer_bwd(proj, dmix, *mixer_w, l,
                                                             deps=[ffn_flight[4]] if l == 0 else [])
        p_in = _mm(dproj, prev_b, "tn", BF16, IN_W, D_MODEL, "mm_dw_in")
        small[l] = (dwc, dwp, dps, dlng, dwsp, dbias, dg1, db1, dg2, db2)
        above = (dproj, g_in, dr1)
        if chip_flight is not None:
            big = list(_rs_chip_finish(chip_flight, [p_in], q_arr, str(l + 1), l + 1, big))
        if l > 0:
            sibling_flight = _rs_begin([p_in, p_gu, p_o, p_dn], str(l))
        else:
            big[1], big[3] = _rs_chip_finish(ffn_flight, [p_in, p_o], q_arr, "0b", 0, [big[1], big[3]])

    def stack(i):
        return jnp.stack([small[l][i] for l in range(L)])

    dwp_bd = stack(1).reshape(L, 2, 2, HALF, 2, HALF)
    dwp_all = jnp.einsum("ltgchd,gh->ltgcd", dwp_bd, eye2).reshape(L, 4, HALF, HALF)
    dbs_all = jnp.swapaxes(stack(5)[:, :, :, :2], 2, 3).reshape(L, 6, CHUNK)
    small_grads = [stack(0), dwp_all, stack(2).reshape(L, POOL_W), stack(3).reshape(L, SGU_W),
                   stack(4).reshape(L, 6, CHUNK, CHUNK), dbs_all] + [stack(i).reshape(L, D_MODEL) for i in (6, 7, 8, 9)]
    small_grads.append(loss_tile[0, :1])
    packed_small = _allreduce_small(_pack(small_grads), deps=[big[1]])
    sibling_flight = _rs_begin([p_in, p_o], "0a", after=[packed_small])
    grad_x = _mm_ln_bwd([above[0]], above[1], above[2], None, None, None, "mm_dx_out", deps=[sibling_flight[4]])
    last_flight = _rs_continue(sibling_flight, [grad_x], c_arr, "0a")
    return loss_tile, grad_x, (big, last_flight, q_arr), (packed_small, [a.shape for a in small_grads])


def _rs_chip_finish(in_flight, after, q, tag, layer, into):
    send_sems, recv_sems, sums, lands, _ = in_flight
    sums, got = _rs_chip_wait(send_sems, recv_sems, sums, lands, after, tag)
    return _rs_finish(sums, got, q, layer, into)


def _reduce_and_update(big_grads, small_grads, big_w, big_m, big_v, small_w, small_m, small_v):
    L = DEPTH
    mx, my, mc = _my_place()
    dev = 4 * mx + 2 * my + mc
    conv_cols = CONV_W // N_DEV
    w_in, w_gate_up, w_o, w_down = big_w
    m_w_in, m_w_gate_up, m_w_o, m_w_down = big_m
    v_w_in, v_w_gate_up, v_w_o, v_w_down = big_v
    packed_g, small_shapes = small_grads
    big, last_flight, q_arr = big_grads

    def widen_conv(a):
        return lax.dynamic_update_slice(jnp.zeros((L, 3, CONV_W), F32), a, (0, 0, dev * conv_cols))

    small_m = [widen_conv(small_m[0])] + list(small_m[1:])
    small_v = [widen_conv(small_v[0])] + list(small_v[1:])
    pk_d, pk_m, pk_v = _adamw(_pack(small_w), packed_g, _pack(small_m), _pack(small_v), packed_g.shape[0] // 2)
    sg = _unpack(packed_g, small_shapes)
    sd = _unpack(pk_d, small_shapes)
    sm = _unpack(pk_m, small_shapes)
    sv = _unpack(pk_v, small_shapes)

    def conv_cols_of(a):
        return lax.dynamic_slice(a, (0, 0, dev * conv_cols), (L, 3, conv_cols))

    for lst in (sg, sd, sm, sv):
        lst[0] = conv_cols_of(lst[0])

    tr = lambda a: jnp.swapaxes(a, 1, 2)
    gt_gu, g_w_dn = big[1], big[3]
    d_gu, m_gu, v_gu = [tr(a) for a in _adamw(tr(w_gate_up), gt_gu, tr(m_w_gate_up), tr(v_w_gate_up), gt_gu.shape[1] // 2)]
    d_dn, m_dn, v_dn = _adamw(w_down, g_w_dn, m_w_down, v_w_down, 352)
    gt_in, g_w_o = _rs_chip_finish(last_flight, [d_gu, d_dn, pk_d], q_arr, "0a", 0, [big[0], big[2]])
    d_in, m_in, v_in = [tr(a) for a in _adamw(tr(w_in), gt_in, tr(m_w_in), tr(v_w_in), gt_in.shape[1])]
    d_o, m_o, v_o = _adamw(w_o, g_w_o, m_w_o, v_w_o, 128)
    g_w_in, g_w_gu = tr(gt_in), tr(gt_gu)

    def ordered(big_in, big_o, big_gu, big_dn, sm_list):
        return [big_in, sm_list[0], sm_list[1], sm_list[2], sm_list[3], sm_list[4], sm_list[5], big_o,
                sm_list[6], sm_list[7], big_gu, big_dn, sm_list[8], sm_list[9]]

    grads = ordered(g_w_in, g_w_o, g_w_gu, g_w_dn, sg)
    deltas = ordered(d_in, d_o, d_gu, d_dn, sd)
    new_m = ordered(m_in, m_o, m_gu, m_dn, sm)
    new_v = ordered(v_in, v_o, v_gu, v_dn, sv)
    return sg[10][0], grads, deltas, new_m, new_v
```

```python
import math

import jax
import jax.numpy as jnp
from jax import lax
from jax.experimental import pallas as pl
from jax.experimental.pallas import tpu as pltpu

F32 = jnp.float32
BF16 = jnp.bfloat16
MESH = pl.DeviceIdType.MESH

D_MODEL = 1024
DEPTH = 4
CONV_W = 384
POOL_W = 256
SGU_W = 384
IN_W = 3 * CONV_W + POOL_W + 2 * SGU_W
D_FF = 2816
CHUNK = 128
ALPHA = float((2 * DEPTH) ** 0.25)
LN_EPS = 1e-5
ADAM_LR, ADAM_B1, ADAM_B2, ADAM_EPS, ADAM_WD, ADAM_STEP = 0.001, 0.9, 0.999, 1e-08, 0.01, 10

N_DEV = 8
LANES = 128
HALF = 64
VMEM_LIMIT = 52 * 1024 * 1024

INV_SQRT2 = 0.7071067811865476
INV_SQRT_2PI = 0.3989422804014327


def _cparams(sem=None, **kw):
    if sem is not None:
        kw["dimension_semantics"] = sem
    return pltpu.CompilerParams(vmem_limit_bytes=VMEM_LIMIT, **kw)


_DN = {"nt": (((1,), (1,)), ((), ())), "tn": (((0,), (0,)), ((), ()))}


def _mm(a, b, mode, out_dtype, tm, tn, name, deps=()):
    if mode == "nt":
        (M, K), N = a.shape, b.shape[0]
        a_spec = pl.BlockSpec((tm, K), lambda i, j: (i, 0))
        b_spec = pl.BlockSpec((tn, K), lambda i, j: (j, 0))
    else:
        (K, M), N = a.shape, b.shape[1]
        a_spec = pl.BlockSpec((K, tm), lambda i, j: (0, i))
        b_spec = pl.BlockSpec((K, tn), lambda i, j: (0, j))
    assert M % tm == 0 and N % tn == 0, (M, N, K, tm, tn)
    nd = len(deps)

    def body(*refs):
        a_ref, b_ref, o_ref = refs[0], refs[1], refs[2 + nd]
        o_ref[...] = lax.dot_general(a_ref[...], b_ref[...], _DN[mode], preferred_element_type=F32).astype(o_ref.dtype)

    return pl.pallas_call(
        body,
        name=name,
        grid=(M // tm, N // tn),
        in_specs=[a_spec, b_spec] + [pl.BlockSpec(memory_space=pl.ANY)] * nd,
        out_specs=pl.BlockSpec((tm, tn), lambda i, j: (i, j)),
        out_shape=jax.ShapeDtypeStruct((M, N), out_dtype),
        compiler_params=_cparams(("parallel", "parallel")),
    )(a, b, *deps)


def _mm_tn_pair(a1, a2, b, tm, name):
    K, M = a1.shape
    N = b.shape[1]
    n1 = M // tm

    def body(a1_ref, a2_ref, b_ref, o_ref):
        i = pl.program_id(0)

        @pl.when(i < n1)
        def _():
            o_ref[...] = lax.dot_general(a1_ref[...], b_ref[...], _DN["tn"], preferred_element_type=F32).astype(o_ref.dtype)

        @pl.when(i >= n1)
        def _():
            o_ref[...] = lax.dot_general(a2_ref[...], b_ref[...], _DN["tn"], preferred_element_type=F32).astype(o_ref.dtype)

    return pl.pallas_call(
        body, name=name, grid=(2 * n1,),
        in_specs=[pl.BlockSpec((K, tm), lambda i: (0, jnp.minimum(i, n1 - 1))),
                  pl.BlockSpec((K, tm), lambda i: (0, jnp.maximum(i - n1, 0))),
                  pl.BlockSpec((K, N), lambda i: (0, 0))],
        out_specs=pl.BlockSpec((tm, N), lambda i: (i, 0)),
        out_shape=jax.ShapeDtypeStruct((2 * M, N), BF16),
        compiler_params=_cparams(("arbitrary",)),
    )(a1, a2, b)


LN_SUB = 256
LN_TM = 512


def _vec(v):
    arr, layer = v
    return arr, pl.BlockSpec((None, 1, D_MODEL), lambda *_: (layer, 0, 0))


def _mm_ln_fwd(a, b, prev, pg, pb, g, bias, name):
    T, K = a.shape
    tm = LN_TM

    def body(a_ref, b_ref, prev_ref, pg_ref, pb_ref, g_ref, bias_ref, xhat_ref, rstd_ref, y_ref):
        for s in range(tm // LN_SUB):
            rows = slice(s * LN_SUB, (s + 1) * LN_SUB)
            mm = jnp.dot(a_ref[rows, :], b_ref[...], preferred_element_type=F32)
            r = ALPHA * (prev_ref[rows, :] * pg_ref[...] + pb_ref[...]) + mm
            mu = jnp.mean(r, axis=-1, keepdims=True)
            xc = r - mu
            var = jnp.mean(xc * xc, axis=-1, keepdims=True)
            rstd = lax.rsqrt(var + LN_EPS)
            xhat = xc * rstd
            xhat_ref[rows, :] = xhat
            rstd_ref[rows, :] = rstd
            y_ref[rows, :] = (xhat * g_ref[...] + bias_ref[...]).astype(y_ref.dtype)

    row = pl.BlockSpec((tm, D_MODEL), lambda i: (i, 0))
    vecs = [_vec(v) for v in (pg, pb, g, bias)]
    return pl.pallas_call(
        body, name=name, grid=(T // tm,),
        in_specs=[pl.BlockSpec((tm, K), lambda i: (i, 0)),
                  pl.BlockSpec((K, D_MODEL), lambda i: (0, 0), pipeline_mode=pl.Buffered(1)),
                  row] + [s for _, s in vecs],
        out_specs=[row, pl.BlockSpec((tm, 1), lambda i: (i, 0)), row],
        out_shape=[jax.ShapeDtypeStruct((T, D_MODEL), F32), jax.ShapeDtypeStruct((T, 1), F32),
                   jax.ShapeDtypeStruct((T, D_MODEL), BF16)],
        compiler_params=_cparams(("parallel",)),
    )(a, b, prev, *[a_ for a_, _ in vecs])


def _mm_ln_bwd(a_list, b, dres, xhat, rstd, g, name, deps=(), w_back=None):
    T = a_list[0].shape[0]
    tm = LN_TM
    na, nd = len(a_list), len(deps)
    ks = [a.shape[1] for a in a_list]
    last = xhat is None
    nout = 1 if last else (5 if w_back is not None else 4)

    def body(*refs):
        a_refs, b_ref, dres_ref = refs[:na], refs[na], refs[na + 1]
        if not last:
            xhat_ref, rstd_ref, g_ref = refs[na + 2:na + 5]
            dr_ref, drb_ref, dg_ref, db_ref = refs[len(refs) - nout:len(refs) - nout + 4]

            @pl.when(pl.program_id(0) == 0)
            def _():
                dg_ref[...] = jnp.zeros_like(dg_ref)
                db_ref[...] = jnp.zeros_like(db_ref)

        for s in range(tm // LN_SUB):
            rows = slice(s * LN_SUB, (s + 1) * LN_SUB)
            mm, off = None, 0
            for a_ref, k in zip(a_refs, ks):
                part = jnp.dot(a_ref[rows, :], b_ref[off:off + k, :], preferred_element_type=F32)
                mm = part if mm is None else mm + part
                off += k
            dy = ALPHA * dres_ref[rows, :] + mm
            if last:
                refs[-1][rows, :] = dy
                continue
            xhat_v = xhat_ref[rows, :]
            dg_ref[...] += jnp.sum(dy * xhat_v, axis=0, keepdims=True)
            db_ref[...] += jnp.sum(dy, axis=0, keepdims=True)
            dxh = dy * g_ref[...]
            m1 = jnp.mean(dxh, axis=-1, keepdims=True)
            m2 = jnp.mean(dxh * xhat_v, axis=-1, keepdims=True)
            dr = rstd_ref[rows, :] * (dxh - m1 - xhat_v * m2)
            dr_ref[rows, :] = dr
            dr_b = dr.astype(drb_ref.dtype)
            drb_ref[rows, :] = dr_b
            if w_back is not None:
                refs[-1][rows, :] = lax.dot_general(dr_b, refs[na + 5][...], _DN["nt"], preferred_element_type=F32)

    row = pl.BlockSpec((tm, D_MODEL), lambda i: (i, 0))
    vec = pl.BlockSpec((1, D_MODEL), lambda i: (0, 0))
    in_specs = [pl.BlockSpec((tm, k), lambda i: (i, 0)) for k in ks]
    in_specs += [pl.BlockSpec((sum(ks), D_MODEL), lambda i: (0, 0), pipeline_mode=pl.Buffered(1)), row]
    args = list(a_list) + [b, dres]
    if last:
        out_specs, out_shape = row, jax.ShapeDtypeStruct((T, D_MODEL), F32)
    else:
        g_arr, g_spec = _vec(g)
        in_specs += [row, pl.BlockSpec((tm, 1), lambda i: (i, 0)), g_spec]
        args += [xhat, rstd, g_arr]
        out_specs = [row, row, vec, vec]
        out_shape = [jax.ShapeDtypeStruct((T, D_MODEL), F32), jax.ShapeDtypeStruct((T, D_MODEL), BF16),
                     jax.ShapeDtypeStruct((1, D_MODEL), F32), jax.ShapeDtypeStruct((1, D_MODEL), F32)]
        if w_back is not None:
            in_specs.append(pl.BlockSpec(w_back.shape, lambda i: (0, 0), pipeline_mode=pl.Buffered(1)))
            args.append(w_back)
            out_specs.append(row)
            out_shape.append(jax.ShapeDtypeStruct((T, w_back.shape[0]), F32))
    return pl.pallas_call(
        body, name=name, grid=(T // tm,),
        in_specs=in_specs + [pl.BlockSpec(memory_space=pl.ANY)] * nd,
        out_specs=out_specs, out_shape=out_shape,
        compiler_params=_cparams(("parallel",) if last else ("arbitrary",)),
    )(*args, *deps)


DW_TM = 1408
FF_TN = 256
FF_TM = 2048
SAVED_GU = BF16


def _mm_swiglu_fwd(h, w_gu, deps=()):
    T = h.shape[0]
    tm = min(T, FF_TM)
    nj = D_FF // FF_TN
    nd = len(deps)

    def body(*refs):
        h_ref, wg_ref, wu_ref = refs[:3]
        g_ref, u_ref, act_ref = refs[3 + nd:]
        hv = h_ref[...]
        gv = lax.dot_general(hv, wg_ref[...], _DN["nt"], preferred_element_type=F32)
        uv = lax.dot_general(hv, wu_ref[...], _DN["nt"], preferred_element_type=F32)
        g_ref[...] = gv.astype(g_ref.dtype)
        u_ref[...] = uv.astype(u_ref.dtype)
        act_ref[...] = (gv * jax.nn.sigmoid(gv) * uv).astype(act_ref.dtype)

    tile = pl.BlockSpec((tm, FF_TN), lambda j, i: (i, j))
    return pl.pallas_call(
        body, name="mm_gate_up_swiglu", grid=(nj, T // tm),
        in_specs=[pl.BlockSpec((tm, D_MODEL), lambda j, i: (i, 0)),
                  pl.BlockSpec((FF_TN, D_MODEL), lambda j, i: (j, 0)),
                  pl.BlockSpec((FF_TN, D_MODEL), lambda j, i: (j + nj, 0))] + [pl.BlockSpec(memory_space=pl.ANY)] * nd,
        out_specs=[tile, tile, tile],
        out_shape=[jax.ShapeDtypeStruct((T, D_FF), SAVED_GU), jax.ShapeDtypeStruct((T, D_FF), SAVED_GU),
                   jax.ShapeDtypeStruct((T, D_FF), BF16)],
        compiler_params=_cparams(("parallel", "parallel")),
    )(h, w_gu, w_gu, *deps)


def _mm_swiglu_bwd(dr, w_dn, g, u, deps=()):
    T = dr.shape[0]
    tm = min(T, FF_TM)

    def body(*refs):
        dr_ref, w_ref, g_ref, u_ref = refs[:4]
        dg_ref, du_ref = refs[-2:]
        da = lax.dot_general(dr_ref[...], w_ref[...], _DN["nt"], preferred_element_type=F32)
        gv, uv = g_ref[...].astype(F32), u_ref[...].astype(F32)
        s = jax.nn.sigmoid(gv)
        du_ref[...] = (da * (gv * s)).astype(du_ref.dtype)
        dg_ref[...] = (da * uv * (s * (1.0 + gv * (1.0 - s)))).astype(dg_ref.dtype)

    tile = pl.BlockSpec((tm, FF_TN), lambda j, i: (i, j))
    return pl.pallas_call(
        body, name="mm_dact_swiglu", grid=(D_FF // FF_TN, T // tm),
        in_specs=[pl.BlockSpec((tm, D_MODEL), lambda j, i: (i, 0)), pl.BlockSpec((FF_TN, D_MODEL), lambda j, i: (j, 0)),
                  tile, tile] + [ANY] * len(deps),
        out_specs=[tile, tile],
        out_shape=[jax.ShapeDtypeStruct((T, D_FF), BF16)] * 2,
        compiler_params=_cparams(("parallel", "parallel")),
    )(dr, w_dn, g, u, *deps)


def _gelu(x):
    return 0.5 * x * (1.0 + lax.erf(x * INV_SQRT2))


def _gelu_grad(x):
    return 0.5 * (1.0 + lax.erf(x * INV_SQRT2)) + x * (jnp.exp(-0.5 * x * x) * INV_SQRT_2PI)


def _shift_down(z, k):
    row = lax.broadcasted_iota(jnp.int32, z.shape, 0)
    return jnp.where(row >= k, pltpu.roll(z, k, 0), 0.0)


def _shift_up(z, k):
    n = z.shape[0]
    row = lax.broadcasted_iota(jnp.int32, z.shape, 0)
    return jnp.where(row < n - k, pltpu.roll(z, n - k, 0), 0.0)


def _lo_mask(shape):
    return lax.broadcasted_iota(jnp.int32, shape, len(shape) - 1) < HALF


def _seg_mean(x, lo):
    a = jnp.sum(jnp.where(lo, x, 0.0), axis=-1, keepdims=True)
    b = jnp.sum(jnp.where(lo, 0.0, x), axis=-1, keepdims=True)
    return jnp.where(lo, a, b) * (1.0 / HALF)


def _pool_windows(first):
    lo = _lo_mask((1, LANES))
    return jnp.where(first, jnp.where(lo, 2.0, 4.0), jnp.where(lo, 8.0, 16.0)), lo


def _pool_mean_minus_token(p, first):
    wl, lo = _pool_windows(first)
    s2 = p + _shift_down(p, 1)
    s4 = s2 + _shift_down(s2, 2)
    s8 = s4 + _shift_down(s4, 4)
    s16 = s8 + _shift_down(s8, 8)
    win = jnp.where(first, jnp.where(lo, s2, s4), jnp.where(lo, s8, s16))
    t1 = (lax.broadcasted_iota(jnp.int32, p.shape, 0) + 1).astype(F32)
    count = jnp.minimum(t1, wl)
    return win / count - p, count


SGU_UNROLL = 4


def _tril_keep():
    r = lax.broadcasted_iota(jnp.int32, (2 * CHUNK, CHUNK), 0)
    s = lax.broadcasted_iota(jnp.int32, (2 * CHUNK, CHUNK), 1)
    return s <= (r & (CHUNK - 1))


def _sgu_chunk_fwd(u, v, g, wm, bias, lo):
    ug = _gelu(u)
    vg = _gelu(v)
    mu = _seg_mean(vg, lo)
    xc = vg - mu
    var = _seg_mean(xc * xc, lo)
    rstd = lax.rsqrt(var + LN_EPS)
    vn = xc * rstd
    vh = (vn * g).astype(BF16)
    mm2 = jnp.dot(wm, vh, preferred_element_type=F32)
    mixed = jnp.where(lo, mm2[:CHUNK], mm2[CHUNK:]) + bias
    return ug, vn, rstd, vh, mixed


def _mixer_fwd(proj, wconv, wpool_bd, pscale, lng, wsp, bias, layer):
    T = proj.shape[0]
    nchunk = T // CHUNK

    def body(a_ref, b_ref, c_ref, wc_ref, wp_ref, ps_ref, lng_ref, wsp_ref, bias_ref, o_ref):
        j = pl.program_id(0)

        @pl.when(j < 3)
        def _conv():
            z = c_ref[...] * a_ref[...]
            w = wc_ref[...]
            y = w[0:1] * _shift_down(z, 2) + w[1:2] * _shift_down(z, 1) + w[2:3] * z
            o_ref[...] = (b_ref[...] * y).astype(o_ref.dtype)

        @pl.when((j >= 3) & (j < 5))
        def _pool():
            d, _ = _pool_mean_minus_token(a_ref[...], j == 3)
            y = jnp.dot(d.astype(BF16), wp_ref[...].astype(BF16), preferred_element_type=F32)
            o_ref[...] = (y * ps_ref[...]).astype(o_ref.dtype)

        @pl.when(j >= 5)
        def _sgu():
            lo = _lo_mask((CHUNK, LANES))
            wm = jnp.where(_tril_keep(), wsp_ref[...], 0.0).astype(BF16)
            bias_t = bias_ref[...]
            g = lng_ref[...]

            def chunk(n, carry):
                rows = pl.ds(pl.multiple_of(n * CHUNK, CHUNK), CHUNK)
                ug, _, _, _, mixed = _sgu_chunk_fwd(a_ref[rows, :], b_ref[rows, :], g, wm, bias_t, lo)
                o_ref[rows, :] = (ug * mixed).astype(o_ref.dtype)
                return carry

            lax.fori_loop(0, nchunk, chunk, 0, unroll=SGU_UNROLL)

    def col(f):
        return lambda j: (0, f(j))

    clip = lambda v, lo, hi: jnp.minimum(jnp.maximum(v, lo), hi)
    return pl.pallas_call(
        body,
        name="mixer_fwd",
        grid=(8,),
        in_specs=[
            pl.BlockSpec((T, LANES), col(lambda j: jnp.where(j < 3, j, jnp.where(j < 5, j + 6, j + 6)))),
            pl.BlockSpec((T, LANES), col(lambda j: jnp.where(j < 3, j + 3, jnp.where(j < 5, 5, j + 9)))),
            pl.BlockSpec((T, LANES), col(lambda j: jnp.where(j < 3, j + 6, 8))),
            pl.BlockSpec((None, 3, LANES), lambda j: (layer, 0, clip(j, 0, 2))),
            pl.BlockSpec((None, None, LANES, LANES), lambda j: (layer, clip(j - 3, 0, 1), 0, 0)),
            pl.BlockSpec((None, 1, LANES), lambda j: (layer, 0, clip(j - 3, 0, 1))),
            pl.BlockSpec((None, 1, LANES), lambda j: (layer, 0, clip(j - 5, 0, 2))),
            pl.BlockSpec((None, None, 2 * CHUNK, CHUNK), lambda j: (layer, clip(j - 5, 0, 2), 0, 0)),
            pl.BlockSpec((None, None, CHUNK, LANES), lambda j: (layer, clip(j - 5, 0, 2), 0, 0)),
        ],
        out_specs=pl.BlockSpec((T, LANES), lambda j: (0, j)),
        out_shape=jax.ShapeDtypeStruct((T, D_MODEL), BF16),
        compiler_params=_cparams(("arbitrary",)),
    )(proj, proj, proj, wconv, wpool_bd, pscale, lng, wsp, bias)


def _mixer_bwd(proj, dmix, wconv, wpool_bd, pscale, lng, wsp, bias, layer, deps=()):
    T = proj.shape[0]
    nchunk = T // CHUNK

    def body(*refs):
        a_ref, b_ref, c_ref, dm_ref, wc_ref, wp_ref, ps_ref, lng_ref, wsp_ref, bias_ref = refs[:10]
        o_ref, dwc_ref, dwp_ref, dps_ref, dlng_ref, dwsp_ref, dbias_ref, keep1, keep2 = refs[10 + len(deps):]
        k = pl.program_id(0)

        @pl.when(k < 3)
        def _conv():
            xa, gb, gc, dya = a_ref[...], b_ref[...], c_ref[...], dm_ref[...]
            w = wc_ref[...]
            z = gc * xa
            z1 = _shift_down(z, 1)
            z2 = _shift_down(z, 2)
            y = w[0:1] * z2 + w[1:2] * z1 + w[2:3] * z
            dyv = dya * gb
            dz = w[2:3] * dyv + w[1:2] * _shift_up(dyv, 1) + w[0:1] * _shift_up(dyv, 2)
            dwc_ref[0:1, :] = jnp.sum(dyv * z2, axis=0, keepdims=True)
            dwc_ref[1:2, :] = jnp.sum(dyv * z1, axis=0, keepdims=True)
            dwc_ref[2:3, :] = jnp.sum(dyv * z, axis=0, keepdims=True)
            o_ref[...] = (dz * gc).astype(o_ref.dtype)
            keep1[k] = (dya * y).astype(keep1.dtype)
            keep1[k + 3] = (dz * xa).astype(keep1.dtype)

        @pl.when((k >= 3) & (k < 9))
        def _emit_gb_gc():
            o_ref[...] = keep1[k - 3]

        @pl.when((k >= 9) & (k < 11))
        def _pool():
            first = k == 9
            p, dyb = a_ref[...], dm_ref[...]
            d, count = _pool_mean_minus_token(p, first)
            w2 = wp_ref[...].astype(BF16)
            db = d.astype(BF16)
            y = jnp.dot(db, w2, preferred_element_type=F32)
            dps_ref[...] = jnp.sum(dyb * y, axis=0, keepdims=True)
            dyv = (dyb * ps_ref[...]).astype(BF16)
            dd = lax.dot_general(dyv, w2, _DN["nt"], preferred_element_type=F32)
            dwp_ref[...] = lax.dot_general(db, dyv, _DN["tn"], preferred_element_type=F32)
            dwin = dd / count
            a2 = dwin + _shift_up(dwin, 1)
            a4 = a2 + _shift_up(a2, 2)
            a8 = a4 + _shift_up(a4, 4)
            a16 = a8 + _shift_up(a8, 8)
            _, lo = _pool_windows(first)
            back = jnp.where(first, jnp.where(lo, a2, a4), jnp.where(lo, a8, a16))
            o_ref[...] = (back - dd).astype(o_ref.dtype)

        @pl.when((k >= 11) & (k < 14))
        def _sgu():
            lo = _lo_mask((CHUNK, LANES))
            keep = _tril_keep()
            wm = jnp.where(keep, wsp_ref[...], 0.0).astype(BF16)
            bias_t = bias_ref[...]
            g = lng_ref[...]
            dwsp_ref[...] = jnp.zeros_like(dwsp_ref)
            dbias_ref[...] = jnp.zeros_like(dbias_ref)
            dlng_ref[...] = jnp.zeros_like(dlng_ref)

            def chunk(n, carry):
                rows = pl.ds(pl.multiple_of(n * CHUNK, CHUNK), CHUNK)
                u, v, dyc = a_ref[rows, :], b_ref[rows, :], dm_ref[rows, :]
                ug, vn, rstd, vh, mixed = _sgu_chunk_fwd(u, v, g, wm, bias_t, lo)
                dmx = dyc * ug
                o_ref[rows, :] = (dyc * mixed * _gelu_grad(u)).astype(o_ref.dtype)
                dbias_ref[...] += dmx
                dst = jnp.concatenate([jnp.where(lo, dmx, 0.0), jnp.where(lo, 0.0, dmx)], axis=0).astype(BF16)
                dwsp_ref[...] += lax.dot_general(dst, vh, _DN["nt"], preferred_element_type=F32)
                dvh = lax.dot_general(wm, dst, _DN["tn"], preferred_element_type=F32)
                dlng_ref[...] += jnp.sum(dvh * vn, axis=0, keepdims=True)
                dvn = dvh * g
                m1 = _seg_mean(dvn, lo)
                m2 = _seg_mean(dvn * vn, lo)
                dvg = rstd * (dvn - m1 - vn * m2)
                keep2[k - 11, rows, :] = (dvg * _gelu_grad(v)).astype(keep2.dtype)
                return carry

            lax.fori_loop(0, nchunk, chunk, 0, unroll=SGU_UNROLL)
            dwsp_ref[...] = jnp.where(keep, dwsp_ref[...], 0.0)
            dbt = dbias_ref[...]
            lane = lax.broadcasted_iota(jnp.int32, (CHUNK, LANES), 1)
            sa = jnp.sum(jnp.where(lo, dbt, 0.0), axis=-1, keepdims=True)
            sb = jnp.sum(jnp.where(lo, 0.0, dbt), axis=-1, keepdims=True)
            dbias_ref[...] = jnp.where(lane == 0, sa, jnp.where(lane == 1, sb, 0.0))

        @pl.when(k >= 14)
        def _emit_v():
            o_ref[...] = keep2[k - 14]

    def col(f):
        return lambda k: (0, f(k))

    clip = lambda v, lo, hi: jnp.minimum(jnp.maximum(v, lo), hi)
    view_a = lambda k: jnp.where(k < 3, k, jnp.where(k < 9, 2, jnp.where(k < 14, k, 13)))
    view_b = lambda k: jnp.where(k < 3, k + 3, jnp.where(k < 11, 5, jnp.where(k < 14, k + 3, 16)))
    view_c = lambda k: jnp.where(k < 3, k + 6, 8)
    view_dm = lambda k: jnp.where(k < 3, k, jnp.where(k < 9, 2, jnp.where(k < 14, k - 6, 7)))
    return pl.pallas_call(
        body,
        name="mixer_bwd",
        grid=(17,),
        in_specs=[
            pl.BlockSpec((T, LANES), col(view_a)),
            pl.BlockSpec((T, LANES), col(view_b)),
            pl.BlockSpec((T, LANES), col(view_c)),
            pl.BlockSpec((T, LANES), col(view_dm)),
            pl.BlockSpec((None, 3, LANES), lambda k: (layer, 0, clip(k, 0, 2))),
            pl.BlockSpec((None, None, LANES, LANES), lambda k: (layer, clip(k - 9, 0, 1), 0, 0)),
            pl.BlockSpec((None, 1, LANES), lambda k: (layer, 0, clip(k - 9, 0, 1))),
            pl.BlockSpec((None, 1, LANES), lambda k: (layer, 0, clip(k - 11, 0, 2))),
            pl.BlockSpec((None, None, 2 * CHUNK, CHUNK), lambda k: (layer, clip(k - 11, 0, 2), 0, 0)),
            pl.BlockSpec((None, None, CHUNK, LANES), lambda k: (layer, clip(k - 11, 0, 2), 0, 0)),
        ] + [pl.BlockSpec(memory_space=pl.ANY)] * len(deps),
        out_specs=[
            pl.BlockSpec((T, LANES), lambda k: (0, k)),
            pl.BlockSpec((3, LANES), col(lambda k: clip(k, 0, 2))),
            pl.BlockSpec((None, LANES, LANES), lambda k: (clip(k - 9, 0, 1), 0, 0)),
            pl.BlockSpec((1, LANES), col(lambda k: clip(k - 9, 0, 1))),
            pl.BlockSpec((1, LANES), col(lambda k: clip(k - 11, 0, 2))),
            pl.BlockSpec((None, 2 * CHUNK, CHUNK), lambda k: (clip(k - 11, 0, 2), 0, 0)),
            pl.BlockSpec((None, CHUNK, LANES), lambda k: (clip(k - 11, 0, 2), 0, 0)),
        ],
        out_shape=[
            jax.ShapeDtypeStruct((T, IN_W), BF16),
            jax.ShapeDtypeStruct((3, CONV_W), F32),
            jax.ShapeDtypeStruct((2, LANES, LANES), F32),
            jax.ShapeDtypeStruct((1, POOL_W), F32),
            jax.ShapeDtypeStruct((1, SGU_W), F32),
            jax.ShapeDtypeStruct((3, 2 * CHUNK, CHUNK), F32),
            jax.ShapeDtypeStruct((3, CHUNK, LANES), F32),
        ],
        scratch_shapes=[pltpu.VMEM((6, T, LANES), BF16), pltpu.VMEM((3, T, LANES), BF16)],
        compiler_params=_cparams(("arbitrary",)),
    )(proj, proj, proj, dmix, wconv, wpool_bd, pscale, lng, wsp, bias, *deps)


def _loss_ln_bwd(xhat, rstd, g, b, target, tm=256):
    T = xhat.shape[0]

    def body(xhat_ref, rstd_ref, g_ref, b_ref, t_ref, loss_ref, dr_ref, drb_ref, dg_ref, db_ref):
        xhat_v = xhat_ref[...]
        err = xhat_v * g_ref[...] + b_ref[...] - t_ref[...]
        dy = err * (1.0 / D_MODEL)

        @pl.when(pl.program_id(0) == 0)
        def _():
            loss_ref[...] = jnp.zeros_like(loss_ref)
            dg_ref[...] = jnp.zeros_like(dg_ref)
            db_ref[...] = jnp.zeros_like(db_ref)

        part = jnp.sum(jnp.sum(err * err, axis=-1, keepdims=True), axis=0, keepdims=True)
        loss_ref[...] += jnp.broadcast_to(part * (0.5 / D_MODEL), loss_ref.shape)
        dg_ref[...] += jnp.sum(dy * xhat_v, axis=0, keepdims=True)
        db_ref[...] += jnp.sum(dy, axis=0, keepdims=True)
        dxh = dy * g_ref[...]
        m1 = jnp.mean(dxh, axis=-1, keepdims=True)
        m2 = jnp.mean(dxh * xhat_v, axis=-1, keepdims=True)
        dr = rstd_ref[...] * (dxh - m1 - xhat_v * m2)
        dr_ref[...] = dr
        drb_ref[...] = dr.astype(drb_ref.dtype)

    row = pl.BlockSpec((tm, D_MODEL), lambda i: (i, 0))
    vec = pl.BlockSpec((1, D_MODEL), lambda i: (0, 0))
    (g_arr, g_spec), (b_arr, b_spec) = _vec(g), _vec(b)
    return pl.pallas_call(
        body,
        name="loss_ln_bwd",
        grid=(T // tm,),
        in_specs=[row, pl.BlockSpec((tm, 1), lambda i: (i, 0)), g_spec, b_spec, row],
        out_specs=[pl.BlockSpec((8, LANES), lambda i: (0, 0)), row, row, vec, vec],
        out_shape=[jax.ShapeDtypeStruct((8, LANES), F32),
                   jax.ShapeDtypeStruct((T, D_MODEL), F32), jax.ShapeDtypeStruct((T, D_MODEL), BF16),
                   jax.ShapeDtypeStruct((1, D_MODEL), F32), jax.ShapeDtypeStruct((1, D_MODEL), F32)],
        compiler_params=_cparams(("arbitrary",)),
    )(xhat, rstd, g_arr, b_arr, target)


def _adamw(w, g, m, v, tr):
    R, C = w.shape[-2:]
    assert R % tr == 0
    c1 = 1.0 - ADAM_B1 ** ADAM_STEP
    c2 = 1.0 - ADAM_B2 ** ADAM_STEP

    def body(w_ref, g_ref, m_ref, v_ref, d_ref, mo_ref, vo_ref):
        gv = g_ref[...]
        mn = ADAM_B1 * m_ref[...] + (1.0 - ADAM_B1) * gv
        vn = ADAM_B2 * v_ref[...] + (1.0 - ADAM_B2) * (gv * gv)
        d_ref[...] = -ADAM_LR * ((mn / c1) / (jnp.sqrt(vn / c2) + ADAM_EPS) + ADAM_WD * w_ref[...])
        mo_ref[...] = mn
        vo_ref[...] = vn

    if w.ndim == 2:
        grid, blk = (R // tr,), pl.BlockSpec((tr, C), lambda i: (i, 0))
    else:
        grid, blk = (w.shape[0], R // tr), pl.BlockSpec((None, tr, C), lambda l, i: (l, i, 0))
    return pl.pallas_call(
        body, name="adamw", grid=grid, in_specs=[blk] * 4, out_specs=[blk] * 3,
        out_shape=[jax.ShapeDtypeStruct(w.shape, F32)] * 3, compiler_params=_cparams(("parallel",) * len(grid)),
    )(w, g, m, v)


def _my_place():
    return lax.axis_index("x"), lax.axis_index("y"), lax.axis_index("c")


ANY = pl.BlockSpec(memory_space=pl.ANY)
HBM = pl.BlockSpec(memory_space=pltpu.HBM)
SEM = pl.BlockSpec(memory_space=pltpu.SEMAPHORE)
EFFECT = pltpu.SideEffectType.DATAFLOW_SIDE_EFFECTING


def _in_hbm(a):
    return pltpu.with_memory_space_constraint(a, pltpu.HBM)


def _block_rows(ref, dev):
    r = ref.shape[0] // N_DEV
    start = pl.multiple_of((4 * dev[0] + 2 * dev[1] + dev[2]) * r, 16)
    return ref.at[pl.ds(start, r), :]


def _ag_first_copies(s_refs, land_refs, send_sems, recv_sems, receiving):
    x, y, c = _my_place()
    peers = [(x, y, 1 - c)] + [(*chip, c) for chip in _other_chips(x, y)]
    copies = []
    for k, peer in enumerate(peers):
        block = peer if receiving else (x, y, c)
        copies += [pltpu.make_async_remote_copy(
            src_ref=s_refs[w], dst_ref=_block_rows(land_refs[w], block),
            send_sem=send_sems.at[k * len(s_refs) + w], recv_sem=recv_sems.at[k * len(s_refs) + w],
            device_id=peer, device_id_type=MESH)
            for w in range(len(s_refs))]
    return copies


def _ag_start(shards, layer, after=()):
    nw = len(shards)

    def body(*refs):
        s_refs, land_refs = refs[:nw], refs[nw:2 * nw]
        token = refs[-1]
        sems = 2 * nw + len(after)
        for cp in _ag_first_copies(s_refs, land_refs, refs[sems], refs[sems + 1], False):
            cp.start()
        token[...] = jnp.zeros_like(token)

    lands = [lax.empty((N_DEV * s.shape[0], D_MODEL), BF16) for s in shards]
    out = pl.pallas_call(
        body, name="ag_start_%s" % layer,
        in_specs=[HBM] * (2 * nw) + [ANY] * len(after),
        out_specs=(SEM, SEM, *[HBM] * (2 * nw), pl.BlockSpec(memory_space=pltpu.VMEM)),
        out_shape=(pltpu.SemaphoreType.DMA((4 * nw,)), pltpu.SemaphoreType.DMA((4 * nw,)),
                   *[pltpu.HBM(a.shape, a.dtype) for a in list(shards) + lands],
                   jax.ShapeDtypeStruct((8, LANES), F32)),
        input_output_aliases={i: 2 + i for i in range(2 * nw)},
        compiler_params=pltpu.CompilerParams(has_side_effects=EFFECT),
    )(*[_in_hbm(a) for a in list(shards) + lands], *after)
    return out[0], out[1], out[2:2 + nw], out[2 + nw:2 + 2 * nw], out[-1]


def _ag_wait(send_sems, recv_sems, shards, lands, after, layer):
    nw = len(shards)

    def body(*refs):
        s_refs, land_refs = refs[:nw], refs[nw:2 * nw]
        for cp in _ag_first_copies(s_refs, land_refs, refs[2 * nw], refs[2 * nw + 1], True):
            cp.wait_send()
            cp.wait_recv()

    out = pl.pallas_call(
        body, name="ag_wait_%s" % layer,
        in_specs=[HBM] * (2 * nw) + [SEM, SEM] + [ANY] * len(after),
        out_specs=[HBM] * (2 * nw),
        out_shape=[pltpu.HBM(a.shape, a.dtype) for a in list(shards) + list(lands)],
        input_output_aliases={i: i for i in range(2 * nw)},
        compiler_params=pltpu.CompilerParams(has_side_effects=EFFECT),
    )(*shards, *lands, send_sems, recv_sems, *after)
    return out[:nw], out[nw:]


def _ag_pass_on(shards, lands):
    nw = len(shards)

    def body(*refs):
        s_refs, g_refs = refs[:nw], refs[2 * nw:3 * nw]
        send_sems, recv_sems, local_sems = refs[3 * nw:3 * nw + 3]
        stage = refs[3 * nw + 3:]
        x, y, c = _my_place()
        load = [pltpu.make_async_copy(s_refs[w], stage[w], local_sems.at[w]) for w in range(nw)]
        mine = [pltpu.make_async_copy(stage[w], _block_rows(g_refs[w], (x, y, c)), local_sems.at[w])
                for w in range(nw)]
        for cp in load:
            cp.start()
        sends, arrivals = [], []
        for j, chip in enumerate(_other_chips(x, y)):
            for w in range(nw):
                rows_out = _block_rows(g_refs[w], (*chip, c))
                rows_in = _block_rows(g_refs[w], (*chip, 1 - c))
                sends.append(pltpu.make_async_remote_copy(
                    src_ref=rows_out, dst_ref=rows_out, send_sem=send_sems.at[j, w], recv_sem=recv_sems.at[j, w],
                    device_id=(x, y, 1 - c), device_id_type=MESH))
                arrivals.append(pltpu.make_async_remote_copy(
                    src_ref=rows_in, dst_ref=rows_in, send_sem=send_sems.at[j, w], recv_sem=recv_sems.at[j, w],
                    device_id=(x, y, 1 - c), device_id_type=MESH))
        for cp in sends:
            cp.start()
        for w in range(nw):
            load[w].wait()
            mine[w].start()
        for cp in arrivals:
            cp.wait_recv()
        for cp in sends:
            cp.wait_send()
        for cp in mine:
            cp.wait()

    return pl.pallas_call(
        body, name="ag_pass_on",
        in_specs=[ANY] * (2 * nw), out_specs=[ANY] * nw,
        out_shape=[jax.ShapeDtypeStruct(a.shape, a.dtype) for a in lands],
        input_output_aliases={nw + i: i for i in range(nw)},
        scratch_shapes=[pltpu.SemaphoreType.DMA((3, nw)), pltpu.SemaphoreType.DMA((3, nw)),
                        pltpu.SemaphoreType.DMA((nw,))] + [pltpu.VMEM(s.shape, s.dtype) for s in shards],
        compiler_params=_cparams(),
    )(*shards, *lands)


def _rs_sibling_copies(p_refs, land_refs, send_sems, recv_sems):
    x, y, c = _my_place()
    return [pltpu.make_async_remote_copy(
        src_ref=p_refs[w].at[:, 1 - c], dst_ref=land_refs[w],
        send_sem=send_sems.at[w], recv_sem=recv_sems.at[w], device_id=(x, y, 1 - c), device_id_type=MESH)
        for w in range(len(p_refs))]


def _rs_sibling_start(parts, tag, after=()):
    nw = len(parts)
    sems = 2 * nw + len(after)

    def body(*refs):
        for cp in _rs_sibling_copies(refs[:nw], refs[nw:2 * nw], refs[sems], refs[sems + 1]):
            cp.start()
        refs[-1][...] = jnp.zeros_like(refs[-1])

    lands = [lax.empty(p.shape[:1] + p.shape[2:], BF16) for p in parts]
    out = pl.pallas_call(
        body, name="rs_sibling_start_%s" % tag,
        in_specs=[HBM] * (2 * nw) + [ANY] * len(after),
        out_specs=(SEM, SEM, *[HBM] * (2 * nw), pl.BlockSpec(memory_space=pltpu.VMEM)),
        out_shape=(pltpu.SemaphoreType.DMA((nw,)), pltpu.SemaphoreType.DMA((nw,)),
                   *[pltpu.HBM(a.shape, a.dtype) for a in list(parts) + lands],
                   jax.ShapeDtypeStruct((8, LANES), F32)),
        input_output_aliases={i: 2 + i for i in range(2 * nw)},
        compiler_params=pltpu.CompilerParams(has_side_effects=EFFECT),
    )(*[_in_hbm(a) for a in list(parts) + lands], *after)
    return out[0], out[1], out[2:2 + nw], out[2 + nw:2 + 2 * nw], out[-1]


def _rs_sibling_wait(send_sems, recv_sems, parts, lands, after, tag):
    nw = len(parts)

    def body(*refs):
        for cp in _rs_sibling_copies(refs[:nw], refs[nw:2 * nw], refs[2 * nw], refs[2 * nw + 1]):
            cp.wait_send()
            cp.wait_recv()

    out = pl.pallas_call(
        body, name="rs_sibling_wait_%s" % tag,
        in_specs=[HBM] * (2 * nw) + [SEM, SEM] + [ANY] * len(after),
        out_specs=[HBM] * (2 * nw),
        out_shape=[pltpu.HBM(a.shape, a.dtype) for a in list(parts) + list(lands)],
        input_output_aliases={i: i for i in range(2 * nw)},
        compiler_params=pltpu.CompilerParams(has_side_effects=EFFECT),
    )(*parts, *lands, send_sems, recv_sems, *after)
    return out[:nw], out[nw:]


def _rs_chip_sum(parts, gots, c):
    n = len(parts)

    def body(c_ref, *refs):
        for p_ref, g_ref, o_ref in zip(refs[:n], refs[n:2 * n], refs[2 * n:]):
            o_ref[...] = (p_ref[...].astype(F32) + g_ref[...].astype(F32)).astype(o_ref.dtype)

    mine = [pl.BlockSpec((None, None, p.shape[2], D_MODEL), lambda q, c_ref: (q, c_ref[0], 0, 0)) for p in parts]
    theirs = [pl.BlockSpec((None, g.shape[1], D_MODEL), lambda q, c_ref: (q, 0, 0)) for g in gots]
    return pl.pallas_call(
        body, name="rs_chip_sum",
        grid_spec=pltpu.PrefetchScalarGridSpec(
            num_scalar_prefetch=1, grid=(4,), in_specs=mine + theirs, out_specs=theirs),
        out_shape=[jax.ShapeDtypeStruct(g.shape, BF16) for g in gots],
        compiler_params=_cparams(("parallel",)),
    )(c, *parts, *gots)


def _other_chips(x, y):
    return [(1 - x, y), (x, 1 - y), (1 - x, 1 - y)]


def _rs_chip_copies(s_refs, land_refs, send_sems, recv_sems):
    x, y, c = _my_place()
    copies = []
    for k, chip in enumerate(_other_chips(x, y)):
        q = 2 * chip[0] + chip[1]
        copies += [pltpu.make_async_remote_copy(
            src_ref=s_refs[w].at[q], dst_ref=land_refs[w].at[k],
            send_sem=send_sems.at[k * len(s_refs) + w], recv_sem=recv_sems.at[k * len(s_refs) + w],
            device_id=(*chip, c), device_id_type=MESH)
            for w in range(len(s_refs))]
    return copies


def _rs_chip_start(sums, layer):
    nw = len(sums)

    def body(*refs):
        s_refs, land_refs = refs[:nw], refs[nw:2 * nw]
        send_sems, recv_sems = refs[2 * nw], refs[2 * nw + 1]
        token = refs[-1]
        for cp in _rs_chip_copies(s_refs, land_refs, send_sems, recv_sems):
            cp.start()
        token[...] = jnp.zeros_like(token)

    lands = [lax.empty((3,) + s.shape[1:], BF16) for s in sums]
    out = pl.pallas_call(
        body, name="rs_chip_start_%s" % layer,
        in_specs=[HBM] * (2 * nw),
        out_specs=(SEM, SEM, *[HBM] * (2 * nw), pl.BlockSpec(memory_space=pltpu.VMEM)),
        out_shape=(pltpu.SemaphoreType.DMA((3 * nw,)), pltpu.SemaphoreType.DMA((3 * nw,)),
                   *[pltpu.HBM(a.shape, a.dtype) for a in list(sums) + lands],
                   jax.ShapeDtypeStruct((8, LANES), F32)),
        input_output_aliases={i: 2 + i for i in range(2 * nw)},
        compiler_params=pltpu.CompilerParams(has_side_effects=EFFECT),
    )(*[_in_hbm(a) for a in list(sums) + lands])
    return out[0], out[1], out[2:2 + nw], out[2 + nw:2 + 2 * nw], out[-1]


def _rs_chip_wait(send_sems, recv_sems, sums, lands, after, layer):
    nw = len(sums)

    def body(*refs):
        s_refs, land_refs = refs[:nw], refs[nw:2 * nw]
        for cp in _rs_chip_copies(s_refs, land_refs, refs[2 * nw], refs[2 * nw + 1]):
            cp.wait_send()
            cp.wait_recv()

    out = pl.pallas_call(
        body, name="rs_chip_wait_%s" % layer,
        in_specs=[HBM] * (2 * nw) + [SEM, SEM] + [ANY] * len(after),
        out_specs=[HBM] * (2 * nw),
        out_shape=[pltpu.HBM(a.shape, a.dtype) for a in list(sums) + list(lands)],
        input_output_aliases={i: i for i in range(2 * nw)},
        compiler_params=pltpu.CompilerParams(has_side_effects=EFFECT),
    )(*sums, *lands, send_sems, recv_sems, *after)
    return out[:nw], out[nw:]


def _rs_finish(sums, gots, q, layer, into):
    n = len(sums)

    def body(q_ref, *refs):
        for s_ref, g_ref, o_ref in zip(refs[:n], refs[n:2 * n], refs[len(refs) - n:]):
            o_ref[...] = ((s_ref[...].astype(F32) + g_ref[0].astype(F32)) + g_ref[1].astype(F32)) + g_ref[2].astype(F32)

    rows = [s.shape[1] for s in sums]
    in_specs = [pl.BlockSpec((None, r, D_MODEL), lambda i, q_ref: (q_ref[0], 0, 0)) for r in rows]
    in_specs += [pl.BlockSpec((3, r, D_MODEL), lambda i, q_ref: (0, 0, 0)) for r in rows]
    args = [q, *sums, *gots]
    aliases = {}
    if into is not None:
        in_specs += [ANY] * n
        aliases = {len(args) + i: i for i in range(n)}
        args += list(into)
    return pl.pallas_call(
        body, name="rs_finish",
        grid_spec=pltpu.PrefetchScalarGridSpec(
            num_scalar_prefetch=1, grid=(1,), in_specs=in_specs,
            out_specs=[pl.BlockSpec((None, r, D_MODEL), lambda i, q_ref: (layer, 0, 0)) for r in rows]),
        out_shape=[jax.ShapeDtypeStruct((DEPTH, r, D_MODEL), F32) for r in rows],
        input_output_aliases=aliases,
        compiler_params=_cparams(("arbitrary",)),
    )(*args)


def _allreduce_small(vec, deps=()):
    R = vec.shape[0]
    assert R % (8 * N_DEV) == 0
    P = R // N_DEV
    nd = len(deps)

    def body(*refs):
        v_ref = refs[0]
        o_ref, buf, send1, recv1, send2, recv2 = refs[1 + nd:]
        x, y, c = _my_place()
        me = 4 * x + 2 * y + c

        def piece(ref, d):
            return ref.at[pl.ds(pl.multiple_of(d * P, 8), P), :]

        def peer(k):
            p = me ^ k
            return p, (p >> 2, (p >> 1) & 1, p & 1)

        scatter = []
        for k in range(1, N_DEV):
            p, where = peer(k)
            scatter.append(pltpu.make_async_remote_copy(
                src_ref=piece(v_ref, p), dst_ref=buf.at[k], send_sem=send1.at[k - 1], recv_sem=recv1.at[k - 1],
                device_id=where, device_id_type=MESH))
        for cp in scatter:
            cp.start()
        buf[0] = piece(v_ref, me)[...]
        for cp in scatter:
            cp.wait()
        acc = buf[me]
        for d in range(1, N_DEV):
            acc = acc + buf[me ^ d]
        piece(o_ref, me)[...] = acc
        spread, arrivals = [], []
        for k in range(1, N_DEV):
            p, where = peer(k)
            spread.append(pltpu.make_async_remote_copy(
                src_ref=piece(o_ref, me), dst_ref=piece(o_ref, me), send_sem=send2.at[k - 1], recv_sem=recv2.at[k - 1],
                device_id=where, device_id_type=MESH))
            arrivals.append(pltpu.make_async_remote_copy(
                src_ref=piece(o_ref, p), dst_ref=piece(o_ref, p), send_sem=send2.at[k - 1], recv_sem=recv2.at[k - 1],
                device_id=where, device_id_type=MESH))
        for cp in spread:
            cp.start()
        for cp in arrivals:
            cp.wait_recv()
        for cp in spread:
            cp.wait_send()

    sems = pltpu.SemaphoreType.DMA((N_DEV - 1,))
    return pl.pallas_call(
        body, name="allreduce_small",
        in_specs=[pl.BlockSpec(memory_space=pltpu.VMEM)] + [ANY] * nd, out_specs=pl.BlockSpec(memory_space=pltpu.VMEM),
        out_shape=jax.ShapeDtypeStruct((R, LANES), F32),
        scratch_shapes=[pltpu.VMEM((N_DEV, P, LANES), F32), sems, sems, sems, sems],
        compiler_params=_cparams(),
    )(vec, *deps)


def _pack(arrs):
    flat = jnp.concatenate([a.reshape(-1) for a in arrs])
    pad = (-flat.shape[0]) % (8 * N_DEV * LANES)
    return jnp.pad(flat, (0, pad)).reshape(-1, LANES)


def _unpack(packed, shapes):
    flat = packed.reshape(-1)
    out, off = [], 0
    for s in shapes:
        n = math.prod(s)
        out.append(flat[off:off + n].reshape(s))
        off += n
    return out


def kernel(x, w_in, w_conv, w_pool, pool_scale, sgu_ln_g, w_spatial, b_spatial, w_o, ln1_g, ln1_b, w_gate_up, w_down, ln2_g, ln2_b, loss_target, m_w_in, m_w_conv, m_w_pool, m_pool_scale, m_sgu_ln_g, m_w_spatial, m_b_spatial, m_w_o, m_ln1_g, m_ln1_b, m_w_gate_up, m_w_down, m_ln2_g, m_ln2_b, v_w_in, v_w_conv, v_w_pool, v_pool_scale, v_sgu_ln_g, v_w_spatial, v_b_spatial, v_w_o, v_ln1_g, v_ln1_b, v_w_gate_up, v_w_down, v_ln2_g, v_ln2_b):
    L = DEPTH
    T = x.shape[1]
    mx, my, mc = _my_place()
    dev = 4 * mx + 2 * my + mc
    xs = x[0]
    target = loss_target[0]

    conv_cols = w_conv.shape[2]
    w_conv_z = lax.dynamic_update_slice(jnp.zeros((L, 3, CONV_W), F32), w_conv, (0, 0, dev * conv_cols))
    w_conv_packed = _allreduce_small(_pack([w_conv_z]))
    w_conv_full = _unpack(w_conv_packed, [(L, 3, CONV_W)])[0]

    shards = (jnp.swapaxes(w_in, 1, 2).astype(BF16), jnp.swapaxes(w_gate_up, 1, 2).astype(BF16),
              w_o.astype(BF16), w_down.astype(BF16))
    first_gather = _ag_start_layer(shards, 0, [w_conv_packed])

    loss_tile, grad_x2, big_grads, small_grads = _local_step(
        xs, target, shards, first_gather, w_conv_full, w_pool, pool_scale, sgu_ln_g, w_spatial, b_spatial,
        ln1_g, ln1_b, ln2_g, ln2_b)
    grad_x = grad_x2[None]
    big_w = (w_in, w_gate_up, w_o, w_down)
    big_m = (m_w_in, m_w_gate_up, m_w_o, m_w_down)
    big_v = (v_w_in, v_w_gate_up, v_w_o, v_w_down)
    small_w = [w_conv_full, w_pool, pool_scale, sgu_ln_g, w_spatial, b_spatial, ln1_g, ln1_b, ln2_g, ln2_b]
    small_m = [m_w_conv, m_w_pool, m_pool_scale, m_sgu_ln_g, m_w_spatial, m_b_spatial, m_ln1_g, m_ln1_b, m_ln2_g, m_ln2_b]
    small_v = [v_w_conv, v_w_pool, v_pool_scale, v_sgu_ln_g, v_w_spatial, v_b_spatial, v_ln1_g, v_ln1_b, v_ln2_g, v_ln2_b]
    loss, grads, deltas, new_m, new_v = _reduce_and_update(
        big_grads, small_grads, big_w, big_m, big_v, small_w, small_m, small_v)
    return (loss, grad_x, *grads, *deltas, *new_m, *new_v)


def _ag_start_layer(shards, l, after):
    s_in, s_gu, s_o, s_dn = [s[l] for s in shards]
    first = _ag_start([s_in, s_o], "%da" % l, after=after)
    return first, _ag_start([s_gu, s_dn], "%db" % l, after=[first[4]])


def _ag_finish(gather, after, tag):
    send_sems, recv_sems, shards, lands, _ = gather
    shards, lands = _ag_wait(send_sems, recv_sems, shards, lands, after, tag)
    return _ag_pass_on(shards, lands)


def _rs_begin(parts, tag, after=()):
    return _rs_sibling_start([p.reshape(4, 2, p.shape[0] // N_DEV, D_MODEL) for p in parts], tag, after)


def _rs_continue(sibling_flight, after, c_arr, tag):
    send_sems, recv_sems, parts, lands, _ = sibling_flight
    parts, got = _rs_sibling_wait(send_sems, recv_sems, parts, lands, after, tag)
    return _rs_chip_start(_rs_chip_sum(parts, got, c_arr), tag)


def _local_step(xs, target, shards, gather, w_conv_full, w_pool, pool_scale, sgu_ln_g, w_spatial, b_spatial,
                ln1_g, ln1_b, ln2_g, ln2_b):
    L = DEPTH
    T = xs.shape[0]
    mx, my, mc = _my_place()
    c_arr = jnp.reshape(mc, (1,)).astype(jnp.int32)
    q_arr = jnp.reshape(2 * mx + my, (1,)).astype(jnp.int32)
    eye2 = jnp.eye(2, dtype=F32)
    wp = w_pool.reshape(L, 2, 2, HALF, HALF)
    wpool_bd = jnp.einsum("ltgcd,gh->ltgchd", wp, eye2).reshape(L, 2, LANES, LANES)
    wsp_t = w_spatial.reshape(L, 3, 2 * CHUNK, CHUNK)
    bias_t = jnp.repeat(jnp.swapaxes(b_spatial.reshape(L, 3, 2, CHUNK), 2, 3), HALF, axis=3)
    mixer_w = (w_conv_full, wpool_bd, pool_scale[:, None, :], sgu_ln_g[:, None, :], wsp_t, bias_t)
    g1, b1, g2, b2 = [a[:, None, :] for a in (ln1_g, ln1_b, ln2_g, ln2_b)]
    one, zero = jnp.ones((1, 1, D_MODEL), F32), jnp.zeros((1, 1, D_MODEL), F32)

    saved = []
    prev, pg, pb = xs, (one, 0), (zero, 0)
    prev_b = xs.astype(BF16)
    weights = []
    for l in range(L):
        g_in, g_o = _ag_finish(gather[0], [] if l == 0 else [prev_b], "%da" % l)
        proj = _mm(prev_b, g_in, "nt", F32, 512, IN_W, "mm_proj", deps=[gather[1][4]] if l == 0 else [])
        mixcat = _mixer_fwd(proj, *mixer_w, l)
        xhat1, rstd1, h_b = _mm_ln_fwd(mixcat, g_o, prev, pg, pb, (g1, l), (b1, l), "mm_wo_ln")
        g_gu, g_dn = _ag_finish(gather[1], [h_b], "%db" % l)
        weights.append((g_in, g_gu, g_o, g_dn))
        deps = []
        if l + 1 < L:
            gather = _ag_start_layer(shards, l + 1, [g_gu])
            deps = [gather[1][4]]
        g_act, u_act, act = _mm_swiglu_fwd(h_b, g_gu, deps=deps)
        xhat2, rstd2, y_b = _mm_ln_fwd(act, g_dn, xhat1, (g1, l), (b1, l), (g2, l), (b2, l), "mm_down_ln")
        saved.append((prev_b, proj, mixcat, xhat1, rstd1, h_b, g_act, u_act, act, xhat2, rstd2))
        prev, pg, pb, prev_b = xhat2, (g2, l), (b2, l), y_b


    small = [None] * L
    big = None
    sibling_flight = None
    above = None
    for l in reversed(range(L)):
        prev_b, proj, mixcat, xhat1, rstd1, h_b, g_act, u_act, act, xhat2, rstd2 = saved[l]
        g_in, g_gu, g_o, g_dn = weights[l]
        chip_flight = None
        if above is None:
            loss_tile, dr2, dr2_b, dg2, db2 = _loss_ln_bwd(xhat2, rstd2, (g2, l), (b2, l), target)
        else:
            dr2, dr2_b, dg2, db2 = _mm_ln_bwd([above[0]], above[1], above[2], xhat2, rstd2, (g2, l),
                                              "mm_dx_ln", deps=[sibling_flight[4]])
            chip_flight = _rs_continue(sibling_flight, [dr2_b], c_arr, str(l + 1))
        dg_b, du_b = _mm_swiglu_bwd(dr2_b, g_dn, g_act, u_act, deps=[chip_flight[4]] if chip_flight else [])
        p_dn = _mm(act, dr2_b, "tn", BF16, DW_TM, D_MODEL, "mm_dw_down")
        p_gu = _mm_tn_pair(dg_b, du_b, h_b, DW_TM, "mm_dw_gate_up")
        ffn_sibling = _rs_begin([p_gu, p_dn], "0b") if l == 0 else None
        dr1, dr1_b, dg1, db1, dmix = _mm_ln_bwd([dg_b, du_b], g_gu, dr2, xhat1, rstd1, (g1, l), "mm_dh_ln",
                                                deps=[ffn_sibling[4]] if l == 0 else [], w_back=g_o)
        ffn_flight = _rs_continue(ffn_sibling, [dr1_b], c_arr, "0b") if l == 0 else None
        p_o = _mm(mixcat, dr1_b, "tn", BF16, 512, D_MODEL, "mm_dw_o")
        dproj, dwc, dwp, dps, dlng, dwsp, dbias = _mixer_bwd(proj, dmix, *mixer_w, l,
                                                             deps=[ffn_flight[4]] if l == 0 else [])
        p_in = _mm(dproj, prev_b, "tn", BF16, IN_W, D_MODEL, "mm_dw_in")
        small[l] = (dwc, dwp, dps, dlng, dwsp, dbias, dg1, db1, dg2, db2)
        above = (dproj, g_in, dr1)
        if chip_flight is not None:
            big = list(_rs_chip_finish(chip_flight, [p_in], q_arr, str(l + 1), l + 1, big))
        if l > 0:
            sibling_flight = _rs_begin([p_in, p_gu, p_o, p_dn], str(l))
        else:
            big[1], big[3] = _rs_chip_finish(ffn_flight, [p_in, p_o], q_arr, "0b", 0, [big[1], big[3]])

    def stack(i):
        return jnp.stack([small[l][i] for l in range(L)])

    dwp_bd = stack(1).reshape(L, 2, 2, HALF, 2, HALF)
    dwp_all = jnp.einsum("ltgchd,gh->ltgcd", dwp_bd, eye2).reshape(L, 4, HALF, HALF)
    dbs_all = jnp.swapaxes(stack(5)[:, :, :, :2], 2, 3).reshape(L, 6, CHUNK)
    small_grads = [stack(0), dwp_all, stack(2).reshape(L, POOL_W), stack(3).reshape(L, SGU_W),
                   stack(4).reshape(L, 6, CHUNK, CHUNK), dbs_all] + [stack(i).reshape(L, D_MODEL) for i in (6, 7, 8, 9)]
    small_grads.append(loss_tile[0, :1])
    packed_small = _allreduce_small(_pack(small_grads), deps=[big[1]])
    sibling_flight = _rs_begin([p_in, p_o], "0a", after=[packed_small])
    grad_x = _mm_ln_bwd([above[0]], above[1], above[2], None, None, None, "mm_dx_out", deps=[sibling_flight[4]])
    last_flight = _rs_continue(sibling_flight, [grad_x], c_arr, "0a")
    return loss_tile, grad_x, (big, last_flight, q_arr), (packed_small, [a.shape for a in small_grads])


def _rs_chip_finish(in_flight, after, q, tag, layer, into):
    send_sems, recv_sems, sums, lands, _ = in_flight
    sums, got = _rs_chip_wait(send_sems, recv_sems, sums, lands, after, tag)
    return _rs_finish(sums, got, q, layer, into)


def _reduce_and_update(big_grads, small_grads, big_w, big_m, big_v, small_w, small_m, small_v):
    L = DEPTH
    mx, my, mc = _my_place()
    dev = 4 * mx + 2 * my + mc
    conv_cols = CONV_W // N_DEV
    w_in, w_gate_up, w_o, w_down = big_w
    m_w_in, m_w_gate_up, m_w_o, m_w_down = big_m
    v_w_in, v_w_gate_up, v_w_o, v_w_down = big_v
    packed_g, small_shapes = small_grads
    big, last_flight, q_arr = big_grads

    def widen_conv(a):
        return lax.dynamic_update_slice(jnp.zeros((L, 3, CONV_W), F32), a, (0, 0, dev * conv_cols))

    small_m = [widen_conv(small_m[0])] + list(small_m[1:])
    small_v = [widen_conv(small_v[0])] + list(small_v[1:])
    pk_d, pk_m, pk_v = _adamw(_pack(small_w), packed_g, _pack(small_m), _pack(small_v), packed_g.shape[0] // 2)
    sg = _unpack(packed_g, small_shapes)
    sd = _unpack(pk_d, small_shapes)
    sm = _unpack(pk_m, small_shapes)
    sv = _unpack(pk_v, small_shapes)

    def conv_cols_of(a):
        return lax.dynamic_slice(a, (0, 0, dev * conv_cols), (L, 3, conv_cols))

    for lst in (sg, sd, sm, sv):
        lst[0] = conv_cols_of(lst[0])

    tr = lambda a: jnp.swapaxes(a, 1, 2)
    gt_gu, g_w_dn = big[1], big[3]
    d_gu, m_gu, v_gu = [tr(a) for a in _adamw(tr(w_gate_up), gt_gu, tr(m_w_gate_up), tr(v_w_gate_up), gt_gu.shape[1] // 2)]
    d_dn, m_dn, v_dn = _adamw(w_down, g_w_dn, m_w_down, v_w_down, 352)
    gt_in, g_w_o = _rs_chip_finish(last_flight, [d_gu, d_dn, pk_d], q_arr, "0a", 0, [big[0], big[2]])
    d_in, m_in, v_in = [tr(a) for a in _adamw(tr(w_in), gt_in, tr(m_w_in), tr(v_w_in), gt_in.shape[1])]
    d_o, m_o, v_o = _adamw(w_o, g_w_o, m_w_o, v_w_o, 128)
    g_w_in, g_w_gu = tr(gt_in), tr(gt_gu)

    def ordered(big_in, big_o, big_gu, big_dn, sm_list):
        return [big_in, sm_list[0], sm_list[1], sm_list[2], sm_list[3], sm_list[4], sm_list[5], big_o,
                sm_list[6], sm_list[7], big_gu, big_dn, sm_list[8], sm_list[9]]

    grads = ordered(g_w_in, g_w_o, g_w_gu, g_w_dn, sg)
    deltas = ordered(d_in, d_o, d_gu, d_dn, sd)
    new_m = ordered(m_in, m_o, m_gu, m_dn, sm)
    new_v = ordered(v_in, v_o, v_gu, v_dn, sv)
    return sg[10][0], grads, deltas, new_m, new_v
```

```python
import math

import jax
import jax.numpy as jnp
from jax import lax
from jax.experimental import pallas as pl
from jax.experimental.pallas import tpu as pltpu

F32 = jnp.float32
BF16 = jnp.bfloat16
MESH = pl.DeviceIdType.MESH

D_MODEL = 1024
DEPTH = 4
CONV_W = 384
POOL_W = 256
SGU_W = 384
IN_W = 3 * CONV_W + POOL_W + 2 * SGU_W
D_FF = 2816
CHUNK = 128
ALPHA = float((2 * DEPTH) ** 0.25)
LN_EPS = 1e-5
ADAM_LR, ADAM_B1, ADAM_B2, ADAM_EPS, ADAM_WD, ADAM_STEP = 0.001, 0.9, 0.999, 1e-08, 0.01, 10

N_DEV = 8
LANES = 128
HALF = 64
VMEM_LIMIT = 52 * 1024 * 1024

INV_SQRT2 = 0.7071067811865476
INV_SQRT_2PI = 0.3989422804014327


def _cparams(sem=None, **kw):
    if sem is not None:
        kw["dimension_semantics"] = sem
    return pltpu.CompilerParams(vmem_limit_bytes=VMEM_LIMIT, **kw)


_DN = {"nt": (((1,), (1,)), ((), ())), "tn": (((0,), (0,)), ((), ()))}


def _mm(a, b, mode, out_dtype, tm, tn, name, deps=()):
    if mode == "nt":
        (M, K), N = a.shape, b.shape[0]
        a_spec = pl.BlockSpec((tm, K), lambda i, j: (i, 0))
        b_spec = pl.BlockSpec((tn, K), lambda i, j: (j, 0))
    else:
        (K, M), N = a.shape, b.shape[1]
        a_spec = pl.BlockSpec((K, tm), lambda i, j: (0, i))
        b_spec = pl.BlockSpec((K, tn), lambda i, j: (0, j))
    assert M % tm == 0 and N % tn == 0, (M, N, K, tm, tn)
    nd = len(deps)

    def body(*refs):
        a_ref, b_ref, o_ref = refs[0], refs[1], refs[2 + nd]
        o_ref[...] = lax.dot_general(a_ref[...], b_ref[...], _DN[mode], preferred_element_type=F32).astype(o_ref.dtype)

    return pl.pallas_call(
        body,
        name=name,
        grid=(M // tm, N // tn),
        in_specs=[a_spec, b_spec] + [pl.BlockSpec(memory_space=pl.ANY)] * nd,
        out_specs=pl.BlockSpec((tm, tn), lambda i, j: (i, j)),
        out_shape=jax.ShapeDtypeStruct((M, N), out_dtype),
        compiler_params=_cparams(("parallel", "parallel")),
    )(a, b, *deps)


def _mm_tn_pair(a1, a2, b, tm, name):
    K, M = a1.shape
    N = b.shape[1]
    n1 = M // tm

    def body(a1_ref, a2_ref, b_ref, o_ref):
        i = pl.program_id(0)

        @pl.when(i < n1)
        def _():
            o_ref[...] = lax.dot_general(a1_ref[...], b_ref[...], _DN["tn"], preferred_element_type=F32).astype(o_ref.dtype)

        @pl.when(i >= n1)
        def _():
            o_ref[...] = lax.dot_general(a2_ref[...], b_ref[...], _DN["tn"], preferred_element_type=F32).astype(o_ref.dtype)

    return pl.pallas_call(
        body, name=name, grid=(2 * n1,),
        in_specs=[pl.BlockSpec((K, tm), lambda i: (0, jnp.minimum(i, n1 - 1))),
                  pl.BlockSpec((K, tm), lambda i: (0, jnp.maximum(i - n1, 0))),
                  pl.BlockSpec((K, N), lambda i: (0, 0))],
        out_specs=pl.BlockSpec((tm, N), lambda i: (i, 0)),
        out_shape=jax.ShapeDtypeStruct((2 * M, N), BF16),
        compiler_params=_cparams(("arbitrary",)),
    )(a1, a2, b)


LN_SUB = 256
LN_TM = 512


def _vec(v):
    arr, layer = v
    return arr, pl.BlockSpec((None, 1, D_MODEL), lambda *_: (layer, 0, 0))


def _mm_ln_fwd(a, b, prev, pg, pb, g, bias, name):
    T, K = a.shape
    tm = LN_TM

    def body(a_ref, b_ref, prev_ref, pg_ref, pb_ref, g_ref, bias_ref, xhat_ref, rstd_ref, y_ref):
        for s in range(tm // LN_SUB):
            rows = slice(s * LN_SUB, (s + 1) * LN_SUB)
            mm = jnp.dot(a_ref[rows, :], b_ref[...], preferred_element_type=F32)
            r = ALPHA * (prev_ref[rows, :] * pg_ref[...] + pb_ref[...]) + mm
            mu = jnp.mean(r, axis=-1, keepdims=True)
            xc = r - mu
            var = jnp.mean(xc * xc, axis=-1, keepdims=True)
            rstd = lax.rsqrt(var + LN_EPS)
            xhat = xc * rstd
            xhat_ref[rows, :] = xhat
            rstd_ref[rows, :] = rstd
            y_ref[rows, :] = (xhat * g_ref[...] + bias_ref[...]).astype(y_ref.dtype)

    row = pl.BlockSpec((tm, D_MODEL), lambda i: (i, 0))
    vecs = [_vec(v) for v in (pg, pb, g, bias)]
    return pl.pallas_call(
        body, name=name, grid=(T // tm,),
        in_specs=[pl.BlockSpec((tm, K), lambda i: (i, 0)),
                  pl.BlockSpec((K, D_MODEL), lambda i: (0, 0), pipeline_mode=pl.Buffered(1)),
                  row] + [s for _, s in vecs],
        out_specs=[row, pl.BlockSpec((tm, 1), lambda i: (i, 0)), row],
        out_shape=[jax.ShapeDtypeStruct((T, D_MODEL), F32), jax.ShapeDtypeStruct((T, 1), F32),
                   jax.ShapeDtypeStruct((T, D_MODEL), BF16)],
        compiler_params=_cparams(("parallel",)),
    )(a, b, prev, *[a_ for a_, _ in vecs])


def _mm_ln_bwd(a_list, b, dres, xhat, rstd, g, name, deps=(), w_back=None):
    T = a_list[0].shape[0]
    tm = LN_TM
    na, nd = len(a_list), len(deps)
    ks = [a.shape[1] for a in a_list]
    last = xhat is None
    nout = 1 if last else (5 if w_back is not None else 4)

    def body(*refs):
        a_refs, b_ref, dres_ref = refs[:na], refs[na], refs[na + 1]
        if not last:
            xhat_ref, rstd_ref, g_ref = refs[na + 2:na + 5]
            dr_ref, drb_ref, dg_ref, db_ref = refs[len(refs) - nout:len(refs) - nout + 4]

            @pl.when(pl.program_id(0) == 0)
            def _():
                dg_ref[...] = jnp.zeros_like(dg_ref)
                db_ref[...] = jnp.zeros_like(db_ref)

        for s in range(tm // LN_SUB):
            rows = slice(s * LN_SUB, (s + 1) * LN_SUB)
            mm, off = None, 0
            for a_ref, k in zip(a_refs, ks):
                part = jnp.dot(a_ref[rows, :], b_ref[off:off + k, :], preferred_element_type=F32)
                mm = part if mm is None else mm + part
                off += k
            dy = ALPHA * dres_ref[rows, :] + mm
            if last:
                refs[-1][rows, :] = dy
                continue
            xhat_v = xhat_ref[rows, :]
            dg_ref[...] += jnp.sum(dy * xhat_v, axis=0, keepdims=True)
            db_ref[...] += jnp.sum(dy, axis=0, keepdims=True)
            dxh = dy * g_ref[...]
            m1 = jnp.mean(dxh, axis=-1, keepdims=True)
            m2 = jnp.mean(dxh * xhat_v, axis=-1, keepdims=True)
            dr = rstd_ref[rows, :] * (dxh - m1 - xhat_v * m2)
            dr_ref[rows, :] = dr
            dr_b = dr.astype(drb_ref.dtype)
            drb_ref[rows, :] = dr_b
            if w_back is not None:
                refs[-1][rows, :] = lax.dot_general(dr_b, refs[na + 5][...], _DN["nt"], preferred_element_type=F32)

    row = pl.BlockSpec((tm, D_MODEL), lambda i: (i, 0))
    vec = pl.BlockSpec((1, D_MODEL), lambda i: (0, 0))
    in_specs = [pl.BlockSpec((tm, k), lambda i: (i, 0)) for k in ks]
    in_specs += [pl.BlockSpec((sum(ks), D_MODEL), lambda i: (0, 0), pipeline_mode=pl.Buffered(1)), row]
    args = list(a_list) + [b, dres]
    if last:
        out_specs, out_shape = row, jax.ShapeDtypeStruct((T, D_MODEL), F32)
    else:
        g_arr, g_spec = _vec(g)
        in_specs += [row, pl.BlockSpec((tm, 1), lambda i: (i, 0)), g_spec]
        args += [xhat, rstd, g_arr]
        out_specs = [row, row, vec, vec]
        out_shape = [jax.ShapeDtypeStruct((T, D_MODEL), F32), jax.ShapeDtypeStruct((T, D_MODEL), BF16),
                     jax.ShapeDtypeStruct((1, D_MODEL), F32), jax.ShapeDtypeStruct((1, D_MODEL), F32)]
        if w_back is not None:
            in_specs.append(pl.BlockSpec(w_back.shape, lambda i: (0, 0), pipeline_mode=pl.Buffered(1)))
            args.append(w_back)
            out_specs.append(row)
            out_shape.append(jax.ShapeDtypeStruct((T, w_back.shape[0]), F32))
    return pl.pallas_call(
        body, name=name, grid=(T // tm,),
        in_specs=in_specs + [pl.BlockSpec(memory_space=pl.ANY)] * nd,
        out_specs=out_specs, out_shape=out_shape,
        compiler_params=_cparams(("parallel",) if last else ("arbitrary",)),
    )(*args, *deps)


DW_TM = 1408
FF_TN = 256
FF_TM = 2048
SAVED_GU = BF16


def _mm_swiglu_fwd(h, w_gu, deps=()):
    T = h.shape[0]
    tm = min(T, FF_TM)
    nj = D_FF // FF_TN
    nd = len(deps)

    def body(*refs):
        h_ref, wg_ref, wu_ref = refs[:3]
        g_ref, u_ref, act_ref = refs[3 + nd:]
        hv = h_ref[...]
        gv = lax.dot_general(hv, wg_ref[...], _DN["nt"], preferred_element_type=F32)
        uv = lax.dot_general(hv, wu_ref[...], _DN["nt"], preferred_element_type=F32)
        g_ref[...] = gv.astype(g_ref.dtype)
        u_ref[...] = uv.astype(u_ref.dtype)
        act_ref[...] = (gv * jax.nn.sigmoid(gv) * uv).astype(act_ref.dtype)

    tile = pl.BlockSpec((tm, FF_TN), lambda j, i: (i, j))
    return pl.pallas_call(
        body, name="mm_gate_up_swiglu", grid=(nj, T // tm),
        in_specs=[pl.BlockSpec((tm, D_MODEL), lambda j, i: (i, 0)),
                  pl.BlockSpec((FF_TN, D_MODEL), lambda j, i: (j, 0)),
                  pl.BlockSpec((FF_TN, D_MODEL), lambda j, i: (j + nj, 0))] + [pl.BlockSpec(memory_space=pl.ANY)] * nd,
        out_specs=[tile, tile, tile],
        out_shape=[jax.ShapeDtypeStruct((T, D_FF), SAVED_GU), jax.ShapeDtypeStruct((T, D_FF), SAVED_GU),
                   jax.ShapeDtypeStruct((T, D_FF), BF16)],
        compiler_params=_cparams(("parallel", "parallel")),
    )(h, w_gu, w_gu, *deps)


def _mm_swiglu_bwd(dr, w_dn, g, u, deps=()):
    T = dr.shape[0]
    tm = min(T, FF_TM)

    def body(*refs):
        dr_ref, w_ref, g_ref, u_ref = refs[:4]
        dg_ref, du_ref = refs[-2:]
        da = lax.dot_general(dr_ref[...], w_ref[...], _DN["nt"], preferred_element_type=F32)
        gv, uv = g_ref[...].astype(F32), u_ref[...].astype(F32)
        s = jax.nn.sigmoid(gv)
        du_ref[...] = (da * (gv * s)).astype(du_ref.dtype)
        dg_ref[...] = (da * uv * (s * (1.0 + gv * (1.0 - s)))).astype(dg_ref.dtype)

    tile = pl.BlockSpec((tm, FF_TN), lambda j, i: (i, j))
    return pl.pallas_call(
        body, name="mm_dact_swiglu", grid=(D_FF // FF_TN, T // tm),
        in_specs=[pl.BlockSpec((tm, D_MODEL), lambda j, i: (i, 0)), pl.BlockSpec((FF_TN, D_MODEL), lambda j, i: (j, 0)),
                  tile, tile] + [ANY] * len(deps),
        out_specs=[tile, tile],
        out_shape=[jax.ShapeDtypeStruct((T, D_FF), BF16)] * 2,
        compiler_params=_cparams(("parallel", "parallel")),
    )(dr, w_dn, g, u, *deps)


def _gelu(x):
    return 0.5 * x * (1.0 + lax.erf(x * INV_SQRT2))


def _gelu_grad(x):
    return 0.5 * (1.0 + lax.erf(x * INV_SQRT2)) + x * (jnp.exp(-0.5 * x * x) * INV_SQRT_2PI)


def _shift_down(z, k):
    row = lax.broadcasted_iota(jnp.int32, z.shape, 0)
    return jnp.where(row >= k, pltpu.roll(z, k, 0), 0.0)


def _shift_up(z, k):
    n = z.shape[0]
    row = lax.broadcasted_iota(jnp.int32, z.shape, 0)
    return jnp.where(row < n - k, pltpu.roll(z, n - k, 0), 0.0)


def _lo_mask(shape):
    return lax.broadcasted_iota(jnp.int32, shape, len(shape) - 1) < HALF


def _seg_mean(x, lo):
    a = jnp.sum(jnp.where(lo, x, 0.0), axis=-1, keepdims=True)
    b = jnp.sum(jnp.where(lo, 0.0, x), axis=-1, keepdims=True)
    return jnp.where(lo, a, b) * (1.0 / HALF)


def _pool_windows(first):
    lo = _lo_mask((1, LANES))
    return jnp.where(first, jnp.where(lo, 2.0, 4.0), jnp.where(lo, 8.0, 16.0)), lo


def _pool_mean_minus_token(p, first):
    wl, lo = _pool_windows(first)
    s2 = p + _shift_down(p, 1)
    s4 = s2 + _shift_down(s2, 2)
    s8 = s4 + _shift_down(s4, 4)
    s16 = s8 + _shift_down(s8, 8)
    win = jnp.where(first, jnp.where(lo, s2, s4), jnp.where(lo, s8, s16))
    t1 = (lax.broadcasted_iota(jnp.int32, p.shape, 0) + 1).astype(F32)
    count = jnp.minimum(t1, wl)
    return win / count - p, count


SGU_UNROLL = 4


def _tril_keep():
    r = lax.broadcasted_iota(jnp.int32, (2 * CHUNK, CHUNK), 0)
    s = lax.broadcasted_iota(jnp.int32, (2 * CHUNK, CHUNK), 1)
    return s <= (r & (CHUNK - 1))


def _sgu_chunk_fwd(u, v, g, wm, bias, lo):
    ug = _gelu(u)
    vg = _gelu(v)
    mu = _seg_mean(vg, lo)
    xc = vg - mu
    var = _seg_mean(xc * xc, lo)
    rstd = lax.rsqrt(var + LN_EPS)
    vn = xc * rstd
    vh = (vn * g).astype(BF16)
    mm2 = jnp.dot(wm, vh, preferred_element_type=F32)
    mixed = jnp.where(lo, mm2[:CHUNK], mm2[CHUNK:]) + bias
    return ug, vn, rstd, vh, mixed


def _mixer_fwd(proj, wconv, wpool_bd, pscale, lng, wsp, bias, layer):
    T = proj.shape[0]
    nchunk = T // CHUNK

    def body(a_ref, b_ref, c_ref, wc_ref, wp_ref, ps_ref, lng_ref, wsp_ref, bias_ref, o_ref):
        j = pl.program_id(0)

        @pl.when(j < 3)
        def _conv():
            z = c_ref[...] * a_ref[...]
            w = wc_ref[...]
            y = w[0:1] * _shift_down(z, 2) + w[1:2] * _shift_down(z, 1) + w[2:3] * z
            o_ref[...] = (b_ref[...] * y).astype(o_ref.dtype)

        @pl.when((j >= 3) & (j < 5))
        def _pool():
            d, _ = _pool_mean_minus_token(a_ref[...], j == 3)
            y = jnp.dot(d.astype(BF16), wp_ref[...].astype(BF16), preferred_element_type=F32)
            o_ref[...] = (y * ps_ref[...]).astype(o_ref.dtype)

        @pl.when(j >= 5)
        def _sgu():
            lo = _lo_mask((CHUNK, LANES))
            wm = jnp.where(_tril_keep(), wsp_ref[...], 0.0).astype(BF16)
            bias_t = bias_ref[...]
            g = lng_ref[...]

            def chunk(n, carry):
                rows = pl.ds(pl.multiple_of(n * CHUNK, CHUNK), CHUNK)
                ug, _, _, _, mixed = _sgu_chunk_fwd(a_ref[rows, :], b_ref[rows, :], g, wm, bias_t, lo)
                o_ref[rows, :] = (ug * mixed).astype(o_ref.dtype)
                return carry

            lax.fori_loop(0, nchunk, chunk, 0, unroll=SGU_UNROLL)

    def col(f):
        return lambda j: (0, f(j))

    clip = lambda v, lo, hi: jnp.minimum(jnp.maximum(v, lo), hi)
    return pl.pallas_call(
        body,
        name="mixer_fwd",
        grid=(8,),
        in_specs=[
            pl.BlockSpec((T, LANES), col(lambda j: jnp.where(j < 3, j, jnp.where(j < 5, j + 6, j + 6)))),
            pl.BlockSpec((T, LANES), col(lambda j: jnp.where(j < 3, j + 3, jnp.where(j < 5, 5, j + 9)))),
            pl.BlockSpec((T, LANES), col(lambda j: jnp.where(j < 3, j + 6, 8))),
            pl.BlockSpec((None, 3, LANES), lambda j: (layer, 0, clip(j, 0, 2))),
            pl.BlockSpec((None, None, LANES, LANES), lambda j: (layer, clip(j - 3, 0, 1), 0, 0)),
            pl.BlockSpec((None, 1, LANES), lambda j: (layer, 0, clip(j - 3, 0, 1))),
            pl.BlockSpec((None, 1, LANES), lambda j: (layer, 0, clip(j - 5, 0, 2))),
            pl.BlockSpec((None, None, 2 * CHUNK, CHUNK), lambda j: (layer, clip(j - 5, 0, 2), 0, 0)),
            pl.BlockSpec((None, None, CHUNK, LANES), lambda j: (layer, clip(j - 5, 0, 2), 0, 0)),
        ],
        out_specs=pl.BlockSpec((T, LANES), lambda j: (0, j)),
        out_shape=jax.ShapeDtypeStruct((T, D_MODEL), BF16),
        compiler_params=_cparams(("arbitrary",)),
    )(proj, proj, proj, wconv, wpool_bd, pscale, lng, wsp, bias)


def _mixer_bwd(proj, dmix, wconv, wpool_bd, pscale, lng, wsp, bias, layer, deps=()):
    T = proj.shape[0]
    nchunk = T // CHUNK

    def body(*refs):
        a_ref, b_ref, c_ref, dm_ref, wc_ref, wp_ref, ps_ref, lng_ref, wsp_ref, bias_ref = refs[:10]
        o_ref, dwc_ref, dwp_ref, dps_ref, dlng_ref, dwsp_ref, dbias_ref, keep1, keep2 = refs[10 + len(deps):]
        k = pl.program_id(0)

        @pl.when(k < 3)
        def _conv():
            xa, gb, gc, dya = a_ref[...], b_ref[...], c_ref[...], dm_ref[...]
            w = wc_ref[...]
            z = gc * xa
            z1 = _shift_down(z, 1)
            z2 = _shift_down(z, 2)
            y = w[0:1] * z2 + w[1:2] * z1 + w[2:3] * z
            dyv = dya * gb
            dz = w[2:3] * dyv + w[1:2] * _shift_up(dyv, 1) + w[0:1] * _shift_up(dyv, 2)
            dwc_ref[0:1, :] = jnp.sum(dyv * z2, axis=0, keepdims=True)
            dwc_ref[1:2, :] = jnp.sum(dyv * z1, axis=0, keepdims=True)
            dwc_ref[2:3, :] = jnp.sum(dyv * z, axis=0, keepdims=True)
            o_ref[...] = (dz * gc).astype(o_ref.dtype)
            keep1[k] = (dya * y).astype(keep1.dtype)
            keep1[k + 3] = (dz * xa).astype(keep1.dtype)

        @pl.when((k >= 3) & (k < 9))
        def _emit_gb_gc():
            o_ref[...] = keep1[k - 3]

        @pl.when((k >= 9) & (k < 11))
        def _pool():
            first = k == 9
            p, dyb = a_ref[...], dm_ref[...]
            d, count = _pool_mean_minus_token(p, first)
            w2 = wp_ref[...].astype(BF16)
            db = d.astype(BF16)
            y = jnp.dot(db, w2, preferred_element_type=F32)
            dps_ref[...] = jnp.sum(dyb * y, axis=0, keepdims=True)
            dyv = (dyb * ps_ref[...]).astype(BF16)
            dd = lax.dot_general(dyv, w2, _DN["nt"], preferred_element_type=F32)
            dwp_ref[...] = lax.dot_general(db, dyv, _DN["tn"], preferred_element_type=F32)
            dwin = dd / count
            a2 = dwin + _shift_up(dwin, 1)
            a4 = a2 + _shift_up(a2, 2)
            a8 = a4 + _shift_up(a4, 4)
            a16 = a8 + _shift_up(a8, 8)
            _, lo = _pool_windows(first)
            back = jnp.where(first, jnp.where(lo, a2, a4), jnp.where(lo, a8, a16))
            o_ref[...] = (back - dd).astype(o_ref.dtype)

        @pl.when((k >= 11) & (k < 14))
        def _sgu():
            lo = _lo_mask((CHUNK, LANES))
            keep = _tril_keep()
            wm = jnp.where(keep, wsp_ref[...], 0.0).astype(BF16)
            bias_t = bias_ref[...]
            g = lng_ref[...]
            dwsp_ref[...] = jnp.zeros_like(dwsp_ref)
            dbias_ref[...] = jnp.zeros_like(dbias_ref)
            dlng_ref[...] = jnp.zeros_like(dlng_ref)

            def chunk(n, carry):
                rows = pl.ds(pl.multiple_of(n * CHUNK, CHUNK), CHUNK)
                u, v, dyc = a_ref[rows, :], b_ref[rows, :], dm_ref[rows, :]
                ug, vn, rstd, vh, mixed = _sgu_chunk_fwd(u, v, g, wm, bias_t, lo)
                dmx = dyc * ug
                o_ref[rows, :] = (dyc * mixed * _gelu_grad(u)).astype(o_ref.dtype)
                dbias_ref[...] += dmx
                dst = jnp.concatenate([jnp.where(lo, dmx, 0.0), jnp.where(lo, 0.0, dmx)], axis=0).astype(BF16)
                dwsp_ref[...] += lax.dot_general(dst, vh, _DN["nt"], preferred_element_type=F32)
                dvh = lax.dot_general(wm, dst, _DN["tn"], preferred_element_type=F32)
                dlng_ref[...] += jnp.sum(dvh * vn, axis=0, keepdims=True)
                dvn = dvh * g
                m1 = _seg_mean(dvn, lo)
                m2 = _seg_mean(dvn * vn, lo)
                dvg = rstd * (dvn - m1 - vn * m2)
                keep2[k - 11, rows, :] = (dvg * _gelu_grad(v)).astype(keep2.dtype)
                return carry

            lax.fori_loop(0, nchunk, chunk, 0, unroll=SGU_UNROLL)
            dwsp_ref[...] = jnp.where(keep, dwsp_ref[...], 0.0)
            dbt = dbias_ref[...]
            lane = lax.broadcasted_iota(jnp.int32, (CHUNK, LANES), 1)
            sa = jnp.sum(jnp.where(lo, dbt, 0.0), axis=-1, keepdims=True)
            sb = jnp.sum(jnp.where(lo, 0.0, dbt), axis=-1, keepdims=True)
            dbias_ref[...] = jnp.where(lane == 0, sa, jnp.where(lane == 1, sb, 0.0))

        @pl.when(k >= 14)
        def _emit_v():
            o_ref[...] = keep2[k - 14]

    def col(f):
        return lambda k: (0, f(k))

    clip = lambda v, lo, hi: jnp.minimum(jnp.maximum(v, lo), hi)
    view_a = lambda k: jnp.where(k < 3, k, jnp.where(k < 9, 2, jnp.where(k < 14, k, 13)))
    view_b = lambda k: jnp.where(k < 3, k + 3, jnp.where(k < 11, 5, jnp.where(k < 14, k + 3, 16)))
    view_c = lambda k: jnp.where(k < 3, k + 6, 8)
    view_dm = lambda k: jnp.where(k < 3, k, jnp.where(k < 9, 2, jnp.where(k < 14, k - 6, 7)))
    return pl.pallas_call(
        body,
        name="mixer_bwd",
        grid=(17,),
        in_specs=[
            pl.BlockSpec((T, LANES), col(view_a)),
            pl.BlockSpec((T, LANES), col(view_b)),
            pl.BlockSpec((T, LANES), col(view_c)),
            pl.BlockSpec((T, LANES), col(view_dm)),
            pl.BlockSpec((None, 3, LANES), lambda k: (layer, 0, clip(k, 0, 2))),
            pl.BlockSpec((None, None, LANES, LANES), lambda k: (layer, clip(k - 9, 0, 1), 0, 0)),
            pl.BlockSpec((None, 1, LANES), lambda k: (layer, 0, clip(k - 9, 0, 1))),
            pl.BlockSpec((None, 1, LANES), lambda k: (layer, 0, clip(k - 11, 0, 2))),
            pl.BlockSpec((None, None, 2 * CHUNK, CHUNK), lambda k: (layer, clip(k - 11, 0, 2), 0, 0)),
            pl.BlockSpec((None, None, CHUNK, LANES), lambda k: (layer, clip(k - 11, 0, 2), 0, 0)),
        ] + [pl.BlockSpec(memory_space=pl.ANY)] * len(deps),
        out_specs=[
            pl.BlockSpec((T, LANES), lambda k: (0, k)),
            pl.BlockSpec((3, LANES), col(lambda k: clip(k, 0, 2))),
            pl.BlockSpec((None, LANES, LANES), lambda k: (clip(k - 9, 0, 1), 0, 0)),
            pl.BlockSpec((1, LANES), col(lambda k: clip(k - 9, 0, 1))),
            pl.BlockSpec((1, LANES), col(lambda k: clip(k - 11, 0, 2))),
            pl.BlockSpec((None, 2 * CHUNK, CHUNK), lambda k: (clip(k - 11, 0, 2), 0, 0)),
            pl.BlockSpec((None, CHUNK, LANES), lambda k: (clip(k - 11, 0, 2), 0, 0)),
        ],
        out_shape=[
            jax.ShapeDtypeStruct((T, IN_W), BF16),
            jax.ShapeDtypeStruct((3, CONV_W), F32),
            jax.ShapeDtypeStruct((2, LANES, LANES), F32),
            jax.ShapeDtypeStruct((1, POOL_W), F32),
            jax.ShapeDtypeStruct((1, SGU_W), F32),
            jax.ShapeDtypeStruct((3, 2 * CHUNK, CHUNK), F32),
            jax.ShapeDtypeStruct((3, CHUNK, LANES), F32),
        ],
        scratch_shapes=[pltpu.VMEM((6, T, LANES), BF16), pltpu.VMEM((3, T, LANES), BF16)],
        compiler_params=_cparams(("arbitrary",)),
    )(proj, proj, proj, dmix, wconv, wpool_bd, pscale, lng, wsp, bias, *deps)


def _loss_ln_bwd(xhat, rstd, g, b, target, tm=256):
    T = xhat.shape[0]

    def body(xhat_ref, rstd_ref, g_ref, b_ref, t_ref, loss_ref, dr_ref, drb_ref, dg_ref, db_ref):
        xhat_v = xhat_ref[...]
        err = xhat_v * g_ref[...] + b_ref[...] - t_ref[...]
        dy = err * (1.0 / D_MODEL)

        @pl.when(pl.program_id(0) == 0)
        def _():
            loss_ref[...] = jnp.zeros_like(loss_ref)
            dg_ref[...] = jnp.zeros_like(dg_ref)
            db_ref[...] = jnp.zeros_like(db_ref)

        part = jnp.sum(jnp.sum(err * err, axis=-1, keepdims=True), axis=0, keepdims=True)
        loss_ref[...] += jnp.broadcast_to(part * (0.5 / D_MODEL), loss_ref.shape)
        dg_ref[...] += jnp.sum(dy * xhat_v, axis=0, keepdims=True)
        db_ref[...] += jnp.sum(dy, axis=0, keepdims=True)
        dxh = dy * g_ref[...]
        m1 = jnp.mean(dxh, axis=-1, keepdims=True)
        m2 = jnp.mean(dxh * xhat_v, axis=-1, keepdims=True)
        dr = rstd_ref[...] * (dxh - m1 - xhat_v * m2)
        dr_ref[...] = dr
        drb_ref[...] = dr.astype(drb_ref.dtype)

    row = pl.BlockSpec((tm, D_MODEL), lambda i: (i, 0))
    vec = pl.BlockSpec((1, D_MODEL), lambda i: (0, 0))
    (g_arr, g_spec), (b_arr, b_spec) = _vec(g), _vec(b)
    return pl.pallas_call(
        body,
        name="loss_ln_bwd",
        grid=(T // tm,),
        in_specs=[row, pl.BlockSpec((tm, 1), lambda i: (i, 0)), g_spec, b_spec, row],
        out_specs=[pl.BlockSpec((8, LANES), lambda i: (0, 0)), row, row, vec, vec],
        out_shape=[jax.ShapeDtypeStruct((8, LANES), F32),
                   jax.ShapeDtypeStruct((T, D_MODEL), F32), jax.ShapeDtypeStruct((T, D_MODEL), BF16),
                   jax.ShapeDtypeStruct((1, D_MODEL), F32), jax.ShapeDtypeStruct((1, D_MODEL), F32)],
        compiler_params=_cparams(("arbitrary",)),
    )(xhat, rstd, g_arr, b_arr, target)


def _adamw(w, g, m, v, tr):
    R, C = w.shape[-2:]
    assert R % tr == 0
    c1 = 1.0 - ADAM_B1 ** ADAM_STEP
    c2 = 1.0 - ADAM_B2 ** ADAM_STEP

    def body(w_ref, g_ref, m_ref, v_ref, d_ref, mo_ref, vo_ref):
        gv = g_ref[...]
        mn = ADAM_B1 * m_ref[...] + (1.0 - ADAM_B1) * gv
        vn = ADAM_B2 * v_ref[...] + (1.0 - ADAM_B2) * (gv * gv)
        d_ref[...] = -ADAM_LR * ((mn / c1) / (jnp.sqrt(vn / c2) + ADAM_EPS) + ADAM_WD * w_ref[...])
        mo_ref[...] = mn
        vo_ref[...] = vn

    if w.ndim == 2:
        grid, blk = (R // tr,), pl.BlockSpec((tr, C), lambda i: (i, 0))
    else:
        grid, blk = (w.shape[0], R // tr), pl.BlockSpec((None, tr, C), lambda l, i: (l, i, 0))
    return pl.pallas_call(
        body, name="adamw", grid=grid, in_specs=[blk] * 4, out_specs=[blk] * 3,
        out_shape=[jax.ShapeDtypeStruct(w.shape, F32)] * 3, compiler_params=_cparams(("parallel",) * len(grid)),
    )(w, g, m, v)


def _my_place():
    return lax.axis_index("x"), lax.axis_index("y"), lax.axis_index("c")


ANY = pl.BlockSpec(memory_space=pl.ANY)
HBM = pl.BlockSpec(memory_space=pltpu.HBM)
SEM = pl.BlockSpec(memory_space=pltpu.SEMAPHORE)
EFFECT = pltpu.SideEffectType.DATAFLOW_SIDE_EFFECTING


def _in_hbm(a):
    return pltpu.with_memory_space_constraint(a, pltpu.HBM)


def _block_rows(ref, dev):
    r = ref.shape[0] // N_DEV
    start = pl.multiple_of((4 * dev[0] + 2 * dev[1] + dev[2]) * r, 16)
    return ref.at[pl.ds(start, r), :]


def _ag_first_copies(s_refs, land_refs, send_sems, recv_sems, receiving):
    x, y, c = _my_place()
    peers = [(x, y, 1 - c)] + [(*chip, c) for chip in _other_chips(x, y)]
    copies = []
    for k, peer in enumerate(peers):
        block = peer if receiving else (x, y, c)
        copies += [pltpu.make_async_remote_copy(
            src_ref=s_refs[w], dst_ref=_block_rows(land_refs[w], block),
            send_sem=send_sems.at[k * len(s_refs) + w], recv_sem=recv_sems.at[k * len(s_refs) + w],
            device_id=peer, device_id_type=MESH)
            for w in range(len(s_refs))]
    return copies


def _ag_start(shards, layer, after=()):
    nw = len(shards)

    def body(*refs):
        s_refs, land_refs = refs[:nw], refs[nw:2 * nw]
        token = refs[-1]
        sems = 2 * nw + len(after)
        for cp in _ag_first_copies(s_refs, land_refs, refs[sems], refs[sems + 1], False):
            cp.start()
        token[...] = jnp.zeros_like(token)

    lands = [lax.empty((N_DEV * s.shape[0], D_MODEL), BF16) for s in shards]
    out = pl.pallas_call(
        body, name="ag_start_%s" % layer,
        in_specs=[HBM] * (2 * nw) + [ANY] * len(after),
        out_specs=(SEM, SEM, *[HBM] * (2 * nw), pl.BlockSpec(memory_space=pltpu.VMEM)),
        out_shape=(pltpu.SemaphoreType.DMA((4 * nw,)), pltpu.SemaphoreType.DMA((4 * nw,)),
                   *[pltpu.HBM(a.shape, a.dtype) for a in list(shards) + lands],
                   jax.ShapeDtypeStruct((8, LANES), F32)),
        input_output_aliases={i: 2 + i for i in range(2 * nw)},
        compiler_params=pltpu.CompilerParams(has_side_effects=EFFECT),
    )(*[_in_hbm(a) for a in list(shards) + lands], *after)
    return out[0], out[1], out[2:2 + nw], out[2 + nw:2 + 2 * nw], out[-1]


def _ag_wait(send_sems, recv_sems, shards, lands, after, layer):
    nw = len(shards)

    def body(*refs):
        s_refs, land_refs = refs[:nw], refs[nw:2 * nw]
        for cp in _ag_first_copies(s_refs, land_refs, refs[2 * nw], refs[2 * nw + 1], True):
            cp.wait_send()
            cp.wait_recv()

    out = pl.pallas_call(
        body, name="ag_wait_%s" % layer,
        in_specs=[HBM] * (2 * nw) + [SEM, SEM] + [ANY] * len(after),
        out_specs=[HBM] * (2 * nw),
        out_shape=[pltpu.HBM(a.shape, a.dtype) for a in list(shards) + list(lands)],
        input_output_aliases={i: i for i in range(2 * nw)},
        compiler_params=pltpu.CompilerParams(has_side_effects=EFFECT),
    )(*shards, *lands, send_sems, recv_sems, *after)
    return out[:nw], out[nw:]


def _ag_pass_on(shards, lands):
    nw = len(shards)

    def body(*refs):
        s_refs, g_refs = refs[:nw], refs[2 * nw:3 * nw]
        send_sems, recv_sems, local_sems = refs[3 * nw:3 * nw + 3]
        stage = refs[3 * nw + 3:]
        x, y, c = _my_place()
        load = [pltpu.make_async_copy(s_refs[w], stage[w], local_sems.at[w]) for w in range(nw)]
        mine = [pltpu.make_async_copy(stage[w], _block_rows(g_refs[w], (x, y, c)), local_sems.at[w])
                for w in range(nw)]
        for cp in load:
            cp.start()
        sends, arrivals = [], []
        for j, chip in enumerate(_other_chips(x, y)):
            for w in range(nw):
                rows_out = _block_rows(g_refs[w], (*chip, c))
                rows_in = _block_rows(g_refs[w], (*chip, 1 - c))
                sends.append(pltpu.make_async_remote_copy(
                    src_ref=rows_out, dst_ref=rows_out, send_sem=send_sems.at[j, w], recv_sem=recv_sems.at[j, w],
                    device_id=(x, y, 1 - c), device_id_type=MESH))
                arrivals.append(pltpu.make_async_remote_copy(
                    src_ref=rows_in, dst_ref=rows_in, send_sem=send_sems.at[j, w], recv_sem=recv_sems.at[j, w],
                    device_id=(x, y, 1 - c), device_id_type=MESH))
        for cp in sends:
            cp.start()
        for w in range(nw):
            load[w].wait()
            mine[w].start()
        for cp in arrivals:
            cp.wait_recv()
        for cp in sends:
            cp.wait_send()
        for cp in mine:
            cp.wait()

    return pl.pallas_call(
        body, name="ag_pass_on",
        in_specs=[ANY] * (2 * nw), out_specs=[ANY] * nw,
        out_shape=[jax.ShapeDtypeStruct(a.shape, a.dtype) for a in lands],
        input_output_aliases={nw + i: i for i in range(nw)},
        scratch_shapes=[pltpu.SemaphoreType.DMA((3, nw)), pltpu.SemaphoreType.DMA((3, nw)),
                        pltpu.SemaphoreType.DMA((nw,))] + [pltpu.VMEM(s.shape, s.dtype) for s in shards],
        compiler_params=_cparams(),
    )(*shards, *lands)


def _rs_sibling_copies(p_refs, land_refs, send_sems, recv_sems):
    x, y, c = _my_place()
    return [pltpu.make_async_remote_copy(
        src_ref=p_refs[w].at[:, 1 - c], dst_ref=land_refs[w],
        send_sem=send_sems.at[w], recv_sem=recv_sems.at[w], device_id=(x, y, 1 - c), device_id_type=MESH)
        for w in range(len(p_refs))]


def _rs_sibling_start(parts, tag, after=()):
    nw = len(parts)
    sems = 2 * nw + len(after)

    def body(*refs):
        for cp in _rs_sibling_copies(refs[:nw], refs[nw:2 * nw], refs[sems], refs[sems + 1]):
            cp.start()
        refs[-1][...] = jnp.zeros_like(refs[-1])

    lands = [lax.empty(p.shape[:1] + p.shape[2:], BF16) for p in parts]
    out = pl.pallas_call(
        body, name="rs_sibling_start_%s" % tag,
        in_specs=[HBM] * (2 * nw) + [ANY] * len(after),
        out_specs=(SEM, SEM, *[HBM] * (2 * nw), pl.BlockSpec(memory_space=pltpu.VMEM)),
        out_shape=(pltpu.SemaphoreType.DMA((nw,)), pltpu.SemaphoreType.DMA((nw,)),
                   *[pltpu.HBM(a.shape, a.dtype) for a in list(parts) + lands],
                   jax.ShapeDtypeStruct((8, LANES), F32)),
        input_output_aliases={i: 2 + i for i in range(2 * nw)},
        compiler_params=pltpu.CompilerParams(has_side_effects=EFFECT),
    )(*[_in_hbm(a) for a in list(parts) + lands], *after)
    return out[0], out[1], out[2:2 + nw], out[2 + nw:2 + 2 * nw], out[-1]


def _rs_sibling_wait(send_sems, recv_sems, parts, lands, after, tag):
    nw = len(parts)

    def body(*refs):
        for cp in _rs_sibling_copies(refs[:nw], refs[nw:2 * nw], refs[2 * nw], refs[2 * nw + 1]):
            cp.wait_send()
            cp.wait_recv()

    out = pl.pallas_call(
        body, name="rs_sibling_wait_%s" % tag,
        in_specs=[HBM] * (2 * nw) + [SEM, SEM] + [ANY] * len(after),
        out_specs=[HBM] * (2 * nw),
        out_shape=[pltpu.HBM(a.shape, a.dtype) for a in list(parts) + list(lands)],
        input_output_aliases={i: i for i in range(2 * nw)},
        compiler_params=pltpu.CompilerParams(has_side_effects=EFFECT),
    )(*parts, *lands, send_sems, recv_sems, *after)
    return out[:nw], out[nw:]


def _rs_chip_sum(parts, gots, c):
    n = len(parts)

    def body(c_ref, *refs):
        for p_ref, g_ref, o_ref in zip(refs[:n], refs[n:2 * n], refs[2 * n:]):
            o_ref[...] = (p_ref[...].astype(F32) + g_ref[...].astype(F32)).astype(o_ref.dtype)

    mine = [pl.BlockSpec((None, None, p.shape[2], D_MODEL), lambda q, c_ref: (q, c_ref[0], 0, 0)) for p in parts]
    theirs = [pl.BlockSpec((None, g.shape[1], D_MODEL), lambda q, c_ref: (q, 0, 0)) for g in gots]
    return pl.pallas_call(
        body, name="rs_chip_sum",
        grid_spec=pltpu.PrefetchScalarGridSpec(
            num_scalar_prefetch=1, grid=(4,), in_specs=mine + theirs, out_specs=theirs),
        out_shape=[jax.ShapeDtypeStruct(g.shape, BF16) for g in gots],
        compiler_params=_cparams(("parallel",)),
    )(c, *parts, *gots)


def _other_chips(x, y):
    return [(1 - x, y), (x, 1 - y), (1 - x, 1 - y)]


def _rs_chip_copies(s_refs, land_refs, send_sems, recv_sems):
    x, y, c = _my_place()
    copies = []
    for k, chip in enumerate(_other_chips(x, y)):
        q = 2 * chip[0] + chip[1]
        copies += [pltpu.make_async_remote_copy(
            src_ref=s_refs[w].at[q], dst_ref=land_refs[w].at[k],
            send_sem=send_sems.at[k * len(s_refs) + w], recv_sem=recv_sems.at[k * len(s_refs) + w],
            device_id=(*chip, c), device_id_type=MESH)
            for w in range(len(s_refs))]
    return copies


def _rs_chip_start(sums, layer):
    nw = len(sums)

    def body(*refs):
        s_refs, land_refs = refs[:nw], refs[nw:2 * nw]
        send_sems, recv_sems = refs[2 * nw], refs[2 * nw + 1]
        token = refs[-1]
        for cp in _rs_chip_copies(s_refs, land_refs, send_sems, recv_sems):
            cp.start()
        token[...] = jnp.zeros_like(token)

    lands = [lax.empty((3,) + s.shape[1:], BF16) for s in sums]
    out = pl.pallas_call(
        body, name="rs_chip_start_%s" % layer,
        in_specs=[HBM] * (2 * nw),
        out_specs=(SEM, SEM, *[HBM] * (2 * nw), pl.BlockSpec(memory_space=pltpu.VMEM)),
        out_shape=(pltpu.SemaphoreType.DMA((3 * nw,)), pltpu.SemaphoreType.DMA((3 * nw,)),
                   *[pltpu.HBM(a.shape, a.dtype) for a in list(sums) + lands],
                   jax.ShapeDtypeStruct((8, LANES), F32)),
        input_output_aliases={i: 2 + i for i in range(2 * nw)},
        compiler_params=pltpu.CompilerParams(has_side_effects=EFFECT),
    )(*[_in_hbm(a) for a in list(sums) + lands])
    return out[0], out[1], out[2:2 + nw], out[2 + nw:2 + 2 * nw], out[-1]


def _rs_chip_wait(send_sems, recv_sems, sums, lands, after, layer):
    nw = len(sums)

    def body(*refs):
        s_refs, land_refs = refs[:nw], refs[nw:2 * nw]
        for cp in _rs_chip_copies(s_refs, land_refs, refs[2 * nw], refs[2 * nw + 1]):
            cp.wait_send()
            cp.wait_recv()

    out = pl.pallas_call(
        body, name="rs_chip_wait_%s" % layer,
        in_specs=[HBM] * (2 * nw) + [SEM, SEM] + [ANY] * len(after),
        out_specs=[HBM] * (2 * nw),
        out_shape=[pltpu.HBM(a.shape, a.dtype) for a in list(sums) + list(lands)],
        input_output_aliases={i: i for i in range(2 * nw)},
        compiler_params=pltpu.CompilerParams(has_side_effects=EFFECT),
    )(*sums, *lands, send_sems, recv_sems, *after)
    return out[:nw], out[nw:]


def _rs_finish(sums, gots, q, layer, into):
    n = len(sums)

    def body(q_ref, *refs):
        for s_ref, g_ref, o_ref in zip(refs[:n], refs[n:2 * n], refs[len(refs) - n:]):
            o_ref[...] = ((s_ref[...].astype(F32) + g_ref[0].astype(F32)) + g_ref[1].astype(F32)) + g_ref[2].astype(F32)

    rows = [s.shape[1] for s in sums]
    in_specs = [pl.BlockSpec((None, r, D_MODEL), lambda i, q_ref: (q_ref[0], 0, 0)) for r in rows]
    in_specs += [pl.BlockSpec((3, r, D_MODEL), lambda i, q_ref: (0, 0, 0)) for r in rows]
    args = [q, *sums, *gots]
    aliases = {}
    if into is not None:
        in_specs += [ANY] * n
        aliases = {len(args) + i: i for i in range(n)}
        args += list(into)
    return pl.pallas_call(
        body, name="rs_finish",
        grid_spec=pltpu.PrefetchScalarGridSpec(
            num_scalar_prefetch=1, grid=(1,), in_specs=in_specs,
            out_specs=[pl.BlockSpec((None, r, D_MODEL), lambda i, q_ref: (layer, 0, 0)) for r in rows]),
        out_shape=[jax.ShapeDtypeStruct((DEPTH, r, D_MODEL), F32) for r in rows],
        input_output_aliases=aliases,
        compiler_params=_cparams(("arbitrary",)),
    )(*args)


def _allreduce_small(vec, deps=()):
    R = vec.shape[0]
    assert R % (8 * N_DEV) == 0
    P = R // N_DEV
    nd = len(deps)

    def body(*refs):
        v_ref = refs[0]
        o_ref, buf, send1, recv1, send2, recv2 = refs[1 + nd:]
        x, y, c = _my_place()
        me = 4 * x + 2 * y + c

        def piece(ref, d):
            return ref.at[pl.ds(pl.multiple_of(d * P, 8), P), :]

        def peer(k):
            p = me ^ k
            return p, (p >> 2, (p >> 1) & 1, p & 1)

        scatter = []
        for k in range(1, N_DEV):
            p, where = peer(k)
            scatter.append(pltpu.make_async_remote_copy(
                src_ref=piece(v_ref, p), dst_ref=buf.at[k], send_sem=send1.at[k - 1], recv_sem=recv1.at[k - 1],
                device_id=where, device_id_type=MESH))
        for cp in scatter:
            cp.start()
        buf[0] = piece(v_ref, me)[...]
        for cp in scatter:
            cp.wait()
        acc = buf[me]
        for d in range(1, N_DEV):
            acc = acc + buf[me ^ d]
        piece(o_ref, me)[...] = acc
        spread, arrivals = [], []
        for k in range(1, N_DEV):
            p, where = peer(k)
            spread.append(pltpu.make_async_remote_copy(
                src_ref=piece(o_ref, me), dst_ref=piece(o_ref, me), send_sem=send2.at[k - 1], recv_sem=recv2.at[k - 1],
                device_id=where, device_id_type=MESH))
            arrivals.append(pltpu.make_async_remote_copy(
                src_ref=piece(o_ref, p), dst_ref=piece(o_ref, p), send_sem=send2.at[k - 1], recv_sem=recv2.at[k - 1],
                device_id=where, device_id_type=MESH))
        for cp in spread:
            cp.start()
        for cp in arrivals:
            cp.wait_recv()
        for cp in spread:
            cp.wait_send()

    sems = pltpu.SemaphoreType.DMA((N_DEV - 1,))
    return pl.pallas_call(
        body, name="allreduce_small",
        in_specs=[pl.BlockSpec(memory_space=pltpu.VMEM)] + [ANY] * nd, out_specs=pl.BlockSpec(memory_space=pltpu.VMEM),
        out_shape=jax.ShapeDtypeStruct((R, LANES), F32),
        scratch_shapes=[pltpu.VMEM((N_DEV, P, LANES), F32), sems, sems, sems, sems],
        compiler_params=_cparams(),
    )(vec, *deps)


def _pack(arrs):
    flat = jnp.concatenate([a.reshape(-1) for a in arrs])
    pad = (-flat.shape[0]) % (8 * N_DEV * LANES)
    return jnp.pad(flat, (0, pad)).reshape(-1, LANES)


def _unpack(packed, shapes):
    flat = packed.reshape(-1)
    out, off = [], 0
    for s in shapes:
        n = math.prod(s)
        out.append(flat[off:off + n].reshape(s))
        off += n
    return out


def kernel(x, w_in, w_conv, w_pool, pool_scale, sgu_ln_g, w_spatial, b_spatial, w_o, ln1_g, ln1_b, w_gate_up, w_down, ln2_g, ln2_b, loss_target, m_w_in, m_w_conv, m_w_pool, m_pool_scale, m_sgu_ln_g, m_w_spatial, m_b_spatial, m_w_o, m_ln1_g, m_ln1_b, m_w_gate_up, m_w_down, m_ln2_g, m_ln2_b, v_w_in, v_w_conv, v_w_pool, v_pool_scale, v_sgu_ln_g, v_w_spatial, v_b_spatial, v_w_o, v_ln1_g, v_ln1_b, v_w_gate_up, v_w_down, v_ln2_g, v_ln2_b):
    L = DEPTH
    T = x.shape[1]
    mx, my, mc = _my_place()
    dev = 4 * mx + 2 * my + mc
    xs = x[0]
    target = loss_target[0]

    conv_cols = w_conv.shape[2]
    w_conv_z = lax.dynamic_update_slice(jnp.zeros((L, 3, CONV_W), F32), w_conv, (0, 0, dev * conv_cols))
    w_conv_packed = _allreduce_small(_pack([w_conv_z]))
    w_conv_full = _unpack(w_conv_packed, [(L, 3, CONV_W)])[0]

    shards = (jnp.swapaxes(w_in, 1, 2).astype(BF16), jnp.swapaxes(w_gate_up, 1, 2).astype(BF16),
              w_o.astype(BF16), w_down.astype(BF16))
    first_gather = _ag_start_layer(shards, 0, [w_conv_packed])

    grad_x2, big_grads, small_grads = _local_step(
        xs, target, shards, first_gather, w_conv_full, w_pool, pool_scale, sgu_ln_g, w_spatial, b_spatial,
        ln1_g, ln1_b, ln2_g, ln2_b)
    grad_x = grad_x2[None]
    big_w = (w_in, w_gate_up, w_o, w_down)
    big_m = (m_w_in, m_w_gate_up, m_w_o, m_w_down)
    big_v = (v_w_in, v_w_gate_up, v_w_o, v_w_down)
    small_w = [w_conv_full, w_pool, pool_scale, sgu_ln_g, w_spatial, b_spatial, ln1_g, ln1_b, ln2_g, ln2_b]
    small_m = [m_w_conv, m_w_pool, m_pool_scale, m_sgu_ln_g, m_w_spatial, m_b_spatial, m_ln1_g, m_ln1_b, m_ln2_g, m_ln2_b]
    small_v = [v_w_conv, v_w_pool, v_pool_scale, v_sgu_ln_g, v_w_spatial, v_b_spatial, v_ln1_g, v_ln1_b, v_ln2_g, v_ln2_b]
    loss, grads, deltas, new_m, new_v = _reduce_and_update(
        big_grads, small_grads, big_w, big_m, big_v, small_w, small_m, small_v)
    return (loss, grad_x, *grads, *deltas, *new_m, *new_v)


def _ag_start_layer(shards, l, after):
    s_in, s_gu, s_o, s_dn = [s[l] for s in shards]
    first = _ag_start([s_in, s_o], "%da" % l, after=after)
    return first, _ag_start([s_gu, s_dn], "%db" % l, after=[first[4]])


def _ag_finish(gather, after, tag):
    send_sems, recv_sems, shards, lands, _ = gather
    shards, lands = _ag_wait(send_sems, recv_sems, shards, lands, after, tag)
    return _ag_pass_on(shards, lands)


def _rs_begin(parts, tag, after=()):
    return _rs_sibling_start([p.reshape(4, 2, p.shape[0] // N_DEV, D_MODEL) for p in parts], tag, after)


def _rs_continue(sibling_flight, after, c_arr, tag):
    send_sems, recv_sems, parts, lands, _ = sibling_flight
    parts, got = _rs_sibling_wait(send_sems, recv_sems, parts, lands, after, tag)
    return _rs_chip_start(_rs_chip_sum(parts, got, c_arr), tag)


def _local_step(xs, target, shards, gather, w_conv_full, w_pool, pool_scale, sgu_ln_g, w_spatial, b_spatial,
                ln1_g, ln1_b, ln2_g, ln2_b):
    L = DEPTH
    T = xs.shape[0]
    mx, my, mc = _my_place()
    c_arr = jnp.reshape(mc, (1,)).astype(jnp.int32)
    q_arr = jnp.reshape(2 * mx + my, (1,)).astype(jnp.int32)
    eye2 = jnp.eye(2, dtype=F32)
    wp = w_pool.reshape(L, 2, 2, HALF, HALF)
    wpool_bd = jnp.einsum("ltgcd,gh->ltgchd", wp, eye2).reshape(L, 2, LANES, LANES)
    wsp_t = w_spatial.reshape(L, 3, 2 * CHUNK, CHUNK)
    bias_t = jnp.repeat(jnp.swapaxes(b_spatial.reshape(L, 3, 2, CHUNK), 2, 3), HALF, axis=3)
    mixer_w = (w_conv_full, wpool_bd, pool_scale[:, None, :], sgu_ln_g[:, None, :], wsp_t, bias_t)
    g1, b1, g2, b2 = [a[:, None, :] for a in (ln1_g, ln1_b, ln2_g, ln2_b)]
    one, zero = jnp.ones((1, 1, D_MODEL), F32), jnp.zeros((1, 1, D_MODEL), F32)

    saved = []
    prev, pg, pb = xs, (one, 0), (zero, 0)
    prev_b = xs.astype(BF16)
    weights = []
    for l in range(L):
        g_in, g_o = _ag_finish(gather[0], [] if l == 0 else [prev_b], "%da" % l)
        proj = _mm(prev_b, g_in, "nt", F32, 512, IN_W, "mm_proj", deps=[gather[1][4]] if l == 0 else [])
        mixcat = _mixer_fwd(proj, *mixer_w, l)
        xhat1, rstd1, h_b = _mm_ln_fwd(mixcat, g_o, prev, pg, pb, (g1, l), (b1, l), "mm_wo_ln")
        g_gu, g_dn = _ag_finish(gather[1], [h_b], "%db" % l)
        weights.append((g_in, g_gu, g_o, g_dn))
        deps = []
        if l + 1 < L:
            gather = _ag_start_layer(shards, l + 1, [g_gu])
            deps = [gather[1][4]]
        g_act, u_act, act = _mm_swiglu_fwd(h_b, g_gu, deps=deps)
        xhat2, rstd2, y_b = _mm_ln_fwd(act, g_dn, xhat1, (g1, l), (b1, l), (g2, l), (b2, l), "mm_down_ln")
        saved.append((prev_b, proj, mixcat, xhat1, rstd1, h_b, g_act, u_act, act, xhat2, rstd2))
        prev, pg, pb, prev_b = xhat2, (g2, l), (b2, l), y_b


    small = [None] * L
    big = None
    sibling_flight = None
    above = None
    for l in reversed(range(L)):
        prev_b, proj, mixcat, xhat1, rstd1, h_b, g_act, u_act, act, xhat2, rstd2 = saved[l]
        g_in, g_gu, g_o, g_dn = weights[l]
        chip_flight = None
        if above is None:
            loss_tile, dr2, dr2_b, dg2, db2 = _loss_ln_bwd(xhat2, rstd2, (g2, l), (b2, l), target)
        else:
            dr2, dr2_b, dg2, db2 = _mm_ln_bwd([above[0]], above[1], above[2], xhat2, rstd2, (g2, l),
                                              "mm_dx_ln", deps=[sibling_flight[4]])
            chip_flight = _rs_continue(sibling_flight, [dr2_b], c_arr, str(l + 1))
        dg_b, du_b = _mm_swiglu_bwd(dr2_b, g_dn, g_act, u_act, deps=[chip_flight[4]] if chip_flight else [])
        p_dn = _mm(act, dr2_b, "tn", BF16, DW_TM, D_MODEL, "mm_dw_down")
        p_gu = _mm_tn_pair(dg_b, du_b, h_b, DW_TM, "mm_dw_gate_up")
        ffn_sibling = _rs_begin([p_gu, p_dn], "0b") if l == 0 else None
        dr1, dr1_b, dg1, db1, dmix = _mm_ln_bwd([dg_b, du_b], g_gu, dr2, xhat1, rstd1, (g1, l), "mm_dh_ln",
                                                deps=[ffn_sibling[4]] if l == 0 else [], w_back=g_o)
        ffn_flight = _rs_continue(ffn_sibling, [dr1_b], c_arr, "0b") if l == 0 else None
        p_o = _mm(mixcat, dr1_b, "tn", BF16, 512, D_MODEL, "mm_dw_o")
        dproj, dwc, dwp, dps, dlng, dwsp, dbias = _mixer_bwd(proj, dmix, *mixer_w, l,
                                                             deps=[ffn_flight[4]] if l == 0 else [])
        p_in = _mm(dproj, prev_b, "tn", BF16, IN_W, D_MODEL // 2, "mm_dw_in")
        small[l] = (dwc, dwp, dps, dlng, dwsp, dbias, dg1, db1, dg2, db2)
        above = (dproj, g_in, dr1)
        if chip_flight is not None:
            big = list(_rs_chip_finish(chip_flight, [p_in], q_arr, str(l + 1), l + 1, big))
        if l > 0:
            sibling_flight = _rs_begin([p_in, p_gu, p_o, p_dn], str(l))
        else:
            big[1], big[3] = _rs_chip_finish(ffn_flight, [p_in, p_o], q_arr, "0b", 0, [big[1], big[3]])

    def stack(i):
        return jnp.stack([small[l][i] for l in range(L)])

    dwp_bd = stack(1).reshape(L, 2, 2, HALF, 2, HALF)
    dwp_all = jnp.einsum("ltgchd,gh->ltgcd", dwp_bd, eye2).reshape(L, 4, HALF, HALF)
    dbs_all = jnp.swapaxes(stack(5)[:, :, :, :2], 2, 3).reshape(L, 6, CHUNK)
    small_grads = [stack(0), dwp_all, stack(2).reshape(L, POOL_W), stack(3).reshape(L, SGU_W),
                   stack(4).reshape(L, 6, CHUNK, CHUNK), dbs_all] + [stack(i).reshape(L, D_MODEL) for i in (6, 7, 8, 9)]
    small_grads.append(loss_tile[0, :1])
    packed_small = _allreduce_small(_pack(small_grads), deps=[big[1]])
    sibling_flight = _rs_begin([p_in, p_o], "0a", after=[packed_small])
    grad_x = _mm_ln_bwd([above[0]], above[1], above[2], None, None, None, "mm_dx_out", deps=[sibling_flight[4]])
    last_flight = _rs_continue(sibling_flight, [grad_x], c_arr, "0a")
    return grad_x, (big, last_flight, q_arr), (packed_small, [a.shape for a in small_grads])


def _rs_chip_finish(in_flight, after, q, tag, layer, into):
    send_sems, recv_sems, sums, lands, _ = in_flight
    sums, got = _rs_chip_wait(send_sems, recv_sems, sums, lands, after, tag)
    return _rs_finish(sums, got, q, layer, into)


def _reduce_and_update(big_grads, small_grads, big_w, big_m, big_v, small_w, small_m, small_v):
    L = DEPTH
    mx, my, mc = _my_place()
    dev = 4 * mx + 2 * my + mc
    conv_cols = CONV_W // N_DEV
    w_in, w_gate_up, w_o, w_down = big_w
    m_w_in, m_w_gate_up, m_w_o, m_w_down = big_m
    v_w_in, v_w_gate_up, v_w_o, v_w_down = big_v
    packed_g, small_shapes = small_grads
    big, last_flight, q_arr = big_grads

    def widen_conv(a):
        return lax.dynamic_update_slice(jnp.zeros((L, 3, CONV_W), F32), a, (0, 0, dev * conv_cols))

    small_m = [widen_conv(small_m[0])] + list(small_m[1:])
    small_v = [widen_conv(small_v[0])] + list(small_v[1:])
    pk_d, pk_m, pk_v = _adamw(_pack(small_w), packed_g, _pack(small_m), _pack(small_v), packed_g.shape[0] // 2)
    sg = _unpack(packed_g, small_shapes)
    sd = _unpack(pk_d, small_shapes)
    sm = _unpack(pk_m, small_shapes)
    sv = _unpack(pk_v, small_shapes)

    def conv_cols_of(a):
        return lax.dynamic_slice(a, (0, 0, dev * conv_cols), (L, 3, conv_cols))

    for lst in (sg, sd, sm, sv):
        lst[0] = conv_cols_of(lst[0])

    tr = lambda a: jnp.swapaxes(a, 1, 2)
    gt_gu, g_w_dn = big[1], big[3]
    d_gu, m_gu, v_gu = [tr(a) for a in _adamw(tr(w_gate_up), gt_gu, tr(m_w_gate_up), tr(v_w_gate_up), gt_gu.shape[1] // 2)]
    d_dn, m_dn, v_dn = _adamw(w_down, g_w_dn, m_w_down, v_w_down, w_down.shape[1])
    gt_in, g_w_o = _rs_chip_finish(last_flight, [d_gu, d_dn, pk_d], q_arr, "0a", 0, [big[0], big[2]])
    d_in, m_in, v_in = [tr(a) for a in _adamw(tr(w_in), gt_in, tr(m_w_in), tr(v_w_in), gt_in.shape[1])]
    d_o, m_o, v_o = _adamw(w_o, g_w_o, m_w_o, v_w_o, w_o.shape[1])
    g_w_in, g_w_gu = tr(gt_in), tr(gt_gu)

    def ordered(big_in, big_o, big_gu, big_dn, sm_list):
        return [big_in, sm_list[0], sm_list[1], sm_list[2], sm_list[3], sm_list[4], sm_list[5], big_o,
                sm_list[6], sm_list[7], big_gu, big_dn, sm_list[8], sm_list[9]]

    grads = ordered(g_w_in, g_w_o, g_w_gu, g_w_dn, sg)
    deltas = ordered(d_in, d_o, d_gu, d_dn, sd)
    new_m = ordered(m_in, m_o, m_gu, m_dn, sm)
    new_v = ordered(v_in, v_o, v_gu, v_dn, sv)
    return sg[10][0], grads, deltas, new_m, new_v
```

```python
import math

import jax
import jax.numpy as jnp
from jax import lax
from jax.experimental import pallas as pl
from jax.experimental.pallas import tpu as pltpu

F32 = jnp.float32
BF16 = jnp.bfloat16
MESH = pl.DeviceIdType.MESH

D_MODEL = 1024
DEPTH = 4
CONV_W = 384
POOL_W = 256
SGU_W = 384
IN_W = 3 * CONV_W + POOL_W + 2 * SGU_W
D_FF = 2816
CHUNK = 128
ALPHA = float((2 * DEPTH) ** 0.25)
LN_EPS = 1e-5
ADAM_LR, ADAM_B1, ADAM_B2, ADAM_EPS, ADAM_WD, ADAM_STEP = 0.001, 0.9, 0.999, 1e-08, 0.01, 10

N_DEV = 8
LANES = 128
HALF = 64
VMEM_LIMIT = 52 * 1024 * 1024

INV_SQRT2 = 0.7071067811865476
INV_SQRT_2PI = 0.3989422804014327


def _cparams(sem=None, **kw):
    if sem is not None:
        kw["dimension_semantics"] = sem
    return pltpu.CompilerParams(vmem_limit_bytes=VMEM_LIMIT, **kw)


_DN = {"nt": (((1,), (1,)), ((), ())), "tn": (((0,), (0,)), ((), ()))}


def _mm(a, b, mode, out_dtype, tm, tn, name, deps=()):
    if mode == "nt":
        (M, K), N = a.shape, b.shape[0]
        a_spec = pl.BlockSpec((tm, K), lambda i, j: (i, 0))
        b_spec = pl.BlockSpec((tn, K), lambda i, j: (j, 0))
    else:
        (K, M), N = a.shape, b.shape[1]
        a_spec = pl.BlockSpec((K, tm), lambda i, j: (0, i))
        b_spec = pl.BlockSpec((K, tn), lambda i, j: (0, j))
    assert M % tm == 0 and N % tn == 0, (M, N, K, tm, tn)
    nd = len(deps)

    def body(*refs):
        a_ref, b_ref, o_ref = refs[0], refs[1], refs[2 + nd]
        o_ref[...] = lax.dot_general(a_ref[...], b_ref[...], _DN[mode], preferred_element_type=F32).astype(o_ref.dtype)

    return pl.pallas_call(
        body,
        name=name,
        grid=(M // tm, N // tn),
        in_specs=[a_spec, b_spec] + [pl.BlockSpec(memory_space=pl.ANY)] * nd,
        out_specs=pl.BlockSpec((tm, tn), lambda i, j: (i, j)),
        out_shape=jax.ShapeDtypeStruct((M, N), out_dtype),
        compiler_params=_cparams(("parallel", "parallel")),
    )(a, b, *deps)


def _mm_tn_pair(a1, a2, b, tm, name):
    K, M = a1.shape
    N = b.shape[1]
    n1 = M // tm

    def body(a1_ref, a2_ref, b_ref, o_ref):
        i = pl.program_id(0)

        @pl.when(i < n1)
        def _():
            o_ref[...] = lax.dot_general(a1_ref[...], b_ref[...], _DN["tn"], preferred_element_type=F32).astype(o_ref.dtype)

        @pl.when(i >= n1)
        def _():
            o_ref[...] = lax.dot_general(a2_ref[...], b_ref[...], _DN["tn"], preferred_element_type=F32).astype(o_ref.dtype)

    return pl.pallas_call(
        body, name=name, grid=(2 * n1,),
        in_specs=[pl.BlockSpec((K, tm), lambda i: (0, jnp.minimum(i, n1 - 1))),
                  pl.BlockSpec((K, tm), lambda i: (0, jnp.maximum(i - n1, 0))),
                  pl.BlockSpec((K, N), lambda i: (0, 0))],
        out_specs=pl.BlockSpec((tm, N), lambda i: (i, 0)),
        out_shape=jax.ShapeDtypeStruct((2 * M, N), BF16),
        compiler_params=_cparams(("arbitrary",)),
    )(a1, a2, b)


LN_SUB = 256
LN_TM = 512


def _vec(v):
    arr, layer = v
    return arr, pl.BlockSpec((None, 1, D_MODEL), lambda *_: (layer, 0, 0))


W_CHUNK = 1408


def _k_chunks(ks):
    out, off = [], 0
    for p, k in enumerate(ks):
        n = -(-k // W_CHUNK)
        base = k // LANES // n * LANES
        for c in range(n):
            start = c * base
            out.append((p, start, off + start, base if c < n - 1 else k - start))
        off += k
    return out


class _ResidentWeight:
    def __init__(self, hbm_ref, vmem_ref, sems, pieces):
        self.vmem = vmem_ref
        self.copies = [pltpu.make_async_copy(hbm_ref.at[pl.ds(r0, n), :], vmem_ref.at[pl.ds(r0, n), :], sems.at[c])
                       for c, (r0, n) in enumerate(pieces)]

    def start(self):
        @pl.when(pl.program_id(0) == 0)
        def _():
            for cp in self.copies:
                cp.start()

    def arrived(self, c):
        @pl.when(pl.program_id(0) == 0)
        def _():
            self.copies[c].wait()


def _mm_ln_fwd(a, b, prev, pg, pb, g, bias, name):
    T, K = a.shape
    tm = LN_TM
    chunks = _k_chunks([K])

    def body(a_ref, b_hbm, prev_ref, pg_ref, pb_ref, g_ref, bias_ref, xhat_ref, rstd_ref, y_ref, b_vmem, b_sems):
        weight = _ResidentWeight(b_hbm, b_vmem, b_sems, [(r0, n) for _, _, r0, n in chunks])
        weight.start()
        for s in range(tm // LN_SUB):
            rows = slice(s * LN_SUB, (s + 1) * LN_SUB)
            mm = None
            for c, (_, c0, r0, n) in enumerate(chunks):
                if s == 0:
                    weight.arrived(c)
                part = jnp.dot(a_ref[rows, c0:c0 + n], b_vmem[r0:r0 + n, :], preferred_element_type=F32)
                mm = part if mm is None else mm + part
            r = ALPHA * (prev_ref[rows, :] * pg_ref[...] + pb_ref[...]) + mm
            mu = jnp.mean(r, axis=-1, keepdims=True)
            xc = r - mu
            var = jnp.mean(xc * xc, axis=-1, keepdims=True)
            rstd = lax.rsqrt(var + LN_EPS)
            xhat = xc * rstd
            xhat_ref[rows, :] = xhat
            rstd_ref[rows, :] = rstd
            y_ref[rows, :] = (xhat * g_ref[...] + bias_ref[...]).astype(y_ref.dtype)

    row = pl.BlockSpec((tm, D_MODEL), lambda i: (i, 0))
    vecs = [_vec(v) for v in (pg, pb, g, bias)]
    return pl.pallas_call(
        body, name=name, grid=(T // tm,),
        in_specs=[pl.BlockSpec((tm, K), lambda i: (i, 0)), pl.BlockSpec(memory_space=pl.ANY),
                  row] + [s for _, s in vecs],
        out_specs=[row, pl.BlockSpec((tm, 1), lambda i: (i, 0)), row],
        out_shape=[jax.ShapeDtypeStruct((T, D_MODEL), F32), jax.ShapeDtypeStruct((T, 1), F32),
                   jax.ShapeDtypeStruct((T, D_MODEL), BF16)],
        scratch_shapes=[pltpu.VMEM(b.shape, b.dtype), pltpu.SemaphoreType.DMA((len(chunks),))],
        compiler_params=_cparams(("arbitrary",)),
    )(a, b, prev, *[a_ for a_, _ in vecs])


def _mm_ln_bwd(a_list, b, dres, xhat, rstd, g, name, deps=(), w_back=None):
    T = a_list[0].shape[0]
    tm = LN_TM
    na, nd = len(a_list), len(deps)
    ks = [a.shape[1] for a in a_list]
    last = xhat is None
    nout = 1 if last else (5 if w_back is not None else 4)
    chunks = _k_chunks(ks)
    nscratch = 2 if w_back is None else 4

    def body(*refs):
        a_refs, b_hbm, dres_ref = refs[:na], refs[na], refs[na + 1]
        scratch = refs[len(refs) - nscratch:]
        refs = refs[:len(refs) - nscratch]
        weight = _ResidentWeight(b_hbm, scratch[0], scratch[1], [(r0, n) for _, _, r0, n in chunks])
        weight.start()
        if w_back is not None:
            back = _ResidentWeight(refs[na + 5], scratch[2], scratch[3], [(0, w_back.shape[0])])
            back.start()
        if not last:
            xhat_ref, rstd_ref, g_ref = refs[na + 2:na + 5]
            dr_ref, drb_ref, dg_ref, db_ref = refs[len(refs) - nout:len(refs) - nout + 4]

            @pl.when(pl.program_id(0) == 0)
            def _():
                dg_ref[...] = jnp.zeros_like(dg_ref)
                db_ref[...] = jnp.zeros_like(db_ref)

        for s in range(tm // LN_SUB):
            rows = slice(s * LN_SUB, (s + 1) * LN_SUB)
            mm = None
            for c, (p, c0, r0, n) in enumerate(chunks):
                if s == 0:
                    weight.arrived(c)
                part = jnp.dot(a_refs[p][rows, c0:c0 + n], weight.vmem[r0:r0 + n, :], preferred_element_type=F32)
                mm = part if mm is None else mm + part
            dy = ALPHA * dres_ref[rows, :] + mm
            if last:
                refs[-1][rows, :] = dy
                continue
            xhat_v = xhat_ref[rows, :]
            dg_ref[...] += jnp.sum(dy * xhat_v, axis=0, keepdims=True)
            db_ref[...] += jnp.sum(dy, axis=0, keepdims=True)
            dxh = dy * g_ref[...]
            m1 = jnp.mean(dxh, axis=-1, keepdims=True)
            m2 = jnp.mean(dxh * xhat_v, axis=-1, keepdims=True)
            dr = rstd_ref[rows, :] * (dxh - m1 - xhat_v * m2)
            dr_ref[rows, :] = dr
            dr_b = dr.astype(drb_ref.dtype)
            drb_ref[rows, :] = dr_b
            if w_back is not None:
                if s == 0:
                    back.arrived(0)
                refs[-1][rows, :] = lax.dot_general(dr_b, back.vmem[...], _DN["nt"], preferred_element_type=F32)

    row = pl.BlockSpec((tm, D_MODEL), lambda i: (i, 0))
    vec = pl.BlockSpec((1, D_MODEL), lambda i: (0, 0))
    in_specs = [pl.BlockSpec((tm, k), lambda i: (i, 0)) for k in ks]
    in_specs += [pl.BlockSpec(memory_space=pl.ANY), row]
    args = list(a_list) + [b, dres]
    scratch = [pltpu.VMEM(b.shape, b.dtype), pltpu.SemaphoreType.DMA((len(chunks),))]
    if last:
        out_specs, out_shape = row, jax.ShapeDtypeStruct((T, D_MODEL), F32)
    else:
        g_arr, g_spec = _vec(g)
        in_specs += [row, pl.BlockSpec((tm, 1), lambda i: (i, 0)), g_spec]
        args += [xhat, rstd, g_arr]
        out_specs = [row, row, vec, vec]
        out_shape = [jax.ShapeDtypeStruct((T, D_MODEL), F32), jax.ShapeDtypeStruct((T, D_MODEL), BF16),
                     jax.ShapeDtypeStruct((1, D_MODEL), F32), jax.ShapeDtypeStruct((1, D_MODEL), F32)]
        if w_back is not None:
            in_specs.append(pl.BlockSpec(memory_space=pl.ANY))
            args.append(w_back)
            out_specs.append(row)
            out_shape.append(jax.ShapeDtypeStruct((T, w_back.shape[0]), F32))
            scratch += [pltpu.VMEM(w_back.shape, w_back.dtype), pltpu.SemaphoreType.DMA((1,))]
    return pl.pallas_call(
        body, name=name, grid=(T // tm,),
        in_specs=in_specs + [pl.BlockSpec(memory_space=pl.ANY)] * nd,
        out_specs=out_specs, out_shape=out_shape, scratch_shapes=scratch,
        compiler_params=_cparams(("arbitrary",)),
    )(*args, *deps)


DW_TM = 1408
FF_TN = 256
FF_TM = 2048
SAVED_GU = BF16


def _mm_swiglu_fwd(h, w_gu, deps=()):
    T = h.shape[0]
    tm = min(T, FF_TM)
    nj = D_FF // FF_TN
    nd = len(deps)

    def body(*refs):
        h_ref, wg_ref, wu_ref = refs[:3]
        g_ref, u_ref, act_ref = refs[3 + nd:]
        hv = h_ref[...]
        gv = lax.dot_general(hv, wg_ref[...], _DN["nt"], preferred_element_type=F32)
        uv = lax.dot_general(hv, wu_ref[...], _DN["nt"], preferred_element_type=F32)
        g_ref[...] = gv.astype(g_ref.dtype)
        u_ref[...] = uv.astype(u_ref.dtype)
        act_ref[...] = (gv * jax.nn.sigmoid(gv) * uv).astype(act_ref.dtype)

    tile = pl.BlockSpec((tm, FF_TN), lambda j, i: (i, j))
    return pl.pallas_call(
        body, name="mm_gate_up_swiglu", grid=(nj, T // tm),
        in_specs=[pl.BlockSpec((tm, D_MODEL), lambda j, i: (i, 0)),
                  pl.BlockSpec((FF_TN, D_MODEL), lambda j, i: (j, 0)),
                  pl.BlockSpec((FF_TN, D_MODEL), lambda j, i: (j + nj, 0))] + [pl.BlockSpec(memory_space=pl.ANY)] * nd,
        out_specs=[tile, tile, tile],
        out_shape=[jax.ShapeDtypeStruct((T, D_FF), SAVED_GU), jax.ShapeDtypeStruct((T, D_FF), SAVED_GU),
                   jax.ShapeDtypeStruct((T, D_FF), BF16)],
        compiler_params=_cparams(("parallel", "parallel")),
    )(h, w_gu, w_gu, *deps)


def _mm_swiglu_bwd(dr, w_dn, g, u, deps=()):
    T = dr.shape[0]
    tm = min(T, FF_TM)

    def body(*refs):
        dr_ref, w_ref, g_ref, u_ref = refs[:4]
        dg_ref, du_ref = refs[-2:]
        da = lax.dot_general(dr_ref[...], w_ref[...], _DN["nt"], preferred_element_type=F32)
        gv, uv = g_ref[...].astype(F32), u_ref[...].astype(F32)
        s = jax.nn.sigmoid(gv)
        du_ref[...] = (da * (gv * s)).astype(du_ref.dtype)
        dg_ref[...] = (da * uv * (s * (1.0 + gv * (1.0 - s)))).astype(dg_ref.dtype)

    tile = pl.BlockSpec((tm, FF_TN), lambda j, i: (i, j))
    return pl.pallas_call(
        body, name="mm_dact_swiglu", grid=(D_FF // FF_TN, T // tm),
        in_specs=[pl.BlockSpec((tm, D_MODEL), lambda j, i: (i, 0)), pl.BlockSpec((FF_TN, D_MODEL), lambda j, i: (j, 0)),
                  tile, tile] + [ANY] * len(deps),
        out_specs=[tile, tile],
        out_shape=[jax.ShapeDtypeStruct((T, D_FF), BF16)] * 2,
        compiler_params=_cparams(("parallel", "parallel")),
    )(dr, w_dn, g, u, *deps)


def _gelu(x):
    return 0.5 * x * (1.0 + lax.erf(x * INV_SQRT2))


def _gelu_grad(x):
    return 0.5 * (1.0 + lax.erf(x * INV_SQRT2)) + x * (jnp.exp(-0.5 * x * x) * INV_SQRT_2PI)


def _shift_down(z, k):
    row = lax.broadcasted_iota(jnp.int32, z.shape, 0)
    return jnp.where(row >= k, pltpu.roll(z, k, 0), 0.0)


def _shift_up(z, k):
    n = z.shape[0]
    row = lax.broadcasted_iota(jnp.int32, z.shape, 0)
    return jnp.where(row < n - k, pltpu.roll(z, n - k, 0), 0.0)


def _lo_mask(shape):
    return lax.broadcasted_iota(jnp.int32, shape, len(shape) - 1) < HALF


def _seg_mean(x, lo):
    a = jnp.sum(jnp.where(lo, x, 0.0), axis=-1, keepdims=True)
    b = jnp.sum(jnp.where(lo, 0.0, x), axis=-1, keepdims=True)
    return jnp.where(lo, a, b) * (1.0 / HALF)


def _pool_windows(first):
    lo = _lo_mask((1, LANES))
    return jnp.where(first, jnp.where(lo, 2.0, 4.0), jnp.where(lo, 8.0, 16.0)), lo


def _pool_mean_minus_token(p, first):
    wl, lo = _pool_windows(first)
    s2 = p + _shift_down(p, 1)
    s4 = s2 + _shift_down(s2, 2)
    s8 = s4 + _shift_down(s4, 4)
    s16 = s8 + _shift_down(s8, 8)
    win = jnp.where(first, jnp.where(lo, s2, s4), jnp.where(lo, s8, s16))
    t1 = (lax.broadcasted_iota(jnp.int32, p.shape, 0) + 1).astype(F32)
    count = jnp.minimum(t1, wl)
    return win / count - p, count


SGU_UNROLL = 4


def _tril_keep():
    r = lax.broadcasted_iota(jnp.int32, (2 * CHUNK, CHUNK), 0)
    s = lax.broadcasted_iota(jnp.int32, (2 * CHUNK, CHUNK), 1)
    return s <= (r & (CHUNK - 1))


def _sgu_chunk_fwd(u, v, g, wm, bias, lo):
    ug = _gelu(u)
    vg = _gelu(v)
    mu = _seg_mean(vg, lo)
    xc = vg - mu
    var = _seg_mean(xc * xc, lo)
    rstd = lax.rsqrt(var + LN_EPS)
    vn = xc * rstd
    vh = (vn * g).astype(BF16)
    mm2 = jnp.dot(wm, vh, preferred_element_type=F32)
    mixed = jnp.where(lo, mm2[:CHUNK], mm2[CHUNK:]) + bias
    return ug, vn, rstd, vh, mixed


def _mixer_fwd(proj, wconv, wpool_bd, pscale, lng, wsp, bias, layer):
    T = proj.shape[0]
    nchunk = T // CHUNK

    def body(a_ref, b_ref, c_ref, wc_ref, wp_ref, ps_ref, lng_ref, wsp_ref, bias_ref, o_ref):
        j = pl.program_id(0)

        @pl.when(j < 3)
        def _conv():
            z = c_ref[...] * a_ref[...]
            w = wc_ref[...]
            y = w[0:1] * _shift_down(z, 2) + w[1:2] * _shift_down(z, 1) + w[2:3] * z
            o_ref[...] = (b_ref[...] * y).astype(o_ref.dtype)

        @pl.when((j >= 3) & (j < 5))
        def _pool():
            d, _ = _pool_mean_minus_token(a_ref[...], j == 3)
            y = jnp.dot(d.astype(BF16), wp_ref[...].astype(BF16), preferred_element_type=F32)
            o_ref[...] = (y * ps_ref[...]).astype(o_ref.dtype)

        @pl.when(j >= 5)
        def _sgu():
            lo = _lo_mask((CHUNK, LANES))
            wm = jnp.where(_tril_keep(), wsp_ref[...], 0.0).astype(BF16)
            bias_t = bias_ref[...]
            g = lng_ref[...]

            def chunk(n, carry):
                rows = pl.ds(pl.multiple_of(n * CHUNK, CHUNK), CHUNK)
                ug, _, _, _, mixed = _sgu_chunk_fwd(a_ref[rows, :], b_ref[rows, :], g, wm, bias_t, lo)
                o_ref[rows, :] = (ug * mixed).astype(o_ref.dtype)
                return carry

            lax.fori_loop(0, nchunk, chunk, 0, unroll=SGU_UNROLL)

    def col(f):
        return lambda j: (0, f(j))

    clip = lambda v, lo, hi: jnp.minimum(jnp.maximum(v, lo), hi)
    return pl.pallas_call(
        body,
        name="mixer_fwd",
        grid=(8,),
        in_specs=[
            pl.BlockSpec((T, LANES), col(lambda j: jnp.where(j < 3, j, jnp.where(j < 5, j + 6, j + 6)))),
            pl.BlockSpec((T, LANES), col(lambda j: jnp.where(j < 3, j + 3, jnp.where(j < 5, 5, j + 9)))),
            pl.BlockSpec((T, LANES), col(lambda j: jnp.where(j < 3, j + 6, 8))),
            pl.BlockSpec((None, 3, LANES), lambda j: (layer, 0, clip(j, 0, 2))),
            pl.BlockSpec((None, None, LANES, LANES), lambda j: (layer, clip(j - 3, 0, 1), 0, 0)),
            pl.BlockSpec((None, 1, LANES), lambda j: (layer, 0, clip(j - 3, 0, 1))),
            pl.BlockSpec((None, 1, LANES), lambda j: (layer, 0, clip(j - 5, 0, 2))),
            pl.BlockSpec((None, None, 2 * CHUNK, CHUNK), lambda j: (layer, clip(j - 5, 0, 2), 0, 0)),
            pl.BlockSpec((None, None, CHUNK, LANES), lambda j: (layer, clip(j - 5, 0, 2), 0, 0)),
        ],
        out_specs=pl.BlockSpec((T, LANES), lambda j: (0, j)),
        out_shape=jax.ShapeDtypeStruct((T, D_MODEL), BF16),
        compiler_params=_cparams(("arbitrary",)),
    )(proj, proj, proj, wconv, wpool_bd, pscale, lng, wsp, bias)


def _mixer_bwd(proj, dmix, wconv, wpool_bd, pscale, lng, wsp, bias, layer, deps=()):
    T = proj.shape[0]
    nchunk = T // CHUNK

    def body(*refs):
        a_ref, b_ref, c_ref, dm_ref, wc_ref, wp_ref, ps_ref, lng_ref, wsp_ref, bias_ref = refs[:10]
        o_ref, dwc_ref, dwp_ref, dps_ref, dlng_ref, dwsp_ref, dbias_ref, keep1, keep2 = refs[10 + len(deps):]
        k = pl.program_id(0)

        @pl.when(k < 3)
        def _conv():
            xa, gb, gc, dya = a_ref[...], b_ref[...], c_ref[...], dm_ref[...]
            w = wc_ref[...]
            z = gc * xa
            z1 = _shift_down(z, 1)
            z2 = _shift_down(z, 2)
            y = w[0:1] * z2 + w[1:2] * z1 + w[2:3] * z
            dyv = dya * gb
            dz = w[2:3] * dyv + w[1:2] * _shift_up(dyv, 1) + w[0:1] * _shift_up(dyv, 2)
            dwc_ref[0:1, :] = jnp.sum(dyv * z2, axis=0, keepdims=True)
            dwc_ref[1:2, :] = jnp.sum(dyv * z1, axis=0, keepdims=True)
            dwc_ref[2:3, :] = jnp.sum(dyv * z, axis=0, keepdims=True)
            o_ref[...] = (dz * gc).astype(o_ref.dtype)
            keep1[k] = (dya * y).astype(keep1.dtype)
            keep1[k + 3] = (dz * xa).astype(keep1.dtype)

        @pl.when((k >= 3) & (k < 9))
        def _emit_gb_gc():
            o_ref[...] = keep1[k - 3]

        @pl.when((k >= 9) & (k < 11))
        def _pool():
            first = k == 9
            p, dyb = a_ref[...], dm_ref[...]
            d, count = _pool_mean_minus_token(p, first)
            w2 = wp_ref[...].astype(BF16)
            db = d.astype(BF16)
            y = jnp.dot(db, w2, preferred_element_type=F32)
            dps_ref[...] = jnp.sum(dyb * y, axis=0, keepdims=True)
            dyv = (dyb * ps_ref[...]).astype(BF16)
            dd = lax.dot_general(dyv, w2, _DN["nt"], preferred_element_type=F32)
            dwp_ref[...] = lax.dot_general(db, dyv, _DN["tn"], preferred_element_type=F32)
            dwin = dd / count
            a2 = dwin + _shift_up(dwin, 1)
            a4 = a2 + _shift_up(a2, 2)
            a8 = a4 + _shift_up(a4, 4)
            a16 = a8 + _shift_up(a8, 8)
            _, lo = _pool_windows(first)
            back = jnp.where(first, jnp.where(lo, a2, a4), jnp.where(lo, a8, a16))
            o_ref[...] = (back - dd).astype(o_ref.dtype)

        @pl.when((k >= 11) & (k < 14))
        def _sgu():
            lo = _lo_mask((CHUNK, LANES))
            keep = _tril_keep()
            wm = jnp.where(keep, wsp_ref[...], 0.0).astype(BF16)
            bias_t = bias_ref[...]
            g = lng_ref[...]
            dwsp_ref[...] = jnp.zeros_like(dwsp_ref)
            dbias_ref[...] = jnp.zeros_like(dbias_ref)
            dlng_ref[...] = jnp.zeros_like(dlng_ref)

            def chunk(n, carry):
                rows = pl.ds(pl.multiple_of(n * CHUNK, CHUNK), CHUNK)
                u, v, dyc = a_ref[rows, :], b_ref[rows, :], dm_ref[rows, :]
                ug, vn, rstd, vh, mixed = _sgu_chunk_fwd(u, v, g, wm, bias_t, lo)
                dmx = dyc * ug
                o_ref[rows, :] = (dyc * mixed * _gelu_grad(u)).astype(o_ref.dtype)
                dbias_ref[...] += dmx
                dst = jnp.concatenate([jnp.where(lo, dmx, 0.0), jnp.where(lo, 0.0, dmx)], axis=0).astype(BF16)
                dwsp_ref[...] += lax.dot_general(dst, vh, _DN["nt"], preferred_element_type=F32)
                dvh = lax.dot_general(wm, dst, _DN["tn"], preferred_element_type=F32)
                dlng_ref[...] += jnp.sum(dvh * vn, axis=0, keepdims=True)
                dvn = dvh * g
                m1 = _seg_mean(dvn, lo)
                m2 = _seg_mean(dvn * vn, lo)
                dvg = rstd * (dvn - m1 - vn * m2)
                keep2[k - 11, rows, :] = (dvg * _gelu_grad(v)).astype(keep2.dtype)
                return carry

            lax.fori_loop(0, nchunk, chunk, 0, unroll=SGU_UNROLL)
            dwsp_ref[...] = jnp.where(keep, dwsp_ref[...], 0.0)
            dbt = dbias_ref[...]
            lane = lax.broadcasted_iota(jnp.int32, (CHUNK, LANES), 1)
            sa = jnp.sum(jnp.where(lo, dbt, 0.0), axis=-1, keepdims=True)
            sb = jnp.sum(jnp.where(lo, 0.0, dbt), axis=-1, keepdims=True)
            dbias_ref[...] = jnp.where(lane == 0, sa, jnp.where(lane == 1, sb, 0.0))

        @pl.when(k >= 14)
        def _emit_v():
            o_ref[...] = keep2[k - 14]

    def col(f):
        return lambda k: (0, f(k))

    clip = lambda v, lo, hi: jnp.minimum(jnp.maximum(v, lo), hi)
    view_a = lambda k: jnp.where(k < 3, k, jnp.where(k < 9, 2, jnp.where(k < 14, k, 13)))
    view_b = lambda k: jnp.where(k < 3, k + 3, jnp.where(k < 11, 5, jnp.where(k < 14, k + 3, 16)))
    view_c = lambda k: jnp.where(k < 3, k + 6, 8)
    view_dm = lambda k: jnp.where(k < 3, k, jnp.where(k < 9, 2, jnp.where(k < 14, k - 6, 7)))
    return pl.pallas_call(
        body,
        name="mixer_bwd",
        grid=(17,),
        in_specs=[
            pl.BlockSpec((T, LANES), col(view_a)),
            pl.BlockSpec((T, LANES), col(view_b)),
            pl.BlockSpec((T, LANES), col(view_c)),
            pl.BlockSpec((T, LANES), col(view_dm)),
            pl.BlockSpec((None, 3, LANES), lambda k: (layer, 0, clip(k, 0, 2))),
            pl.BlockSpec((None, None, LANES, LANES), lambda k: (layer, clip(k - 9, 0, 1), 0, 0)),
            pl.BlockSpec((None, 1, LANES), lambda k: (layer, 0, clip(k - 9, 0, 1))),
            pl.BlockSpec((None, 1, LANES), lambda k: (layer, 0, clip(k - 11, 0, 2))),
            pl.BlockSpec((None, None, 2 * CHUNK, CHUNK), lambda k: (layer, clip(k - 11, 0, 2), 0, 0)),
            pl.BlockSpec((None, None, CHUNK, LANES), lambda k: (layer, clip(k - 11, 0, 2), 0, 0)),
        ] + [pl.BlockSpec(memory_space=pl.ANY)] * len(deps),
        out_specs=[
            pl.BlockSpec((T, LANES), lambda k: (0, k)),
            pl.BlockSpec((3, LANES), col(lambda k: clip(k, 0, 2))),
            pl.BlockSpec((None, LANES, LANES), lambda k: (clip(k - 9, 0, 1), 0, 0)),
            pl.BlockSpec((1, LANES), col(lambda k: clip(k - 9, 0, 1))),
            pl.BlockSpec((1, LANES), col(lambda k: clip(k - 11, 0, 2))),
            pl.BlockSpec((None, 2 * CHUNK, CHUNK), lambda k: (clip(k - 11, 0, 2), 0, 0)),
            pl.BlockSpec((None, CHUNK, LANES), lambda k: (clip(k - 11, 0, 2), 0, 0)),
        ],
        out_shape=[
            jax.ShapeDtypeStruct((T, IN_W), BF16),
            jax.ShapeDtypeStruct((3, CONV_W), F32),
            jax.ShapeDtypeStruct((2, LANES, LANES), F32),
            jax.ShapeDtypeStruct((1, POOL_W), F32),
            jax.ShapeDtypeStruct((1, SGU_W), F32),
            jax.ShapeDtypeStruct((3, 2 * CHUNK, CHUNK), F32),
            jax.ShapeDtypeStruct((3, CHUNK, LANES), F32),
        ],
        scratch_shapes=[pltpu.VMEM((6, T, LANES), BF16), pltpu.VMEM((3, T, LANES), BF16)],
        compiler_params=_cparams(("arbitrary",)),
    )(proj, proj, proj, dmix, wconv, wpool_bd, pscale, lng, wsp, bias, *deps)


def _loss_ln_bwd(xhat, rstd, g, b, target, tm=256):
    T = xhat.shape[0]

    def body(xhat_ref, rstd_ref, g_ref, b_ref, t_ref, loss_ref, dr_ref, drb_ref, dg_ref, db_ref):
        xhat_v = xhat_ref[...]
        err = xhat_v * g_ref[...] + b_ref[...] - t_ref[...]
        dy = err * (1.0 / D_MODEL)

        @pl.when(pl.program_id(0) == 0)
        def _():
            loss_ref[...] = jnp.zeros_like(loss_ref)
            dg_ref[...] = jnp.zeros_like(dg_ref)
            db_ref[...] = jnp.zeros_like(db_ref)

        part = jnp.sum(jnp.sum(err * err, axis=-1, keepdims=True), axis=0, keepdims=True)
        loss_ref[...] += jnp.broadcast_to(part * (0.5 / D_MODEL), loss_ref.shape)
        dg_ref[...] += jnp.sum(dy * xhat_v, axis=0, keepdims=True)
        db_ref[...] += jnp.sum(dy, axis=0, keepdims=True)
        dxh = dy * g_ref[...]
        m1 = jnp.mean(dxh, axis=-1, keepdims=True)
        m2 = jnp.mean(dxh * xhat_v, axis=-1, keepdims=True)
        dr = rstd_ref[...] * (dxh - m1 - xhat_v * m2)
        dr_ref[...] = dr
        drb_ref[...] = dr.astype(drb_ref.dtype)

    row = pl.BlockSpec((tm, D_MODEL), lambda i: (i, 0))
    vec = pl.BlockSpec((1, D_MODEL), lambda i: (0, 0))
    (g_arr, g_spec), (b_arr, b_spec) = _vec(g), _vec(b)
    return pl.pallas_call(
        body,
        name="loss_ln_bwd",
        grid=(T // tm,),
        in_specs=[row, pl.BlockSpec((tm, 1), lambda i: (i, 0)), g_spec, b_spec, row],
        out_specs=[pl.BlockSpec((8, LANES), lambda i: (0, 0)), row, row, vec, vec],
        out_shape=[jax.ShapeDtypeStruct((8, LANES), F32),
                   jax.ShapeDtypeStruct((T, D_MODEL), F32), jax.ShapeDtypeStruct((T, D_MODEL), BF16),
                   jax.ShapeDtypeStruct((1, D_MODEL), F32), jax.ShapeDtypeStruct((1, D_MODEL), F32)],
        compiler_params=_cparams(("arbitrary",)),
    )(xhat, rstd, g_arr, b_arr, target)


def _adamw(w, g, m, v, tr):
    R, C = w.shape[-2:]
    assert R % tr == 0
    c1 = 1.0 - ADAM_B1 ** ADAM_STEP
    c2 = 1.0 - ADAM_B2 ** ADAM_STEP

    def body(w_ref, g_ref, m_ref, v_ref, d_ref, mo_ref, vo_ref):
        gv = g_ref[...]
        mn = ADAM_B1 * m_ref[...] + (1.0 - ADAM_B1) * gv
        vn = ADAM_B2 * v_ref[...] + (1.0 - ADAM_B2) * (gv * gv)
        d_ref[...] = -ADAM_LR * ((mn / c1) / (jnp.sqrt(vn / c2) + ADAM_EPS) + ADAM_WD * w_ref[...])
        mo_ref[...] = mn
        vo_ref[...] = vn

    if w.ndim == 2:
        grid, blk = (R // tr,), pl.BlockSpec((tr, C), lambda i: (i, 0))
    else:
        grid, blk = (w.shape[0], R // tr), pl.BlockSpec((None, tr, C), lambda l, i: (l, i, 0))
    return pl.pallas_call(
        body, name="adamw", grid=grid, in_specs=[blk] * 4, out_specs=[blk] * 3,
        out_shape=[jax.ShapeDtypeStruct(w.shape, F32)] * 3, compiler_params=_cparams(("parallel",) * len(grid)),
    )(w, g, m, v)


def _my_place():
    return lax.axis_index("x"), lax.axis_index("y"), lax.axis_index("c")


ANY = pl.BlockSpec(memory_space=pl.ANY)
HBM = pl.BlockSpec(memory_space=pltpu.HBM)
SEM = pl.BlockSpec(memory_space=pltpu.SEMAPHORE)
EFFECT = pltpu.SideEffectType.DATAFLOW_SIDE_EFFECTING


def _in_hbm(a):
    return pltpu.with_memory_space_constraint(a, pltpu.HBM)


def _block_rows(ref, dev):
    r = ref.shape[0] // N_DEV
    start = pl.multiple_of((4 * dev[0] + 2 * dev[1] + dev[2]) * r, 16)
    return ref.at[pl.ds(start, r), :]


def _ag_first_copies(s_refs, land_refs, send_sems, recv_sems, receiving):
    x, y, c = _my_place()
    peers = [(x, y, 1 - c)] + [(*chip, c) for chip in _other_chips(x, y)]
    copies = []
    for k, peer in enumerate(peers):
        block = peer if receiving else (x, y, c)
        copies += [pltpu.make_async_remote_copy(
            src_ref=s_refs[w], dst_ref=_block_rows(land_refs[w], block),
            send_sem=send_sems.at[k * len(s_refs) + w], recv_sem=recv_sems.at[k * len(s_refs) + w],
            device_id=peer, device_id_type=MESH)
            for w in range(len(s_refs))]
    return copies


def _ag_start(shards, layer, after=()):
    nw = len(shards)

    def body(*refs):
        s_refs, land_refs = refs[:nw], refs[nw:2 * nw]
        token = refs[-1]
        sems = 2 * nw + len(after)
        for cp in _ag_first_copies(s_refs, land_refs, refs[sems], refs[sems + 1], False):
            cp.start()
        token[...] = jnp.zeros_like(token)

    lands = [lax.empty((N_DEV * s.shape[0], D_MODEL), BF16) for s in shards]
    out = pl.pallas_call(
        body, name="ag_start_%s" % layer,
        in_specs=[HBM] * (2 * nw) + [ANY] * len(after),
        out_specs=(SEM, SEM, *[HBM] * (2 * nw), pl.BlockSpec(memory_space=pltpu.VMEM)),
        out_shape=(pltpu.SemaphoreType.DMA((4 * nw,)), pltpu.SemaphoreType.DMA((4 * nw,)),
                   *[pltpu.HBM(a.shape, a.dtype) for a in list(shards) + lands],
                   jax.ShapeDtypeStruct((8, LANES), F32)),
        input_output_aliases={i: 2 + i for i in range(2 * nw)},
        compiler_params=pltpu.CompilerParams(has_side_effects=EFFECT),
    )(*[_in_hbm(a) for a in list(shards) + lands], *after)
    return out[0], out[1], out[2:2 + nw], out[2 + nw:2 + 2 * nw], out[-1]


def _ag_wait(send_sems, recv_sems, shards, lands, after, layer):
    nw = len(shards)

    def body(*refs):
        s_refs, land_refs = refs[:nw], refs[nw:2 * nw]
        for cp in _ag_first_copies(s_refs, land_refs, refs[2 * nw], refs[2 * nw + 1], True):
            cp.wait_send()
            cp.wait_recv()

    out = pl.pallas_call(
        body, name="ag_wait_%s" % layer,
        in_specs=[HBM] * (2 * nw) + [SEM, SEM] + [ANY] * len(after),
        out_specs=[HBM] * (2 * nw),
        out_shape=[pltpu.HBM(a.shape, a.dtype) for a in list(shards) + list(lands)],
        input_output_aliases={i: i for i in range(2 * nw)},
        compiler_params=pltpu.CompilerParams(has_side_effects=EFFECT),
    )(*shards, *lands, send_sems, recv_sems, *after)
    return out[:nw], out[nw:]


def _ag_pass_on(shards, lands):
    nw = len(shards)

    def body(*refs):
        s_refs, g_refs = refs[:nw], refs[2 * nw:3 * nw]
        send_sems, recv_sems, local_sems = refs[3 * nw:3 * nw + 3]
        stage = refs[3 * nw + 3:]
        x, y, c = _my_place()
        load = [pltpu.make_async_copy(s_refs[w], stage[w], local_sems.at[w]) for w in range(nw)]
        mine = [pltpu.make_async_copy(stage[w], _block_rows(g_refs[w], (x, y, c)), local_sems.at[w])
                for w in range(nw)]
        for cp in load:
            cp.start()
        sends, arrivals = [], []
        for j, chip in enumerate(_other_chips(x, y)):
            for w in range(nw):
                rows_out = _block_rows(g_refs[w], (*chip, c))
                rows_in = _block_rows(g_refs[w], (*chip, 1 - c))
                sends.append(pltpu.make_async_remote_copy(
                    src_ref=rows_out, dst_ref=rows_out, send_sem=send_sems.at[j, w], recv_sem=recv_sems.at[j, w],
                    device_id=(x, y, 1 - c), device_id_type=MESH))
                arrivals.append(pltpu.make_async_remote_copy(
                    src_ref=rows_in, dst_ref=rows_in, send_sem=send_sems.at[j, w], recv_sem=recv_sems.at[j, w],
                    device_id=(x, y, 1 - c), device_id_type=MESH))
        for cp in sends:
            cp.start()
        for w in range(nw):
            load[w].wait()
            mine[w].start()
        for cp in arrivals:
            cp.wait_recv()
        for cp in sends:
            cp.wait_send()
        for cp in mine:
            cp.wait()

    return pl.pallas_call(
        body, name="ag_pass_on",
        in_specs=[ANY] * (2 * nw), out_specs=[ANY] * nw,
        out_shape=[jax.ShapeDtypeStruct(a.shape, a.dtype) for a in lands],
        input_output_aliases={nw + i: i for i in range(nw)},
        scratch_shapes=[pltpu.SemaphoreType.DMA((3, nw)), pltpu.SemaphoreType.DMA((3, nw)),
                        pltpu.SemaphoreType.DMA((nw,))] + [pltpu.VMEM(s.shape, s.dtype) for s in shards],
        compiler_params=_cparams(),
    )(*shards, *lands)


def _rs_sibling_copies(p_refs, land_refs, send_sems, recv_sems):
    x, y, c = _my_place()
    return [pltpu.make_async_remote_copy(
        src_ref=p_refs[w].at[:, 1 - c], dst_ref=land_refs[w],
        send_sem=send_sems.at[w], recv_sem=recv_sems.at[w], device_id=(x, y, 1 - c), device_id_type=MESH)
        for w in range(len(p_refs))]


def _rs_sibling_start(parts, tag, after=()):
    nw = len(parts)
    sems = 2 * nw + len(after)

    def body(*refs):
        for cp in _rs_sibling_copies(refs[:nw], refs[nw:2 * nw], refs[sems], refs[sems + 1]):
            cp.start()
        refs[-1][...] = jnp.zeros_like(refs[-1])

    lands = [lax.empty(p.shape[:1] + p.shape[2:], BF16) for p in parts]
    out = pl.pallas_call(
        body, name="rs_sibling_start_%s" % tag,
        in_specs=[HBM] * (2 * nw) + [ANY] * len(after),
        out_specs=(SEM, SEM, *[HBM] * (2 * nw), pl.BlockSpec(memory_space=pltpu.VMEM)),
        out_shape=(pltpu.SemaphoreType.DMA((nw,)), pltpu.SemaphoreType.DMA((nw,)),
                   *[pltpu.HBM(a.shape, a.dtype) for a in list(parts) + lands],
                   jax.ShapeDtypeStruct((8, LANES), F32)),
        input_output_aliases={i: 2 + i for i in range(2 * nw)},
        compiler_params=pltpu.CompilerParams(has_side_effects=EFFECT),
    )(*[_in_hbm(a) for a in list(parts) + lands], *after)
    return out[0], out[1], out[2:2 + nw], out[2 + nw:2 + 2 * nw], out[-1]


def _rs_sibling_wait(send_sems, recv_sems, parts, lands, after, tag):
    nw = len(parts)

    def body(*refs):
        for cp in _rs_sibling_copies(refs[:nw], refs[nw:2 * nw], refs[2 * nw], refs[2 * nw + 1]):
            cp.wait_send()
            cp.wait_recv()

    out = pl.pallas_call(
        body, name="rs_sibling_wait_%s" % tag,
        in_specs=[HBM] * (2 * nw) + [SEM, SEM] + [ANY] * len(after),
        out_specs=[HBM] * (2 * nw),
        out_shape=[pltpu.HBM(a.shape, a.dtype) for a in list(parts) + list(lands)],
        input_output_aliases={i: i for i in range(2 * nw)},
        compiler_params=pltpu.CompilerParams(has_side_effects=EFFECT),
    )(*parts, *lands, send_sems, recv_sems, *after)
    return out[:nw], out[nw:]


def _rs_chip_sum(parts, gots, c):
    n = len(parts)

    def body(c_ref, *refs):
        for p_ref, g_ref, o_ref in zip(refs[:n], refs[n:2 * n], refs[2 * n:]):
            o_ref[...] = (p_ref[...].astype(F32) + g_ref[...].astype(F32)).astype(o_ref.dtype)

    mine = [pl.BlockSpec((None, None, p.shape[2], D_MODEL), lambda q, c_ref: (q, c_ref[0], 0, 0)) for p in parts]
    theirs = [pl.BlockSpec((None, g.shape[1], D_MODEL), lambda q, c_ref: (q, 0, 0)) for g in gots]
    return pl.pallas_call(
        body, name="rs_chip_sum",
        grid_spec=pltpu.PrefetchScalarGridSpec(
            num_scalar_prefetch=1, grid=(4,), in_specs=mine + theirs, out_specs=theirs),
        out_shape=[jax.ShapeDtypeStruct(g.shape, BF16) for g in gots],
        compiler_params=_cparams(("parallel",)),
    )(c, *parts, *gots)


def _other_chips(x, y):
    return [(1 - x, y), (x, 1 - y), (1 - x, 1 - y)]


def _rs_chip_copies(s_refs, land_refs, send_sems, recv_sems):
    x, y, c = _my_place()
    copies = []
    for k, chip in enumerate(_other_chips(x, y)):
        q = 2 * chip[0] + chip[1]
        copies += [pltpu.make_async_remote_copy(
            src_ref=s_refs[w].at[q], dst_ref=land_refs[w].at[k],
            send_sem=send_sems.at[k * len(s_refs) + w], recv_sem=recv_sems.at[k * len(s_refs) + w],
            device_id=(*chip, c), device_id_type=MESH)
            for w in range(len(s_refs))]
    return copies


def _rs_chip_start(sums, layer):
    nw = len(sums)

    def body(*refs):
        s_refs, land_refs = refs[:nw], refs[nw:2 * nw]
        send_sems, recv_sems = refs[2 * nw], refs[2 * nw + 1]
        token = refs[-1]
        for cp in _rs_chip_copies(s_refs, land_refs, send_sems, recv_sems):
            cp.start()
        token[...] = jnp.zeros_like(token)

    lands = [lax.empty((3,) + s.shape[1:], BF16) for s in sums]
    out = pl.pallas_call(
        body, name="rs_chip_start_%s" % layer,
        in_specs=[HBM] * (2 * nw),
        out_specs=(SEM, SEM, *[HBM] * (2 * nw), pl.BlockSpec(memory_space=pltpu.VMEM)),
        out_shape=(pltpu.SemaphoreType.DMA((3 * nw,)), pltpu.SemaphoreType.DMA((3 * nw,)),
                   *[pltpu.HBM(a.shape, a.dtype) for a in list(sums) + lands],
                   jax.ShapeDtypeStruct((8, LANES), F32)),
        input_output_aliases={i: 2 + i for i in range(2 * nw)},
        compiler_params=pltpu.CompilerParams(has_side_effects=EFFECT),
    )(*[_in_hbm(a) for a in list(sums) + lands])
    return out[0], out[1], out[2:2 + nw], out[2 + nw:2 + 2 * nw], out[-1]


def _rs_chip_wait(send_sems, recv_sems, sums, lands, after, layer):
    nw = len(sums)

    def body(*refs):
        s_refs, land_refs = refs[:nw], refs[nw:2 * nw]
        for cp in _rs_chip_copies(s_refs, land_refs, refs[2 * nw], refs[2 * nw + 1]):
            cp.wait_send()
            cp.wait_recv()

    out = pl.pallas_call(
        body, name="rs_chip_wait_%s" % layer,
        in_specs=[HBM] * (2 * nw) + [SEM, SEM] + [ANY] * len(after),
        out_specs=[HBM] * (2 * nw),
        out_shape=[pltpu.HBM(a.shape, a.dtype) for a in list(sums) + list(lands)],
        input_output_aliases={i: i for i in range(2 * nw)},
        compiler_params=pltpu.CompilerParams(has_side_effects=EFFECT),
    )(*sums, *lands, send_sems, recv_sems, *after)
    return out[:nw], out[nw:]


def _rs_finish(sums, gots, q, layer, into):
    n = len(sums)

    def body(q_ref, *refs):
        for s_ref, g_ref, o_ref in zip(refs[:n], refs[n:2 * n], refs[len(refs) - n:]):
            o_ref[...] = ((s_ref[...].astype(F32) + g_ref[0].astype(F32)) + g_ref[1].astype(F32)) + g_ref[2].astype(F32)

    rows = [s.shape[1] for s in sums]
    in_specs = [pl.BlockSpec((None, r, D_MODEL), lambda i, q_ref: (q_ref[0], 0, 0)) for r in rows]
    in_specs += [pl.BlockSpec((3, r, D_MODEL), lambda i, q_ref: (0, 0, 0)) for r in rows]
    args = [q, *sums, *gots]
    aliases = {}
    if into is not None:
        in_specs += [ANY] * n
        aliases = {len(args) + i: i for i in range(n)}
        args += list(into)
    return pl.pallas_call(
        body, name="rs_finish",
        grid_spec=pltpu.PrefetchScalarGridSpec(
            num_scalar_prefetch=1, grid=(1,), in_specs=in_specs,
            out_specs=[pl.BlockSpec((None, r, D_MODEL), lambda i, q_ref: (layer, 0, 0)) for r in rows]),
        out_shape=[jax.ShapeDtypeStruct((DEPTH, r, D_MODEL), F32) for r in rows],
        input_output_aliases=aliases,
        compiler_params=_cparams(("arbitrary",)),
    )(*args)


def _allreduce_small(vec, deps=()):
    R = vec.shape[0]
    assert R % (8 * N_DEV) == 0
    P = R // N_DEV
    nd = len(deps)

    def body(*refs):
        v_ref = refs[0]
        o_ref, buf, send1, recv1, send2, recv2 = refs[1 + nd:]
        x, y, c = _my_place()
        me = 4 * x + 2 * y + c

        def piece(ref, d):
            return ref.at[pl.ds(pl.multiple_of(d * P, 8), P), :]

        def peer(k):
            p = me ^ k
            return p, (p >> 2, (p >> 1) & 1, p & 1)

        scatter = []
        for k in range(1, N_DEV):
            p, where = peer(k)
            scatter.append(pltpu.make_async_remote_copy(
                src_ref=piece(v_ref, p), dst_ref=buf.at[k], send_sem=send1.at[k - 1], recv_sem=recv1.at[k - 1],
                device_id=where, device_id_type=MESH))
        for cp in scatter:
            cp.start()
        buf[0] = piece(v_ref, me)[...]
        for cp in scatter:
            cp.wait()
        acc = buf[me]
        for d in range(1, N_DEV):
            acc = acc + buf[me ^ d]
        piece(o_ref, me)[...] = acc
        spread, arrivals = [], []
        for k in range(1, N_DEV):
            p, where = peer(k)
            spread.append(pltpu.make_async_remote_copy(
                src_ref=piece(o_ref, me), dst_ref=piece(o_ref, me), send_sem=send2.at[k - 1], recv_sem=recv2.at[k - 1],
                device_id=where, device_id_type=MESH))
            arrivals.append(pltpu.make_async_remote_copy(
                src_ref=piece(o_ref, p), dst_ref=piece(o_ref, p), send_sem=send2.at[k - 1], recv_sem=recv2.at[k - 1],
                device_id=where, device_id_type=MESH))
        for cp in spread:
            cp.start()
        for cp in arrivals:
            cp.wait_recv()
        for cp in spread:
            cp.wait_send()

    sems = pltpu.SemaphoreType.DMA((N_DEV - 1,))
    return pl.pallas_call(
        body, name="allreduce_small",
        in_specs=[pl.BlockSpec(memory_space=pltpu.VMEM)] + [ANY] * nd, out_specs=pl.BlockSpec(memory_space=pltpu.VMEM),
        out_shape=jax.ShapeDtypeStruct((R, LANES), F32),
        scratch_shapes=[pltpu.VMEM((N_DEV, P, LANES), F32), sems, sems, sems, sems],
        compiler_params=_cparams(),
    )(vec, *deps)


def _pack(arrs):
    flat = jnp.concatenate([a.reshape(-1) for a in arrs])
    pad = (-flat.shape[0]) % (8 * N_DEV * LANES)
    return jnp.pad(flat, (0, pad)).reshape(-1, LANES)


def _unpack(packed, shapes):
    flat = packed.reshape(-1)
    out, off = [], 0
    for s in shapes:
        n = math.prod(s)
        out.append(flat[off:off + n].reshape(s))
        off += n
    return out


def kernel(x, w_in, w_conv, w_pool, pool_scale, sgu_ln_g, w_spatial, b_spatial, w_o, ln1_g, ln1_b, w_gate_up, w_down, ln2_g, ln2_b, loss_target, m_w_in, m_w_conv, m_w_pool, m_pool_scale, m_sgu_ln_g, m_w_spatial, m_b_spatial, m_w_o, m_ln1_g, m_ln1_b, m_w_gate_up, m_w_down, m_ln2_g, m_ln2_b, v_w_in, v_w_conv, v_w_pool, v_pool_scale, v_sgu_ln_g, v_w_spatial, v_b_spatial, v_w_o, v_ln1_g, v_ln1_b, v_w_gate_up, v_w_down, v_ln2_g, v_ln2_b):
    L = DEPTH
    T = x.shape[1]
    mx, my, mc = _my_place()
    dev = 4 * mx + 2 * my + mc
    xs = x[0]
    target = loss_target[0]

    conv_cols = w_conv.shape[2]
    w_conv_z = lax.dynamic_update_slice(jnp.zeros((L, 3, CONV_W), F32), w_conv, (0, 0, dev * conv_cols))
    w_conv_packed = _allreduce_small(_pack([w_conv_z]))
    w_conv_full = _unpack(w_conv_packed, [(L, 3, CONV_W)])[0]

    shards = (jnp.swapaxes(w_in, 1, 2).astype(BF16), jnp.swapaxes(w_gate_up, 1, 2).astype(BF16),
              w_o.astype(BF16), w_down.astype(BF16))
    first_gather = _ag_start_layer(shards, 0, [w_conv_packed])

    grad_x2, big_grads, small_grads = _local_step(
        xs, target, shards, first_gather, w_conv_full, w_pool, pool_scale, sgu_ln_g, w_spatial, b_spatial,
        ln1_g, ln1_b, ln2_g, ln2_b)
    grad_x = grad_x2[None]
    big_w = (w_in, w_gate_up, w_o, w_down)
    big_m = (m_w_in, m_w_gate_up, m_w_o, m_w_down)
    big_v = (v_w_in, v_w_gate_up, v_w_o, v_w_down)
    small_w = [w_conv_full, w_pool, pool_scale, sgu_ln_g, w_spatial, b_spatial, ln1_g, ln1_b, ln2_g, ln2_b]
    small_m = [m_w_conv, m_w_pool, m_pool_scale, m_sgu_ln_g, m_w_spatial, m_b_spatial, m_ln1_g, m_ln1_b, m_ln2_g, m_ln2_b]
    small_v = [v_w_conv, v_w_pool, v_pool_scale, v_sgu_ln_g, v_w_spatial, v_b_spatial, v_ln1_g, v_ln1_b, v_ln2_g, v_ln2_b]
    loss, grads, deltas, new_m, new_v = _reduce_and_update(
        big_grads, small_grads, big_w, big_m, big_v, small_w, small_m, small_v)
    return (loss, grad_x, *grads, *deltas, *new_m, *new_v)


def _ag_start_layer(shards, l, after):
    s_in, s_gu, s_o, s_dn = [s[l] for s in shards]
    first = _ag_start([s_in, s_o], "%da" % l, after=after)
    return first, _ag_start([s_gu, s_dn], "%db" % l, after=[first[4]])


def _ag_finish(gather, after, tag):
    send_sems, recv_sems, shards, lands, _ = gather
    shards, lands = _ag_wait(send_sems, recv_sems, shards, lands, after, tag)
    return _ag_pass_on(shards, lands)


def _rs_begin(parts, tag, after=()):
    return _rs_sibling_start([p.reshape(4, 2, p.shape[0] // N_DEV, D_MODEL) for p in parts], tag, after)


def _rs_continue(sibling_flight, after, c_arr, tag):
    send_sems, recv_sems, parts, lands, _ = sibling_flight
    parts, got = _rs_sibling_wait(send_sems, recv_sems, parts, lands, after, tag)
    return _rs_chip_start(_rs_chip_sum(parts, got, c_arr), tag)


def _local_step(xs, target, shards, gather, w_conv_full, w_pool, pool_scale, sgu_ln_g, w_spatial, b_spatial,
                ln1_g, ln1_b, ln2_g, ln2_b):
    L = DEPTH
    T = xs.shape[0]
    mx, my, mc = _my_place()
    c_arr = jnp.reshape(mc, (1,)).astype(jnp.int32)
    q_arr = jnp.reshape(2 * mx + my, (1,)).astype(jnp.int32)
    eye2 = jnp.eye(2, dtype=F32)
    wp = w_pool.reshape(L, 2, 2, HALF, HALF)
    wpool_bd = jnp.einsum("ltgcd,gh->ltgchd", wp, eye2).reshape(L, 2, LANES, LANES)
    wsp_t = w_spatial.reshape(L, 3, 2 * CHUNK, CHUNK)
    bias_t = jnp.repeat(jnp.swapaxes(b_spatial.reshape(L, 3, 2, CHUNK), 2, 3), HALF, axis=3)
    mixer_w = (w_conv_full, wpool_bd, pool_scale[:, None, :], sgu_ln_g[:, None, :], wsp_t, bias_t)
    g1, b1, g2, b2 = [a[:, None, :] for a in (ln1_g, ln1_b, ln2_g, ln2_b)]
    one, zero = jnp.ones((1, 1, D_MODEL), F32), jnp.zeros((1, 1, D_MODEL), F32)

    saved = []
    prev, pg, pb = xs, (one, 0), (zero, 0)
    prev_b = xs.astype(BF16)
    weights = []
    for l in range(L):
        g_in, g_o = _ag_finish(gather[0], [] if l == 0 else [prev_b], "%da" % l)
        proj = _mm(prev_b, g_in, "nt", F32, 512, IN_W, "mm_proj", deps=[gather[1][4]] if l == 0 else [])
        mixcat = _mixer_fwd(proj, *mixer_w, l)
        xhat1, rstd1, h_b = _mm_ln_fwd(mixcat, g_o, prev, pg, pb, (g1, l), (b1, l), "mm_wo_ln")
        g_gu, g_dn = _ag_finish(gather[1], [h_b], "%db" % l)
        weights.append((g_in, g_gu, g_o, g_dn))
        deps = []
        if l + 1 < L:
            gather = _ag_start_layer(shards, l + 1, [g_gu])
            deps = [gather[1][4]]
        g_act, u_act, act = _mm_swiglu_fwd(h_b, g_gu, deps=deps)
        xhat2, rstd2, y_b = _mm_ln_fwd(act, g_dn, xhat1, (g1, l), (b1, l), (g2, l), (b2, l), "mm_down_ln")
        saved.append((prev_b, proj, mixcat, xhat1, rstd1, h_b, g_act, u_act, act, xhat2, rstd2))
        prev, pg, pb, prev_b = xhat2, (g2, l), (b2, l), y_b


    small = [None] * L
    big = None
    sibling_flight = None
    above = None
    for l in reversed(range(L)):
        prev_b, proj, mixcat, xhat1, rstd1, h_b, g_act, u_act, act, xhat2, rstd2 = saved[l]
        g_in, g_gu, g_o, g_dn = weights[l]
        chip_flight = None
        if above is None:
            loss_tile, dr2, dr2_b, dg2, db2 = _loss_ln_bwd(xhat2, rstd2, (g2, l), (b2, l), target)
        else:
            dr2, dr2_b, dg2, db2 = _mm_ln_bwd([above[0]], above[1], above[2], xhat2, rstd2, (g2, l),
                                              "mm_dx_ln", deps=[sibling_flight[4]])
            chip_flight = _rs_continue(sibling_flight, [dr2_b], c_arr, str(l + 1))
        dg_b, du_b = _mm_swiglu_bwd(dr2_b, g_dn, g_act, u_act, deps=[chip_flight[4]] if chip_flight else [])
        p_dn = _mm(act, dr2_b, "tn", BF16, DW_TM, D_MODEL, "mm_dw_down")
        p_gu = _mm_tn_pair(dg_b, du_b, h_b, DW_TM, "mm_dw_gate_up")
        ffn_sibling = _rs_begin([p_gu, p_dn], "0b") if l == 0 else None
        dr1, dr1_b, dg1, db1, dmix = _mm_ln_bwd([dg_b, du_b], g_gu, dr2, xhat1, rstd1, (g1, l), "mm_dh_ln",
                                                deps=[ffn_sibling[4]] if l == 0 else [], w_back=g_o)
        ffn_flight = _rs_continue(ffn_sibling, [dr1_b], c_arr, "0b") if l == 0 else None
        p_o = _mm(mixcat, dr1_b, "tn", BF16, 512, D_MODEL, "mm_dw_o")
        dproj, dwc, dwp, dps, dlng, dwsp, dbias = _mixer_bwd(proj, dmix, *mixer_w, l,
                                                             deps=[ffn_flight[4]] if l == 0 else [])
        p_in = _mm(dproj, prev_b, "tn", BF16, IN_W, D_MODEL // 2, "mm_dw_in")
        small[l] = (dwc, dwp, dps, dlng, dwsp, dbias, dg1, db1, dg2, db2)
        above = (dproj, g_in, dr1)
        if chip_flight is not None:
            big = list(_rs_chip_finish(chip_flight, [p_in], q_arr, str(l + 1), l + 1, big))
        if l > 0:
            sibling_flight = _rs_begin([p_in, p_gu, p_o, p_dn], str(l))
        else:
            big[1], big[3] = _rs_chip_finish(ffn_flight, [p_in, p_o], q_arr, "0b", 0, [big[1], big[3]])

    def stack(i):
        return jnp.stack([small[l][i] for l in range(L)])

    dwp_bd = stack(1).reshape(L, 2, 2, HALF, 2, HALF)
    dwp_all = jnp.einsum("ltgchd,gh->ltgcd", dwp_bd, eye2).reshape(L, 4, HALF, HALF)
    dbs_all = jnp.swapaxes(stack(5)[:, :, :, :2], 2, 3).reshape(L, 6, CHUNK)
    small_grads = [stack(0), dwp_all, stack(2).reshape(L, POOL_W), stack(3).reshape(L, SGU_W),
                   stack(4).reshape(L, 6, CHUNK, CHUNK), dbs_all] + [stack(i).reshape(L, D_MODEL) for i in (6, 7, 8, 9)]
    small_grads.append(loss_tile[0, :1])
    packed_small = _allreduce_small(_pack(small_grads), deps=[big[1]])
    sibling_flight = _rs_begin([p_in, p_o], "0a", after=[packed_small])
    grad_x = _mm_ln_bwd([above[0]], above[1], above[2], None, None, None, "mm_dx_out", deps=[sibling_flight[4]])
    last_flight = _rs_continue(sibling_flight, [grad_x], c_arr, "0a")
    return grad_x, (big, last_flight, q_arr), (packed_small, [a.shape for a in small_grads])


def _rs_chip_finish(in_flight, after, q, tag, layer, into):
    send_sems, recv_sems, sums, lands, _ = in_flight
    sums, got = _rs_chip_wait(send_sems, recv_sems, sums, lands, after, tag)
    return _rs_finish(sums, got, q, layer, into)


def _reduce_and_update(big_grads, small_grads, big_w, big_m, big_v, small_w, small_m, small_v):
    L = DEPTH
    mx, my, mc = _my_place()
    dev = 4 * mx + 2 * my + mc
    conv_cols = CONV_W // N_DEV
    w_in, w_gate_up, w_o, w_down = big_w
    m_w_in, m_w_gate_up, m_w_o, m_w_down = big_m
    v_w_in, v_w_gate_up, v_w_o, v_w_down = big_v
    packed_g, small_shapes = small_grads
    big, last_flight, q_arr = big_grads

    def widen_conv(a):
        return lax.dynamic_update_slice(jnp.zeros((L, 3, CONV_W), F32), a, (0, 0, dev * conv_cols))

    small_m = [widen_conv(small_m[0])] + list(small_m[1:])
    small_v = [widen_conv(small_v[0])] + list(small_v[1:])
    pk_d, pk_m, pk_v = _adamw(_pack(small_w), packed_g, _pack(small_m), _pack(small_v), packed_g.shape[0] // 2)
    sg = _unpack(packed_g, small_shapes)
    sd = _unpack(pk_d, small_shapes)
    sm = _unpack(pk_m, small_shapes)
    sv = _unpack(pk_v, small_shapes)

    def conv_cols_of(a):
        return lax.dynamic_slice(a, (0, 0, dev * conv_cols), (L, 3, conv_cols))

    for lst in (sg, sd, sm, sv):
        lst[0] = conv_cols_of(lst[0])

    tr = lambda a: jnp.swapaxes(a, 1, 2)
    gt_gu, g_w_dn = big[1], big[3]
    d_gu, m_gu, v_gu = [tr(a) for a in _adamw(tr(w_gate_up), gt_gu, tr(m_w_gate_up), tr(v_w_gate_up), gt_gu.shape[1] // 2)]
    d_dn, m_dn, v_dn = _adamw(w_down, g_w_dn, m_w_down, v_w_down, w_down.shape[1])
    gt_in, g_w_o = _rs_chip_finish(last_flight, [d_gu, d_dn, pk_d], q_arr, "0a", 0, [big[0], big[2]])
    d_in, m_in, v_in = [tr(a) for a in _adamw(tr(w_in), gt_in, tr(m_w_in), tr(v_w_in), gt_in.shape[1])]
    d_o, m_o, v_o = _adamw(w_o, g_w_o, m_w_o, v_w_o, w_o.shape[1])
    g_w_in, g_w_gu = tr(gt_in), tr(gt_gu)

    def ordered(big_in, big_o, big_gu, big_dn, sm_list):
        return [big_in, sm_list[0], sm_list[1], sm_list[2], sm_list[3], sm_list[4], sm_list[5], big_o,
                sm_list[6], sm_list[7], big_gu, big_dn, sm_list[8], sm_list[9]]

    grads = ordered(g_w_in, g_w_o, g_w_gu, g_w_dn, sg)
    deltas = ordered(d_in, d_o, d_gu, d_dn, sd)
    new_m = ordered(m_in, m_o, m_gu, m_dn, sm)
    new_v = ordered(v_in, v_o, v_gu, v_dn, sv)
    return sg[10][0], grads, deltas, new_m, new_v
```

```python
import math

import jax
import jax.numpy as jnp
from jax import lax
from jax.experimental import pallas as pl
from jax.experimental.pallas import tpu as pltpu

F32 = jnp.float32
BF16 = jnp.bfloat16
MESH = pl.DeviceIdType.MESH

D_MODEL = 1024
DEPTH = 4
CONV_W = 384
POOL_W = 256
SGU_W = 384
IN_W = 3 * CONV_W + POOL_W + 2 * SGU_W
D_FF = 2816
CHUNK = 128
ALPHA = float((2 * DEPTH) ** 0.25)
LN_EPS = 1e-5
ADAM_LR, ADAM_B1, ADAM_B2, ADAM_EPS, ADAM_WD, ADAM_STEP = 0.001, 0.9, 0.999, 1e-08, 0.01, 10

N_DEV = 8
LANES = 128
HALF = 64
VMEM_LIMIT = 52 * 1024 * 1024

INV_SQRT2 = 0.7071067811865476
INV_SQRT_2PI = 0.3989422804014327


def _cparams(sem=None, **kw):
    if sem is not None:
        kw["dimension_semantics"] = sem
    return pltpu.CompilerParams(vmem_limit_bytes=VMEM_LIMIT, **kw)


_DN = {"nt": (((1,), (1,)), ((), ())), "tn": (((0,), (0,)), ((), ()))}


def _mm(a, b, mode, out_dtype, tm, tn, name, deps=(), lane_tiles=False):
    if mode == "nt":
        (M, K), N = a.shape, b.shape[0]
        a_spec = pl.BlockSpec((tm, K), lambda i, j: (i, 0))
        b_spec = pl.BlockSpec((tn, K), lambda i, j: (j, 0))
    else:
        (K, M), N = a.shape, b.shape[1]
        a_spec = pl.BlockSpec((K, tm), lambda i, j: (0, i))
        b_spec = pl.BlockSpec((K, tn), lambda i, j: (0, j))
    assert M % tm == 0 and N % tn == 0, (M, N, K, tm, tn)
    nd = len(deps)

    def body(*refs):
        a_ref, b_ref, o_ref = refs[0], refs[1], refs[2 + nd]
        res = lax.dot_general(a_ref[...], b_ref[...], _DN[mode], preferred_element_type=F32).astype(o_ref.dtype)
        if lane_tiles:
            _store_lane_tiles(o_ref, slice(None), res)
        else:
            o_ref[...] = res

    if lane_tiles:
        assert tn == N
        out_spec = pl.BlockSpec((N // LANES, tm, LANES), lambda i, j: (0, i, 0))
        out_shape = jax.ShapeDtypeStruct((N // LANES, M, LANES), out_dtype)
    else:
        out_spec = pl.BlockSpec((tm, tn), lambda i, j: (i, j))
        out_shape = jax.ShapeDtypeStruct((M, N), out_dtype)
    return pl.pallas_call(
        body,
        name=name,
        grid=(M // tm, N // tn),
        in_specs=[a_spec, b_spec] + [pl.BlockSpec(memory_space=pl.ANY)] * nd,
        out_specs=out_spec,
        out_shape=out_shape,
        compiler_params=_cparams(("parallel", "parallel")),
    )(a, b, *deps)


def _store_lane_tiles(o_ref, rows, value):
    for c in range(value.shape[1] // LANES):
        o_ref[c, rows, :] = value[:, c * LANES:(c + 1) * LANES]


def _mm_tn_pair(a1, a2, b, tm, name):
    K, M = a1.shape
    N = b.shape[1]
    n1 = M // tm

    def body(a1_ref, a2_ref, b_ref, o_ref):
        i = pl.program_id(0)

        @pl.when(i < n1)
        def _():
            o_ref[...] = lax.dot_general(a1_ref[...], b_ref[...], _DN["tn"], preferred_element_type=F32).astype(o_ref.dtype)

        @pl.when(i >= n1)
        def _():
            o_ref[...] = lax.dot_general(a2_ref[...], b_ref[...], _DN["tn"], preferred_element_type=F32).astype(o_ref.dtype)

    return pl.pallas_call(
        body, name=name, grid=(2 * n1,),
        in_specs=[pl.BlockSpec((K, tm), lambda i: (0, jnp.minimum(i, n1 - 1))),
                  pl.BlockSpec((K, tm), lambda i: (0, jnp.maximum(i - n1, 0))),
                  pl.BlockSpec((K, N), lambda i: (0, 0))],
        out_specs=pl.BlockSpec((tm, N), lambda i: (i, 0)),
        out_shape=jax.ShapeDtypeStruct((2 * M, N), BF16),
        compiler_params=_cparams(("arbitrary",)),
    )(a1, a2, b)


LN_SUB = 256
LN_TM = 512


def _vec(v):
    arr, layer = v
    return arr, pl.BlockSpec((None, 1, D_MODEL), lambda *_: (layer, 0, 0))


def _mm_ln_fwd(a, b, prev, pg, pb, g, bias, name):
    T, K = a.shape
    tm = LN_TM

    def body(a_ref, b_ref, prev_ref, pg_ref, pb_ref, g_ref, bias_ref, xhat_ref, rstd_ref, y_ref):
        for s in range(tm // LN_SUB):
            rows = slice(s * LN_SUB, (s + 1) * LN_SUB)
            mm = jnp.dot(a_ref[rows, :], b_ref[...], preferred_element_type=F32)
            r = ALPHA * (prev_ref[rows, :] * pg_ref[...] + pb_ref[...]) + mm
            mu = jnp.mean(r, axis=-1, keepdims=True)
            xc = r - mu
            var = jnp.mean(xc * xc, axis=-1, keepdims=True)
            rstd = lax.rsqrt(var + LN_EPS)
            xhat = xc * rstd
            xhat_ref[rows, :] = xhat
            rstd_ref[rows, :] = rstd
            y_ref[rows, :] = (xhat * g_ref[...] + bias_ref[...]).astype(y_ref.dtype)

    row = pl.BlockSpec((tm, D_MODEL), lambda i: (i, 0))
    vecs = [_vec(v) for v in (pg, pb, g, bias)]
    return pl.pallas_call(
        body, name=name, grid=(T // tm,),
        in_specs=[pl.BlockSpec((tm, K), lambda i: (i, 0)),
                  pl.BlockSpec((K, D_MODEL), lambda i: (0, 0), pipeline_mode=pl.Buffered(1)),
                  row] + [s for _, s in vecs],
        out_specs=[row, pl.BlockSpec((tm, 1), lambda i: (i, 0)), row],
        out_shape=[jax.ShapeDtypeStruct((T, D_MODEL), F32), jax.ShapeDtypeStruct((T, 1), F32),
                   jax.ShapeDtypeStruct((T, D_MODEL), BF16)],
        compiler_params=_cparams(("parallel",)),
    )(a, b, prev, *[a_ for a_, _ in vecs])


def _mm_ln_bwd(a_list, b, dres, xhat, rstd, g, name, deps=(), w_back=None):
    T = a_list[0].shape[0]
    tm = LN_TM
    na, nd = len(a_list), len(deps)
    ks = [a.shape[1] for a in a_list]
    last = xhat is None
    nout = 1 if last else (5 if w_back is not None else 4)

    def body(*refs):
        a_refs, b_ref, dres_ref = refs[:na], refs[na], refs[na + 1]
        if not last:
            xhat_ref, rstd_ref, g_ref = refs[na + 2:na + 5]
            dr_ref, drb_ref, dg_ref, db_ref = refs[len(refs) - nout:len(refs) - nout + 4]

            @pl.when(pl.program_id(0) == 0)
            def _():
                dg_ref[...] = jnp.zeros_like(dg_ref)
                db_ref[...] = jnp.zeros_like(db_ref)

        for s in range(tm // LN_SUB):
            rows = slice(s * LN_SUB, (s + 1) * LN_SUB)
            mm, off = None, 0
            for a_ref, k in zip(a_refs, ks):
                part = jnp.dot(a_ref[rows, :], b_ref[off:off + k, :], preferred_element_type=F32)
                mm = part if mm is None else mm + part
                off += k
            dy = ALPHA * dres_ref[rows, :] + mm
            if last:
                refs[-1][rows, :] = dy
                continue
            xhat_v = xhat_ref[rows, :]
            dg_ref[...] += jnp.sum(dy * xhat_v, axis=0, keepdims=True)
            db_ref[...] += jnp.sum(dy, axis=0, keepdims=True)
            dxh = dy * g_ref[...]
            m1 = jnp.mean(dxh, axis=-1, keepdims=True)
            m2 = jnp.mean(dxh * xhat_v, axis=-1, keepdims=True)
            dr = rstd_ref[rows, :] * (dxh - m1 - xhat_v * m2)
            dr_ref[rows, :] = dr
            dr_b = dr.astype(drb_ref.dtype)
            drb_ref[rows, :] = dr_b
            if w_back is not None:
                _store_lane_tiles(
                    refs[-1], rows, lax.dot_general(dr_b, refs[na + 5][...], _DN["nt"], preferred_element_type=F32))

    row = pl.BlockSpec((tm, D_MODEL), lambda i: (i, 0))
    vec = pl.BlockSpec((1, D_MODEL), lambda i: (0, 0))
    in_specs = [pl.BlockSpec((tm, k), lambda i: (i, 0)) for k in ks]
    in_specs += [pl.BlockSpec((sum(ks), D_MODEL), lambda i: (0, 0), pipeline_mode=pl.Buffered(1)), row]
    args = list(a_list) + [b, dres]
    if last:
        out_specs, out_shape = row, jax.ShapeDtypeStruct((T, D_MODEL), F32)
    else:
        g_arr, g_spec = _vec(g)
        in_specs += [row, pl.BlockSpec((tm, 1), lambda i: (i, 0)), g_spec]
        args += [xhat, rstd, g_arr]
        out_specs = [row, row, vec, vec]
        out_shape = [jax.ShapeDtypeStruct((T, D_MODEL), F32), jax.ShapeDtypeStruct((T, D_MODEL), BF16),
                     jax.ShapeDtypeStruct((1, D_MODEL), F32), jax.ShapeDtypeStruct((1, D_MODEL), F32)]
        if w_back is not None:
            in_specs.append(pl.BlockSpec(w_back.shape, lambda i: (0, 0), pipeline_mode=pl.Buffered(1)))
            args.append(w_back)
            tiles = w_back.shape[0] // LANES
            out_specs.append(pl.BlockSpec((tiles, tm, LANES), lambda i: (0, i, 0)))
            out_shape.append(jax.ShapeDtypeStruct((tiles, T, LANES), F32))
    return pl.pallas_call(
        body, name=name, grid=(T // tm,),
        in_specs=in_specs + [pl.BlockSpec(memory_space=pl.ANY)] * nd,
        out_specs=out_specs, out_shape=out_shape,
        compiler_params=_cparams(("parallel",) if last else ("arbitrary",)),
    )(*args, *deps)


DW_TM = 1408
FF_TN = 256
FF_TM = 2048
SAVED_GU = BF16


def _mm_swiglu_fwd(h, w_gu, deps=()):
    T = h.shape[0]
    tm = min(T, FF_TM)
    nj = D_FF // FF_TN
    nd = len(deps)

    def body(*refs):
        h_ref, wg_ref, wu_ref = refs[:3]
        g_ref, u_ref, act_ref = refs[3 + nd:]
        hv = h_ref[...]
        gv = lax.dot_general(hv, wg_ref[...], _DN["nt"], preferred_element_type=F32)
        uv = lax.dot_general(hv, wu_ref[...], _DN["nt"], preferred_element_type=F32)
        g_ref[...] = gv.astype(g_ref.dtype)
        u_ref[...] = uv.astype(u_ref.dtype)
        act_ref[...] = (gv * jax.nn.sigmoid(gv) * uv).astype(act_ref.dtype)

    tile = pl.BlockSpec((tm, FF_TN), lambda j, i: (i, j))
    return pl.pallas_call(
        body, name="mm_gate_up_swiglu", grid=(nj, T // tm),
        in_specs=[pl.BlockSpec((tm, D_MODEL), lambda j, i: (i, 0)),
                  pl.BlockSpec((FF_TN, D_MODEL), lambda j, i: (j, 0)),
                  pl.BlockSpec((FF_TN, D_MODEL), lambda j, i: (j + nj, 0))] + [pl.BlockSpec(memory_space=pl.ANY)] * nd,
        out_specs=[tile, tile, tile],
        out_shape=[jax.ShapeDtypeStruct((T, D_FF), SAVED_GU), jax.ShapeDtypeStruct((T, D_FF), SAVED_GU),
                   jax.ShapeDtypeStruct((T, D_FF), BF16)],
        compiler_params=_cparams(("parallel", "parallel")),
    )(h, w_gu, w_gu, *deps)


def _mm_swiglu_bwd(dr, w_dn, g, u, deps=()):
    T = dr.shape[0]
    tm = min(T, FF_TM)

    def body(*refs):
        dr_ref, w_ref, g_ref, u_ref = refs[:4]
        dg_ref, du_ref = refs[-2:]
        da = lax.dot_general(dr_ref[...], w_ref[...], _DN["nt"], preferred_element_type=F32)
        gv, uv = g_ref[...].astype(F32), u_ref[...].astype(F32)
        s = jax.nn.sigmoid(gv)
        du_ref[...] = (da * (gv * s)).astype(du_ref.dtype)
        dg_ref[...] = (da * uv * (s * (1.0 + gv * (1.0 - s)))).astype(dg_ref.dtype)

    tile = pl.BlockSpec((tm, FF_TN), lambda j, i: (i, j))
    return pl.pallas_call(
        body, name="mm_dact_swiglu", grid=(D_FF // FF_TN, T // tm),
        in_specs=[pl.BlockSpec((tm, D_MODEL), lambda j, i: (i, 0)), pl.BlockSpec((FF_TN, D_MODEL), lambda j, i: (j, 0)),
                  tile, tile] + [ANY] * len(deps),
        out_specs=[tile, tile],
        out_shape=[jax.ShapeDtypeStruct((T, D_FF), BF16)] * 2,
        compiler_params=_cparams(("parallel", "parallel")),
    )(dr, w_dn, g, u, *deps)


def _gelu(x):
    return 0.5 * x * (1.0 + lax.erf(x * INV_SQRT2))


def _gelu_grad(x):
    return 0.5 * (1.0 + lax.erf(x * INV_SQRT2)) + x * (jnp.exp(-0.5 * x * x) * INV_SQRT_2PI)


def _shift_down(z, k):
    row = lax.broadcasted_iota(jnp.int32, z.shape, 0)
    return jnp.where(row >= k, pltpu.roll(z, k, 0), 0.0)


def _shift_up(z, k):
    n = z.shape[0]
    row = lax.broadcasted_iota(jnp.int32, z.shape, 0)
    return jnp.where(row < n - k, pltpu.roll(z, n - k, 0), 0.0)


def _lo_mask(shape):
    return lax.broadcasted_iota(jnp.int32, shape, len(shape) - 1) < HALF


def _seg_mean(x, lo):
    a = jnp.sum(jnp.where(lo, x, 0.0), axis=-1, keepdims=True)
    b = jnp.sum(jnp.where(lo, 0.0, x), axis=-1, keepdims=True)
    return jnp.where(lo, a, b) * (1.0 / HALF)


def _pool_windows(first):
    lo = _lo_mask((1, LANES))
    return jnp.where(first, jnp.where(lo, 2.0, 4.0), jnp.where(lo, 8.0, 16.0)), lo


def _pool_mean_minus_token(p, first):
    wl, lo = _pool_windows(first)
    s2 = p + _shift_down(p, 1)
    s4 = s2 + _shift_down(s2, 2)
    s8 = s4 + _shift_down(s4, 4)
    s16 = s8 + _shift_down(s8, 8)
    win = jnp.where(first, jnp.where(lo, s2, s4), jnp.where(lo, s8, s16))
    t1 = (lax.broadcasted_iota(jnp.int32, p.shape, 0) + 1).astype(F32)
    count = jnp.minimum(t1, wl)
    return win / count - p, count


SGU_UNROLL = 4


def _tril_keep():
    r = lax.broadcasted_iota(jnp.int32, (2 * CHUNK, CHUNK), 0)
    s = lax.broadcasted_iota(jnp.int32, (2 * CHUNK, CHUNK), 1)
    return s <= (r & (CHUNK - 1))


def _sgu_chunk_fwd(u, v, g, wm, bias, lo):
    ug = _gelu(u)
    vg = _gelu(v)
    mu = _seg_mean(vg, lo)
    xc = vg - mu
    var = _seg_mean(xc * xc, lo)
    rstd = lax.rsqrt(var + LN_EPS)
    vn = xc * rstd
    vh = (vn * g).astype(BF16)
    mm2 = jnp.dot(wm, vh, preferred_element_type=F32)
    mixed = jnp.where(lo, mm2[:CHUNK], mm2[CHUNK:]) + bias
    return ug, vn, rstd, vh, mixed


def _mixer_fwd(proj, wconv, wpool_bd, pscale, lng, wsp, bias, layer):
    T = proj.shape[1]
    nchunk = T // CHUNK

    def body(a_ref, b_ref, c_ref, wc_ref, wp_ref, ps_ref, lng_ref, wsp_ref, bias_ref, o_ref):
        j = pl.program_id(0)

        @pl.when(j < 3)
        def _conv():
            z = c_ref[...] * a_ref[...]
            w = wc_ref[...]
            y = w[0:1] * _shift_down(z, 2) + w[1:2] * _shift_down(z, 1) + w[2:3] * z
            o_ref[...] = (b_ref[...] * y).astype(o_ref.dtype)

        @pl.when((j >= 3) & (j < 5))
        def _pool():
            d, _ = _pool_mean_minus_token(a_ref[...], j == 3)
            y = jnp.dot(d.astype(BF16), wp_ref[...].astype(BF16), preferred_element_type=F32)
            o_ref[...] = (y * ps_ref[...]).astype(o_ref.dtype)

        @pl.when(j >= 5)
        def _sgu():
            lo = _lo_mask((CHUNK, LANES))
            wm = jnp.where(_tril_keep(), wsp_ref[...], 0.0).astype(BF16)
            bias_t = bias_ref[...]
            g = lng_ref[...]

            def chunk(n, carry):
                rows = pl.ds(pl.multiple_of(n * CHUNK, CHUNK), CHUNK)
                ug, _, _, _, mixed = _sgu_chunk_fwd(a_ref[rows, :], b_ref[rows, :], g, wm, bias_t, lo)
                o_ref[rows, :] = (ug * mixed).astype(o_ref.dtype)
                return carry

            lax.fori_loop(0, nchunk, chunk, 0, unroll=SGU_UNROLL)

    def col(f):
        return lambda j: (0, f(j))

    clip = lambda v, lo, hi: jnp.minimum(jnp.maximum(v, lo), hi)
    return pl.pallas_call(
        body,
        name="mixer_fwd",
        grid=(8,),
        in_specs=[
            pl.BlockSpec((None, T, LANES), lambda j: (jnp.where(j < 3, j, j + 6), 0, 0)),
            pl.BlockSpec((None, T, LANES), lambda j: (jnp.where(j < 3, j + 3, jnp.where(j < 5, 5, j + 9)), 0, 0)),
            pl.BlockSpec((None, T, LANES), lambda j: (jnp.where(j < 3, j + 6, 8), 0, 0)),
            pl.BlockSpec((None, 3, LANES), lambda j: (layer, 0, clip(j, 0, 2))),
            pl.BlockSpec((None, None, LANES, LANES), lambda j: (layer, clip(j - 3, 0, 1), 0, 0)),
            pl.BlockSpec((None, 1, LANES), lambda j: (layer, 0, clip(j - 3, 0, 1))),
            pl.BlockSpec((None, 1, LANES), lambda j: (layer, 0, clip(j - 5, 0, 2))),
            pl.BlockSpec((None, None, 2 * CHUNK, CHUNK), lambda j: (layer, clip(j - 5, 0, 2), 0, 0)),
            pl.BlockSpec((None, None, CHUNK, LANES), lambda j: (layer, clip(j - 5, 0, 2), 0, 0)),
        ],
        out_specs=pl.BlockSpec((T, LANES), lambda j: (0, j)),
        out_shape=jax.ShapeDtypeStruct((T, D_MODEL), BF16),
        compiler_params=_cparams(("arbitrary",)),
    )(proj, proj, proj, wconv, wpool_bd, pscale, lng, wsp, bias)


def _mixer_bwd(proj, dmix, wconv, wpool_bd, pscale, lng, wsp, bias, layer, deps=()):
    T = proj.shape[1]
    nchunk = T // CHUNK

    def body(*refs):
        a_ref, b_ref, c_ref, dm_ref, wc_ref, wp_ref, ps_ref, lng_ref, wsp_ref, bias_ref = refs[:10]
        o_ref, dwc_ref, dwp_ref, dps_ref, dlng_ref, dwsp_ref, dbias_ref, keep1, keep2 = refs[10 + len(deps):]
        k = pl.program_id(0)

        @pl.when(k < 3)
        def _conv():
            xa, gb, gc, dya = a_ref[...], b_ref[...], c_ref[...], dm_ref[...]
            w = wc_ref[...]
            z = gc * xa
            z1 = _shift_down(z, 1)
            z2 = _shift_down(z, 2)
            y = w[0:1] * z2 + w[1:2] * z1 + w[2:3] * z
            dyv = dya * gb
            dz = w[2:3] * dyv + w[1:2] * _shift_up(dyv, 1) + w[0:1] * _shift_up(dyv, 2)
            dwc_ref[0:1, :] = jnp.sum(dyv * z2, axis=0, keepdims=True)
            dwc_ref[1:2, :] = jnp.sum(dyv * z1, axis=0, keepdims=True)
            dwc_ref[2:3, :] = jnp.sum(dyv * z, axis=0, keepdims=True)
            o_ref[...] = (dz * gc).astype(o_ref.dtype)
            keep1[k] = (dya * y).astype(keep1.dtype)
            keep1[k + 3] = (dz * xa).astype(keep1.dtype)

        @pl.when((k >= 3) & (k < 9))
        def _emit_gb_gc():
            o_ref[...] = keep1[k - 3]

        @pl.when((k >= 9) & (k < 11))
        def _pool():
            first = k == 9
            p, dyb = a_ref[...], dm_ref[...]
            d, count = _pool_mean_minus_token(p, first)
            w2 = wp_ref[...].astype(BF16)
            db = d.astype(BF16)
            y = jnp.dot(db, w2, preferred_element_type=F32)
            dps_ref[...] = jnp.sum(dyb * y, axis=0, keepdims=True)
            dyv = (dyb * ps_ref[...]).astype(BF16)
            dd = lax.dot_general(dyv, w2, _DN["nt"], preferred_element_type=F32)
            dwp_ref[...] = lax.dot_general(db, dyv, _DN["tn"], preferred_element_type=F32)
            dwin = dd / count
            a2 = dwin + _shift_up(dwin, 1)
            a4 = a2 + _shift_up(a2, 2)
            a8 = a4 + _shift_up(a4, 4)
            a16 = a8 + _shift_up(a8, 8)
            _, lo = _pool_windows(first)
            back = jnp.where(first, jnp.where(lo, a2, a4), jnp.where(lo, a8, a16))
            o_ref[...] = (back - dd).astype(o_ref.dtype)

        @pl.when((k >= 11) & (k < 14))
        def _sgu():
            lo = _lo_mask((CHUNK, LANES))
            keep = _tril_keep()
            wm = jnp.where(keep, wsp_ref[...], 0.0).astype(BF16)
            bias_t = bias_ref[...]
            g = lng_ref[...]
            dwsp_ref[...] = jnp.zeros_like(dwsp_ref)
            dbias_ref[...] = jnp.zeros_like(dbias_ref)
            dlng_ref[...] = jnp.zeros_like(dlng_ref)

            def chunk(n, carry):
                rows = pl.ds(pl.multiple_of(n * CHUNK, CHUNK), CHUNK)
                u, v, dyc = a_ref[rows, :], b_ref[rows, :], dm_ref[rows, :]
                ug, vn, rstd, vh, mixed = _sgu_chunk_fwd(u, v, g, wm, bias_t, lo)
                dmx = dyc * ug
                o_ref[rows, :] = (dyc * mixed * _gelu_grad(u)).astype(o_ref.dtype)
                dbias_ref[...] += dmx
                dst = jnp.concatenate([jnp.where(lo, dmx, 0.0), jnp.where(lo, 0.0, dmx)], axis=0).astype(BF16)
                dwsp_ref[...] += lax.dot_general(dst, vh, _DN["nt"], preferred_element_type=F32)
                dvh = lax.dot_general(wm, dst, _DN["tn"], preferred_element_type=F32)
                dlng_ref[...] += jnp.sum(dvh * vn, axis=0, keepdims=True)
                dvn = dvh * g
                m1 = _seg_mean(dvn, lo)
                m2 = _seg_mean(dvn * vn, lo)
                dvg = rstd * (dvn - m1 - vn * m2)
                keep2[k - 11, rows, :] = (dvg * _gelu_grad(v)).astype(keep2.dtype)
                return carry

            lax.fori_loop(0, nchunk, chunk, 0, unroll=SGU_UNROLL)
            dwsp_ref[...] = jnp.where(keep, dwsp_ref[...], 0.0)
            dbt = dbias_ref[...]
            lane = lax.broadcasted_iota(jnp.int32, (CHUNK, LANES), 1)
            sa = jnp.sum(jnp.where(lo, dbt, 0.0), axis=-1, keepdims=True)
            sb = jnp.sum(jnp.where(lo, 0.0, dbt), axis=-1, keepdims=True)
            dbias_ref[...] = jnp.where(lane == 0, sa, jnp.where(lane == 1, sb, 0.0))

        @pl.when(k >= 14)
        def _emit_v():
            o_ref[...] = keep2[k - 14]

    def col(f):
        return lambda k: (0, f(k))

    clip = lambda v, lo, hi: jnp.minimum(jnp.maximum(v, lo), hi)
    view_a = lambda k: jnp.where(k < 3, k, jnp.where(k < 9, 2, jnp.where(k < 14, k, 13)))
    view_b = lambda k: jnp.where(k < 3, k + 3, jnp.where(k < 11, 5, jnp.where(k < 14, k + 3, 16)))
    view_c = lambda k: jnp.where(k < 3, k + 6, 8)
    view_dm = lambda k: jnp.where(k < 3, k, jnp.where(k < 9, 2, jnp.where(k < 14, k - 6, 7)))
    return pl.pallas_call(
        body,
        name="mixer_bwd",
        grid=(17,),
        in_specs=[
            pl.BlockSpec((None, T, LANES), lambda k: (view_a(k), 0, 0)),
            pl.BlockSpec((None, T, LANES), lambda k: (view_b(k), 0, 0)),
            pl.BlockSpec((None, T, LANES), lambda k: (view_c(k), 0, 0)),
            pl.BlockSpec((None, T, LANES), lambda k: (view_dm(k), 0, 0)),
            pl.BlockSpec((None, 3, LANES), lambda k: (layer, 0, clip(k, 0, 2))),
            pl.BlockSpec((None, None, LANES, LANES), lambda k: (layer, clip(k - 9, 0, 1), 0, 0)),
            pl.BlockSpec((None, 1, LANES), lambda k: (layer, 0, clip(k - 9, 0, 1))),
            pl.BlockSpec((None, 1, LANES), lambda k: (layer, 0, clip(k - 11, 0, 2))),
            pl.BlockSpec((None, None, 2 * CHUNK, CHUNK), lambda k: (layer, clip(k - 11, 0, 2), 0, 0)),
            pl.BlockSpec((None, None, CHUNK, LANES), lambda k: (layer, clip(k - 11, 0, 2), 0, 0)),
        ] + [pl.BlockSpec(memory_space=pl.ANY)] * len(deps),
        out_specs=[
            pl.BlockSpec((T, LANES), lambda k: (0, k)),
            pl.BlockSpec((3, LANES), col(lambda k: clip(k, 0, 2))),
            pl.BlockSpec((None, LANES, LANES), lambda k: (clip(k - 9, 0, 1), 0, 0)),
            pl.BlockSpec((1, LANES), col(lambda k: clip(k - 9, 0, 1))),
            pl.BlockSpec((1, LANES), col(lambda k: clip(k - 11, 0, 2))),
            pl.BlockSpec((None, 2 * CHUNK, CHUNK), lambda k: (clip(k - 11, 0, 2), 0, 0)),
            pl.BlockSpec((None, CHUNK, LANES), lambda k: (clip(k - 11, 0, 2), 0, 0)),
        ],
        out_shape=[
            jax.ShapeDtypeStruct((T, IN_W), BF16),
            jax.ShapeDtypeStruct((3, CONV_W), F32),
            jax.ShapeDtypeStruct((2, LANES, LANES), F32),
            jax.ShapeDtypeStruct((1, POOL_W), F32),
            jax.ShapeDtypeStruct((1, SGU_W), F32),
            jax.ShapeDtypeStruct((3, 2 * CHUNK, CHUNK), F32),
            jax.ShapeDtypeStruct((3, CHUNK, LANES), F32),
        ],
        scratch_shapes=[pltpu.VMEM((6, T, LANES), BF16), pltpu.VMEM((3, T, LANES), BF16)],
        compiler_params=_cparams(("arbitrary",)),
    )(proj, proj, proj, dmix, wconv, wpool_bd, pscale, lng, wsp, bias, *deps)


def _loss_ln_bwd(xhat, rstd, g, b, target, tm=256):
    T = xhat.shape[0]

    def body(xhat_ref, rstd_ref, g_ref, b_ref, t_ref, loss_ref, dr_ref, drb_ref, dg_ref, db_ref):
        xhat_v = xhat_ref[...]
        err = xhat_v * g_ref[...] + b_ref[...] - t_ref[...]
        dy = err * (1.0 / D_MODEL)

        @pl.when(pl.program_id(0) == 0)
        def _():
            loss_ref[...] = jnp.zeros_like(loss_ref)
            dg_ref[...] = jnp.zeros_like(dg_ref)
            db_ref[...] = jnp.zeros_like(db_ref)

        part = jnp.sum(jnp.sum(err * err, axis=-1, keepdims=True), axis=0, keepdims=True)
        loss_ref[...] += jnp.broadcast_to(part * (0.5 / D_MODEL), loss_ref.shape)
        dg_ref[...] += jnp.sum(dy * xhat_v, axis=0, keepdims=True)
        db_ref[...] += jnp.sum(dy, axis=0, keepdims=True)
        dxh = dy * g_ref[...]
        m1 = jnp.mean(dxh, axis=-1, keepdims=True)
        m2 = jnp.mean(dxh * xhat_v, axis=-1, keepdims=True)
        dr = rstd_ref[...] * (dxh - m1 - xhat_v * m2)
        dr_ref[...] = dr
        drb_ref[...] = dr.astype(drb_ref.dtype)

    row = pl.BlockSpec((tm, D_MODEL), lambda i: (i, 0))
    vec = pl.BlockSpec((1, D_MODEL), lambda i: (0, 0))
    (g_arr, g_spec), (b_arr, b_spec) = _vec(g), _vec(b)
    return pl.pallas_call(
        body,
        name="loss_ln_bwd",
        grid=(T // tm,),
        in_specs=[row, pl.BlockSpec((tm, 1), lambda i: (i, 0)), g_spec, b_spec, row],
        out_specs=[pl.BlockSpec((8, LANES), lambda i: (0, 0)), row, row, vec, vec],
        out_shape=[jax.ShapeDtypeStruct((8, LANES), F32),
                   jax.ShapeDtypeStruct((T, D_MODEL), F32), jax.ShapeDtypeStruct((T, D_MODEL), BF16),
                   jax.ShapeDtypeStruct((1, D_MODEL), F32), jax.ShapeDtypeStruct((1, D_MODEL), F32)],
        compiler_params=_cparams(("arbitrary",)),
    )(xhat, rstd, g_arr, b_arr, target)


def _adamw(w, g, m, v, tr):
    R, C = w.shape[-2:]
    assert R % tr == 0
    c1 = 1.0 - ADAM_B1 ** ADAM_STEP
    c2 = 1.0 - ADAM_B2 ** ADAM_STEP

    def body(w_ref, g_ref, m_ref, v_ref, d_ref, mo_ref, vo_ref):
        gv = g_ref[...]
        mn = ADAM_B1 * m_ref[...] + (1.0 - ADAM_B1) * gv
        vn = ADAM_B2 * v_ref[...] + (1.0 - ADAM_B2) * (gv * gv)
        d_ref[...] = -ADAM_LR * ((mn / c1) / (jnp.sqrt(vn / c2) + ADAM_EPS) + ADAM_WD * w_ref[...])
        mo_ref[...] = mn
        vo_ref[...] = vn

    if w.ndim == 2:
        grid, blk = (R // tr,), pl.BlockSpec((tr, C), lambda i: (i, 0))
    else:
        grid, blk = (w.shape[0], R // tr), pl.BlockSpec((None, tr, C), lambda l, i: (l, i, 0))
    return pl.pallas_call(
        body, name="adamw", grid=grid, in_specs=[blk] * 4, out_specs=[blk] * 3,
        out_shape=[jax.ShapeDtypeStruct(w.shape, F32)] * 3, compiler_params=_cparams(("parallel",) * len(grid)),
    )(w, g, m, v)


def _my_place():
    return lax.axis_index("x"), lax.axis_index("y"), lax.axis_index("c")


ANY = pl.BlockSpec(memory_space=pl.ANY)
HBM = pl.BlockSpec(memory_space=pltpu.HBM)
SEM = pl.BlockSpec(memory_space=pltpu.SEMAPHORE)
EFFECT = pltpu.SideEffectType.DATAFLOW_SIDE_EFFECTING


def _in_hbm(a):
    return pltpu.with_memory_space_constraint(a, pltpu.HBM)


def _block_rows(ref, dev):
    r = ref.shape[0] // N_DEV
    start = pl.multiple_of((4 * dev[0] + 2 * dev[1] + dev[2]) * r, 16)
    return ref.at[pl.ds(start, r), :]


def _ag_first_copies(s_refs, land_refs, send_sems, recv_sems, receiving):
    x, y, c = _my_place()
    peers = [(x, y, 1 - c)] + [(*chip, c) for chip in _other_chips(x, y)]
    copies = []
    for k, peer in enumerate(peers):
        block = peer if receiving else (x, y, c)
        copies += [pltpu.make_async_remote_copy(
            src_ref=s_refs[w], dst_ref=_block_rows(land_refs[w], block),
            send_sem=send_sems.at[k * len(s_refs) + w], recv_sem=recv_sems.at[k * len(s_refs) + w],
            device_id=peer, device_id_type=MESH)
            for w in range(len(s_refs))]
    return copies


def _ag_start(shards, layer, after=()):
    nw = len(shards)

    def body(*refs):
        s_refs, land_refs = refs[:nw], refs[nw:2 * nw]
        token = refs[-1]
        sems = 2 * nw + len(after)
        for cp in _ag_first_copies(s_refs, land_refs, refs[sems], refs[sems + 1], False):
            cp.start()
        token[...] = jnp.zeros_like(token)

    lands = [lax.empty((N_DEV * s.shape[0], D_MODEL), BF16) for s in shards]
    out = pl.pallas_call(
        body, name="ag_start_%s" % layer,
        in_specs=[HBM] * (2 * nw) + [ANY] * len(after),
        out_specs=(SEM, SEM, *[HBM] * (2 * nw), pl.BlockSpec(memory_space=pltpu.VMEM)),
        out_shape=(pltpu.SemaphoreType.DMA((4 * nw,)), pltpu.SemaphoreType.DMA((4 * nw,)),
                   *[pltpu.HBM(a.shape, a.dtype) for a in list(shards) + lands],
                   jax.ShapeDtypeStruct((8, LANES), F32)),
        input_output_aliases={i: 2 + i for i in range(2 * nw)},
        compiler_params=pltpu.CompilerParams(has_side_effects=EFFECT),
    )(*[_in_hbm(a) for a in list(shards) + lands], *after)
    return out[0], out[1], out[2:2 + nw], out[2 + nw:2 + 2 * nw], out[-1]


def _ag_wait(send_sems, recv_sems, shards, lands, after, layer):
    nw = len(shards)

    def body(*refs):
        s_refs, land_refs = refs[:nw], refs[nw:2 * nw]
        for cp in _ag_first_copies(s_refs, land_refs, refs[2 * nw], refs[2 * nw + 1], True):
            cp.wait_send()
            cp.wait_recv()

    out = pl.pallas_call(
        body, name="ag_wait_%s" % layer,
        in_specs=[HBM] * (2 * nw) + [SEM, SEM] + [ANY] * len(after),
        out_specs=[HBM] * (2 * nw),
        out_shape=[pltpu.HBM(a.shape, a.dtype) for a in list(shards) + list(lands)],
        input_output_aliases={i: i for i in range(2 * nw)},
        compiler_params=pltpu.CompilerParams(has_side_effects=EFFECT),
    )(*shards, *lands, send_sems, recv_sems, *after)
    return out[:nw], out[nw:]


def _ag_pass_on(shards, lands):
    nw = len(shards)

    def body(*refs):
        s_refs, g_refs = refs[:nw], refs[2 * nw:3 * nw]
        send_sems, recv_sems, local_sems = refs[3 * nw:3 * nw + 3]
        stage = refs[3 * nw + 3:]
        x, y, c = _my_place()
        load = [pltpu.make_async_copy(s_refs[w], stage[w], local_sems.at[w]) for w in range(nw)]
        mine = [pltpu.make_async_copy(stage[w], _block_rows(g_refs[w], (x, y, c)), local_sems.at[w])
                for w in range(nw)]
        for cp in load:
            cp.start()
        sends, arrivals = [], []
        for j, chip in enumerate(_other_chips(x, y)):
            for w in range(nw):
                rows_out = _block_rows(g_refs[w], (*chip, c))
                rows_in = _block_rows(g_refs[w], (*chip, 1 - c))
                sends.append(pltpu.make_async_remote_copy(
                    src_ref=rows_out, dst_ref=rows_out, send_sem=send_sems.at[j, w], recv_sem=recv_sems.at[j, w],
                    device_id=(x, y, 1 - c), device_id_type=MESH))
                arrivals.append(pltpu.make_async_remote_copy(
                    src_ref=rows_in, dst_ref=rows_in, send_sem=send_sems.at[j, w], recv_sem=recv_sems.at[j, w],
                    device_id=(x, y, 1 - c), device_id_type=MESH))
        for cp in sends:
            cp.start()
        for w in range(nw):
            load[w].wait()
            mine[w].start()
        for cp in arrivals:
            cp.wait_recv()
        for cp in sends:
            cp.wait_send()
        for cp in mine:
            cp.wait()

    return pl.pallas_call(
        body, name="ag_pass_on",
        in_specs=[ANY] * (2 * nw), out_specs=[ANY] * nw,
        out_shape=[jax.ShapeDtypeStruct(a.shape, a.dtype) for a in lands],
        input_output_aliases={nw + i: i for i in range(nw)},
        scratch_shapes=[pltpu.SemaphoreType.DMA((3, nw)), pltpu.SemaphoreType.DMA((3, nw)),
                        pltpu.SemaphoreType.DMA((nw,))] + [pltpu.VMEM(s.shape, s.dtype) for s in shards],
        compiler_params=_cparams(),
    )(*shards, *lands)


def _rs_sibling_copies(p_refs, land_refs, send_sems, recv_sems):
    x, y, c = _my_place()
    return [pltpu.make_async_remote_copy(
        src_ref=p_refs[w].at[:, 1 - c], dst_ref=land_refs[w],
        send_sem=send_sems.at[w], recv_sem=recv_sems.at[w], device_id=(x, y, 1 - c), device_id_type=MESH)
        for w in range(len(p_refs))]


def _rs_sibling_start(parts, tag, after=()):
    nw = len(parts)
    sems = 2 * nw + len(after)

    def body(*refs):
        for cp in _rs_sibling_copies(refs[:nw], refs[nw:2 * nw], refs[sems], refs[sems + 1]):
            cp.start()
        refs[-1][...] = jnp.zeros_like(refs[-1])

    lands = [lax.empty(p.shape[:1] + p.shape[2:], BF16) for p in parts]
    out = pl.pallas_call(
        body, name="rs_sibling_start_%s" % tag,
        in_specs=[HBM] * (2 * nw) + [ANY] * len(after),
        out_specs=(SEM, SEM, *[HBM] * (2 * nw), pl.BlockSpec(memory_space=pltpu.VMEM)),
        out_shape=(pltpu.SemaphoreType.DMA((nw,)), pltpu.SemaphoreType.DMA((nw,)),
                   *[pltpu.HBM(a.shape, a.dtype) for a in list(parts) + lands],
                   jax.ShapeDtypeStruct((8, LANES), F32)),
        input_output_aliases={i: 2 + i for i in range(2 * nw)},
        compiler_params=pltpu.CompilerParams(has_side_effects=EFFECT),
    )(*[_in_hbm(a) for a in list(parts) + lands], *after)
    return out[0], out[1], out[2:2 + nw], out[2 + nw:2 + 2 * nw], out[-1]


def _rs_sibling_wait(send_sems, recv_sems, parts, lands, after, tag):
    nw = len(parts)

    def body(*refs):
        for cp in _rs_sibling_copies(refs[:nw], refs[nw:2 * nw], refs[2 * nw], refs[2 * nw + 1]):
            cp.wait_send()
            cp.wait_recv()

    out = pl.pallas_call(
        body, name="rs_sibling_wait_%s" % tag,
        in_specs=[HBM] * (2 * nw) + [SEM, SEM] + [ANY] * len(after),
        out_specs=[HBM] * (2 * nw),
        out_shape=[pltpu.HBM(a.shape, a.dtype) for a in list(parts) + list(lands)],
        input_output_aliases={i: i for i in range(2 * nw)},
        compiler_params=pltpu.CompilerParams(has_side_effects=EFFECT),
    )(*parts, *lands, send_sems, recv_sems, *after)
    return out[:nw], out[nw:]


def _rs_chip_sum(parts, gots, c):
    n = len(parts)

    def body(c_ref, *refs):
        for p_ref, g_ref, o_ref in zip(refs[:n], refs[n:2 * n], refs[2 * n:]):
            o_ref[...] = (p_ref[...].astype(F32) + g_ref[...].astype(F32)).astype(o_ref.dtype)

    mine = [pl.BlockSpec((None, None, p.shape[2], D_MODEL), lambda q, c_ref: (q, c_ref[0], 0, 0)) for p in parts]
    theirs = [pl.BlockSpec((None, g.shape[1], D_MODEL), lambda q, c_ref: (q, 0, 0)) for g in gots]
    return pl.pallas_call(
        body, name="rs_chip_sum",
        grid_spec=pltpu.PrefetchScalarGridSpec(
            num_scalar_prefetch=1, grid=(4,), in_specs=mine + theirs, out_specs=theirs),
        out_shape=[jax.ShapeDtypeStruct(g.shape, BF16) for g in gots],
        compiler_params=_cparams(("parallel",)),
    )(c, *parts, *gots)


def _other_chips(x, y):
    return [(1 - x, y), (x, 1 - y), (1 - x, 1 - y)]


def _rs_chip_copies(s_refs, land_refs, send_sems, recv_sems):
    x, y, c = _my_place()
    copies = []
    for k, chip in enumerate(_other_chips(x, y)):
        q = 2 * chip[0] + chip[1]
        copies += [pltpu.make_async_remote_copy(
            src_ref=s_refs[w].at[q], dst_ref=land_refs[w].at[k],
            send_sem=send_sems.at[k * len(s_refs) + w], recv_sem=recv_sems.at[k * len(s_refs) + w],
            device_id=(*chip, c), device_id_type=MESH)
            for w in range(len(s_refs))]
    return copies


def _rs_chip_start(sums, layer):
    nw = len(sums)

    def body(*refs):
        s_refs, land_refs = refs[:nw], refs[nw:2 * nw]
        send_sems, recv_sems = refs[2 * nw], refs[2 * nw + 1]
        token = refs[-1]
        for cp in _rs_chip_copies(s_refs, land_refs, send_sems, recv_sems):
            cp.start()
        token[...] = jnp.zeros_like(token)

    lands = [lax.empty((3,) + s.shape[1:], BF16) for s in sums]
    out = pl.pallas_call(
        body, name="rs_chip_start_%s" % layer,
        in_specs=[HBM] * (2 * nw),
        out_specs=(SEM, SEM, *[HBM] * (2 * nw), pl.BlockSpec(memory_space=pltpu.VMEM)),
        out_shape=(pltpu.SemaphoreType.DMA((3 * nw,)), pltpu.SemaphoreType.DMA((3 * nw,)),
                   *[pltpu.HBM(a.shape, a.dtype) for a in list(sums) + lands],
                   jax.ShapeDtypeStruct((8, LANES), F32)),
        input_output_aliases={i: 2 + i for i in range(2 * nw)},
        compiler_params=pltpu.CompilerParams(has_side_effects=EFFECT),
    )(*[_in_hbm(a) for a in list(sums) + lands])
    return out[0], out[1], out[2:2 + nw], out[2 + nw:2 + 2 * nw], out[-1]


def _rs_chip_wait(send_sems, recv_sems, sums, lands, after, layer):
    nw = len(sums)

    def body(*refs):
        s_refs, land_refs = refs[:nw], refs[nw:2 * nw]
        for cp in _rs_chip_copies(s_refs, land_refs, refs[2 * nw], refs[2 * nw + 1]):
            cp.wait_send()
            cp.wait_recv()

    out = pl.pallas_call(
        body, name="rs_chip_wait_%s" % layer,
        in_specs=[HBM] * (2 * nw) + [SEM, SEM] + [ANY] * len(after),
        out_specs=[HBM] * (2 * nw),
        out_shape=[pltpu.HBM(a.shape, a.dtype) for a in list(sums) + list(lands)],
        input_output_aliases={i: i for i in range(2 * nw)},
        compiler_params=pltpu.CompilerParams(has_side_effects=EFFECT),
    )(*sums, *lands, send_sems, recv_sems, *after)
    return out[:nw], out[nw:]


def _rs_finish(sums, gots, q, layer, into):
    n = len(sums)

    def body(q_ref, *refs):
        for s_ref, g_ref, o_ref in zip(refs[:n], refs[n:2 * n], refs[len(refs) - n:]):
            o_ref[...] = ((s_ref[...].astype(F32) + g_ref[0].astype(F32)) + g_ref[1].astype(F32)) + g_ref[2].astype(F32)

    rows = [s.shape[1] for s in sums]
    in_specs = [pl.BlockSpec((None, r, D_MODEL), lambda i, q_ref: (q_ref[0], 0, 0)) for r in rows]
    in_specs += [pl.BlockSpec((3, r, D_MODEL), lambda i, q_ref: (0, 0, 0)) for r in rows]
    args = [q, *sums, *gots]
    aliases = {}
    if into is not None:
        in_specs += [ANY] * n
        aliases = {len(args) + i: i for i in range(n)}
        args += list(into)
    return pl.pallas_call(
        body, name="rs_finish",
        grid_spec=pltpu.PrefetchScalarGridSpec(
            num_scalar_prefetch=1, grid=(1,), in_specs=in_specs,
            out_specs=[pl.BlockSpec((None, r, D_MODEL), lambda i, q_ref: (layer, 0, 0)) for r in rows]),
        out_shape=[jax.ShapeDtypeStruct((DEPTH, r, D_MODEL), F32) for r in rows],
        input_output_aliases=aliases,
        compiler_params=_cparams(("arbitrary",)),
    )(*args)


def _allreduce_small(vec, deps=()):
    R = vec.shape[0]
    assert R % (8 * N_DEV) == 0
    P = R // N_DEV
    nd = len(deps)

    def body(*refs):
        v_ref = refs[0]
        o_ref, buf, send1, recv1, send2, recv2 = refs[1 + nd:]
        x, y, c = _my_place()
        me = 4 * x + 2 * y + c

        def piece(ref, d):
            return ref.at[pl.ds(pl.multiple_of(d * P, 8), P), :]

        def peer(k):
            p = me ^ k
            return p, (p >> 2, (p >> 1) & 1, p & 1)

        scatter = []
        for k in range(1, N_DEV):
            p, where = peer(k)
            scatter.append(pltpu.make_async_remote_copy(
                src_ref=piece(v_ref, p), dst_ref=buf.at[k], send_sem=send1.at[k - 1], recv_sem=recv1.at[k - 1],
                device_id=where, device_id_type=MESH))
        for cp in scatter:
            cp.start()
        buf[0] = piece(v_ref, me)[...]
        for cp in scatter:
            cp.wait()
        acc = buf[me]
        for d in range(1, N_DEV):
            acc = acc + buf[me ^ d]
        piece(o_ref, me)[...] = acc
        spread, arrivals = [], []
        for k in range(1, N_DEV):
            p, where = peer(k)
            spread.append(pltpu.make_async_remote_copy(
                src_ref=piece(o_ref, me), dst_ref=piece(o_ref, me), send_sem=send2.at[k - 1], recv_sem=recv2.at[k - 1],
                device_id=where, device_id_type=MESH))
            arrivals.append(pltpu.make_async_remote_copy(
                src_ref=piece(o_ref, p), dst_ref=piece(o_ref, p), send_sem=send2.at[k - 1], recv_sem=recv2.at[k - 1],
                device_id=where, device_id_type=MESH))
        for cp in spread:
            cp.start()
        for cp in arrivals:
            cp.wait_recv()
        for cp in spread:
            cp.wait_send()

    sems = pltpu.SemaphoreType.DMA((N_DEV - 1,))
    return pl.pallas_call(
        body, name="allreduce_small",
        in_specs=[pl.BlockSpec(memory_space=pltpu.VMEM)] + [ANY] * nd, out_specs=pl.BlockSpec(memory_space=pltpu.VMEM),
        out_shape=jax.ShapeDtypeStruct((R, LANES), F32),
        scratch_shapes=[pltpu.VMEM((N_DEV, P, LANES), F32), sems, sems, sems, sems],
        compiler_params=_cparams(),
    )(vec, *deps)


def _pack(arrs):
    flat = jnp.concatenate([a.reshape(-1) for a in arrs])
    pad = (-flat.shape[0]) % (8 * N_DEV * LANES)
    return jnp.pad(flat, (0, pad)).reshape(-1, LANES)


def _unpack(packed, shapes):
    flat = packed.reshape(-1)
    out, off = [], 0
    for s in shapes:
        n = math.prod(s)
        out.append(flat[off:off + n].reshape(s))
        off += n
    return out


def kernel(x, w_in, w_conv, w_pool, pool_scale, sgu_ln_g, w_spatial, b_spatial, w_o, ln1_g, ln1_b, w_gate_up, w_down, ln2_g, ln2_b, loss_target, m_w_in, m_w_conv, m_w_pool, m_pool_scale, m_sgu_ln_g, m_w_spatial, m_b_spatial, m_w_o, m_ln1_g, m_ln1_b, m_w_gate_up, m_w_down, m_ln2_g, m_ln2_b, v_w_in, v_w_conv, v_w_pool, v_pool_scale, v_sgu_ln_g, v_w_spatial, v_b_spatial, v_w_o, v_ln1_g, v_ln1_b, v_w_gate_up, v_w_down, v_ln2_g, v_ln2_b):
    L = DEPTH
    T = x.shape[1]
    mx, my, mc = _my_place()
    dev = 4 * mx + 2 * my + mc
    xs = x[0]
    target = loss_target[0]

    conv_cols = w_conv.shape[2]
    w_conv_z = lax.dynamic_update_slice(jnp.zeros((L, 3, CONV_W), F32), w_conv, (0, 0, dev * conv_cols))
    w_conv_packed = _allreduce_small(_pack([w_conv_z]))
    w_conv_full = _unpack(w_conv_packed, [(L, 3, CONV_W)])[0]

    shards = (jnp.swapaxes(w_in, 1, 2).astype(BF16), jnp.swapaxes(w_gate_up, 1, 2).astype(BF16),
              w_o.astype(BF16), w_down.astype(BF16))
    first_gather = _ag_start_layer(shards, 0, [w_conv_packed])

    grad_x2, big_grads, small_grads = _local_step(
        xs, target, shards, first_gather, w_conv_full, w_pool, pool_scale, sgu_ln_g, w_spatial, b_spatial,
        ln1_g, ln1_b, ln2_g, ln2_b)
    grad_x = grad_x2[None]
    big_w = (w_in, w_gate_up, w_o, w_down)
    big_m = (m_w_in, m_w_gate_up, m_w_o, m_w_down)
    big_v = (v_w_in, v_w_gate_up, v_w_o, v_w_down)
    small_w = [w_conv_full, w_pool, pool_scale, sgu_ln_g, w_spatial, b_spatial, ln1_g, ln1_b, ln2_g, ln2_b]
    small_m = [m_w_conv, m_w_pool, m_pool_scale, m_sgu_ln_g, m_w_spatial, m_b_spatial, m_ln1_g, m_ln1_b, m_ln2_g, m_ln2_b]
    small_v = [v_w_conv, v_w_pool, v_pool_scale, v_sgu_ln_g, v_w_spatial, v_b_spatial, v_ln1_g, v_ln1_b, v_ln2_g, v_ln2_b]
    loss, grads, deltas, new_m, new_v = _reduce_and_update(
        big_grads, small_grads, big_w, big_m, big_v, small_w, small_m, small_v)
    return (loss, grad_x, *grads, *deltas, *new_m, *new_v)


def _ag_start_layer(shards, l, after):
    s_in, s_gu, s_o, s_dn = [s[l] for s in shards]
    first = _ag_start([s_in, s_o], "%da" % l, after=after)
    return first, _ag_start([s_gu, s_dn], "%db" % l, after=[first[4]])


def _ag_finish(gather, after, tag):
    send_sems, recv_sems, shards, lands, _ = gather
    shards, lands = _ag_wait(send_sems, recv_sems, shards, lands, after, tag)
    return _ag_pass_on(shards, lands)


def _rs_begin(parts, tag, after=()):
    return _rs_sibling_start([p.reshape(4, 2, p.shape[0] // N_DEV, D_MODEL) for p in parts], tag, after)


def _rs_continue(sibling_flight, after, c_arr, tag):
    send_sems, recv_sems, parts, lands, _ = sibling_flight
    parts, got = _rs_sibling_wait(send_sems, recv_sems, parts, lands, after, tag)
    return _rs_chip_start(_rs_chip_sum(parts, got, c_arr), tag)


def _local_step(xs, target, shards, gather, w_conv_full, w_pool, pool_scale, sgu_ln_g, w_spatial, b_spatial,
                ln1_g, ln1_b, ln2_g, ln2_b):
    L = DEPTH
    T = xs.shape[0]
    mx, my, mc = _my_place()
    c_arr = jnp.reshape(mc, (1,)).astype(jnp.int32)
    q_arr = jnp.reshape(2 * mx + my, (1,)).astype(jnp.int32)
    eye2 = jnp.eye(2, dtype=F32)
    wp = w_pool.reshape(L, 2, 2, HALF, HALF)
    wpool_bd = jnp.einsum("ltgcd,gh->ltgchd", wp, eye2).reshape(L, 2, LANES, LANES)
    wsp_t = w_spatial.reshape(L, 3, 2 * CHUNK, CHUNK)
    bias_t = jnp.repeat(jnp.swapaxes(b_spatial.reshape(L, 3, 2, CHUNK), 2, 3), HALF, axis=3)
    mixer_w = (w_conv_full, wpool_bd, pool_scale[:, None, :], sgu_ln_g[:, None, :], wsp_t, bias_t)
    g1, b1, g2, b2 = [a[:, None, :] for a in (ln1_g, ln1_b, ln2_g, ln2_b)]
    one, zero = jnp.ones((1, 1, D_MODEL), F32), jnp.zeros((1, 1, D_MODEL), F32)

    saved = []
    prev, pg, pb = xs, (one, 0), (zero, 0)
    prev_b = xs.astype(BF16)
    weights = []
    for l in range(L):
        g_in, g_o = _ag_finish(gather[0], [] if l == 0 else [prev_b], "%da" % l)
        proj = _mm(prev_b, g_in, "nt", F32, 512, IN_W, "mm_proj", deps=[gather[1][4]] if l == 0 else [],
                   lane_tiles=True)
        mixcat = _mixer_fwd(proj, *mixer_w, l)
        xhat1, rstd1, h_b = _mm_ln_fwd(mixcat, g_o, prev, pg, pb, (g1, l), (b1, l), "mm_wo_ln")
        g_gu, g_dn = _ag_finish(gather[1], [h_b], "%db" % l)
        weights.append((g_in, g_gu, g_o, g_dn))
        deps = []
        if l + 1 < L:
            gather = _ag_start_layer(shards, l + 1, [g_gu])
            deps = [gather[1][4]]
        g_act, u_act, act = _mm_swiglu_fwd(h_b, g_gu, deps=deps)
        xhat2, rstd2, y_b = _mm_ln_fwd(act, g_dn, xhat1, (g1, l), (b1, l), (g2, l), (b2, l), "mm_down_ln")
        saved.append((prev_b, proj, mixcat, xhat1, rstd1, h_b, g_act, u_act, act, xhat2, rstd2))
        prev, pg, pb, prev_b = xhat2, (g2, l), (b2, l), y_b


    small = [None] * L
    big = None
    sibling_flight = None
    above = None
    for l in reversed(range(L)):
        prev_b, proj, mixcat, xhat1, rstd1, h_b, g_act, u_act, act, xhat2, rstd2 = saved[l]
        g_in, g_gu, g_o, g_dn = weights[l]
        chip_flight = None
        if above is None:
            loss_tile, dr2, dr2_b, dg2, db2 = _loss_ln_bwd(xhat2, rstd2, (g2, l), (b2, l), target)
        else:
            dr2, dr2_b, dg2, db2 = _mm_ln_bwd([above[0]], above[1], above[2], xhat2, rstd2, (g2, l),
                                              "mm_dx_ln", deps=[sibling_flight[4]])
            chip_flight = _rs_continue(sibling_flight, [dr2_b], c_arr, str(l + 1))
        dg_b, du_b = _mm_swiglu_bwd(dr2_b, g_dn, g_act, u_act, deps=[chip_flight[4]] if chip_flight else [])
        p_dn = _mm(act, dr2_b, "tn", BF16, DW_TM, D_MODEL, "mm_dw_down")
        p_gu = _mm_tn_pair(dg_b, du_b, h_b, DW_TM, "mm_dw_gate_up")
        ffn_sibling = _rs_begin([p_gu, p_dn], "0b") if l == 0 else None
        dr1, dr1_b, dg1, db1, dmix = _mm_ln_bwd([dg_b, du_b], g_gu, dr2, xhat1, rstd1, (g1, l), "mm_dh_ln",
                                                deps=[ffn_sibling[4]] if l == 0 else [], w_back=g_o)
        ffn_flight = _rs_continue(ffn_sibling, [dr1_b], c_arr, "0b") if l == 0 else None
        p_o = _mm(mixcat, dr1_b, "tn", BF16, 512, D_MODEL, "mm_dw_o")
        dproj, dwc, dwp, dps, dlng, dwsp, dbias = _mixer_bwd(proj, dmix, *mixer_w, l,
                                                             deps=[ffn_flight[4]] if l == 0 else [])
        p_in = _mm(dproj, prev_b, "tn", BF16, IN_W, D_MODEL // 2, "mm_dw_in")
        small[l] = (dwc, dwp, dps, dlng, dwsp, dbias, dg1, db1, dg2, db2)
        above = (dproj, g_in, dr1)
        if chip_flight is not None:
            big = list(_rs_chip_finish(chip_flight, [p_in], q_arr, str(l + 1), l + 1, big))
        if l > 0:
            sibling_flight = _rs_begin([p_in, p_gu, p_o, p_dn], str(l))
        else:
            big[1], big[3] = _rs_chip_finish(ffn_flight, [p_in, p_o], q_arr, "0b", 0, [big[1], big[3]])

    def stack(i):
        return jnp.stack([small[l][i] for l in range(L)])

    dwp_bd = stack(1).reshape(L, 2, 2, HALF, 2, HALF)
    dwp_all = jnp.einsum("ltgchd,gh->ltgcd", dwp_bd, eye2).reshape(L, 4, HALF, HALF)
    dbs_all = jnp.swapaxes(stack(5)[:, :, :, :2], 2, 3).reshape(L, 6, CHUNK)
    small_grads = [stack(0), dwp_all, stack(2).reshape(L, POOL_W), stack(3).reshape(L, SGU_W),
                   stack(4).reshape(L, 6, CHUNK, CHUNK), dbs_all] + [stack(i).reshape(L, D_MODEL) for i in (6, 7, 8, 9)]
    small_grads.append(loss_tile[0, :1])
    packed_small = _allreduce_small(_pack(small_grads), deps=[big[1]])
    sibling_flight = _rs_begin([p_in, p_o], "0a", after=[packed_small])
    grad_x = _mm_ln_bwd([above[0]], above[1], above[2], None, None, None, "mm_dx_out", deps=[sibling_flight[4]])
    last_flight = _rs_continue(sibling_flight, [grad_x], c_arr, "0a")
    return grad_x, (big, last_flight, q_arr), (packed_small, [a.shape for a in small_grads])


def _rs_chip_finish(in_flight, after, q, tag, layer, into):
    send_sems, recv_sems, sums, lands, _ = in_flight
    sums, got = _rs_chip_wait(send_sems, recv_sems, sums, lands, after, tag)
    return _rs_finish(sums, got, q, layer, into)


def _reduce_and_update(big_grads, small_grads, big_w, big_m, big_v, small_w, small_m, small_v):
    L = DEPTH
    mx, my, mc = _my_place()
    dev = 4 * mx + 2 * my + mc
    conv_cols = CONV_W // N_DEV
    w_in, w_gate_up, w_o, w_down = big_w
    m_w_in, m_w_gate_up, m_w_o, m_w_down = big_m
    v_w_in, v_w_gate_up, v_w_o, v_w_down = big_v
    packed_g, small_shapes = small_grads
    big, last_flight, q_arr = big_grads

    def widen_conv(a):
        return lax.dynamic_update_slice(jnp.zeros((L, 3, CONV_W), F32), a, (0, 0, dev * conv_cols))

    small_m = [widen_conv(small_m[0])] + list(small_m[1:])
    small_v = [widen_conv(small_v[0])] + list(small_v[1:])
    pk_d, pk_m, pk_v = _adamw(_pack(small_w), packed_g, _pack(small_m), _pack(small_v), packed_g.shape[0] // 2)
    sg = _unpack(packed_g, small_shapes)
    sd = _unpack(pk_d, small_shapes)
    sm = _unpack(pk_m, small_shapes)
    sv = _unpack(pk_v, small_shapes)

    def conv_cols_of(a):
        return lax.dynamic_slice(a, (0, 0, dev * conv_cols), (L, 3, conv_cols))

    for lst in (sg, sd, sm, sv):
        lst[0] = conv_cols_of(lst[0])

    tr = lambda a: jnp.swapaxes(a, 1, 2)
    gt_gu, g_w_dn = big[1], big[3]
    d_gu, m_gu, v_gu = [tr(a) for a in _adamw(tr(w_gate_up), gt_gu, tr(m_w_gate_up), tr(v_w_gate_up), gt_gu.shape[1] // 2)]
    d_dn, m_dn, v_dn = _adamw(w_down, g_w_dn, m_w_down, v_w_down, w_down.shape[1])
    gt_in, g_w_o = _rs_chip_finish(last_flight, [d_gu, d_dn, pk_d], q_arr, "0a", 0, [big[0], big[2]])
    d_in, m_in, v_in = [tr(a) for a in _adamw(tr(w_in), gt_in, tr(m_w_in), tr(v_w_in), gt_in.shape[1])]
    d_o, m_o, v_o = _adamw(w_o, g_w_o, m_w_o, v_w_o, w_o.shape[1])
    g_w_in, g_w_gu = tr(gt_in), tr(gt_gu)

    def ordered(big_in, big_o, big_gu, big_dn, sm_list):
        return [big_in, sm_list[0], sm_list[1], sm_list[2], sm_list[3], sm_list[4], sm_list[5], big_o,
                sm_list[6], sm_list[7], big_gu, big_dn, sm_list[8], sm_list[9]]

    grads = ordered(g_w_in, g_w_o, g_w_gu, g_w_dn, sg)
    deltas = ordered(d_in, d_o, d_gu, d_dn, sd)
    new_m = ordered(m_in, m_o, m_gu, m_dn, sm)
    new_v = ordered(v_in, v_o, v_gu, v_dn, sv)
    return sg[10][0], grads, deltas, new_m, new_v
```

```python
import math

import jax
import jax.numpy as jnp
from jax import lax
from jax.experimental import pallas as pl
from jax.experimental.pallas import tpu as pltpu

F32 = jnp.float32
BF16 = jnp.bfloat16
MESH = pl.DeviceIdType.MESH

D_MODEL = 1024
DEPTH = 4
CONV_W = 384
POOL_W = 256
SGU_W = 384
IN_W = 3 * CONV_W + POOL_W + 2 * SGU_W
D_FF = 2816
CHUNK = 128
ALPHA = float((2 * DEPTH) ** 0.25)
LN_EPS = 1e-5
ADAM_LR, ADAM_B1, ADAM_B2, ADAM_EPS, ADAM_WD, ADAM_STEP = 0.001, 0.9, 0.999, 1e-08, 0.01, 10

N_DEV = 8
LANES = 128
HALF = 64
VMEM_LIMIT = 52 * 1024 * 1024

INV_SQRT2 = 0.7071067811865476
INV_SQRT_2PI = 0.3989422804014327


def _cparams(sem=None, **kw):
    if sem is not None:
        kw["dimension_semantics"] = sem
    return pltpu.CompilerParams(vmem_limit_bytes=VMEM_LIMIT, **kw)


_DN = {"nt": (((1,), (1,)), ((), ())), "tn": (((0,), (0,)), ((), ()))}


def _mm(a, b, mode, out_dtype, tm, tn, name, deps=()):
    if mode == "nt":
        (M, K), N = a.shape, b.shape[0]
        a_spec = pl.BlockSpec((tm, K), lambda i, j: (i, 0))
        b_spec = pl.BlockSpec((tn, K), lambda i, j: (j, 0))
    else:
        (K, M), N = a.shape, b.shape[1]
        a_spec = pl.BlockSpec((K, tm), lambda i, j: (0, i))
        b_spec = pl.BlockSpec((K, tn), lambda i, j: (0, j))
    assert M % tm == 0 and N % tn == 0, (M, N, K, tm, tn)
    nd = len(deps)

    def body(*refs):
        a_ref, b_ref, o_ref = refs[0], refs[1], refs[2 + nd]
        o_ref[...] = lax.dot_general(a_ref[...], b_ref[...], _DN[mode], preferred_element_type=F32).astype(o_ref.dtype)

    return pl.pallas_call(
        body,
        name=name,
        grid=(M // tm, N // tn),
        in_specs=[a_spec, b_spec] + [pl.BlockSpec(memory_space=pl.ANY)] * nd,
        out_specs=pl.BlockSpec((tm, tn), lambda i, j: (i, j)),
        out_shape=jax.ShapeDtypeStruct((M, N), out_dtype),
        compiler_params=_cparams(("parallel", "parallel")),
    )(a, b, *deps)


def _mm_tn_pair(a1, a2, b, tm, name):
    K, M = a1.shape
    N = b.shape[1]
    n1 = M // tm

    def body(a1_ref, a2_ref, b_ref, o_ref):
        i = pl.program_id(0)

        @pl.when(i < n1)
        def _():
            o_ref[...] = lax.dot_general(a1_ref[...], b_ref[...], _DN["tn"], preferred_element_type=F32).astype(o_ref.dtype)

        @pl.when(i >= n1)
        def _():
            o_ref[...] = lax.dot_general(a2_ref[...], b_ref[...], _DN["tn"], preferred_element_type=F32).astype(o_ref.dtype)

    return pl.pallas_call(
        body, name=name, grid=(2 * n1,),
        in_specs=[pl.BlockSpec((K, tm), lambda i: (0, jnp.minimum(i, n1 - 1))),
                  pl.BlockSpec((K, tm), lambda i: (0, jnp.maximum(i - n1, 0))),
                  pl.BlockSpec((K, N), lambda i: (0, 0))],
        out_specs=pl.BlockSpec((tm, N), lambda i: (i, 0)),
        out_shape=jax.ShapeDtypeStruct((2 * M, N), BF16),
        compiler_params=_cparams(("arbitrary",)),
    )(a1, a2, b)


LN_SUB = 256
LN_TM = 512


def _vec(v):
    arr, layer = v
    return arr, pl.BlockSpec((None, 1, D_MODEL), lambda *_: (layer, 0, 0))


def _mm_ln_fwd(a, b, prev, pg, pb, g, bias, name):
    T, K = a.shape
    tm = LN_TM

    def body(a_ref, b_ref, prev_ref, pg_ref, pb_ref, g_ref, bias_ref, xhat_ref, rstd_ref, y_ref):
        for s in range(tm // LN_SUB):
            rows = slice(s * LN_SUB, (s + 1) * LN_SUB)
            mm = jnp.dot(a_ref[rows, :], b_ref[...], preferred_element_type=F32)
            r = ALPHA * (prev_ref[rows, :] * pg_ref[...] + pb_ref[...]) + mm
            mu = jnp.mean(r, axis=-1, keepdims=True)
            xc = r - mu
            var = jnp.mean(xc * xc, axis=-1, keepdims=True)
            rstd = lax.rsqrt(var + LN_EPS)
            xhat = xc * rstd
            xhat_ref[rows, :] = xhat
            rstd_ref[rows, :] = rstd
            y_ref[rows, :] = (xhat * g_ref[...] + bias_ref[...]).astype(y_ref.dtype)

    row = pl.BlockSpec((tm, D_MODEL), lambda i: (i, 0))
    vecs = [_vec(v) for v in (pg, pb, g, bias)]
    return pl.pallas_call(
        body, name=name, grid=(T // tm,),
        in_specs=[pl.BlockSpec((tm, K), lambda i: (i, 0)),
                  pl.BlockSpec((K, D_MODEL), lambda i: (0, 0), pipeline_mode=pl.Buffered(1)),
                  row] + [s for _, s in vecs],
        out_specs=[row, pl.BlockSpec((tm, 1), lambda i: (i, 0)), row],
        out_shape=[jax.ShapeDtypeStruct((T, D_MODEL), F32), jax.ShapeDtypeStruct((T, 1), F32),
                   jax.ShapeDtypeStruct((T, D_MODEL), BF16)],
        compiler_params=_cparams(("parallel",)),
    )(a, b, prev, *[a_ for a_, _ in vecs])


def _mm_ln_bwd(a_list, b, dres, xhat, rstd, g, name, deps=(), w_back=None):
    T = a_list[0].shape[0]
    tm = LN_TM
    na, nd = len(a_list), len(deps)
    ks = [a.shape[1] for a in a_list]
    last = xhat is None
    nout = 1 if last else (5 if w_back is not None else 4)

    def body(*refs):
        a_refs, b_ref, dres_ref = refs[:na], refs[na], refs[na + 1]
        if not last:
            xhat_ref, rstd_ref, g_ref = refs[na + 2:na + 5]
            dr_ref, drb_ref, dg_ref, db_ref = refs[len(refs) - nout:len(refs) - nout + 4]

            @pl.when(pl.program_id(0) == 0)
            def _():
                dg_ref[...] = jnp.zeros_like(dg_ref)
                db_ref[...] = jnp.zeros_like(db_ref)

        for s in range(tm // LN_SUB):
            rows = slice(s * LN_SUB, (s + 1) * LN_SUB)
            mm, off = None, 0
            for a_ref, k in zip(a_refs, ks):
                part = jnp.dot(a_ref[rows, :], b_ref[off:off + k, :], preferred_element_type=F32)
                mm = part if mm is None else mm + part
                off += k
            dy = ALPHA * dres_ref[rows, :] + mm
            if last:
                refs[-1][rows, :] = dy
                continue
            xhat_v = xhat_ref[rows, :]
            dg_ref[...] += jnp.sum(dy * xhat_v, axis=0, keepdims=True)
            db_ref[...] += jnp.sum(dy, axis=0, keepdims=True)
            dxh = dy * g_ref[...]
            m1 = jnp.mean(dxh, axis=-1, keepdims=True)
            m2 = jnp.mean(dxh * xhat_v, axis=-1, keepdims=True)
            dr = rstd_ref[rows, :] * (dxh - m1 - xhat_v * m2)
            dr_ref[rows, :] = dr
            dr_b = dr.astype(drb_ref.dtype)
            drb_ref[rows, :] = dr_b
            if w_back is not None:
                refs[-1][rows, :] = lax.dot_general(dr_b, refs[na + 5][...], _DN["nt"], preferred_element_type=F32)

    row = pl.BlockSpec((tm, D_MODEL), lambda i: (i, 0))
    vec = pl.BlockSpec((1, D_MODEL), lambda i: (0, 0))
    in_specs = [pl.BlockSpec((tm, k), lambda i: (i, 0)) for k in ks]
    in_specs += [pl.BlockSpec((sum(ks), D_MODEL), lambda i: (0, 0), pipeline_mode=pl.Buffered(1)), row]
    args = list(a_list) + [b, dres]
    if last:
        out_specs, out_shape = row, jax.ShapeDtypeStruct((T, D_MODEL), F32)
    else:
        g_arr, g_spec = _vec(g)
        in_specs += [row, pl.BlockSpec((tm, 1), lambda i: (i, 0)), g_spec]
        args += [xhat, rstd, g_arr]
        out_specs = [row, row, vec, vec]
        out_shape = [jax.ShapeDtypeStruct((T, D_MODEL), F32), jax.ShapeDtypeStruct((T, D_MODEL), BF16),
                     jax.ShapeDtypeStruct((1, D_MODEL), F32), jax.ShapeDtypeStruct((1, D_MODEL), F32)]
        if w_back is not None:
            in_specs.append(pl.BlockSpec(w_back.shape, lambda i: (0, 0), pipeline_mode=pl.Buffered(1)))
            args.append(w_back)
            out_specs.append(row)
            out_shape.append(jax.ShapeDtypeStruct((T, w_back.shape[0]), F32))
    return pl.pallas_call(
        body, name=name, grid=(T // tm,),
        in_specs=in_specs + [pl.BlockSpec(memory_space=pl.ANY)] * nd,
        out_specs=out_specs, out_shape=out_shape,
        compiler_params=_cparams(("parallel",) if last else ("arbitrary",)),
    )(*args, *deps)


DW_TM = 1408
FF_TN = 256
FF_TM = 2048
SAVED_GU = BF16


def _mm_swiglu_fwd(h, w_gu, deps=()):
    T = h.shape[0]
    tm = min(T, FF_TM)
    nj = D_FF // FF_TN
    nd = len(deps)

    def body(*refs):
        h_ref, wg_ref, wu_ref = refs[:3]
        g_ref, u_ref, act_ref = refs[3 + nd:]
        hv = h_ref[...]
        gv = lax.dot_general(hv, wg_ref[...], _DN["nt"], preferred_element_type=F32)
        uv = lax.dot_general(hv, wu_ref[...], _DN["nt"], preferred_element_type=F32)
        g_ref[...] = gv.astype(g_ref.dtype)
        u_ref[...] = uv.astype(u_ref.dtype)
        act_ref[...] = (gv * jax.nn.sigmoid(gv) * uv).astype(act_ref.dtype)

    tile = pl.BlockSpec((tm, FF_TN), lambda j, i: (i, j))
    return pl.pallas_call(
        body, name="mm_gate_up_swiglu", grid=(nj, T // tm),
        in_specs=[pl.BlockSpec((tm, D_MODEL), lambda j, i: (i, 0)),
                  pl.BlockSpec((FF_TN, D_MODEL), lambda j, i: (j, 0)),
                  pl.BlockSpec((FF_TN, D_MODEL), lambda j, i: (j + nj, 0))] + [pl.BlockSpec(memory_space=pl.ANY)] * nd,
        out_specs=[tile, tile, tile],
        out_shape=[jax.ShapeDtypeStruct((T, D_FF), SAVED_GU), jax.ShapeDtypeStruct((T, D_FF), SAVED_GU),
                   jax.ShapeDtypeStruct((T, D_FF), BF16)],
        compiler_params=_cparams(("parallel", "parallel")),
    )(h, w_gu, w_gu, *deps)


def _mm_swiglu_bwd(dr, w_dn, g, u, deps=()):
    T = dr.shape[0]
    tm = min(T, FF_TM)

    def body(*refs):
        dr_ref, w_ref, g_ref, u_ref = refs[:4]
        dg_ref, du_ref = refs[-2:]
        da = lax.dot_general(dr_ref[...], w_ref[...], _DN["nt"], preferred_element_type=F32)
        gv, uv = g_ref[...].astype(F32), u_ref[...].astype(F32)
        s = jax.nn.sigmoid(gv)
        du_ref[...] = (da * (gv * s)).astype(du_ref.dtype)
        dg_ref[...] = (da * uv * (s * (1.0 + gv * (1.0 - s)))).astype(dg_ref.dtype)

    tile = pl.BlockSpec((tm, FF_TN), lambda j, i: (i, j))
    return pl.pallas_call(
        body, name="mm_dact_swiglu", grid=(D_FF // FF_TN, T // tm),
        in_specs=[pl.BlockSpec((tm, D_MODEL), lambda j, i: (i, 0)), pl.BlockSpec((FF_TN, D_MODEL), lambda j, i: (j, 0)),
                  tile, tile] + [ANY] * len(deps),
        out_specs=[tile, tile],
        out_shape=[jax.ShapeDtypeStruct((T, D_FF), BF16)] * 2,
        compiler_params=_cparams(("parallel", "parallel")),
    )(dr, w_dn, g, u, *deps)


def _gelu(x):
    return 0.5 * x * (1.0 + lax.erf(x * INV_SQRT2))


def _gelu_grad(x):
    return 0.5 * (1.0 + lax.erf(x * INV_SQRT2)) + x * (jnp.exp(-0.5 * x * x) * INV_SQRT_2PI)


def _shift_down(z, k):
    row = lax.broadcasted_iota(jnp.int32, z.shape, 0)
    return jnp.where(row >= k, pltpu.roll(z, k, 0), 0.0)


def _shift_up(z, k):
    n = z.shape[0]
    row = lax.broadcasted_iota(jnp.int32, z.shape, 0)
    return jnp.where(row < n - k, pltpu.roll(z, n - k, 0), 0.0)


def _lo_mask(shape):
    return lax.broadcasted_iota(jnp.int32, shape, len(shape) - 1) < HALF


def _seg_mean(x, lo):
    a = jnp.sum(jnp.where(lo, x, 0.0), axis=-1, keepdims=True)
    b = jnp.sum(jnp.where(lo, 0.0, x), axis=-1, keepdims=True)
    return jnp.where(lo, a, b) * (1.0 / HALF)


def _pool_windows(first):
    lo = _lo_mask((1, LANES))
    return jnp.where(first, jnp.where(lo, 2.0, 4.0), jnp.where(lo, 8.0, 16.0)), lo


def _pool_mean_minus_token(p, first):
    wl, lo = _pool_windows(first)
    s2 = p + _shift_down(p, 1)
    s4 = s2 + _shift_down(s2, 2)
    s8 = s4 + _shift_down(s4, 4)
    s16 = s8 + _shift_down(s8, 8)
    win = jnp.where(first, jnp.where(lo, s2, s4), jnp.where(lo, s8, s16))
    t1 = (lax.broadcasted_iota(jnp.int32, p.shape, 0) + 1).astype(F32)
    count = jnp.minimum(t1, wl)
    return win / count - p, count


SGU_UNROLL = 4


def _tril_keep():
    r = lax.broadcasted_iota(jnp.int32, (2 * CHUNK, CHUNK), 0)
    s = lax.broadcasted_iota(jnp.int32, (2 * CHUNK, CHUNK), 1)
    return s <= (r & (CHUNK - 1))


def _sgu_chunk_fwd(u, v, g, wm, bias, lo):
    ug = _gelu(u)
    vg = _gelu(v)
    mu = _seg_mean(vg, lo)
    xc = vg - mu
    var = _seg_mean(xc * xc, lo)
    rstd = lax.rsqrt(var + LN_EPS)
    vn = xc * rstd
    vh = (vn * g).astype(BF16)
    mm2 = jnp.dot(wm, vh, preferred_element_type=F32)
    mixed = jnp.where(lo, mm2[:CHUNK], mm2[CHUNK:]) + bias
    return ug, vn, rstd, vh, mixed


def _mixer_fwd(proj, wconv, wpool_bd, pscale, lng, wsp, bias, layer):
    T = proj.shape[0]
    nchunk = T // CHUNK

    def body(a_ref, b_ref, c_ref, wc_ref, wp_ref, ps_ref, lng_ref, wsp_ref, bias_ref, o_ref):
        j = pl.program_id(0)

        @pl.when(j < 3)
        def _conv():
            z = c_ref[...] * a_ref[...]
            w = wc_ref[...]
            y = w[0:1] * _shift_down(z, 2) + w[1:2] * _shift_down(z, 1) + w[2:3] * z
            o_ref[...] = (b_ref[...] * y).astype(o_ref.dtype)

        @pl.when((j >= 3) & (j < 5))
        def _pool():
            d, _ = _pool_mean_minus_token(a_ref[...], j == 3)
            y = jnp.dot(d.astype(BF16), wp_ref[...].astype(BF16), preferred_element_type=F32)
            o_ref[...] = (y * ps_ref[...]).astype(o_ref.dtype)

        @pl.when(j >= 5)
        def _sgu():
            lo = _lo_mask((CHUNK, LANES))
            wm = jnp.where(_tril_keep(), wsp_ref[...], 0.0).astype(BF16)
            bias_t = bias_ref[...]
            g = lng_ref[...]

            def chunk(n, carry):
                rows = pl.ds(pl.multiple_of(n * CHUNK, CHUNK), CHUNK)
                ug, _, _, _, mixed = _sgu_chunk_fwd(a_ref[rows, :], b_ref[rows, :], g, wm, bias_t, lo)
                o_ref[rows, :] = (ug * mixed).astype(o_ref.dtype)
                return carry

            lax.fori_loop(0, nchunk, chunk, 0, unroll=SGU_UNROLL)

    def col(f):
        return lambda j: (0, f(j))

    clip = lambda v, lo, hi: jnp.minimum(jnp.maximum(v, lo), hi)
    return pl.pallas_call(
        body,
        name="mixer_fwd",
        grid=(8,),
        in_specs=[
            pl.BlockSpec((T, LANES), col(lambda j: jnp.where(j < 3, j, jnp.where(j < 5, j + 6, j + 6)))),
            pl.BlockSpec((T, LANES), col(lambda j: jnp.where(j < 3, j + 3, jnp.where(j < 5, 5, j + 9)))),
            pl.BlockSpec((T, LANES), col(lambda j: jnp.where(j < 3, j + 6, 8))),
            pl.BlockSpec((3, LANES), col(lambda j: clip(j, 0, 2))),
            pl.BlockSpec((None, None, LANES, LANES), lambda j: (layer, clip(j - 3, 0, 1), 0, 0)),
            pl.BlockSpec((None, 1, LANES), lambda j: (layer, 0, clip(j - 3, 0, 1))),
            pl.BlockSpec((None, 1, LANES), lambda j: (layer, 0, clip(j - 5, 0, 2))),
            pl.BlockSpec((None, None, 2 * CHUNK, CHUNK), lambda j: (layer, clip(j - 5, 0, 2), 0, 0)),
            pl.BlockSpec((None, None, CHUNK, LANES), lambda j: (layer, clip(j - 5, 0, 2), 0, 0)),
        ],
        out_specs=pl.BlockSpec((T, LANES), lambda j: (0, j)),
        out_shape=jax.ShapeDtypeStruct((T, D_MODEL), BF16),
        compiler_params=_cparams(("arbitrary",)),
    )(proj, proj, proj, wconv, wpool_bd, pscale, lng, wsp, bias)


def _mixer_bwd(proj, dmix, wconv, wpool_bd, pscale, lng, wsp, bias, layer, deps=()):
    T = proj.shape[0]
    nchunk = T // CHUNK

    def body(*refs):
        a_ref, b_ref, c_ref, dm_ref, wc_ref, wp_ref, ps_ref, lng_ref, wsp_ref, bias_ref = refs[:10]
        o_ref, dwc_ref, dwp_ref, dps_ref, dlng_ref, dwsp_ref, dbias_ref, keep1, keep2 = refs[10 + len(deps):]
        k = pl.program_id(0)

        @pl.when(k < 3)
        def _conv():
            xa, gb, gc, dya = a_ref[...], b_ref[...], c_ref[...], dm_ref[...]
            w = wc_ref[...]
            z = gc * xa
            z1 = _shift_down(z, 1)
            z2 = _shift_down(z, 2)
            y = w[0:1] * z2 + w[1:2] * z1 + w[2:3] * z
            dyv = dya * gb
            dz = w[2:3] * dyv + w[1:2] * _shift_up(dyv, 1) + w[0:1] * _shift_up(dyv, 2)
            dwc_ref[0:1, :] = jnp.sum(dyv * z2, axis=0, keepdims=True)
            dwc_ref[1:2, :] = jnp.sum(dyv * z1, axis=0, keepdims=True)
            dwc_ref[2:3, :] = jnp.sum(dyv * z, axis=0, keepdims=True)
            o_ref[...] = (dz * gc).astype(o_ref.dtype)
            keep1[k] = (dya * y).astype(keep1.dtype)
            keep1[k + 3] = (dz * xa).astype(keep1.dtype)

        @pl.when((k >= 3) & (k < 9))
        def _emit_gb_gc():
            o_ref[...] = keep1[k - 3]

        @pl.when((k >= 9) & (k < 11))
        def _pool():
            first = k == 9
            p, dyb = a_ref[...], dm_ref[...]
            d, count = _pool_mean_minus_token(p, first)
            w2 = wp_ref[...].astype(BF16)
            db = d.astype(BF16)
            y = jnp.dot(db, w2, preferred_element_type=F32)
            dps_ref[...] = jnp.sum(dyb * y, axis=0, keepdims=True)
            dyv = (dyb * ps_ref[...]).astype(BF16)
            dd = lax.dot_general(dyv, w2, _DN["nt"], preferred_element_type=F32)
            dwp_ref[...] = lax.dot_general(db, dyv, _DN["tn"], preferred_element_type=F32)
            dwin = dd / count
            a2 = dwin + _shift_up(dwin, 1)
            a4 = a2 + _shift_up(a2, 2)
            a8 = a4 + _shift_up(a4, 4)
            a16 = a8 + _shift_up(a8, 8)
            _, lo = _pool_windows(first)
            back = jnp.where(first, jnp.where(lo, a2, a4), jnp.where(lo, a8, a16))
            o_ref[...] = (back - dd).astype(o_ref.dtype)

        @pl.when((k >= 11) & (k < 14))
        def _sgu():
            lo = _lo_mask((CHUNK, LANES))
            keep = _tril_keep()
            wm = jnp.where(keep, wsp_ref[...], 0.0).astype(BF16)
            bias_t = bias_ref[...]
            g = lng_ref[...]
            dwsp_ref[...] = jnp.zeros_like(dwsp_ref)
            dbias_ref[...] = jnp.zeros_like(dbias_ref)
            dlng_ref[...] = jnp.zeros_like(dlng_ref)

            def chunk(n, carry):
                rows = pl.ds(pl.multiple_of(n * CHUNK, CHUNK), CHUNK)
                u, v, dyc = a_ref[rows, :], b_ref[rows, :], dm_ref[rows, :]
                ug, vn, rstd, vh, mixed = _sgu_chunk_fwd(u, v, g, wm, bias_t, lo)
                dmx = dyc * ug
                o_ref[rows, :] = (dyc * mixed * _gelu_grad(u)).astype(o_ref.dtype)
                dbias_ref[...] += dmx
                dst = jnp.concatenate([jnp.where(lo, dmx, 0.0), jnp.where(lo, 0.0, dmx)], axis=0).astype(BF16)
                dwsp_ref[...] += lax.dot_general(dst, vh, _DN["nt"], preferred_element_type=F32)
                dvh = lax.dot_general(wm, dst, _DN["tn"], preferred_element_type=F32)
                dlng_ref[...] += jnp.sum(dvh * vn, axis=0, keepdims=True)
                dvn = dvh * g
                m1 = _seg_mean(dvn, lo)
                m2 = _seg_mean(dvn * vn, lo)
                dvg = rstd * (dvn - m1 - vn * m2)
                keep2[k - 11, rows, :] = (dvg * _gelu_grad(v)).astype(keep2.dtype)
                return carry

            lax.fori_loop(0, nchunk, chunk, 0, unroll=SGU_UNROLL)
            dwsp_ref[...] = jnp.where(keep, dwsp_ref[...], 0.0)
            dbt = dbias_ref[...]
            lane = lax.broadcasted_iota(jnp.int32, (CHUNK, LANES), 1)
            sa = jnp.sum(jnp.where(lo, dbt, 0.0), axis=-1, keepdims=True)
            sb = jnp.sum(jnp.where(lo, 0.0, dbt), axis=-1, keepdims=True)
            dbias_ref[...] = jnp.where(lane == 0, sa, jnp.where(lane == 1, sb, 0.0))

        @pl.when(k >= 14)
        def _emit_v():
            o_ref[...] = keep2[k - 14]

    def col(f):
        return lambda k: (0, f(k))

    clip = lambda v, lo, hi: jnp.minimum(jnp.maximum(v, lo), hi)
    view_a = lambda k: jnp.where(k < 3, k, jnp.where(k < 9, 2, jnp.where(k < 14, k, 13)))
    view_b = lambda k: jnp.where(k < 3, k + 3, jnp.where(k < 11, 5, jnp.where(k < 14, k + 3, 16)))
    view_c = lambda k: jnp.where(k < 3, k + 6, 8)
    view_dm = lambda k: jnp.where(k < 3, k, jnp.where(k < 9, 2, jnp.where(k < 14, k - 6, 7)))
    return pl.pallas_call(
        body,
        name="mixer_bwd",
        grid=(17,),
        in_specs=[
            pl.BlockSpec((T, LANES), col(view_a)),
            pl.BlockSpec((T, LANES), col(view_b)),
            pl.BlockSpec((T, LANES), col(view_c)),
            pl.BlockSpec((T, LANES), col(view_dm)),
            pl.BlockSpec((3, LANES), col(lambda k: clip(k, 0, 2))),
            pl.BlockSpec((None, None, LANES, LANES), lambda k: (layer, clip(k - 9, 0, 1), 0, 0)),
            pl.BlockSpec((None, 1, LANES), lambda k: (layer, 0, clip(k - 9, 0, 1))),
            pl.BlockSpec((None, 1, LANES), lambda k: (layer, 0, clip(k - 11, 0, 2))),
            pl.BlockSpec((None, None, 2 * CHUNK, CHUNK), lambda k: (layer, clip(k - 11, 0, 2), 0, 0)),
            pl.BlockSpec((None, None, CHUNK, LANES), lambda k: (layer, clip(k - 11, 0, 2), 0, 0)),
        ] + [pl.BlockSpec(memory_space=pl.ANY)] * len(deps),
        out_specs=[
            pl.BlockSpec((T, LANES), lambda k: (0, k)),
            pl.BlockSpec((3, LANES), col(lambda k: clip(k, 0, 2))),
            pl.BlockSpec((None, LANES, LANES), lambda k: (clip(k - 9, 0, 1), 0, 0)),
            pl.BlockSpec((1, LANES), col(lambda k: clip(k - 9, 0, 1))),
            pl.BlockSpec((1, LANES), col(lambda k: clip(k - 11, 0, 2))),
            pl.BlockSpec((None, 2 * CHUNK, CHUNK), lambda k: (clip(k - 11, 0, 2), 0, 0)),
            pl.BlockSpec((None, CHUNK, LANES), lambda k: (clip(k - 11, 0, 2), 0, 0)),
        ],
        out_shape=[
            jax.ShapeDtypeStruct((T, IN_W), BF16),
            jax.ShapeDtypeStruct((3, CONV_W), F32),
            jax.ShapeDtypeStruct((2, LANES, LANES), F32),
            jax.ShapeDtypeStruct((1, POOL_W), F32),
            jax.ShapeDtypeStruct((1, SGU_W), F32),
            jax.ShapeDtypeStruct((3, 2 * CHUNK, CHUNK), F32),
            jax.ShapeDtypeStruct((3, CHUNK, LANES), F32),
        ],
        scratch_shapes=[pltpu.VMEM((6, T, LANES), BF16), pltpu.VMEM((3, T, LANES), BF16)],
        compiler_params=_cparams(("arbitrary",)),
    )(proj, proj, proj, dmix, wconv, wpool_bd, pscale, lng, wsp, bias, *deps)


def _loss_ln_bwd(xhat, rstd, g, b, target, tm=256):
    T = xhat.shape[0]

    def body(xhat_ref, rstd_ref, g_ref, b_ref, t_ref, loss_ref, dr_ref, drb_ref, dg_ref, db_ref):
        xhat_v = xhat_ref[...]
        err = xhat_v * g_ref[...] + b_ref[...] - t_ref[...]
        dy = err * (1.0 / D_MODEL)

        @pl.when(pl.program_id(0) == 0)
        def _():
            loss_ref[...] = jnp.zeros_like(loss_ref)
            dg_ref[...] = jnp.zeros_like(dg_ref)
            db_ref[...] = jnp.zeros_like(db_ref)

        part = jnp.sum(jnp.sum(err * err, axis=-1, keepdims=True), axis=0, keepdims=True)
        loss_ref[...] += jnp.broadcast_to(part * (0.5 / D_MODEL), loss_ref.shape)
        dg_ref[...] += jnp.sum(dy * xhat_v, axis=0, keepdims=True)
        db_ref[...] += jnp.sum(dy, axis=0, keepdims=True)
        dxh = dy * g_ref[...]
        m1 = jnp.mean(dxh, axis=-1, keepdims=True)
        m2 = jnp.mean(dxh * xhat_v, axis=-1, keepdims=True)
        dr = rstd_ref[...] * (dxh - m1 - xhat_v * m2)
        dr_ref[...] = dr
        drb_ref[...] = dr.astype(drb_ref.dtype)

    row = pl.BlockSpec((tm, D_MODEL), lambda i: (i, 0))
    vec = pl.BlockSpec((1, D_MODEL), lambda i: (0, 0))
    (g_arr, g_spec), (b_arr, b_spec) = _vec(g), _vec(b)
    return pl.pallas_call(
        body,
        name="loss_ln_bwd",
        grid=(T // tm,),
        in_specs=[row, pl.BlockSpec((tm, 1), lambda i: (i, 0)), g_spec, b_spec, row],
        out_specs=[pl.BlockSpec((8, LANES), lambda i: (0, 0)), row, row, vec, vec],
        out_shape=[jax.ShapeDtypeStruct((8, LANES), F32),
                   jax.ShapeDtypeStruct((T, D_MODEL), F32), jax.ShapeDtypeStruct((T, D_MODEL), BF16),
                   jax.ShapeDtypeStruct((1, D_MODEL), F32), jax.ShapeDtypeStruct((1, D_MODEL), F32)],
        compiler_params=_cparams(("arbitrary",)),
    )(xhat, rstd, g_arr, b_arr, target)


def _adamw(w, g, m, v, tr):
    R, C = w.shape[-2:]
    assert R % tr == 0
    c1 = 1.0 - ADAM_B1 ** ADAM_STEP
    c2 = 1.0 - ADAM_B2 ** ADAM_STEP

    def body(w_ref, g_ref, m_ref, v_ref, d_ref, mo_ref, vo_ref):
        gv = g_ref[...]
        mn = ADAM_B1 * m_ref[...] + (1.0 - ADAM_B1) * gv
        vn = ADAM_B2 * v_ref[...] + (1.0 - ADAM_B2) * (gv * gv)
        d_ref[...] = -ADAM_LR * ((mn / c1) / (jnp.sqrt(vn / c2) + ADAM_EPS) + ADAM_WD * w_ref[...])
        mo_ref[...] = mn
        vo_ref[...] = vn

    if w.ndim == 2:
        grid, blk = (R // tr,), pl.BlockSpec((tr, C), lambda i: (i, 0))
    else:
        grid, blk = (w.shape[0], R // tr), pl.BlockSpec((None, tr, C), lambda l, i: (l, i, 0))
    return pl.pallas_call(
        body, name="adamw", grid=grid, in_specs=[blk] * 4, out_specs=[blk] * 3,
        out_shape=[jax.ShapeDtypeStruct(w.shape, F32)] * 3, compiler_params=_cparams(("parallel",) * len(grid)),
    )(w, g, m, v)


def _my_place():
    return lax.axis_index("x"), lax.axis_index("y"), lax.axis_index("c")


ANY = pl.BlockSpec(memory_space=pl.ANY)
HBM = pl.BlockSpec(memory_space=pltpu.HBM)
SEM = pl.BlockSpec(memory_space=pltpu.SEMAPHORE)
EFFECT = pltpu.SideEffectType.DATAFLOW_SIDE_EFFECTING


def _in_hbm(a):
    return pltpu.with_memory_space_constraint(a, pltpu.HBM)


def _block_rows(ref, dev):
    r = ref.shape[0] // N_DEV
    start = pl.multiple_of((4 * dev[0] + 2 * dev[1] + dev[2]) * r, 16)
    return ref.at[pl.ds(start, r), :]


def _ag_first_copies(s_refs, land_refs, send_sems, recv_sems, receiving):
    x, y, c = _my_place()
    peers = [(x, y, 1 - c)] + [(*chip, c) for chip in _other_chips(x, y)]
    copies = []
    for k, peer in enumerate(peers):
        block = peer if receiving else (x, y, c)
        copies += [pltpu.make_async_remote_copy(
            src_ref=s_refs[w], dst_ref=_block_rows(land_refs[w], block),
            send_sem=send_sems.at[k * len(s_refs) + w], recv_sem=recv_sems.at[k * len(s_refs) + w],
            device_id=peer, device_id_type=MESH)
            for w in range(len(s_refs))]
    return copies


def _ag_start(shards, layer, after=()):
    nw = len(shards)

    def body(*refs):
        s_refs, land_refs = refs[:nw], refs[nw:2 * nw]
        token = refs[-1]
        sems = 2 * nw + len(after)
        for cp in _ag_first_copies(s_refs, land_refs, refs[sems], refs[sems + 1], False):
            cp.start()
        token[...] = jnp.zeros_like(token)

    lands = [lax.empty((N_DEV * s.shape[0], D_MODEL), BF16) for s in shards]
    out = pl.pallas_call(
        body, name="ag_start_%s" % layer,
        in_specs=[HBM] * (2 * nw) + [ANY] * len(after),
        out_specs=(SEM, SEM, *[HBM] * (2 * nw), pl.BlockSpec(memory_space=pltpu.VMEM)),
        out_shape=(pltpu.SemaphoreType.DMA((4 * nw,)), pltpu.SemaphoreType.DMA((4 * nw,)),
                   *[pltpu.HBM(a.shape, a.dtype) for a in list(shards) + lands],
                   jax.ShapeDtypeStruct((8, LANES), F32)),
        input_output_aliases={i: 2 + i for i in range(2 * nw)},
        compiler_params=pltpu.CompilerParams(has_side_effects=EFFECT),
    )(*[_in_hbm(a) for a in list(shards) + lands], *after)
    return out[0], out[1], out[2:2 + nw], out[2 + nw:2 + 2 * nw], out[-1]


def _ag_wait(send_sems, recv_sems, shards, lands, after, layer):
    nw = len(shards)

    def body(*refs):
        s_refs, land_refs = refs[:nw], refs[nw:2 * nw]
        for cp in _ag_first_copies(s_refs, land_refs, refs[2 * nw], refs[2 * nw + 1], True):
            cp.wait_send()
            cp.wait_recv()

    out = pl.pallas_call(
        body, name="ag_wait_%s" % layer,
        in_specs=[HBM] * (2 * nw) + [SEM, SEM] + [ANY] * len(after),
        out_specs=[HBM] * (2 * nw),
        out_shape=[pltpu.HBM(a.shape, a.dtype) for a in list(shards) + list(lands)],
        input_output_aliases={i: i for i in range(2 * nw)},
        compiler_params=pltpu.CompilerParams(has_side_effects=EFFECT),
    )(*shards, *lands, send_sems, recv_sems, *after)
    return out[:nw], out[nw:]


def _ag_pass_on(shards, lands):
    nw = len(shards)

    def body(*refs):
        s_refs, g_refs = refs[:nw], refs[2 * nw:3 * nw]
        send_sems, recv_sems, local_sems = refs[3 * nw:3 * nw + 3]
        stage = refs[3 * nw + 3:]
        x, y, c = _my_place()
        load = [pltpu.make_async_copy(s_refs[w], stage[w], local_sems.at[w]) for w in range(nw)]
        mine = [pltpu.make_async_copy(stage[w], _block_rows(g_refs[w], (x, y, c)), local_sems.at[w])
                for w in range(nw)]
        for cp in load:
            cp.start()
        sends, arrivals = [], []
        for j, chip in enumerate(_other_chips(x, y)):
            for w in range(nw):
                rows_out = _block_rows(g_refs[w], (*chip, c))
                rows_in = _block_rows(g_refs[w], (*chip, 1 - c))
                sends.append(pltpu.make_async_remote_copy(
                    src_ref=rows_out, dst_ref=rows_out, send_sem=send_sems.at[j, w], recv_sem=recv_sems.at[j, w],
                    device_id=(x, y, 1 - c), device_id_type=MESH))
                arrivals.append(pltpu.make_async_remote_copy(
                    src_ref=rows_in, dst_ref=rows_in, send_sem=send_sems.at[j, w], recv_sem=recv_sems.at[j, w],
                    device_id=(x, y, 1 - c), device_id_type=MESH))
        for cp in sends:
            cp.start()
        for w in range(nw):
            load[w].wait()
            mine[w].start()
        for cp in arrivals:
            cp.wait_recv()
        for cp in sends:
            cp.wait_send()
        for cp in mine:
            cp.wait()

    return pl.pallas_call(
        body, name="ag_pass_on",
        in_specs=[ANY] * (2 * nw), out_specs=[ANY] * nw,
        out_shape=[jax.ShapeDtypeStruct(a.shape, a.dtype) for a in lands],
        input_output_aliases={nw + i: i for i in range(nw)},
        scratch_shapes=[pltpu.SemaphoreType.DMA((3, nw)), pltpu.SemaphoreType.DMA((3, nw)),
                        pltpu.SemaphoreType.DMA((nw,))] + [pltpu.VMEM(s.shape, s.dtype) for s in shards],
        compiler_params=_cparams(),
    )(*shards, *lands)


def _rs_sibling_copies(p_refs, land_refs, send_sems, recv_sems):
    x, y, c = _my_place()
    return [pltpu.make_async_remote_copy(
        src_ref=p_refs[w].at[:, 1 - c], dst_ref=land_refs[w],
        send_sem=send_sems.at[w], recv_sem=recv_sems.at[w], device_id=(x, y, 1 - c), device_id_type=MESH)
        for w in range(len(p_refs))]


def _rs_sibling_start(parts, tag, after=()):
    nw = len(parts)
    sems = 2 * nw + len(after)

    def body(*refs):
        for cp in _rs_sibling_copies(refs[:nw], refs[nw:2 * nw], refs[sems], refs[sems + 1]):
            cp.start()
        refs[-1][...] = jnp.zeros_like(refs[-1])

    lands = [lax.empty(p.shape[:1] + p.shape[2:], BF16) for p in parts]
    out = pl.pallas_call(
        body, name="rs_sibling_start_%s" % tag,
        in_specs=[HBM] * (2 * nw) + [ANY] * len(after),
        out_specs=(SEM, SEM, *[HBM] * (2 * nw), pl.BlockSpec(memory_space=pltpu.VMEM)),
        out_shape=(pltpu.SemaphoreType.DMA((nw,)), pltpu.SemaphoreType.DMA((nw,)),
                   *[pltpu.HBM(a.shape, a.dtype) for a in list(parts) + lands],
                   jax.ShapeDtypeStruct((8, LANES), F32)),
        input_output_aliases={i: 2 + i for i in range(2 * nw)},
        compiler_params=pltpu.CompilerParams(has_side_effects=EFFECT),
    )(*[_in_hbm(a) for a in list(parts) + lands], *after)
    return out[0], out[1], out[2:2 + nw], out[2 + nw:2 + 2 * nw], out[-1]


def _rs_sibling_wait(send_sems, recv_sems, parts, lands, after, tag):
    nw = len(parts)

    def body(*refs):
        for cp in _rs_sibling_copies(refs[:nw], refs[nw:2 * nw], refs[2 * nw], refs[2 * nw + 1]):
            cp.wait_send()
            cp.wait_recv()

    out = pl.pallas_call(
        body, name="rs_sibling_wait_%s" % tag,
        in_specs=[HBM] * (2 * nw) + [SEM, SEM] + [ANY] * len(after),
        out_specs=[HBM] * (2 * nw),
        out_shape=[pltpu.HBM(a.shape, a.dtype) for a in list(parts) + list(lands)],
        input_output_aliases={i: i for i in range(2 * nw)},
        compiler_params=pltpu.CompilerParams(has_side_effects=EFFECT),
    )(*parts, *lands, send_sems, recv_sems, *after)
    return out[:nw], out[nw:]


def _rs_chip_sum(parts, gots, c):
    n = len(parts)

    def body(c_ref, *refs):
        for p_ref, g_ref, o_ref in zip(refs[:n], refs[n:2 * n], refs[2 * n:]):
            o_ref[...] = (p_ref[...].astype(F32) + g_ref[...].astype(F32)).astype(o_ref.dtype)

    mine = [pl.BlockSpec((None, None, p.shape[2], D_MODEL), lambda q, c_ref: (q, c_ref[0], 0, 0)) for p in parts]
    theirs = [pl.BlockSpec((None, g.shape[1], D_MODEL), lambda q, c_ref: (q, 0, 0)) for g in gots]
    return pl.pallas_call(
        body, name="rs_chip_sum",
        grid_spec=pltpu.PrefetchScalarGridSpec(
            num_scalar_prefetch=1, grid=(4,), in_specs=mine + theirs, out_specs=theirs),
        out_shape=[jax.ShapeDtypeStruct(g.shape, BF16) for g in gots],
        compiler_params=_cparams(("parallel",)),
    )(c, *parts, *gots)


def _other_chips(x, y):
    return [(1 - x, y), (x, 1 - y), (1 - x, 1 - y)]


def _rs_chip_copies(s_refs, land_refs, send_sems, recv_sems):
    x, y, c = _my_place()
    copies = []
    for k, chip in enumerate(_other_chips(x, y)):
        q = 2 * chip[0] + chip[1]
        copies += [pltpu.make_async_remote_copy(
            src_ref=s_refs[w].at[q], dst_ref=land_refs[w].at[k],
            send_sem=send_sems.at[k * len(s_refs) + w], recv_sem=recv_sems.at[k * len(s_refs) + w],
            device_id=(*chip, c), device_id_type=MESH)
            for w in range(len(s_refs))]
    return copies


def _rs_chip_start(sums, layer):
    nw = len(sums)

    def body(*refs):
        s_refs, land_refs = refs[:nw], refs[nw:2 * nw]
        send_sems, recv_sems = refs[2 * nw], refs[2 * nw + 1]
        token = refs[-1]
        for cp in _rs_chip_copies(s_refs, land_refs, send_sems, recv_sems):
            cp.start()
        token[...] = jnp.zeros_like(token)

    lands = [lax.empty((3,) + s.shape[1:], BF16) for s in sums]
    out = pl.pallas_call(
        body, name="rs_chip_start_%s" % layer,
        in_specs=[HBM] * (2 * nw),
        out_specs=(SEM, SEM, *[HBM] * (2 * nw), pl.BlockSpec(memory_space=pltpu.VMEM)),
        out_shape=(pltpu.SemaphoreType.DMA((3 * nw,)), pltpu.SemaphoreType.DMA((3 * nw,)),
                   *[pltpu.HBM(a.shape, a.dtype) for a in list(sums) + lands],
                   jax.ShapeDtypeStruct((8, LANES), F32)),
        input_output_aliases={i: 2 + i for i in range(2 * nw)},
        compiler_params=pltpu.CompilerParams(has_side_effects=EFFECT),
    )(*[_in_hbm(a) for a in list(sums) + lands])
    return out[0], out[1], out[2:2 + nw], out[2 + nw:2 + 2 * nw], out[-1]


def _rs_chip_wait(send_sems, recv_sems, sums, lands, after, layer):
    nw = len(sums)

    def body(*refs):
        s_refs, land_refs = refs[:nw], refs[nw:2 * nw]
        for cp in _rs_chip_copies(s_refs, land_refs, refs[2 * nw], refs[2 * nw + 1]):
            cp.wait_send()
            cp.wait_recv()

    out = pl.pallas_call(
        body, name="rs_chip_wait_%s" % layer,
        in_specs=[HBM] * (2 * nw) + [SEM, SEM] + [ANY] * len(after),
        out_specs=[HBM] * (2 * nw),
        out_shape=[pltpu.HBM(a.shape, a.dtype) for a in list(sums) + list(lands)],
        input_output_aliases={i: i for i in range(2 * nw)},
        compiler_params=pltpu.CompilerParams(has_side_effects=EFFECT),
    )(*sums, *lands, send_sems, recv_sems, *after)
    return out[:nw], out[nw:]


def _rs_finish(sums, gots, q, layer, into):
    n = len(sums)

    def body(q_ref, *refs):
        for s_ref, g_ref, o_ref in zip(refs[:n], refs[n:2 * n], refs[len(refs) - n:]):
            o_ref[...] = ((s_ref[...].astype(F32) + g_ref[0].astype(F32)) + g_ref[1].astype(F32)) + g_ref[2].astype(F32)

    rows = [s.shape[1] for s in sums]
    in_specs = [pl.BlockSpec((None, r, D_MODEL), lambda i, q_ref: (q_ref[0], 0, 0)) for r in rows]
    in_specs += [pl.BlockSpec((3, r, D_MODEL), lambda i, q_ref: (0, 0, 0)) for r in rows]
    args = [q, *sums, *gots]
    aliases = {}
    if into is not None:
        in_specs += [ANY] * n
        aliases = {len(args) + i: i for i in range(n)}
        args += list(into)
    return pl.pallas_call(
        body, name="rs_finish",
        grid_spec=pltpu.PrefetchScalarGridSpec(
            num_scalar_prefetch=1, grid=(1,), in_specs=in_specs,
            out_specs=[pl.BlockSpec((None, r, D_MODEL), lambda i, q_ref: (layer, 0, 0)) for r in rows]),
        out_shape=[jax.ShapeDtypeStruct((DEPTH, r, D_MODEL), F32) for r in rows],
        input_output_aliases=aliases,
        compiler_params=_cparams(("arbitrary",)),
    )(*args)


def _allreduce_small(vec, deps=()):
    R = vec.shape[0]
    assert R % (8 * N_DEV) == 0
    P = R // N_DEV
    nd = len(deps)

    def body(*refs):
        v_ref = refs[0]
        o_ref, buf, send1, recv1, send2, recv2 = refs[1 + nd:]
        x, y, c = _my_place()
        me = 4 * x + 2 * y + c

        def piece(ref, d):
            return ref.at[pl.ds(pl.multiple_of(d * P, 8), P), :]

        def peer(k):
            p = me ^ k
            return p, (p >> 2, (p >> 1) & 1, p & 1)

        scatter = []
        for k in range(1, N_DEV):
            p, where = peer(k)
            scatter.append(pltpu.make_async_remote_copy(
                src_ref=piece(v_ref, p), dst_ref=buf.at[k], send_sem=send1.at[k - 1], recv_sem=recv1.at[k - 1],
                device_id=where, device_id_type=MESH))
        for cp in scatter:
            cp.start()
        buf[0] = piece(v_ref, me)[...]
        for cp in scatter:
            cp.wait()
        acc = buf[me]
        for d in range(1, N_DEV):
            acc = acc + buf[me ^ d]
        piece(o_ref, me)[...] = acc
        spread, arrivals = [], []
        for k in range(1, N_DEV):
            p, where = peer(k)
            spread.append(pltpu.make_async_remote_copy(
                src_ref=piece(o_ref, me), dst_ref=piece(o_ref, me), send_sem=send2.at[k - 1], recv_sem=recv2.at[k - 1],
                device_id=where, device_id_type=MESH))
            arrivals.append(pltpu.make_async_remote_copy(
                src_ref=piece(o_ref, p), dst_ref=piece(o_ref, p), send_sem=send2.at[k - 1], recv_sem=recv2.at[k - 1],
                device_id=where, device_id_type=MESH))
        for cp in spread:
            cp.start()
        for cp in arrivals:
            cp.wait_recv()
        for cp in spread:
            cp.wait_send()

    sems = pltpu.SemaphoreType.DMA((N_DEV - 1,))
    return pl.pallas_call(
        body, name="allreduce_small",
        in_specs=[pl.BlockSpec(memory_space=pltpu.VMEM)] + [ANY] * nd, out_specs=pl.BlockSpec(memory_space=pltpu.VMEM),
        out_shape=jax.ShapeDtypeStruct((R, LANES), F32),
        scratch_shapes=[pltpu.VMEM((N_DEV, P, LANES), F32), sems, sems, sems, sems],
        compiler_params=_cparams(),
    )(vec, *deps)


def _pack(arrs):
    flat = jnp.concatenate([a.reshape(-1) for a in arrs])
    pad = (-flat.shape[0]) % (8 * N_DEV * LANES)
    return jnp.pad(flat, (0, pad)).reshape(-1, LANES)


def _unpack(packed, shapes):
    flat = packed.reshape(-1)
    out, off = [], 0
    for s in shapes:
        n = math.prod(s)
        out.append(flat[off:off + n].reshape(s))
        off += n
    return out


def kernel(x, w_in, w_conv, w_pool, pool_scale, sgu_ln_g, w_spatial, b_spatial, w_o, ln1_g, ln1_b, w_gate_up, w_down, ln2_g, ln2_b, loss_target, m_w_in, m_w_conv, m_w_pool, m_pool_scale, m_sgu_ln_g, m_w_spatial, m_b_spatial, m_w_o, m_ln1_g, m_ln1_b, m_w_gate_up, m_w_down, m_ln2_g, m_ln2_b, v_w_in, v_w_conv, v_w_pool, v_pool_scale, v_sgu_ln_g, v_w_spatial, v_b_spatial, v_w_o, v_ln1_g, v_ln1_b, v_w_gate_up, v_w_down, v_ln2_g, v_ln2_b):
    L = DEPTH
    T = x.shape[1]
    mx, my, mc = _my_place()
    dev = 4 * mx + 2 * my + mc
    xs = x[0]
    target = loss_target[0]

    conv_bits = lax.bitcast_convert_type(w_conv, BF16).reshape(L, 1, -1)
    conv_rows = jnp.pad(conv_bits, ((0, 0), (0, CONV_PAD_ROWS - 1), (0, D_MODEL - conv_bits.shape[2])))
    shards = (jnp.swapaxes(w_in, 1, 2).astype(BF16), jnp.swapaxes(w_gate_up, 1, 2).astype(BF16),
              w_o.astype(BF16), w_down.astype(BF16), conv_rows)
    first_gather = _ag_start_layer(shards, 0, [])

    grad_x2, big_grads, small_grads, w_conv_full = _local_step(
        xs, target, shards, first_gather, w_pool, pool_scale, sgu_ln_g, w_spatial, b_spatial,
        ln1_g, ln1_b, ln2_g, ln2_b)
    grad_x = grad_x2[None]
    big_w = (w_in, w_gate_up, w_o, w_down)
    big_m = (m_w_in, m_w_gate_up, m_w_o, m_w_down)
    big_v = (v_w_in, v_w_gate_up, v_w_o, v_w_down)
    small_w = [w_conv_full, w_pool, pool_scale, sgu_ln_g, w_spatial, b_spatial, ln1_g, ln1_b, ln2_g, ln2_b]
    small_m = [m_w_conv, m_w_pool, m_pool_scale, m_sgu_ln_g, m_w_spatial, m_b_spatial, m_ln1_g, m_ln1_b, m_ln2_g, m_ln2_b]
    small_v = [v_w_conv, v_w_pool, v_pool_scale, v_sgu_ln_g, v_w_spatial, v_b_spatial, v_ln1_g, v_ln1_b, v_ln2_g, v_ln2_b]
    loss, grads, deltas, new_m, new_v = _reduce_and_update(
        big_grads, small_grads, big_w, big_m, big_v, small_w, small_m, small_v)
    return (loss, grad_x, *grads, *deltas, *new_m, *new_v)


CONV_PAD_ROWS = 16


def _ag_start_layer(shards, l, after):
    s_in, s_gu, s_o, s_dn, s_conv = [s[l] for s in shards]
    first = _ag_start([s_in, s_o, s_conv], "%da" % l, after=after)
    return first, _ag_start([s_gu, s_dn], "%db" % l, after=[first[4]])


def _w_conv_of(gathered):
    n = 2 * 3 * (CONV_W // N_DEV)
    bits = gathered.reshape(N_DEV, CONV_PAD_ROWS, D_MODEL)[:, 0, :n].reshape(N_DEV, 3, CONV_W // N_DEV, 2)
    return jnp.swapaxes(lax.bitcast_convert_type(bits, F32), 0, 1).reshape(3, CONV_W)


def _ag_finish(gather, after, tag):
    send_sems, recv_sems, shards, lands, _ = gather
    shards, lands = _ag_wait(send_sems, recv_sems, shards, lands, after, tag)
    return _ag_pass_on(shards, lands)


def _rs_begin(parts, tag, after=()):
    return _rs_sibling_start([p.reshape(4, 2, p.shape[0] // N_DEV, D_MODEL) for p in parts], tag, after)


def _rs_continue(sibling_flight, after, c_arr, tag):
    send_sems, recv_sems, parts, lands, _ = sibling_flight
    parts, got = _rs_sibling_wait(send_sems, recv_sems, parts, lands, after, tag)
    return _rs_chip_start(_rs_chip_sum(parts, got, c_arr), tag)


def _local_step(xs, target, shards, gather, w_pool, pool_scale, sgu_ln_g, w_spatial, b_spatial,
                ln1_g, ln1_b, ln2_g, ln2_b):
    L = DEPTH
    T = xs.shape[0]
    mx, my, mc = _my_place()
    c_arr = jnp.reshape(mc, (1,)).astype(jnp.int32)
    q_arr = jnp.reshape(2 * mx + my, (1,)).astype(jnp.int32)
    eye2 = jnp.eye(2, dtype=F32)
    wp = w_pool.reshape(L, 2, 2, HALF, HALF)
    wpool_bd = jnp.einsum("ltgcd,gh->ltgchd", wp, eye2).reshape(L, 2, LANES, LANES)
    wsp_t = w_spatial.reshape(L, 3, 2 * CHUNK, CHUNK)
    bias_t = jnp.repeat(jnp.swapaxes(b_spatial.reshape(L, 3, 2, CHUNK), 2, 3), HALF, axis=3)
    mixer_w = (wpool_bd, pool_scale[:, None, :], sgu_ln_g[:, None, :], wsp_t, bias_t)
    w_conv = []
    g1, b1, g2, b2 = [a[:, None, :] for a in (ln1_g, ln1_b, ln2_g, ln2_b)]
    one, zero = jnp.ones((1, 1, D_MODEL), F32), jnp.zeros((1, 1, D_MODEL), F32)

    saved = []
    prev, pg, pb = xs, (one, 0), (zero, 0)
    prev_b = xs.astype(BF16)
    weights = []
    for l in range(L):
        g_in, g_o, g_conv = _ag_finish(gather[0], [] if l == 0 else [prev_b], "%da" % l)
        w_conv.append(_w_conv_of(g_conv))
        proj = _mm(prev_b, g_in, "nt", F32, 512, IN_W, "mm_proj", deps=[gather[1][4]] if l == 0 else [])
        mixcat = _mixer_fwd(proj, w_conv[l], *mixer_w, l)
        xhat1, rstd1, h_b = _mm_ln_fwd(mixcat, g_o, prev, pg, pb, (g1, l), (b1, l), "mm_wo_ln")
        g_gu, g_dn = _ag_finish(gather[1], [h_b], "%db" % l)
        weights.append((g_in, g_gu, g_o, g_dn))
        deps = []
        if l + 1 < L:
            gather = _ag_start_layer(shards, l + 1, [g_gu])
            deps = [gather[1][4]]
        g_act, u_act, act = _mm_swiglu_fwd(h_b, g_gu, deps=deps)
        xhat2, rstd2, y_b = _mm_ln_fwd(act, g_dn, xhat1, (g1, l), (b1, l), (g2, l), (b2, l), "mm_down_ln")
        saved.append((prev_b, proj, mixcat, xhat1, rstd1, h_b, g_act, u_act, act, xhat2, rstd2))
        prev, pg, pb, prev_b = xhat2, (g2, l), (b2, l), y_b


    small = [None] * L
    big = None
    sibling_flight = None
    above = None
    for l in reversed(range(L)):
        prev_b, proj, mixcat, xhat1, rstd1, h_b, g_act, u_act, act, xhat2, rstd2 = saved[l]
        g_in, g_gu, g_o, g_dn = weights[l]
        chip_flight = None
        if above is None:
            loss_tile, dr2, dr2_b, dg2, db2 = _loss_ln_bwd(xhat2, rstd2, (g2, l), (b2, l), target)
        else:
            dr2, dr2_b, dg2, db2 = _mm_ln_bwd([above[0]], above[1], above[2], xhat2, rstd2, (g2, l),
                                              "mm_dx_ln", deps=[sibling_flight[4]])
            chip_flight = _rs_continue(sibling_flight, [dr2_b], c_arr, str(l + 1))
        dg_b, du_b = _mm_swiglu_bwd(dr2_b, g_dn, g_act, u_act, deps=[chip_flight[4]] if chip_flight else [])
        p_dn = _mm(act, dr2_b, "tn", BF16, DW_TM, D_MODEL, "mm_dw_down")
        p_gu = _mm_tn_pair(dg_b, du_b, h_b, DW_TM, "mm_dw_gate_up")
        ffn_sibling = _rs_begin([p_gu, p_dn], "0b") if l == 0 else None
        dr1, dr1_b, dg1, db1, dmix = _mm_ln_bwd([dg_b, du_b], g_gu, dr2, xhat1, rstd1, (g1, l), "mm_dh_ln",
                                                deps=[ffn_sibling[4]] if l == 0 else [], w_back=g_o)
        ffn_flight = _rs_continue(ffn_sibling, [dr1_b], c_arr, "0b") if l == 0 else None
        p_o = _mm(mixcat, dr1_b, "tn", BF16, 512, D_MODEL, "mm_dw_o")
        dproj, dwc, dwp, dps, dlng, dwsp, dbias = _mixer_bwd(proj, dmix, w_conv[l], *mixer_w, l,
                                                             deps=[ffn_flight[4]] if l == 0 else [])
        p_in = _mm(dproj, prev_b, "tn", BF16, IN_W, D_MODEL // 2, "mm_dw_in")
        small[l] = (dwc, dwp, dps, dlng, dwsp, dbias, dg1, db1, dg2, db2)
        above = (dproj, g_in, dr1)
        if chip_flight is not None:
            big = list(_rs_chip_finish(chip_flight, [p_in], q_arr, str(l + 1), l + 1, big))
        if l > 0:
            sibling_flight = _rs_begin([p_in, p_gu, p_o, p_dn], str(l))
        else:
            big[1], big[3] = _rs_chip_finish(ffn_flight, [p_in, p_o], q_arr, "0b", 0, [big[1], big[3]])

    def stack(i):
        return jnp.stack([small[l][i] for l in range(L)])

    dwp_bd = stack(1).reshape(L, 2, 2, HALF, 2, HALF)
    dwp_all = jnp.einsum("ltgchd,gh->ltgcd", dwp_bd, eye2).reshape(L, 4, HALF, HALF)
    dbs_all = jnp.swapaxes(stack(5)[:, :, :, :2], 2, 3).reshape(L, 6, CHUNK)
    small_grads = [stack(0), dwp_all, stack(2).reshape(L, POOL_W), stack(3).reshape(L, SGU_W),
                   stack(4).reshape(L, 6, CHUNK, CHUNK), dbs_all] + [stack(i).reshape(L, D_MODEL) for i in (6, 7, 8, 9)]
    small_grads.append(loss_tile[0, :1])
    packed_small = _allreduce_small(_pack(small_grads), deps=[big[1]])
    sibling_flight = _rs_begin([p_in, p_o], "0a", after=[packed_small])
    grad_x = _mm_ln_bwd([above[0]], above[1], above[2], None, None, None, "mm_dx_out", deps=[sibling_flight[4]])
    last_flight = _rs_continue(sibling_flight, [grad_x], c_arr, "0a")
    return grad_x, (big, last_flight, q_arr), (packed_small, [a.shape for a in small_grads]), jnp.stack(w_conv)


def _rs_chip_finish(in_flight, after, q, tag, layer, into):
    send_sems, recv_sems, sums, lands, _ = in_flight
    sums, got = _rs_chip_wait(send_sems, recv_sems, sums, lands, after, tag)
    return _rs_finish(sums, got, q, layer, into)


def _reduce_and_update(big_grads, small_grads, big_w, big_m, big_v, small_w, small_m, small_v):
    L = DEPTH
    mx, my, mc = _my_place()
    dev = 4 * mx + 2 * my + mc
    conv_cols = CONV_W // N_DEV
    w_in, w_gate_up, w_o, w_down = big_w
    m_w_in, m_w_gate_up, m_w_o, m_w_down = big_m
    v_w_in, v_w_gate_up, v_w_o, v_w_down = big_v
    packed_g, small_shapes = small_grads
    big, last_flight, q_arr = big_grads

    def widen_conv(a):
        return lax.dynamic_update_slice(jnp.zeros((L, 3, CONV_W), F32), a, (0, 0, dev * conv_cols))

    small_m = [widen_conv(small_m[0])] + list(small_m[1:])
    small_v = [widen_conv(small_v[0])] + list(small_v[1:])
    pk_d, pk_m, pk_v = _adamw(_pack(small_w), packed_g, _pack(small_m), _pack(small_v), packed_g.shape[0] // 2)
    sg = _unpack(packed_g, small_shapes)
    sd = _unpack(pk_d, small_shapes)
    sm = _unpack(pk_m, small_shapes)
    sv = _unpack(pk_v, small_shapes)

    def conv_cols_of(a):
        return lax.dynamic_slice(a, (0, 0, dev * conv_cols), (L, 3, conv_cols))

    for lst in (sg, sd, sm, sv):
        lst[0] = conv_cols_of(lst[0])

    tr = lambda a: jnp.swapaxes(a, 1, 2)
    gt_gu, g_w_dn = big[1], big[3]
    d_gu, m_gu, v_gu = [tr(a) for a in _adamw(tr(w_gate_up), gt_gu, tr(m_w_gate_up), tr(v_w_gate_up), gt_gu.shape[1] // 2)]
    d_dn, m_dn, v_dn = _adamw(w_down, g_w_dn, m_w_down, v_w_down, w_down.shape[1])
    gt_in, g_w_o = _rs_chip_finish(last_flight, [d_gu, d_dn, pk_d], q_arr, "0a", 0, [big[0], big[2]])
    d_in, m_in, v_in = [tr(a) for a in _adamw(tr(w_in), gt_in, tr(m_w_in), tr(v_w_in), gt_in.shape[1])]
    d_o, m_o, v_o = _adamw(w_o, g_w_o, m_w_o, v_w_o, w_o.shape[1])
    g_w_in, g_w_gu = tr(gt_in), tr(gt_gu)

    def ordered(big_in, big_o, big_gu, big_dn, sm_list):
        return [big_in, sm_list[0], sm_list[1], sm_list[2], sm_list[3], sm_list[4], sm_list[5], big_o,
                sm_list[6], sm_list[7], big_gu, big_dn, sm_list[8], sm_list[9]]

    grads = ordered(g_w_in, g_w_o, g_w_gu, g_w_dn, sg)
    deltas = ordered(d_in, d_o, d_gu, d_dn, sd)
    new_m = ordered(m_in, m_o, m_gu, m_dn, sm)
    new_v = ordered(v_in, v_o, v_gu, v_dn, sv)
    return sg[10][0], grads, deltas, new_m, new_v
```

```python
import math

import jax
import jax.numpy as jnp
from jax import lax
from jax.experimental import pallas as pl
from jax.experimental.pallas import tpu as pltpu

F32 = jnp.float32
BF16 = jnp.bfloat16
MESH = pl.DeviceIdType.MESH

D_MODEL = 1024
DEPTH = 4
CONV_W = 384
POOL_W = 256
SGU_W = 384
IN_W = 3 * CONV_W + POOL_W + 2 * SGU_W
D_FF = 2816
CHUNK = 128
ALPHA = float((2 * DEPTH) ** 0.25)
LN_EPS = 1e-5
ADAM_LR, ADAM_B1, ADAM_B2, ADAM_EPS, ADAM_WD, ADAM_STEP = 0.001, 0.9, 0.999, 1e-08, 0.01, 10

N_DEV = 8
LANES = 128
HALF = 64
VMEM_LIMIT = 52 * 1024 * 1024

INV_SQRT2 = 0.7071067811865476
INV_SQRT_2PI = 0.3989422804014327


def _cparams(sem=None, **kw):
    if sem is not None:
        kw["dimension_semantics"] = sem
    return pltpu.CompilerParams(vmem_limit_bytes=VMEM_LIMIT, **kw)


_DN = {"nt": (((1,), (1,)), ((), ())), "tn": (((0,), (0,)), ((), ()))}


def _mm(a, b, mode, out_dtype, tm, tn, name, deps=()):
    if mode == "nt":
        (M, K), N = a.shape, b.shape[0]
        a_spec = pl.BlockSpec((tm, K), lambda i, j: (i, 0))
        b_spec = pl.BlockSpec((tn, K), lambda i, j: (j, 0))
    else:
        (K, M), N = a.shape, b.shape[1]
        a_spec = pl.BlockSpec((K, tm), lambda i, j: (0, i))
        b_spec = pl.BlockSpec((K, tn), lambda i, j: (0, j))
    assert M % tm == 0 and N % tn == 0, (M, N, K, tm, tn)
    nd = len(deps)

    def body(*refs):
        a_ref, b_ref, o_ref = refs[0], refs[1], refs[2 + nd]
        o_ref[...] = lax.dot_general(a_ref[...], b_ref[...], _DN[mode], preferred_element_type=F32).astype(o_ref.dtype)

    return pl.pallas_call(
        body,
        name=name,
        grid=(M // tm, N // tn),
        in_specs=[a_spec, b_spec] + [pl.BlockSpec(memory_space=pl.ANY)] * nd,
        out_specs=pl.BlockSpec((tm, tn), lambda i, j: (i, j)),
        out_shape=jax.ShapeDtypeStruct((M, N), out_dtype),
        compiler_params=_cparams(("parallel", "parallel")),
    )(*_hbm_all(a, b, *deps))


def _mm_tn_pair(a1, a2, b, tm, name):
    K, M = a1.shape
    N = b.shape[1]
    n1 = M // tm

    def body(a1_ref, a2_ref, b_ref, o_ref):
        i = pl.program_id(0)

        @pl.when(i < n1)
        def _():
            o_ref[...] = lax.dot_general(a1_ref[...], b_ref[...], _DN["tn"], preferred_element_type=F32).astype(o_ref.dtype)

        @pl.when(i >= n1)
        def _():
            o_ref[...] = lax.dot_general(a2_ref[...], b_ref[...], _DN["tn"], preferred_element_type=F32).astype(o_ref.dtype)

    return pl.pallas_call(
        body, name=name, grid=(2 * n1,),
        in_specs=[pl.BlockSpec((K, tm), lambda i: (0, jnp.minimum(i, n1 - 1))),
                  pl.BlockSpec((K, tm), lambda i: (0, jnp.maximum(i - n1, 0))),
                  pl.BlockSpec((K, N), lambda i: (0, 0))],
        out_specs=pl.BlockSpec((tm, N), lambda i: (i, 0)),
        out_shape=jax.ShapeDtypeStruct((2 * M, N), BF16),
        compiler_params=_cparams(("arbitrary",)),
    )(*_hbm_all(a1, a2, b))


LN_SUB = 256
LN_TM = 512


def _vec(v):
    arr, layer = v
    return arr, pl.BlockSpec((None, 1, D_MODEL), lambda *_: (layer, 0, 0))


def _mm_ln_fwd(a, b, prev, pg, pb, g, bias, name):
    T, K = a.shape
    tm = LN_TM

    def body(a_ref, b_ref, prev_ref, pg_ref, pb_ref, g_ref, bias_ref, xhat_ref, rstd_ref, y_ref):
        for s in range(tm // LN_SUB):
            rows = slice(s * LN_SUB, (s + 1) * LN_SUB)
            mm = jnp.dot(a_ref[rows, :], b_ref[...], preferred_element_type=F32)
            r = ALPHA * (prev_ref[rows, :] * pg_ref[...] + pb_ref[...]) + mm
            mu = jnp.mean(r, axis=-1, keepdims=True)
            xc = r - mu
            var = jnp.mean(xc * xc, axis=-1, keepdims=True)
            rstd = lax.rsqrt(var + LN_EPS)
            xhat = xc * rstd
            xhat_ref[rows, :] = xhat
            rstd_ref[rows, :] = rstd
            y_ref[rows, :] = (xhat * g_ref[...] + bias_ref[...]).astype(y_ref.dtype)

    row = pl.BlockSpec((tm, D_MODEL), lambda i: (i, 0))
    vecs = [_vec(v) for v in (pg, pb, g, bias)]
    return pl.pallas_call(
        body, name=name, grid=(T // tm,),
        in_specs=[pl.BlockSpec((tm, K), lambda i: (i, 0)),
                  pl.BlockSpec((K, D_MODEL), lambda i: (0, 0), pipeline_mode=pl.Buffered(1)),
                  row] + [s for _, s in vecs],
        out_specs=[row, pl.BlockSpec((tm, 1), lambda i: (i, 0)), row],
        out_shape=[jax.ShapeDtypeStruct((T, D_MODEL), F32), jax.ShapeDtypeStruct((T, 1), F32),
                   jax.ShapeDtypeStruct((T, D_MODEL), BF16)],
        compiler_params=_cparams(("parallel",)),
    )(*_hbm_all(a, b, prev, *[a_ for a_, _ in vecs]))


def _mm_ln_bwd(a_list, b, dres, xhat, rstd, g, name, deps=(), w_back=None):
    T = a_list[0].shape[0]
    tm = LN_TM
    na, nd = len(a_list), len(deps)
    ks = [a.shape[1] for a in a_list]
    last = xhat is None
    nout = 1 if last else (5 if w_back is not None else 4)

    def body(*refs):
        a_refs, b_ref, dres_ref = refs[:na], refs[na], refs[na + 1]
        if not last:
            xhat_ref, rstd_ref, g_ref = refs[na + 2:na + 5]
            dr_ref, drb_ref, dg_ref, db_ref = refs[len(refs) - nout:len(refs) - nout + 4]

            @pl.when(pl.program_id(0) == 0)
            def _():
                dg_ref[...] = jnp.zeros_like(dg_ref)
                db_ref[...] = jnp.zeros_like(db_ref)

        for s in range(tm // LN_SUB):
            rows = slice(s * LN_SUB, (s + 1) * LN_SUB)
            mm, off = None, 0
            for a_ref, k in zip(a_refs, ks):
                part = jnp.dot(a_ref[rows, :], b_ref[off:off + k, :], preferred_element_type=F32)
                mm = part if mm is None else mm + part
                off += k
            dy = ALPHA * dres_ref[rows, :] + mm
            if last:
                refs[-1][rows, :] = dy
                continue
            xhat_v = xhat_ref[rows, :]
            dg_ref[...] += jnp.sum(dy * xhat_v, axis=0, keepdims=True)
            db_ref[...] += jnp.sum(dy, axis=0, keepdims=True)
            dxh = dy * g_ref[...]
            m1 = jnp.mean(dxh, axis=-1, keepdims=True)
            m2 = jnp.mean(dxh * xhat_v, axis=-1, keepdims=True)
            dr = rstd_ref[rows, :] * (dxh - m1 - xhat_v * m2)
            dr_ref[rows, :] = dr
            dr_b = dr.astype(drb_ref.dtype)
            drb_ref[rows, :] = dr_b
            if w_back is not None:
                refs[-1][rows, :] = lax.dot_general(dr_b, refs[na + 5][...], _DN["nt"], preferred_element_type=F32)

    row = pl.BlockSpec((tm, D_MODEL), lambda i: (i, 0))
    vec = pl.BlockSpec((1, D_MODEL), lambda i: (0, 0))
    in_specs = [pl.BlockSpec((tm, k), lambda i: (i, 0)) for k in ks]
    in_specs += [pl.BlockSpec((sum(ks), D_MODEL), lambda i: (0, 0), pipeline_mode=pl.Buffered(1)), row]
    args = list(a_list) + [b, dres]
    if last:
        out_specs, out_shape = row, jax.ShapeDtypeStruct((T, D_MODEL), F32)
    else:
        g_arr, g_spec = _vec(g)
        in_specs += [row, pl.BlockSpec((tm, 1), lambda i: (i, 0)), g_spec]
        args += [xhat, rstd, g_arr]
        out_specs = [row, row, vec, vec]
        out_shape = [jax.ShapeDtypeStruct((T, D_MODEL), F32), jax.ShapeDtypeStruct((T, D_MODEL), BF16),
                     jax.ShapeDtypeStruct((1, D_MODEL), F32), jax.ShapeDtypeStruct((1, D_MODEL), F32)]
        if w_back is not None:
            in_specs.append(pl.BlockSpec(w_back.shape, lambda i: (0, 0), pipeline_mode=pl.Buffered(1)))
            args.append(w_back)
            out_specs.append(row)
            out_shape.append(jax.ShapeDtypeStruct((T, w_back.shape[0]), F32))
    return pl.pallas_call(
        body, name=name, grid=(T // tm,),
        in_specs=in_specs + [pl.BlockSpec(memory_space=pl.ANY)] * nd,
        out_specs=out_specs, out_shape=out_shape,
        compiler_params=_cparams(("parallel",) if last else ("arbitrary",)),
    )(*_hbm_all(*args, *deps))


DW_TM = 1408
FF_TN = 256
FF_TM = 2048
SAVED_GU = BF16


def _mm_swiglu_fwd(h, w_gu, deps=()):
    T = h.shape[0]
    tm = min(T, FF_TM)
    nj = D_FF // FF_TN
    nd = len(deps)

    def body(*refs):
        h_ref, wg_ref, wu_ref = refs[:3]
        g_ref, u_ref, act_ref = refs[3 + nd:]
        hv = h_ref[...]
        gv = lax.dot_general(hv, wg_ref[...], _DN["nt"], preferred_element_type=F32)
        uv = lax.dot_general(hv, wu_ref[...], _DN["nt"], preferred_element_type=F32)
        g_ref[...] = gv.astype(g_ref.dtype)
        u_ref[...] = uv.astype(u_ref.dtype)
        act_ref[...] = (gv * jax.nn.sigmoid(gv) * uv).astype(act_ref.dtype)

    tile = pl.BlockSpec((tm, FF_TN), lambda j, i: (i, j))
    return pl.pallas_call(
        body, name="mm_gate_up_swiglu", grid=(nj, T // tm),
        in_specs=[pl.BlockSpec((tm, D_MODEL), lambda j, i: (i, 0)),
                  pl.BlockSpec((FF_TN, D_MODEL), lambda j, i: (j, 0)),
                  pl.BlockSpec((FF_TN, D_MODEL), lambda j, i: (j + nj, 0))] + [pl.BlockSpec(memory_space=pl.ANY)] * nd,
        out_specs=[tile, tile, tile],
        out_shape=[jax.ShapeDtypeStruct((T, D_FF), SAVED_GU), jax.ShapeDtypeStruct((T, D_FF), SAVED_GU),
                   jax.ShapeDtypeStruct((T, D_FF), BF16)],
        compiler_params=_cparams(("parallel", "parallel")),
    )(*_hbm_all(h, w_gu, w_gu, *deps))


def _mm_swiglu_bwd(dr, w_dn, g, u, deps=()):
    T = dr.shape[0]
    tm = min(T, FF_TM)

    def body(*refs):
        dr_ref, w_ref, g_ref, u_ref = refs[:4]
        dg_ref, du_ref = refs[-2:]
        da = lax.dot_general(dr_ref[...], w_ref[...], _DN["nt"], preferred_element_type=F32)
        gv, uv = g_ref[...].astype(F32), u_ref[...].astype(F32)
        s = jax.nn.sigmoid(gv)
        du_ref[...] = (da * (gv * s)).astype(du_ref.dtype)
        dg_ref[...] = (da * uv * (s * (1.0 + gv * (1.0 - s)))).astype(dg_ref.dtype)

    tile = pl.BlockSpec((tm, FF_TN), lambda j, i: (i, j))
    return pl.pallas_call(
        body, name="mm_dact_swiglu", grid=(D_FF // FF_TN, T // tm),
        in_specs=[pl.BlockSpec((tm, D_MODEL), lambda j, i: (i, 0)), pl.BlockSpec((FF_TN, D_MODEL), lambda j, i: (j, 0)),
                  tile, tile] + [ANY] * len(deps),
        out_specs=[tile, tile],
        out_shape=[jax.ShapeDtypeStruct((T, D_FF), BF16)] * 2,
        compiler_params=_cparams(("parallel", "parallel")),
    )(*_hbm_all(dr, w_dn, g, u, *deps))


def _gelu(x):
    return 0.5 * x * (1.0 + lax.erf(x * INV_SQRT2))


def _gelu_grad(x):
    return 0.5 * (1.0 + lax.erf(x * INV_SQRT2)) + x * (jnp.exp(-0.5 * x * x) * INV_SQRT_2PI)


def _shift_down(z, k):
    row = lax.broadcasted_iota(jnp.int32, z.shape, 0)
    return jnp.where(row >= k, pltpu.roll(z, k, 0), 0.0)


def _shift_up(z, k):
    n = z.shape[0]
    row = lax.broadcasted_iota(jnp.int32, z.shape, 0)
    return jnp.where(row < n - k, pltpu.roll(z, n - k, 0), 0.0)


def _lo_mask(shape):
    return lax.broadcasted_iota(jnp.int32, shape, len(shape) - 1) < HALF


def _seg_mean(x, lo):
    a = jnp.sum(jnp.where(lo, x, 0.0), axis=-1, keepdims=True)
    b = jnp.sum(jnp.where(lo, 0.0, x), axis=-1, keepdims=True)
    return jnp.where(lo, a, b) * (1.0 / HALF)


def _pool_windows(first):
    lo = _lo_mask((1, LANES))
    return jnp.where(first, jnp.where(lo, 2.0, 4.0), jnp.where(lo, 8.0, 16.0)), lo


def _pool_mean_minus_token(p, first):
    wl, lo = _pool_windows(first)
    s2 = p + _shift_down(p, 1)
    s4 = s2 + _shift_down(s2, 2)
    s8 = s4 + _shift_down(s4, 4)
    s16 = s8 + _shift_down(s8, 8)
    win = jnp.where(first, jnp.where(lo, s2, s4), jnp.where(lo, s8, s16))
    t1 = (lax.broadcasted_iota(jnp.int32, p.shape, 0) + 1).astype(F32)
    count = jnp.minimum(t1, wl)
    return win / count - p, count


SGU_UNROLL = 4


def _tril_keep():
    r = lax.broadcasted_iota(jnp.int32, (2 * CHUNK, CHUNK), 0)
    s = lax.broadcasted_iota(jnp.int32, (2 * CHUNK, CHUNK), 1)
    return s <= (r & (CHUNK - 1))


def _sgu_chunk_fwd(u, v, g, wm, bias, lo):
    ug = _gelu(u)
    vg = _gelu(v)
    mu = _seg_mean(vg, lo)
    xc = vg - mu
    var = _seg_mean(xc * xc, lo)
    rstd = lax.rsqrt(var + LN_EPS)
    vn = xc * rstd
    vh = (vn * g).astype(BF16)
    mm2 = jnp.dot(wm, vh, preferred_element_type=F32)
    mixed = jnp.where(lo, mm2[:CHUNK], mm2[CHUNK:]) + bias
    return ug, vn, rstd, vh, mixed


def _mixer_fwd(proj, wconv, wpool_bd, pscale, lng, wsp, bias, layer):
    T = proj.shape[0]
    nchunk = T // CHUNK

    def body(a_ref, b_ref, c_ref, wc_ref, wp_ref, ps_ref, lng_ref, wsp_ref, bias_ref, o_ref):
        j = pl.program_id(0)

        @pl.when(j < 3)
        def _conv():
            z = c_ref[...] * a_ref[...]
            w = wc_ref[...]
            y = w[0:1] * _shift_down(z, 2) + w[1:2] * _shift_down(z, 1) + w[2:3] * z
            o_ref[...] = (b_ref[...] * y).astype(o_ref.dtype)

        @pl.when((j >= 3) & (j < 5))
        def _pool():
            d, _ = _pool_mean_minus_token(a_ref[...], j == 3)
            y = jnp.dot(d.astype(BF16), wp_ref[...].astype(BF16), preferred_element_type=F32)
            o_ref[...] = (y * ps_ref[...]).astype(o_ref.dtype)

        @pl.when(j >= 5)
        def _sgu():
            lo = _lo_mask((CHUNK, LANES))
            wm = jnp.where(_tril_keep(), wsp_ref[...], 0.0).astype(BF16)
            bias_t = bias_ref[...]
            g = lng_ref[...]

            def chunk(n, carry):
                rows = pl.ds(pl.multiple_of(n * CHUNK, CHUNK), CHUNK)
                ug, _, _, _, mixed = _sgu_chunk_fwd(a_ref[rows, :], b_ref[rows, :], g, wm, bias_t, lo)
                o_ref[rows, :] = (ug * mixed).astype(o_ref.dtype)
                return carry

            lax.fori_loop(0, nchunk, chunk, 0, unroll=SGU_UNROLL)

    def col(f):
        return lambda j: (0, f(j))

    clip = lambda v, lo, hi: jnp.minimum(jnp.maximum(v, lo), hi)
    return pl.pallas_call(
        body,
        name="mixer_fwd",
        grid=(8,),
        in_specs=[
            pl.BlockSpec((T, LANES), col(lambda j: jnp.where(j < 3, j, jnp.where(j < 5, j + 6, j + 6)))),
            pl.BlockSpec((T, LANES), col(lambda j: jnp.where(j < 3, j + 3, jnp.where(j < 5, 5, j + 9)))),
            pl.BlockSpec((T, LANES), col(lambda j: jnp.where(j < 3, j + 6, 8))),
            pl.BlockSpec((3, LANES), col(lambda j: clip(j, 0, 2))),
            pl.BlockSpec((None, None, LANES, LANES), lambda j: (layer, clip(j - 3, 0, 1), 0, 0)),
            pl.BlockSpec((None, 1, LANES), lambda j: (layer, 0, clip(j - 3, 0, 1))),
            pl.BlockSpec((None, 1, LANES), lambda j: (layer, 0, clip(j - 5, 0, 2))),
            pl.BlockSpec((None, None, 2 * CHUNK, CHUNK), lambda j: (layer, clip(j - 5, 0, 2), 0, 0)),
            pl.BlockSpec((None, None, CHUNK, LANES), lambda j: (layer, clip(j - 5, 0, 2), 0, 0)),
        ],
        out_specs=pl.BlockSpec((T, LANES), lambda j: (0, j)),
        out_shape=jax.ShapeDtypeStruct((T, D_MODEL), BF16),
        compiler_params=_cparams(("arbitrary",)),
    )(*_hbm_all(proj, proj, proj, wconv, wpool_bd, pscale, lng, wsp, bias))


def _mixer_bwd(proj, dmix, wconv, wpool_bd, pscale, lng, wsp, bias, layer, deps=()):
    T = proj.shape[0]
    nchunk = T // CHUNK

    def body(*refs):
        a_ref, b_ref, c_ref, dm_ref, wc_ref, wp_ref, ps_ref, lng_ref, wsp_ref, bias_ref = refs[:10]
        o_ref, dwc_ref, dwp_ref, dps_ref, dlng_ref, dwsp_ref, dbias_ref, keep1, keep2 = refs[10 + len(deps):]
        k = pl.program_id(0)

        @pl.when(k < 3)
        def _conv():
            xa, gb, gc, dya = a_ref[...], b_ref[...], c_ref[...], dm_ref[...]
            w = wc_ref[...]
            z = gc * xa
            z1 = _shift_down(z, 1)
            z2 = _shift_down(z, 2)
            y = w[0:1] * z2 + w[1:2] * z1 + w[2:3] * z
            dyv = dya * gb
            dz = w[2:3] * dyv + w[1:2] * _shift_up(dyv, 1) + w[0:1] * _shift_up(dyv, 2)
            dwc_ref[0:1, :] = jnp.sum(dyv * z2, axis=0, keepdims=True)
            dwc_ref[1:2, :] = jnp.sum(dyv * z1, axis=0, keepdims=True)
            dwc_ref[2:3, :] = jnp.sum(dyv * z, axis=0, keepdims=True)
            o_ref[...] = (dz * gc).astype(o_ref.dtype)
            keep1[k] = (dya * y).astype(keep1.dtype)
            keep1[k + 3] = (dz * xa).astype(keep1.dtype)

        @pl.when((k >= 3) & (k < 9))
        def _emit_gb_gc():
            o_ref[...] = keep1[k - 3]

        @pl.when((k >= 9) & (k < 11))
        def _pool():
            first = k == 9
            p, dyb = a_ref[...], dm_ref[...]
            d, count = _pool_mean_minus_token(p, first)
            w2 = wp_ref[...].astype(BF16)
            db = d.astype(BF16)
            y = jnp.dot(db, w2, preferred_element_type=F32)
            dps_ref[...] = jnp.sum(dyb * y, axis=0, keepdims=True)
            dyv = (dyb * ps_ref[...]).astype(BF16)
            dd = lax.dot_general(dyv, w2, _DN["nt"], preferred_element_type=F32)
            dwp_ref[...] = lax.dot_general(db, dyv, _DN["tn"], preferred_element_type=F32)
            dwin = dd / count
            a2 = dwin + _shift_up(dwin, 1)
            a4 = a2 + _shift_up(a2, 2)
            a8 = a4 + _shift_up(a4, 4)
            a16 = a8 + _shift_up(a8, 8)
            _, lo = _pool_windows(first)
            back = jnp.where(first, jnp.where(lo, a2, a4), jnp.where(lo, a8, a16))
            o_ref[...] = (back - dd).astype(o_ref.dtype)

        @pl.when((k >= 11) & (k < 14))
        def _sgu():
            lo = _lo_mask((CHUNK, LANES))
            keep = _tril_keep()
            wm = jnp.where(keep, wsp_ref[...], 0.0).astype(BF16)
            bias_t = bias_ref[...]
            g = lng_ref[...]
            dwsp_ref[...] = jnp.zeros_like(dwsp_ref)
            dbias_ref[...] = jnp.zeros_like(dbias_ref)
            dlng_ref[...] = jnp.zeros_like(dlng_ref)

            def chunk(n, carry):
                rows = pl.ds(pl.multiple_of(n * CHUNK, CHUNK), CHUNK)
                u, v, dyc = a_ref[rows, :], b_ref[rows, :], dm_ref[rows, :]
                ug, vn, rstd, vh, mixed = _sgu_chunk_fwd(u, v, g, wm, bias_t, lo)
                dmx = dyc * ug
                o_ref[rows, :] = (dyc * mixed * _gelu_grad(u)).astype(o_ref.dtype)
                dbias_ref[...] += dmx
                dst = jnp.concatenate([jnp.where(lo, dmx, 0.0), jnp.where(lo, 0.0, dmx)], axis=0).astype(BF16)
                dwsp_ref[...] += lax.dot_general(dst, vh, _DN["nt"], preferred_element_type=F32)
                dvh = lax.dot_general(wm, dst, _DN["tn"], preferred_element_type=F32)
                dlng_ref[...] += jnp.sum(dvh * vn, axis=0, keepdims=True)
                dvn = dvh * g
                m1 = _seg_mean(dvn, lo)
                m2 = _seg_mean(dvn * vn, lo)
                dvg = rstd * (dvn - m1 - vn * m2)
                keep2[k - 11, rows, :] = (dvg * _gelu_grad(v)).astype(keep2.dtype)
                return carry

            lax.fori_loop(0, nchunk, chunk, 0, unroll=SGU_UNROLL)
            dwsp_ref[...] = jnp.where(keep, dwsp_ref[...], 0.0)
            dbt = dbias_ref[...]
            lane = lax.broadcasted_iota(jnp.int32, (CHUNK, LANES), 1)
            sa = jnp.sum(jnp.where(lo, dbt, 0.0), axis=-1, keepdims=True)
            sb = jnp.sum(jnp.where(lo, 0.0, dbt), axis=-1, keepdims=True)
            dbias_ref[...] = jnp.where(lane == 0, sa, jnp.where(lane == 1, sb, 0.0))

        @pl.when(k >= 14)
        def _emit_v():
            o_ref[...] = keep2[k - 14]

    def col(f):
        return lambda k: (0, f(k))

    clip = lambda v, lo, hi: jnp.minimum(jnp.maximum(v, lo), hi)
    view_a = lambda k: jnp.where(k < 3, k, jnp.where(k < 9, 2, jnp.where(k < 14, k, 13)))
    view_b = lambda k: jnp.where(k < 3, k + 3, jnp.where(k < 11, 5, jnp.where(k < 14, k + 3, 16)))
    view_c = lambda k: jnp.where(k < 3, k + 6, 8)
    view_dm = lambda k: jnp.where(k < 3, k, jnp.where(k < 9, 2, jnp.where(k < 14, k - 6, 7)))
    return pl.pallas_call(
        body,
        name="mixer_bwd",
        grid=(17,),
        in_specs=[
            pl.BlockSpec((T, LANES), col(view_a)),
            pl.BlockSpec((T, LANES), col(view_b)),
            pl.BlockSpec((T, LANES), col(view_c)),
            pl.BlockSpec((T, LANES), col(view_dm)),
            pl.BlockSpec((3, LANES), col(lambda k: clip(k, 0, 2))),
            pl.BlockSpec((None, None, LANES, LANES), lambda k: (layer, clip(k - 9, 0, 1), 0, 0)),
            pl.BlockSpec((None, 1, LANES), lambda k: (layer, 0, clip(k - 9, 0, 1))),
            pl.BlockSpec((None, 1, LANES), lambda k: (layer, 0, clip(k - 11, 0, 2))),
            pl.BlockSpec((None, None, 2 * CHUNK, CHUNK), lambda k: (layer, clip(k - 11, 0, 2), 0, 0)),
            pl.BlockSpec((None, None, CHUNK, LANES), lambda k: (layer, clip(k - 11, 0, 2), 0, 0)),
        ] + [pl.BlockSpec(memory_space=pl.ANY)] * len(deps),
        out_specs=[
            pl.BlockSpec((T, LANES), lambda k: (0, k)),
            pl.BlockSpec((3, LANES), col(lambda k: clip(k, 0, 2))),
            pl.BlockSpec((None, LANES, LANES), lambda k: (clip(k - 9, 0, 1), 0, 0)),
            pl.BlockSpec((1, LANES), col(lambda k: clip(k - 9, 0, 1))),
            pl.BlockSpec((1, LANES), col(lambda k: clip(k - 11, 0, 2))),
            pl.BlockSpec((None, 2 * CHUNK, CHUNK), lambda k: (clip(k - 11, 0, 2), 0, 0)),
            pl.BlockSpec((None, CHUNK, LANES), lambda k: (clip(k - 11, 0, 2), 0, 0)),
        ],
        out_shape=[
            jax.ShapeDtypeStruct((T, IN_W), BF16),
            jax.ShapeDtypeStruct((3, CONV_W), F32),
            jax.ShapeDtypeStruct((2, LANES, LANES), F32),
            jax.ShapeDtypeStruct((1, POOL_W), F32),
            jax.ShapeDtypeStruct((1, SGU_W), F32),
            jax.ShapeDtypeStruct((3, 2 * CHUNK, CHUNK), F32),
            jax.ShapeDtypeStruct((3, CHUNK, LANES), F32),
        ],
        scratch_shapes=[pltpu.VMEM((6, T, LANES), BF16), pltpu.VMEM((3, T, LANES), BF16)],
        compiler_params=_cparams(("arbitrary",)),
    )(*_hbm_all(proj, proj, proj, dmix, wconv, wpool_bd, pscale, lng, wsp, bias, *deps))


def _loss_ln_bwd(xhat, rstd, g, b, target, tm=256):
    T = xhat.shape[0]

    def body(xhat_ref, rstd_ref, g_ref, b_ref, t_ref, loss_ref, dr_ref, drb_ref, dg_ref, db_ref):
        xhat_v = xhat_ref[...]
        err = xhat_v * g_ref[...] + b_ref[...] - t_ref[...]
        dy = err * (1.0 / D_MODEL)

        @pl.when(pl.program_id(0) == 0)
        def _():
            loss_ref[...] = jnp.zeros_like(loss_ref)
            dg_ref[...] = jnp.zeros_like(dg_ref)
            db_ref[...] = jnp.zeros_like(db_ref)

        part = jnp.sum(jnp.sum(err * err, axis=-1, keepdims=True), axis=0, keepdims=True)
        loss_ref[...] += jnp.broadcast_to(part * (0.5 / D_MODEL), loss_ref.shape)
        dg_ref[...] += jnp.sum(dy * xhat_v, axis=0, keepdims=True)
        db_ref[...] += jnp.sum(dy, axis=0, keepdims=True)
        dxh = dy * g_ref[...]
        m1 = jnp.mean(dxh, axis=-1, keepdims=True)
        m2 = jnp.mean(dxh * xhat_v, axis=-1, keepdims=True)
        dr = rstd_ref[...] * (dxh - m1 - xhat_v * m2)
        dr_ref[...] = dr
        drb_ref[...] = dr.astype(drb_ref.dtype)

    row = pl.BlockSpec((tm, D_MODEL), lambda i: (i, 0))
    vec = pl.BlockSpec((1, D_MODEL), lambda i: (0, 0))
    (g_arr, g_spec), (b_arr, b_spec) = _vec(g), _vec(b)
    return pl.pallas_call(
        body,
        name="loss_ln_bwd",
        grid=(T // tm,),
        in_specs=[row, pl.BlockSpec((tm, 1), lambda i: (i, 0)), g_spec, b_spec, row],
        out_specs=[pl.BlockSpec((8, LANES), lambda i: (0, 0)), row, row, vec, vec],
        out_shape=[jax.ShapeDtypeStruct((8, LANES), F32),
                   jax.ShapeDtypeStruct((T, D_MODEL), F32), jax.ShapeDtypeStruct((T, D_MODEL), BF16),
                   jax.ShapeDtypeStruct((1, D_MODEL), F32), jax.ShapeDtypeStruct((1, D_MODEL), F32)],
        compiler_params=_cparams(("arbitrary",)),
    )(*_hbm_all(xhat, rstd, g_arr, b_arr, target))


def _adamw(w, g, m, v, tr):
    R, C = w.shape[-2:]
    assert R % tr == 0
    c1 = 1.0 - ADAM_B1 ** ADAM_STEP
    c2 = 1.0 - ADAM_B2 ** ADAM_STEP

    def body(w_ref, g_ref, m_ref, v_ref, d_ref, mo_ref, vo_ref):
        gv = g_ref[...]
        mn = ADAM_B1 * m_ref[...] + (1.0 - ADAM_B1) * gv
        vn = ADAM_B2 * v_ref[...] + (1.0 - ADAM_B2) * (gv * gv)
        d_ref[...] = -ADAM_LR * ((mn / c1) / (jnp.sqrt(vn / c2) + ADAM_EPS) + ADAM_WD * w_ref[...])
        mo_ref[...] = mn
        vo_ref[...] = vn

    if w.ndim == 2:
        grid, blk = (R // tr,), pl.BlockSpec((tr, C), lambda i: (i, 0))
    else:
        grid, blk = (w.shape[0], R // tr), pl.BlockSpec((None, tr, C), lambda l, i: (l, i, 0))
    return pl.pallas_call(
        body, name="adamw", grid=grid, in_specs=[blk] * 4, out_specs=[blk] * 3,
        out_shape=[jax.ShapeDtypeStruct(w.shape, F32)] * 3, compiler_params=_cparams(("parallel",) * len(grid)),
    )(*_hbm_all(w, g, m, v))


def _my_place():
    return lax.axis_index("x"), lax.axis_index("y"), lax.axis_index("c")


ANY = pl.BlockSpec(memory_space=pl.ANY)
HBM = pl.BlockSpec(memory_space=pltpu.HBM)
SEM = pl.BlockSpec(memory_space=pltpu.SEMAPHORE)
EFFECT = pltpu.SideEffectType.DATAFLOW_SIDE_EFFECTING


def _in_hbm(a):
    return pltpu.with_memory_space_constraint(a, pltpu.HBM)


def _hbm_all(*arrays):
    return [_in_hbm(a) for a in arrays]


def _block_rows(ref, dev):
    r = ref.shape[0] // N_DEV
    start = pl.multiple_of((4 * dev[0] + 2 * dev[1] + dev[2]) * r, 16)
    return ref.at[pl.ds(start, r), :]


def _ag_first_copies(s_refs, land_refs, send_sems, recv_sems, receiving):
    x, y, c = _my_place()
    peers = [(x, y, 1 - c)] + [(*chip, c) for chip in _other_chips(x, y)]
    copies = []
    for k, peer in enumerate(peers):
        block = peer if receiving else (x, y, c)
        copies += [pltpu.make_async_remote_copy(
            src_ref=s_refs[w], dst_ref=_block_rows(land_refs[w], block),
            send_sem=send_sems.at[k * len(s_refs) + w], recv_sem=recv_sems.at[k * len(s_refs) + w],
            device_id=peer, device_id_type=MESH)
            for w in range(len(s_refs))]
    return copies


def _ag_start(shards, layer, after=()):
    nw = len(shards)

    def body(*refs):
        s_refs, land_refs = refs[:nw], refs[nw:2 * nw]
        token = refs[-1]
        sems = 2 * nw + len(after)
        for cp in _ag_first_copies(s_refs, land_refs, refs[sems], refs[sems + 1], False):
            cp.start()
        token[...] = jnp.zeros_like(token)

    lands = [lax.empty((N_DEV * s.shape[0], D_MODEL), BF16) for s in shards]
    out = pl.pallas_call(
        body, name="ag_start_%s" % layer,
        in_specs=[HBM] * (2 * nw) + [ANY] * len(after),
        out_specs=(SEM, SEM, *[HBM] * (2 * nw), pl.BlockSpec(memory_space=pltpu.VMEM)),
        out_shape=(pltpu.SemaphoreType.DMA((4 * nw,)), pltpu.SemaphoreType.DMA((4 * nw,)),
                   *[pltpu.HBM(a.shape, a.dtype) for a in list(shards) + lands],
                   jax.ShapeDtypeStruct((8, LANES), F32)),
        input_output_aliases={i: 2 + i for i in range(2 * nw)},
        compiler_params=pltpu.CompilerParams(has_side_effects=EFFECT),
    )(*[_in_hbm(a) for a in list(shards) + lands], *after)
    return out[0], out[1], out[2:2 + nw], out[2 + nw:2 + 2 * nw], out[-1]


def _ag_wait(send_sems, recv_sems, shards, lands, after, layer):
    nw = len(shards)

    def body(*refs):
        s_refs, land_refs = refs[:nw], refs[nw:2 * nw]
        for cp in _ag_first_copies(s_refs, land_refs, refs[2 * nw], refs[2 * nw + 1], True):
            cp.wait_send()
            cp.wait_recv()

    out = pl.pallas_call(
        body, name="ag_wait_%s" % layer,
        in_specs=[HBM] * (2 * nw) + [SEM, SEM] + [ANY] * len(after),
        out_specs=[HBM] * (2 * nw),
        out_shape=[pltpu.HBM(a.shape, a.dtype) for a in list(shards) + list(lands)],
        input_output_aliases={i: i for i in range(2 * nw)},
        compiler_params=pltpu.CompilerParams(has_side_effects=EFFECT),
    )(*shards, *lands, send_sems, recv_sems, *after)
    return out[:nw], out[nw:]


def _ag_pass_on(shards, lands):
    nw = len(shards)

    def body(*refs):
        s_refs, g_refs = refs[:nw], refs[2 * nw:3 * nw]
        send_sems, recv_sems, local_sems = refs[3 * nw:3 * nw + 3]
        stage = refs[3 * nw + 3:]
        x, y, c = _my_place()
        load = [pltpu.make_async_copy(s_refs[w], stage[w], local_sems.at[w]) for w in range(nw)]
        mine = [pltpu.make_async_copy(stage[w], _block_rows(g_refs[w], (x, y, c)), local_sems.at[w])
                for w in range(nw)]
        for cp in load:
            cp.start()
        sends, arrivals = [], []
        for j, chip in enumerate(_other_chips(x, y)):
            for w in range(nw):
                rows_out = _block_rows(g_refs[w], (*chip, c))
                rows_in = _block_rows(g_refs[w], (*chip, 1 - c))
                sends.append(pltpu.make_async_remote_copy(
                    src_ref=rows_out, dst_ref=rows_out, send_sem=send_sems.at[j, w], recv_sem=recv_sems.at[j, w],
                    device_id=(x, y, 1 - c), device_id_type=MESH))
                arrivals.append(pltpu.make_async_remote_copy(
                    src_ref=rows_in, dst_ref=rows_in, send_sem=send_sems.at[j, w], recv_sem=recv_sems.at[j, w],
                    device_id=(x, y, 1 - c), device_id_type=MESH))
        for cp in sends:
            cp.start()
        for w in range(nw):
            load[w].wait()
            mine[w].start()
        for cp in arrivals:
            cp.wait_recv()
        for cp in sends:
            cp.wait_send()
        for cp in mine:
            cp.wait()

    return pl.pallas_call(
        body, name="ag_pass_on",
        in_specs=[ANY] * (2 * nw), out_specs=[ANY] * nw,
        out_shape=[jax.ShapeDtypeStruct(a.shape, a.dtype) for a in lands],
        input_output_aliases={nw + i: i for i in range(nw)},
        scratch_shapes=[pltpu.SemaphoreType.DMA((3, nw)), pltpu.SemaphoreType.DMA((3, nw)),
                        pltpu.SemaphoreType.DMA((nw,))] + [pltpu.VMEM(s.shape, s.dtype) for s in shards],
        compiler_params=_cparams(),
    )(*shards, *lands)


def _rs_sibling_copies(p_refs, land_refs, send_sems, recv_sems):
    x, y, c = _my_place()
    return [pltpu.make_async_remote_copy(
        src_ref=p_refs[w].at[:, 1 - c], dst_ref=land_refs[w],
        send_sem=send_sems.at[w], recv_sem=recv_sems.at[w], device_id=(x, y, 1 - c), device_id_type=MESH)
        for w in range(len(p_refs))]


def _rs_sibling_start(parts, tag, after=()):
    nw = len(parts)
    sems = 2 * nw + len(after)

    def body(*refs):
        for cp in _rs_sibling_copies(refs[:nw], refs[nw:2 * nw], refs[sems], refs[sems + 1]):
            cp.start()
        refs[-1][...] = jnp.zeros_like(refs[-1])

    lands = [lax.empty(p.shape[:1] + p.shape[2:], BF16) for p in parts]
    out = pl.pallas_call(
        body, name="rs_sibling_start_%s" % tag,
        in_specs=[HBM] * (2 * nw) + [ANY] * len(after),
        out_specs=(SEM, SEM, *[HBM] * (2 * nw), pl.BlockSpec(memory_space=pltpu.VMEM)),
        out_shape=(pltpu.SemaphoreType.DMA((nw,)), pltpu.SemaphoreType.DMA((nw,)),
                   *[pltpu.HBM(a.shape, a.dtype) for a in list(parts) + lands],
                   jax.ShapeDtypeStruct((8, LANES), F32)),
        input_output_aliases={i: 2 + i for i in range(2 * nw)},
        compiler_params=pltpu.CompilerParams(has_side_effects=EFFECT),
    )(*[_in_hbm(a) for a in list(parts) + lands], *after)
    return out[0], out[1], out[2:2 + nw], out[2 + nw:2 + 2 * nw], out[-1]


def _rs_sibling_wait(send_sems, recv_sems, parts, lands, after, tag):
    nw = len(parts)

    def body(*refs):
        for cp in _rs_sibling_copies(refs[:nw], refs[nw:2 * nw], refs[2 * nw], refs[2 * nw + 1]):
            cp.wait_send()
            cp.wait_recv()

    out = pl.pallas_call(
        body, name="rs_sibling_wait_%s" % tag,
        in_specs=[HBM] * (2 * nw) + [SEM, SEM] + [ANY] * len(after),
        out_specs=[HBM] * (2 * nw),
        out_shape=[pltpu.HBM(a.shape, a.dtype) for a in list(parts) + list(lands)],
        input_output_aliases={i: i for i in range(2 * nw)},
        compiler_params=pltpu.CompilerParams(has_side_effects=EFFECT),
    )(*parts, *lands, send_sems, recv_sems, *after)
    return out[:nw], out[nw:]


def _rs_chip_sum(parts, gots, c):
    n = len(parts)

    def body(c_ref, *refs):
        for p_ref, g_ref, o_ref in zip(refs[:n], refs[n:2 * n], refs[2 * n:]):
            o_ref[...] = (p_ref[...].astype(F32) + g_ref[...].astype(F32)).astype(o_ref.dtype)

    mine = [pl.BlockSpec((None, None, p.shape[2], D_MODEL), lambda q, c_ref: (q, c_ref[0], 0, 0)) for p in parts]
    theirs = [pl.BlockSpec((None, g.shape[1], D_MODEL), lambda q, c_ref: (q, 0, 0)) for g in gots]
    return pl.pallas_call(
        body, name="rs_chip_sum",
        grid_spec=pltpu.PrefetchScalarGridSpec(
            num_scalar_prefetch=1, grid=(4,), in_specs=mine + theirs, out_specs=theirs),
        out_shape=[jax.ShapeDtypeStruct(g.shape, BF16) for g in gots],
        compiler_params=_cparams(("parallel",)),
    )(c, *_hbm_all(*parts, *gots))


def _other_chips(x, y):
    return [(1 - x, y), (x, 1 - y), (1 - x, 1 - y)]


def _rs_chip_copies(s_refs, land_refs, send_sems, recv_sems):
    x, y, c = _my_place()
    copies = []
    for k, chip in enumerate(_other_chips(x, y)):
        q = 2 * chip[0] + chip[1]
        copies += [pltpu.make_async_remote_copy(
            src_ref=s_refs[w].at[q], dst_ref=land_refs[w].at[k],
            send_sem=send_sems.at[k * len(s_refs) + w], recv_sem=recv_sems.at[k * len(s_refs) + w],
            device_id=(*chip, c), device_id_type=MESH)
            for w in range(len(s_refs))]
    return copies


def _rs_chip_start(sums, layer):
    nw = len(sums)

    def body(*refs):
        s_refs, land_refs = refs[:nw], refs[nw:2 * nw]
        send_sems, recv_sems = refs[2 * nw], refs[2 * nw + 1]
        token = refs[-1]
        for cp in _rs_chip_copies(s_refs, land_refs, send_sems, recv_sems):
            cp.start()
        token[...] = jnp.zeros_like(token)

    lands = [lax.empty((3,) + s.shape[1:], BF16) for s in sums]
    out = pl.pallas_call(
        body, name="rs_chip_start_%s" % layer,
        in_specs=[HBM] * (2 * nw),
        out_specs=(SEM, SEM, *[HBM] * (2 * nw), pl.BlockSpec(memory_space=pltpu.VMEM)),
        out_shape=(pltpu.SemaphoreType.DMA((3 * nw,)), pltpu.SemaphoreType.DMA((3 * nw,)),
                   *[pltpu.HBM(a.shape, a.dtype) for a in list(sums) + lands],
                   jax.ShapeDtypeStruct((8, LANES), F32)),
        input_output_aliases={i: 2 + i for i in range(2 * nw)},
        compiler_params=pltpu.CompilerParams(has_side_effects=EFFECT),
    )(*[_in_hbm(a) for a in list(sums) + lands])
    return out[0], out[1], out[2:2 + nw], out[2 + nw:2 + 2 * nw], out[-1]


def _rs_chip_wait(send_sems, recv_sems, sums, lands, after, layer):
    nw = len(sums)

    def body(*refs):
        s_refs, land_refs = refs[:nw], refs[nw:2 * nw]
        for cp in _rs_chip_copies(s_refs, land_refs, refs[2 * nw], refs[2 * nw + 1]):
            cp.wait_send()
            cp.wait_recv()

    out = pl.pallas_call(
        body, name="rs_chip_wait_%s" % layer,
        in_specs=[HBM] * (2 * nw) + [SEM, SEM] + [ANY] * len(after),
        out_specs=[HBM] * (2 * nw),
        out_shape=[pltpu.HBM(a.shape, a.dtype) for a in list(sums) + list(lands)],
        input_output_aliases={i: i for i in range(2 * nw)},
        compiler_params=pltpu.CompilerParams(has_side_effects=EFFECT),
    )(*sums, *lands, send_sems, recv_sems, *after)
    return out[:nw], out[nw:]


def _rs_finish(sums, gots, q, layer, into):
    n = len(sums)

    def body(q_ref, *refs):
        for s_ref, g_ref, o_ref in zip(refs[:n], refs[n:2 * n], refs[len(refs) - n:]):
            o_ref[...] = ((s_ref[...].astype(F32) + g_ref[0].astype(F32)) + g_ref[1].astype(F32)) + g_ref[2].astype(F32)

    rows = [s.shape[1] for s in sums]
    in_specs = [pl.BlockSpec((None, r, D_MODEL), lambda i, q_ref: (q_ref[0], 0, 0)) for r in rows]
    in_specs += [pl.BlockSpec((3, r, D_MODEL), lambda i, q_ref: (0, 0, 0)) for r in rows]
    args = [q, *sums, *gots]
    aliases = {}
    if into is not None:
        in_specs += [ANY] * n
        aliases = {len(args) + i: i for i in range(n)}
        args += list(into)
    return pl.pallas_call(
        body, name="rs_finish",
        grid_spec=pltpu.PrefetchScalarGridSpec(
            num_scalar_prefetch=1, grid=(1,), in_specs=in_specs,
            out_specs=[pl.BlockSpec((None, r, D_MODEL), lambda i, q_ref: (layer, 0, 0)) for r in rows]),
        out_shape=[jax.ShapeDtypeStruct((DEPTH, r, D_MODEL), F32) for r in rows],
        input_output_aliases=aliases,
        compiler_params=_cparams(("arbitrary",)),
    )(*args)


def _allreduce_small(vec, deps=()):
    R = vec.shape[0]
    assert R % (8 * N_DEV) == 0
    P = R // N_DEV
    nd = len(deps)

    def body(*refs):
        v_ref = refs[0]
        o_ref, buf, send1, recv1, send2, recv2 = refs[1 + nd:]
        x, y, c = _my_place()
        me = 4 * x + 2 * y + c

        def piece(ref, d):
            return ref.at[pl.ds(pl.multiple_of(d * P, 8), P), :]

        def peer(k):
            p = me ^ k
            return p, (p >> 2, (p >> 1) & 1, p & 1)

        scatter = []
        for k in range(1, N_DEV):
            p, where = peer(k)
            scatter.append(pltpu.make_async_remote_copy(
                src_ref=piece(v_ref, p), dst_ref=buf.at[k], send_sem=send1.at[k - 1], recv_sem=recv1.at[k - 1],
                device_id=where, device_id_type=MESH))
        for cp in scatter:
            cp.start()
        buf[0] = piece(v_ref, me)[...]
        for cp in scatter:
            cp.wait()
        acc = buf[me]
        for d in range(1, N_DEV):
            acc = acc + buf[me ^ d]
        piece(o_ref, me)[...] = acc
        spread, arrivals = [], []
        for k in range(1, N_DEV):
            p, where = peer(k)
            spread.append(pltpu.make_async_remote_copy(
                src_ref=piece(o_ref, me), dst_ref=piece(o_ref, me), send_sem=send2.at[k - 1], recv_sem=recv2.at[k - 1],
                device_id=where, device_id_type=MESH))
            arrivals.append(pltpu.make_async_remote_copy(
                src_ref=piece(o_ref, p), dst_ref=piece(o_ref, p), send_sem=send2.at[k - 1], recv_sem=recv2.at[k - 1],
                device_id=where, device_id_type=MESH))
        for cp in spread:
            cp.start()
        for cp in arrivals:
            cp.wait_recv()
        for cp in spread:
            cp.wait_send()

    sems = pltpu.SemaphoreType.DMA((N_DEV - 1,))
    return pl.pallas_call(
        body, name="allreduce_small",
        in_specs=[pl.BlockSpec(memory_space=pltpu.VMEM)] + [ANY] * nd, out_specs=pl.BlockSpec(memory_space=pltpu.VMEM),
        out_shape=jax.ShapeDtypeStruct((R, LANES), F32),
        scratch_shapes=[pltpu.VMEM((N_DEV, P, LANES), F32), sems, sems, sems, sems],
        compiler_params=_cparams(),
    )(vec, *deps)


def _pack(arrs):
    flat = jnp.concatenate([a.reshape(-1) for a in arrs])
    pad = (-flat.shape[0]) % (8 * N_DEV * LANES)
    return jnp.pad(flat, (0, pad)).reshape(-1, LANES)


def _unpack(packed, shapes):
    flat = packed.reshape(-1)
    out, off = [], 0
    for s in shapes:
        n = math.prod(s)
        out.append(flat[off:off + n].reshape(s))
        off += n
    return out


def kernel(x, w_in, w_conv, w_pool, pool_scale, sgu_ln_g, w_spatial, b_spatial, w_o, ln1_g, ln1_b, w_gate_up, w_down, ln2_g, ln2_b, loss_target, m_w_in, m_w_conv, m_w_pool, m_pool_scale, m_sgu_ln_g, m_w_spatial, m_b_spatial, m_w_o, m_ln1_g, m_ln1_b, m_w_gate_up, m_w_down, m_ln2_g, m_ln2_b, v_w_in, v_w_conv, v_w_pool, v_pool_scale, v_sgu_ln_g, v_w_spatial, v_b_spatial, v_w_o, v_ln1_g, v_ln1_b, v_w_gate_up, v_w_down, v_ln2_g, v_ln2_b):
    L = DEPTH
    T = x.shape[1]
    mx, my, mc = _my_place()
    dev = 4 * mx + 2 * my + mc
    xs = x[0]
    target = loss_target[0]

    conv_bits = lax.bitcast_convert_type(w_conv, BF16).reshape(L, 1, -1)
    conv_rows = jnp.pad(conv_bits, ((0, 0), (0, CONV_PAD_ROWS - 1), (0, D_MODEL - conv_bits.shape[2])))
    shards = (jnp.swapaxes(w_in, 1, 2).astype(BF16), jnp.swapaxes(w_gate_up, 1, 2).astype(BF16),
              w_o.astype(BF16), w_down.astype(BF16), conv_rows)
    first_gather = _ag_start_layer(shards, 0, [])

    grad_x2, big_grads, small_grads, w_conv_full = _local_step(
        xs, target, shards, first_gather, w_pool, pool_scale, sgu_ln_g, w_spatial, b_spatial,
        ln1_g, ln1_b, ln2_g, ln2_b)
    grad_x = grad_x2[None]
    big_w = (w_in, w_gate_up, w_o, w_down)
    big_m = (m_w_in, m_w_gate_up, m_w_o, m_w_down)
    big_v = (v_w_in, v_w_gate_up, v_w_o, v_w_down)
    small_w = [w_conv_full, w_pool, pool_scale, sgu_ln_g, w_spatial, b_spatial, ln1_g, ln1_b, ln2_g, ln2_b]
    small_m = [m_w_conv, m_w_pool, m_pool_scale, m_sgu_ln_g, m_w_spatial, m_b_spatial, m_ln1_g, m_ln1_b, m_ln2_g, m_ln2_b]
    small_v = [v_w_conv, v_w_pool, v_pool_scale, v_sgu_ln_g, v_w_spatial, v_b_spatial, v_ln1_g, v_ln1_b, v_ln2_g, v_ln2_b]
    loss, grads, deltas, new_m, new_v = _reduce_and_update(
        big_grads, small_grads, big_w, big_m, big_v, small_w, small_m, small_v)
    return (loss, grad_x, *grads, *deltas, *new_m, *new_v)


CONV_PAD_ROWS = 16


def _ag_start_layer(shards, l, after):
    s_in, s_gu, s_o, s_dn, s_conv = [s[l] for s in shards]
    first = _ag_start([s_in, s_o, s_conv], "%da" % l, after=after)
    return first, _ag_start([s_gu, s_dn], "%db" % l, after=[first[4]])


def _w_conv_of(gathered):
    n = 2 * 3 * (CONV_W // N_DEV)
    bits = gathered.reshape(N_DEV, CONV_PAD_ROWS, D_MODEL)[:, 0, :n].reshape(N_DEV, 3, CONV_W // N_DEV, 2)
    return jnp.swapaxes(lax.bitcast_convert_type(bits, F32), 0, 1).reshape(3, CONV_W)


def _ag_finish(gather, after, tag):
    send_sems, recv_sems, shards, lands, _ = gather
    shards, lands = _ag_wait(send_sems, recv_sems, shards, lands, after, tag)
    return _ag_pass_on(shards, lands)


def _rs_begin(parts, tag, after=()):
    return _rs_sibling_start([p.reshape(4, 2, p.shape[0] // N_DEV, D_MODEL) for p in parts], tag, after)


def _rs_continue(sibling_flight, after, c_arr, tag):
    send_sems, recv_sems, parts, lands, _ = sibling_flight
    parts, got = _rs_sibling_wait(send_sems, recv_sems, parts, lands, after, tag)
    return _rs_chip_start(_rs_chip_sum(parts, got, c_arr), tag)


def _local_step(xs, target, shards, gather, w_pool, pool_scale, sgu_ln_g, w_spatial, b_spatial,
                ln1_g, ln1_b, ln2_g, ln2_b):
    L = DEPTH
    T = xs.shape[0]
    mx, my, mc = _my_place()
    c_arr = jnp.reshape(mc, (1,)).astype(jnp.int32)
    q_arr = jnp.reshape(2 * mx + my, (1,)).astype(jnp.int32)
    eye2 = jnp.eye(2, dtype=F32)
    wp = w_pool.reshape(L, 2, 2, HALF, HALF)
    wpool_bd = jnp.einsum("ltgcd,gh->ltgchd", wp, eye2).reshape(L, 2, LANES, LANES)
    wsp_t = w_spatial.reshape(L, 3, 2 * CHUNK, CHUNK)
    bias_t = jnp.repeat(jnp.swapaxes(b_spatial.reshape(L, 3, 2, CHUNK), 2, 3), HALF, axis=3)
    mixer_w = (wpool_bd, pool_scale[:, None, :], sgu_ln_g[:, None, :], wsp_t, bias_t)
    w_conv = []
    g1, b1, g2, b2 = [a[:, None, :] for a in (ln1_g, ln1_b, ln2_g, ln2_b)]
    one, zero = jnp.ones((1, 1, D_MODEL), F32), jnp.zeros((1, 1, D_MODEL), F32)

    saved = []
    prev, pg, pb = xs, (one, 0), (zero, 0)
    prev_b = xs.astype(BF16)
    weights = []
    for l in range(L):
        g_in, g_o, g_conv = _ag_finish(gather[0], [] if l == 0 else [prev_b], "%da" % l)
        w_conv.append(_w_conv_of(g_conv))
        proj = _mm(prev_b, g_in, "nt", F32, 512, IN_W, "mm_proj", deps=[gather[1][4]] if l == 0 else [])
        mixcat = _mixer_fwd(proj, w_conv[l], *mixer_w, l)
        xhat1, rstd1, h_b = _mm_ln_fwd(mixcat, g_o, prev, pg, pb, (g1, l), (b1, l), "mm_wo_ln")
        g_gu, g_dn = _ag_finish(gather[1], [h_b], "%db" % l)
        weights.append((g_in, g_gu, g_o, g_dn))
        deps = []
        if l + 1 < L:
            gather = _ag_start_layer(shards, l + 1, [g_gu])
            deps = [gather[1][4]]
        g_act, u_act, act = _mm_swiglu_fwd(h_b, g_gu, deps=deps)
        xhat2, rstd2, y_b = _mm_ln_fwd(act, g_dn, xhat1, (g1, l), (b1, l), (g2, l), (b2, l), "mm_down_ln")
        saved.append((prev_b, proj, mixcat, xhat1, rstd1, h_b, g_act, u_act, act, xhat2, rstd2))
        prev, pg, pb, prev_b = xhat2, (g2, l), (b2, l), y_b


    small = [None] * L
    big = None
    sibling_flight = None
    above = None
    for l in reversed(range(L)):
        prev_b, proj, mixcat, xhat1, rstd1, h_b, g_act, u_act, act, xhat2, rstd2 = saved[l]
        g_in, g_gu, g_o, g_dn = weights[l]
        chip_flight = None
        if above is None:
            loss_tile, dr2, dr2_b, dg2, db2 = _loss_ln_bwd(xhat2, rstd2, (g2, l), (b2, l), target)
        else:
            dr2, dr2_b, dg2, db2 = _mm_ln_bwd([above[0]], above[1], above[2], xhat2, rstd2, (g2, l),
                                              "mm_dx_ln", deps=[sibling_flight[4]])
            chip_flight = _rs_continue(sibling_flight, [dr2_b], c_arr, str(l + 1))
        dg_b, du_b = _mm_swiglu_bwd(dr2_b, g_dn, g_act, u_act, deps=[chip_flight[4]] if chip_flight else [])
        p_dn = _mm(act, dr2_b, "tn", BF16, DW_TM, D_MODEL, "mm_dw_down")
        p_gu = _mm_tn_pair(dg_b, du_b, h_b, DW_TM, "mm_dw_gate_up")
        ffn_sibling = _rs_begin([p_gu, p_dn], "0b") if l == 0 else None
        dr1, dr1_b, dg1, db1, dmix = _mm_ln_bwd([dg_b, du_b], g_gu, dr2, xhat1, rstd1, (g1, l), "mm_dh_ln",
                                                deps=[ffn_sibling[4]] if l == 0 else [], w_back=g_o)
        ffn_flight = _rs_continue(ffn_sibling, [dr1_b], c_arr, "0b") if l == 0 else None
        p_o = _mm(mixcat, dr1_b, "tn", BF16, 512, D_MODEL, "mm_dw_o")
        dproj, dwc, dwp, dps, dlng, dwsp, dbias = _mixer_bwd(proj, dmix, w_conv[l], *mixer_w, l,
                                                             deps=[ffn_flight[4]] if l == 0 else [])
        p_in = _mm(dproj, prev_b, "tn", BF16, IN_W, D_MODEL // 2, "mm_dw_in")
        small[l] = (dwc, dwp, dps, dlng, dwsp, dbias, dg1, db1, dg2, db2)
        above = (dproj, g_in, dr1)
        if chip_flight is not None:
            big = list(_rs_chip_finish(chip_flight, [p_in], q_arr, str(l + 1), l + 1, big))
        if l > 0:
            sibling_flight = _rs_begin([p_in, p_gu, p_o, p_dn], str(l))
        else:
            big[1], big[3] = _rs_chip_finish(ffn_flight, [p_in, p_o], q_arr, "0b", 0, [big[1], big[3]])

    def stack(i):
        return jnp.stack([small[l][i] for l in range(L)])

    dwp_bd = stack(1).reshape(L, 2, 2, HALF, 2, HALF)
    dwp_all = jnp.einsum("ltgchd,gh->ltgcd", dwp_bd, eye2).reshape(L, 4, HALF, HALF)
    dbs_all = jnp.swapaxes(stack(5)[:, :, :, :2], 2, 3).reshape(L, 6, CHUNK)
    small_grads = [stack(0), dwp_all, stack(2).reshape(L, POOL_W), stack(3).reshape(L, SGU_W),
                   stack(4).reshape(L, 6, CHUNK, CHUNK), dbs_all] + [stack(i).reshape(L, D_MODEL) for i in (6, 7, 8, 9)]
    small_grads.append(loss_tile[0, :1])
    packed_small = _allreduce_small(_pack(small_grads), deps=[big[1]])
    sibling_flight = _rs_begin([p_in, p_o], "0a", after=[packed_small])
    grad_x = _mm_ln_bwd([above[0]], above[1], above[2], None, None, None, "mm_dx_out", deps=[sibling_flight[4]])
    last_flight = _rs_continue(sibling_flight, [grad_x], c_arr, "0a")
    return grad_x, (big, last_flight, q_arr), (packed_small, [a.shape for a in small_grads]), jnp.stack(w_conv)


def _rs_chip_finish(in_flight, after, q, tag, layer, into):
    send_sems, recv_sems, sums, lands, _ = in_flight
    sums, got = _rs_chip_wait(send_sems, recv_sems, sums, lands, after, tag)
    return _rs_finish(sums, got, q, layer, into)


def _reduce_and_update(big_grads, small_grads, big_w, big_m, big_v, small_w, small_m, small_v):
    L = DEPTH
    mx, my, mc = _my_place()
    dev = 4 * mx + 2 * my + mc
    conv_cols = CONV_W // N_DEV
    w_in, w_gate_up, w_o, w_down = big_w
    m_w_in, m_w_gate_up, m_w_o, m_w_down = big_m
    v_w_in, v_w_gate_up, v_w_o, v_w_down = big_v
    packed_g, small_shapes = small_grads
    big, last_flight, q_arr = big_grads

    def widen_conv(a):
        return lax.dynamic_update_slice(jnp.zeros((L, 3, CONV_W), F32), a, (0, 0, dev * conv_cols))

    small_m = [widen_conv(small_m[0])] + list(small_m[1:])
    small_v = [widen_conv(small_v[0])] + list(small_v[1:])
    pk_d, pk_m, pk_v = _adamw(_pack(small_w), packed_g, _pack(small_m), _pack(small_v), packed_g.shape[0] // 2)
    sg = _unpack(packed_g, small_shapes)
    sd = _unpack(pk_d, small_shapes)
    sm = _unpack(pk_m, small_shapes)
    sv = _unpack(pk_v, small_shapes)

    def conv_cols_of(a):
        return lax.dynamic_slice(a, (0, 0, dev * conv_cols), (L, 3, conv_cols))

    for lst in (sg, sd, sm, sv):
        lst[0] = conv_cols_of(lst[0])

    tr = lambda a: jnp.swapaxes(a, 1, 2)
    gt_gu, g_w_dn = big[1], big[3]
    d_gu, m_gu, v_gu = [tr(a) for a in _adamw(tr(w_gate_up), gt_gu, tr(m_w_gate_up), tr(v_w_gate_up), gt_gu.shape[1] // 2)]
    d_dn, m_dn, v_dn = _adamw(w_down, g_w_dn, m_w_down, v_w_down, w_down.shape[1])
    gt_in, g_w_o = _rs_chip_finish(last_flight, [d_gu, d_dn, pk_d], q_arr, "0a", 0, [big[0], big[2]])
    d_in, m_in, v_in = [tr(a) for a in _adamw(tr(w_in), gt_in, tr(m_w_in), tr(v_w_in), gt_in.shape[1])]
    d_o, m_o, v_o = _adamw(w_o, g_w_o, m_w_o, v_w_o, w_o.shape[1])
    g_w_in, g_w_gu = tr(gt_in), tr(gt_gu)

    def ordered(big_in, big_o, big_gu, big_dn, sm_list):
        return [big_in, sm_list[0], sm_list[1], sm_list[2], sm_list[3], sm_list[4], sm_list[5], big_o,
                sm_list[6], sm_list[7], big_gu, big_dn, sm_list[8], sm_list[9]]

    grads = ordered(g_w_in, g_w_o, g_w_gu, g_w_dn, sg)
    deltas = ordered(d_in, d_o, d_gu, d_dn, sd)
    new_m = ordered(m_in, m_o, m_gu, m_dn, sm)
    new_v = ordered(v_in, v_o, v_gu, v_dn, sv)
    return sg[10][0], grads, deltas, new_m, new_v
```

```python
import math

import jax
import jax.numpy as jnp
from jax import lax
from jax.experimental import pallas as pl
from jax.experimental.pallas import tpu as pltpu

F32 = jnp.float32
BF16 = jnp.bfloat16
MESH = pl.DeviceIdType.MESH

D_MODEL = 1024
DEPTH = 4
CONV_W = 384
POOL_W = 256
SGU_W = 384
IN_W = 3 * CONV_W + POOL_W + 2 * SGU_W
D_FF = 2816
CHUNK = 128
ALPHA = float((2 * DEPTH) ** 0.25)
LN_EPS = 1e-5
ADAM_LR, ADAM_B1, ADAM_B2, ADAM_EPS, ADAM_WD, ADAM_STEP = 0.001, 0.9, 0.999, 1e-08, 0.01, 10

N_DEV = 8
LANES = 128
HALF = 64
VMEM_LIMIT = 52 * 1024 * 1024

INV_SQRT2 = 0.7071067811865476
INV_SQRT_2PI = 0.3989422804014327


def _cparams(sem=None, **kw):
    if sem is not None:
        kw["dimension_semantics"] = sem
    return pltpu.CompilerParams(vmem_limit_bytes=VMEM_LIMIT, **kw)


_DN = {"nt": (((1,), (1,)), ((), ())), "tn": (((0,), (0,)), ((), ()))}


def _mm(a, b, mode, out_dtype, tm, tn, name, deps=()):
    if mode == "nt":
        (M, K), N = a.shape, b.shape[0]
        a_spec = pl.BlockSpec((tm, K), lambda i, j: (i, 0))
        b_spec = pl.BlockSpec((tn, K), lambda i, j: (j, 0))
    else:
        (K, M), N = a.shape, b.shape[1]
        a_spec = pl.BlockSpec((K, tm), lambda i, j: (0, i))
        b_spec = pl.BlockSpec((K, tn), lambda i, j: (0, j))
    assert M % tm == 0 and N % tn == 0, (M, N, K, tm, tn)
    nd = len(deps)

    def body(*refs):
        a_ref, b_ref, o_ref = refs[0], refs[1], refs[2 + nd]
        o_ref[...] = lax.dot_general(a_ref[...], b_ref[...], _DN[mode], preferred_element_type=F32).astype(o_ref.dtype)

    return pl.pallas_call(
        body,
        name=name,
        grid=(M // tm, N // tn),
        in_specs=[a_spec, b_spec] + [pl.BlockSpec(memory_space=pl.ANY)] * nd,
        out_specs=pl.BlockSpec((tm, tn), lambda i, j: (i, j)),
        out_shape=jax.ShapeDtypeStruct((M, N), out_dtype),
        compiler_params=_cparams(("parallel", "parallel")),
    )(a, b, *deps)


def _mm_tn_pair(a1, a2, b, tm, name):
    K, M = a1.shape
    N = b.shape[1]
    n1 = M // tm

    def body(a1_ref, a2_ref, b_ref, o_ref):
        i = pl.program_id(0)

        @pl.when(i < n1)
        def _():
            o_ref[...] = lax.dot_general(a1_ref[...], b_ref[...], _DN["tn"], preferred_element_type=F32).astype(o_ref.dtype)

        @pl.when(i >= n1)
        def _():
            o_ref[...] = lax.dot_general(a2_ref[...], b_ref[...], _DN["tn"], preferred_element_type=F32).astype(o_ref.dtype)

    return pl.pallas_call(
        body, name=name, grid=(2 * n1,),
        in_specs=[pl.BlockSpec((K, tm), lambda i: (0, jnp.minimum(i, n1 - 1))),
                  pl.BlockSpec((K, tm), lambda i: (0, jnp.maximum(i - n1, 0))),
                  pl.BlockSpec((K, N), lambda i: (0, 0))],
        out_specs=pl.BlockSpec((tm, N), lambda i: (i, 0)),
        out_shape=jax.ShapeDtypeStruct((2 * M, N), BF16),
        compiler_params=_cparams(("arbitrary",)),
    )(a1, a2, b)


LN_SUB = 256
LN_TM = 512


def _vec(v):
    arr, layer = v
    return arr, pl.BlockSpec((None, 1, D_MODEL), lambda *_: (layer, 0, 0))


def _mm_ln_fwd(a, b, prev, pg, pb, g, bias, name):
    T, K = a.shape
    tm = LN_TM

    def body(a_ref, b_ref, prev_ref, pg_ref, pb_ref, g_ref, bias_ref, xhat_ref, rstd_ref, y_ref):
        for s in range(tm // LN_SUB):
            rows = slice(s * LN_SUB, (s + 1) * LN_SUB)
            mm = jnp.dot(a_ref[rows, :], b_ref[...], preferred_element_type=F32)
            r = ALPHA * (prev_ref[rows, :] * pg_ref[...] + pb_ref[...]) + mm
            mu = jnp.mean(r, axis=-1, keepdims=True)
            xc = r - mu
            var = jnp.mean(xc * xc, axis=-1, keepdims=True)
            rstd = lax.rsqrt(var + LN_EPS)
            xhat = xc * rstd
            xhat_ref[rows, :] = xhat
            rstd_ref[rows, :] = rstd
            y_ref[rows, :] = (xhat * g_ref[...] + bias_ref[...]).astype(y_ref.dtype)

    row = pl.BlockSpec((tm, D_MODEL), lambda i: (i, 0))
    vecs = [_vec(v) for v in (pg, pb, g, bias)]
    return pl.pallas_call(
        body, name=name, grid=(T // tm,),
        in_specs=[pl.BlockSpec((tm, K), lambda i: (i, 0)),
                  pl.BlockSpec((K, D_MODEL), lambda i: (0, 0), pipeline_mode=pl.Buffered(1)),
                  row] + [s for _, s in vecs],
        out_specs=[row, pl.BlockSpec((tm, 1), lambda i: (i, 0)), row],
        out_shape=[jax.ShapeDtypeStruct((T, D_MODEL), F32), jax.ShapeDtypeStruct((T, 1), F32),
                   jax.ShapeDtypeStruct((T, D_MODEL), BF16)],
        compiler_params=_cparams(("parallel",)),
    )(a, b, prev, *[a_ for a_, _ in vecs])


def _mm_ln_bwd(a_list, b, dres, xhat, rstd, g, name, deps=(), w_back=None):
    T = a_list[0].shape[0]
    tm = LN_TM
    na, nd = len(a_list), len(deps)
    ks = [a.shape[1] for a in a_list]
    last = xhat is None
    nout = 1 if last else (5 if w_back is not None else 4)

    def body(*refs):
        a_refs, b_ref, dres_ref = refs[:na], refs[na], refs[na + 1]
        if not last:
            xhat_ref, rstd_ref, g_ref = refs[na + 2:na + 5]
            dr_ref, drb_ref, dg_ref, db_ref = refs[len(refs) - nout:len(refs) - nout + 4]

            @pl.when(pl.program_id(0) == 0)
            def _():
                dg_ref[...] = jnp.zeros_like(dg_ref)
                db_ref[...] = jnp.zeros_like(db_ref)

        for s in range(tm // LN_SUB):
            rows = slice(s * LN_SUB, (s + 1) * LN_SUB)
            mm, off = None, 0
            for a_ref, k in zip(a_refs, ks):
                part = jnp.dot(a_ref[rows, :], b_ref[off:off + k, :], preferred_element_type=F32)
                mm = part if mm is None else mm + part
                off += k
            dy = ALPHA * dres_ref[rows, :] + mm
            if last:
                refs[-1][rows, :] = dy
                continue
            xhat_v = xhat_ref[rows, :]
            dg_ref[...] += jnp.sum(dy * xhat_v, axis=0, keepdims=True)
            db_ref[...] += jnp.sum(dy, axis=0, keepdims=True)
            dxh = dy * g_ref[...]
            m1 = jnp.mean(dxh, axis=-1, keepdims=True)
            m2 = jnp.mean(dxh * xhat_v, axis=-1, keepdims=True)
            dr = rstd_ref[rows, :] * (dxh - m1 - xhat_v * m2)
            dr_ref[rows, :] = dr
            dr_b = dr.astype(drb_ref.dtype)
            drb_ref[rows, :] = dr_b
            if w_back is not None:
                refs[-1][rows, :] = lax.dot_general(dr_b, refs[na + 5][...], _DN["nt"], preferred_element_type=F32)

    row = pl.BlockSpec((tm, D_MODEL), lambda i: (i, 0))
    vec = pl.BlockSpec((1, D_MODEL), lambda i: (0, 0))
    in_specs = [pl.BlockSpec((tm, k), lambda i: (i, 0)) for k in ks]
    in_specs += [pl.BlockSpec((sum(ks), D_MODEL), lambda i: (0, 0), pipeline_mode=pl.Buffered(1)), row]
    args = list(a_list) + [b, dres]
    if last:
        out_specs, out_shape = row, jax.ShapeDtypeStruct((T, D_MODEL), F32)
    else:
        g_arr, g_spec = _vec(g)
        in_specs += [row, pl.BlockSpec((tm, 1), lambda i: (i, 0)), g_spec]
        args += [xhat, rstd, g_arr]
        out_specs = [row, row, vec, vec]
        out_shape = [jax.ShapeDtypeStruct((T, D_MODEL), F32), jax.ShapeDtypeStruct((T, D_MODEL), BF16),
                     jax.ShapeDtypeStruct((1, D_MODEL), F32), jax.ShapeDtypeStruct((1, D_MODEL), F32)]
        if w_back is not None:
            in_specs.append(pl.BlockSpec(w_back.shape, lambda i: (0, 0), pipeline_mode=pl.Buffered(1)))
            args.append(w_back)
            out_specs.append(row)
            out_shape.append(jax.ShapeDtypeStruct((T, w_back.shape[0]), F32))
    return pl.pallas_call(
        body, name=name, grid=(T // tm,),
        in_specs=in_specs + [pl.BlockSpec(memory_space=pl.ANY)] * nd,
        out_specs=out_specs, out_shape=out_shape,
        compiler_params=_cparams(("parallel",) if last else ("arbitrary",)),
    )(*args, *deps)


DW_TM = 1408
FF_TN = 256
FF_TM = 2048
SAVED_GU = BF16


def _mm_swiglu_fwd(h, w_gu, deps=()):
    T = h.shape[0]
    tm = min(T, FF_TM)
    nj = D_FF // FF_TN
    nd = len(deps)

    def body(*refs):
        h_ref, wg_ref, wu_ref = refs[:3]
        g_ref, u_ref, act_ref = refs[3 + nd:]
        hv = h_ref[...]
        gv = lax.dot_general(hv, wg_ref[...], _DN["nt"], preferred_element_type=F32)
        uv = lax.dot_general(hv, wu_ref[...], _DN["nt"], preferred_element_type=F32)
        g_ref[...] = gv.astype(g_ref.dtype)
        u_ref[...] = uv.astype(u_ref.dtype)
        act_ref[...] = (gv * jax.nn.sigmoid(gv) * uv).astype(act_ref.dtype)

    tile = pl.BlockSpec((tm, FF_TN), lambda j, i: (i, j))
    return pl.pallas_call(
        body, name="mm_gate_up_swiglu", grid=(nj, T // tm),
        in_specs=[pl.BlockSpec((tm, D_MODEL), lambda j, i: (i, 0)),
                  pl.BlockSpec((FF_TN, D_MODEL), lambda j, i: (j, 0)),
                  pl.BlockSpec((FF_TN, D_MODEL), lambda j, i: (j + nj, 0))] + [pl.BlockSpec(memory_space=pl.ANY)] * nd,
        out_specs=[tile, tile, tile],
        out_shape=[jax.ShapeDtypeStruct((T, D_FF), SAVED_GU), jax.ShapeDtypeStruct((T, D_FF), SAVED_GU),
                   jax.ShapeDtypeStruct((T, D_FF), BF16)],
        compiler_params=_cparams(("parallel", "parallel")),
    )(h, w_gu, w_gu, *deps)


def _mm_swiglu_bwd(dr, w_dn, g, u, deps=()):
    T = dr.shape[0]
    tm = min(T, FF_TM)

    def body(*refs):
        dr_ref, w_ref, g_ref, u_ref = refs[:4]
        dg_ref, du_ref = refs[-2:]
        da = lax.dot_general(dr_ref[...], w_ref[...], _DN["nt"], preferred_element_type=F32)
        gv, uv = g_ref[...].astype(F32), u_ref[...].astype(F32)
        s = jax.nn.sigmoid(gv)
        du_ref[...] = (da * (gv * s)).astype(du_ref.dtype)
        dg_ref[...] = (da * uv * (s * (1.0 + gv * (1.0 - s)))).astype(dg_ref.dtype)

    tile = pl.BlockSpec((tm, FF_TN), lambda j, i: (i, j))
    return pl.pallas_call(
        body, name="mm_dact_swiglu", grid=(D_FF // FF_TN, T // tm),
        in_specs=[pl.BlockSpec((tm, D_MODEL), lambda j, i: (i, 0)), pl.BlockSpec((FF_TN, D_MODEL), lambda j, i: (j, 0)),
                  tile, tile] + [ANY] * len(deps),
        out_specs=[tile, tile],
        out_shape=[jax.ShapeDtypeStruct((T, D_FF), BF16)] * 2,
        compiler_params=_cparams(("parallel", "parallel")),
    )(dr, w_dn, g, u, *deps)


def _gelu(x):
    return 0.5 * x * (1.0 + lax.erf(x * INV_SQRT2))


def _gelu_grad(x):
    return 0.5 * (1.0 + lax.erf(x * INV_SQRT2)) + x * (jnp.exp(-0.5 * x * x) * INV_SQRT_2PI)


def _shift_down(z, k):
    row = lax.broadcasted_iota(jnp.int32, z.shape, 0)
    return jnp.where(row >= k, pltpu.roll(z, k, 0), 0.0)


def _shift_up(z, k):
    n = z.shape[0]
    row = lax.broadcasted_iota(jnp.int32, z.shape, 0)
    return jnp.where(row < n - k, pltpu.roll(z, n - k, 0), 0.0)


def _lo_mask(shape):
    return lax.broadcasted_iota(jnp.int32, shape, len(shape) - 1) < HALF


def _seg_mean(x, lo):
    a = jnp.sum(jnp.where(lo, x, 0.0), axis=-1, keepdims=True)
    b = jnp.sum(jnp.where(lo, 0.0, x), axis=-1, keepdims=True)
    return jnp.where(lo, a, b) * (1.0 / HALF)


def _pool_windows(first):
    lo = _lo_mask((1, LANES))
    return jnp.where(first, jnp.where(lo, 2.0, 4.0), jnp.where(lo, 8.0, 16.0)), lo


def _pool_mean_minus_token(p, first):
    wl, lo = _pool_windows(first)
    s2 = p + _shift_down(p, 1)
    s4 = s2 + _shift_down(s2, 2)
    s8 = s4 + _shift_down(s4, 4)
    s16 = s8 + _shift_down(s8, 8)
    win = jnp.where(first, jnp.where(lo, s2, s4), jnp.where(lo, s8, s16))
    t1 = (lax.broadcasted_iota(jnp.int32, p.shape, 0) + 1).astype(F32)
    count = jnp.minimum(t1, wl)
    return win / count - p, count


SGU_UNROLL = 2


def _tril_keep():
    r = lax.broadcasted_iota(jnp.int32, (2 * CHUNK, CHUNK), 0)
    s = lax.broadcasted_iota(jnp.int32, (2 * CHUNK, CHUNK), 1)
    return s <= (r & (CHUNK - 1))


def _sgu_chunk_fwd(u, v, g, wm, bias, lo):
    ug = _gelu(u)
    vg = _gelu(v)
    mu = _seg_mean(vg, lo)
    xc = vg - mu
    var = _seg_mean(xc * xc, lo)
    rstd = lax.rsqrt(var + LN_EPS)
    vn = xc * rstd
    vh = (vn * g).astype(BF16)
    mm2 = jnp.dot(wm, vh, preferred_element_type=F32)
    mixed = jnp.where(lo, mm2[:CHUNK], mm2[CHUNK:]) + bias
    return ug, vn, rstd, vh, mixed


def _mixer_fwd(proj, wconv, wpool_bd, pscale, lng, wsp, bias, layer):
    T = proj.shape[0]
    nchunk = T // CHUNK

    def body(a_ref, b_ref, c_ref, wc_ref, wp_ref, ps_ref, lng_ref, wsp_ref, bias_ref, o_ref):
        j = pl.program_id(0)

        @pl.when(j < 3)
        def _conv():
            z = c_ref[...] * a_ref[...]
            w = wc_ref[...]
            y = w[0:1] * _shift_down(z, 2) + w[1:2] * _shift_down(z, 1) + w[2:3] * z
            o_ref[...] = (b_ref[...] * y).astype(o_ref.dtype)

        @pl.when((j >= 3) & (j < 5))
        def _pool():
            d, _ = _pool_mean_minus_token(a_ref[...], j == 3)
            y = jnp.dot(d.astype(BF16), wp_ref[...].astype(BF16), preferred_element_type=F32)
            o_ref[...] = (y * ps_ref[...]).astype(o_ref.dtype)

        @pl.when(j >= 5)
        def _sgu():
            lo = _lo_mask((CHUNK, LANES))
            wm = jnp.where(_tril_keep(), wsp_ref[...], 0.0).astype(BF16)
            bias_t = bias_ref[...]
            g = lng_ref[...]

            def chunk(n, carry):
                rows = pl.ds(pl.multiple_of(n * CHUNK, CHUNK), CHUNK)
                ug, _, _, _, mixed = _sgu_chunk_fwd(a_ref[rows, :], b_ref[rows, :], g, wm, bias_t, lo)
                o_ref[rows, :] = (ug * mixed).astype(o_ref.dtype)
                return carry

            lax.fori_loop(0, nchunk, chunk, 0, unroll=SGU_UNROLL)

    def col(f):
        return lambda j: (0, f(j))

    clip = lambda v, lo, hi: jnp.minimum(jnp.maximum(v, lo), hi)
    return pl.pallas_call(
        body,
        name="mixer_fwd",
        grid=(8,),
        in_specs=[
            pl.BlockSpec((T, LANES), col(lambda j: jnp.where(j < 3, j, jnp.where(j < 5, j + 6, j + 6)))),
            pl.BlockSpec((T, LANES), col(lambda j: jnp.where(j < 3, j + 3, jnp.where(j < 5, 5, j + 9)))),
            pl.BlockSpec((T, LANES), col(lambda j: jnp.where(j < 3, j + 6, 8))),
            pl.BlockSpec((None, 3, LANES), lambda j: (layer, 0, clip(j, 0, 2))),
            pl.BlockSpec((None, None, LANES, LANES), lambda j: (layer, clip(j - 3, 0, 1), 0, 0)),
            pl.BlockSpec((None, 1, LANES), lambda j: (layer, 0, clip(j - 3, 0, 1))),
            pl.BlockSpec((None, 1, LANES), lambda j: (layer, 0, clip(j - 5, 0, 2))),
            pl.BlockSpec((None, None, 2 * CHUNK, CHUNK), lambda j: (layer, clip(j - 5, 0, 2), 0, 0)),
            pl.BlockSpec((None, None, CHUNK, LANES), lambda j: (layer, clip(j - 5, 0, 2), 0, 0)),
        ],
        out_specs=pl.BlockSpec((T, LANES), lambda j: (0, j)),
        out_shape=jax.ShapeDtypeStruct((T, D_MODEL), BF16),
        compiler_params=_cparams(("arbitrary",)),
    )(proj, proj, proj, wconv, wpool_bd, pscale, lng, wsp, bias)


def _mixer_bwd(proj, dmix, wconv, wpool_bd, pscale, lng, wsp, bias, layer, deps=()):
    T = proj.shape[0]
    nchunk = T // CHUNK

    def body(*refs):
        a_ref, b_ref, c_ref, dm_ref, wc_ref, wp_ref, ps_ref, lng_ref, wsp_ref, bias_ref = refs[:10]
        o_ref, dwc_ref, dwp_ref, dps_ref, dlng_ref, dwsp_ref, dbias_ref, keep1, keep2 = refs[10 + len(deps):]
        k = pl.program_id(0)

        @pl.when(k < 3)
        def _conv():
            xa, gb, gc, dya = a_ref[...], b_ref[...], c_ref[...], dm_ref[...]
            w = wc_ref[...]
            z = gc * xa
            z1 = _shift_down(z, 1)
            z2 = _shift_down(z, 2)
            y = w[0:1] * z2 + w[1:2] * z1 + w[2:3] * z
            dyv = dya * gb
            dz = w[2:3] * dyv + w[1:2] * _shift_up(dyv, 1) + w[0:1] * _shift_up(dyv, 2)
            dwc_ref[0:1, :] = jnp.sum(dyv * z2, axis=0, keepdims=True)
            dwc_ref[1:2, :] = jnp.sum(dyv * z1, axis=0, keepdims=True)
            dwc_ref[2:3, :] = jnp.sum(dyv * z, axis=0, keepdims=True)
            o_ref[...] = (dz * gc).astype(o_ref.dtype)
            keep1[k] = (dya * y).astype(keep1.dtype)
            keep1[k + 3] = (dz * xa).astype(keep1.dtype)

        @pl.when((k >= 3) & (k < 9))
        def _emit_gb_gc():
            o_ref[...] = keep1[k - 3]

        @pl.when((k >= 9) & (k < 11))
        def _pool():
            first = k == 9
            p, dyb = a_ref[...], dm_ref[...]
            d, count = _pool_mean_minus_token(p, first)
            w2 = wp_ref[...].astype(BF16)
            db = d.astype(BF16)
            y = jnp.dot(db, w2, preferred_element_type=F32)
            dps_ref[...] = jnp.sum(dyb * y, axis=0, keepdims=True)
            dyv = (dyb * ps_ref[...]).astype(BF16)
            dd = lax.dot_general(dyv, w2, _DN["nt"], preferred_element_type=F32)
            dwp_ref[...] = lax.dot_general(db, dyv, _DN["tn"], preferred_element_type=F32)
            dwin = dd / count
            a2 = dwin + _shift_up(dwin, 1)
            a4 = a2 + _shift_up(a2, 2)
            a8 = a4 + _shift_up(a4, 4)
            a16 = a8 + _shift_up(a8, 8)
            _, lo = _pool_windows(first)
            back = jnp.where(first, jnp.where(lo, a2, a4), jnp.where(lo, a8, a16))
            o_ref[...] = (back - dd).astype(o_ref.dtype)

        @pl.when((k >= 11) & (k < 14))
        def _sgu():
            lo = _lo_mask((CHUNK, LANES))
            keep = _tril_keep()
            wm = jnp.where(keep, wsp_ref[...], 0.0).astype(BF16)
            bias_t = bias_ref[...]
            g = lng_ref[...]
            dwsp_ref[...] = jnp.zeros_like(dwsp_ref)
            dbias_ref[...] = jnp.zeros_like(dbias_ref)
            dlng_ref[...] = jnp.zeros_like(dlng_ref)

            def chunk(n, carry):
                rows = pl.ds(pl.multiple_of(n * CHUNK, CHUNK), CHUNK)
                u, v, dyc = a_ref[rows, :], b_ref[rows, :], dm_ref[rows, :]
                ug, vn, rstd, vh, mixed = _sgu_chunk_fwd(u, v, g, wm, bias_t, lo)
                dmx = dyc * ug
                o_ref[rows, :] = (dyc * mixed * _gelu_grad(u)).astype(o_ref.dtype)
                dbias_ref[...] += dmx
                dst = jnp.concatenate([jnp.where(lo, dmx, 0.0), jnp.where(lo, 0.0, dmx)], axis=0).astype(BF16)
                dwsp_ref[...] += lax.dot_general(dst, vh, _DN["nt"], preferred_element_type=F32)
                dvh = lax.dot_general(wm, dst, _DN["tn"], preferred_element_type=F32)
                dlng_ref[...] += jnp.sum(dvh * vn, axis=0, keepdims=True)
                dvn = dvh * g
                m1 = _seg_mean(dvn, lo)
                m2 = _seg_mean(dvn * vn, lo)
                dvg = rstd * (dvn - m1 - vn * m2)
                keep2[k - 11, rows, :] = (dvg * _gelu_grad(v)).astype(keep2.dtype)
                return carry

            lax.fori_loop(0, nchunk, chunk, 0, unroll=SGU_UNROLL)
            dwsp_ref[...] = jnp.where(keep, dwsp_ref[...], 0.0)
            dbt = dbias_ref[...]
            lane = lax.broadcasted_iota(jnp.int32, (CHUNK, LANES), 1)
            sa = jnp.sum(jnp.where(lo, dbt, 0.0), axis=-1, keepdims=True)
            sb = jnp.sum(jnp.where(lo, 0.0, dbt), axis=-1, keepdims=True)
            dbias_ref[...] = jnp.where(lane == 0, sa, jnp.where(lane == 1, sb, 0.0))

        @pl.when(k >= 14)
        def _emit_v():
            o_ref[...] = keep2[k - 14]

    def col(f):
        return lambda k: (0, f(k))

    clip = lambda v, lo, hi: jnp.minimum(jnp.maximum(v, lo), hi)
    view_a = lambda k: jnp.where(k < 3, k, jnp.where(k < 9, 2, jnp.where(k < 14, k, 13)))
    view_b = lambda k: jnp.where(k < 3, k + 3, jnp.where(k < 11, 5, jnp.where(k < 14, k + 3, 16)))
    view_c = lambda k: jnp.where(k < 3, k + 6, 8)
    view_dm = lambda k: jnp.where(k < 3, k, jnp.where(k < 9, 2, jnp.where(k < 14, k - 6, 7)))
    return pl.pallas_call(
        body,
        name="mixer_bwd",
        grid=(17,),
        in_specs=[
            pl.BlockSpec((T, LANES), col(view_a)),
            pl.BlockSpec((T, LANES), col(view_b)),
            pl.BlockSpec((T, LANES), col(view_c)),
            pl.BlockSpec((T, LANES), col(view_dm)),
            pl.BlockSpec((None, 3, LANES), lambda k: (layer, 0, clip(k, 0, 2))),
            pl.BlockSpec((None, None, LANES, LANES), lambda k: (layer, clip(k - 9, 0, 1), 0, 0)),
            pl.BlockSpec((None, 1, LANES), lambda k: (layer, 0, clip(k - 9, 0, 1))),
            pl.BlockSpec((None, 1, LANES), lambda k: (layer, 0, clip(k - 11, 0, 2))),
            pl.BlockSpec((None, None, 2 * CHUNK, CHUNK), lambda k: (layer, clip(k - 11, 0, 2), 0, 0)),
            pl.BlockSpec((None, None, CHUNK, LANES), lambda k: (layer, clip(k - 11, 0, 2), 0, 0)),
        ] + [pl.BlockSpec(memory_space=pl.ANY)] * len(deps),
        out_specs=[
            pl.BlockSpec((T, LANES), lambda k: (0, k)),
            pl.BlockSpec((3, LANES), col(lambda k: clip(k, 0, 2))),
            pl.BlockSpec((None, LANES, LANES), lambda k: (clip(k - 9, 0, 1), 0, 0)),
            pl.BlockSpec((1, LANES), col(lambda k: clip(k - 9, 0, 1))),
            pl.BlockSpec((1, LANES), col(lambda k: clip(k - 11, 0, 2))),
            pl.BlockSpec((None, 2 * CHUNK, CHUNK), lambda k: (clip(k - 11, 0, 2), 0, 0)),
            pl.BlockSpec((None, CHUNK, LANES), lambda k: (clip(k - 11, 0, 2), 0, 0)),
        ],
        out_shape=[
            jax.ShapeDtypeStruct((T, IN_W), BF16),
            jax.ShapeDtypeStruct((3, CONV_W), F32),
            jax.ShapeDtypeStruct((2, LANES, LANES), F32),
            jax.ShapeDtypeStruct((1, POOL_W), F32),
            jax.ShapeDtypeStruct((1, SGU_W), F32),
            jax.ShapeDtypeStruct((3, 2 * CHUNK, CHUNK), F32),
            jax.ShapeDtypeStruct((3, CHUNK, LANES), F32),
        ],
        scratch_shapes=[pltpu.VMEM((6, T, LANES), BF16), pltpu.VMEM((3, T, LANES), BF16)],
        compiler_params=_cparams(("arbitrary",)),
    )(proj, proj, proj, dmix, wconv, wpool_bd, pscale, lng, wsp, bias, *deps)


def _loss_ln_bwd(xhat, rstd, g, b, target, tm=256):
    T = xhat.shape[0]

    def body(xhat_ref, rstd_ref, g_ref, b_ref, t_ref, loss_ref, dr_ref, drb_ref, dg_ref, db_ref):
        xhat_v = xhat_ref[...]
        err = xhat_v * g_ref[...] + b_ref[...] - t_ref[...]
        dy = err * (1.0 / D_MODEL)

        @pl.when(pl.program_id(0) == 0)
        def _():
            loss_ref[...] = jnp.zeros_like(loss_ref)
            dg_ref[...] = jnp.zeros_like(dg_ref)
            db_ref[...] = jnp.zeros_like(db_ref)

        part = jnp.sum(jnp.sum(err * err, axis=-1, keepdims=True), axis=0, keepdims=True)
        loss_ref[...] += jnp.broadcast_to(part * (0.5 / D_MODEL), loss_ref.shape)
        dg_ref[...] += jnp.sum(dy * xhat_v, axis=0, keepdims=True)
        db_ref[...] += jnp.sum(dy, axis=0, keepdims=True)
        dxh = dy * g_ref[...]
        m1 = jnp.mean(dxh, axis=-1, keepdims=True)
        m2 = jnp.mean(dxh * xhat_v, axis=-1, keepdims=True)
        dr = rstd_ref[...] * (dxh - m1 - xhat_v * m2)
        dr_ref[...] = dr
        drb_ref[...] = dr.astype(drb_ref.dtype)

    row = pl.BlockSpec((tm, D_MODEL), lambda i: (i, 0))
    vec = pl.BlockSpec((1, D_MODEL), lambda i: (0, 0))
    (g_arr, g_spec), (b_arr, b_spec) = _vec(g), _vec(b)
    return pl.pallas_call(
        body,
        name="loss_ln_bwd",
        grid=(T // tm,),
        in_specs=[row, pl.BlockSpec((tm, 1), lambda i: (i, 0)), g_spec, b_spec, row],
        out_specs=[pl.BlockSpec((8, LANES), lambda i: (0, 0)), row, row, vec, vec],
        out_shape=[jax.ShapeDtypeStruct((8, LANES), F32),
                   jax.ShapeDtypeStruct((T, D_MODEL), F32), jax.ShapeDtypeStruct((T, D_MODEL), BF16),
                   jax.ShapeDtypeStruct((1, D_MODEL), F32), jax.ShapeDtypeStruct((1, D_MODEL), F32)],
        compiler_params=_cparams(("arbitrary",)),
    )(xhat, rstd, g_arr, b_arr, target)


def _adamw(w, g, m, v, tr):
    R, C = w.shape[-2:]
    assert R % tr == 0
    c1 = 1.0 - ADAM_B1 ** ADAM_STEP
    c2 = 1.0 - ADAM_B2 ** ADAM_STEP

    def body(w_ref, g_ref, m_ref, v_ref, d_ref, mo_ref, vo_ref):
        gv = g_ref[...]
        mn = ADAM_B1 * m_ref[...] + (1.0 - ADAM_B1) * gv
        vn = ADAM_B2 * v_ref[...] + (1.0 - ADAM_B2) * (gv * gv)
        d_ref[...] = -ADAM_LR * ((mn / c1) / (jnp.sqrt(vn / c2) + ADAM_EPS) + ADAM_WD * w_ref[...])
        mo_ref[...] = mn
        vo_ref[...] = vn

    if w.ndim == 2:
        grid, blk = (R // tr,), pl.BlockSpec((tr, C), lambda i: (i, 0))
    else:
        grid, blk = (w.shape[0], R // tr), pl.BlockSpec((None, tr, C), lambda l, i: (l, i, 0))
    return pl.pallas_call(
        body, name="adamw", grid=grid, in_specs=[blk] * 4, out_specs=[blk] * 3,
        out_shape=[jax.ShapeDtypeStruct(w.shape, F32)] * 3, compiler_params=_cparams(("parallel",) * len(grid)),
    )(w, g, m, v)


def _my_place():
    return lax.axis_index("x"), lax.axis_index("y"), lax.axis_index("c")


ANY = pl.BlockSpec(memory_space=pl.ANY)
HBM = pl.BlockSpec(memory_space=pltpu.HBM)
SEM = pl.BlockSpec(memory_space=pltpu.SEMAPHORE)
EFFECT = pltpu.SideEffectType.DATAFLOW_SIDE_EFFECTING


def _in_hbm(a):
    return pltpu.with_memory_space_constraint(a, pltpu.HBM)


def _block_rows(ref, dev):
    r = ref.shape[0] // N_DEV
    start = pl.multiple_of((4 * dev[0] + 2 * dev[1] + dev[2]) * r, 16)
    return ref.at[pl.ds(start, r), :]


def _ag_first_copies(s_refs, land_refs, send_sems, recv_sems, receiving):
    x, y, c = _my_place()
    peers = [(x, y, 1 - c)] + [(*chip, c) for chip in _other_chips(x, y)]
    copies = []
    for k, peer in enumerate(peers):
        block = peer if receiving else (x, y, c)
        copies += [pltpu.make_async_remote_copy(
            src_ref=s_refs[w], dst_ref=_block_rows(land_refs[w], block),
            send_sem=send_sems.at[k * len(s_refs) + w], recv_sem=recv_sems.at[k * len(s_refs) + w],
            device_id=peer, device_id_type=MESH)
            for w in range(len(s_refs))]
    return copies


def _ag_start(shards, layer, after=()):
    nw = len(shards)

    def body(*refs):
        s_refs, land_refs = refs[:nw], refs[nw:2 * nw]
        token = refs[-1]
        sems = 2 * nw + len(after)
        for cp in _ag_first_copies(s_refs, land_refs, refs[sems], refs[sems + 1], False):
            cp.start()
        token[...] = jnp.zeros_like(token)

    lands = [lax.empty((N_DEV * s.shape[0], D_MODEL), BF16) for s in shards]
    out = pl.pallas_call(
        body, name="ag_start_%s" % layer,
        in_specs=[HBM] * (2 * nw) + [ANY] * len(after),
        out_specs=(SEM, SEM, *[HBM] * (2 * nw), pl.BlockSpec(memory_space=pltpu.VMEM)),
        out_shape=(pltpu.SemaphoreType.DMA((4 * nw,)), pltpu.SemaphoreType.DMA((4 * nw,)),
                   *[pltpu.HBM(a.shape, a.dtype) for a in list(shards) + lands],
                   jax.ShapeDtypeStruct((8, LANES), F32)),
        input_output_aliases={i: 2 + i for i in range(2 * nw)},
        compiler_params=pltpu.CompilerParams(has_side_effects=EFFECT),
    )(*[_in_hbm(a) for a in list(shards) + lands], *after)
    return out[0], out[1], out[2:2 + nw], out[2 + nw:2 + 2 * nw], out[-1]


def _ag_wait(send_sems, recv_sems, shards, lands, after, layer):
    nw = len(shards)

    def body(*refs):
        s_refs, land_refs = refs[:nw], refs[nw:2 * nw]
        for cp in _ag_first_copies(s_refs, land_refs, refs[2 * nw], refs[2 * nw + 1], True):
            cp.wait_send()
            cp.wait_recv()

    out = pl.pallas_call(
        body, name="ag_wait_%s" % layer,
        in_specs=[HBM] * (2 * nw) + [SEM, SEM] + [ANY] * len(after),
        out_specs=[HBM] * (2 * nw),
        out_shape=[pltpu.HBM(a.shape, a.dtype) for a in list(shards) + list(lands)],
        input_output_aliases={i: i for i in range(2 * nw)},
        compiler_params=pltpu.CompilerParams(has_side_effects=EFFECT),
    )(*shards, *lands, send_sems, recv_sems, *after)
    return out[:nw], out[nw:]


def _ag_pass_on(shards, lands):
    nw = len(shards)

    def body(*refs):
        s_refs, g_refs = refs[:nw], refs[2 * nw:3 * nw]
        send_sems, recv_sems, local_sems = refs[3 * nw:3 * nw + 3]
        stage = refs[3 * nw + 3:]
        x, y, c = _my_place()
        load = [pltpu.make_async_copy(s_refs[w], stage[w], local_sems.at[w]) for w in range(nw)]
        mine = [pltpu.make_async_copy(stage[w], _block_rows(g_refs[w], (x, y, c)), local_sems.at[w])
                for w in range(nw)]
        for cp in load:
            cp.start()
        sends, arrivals = [], []
        for j, chip in enumerate(_other_chips(x, y)):
            for w in range(nw):
                rows_out = _block_rows(g_refs[w], (*chip, c))
                rows_in = _block_rows(g_refs[w], (*chip, 1 - c))
                sends.append(pltpu.make_async_remote_copy(
                    src_ref=rows_out, dst_ref=rows_out, send_sem=send_sems.at[j, w], recv_sem=recv_sems.at[j, w],
                    device_id=(x, y, 1 - c), device_id_type=MESH))
                arrivals.append(pltpu.make_async_remote_copy(
                    src_ref=rows_in, dst_ref=rows_in, send_sem=send_sems.at[j, w], recv_sem=recv_sems.at[j, w],
                    device_id=(x, y, 1 - c), device_id_type=MESH))
        for cp in sends:
            cp.start()
        for w in range(nw):
            load[w].wait()
            mine[w].start()
        for cp in arrivals:
            cp.wait_recv()
        for cp in sends:
            cp.wait_send()
        for cp in mine:
            cp.wait()

    return pl.pallas_call(
        body, name="ag_pass_on",
        in_specs=[ANY] * (2 * nw), out_specs=[ANY] * nw,
        out_shape=[jax.ShapeDtypeStruct(a.shape, a.dtype) for a in lands],
        input_output_aliases={nw + i: i for i in range(nw)},
        scratch_shapes=[pltpu.SemaphoreType.DMA((3, nw)), pltpu.SemaphoreType.DMA((3, nw)),
                        pltpu.SemaphoreType.DMA((nw,))] + [pltpu.VMEM(s.shape, s.dtype) for s in shards],
        compiler_params=_cparams(),
    )(*shards, *lands)


def _rs_sibling_copies(p_refs, land_refs, send_sems, recv_sems):
    x, y, c = _my_place()
    return [pltpu.make_async_remote_copy(
        src_ref=p_refs[w].at[:, 1 - c], dst_ref=land_refs[w],
        send_sem=send_sems.at[w], recv_sem=recv_sems.at[w], device_id=(x, y, 1 - c), device_id_type=MESH)
        for w in range(len(p_refs))]


def _rs_sibling_start(parts, tag, after=()):
    nw = len(parts)
    sems = 2 * nw + len(after)

    def body(*refs):
        for cp in _rs_sibling_copies(refs[:nw], refs[nw:2 * nw], refs[sems], refs[sems + 1]):
            cp.start()
        refs[-1][...] = jnp.zeros_like(refs[-1])

    lands = [lax.empty(p.shape[:1] + p.shape[2:], BF16) for p in parts]
    out = pl.pallas_call(
        body, name="rs_sibling_start_%s" % tag,
        in_specs=[HBM] * (2 * nw) + [ANY] * len(after),
        out_specs=(SEM, SEM, *[HBM] * (2 * nw), pl.BlockSpec(memory_space=pltpu.VMEM)),
        out_shape=(pltpu.SemaphoreType.DMA((nw,)), pltpu.SemaphoreType.DMA((nw,)),
                   *[pltpu.HBM(a.shape, a.dtype) for a in list(parts) + lands],
                   jax.ShapeDtypeStruct((8, LANES), F32)),
        input_output_aliases={i: 2 + i for i in range(2 * nw)},
        compiler_params=pltpu.CompilerParams(has_side_effects=EFFECT),
    )(*[_in_hbm(a) for a in list(parts) + lands], *after)
    return out[0], out[1], out[2:2 + nw], out[2 + nw:2 + 2 * nw], out[-1]


def _rs_sibling_wait(send_sems, recv_sems, parts, lands, after, tag):
    nw = len(parts)

    def body(*refs):
        for cp in _rs_sibling_copies(refs[:nw], refs[nw:2 * nw], refs[2 * nw], refs[2 * nw + 1]):
            cp.wait_send()
            cp.wait_recv()

    out = pl.pallas_call(
        body, name="rs_sibling_wait_%s" % tag,
        in_specs=[HBM] * (2 * nw) + [SEM, SEM] + [ANY] * len(after),
        out_specs=[HBM] * (2 * nw),
        out_shape=[pltpu.HBM(a.shape, a.dtype) for a in list(parts) + list(lands)],
        input_output_aliases={i: i for i in range(2 * nw)},
        compiler_params=pltpu.CompilerParams(has_side_effects=EFFECT),
    )(*parts, *lands, send_sems, recv_sems, *after)
    return out[:nw], out[nw:]


def _rs_chip_sum(parts, gots, c):
    n = len(parts)

    def body(c_ref, *refs):
        for p_ref, g_ref, o_ref in zip(refs[:n], refs[n:2 * n], refs[2 * n:]):
            o_ref[...] = (p_ref[...].astype(F32) + g_ref[...].astype(F32)).astype(o_ref.dtype)

    mine = [pl.BlockSpec((None, None, p.shape[2], D_MODEL), lambda q, c_ref: (q, c_ref[0], 0, 0)) for p in parts]
    theirs = [pl.BlockSpec((None, g.shape[1], D_MODEL), lambda q, c_ref: (q, 0, 0)) for g in gots]
    return pl.pallas_call(
        body, name="rs_chip_sum",
        grid_spec=pltpu.PrefetchScalarGridSpec(
            num_scalar_prefetch=1, grid=(4,), in_specs=mine + theirs, out_specs=theirs),
        out_shape=[jax.ShapeDtypeStruct(g.shape, BF16) for g in gots],
        compiler_params=_cparams(("parallel",)),
    )(c, *parts, *gots)


def _other_chips(x, y):
    return [(1 - x, y), (x, 1 - y), (1 - x, 1 - y)]


def _rs_chip_copies(s_refs, land_refs, send_sems, recv_sems):
    x, y, c = _my_place()
    copies = []
    for k, chip in enumerate(_other_chips(x, y)):
        q = 2 * chip[0] + chip[1]
        copies += [pltpu.make_async_remote_copy(
            src_ref=s_refs[w].at[q], dst_ref=land_refs[w].at[k],
            send_sem=send_sems.at[k * len(s_refs) + w], recv_sem=recv_sems.at[k * len(s_refs) + w],
            device_id=(*chip, c), device_id_type=MESH)
            for w in range(len(s_refs))]
    return copies


def _rs_chip_start(sums, layer):
    nw = len(sums)

    def body(*refs):
        s_refs, land_refs = refs[:nw], refs[nw:2 * nw]
        send_sems, recv_sems = refs[2 * nw], refs[2 * nw + 1]
        token = refs[-1]
        for cp in _rs_chip_copies(s_refs, land_refs, send_sems, recv_sems):
            cp.start()
        token[...] = jnp.zeros_like(token)

    lands = [lax.empty((3,) + s.shape[1:], BF16) for s in sums]
    out = pl.pallas_call(
        body, name="rs_chip_start_%s" % layer,
        in_specs=[HBM] * (2 * nw),
        out_specs=(SEM, SEM, *[HBM] * (2 * nw), pl.BlockSpec(memory_space=pltpu.VMEM)),
        out_shape=(pltpu.SemaphoreType.DMA((3 * nw,)), pltpu.SemaphoreType.DMA((3 * nw,)),
                   *[pltpu.HBM(a.shape, a.dtype) for a in list(sums) + lands],
                   jax.ShapeDtypeStruct((8, LANES), F32)),
        input_output_aliases={i: 2 + i for i in range(2 * nw)},
        compiler_params=pltpu.CompilerParams(has_side_effects=EFFECT),
    )(*[_in_hbm(a) for a in list(sums) + lands])
    return out[0], out[1], out[2:2 + nw], out[2 + nw:2 + 2 * nw], out[-1]


def _rs_chip_wait(send_sems, recv_sems, sums, lands, after, layer):
    nw = len(sums)

    def body(*refs):
        s_refs, land_refs = refs[:nw], refs[nw:2 * nw]
        for cp in _rs_chip_copies(s_refs, land_refs, refs[2 * nw], refs[2 * nw + 1]):
            cp.wait_send()
            cp.wait_recv()

    out = pl.pallas_call(
        body, name="rs_chip_wait_%s" % layer,
        in_specs=[HBM] * (2 * nw) + [SEM, SEM] + [ANY] * len(after),
        out_specs=[HBM] * (2 * nw),
        out_shape=[pltpu.HBM(a.shape, a.dtype) for a in list(sums) + list(lands)],
        input_output_aliases={i: i for i in range(2 * nw)},
        compiler_params=pltpu.CompilerParams(has_side_effects=EFFECT),
    )(*sums, *lands, send_sems, recv_sems, *after)
    return out[:nw], out[nw:]


def _rs_finish(sums, gots, q, layer, into):
    n = len(sums)

    def body(q_ref, *refs):
        for s_ref, g_ref, o_ref in zip(refs[:n], refs[n:2 * n], refs[len(refs) - n:]):
            o_ref[...] = ((s_ref[...].astype(F32) + g_ref[0].astype(F32)) + g_ref[1].astype(F32)) + g_ref[2].astype(F32)

    rows = [s.shape[1] for s in sums]
    in_specs = [pl.BlockSpec((None, r, D_MODEL), lambda i, q_ref: (q_ref[0], 0, 0)) for r in rows]
    in_specs += [pl.BlockSpec((3, r, D_MODEL), lambda i, q_ref: (0, 0, 0)) for r in rows]
    args = [q, *sums, *gots]
    aliases = {}
    if into is not None:
        in_specs += [ANY] * n
        aliases = {len(args) + i: i for i in range(n)}
        args += list(into)
    return pl.pallas_call(
        body, name="rs_finish",
        grid_spec=pltpu.PrefetchScalarGridSpec(
            num_scalar_prefetch=1, grid=(1,), in_specs=in_specs,
            out_specs=[pl.BlockSpec((None, r, D_MODEL), lambda i, q_ref: (layer, 0, 0)) for r in rows]),
        out_shape=[jax.ShapeDtypeStruct((DEPTH, r, D_MODEL), F32) for r in rows],
        input_output_aliases=aliases,
        compiler_params=_cparams(("arbitrary",)),
    )(*args)


def _allreduce_small(vec, deps=()):
    R = vec.shape[0]
    assert R % (8 * N_DEV) == 0
    P = R // N_DEV
    nd = len(deps)

    def body(*refs):
        v_ref = refs[0]
        o_ref, buf, send1, recv1, send2, recv2 = refs[1 + nd:]
        x, y, c = _my_place()
        me = 4 * x + 2 * y + c

        def piece(ref, d):
            return ref.at[pl.ds(pl.multiple_of(d * P, 8), P), :]

        def peer(k):
            p = me ^ k
            return p, (p >> 2, (p >> 1) & 1, p & 1)

        scatter = []
        for k in range(1, N_DEV):
            p, where = peer(k)
            scatter.append(pltpu.make_async_remote_copy(
                src_ref=piece(v_ref, p), dst_ref=buf.at[k], send_sem=send1.at[k - 1], recv_sem=recv1.at[k - 1],
                device_id=where, device_id_type=MESH))
        for cp in scatter:
            cp.start()
        buf[0] = piece(v_ref, me)[...]
        for cp in scatter:
            cp.wait()
        acc = buf[me]
        for d in range(1, N_DEV):
            acc = acc + buf[me ^ d]
        piece(o_ref, me)[...] = acc
        spread, arrivals = [], []
        for k in range(1, N_DEV):
            p, where = peer(k)
            spread.append(pltpu.make_async_remote_copy(
                src_ref=piece(o_ref, me), dst_ref=piece(o_ref, me), send_sem=send2.at[k - 1], recv_sem=recv2.at[k - 1],
                device_id=where, device_id_type=MESH))
            arrivals.append(pltpu.make_async_remote_copy(
                src_ref=piece(o_ref, p), dst_ref=piece(o_ref, p), send_sem=send2.at[k - 1], recv_sem=recv2.at[k - 1],
                device_id=where, device_id_type=MESH))
        for cp in spread:
            cp.start()
        for cp in arrivals:
            cp.wait_recv()
        for cp in spread:
            cp.wait_send()

    sems = pltpu.SemaphoreType.DMA((N_DEV - 1,))
    return pl.pallas_call(
        body, name="allreduce_small",
        in_specs=[pl.BlockSpec(memory_space=pltpu.VMEM)] + [ANY] * nd, out_specs=pl.BlockSpec(memory_space=pltpu.VMEM),
        out_shape=jax.ShapeDtypeStruct((R, LANES), F32),
        scratch_shapes=[pltpu.VMEM((N_DEV, P, LANES), F32), sems, sems, sems, sems],
        compiler_params=_cparams(),
    )(vec, *deps)


def _pack(arrs):
    flat = jnp.concatenate([a.reshape(-1) for a in arrs])
    pad = (-flat.shape[0]) % (8 * N_DEV * LANES)
    return jnp.pad(flat, (0, pad)).reshape(-1, LANES)


def _unpack(packed, shapes):
    flat = packed.reshape(-1)
    out, off = [], 0
    for s in shapes:
        n = math.prod(s)
        out.append(flat[off:off + n].reshape(s))
        off += n
    return out


def kernel(x, w_in, w_conv, w_pool, pool_scale, sgu_ln_g, w_spatial, b_spatial, w_o, ln1_g, ln1_b, w_gate_up, w_down, ln2_g, ln2_b, loss_target, m_w_in, m_w_conv, m_w_pool, m_pool_scale, m_sgu_ln_g, m_w_spatial, m_b_spatial, m_w_o, m_ln1_g, m_ln1_b, m_w_gate_up, m_w_down, m_ln2_g, m_ln2_b, v_w_in, v_w_conv, v_w_pool, v_pool_scale, v_sgu_ln_g, v_w_spatial, v_b_spatial, v_w_o, v_ln1_g, v_ln1_b, v_w_gate_up, v_w_down, v_ln2_g, v_ln2_b):
    L = DEPTH
    T = x.shape[1]
    mx, my, mc = _my_place()
    dev = 4 * mx + 2 * my + mc
    xs = x[0]
    target = loss_target[0]

    conv_cols = w_conv.shape[2]
    w_conv_z = lax.dynamic_update_slice(jnp.zeros((L, 3, CONV_W), F32), w_conv, (0, 0, dev * conv_cols))
    w_conv_packed = _allreduce_small(_pack([w_conv_z]))
    w_conv_full = _unpack(w_conv_packed, [(L, 3, CONV_W)])[0]

    shards = (jnp.swapaxes(w_in, 1, 2).astype(BF16), jnp.swapaxes(w_gate_up, 1, 2).astype(BF16),
              w_o.astype(BF16), w_down.astype(BF16))
    first_gather = _ag_start_layer(shards, 0, [w_conv_packed])

    grad_x2, big_grads, small_grads = _local_step(
        xs, target, shards, first_gather, w_conv_full, w_pool, pool_scale, sgu_ln_g, w_spatial, b_spatial,
        ln1_g, ln1_b, ln2_g, ln2_b)
    grad_x = grad_x2[None]
    big_w = (w_in, w_gate_up, w_o, w_down)
    big_m = (m_w_in, m_w_gate_up, m_w_o, m_w_down)
    big_v = (v_w_in, v_w_gate_up, v_w_o, v_w_down)
    small_w = [w_conv_full, w_pool, pool_scale, sgu_ln_g, w_spatial, b_spatial, ln1_g, ln1_b, ln2_g, ln2_b]
    small_m = [m_w_conv, m_w_pool, m_pool_scale, m_sgu_ln_g, m_w_spatial, m_b_spatial, m_ln1_g, m_ln1_b, m_ln2_g, m_ln2_b]
    small_v = [v_w_conv, v_w_pool, v_pool_scale, v_sgu_ln_g, v_w_spatial, v_b_spatial, v_ln1_g, v_ln1_b, v_ln2_g, v_ln2_b]
    loss, grads, deltas, new_m, new_v = _reduce_and_update(
        big_grads, small_grads, big_w, big_m, big_v, small_w, small_m, small_v)
    return (loss, grad_x, *grads, *deltas, *new_m, *new_v)


def _ag_start_layer(shards, l, after):
    s_in, s_gu, s_o, s_dn = [s[l] for s in shards]
    first = _ag_start([s_in, s_o], "%da" % l, after=after)
    return first, _ag_start([s_gu, s_dn], "%db" % l, after=[first[4]])


def _ag_finish(gather, after, tag):
    send_sems, recv_sems, shards, lands, _ = gather
    shards, lands = _ag_wait(send_sems, recv_sems, shards, lands, after, tag)
    return _ag_pass_on(shards, lands)


def _rs_begin(parts, tag, after=()):
    return _rs_sibling_start([p.reshape(4, 2, p.shape[0] // N_DEV, D_MODEL) for p in parts], tag, after)


def _rs_continue(sibling_flight, after, c_arr, tag):
    send_sems, recv_sems, parts, lands, _ = sibling_flight
    parts, got = _rs_sibling_wait(send_sems, recv_sems, parts, lands, after, tag)
    return _rs_chip_start(_rs_chip_sum(parts, got, c_arr), tag)


def _local_step(xs, target, shards, gather, w_conv_full, w_pool, pool_scale, sgu_ln_g, w_spatial, b_spatial,
                ln1_g, ln1_b, ln2_g, ln2_b):
    L = DEPTH
    T = xs.shape[0]
    mx, my, mc = _my_place()
    c_arr = jnp.reshape(mc, (1,)).astype(jnp.int32)
    q_arr = jnp.reshape(2 * mx + my, (1,)).astype(jnp.int32)
    eye2 = jnp.eye(2, dtype=F32)
    wp = w_pool.reshape(L, 2, 2, HALF, HALF)
    wpool_bd = jnp.einsum("ltgcd,gh->ltgchd", wp, eye2).reshape(L, 2, LANES, LANES)
    wsp_t = w_spatial.reshape(L, 3, 2 * CHUNK, CHUNK)
    bias_t = jnp.repeat(jnp.swapaxes(b_spatial.reshape(L, 3, 2, CHUNK), 2, 3), HALF, axis=3)
    mixer_w = (w_conv_full, wpool_bd, pool_scale[:, None, :], sgu_ln_g[:, None, :], wsp_t, bias_t)
    g1, b1, g2, b2 = [a[:, None, :] for a in (ln1_g, ln1_b, ln2_g, ln2_b)]
    one, zero = jnp.ones((1, 1, D_MODEL), F32), jnp.zeros((1, 1, D_MODEL), F32)

    saved = []
    prev, pg, pb = xs, (one, 0), (zero, 0)
    prev_b = xs.astype(BF16)
    weights = []
    for l in range(L):
        g_in, g_o = _ag_finish(gather[0], [] if l == 0 else [prev_b], "%da" % l)
        proj = _mm(prev_b, g_in, "nt", F32, 512, IN_W, "mm_proj", deps=[gather[1][4]] if l == 0 else [])
        mixcat = _mixer_fwd(proj, *mixer_w, l)
        xhat1, rstd1, h_b = _mm_ln_fwd(mixcat, g_o, prev, pg, pb, (g1, l), (b1, l), "mm_wo_ln")
        g_gu, g_dn = _ag_finish(gather[1], [h_b], "%db" % l)
        weights.append((g_in, g_gu, g_o, g_dn))
        deps = []
        if l + 1 < L:
            gather = _ag_start_layer(shards, l + 1, [g_gu])
            deps = [gather[1][4]]
        g_act, u_act, act = _mm_swiglu_fwd(h_b, g_gu, deps=deps)
        xhat2, rstd2, y_b = _mm_ln_fwd(act, g_dn, xhat1, (g1, l), (b1, l), (g2, l), (b2, l), "mm_down_ln")
        saved.append((prev_b, proj, mixcat, xhat1, rstd1, h_b, g_act, u_act, act, xhat2, rstd2))
        prev, pg, pb, prev_b = xhat2, (g2, l), (b2, l), y_b


    small = [None] * L
    big = None
    sibling_flight = None
    above = None
    for l in reversed(range(L)):
        prev_b, proj, mixcat, xhat1, rstd1, h_b, g_act, u_act, act, xhat2, rstd2 = saved[l]
        g_in, g_gu, g_o, g_dn = weights[l]
        chip_flight = None
        if above is None:
            loss_tile, dr2, dr2_b, dg2, db2 = _loss_ln_bwd(xhat2, rstd2, (g2, l), (b2, l), target)
        else:
            dr2, dr2_b, dg2, db2 = _mm_ln_bwd([above[0]], above[1], above[2], xhat2, rstd2, (g2, l),
                                              "mm_dx_ln", deps=[sibling_flight[4]])
            chip_flight = _rs_continue(sibling_flight, [dr2_b], c_arr, str(l + 1))
        dg_b, du_b = _mm_swiglu_bwd(dr2_b, g_dn, g_act, u_act, deps=[chip_flight[4]] if chip_flight else [])
        p_dn = _mm(act, dr2_b, "tn", BF16, DW_TM, D_MODEL // 2, "mm_dw_down")
        p_gu = _mm_tn_pair(dg_b, du_b, h_b, DW_TM, "mm_dw_gate_up")
        ffn_sibling = _rs_begin([p_gu, p_dn], "0b") if l == 0 else None
        dr1, dr1_b, dg1, db1, dmix = _mm_ln_bwd([dg_b, du_b], g_gu, dr2, xhat1, rstd1, (g1, l), "mm_dh_ln",
                                                deps=[ffn_sibling[4]] if l == 0 else [], w_back=g_o)
        ffn_flight = _rs_continue(ffn_sibling, [dr1_b], c_arr, "0b") if l == 0 else None
        p_o = _mm(mixcat, dr1_b, "tn", BF16, 512, D_MODEL, "mm_dw_o")
        dproj, dwc, dwp, dps, dlng, dwsp, dbias = _mixer_bwd(proj, dmix, *mixer_w, l,
                                                             deps=[ffn_flight[4]] if l == 0 else [])
        p_in = _mm(dproj, prev_b, "tn", BF16, IN_W, D_MODEL // 2, "mm_dw_in")
        small[l] = (dwc, dwp, dps, dlng, dwsp, dbias, dg1, db1, dg2, db2)
        above = (dproj, g_in, dr1)
        if chip_flight is not None:
            big = list(_rs_chip_finish(chip_flight, [p_in], q_arr, str(l + 1), l + 1, big))
        if l > 0:
            sibling_flight = _rs_begin([p_in, p_gu, p_o, p_dn], str(l))
        else:
            big[1], big[3] = _rs_chip_finish(ffn_flight, [p_in, p_o], q_arr, "0b", 0, [big[1], big[3]])

    def stack(i):
        return jnp.stack([small[l][i] for l in range(L)])

    dwp_bd = stack(1).reshape(L, 2, 2, HALF, 2, HALF)
    dwp_all = jnp.einsum("ltgchd,gh->ltgcd", dwp_bd, eye2).reshape(L, 4, HALF, HALF)
    dbs_all = jnp.swapaxes(stack(5)[:, :, :, :2], 2, 3).reshape(L, 6, CHUNK)
    small_grads = [stack(0), dwp_all, stack(2).reshape(L, POOL_W), stack(3).reshape(L, SGU_W),
                   stack(4).reshape(L, 6, CHUNK, CHUNK), dbs_all] + [stack(i).reshape(L, D_MODEL) for i in (6, 7, 8, 9)]
    small_grads.append(loss_tile[0, :1])
    packed_small = _allreduce_small(_pack(small_grads), deps=[big[1]])
    sibling_flight = _rs_begin([p_in, p_o], "0a", after=[packed_small])
    grad_x = _mm_ln_bwd([above[0]], above[1], above[2], None, None, None, "mm_dx_out", deps=[sibling_flight[4]])
    last_flight = _rs_continue(sibling_flight, [grad_x], c_arr, "0a")
    return grad_x, (big, last_flight, q_arr), (packed_small, [a.shape for a in small_grads])


def _rs_chip_finish(in_flight, after, q, tag, layer, into):
    send_sems, recv_sems, sums, lands, _ = in_flight
    sums, got = _rs_chip_wait(send_sems, recv_sems, sums, lands, after, tag)
    return _rs_finish(sums, got, q, layer, into)


def _reduce_and_update(big_grads, small_grads, big_w, big_m, big_v, small_w, small_m, small_v):
    L = DEPTH
    mx, my, mc = _my_place()
    dev = 4 * mx + 2 * my + mc
    conv_cols = CONV_W // N_DEV
    w_in, w_gate_up, w_o, w_down = big_w
    m_w_in, m_w_gate_up, m_w_o, m_w_down = big_m
    v_w_in, v_w_gate_up, v_w_o, v_w_down = big_v
    packed_g, small_shapes = small_grads
    big, last_flight, q_arr = big_grads

    def widen_conv(a):
        return lax.dynamic_update_slice(jnp.zeros((L, 3, CONV_W), F32), a, (0, 0, dev * conv_cols))

    small_m = [widen_conv(small_m[0])] + list(small_m[1:])
    small_v = [widen_conv(small_v[0])] + list(small_v[1:])
    pk_d, pk_m, pk_v = _adamw(_pack(small_w), packed_g, _pack(small_m), _pack(small_v), packed_g.shape[0] // 2)
    sg = _unpack(packed_g, small_shapes)
    sd = _unpack(pk_d, small_shapes)
    sm = _unpack(pk_m, small_shapes)
    sv = _unpack(pk_v, small_shapes)

    def conv_cols_of(a):
        return lax.dynamic_slice(a, (0, 0, dev * conv_cols), (L, 3, conv_cols))

    for lst in (sg, sd, sm, sv):
        lst[0] = conv_cols_of(lst[0])

    tr = lambda a: jnp.swapaxes(a, 1, 2)
    gt_gu, g_w_dn = big[1], big[3]
    d_gu, m_gu, v_gu = [tr(a) for a in _adamw(tr(w_gate_up), gt_gu, tr(m_w_gate_up), tr(v_w_gate_up), gt_gu.shape[1] // 2)]
    d_dn, m_dn, v_dn = _adamw(w_down, g_w_dn, m_w_down, v_w_down, w_down.shape[1])
    gt_in, g_w_o = _rs_chip_finish(last_flight, [d_gu, d_dn, pk_d], q_arr, "0a", 0, [big[0], big[2]])
    d_in, m_in, v_in = [tr(a) for a in _adamw(tr(w_in), gt_in, tr(m_w_in), tr(v_w_in), gt_in.shape[1])]
    d_o, m_o, v_o = _adamw(w_o, g_w_o, m_w_o, v_w_o, w_o.shape[1])
    g_w_in, g_w_gu = tr(gt_in), tr(gt_gu)

    def ordered(big_in, big_o, big_gu, big_dn, sm_list):
        return [big_in, sm_list[0], sm_list[1], sm_list[2], sm_list[3], sm_list[4], sm_list[5], big_o,
                sm_list[6], sm_list[7], big_gu, big_dn, sm_list[8], sm_list[9]]

    grads = ordered(g_w_in, g_w_o, g_w_gu, g_w_dn, sg)
    deltas = ordered(d_in, d_o, d_gu, d_dn, sd)
    new_m = ordered(m_in, m_o, m_gu, m_dn, sm)
    new_v = ordered(v_in, v_o, v_gu, v_dn, sv)
    return sg[10][0], grads, deltas, new_m, new_v
```

```python
import math

import jax
import jax.numpy as jnp
from jax import lax
from jax.experimental import pallas as pl
from jax.experimental.pallas import tpu as pltpu

F32 = jnp.float32
BF16 = jnp.bfloat16
MESH = pl.DeviceIdType.MESH

D_MODEL = 1024
DEPTH = 4
CONV_W = 384
POOL_W = 256
SGU_W = 384
IN_W = 3 * CONV_W + POOL_W + 2 * SGU_W
D_FF = 2816
CHUNK = 128
ALPHA = float((2 * DEPTH) ** 0.25)
LN_EPS = 1e-5
ADAM_LR, ADAM_B1, ADAM_B2, ADAM_EPS, ADAM_WD, ADAM_STEP = 0.001, 0.9, 0.999, 1e-08, 0.01, 10

N_DEV = 8
LANES = 128
HALF = 64
VMEM_LIMIT = 52 * 1024 * 1024

INV_SQRT2 = 0.7071067811865476
INV_SQRT_2PI = 0.3989422804014327


def _cparams(sem=None, **kw):
    if sem is not None:
        kw["dimension_semantics"] = sem
    return pltpu.CompilerParams(vmem_limit_bytes=VMEM_LIMIT, **kw)


_DN = {"nt": (((1,), (1,)), ((), ())), "tn": (((0,), (0,)), ((), ()))}


def _mm(a, b, mode, out_dtype, tm, tn, name, deps=()):
    if mode == "nt":
        (M, K), N = a.shape, b.shape[0]
        a_spec = pl.BlockSpec((tm, K), lambda i, j: (i, 0))
        b_spec = pl.BlockSpec((tn, K), lambda i, j: (j, 0))
    else:
        (K, M), N = a.shape, b.shape[1]
        a_spec = pl.BlockSpec((K, tm), lambda i, j: (0, i))
        b_spec = pl.BlockSpec((K, tn), lambda i, j: (0, j))
    assert M % tm == 0 and N % tn == 0, (M, N, K, tm, tn)
    nd = len(deps)

    def body(*refs):
        a_ref, b_ref, o_ref = refs[0], refs[1], refs[2 + nd]
        o_ref[...] = lax.dot_general(a_ref[...], b_ref[...], _DN[mode], preferred_element_type=F32).astype(o_ref.dtype)

    return pl.pallas_call(
        body,
        name=name,
        grid=(M // tm, N // tn),
        in_specs=[a_spec, b_spec] + [pl.BlockSpec(memory_space=pl.ANY)] * nd,
        out_specs=pl.BlockSpec((tm, tn), lambda i, j: (i, j)),
        out_shape=jax.ShapeDtypeStruct((M, N), out_dtype),
        compiler_params=_cparams(("parallel", "parallel")),
    )(a, b, *deps)


def _mm_tn_pair(a1, a2, b, tm, name):
    K, M = a1.shape
    N = b.shape[1]
    n1 = M // tm

    def body(a1_ref, a2_ref, b_ref, o_ref):
        i = pl.program_id(0)

        @pl.when(i < n1)
        def _():
            o_ref[...] = lax.dot_general(a1_ref[...], b_ref[...], _DN["tn"], preferred_element_type=F32).astype(o_ref.dtype)

        @pl.when(i >= n1)
        def _():
            o_ref[...] = lax.dot_general(a2_ref[...], b_ref[...], _DN["tn"], preferred_element_type=F32).astype(o_ref.dtype)

    return pl.pallas_call(
        body, name=name, grid=(2 * n1,),
        in_specs=[pl.BlockSpec((K, tm), lambda i: (0, jnp.minimum(i, n1 - 1))),
                  pl.BlockSpec((K, tm), lambda i: (0, jnp.maximum(i - n1, 0))),
                  pl.BlockSpec((K, N), lambda i: (0, 0))],
        out_specs=pl.BlockSpec((tm, N), lambda i: (i, 0)),
        out_shape=jax.ShapeDtypeStruct((2 * M, N), BF16),
        compiler_params=_cparams(("arbitrary",)),
    )(a1, a2, b)


LN_SUB = 256
LN_TM = 512


def _vec(v):
    arr, layer = v
    return arr, pl.BlockSpec((None, 1, D_MODEL), lambda *_: (layer, 0, 0))


def _mm_ln_fwd(a, b, prev, pg, pb, g, bias, name):
    T, K = a.shape
    tm = LN_TM

    def body(a_ref, b_ref, prev_ref, pg_ref, pb_ref, g_ref, bias_ref, xhat_ref, rstd_ref, y_ref):
        for s in range(tm // LN_SUB):
            rows = slice(s * LN_SUB, (s + 1) * LN_SUB)
            mm = jnp.dot(a_ref[rows, :], b_ref[...], preferred_element_type=F32)
            r = ALPHA * (prev_ref[rows, :] * pg_ref[...] + pb_ref[...]) + mm
            mu = jnp.mean(r, axis=-1, keepdims=True)
            xc = r - mu
            var = jnp.mean(xc * xc, axis=-1, keepdims=True)
            rstd = lax.rsqrt(var + LN_EPS)
            xhat = xc * rstd
            xhat_ref[rows, :] = xhat
            rstd_ref[rows, :] = rstd
            y_ref[rows, :] = (xhat * g_ref[...] + bias_ref[...]).astype(y_ref.dtype)

    row = pl.BlockSpec((tm, D_MODEL), lambda i: (i, 0))
    vecs = [_vec(v) for v in (pg, pb, g, bias)]
    return pl.pallas_call(
        body, name=name, grid=(T // tm,),
        in_specs=[pl.BlockSpec((tm, K), lambda i: (i, 0)),
                  pl.BlockSpec((K, D_MODEL), lambda i: (0, 0), pipeline_mode=pl.Buffered(1)),
                  row] + [s for _, s in vecs],
        out_specs=[row, pl.BlockSpec((tm, 1), lambda i: (i, 0)), row],
        out_shape=[jax.ShapeDtypeStruct((T, D_MODEL), F32), jax.ShapeDtypeStruct((T, 1), F32),
                   jax.ShapeDtypeStruct((T, D_MODEL), BF16)],
        compiler_params=_cparams(("parallel",)),
    )(a, b, prev, *[a_ for a_, _ in vecs])


def _mm_ln_bwd(a_list, b, dres, xhat, rstd, g, name, deps=(), w_back=None):
    T = a_list[0].shape[0]
    tm = LN_TM
    na, nd = len(a_list), len(deps)
    ks = [a.shape[1] for a in a_list]
    last = xhat is None
    nout = 1 if last else (5 if w_back is not None else 4)

    def body(*refs):
        a_refs, b_ref, dres_ref = refs[:na], refs[na], refs[na + 1]
        if not last:
            xhat_ref, rstd_ref, g_ref = refs[na + 2:na + 5]
            dr_ref, drb_ref, dg_ref, db_ref = refs[len(refs) - nout:len(refs) - nout + 4]

            @pl.when(pl.program_id(0) == 0)
            def _():
                dg_ref[...] = jnp.zeros_like(dg_ref)
                db_ref[...] = jnp.zeros_like(db_ref)

        for s in range(tm // LN_SUB):
            rows = slice(s * LN_SUB, (s + 1) * LN_SUB)
            mm, off = None, 0
            for a_ref, k in zip(a_refs, ks):
                part = jnp.dot(a_ref[rows, :], b_ref[off:off + k, :], preferred_element_type=F32)
                mm = part if mm is None else mm + part
                off += k
            dy = ALPHA * dres_ref[rows, :] + mm
            if last:
                refs[-1][rows, :] = dy
                continue
            xhat_v = xhat_ref[rows, :]
            dg_ref[...] += jnp.sum(dy * xhat_v, axis=0, keepdims=True)
            db_ref[...] += jnp.sum(dy, axis=0, keepdims=True)
            dxh = dy * g_ref[...]
            m1 = jnp.mean(dxh, axis=-1, keepdims=True)
            m2 = jnp.mean(dxh * xhat_v, axis=-1, keepdims=True)
            dr = rstd_ref[rows, :] * (dxh - m1 - xhat_v * m2)
            dr_ref[rows, :] = dr
            dr_b = dr.astype(drb_ref.dtype)
            drb_ref[rows, :] = dr_b
            if w_back is not None:
                refs[-1][rows, :] = lax.dot_general(dr_b, refs[na + 5][...], _DN["nt"], preferred_element_type=F32)

    row = pl.BlockSpec((tm, D_MODEL), lambda i: (i, 0))
    vec = pl.BlockSpec((1, D_MODEL), lambda i: (0, 0))
    in_specs = [pl.BlockSpec((tm, k), lambda i: (i, 0)) for k in ks]
    in_specs += [pl.BlockSpec((sum(ks), D_MODEL), lambda i: (0, 0), pipeline_mode=pl.Buffered(1)), row]
    args = list(a_list) + [b, dres]
    if last:
        out_specs, out_shape = row, jax.ShapeDtypeStruct((T, D_MODEL), F32)
    else:
        g_arr, g_spec = _vec(g)
        in_specs += [row, pl.BlockSpec((tm, 1), lambda i: (i, 0)), g_spec]
        args += [xhat, rstd, g_arr]
        out_specs = [row, row, vec, vec]
        out_shape = [jax.ShapeDtypeStruct((T, D_MODEL), F32), jax.ShapeDtypeStruct((T, D_MODEL), BF16),
                     jax.ShapeDtypeStruct((1, D_MODEL), F32), jax.ShapeDtypeStruct((1, D_MODEL), F32)]
        if w_back is not None:
            in_specs.append(pl.BlockSpec(w_back.shape, lambda i: (0, 0), pipeline_mode=pl.Buffered(1)))
            args.append(w_back)
            out_specs.append(row)
            out_shape.append(jax.ShapeDtypeStruct((T, w_back.shape[0]), F32))
    return pl.pallas_call(
        body, name=name, grid=(T // tm,),
        in_specs=in_specs + [pl.BlockSpec(memory_space=pl.ANY)] * nd,
        out_specs=out_specs, out_shape=out_shape,
        compiler_params=_cparams(("parallel",) if last else ("arbitrary",)),
    )(*args, *deps)


DW_TM = 1408
FF_TN = 256
FF_TM = 2048
SAVED_GU = BF16


def _mm_swiglu_fwd(h, w_gu, deps=()):
    T = h.shape[0]
    tm = min(T, FF_TM)
    nj = D_FF // FF_TN
    nd = len(deps)

    def body(*refs):
        h_ref, wg_ref, wu_ref = refs[:3]
        g_ref, u_ref, act_ref = refs[3 + nd:]
        hv = h_ref[...]
        gv = lax.dot_general(hv, wg_ref[...], _DN["nt"], preferred_element_type=F32)
        uv = lax.dot_general(hv, wu_ref[...], _DN["nt"], preferred_element_type=F32)
        g_ref[...] = gv.astype(g_ref.dtype)
        u_ref[...] = uv.astype(u_ref.dtype)
        act_ref[...] = (gv * jax.nn.sigmoid(gv) * uv).astype(act_ref.dtype)

    tile = pl.BlockSpec((tm, FF_TN), lambda j, i: (i, j))
    return pl.pallas_call(
        body, name="mm_gate_up_swiglu", grid=(nj, T // tm),
        in_specs=[pl.BlockSpec((tm, D_MODEL), lambda j, i: (i, 0)),
                  pl.BlockSpec((FF_TN, D_MODEL), lambda j, i: (j, 0)),
                  pl.BlockSpec((FF_TN, D_MODEL), lambda j, i: (j + nj, 0))] + [pl.BlockSpec(memory_space=pl.ANY)] * nd,
        out_specs=[tile, tile, tile],
        out_shape=[jax.ShapeDtypeStruct((T, D_FF), SAVED_GU), jax.ShapeDtypeStruct((T, D_FF), SAVED_GU),
                   jax.ShapeDtypeStruct((T, D_FF), BF16)],
        compiler_params=_cparams(("parallel", "parallel")),
    )(h, w_gu, w_gu, *deps)


def _mm_swiglu_bwd(dr, w_dn, g, u, deps=()):
    T = dr.shape[0]
    tm = min(T, FF_TM)

    def body(*refs):
        dr_ref, w_ref, g_ref, u_ref = refs[:4]
        dg_ref, du_ref = refs[-2:]
        da = lax.dot_general(dr_ref[...], w_ref[...], _DN["nt"], preferred_element_type=F32)
        gv, uv = g_ref[...].astype(F32), u_ref[...].astype(F32)
        s = jax.nn.sigmoid(gv)
        du_ref[...] = (da * (gv * s)).astype(du_ref.dtype)
        dg_ref[...] = (da * uv * (s * (1.0 + gv * (1.0 - s)))).astype(dg_ref.dtype)

    tile = pl.BlockSpec((tm, FF_TN), lambda j, i: (i, j))
    return pl.pallas_call(
        body, name="mm_dact_swiglu", grid=(D_FF // FF_TN, T // tm),
        in_specs=[pl.BlockSpec((tm, D_MODEL), lambda j, i: (i, 0)), pl.BlockSpec((FF_TN, D_MODEL), lambda j, i: (j, 0)),
                  tile, tile] + [ANY] * len(deps),
        out_specs=[tile, tile],
        out_shape=[jax.ShapeDtypeStruct((T, D_FF), BF16)] * 2,
        compiler_params=_cparams(("parallel", "parallel")),
    )(dr, w_dn, g, u, *deps)


def _gelu(x):
    return 0.5 * x * (1.0 + lax.erf(x * INV_SQRT2))


def _gelu_grad(x):
    return 0.5 * (1.0 + lax.erf(x * INV_SQRT2)) + x * (jnp.exp(-0.5 * x * x) * INV_SQRT_2PI)


def _shift_down(z, k):
    row = lax.broadcasted_iota(jnp.int32, z.shape, 0)
    return jnp.where(row >= k, pltpu.roll(z, k, 0), 0.0)


def _shift_up(z, k):
    n = z.shape[0]
    row = lax.broadcasted_iota(jnp.int32, z.shape, 0)
    return jnp.where(row < n - k, pltpu.roll(z, n - k, 0), 0.0)


def _lo_mask(shape):
    return lax.broadcasted_iota(jnp.int32, shape, len(shape) - 1) < HALF


def _seg_mean(x, lo):
    a = jnp.sum(jnp.where(lo, x, 0.0), axis=-1, keepdims=True)
    b = jnp.sum(jnp.where(lo, 0.0, x), axis=-1, keepdims=True)
    return jnp.where(lo, a, b) * (1.0 / HALF)


def _pool_windows(first):
    lo = _lo_mask((1, LANES))
    return jnp.where(first, jnp.where(lo, 2.0, 4.0), jnp.where(lo, 8.0, 16.0)), lo


def _pool_mean_minus_token(p, first):
    wl, lo = _pool_windows(first)
    s2 = p + _shift_down(p, 1)
    s4 = s2 + _shift_down(s2, 2)
    s8 = s4 + _shift_down(s4, 4)
    s16 = s8 + _shift_down(s8, 8)
    win = jnp.where(first, jnp.where(lo, s2, s4), jnp.where(lo, s8, s16))
    t1 = (lax.broadcasted_iota(jnp.int32, p.shape, 0) + 1).astype(F32)
    count = jnp.minimum(t1, wl)
    return win / count - p, count


SGU_UNROLL_FWD = 4
SGU_UNROLL_BWD = 2


def _tril_keep():
    r = lax.broadcasted_iota(jnp.int32, (2 * CHUNK, CHUNK), 0)
    s = lax.broadcasted_iota(jnp.int32, (2 * CHUNK, CHUNK), 1)
    return s <= (r & (CHUNK - 1))


def _sgu_chunk_fwd(u, v, g, wm, bias, lo):
    ug = _gelu(u)
    vg = _gelu(v)
    mu = _seg_mean(vg, lo)
    xc = vg - mu
    var = _seg_mean(xc * xc, lo)
    rstd = lax.rsqrt(var + LN_EPS)
    vn = xc * rstd
    vh = (vn * g).astype(BF16)
    mm2 = jnp.dot(wm, vh, preferred_element_type=F32)
    mixed = jnp.where(lo, mm2[:CHUNK], mm2[CHUNK:]) + bias
    return ug, vn, rstd, vh, mixed


def _mixer_fwd(proj, wconv, wpool_bd, pscale, lng, wsp, bias, layer):
    T = proj.shape[0]
    nchunk = T // CHUNK

    def body(a_ref, b_ref, c_ref, wc_ref, wp_ref, ps_ref, lng_ref, wsp_ref, bias_ref, o_ref):
        j = pl.program_id(0)

        @pl.when(j < 3)
        def _conv():
            z = c_ref[...] * a_ref[...]
            w = wc_ref[...]
            y = w[0:1] * _shift_down(z, 2) + w[1:2] * _shift_down(z, 1) + w[2:3] * z
            o_ref[...] = (b_ref[...] * y).astype(o_ref.dtype)

        @pl.when((j >= 3) & (j < 5))
        def _pool():
            d, _ = _pool_mean_minus_token(a_ref[...], j == 3)
            y = jnp.dot(d.astype(BF16), wp_ref[...].astype(BF16), preferred_element_type=F32)
            o_ref[...] = (y * ps_ref[...]).astype(o_ref.dtype)

        @pl.when(j >= 5)
        def _sgu():
            lo = _lo_mask((CHUNK, LANES))
            wm = jnp.where(_tril_keep(), wsp_ref[...], 0.0).astype(BF16)
            bias_t = bias_ref[...]
            g = lng_ref[...]

            def chunk(n, carry):
                rows = pl.ds(pl.multiple_of(n * CHUNK, CHUNK), CHUNK)
                ug, _, _, _, mixed = _sgu_chunk_fwd(a_ref[rows, :], b_ref[rows, :], g, wm, bias_t, lo)
                o_ref[rows, :] = (ug * mixed).astype(o_ref.dtype)
                return carry

            lax.fori_loop(0, nchunk, chunk, 0, unroll=SGU_UNROLL_FWD)

    def col(f):
        return lambda j: (0, f(j))

    clip = lambda v, lo, hi: jnp.minimum(jnp.maximum(v, lo), hi)
    return pl.pallas_call(
        body,
        name="mixer_fwd",
        grid=(8,),
        in_specs=[
            pl.BlockSpec((T, LANES), col(lambda j: jnp.where(j < 3, j, jnp.where(j < 5, j + 6, j + 6)))),
            pl.BlockSpec((T, LANES), col(lambda j: jnp.where(j < 3, j + 3, jnp.where(j < 5, 5, j + 9)))),
            pl.BlockSpec((T, LANES), col(lambda j: jnp.where(j < 3, j + 6, 8))),
            pl.BlockSpec((None, 3, LANES), lambda j: (layer, 0, clip(j, 0, 2))),
            pl.BlockSpec((None, None, LANES, LANES), lambda j: (layer, clip(j - 3, 0, 1), 0, 0)),
            pl.BlockSpec((None, 1, LANES), lambda j: (layer, 0, clip(j - 3, 0, 1))),
            pl.BlockSpec((None, 1, LANES), lambda j: (layer, 0, clip(j - 5, 0, 2))),
            pl.BlockSpec((None, None, 2 * CHUNK, CHUNK), lambda j: (layer, clip(j - 5, 0, 2), 0, 0)),
            pl.BlockSpec((None, None, CHUNK, LANES), lambda j: (layer, clip(j - 5, 0, 2), 0, 0)),
        ],
        out_specs=pl.BlockSpec((T, LANES), lambda j: (0, j)),
        out_shape=jax.ShapeDtypeStruct((T, D_MODEL), BF16),
        compiler_params=_cparams(("arbitrary",)),
    )(proj, proj, proj, wconv, wpool_bd, pscale, lng, wsp, bias)


def _mixer_bwd(proj, dmix, wconv, wpool_bd, pscale, lng, wsp, bias, layer, deps=()):
    T = proj.shape[0]
    nchunk = T // CHUNK

    def body(*refs):
        a_ref, b_ref, c_ref, dm_ref, wc_ref, wp_ref, ps_ref, lng_ref, wsp_ref, bias_ref = refs[:10]
        o_ref, dwc_ref, dwp_ref, dps_ref, dlng_ref, dwsp_ref, dbias_ref, keep1, keep2 = refs[10 + len(deps):]
        k = pl.program_id(0)

        @pl.when(k < 3)
        def _conv():
            xa, gb, gc, dya = a_ref[...], b_ref[...], c_ref[...], dm_ref[...]
            w = wc_ref[...]
            z = gc * xa
            z1 = _shift_down(z, 1)
            z2 = _shift_down(z, 2)
            y = w[0:1] * z2 + w[1:2] * z1 + w[2:3] * z
            dyv = dya * gb
            dz = w[2:3] * dyv + w[1:2] * _shift_up(dyv, 1) + w[0:1] * _shift_up(dyv, 2)
            dwc_ref[0:1, :] = jnp.sum(dyv * z2, axis=0, keepdims=True)
            dwc_ref[1:2, :] = jnp.sum(dyv * z1, axis=0, keepdims=True)
            dwc_ref[2:3, :] = jnp.sum(dyv * z, axis=0, keepdims=True)
            o_ref[...] = (dz * gc).astype(o_ref.dtype)
            keep1[k] = (dya * y).astype(keep1.dtype)
            keep1[k + 3] = (dz * xa).astype(keep1.dtype)

        @pl.when((k >= 3) & (k < 9))
        def _emit_gb_gc():
            o_ref[...] = keep1[k - 3]

        @pl.when((k >= 9) & (k < 11))
        def _pool():
            first = k == 9
            p, dyb = a_ref[...], dm_ref[...]
            d, count = _pool_mean_minus_token(p, first)
            w2 = wp_ref[...].astype(BF16)
            db = d.astype(BF16)
            y = jnp.dot(db, w2, preferred_element_type=F32)
            dps_ref[...] = jnp.sum(dyb * y, axis=0, keepdims=True)
            dyv = (dyb * ps_ref[...]).astype(BF16)
            dd = lax.dot_general(dyv, w2, _DN["nt"], preferred_element_type=F32)
            dwp_ref[...] = lax.dot_general(db, dyv, _DN["tn"], preferred_element_type=F32)
            dwin = dd / count
            a2 = dwin + _shift_up(dwin, 1)
            a4 = a2 + _shift_up(a2, 2)
            a8 = a4 + _shift_up(a4, 4)
            a16 = a8 + _shift_up(a8, 8)
            _, lo = _pool_windows(first)
            back = jnp.where(first, jnp.where(lo, a2, a4), jnp.where(lo, a8, a16))
            o_ref[...] = (back - dd).astype(o_ref.dtype)

        @pl.when((k >= 11) & (k < 14))
        def _sgu():
            lo = _lo_mask((CHUNK, LANES))
            keep = _tril_keep()
            wm = jnp.where(keep, wsp_ref[...], 0.0).astype(BF16)
            bias_t = bias_ref[...]
            g = lng_ref[...]
            dwsp_ref[...] = jnp.zeros_like(dwsp_ref)
            dbias_ref[...] = jnp.zeros_like(dbias_ref)
            dlng_ref[...] = jnp.zeros_like(dlng_ref)

            def chunk(n, carry):
                rows = pl.ds(pl.multiple_of(n * CHUNK, CHUNK), CHUNK)
                u, v, dyc = a_ref[rows, :], b_ref[rows, :], dm_ref[rows, :]
                ug, vn, rstd, vh, mixed = _sgu_chunk_fwd(u, v, g, wm, bias_t, lo)
                dmx = dyc * ug
                o_ref[rows, :] = (dyc * mixed * _gelu_grad(u)).astype(o_ref.dtype)
                dbias_ref[...] += dmx
                dst = jnp.concatenate([jnp.where(lo, dmx, 0.0), jnp.where(lo, 0.0, dmx)], axis=0).astype(BF16)
                dwsp_ref[...] += lax.dot_general(dst, vh, _DN["nt"], preferred_element_type=F32)
                dvh = lax.dot_general(wm, dst, _DN["tn"], preferred_element_type=F32)
                dlng_ref[...] += jnp.sum(dvh * vn, axis=0, keepdims=True)
                dvn = dvh * g
                m1 = _seg_mean(dvn, lo)
                m2 = _seg_mean(dvn * vn, lo)
                dvg = rstd * (dvn - m1 - vn * m2)
                keep2[k - 11, rows, :] = (dvg * _gelu_grad(v)).astype(keep2.dtype)
                return carry

            lax.fori_loop(0, nchunk, chunk, 0, unroll=SGU_UNROLL_BWD)
            dwsp_ref[...] = jnp.where(keep, dwsp_ref[...], 0.0)
            dbt = dbias_ref[...]
            lane = lax.broadcasted_iota(jnp.int32, (CHUNK, LANES), 1)
            sa = jnp.sum(jnp.where(lo, dbt, 0.0), axis=-1, keepdims=True)
            sb = jnp.sum(jnp.where(lo, 0.0, dbt), axis=-1, keepdims=True)
            dbias_ref[...] = jnp.where(lane == 0, sa, jnp.where(lane == 1, sb, 0.0))

        @pl.when(k >= 14)
        def _emit_v():
            o_ref[...] = keep2[k - 14]

    def col(f):
        return lambda k: (0, f(k))

    clip = lambda v, lo, hi: jnp.minimum(jnp.maximum(v, lo), hi)
    view_a = lambda k: jnp.where(k < 3, k, jnp.where(k < 9, 2, jnp.where(k < 14, k, 13)))
    view_b = lambda k: jnp.where(k < 3, k + 3, jnp.where(k < 11, 5, jnp.where(k < 14, k + 3, 16)))
    view_c = lambda k: jnp.where(k < 3, k + 6, 8)
    view_dm = lambda k: jnp.where(k < 3, k, jnp.where(k < 9, 2, jnp.where(k < 14, k - 6, 7)))
    return pl.pallas_call(
        body,
        name="mixer_bwd",
        grid=(17,),
        in_specs=[
            pl.BlockSpec((T, LANES), col(view_a)),
            pl.BlockSpec((T, LANES), col(view_b)),
            pl.BlockSpec((T, LANES), col(view_c)),
            pl.BlockSpec((T, LANES), col(view_dm)),
            pl.BlockSpec((None, 3, LANES), lambda k: (layer, 0, clip(k, 0, 2))),
            pl.BlockSpec((None, None, LANES, LANES), lambda k: (layer, clip(k - 9, 0, 1), 0, 0)),
            pl.BlockSpec((None, 1, LANES), lambda k: (layer, 0, clip(k - 9, 0, 1))),
            pl.BlockSpec((None, 1, LANES), lambda k: (layer, 0, clip(k - 11, 0, 2))),
            pl.BlockSpec((None, None, 2 * CHUNK, CHUNK), lambda k: (layer, clip(k - 11, 0, 2), 0, 0)),
            pl.BlockSpec((None, None, CHUNK, LANES), lambda k: (layer, clip(k - 11, 0, 2), 0, 0)),
        ] + [pl.BlockSpec(memory_space=pl.ANY)] * len(deps),
        out_specs=[
            pl.BlockSpec((T, LANES), lambda k: (0, k)),
            pl.BlockSpec((3, LANES), col(lambda k: clip(k, 0, 2))),
            pl.BlockSpec((None, LANES, LANES), lambda k: (clip(k - 9, 0, 1), 0, 0)),
            pl.BlockSpec((1, LANES), col(lambda k: clip(k - 9, 0, 1))),
            pl.BlockSpec((1, LANES), col(lambda k: clip(k - 11, 0, 2))),
            pl.BlockSpec((None, 2 * CHUNK, CHUNK), lambda k: (clip(k - 11, 0, 2), 0, 0)),
            pl.BlockSpec((None, CHUNK, LANES), lambda k: (clip(k - 11, 0, 2), 0, 0)),
        ],
        out_shape=[
            jax.ShapeDtypeStruct((T, IN_W), BF16),
            jax.ShapeDtypeStruct((3, CONV_W), F32),
            jax.ShapeDtypeStruct((2, LANES, LANES), F32),
            jax.ShapeDtypeStruct((1, POOL_W), F32),
            jax.ShapeDtypeStruct((1, SGU_W), F32),
            jax.ShapeDtypeStruct((3, 2 * CHUNK, CHUNK), F32),
            jax.ShapeDtypeStruct((3, CHUNK, LANES), F32),
        ],
        scratch_shapes=[pltpu.VMEM((6, T, LANES), BF16), pltpu.VMEM((3, T, LANES), BF16)],
        compiler_params=_cparams(("arbitrary",)),
    )(proj, proj, proj, dmix, wconv, wpool_bd, pscale, lng, wsp, bias, *deps)


def _loss_ln_bwd(xhat, rstd, g, b, target, tm=256):
    T = xhat.shape[0]

    def body(xhat_ref, rstd_ref, g_ref, b_ref, t_ref, loss_ref, dr_ref, drb_ref, dg_ref, db_ref):
        xhat_v = xhat_ref[...]
        err = xhat_v * g_ref[...] + b_ref[...] - t_ref[...]
        dy = err * (1.0 / D_MODEL)

        @pl.when(pl.program_id(0) == 0)
        def _():
            loss_ref[...] = jnp.zeros_like(loss_ref)
            dg_ref[...] = jnp.zeros_like(dg_ref)
            db_ref[...] = jnp.zeros_like(db_ref)

        part = jnp.sum(jnp.sum(err * err, axis=-1, keepdims=True), axis=0, keepdims=True)
        loss_ref[...] += jnp.broadcast_to(part * (0.5 / D_MODEL), loss_ref.shape)
        dg_ref[...] += jnp.sum(dy * xhat_v, axis=0, keepdims=True)
        db_ref[...] += jnp.sum(dy, axis=0, keepdims=True)
        dxh = dy * g_ref[...]
        m1 = jnp.mean(dxh, axis=-1, keepdims=True)
        m2 = jnp.mean(dxh * xhat_v, axis=-1, keepdims=True)
        dr = rstd_ref[...] * (dxh - m1 - xhat_v * m2)
        dr_ref[...] = dr
        drb_ref[...] = dr.astype(drb_ref.dtype)

    row = pl.BlockSpec((tm, D_MODEL), lambda i: (i, 0))
    vec = pl.BlockSpec((1, D_MODEL), lambda i: (0, 0))
    (g_arr, g_spec), (b_arr, b_spec) = _vec(g), _vec(b)
    return pl.pallas_call(
        body,
        name="loss_ln_bwd",
        grid=(T // tm,),
        in_specs=[row, pl.BlockSpec((tm, 1), lambda i: (i, 0)), g_spec, b_spec, row],
        out_specs=[pl.BlockSpec((8, LANES), lambda i: (0, 0)), row, row, vec, vec],
        out_shape=[jax.ShapeDtypeStruct((8, LANES), F32),
                   jax.ShapeDtypeStruct((T, D_MODEL), F32), jax.ShapeDtypeStruct((T, D_MODEL), BF16),
                   jax.ShapeDtypeStruct((1, D_MODEL), F32), jax.ShapeDtypeStruct((1, D_MODEL), F32)],
        compiler_params=_cparams(("arbitrary",)),
    )(xhat, rstd, g_arr, b_arr, target)


def _adamw(w, g, m, v, tr):
    R, C = w.shape[-2:]
    assert R % tr == 0
    c1 = 1.0 - ADAM_B1 ** ADAM_STEP
    c2 = 1.0 - ADAM_B2 ** ADAM_STEP

    def body(w_ref, g_ref, m_ref, v_ref, d_ref, mo_ref, vo_ref):
        gv = g_ref[...]
        mn = ADAM_B1 * m_ref[...] + (1.0 - ADAM_B1) * gv
        vn = ADAM_B2 * v_ref[...] + (1.0 - ADAM_B2) * (gv * gv)
        d_ref[...] = -ADAM_LR * ((mn / c1) / (jnp.sqrt(vn / c2) + ADAM_EPS) + ADAM_WD * w_ref[...])
        mo_ref[...] = mn
        vo_ref[...] = vn

    if w.ndim == 2:
        grid, blk = (R // tr,), pl.BlockSpec((tr, C), lambda i: (i, 0))
    else:
        grid, blk = (w.shape[0], R // tr), pl.BlockSpec((None, tr, C), lambda l, i: (l, i, 0))
    return pl.pallas_call(
        body, name="adamw", grid=grid, in_specs=[blk] * 4, out_specs=[blk] * 3,
        out_shape=[jax.ShapeDtypeStruct(w.shape, F32)] * 3, compiler_params=_cparams(("parallel",) * len(grid)),
    )(w, g, m, v)


def _my_place():
    return lax.axis_index("x"), lax.axis_index("y"), lax.axis_index("c")


ANY = pl.BlockSpec(memory_space=pl.ANY)
HBM = pl.BlockSpec(memory_space=pltpu.HBM)
SEM = pl.BlockSpec(memory_space=pltpu.SEMAPHORE)
EFFECT = pltpu.SideEffectType.DATAFLOW_SIDE_EFFECTING


def _in_hbm(a):
    return pltpu.with_memory_space_constraint(a, pltpu.HBM)


def _block_rows(ref, dev):
    r = ref.shape[0] // N_DEV
    start = pl.multiple_of((4 * dev[0] + 2 * dev[1] + dev[2]) * r, 16)
    return ref.at[pl.ds(start, r), :]


def _ag_first_copies(s_refs, land_refs, send_sems, recv_sems, receiving):
    x, y, c = _my_place()
    peers = [(x, y, 1 - c)] + [(*chip, c) for chip in _other_chips(x, y)]
    copies = []
    for k, peer in enumerate(peers):
        block = peer if receiving else (x, y, c)
        copies += [pltpu.make_async_remote_copy(
            src_ref=s_refs[w], dst_ref=_block_rows(land_refs[w], block),
            send_sem=send_sems.at[k * len(s_refs) + w], recv_sem=recv_sems.at[k * len(s_refs) + w],
            device_id=peer, device_id_type=MESH)
            for w in range(len(s_refs))]
    return copies


def _ag_start(shards, layer, after=()):
    nw = len(shards)

    def body(*refs):
        s_refs, land_refs = refs[:nw], refs[nw:2 * nw]
        token = refs[-1]
        sems = 2 * nw + len(after)
        for cp in _ag_first_copies(s_refs, land_refs, refs[sems], refs[sems + 1], False):
            cp.start()
        token[...] = jnp.zeros_like(token)

    lands = [lax.empty((N_DEV * s.shape[0], D_MODEL), BF16) for s in shards]
    out = pl.pallas_call(
        body, name="ag_start_%s" % layer,
        in_specs=[HBM] * (2 * nw) + [ANY] * len(after),
        out_specs=(SEM, SEM, *[HBM] * (2 * nw), pl.BlockSpec(memory_space=pltpu.VMEM)),
        out_shape=(pltpu.SemaphoreType.DMA((4 * nw,)), pltpu.SemaphoreType.DMA((4 * nw,)),
                   *[pltpu.HBM(a.shape, a.dtype) for a in list(shards) + lands],
                   jax.ShapeDtypeStruct((8, LANES), F32)),
        input_output_aliases={i: 2 + i for i in range(2 * nw)},
        compiler_params=pltpu.CompilerParams(has_side_effects=EFFECT),
    )(*[_in_hbm(a) for a in list(shards) + lands], *after)
    return out[0], out[1], out[2:2 + nw], out[2 + nw:2 + 2 * nw], out[-1]


def _ag_wait(send_sems, recv_sems, shards, lands, after, layer):
    nw = len(shards)

    def body(*refs):
        s_refs, land_refs = refs[:nw], refs[nw:2 * nw]
        for cp in _ag_first_copies(s_refs, land_refs, refs[2 * nw], refs[2 * nw + 1], True):
            cp.wait_send()
            cp.wait_recv()

    out = pl.pallas_call(
        body, name="ag_wait_%s" % layer,
        in_specs=[HBM] * (2 * nw) + [SEM, SEM] + [ANY] * len(after),
        out_specs=[HBM] * (2 * nw),
        out_shape=[pltpu.HBM(a.shape, a.dtype) for a in list(shards) + list(lands)],
        input_output_aliases={i: i for i in range(2 * nw)},
        compiler_params=pltpu.CompilerParams(has_side_effects=EFFECT),
    )(*shards, *lands, send_sems, recv_sems, *after)
    return out[:nw], out[nw:]


def _ag_pass_on(shards, lands):
    nw = len(shards)

    def body(*refs):
        s_refs, g_refs = refs[:nw], refs[2 * nw:3 * nw]
        send_sems, recv_sems, local_sems = refs[3 * nw:3 * nw + 3]
        stage = refs[3 * nw + 3:]
        x, y, c = _my_place()
        load = [pltpu.make_async_copy(s_refs[w], stage[w], local_sems.at[w]) for w in range(nw)]
        mine = [pltpu.make_async_copy(stage[w], _block_rows(g_refs[w], (x, y, c)), local_sems.at[w])
                for w in range(nw)]
        for cp in load:
            cp.start()
        sends, arrivals = [], []
        for j, chip in enumerate(_other_chips(x, y)):
            for w in range(nw):
                rows_out = _block_rows(g_refs[w], (*chip, c))
                rows_in = _block_rows(g_refs[w], (*chip, 1 - c))
                sends.append(pltpu.make_async_remote_copy(
                    src_ref=rows_out, dst_ref=rows_out, send_sem=send_sems.at[j, w], recv_sem=recv_sems.at[j, w],
                    device_id=(x, y, 1 - c), device_id_type=MESH))
                arrivals.append(pltpu.make_async_remote_copy(
                    src_ref=rows_in, dst_ref=rows_in, send_sem=send_sems.at[j, w], recv_sem=recv_sems.at[j, w],
                    device_id=(x, y, 1 - c), device_id_type=MESH))
        for cp in sends:
            cp.start()
        for w in range(nw):
            load[w].wait()
            mine[w].start()
        for cp in arrivals:
            cp.wait_recv()
        for cp in sends:
            cp.wait_send()
        for cp in mine:
            cp.wait()

    return pl.pallas_call(
        body, name="ag_pass_on",
        in_specs=[ANY] * (2 * nw), out_specs=[ANY] * nw,
        out_shape=[jax.ShapeDtypeStruct(a.shape, a.dtype) for a in lands],
        input_output_aliases={nw + i: i for i in range(nw)},
        scratch_shapes=[pltpu.SemaphoreType.DMA((3, nw)), pltpu.SemaphoreType.DMA((3, nw)),
                        pltpu.SemaphoreType.DMA((nw,))] + [pltpu.VMEM(s.shape, s.dtype) for s in shards],
        compiler_params=_cparams(),
    )(*shards, *lands)


def _rs_sibling_copies(p_refs, land_refs, send_sems, recv_sems):
    x, y, c = _my_place()
    return [pltpu.make_async_remote_copy(
        src_ref=p_refs[w].at[:, 1 - c], dst_ref=land_refs[w],
        send_sem=send_sems.at[w], recv_sem=recv_sems.at[w], device_id=(x, y, 1 - c), device_id_type=MESH)
        for w in range(len(p_refs))]


def _rs_sibling_start(parts, tag, after=()):
    nw = len(parts)
    sems = 2 * nw + len(after)

    def body(*refs):
        for cp in _rs_sibling_copies(refs[:nw], refs[nw:2 * nw], refs[sems], refs[sems + 1]):
            cp.start()
        refs[-1][...] = jnp.zeros_like(refs[-1])

    lands = [lax.empty(p.shape[:1] + p.shape[2:], BF16) for p in parts]
    out = pl.pallas_call(
        body, name="rs_sibling_start_%s" % tag,
        in_specs=[HBM] * (2 * nw) + [ANY] * len(after),
        out_specs=(SEM, SEM, *[HBM] * (2 * nw), pl.BlockSpec(memory_space=pltpu.VMEM)),
        out_shape=(pltpu.SemaphoreType.DMA((nw,)), pltpu.SemaphoreType.DMA((nw,)),
                   *[pltpu.HBM(a.shape, a.dtype) for a in list(parts) + lands],
                   jax.ShapeDtypeStruct((8, LANES), F32)),
        input_output_aliases={i: 2 + i for i in range(2 * nw)},
        compiler_params=pltpu.CompilerParams(has_side_effects=EFFECT),
    )(*[_in_hbm(a) for a in list(parts) + lands], *after)
    return out[0], out[1], out[2:2 + nw], out[2 + nw:2 + 2 * nw], out[-1]


def _rs_sibling_wait(send_sems, recv_sems, parts, lands, after, tag):
    nw = len(parts)

    def body(*refs):
        for cp in _rs_sibling_copies(refs[:nw], refs[nw:2 * nw], refs[2 * nw], refs[2 * nw + 1]):
            cp.wait_send()
            cp.wait_recv()

    out = pl.pallas_call(
        body, name="rs_sibling_wait_%s" % tag,
        in_specs=[HBM] * (2 * nw) + [SEM, SEM] + [ANY] * len(after),
        out_specs=[HBM] * (2 * nw),
        out_shape=[pltpu.HBM(a.shape, a.dtype) for a in list(parts) + list(lands)],
        input_output_aliases={i: i for i in range(2 * nw)},
        compiler_params=pltpu.CompilerParams(has_side_effects=EFFECT),
    )(*parts, *lands, send_sems, recv_sems, *after)
    return out[:nw], out[nw:]


def _rs_chip_sum(parts, gots, c):
    n = len(parts)

    def body(c_ref, *refs):
        for p_ref, g_ref, o_ref in zip(refs[:n], refs[n:2 * n], refs[2 * n:]):
            o_ref[...] = (p_ref[...].astype(F32) + g_ref[...].astype(F32)).astype(o_ref.dtype)

    mine = [pl.BlockSpec((None, None, p.shape[2], D_MODEL), lambda q, c_ref: (q, c_ref[0], 0, 0)) for p in parts]
    theirs = [pl.BlockSpec((None, g.shape[1], D_MODEL), lambda q, c_ref: (q, 0, 0)) for g in gots]
    return pl.pallas_call(
        body, name="rs_chip_sum",
        grid_spec=pltpu.PrefetchScalarGridSpec(
            num_scalar_prefetch=1, grid=(4,), in_specs=mine + theirs, out_specs=theirs),
        out_shape=[jax.ShapeDtypeStruct(g.shape, BF16) for g in gots],
        compiler_params=_cparams(("parallel",)),
    )(c, *parts, *gots)


def _other_chips(x, y):
    return [(1 - x, y), (x, 1 - y), (1 - x, 1 - y)]


def _rs_chip_copies(s_refs, land_refs, send_sems, recv_sems):
    x, y, c = _my_place()
    copies = []
    for k, chip in enumerate(_other_chips(x, y)):
        q = 2 * chip[0] + chip[1]
        copies += [pltpu.make_async_remote_copy(
            src_ref=s_refs[w].at[q], dst_ref=land_refs[w].at[k],
            send_sem=send_sems.at[k * len(s_refs) + w], recv_sem=recv_sems.at[k * len(s_refs) + w],
            device_id=(*chip, c), device_id_type=MESH)
            for w in range(len(s_refs))]
    return copies


def _rs_chip_start(sums, layer):
    nw = len(sums)

    def body(*refs):
        s_refs, land_refs = refs[:nw], refs[nw:2 * nw]
        send_sems, recv_sems = refs[2 * nw], refs[2 * nw + 1]
        token = refs[-1]
        for cp in _rs_chip_copies(s_refs, land_refs, send_sems, recv_sems):
            cp.start()
        token[...] = jnp.zeros_like(token)

    lands = [lax.empty((3,) + s.shape[1:], BF16) for s in sums]
    out = pl.pallas_call(
        body, name="rs_chip_start_%s" % layer,
        in_specs=[HBM] * (2 * nw),
        out_specs=(SEM, SEM, *[HBM] * (2 * nw), pl.BlockSpec(memory_space=pltpu.VMEM)),
        out_shape=(pltpu.SemaphoreType.DMA((3 * nw,)), pltpu.SemaphoreType.DMA((3 * nw,)),
                   *[pltpu.HBM(a.shape, a.dtype) for a in list(sums) + lands],
                   jax.ShapeDtypeStruct((8, LANES), F32)),
        input_output_aliases={i: 2 + i for i in range(2 * nw)},
        compiler_params=pltpu.CompilerParams(has_side_effects=EFFECT),
    )(*[_in_hbm(a) for a in list(sums) + lands])
    return out[0], out[1], out[2:2 + nw], out[2 + nw:2 + 2 * nw], out[-1]


def _rs_chip_wait(send_sems, recv_sems, sums, lands, after, layer):
    nw = len(sums)

    def body(*refs):
        s_refs, land_refs = refs[:nw], refs[nw:2 * nw]
        for cp in _rs_chip_copies(s_refs, land_refs, refs[2 * nw], refs[2 * nw + 1]):
            cp.wait_send()
            cp.wait_recv()

    out = pl.pallas_call(
        body, name="rs_chip_wait_%s" % layer,
        in_specs=[HBM] * (2 * nw) + [SEM, SEM] + [ANY] * len(after),
        out_specs=[HBM] * (2 * nw),
        out_shape=[pltpu.HBM(a.shape, a.dtype) for a in list(sums) + list(lands)],
        input_output_aliases={i: i for i in range(2 * nw)},
        compiler_params=pltpu.CompilerParams(has_side_effects=EFFECT),
    )(*sums, *lands, send_sems, recv_sems, *after)
    return out[:nw], out[nw:]


def _rs_finish(sums, gots, q, layer, into):
    n = len(sums)

    def body(q_ref, *refs):
        for s_ref, g_ref, o_ref in zip(refs[:n], refs[n:2 * n], refs[len(refs) - n:]):
            o_ref[...] = ((s_ref[...].astype(F32) + g_ref[0].astype(F32)) + g_ref[1].astype(F32)) + g_ref[2].astype(F32)

    rows = [s.shape[1] for s in sums]
    in_specs = [pl.BlockSpec((None, r, D_MODEL), lambda i, q_ref: (q_ref[0], 0, 0)) for r in rows]
    in_specs += [pl.BlockSpec((3, r, D_MODEL), lambda i, q_ref: (0, 0, 0)) for r in rows]
    args = [q, *sums, *gots]
    aliases = {}
    if into is not None:
        in_specs += [ANY] * n
        aliases = {len(args) + i: i for i in range(n)}
        args += list(into)
    return pl.pallas_call(
        body, name="rs_finish",
        grid_spec=pltpu.PrefetchScalarGridSpec(
            num_scalar_prefetch=1, grid=(1,), in_specs=in_specs,
            out_specs=[pl.BlockSpec((None, r, D_MODEL), lambda i, q_ref: (layer, 0, 0)) for r in rows]),
        out_shape=[jax.ShapeDtypeStruct((DEPTH, r, D_MODEL), F32) for r in rows],
        input_output_aliases=aliases,
        compiler_params=_cparams(("arbitrary",)),
    )(*args)


def _allreduce_small(vec, deps=()):
    R = vec.shape[0]
    assert R % (8 * N_DEV) == 0
    P = R // N_DEV
    nd = len(deps)

    def body(*refs):
        v_ref = refs[0]
        o_ref, buf, send1, recv1, send2, recv2 = refs[1 + nd:]
        x, y, c = _my_place()
        me = 4 * x + 2 * y + c

        def piece(ref, d):
            return ref.at[pl.ds(pl.multiple_of(d * P, 8), P), :]

        def peer(k):
            p = me ^ k
            return p, (p >> 2, (p >> 1) & 1, p & 1)

        scatter = []
        for k in range(1, N_DEV):
            p, where = peer(k)
            scatter.append(pltpu.make_async_remote_copy(
                src_ref=piece(v_ref, p), dst_ref=buf.at[k], send_sem=send1.at[k - 1], recv_sem=recv1.at[k - 1],
                device_id=where, device_id_type=MESH))
        for cp in scatter:
            cp.start()
        buf[0] = piece(v_ref, me)[...]
        for cp in scatter:
            cp.wait()
        acc = buf[me]
        for d in range(1, N_DEV):
            acc = acc + buf[me ^ d]
        piece(o_ref, me)[...] = acc
        spread, arrivals = [], []
        for k in range(1, N_DEV):
            p, where = peer(k)
            spread.append(pltpu.make_async_remote_copy(
                src_ref=piece(o_ref, me), dst_ref=piece(o_ref, me), send_sem=send2.at[k - 1], recv_sem=recv2.at[k - 1],
                device_id=where, device_id_type=MESH))
            arrivals.append(pltpu.make_async_remote_copy(
                src_ref=piece(o_ref, p), dst_ref=piece(o_ref, p), send_sem=send2.at[k - 1], recv_sem=recv2.at[k - 1],
                device_id=where, device_id_type=MESH))
        for cp in spread:
            cp.start()
        for cp in arrivals:
            cp.wait_recv()
        for cp in spread:
            cp.wait_send()

    sems = pltpu.SemaphoreType.DMA((N_DEV - 1,))
    return pl.pallas_call(
        body, name="allreduce_small",
        in_specs=[pl.BlockSpec(memory_space=pltpu.VMEM)] + [ANY] * nd, out_specs=pl.BlockSpec(memory_space=pltpu.VMEM),
        out_shape=jax.ShapeDtypeStruct((R, LANES), F32),
        scratch_shapes=[pltpu.VMEM((N_DEV, P, LANES), F32), sems, sems, sems, sems],
        compiler_params=_cparams(),
    )(vec, *deps)


def _pack(arrs):
    flat = jnp.concatenate([a.reshape(-1) for a in arrs])
    pad = (-flat.shape[0]) % (8 * N_DEV * LANES)
    return jnp.pad(flat, (0, pad)).reshape(-1, LANES)


def _unpack(packed, shapes):
    flat = packed.reshape(-1)
    out, off = [], 0
    for s in shapes:
        n = math.prod(s)
        out.append(flat[off:off + n].reshape(s))
        off += n
    return out


def kernel(x, w_in, w_conv, w_pool, pool_scale, sgu_ln_g, w_spatial, b_spatial, w_o, ln1_g, ln1_b, w_gate_up, w_down, ln2_g, ln2_b, loss_target, m_w_in, m_w_conv, m_w_pool, m_pool_scale, m_sgu_ln_g, m_w_spatial, m_b_spatial, m_w_o, m_ln1_g, m_ln1_b, m_w_gate_up, m_w_down, m_ln2_g, m_ln2_b, v_w_in, v_w_conv, v_w_pool, v_pool_scale, v_sgu_ln_g, v_w_spatial, v_b_spatial, v_w_o, v_ln1_g, v_ln1_b, v_w_gate_up, v_w_down, v_ln2_g, v_ln2_b):
    L = DEPTH
    T = x.shape[1]
    mx, my, mc = _my_place()
    dev = 4 * mx + 2 * my + mc
    xs = x[0]
    target = loss_target[0]

    conv_cols = w_conv.shape[2]
    w_conv_z = lax.dynamic_update_slice(jnp.zeros((L, 3, CONV_W), F32), w_conv, (0, 0, dev * conv_cols))
    w_conv_packed = _allreduce_small(_pack([w_conv_z]))
    w_conv_full = _unpack(w_conv_packed, [(L, 3, CONV_W)])[0]

    shards = (jnp.swapaxes(w_in, 1, 2).astype(BF16), jnp.swapaxes(w_gate_up, 1, 2).astype(BF16),
              w_o.astype(BF16), w_down.astype(BF16))
    first_gather = _ag_start_layer(shards, 0, [w_conv_packed])

    grad_x2, big_grads, small_grads = _local_step(
        xs, target, shards, first_gather, w_conv_full, w_pool, pool_scale, sgu_ln_g, w_spatial, b_spatial,
        ln1_g, ln1_b, ln2_g, ln2_b)
    grad_x = grad_x2[None]
    big_w = (w_in, w_gate_up, w_o, w_down)
    big_m = (m_w_in, m_w_gate_up, m_w_o, m_w_down)
    big_v = (v_w_in, v_w_gate_up, v_w_o, v_w_down)
    small_w = [w_conv_full, w_pool, pool_scale, sgu_ln_g, w_spatial, b_spatial, ln1_g, ln1_b, ln2_g, ln2_b]
    small_m = [m_w_conv, m_w_pool, m_pool_scale, m_sgu_ln_g, m_w_spatial, m_b_spatial, m_ln1_g, m_ln1_b, m_ln2_g, m_ln2_b]
    small_v = [v_w_conv, v_w_pool, v_pool_scale, v_sgu_ln_g, v_w_spatial, v_b_spatial, v_ln1_g, v_ln1_b, v_ln2_g, v_ln2_b]
    loss, grads, deltas, new_m, new_v = _reduce_and_update(
        big_grads, small_grads, big_w, big_m, big_v, small_w, small_m, small_v)
    return (loss, grad_x, *grads, *deltas, *new_m, *new_v)


def _ag_start_layer(shards, l, after):
    s_in, s_gu, s_o, s_dn = [s[l] for s in shards]
    first = _ag_start([s_in, s_o], "%da" % l, after=after)
    return first, _ag_start([s_gu, s_dn], "%db" % l, after=[first[4]])


def _ag_finish(gather, after, tag):
    send_sems, recv_sems, shards, lands, _ = gather
    shards, lands = _ag_wait(send_sems, recv_sems, shards, lands, after, tag)
    return _ag_pass_on(shards, lands)


def _rs_begin(parts, tag, after=()):
    return _rs_sibling_start([p.reshape(4, 2, p.shape[0] // N_DEV, D_MODEL) for p in parts], tag, after)


def _rs_continue(sibling_flight, after, c_arr, tag):
    send_sems, recv_sems, parts, lands, _ = sibling_flight
    parts, got = _rs_sibling_wait(send_sems, recv_sems, parts, lands, after, tag)
    return _rs_chip_start(_rs_chip_sum(parts, got, c_arr), tag)


def _local_step(xs, target, shards, gather, w_conv_full, w_pool, pool_scale, sgu_ln_g, w_spatial, b_spatial,
                ln1_g, ln1_b, ln2_g, ln2_b):
    L = DEPTH
    T = xs.shape[0]
    mx, my, mc = _my_place()
    c_arr = jnp.reshape(mc, (1,)).astype(jnp.int32)
    q_arr = jnp.reshape(2 * mx + my, (1,)).astype(jnp.int32)
    eye2 = jnp.eye(2, dtype=F32)
    wp = w_pool.reshape(L, 2, 2, HALF, HALF)
    wpool_bd = jnp.einsum("ltgcd,gh->ltgchd", wp, eye2).reshape(L, 2, LANES, LANES)
    wsp_t = w_spatial.reshape(L, 3, 2 * CHUNK, CHUNK)
    bias_t = jnp.repeat(jnp.swapaxes(b_spatial.reshape(L, 3, 2, CHUNK), 2, 3), HALF, axis=3)
    mixer_w = (w_conv_full, wpool_bd, pool_scale[:, None, :], sgu_ln_g[:, None, :], wsp_t, bias_t)
    g1, b1, g2, b2 = [a[:, None, :] for a in (ln1_g, ln1_b, ln2_g, ln2_b)]
    one, zero = jnp.ones((1, 1, D_MODEL), F32), jnp.zeros((1, 1, D_MODEL), F32)

    saved = []
    prev, pg, pb = xs, (one, 0), (zero, 0)
    prev_b = xs.astype(BF16)
    weights = []
    for l in range(L):
        g_in, g_o = _ag_finish(gather[0], [] if l == 0 else [prev_b], "%da" % l)
        proj = _mm(prev_b, g_in, "nt", F32, 512, IN_W, "mm_proj", deps=[gather[1][4]] if l == 0 else [])
        mixcat = _mixer_fwd(proj, *mixer_w, l)
        xhat1, rstd1, h_b = _mm_ln_fwd(mixcat, g_o, prev, pg, pb, (g1, l), (b1, l), "mm_wo_ln")
        g_gu, g_dn = _ag_finish(gather[1], [h_b], "%db" % l)
        weights.append((g_in, g_gu, g_o, g_dn))
        deps = []
        if l + 1 < L:
            gather = _ag_start_layer(shards, l + 1, [g_gu])
            deps = [gather[1][4]]
        g_act, u_act, act = _mm_swiglu_fwd(h_b, g_gu, deps=deps)
        xhat2, rstd2, y_b = _mm_ln_fwd(act, g_dn, xhat1, (g1, l), (b1, l), (g2, l), (b2, l), "mm_down_ln")
        saved.append((prev_b, proj, mixcat, xhat1, rstd1, h_b, g_act, u_act, act, xhat2, rstd2))
        prev, pg, pb, prev_b = xhat2, (g2, l), (b2, l), y_b


    small = [None] * L
    big = None
    sibling_flight = None
    above = None
    for l in reversed(range(L)):
        prev_b, proj, mixcat, xhat1, rstd1, h_b, g_act, u_act, act, xhat2, rstd2 = saved[l]
        g_in, g_gu, g_o, g_dn = weights[l]
        chip_flight = None
        if above is None:
            loss_tile, dr2, dr2_b, dg2, db2 = _loss_ln_bwd(xhat2, rstd2, (g2, l), (b2, l), target)
        else:
            dr2, dr2_b, dg2, db2 = _mm_ln_bwd([above[0]], above[1], above[2], xhat2, rstd2, (g2, l),
                                              "mm_dx_ln", deps=[sibling_flight[4]])
            chip_flight = _rs_continue(sibling_flight, [dr2_b], c_arr, str(l + 1))
        dg_b, du_b = _mm_swiglu_bwd(dr2_b, g_dn, g_act, u_act, deps=[chip_flight[4]] if chip_flight else [])
        p_dn = _mm(act, dr2_b, "tn", BF16, DW_TM, D_MODEL // 2, "mm_dw_down")
        p_gu = _mm_tn_pair(dg_b, du_b, h_b, DW_TM, "mm_dw_gate_up")
        ffn_sibling = _rs_begin([p_gu, p_dn], "0b") if l == 0 else None
        dr1, dr1_b, dg1, db1, dmix = _mm_ln_bwd([dg_b, du_b], g_gu, dr2, xhat1, rstd1, (g1, l), "mm_dh_ln",
                                                deps=[ffn_sibling[4]] if l == 0 else [], w_back=g_o)
        ffn_flight = _rs_continue(ffn_sibling, [dr1_b], c_arr, "0b") if l == 0 else None
        p_o = _mm(mixcat, dr1_b, "tn", BF16, 512, D_MODEL, "mm_dw_o")
        dproj, dwc, dwp, dps, dlng, dwsp, dbias = _mixer_bwd(proj, dmix, *mixer_w, l,
                                                             deps=[ffn_flight[4]] if l == 0 else [])
        p_in = _mm(dproj, prev_b, "tn", BF16, IN_W, D_MODEL // 2, "mm_dw_in")
        small[l] = (dwc, dwp, dps, dlng, dwsp, dbias, dg1, db1, dg2, db2)
        above = (dproj, g_in, dr1)
        if chip_flight is not None:
            big = list(_rs_chip_finish(chip_flight, [p_in], q_arr, str(l + 1), l + 1, big))
        if l > 0:
            sibling_flight = _rs_begin([p_in, p_gu, p_o, p_dn], str(l))
        else:
            big[1], big[3] = _rs_chip_finish(ffn_flight, [p_in, p_o], q_arr, "0b", 0, [big[1], big[3]])

    def stack(i):
        return jnp.stack([small[l][i] for l in range(L)])

    dwp_bd = stack(1).reshape(L, 2, 2, HALF, 2, HALF)
    dwp_all = jnp.einsum("ltgchd,gh->ltgcd", dwp_bd, eye2).reshape(L, 4, HALF, HALF)
    dbs_all = jnp.swapaxes(stack(5)[:, :, :, :2], 2, 3).reshape(L, 6, CHUNK)
    small_grads = [stack(0), dwp_all, stack(2).reshape(L, POOL_W), stack(3).reshape(L, SGU_W),
                   stack(4).reshape(L, 6, CHUNK, CHUNK), dbs_all] + [stack(i).reshape(L, D_MODEL) for i in (6, 7, 8, 9)]
    small_grads.append(loss_tile[0, :1])
    packed_small = _allreduce_small(_pack(small_grads), deps=[big[1]])
    sibling_flight = _rs_begin([p_in, p_o], "0a", after=[packed_small])
    grad_x = _mm_ln_bwd([above[0]], above[1], above[2], None, None, None, "mm_dx_out", deps=[sibling_flight[4]])
    last_flight = _rs_continue(sibling_flight, [grad_x], c_arr, "0a")
    return grad_x, (big, last_flight, q_arr), (packed_small, [a.shape for a in small_grads])


def _rs_chip_finish(in_flight, after, q, tag, layer, into):
    send_sems, recv_sems, sums, lands, _ = in_flight
    sums, got = _rs_chip_wait(send_sems, recv_sems, sums, lands, after, tag)
    return _rs_finish(sums, got, q, layer, into)


def _reduce_and_update(big_grads, small_grads, big_w, big_m, big_v, small_w, small_m, small_v):
    L = DEPTH
    mx, my, mc = _my_place()
    dev = 4 * mx + 2 * my + mc
    conv_cols = CONV_W // N_DEV
    w_in, w_gate_up, w_o, w_down = big_w
    m_w_in, m_w_gate_up, m_w_o, m_w_down = big_m
    v_w_in, v_w_gate_up, v_w_o, v_w_down = big_v
    packed_g, small_shapes = small_grads
    big, last_flight, q_arr = big_grads

    def widen_conv(a):
        return lax.dynamic_update_slice(jnp.zeros((L, 3, CONV_W), F32), a, (0, 0, dev * conv_cols))

    small_m = [widen_conv(small_m[0])] + list(small_m[1:])
    small_v = [widen_conv(small_v[0])] + list(small_v[1:])
    pk_d, pk_m, pk_v = _adamw(_pack(small_w), packed_g, _pack(small_m), _pack(small_v), packed_g.shape[0] // 2)
    sg = _unpack(packed_g, small_shapes)
    sd = _unpack(pk_d, small_shapes)
    sm = _unpack(pk_m, small_shapes)
    sv = _unpack(pk_v, small_shapes)

    def conv_cols_of(a):
        return lax.dynamic_slice(a, (0, 0, dev * conv_cols), (L, 3, conv_cols))

    for lst in (sg, sd, sm, sv):
        lst[0] = conv_cols_of(lst[0])

    tr = lambda a: jnp.swapaxes(a, 1, 2)
    gt_gu, g_w_dn = big[1], big[3]
    d_gu, m_gu, v_gu = [tr(a) for a in _adamw(tr(w_gate_up), gt_gu, tr(m_w_gate_up), tr(v_w_gate_up), gt_gu.shape[1] // 2)]
    d_dn, m_dn, v_dn = _adamw(w_down, g_w_dn, m_w_down, v_w_down, w_down.shape[1])
    gt_in, g_w_o = _rs_chip_finish(last_flight, [d_gu, d_dn, pk_d], q_arr, "0a", 0, [big[0], big[2]])
    d_in, m_in, v_in = [tr(a) for a in _adamw(tr(w_in), gt_in, tr(m_w_in), tr(v_w_in), gt_in.shape[1])]
    d_o, m_o, v_o = _adamw(w_o, g_w_o, m_w_o, v_w_o, w_o.shape[1])
    g_w_in, g_w_gu = tr(gt_in), tr(gt_gu)

    def ordered(big_in, big_o, big_gu, big_dn, sm_list):
        return [big_in, sm_list[0], sm_list[1], sm_list[2], sm_list[3], sm_list[4], sm_list[5], big_o,
                sm_list[6], sm_list[7], big_gu, big_dn, sm_list[8], sm_list[9]]

    grads = ordered(g_w_in, g_w_o, g_w_gu, g_w_dn, sg)
    deltas = ordered(d_in, d_o, d_gu, d_dn, sd)
    new_m = ordered(m_in, m_o, m_gu, m_dn, sm)
    new_v = ordered(v_in, v_o, v_gu, v_dn, sv)
    return sg[10][0], grads, deltas, new_m, new_v
```

```python
import math

import jax
import jax.numpy as jnp
from jax import lax
from jax.experimental import pallas as pl
from jax.experimental.pallas import tpu as pltpu

F32 = jnp.float32
BF16 = jnp.bfloat16
MESH = pl.DeviceIdType.MESH

D_MODEL = 1024
DEPTH = 4
CONV_W = 384
POOL_W = 256
SGU_W = 384
IN_W = 3 * CONV_W + POOL_W + 2 * SGU_W
D_FF = 2816
CHUNK = 128
ALPHA = float((2 * DEPTH) ** 0.25)
LN_EPS = 1e-5
ADAM_LR, ADAM_B1, ADAM_B2, ADAM_EPS, ADAM_WD, ADAM_STEP = 0.001, 0.9, 0.999, 1e-08, 0.01, 10

N_DEV = 8
LANES = 128
HALF = 64
VMEM_LIMIT = 52 * 1024 * 1024

INV_SQRT2 = 0.7071067811865476
INV_SQRT_2PI = 0.3989422804014327


def _cparams(sem=None, **kw):
    if sem is not None:
        kw["dimension_semantics"] = sem
    return pltpu.CompilerParams(vmem_limit_bytes=VMEM_LIMIT, **kw)


_DN = {"nt": (((1,), (1,)), ((), ())), "tn": (((0,), (0,)), ((), ()))}


def _mm(a, b, mode, out_dtype, tm, tn, name, deps=()):
    if mode == "nt":
        (M, K), N = a.shape, b.shape[0]
        a_spec = pl.BlockSpec((tm, K), lambda i, j: (i, 0))
        b_spec = pl.BlockSpec((tn, K), lambda i, j: (j, 0))
    else:
        (K, M), N = a.shape, b.shape[1]
        a_spec = pl.BlockSpec((K, tm), lambda i, j: (0, i))
        b_spec = pl.BlockSpec((K, tn), lambda i, j: (0, j))
    assert M % tm == 0 and N % tn == 0, (M, N, K, tm, tn)
    nd = len(deps)

    def body(*refs):
        a_ref, b_ref, o_ref = refs[0], refs[1], refs[2 + nd]
        o_ref[...] = lax.dot_general(a_ref[...], b_ref[...], _DN[mode], preferred_element_type=F32).astype(o_ref.dtype)

    return pl.pallas_call(
        body,
        name=name,
        grid=(M // tm, N // tn),
        in_specs=[a_spec, b_spec] + [pl.BlockSpec(memory_space=pl.ANY)] * nd,
        out_specs=pl.BlockSpec((tm, tn), lambda i, j: (i, j)),
        out_shape=jax.ShapeDtypeStruct((M, N), out_dtype),
        compiler_params=_cparams(("parallel", "parallel")),
    )(a, b, *deps)


def _mm_tn_pair(a1, a2, b, tm, name):
    K, M = a1.shape
    N = b.shape[1]
    n1 = M // tm

    def body(a1_ref, a2_ref, b_ref, o_ref):
        i = pl.program_id(0)

        @pl.when(i < n1)
        def _():
            o_ref[...] = lax.dot_general(a1_ref[...], b_ref[...], _DN["tn"], preferred_element_type=F32).astype(o_ref.dtype)

        @pl.when(i >= n1)
        def _():
            o_ref[...] = lax.dot_general(a2_ref[...], b_ref[...], _DN["tn"], preferred_element_type=F32).astype(o_ref.dtype)

    return pl.pallas_call(
        body, name=name, grid=(2 * n1,),
        in_specs=[pl.BlockSpec((K, tm), lambda i: (0, jnp.minimum(i, n1 - 1))),
                  pl.BlockSpec((K, tm), lambda i: (0, jnp.maximum(i - n1, 0))),
                  pl.BlockSpec((K, N), lambda i: (0, 0))],
        out_specs=pl.BlockSpec((tm, N), lambda i: (i, 0)),
        out_shape=jax.ShapeDtypeStruct((2 * M, N), BF16),
        compiler_params=_cparams(("arbitrary",)),
    )(a1, a2, b)


LN_SUB = 256
LN_TM = 512


def _vec(v):
    arr, layer = v
    return arr, pl.BlockSpec((None, 1, D_MODEL), lambda *_: (layer, 0, 0))


def _mm_ln_fwd(a, b, prev, pg, pb, g, bias, name):
    T, K = a.shape
    tm = LN_TM

    def body(a_ref, b_ref, prev_ref, pg_ref, pb_ref, g_ref, bias_ref, xhat_ref, rstd_ref, y_ref):
        for s in range(tm // LN_SUB):
            rows = slice(s * LN_SUB, (s + 1) * LN_SUB)
            mm = jnp.dot(a_ref[rows, :], b_ref[...], preferred_element_type=F32)
            r = ALPHA * (prev_ref[rows, :] * pg_ref[...] + pb_ref[...]) + mm
            mu = jnp.mean(r, axis=-1, keepdims=True)
            xc = r - mu
            var = jnp.mean(xc * xc, axis=-1, keepdims=True)
            rstd = lax.rsqrt(var + LN_EPS)
            xhat = xc * rstd
            xhat_ref[rows, :] = xhat
            rstd_ref[rows, :] = rstd
            y_ref[rows, :] = (xhat * g_ref[...] + bias_ref[...]).astype(y_ref.dtype)

    row = pl.BlockSpec((tm, D_MODEL), lambda i: (i, 0))
    vecs = [_vec(v) for v in (pg, pb, g, bias)]
    return pl.pallas_call(
        body, name=name, grid=(T // tm,),
        in_specs=[pl.BlockSpec((tm, K), lambda i: (i, 0)),
                  pl.BlockSpec((K, D_MODEL), lambda i: (0, 0), pipeline_mode=pl.Buffered(1)),
                  row] + [s for _, s in vecs],
        out_specs=[row, pl.BlockSpec((tm, 1), lambda i: (i, 0)), row],
        out_shape=[jax.ShapeDtypeStruct((T, D_MODEL), F32), jax.ShapeDtypeStruct((T, 1), F32),
                   jax.ShapeDtypeStruct((T, D_MODEL), BF16)],
        compiler_params=_cparams(("parallel",)),
    )(a, b, prev, *[a_ for a_, _ in vecs])


def _mm_ln_bwd(a_list, b, dres, xhat, rstd, g, name, deps=(), w_back=None):
    T = a_list[0].shape[0]
    tm = LN_TM
    na, nd = len(a_list), len(deps)
    ks = [a.shape[1] for a in a_list]
    last = xhat is None
    nout = 1 if last else (5 if w_back is not None else 4)

    def body(*refs):
        a_refs, b_ref, dres_ref = refs[:na], refs[na], refs[na + 1]
        if not last:
            xhat_ref, rstd_ref, g_ref = refs[na + 2:na + 5]
            dr_ref, drb_ref, dg_ref, db_ref = refs[len(refs) - nout:len(refs) - nout + 4]

            @pl.when(pl.program_id(0) == 0)
            def _():
                dg_ref[...] = jnp.zeros_like(dg_ref)
                db_ref[...] = jnp.zeros_like(db_ref)

        for s in range(tm // LN_SUB):
            rows = slice(s * LN_SUB, (s + 1) * LN_SUB)
            mm, off = None, 0
            for a_ref, k in zip(a_refs, ks):
                part = jnp.dot(a_ref[rows, :], b_ref[off:off + k, :], preferred_element_type=F32)
                mm = part if mm is None else mm + part
                off += k
            dy = ALPHA * dres_ref[rows, :] + mm
            if last:
                refs[-1][rows, :] = dy
                continue
            xhat_v = xhat_ref[rows, :]
            dg_ref[...] += jnp.sum(dy * xhat_v, axis=0, keepdims=True)
            db_ref[...] += jnp.sum(dy, axis=0, keepdims=True)
            dxh = dy * g_ref[...]
            m1 = jnp.mean(dxh, axis=-1, keepdims=True)
            m2 = jnp.mean(dxh * xhat_v, axis=-1, keepdims=True)
            dr = rstd_ref[rows, :] * (dxh - m1 - xhat_v * m2)
            dr_ref[rows, :] = dr
            dr_b = dr.astype(drb_ref.dtype)
            drb_ref[rows, :] = dr_b
            if w_back is not None:
                refs[-1][rows, :] = lax.dot_general(dr_b, refs[na + 5][...], _DN["nt"], preferred_element_type=F32)

    row = pl.BlockSpec((tm, D_MODEL), lambda i: (i, 0))
    vec = pl.BlockSpec((1, D_MODEL), lambda i: (0, 0))
    in_specs = [pl.BlockSpec((tm, k), lambda i: (i, 0)) for k in ks]
    in_specs += [pl.BlockSpec((sum(ks), D_MODEL), lambda i: (0, 0), pipeline_mode=pl.Buffered(1)), row]
    args = list(a_list) + [b, dres]
    if last:
        out_specs, out_shape = row, jax.ShapeDtypeStruct((T, D_MODEL), F32)
    else:
        g_arr, g_spec = _vec(g)
        in_specs += [row, pl.BlockSpec((tm, 1), lambda i: (i, 0)), g_spec]
        args += [xhat, rstd, g_arr]
        out_specs = [row, row, vec, vec]
        out_shape = [jax.ShapeDtypeStruct((T, D_MODEL), F32), jax.ShapeDtypeStruct((T, D_MODEL), BF16),
                     jax.ShapeDtypeStruct((1, D_MODEL), F32), jax.ShapeDtypeStruct((1, D_MODEL), F32)]
        if w_back is not None:
            in_specs.append(pl.BlockSpec(w_back.shape, lambda i: (0, 0), pipeline_mode=pl.Buffered(1)))
            args.append(w_back)
            out_specs.append(row)
            out_shape.append(jax.ShapeDtypeStruct((T, w_back.shape[0]), F32))
    return pl.pallas_call(
        body, name=name, grid=(T // tm,),
        in_specs=in_specs + [pl.BlockSpec(memory_space=pl.ANY)] * nd,
        out_specs=out_specs, out_shape=out_shape,
        compiler_params=_cparams(("parallel",) if last else ("arbitrary",)),
    )(*args, *deps)


DW_TM = 1408
FF_TN = 256
FF_TM = 2048
SAVED_GU = BF16


def _mm_swiglu_fwd(h, w_gu, deps=()):
    T = h.shape[0]
    tm = min(T, FF_TM)
    nj = D_FF // FF_TN
    nd = len(deps)

    def body(*refs):
        h_ref, wg_ref, wu_ref = refs[:3]
        g_ref, u_ref, act_ref = refs[3 + nd:]
        hv = h_ref[...]
        gv = lax.dot_general(hv, wg_ref[...], _DN["nt"], preferred_element_type=F32)
        uv = lax.dot_general(hv, wu_ref[...], _DN["nt"], preferred_element_type=F32)
        g_ref[...] = gv.astype(g_ref.dtype)
        u_ref[...] = uv.astype(u_ref.dtype)
        act_ref[...] = (gv * jax.nn.sigmoid(gv) * uv).astype(act_ref.dtype)

    tile = pl.BlockSpec((tm, FF_TN), lambda j, i: (i, j))
    return pl.pallas_call(
        body, name="mm_gate_up_swiglu", grid=(nj, T // tm),
        in_specs=[pl.BlockSpec((tm, D_MODEL), lambda j, i: (i, 0)),
                  pl.BlockSpec((FF_TN, D_MODEL), lambda j, i: (j, 0)),
                  pl.BlockSpec((FF_TN, D_MODEL), lambda j, i: (j + nj, 0))] + [pl.BlockSpec(memory_space=pl.ANY)] * nd,
        out_specs=[tile, tile, tile],
        out_shape=[jax.ShapeDtypeStruct((T, D_FF), SAVED_GU), jax.ShapeDtypeStruct((T, D_FF), SAVED_GU),
                   jax.ShapeDtypeStruct((T, D_FF), BF16)],
        compiler_params=_cparams(("parallel", "parallel")),
    )(h, w_gu, w_gu, *deps)


def _mm_swiglu_bwd(dr, w_dn, g, u, deps=()):
    T = dr.shape[0]
    tm = min(T, FF_TM)

    def body(*refs):
        dr_ref, w_ref, g_ref, u_ref = refs[:4]
        dg_ref, du_ref = refs[-2:]
        da = lax.dot_general(dr_ref[...], w_ref[...], _DN["nt"], preferred_element_type=F32)
        gv, uv = g_ref[...].astype(F32), u_ref[...].astype(F32)
        s = jax.nn.sigmoid(gv)
        du_ref[...] = (da * (gv * s)).astype(du_ref.dtype)
        dg_ref[...] = (da * uv * (s * (1.0 + gv * (1.0 - s)))).astype(dg_ref.dtype)

    tile = pl.BlockSpec((tm, FF_TN), lambda j, i: (i, j))
    return pl.pallas_call(
        body, name="mm_dact_swiglu", grid=(D_FF // FF_TN, T // tm),
        in_specs=[pl.BlockSpec((tm, D_MODEL), lambda j, i: (i, 0)), pl.BlockSpec((FF_TN, D_MODEL), lambda j, i: (j, 0)),
                  tile, tile] + [ANY] * len(deps),
        out_specs=[tile, tile],
        out_shape=[jax.ShapeDtypeStruct((T, D_FF), BF16)] * 2,
        compiler_params=_cparams(("parallel", "parallel")),
    )(dr, w_dn, g, u, *deps)


def _gelu(x):
    return 0.5 * x * (1.0 + lax.erf(x * INV_SQRT2))


def _gelu_grad(x):
    return 0.5 * (1.0 + lax.erf(x * INV_SQRT2)) + x * (jnp.exp(-0.5 * x * x) * INV_SQRT_2PI)


def _shift_down(z, k):
    row = lax.broadcasted_iota(jnp.int32, z.shape, 0)
    return jnp.where(row >= k, pltpu.roll(z, k, 0), 0.0)


def _shift_up(z, k):
    n = z.shape[0]
    row = lax.broadcasted_iota(jnp.int32, z.shape, 0)
    return jnp.where(row < n - k, pltpu.roll(z, n - k, 0), 0.0)


def _lo_mask(shape):
    return lax.broadcasted_iota(jnp.int32, shape, len(shape) - 1) < HALF


def _seg_mean(x, lo):
    a = jnp.sum(jnp.where(lo, x, 0.0), axis=-1, keepdims=True)
    b = jnp.sum(jnp.where(lo, 0.0, x), axis=-1, keepdims=True)
    return jnp.where(lo, a, b) * (1.0 / HALF)


def _pool_windows(first):
    lo = _lo_mask((1, LANES))
    return jnp.where(first, jnp.where(lo, 2.0, 4.0), jnp.where(lo, 8.0, 16.0)), lo


def _pool_mean_minus_token(p, first):
    wl, lo = _pool_windows(first)
    s2 = p + _shift_down(p, 1)
    s4 = s2 + _shift_down(s2, 2)
    s8 = s4 + _shift_down(s4, 4)
    s16 = s8 + _shift_down(s8, 8)
    win = jnp.where(first, jnp.where(lo, s2, s4), jnp.where(lo, s8, s16))
    t1 = (lax.broadcasted_iota(jnp.int32, p.shape, 0) + 1).astype(F32)
    count = jnp.minimum(t1, wl)
    return win / count - p, count


SGU_UNROLL = 2


def _tril_keep():
    r = lax.broadcasted_iota(jnp.int32, (2 * CHUNK, CHUNK), 0)
    s = lax.broadcasted_iota(jnp.int32, (2 * CHUNK, CHUNK), 1)
    return s <= (r & (CHUNK - 1))


def _sgu_chunk_fwd(u, v, g, wm, bias, lo):
    ug = _gelu(u)
    vg = _gelu(v)
    mu = _seg_mean(vg, lo)
    xc = vg - mu
    var = _seg_mean(xc * xc, lo)
    rstd = lax.rsqrt(var + LN_EPS)
    vn = xc * rstd
    vh = (vn * g).astype(BF16)
    mm2 = jnp.dot(wm, vh, preferred_element_type=F32)
    mixed = jnp.where(lo, mm2[:CHUNK], mm2[CHUNK:]) + bias
    return ug, vn, rstd, vh, mixed


def _mixer_fwd(proj, wconv, wpool_bd, pscale, lng, wsp, bias, layer):
    T = proj.shape[0]
    nchunk = T // CHUNK

    def body(a_ref, b_ref, c_ref, wc_ref, wp_ref, ps_ref, lng_ref, wsp_ref, bias_ref, o_ref):
        j = pl.program_id(0)

        @pl.when(j < 3)
        def _conv():
            z = c_ref[...] * a_ref[...]
            w = wc_ref[...]
            y = w[0:1] * _shift_down(z, 2) + w[1:2] * _shift_down(z, 1) + w[2:3] * z
            o_ref[...] = (b_ref[...] * y).astype(o_ref.dtype)

        @pl.when((j >= 3) & (j < 5))
        def _pool():
            d, _ = _pool_mean_minus_token(a_ref[...], j == 3)
            y = jnp.dot(d.astype(BF16), wp_ref[...].astype(BF16), preferred_element_type=F32)
            o_ref[...] = (y * ps_ref[...]).astype(o_ref.dtype)

        @pl.when(j >= 5)
        def _sgu():
            lo = _lo_mask((CHUNK, LANES))
            wm = jnp.where(_tril_keep(), wsp_ref[...], 0.0).astype(BF16)
            bias_t = bias_ref[...]
            g = lng_ref[...]

            def chunk(n, carry):
                rows = pl.ds(pl.multiple_of(n * CHUNK, CHUNK), CHUNK)
                ug, _, _, _, mixed = _sgu_chunk_fwd(a_ref[rows, :], b_ref[rows, :], g, wm, bias_t, lo)
                o_ref[rows, :] = (ug * mixed).astype(o_ref.dtype)
                return carry

            lax.fori_loop(0, nchunk, chunk, 0, unroll=SGU_UNROLL)

    def col(f):
        return lambda j: (0, f(j))

    clip = lambda v, lo, hi: jnp.minimum(jnp.maximum(v, lo), hi)
    return pl.pallas_call(
        body,
        name="mixer_fwd",
        grid=(8,),
        in_specs=[
            pl.BlockSpec((T, LANES), col(lambda j: jnp.where(j < 3, j, jnp.where(j < 5, j + 6, j + 6)))),
            pl.BlockSpec((T, LANES), col(lambda j: jnp.where(j < 3, j + 3, jnp.where(j < 5, 5, j + 9)))),
            pl.BlockSpec((T, LANES), col(lambda j: jnp.where(j < 3, j + 6, 8))),
            pl.BlockSpec((None, 3, LANES), lambda j: (layer, 0, clip(j, 0, 2))),
            pl.BlockSpec((None, None, LANES, LANES), lambda j: (layer, clip(j - 3, 0, 1), 0, 0)),
            pl.BlockSpec((None, 1, LANES), lambda j: (layer, 0, clip(j - 3, 0, 1))),
            pl.BlockSpec((None, 1, LANES), lambda j: (layer, 0, clip(j - 5, 0, 2))),
            pl.BlockSpec((None, None, 2 * CHUNK, CHUNK), lambda j: (layer, clip(j - 5, 0, 2), 0, 0)),
            pl.BlockSpec((None, None, CHUNK, LANES), lambda j: (layer, clip(j - 5, 0, 2), 0, 0)),
        ],
        out_specs=pl.BlockSpec((T, LANES), lambda j: (0, j)),
        out_shape=jax.ShapeDtypeStruct((T, D_MODEL), BF16),
        compiler_params=_cparams(("arbitrary",)),
    )(proj, proj, proj, wconv, wpool_bd, pscale, lng, wsp, bias)


def _mixer_bwd(proj, dmix, wconv, wpool_bd, pscale, lng, wsp, bias, layer, deps=()):
    T = proj.shape[0]
    nchunk = T // CHUNK

    def body(*refs):
        a_ref, b_ref, c_ref, dm_ref, wc_ref, wp_ref, ps_ref, lng_ref, wsp_ref, bias_ref = refs[:10]
        o_ref, dwc_ref, dwp_ref, dps_ref, dlng_ref, dwsp_ref, dbias_ref, keep1, keep2 = refs[10 + len(deps):]
        k = pl.program_id(0)

        @pl.when(k < 3)
        def _conv():
            xa, gb, gc, dya = a_ref[...], b_ref[...], c_ref[...], dm_ref[...]
            w = wc_ref[...]
            z = gc * xa
            z1 = _shift_down(z, 1)
            z2 = _shift_down(z, 2)
            y = w[0:1] * z2 + w[1:2] * z1 + w[2:3] * z
            dyv = dya * gb
            dz = w[2:3] * dyv + w[1:2] * _shift_up(dyv, 1) + w[0:1] * _shift_up(dyv, 2)
            dwc_ref[0:1, :] = jnp.sum(dyv * z2, axis=0, keepdims=True)
            dwc_ref[1:2, :] = jnp.sum(dyv * z1, axis=0, keepdims=True)
            dwc_ref[2:3, :] = jnp.sum(dyv * z, axis=0, keepdims=True)
            o_ref[...] = (dz * gc).astype(o_ref.dtype)
            keep1[k] = (dya * y).astype(keep1.dtype)
            keep1[k + 3] = (dz * xa).astype(keep1.dtype)

        @pl.when((k >= 3) & (k < 9))
        def _emit_gb_gc():
            o_ref[...] = keep1[k - 3]

        @pl.when((k >= 9) & (k < 11))
        def _pool():
            first = k == 9
            p, dyb = a_ref[...], dm_ref[...]
            d, count = _pool_mean_minus_token(p, first)
            w2 = wp_ref[...].astype(BF16)
            db = d.astype(BF16)
            y = jnp.dot(db, w2, preferred_element_type=F32)
            dps_ref[...] = jnp.sum(dyb * y, axis=0, keepdims=True)
            dyv = (dyb * ps_ref[...]).astype(BF16)
            dd = lax.dot_general(dyv, w2, _DN["nt"], preferred_element_type=F32)
            dwp_ref[...] = lax.dot_general(db, dyv, _DN["tn"], preferred_element_type=F32)
            dwin = dd / count
            a2 = dwin + _shift_up(dwin, 1)
            a4 = a2 + _shift_up(a2, 2)
            a8 = a4 + _shift_up(a4, 4)
            a16 = a8 + _shift_up(a8, 8)
            _, lo = _pool_windows(first)
            back = jnp.where(first, jnp.where(lo, a2, a4), jnp.where(lo, a8, a16))
            o_ref[...] = (back - dd).astype(o_ref.dtype)

        @pl.when((k >= 11) & (k < 14))
        def _sgu():
            lo = _lo_mask((CHUNK, LANES))
            keep = _tril_keep()
            wm = jnp.where(keep, wsp_ref[...], 0.0).astype(BF16)
            bias_t = bias_ref[...]
            g = lng_ref[...]
            dwsp_ref[...] = jnp.zeros_like(dwsp_ref)
            dbias_ref[...] = jnp.zeros_like(dbias_ref)
            dlng_ref[...] = jnp.zeros_like(dlng_ref)

            def chunk(n, carry):
                rows = pl.ds(pl.multiple_of(n * CHUNK, CHUNK), CHUNK)
                u, v, dyc = a_ref[rows, :], b_ref[rows, :], dm_ref[rows, :]
                ug, vn, rstd, vh, mixed = _sgu_chunk_fwd(u, v, g, wm, bias_t, lo)
                dmx = dyc * ug
                o_ref[rows, :] = (dyc * mixed * _gelu_grad(u)).astype(o_ref.dtype)
                dbias_ref[...] += dmx
                dst = jnp.concatenate([jnp.where(lo, dmx, 0.0), jnp.where(lo, 0.0, dmx)], axis=0).astype(BF16)
                dwsp_ref[...] += lax.dot_general(dst, vh, _DN["nt"], preferred_element_type=F32)
                dvh = lax.dot_general(wm, dst, _DN["tn"], preferred_element_type=F32)
                dlng_ref[...] += jnp.sum(dvh * vn, axis=0, keepdims=True)
                dvn = dvh * g
                m1 = _seg_mean(dvn, lo)
                m2 = _seg_mean(dvn * vn, lo)
                dvg = rstd * (dvn - m1 - vn * m2)
                keep2[k - 11, rows, :] = (dvg * _gelu_grad(v)).astype(keep2.dtype)
                return carry

            lax.fori_loop(0, nchunk, chunk, 0, unroll=SGU_UNROLL)
            dwsp_ref[...] = jnp.where(keep, dwsp_ref[...], 0.0)
            dbt = dbias_ref[...]
            lane = lax.broadcasted_iota(jnp.int32, (CHUNK, LANES), 1)
            sa = jnp.sum(jnp.where(lo, dbt, 0.0), axis=-1, keepdims=True)
            sb = jnp.sum(jnp.where(lo, 0.0, dbt), axis=-1, keepdims=True)
            dbias_ref[...] = jnp.where(lane == 0, sa, jnp.where(lane == 1, sb, 0.0))

        @pl.when(k >= 14)
        def _emit_v():
            o_ref[...] = keep2[k - 14]

    def col(f):
        return lambda k: (0, f(k))

    clip = lambda v, lo, hi: jnp.minimum(jnp.maximum(v, lo), hi)
    view_a = lambda k: jnp.where(k < 3, k, jnp.where(k < 9, 2, jnp.where(k < 14, k, 13)))
    view_b = lambda k: jnp.where(k < 3, k + 3, jnp.where(k < 11, 5, jnp.where(k < 14, k + 3, 16)))
    view_c = lambda k: jnp.where(k < 3, k + 6, 8)
    view_dm = lambda k: jnp.where(k < 3, k, jnp.where(k < 9, 2, jnp.where(k < 14, k - 6, 7)))
    return pl.pallas_call(
        body,
        name="mixer_bwd",
        grid=(17,),
        in_specs=[
            pl.BlockSpec((T, LANES), col(view_a)),
            pl.BlockSpec((T, LANES), col(view_b)),
            pl.BlockSpec((T, LANES), col(view_c)),
            pl.BlockSpec((T, LANES), col(view_dm)),
            pl.BlockSpec((None, 3, LANES), lambda k: (layer, 0, clip(k, 0, 2))),
            pl.BlockSpec((None, None, LANES, LANES), lambda k: (layer, clip(k - 9, 0, 1), 0, 0)),
            pl.BlockSpec((None, 1, LANES), lambda k: (layer, 0, clip(k - 9, 0, 1))),
            pl.BlockSpec((None, 1, LANES), lambda k: (layer, 0, clip(k - 11, 0, 2))),
            pl.BlockSpec((None, None, 2 * CHUNK, CHUNK), lambda k: (layer, clip(k - 11, 0, 2), 0, 0)),
            pl.BlockSpec((None, None, CHUNK, LANES), lambda k: (layer, clip(k - 11, 0, 2), 0, 0)),
        ] + [pl.BlockSpec(memory_space=pl.ANY)] * len(deps),
        out_specs=[
            pl.BlockSpec((T, LANES), lambda k: (0, k)),
            pl.BlockSpec((3, LANES), col(lambda k: clip(k, 0, 2))),
            pl.BlockSpec((None, LANES, LANES), lambda k: (clip(k - 9, 0, 1), 0, 0)),
            pl.BlockSpec((1, LANES), col(lambda k: clip(k - 9, 0, 1))),
            pl.BlockSpec((1, LANES), col(lambda k: clip(k - 11, 0, 2))),
            pl.BlockSpec((None, 2 * CHUNK, CHUNK), lambda k: (clip(k - 11, 0, 2), 0, 0)),
            pl.BlockSpec((None, CHUNK, LANES), lambda k: (clip(k - 11, 0, 2), 0, 0)),
        ],
        out_shape=[
            jax.ShapeDtypeStruct((T, IN_W), BF16),
            jax.ShapeDtypeStruct((3, CONV_W), F32),
            jax.ShapeDtypeStruct((2, LANES, LANES), F32),
            jax.ShapeDtypeStruct((1, POOL_W), F32),
            jax.ShapeDtypeStruct((1, SGU_W), F32),
            jax.ShapeDtypeStruct((3, 2 * CHUNK, CHUNK), F32),
            jax.ShapeDtypeStruct((3, CHUNK, LANES), F32),
        ],
        scratch_shapes=[pltpu.VMEM((6, T, LANES), BF16), pltpu.VMEM((3, T, LANES), BF16)],
        compiler_params=_cparams(("arbitrary",)),
    )(proj, proj, proj, dmix, wconv, wpool_bd, pscale, lng, wsp, bias, *deps)


def _loss_ln_bwd(xhat, rstd, g, b, target, tm=256):
    T = xhat.shape[0]

    def body(xhat_ref, rstd_ref, g_ref, b_ref, t_ref, loss_ref, dr_ref, drb_ref, dg_ref, db_ref):
        xhat_v = xhat_ref[...]
        err = xhat_v * g_ref[...] + b_ref[...] - t_ref[...]
        dy = err * (1.0 / D_MODEL)

        @pl.when(pl.program_id(0) == 0)
        def _():
            loss_ref[...] = jnp.zeros_like(loss_ref)
            dg_ref[...] = jnp.zeros_like(dg_ref)
            db_ref[...] = jnp.zeros_like(db_ref)

        part = jnp.sum(jnp.sum(err * err, axis=-1, keepdims=True), axis=0, keepdims=True)
        loss_ref[...] += jnp.broadcast_to(part * (0.5 / D_MODEL), loss_ref.shape)
        dg_ref[...] += jnp.sum(dy * xhat_v, axis=0, keepdims=True)
        db_ref[...] += jnp.sum(dy, axis=0, keepdims=True)
        dxh = dy * g_ref[...]
        m1 = jnp.mean(dxh, axis=-1, keepdims=True)
        m2 = jnp.mean(dxh * xhat_v, axis=-1, keepdims=True)
        dr = rstd_ref[...] * (dxh - m1 - xhat_v * m2)
        dr_ref[...] = dr
        drb_ref[...] = dr.astype(drb_ref.dtype)

    row = pl.BlockSpec((tm, D_MODEL), lambda i: (i, 0))
    vec = pl.BlockSpec((1, D_MODEL), lambda i: (0, 0))
    (g_arr, g_spec), (b_arr, b_spec) = _vec(g), _vec(b)
    return pl.pallas_call(
        body,
        name="loss_ln_bwd",
        grid=(T // tm,),
        in_specs=[row, pl.BlockSpec((tm, 1), lambda i: (i, 0)), g_spec, b_spec, row],
        out_specs=[pl.BlockSpec((8, LANES), lambda i: (0, 0)), row, row, vec, vec],
        out_shape=[jax.ShapeDtypeStruct((8, LANES), F32),
                   jax.ShapeDtypeStruct((T, D_MODEL), F32), jax.ShapeDtypeStruct((T, D_MODEL), BF16),
                   jax.ShapeDtypeStruct((1, D_MODEL), F32), jax.ShapeDtypeStruct((1, D_MODEL), F32)],
        compiler_params=_cparams(("arbitrary",)),
    )(xhat, rstd, g_arr, b_arr, target)


def _adamw(w, g, m, v, tr):
    R, C = w.shape[-2:]
    assert R % tr == 0
    c1 = 1.0 - ADAM_B1 ** ADAM_STEP
    c2 = 1.0 - ADAM_B2 ** ADAM_STEP

    def body(w_ref, g_ref, m_ref, v_ref, d_ref, mo_ref, vo_ref):
        gv = g_ref[...]
        mn = ADAM_B1 * m_ref[...] + (1.0 - ADAM_B1) * gv
        vn = ADAM_B2 * v_ref[...] + (1.0 - ADAM_B2) * (gv * gv)
        d_ref[...] = -ADAM_LR * ((mn / c1) / (jnp.sqrt(vn / c2) + ADAM_EPS) + ADAM_WD * w_ref[...])
        mo_ref[...] = mn
        vo_ref[...] = vn

    if w.ndim == 2:
        grid, blk = (R // tr,), pl.BlockSpec((tr, C), lambda i: (i, 0))
    else:
        grid, blk = (w.shape[0], R // tr), pl.BlockSpec((None, tr, C), lambda l, i: (l, i, 0))
    return pl.pallas_call(
        body, name="adamw", grid=grid, in_specs=[blk] * 4, out_specs=[blk] * 3,
        out_shape=[jax.ShapeDtypeStruct(w.shape, F32)] * 3, compiler_params=_cparams(("parallel",) * len(grid)),
    )(w, g, m, v)


def _my_place():
    return lax.axis_index("x"), lax.axis_index("y"), lax.axis_index("c")


ANY = pl.BlockSpec(memory_space=pl.ANY)
HBM = pl.BlockSpec(memory_space=pltpu.HBM)
SEM = pl.BlockSpec(memory_space=pltpu.SEMAPHORE)
EFFECT = pltpu.SideEffectType.DATAFLOW_SIDE_EFFECTING


def _in_hbm(a):
    return pltpu.with_memory_space_constraint(a, pltpu.HBM)


def _block_rows(ref, dev):
    r = ref.shape[0] // N_DEV
    start = pl.multiple_of((4 * dev[0] + 2 * dev[1] + dev[2]) * r, 16)
    return ref.at[pl.ds(start, r), :]


def _ag_first_copies(s_refs, land_refs, send_sems, recv_sems, receiving):
    x, y, c = _my_place()
    peers = [(x, y, 1 - c)] + [(*chip, c) for chip in _other_chips(x, y)]
    copies = []
    for k, peer in enumerate(peers):
        block = peer if receiving else (x, y, c)
        copies += [pltpu.make_async_remote_copy(
            src_ref=s_refs[w], dst_ref=_block_rows(land_refs[w], block),
            send_sem=send_sems.at[k * len(s_refs) + w], recv_sem=recv_sems.at[k * len(s_refs) + w],
            device_id=peer, device_id_type=MESH)
            for w in range(len(s_refs))]
    return copies


def _ag_start(shards, layer, after=()):
    nw = len(shards)

    def body(*refs):
        s_refs, land_refs = refs[:nw], refs[nw:2 * nw]
        token = refs[-1]
        sems = 2 * nw + len(after)
        for cp in _ag_first_copies(s_refs, land_refs, refs[sems], refs[sems + 1], False):
            cp.start()
        token[...] = jnp.zeros_like(token)

    lands = [lax.empty((N_DEV * s.shape[0], D_MODEL), BF16) for s in shards]
    out = pl.pallas_call(
        body, name="ag_start_%s" % layer,
        in_specs=[HBM] * (2 * nw) + [ANY] * len(after),
        out_specs=(SEM, SEM, *[HBM] * (2 * nw), pl.BlockSpec(memory_space=pltpu.VMEM)),
        out_shape=(pltpu.SemaphoreType.DMA((4 * nw,)), pltpu.SemaphoreType.DMA((4 * nw,)),
                   *[pltpu.HBM(a.shape, a.dtype) for a in list(shards) + lands],
                   jax.ShapeDtypeStruct((8, LANES), F32)),
        input_output_aliases={i: 2 + i for i in range(2 * nw)},
        compiler_params=pltpu.CompilerParams(has_side_effects=EFFECT),
    )(*[_in_hbm(a) for a in list(shards) + lands], *after)
    return out[0], out[1], out[2:2 + nw], out[2 + nw:2 + 2 * nw], out[-1]


def _ag_wait(send_sems, recv_sems, shards, lands, after, layer):
    nw = len(shards)

    def body(*refs):
        s_refs, land_refs = refs[:nw], refs[nw:2 * nw]
        for cp in _ag_first_copies(s_refs, land_refs, refs[2 * nw], refs[2 * nw + 1], True):
            cp.wait_send()
            cp.wait_recv()

    out = pl.pallas_call(
        body, name="ag_wait_%s" % layer,
        in_specs=[HBM] * (2 * nw) + [SEM, SEM] + [ANY] * len(after),
        out_specs=[HBM] * (2 * nw),
        out_shape=[pltpu.HBM(a.shape, a.dtype) for a in list(shards) + list(lands)],
        input_output_aliases={i: i for i in range(2 * nw)},
        compiler_params=pltpu.CompilerParams(has_side_effects=EFFECT),
    )(*shards, *lands, send_sems, recv_sems, *after)
    return out[:nw], out[nw:]


def _ag_pass_on(shards, lands):
    nw = len(shards)

    def body(*refs):
        s_refs, g_refs = refs[:nw], refs[2 * nw:3 * nw]
        send_sems, recv_sems, local_sems = refs[3 * nw:3 * nw + 3]
        stage = refs[3 * nw + 3:]
        x, y, c = _my_place()
        load = [pltpu.make_async_copy(s_refs[w], stage[w], local_sems.at[w]) for w in range(nw)]
        mine = [pltpu.make_async_copy(stage[w], _block_rows(g_refs[w], (x, y, c)), local_sems.at[w])
                for w in range(nw)]
        for cp in load:
            cp.start()
        sends, arrivals = [], []
        for j, chip in enumerate(_other_chips(x, y)):
            for w in range(nw):
                rows_out = _block_rows(g_refs[w], (*chip, c))
                rows_in = _block_rows(g_refs[w], (*chip, 1 - c))
                sends.append(pltpu.make_async_remote_copy(
                    src_ref=rows_out, dst_ref=rows_out, send_sem=send_sems.at[j, w], recv_sem=recv_sems.at[j, w],
                    device_id=(x, y, 1 - c), device_id_type=MESH))
                arrivals.append(pltpu.make_async_remote_copy(
                    src_ref=rows_in, dst_ref=rows_in, send_sem=send_sems.at[j, w], recv_sem=recv_sems.at[j, w],
                    device_id=(x, y, 1 - c), device_id_type=MESH))
        for cp in sends:
            cp.start()
        for w in range(nw):
            load[w].wait()
            mine[w].start()
        for cp in arrivals:
            cp.wait_recv()
        for cp in sends:
            cp.wait_send()
        for cp in mine:
            cp.wait()

    return pl.pallas_call(
        body, name="ag_pass_on",
        in_specs=[ANY] * (2 * nw), out_specs=[ANY] * nw,
        out_shape=[jax.ShapeDtypeStruct(a.shape, a.dtype) for a in lands],
        input_output_aliases={nw + i: i for i in range(nw)},
        scratch_shapes=[pltpu.SemaphoreType.DMA((3, nw)), pltpu.SemaphoreType.DMA((3, nw)),
                        pltpu.SemaphoreType.DMA((nw,))] + [pltpu.VMEM(s.shape, s.dtype) for s in shards],
        compiler_params=_cparams(),
    )(*shards, *lands)


def _rs_sibling_copies(p_refs, land_refs, send_sems, recv_sems):
    x, y, c = _my_place()
    return [pltpu.make_async_remote_copy(
        src_ref=p_refs[w].at[:, 1 - c], dst_ref=land_refs[w],
        send_sem=send_sems.at[w], recv_sem=recv_sems.at[w], device_id=(x, y, 1 - c), device_id_type=MESH)
        for w in range(len(p_refs))]


def _rs_sibling_start(parts, tag, after=()):
    nw = len(parts)
    sems = 2 * nw + len(after)

    def body(*refs):
        for cp in _rs_sibling_copies(refs[:nw], refs[nw:2 * nw], refs[sems], refs[sems + 1]):
            cp.start()
        refs[-1][...] = jnp.zeros_like(refs[-1])

    lands = [lax.empty(p.shape[:1] + p.shape[2:], BF16) for p in parts]
    out = pl.pallas_call(
        body, name="rs_sibling_start_%s" % tag,
        in_specs=[HBM] * (2 * nw) + [ANY] * len(after),
        out_specs=(SEM, SEM, *[HBM] * (2 * nw), pl.BlockSpec(memory_space=pltpu.VMEM)),
        out_shape=(pltpu.SemaphoreType.DMA((nw,)), pltpu.SemaphoreType.DMA((nw,)),
                   *[pltpu.HBM(a.shape, a.dtype) for a in list(parts) + lands],
                   jax.ShapeDtypeStruct((8, LANES), F32)),
        input_output_aliases={i: 2 + i for i in range(2 * nw)},
        compiler_params=pltpu.CompilerParams(has_side_effects=EFFECT),
    )(*[_in_hbm(a) for a in list(parts) + lands], *after)
    return out[0], out[1], out[2:2 + nw], out[2 + nw:2 + 2 * nw], out[-1]


def _rs_sibling_wait(send_sems, recv_sems, parts, lands, after, tag):
    nw = len(parts)

    def body(*refs):
        for cp in _rs_sibling_copies(refs[:nw], refs[nw:2 * nw], refs[2 * nw], refs[2 * nw + 1]):
            cp.wait_send()
            cp.wait_recv()

    out = pl.pallas_call(
        body, name="rs_sibling_wait_%s" % tag,
        in_specs=[HBM] * (2 * nw) + [SEM, SEM] + [ANY] * len(after),
        out_specs=[HBM] * (2 * nw),
        out_shape=[pltpu.HBM(a.shape, a.dtype) for a in list(parts) + list(lands)],
        input_output_aliases={i: i for i in range(2 * nw)},
        compiler_params=pltpu.CompilerParams(has_side_effects=EFFECT),
    )(*parts, *lands, send_sems, recv_sems, *after)
    return out[:nw], out[nw:]


def _rs_chip_sum(parts, gots, c):
    n = len(parts)

    def body(c_ref, *refs):
        for p_ref, g_ref, o_ref in zip(refs[:n], refs[n:2 * n], refs[2 * n:]):
            o_ref[...] = (p_ref[...].astype(F32) + g_ref[...].astype(F32)).astype(o_ref.dtype)

    mine = [pl.BlockSpec((None, None, p.shape[2], D_MODEL), lambda q, c_ref: (q, c_ref[0], 0, 0)) for p in parts]
    theirs = [pl.BlockSpec((None, g.shape[1], D_MODEL), lambda q, c_ref: (q, 0, 0)) for g in gots]
    return pl.pallas_call(
        body, name="rs_chip_sum",
        grid_spec=pltpu.PrefetchScalarGridSpec(
            num_scalar_prefetch=1, grid=(4,), in_specs=mine + theirs, out_specs=theirs),
        out_shape=[jax.ShapeDtypeStruct(g.shape, BF16) for g in gots],
        compiler_params=_cparams(("parallel",)),
    )(c, *parts, *gots)


def _other_chips(x, y):
    return [(1 - x, y), (x, 1 - y), (1 - x, 1 - y)]


def _rs_chip_copies(s_refs, land_refs, send_sems, recv_sems):
    x, y, c = _my_place()
    copies = []
    for k, chip in enumerate(_other_chips(x, y)):
        q = 2 * chip[0] + chip[1]
        copies += [pltpu.make_async_remote_copy(
            src_ref=s_refs[w].at[q], dst_ref=land_refs[w].at[k],
            send_sem=send_sems.at[k * len(s_refs) + w], recv_sem=recv_sems.at[k * len(s_refs) + w],
            device_id=(*chip, c), device_id_type=MESH)
            for w in range(len(s_refs))]
    return copies


def _rs_chip_start(sums, layer):
    nw = len(sums)

    def body(*refs):
        s_refs, land_refs = refs[:nw], refs[nw:2 * nw]
        send_sems, recv_sems = refs[2 * nw], refs[2 * nw + 1]
        token = refs[-1]
        for cp in _rs_chip_copies(s_refs, land_refs, send_sems, recv_sems):
            cp.start()
        token[...] = jnp.zeros_like(token)

    lands = [lax.empty((3,) + s.shape[1:], BF16) for s in sums]
    out = pl.pallas_call(
        body, name="rs_chip_start_%s" % layer,
        in_specs=[HBM] * (2 * nw),
        out_specs=(SEM, SEM, *[HBM] * (2 * nw), pl.BlockSpec(memory_space=pltpu.VMEM)),
        out_shape=(pltpu.SemaphoreType.DMA((3 * nw,)), pltpu.SemaphoreType.DMA((3 * nw,)),
                   *[pltpu.HBM(a.shape, a.dtype) for a in list(sums) + lands],
                   jax.ShapeDtypeStruct((8, LANES), F32)),
        input_output_aliases={i: 2 + i for i in range(2 * nw)},
        compiler_params=pltpu.CompilerParams(has_side_effects=EFFECT),
    )(*[_in_hbm(a) for a in list(sums) + lands])
    return out[0], out[1], out[2:2 + nw], out[2 + nw:2 + 2 * nw], out[-1]


def _rs_chip_wait(send_sems, recv_sems, sums, lands, after, layer):
    nw = len(sums)

    def body(*refs):
        s_refs, land_refs = refs[:nw], refs[nw:2 * nw]
        for cp in _rs_chip_copies(s_refs, land_refs, refs[2 * nw], refs[2 * nw + 1]):
            cp.wait_send()
            cp.wait_recv()

    out = pl.pallas_call(
        body, name="rs_chip_wait_%s" % layer,
        in_specs=[HBM] * (2 * nw) + [SEM, SEM] + [ANY] * len(after),
        out_specs=[HBM] * (2 * nw),
        out_shape=[pltpu.HBM(a.shape, a.dtype) for a in list(sums) + list(lands)],
        input_output_aliases={i: i for i in range(2 * nw)},
        compiler_params=pltpu.CompilerParams(has_side_effects=EFFECT),
    )(*sums, *lands, send_sems, recv_sems, *after)
    return out[:nw], out[nw:]


def _rs_finish(sums, gots, q, layer, into):
    n = len(sums)

    def body(q_ref, *refs):
        for s_ref, g_ref, o_ref in zip(refs[:n], refs[n:2 * n], refs[len(refs) - n:]):
            o_ref[...] = ((s_ref[...].astype(F32) + g_ref[0].astype(F32)) + g_ref[1].astype(F32)) + g_ref[2].astype(F32)

    rows = [s.shape[1] for s in sums]
    in_specs = [pl.BlockSpec((None, r, D_MODEL), lambda i, q_ref: (q_ref[0], 0, 0)) for r in rows]
    in_specs += [pl.BlockSpec((3, r, D_MODEL), lambda i, q_ref: (0, 0, 0)) for r in rows]
    args = [q, *sums, *gots]
    aliases = {}
    if into is not None:
        in_specs += [ANY] * n
        aliases = {len(args) + i: i for i in range(n)}
        args += list(into)
    return pl.pallas_call(
        body, name="rs_finish",
        grid_spec=pltpu.PrefetchScalarGridSpec(
            num_scalar_prefetch=1, grid=(1,), in_specs=in_specs,
            out_specs=[pl.BlockSpec((None, r, D_MODEL), lambda i, q_ref: (layer, 0, 0)) for r in rows]),
        out_shape=[jax.ShapeDtypeStruct((DEPTH, r, D_MODEL), F32) for r in rows],
        input_output_aliases=aliases,
        compiler_params=_cparams(("arbitrary",)),
    )(*args)


def _allreduce_small(vec, deps=()):
    R = vec.shape[0]
    assert R % (8 * N_DEV) == 0
    P = R // N_DEV
    nd = len(deps)

    def body(*refs):
        v_ref = refs[0]
        o_ref, buf, send1, recv1, send2, recv2 = refs[1 + nd:]
        x, y, c = _my_place()
        me = 4 * x + 2 * y + c

        def piece(ref, d):
            return ref.at[pl.ds(pl.multiple_of(d * P, 8), P), :]

        def peer(k):
            p = me ^ k
            return p, (p >> 2, (p >> 1) & 1, p & 1)

        scatter = []
        for k in range(1, N_DEV):
            p, where = peer(k)
            scatter.append(pltpu.make_async_remote_copy(
                src_ref=piece(v_ref, p), dst_ref=buf.at[k], send_sem=send1.at[k - 1], recv_sem=recv1.at[k - 1],
                device_id=where, device_id_type=MESH))
        for cp in scatter:
            cp.start()
        buf[0] = piece(v_ref, me)[...]
        for cp in scatter:
            cp.wait()
        acc = buf[me]
        for d in range(1, N_DEV):
            acc = acc + buf[me ^ d]
        piece(o_ref, me)[...] = acc
        spread, arrivals = [], []
        for k in range(1, N_DEV):
            p, where = peer(k)
            spread.append(pltpu.make_async_remote_copy(
                src_ref=piece(o_ref, me), dst_ref=piece(o_ref, me), send_sem=send2.at[k - 1], recv_sem=recv2.at[k - 1],
                device_id=where, device_id_type=MESH))
            arrivals.append(pltpu.make_async_remote_copy(
                src_ref=piece(o_ref, p), dst_ref=piece(o_ref, p), send_sem=send2.at[k - 1], recv_sem=recv2.at[k - 1],
                device_id=where, device_id_type=MESH))
        for cp in spread:
            cp.start()
        for cp in arrivals:
            cp.wait_recv()
        for cp in spread:
            cp.wait_send()

    sems = pltpu.SemaphoreType.DMA((N_DEV - 1,))
    return pl.pallas_call(
        body, name="allreduce_small",
        in_specs=[pl.BlockSpec(memory_space=pltpu.VMEM)] + [ANY] * nd, out_specs=pl.BlockSpec(memory_space=pltpu.VMEM),
        out_shape=jax.ShapeDtypeStruct((R, LANES), F32),
        scratch_shapes=[pltpu.VMEM((N_DEV, P, LANES), F32), sems, sems, sems, sems],
        compiler_params=_cparams(),
    )(vec, *deps)


def _pack(arrs):
    flat = jnp.concatenate([a.reshape(-1) for a in arrs])
    pad = (-flat.shape[0]) % (8 * N_DEV * LANES)
    return jnp.pad(flat, (0, pad)).reshape(-1, LANES)


def _unpack(packed, shapes):
    flat = packed.reshape(-1)
    out, off = [], 0
    for s in shapes:
        n = math.prod(s)
        out.append(flat[off:off + n].reshape(s))
        off += n
    return out


def kernel(x, w_in, w_conv, w_pool, pool_scale, sgu_ln_g, w_spatial, b_spatial, w_o, ln1_g, ln1_b, w_gate_up, w_down, ln2_g, ln2_b, loss_target, m_w_in, m_w_conv, m_w_pool, m_pool_scale, m_sgu_ln_g, m_w_spatial, m_b_spatial, m_w_o, m_ln1_g, m_ln1_b, m_w_gate_up, m_w_down, m_ln2_g, m_ln2_b, v_w_in, v_w_conv, v_w_pool, v_pool_scale, v_sgu_ln_g, v_w_spatial, v_b_spatial, v_w_o, v_ln1_g, v_ln1_b, v_w_gate_up, v_w_down, v_ln2_g, v_ln2_b):
    L = DEPTH
    T = x.shape[1]
    mx, my, mc = _my_place()
    dev = 4 * mx + 2 * my + mc
    xs = x[0]
    target = loss_target[0]

    conv_cols = w_conv.shape[2]
    w_conv_z = lax.dynamic_update_slice(jnp.zeros((L, 3, CONV_W), F32), w_conv, (0, 0, dev * conv_cols))
    w_conv_packed = _allreduce_small(_pack([w_conv_z]))
    w_conv_full = _unpack(w_conv_packed, [(L, 3, CONV_W)])[0]

    shards = (jnp.swapaxes(w_in, 1, 2).astype(BF16), jnp.swapaxes(w_gate_up, 1, 2).astype(BF16),
              w_o.astype(BF16), w_down.astype(BF16))
    first_gather = _ag_start_layer(shards, 0, [w_conv_packed])

    grad_x2, big_grads, small_grads = _local_step(
        xs, target, shards, first_gather, w_conv_full, w_pool, pool_scale, sgu_ln_g, w_spatial, b_spatial,
        ln1_g, ln1_b, ln2_g, ln2_b)
    grad_x = grad_x2[None]
    big_w = (w_in, w_gate_up, w_o, w_down)
    big_m = (m_w_in, m_w_gate_up, m_w_o, m_w_down)
    big_v = (v_w_in, v_w_gate_up, v_w_o, v_w_down)
    small_w = [w_conv_full, w_pool, pool_scale, sgu_ln_g, w_spatial, b_spatial, ln1_g, ln1_b, ln2_g, ln2_b]
    small_m = [m_w_conv, m_w_pool, m_pool_scale, m_sgu_ln_g, m_w_spatial, m_b_spatial, m_ln1_g, m_ln1_b, m_ln2_g, m_ln2_b]
    small_v = [v_w_conv, v_w_pool, v_pool_scale, v_sgu_ln_g, v_w_spatial, v_b_spatial, v_ln1_g, v_ln1_b, v_ln2_g, v_ln2_b]
    loss, grads, deltas, new_m, new_v = _reduce_and_update(
        big_grads, small_grads, big_w, big_m, big_v, small_w, small_m, small_v)
    return (loss, grad_x, *grads, *deltas, *new_m, *new_v)


def _ag_start_layer(shards, l, after):
    s_in, s_gu, s_o, s_dn = [s[l] for s in shards]
    first = _ag_start([s_in, s_o], "%da" % l, after=after)
    return first, _ag_start([s_gu, s_dn], "%db" % l, after=[first[4]])


def _ag_finish(gather, after, tag):
    send_sems, recv_sems, shards, lands, _ = gather
    shards, lands = _ag_wait(send_sems, recv_sems, shards, lands, after, tag)
    return _ag_pass_on(shards, lands)


def _rs_begin(parts, tag, after=()):
    return _rs_sibling_start([p.reshape(4, 2, p.shape[0] // N_DEV, D_MODEL) for p in parts], tag, after)


def _rs_continue(sibling_flight, after, c_arr, tag):
    send_sems, recv_sems, parts, lands, _ = sibling_flight
    parts, got = _rs_sibling_wait(send_sems, recv_sems, parts, lands, after, tag)
    return _rs_chip_start(_rs_chip_sum(parts, got, c_arr), tag)


def _local_step(xs, target, shards, gather, w_conv_full, w_pool, pool_scale, sgu_ln_g, w_spatial, b_spatial,
                ln1_g, ln1_b, ln2_g, ln2_b):
    L = DEPTH
    T = xs.shape[0]
    mx, my, mc = _my_place()
    c_arr = jnp.reshape(mc, (1,)).astype(jnp.int32)
    q_arr = jnp.reshape(2 * mx + my, (1,)).astype(jnp.int32)
    eye2 = jnp.eye(2, dtype=F32)
    wp = w_pool.reshape(L, 2, 2, HALF, HALF)
    wpool_bd = jnp.einsum("ltgcd,gh->ltgchd", wp, eye2).reshape(L, 2, LANES, LANES)
    wsp_t = w_spatial.reshape(L, 3, 2 * CHUNK, CHUNK)
    bias_t = jnp.repeat(jnp.swapaxes(b_spatial.reshape(L, 3, 2, CHUNK), 2, 3), HALF, axis=3)
    mixer_w = (w_conv_full, wpool_bd, pool_scale[:, None, :], sgu_ln_g[:, None, :], wsp_t, bias_t)
    g1, b1, g2, b2 = [a[:, None, :] for a in (ln1_g, ln1_b, ln2_g, ln2_b)]
    one, zero = jnp.ones((1, 1, D_MODEL), F32), jnp.zeros((1, 1, D_MODEL), F32)

    saved = []
    prev, pg, pb = xs, (one, 0), (zero, 0)
    prev_b = xs.astype(BF16)
    weights = []
    for l in range(L):
        g_in, g_o = _ag_finish(gather[0], [] if l == 0 else [prev_b], "%da" % l)
        proj = _mm(prev_b, g_in, "nt", F32, 512, IN_W, "mm_proj", deps=[gather[1][4]] if l == 0 else [])
        mixcat = _mixer_fwd(proj, *mixer_w, l)
        xhat1, rstd1, h_b = _mm_ln_fwd(mixcat, g_o, prev, pg, pb, (g1, l), (b1, l), "mm_wo_ln")
        send_sems, recv_sems, shards_b, lands_b, _ = gather[1]
        shards_b, lands_b = _ag_wait(send_sems, recv_sems, shards_b, lands_b, [h_b], "%db" % l)
        deps = []
        if l + 1 < L:
            gather = _ag_start_layer(shards, l + 1, [lands_b[0]])
            deps = [gather[1][4]]
        g_gu, g_dn = _ag_pass_on(shards_b, lands_b)
        weights.append((g_in, g_gu, g_o, g_dn))
        g_act, u_act, act = _mm_swiglu_fwd(h_b, g_gu, deps=deps)
        xhat2, rstd2, y_b = _mm_ln_fwd(act, g_dn, xhat1, (g1, l), (b1, l), (g2, l), (b2, l), "mm_down_ln")
        saved.append((prev_b, proj, mixcat, xhat1, rstd1, h_b, g_act, u_act, act, xhat2, rstd2))
        prev, pg, pb, prev_b = xhat2, (g2, l), (b2, l), y_b


    small = [None] * L
    big = None
    sibling_flight = None
    above = None
    for l in reversed(range(L)):
        prev_b, proj, mixcat, xhat1, rstd1, h_b, g_act, u_act, act, xhat2, rstd2 = saved[l]
        g_in, g_gu, g_o, g_dn = weights[l]
        chip_flight = None
        if above is None:
            loss_tile, dr2, dr2_b, dg2, db2 = _loss_ln_bwd(xhat2, rstd2, (g2, l), (b2, l), target)
        else:
            dr2, dr2_b, dg2, db2 = _mm_ln_bwd([above[0]], above[1], above[2], xhat2, rstd2, (g2, l),
                                              "mm_dx_ln", deps=[sibling_flight[4]])
            chip_flight = _rs_continue(sibling_flight, [dr2_b], c_arr, str(l + 1))
        dg_b, du_b = _mm_swiglu_bwd(dr2_b, g_dn, g_act, u_act, deps=[chip_flight[4]] if chip_flight else [])
        p_dn = _mm(act, dr2_b, "tn", BF16, DW_TM, D_MODEL // 2, "mm_dw_down")
        p_gu = _mm_tn_pair(dg_b, du_b, h_b, DW_TM, "mm_dw_gate_up")
        ffn_sibling = _rs_begin([p_gu, p_dn], "0b") if l == 0 else None
        dr1, dr1_b, dg1, db1, dmix = _mm_ln_bwd([dg_b, du_b], g_gu, dr2, xhat1, rstd1, (g1, l), "mm_dh_ln",
                                                deps=[ffn_sibling[4]] if l == 0 else [], w_back=g_o)
        ffn_flight = _rs_continue(ffn_sibling, [dr1_b], c_arr, "0b") if l == 0 else None
        p_o = _mm(mixcat, dr1_b, "tn", BF16, 512, D_MODEL, "mm_dw_o")
        dproj, dwc, dwp, dps, dlng, dwsp, dbias = _mixer_bwd(proj, dmix, *mixer_w, l,
                                                             deps=[ffn_flight[4]] if l == 0 else [])
        p_in = _mm(dproj, prev_b, "tn", BF16, IN_W, D_MODEL // 2, "mm_dw_in")
        small[l] = (dwc, dwp, dps, dlng, dwsp, dbias, dg1, db1, dg2, db2)
        above = (dproj, g_in, dr1)
        if chip_flight is not None:
            big = list(_rs_chip_finish(chip_flight, [p_in], q_arr, str(l + 1), l + 1, big))
        if l > 0:
            sibling_flight = _rs_begin([p_in, p_gu, p_o, p_dn], str(l))
        else:
            big[1], big[3] = _rs_chip_finish(ffn_flight, [p_in, p_o], q_arr, "0b", 0, [big[1], big[3]])

    def stack(i):
        return jnp.stack([small[l][i] for l in range(L)])

    dwp_bd = stack(1).reshape(L, 2, 2, HALF, 2, HALF)
    dwp_all = jnp.einsum("ltgchd,gh->ltgcd", dwp_bd, eye2).reshape(L, 4, HALF, HALF)
    dbs_all = jnp.swapaxes(stack(5)[:, :, :, :2], 2, 3).reshape(L, 6, CHUNK)
    small_grads = [stack(0), dwp_all, stack(2).reshape(L, POOL_W), stack(3).reshape(L, SGU_W),
                   stack(4).reshape(L, 6, CHUNK, CHUNK), dbs_all] + [stack(i).reshape(L, D_MODEL) for i in (6, 7, 8, 9)]
    small_grads.append(loss_tile[0, :1])
    packed_small = _allreduce_small(_pack(small_grads), deps=[big[1]])
    sibling_flight = _rs_begin([p_in, p_o], "0a", after=[packed_small])
    grad_x = _mm_ln_bwd([above[0]], above[1], above[2], None, None, None, "mm_dx_out", deps=[sibling_flight[4]])
    last_flight = _rs_continue(sibling_flight, [grad_x], c_arr, "0a")
    return grad_x, (big, last_flight, q_arr), (packed_small, [a.shape for a in small_grads])


def _rs_chip_finish(in_flight, after, q, tag, layer, into):
    send_sems, recv_sems, sums, lands, _ = in_flight
    sums, got = _rs_chip_wait(send_sems, recv_sems, sums, lands, after, tag)
    return _rs_finish(sums, got, q, layer, into)


def _reduce_and_update(big_grads, small_grads, big_w, big_m, big_v, small_w, small_m, small_v):
    L = DEPTH
    mx, my, mc = _my_place()
    dev = 4 * mx + 2 * my + mc
    conv_cols = CONV_W // N_DEV
    w_in, w_gate_up, w_o, w_down = big_w
    m_w_in, m_w_gate_up, m_w_o, m_w_down = big_m
    v_w_in, v_w_gate_up, v_w_o, v_w_down = big_v
    packed_g, small_shapes = small_grads
    big, last_flight, q_arr = big_grads

    def widen_conv(a):
        return lax.dynamic_update_slice(jnp.zeros((L, 3, CONV_W), F32), a, (0, 0, dev * conv_cols))

    small_m = [widen_conv(small_m[0])] + list(small_m[1:])
    small_v = [widen_conv(small_v[0])] + list(small_v[1:])
    pk_d, pk_m, pk_v = _adamw(_pack(small_w), packed_g, _pack(small_m), _pack(small_v), packed_g.shape[0] // 2)
    sg = _unpack(packed_g, small_shapes)
    sd = _unpack(pk_d, small_shapes)
    sm = _unpack(pk_m, small_shapes)
    sv = _unpack(pk_v, small_shapes)

    def conv_cols_of(a):
        return lax.dynamic_slice(a, (0, 0, dev * conv_cols), (L, 3, conv_cols))

    for lst in (sg, sd, sm, sv):
        lst[0] = conv_cols_of(lst[0])

    tr = lambda a: jnp.swapaxes(a, 1, 2)
    gt_gu, g_w_dn = big[1], big[3]
    d_gu, m_gu, v_gu = [tr(a) for a in _adamw(tr(w_gate_up), gt_gu, tr(m_w_gate_up), tr(v_w_gate_up), gt_gu.shape[1] // 2)]
    d_dn, m_dn, v_dn = _adamw(w_down, g_w_dn, m_w_down, v_w_down, w_down.shape[1])
    gt_in, g_w_o = _rs_chip_finish(last_flight, [d_gu, d_dn, pk_d], q_arr, "0a", 0, [big[0], big[2]])
    d_in, m_in, v_in = [tr(a) for a in _adamw(tr(w_in), gt_in, tr(m_w_in), tr(v_w_in), gt_in.shape[1])]
    d_o, m_o, v_o = _adamw(w_o, g_w_o, m_w_o, v_w_o, w_o.shape[1])
    g_w_in, g_w_gu = tr(gt_in), tr(gt_gu)

    def ordered(big_in, big_o, big_gu, big_dn, sm_list):
        return [big_in, sm_list[0], sm_list[1], sm_list[2], sm_list[3], sm_list[4], sm_list[5], big_o,
                sm_list[6], sm_list[7], big_gu, big_dn, sm_list[8], sm_list[9]]

    grads = ordered(g_w_in, g_w_o, g_w_gu, g_w_dn, sg)
    deltas = ordered(d_in, d_o, d_gu, d_dn, sd)
    new_m = ordered(m_in, m_o, m_gu, m_dn, sm)
    new_v = ordered(v_in, v_o, v_gu, v_dn, sv)
    return sg[10][0], grads, deltas, new_m, new_v
```

```python
import math

import jax
import jax.numpy as jnp
from jax import lax
from jax.experimental import pallas as pl
from jax.experimental.pallas import tpu as pltpu

F32 = jnp.float32
BF16 = jnp.bfloat16
MESH = pl.DeviceIdType.MESH

D_MODEL = 1024
DEPTH = 4
CONV_W = 384
POOL_W = 256
SGU_W = 384
IN_W = 3 * CONV_W + POOL_W + 2 * SGU_W
D_FF = 2816
CHUNK = 128
ALPHA = float((2 * DEPTH) ** 0.25)
LN_EPS = 1e-5
ADAM_LR, ADAM_B1, ADAM_B2, ADAM_EPS, ADAM_WD, ADAM_STEP = 0.001, 0.9, 0.999, 1e-08, 0.01, 10

N_DEV = 8
LANES = 128
HALF = 64
VMEM_LIMIT = 52 * 1024 * 1024

INV_SQRT2 = 0.7071067811865476
INV_SQRT_2PI = 0.3989422804014327


def _cparams(sem=None, **kw):
    if sem is not None:
        kw["dimension_semantics"] = sem
    return pltpu.CompilerParams(vmem_limit_bytes=VMEM_LIMIT, **kw)


_DN = {"nt": (((1,), (1,)), ((), ())), "tn": (((0,), (0,)), ((), ()))}


def _mm(a, b, mode, out_dtype, tm, tn, name, deps=()):
    if mode == "nt":
        (M, K), N = a.shape, b.shape[0]
        a_spec = pl.BlockSpec((tm, K), lambda i, j: (i, 0))
        b_spec = pl.BlockSpec((tn, K), lambda i, j: (j, 0))
    else:
        (K, M), N = a.shape, b.shape[1]
        a_spec = pl.BlockSpec((K, tm), lambda i, j: (0, i))
        b_spec = pl.BlockSpec((K, tn), lambda i, j: (0, j))
    assert M % tm == 0 and N % tn == 0, (M, N, K, tm, tn)
    nd = len(deps)

    def body(*refs):
        a_ref, b_ref, o_ref = refs[0], refs[1], refs[2 + nd]
        o_ref[...] = lax.dot_general(a_ref[...], b_ref[...], _DN[mode], preferred_element_type=F32).astype(o_ref.dtype)

    return pl.pallas_call(
        body,
        name=name,
        grid=(M // tm, N // tn),
        in_specs=[a_spec, b_spec] + [pl.BlockSpec(memory_space=pl.ANY)] * nd,
        out_specs=pl.BlockSpec((tm, tn), lambda i, j: (i, j)),
        out_shape=jax.ShapeDtypeStruct((M, N), out_dtype),
        compiler_params=_cparams(("parallel", "parallel")),
    )(a, b, *deps)


def _mm_tn_pair(a1, a2, b, tm, name):
    K, M = a1.shape
    N = b.shape[1]
    n1 = M // tm

    def body(a1_ref, a2_ref, b_ref, o_ref):
        i = pl.program_id(0)

        @pl.when(i < n1)
        def _():
            o_ref[...] = lax.dot_general(a1_ref[...], b_ref[...], _DN["tn"], preferred_element_type=F32).astype(o_ref.dtype)

        @pl.when(i >= n1)
        def _():
            o_ref[...] = lax.dot_general(a2_ref[...], b_ref[...], _DN["tn"], preferred_element_type=F32).astype(o_ref.dtype)

    return pl.pallas_call(
        body, name=name, grid=(2 * n1,),
        in_specs=[pl.BlockSpec((K, tm), lambda i: (0, jnp.minimum(i, n1 - 1))),
                  pl.BlockSpec((K, tm), lambda i: (0, jnp.maximum(i - n1, 0))),
                  pl.BlockSpec((K, N), lambda i: (0, 0))],
        out_specs=pl.BlockSpec((tm, N), lambda i: (i, 0)),
        out_shape=jax.ShapeDtypeStruct((2 * M, N), BF16),
        compiler_params=_cparams(("arbitrary",)),
    )(a1, a2, b)


LN_SUB = 256
LN_TM = 512


def _vec(v):
    arr, layer = v
    return arr, pl.BlockSpec((None, 1, D_MODEL), lambda *_: (layer, 0, 0))


def _mm_ln_fwd(a, b, prev, pg, pb, g, bias, name):
    T, K = a.shape
    tm = LN_TM

    def body(a_ref, b_ref, prev_ref, pg_ref, pb_ref, g_ref, bias_ref, xhat_ref, rstd_ref, y_ref):
        for s in range(tm // LN_SUB):
            rows = slice(s * LN_SUB, (s + 1) * LN_SUB)
            mm = jnp.dot(a_ref[rows, :], b_ref[...], preferred_element_type=F32)
            r = ALPHA * (prev_ref[rows, :] * pg_ref[...] + pb_ref[...]) + mm
            mu = jnp.mean(r, axis=-1, keepdims=True)
            xc = r - mu
            var = jnp.mean(xc * xc, axis=-1, keepdims=True)
            rstd = lax.rsqrt(var + LN_EPS)
            xhat = xc * rstd
            xhat_ref[rows, :] = xhat
            rstd_ref[rows, :] = rstd
            y_ref[rows, :] = (xhat * g_ref[...] + bias_ref[...]).astype(y_ref.dtype)

    row = pl.BlockSpec((tm, D_MODEL), lambda i: (i, 0))
    vecs = [_vec(v) for v in (pg, pb, g, bias)]
    return pl.pallas_call(
        body, name=name, grid=(T // tm,),
        in_specs=[pl.BlockSpec((tm, K), lambda i: (i, 0)),
                  pl.BlockSpec((K, D_MODEL), lambda i: (0, 0), pipeline_mode=pl.Buffered(1)),
                  row] + [s for _, s in vecs],
        out_specs=[row, pl.BlockSpec((tm, 1), lambda i: (i, 0)), row],
        out_shape=[jax.ShapeDtypeStruct((T, D_MODEL), F32), jax.ShapeDtypeStruct((T, 1), F32),
                   jax.ShapeDtypeStruct((T, D_MODEL), BF16)],
        compiler_params=_cparams(("parallel",)),
    )(a, b, prev, *[a_ for a_, _ in vecs])


def _mm_ln_bwd(a_list, b, dres, xhat, rstd, g, name, deps=(), w_back=None):
    T = a_list[0].shape[0]
    tm = LN_TM
    na, nd = len(a_list), len(deps)
    ks = [a.shape[1] for a in a_list]
    last = xhat is None
    nout = 1 if last else (5 if w_back is not None else 4)

    def body(*refs):
        a_refs, b_ref, dres_ref = refs[:na], refs[na], refs[na + 1]
        if not last:
            xhat_ref, rstd_ref, g_ref = refs[na + 2:na + 5]
            dr_ref, drb_ref, dg_ref, db_ref = refs[len(refs) - nout:len(refs) - nout + 4]

            @pl.when(pl.program_id(0) == 0)
            def _():
                dg_ref[...] = jnp.zeros_like(dg_ref)
                db_ref[...] = jnp.zeros_like(db_ref)

        for s in range(tm // LN_SUB):
            rows = slice(s * LN_SUB, (s + 1) * LN_SUB)
            mm, off = None, 0
            for a_ref, k in zip(a_refs, ks):
                part = jnp.dot(a_ref[rows, :], b_ref[off:off + k, :], preferred_element_type=F32)
                mm = part if mm is None else mm + part
                off += k
            dy = ALPHA * dres_ref[rows, :] + mm
            if last:
                refs[-1][rows, :] = dy
                continue
            xhat_v = xhat_ref[rows, :]
            dg_ref[...] += jnp.sum(dy * xhat_v, axis=0, keepdims=True)
            db_ref[...] += jnp.sum(dy, axis=0, keepdims=True)
            dxh = dy * g_ref[...]
            m1 = jnp.mean(dxh, axis=-1, keepdims=True)
            m2 = jnp.mean(dxh * xhat_v, axis=-1, keepdims=True)
            dr = rstd_ref[rows, :] * (dxh - m1 - xhat_v * m2)
            dr_ref[rows, :] = dr
            dr_b = dr.astype(drb_ref.dtype)
            drb_ref[rows, :] = dr_b
            if w_back is not None:
                refs[-1][rows, :] = lax.dot_general(dr_b, refs[na + 5][...], _DN["nt"], preferred_element_type=F32)

    row = pl.BlockSpec((tm, D_MODEL), lambda i: (i, 0))
    vec = pl.BlockSpec((1, D_MODEL), lambda i: (0, 0))
    in_specs = [pl.BlockSpec((tm, k), lambda i: (i, 0)) for k in ks]
    in_specs += [pl.BlockSpec((sum(ks), D_MODEL), lambda i: (0, 0), pipeline_mode=pl.Buffered(1)), row]
    args = list(a_list) + [b, dres]
    if last:
        out_specs, out_shape = row, jax.ShapeDtypeStruct((T, D_MODEL), F32)
    else:
        g_arr, g_spec = _vec(g)
        in_specs += [row, pl.BlockSpec((tm, 1), lambda i: (i, 0)), g_spec]
        args += [xhat, rstd, g_arr]
        out_specs = [row, row, vec, vec]
        out_shape = [jax.ShapeDtypeStruct((T, D_MODEL), F32), jax.ShapeDtypeStruct((T, D_MODEL), BF16),
                     jax.ShapeDtypeStruct((1, D_MODEL), F32), jax.ShapeDtypeStruct((1, D_MODEL), F32)]
        if w_back is not None:
            in_specs.append(pl.BlockSpec(w_back.shape, lambda i: (0, 0), pipeline_mode=pl.Buffered(1)))
            args.append(w_back)
            out_specs.append(row)
            out_shape.append(jax.ShapeDtypeStruct((T, w_back.shape[0]), F32))
    return pl.pallas_call(
        body, name=name, grid=(T // tm,),
        in_specs=in_specs + [pl.BlockSpec(memory_space=pl.ANY)] * nd,
        out_specs=out_specs, out_shape=out_shape,
        compiler_params=_cparams(("parallel",) if last else ("arbitrary",)),
    )(*args, *deps)


DW_TM = 1408
FF_TN = 256
FF_TM = 2048
SAVED_GU = BF16


def _mm_swiglu_fwd(h, w_gu, deps=()):
    T = h.shape[0]
    tm = min(T, FF_TM)
    nj = D_FF // FF_TN
    nd = len(deps)

    def body(*refs):
        h_ref, wg_ref, wu_ref = refs[:3]
        g_ref, u_ref, act_ref = refs[3 + nd:]
        hv = h_ref[...]
        gv = lax.dot_general(hv, wg_ref[...], _DN["nt"], preferred_element_type=F32)
        uv = lax.dot_general(hv, wu_ref[...], _DN["nt"], preferred_element_type=F32)
        g_ref[...] = gv.astype(g_ref.dtype)
        u_ref[...] = uv.astype(u_ref.dtype)
        act_ref[...] = (gv * jax.nn.sigmoid(gv) * uv).astype(act_ref.dtype)

    tile = pl.BlockSpec((tm, FF_TN), lambda j, i: (i, j))
    return pl.pallas_call(
        body, name="mm_gate_up_swiglu", grid=(nj, T // tm),
        in_specs=[pl.BlockSpec((tm, D_MODEL), lambda j, i: (i, 0)),
                  pl.BlockSpec((FF_TN, D_MODEL), lambda j, i: (j, 0)),
                  pl.BlockSpec((FF_TN, D_MODEL), lambda j, i: (j + nj, 0))] + [pl.BlockSpec(memory_space=pl.ANY)] * nd,
        out_specs=[tile, tile, tile],
        out_shape=[jax.ShapeDtypeStruct((T, D_FF), SAVED_GU), jax.ShapeDtypeStruct((T, D_FF), SAVED_GU),
                   jax.ShapeDtypeStruct((T, D_FF), BF16)],
        compiler_params=_cparams(("parallel", "parallel")),
    )(h, w_gu, w_gu, *deps)


def _mm_swiglu_bwd(dr, w_dn, g, u, deps=()):
    T = dr.shape[0]
    tm = min(T, FF_TM)

    def body(*refs):
        dr_ref, w_ref, g_ref, u_ref = refs[:4]
        dg_ref, du_ref = refs[-2:]
        da = lax.dot_general(dr_ref[...], w_ref[...], _DN["nt"], preferred_element_type=F32)
        gv, uv = g_ref[...].astype(F32), u_ref[...].astype(F32)
        s = jax.nn.sigmoid(gv)
        du_ref[...] = (da * (gv * s)).astype(du_ref.dtype)
        dg_ref[...] = (da * uv * (s * (1.0 + gv * (1.0 - s)))).astype(dg_ref.dtype)

    tile = pl.BlockSpec((tm, FF_TN), lambda j, i: (i, j))
    return pl.pallas_call(
        body, name="mm_dact_swiglu", grid=(D_FF // FF_TN, T // tm),
        in_specs=[pl.BlockSpec((tm, D_MODEL), lambda j, i: (i, 0)), pl.BlockSpec((FF_TN, D_MODEL), lambda j, i: (j, 0)),
                  tile, tile] + [ANY] * len(deps),
        out_specs=[tile, tile],
        out_shape=[jax.ShapeDtypeStruct((T, D_FF), BF16)] * 2,
        compiler_params=_cparams(("parallel", "parallel")),
    )(dr, w_dn, g, u, *deps)


def _gelu(x):
    return 0.5 * x * (1.0 + lax.erf(x * INV_SQRT2))


def _gelu_grad(x):
    return 0.5 * (1.0 + lax.erf(x * INV_SQRT2)) + x * (jnp.exp(-0.5 * x * x) * INV_SQRT_2PI)


def _shift_down(z, k):
    row = lax.broadcasted_iota(jnp.int32, z.shape, 0)
    return jnp.where(row >= k, pltpu.roll(z, k, 0), 0.0)


def _shift_up(z, k):
    n = z.shape[0]
    row = lax.broadcasted_iota(jnp.int32, z.shape, 0)
    return jnp.where(row < n - k, pltpu.roll(z, n - k, 0), 0.0)


def _lo_mask(shape):
    return lax.broadcasted_iota(jnp.int32, shape, len(shape) - 1) < HALF


def _seg_mean(x, lo):
    a = jnp.sum(jnp.where(lo, x, 0.0), axis=-1, keepdims=True)
    b = jnp.sum(jnp.where(lo, 0.0, x), axis=-1, keepdims=True)
    return jnp.where(lo, a, b) * (1.0 / HALF)


def _pool_windows(first):
    lo = _lo_mask((1, LANES))
    return jnp.where(first, jnp.where(lo, 2.0, 4.0), jnp.where(lo, 8.0, 16.0)), lo


def _pool_mean_minus_token(p, first):
    wl, lo = _pool_windows(first)
    s2 = p + _shift_down(p, 1)
    s4 = s2 + _shift_down(s2, 2)
    s8 = s4 + _shift_down(s4, 4)
    s16 = s8 + _shift_down(s8, 8)
    win = jnp.where(first, jnp.where(lo, s2, s4), jnp.where(lo, s8, s16))
    t1 = (lax.broadcasted_iota(jnp.int32, p.shape, 0) + 1).astype(F32)
    count = jnp.minimum(t1, wl)
    return win / count - p, count


SGU_UNROLL_FWD = 4
SGU_UNROLL_BWD = 2


def _tril_keep():
    r = lax.broadcasted_iota(jnp.int32, (2 * CHUNK, CHUNK), 0)
    s = lax.broadcasted_iota(jnp.int32, (2 * CHUNK, CHUNK), 1)
    return s <= (r & (CHUNK - 1))


def _sgu_chunk_fwd(u, v, g, wm, bias, lo):
    ug = _gelu(u)
    vg = _gelu(v)
    mu = _seg_mean(vg, lo)
    xc = vg - mu
    var = _seg_mean(xc * xc, lo)
    rstd = lax.rsqrt(var + LN_EPS)
    vn = xc * rstd
    vh = (vn * g).astype(BF16)
    mm2 = jnp.dot(wm, vh, preferred_element_type=F32)
    mixed = jnp.where(lo, mm2[:CHUNK], mm2[CHUNK:]) + bias
    return ug, vn, rstd, vh, mixed


def _mixer_fwd(proj, wconv, wpool_bd, pscale, lng, wsp, bias, layer):
    T = proj.shape[0]
    nchunk = T // CHUNK

    def body(a_ref, b_ref, c_ref, wc_ref, wp_ref, ps_ref, lng_ref, wsp_ref, bias_ref, o_ref):
        j = pl.program_id(0)

        @pl.when(j < 3)
        def _conv():
            z = c_ref[...] * a_ref[...]
            w = wc_ref[...]
            y = w[0:1] * _shift_down(z, 2) + w[1:2] * _shift_down(z, 1) + w[2:3] * z
            o_ref[...] = (b_ref[...] * y).astype(o_ref.dtype)

        @pl.when((j >= 3) & (j < 5))
        def _pool():
            d, _ = _pool_mean_minus_token(a_ref[...], j == 3)
            y = jnp.dot(d.astype(BF16), wp_ref[...].astype(BF16), preferred_element_type=F32)
            o_ref[...] = (y * ps_ref[...]).astype(o_ref.dtype)

        @pl.when(j >= 5)
        def _sgu():
            lo = _lo_mask((CHUNK, LANES))
            wm = jnp.where(_tril_keep(), wsp_ref[...], 0.0).astype(BF16)
            bias_t = bias_ref[...]
            g = lng_ref[...]

            def chunk(n, carry):
                rows = pl.ds(pl.multiple_of(n * CHUNK, CHUNK), CHUNK)
                ug, _, _, _, mixed = _sgu_chunk_fwd(a_ref[rows, :], b_ref[rows, :], g, wm, bias_t, lo)
                o_ref[rows, :] = (ug * mixed).astype(o_ref.dtype)
                return carry

            lax.fori_loop(0, nchunk, chunk, 0, unroll=SGU_UNROLL_FWD)

    def col(f):
        return lambda j: (0, f(j))

    clip = lambda v, lo, hi: jnp.minimum(jnp.maximum(v, lo), hi)
    return pl.pallas_call(
        body,
        name="mixer_fwd",
        grid=(8,),
        in_specs=[
            pl.BlockSpec((T, LANES), col(lambda j: jnp.where(j < 3, j, jnp.where(j < 5, j + 6, j + 6)))),
            pl.BlockSpec((T, LANES), col(lambda j: jnp.where(j < 3, j + 3, jnp.where(j < 5, 5, j + 9)))),
            pl.BlockSpec((T, LANES), col(lambda j: jnp.where(j < 3, j + 6, 8))),
            pl.BlockSpec((None, 3, LANES), lambda j: (layer, 0, clip(j, 0, 2))),
            pl.BlockSpec((None, None, LANES, LANES), lambda j: (layer, clip(j - 3, 0, 1), 0, 0)),
            pl.BlockSpec((None, 1, LANES), lambda j: (layer, 0, clip(j - 3, 0, 1))),
            pl.BlockSpec((None, 1, LANES), lambda j: (layer, 0, clip(j - 5, 0, 2))),
            pl.BlockSpec((None, None, 2 * CHUNK, CHUNK), lambda j: (layer, clip(j - 5, 0, 2), 0, 0)),
            pl.BlockSpec((None, None, CHUNK, LANES), lambda j: (layer, clip(j - 5, 0, 2), 0, 0)),
        ],
        out_specs=pl.BlockSpec((T, LANES), lambda j: (0, j)),
        out_shape=jax.ShapeDtypeStruct((T, D_MODEL), BF16),
        compiler_params=_cparams(("arbitrary",)),
    )(proj, proj, proj, wconv, wpool_bd, pscale, lng, wsp, bias)


def _mixer_bwd(proj, dmix, wconv, wpool_bd, pscale, lng, wsp, bias, layer, deps=()):
    T = proj.shape[0]
    nchunk = T // CHUNK

    def body(*refs):
        a_ref, b_ref, c_ref, dm_ref, wc_ref, wp_ref, ps_ref, lng_ref, wsp_ref, bias_ref = refs[:10]
        o_ref, dwc_ref, dwp_ref, dps_ref, dlng_ref, dwsp_ref, dbias_ref, keep1, keep2 = refs[10 + len(deps):]
        k = pl.program_id(0)

        @pl.when(k < 3)
        def _conv():
            xa, gb, gc, dya = a_ref[...], b_ref[...], c_ref[...], dm_ref[...]
            w = wc_ref[...]
            z = gc * xa
            z1 = _shift_down(z, 1)
            z2 = _shift_down(z, 2)
            y = w[0:1] * z2 + w[1:2] * z1 + w[2:3] * z
            dyv = dya * gb
            dz = w[2:3] * dyv + w[1:2] * _shift_up(dyv, 1) + w[0:1] * _shift_up(dyv, 2)
            dwc_ref[0:1, :] = jnp.sum(dyv * z2, axis=0, keepdims=True)
            dwc_ref[1:2, :] = jnp.sum(dyv * z1, axis=0, keepdims=True)
            dwc_ref[2:3, :] = jnp.sum(dyv * z, axis=0, keepdims=True)
            o_ref[...] = (dz * gc).astype(o_ref.dtype)
            keep1[k] = (dya * y).astype(keep1.dtype)
            keep1[k + 3] = (dz * xa).astype(keep1.dtype)

        @pl.when((k >= 3) & (k < 9))
        def _emit_gb_gc():
            o_ref[...] = keep1[k - 3]

        @pl.when((k >= 9) & (k < 11))
        def _pool():
            first = k == 9
            p, dyb = a_ref[...], dm_ref[...]
            d, count = _pool_mean_minus_token(p, first)
            w2 = wp_ref[...].astype(BF16)
            db = d.astype(BF16)
            y = jnp.dot(db, w2, preferred_element_type=F32)
            dps_ref[...] = jnp.sum(dyb * y, axis=0, keepdims=True)
            dyv = (dyb * ps_ref[...]).astype(BF16)
            dd = lax.dot_general(dyv, w2, _DN["nt"], preferred_element_type=F32)
            dwp_ref[...] = lax.dot_general(db, dyv, _DN["tn"], preferred_element_type=F32)
            dwin = dd / count
            a2 = dwin + _shift_up(dwin, 1)
            a4 = a2 + _shift_up(a2, 2)
            a8 = a4 + _shift_up(a4, 4)
            a16 = a8 + _shift_up(a8, 8)
            _, lo = _pool_windows(first)
            back = jnp.where(first, jnp.where(lo, a2, a4), jnp.where(lo, a8, a16))
            o_ref[...] = (back - dd).astype(o_ref.dtype)

        @pl.when((k >= 11) & (k < 14))
        def _sgu():
            lo = _lo_mask((CHUNK, LANES))
            keep = _tril_keep()
            wm = jnp.where(keep, wsp_ref[...], 0.0).astype(BF16)
            bias_t = bias_ref[...]
            g = lng_ref[...]
            dwsp_ref[...] = jnp.zeros_like(dwsp_ref)
            dbias_ref[...] = jnp.zeros_like(dbias_ref)
            dlng_ref[...] = jnp.zeros_like(dlng_ref)

            def chunk(n, carry):
                rows = pl.ds(pl.multiple_of(n * CHUNK, CHUNK), CHUNK)
                u, v, dyc = a_ref[rows, :], b_ref[rows, :], dm_ref[rows, :]
                ug, vn, rstd, vh, mixed = _sgu_chunk_fwd(u, v, g, wm, bias_t, lo)
                dmx = dyc * ug
                o_ref[rows, :] = (dyc * mixed * _gelu_grad(u)).astype(o_ref.dtype)
                dbias_ref[...] += dmx
                dst = jnp.concatenate([jnp.where(lo, dmx, 0.0), jnp.where(lo, 0.0, dmx)], axis=0).astype(BF16)
                dwsp_ref[...] += lax.dot_general(dst, vh, _DN["nt"], preferred_element_type=F32)
                dvh = lax.dot_general(wm, dst, _DN["tn"], preferred_element_type=F32)
                dlng_ref[...] += jnp.sum(dvh * vn, axis=0, keepdims=True)
                dvn = dvh * g
                m1 = _seg_mean(dvn, lo)
                m2 = _seg_mean(dvn * vn, lo)
                dvg = rstd * (dvn - m1 - vn * m2)
                keep2[k - 11, rows, :] = (dvg * _gelu_grad(v)).astype(keep2.dtype)
                return carry

            lax.fori_loop(0, nchunk, chunk, 0, unroll=SGU_UNROLL_BWD)
            dwsp_ref[...] = jnp.where(keep, dwsp_ref[...], 0.0)
            dbt = dbias_ref[...]
            lane = lax.broadcasted_iota(jnp.int32, (CHUNK, LANES), 1)
            sa = jnp.sum(jnp.where(lo, dbt, 0.0), axis=-1, keepdims=True)
            sb = jnp.sum(jnp.where(lo, 0.0, dbt), axis=-1, keepdims=True)
            dbias_ref[...] = jnp.where(lane == 0, sa, jnp.where(lane == 1, sb, 0.0))

        @pl.when(k >= 14)
        def _emit_v():
            o_ref[...] = keep2[k - 14]

    def col(f):
        return lambda k: (0, f(k))

    clip = lambda v, lo, hi: jnp.minimum(jnp.maximum(v, lo), hi)
    view_a = lambda k: jnp.where(k < 3, k, jnp.where(k < 9, 2, jnp.where(k < 14, k, 13)))
    view_b = lambda k: jnp.where(k < 3, k + 3, jnp.where(k < 11, 5, jnp.where(k < 14, k + 3, 16)))
    view_c = lambda k: jnp.where(k < 3, k + 6, 8)
    view_dm = lambda k: jnp.where(k < 3, k, jnp.where(k < 9, 2, jnp.where(k < 14, k - 6, 7)))
    return pl.pallas_call(
        body,
        name="mixer_bwd",
        grid=(17,),
        in_specs=[
            pl.BlockSpec((T, LANES), col(view_a)),
            pl.BlockSpec((T, LANES), col(view_b)),
            pl.BlockSpec((T, LANES), col(view_c)),
            pl.BlockSpec((T, LANES), col(view_dm)),
            pl.BlockSpec((None, 3, LANES), lambda k: (layer, 0, clip(k, 0, 2))),
            pl.BlockSpec((None, None, LANES, LANES), lambda k: (layer, clip(k - 9, 0, 1), 0, 0)),
            pl.BlockSpec((None, 1, LANES), lambda k: (layer, 0, clip(k - 9, 0, 1))),
            pl.BlockSpec((None, 1, LANES), lambda k: (layer, 0, clip(k - 11, 0, 2))),
            pl.BlockSpec((None, None, 2 * CHUNK, CHUNK), lambda k: (layer, clip(k - 11, 0, 2), 0, 0)),
            pl.BlockSpec((None, None, CHUNK, LANES), lambda k: (layer, clip(k - 11, 0, 2), 0, 0)),
        ] + [pl.BlockSpec(memory_space=pl.ANY)] * len(deps),
        out_specs=[
            pl.BlockSpec((T, LANES), lambda k: (0, k)),
            pl.BlockSpec((3, LANES), col(lambda k: clip(k, 0, 2))),
            pl.BlockSpec((None, LANES, LANES), lambda k: (clip(k - 9, 0, 1), 0, 0)),
            pl.BlockSpec((1, LANES), col(lambda k: clip(k - 9, 0, 1))),
            pl.BlockSpec((1, LANES), col(lambda k: clip(k - 11, 0, 2))),
            pl.BlockSpec((None, 2 * CHUNK, CHUNK), lambda k: (clip(k - 11, 0, 2), 0, 0)),
            pl.BlockSpec((None, CHUNK, LANES), lambda k: (clip(k - 11, 0, 2), 0, 0)),
        ],
        out_shape=[
            jax.ShapeDtypeStruct((T, IN_W), BF16),
            jax.ShapeDtypeStruct((3, CONV_W), F32),
            jax.ShapeDtypeStruct((2, LANES, LANES), F32),
            jax.ShapeDtypeStruct((1, POOL_W), F32),
            jax.ShapeDtypeStruct((1, SGU_W), F32),
            jax.ShapeDtypeStruct((3, 2 * CHUNK, CHUNK), F32),
            jax.ShapeDtypeStruct((3, CHUNK, LANES), F32),
        ],
        scratch_shapes=[pltpu.VMEM((6, T, LANES), BF16), pltpu.VMEM((3, T, LANES), BF16)],
        compiler_params=_cparams(("arbitrary",)),
    )(proj, proj, proj, dmix, wconv, wpool_bd, pscale, lng, wsp, bias, *deps)


def _loss_ln_bwd(xhat, rstd, g, b, target, tm=256):
    T = xhat.shape[0]

    def body(xhat_ref, rstd_ref, g_ref, b_ref, t_ref, loss_ref, dr_ref, drb_ref, dg_ref, db_ref):
        xhat_v = xhat_ref[...]
        err = xhat_v * g_ref[...] + b_ref[...] - t_ref[...]
        dy = err * (1.0 / D_MODEL)

        @pl.when(pl.program_id(0) == 0)
        def _():
            loss_ref[...] = jnp.zeros_like(loss_ref)
            dg_ref[...] = jnp.zeros_like(dg_ref)
            db_ref[...] = jnp.zeros_like(db_ref)

        part = jnp.sum(jnp.sum(err * err, axis=-1, keepdims=True), axis=0, keepdims=True)
        loss_ref[...] += jnp.broadcast_to(part * (0.5 / D_MODEL), loss_ref.shape)
        dg_ref[...] += jnp.sum(dy * xhat_v, axis=0, keepdims=True)
        db_ref[...] += jnp.sum(dy, axis=0, keepdims=True)
        dxh = dy * g_ref[...]
        m1 = jnp.mean(dxh, axis=-1, keepdims=True)
        m2 = jnp.mean(dxh * xhat_v, axis=-1, keepdims=True)
        dr = rstd_ref[...] * (dxh - m1 - xhat_v * m2)
        dr_ref[...] = dr
        drb_ref[...] = dr.astype(drb_ref.dtype)

    row = pl.BlockSpec((tm, D_MODEL), lambda i: (i, 0))
    vec = pl.BlockSpec((1, D_MODEL), lambda i: (0, 0))
    (g_arr, g_spec), (b_arr, b_spec) = _vec(g), _vec(b)
    return pl.pallas_call(
        body,
        name="loss_ln_bwd",
        grid=(T // tm,),
        in_specs=[row, pl.BlockSpec((tm, 1), lambda i: (i, 0)), g_spec, b_spec, row],
        out_specs=[pl.BlockSpec((8, LANES), lambda i: (0, 0)), row, row, vec, vec],
        out_shape=[jax.ShapeDtypeStruct((8, LANES), F32),
                   jax.ShapeDtypeStruct((T, D_MODEL), F32), jax.ShapeDtypeStruct((T, D_MODEL), BF16),
                   jax.ShapeDtypeStruct((1, D_MODEL), F32), jax.ShapeDtypeStruct((1, D_MODEL), F32)],
        compiler_params=_cparams(("arbitrary",)),
    )(xhat, rstd, g_arr, b_arr, target)


def _adamw(w, g, m, v, tr):
    R, C = w.shape[-2:]
    assert R % tr == 0
    c1 = 1.0 - ADAM_B1 ** ADAM_STEP
    c2 = 1.0 - ADAM_B2 ** ADAM_STEP

    def body(w_ref, g_ref, m_ref, v_ref, d_ref, mo_ref, vo_ref):
        gv = g_ref[...]
        mn = ADAM_B1 * m_ref[...] + (1.0 - ADAM_B1) * gv
        vn = ADAM_B2 * v_ref[...] + (1.0 - ADAM_B2) * (gv * gv)
        d_ref[...] = -ADAM_LR * ((mn / c1) / (jnp.sqrt(vn / c2) + ADAM_EPS) + ADAM_WD * w_ref[...])
        mo_ref[...] = mn
        vo_ref[...] = vn

    if w.ndim == 2:
        grid, blk = (R // tr,), pl.BlockSpec((tr, C), lambda i: (i, 0))
    else:
        grid, blk = (w.shape[0], R // tr), pl.BlockSpec((None, tr, C), lambda l, i: (l, i, 0))
    return pl.pallas_call(
        body, name="adamw", grid=grid, in_specs=[blk] * 4, out_specs=[blk] * 3,
        out_shape=[jax.ShapeDtypeStruct(w.shape, F32)] * 3, compiler_params=_cparams(("parallel",) * len(grid)),
    )(w, g, m, v)


def _my_place():
    return lax.axis_index("x"), lax.axis_index("y"), lax.axis_index("c")


ANY = pl.BlockSpec(memory_space=pl.ANY)
HBM = pl.BlockSpec(memory_space=pltpu.HBM)
SEM = pl.BlockSpec(memory_space=pltpu.SEMAPHORE)
EFFECT = pltpu.SideEffectType.DATAFLOW_SIDE_EFFECTING


def _in_hbm(a):
    return pltpu.with_memory_space_constraint(a, pltpu.HBM)


def _block_rows(ref, dev):
    r = ref.shape[0] // N_DEV
    start = pl.multiple_of((4 * dev[0] + 2 * dev[1] + dev[2]) * r, 16)
    return ref.at[pl.ds(start, r), :]


def _ag_first_copies(s_refs, land_refs, send_sems, recv_sems, receiving):
    x, y, c = _my_place()
    peers = [(x, y, 1 - c)] + [(*chip, c) for chip in _other_chips(x, y)]
    copies = []
    for k, peer in enumerate(peers):
        block = peer if receiving else (x, y, c)
        copies += [pltpu.make_async_remote_copy(
            src_ref=s_refs[w], dst_ref=_block_rows(land_refs[w], block),
            send_sem=send_sems.at[k * len(s_refs) + w], recv_sem=recv_sems.at[k * len(s_refs) + w],
            device_id=peer, device_id_type=MESH)
            for w in range(len(s_refs))]
    return copies


def _ag_start(shards, layer, after=()):
    nw = len(shards)

    def body(*refs):
        s_refs, land_refs = refs[:nw], refs[nw:2 * nw]
        token = refs[-1]
        sems = 2 * nw + len(after)
        for cp in _ag_first_copies(s_refs, land_refs, refs[sems], refs[sems + 1], False):
            cp.start()
        token[...] = jnp.zeros_like(token)

    lands = [lax.empty((N_DEV * s.shape[0], D_MODEL), BF16) for s in shards]
    out = pl.pallas_call(
        body, name="ag_start_%s" % layer,
        in_specs=[HBM] * (2 * nw) + [ANY] * len(after),
        out_specs=(SEM, SEM, *[HBM] * (2 * nw), pl.BlockSpec(memory_space=pltpu.VMEM)),
        out_shape=(pltpu.SemaphoreType.DMA((4 * nw,)), pltpu.SemaphoreType.DMA((4 * nw,)),
                   *[pltpu.HBM(a.shape, a.dtype) for a in list(shards) + lands],
                   jax.ShapeDtypeStruct((8, LANES), F32)),
        input_output_aliases={i: 2 + i for i in range(2 * nw)},
        compiler_params=pltpu.CompilerParams(has_side_effects=EFFECT),
    )(*[_in_hbm(a) for a in list(shards) + lands], *after)
    return out[0], out[1], out[2:2 + nw], out[2 + nw:2 + 2 * nw], out[-1]


def _ag_wait(send_sems, recv_sems, shards, lands, after, layer):
    nw = len(shards)

    def body(*refs):
        s_refs, land_refs = refs[:nw], refs[nw:2 * nw]
        for cp in _ag_first_copies(s_refs, land_refs, refs[2 * nw], refs[2 * nw + 1], True):
            cp.wait_send()
            cp.wait_recv()

    out = pl.pallas_call(
        body, name="ag_wait_%s" % layer,
        in_specs=[HBM] * (2 * nw) + [SEM, SEM] + [ANY] * len(after),
        out_specs=[HBM] * (2 * nw),
        out_shape=[pltpu.HBM(a.shape, a.dtype) for a in list(shards) + list(lands)],
        input_output_aliases={i: i for i in range(2 * nw)},
        compiler_params=pltpu.CompilerParams(has_side_effects=EFFECT),
    )(*shards, *lands, send_sems, recv_sems, *after)
    return out[:nw], out[nw:]


def _ag_pass_on(shards, lands):
    nw = len(shards)

    def body(*refs):
        s_refs, g_refs = refs[:nw], refs[2 * nw:3 * nw]
        send_sems, recv_sems, local_sems = refs[3 * nw:3 * nw + 3]
        stage = refs[3 * nw + 3:]
        x, y, c = _my_place()
        load = [pltpu.make_async_copy(s_refs[w], stage[w], local_sems.at[w]) for w in range(nw)]
        mine = [pltpu.make_async_copy(stage[w], _block_rows(g_refs[w], (x, y, c)), local_sems.at[w])
                for w in range(nw)]
        for cp in load:
            cp.start()
        sends, arrivals = [], []
        for j, chip in enumerate(_other_chips(x, y)):
            for w in range(nw):
                rows_out = _block_rows(g_refs[w], (*chip, c))
                rows_in = _block_rows(g_refs[w], (*chip, 1 - c))
                sends.append(pltpu.make_async_remote_copy(
                    src_ref=rows_out, dst_ref=rows_out, send_sem=send_sems.at[j, w], recv_sem=recv_sems.at[j, w],
                    device_id=(x, y, 1 - c), device_id_type=MESH))
                arrivals.append(pltpu.make_async_remote_copy(
                    src_ref=rows_in, dst_ref=rows_in, send_sem=send_sems.at[j, w], recv_sem=recv_sems.at[j, w],
                    device_id=(x, y, 1 - c), device_id_type=MESH))
        for cp in sends:
            cp.start()
        for w in range(nw):
            load[w].wait()
            mine[w].start()
        for cp in arrivals:
            cp.wait_recv()
        for cp in sends:
            cp.wait_send()
        for cp in mine:
            cp.wait()

    return pl.pallas_call(
        body, name="ag_pass_on",
        in_specs=[ANY] * (2 * nw), out_specs=[ANY] * nw,
        out_shape=[jax.ShapeDtypeStruct(a.shape, a.dtype) for a in lands],
        input_output_aliases={nw + i: i for i in range(nw)},
        scratch_shapes=[pltpu.SemaphoreType.DMA((3, nw)), pltpu.SemaphoreType.DMA((3, nw)),
                        pltpu.SemaphoreType.DMA((nw,))] + [pltpu.VMEM(s.shape, s.dtype) for s in shards],
        compiler_params=_cparams(),
    )(*shards, *lands)


def _rs_sibling_copies(p_refs, land_refs, send_sems, recv_sems):
    x, y, c = _my_place()
    return [pltpu.make_async_remote_copy(
        src_ref=p_refs[w].at[:, 1 - c], dst_ref=land_refs[w],
        send_sem=send_sems.at[w], recv_sem=recv_sems.at[w], device_id=(x, y, 1 - c), device_id_type=MESH)
        for w in range(len(p_refs))]


def _rs_sibling_start(parts, tag, after=()):
    nw = len(parts)
    sems = 2 * nw + len(after)

    def body(*refs):
        for cp in _rs_sibling_copies(refs[:nw], refs[nw:2 * nw], refs[sems], refs[sems + 1]):
            cp.start()
        refs[-1][...] = jnp.zeros_like(refs[-1])

    lands = [lax.empty(p.shape[:1] + p.shape[2:], BF16) for p in parts]
    out = pl.pallas_call(
        body, name="rs_sibling_start_%s" % tag,
        in_specs=[HBM] * (2 * nw) + [ANY] * len(after),
        out_specs=(SEM, SEM, *[HBM] * (2 * nw), pl.BlockSpec(memory_space=pltpu.VMEM)),
        out_shape=(pltpu.SemaphoreType.DMA((nw,)), pltpu.SemaphoreType.DMA((nw,)),
                   *[pltpu.HBM(a.shape, a.dtype) for a in list(parts) + lands],
                   jax.ShapeDtypeStruct((8, LANES), F32)),
        input_output_aliases={i: 2 + i for i in range(2 * nw)},
        compiler_params=pltpu.CompilerParams(has_side_effects=EFFECT),
    )(*[_in_hbm(a) for a in list(parts) + lands], *after)
    return out[0], out[1], out[2:2 + nw], out[2 + nw:2 + 2 * nw], out[-1]


def _rs_sibling_wait(send_sems, recv_sems, parts, lands, after, tag):
    nw = len(parts)

    def body(*refs):
        for cp in _rs_sibling_copies(refs[:nw], refs[nw:2 * nw], refs[2 * nw], refs[2 * nw + 1]):
            cp.wait_send()
            cp.wait_recv()

    out = pl.pallas_call(
        body, name="rs_sibling_wait_%s" % tag,
        in_specs=[HBM] * (2 * nw) + [SEM, SEM] + [ANY] * len(after),
        out_specs=[HBM] * (2 * nw),
        out_shape=[pltpu.HBM(a.shape, a.dtype) for a in list(parts) + list(lands)],
        input_output_aliases={i: i for i in range(2 * nw)},
        compiler_params=pltpu.CompilerParams(has_side_effects=EFFECT),
    )(*parts, *lands, send_sems, recv_sems, *after)
    return out[:nw], out[nw:]


def _rs_chip_sum(parts, gots, c):
    n = len(parts)

    def body(c_ref, *refs):
        for p_ref, g_ref, o_ref in zip(refs[:n], refs[n:2 * n], refs[2 * n:]):
            o_ref[...] = (p_ref[...].astype(F32) + g_ref[...].astype(F32)).astype(o_ref.dtype)

    mine = [pl.BlockSpec((None, None, p.shape[2], D_MODEL), lambda q, c_ref: (q, c_ref[0], 0, 0)) for p in parts]
    theirs = [pl.BlockSpec((None, g.shape[1], D_MODEL), lambda q, c_ref: (q, 0, 0)) for g in gots]
    return pl.pallas_call(
        body, name="rs_chip_sum",
        grid_spec=pltpu.PrefetchScalarGridSpec(
            num_scalar_prefetch=1, grid=(4,), in_specs=mine + theirs, out_specs=theirs),
        out_shape=[jax.ShapeDtypeStruct(g.shape, BF16) for g in gots],
        compiler_params=_cparams(("parallel",)),
    )(c, *parts, *gots)


def _other_chips(x, y):
    return [(1 - x, y), (x, 1 - y), (1 - x, 1 - y)]


def _rs_chip_copies(s_refs, land_refs, send_sems, recv_sems):
    x, y, c = _my_place()
    copies = []
    for k, chip in enumerate(_other_chips(x, y)):
        q = 2 * chip[0] + chip[1]
        copies += [pltpu.make_async_remote_copy(
            src_ref=s_refs[w].at[q], dst_ref=land_refs[w].at[k],
            send_sem=send_sems.at[k * len(s_refs) + w], recv_sem=recv_sems.at[k * len(s_refs) + w],
            device_id=(*chip, c), device_id_type=MESH)
            for w in range(len(s_refs))]
    return copies


def _rs_chip_start(sums, layer):
    nw = len(sums)

    def body(*refs):
        s_refs, land_refs = refs[:nw], refs[nw:2 * nw]
        send_sems, recv_sems = refs[2 * nw], refs[2 * nw + 1]
        token = refs[-1]
        for cp in _rs_chip_copies(s_refs, land_refs, send_sems, recv_sems):
            cp.start()
        token[...] = jnp.zeros_like(token)

    lands = [lax.empty((3,) + s.shape[1:], BF16) for s in sums]
    out = pl.pallas_call(
        body, name="rs_chip_start_%s" % layer,
        in_specs=[HBM] * (2 * nw),
        out_specs=(SEM, SEM, *[HBM] * (2 * nw), pl.BlockSpec(memory_space=pltpu.VMEM)),
        out_shape=(pltpu.SemaphoreType.DMA((3 * nw,)), pltpu.SemaphoreType.DMA((3 * nw,)),
                   *[pltpu.HBM(a.shape, a.dtype) for a in list(sums) + lands],
                   jax.ShapeDtypeStruct((8, LANES), F32)),
        input_output_aliases={i: 2 + i for i in range(2 * nw)},
        compiler_params=pltpu.CompilerParams(has_side_effects=EFFECT),
    )(*[_in_hbm(a) for a in list(sums) + lands])
    return out[0], out[1], out[2:2 + nw], out[2 + nw:2 + 2 * nw], out[-1]


def _rs_chip_wait(send_sems, recv_sems, sums, lands, after, layer):
    nw = len(sums)

    def body(*refs):
        s_refs, land_refs = refs[:nw], refs[nw:2 * nw]
        for cp in _rs_chip_copies(s_refs, land_refs, refs[2 * nw], refs[2 * nw + 1]):
            cp.wait_send()
            cp.wait_recv()

    out = pl.pallas_call(
        body, name="rs_chip_wait_%s" % layer,
        in_specs=[HBM] * (2 * nw) + [SEM, SEM] + [ANY] * len(after),
        out_specs=[HBM] * (2 * nw),
        out_shape=[pltpu.HBM(a.shape, a.dtype) for a in list(sums) + list(lands)],
        input_output_aliases={i: i for i in range(2 * nw)},
        compiler_params=pltpu.CompilerParams(has_side_effects=EFFECT),
    )(*sums, *lands, send_sems, recv_sems, *after)
    return out[:nw], out[nw:]


def _rs_finish(sums, gots, q, layer, into):
    n = len(sums)

    def body(q_ref, *refs):
        for s_ref, g_ref, o_ref in zip(refs[:n], refs[n:2 * n], refs[len(refs) - n:]):
            o_ref[...] = ((s_ref[...].astype(F32) + g_ref[0].astype(F32)) + g_ref[1].astype(F32)) + g_ref[2].astype(F32)

    rows = [s.shape[1] for s in sums]
    in_specs = [pl.BlockSpec((None, r, D_MODEL), lambda i, q_ref: (q_ref[0], 0, 0)) for r in rows]
    in_specs += [pl.BlockSpec((3, r, D_MODEL), lambda i, q_ref: (0, 0, 0)) for r in rows]
    args = [q, *sums, *gots]
    aliases = {}
    if into is not None:
        in_specs += [ANY] * n
        aliases = {len(args) + i: i for i in range(n)}
        args += list(into)
    return pl.pallas_call(
        body, name="rs_finish",
        grid_spec=pltpu.PrefetchScalarGridSpec(
            num_scalar_prefetch=1, grid=(1,), in_specs=in_specs,
            out_specs=[pl.BlockSpec((None, r, D_MODEL), lambda i, q_ref: (layer, 0, 0)) for r in rows]),
        out_shape=[jax.ShapeDtypeStruct((DEPTH, r, D_MODEL), F32) for r in rows],
        input_output_aliases=aliases,
        compiler_params=_cparams(("arbitrary",)),
    )(*args)


def _allreduce_small(vec, deps=()):
    R = vec.shape[0]
    assert R % (8 * N_DEV) == 0
    P = R // N_DEV
    nd = len(deps)

    def body(*refs):
        v_ref = refs[0]
        o_ref, buf, send1, recv1, send2, recv2 = refs[1 + nd:]
        x, y, c = _my_place()
        me = 4 * x + 2 * y + c

        def piece(ref, d):
            return ref.at[pl.ds(pl.multiple_of(d * P, 8), P), :]

        def peer(k):
            p = me ^ k
            return p, (p >> 2, (p >> 1) & 1, p & 1)

        scatter = []
        for k in range(1, N_DEV):
            p, where = peer(k)
            scatter.append(pltpu.make_async_remote_copy(
                src_ref=piece(v_ref, p), dst_ref=buf.at[k], send_sem=send1.at[k - 1], recv_sem=recv1.at[k - 1],
                device_id=where, device_id_type=MESH))
        for cp in scatter:
            cp.start()
        buf[0] = piece(v_ref, me)[...]
        for cp in scatter:
            cp.wait()
        acc = buf[me]
        for d in range(1, N_DEV):
            acc = acc + buf[me ^ d]
        piece(o_ref, me)[...] = acc
        spread, arrivals = [], []
        for k in range(1, N_DEV):
            p, where = peer(k)
            spread.append(pltpu.make_async_remote_copy(
                src_ref=piece(o_ref, me), dst_ref=piece(o_ref, me), send_sem=send2.at[k - 1], recv_sem=recv2.at[k - 1],
                device_id=where, device_id_type=MESH))
            arrivals.append(pltpu.make_async_remote_copy(
                src_ref=piece(o_ref, p), dst_ref=piece(o_ref, p), send_sem=send2.at[k - 1], recv_sem=recv2.at[k - 1],
                device_id=where, device_id_type=MESH))
        for cp in spread:
            cp.start()
        for cp in arrivals:
            cp.wait_recv()
        for cp in spread:
            cp.wait_send()

    sems = pltpu.SemaphoreType.DMA((N_DEV - 1,))
    return pl.pallas_call(
        body, name="allreduce_small",
        in_specs=[pl.BlockSpec(memory_space=pltpu.VMEM)] + [ANY] * nd, out_specs=pl.BlockSpec(memory_space=pltpu.VMEM),
        out_shape=jax.ShapeDtypeStruct((R, LANES), F32),
        scratch_shapes=[pltpu.VMEM((N_DEV, P, LANES), F32), sems, sems, sems, sems],
        compiler_params=_cparams(),
    )(vec, *deps)


def _pack(arrs):
    flat = jnp.concatenate([a.reshape(-1) for a in arrs])
    pad = (-flat.shape[0]) % (8 * N_DEV * LANES)
    return jnp.pad(flat, (0, pad)).reshape(-1, LANES)


def _unpack(packed, shapes):
    flat = packed.reshape(-1)
    out, off = [], 0
    for s in shapes:
        n = math.prod(s)
        out.append(flat[off:off + n].reshape(s))
        off += n
    return out


def kernel(x, w_in, w_conv, w_pool, pool_scale, sgu_ln_g, w_spatial, b_spatial, w_o, ln1_g, ln1_b, w_gate_up, w_down, ln2_g, ln2_b, loss_target, m_w_in, m_w_conv, m_w_pool, m_pool_scale, m_sgu_ln_g, m_w_spatial, m_b_spatial, m_w_o, m_ln1_g, m_ln1_b, m_w_gate_up, m_w_down, m_ln2_g, m_ln2_b, v_w_in, v_w_conv, v_w_pool, v_pool_scale, v_sgu_ln_g, v_w_spatial, v_b_spatial, v_w_o, v_ln1_g, v_ln1_b, v_w_gate_up, v_w_down, v_ln2_g, v_ln2_b):
    L = DEPTH
    T = x.shape[1]
    mx, my, mc = _my_place()
    dev = 4 * mx + 2 * my + mc
    xs = x[0]
    target = loss_target[0]

    conv_cols = w_conv.shape[2]
    w_conv_z = lax.dynamic_update_slice(jnp.zeros((L, 3, CONV_W), F32), w_conv, (0, 0, dev * conv_cols))
    w_conv_packed = _allreduce_small(_pack([w_conv_z]))
    w_conv_full = _unpack(w_conv_packed, [(L, 3, CONV_W)])[0]

    shards = (jnp.swapaxes(w_in, 1, 2).astype(BF16), jnp.swapaxes(w_gate_up, 1, 2).astype(BF16),
              w_o.astype(BF16), w_down.astype(BF16))
    first_gather = _ag_start_layer(shards, 0, [w_conv_packed])

    grad_x2, big_grads, small_grads = _local_step(
        xs, target, shards, first_gather, w_conv_full, w_pool, pool_scale, sgu_ln_g, w_spatial, b_spatial,
        ln1_g, ln1_b, ln2_g, ln2_b)
    grad_x = grad_x2[None]
    big_w = (w_in, w_gate_up, w_o, w_down)
    big_m = (m_w_in, m_w_gate_up, m_w_o, m_w_down)
    big_v = (v_w_in, v_w_gate_up, v_w_o, v_w_down)
    small_w = [w_conv_full, w_pool, pool_scale, sgu_ln_g, w_spatial, b_spatial, ln1_g, ln1_b, ln2_g, ln2_b]
    small_m = [m_w_conv, m_w_pool, m_pool_scale, m_sgu_ln_g, m_w_spatial, m_b_spatial, m_ln1_g, m_ln1_b, m_ln2_g, m_ln2_b]
    small_v = [v_w_conv, v_w_pool, v_pool_scale, v_sgu_ln_g, v_w_spatial, v_b_spatial, v_ln1_g, v_ln1_b, v_ln2_g, v_ln2_b]
    loss, grads, deltas, new_m, new_v = _reduce_and_update(
        big_grads, small_grads, big_w, big_m, big_v, small_w, small_m, small_v)
    return (loss, grad_x, *grads, *deltas, *new_m, *new_v)


def _ag_start_layer(shards, l, after):
    s_in, s_gu, s_o, s_dn = [s[l] for s in shards]
    first = _ag_start([s_in, s_o], "%da" % l, after=after)
    return first, _ag_start([s_gu, s_dn], "%db" % l, after=[first[4]])


def _ag_finish(gather, after, tag):
    send_sems, recv_sems, shards, lands, _ = gather
    shards, lands = _ag_wait(send_sems, recv_sems, shards, lands, after, tag)
    return _ag_pass_on(shards, lands)


def _rs_begin(parts, tag, after=()):
    return _rs_sibling_start([p.reshape(4, 2, p.shape[0] // N_DEV, D_MODEL) for p in parts], tag, after)


def _rs_continue(sibling_flight, after, c_arr, tag):
    send_sems, recv_sems, parts, lands, _ = sibling_flight
    parts, got = _rs_sibling_wait(send_sems, recv_sems, parts, lands, after, tag)
    return _rs_chip_start(_rs_chip_sum(parts, got, c_arr), tag)


def _local_step(xs, target, shards, gather, w_conv_full, w_pool, pool_scale, sgu_ln_g, w_spatial, b_spatial,
                ln1_g, ln1_b, ln2_g, ln2_b):
    L = DEPTH
    T = xs.shape[0]
    mx, my, mc = _my_place()
    c_arr = jnp.reshape(mc, (1,)).astype(jnp.int32)
    q_arr = jnp.reshape(2 * mx + my, (1,)).astype(jnp.int32)
    eye2 = jnp.eye(2, dtype=F32)
    wp = w_pool.reshape(L, 2, 2, HALF, HALF)
    wpool_bd = jnp.einsum("ltgcd,gh->ltgchd", wp, eye2).reshape(L, 2, LANES, LANES)
    wsp_t = w_spatial.reshape(L, 3, 2 * CHUNK, CHUNK)
    bias_t = jnp.repeat(jnp.swapaxes(b_spatial.reshape(L, 3, 2, CHUNK), 2, 3), HALF, axis=3)
    mixer_w = (w_conv_full, wpool_bd, pool_scale[:, None, :], sgu_ln_g[:, None, :], wsp_t, bias_t)
    g1, b1, g2, b2 = [a[:, None, :] for a in (ln1_g, ln1_b, ln2_g, ln2_b)]
    one, zero = jnp.ones((1, 1, D_MODEL), F32), jnp.zeros((1, 1, D_MODEL), F32)

    saved = []
    prev, pg, pb = xs, (one, 0), (zero, 0)
    prev_b = xs.astype(BF16)
    weights = []
    for l in range(L):
        g_in, g_o = _ag_finish(gather[0], [] if l == 0 else [prev_b], "%da" % l)
        proj = _mm(prev_b, g_in, "nt", F32, 512, IN_W, "mm_proj", deps=[gather[1][4]] if l == 0 else [])
        mixcat = _mixer_fwd(proj, *mixer_w, l)
        xhat1, rstd1, h_b = _mm_ln_fwd(mixcat, g_o, prev, pg, pb, (g1, l), (b1, l), "mm_wo_ln")
        send_sems, recv_sems, shards_b, lands_b, _ = gather[1]
        shards_b, lands_b = _ag_wait(send_sems, recv_sems, shards_b, lands_b, [h_b], "%db" % l)
        deps = []
        if l + 1 < L:
            gather = _ag_start_layer(shards, l + 1, [lands_b[0]])
            deps = [gather[1][4]]
        g_gu, g_dn = _ag_pass_on(shards_b, lands_b)
        weights.append((g_in, g_gu, g_o, g_dn))
        g_act, u_act, act = _mm_swiglu_fwd(h_b, g_gu, deps=deps)
        xhat2, rstd2, y_b = _mm_ln_fwd(act, g_dn, xhat1, (g1, l), (b1, l), (g2, l), (b2, l), "mm_down_ln")
        saved.append((prev_b, proj, mixcat, xhat1, rstd1, h_b, g_act, u_act, act, xhat2, rstd2))
        prev, pg, pb, prev_b = xhat2, (g2, l), (b2, l), y_b


    small = [None] * L
    big = None
    sibling_flight = None
    above = None
    for l in reversed(range(L)):
        prev_b, proj, mixcat, xhat1, rstd1, h_b, g_act, u_act, act, xhat2, rstd2 = saved[l]
        g_in, g_gu, g_o, g_dn = weights[l]
        chip_flight = None
        if above is None:
            loss_tile, dr2, dr2_b, dg2, db2 = _loss_ln_bwd(xhat2, rstd2, (g2, l), (b2, l), target)
        else:
            dr2, dr2_b, dg2, db2 = _mm_ln_bwd([above[0]], above[1], above[2], xhat2, rstd2, (g2, l),
                                              "mm_dx_ln", deps=[sibling_flight[4]])
            chip_flight = _rs_continue(sibling_flight, [dr2_b], c_arr, str(l + 1))
        dg_b, du_b = _mm_swiglu_bwd(dr2_b, g_dn, g_act, u_act, deps=[chip_flight[4]] if chip_flight else [])
        p_dn = _mm(act, dr2_b, "tn", BF16, DW_TM, D_MODEL // 2, "mm_dw_down")
        p_gu = _mm_tn_pair(dg_b, du_b, h_b, DW_TM, "mm_dw_gate_up")
        ffn_sibling = _rs_begin([p_gu, p_dn], "0b") if l == 0 else None
        dr1, dr1_b, dg1, db1, dmix = _mm_ln_bwd([dg_b, du_b], g_gu, dr2, xhat1, rstd1, (g1, l), "mm_dh_ln",
                                                deps=[ffn_sibling[4]] if l == 0 else [], w_back=g_o)
        ffn_flight = _rs_continue(ffn_sibling, [dr1_b], c_arr, "0b") if l == 0 else None
        p_o = _mm(mixcat, dr1_b, "tn", BF16, 512, D_MODEL, "mm_dw_o")
        dproj, dwc, dwp, dps, dlng, dwsp, dbias = _mixer_bwd(proj, dmix, *mixer_w, l,
                                                             deps=[ffn_flight[4]] if l == 0 else [])
        p_in = _mm(dproj, prev_b, "tn", BF16, IN_W, D_MODEL // 2, "mm_dw_in")
        small[l] = (dwc, dwp, dps, dlng, dwsp, dbias, dg1, db1, dg2, db2)
        above = (dproj, g_in, dr1)
        if chip_flight is not None:
            big = list(_rs_chip_finish(chip_flight, [p_in], q_arr, str(l + 1), l + 1, big))
        if l > 0:
            sibling_flight = _rs_begin([p_in, p_gu, p_o, p_dn], str(l))
        else:
            big[1], big[3] = _rs_chip_finish(ffn_flight, [p_in, p_o], q_arr, "0b", 0, [big[1], big[3]])

    def stack(i):
        return jnp.stack([small[l][i] for l in range(L)])

    dwp_bd = stack(1).reshape(L, 2, 2, HALF, 2, HALF)
    dwp_all = jnp.einsum("ltgchd,gh->ltgcd", dwp_bd, eye2).reshape(L, 4, HALF, HALF)
    dbs_all = jnp.swapaxes(stack(5)[:, :, :, :2], 2, 3).reshape(L, 6, CHUNK)
    small_grads = [stack(0), dwp_all, stack(2).reshape(L, POOL_W), stack(3).reshape(L, SGU_W),
                   stack(4).reshape(L, 6, CHUNK, CHUNK), dbs_all] + [stack(i).reshape(L, D_MODEL) for i in (6, 7, 8, 9)]
    small_grads.append(loss_tile[0, :1])
    packed_small = _allreduce_small(_pack(small_grads), deps=[big[1]])
    sibling_flight = _rs_begin([p_in, p_o], "0a", after=[packed_small])
    grad_x = _mm_ln_bwd([above[0]], above[1], above[2], None, None, None, "mm_dx_out", deps=[sibling_flight[4]])
    last_flight = _rs_continue(sibling_flight, [grad_x], c_arr, "0a")
    return grad_x, (big, last_flight, q_arr), (packed_small, [a.shape for a in small_grads])


def _rs_chip_finish(in_flight, after, q, tag, layer, into):
    send_sems, recv_sems, sums, lands, _ = in_flight
    sums, got = _rs_chip_wait(send_sems, recv_sems, sums, lands, after, tag)
    return _rs_finish(sums, got, q, layer, into)


def _reduce_and_update(big_grads, small_grads, big_w, big_m, big_v, small_w, small_m, small_v):
    L = DEPTH
    mx, my, mc = _my_place()
    dev = 4 * mx + 2 * my + mc
    conv_cols = CONV_W // N_DEV
    w_in, w_gate_up, w_o, w_down = big_w
    m_w_in, m_w_gate_up, m_w_o, m_w_down = big_m
    v_w_in, v_w_gate_up, v_w_o, v_w_down = big_v
    packed_g, small_shapes = small_grads
    big, last_flight, q_arr = big_grads

    def widen_conv(a):
        return lax.dynamic_update_slice(jnp.zeros((L, 3, CONV_W), F32), a, (0, 0, dev * conv_cols))

    small_m = [widen_conv(small_m[0])] + list(small_m[1:])
    small_v = [widen_conv(small_v[0])] + list(small_v[1:])
    pk_d, pk_m, pk_v = _adamw(_pack(small_w), packed_g, _pack(small_m), _pack(small_v), packed_g.shape[0] // 2)
    sg = _unpack(packed_g, small_shapes)
    sd = _unpack(pk_d, small_shapes)
    sm = _unpack(pk_m, small_shapes)
    sv = _unpack(pk_v, small_shapes)

    def conv_cols_of(a):
        return lax.dynamic_slice(a, (0, 0, dev * conv_cols), (L, 3, conv_cols))

    for lst in (sg, sd, sm, sv):
        lst[0] = conv_cols_of(lst[0])

    tr = lambda a: jnp.swapaxes(a, 1, 2)
    gt_gu, g_w_dn = big[1], big[3]
    d_gu, m_gu, v_gu = [tr(a) for a in _adamw(tr(w_gate_up), gt_gu, tr(m_w_gate_up), tr(v_w_gate_up), gt_gu.shape[1] // 2)]
    d_dn, m_dn, v_dn = _adamw(w_down, g_w_dn, m_w_down, v_w_down, w_down.shape[1])
    gt_in, g_w_o = _rs_chip_finish(last_flight, [d_gu, d_dn, pk_d], q_arr, "0a", 0, [big[0], big[2]])
    d_in, m_in, v_in = [tr(a) for a in _adamw(tr(w_in), gt_in, tr(m_w_in), tr(v_w_in), gt_in.shape[1])]
    d_o, m_o, v_o = _adamw(w_o, g_w_o, m_w_o, v_w_o, w_o.shape[1])
    g_w_in, g_w_gu = tr(gt_in), tr(gt_gu)

    def ordered(big_in, big_o, big_gu, big_dn, sm_list):
        return [big_in, sm_list[0], sm_list[1], sm_list[2], sm_list[3], sm_list[4], sm_list[5], big_o,
                sm_list[6], sm_list[7], big_gu, big_dn, sm_list[8], sm_list[9]]

    grads = ordered(g_w_in, g_w_o, g_w_gu, g_w_dn, sg)
    deltas = ordered(d_in, d_o, d_gu, d_dn, sd)
    new_m = ordered(m_in, m_o, m_gu, m_dn, sm)
    new_v = ordered(v_in, v_o, v_gu, v_dn, sv)
    return sg[10][0], grads, deltas, new_m, new_v
```

```python
import math

import jax
import jax.numpy as jnp
from jax import lax
from jax.experimental import pallas as pl
from jax.experimental.pallas import tpu as pltpu

F32 = jnp.float32
BF16 = jnp.bfloat16
MESH = pl.DeviceIdType.MESH

D_MODEL = 1024
DEPTH = 4
CONV_W = 384
POOL_W = 256
SGU_W = 384
IN_W = 3 * CONV_W + POOL_W + 2 * SGU_W
D_FF = 2816
CHUNK = 128
ALPHA = float((2 * DEPTH) ** 0.25)
LN_EPS = 1e-5
ADAM_LR, ADAM_B1, ADAM_B2, ADAM_EPS, ADAM_WD, ADAM_STEP = 0.001, 0.9, 0.999, 1e-08, 0.01, 10

N_DEV = 8
LANES = 128
HALF = 64
VMEM_LIMIT = 52 * 1024 * 1024

INV_SQRT2 = 0.7071067811865476
INV_SQRT_2PI = 0.3989422804014327


def _cparams(sem=None, **kw):
    if sem is not None:
        kw["dimension_semantics"] = sem
    return pltpu.CompilerParams(vmem_limit_bytes=VMEM_LIMIT, **kw)


_DN = {"nt": (((1,), (1,)), ((), ())), "tn": (((0,), (0,)), ((), ()))}


def _mm(a, b, mode, out_dtype, tm, tn, name, deps=()):
    if mode == "nt":
        (M, K), N = a.shape, b.shape[0]
        a_spec = pl.BlockSpec((tm, K), lambda i, j: (i, 0))
        b_spec = pl.BlockSpec((tn, K), lambda i, j: (j, 0))
    else:
        (K, M), N = a.shape, b.shape[1]
        a_spec = pl.BlockSpec((K, tm), lambda i, j: (0, i))
        b_spec = pl.BlockSpec((K, tn), lambda i, j: (0, j))
    assert M % tm == 0 and N % tn == 0, (M, N, K, tm, tn)
    nd = len(deps)

    def body(*refs):
        a_ref, b_ref, o_ref = refs[0], refs[1], refs[2 + nd]
        o_ref[...] = lax.dot_general(a_ref[...], b_ref[...], _DN[mode], preferred_element_type=F32).astype(o_ref.dtype)

    return pl.pallas_call(
        body,
        name=name,
        grid=(M // tm, N // tn),
        in_specs=[a_spec, b_spec] + [pl.BlockSpec(memory_space=pl.ANY)] * nd,
        out_specs=pl.BlockSpec((tm, tn), lambda i, j: (i, j)),
        out_shape=jax.ShapeDtypeStruct((M, N), out_dtype),
        compiler_params=_cparams(("parallel", "parallel")),
    )(a, b, *deps)


def _mm_tn_pair(a1, a2, b, tm, name):
    K, M = a1.shape
    N = b.shape[1]
    n1 = M // tm

    def body(a1_ref, a2_ref, b_ref, o_ref):
        i = pl.program_id(0)

        @pl.when(i < n1)
        def _():
            o_ref[...] = lax.dot_general(a1_ref[...], b_ref[...], _DN["tn"], preferred_element_type=F32).astype(o_ref.dtype)

        @pl.when(i >= n1)
        def _():
            o_ref[...] = lax.dot_general(a2_ref[...], b_ref[...], _DN["tn"], preferred_element_type=F32).astype(o_ref.dtype)

    return pl.pallas_call(
        body, name=name, grid=(2 * n1,),
        in_specs=[pl.BlockSpec((K, tm), lambda i: (0, jnp.minimum(i, n1 - 1))),
                  pl.BlockSpec((K, tm), lambda i: (0, jnp.maximum(i - n1, 0))),
                  pl.BlockSpec((K, N), lambda i: (0, 0))],
        out_specs=pl.BlockSpec((tm, N), lambda i: (i, 0)),
        out_shape=jax.ShapeDtypeStruct((2 * M, N), BF16),
        compiler_params=_cparams(("arbitrary",)),
    )(a1, a2, b)


LN_SUB = 256
LN_TM = 512


def _vec(v):
    arr, layer = v
    return arr, pl.BlockSpec((None, 1, D_MODEL), lambda *_: (layer, 0, 0))


def _mm_ln_fwd(a, b, prev, pg, pb, g, bias, name):
    T, K = a.shape
    tm = LN_TM

    def body(a_ref, b_ref, prev_ref, pg_ref, pb_ref, g_ref, bias_ref, xhat_ref, rstd_ref, y_ref):
        for s in range(tm // LN_SUB):
            rows = slice(s * LN_SUB, (s + 1) * LN_SUB)
            mm = jnp.dot(a_ref[rows, :], b_ref[...], preferred_element_type=F32)
            r = ALPHA * (prev_ref[rows, :] * pg_ref[...] + pb_ref[...]) + mm
            mu = jnp.mean(r, axis=-1, keepdims=True)
            xc = r - mu
            var = jnp.mean(xc * xc, axis=-1, keepdims=True)
            rstd = lax.rsqrt(var + LN_EPS)
            xhat = xc * rstd
            xhat_ref[rows, :] = xhat
            rstd_ref[rows, :] = rstd
            y_ref[rows, :] = (xhat * g_ref[...] + bias_ref[...]).astype(y_ref.dtype)

    row = pl.BlockSpec((tm, D_MODEL), lambda i: (i, 0))
    vecs = [_vec(v) for v in (pg, pb, g, bias)]
    return pl.pallas_call(
        body, name=name, grid=(T // tm,),
        in_specs=[pl.BlockSpec((tm, K), lambda i: (i, 0)),
                  pl.BlockSpec((K, D_MODEL), lambda i: (0, 0), pipeline_mode=pl.Buffered(1)),
                  row] + [s for _, s in vecs],
        out_specs=[row, pl.BlockSpec((tm, 1), lambda i: (i, 0)), row],
        out_shape=[jax.ShapeDtypeStruct((T, D_MODEL), F32), jax.ShapeDtypeStruct((T, 1), F32),
                   jax.ShapeDtypeStruct((T, D_MODEL), BF16)],
        compiler_params=_cparams(("parallel",)),
    )(a, b, prev, *[a_ for a_, _ in vecs])


def _mm_ln_bwd(a_list, b, dres, xhat, rstd, g, name, deps=(), w_back=None):
    T = a_list[0].shape[0]
    tm = LN_TM
    na, nd = len(a_list), len(deps)
    ks = [a.shape[1] for a in a_list]
    last = xhat is None
    nout = 1 if last else (5 if w_back is not None else 4)

    def body(*refs):
        a_refs, b_ref, dres_ref = refs[:na], refs[na], refs[na + 1]
        if not last:
            xhat_ref, rstd_ref, g_ref = refs[na + 2:na + 5]
            dr_ref, drb_ref, dg_ref, db_ref = refs[len(refs) - nout:len(refs) - nout + 4]

            @pl.when(pl.program_id(0) == 0)
            def _():
                dg_ref[...] = jnp.zeros_like(dg_ref)
                db_ref[...] = jnp.zeros_like(db_ref)

        for s in range(tm // LN_SUB):
            rows = slice(s * LN_SUB, (s + 1) * LN_SUB)
            mm, off = None, 0
            for a_ref, k in zip(a_refs, ks):
                part = jnp.dot(a_ref[rows, :], b_ref[off:off + k, :], preferred_element_type=F32)
                mm = part if mm is None else mm + part
                off += k
            dy = ALPHA * dres_ref[rows, :] + mm
            if last:
                refs[-1][rows, :] = dy
                continue
            xhat_v = xhat_ref[rows, :]
            dg_ref[...] += jnp.sum(dy * xhat_v, axis=0, keepdims=True)
            db_ref[...] += jnp.sum(dy, axis=0, keepdims=True)
            dxh = dy * g_ref[...]
            m1 = jnp.mean(dxh, axis=-1, keepdims=True)
            m2 = jnp.mean(dxh * xhat_v, axis=-1, keepdims=True)
            dr = rstd_ref[rows, :] * (dxh - m1 - xhat_v * m2)
            dr_ref[rows, :] = dr
            dr_b = dr.astype(drb_ref.dtype)
            drb_ref[rows, :] = dr_b
            if w_back is not None:
                refs[-1][rows, :] = lax.dot_general(dr_b, refs[na + 5][...], _DN["nt"], preferred_element_type=F32)

    row = pl.BlockSpec((tm, D_MODEL), lambda i: (i, 0))
    vec = pl.BlockSpec((1, D_MODEL), lambda i: (0, 0))
    in_specs = [pl.BlockSpec((tm, k), lambda i: (i, 0)) for k in ks]
    in_specs += [pl.BlockSpec((sum(ks), D_MODEL), lambda i: (0, 0), pipeline_mode=pl.Buffered(1)), row]
    args = list(a_list) + [b, dres]
    if last:
        out_specs, out_shape = row, jax.ShapeDtypeStruct((T, D_MODEL), F32)
    else:
        g_arr, g_spec = _vec(g)
        in_specs += [row, pl.BlockSpec((tm, 1), lambda i: (i, 0)), g_spec]
        args += [xhat, rstd, g_arr]
        out_specs = [row, row, vec, vec]
        out_shape = [jax.ShapeDtypeStruct((T, D_MODEL), F32), jax.ShapeDtypeStruct((T, D_MODEL), BF16),
                     jax.ShapeDtypeStruct((1, D_MODEL), F32), jax.ShapeDtypeStruct((1, D_MODEL), F32)]
        if w_back is not None:
            in_specs.append(pl.BlockSpec(w_back.shape, lambda i: (0, 0), pipeline_mode=pl.Buffered(1)))
            args.append(w_back)
            out_specs.append(row)
            out_shape.append(jax.ShapeDtypeStruct((T, w_back.shape[0]), F32))
    return pl.pallas_call(
        body, name=name, grid=(T // tm,),
        in_specs=in_specs + [pl.BlockSpec(memory_space=pl.ANY)] * nd,
        out_specs=out_specs, out_shape=out_shape,
        compiler_params=_cparams(("parallel",) if last else ("arbitrary",)),
    )(*args, *deps)


DW_TM = 1408
FF_TN = 256
FF_TM = 2048
SAVED_GU = BF16


def _mm_swiglu_fwd(h, w_gu, deps=()):
    T = h.shape[0]
    tm = min(T, FF_TM)
    nj = D_FF // FF_TN
    nd = len(deps)

    def body(*refs):
        h_ref, wg_ref, wu_ref = refs[:3]
        g_ref, u_ref, act_ref = refs[3 + nd:]
        hv = h_ref[...]
        gv = lax.dot_general(hv, wg_ref[...], _DN["nt"], preferred_element_type=F32)
        uv = lax.dot_general(hv, wu_ref[...], _DN["nt"], preferred_element_type=F32)
        g_ref[...] = gv.astype(g_ref.dtype)
        u_ref[...] = uv.astype(u_ref.dtype)
        act_ref[...] = (gv * jax.nn.sigmoid(gv) * uv).astype(act_ref.dtype)

    tile = pl.BlockSpec((tm, FF_TN), lambda j, i: (i, j))
    return pl.pallas_call(
        body, name="mm_gate_up_swiglu", grid=(nj, T // tm),
        in_specs=[pl.BlockSpec((tm, D_MODEL), lambda j, i: (i, 0)),
                  pl.BlockSpec((FF_TN, D_MODEL), lambda j, i: (j, 0)),
                  pl.BlockSpec((FF_TN, D_MODEL), lambda j, i: (j + nj, 0))] + [pl.BlockSpec(memory_space=pl.ANY)] * nd,
        out_specs=[tile, tile, tile],
        out_shape=[jax.ShapeDtypeStruct((T, D_FF), SAVED_GU), jax.ShapeDtypeStruct((T, D_FF), SAVED_GU),
                   jax.ShapeDtypeStruct((T, D_FF), BF16)],
        compiler_params=_cparams(("parallel", "parallel")),
    )(h, w_gu, w_gu, *deps)


def _mm_swiglu_bwd(dr, w_dn, g, u, deps=()):
    T = dr.shape[0]
    tm = min(T, FF_TM)

    def body(*refs):
        dr_ref, w_ref, g_ref, u_ref = refs[:4]
        dg_ref, du_ref = refs[-2:]
        da = lax.dot_general(dr_ref[...], w_ref[...], _DN["nt"], preferred_element_type=F32)
        gv, uv = g_ref[...].astype(F32), u_ref[...].astype(F32)
        s = jax.nn.sigmoid(gv)
        du_ref[...] = (da * (gv * s)).astype(du_ref.dtype)
        dg_ref[...] = (da * uv * (s * (1.0 + gv * (1.0 - s)))).astype(dg_ref.dtype)

    tile = pl.BlockSpec((tm, FF_TN), lambda j, i: (i, j))
    return pl.pallas_call(
        body, name="mm_dact_swiglu", grid=(D_FF // FF_TN, T // tm),
        in_specs=[pl.BlockSpec((tm, D_MODEL), lambda j, i: (i, 0)), pl.BlockSpec((FF_TN, D_MODEL), lambda j, i: (j, 0)),
                  tile, tile] + [ANY] * len(deps),
        out_specs=[tile, tile],
        out_shape=[jax.ShapeDtypeStruct((T, D_FF), BF16)] * 2,
        compiler_params=_cparams(("parallel", "parallel")),
    )(dr, w_dn, g, u, *deps)


def _gelu(x):
    return 0.5 * x * (1.0 + lax.erf(x * INV_SQRT2))


def _gelu_grad(x):
    return 0.5 * (1.0 + lax.erf(x * INV_SQRT2)) + x * (jnp.exp(-0.5 * x * x) * INV_SQRT_2PI)


def _shift_down(z, k):
    row = lax.broadcasted_iota(jnp.int32, z.shape, 0)
    return jnp.where(row >= k, pltpu.roll(z, k, 0), 0.0)


def _shift_up(z, k):
    n = z.shape[0]
    row = lax.broadcasted_iota(jnp.int32, z.shape, 0)
    return jnp.where(row < n - k, pltpu.roll(z, n - k, 0), 0.0)


def _lo_mask(shape):
    return lax.broadcasted_iota(jnp.int32, shape, len(shape) - 1) < HALF


def _seg_mean(x, lo):
    a = jnp.sum(jnp.where(lo, x, 0.0), axis=-1, keepdims=True)
    b = jnp.sum(jnp.where(lo, 0.0, x), axis=-1, keepdims=True)
    return jnp.where(lo, a, b) * (1.0 / HALF)


def _pool_windows(first):
    lo = _lo_mask((1, LANES))
    return jnp.where(first, jnp.where(lo, 2.0, 4.0), jnp.where(lo, 8.0, 16.0)), lo


def _pool_mean_minus_token(p, first):
    wl, lo = _pool_windows(first)
    s2 = p + _shift_down(p, 1)
    s4 = s2 + _shift_down(s2, 2)
    s8 = s4 + _shift_down(s4, 4)
    s16 = s8 + _shift_down(s8, 8)
    win = jnp.where(first, jnp.where(lo, s2, s4), jnp.where(lo, s8, s16))
    t1 = (lax.broadcasted_iota(jnp.int32, p.shape, 0) + 1).astype(F32)
    count = jnp.minimum(t1, wl)
    return win / count - p, count


SGU_UNROLL_FWD = 4
SGU_UNROLL_BWD = 2


def _tril_keep():
    r = lax.broadcasted_iota(jnp.int32, (2 * CHUNK, CHUNK), 0)
    s = lax.broadcasted_iota(jnp.int32, (2 * CHUNK, CHUNK), 1)
    return s <= (r & (CHUNK - 1))


def _sgu_chunk_fwd(u, v, g, wm, bias, lo):
    ug = _gelu(u)
    vg = _gelu(v)
    mu = _seg_mean(vg, lo)
    xc = vg - mu
    var = _seg_mean(xc * xc, lo)
    rstd = lax.rsqrt(var + LN_EPS)
    vn = xc * rstd
    vh = (vn * g).astype(BF16)
    mm2 = jnp.dot(wm, vh, preferred_element_type=F32)
    mixed = jnp.where(lo, mm2[:CHUNK], mm2[CHUNK:]) + bias
    return ug, vn, rstd, vh, mixed


def _mixer_fwd(proj, wconv, wpool_bd, pscale, lng, wsp, bias, layer):
    T = proj.shape[0]
    nchunk = T // CHUNK

    def body(a_ref, b_ref, c_ref, wc_ref, wp_ref, ps_ref, lng_ref, wsp_ref, bias_ref, o_ref):
        j = pl.program_id(0)

        @pl.when(j < 3)
        def _conv():
            z = c_ref[...] * a_ref[...]
            w = wc_ref[...]
            y = w[0:1] * _shift_down(z, 2) + w[1:2] * _shift_down(z, 1) + w[2:3] * z
            o_ref[...] = (b_ref[...] * y).astype(o_ref.dtype)

        @pl.when((j >= 3) & (j < 5))
        def _pool():
            d, _ = _pool_mean_minus_token(a_ref[...], j == 3)
            y = jnp.dot(d.astype(BF16), wp_ref[...].astype(BF16), preferred_element_type=F32)
            o_ref[...] = (y * ps_ref[...]).astype(o_ref.dtype)

        @pl.when(j >= 5)
        def _sgu():
            lo = _lo_mask((CHUNK, LANES))
            wm = jnp.where(_tril_keep(), wsp_ref[...], 0.0).astype(BF16)
            bias_t = bias_ref[...]
            g = lng_ref[...]

            def chunk(n, carry):
                rows = pl.ds(pl.multiple_of(n * CHUNK, CHUNK), CHUNK)
                ug, _, _, _, mixed = _sgu_chunk_fwd(a_ref[rows, :], b_ref[rows, :], g, wm, bias_t, lo)
                o_ref[rows, :] = (ug * mixed).astype(o_ref.dtype)
                return carry

            lax.fori_loop(0, nchunk, chunk, 0, unroll=SGU_UNROLL_FWD)

    def col(f):
        return lambda j: (0, f(j))

    clip = lambda v, lo, hi: jnp.minimum(jnp.maximum(v, lo), hi)
    return pl.pallas_call(
        body,
        name="mixer_fwd",
        grid=(8,),
        in_specs=[
            pl.BlockSpec((T, LANES), col(lambda j: jnp.where(j < 3, j, jnp.where(j < 5, j + 6, j + 6)))),
            pl.BlockSpec((T, LANES), col(lambda j: jnp.where(j < 3, j + 3, jnp.where(j < 5, 5, j + 9)))),
            pl.BlockSpec((T, LANES), col(lambda j: jnp.where(j < 3, j + 6, 8))),
            pl.BlockSpec((None, 3, LANES), lambda j: (layer, 0, clip(j, 0, 2))),
            pl.BlockSpec((None, None, LANES, LANES), lambda j: (layer, clip(j - 3, 0, 1), 0, 0)),
            pl.BlockSpec((None, 1, LANES), lambda j: (layer, 0, clip(j - 3, 0, 1))),
            pl.BlockSpec((None, 1, LANES), lambda j: (layer, 0, clip(j - 5, 0, 2))),
            pl.BlockSpec((None, None, 2 * CHUNK, CHUNK), lambda j: (layer, clip(j - 5, 0, 2), 0, 0)),
            pl.BlockSpec((None, None, CHUNK, LANES), lambda j: (layer, clip(j - 5, 0, 2), 0, 0)),
        ],
        out_specs=pl.BlockSpec((T, LANES), lambda j: (0, j)),
        out_shape=jax.ShapeDtypeStruct((T, D_MODEL), BF16),
        compiler_params=_cparams(("arbitrary",)),
    )(proj, proj, proj, wconv, wpool_bd, pscale, lng, wsp, bias)


def _mixer_bwd(proj, dmix, wconv, wpool_bd, pscale, lng, wsp, bias, layer, deps=()):
    T = proj.shape[0]
    nchunk = T // CHUNK

    def body(*refs):
        a_ref, b_ref, c_ref, dm_ref, wc_ref, wp_ref, ps_ref, lng_ref, wsp_ref, bias_ref = refs[:10]
        o_ref, dwc_ref, dwp_ref, dps_ref, dlng_ref, dwsp_ref, dbias_ref, keep1, keep2 = refs[10 + len(deps):]
        k = pl.program_id(0)

        @pl.when(k < 3)
        def _conv():
            xa, gb, gc, dya = a_ref[...], b_ref[...], c_ref[...], dm_ref[...]
            w = wc_ref[...]
            z = gc * xa
            z1 = _shift_down(z, 1)
            z2 = _shift_down(z, 2)
            y = w[0:1] * z2 + w[1:2] * z1 + w[2:3] * z
            dyv = dya * gb
            dz = w[2:3] * dyv + w[1:2] * _shift_up(dyv, 1) + w[0:1] * _shift_up(dyv, 2)
            dwc_ref[0:1, :] = jnp.sum(dyv * z2, axis=0, keepdims=True)
            dwc_ref[1:2, :] = jnp.sum(dyv * z1, axis=0, keepdims=True)
            dwc_ref[2:3, :] = jnp.sum(dyv * z, axis=0, keepdims=True)
            o_ref[...] = (dz * gc).astype(o_ref.dtype)
            keep1[k] = (dya * y).astype(keep1.dtype)
            keep1[k + 3] = (dz * xa).astype(keep1.dtype)

        @pl.when((k >= 3) & (k < 9))
        def _emit_gb_gc():
            o_ref[...] = keep1[k - 3]

        @pl.when((k >= 9) & (k < 11))
        def _pool():
            first = k == 9
            p, dyb = a_ref[...], dm_ref[...]
            d, count = _pool_mean_minus_token(p, first)
            w2 = wp_ref[...].astype(BF16)
            db = d.astype(BF16)
            y = jnp.dot(db, w2, preferred_element_type=F32)
            dps_ref[...] = jnp.sum(dyb * y, axis=0, keepdims=True)
            dyv = (dyb * ps_ref[...]).astype(BF16)
            dd = lax.dot_general(dyv, w2, _DN["nt"], preferred_element_type=F32)
            dwp_ref[...] = lax.dot_general(db, dyv, _DN["tn"], preferred_element_type=F32)
            dwin = dd / count
            a2 = dwin + _shift_up(dwin, 1)
            a4 = a2 + _shift_up(a2, 2)
            a8 = a4 + _shift_up(a4, 4)
            a16 = a8 + _shift_up(a8, 8)
            _, lo = _pool_windows(first)
            back = jnp.where(first, jnp.where(lo, a2, a4), jnp.where(lo, a8, a16))
            o_ref[...] = (back - dd).astype(o_ref.dtype)

        @pl.when((k >= 11) & (k < 14))
        def _sgu():
            lo = _lo_mask((CHUNK, LANES))
            keep = _tril_keep()
            wm = jnp.where(keep, wsp_ref[...], 0.0).astype(BF16)
            bias_t = bias_ref[...]
            g = lng_ref[...]
            dwsp_ref[...] = jnp.zeros_like(dwsp_ref)
            dbias_ref[...] = jnp.zeros_like(dbias_ref)
            dlng_ref[...] = jnp.zeros_like(dlng_ref)

            def chunk(n, carry):
                rows = pl.ds(pl.multiple_of(n * CHUNK, CHUNK), CHUNK)
                u, v, dyc = a_ref[rows, :], b_ref[rows, :], dm_ref[rows, :]
                ug, vn, rstd, vh, mixed = _sgu_chunk_fwd(u, v, g, wm, bias_t, lo)
                dmx = dyc * ug
                o_ref[rows, :] = (dyc * mixed * _gelu_grad(u)).astype(o_ref.dtype)
                dbias_ref[...] += dmx
                dst = jnp.concatenate([jnp.where(lo, dmx, 0.0), jnp.where(lo, 0.0, dmx)], axis=0).astype(BF16)
                dwsp_ref[...] += lax.dot_general(dst, vh, _DN["nt"], preferred_element_type=F32)
                dvh = lax.dot_general(wm, dst, _DN["tn"], preferred_element_type=F32)
                dlng_ref[...] += jnp.sum(dvh * vn, axis=0, keepdims=True)
                dvn = dvh * g
                m1 = _seg_mean(dvn, lo)
                m2 = _seg_mean(dvn * vn, lo)
                dvg = rstd * (dvn - m1 - vn * m2)
                keep2[k - 11, rows, :] = (dvg * _gelu_grad(v)).astype(keep2.dtype)
                return carry

            lax.fori_loop(0, nchunk, chunk, 0, unroll=SGU_UNROLL_BWD)
            dwsp_ref[...] = jnp.where(keep, dwsp_ref[...], 0.0)
            dbt = dbias_ref[...]
            lane = lax.broadcasted_iota(jnp.int32, (CHUNK, LANES), 1)
            sa = jnp.sum(jnp.where(lo, dbt, 0.0), axis=-1, keepdims=True)
            sb = jnp.sum(jnp.where(lo, 0.0, dbt), axis=-1, keepdims=True)
            dbias_ref[...] = jnp.where(lane == 0, sa, jnp.where(lane == 1, sb, 0.0))

        @pl.when(k >= 14)
        def _emit_v():
            o_ref[...] = keep2[k - 14]

    def col(f):
        return lambda k: (0, f(k))

    clip = lambda v, lo, hi: jnp.minimum(jnp.maximum(v, lo), hi)
    view_a = lambda k: jnp.where(k < 3, k, jnp.where(k < 9, 2, jnp.where(k < 14, k, 13)))
    view_b = lambda k: jnp.where(k < 3, k + 3, jnp.where(k < 11, 5, jnp.where(k < 14, k + 3, 16)))
    view_c = lambda k: jnp.where(k < 3, k + 6, 8)
    view_dm = lambda k: jnp.where(k < 3, k, jnp.where(k < 9, 2, jnp.where(k < 14, k - 6, 7)))
    return pl.pallas_call(
        body,
        name="mixer_bwd",
        grid=(17,),
        in_specs=[
            pl.BlockSpec((T, LANES), col(view_a)),
            pl.BlockSpec((T, LANES), col(view_b)),
            pl.BlockSpec((T, LANES), col(view_c)),
            pl.BlockSpec((T, LANES), col(view_dm)),
            pl.BlockSpec((None, 3, LANES), lambda k: (layer, 0, clip(k, 0, 2))),
            pl.BlockSpec((None, None, LANES, LANES), lambda k: (layer, clip(k - 9, 0, 1), 0, 0)),
            pl.BlockSpec((None, 1, LANES), lambda k: (layer, 0, clip(k - 9, 0, 1))),
            pl.BlockSpec((None, 1, LANES), lambda k: (layer, 0, clip(k - 11, 0, 2))),
            pl.BlockSpec((None, None, 2 * CHUNK, CHUNK), lambda k: (layer, clip(k - 11, 0, 2), 0, 0)),
            pl.BlockSpec((None, None, CHUNK, LANES), lambda k: (layer, clip(k - 11, 0, 2), 0, 0)),
        ] + [pl.BlockSpec(memory_space=pl.ANY)] * len(deps),
        out_specs=[
            pl.BlockSpec((T, LANES), lambda k: (0, k)),
            pl.BlockSpec((3, LANES), col(lambda k: clip(k, 0, 2))),
            pl.BlockSpec((None, LANES, LANES), lambda k: (clip(k - 9, 0, 1), 0, 0)),
            pl.BlockSpec((1, LANES), col(lambda k: clip(k - 9, 0, 1))),
            pl.BlockSpec((1, LANES), col(lambda k: clip(k - 11, 0, 2))),
            pl.BlockSpec((None, 2 * CHUNK, CHUNK), lambda k: (clip(k - 11, 0, 2), 0, 0)),
            pl.BlockSpec((None, CHUNK, LANES), lambda k: (clip(k - 11, 0, 2), 0, 0)),
        ],
        out_shape=[
            jax.ShapeDtypeStruct((T, IN_W), BF16),
            jax.ShapeDtypeStruct((3, CONV_W), F32),
            jax.ShapeDtypeStruct((2, LANES, LANES), F32),
            jax.ShapeDtypeStruct((1, POOL_W), F32),
            jax.ShapeDtypeStruct((1, SGU_W), F32),
            jax.ShapeDtypeStruct((3, 2 * CHUNK, CHUNK), F32),
            jax.ShapeDtypeStruct((3, CHUNK, LANES), F32),
        ],
        scratch_shapes=[pltpu.VMEM((6, T, LANES), BF16), pltpu.VMEM((3, T, LANES), BF16)],
        compiler_params=_cparams(("arbitrary",)),
    )(proj, proj, proj, dmix, wconv, wpool_bd, pscale, lng, wsp, bias, *deps)


def _loss_ln_bwd(xhat, rstd, g, b, target, tm=256):
    T = xhat.shape[0]

    def body(xhat_ref, rstd_ref, g_ref, b_ref, t_ref, loss_ref, dr_ref, drb_ref, dg_ref, db_ref):
        xhat_v = xhat_ref[...]
        err = xhat_v * g_ref[...] + b_ref[...] - t_ref[...]
        dy = err * (1.0 / D_MODEL)

        @pl.when(pl.program_id(0) == 0)
        def _():
            loss_ref[...] = jnp.zeros_like(loss_ref)
            dg_ref[...] = jnp.zeros_like(dg_ref)
            db_ref[...] = jnp.zeros_like(db_ref)

        part = jnp.sum(jnp.sum(err * err, axis=-1, keepdims=True), axis=0, keepdims=True)
        loss_ref[...] += jnp.broadcast_to(part * (0.5 / D_MODEL), loss_ref.shape)
        dg_ref[...] += jnp.sum(dy * xhat_v, axis=0, keepdims=True)
        db_ref[...] += jnp.sum(dy, axis=0, keepdims=True)
        dxh = dy * g_ref[...]
        m1 = jnp.mean(dxh, axis=-1, keepdims=True)
        m2 = jnp.mean(dxh * xhat_v, axis=-1, keepdims=True)
        dr = rstd_ref[...] * (dxh - m1 - xhat_v * m2)
        dr_ref[...] = dr
        drb_ref[...] = dr.astype(drb_ref.dtype)

    row = pl.BlockSpec((tm, D_MODEL), lambda i: (i, 0))
    vec = pl.BlockSpec((1, D_MODEL), lambda i: (0, 0))
    (g_arr, g_spec), (b_arr, b_spec) = _vec(g), _vec(b)
    return pl.pallas_call(
        body,
        name="loss_ln_bwd",
        grid=(T // tm,),
        in_specs=[row, pl.BlockSpec((tm, 1), lambda i: (i, 0)), g_spec, b_spec, row],
        out_specs=[pl.BlockSpec((8, LANES), lambda i: (0, 0)), row, row, vec, vec],
        out_shape=[jax.ShapeDtypeStruct((8, LANES), F32),
                   jax.ShapeDtypeStruct((T, D_MODEL), F32), jax.ShapeDtypeStruct((T, D_MODEL), BF16),
                   jax.ShapeDtypeStruct((1, D_MODEL), F32), jax.ShapeDtypeStruct((1, D_MODEL), F32)],
        compiler_params=_cparams(("arbitrary",)),
    )(xhat, rstd, g_arr, b_arr, target)


def _adamw(w, g, m, v, tr):
    R, C = w.shape[-2:]
    assert R % tr == 0
    c1 = 1.0 - ADAM_B1 ** ADAM_STEP
    c2 = 1.0 - ADAM_B2 ** ADAM_STEP

    def body(w_ref, g_ref, m_ref, v_ref, d_ref, mo_ref, vo_ref):
        gv = g_ref[...]
        mn = ADAM_B1 * m_ref[...] + (1.0 - ADAM_B1) * gv
        vn = ADAM_B2 * v_ref[...] + (1.0 - ADAM_B2) * (gv * gv)
        d_ref[...] = -ADAM_LR * ((mn / c1) / (jnp.sqrt(vn / c2) + ADAM_EPS) + ADAM_WD * w_ref[...])
        mo_ref[...] = mn
        vo_ref[...] = vn

    if w.ndim == 2:
        grid, blk = (R // tr,), pl.BlockSpec((tr, C), lambda i: (i, 0))
    else:
        grid, blk = (w.shape[0], R // tr), pl.BlockSpec((None, tr, C), lambda l, i: (l, i, 0))
    return pl.pallas_call(
        body, name="adamw", grid=grid, in_specs=[blk] * 4, out_specs=[blk] * 3,
        out_shape=[jax.ShapeDtypeStruct(w.shape, F32)] * 3, compiler_params=_cparams(("parallel",) * len(grid)),
    )(w, g, m, v)


def _my_place():
    return lax.axis_index("x"), lax.axis_index("y"), lax.axis_index("c")


ANY = pl.BlockSpec(memory_space=pl.ANY)
HBM = pl.BlockSpec(memory_space=pltpu.HBM)
SEM = pl.BlockSpec(memory_space=pltpu.SEMAPHORE)
EFFECT = pltpu.SideEffectType.DATAFLOW_SIDE_EFFECTING


def _in_hbm(a):
    return pltpu.with_memory_space_constraint(a, pltpu.HBM)


def _block_rows(ref, dev):
    r = ref.shape[0] // N_DEV
    start = pl.multiple_of((4 * dev[0] + 2 * dev[1] + dev[2]) * r, 16)
    return ref.at[pl.ds(start, r), :]


def _ag_first_copies(s_refs, land_refs, send_sems, recv_sems, receiving):
    x, y, c = _my_place()
    peers = [(x, y, 1 - c)] + [(*chip, c) for chip in _other_chips(x, y)]
    copies = []
    for k, peer in enumerate(peers):
        block = peer if receiving else (x, y, c)
        copies += [pltpu.make_async_remote_copy(
            src_ref=s_refs[w], dst_ref=_block_rows(land_refs[w], block),
            send_sem=send_sems.at[k * len(s_refs) + w], recv_sem=recv_sems.at[k * len(s_refs) + w],
            device_id=peer, device_id_type=MESH)
            for w in range(len(s_refs))]
    return copies


def _ag_start(shards, layer, after=()):
    nw = len(shards)

    def body(*refs):
        s_refs, land_refs = refs[:nw], refs[nw:2 * nw]
        token = refs[-1]
        sems = 2 * nw + len(after)
        for cp in _ag_first_copies(s_refs, land_refs, refs[sems], refs[sems + 1], False):
            cp.start()
        token[...] = jnp.zeros_like(token)

    lands = [lax.empty((N_DEV * s.shape[0], D_MODEL), BF16) for s in shards]
    out = pl.pallas_call(
        body, name="ag_start_%s" % layer,
        in_specs=[HBM] * (2 * nw) + [ANY] * len(after),
        out_specs=(SEM, SEM, *[HBM] * (2 * nw), pl.BlockSpec(memory_space=pltpu.VMEM)),
        out_shape=(pltpu.SemaphoreType.DMA((4 * nw,)), pltpu.SemaphoreType.DMA((4 * nw,)),
                   *[pltpu.HBM(a.shape, a.dtype) for a in list(shards) + lands],
                   jax.ShapeDtypeStruct((8, LANES), F32)),
        input_output_aliases={i: 2 + i for i in range(2 * nw)},
        compiler_params=pltpu.CompilerParams(has_side_effects=EFFECT),
    )(*[_in_hbm(a) for a in list(shards) + lands], *after)
    return out[0], out[1], out[2:2 + nw], out[2 + nw:2 + 2 * nw], out[-1]


def _ag_wait(send_sems, recv_sems, shards, lands, after, layer):
    nw = len(shards)

    def body(*refs):
        s_refs, land_refs = refs[:nw], refs[nw:2 * nw]
        for cp in _ag_first_copies(s_refs, land_refs, refs[2 * nw], refs[2 * nw + 1], True):
            cp.wait_send()
            cp.wait_recv()

    out = pl.pallas_call(
        body, name="ag_wait_%s" % layer,
        in_specs=[HBM] * (2 * nw) + [SEM, SEM] + [ANY] * len(after),
        out_specs=[HBM] * (2 * nw),
        out_shape=[pltpu.HBM(a.shape, a.dtype) for a in list(shards) + list(lands)],
        input_output_aliases={i: i for i in range(2 * nw)},
        compiler_params=pltpu.CompilerParams(has_side_effects=EFFECT),
    )(*shards, *lands, send_sems, recv_sems, *after)
    return out[:nw], out[nw:]


def _ag_pass_on(shards, lands):
    nw = len(shards)

    def body(*refs):
        s_refs, g_refs = refs[:nw], refs[2 * nw:3 * nw]
        send_sems, recv_sems, local_sems = refs[3 * nw:3 * nw + 3]
        stage = refs[3 * nw + 3:]
        x, y, c = _my_place()
        load = [pltpu.make_async_copy(s_refs[w], stage[w], local_sems.at[w]) for w in range(nw)]
        mine = [pltpu.make_async_copy(stage[w], _block_rows(g_refs[w], (x, y, c)), local_sems.at[w])
                for w in range(nw)]
        for cp in load:
            cp.start()
        sends, arrivals = [], []
        for j, chip in enumerate(_other_chips(x, y)):
            for w in range(nw):
                rows_out = _block_rows(g_refs[w], (*chip, c))
                rows_in = _block_rows(g_refs[w], (*chip, 1 - c))
                sends.append(pltpu.make_async_remote_copy(
                    src_ref=rows_out, dst_ref=rows_out, send_sem=send_sems.at[j, w], recv_sem=recv_sems.at[j, w],
                    device_id=(x, y, 1 - c), device_id_type=MESH))
                arrivals.append(pltpu.make_async_remote_copy(
                    src_ref=rows_in, dst_ref=rows_in, send_sem=send_sems.at[j, w], recv_sem=recv_sems.at[j, w],
                    device_id=(x, y, 1 - c), device_id_type=MESH))
        for cp in sends:
            cp.start()
        for w in range(nw):
            load[w].wait()
            mine[w].start()
        for cp in arrivals:
            cp.wait_recv()
        for cp in sends:
            cp.wait_send()
        for cp in mine:
            cp.wait()

    return pl.pallas_call(
        body, name="ag_pass_on",
        in_specs=[ANY] * (2 * nw), out_specs=[ANY] * nw,
        out_shape=[jax.ShapeDtypeStruct(a.shape, a.dtype) for a in lands],
        input_output_aliases={nw + i: i for i in range(nw)},
        scratch_shapes=[pltpu.SemaphoreType.DMA((3, nw)), pltpu.SemaphoreType.DMA((3, nw)),
                        pltpu.SemaphoreType.DMA((nw,))] + [pltpu.VMEM(s.shape, s.dtype) for s in shards],
        compiler_params=_cparams(),
    )(*shards, *lands)


def _rs_sibling_copies(p_refs, land_refs, send_sems, recv_sems):
    x, y, c = _my_place()
    return [pltpu.make_async_remote_copy(
        src_ref=p_refs[w].at[:, 1 - c], dst_ref=land_refs[w],
        send_sem=send_sems.at[w], recv_sem=recv_sems.at[w], device_id=(x, y, 1 - c), device_id_type=MESH)
        for w in range(len(p_refs))]


def _rs_sibling_start(parts, tag, after=()):
    nw = len(parts)
    sems = 2 * nw + len(after)

    def body(*refs):
        for cp in _rs_sibling_copies(refs[:nw], refs[nw:2 * nw], refs[sems], refs[sems + 1]):
            cp.start()
        refs[-1][...] = jnp.zeros_like(refs[-1])

    lands = [lax.empty(p.shape[:1] + p.shape[2:], BF16) for p in parts]
    out = pl.pallas_call(
        body, name="rs_sibling_start_%s" % tag,
        in_specs=[HBM] * (2 * nw) + [ANY] * len(after),
        out_specs=(SEM, SEM, *[HBM] * (2 * nw), pl.BlockSpec(memory_space=pltpu.VMEM)),
        out_shape=(pltpu.SemaphoreType.DMA((nw,)), pltpu.SemaphoreType.DMA((nw,)),
                   *[pltpu.HBM(a.shape, a.dtype) for a in list(parts) + lands],
                   jax.ShapeDtypeStruct((8, LANES), F32)),
        input_output_aliases={i: 2 + i for i in range(2 * nw)},
        compiler_params=pltpu.CompilerParams(has_side_effects=EFFECT),
    )(*[_in_hbm(a) for a in list(parts) + lands], *after)
    return out[0], out[1], out[2:2 + nw], out[2 + nw:2 + 2 * nw], out[-1]


def _rs_sibling_wait(send_sems, recv_sems, parts, lands, after, tag):
    nw = len(parts)

    def body(*refs):
        for cp in _rs_sibling_copies(refs[:nw], refs[nw:2 * nw], refs[2 * nw], refs[2 * nw + 1]):
            cp.wait_send()
            cp.wait_recv()

    out = pl.pallas_call(
        body, name="rs_sibling_wait_%s" % tag,
        in_specs=[HBM] * (2 * nw) + [SEM, SEM] + [ANY] * len(after),
        out_specs=[HBM] * (2 * nw),
        out_shape=[pltpu.HBM(a.shape, a.dtype) for a in list(parts) + list(lands)],
        input_output_aliases={i: i for i in range(2 * nw)},
        compiler_params=pltpu.CompilerParams(has_side_effects=EFFECT),
    )(*parts, *lands, send_sems, recv_sems, *after)
    return out[:nw], out[nw:]


def _rs_chip_sum(parts, gots, c):
    n = len(parts)

    def body(c_ref, *refs):
        for p_ref, g_ref, o_ref in zip(refs[:n], refs[n:2 * n], refs[2 * n:]):
            o_ref[...] = (p_ref[...].astype(F32) + g_ref[...].astype(F32)).astype(o_ref.dtype)

    mine = [pl.BlockSpec((None, None, p.shape[2], D_MODEL), lambda q, c_ref: (q, c_ref[0], 0, 0)) for p in parts]
    theirs = [pl.BlockSpec((None, g.shape[1], D_MODEL), lambda q, c_ref: (q, 0, 0)) for g in gots]
    return pl.pallas_call(
        body, name="rs_chip_sum",
        grid_spec=pltpu.PrefetchScalarGridSpec(
            num_scalar_prefetch=1, grid=(4,), in_specs=mine + theirs, out_specs=theirs),
        out_shape=[jax.ShapeDtypeStruct(g.shape, BF16) for g in gots],
        compiler_params=_cparams(("parallel",)),
    )(c, *parts, *gots)


def _other_chips(x, y):
    return [(1 - x, y), (x, 1 - y), (1 - x, 1 - y)]


def _rs_chip_copies(s_refs, land_refs, send_sems, recv_sems):
    x, y, c = _my_place()
    copies = []
    for k, chip in enumerate(_other_chips(x, y)):
        q = 2 * chip[0] + chip[1]
        copies += [pltpu.make_async_remote_copy(
            src_ref=s_refs[w].at[q], dst_ref=land_refs[w].at[k],
            send_sem=send_sems.at[k * len(s_refs) + w], recv_sem=recv_sems.at[k * len(s_refs) + w],
            device_id=(*chip, c), device_id_type=MESH)
            for w in range(len(s_refs))]
    return copies


def _rs_chip_start(sums, layer):
    nw = len(sums)

    def body(*refs):
        s_refs, land_refs = refs[:nw], refs[nw:2 * nw]
        send_sems, recv_sems = refs[2 * nw], refs[2 * nw + 1]
        token = refs[-1]
        for cp in _rs_chip_copies(s_refs, land_refs, send_sems, recv_sems):
            cp.start()
        token[...] = jnp.zeros_like(token)

    lands = [lax.empty((3,) + s.shape[1:], BF16) for s in sums]
    out = pl.pallas_call(
        body, name="rs_chip_start_%s" % layer,
        in_specs=[HBM] * (2 * nw),
        out_specs=(SEM, SEM, *[HBM] * (2 * nw), pl.BlockSpec(memory_space=pltpu.VMEM)),
        out_shape=(pltpu.SemaphoreType.DMA((3 * nw,)), pltpu.SemaphoreType.DMA((3 * nw,)),
                   *[pltpu.HBM(a.shape, a.dtype) for a in list(sums) + lands],
                   jax.ShapeDtypeStruct((8, LANES), F32)),
        input_output_aliases={i: 2 + i for i in range(2 * nw)},
        compiler_params=pltpu.CompilerParams(has_side_effects=EFFECT),
    )(*[_in_hbm(a) for a in list(sums) + lands])
    return out[0], out[1], out[2:2 + nw], out[2 + nw:2 + 2 * nw], out[-1]


def _rs_chip_wait(send_sems, recv_sems, sums, lands, after, layer):
    nw = len(sums)

    def body(*refs):
        s_refs, land_refs = refs[:nw], refs[nw:2 * nw]
        for cp in _rs_chip_copies(s_refs, land_refs, refs[2 * nw], refs[2 * nw + 1]):
            cp.wait_send()
            cp.wait_recv()

    out = pl.pallas_call(
        body, name="rs_chip_wait_%s" % layer,
        in_specs=[HBM] * (2 * nw) + [SEM, SEM] + [ANY] * len(after),
        out_specs=[HBM] * (2 * nw),
        out_shape=[pltpu.HBM(a.shape, a.dtype) for a in list(sums) + list(lands)],
        input_output_aliases={i: i for i in range(2 * nw)},
        compiler_params=pltpu.CompilerParams(has_side_effects=EFFECT),
    )(*sums, *lands, send_sems, recv_sems, *after)
    return out[:nw], out[nw:]


def _rs_finish(sums, gots, q, layer, into):
    n = len(sums)

    def body(q_ref, *refs):
        for s_ref, g_ref, o_ref in zip(refs[:n], refs[n:2 * n], refs[len(refs) - n:]):
            o_ref[...] = ((s_ref[...].astype(F32) + g_ref[0].astype(F32)) + g_ref[1].astype(F32)) + g_ref[2].astype(F32)

    rows = [s.shape[1] for s in sums]
    in_specs = [pl.BlockSpec((None, r, D_MODEL), lambda i, q_ref: (q_ref[0], 0, 0)) for r in rows]
    in_specs += [pl.BlockSpec((3, r, D_MODEL), lambda i, q_ref: (0, 0, 0)) for r in rows]
    args = [q, *sums, *gots]
    aliases = {}
    if into is not None:
        in_specs += [ANY] * n
        aliases = {len(args) + i: i for i in range(n)}
        args += list(into)
    return pl.pallas_call(
        body, name="rs_finish",
        grid_spec=pltpu.PrefetchScalarGridSpec(
            num_scalar_prefetch=1, grid=(1,), in_specs=in_specs,
            out_specs=[pl.BlockSpec((None, r, D_MODEL), lambda i, q_ref: (layer, 0, 0)) for r in rows]),
        out_shape=[jax.ShapeDtypeStruct((DEPTH, r, D_MODEL), F32) for r in rows],
        input_output_aliases=aliases,
        compiler_params=_cparams(("arbitrary",)),
    )(*args)


def _allreduce_small(vec, deps=()):
    R = vec.shape[0]
    assert R % (8 * N_DEV) == 0
    P = R // N_DEV
    nd = len(deps)

    def body(*refs):
        v_ref = refs[0]
        o_ref, buf, send1, recv1, send2, recv2 = refs[1 + nd:]
        x, y, c = _my_place()
        me = 4 * x + 2 * y + c

        def piece(ref, d):
            return ref.at[pl.ds(pl.multiple_of(d * P, 8), P), :]

        def peer(k):
            p = me ^ k
            return p, (p >> 2, (p >> 1) & 1, p & 1)

        scatter = []
        for k in range(1, N_DEV):
            p, where = peer(k)
            scatter.append(pltpu.make_async_remote_copy(
                src_ref=piece(v_ref, p), dst_ref=buf.at[k], send_sem=send1.at[k - 1], recv_sem=recv1.at[k - 1],
                device_id=where, device_id_type=MESH))
        for cp in scatter:
            cp.start()
        buf[0] = piece(v_ref, me)[...]
        for cp in scatter:
            cp.wait()
        acc = buf[me]
        for d in range(1, N_DEV):
            acc = acc + buf[me ^ d]
        piece(o_ref, me)[...] = acc
        spread, arrivals = [], []
        for k in range(1, N_DEV):
            p, where = peer(k)
            spread.append(pltpu.make_async_remote_copy(
                src_ref=piece(o_ref, me), dst_ref=piece(o_ref, me), send_sem=send2.at[k - 1], recv_sem=recv2.at[k - 1],
                device_id=where, device_id_type=MESH))
            arrivals.append(pltpu.make_async_remote_copy(
                src_ref=piece(o_ref, p), dst_ref=piece(o_ref, p), send_sem=send2.at[k - 1], recv_sem=recv2.at[k - 1],
                device_id=where, device_id_type=MESH))
        for cp in spread:
            cp.start()
        for cp in arrivals:
            cp.wait_recv()
        for cp in spread:
            cp.wait_send()

    sems = pltpu.SemaphoreType.DMA((N_DEV - 1,))
    return pl.pallas_call(
        body, name="allreduce_small",
        in_specs=[pl.BlockSpec(memory_space=pltpu.VMEM)] + [ANY] * nd, out_specs=pl.BlockSpec(memory_space=pltpu.VMEM),
        out_shape=jax.ShapeDtypeStruct((R, LANES), F32),
        scratch_shapes=[pltpu.VMEM((N_DEV, P, LANES), F32), sems, sems, sems, sems],
        compiler_params=_cparams(),
    )(vec, *deps)


def _pack(arrs):
    flat = jnp.concatenate([a.reshape(-1) for a in arrs])
    pad = (-flat.shape[0]) % (8 * N_DEV * LANES)
    return jnp.pad(flat, (0, pad)).reshape(-1, LANES)


def _unpack(packed, shapes):
    flat = packed.reshape(-1)
    out, off = [], 0
    for s in shapes:
        n = math.prod(s)
        out.append(flat[off:off + n].reshape(s))
        off += n
    return out


def kernel(x, w_in, w_conv, w_pool, pool_scale, sgu_ln_g, w_spatial, b_spatial, w_o, ln1_g, ln1_b, w_gate_up, w_down, ln2_g, ln2_b, loss_target, m_w_in, m_w_conv, m_w_pool, m_pool_scale, m_sgu_ln_g, m_w_spatial, m_b_spatial, m_w_o, m_ln1_g, m_ln1_b, m_w_gate_up, m_w_down, m_ln2_g, m_ln2_b, v_w_in, v_w_conv, v_w_pool, v_pool_scale, v_sgu_ln_g, v_w_spatial, v_b_spatial, v_w_o, v_ln1_g, v_ln1_b, v_w_gate_up, v_w_down, v_ln2_g, v_ln2_b):
    L = DEPTH
    T = x.shape[1]
    mx, my, mc = _my_place()
    dev = 4 * mx + 2 * my + mc
    xs = x[0]
    target = loss_target[0]

    conv_cols = w_conv.shape[2]
    w_conv_z = lax.dynamic_update_slice(jnp.zeros((L, 3, CONV_W), F32), w_conv, (0, 0, dev * conv_cols))
    w_conv_packed = _allreduce_small(_pack([w_conv_z]))
    w_conv_full = _unpack(w_conv_packed, [(L, 3, CONV_W)])[0]

    shards = (jnp.swapaxes(w_in, 1, 2).astype(BF16), jnp.swapaxes(w_gate_up, 1, 2).astype(BF16),
              w_o.astype(BF16), w_down.astype(BF16))
    first_gather = _ag_start_layer(shards, 0, [w_conv_packed])

    grad_x2, big_grads, small_grads = _local_step(
        xs, target, shards, first_gather, w_conv_full, w_pool, pool_scale, sgu_ln_g, w_spatial, b_spatial,
        ln1_g, ln1_b, ln2_g, ln2_b)
    grad_x = grad_x2[None]
    big_w = (w_in, w_gate_up, w_o, w_down)
    big_m = (m_w_in, m_w_gate_up, m_w_o, m_w_down)
    big_v = (v_w_in, v_w_gate_up, v_w_o, v_w_down)
    small_w = [w_conv_full, w_pool, pool_scale, sgu_ln_g, w_spatial, b_spatial, ln1_g, ln1_b, ln2_g, ln2_b]
    small_m = [m_w_conv, m_w_pool, m_pool_scale, m_sgu_ln_g, m_w_spatial, m_b_spatial, m_ln1_g, m_ln1_b, m_ln2_g, m_ln2_b]
    small_v = [v_w_conv, v_w_pool, v_pool_scale, v_sgu_ln_g, v_w_spatial, v_b_spatial, v_ln1_g, v_ln1_b, v_ln2_g, v_ln2_b]
    loss, grads, deltas, new_m, new_v = _reduce_and_update(
        big_grads, small_grads, big_w, big_m, big_v, small_w, small_m, small_v)
    return (loss, grad_x, *grads, *deltas, *new_m, *new_v)


def _ag_start_layer(shards, l, after):
    s_in, s_gu, s_o, s_dn = [s[l] for s in shards]
    first = _ag_start([s_in, s_o], "%da" % l, after=after)
    return first, _ag_start([s_gu, s_dn], "%db" % l, after=[first[4]])


def _ag_finish(gather, after, tag):
    send_sems, recv_sems, shards, lands, _ = gather
    shards, lands = _ag_wait(send_sems, recv_sems, shards, lands, after, tag)
    return _ag_pass_on(shards, lands)


def _rs_begin(parts, tag, after=()):
    return _rs_sibling_start([p.reshape(4, 2, p.shape[0] // N_DEV, D_MODEL) for p in parts], tag, after)


def _rs_continue(sibling_flight, after, c_arr, tag):
    send_sems, recv_sems, parts, lands, _ = sibling_flight
    parts, got = _rs_sibling_wait(send_sems, recv_sems, parts, lands, after, tag)
    return _rs_chip_start(_rs_chip_sum(parts, got, c_arr), tag)


def _local_step(xs, target, shards, gather, w_conv_full, w_pool, pool_scale, sgu_ln_g, w_spatial, b_spatial,
                ln1_g, ln1_b, ln2_g, ln2_b):
    L = DEPTH
    T = xs.shape[0]
    mx, my, mc = _my_place()
    c_arr = jnp.reshape(mc, (1,)).astype(jnp.int32)
    q_arr = jnp.reshape(2 * mx + my, (1,)).astype(jnp.int32)
    eye2 = jnp.eye(2, dtype=F32)
    wp = w_pool.reshape(L, 2, 2, HALF, HALF)
    wpool_bd = jnp.einsum("ltgcd,gh->ltgchd", wp, eye2).reshape(L, 2, LANES, LANES)
    wsp_t = w_spatial.reshape(L, 3, 2 * CHUNK, CHUNK)
    bias_t = jnp.repeat(jnp.swapaxes(b_spatial.reshape(L, 3, 2, CHUNK), 2, 3), HALF, axis=3)
    mixer_w = (w_conv_full, wpool_bd, pool_scale[:, None, :], sgu_ln_g[:, None, :], wsp_t, bias_t)
    g1, b1, g2, b2 = [a[:, None, :] for a in (ln1_g, ln1_b, ln2_g, ln2_b)]
    one, zero = jnp.ones((1, 1, D_MODEL), F32), jnp.zeros((1, 1, D_MODEL), F32)

    saved = []
    prev, pg, pb = xs, (one, 0), (zero, 0)
    prev_b = xs.astype(BF16)
    weights = []
    for l in range(L):
        g_in, g_o = _ag_finish(gather[0], [] if l == 0 else [prev_b], "%da" % l)
        second, deps = gather[1], [gather[1][4]]
        if l + 1 < L:
            gather = _ag_start_layer(shards, l + 1, [g_in])
            deps = [gather[1][4]]
        proj = _mm(prev_b, g_in, "nt", F32, 512, IN_W, "mm_proj", deps=deps)
        mixcat = _mixer_fwd(proj, *mixer_w, l)
        xhat1, rstd1, h_b = _mm_ln_fwd(mixcat, g_o, prev, pg, pb, (g1, l), (b1, l), "mm_wo_ln")
        g_gu, g_dn = _ag_finish(second, [h_b], "%db" % l)
        weights.append((g_in, g_gu, g_o, g_dn))
        g_act, u_act, act = _mm_swiglu_fwd(h_b, g_gu)
        xhat2, rstd2, y_b = _mm_ln_fwd(act, g_dn, xhat1, (g1, l), (b1, l), (g2, l), (b2, l), "mm_down_ln")
        saved.append((prev_b, proj, mixcat, xhat1, rstd1, h_b, g_act, u_act, act, xhat2, rstd2))
        prev, pg, pb, prev_b = xhat2, (g2, l), (b2, l), y_b


    small = [None] * L
    big = None
    sibling_flight = None
    above = None
    for l in reversed(range(L)):
        prev_b, proj, mixcat, xhat1, rstd1, h_b, g_act, u_act, act, xhat2, rstd2 = saved[l]
        g_in, g_gu, g_o, g_dn = weights[l]
        chip_flight = None
        if above is None:
            loss_tile, dr2, dr2_b, dg2, db2 = _loss_ln_bwd(xhat2, rstd2, (g2, l), (b2, l), target)
        else:
            dr2, dr2_b, dg2, db2 = _mm_ln_bwd([above[0]], above[1], above[2], xhat2, rstd2, (g2, l),
                                              "mm_dx_ln", deps=[sibling_flight[4]])
            chip_flight = _rs_continue(sibling_flight, [dr2_b], c_arr, str(l + 1))
        dg_b, du_b = _mm_swiglu_bwd(dr2_b, g_dn, g_act, u_act, deps=[chip_flight[4]] if chip_flight else [])
        p_dn = _mm(act, dr2_b, "tn", BF16, DW_TM, D_MODEL // 2, "mm_dw_down")
        p_gu = _mm_tn_pair(dg_b, du_b, h_b, DW_TM, "mm_dw_gate_up")
        ffn_sibling = _rs_begin([p_gu, p_dn], "0b") if l == 0 else None
        dr1, dr1_b, dg1, db1, dmix = _mm_ln_bwd([dg_b, du_b], g_gu, dr2, xhat1, rstd1, (g1, l), "mm_dh_ln",
                                                deps=[ffn_sibling[4]] if l == 0 else [], w_back=g_o)
        ffn_flight = _rs_continue(ffn_sibling, [dr1_b], c_arr, "0b") if l == 0 else None
        p_o = _mm(mixcat, dr1_b, "tn", BF16, 512, D_MODEL, "mm_dw_o")
        dproj, dwc, dwp, dps, dlng, dwsp, dbias = _mixer_bwd(proj, dmix, *mixer_w, l,
                                                             deps=[ffn_flight[4]] if l == 0 else [])
        p_in = _mm(dproj, prev_b, "tn", BF16, IN_W, D_MODEL // 2, "mm_dw_in")
        small[l] = (dwc, dwp, dps, dlng, dwsp, dbias, dg1, db1, dg2, db2)
        above = (dproj, g_in, dr1)
        if chip_flight is not None:
            big = list(_rs_chip_finish(chip_flight, [p_in], q_arr, str(l + 1), l + 1, big))
        if l > 0:
            sibling_flight = _rs_begin([p_in, p_gu, p_o, p_dn], str(l))
        else:
            big[1], big[3] = _rs_chip_finish(ffn_flight, [p_in, p_o], q_arr, "0b", 0, [big[1], big[3]])

    def stack(i):
        return jnp.stack([small[l][i] for l in range(L)])

    dwp_bd = stack(1).reshape(L, 2, 2, HALF, 2, HALF)
    dwp_all = jnp.einsum("ltgchd,gh->ltgcd", dwp_bd, eye2).reshape(L, 4, HALF, HALF)
    dbs_all = jnp.swapaxes(stack(5)[:, :, :, :2], 2, 3).reshape(L, 6, CHUNK)
    small_grads = [stack(0), dwp_all, stack(2).reshape(L, POOL_W), stack(3).reshape(L, SGU_W),
                   stack(4).reshape(L, 6, CHUNK, CHUNK), dbs_all] + [stack(i).reshape(L, D_MODEL) for i in (6, 7, 8, 9)]
    small_grads.append(loss_tile[0, :1])
    packed_small = _allreduce_small(_pack(small_grads), deps=[big[1]])
    sibling_flight = _rs_begin([p_in, p_o], "0a", after=[packed_small])
    grad_x = _mm_ln_bwd([above[0]], above[1], above[2], None, None, None, "mm_dx_out", deps=[sibling_flight[4]])
    last_flight = _rs_continue(sibling_flight, [grad_x], c_arr, "0a")
    return grad_x, (big, last_flight, q_arr), (packed_small, [a.shape for a in small_grads])


def _rs_chip_finish(in_flight, after, q, tag, layer, into):
    send_sems, recv_sems, sums, lands, _ = in_flight
    sums, got = _rs_chip_wait(send_sems, recv_sems, sums, lands, after, tag)
    return _rs_finish(sums, got, q, layer, into)


def _reduce_and_update(big_grads, small_grads, big_w, big_m, big_v, small_w, small_m, small_v):
    L = DEPTH
    mx, my, mc = _my_place()
    dev = 4 * mx + 2 * my + mc
    conv_cols = CONV_W // N_DEV
    w_in, w_gate_up, w_o, w_down = big_w
    m_w_in, m_w_gate_up, m_w_o, m_w_down = big_m
    v_w_in, v_w_gate_up, v_w_o, v_w_down = big_v
    packed_g, small_shapes = small_grads
    big, last_flight, q_arr = big_grads

    def widen_conv(a):
        return lax.dynamic_update_slice(jnp.zeros((L, 3, CONV_W), F32), a, (0, 0, dev * conv_cols))

    small_m = [widen_conv(small_m[0])] + list(small_m[1:])
    small_v = [widen_conv(small_v[0])] + list(small_v[1:])
    pk_d, pk_m, pk_v = _adamw(_pack(small_w), packed_g, _pack(small_m), _pack(small_v), packed_g.shape[0] // 2)
    sg = _unpack(packed_g, small_shapes)
    sd = _unpack(pk_d, small_shapes)
    sm = _unpack(pk_m, small_shapes)
    sv = _unpack(pk_v, small_shapes)

    def conv_cols_of(a):
        return lax.dynamic_slice(a, (0, 0, dev * conv_cols), (L, 3, conv_cols))

    for lst in (sg, sd, sm, sv):
        lst[0] = conv_cols_of(lst[0])

    tr = lambda a: jnp.swapaxes(a, 1, 2)
    gt_gu, g_w_dn = big[1], big[3]
    d_gu, m_gu, v_gu = [tr(a) for a in _adamw(tr(w_gate_up), gt_gu, tr(m_w_gate_up), tr(v_w_gate_up), gt_gu.shape[1] // 2)]
    d_dn, m_dn, v_dn = _adamw(w_down, g_w_dn, m_w_down, v_w_down, w_down.shape[1])
    gt_in, g_w_o = _rs_chip_finish(last_flight, [d_gu, d_dn, pk_d], q_arr, "0a", 0, [big[0], big[2]])
    d_in, m_in, v_in = [tr(a) for a in _adamw(tr(w_in), gt_in, tr(m_w_in), tr(v_w_in), gt_in.shape[1])]
    d_o, m_o, v_o = _adamw(w_o, g_w_o, m_w_o, v_w_o, w_o.shape[1])
    g_w_in, g_w_gu = tr(gt_in), tr(gt_gu)

    def ordered(big_in, big_o, big_gu, big_dn, sm_list):
        return [big_in, sm_list[0], sm_list[1], sm_list[2], sm_list[3], sm_list[4], sm_list[5], big_o,
                sm_list[6], sm_list[7], big_gu, big_dn, sm_list[8], sm_list[9]]

    grads = ordered(g_w_in, g_w_o, g_w_gu, g_w_dn, sg)
    deltas = ordered(d_in, d_o, d_gu, d_dn, sd)
    new_m = ordered(m_in, m_o, m_gu, m_dn, sm)
    new_v = ordered(v_in, v_o, v_gu, v_dn, sv)
    return sg[10][0], grads, deltas, new_m, new_v
```
